```python
import math
import jax, jax.numpy as jnp
from jax import lax
import numpy as np

D_MODEL = 1024
BATCH = 32
SEQ = 2048
DEPTH = 1

N_META = 16
HEAD_DIM = 64
N_Q_HEADS = D_MODEL // HEAD_DIM
N_KV_HEADS = N_Q_HEADS // 4
Q_PER_KV = N_Q_HEADS // N_KV_HEADS
WINDOW = 128
BLOCK = 128
ATTN_WIDTH = N_Q_HEADS * HEAD_DIM
KV_WIDTH = N_KV_HEADS * HEAD_DIM
SSM_GROUP = 16
SSM_WIDTH = D_MODEL // 2
SSM_GROUPS = SSM_WIDTH // SSM_GROUP
SSM_STATE = 64
D_FF = ((8 * D_MODEL // 3 + 255) // 256) * 256
NORM_EPS = 1e-6
NEG_INF = -1e30
SPLITS = [ATTN_WIDTH, ATTN_WIDTH + KV_WIDTH, ATTN_WIDTH + 2 * KV_WIDTH,
          ATTN_WIDTH + 2 * KV_WIDTH + SSM_WIDTH,
          ATTN_WIDTH + 2 * KV_WIDTH + SSM_WIDTH + D_MODEL]
IN_WIDTH = SPLITS[-1] + D_MODEL

kernel_name = "hybrid_swa_s5_gated_macaron"


def rmsnorm(x, g):
    xf = x.astype(jnp.float32)
    y = xf * lax.rsqrt(jnp.mean(xf * xf, axis=-1, keepdims=True) + NORM_EPS)
    return (y * g.astype(jnp.float32)).astype(x.dtype)


def swiglu(h, w1, w3, w2):
    return (jax.nn.silu(h @ w1) * (h @ w3)) @ w2


def sink_softmax(scores, mask, sink):
    scores = jnp.where(mask, scores, NEG_INF)
    sink_b = jnp.broadcast_to(sink[:, :, None, None], scores.shape[:-1] + (1,))
    p = jax.nn.softmax(jnp.concatenate([scores, sink_b], axis=-1), axis=-1)
    return p[..., :-1]


def sliding_window_attention(q, k, v, sinks):
    b, l = q.shape[0], q.shape[1]
    s = l - N_META
    nb = s // BLOCK
    q = (q * (HEAD_DIM ** -0.5)).reshape(b, l, N_KV_HEADS, Q_PER_KV, HEAD_DIM)
    k = k.reshape(b, l, N_KV_HEADS, HEAD_DIM)
    v = v.reshape(b, l, N_KV_HEADS, HEAD_DIM)
    sink = sinks.astype(jnp.float32).reshape(N_KV_HEADS, Q_PER_KV)
    q_meta, q_real = q[:, :N_META], q[:, N_META:]
    k_meta, k_real = k[:, :N_META], k[:, N_META:]
    v_meta, v_real = v[:, :N_META], v[:, N_META:]

    sc_m = jnp.einsum('bqkgd,bskd->bkgqs', q_meta, k_meta).astype(jnp.float32)
    mask_m = jnp.tril(jnp.ones((N_META, N_META), dtype=bool))
    p_m = sink_softmax(sc_m, mask_m, sink)
    out_m = jnp.einsum('bkgqs,bskd->bqkgd', p_m.astype(v.dtype), v_meta)

    qb = q_real.reshape(b, nb, BLOCK, N_KV_HEADS, Q_PER_KV, HEAD_DIM)
    kb = k_real.reshape(b, nb, BLOCK, N_KV_HEADS, HEAD_DIM)
    vb = v_real.reshape(b, nb, BLOCK, N_KV_HEADS, HEAD_DIM)
    k_band = jnp.concatenate([jnp.concatenate([jnp.zeros_like(kb[:, :1]), kb[:, :-1]], axis=1), kb], axis=2)
    v_band = jnp.concatenate([jnp.concatenate([jnp.zeros_like(vb[:, :1]), vb[:, :-1]], axis=1), vb], axis=2)
    qi = jnp.arange(BLOCK)[:, None]
    kj = jnp.arange(2 * BLOCK)[None, :]
    rel = BLOCK + qi - kj
    band_ok = (rel >= 0) & (rel < WINDOW)
    meta_ok = jnp.ones((BLOCK, N_META), dtype=bool)

    def block_fn(args):
        n, qn, kn, vn = args
        keys = jnp.concatenate([k_meta, kn], axis=1)
        vals = jnp.concatenate([v_meta, vn], axis=1)
        sc = jnp.einsum('bqkgd,bskd->bkgqs', qn, keys).astype(jnp.float32)
        valid = band_ok & ((n - 1) * BLOCK + kj >= 0)
        mask = jnp.concatenate([meta_ok, valid], axis=1)
        p = sink_softmax(sc, mask, sink)
        return jnp.einsum('bkgqs,bskd->bqkgd', p.astype(vals.dtype), vals)

    out_r = lax.map(block_fn, (jnp.arange(nb, dtype=jnp.int32), jnp.moveaxis(qb, 1, 0),
                               jnp.moveaxis(k_band, 1, 0), jnp.moveaxis(v_band, 1, 0)))
    out_r = jnp.moveaxis(out_r, 0, 1).reshape(b, s, ATTN_WIDTH)
    return jnp.concatenate([out_m.reshape(b, N_META, ATTN_WIDTH), out_r], axis=1)


def s5_ssm(u, a_re, a_im, log_step, b_re, b_im, c_re, c_im, d_skip):
    b, l, _ = u.shape
    uf = u.astype(jnp.float32).reshape(b, l, SSM_GROUPS, SSM_GROUP)
    ar, ai = a_re.astype(jnp.float32), a_im.astype(jnp.float32)
    step = jnp.exp(log_step.astype(jnp.float32))[:, None]
    mag = jnp.exp(ar * step)
    ang = ai * step
    lam_re, lam_im = mag * jnp.cos(ang), mag * jnp.sin(ang)
    den = ar * ar + ai * ai
    nr, ni = lam_re - 1.0, lam_im
    coef_re = (nr * ar + ni * ai) / den
    coef_im = (ni * ar - nr * ai) / den
    br, bi = b_re.astype(jnp.float32), b_im.astype(jnp.float32)
    bb_re = coef_re[..., None] * br - coef_im[..., None] * bi
    bb_im = coef_re[..., None] * bi + coef_im[..., None] * br
    bu_re = jnp.einsum('blgc,gnc->blgn', uf, bb_re)
    bu_im = jnp.einsum('blgc,gnc->blgn', uf, bb_im)
    la_re = jnp.broadcast_to(lam_re[None, None], (1, l, SSM_GROUPS, SSM_STATE))
    la_im = jnp.broadcast_to(lam_im[None, None], (1, l, SSM_GROUPS, SSM_STATE))

    def combine(e1, e2):
        a1r, a1i, b1r, b1i = e1
        a2r, a2i, b2r, b2i = e2
        return (a2r * a1r - a2i * a1i, a2r * a1i + a2i * a1r,
                a2r * b1r - a2i * b1i + b2r, a2r * b1i + a2i * b1r + b2i)

    _, _, x_re, x_im = lax.associative_scan(combine, (la_re, la_im, bu_re, bu_im), axis=1)
    y = (jnp.einsum('blgn,gcn->blgc', x_re, c_re.astype(jnp.float32))
         - jnp.einsum('blgn,gcn->blgc', x_im, c_im.astype(jnp.float32)))
    y = y + d_skip.astype(jnp.float32).reshape(SSM_GROUPS, SSM_GROUP) * uf
    return y.reshape(b, l, SSM_WIDTH).astype(u.dtype)


def _fwd_setup_inputs(seed: int = 0) -> dict:
    key = jax.random.key(seed)
    ks = jax.random.split(key, 32)
    nrm = lambda k, shape, scale: jax.random.normal(k, shape, jnp.float32) * scale
    n_idx = jnp.arange(SSM_STATE, dtype=jnp.float32)
    return {
        "x": nrm(ks[0], (BATCH, SEQ, D_MODEL), 1.0),
        "meta_tokens": nrm(ks[1], (N_META, D_MODEL), 1.0),
        "ffn1_norm": 1.0 + nrm(ks[2], (DEPTH, D_MODEL), 0.02),
        "ffn1_w1": nrm(ks[3], (DEPTH, D_MODEL, D_FF), D_MODEL ** -0.5),
        "ffn1_w3": nrm(ks[4], (DEPTH, D_MODEL, D_FF), D_MODEL ** -0.5),
        "ffn1_w2": nrm(ks[5], (DEPTH, D_FF, D_MODEL), D_FF ** -0.5),
        "mix_norm": 1.0 + nrm(ks[6], (DEPTH, D_MODEL), 0.02),
        "w_in": nrm(ks[7], (DEPTH, D_MODEL, IN_WIDTH), D_MODEL ** -0.5),
        "attn_sinks": nrm(ks[8], (DEPTH, N_Q_HEADS), 0.5),
        "ssm_a_re": -0.5 + nrm(ks[9], (DEPTH, SSM_GROUPS, SSM_STATE), 0.01),
        "ssm_a_im": math.pi * n_idx[None, None, :] + nrm(ks[10], (DEPTH, SSM_GROUPS, SSM_STATE), 0.01),
        "ssm_log_step": jax.random.uniform(ks[11], (DEPTH, SSM_GROUPS), jnp.float32,
                                           math.log(0.001), math.log(0.1)),
        "ssm_b_re": nrm(ks[12], (DEPTH, SSM_GROUPS, SSM_STATE, SSM_GROUP), (2 * SSM_GROUP) ** -0.5),
        "ssm_b_im": nrm(ks[13], (DEPTH, SSM_GROUPS, SSM_STATE, SSM_GROUP), (2 * SSM_GROUP) ** -0.5),
        "ssm_c_re": nrm(ks[14], (DEPTH, SSM_GROUPS, SSM_GROUP, SSM_STATE), SSM_STATE ** -0.5),
        "ssm_c_im": nrm(ks[15], (DEPTH, SSM_GROUPS, SSM_GROUP, SSM_STATE), SSM_STATE ** -0.5),
        "ssm_d": nrm(ks[16], (DEPTH, SSM_WIDTH), 1.0),
        "ssm_glu_a": nrm(ks[17], (DEPTH, SSM_WIDTH, D_MODEL), SSM_WIDTH ** -0.5),
        "ssm_glu_b": nrm(ks[18], (DEPTH, SSM_WIDTH, D_MODEL), SSM_WIDTH ** -0.5),
        "w_out": nrm(ks[19], (DEPTH, D_MODEL, D_MODEL), D_MODEL ** -0.5),
        "ffn2_norm": 1.0 + nrm(ks[20], (DEPTH, D_MODEL), 0.02),
        "ffn2_w1": nrm(ks[21], (DEPTH, D_MODEL, D_FF), D_MODEL ** -0.5),
        "ffn2_w3": nrm(ks[22], (DEPTH, D_MODEL, D_FF), D_MODEL ** -0.5),
        "ffn2_w2": nrm(ks[23], (DEPTH, D_FF, D_MODEL), D_FF ** -0.5),
        "final_norm": 1.0 + nrm(ks[24], (D_MODEL,), 0.02),
    }


def _fwd_reference(x, meta_tokens, ffn1_norm, ffn1_w1, ffn1_w3, ffn1_w2, mix_norm, w_in,
              attn_sinks, ssm_a_re, ssm_a_im, ssm_log_step, ssm_b_re, ssm_b_im,
              ssm_c_re, ssm_c_im, ssm_d, ssm_glu_a, ssm_glu_b, w_out,
              ffn2_norm, ffn2_w1, ffn2_w3, ffn2_w2, final_norm):
    b = x.shape[0]
    meta = jnp.broadcast_to(meta_tokens[None].astype(x.dtype), (b, N_META, D_MODEL))
    h = jnp.concatenate([meta, x], axis=1)
    for i in range(DEPTH):
        h = h + 0.5 * swiglu(rmsnorm(h, ffn1_norm[i]), ffn1_w1[i], ffn1_w3[i], ffn1_w2[i])
        hn = rmsnorm(h, mix_norm[i])
        q, k, v, u, g_attn, g_ssm = jnp.split(hn @ w_in[i], SPLITS, axis=-1)
        attn = sliding_window_attention(q, k, v, attn_sinks[i])
        y = s5_ssm(u, ssm_a_re[i], ssm_a_im[i], ssm_log_step[i], ssm_b_re[i], ssm_b_im[i],
                   ssm_c_re[i], ssm_c_im[i], ssm_d[i])
        y = jax.nn.gelu(y)
        ssm = (y @ ssm_glu_a[i]) * jax.nn.sigmoid(y @ ssm_glu_b[i])
        merged = jax.nn.sigmoid(g_attn) * attn + jax.nn.sigmoid(g_ssm) * ssm
        h = h + merged @ w_out[i]
        h = h + 0.5 * swiglu(rmsnorm(h, ffn2_norm[i]), ffn2_w1[i], ffn2_w3[i], ffn2_w2[i])
    return rmsnorm(h, final_norm)[:, N_META:]


import jax as _jax
import jax.numpy as _jnp

TWIN_FORMAT = 'train_step'
FWD_PARAMS = ['x', 'meta_tokens', 'ffn1_norm', 'ffn1_w1', 'ffn1_w3', 'ffn1_w2', 'mix_norm', 'w_in', 'attn_sinks', 'ssm_a_re', 'ssm_a_im', 'ssm_log_step', 'ssm_b_re', 'ssm_b_im', 'ssm_c_re', 'ssm_c_im', 'ssm_d', 'ssm_glu_a', 'ssm_glu_b', 'w_out', 'ffn2_norm', 'ffn2_w1', 'ffn2_w3', 'ffn2_w2', 'final_norm']
TWIN_WEIGHTS = ['meta_tokens', 'ffn1_norm', 'ffn1_w1', 'ffn1_w3', 'ffn1_w2', 'mix_norm', 'w_in', 'attn_sinks', 'ssm_a_re', 'ssm_a_im', 'ssm_log_step', 'ssm_b_re', 'ssm_b_im', 'ssm_c_re', 'ssm_c_im', 'ssm_d', 'ssm_glu_a', 'ssm_glu_b', 'w_out', 'ffn2_norm', 'ffn2_w1', 'ffn2_w3', 'ffn2_w2', 'final_norm']
TWIN_DIFF_INPUT = 'x'
TWIN_INPUTS = ['x', 'meta_tokens', 'ffn1_norm', 'ffn1_w1', 'ffn1_w3', 'ffn1_w2', 'mix_norm', 'w_in', 'attn_sinks', 'ssm_a_re', 'ssm_a_im', 'ssm_log_step', 'ssm_b_re', 'ssm_b_im', 'ssm_c_re', 'ssm_c_im', 'ssm_d', 'ssm_glu_a', 'ssm_glu_b', 'w_out', 'ffn2_norm', 'ffn2_w1', 'ffn2_w3', 'ffn2_w2', 'final_norm', 'loss_target', 'm_meta_tokens', 'm_ffn1_norm', 'm_ffn1_w1', 'm_ffn1_w3', 'm_ffn1_w2', 'm_mix_norm', 'm_w_in', 'm_attn_sinks', 'm_ssm_a_re', 'm_ssm_a_im', 'm_ssm_log_step', 'm_ssm_b_re', 'm_ssm_b_im', 'm_ssm_c_re', 'm_ssm_c_im', 'm_ssm_d', 'm_ssm_glu_a', 'm_ssm_glu_b', 'm_w_out', 'm_ffn2_norm', 'm_ffn2_w1', 'm_ffn2_w3', 'm_ffn2_w2', 'm_final_norm', 'v_meta_tokens', 'v_ffn1_norm', 'v_ffn1_w1', 'v_ffn1_w3', 'v_ffn1_w2', 'v_mix_norm', 'v_w_in', 'v_attn_sinks', 'v_ssm_a_re', 'v_ssm_a_im', 'v_ssm_log_step', 'v_ssm_b_re', 'v_ssm_b_im', 'v_ssm_c_re', 'v_ssm_c_im', 'v_ssm_d', 'v_ssm_glu_a', 'v_ssm_glu_b', 'v_w_out', 'v_ffn2_norm', 'v_ffn2_w1', 'v_ffn2_w3', 'v_ffn2_w2', 'v_final_norm']
TWIN_OUTPUTS = ['loss', 'grad_x', 'grad_meta_tokens', 'grad_ffn1_norm', 'grad_ffn1_w1', 'grad_ffn1_w3', 'grad_ffn1_w2', 'grad_mix_norm', 'grad_w_in', 'grad_attn_sinks', 'grad_ssm_a_re', 'grad_ssm_a_im', 'grad_ssm_log_step', 'grad_ssm_b_re', 'grad_ssm_b_im', 'grad_ssm_c_re', 'grad_ssm_c_im', 'grad_ssm_d', 'grad_ssm_glu_a', 'grad_ssm_glu_b', 'grad_w_out', 'grad_ffn2_norm', 'grad_ffn2_w1', 'grad_ffn2_w3', 'grad_ffn2_w2', 'grad_final_norm', 'delta_meta_tokens', 'delta_ffn1_norm', 'delta_ffn1_w1', 'delta_ffn1_w3', 'delta_ffn1_w2', 'delta_mix_norm', 'delta_w_in', 'delta_attn_sinks', 'delta_ssm_a_re', 'delta_ssm_a_im', 'delta_ssm_log_step', 'delta_ssm_b_re', 'delta_ssm_b_im', 'delta_ssm_c_re', 'delta_ssm_c_im', 'delta_ssm_d', 'delta_ssm_glu_a', 'delta_ssm_glu_b', 'delta_w_out', 'delta_ffn2_norm', 'delta_ffn2_w1', 'delta_ffn2_w3', 'delta_ffn2_w2', 'delta_final_norm', 'new_m_meta_tokens', 'new_m_ffn1_norm', 'new_m_ffn1_w1', 'new_m_ffn1_w3', 'new_m_ffn1_w2', 'new_m_mix_norm', 'new_m_w_in', 'new_m_attn_sinks', 'new_m_ssm_a_re', 'new_m_ssm_a_im', 'new_m_ssm_log_step', 'new_m_ssm_b_re', 'new_m_ssm_b_im', 'new_m_ssm_c_re', 'new_m_ssm_c_im', 'new_m_ssm_d', 'new_m_ssm_glu_a', 'new_m_ssm_glu_b', 'new_m_w_out', 'new_m_ffn2_norm', 'new_m_ffn2_w1', 'new_m_ffn2_w3', 'new_m_ffn2_w2', 'new_m_final_norm', 'new_v_meta_tokens', 'new_v_ffn1_norm', 'new_v_ffn1_w1', 'new_v_ffn1_w3', 'new_v_ffn1_w2', 'new_v_mix_norm', 'new_v_w_in', 'new_v_attn_sinks', 'new_v_ssm_a_re', 'new_v_ssm_a_im', 'new_v_ssm_log_step', 'new_v_ssm_b_re', 'new_v_ssm_b_im', 'new_v_ssm_c_re', 'new_v_ssm_c_im', 'new_v_ssm_d', 'new_v_ssm_glu_a', 'new_v_ssm_glu_b', 'new_v_w_out', 'new_v_ffn2_norm', 'new_v_ffn2_w1', 'new_v_ffn2_w3', 'new_v_ffn2_w2', 'new_v_final_norm']
TWIN_LEAF_KINDS = {'loss': 'loss', 'grad_x': 'grad_x', 'grad_meta_tokens': 'grad_w', 'grad_ffn1_norm': 'grad_w', 'grad_ffn1_w1': 'grad_w', 'grad_ffn1_w3': 'grad_w', 'grad_ffn1_w2': 'grad_w', 'grad_mix_norm': 'grad_w', 'grad_w_in': 'grad_w', 'grad_attn_sinks': 'grad_w', 'grad_ssm_a_re': 'grad_w', 'grad_ssm_a_im': 'grad_w', 'grad_ssm_log_step': 'grad_w', 'grad_ssm_b_re': 'grad_w', 'grad_ssm_b_im': 'grad_w', 'grad_ssm_c_re': 'grad_w', 'grad_ssm_c_im': 'grad_w', 'grad_ssm_d': 'grad_w', 'grad_ssm_glu_a': 'grad_w', 'grad_ssm_glu_b': 'grad_w', 'grad_w_out': 'grad_w', 'grad_ffn2_norm': 'grad_w', 'grad_ffn2_w1': 'grad_w', 'grad_ffn2_w3': 'grad_w', 'grad_ffn2_w2': 'grad_w', 'grad_final_norm': 'grad_w', 'delta_meta_tokens': 'delta_w', 'delta_ffn1_norm': 'delta_w', 'delta_ffn1_w1': 'delta_w', 'delta_ffn1_w3': 'delta_w', 'delta_ffn1_w2': 'delta_w', 'delta_mix_norm': 'delta_w', 'delta_w_in': 'delta_w', 'delta_attn_sinks': 'delta_w', 'delta_ssm_a_re': 'delta_w', 'delta_ssm_a_im': 'delta_w', 'delta_ssm_log_step': 'delta_w', 'delta_ssm_b_re': 'delta_w', 'delta_ssm_b_im': 'delta_w', 'delta_ssm_c_re': 'delta_w', 'delta_ssm_c_im': 'delta_w', 'delta_ssm_d': 'delta_w', 'delta_ssm_glu_a': 'delta_w', 'delta_ssm_glu_b': 'delta_w', 'delta_w_out': 'delta_w', 'delta_ffn2_norm': 'delta_w', 'delta_ffn2_w1': 'delta_w', 'delta_ffn2_w3': 'delta_w', 'delta_ffn2_w2': 'delta_w', 'delta_final_norm': 'delta_w', 'new_m_meta_tokens': 'new_m', 'new_m_ffn1_norm': 'new_m', 'new_m_ffn1_w1': 'new_m', 'new_m_ffn1_w3': 'new_m', 'new_m_ffn1_w2': 'new_m', 'new_m_mix_norm': 'new_m', 'new_m_w_in': 'new_m', 'new_m_attn_sinks': 'new_m', 'new_m_ssm_a_re': 'new_m', 'new_m_ssm_a_im': 'new_m', 'new_m_ssm_log_step': 'new_m', 'new_m_ssm_b_re': 'new_m', 'new_m_ssm_b_im': 'new_m', 'new_m_ssm_c_re': 'new_m', 'new_m_ssm_c_im': 'new_m', 'new_m_ssm_d': 'new_m', 'new_m_ssm_glu_a': 'new_m', 'new_m_ssm_glu_b': 'new_m', 'new_m_w_out': 'new_m', 'new_m_ffn2_norm': 'new_m', 'new_m_ffn2_w1': 'new_m', 'new_m_ffn2_w3': 'new_m', 'new_m_ffn2_w2': 'new_m', 'new_m_final_norm': 'new_m', 'new_v_meta_tokens': 'new_v', 'new_v_ffn1_norm': 'new_v', 'new_v_ffn1_w1': 'new_v', 'new_v_ffn1_w3': 'new_v', 'new_v_ffn1_w2': 'new_v', 'new_v_mix_norm': 'new_v', 'new_v_w_in': 'new_v', 'new_v_attn_sinks': 'new_v', 'new_v_ssm_a_re': 'new_v', 'new_v_ssm_a_im': 'new_v', 'new_v_ssm_log_step': 'new_v', 'new_v_ssm_b_re': 'new_v', 'new_v_ssm_b_im': 'new_v', 'new_v_ssm_c_re': 'new_v', 'new_v_ssm_c_im': 'new_v', 'new_v_ssm_d': 'new_v', 'new_v_ssm_glu_a': 'new_v', 'new_v_ssm_glu_b': 'new_v', 'new_v_w_out': 'new_v', 'new_v_ffn2_norm': 'new_v', 'new_v_ffn2_w1': 'new_v', 'new_v_ffn2_w3': 'new_v', 'new_v_ffn2_w2': 'new_v', 'new_v_final_norm': 'new_v'}


def _forward(args):
    return _fwd_reference(*[args[k] for k in FWD_PARAMS])


def _output_shape():
    out = _jax.eval_shape(lambda: _forward(_fwd_setup_inputs(0)))
    return out.shape, out.dtype

N_MICROBATCH = 1
ADAM_LR = 0.001
ADAM_B1 = 0.9
ADAM_B2 = 0.999
ADAM_EPS = 1e-08
ADAM_WD = 0.01
ADAM_STEP = 10
PER_EXAMPLE_BATCH_AXIS = {'x': 0, 'loss_target': 0}
SHARED_INPUTS = []
_WEIGHT_DTYPES = {'meta_tokens': _jnp.float32, 'ffn1_norm': _jnp.float32, 'ffn1_w1': _jnp.float32, 'ffn1_w3': _jnp.float32, 'ffn1_w2': _jnp.float32, 'mix_norm': _jnp.float32, 'w_in': _jnp.float32, 'attn_sinks': _jnp.float32, 'ssm_a_re': _jnp.float32, 'ssm_a_im': _jnp.float32, 'ssm_log_step': _jnp.float32, 'ssm_b_re': _jnp.float32, 'ssm_b_im': _jnp.float32, 'ssm_c_re': _jnp.float32, 'ssm_c_im': _jnp.float32, 'ssm_d': _jnp.float32, 'ssm_glu_a': _jnp.float32, 'ssm_glu_b': _jnp.float32, 'w_out': _jnp.float32, 'ffn2_norm': _jnp.float32, 'ffn2_w1': _jnp.float32, 'ffn2_w3': _jnp.float32, 'ffn2_w2': _jnp.float32, 'final_norm': _jnp.float32}
MOMENT_SCALE = {'meta_tokens': 4.382828e-03, 'ffn1_norm': 1.077563e-01, 'ffn1_w1': 4.691154e-02, 'ffn1_w3': 4.538169e-02, 'ffn1_w2': 7.512503e-02, 'mix_norm': 6.672042e-02, 'w_in': 3.239108e-02, 'attn_sinks': 2.192195e-03, 'ssm_a_re': 4.937638e-03, 'ssm_a_im': 6.419089e-03, 'ssm_log_step': 5.502873e+00, 'ssm_b_re': 3.320724e-03, 'ssm_b_im': 3.279992e-03, 'ssm_c_re': 4.917126e-03, 'ssm_c_im': 4.909680e-03, 'ssm_d': 7.419307e-02, 'ssm_glu_a': 4.564495e-02, 'ssm_glu_b': 1.332947e-02, 'w_out': 5.058846e-02, 'ffn2_norm': 1.062467e-01, 'ffn2_w1': 4.233058e-02, 'ffn2_w3': 4.091990e-02, 'ffn2_w2': 6.808134e-02, 'final_norm': 6.396900e+01}


def _to_microbatches(a, axis):
    t = _jnp.moveaxis(a, axis, 0)
    t = t.reshape((N_MICROBATCH, t.shape[0] // N_MICROBATCH) + t.shape[1:])
    return _jnp.moveaxis(t, 1, axis + 1)


def setup_inputs(seed: int = 0) -> dict:
    inp = _fwd_setup_inputs(seed)
    key = _jax.random.fold_in(_jax.random.key(seed), 7919)
    shape, _ = _output_shape()
    out = dict(inp)
    out["loss_target"] = _jax.random.normal(_jax.random.fold_in(key, 0), shape, _jnp.float32)
    for i, name in enumerate(TWIN_WEIGHTS):
        w = inp[name].astype(_jnp.float32)
        if MOMENT_SCALE is None:
            s = _jnp.sqrt(_jnp.mean(_jnp.square(w)) + 1e-30)
        else:
            s = MOMENT_SCALE[name]
        km, kv = _jax.random.split(_jax.random.fold_in(key, i + 1))
        out[name] = w
        out["m_" + name] = s * _jax.random.normal(km, w.shape, _jnp.float32)
        out["v_" + name] = (s * s) * _jax.random.uniform(kv, w.shape, _jnp.float32, 0.5, 1.5)
    if N_MICROBATCH > 1:
        for name, axis in PER_EXAMPLE_BATCH_AXIS.items():
            out[name] = _to_microbatches(out[name], axis)
    return {'x': out['x'], 'meta_tokens': out['meta_tokens'], 'ffn1_norm': out['ffn1_norm'], 'ffn1_w1': out['ffn1_w1'], 'ffn1_w3': out['ffn1_w3'], 'ffn1_w2': out['ffn1_w2'], 'mix_norm': out['mix_norm'], 'w_in': out['w_in'], 'attn_sinks': out['attn_sinks'], 'ssm_a_re': out['ssm_a_re'], 'ssm_a_im': out['ssm_a_im'], 'ssm_log_step': out['ssm_log_step'], 'ssm_b_re': out['ssm_b_re'], 'ssm_b_im': out['ssm_b_im'], 'ssm_c_re': out['ssm_c_re'], 'ssm_c_im': out['ssm_c_im'], 'ssm_d': out['ssm_d'], 'ssm_glu_a': out['ssm_glu_a'], 'ssm_glu_b': out['ssm_glu_b'], 'w_out': out['w_out'], 'ffn2_norm': out['ffn2_norm'], 'ffn2_w1': out['ffn2_w1'], 'ffn2_w3': out['ffn2_w3'], 'ffn2_w2': out['ffn2_w2'], 'final_norm': out['final_norm'], 'loss_target': out['loss_target'], 'm_meta_tokens': out['m_meta_tokens'], 'm_ffn1_norm': out['m_ffn1_norm'], 'm_ffn1_w1': out['m_ffn1_w1'], 'm_ffn1_w3': out['m_ffn1_w3'], 'm_ffn1_w2': out['m_ffn1_w2'], 'm_mix_norm': out['m_mix_norm'], 'm_w_in': out['m_w_in'], 'm_attn_sinks': out['m_attn_sinks'], 'm_ssm_a_re': out['m_ssm_a_re'], 'm_ssm_a_im': out['m_ssm_a_im'], 'm_ssm_log_step': out['m_ssm_log_step'], 'm_ssm_b_re': out['m_ssm_b_re'], 'm_ssm_b_im': out['m_ssm_b_im'], 'm_ssm_c_re': out['m_ssm_c_re'], 'm_ssm_c_im': out['m_ssm_c_im'], 'm_ssm_d': out['m_ssm_d'], 'm_ssm_glu_a': out['m_ssm_glu_a'], 'm_ssm_glu_b': out['m_ssm_glu_b'], 'm_w_out': out['m_w_out'], 'm_ffn2_norm': out['m_ffn2_norm'], 'm_ffn2_w1': out['m_ffn2_w1'], 'm_ffn2_w3': out['m_ffn2_w3'], 'm_ffn2_w2': out['m_ffn2_w2'], 'm_final_norm': out['m_final_norm'], 'v_meta_tokens': out['v_meta_tokens'], 'v_ffn1_norm': out['v_ffn1_norm'], 'v_ffn1_w1': out['v_ffn1_w1'], 'v_ffn1_w3': out['v_ffn1_w3'], 'v_ffn1_w2': out['v_ffn1_w2'], 'v_mix_norm': out['v_mix_norm'], 'v_w_in': out['v_w_in'], 'v_attn_sinks': out['v_attn_sinks'], 'v_ssm_a_re': out['v_ssm_a_re'], 'v_ssm_a_im': out['v_ssm_a_im'], 'v_ssm_log_step': out['v_ssm_log_step'], 'v_ssm_b_re': out['v_ssm_b_re'], 'v_ssm_b_im': out['v_ssm_b_im'], 'v_ssm_c_re': out['v_ssm_c_re'], 'v_ssm_c_im': out['v_ssm_c_im'], 'v_ssm_d': out['v_ssm_d'], 'v_ssm_glu_a': out['v_ssm_glu_a'], 'v_ssm_glu_b': out['v_ssm_glu_b'], 'v_w_out': out['v_w_out'], 'v_ffn2_norm': out['v_ffn2_norm'], 'v_ffn2_w1': out['v_ffn2_w1'], 'v_ffn2_w3': out['v_ffn2_w3'], 'v_ffn2_w2': out['v_ffn2_w2'], 'v_final_norm': out['v_final_norm']}


def _loss(weights, diff, rest, loss_target):
    with _jax.named_scope("forward"):
        args = {**rest, TWIN_DIFF_INPUT: diff, **{k: w.astype(_WEIGHT_DTYPES[k]) for k, w in weights.items()}}
        y = _forward(args)
    with _jax.named_scope("loss_head"):
        err = _jnp.square(y.astype(_jnp.float32) - loss_target)
        return 0.5 * _jnp.sum(_jnp.mean(err, axis=-1)) if err.ndim else 0.5 * err


def _adamw(w, g, m, v):
    m = ADAM_B1 * m + (1.0 - ADAM_B1) * g
    v = ADAM_B2 * v + (1.0 - ADAM_B2) * _jnp.square(g)
    m_hat = m / (1.0 - ADAM_B1 ** ADAM_STEP)
    v_hat = v / (1.0 - ADAM_B2 ** ADAM_STEP)
    delta = -ADAM_LR * (m_hat / (_jnp.sqrt(v_hat) + ADAM_EPS) + ADAM_WD * w)
    return delta, m, v


def reference(x, meta_tokens, ffn1_norm, ffn1_w1, ffn1_w3, ffn1_w2, mix_norm, w_in, attn_sinks, ssm_a_re, ssm_a_im, ssm_log_step, ssm_b_re, ssm_b_im, ssm_c_re, ssm_c_im, ssm_d, ssm_glu_a, ssm_glu_b, w_out, ffn2_norm, ffn2_w1, ffn2_w3, ffn2_w2, final_norm, loss_target, m_meta_tokens, m_ffn1_norm, m_ffn1_w1, m_ffn1_w3, m_ffn1_w2, m_mix_norm, m_w_in, m_attn_sinks, m_ssm_a_re, m_ssm_a_im, m_ssm_log_step, m_ssm_b_re, m_ssm_b_im, m_ssm_c_re, m_ssm_c_im, m_ssm_d, m_ssm_glu_a, m_ssm_glu_b, m_w_out, m_ffn2_norm, m_ffn2_w1, m_ffn2_w3, m_ffn2_w2, m_final_norm, v_meta_tokens, v_ffn1_norm, v_ffn1_w1, v_ffn1_w3, v_ffn1_w2, v_mix_norm, v_w_in, v_attn_sinks, v_ssm_a_re, v_ssm_a_im, v_ssm_log_step, v_ssm_b_re, v_ssm_b_im, v_ssm_c_re, v_ssm_c_im, v_ssm_d, v_ssm_glu_a, v_ssm_glu_b, v_w_out, v_ffn2_norm, v_ffn2_w1, v_ffn2_w3, v_ffn2_w2, v_final_norm):
    given = dict(x=x, meta_tokens=meta_tokens, ffn1_norm=ffn1_norm, ffn1_w1=ffn1_w1, ffn1_w3=ffn1_w3, ffn1_w2=ffn1_w2, mix_norm=mix_norm, w_in=w_in, attn_sinks=attn_sinks, ssm_a_re=ssm_a_re, ssm_a_im=ssm_a_im, ssm_log_step=ssm_log_step, ssm_b_re=ssm_b_re, ssm_b_im=ssm_b_im, ssm_c_re=ssm_c_re, ssm_c_im=ssm_c_im, ssm_d=ssm_d, ssm_glu_a=ssm_glu_a, ssm_glu_b=ssm_glu_b, w_out=w_out, ffn2_norm=ffn2_norm, ffn2_w1=ffn2_w1, ffn2_w3=ffn2_w3, ffn2_w2=ffn2_w2, final_norm=final_norm, loss_target=loss_target, m_meta_tokens=m_meta_tokens, m_ffn1_norm=m_ffn1_norm, m_ffn1_w1=m_ffn1_w1, m_ffn1_w3=m_ffn1_w3, m_ffn1_w2=m_ffn1_w2, m_mix_norm=m_mix_norm, m_w_in=m_w_in, m_attn_sinks=m_attn_sinks, m_ssm_a_re=m_ssm_a_re, m_ssm_a_im=m_ssm_a_im, m_ssm_log_step=m_ssm_log_step, m_ssm_b_re=m_ssm_b_re, m_ssm_b_im=m_ssm_b_im, m_ssm_c_re=m_ssm_c_re, m_ssm_c_im=m_ssm_c_im, m_ssm_d=m_ssm_d, m_ssm_glu_a=m_ssm_glu_a, m_ssm_glu_b=m_ssm_glu_b, m_w_out=m_w_out, m_ffn2_norm=m_ffn2_norm, m_ffn2_w1=m_ffn2_w1, m_ffn2_w3=m_ffn2_w3, m_ffn2_w2=m_ffn2_w2, m_final_norm=m_final_norm, v_meta_tokens=v_meta_tokens, v_ffn1_norm=v_ffn1_norm, v_ffn1_w1=v_ffn1_w1, v_ffn1_w3=v_ffn1_w3, v_ffn1_w2=v_ffn1_w2, v_mix_norm=v_mix_norm, v_w_in=v_w_in, v_attn_sinks=v_attn_sinks, v_ssm_a_re=v_ssm_a_re, v_ssm_a_im=v_ssm_a_im, v_ssm_log_step=v_ssm_log_step, v_ssm_b_re=v_ssm_b_re, v_ssm_b_im=v_ssm_b_im, v_ssm_c_re=v_ssm_c_re, v_ssm_c_im=v_ssm_c_im, v_ssm_d=v_ssm_d, v_ssm_glu_a=v_ssm_glu_a, v_ssm_glu_b=v_ssm_glu_b, v_w_out=v_w_out, v_ffn2_norm=v_ffn2_norm, v_ffn2_w1=v_ffn2_w1, v_ffn2_w3=v_ffn2_w3, v_ffn2_w2=v_ffn2_w2, v_final_norm=v_final_norm)
    weights = {n: given[n] for n in TWIN_WEIGHTS}
    shared = {n: given[n] for n in SHARED_INPUTS}
    per_example = {n: given[n] for n in ['x']}
    grad_fn = _jax.value_and_grad(_loss, argnums=(0, 1))

    def one_microbatch(ex, loss_target):
        ex = dict(ex)
        diff = ex.pop(TWIN_DIFF_INPUT)
        return grad_fn(weights, diff, {**shared, **ex}, loss_target)

    if N_MICROBATCH == 1:
        loss, (grad_w, grad_x) = one_microbatch(per_example, given["loss_target"])
    else:
        def body(carry, xs):
            loss_sum, grad_sum = carry
            l_k, (gw_k, gx_k) = one_microbatch(xs[0], xs[1])
            with _jax.named_scope("update"):
                return (loss_sum + l_k, _jax.tree.map(_jnp.add, grad_sum, gw_k)), gx_k

        init = (_jnp.zeros((), _jnp.float32), _jax.tree.map(_jnp.zeros_like, weights))
        (loss, grad_w), grad_x = _jax.lax.scan(body, init, (per_example, given["loss_target"]))
    with _jax.named_scope("update"):
        delta_w, new_m, new_v = {}, {}, {}
        for n in TWIN_WEIGHTS:
            delta_w[n], new_m[n], new_v[n] = _adamw(weights[n], grad_w[n], given["m_" + n], given["v_" + n])
    return (loss, grad_x, *[grad_w[n] for n in TWIN_WEIGHTS], *[delta_w[n] for n in TWIN_WEIGHTS],
            *[new_m[n] for n in TWIN_WEIGHTS], *[new_v[n] for n in TWIN_WEIGHTS])
```

```python
import functools

import jax
import jax.numpy as jnp
from jax import lax
from jax.experimental import pallas as pl
from jax.experimental.pallas import tpu as pltpu

F32 = jnp.float32
BF16 = jnp.bfloat16
MESH = pl.DeviceIdType.MESH

N_DEV = 8
D_MODEL = 1024
N_META = 16
HEAD_DIM = 64
N_KV_HEADS = 4
Q_PER_KV = 4
BLOCK = 128
KV_WIDTH = N_KV_HEADS * HEAD_DIM
SSM_GROUP = 16
SSM_WIDTH = 512
SSM_GROUPS = 32
SSM_STATE = 64
N_STATES = SSM_GROUPS * SSM_STATE
D_FF = 2816
FF_BLK = D_FF // N_DEV
IN_WIDTH = 4096
IN_BLK = IN_WIDTH // N_DEV
NORM_EPS = 1e-6
NEG_INF = -1e30
SCAN_COLS = 256
N_SCAN_BLK = N_STATES // SCAN_COLS
SUBLANES = 8
LANES = 128

ADAM_LR = 0.001
ADAM_B1 = 0.9
ADAM_B2 = 0.999
ADAM_EPS = 1e-08
ADAM_WD = 0.01
ADAM_STEP = 10

VMEM_BIG = 56 * 1024 * 1024


def _cp(sem=None, vmem=None):
    kw = {}
    if sem is not None:
        kw["dimension_semantics"] = sem
    if vmem is not None:
        kw["vmem_limit_bytes"] = vmem
    return pltpu.CompilerParams(**kw)


def _pcall(body, **kw):
    return pl.pallas_call(body, **kw)


def _dot(a, b):
    return jnp.dot(a, b, preferred_element_type=F32)


def _dot_nt(a, b):
    return lax.dot_general(a, b, (((1,), (1,)), ((), ())), preferred_element_type=F32)


def _dot_tn(a, b):
    return lax.dot_general(a, b, (((0,), (0,)), ((), ())), preferred_element_type=F32)


def _sigmoid(x):
    return 1.0 / (1.0 + jnp.exp(-x))


def _row_tile(rows, cap):
    best = None
    for t in range(16, min(rows, cap) + 1, 16):
        if rows % t == 0:
            best = t
    assert best is not None, rows
    return best


def _my_place():
    return lax.axis_index("x"), lax.axis_index("y"), lax.axis_index("c")


def all_gather_list(shards, name):
    n = len(shards)

    def body(*refs):
        ins, outs = refs[:n], refs[n:2 * n]
        send_sems, recv_sems, local_sems = refs[2 * n:]
        x, y, c = _my_place()
        me, sibling = (x, y, c), (x, y, 1 - c)
        chips = [(1 - x, y), (x, 1 - y), (1 - x, 1 - y)]

        def blk(a, px, py, pc):
            return outs[a].at[4 * px + 2 * py + pc]

        def copy(a, k, block, to, src=None):
            return pltpu.make_async_remote_copy(
                src_ref=blk(a, *block) if src is None else src, dst_ref=blk(a, *block),
                send_sem=send_sems.at[a * 7 + k], recv_sem=recv_sems.at[a * 7 + k],
                device_id=to, device_id_type=MESH)

        mine = [pltpu.make_async_copy(ins[a], blk(a, *me), local_sems.at[a]) for a in range(n)]
        for cp in mine:
            cp.start()
        first = []
        for a in range(n):
            first.append(copy(a, 0, me, sibling, src=ins[a]))
            first += [copy(a, 1 + j, me, (*chip, c), src=ins[a]) for j, chip in enumerate(chips)]
        for cp in first:
            cp.start()
        passed = []
        for j, chip in enumerate(chips):
            for a in range(n):
                copy(a, 1 + j, (*chip, c), me).wait_recv()
                cp = copy(a, 4 + j, (*chip, c), sibling)
                cp.start()
                passed.append(cp)
        for a in range(n):
            copy(a, 0, sibling, me).wait_recv()
            for j, chip in enumerate(chips):
                copy(a, 4 + j, (*chip, 1 - c), me).wait_recv()
        for cp in first + passed:
            cp.wait_send()
        for cp in mine:
            cp.wait()

    any_spec = pl.BlockSpec(memory_space=pl.ANY)
    return _pcall(
        body, name=name,
        out_shape=[jax.ShapeDtypeStruct((N_DEV,) + s.shape, s.dtype) for s in shards],
        in_specs=[any_spec] * n, out_specs=[any_spec] * n,
        scratch_shapes=[pltpu.SemaphoreType.DMA((7 * n,)), pltpu.SemaphoreType.DMA((7 * n,)),
                        pltpu.SemaphoreType.DMA((n,))],
    )(*shards)


def rs_sibling_swap(grads, name):
    n = len(grads)

    def body(*refs):
        ins, outs = refs[:n], refs[n:2 * n]
        send_sems, recv_sems = refs[2 * n:]
        x, y, c = _my_place()
        chips = [(x, y), (1 - x, y), (x, 1 - y), (1 - x, 1 - y)]
        copies = []
        for a in range(n):
            for k, (px, py) in enumerate(chips):
                copies.append(pltpu.make_async_remote_copy(
                    src_ref=ins[a].at[4 * px + 2 * py + (1 - c)], dst_ref=outs[a].at[k],
                    send_sem=send_sems.at[4 * a + k], recv_sem=recv_sems.at[4 * a + k],
                    device_id=(x, y, 1 - c), device_id_type=MESH))
        for cp in copies:
            cp.start()
        for cp in copies:
            cp.wait()

    any_spec = pl.BlockSpec(memory_space=pl.ANY)
    return _pcall(
        body, name=name,
        out_shape=[jax.ShapeDtypeStruct((4,) + g.shape[1:], g.dtype) for g in grads],
        in_specs=[any_spec] * n, out_specs=[any_spec] * n,
        scratch_shapes=[pltpu.SemaphoreType.DMA((4 * n,)), pltpu.SemaphoreType.DMA((4 * n,))],
    )(*grads)


def rs_chip_exchange(parts, name):
    n = len(parts)

    def body(*refs):
        ins, outs = refs[:n], refs[n:2 * n]
        send_sems, recv_sems = refs[2 * n:]
        x, y, c = _my_place()
        chips = [(1 - x, y), (x, 1 - y), (1 - x, 1 - y)]
        copies = []
        for a in range(n):
            for k, (px, py) in enumerate(chips):
                copies.append(pltpu.make_async_remote_copy(
                    src_ref=ins[a].at[k], dst_ref=outs[a].at[k],
                    send_sem=send_sems.at[3 * a + k], recv_sem=recv_sems.at[3 * a + k],
                    device_id=(px, py, c), device_id_type=MESH))
        for cp in copies:
            cp.start()
        for cp in copies:
            cp.wait()

    any_spec = pl.BlockSpec(memory_space=pl.ANY)
    return _pcall(
        body, name=name,
        out_shape=[jax.ShapeDtypeStruct(p.shape, p.dtype) for p in parts],
        in_specs=[any_spec] * n, out_specs=[any_spec] * n,
        scratch_shapes=[pltpu.SemaphoreType.DMA((3 * n,)), pltpu.SemaphoreType.DMA((3 * n,))],
    )(*parts)


def pair_sums(idx, g, r1, name):
    _, rows, cols = g.shape
    tr = _row_tile(rows, 256)

    def body(idx_ref, g_ref, r_ref, o_ref):
        o_ref[...] = (g_ref[...].astype(F32) + r_ref[...].astype(F32)).astype(BF16)

    return _pcall(
        body, name=name,
        out_shape=jax.ShapeDtypeStruct((3, rows, cols), BF16),
        grid_spec=pltpu.PrefetchScalarGridSpec(
            num_scalar_prefetch=1, grid=(3, rows // tr),
            in_specs=[pl.BlockSpec((None, tr, cols), lambda k, r, ix: (ix[k + 1], r, 0)),
                      pl.BlockSpec((None, tr, cols), lambda k, r, ix: (k + 1, r, 0))],
            out_specs=pl.BlockSpec((None, tr, cols), lambda k, r, ix: (k, r, 0))),
        compiler_params=_cp(("arbitrary", "arbitrary")),
    )(idx, g, r1)


def _adam_math(w, g, m, v):
    m = ADAM_B1 * m + (1.0 - ADAM_B1) * g
    v = ADAM_B2 * v + (1.0 - ADAM_B2) * (g * g)
    m_hat = m / (1.0 - ADAM_B1 ** ADAM_STEP)
    v_hat = v / (1.0 - ADAM_B2 ** ADAM_STEP)
    delta = -ADAM_LR * (m_hat / (jnp.sqrt(v_hat) + ADAM_EPS) + ADAM_WD * w)
    return delta, m, v


def adamw_sharded(idx, g, r1, r2, w, m, v, name):
    rows, cols = w.shape
    tr = _row_tile(rows, 256)

    def body(idx_ref, g_ref, r1_ref, r2_ref, w_ref, m_ref, v_ref, go_ref, d_ref, mo_ref, vo_ref):
        grad = g_ref[...].astype(F32) + r1_ref[...].astype(F32)
        for k in range(3):
            grad = grad + r2_ref[k].astype(F32)
        delta, m_new, v_new = _adam_math(w_ref[...], grad, m_ref[...], v_ref[...])
        go_ref[...] = grad
        d_ref[...] = delta
        mo_ref[...] = m_new
        vo_ref[...] = v_new

    tile = pl.BlockSpec((tr, cols), lambda r, ix: (r, 0))
    out = jax.ShapeDtypeStruct((rows, cols), F32)
    return _pcall(
        body, name=name, out_shape=[out] * 4,
        grid_spec=pltpu.PrefetchScalarGridSpec(
            num_scalar_prefetch=1, grid=(rows // tr,),
            in_specs=[pl.BlockSpec((None, tr, cols), lambda r, ix: (ix[0], r, 0)),
                      pl.BlockSpec((None, tr, cols), lambda r, ix: (0, r, 0)),
                      pl.BlockSpec((3, tr, cols), lambda r, ix: (0, r, 0)),
                      tile, tile, tile],
            out_specs=[tile] * 4),
        compiler_params=_cp(("arbitrary",)),
    )(idx, g, r1, r2, w, m, v)


def adamw_small(parts, w, m, v, name):
    _, rows, cols = parts.shape

    def body(p_ref, w_ref, m_ref, v_ref, go_ref, d_ref, mo_ref, vo_ref):
        grad = p_ref[0]
        for k in range(1, N_DEV):
            grad = grad + p_ref[k]
        delta, m_new, v_new = _adam_math(w_ref[...], grad, m_ref[...], v_ref[...])
        go_ref[...] = grad
        d_ref[...] = delta
        mo_ref[...] = m_new
        vo_ref[...] = v_new

    out = jax.ShapeDtypeStruct((rows, cols), F32)
    return _pcall(body, name=name, out_shape=[out] * 4, compiler_params=_cp(vmem=VMEM_BIG))(parts, w, m, v)


def adamw_plain(g, w, m, v, name):
    def body(g_ref, w_ref, m_ref, v_ref, d_ref, mo_ref, vo_ref):
        delta, m_new, v_new = _adam_math(w_ref[...], g_ref[...], m_ref[...], v_ref[...])
        d_ref[...] = delta
        mo_ref[...] = m_new
        vo_ref[...] = v_new

    out = jax.ShapeDtypeStruct(w.shape, F32)
    return _pcall(body, name=name, out_shape=[out] * 3)(g, w, m, v)


def _rms_fwd(x, g):
    r = lax.rsqrt(jnp.mean(x * x, axis=-1, keepdims=True) + NORM_EPS)
    return x * r * g


def _rms_bwd(x, g, dy):
    r = lax.rsqrt(jnp.mean(x * x, axis=-1, keepdims=True) + NORM_EPS)
    xh = x * r
    t = dy * g
    dx = r * (t - xh * jnp.mean(t * xh, axis=-1, keepdims=True))
    return dx, jnp.sum(dy * xh, axis=0, keepdims=True)


def _accumulate(ref, val, first):
    @pl.when(first)
    def _():
        ref[...] = val

    @pl.when(jnp.logical_not(first))
    def _():
        ref[...] += val


def ffn_forward(h, norm, w13g, w2g, sel, tm, name):
    t_rows = h.shape[0]
    nj = N_DEV

    def body(h_ref, g_ref, w1_ref, w3_ref, w2_ref, out_ref, hn_ref, a_ref, b_ref, acc_ref):
        j = pl.program_id(1)

        @pl.when(j == 0)
        def _():
            hn_ref[...] = _rms_fwd(h_ref[...], g_ref[...]).astype(BF16)
            acc_ref[...] = jnp.zeros_like(acc_ref)

        hn = hn_ref[...]
        a = _dot(hn, w1_ref[...])
        b = _dot(hn, w3_ref[...])
        a_ref[...] = a.astype(BF16)
        b_ref[...] = b.astype(BF16)
        hid = (a * _sigmoid(a) * b).astype(BF16)
        acc_ref[...] += _dot(hid, w2_ref[...])

        @pl.when(j == nj - 1)
        def _():
            out_ref[...] = h_ref[...] + 0.5 * acc_ref[...]

    row = pl.BlockSpec((tm, D_MODEL), lambda i, j: (i, 0))
    hid_blk = pl.BlockSpec((None, tm, FF_BLK), lambda i, j: (j, i, 0))
    return _pcall(
        body, name=name, grid=(t_rows // tm, nj),
        in_specs=[row, pl.BlockSpec((1, D_MODEL), lambda i, j: (0, 0)),
                  pl.BlockSpec((None, None, D_MODEL, FF_BLK), lambda i, j: (j, 2 * sel, 0, 0)),
                  pl.BlockSpec((None, None, D_MODEL, FF_BLK), lambda i, j: (j, 2 * sel + 1, 0, 0)),
                  pl.BlockSpec((None, None, FF_BLK, D_MODEL), lambda i, j: (j, sel, 0, 0))],
        out_specs=[row, row, hid_blk, hid_blk],
        out_shape=[jax.ShapeDtypeStruct((t_rows, D_MODEL), F32), jax.ShapeDtypeStruct((t_rows, D_MODEL), BF16),
                   jax.ShapeDtypeStruct((nj, t_rows, FF_BLK), BF16), jax.ShapeDtypeStruct((nj, t_rows, FF_BLK), BF16)],
        scratch_shapes=[pltpu.VMEM((tm, D_MODEL), F32)],
        compiler_params=_cp(("arbitrary", "arbitrary"), VMEM_BIG),
    )(h, norm, w13g, w13g, w2g)


def ffn_backward_act(dh, h, norm, a, b, w13g, w2g, sel, tm, name):
    t_rows = h.shape[0]
    nj = N_DEV

    def body(dh_ref, h_ref, g_ref, a_ref, b_ref, w1_ref, w3_ref, w2_ref,
             dhin_ref, da_ref, db_ref, dg_ref, dhb_ref, acc_ref):
        i, j = pl.program_id(0), pl.program_id(1)

        @pl.when(j == 0)
        def _():
            dhb_ref[...] = (0.5 * dh_ref[...]).astype(BF16)
            acc_ref[...] = jnp.zeros_like(acc_ref)

        dhid = _dot_nt(dhb_ref[...], w2_ref[...])
        av = a_ref[...].astype(F32)
        bv = b_ref[...].astype(F32)
        s = _sigmoid(av)
        da = (dhid * bv * (s * (1.0 + av * (1.0 - s)))).astype(BF16)
        db = (dhid * (av * s)).astype(BF16)
        da_ref[...] = da
        db_ref[...] = db
        acc_ref[...] += _dot_nt(da, w1_ref[...]) + _dot_nt(db, w3_ref[...])

        @pl.when(j == nj - 1)
        def _():
            dx, dg = _rms_bwd(h_ref[...], g_ref[...], acc_ref[...])
            dhin_ref[...] = dh_ref[...] + dx
            _accumulate(dg_ref, dg, i == 0)

    row = pl.BlockSpec((tm, D_MODEL), lambda i, j: (i, 0))
    vec = pl.BlockSpec((1, D_MODEL), lambda i, j: (0, 0))
    hid_blk = pl.BlockSpec((None, tm, FF_BLK), lambda i, j: (j, i, 0))
    return _pcall(
        body, name=name, grid=(t_rows // tm, nj),
        in_specs=[row, row, vec, hid_blk, hid_blk,
                  pl.BlockSpec((None, None, D_MODEL, FF_BLK), lambda i, j: (j, 2 * sel, 0, 0)),
                  pl.BlockSpec((None, None, D_MODEL, FF_BLK), lambda i, j: (j, 2 * sel + 1, 0, 0)),
                  pl.BlockSpec((None, None, FF_BLK, D_MODEL), lambda i, j: (j, sel, 0, 0))],
        out_specs=[row, hid_blk, hid_blk, vec],
        out_shape=[jax.ShapeDtypeStruct((t_rows, D_MODEL), F32),
                   jax.ShapeDtypeStruct((nj, t_rows, FF_BLK), BF16), jax.ShapeDtypeStruct((nj, t_rows, FF_BLK), BF16),
                   jax.ShapeDtypeStruct((1, D_MODEL), F32)],
        scratch_shapes=[pltpu.VMEM((tm, D_MODEL), BF16), pltpu.VMEM((tm, D_MODEL), F32)],
        compiler_params=_cp(("arbitrary", "arbitrary"), VMEM_BIG),
    )(dh, h, norm, a, b, w13g, w13g, w2g)


def ffn_backward_weights(hn, dh, a, b, da, db, tm, name):
    t_rows = hn.shape[0]
    ni = t_rows // tm

    def body(hn_ref, dh_ref, a_ref, b_ref, da_ref, db_ref, dw1_ref, dw3_ref, dw2_ref, acc1, acc3, acc2):
        i = pl.program_id(1)

        @pl.when(i == 0)
        def _():
            acc1[...] = jnp.zeros_like(acc1)
            acc3[...] = jnp.zeros_like(acc3)
            acc2[...] = jnp.zeros_like(acc2)

        hn_v = hn_ref[...]
        acc1[...] += _dot_tn(hn_v, da_ref[...])
        acc3[...] += _dot_tn(hn_v, db_ref[...])
        av = a_ref[...].astype(F32)
        hid = (av * _sigmoid(av) * b_ref[...].astype(F32)).astype(BF16)
        acc2[...] += _dot_tn(hid, (0.5 * dh_ref[...]).astype(BF16))

        @pl.when(i == ni - 1)
        def _():
            dw1_ref[...] = acc1[...].astype(BF16)
            dw3_ref[...] = acc3[...].astype(BF16)
            dw2_ref[...] = acc2[...].astype(BF16)

    row = pl.BlockSpec((tm, D_MODEL), lambda j, i: (i, 0))
    hid_blk = pl.BlockSpec((None, tm, FF_BLK), lambda j, i: (j, i, 0))
    w_in_blk = pl.BlockSpec((None, D_MODEL, FF_BLK), lambda j, i: (j, 0, 0))
    w_out_blk = pl.BlockSpec((None, FF_BLK, D_MODEL), lambda j, i: (j, 0, 0))
    return _pcall(
        body, name=name, grid=(N_DEV, ni),
        in_specs=[row, row, hid_blk, hid_blk, hid_blk, hid_blk],
        out_specs=[w_in_blk, w_in_blk, w_out_blk],
        out_shape=[jax.ShapeDtypeStruct((N_DEV, D_MODEL, FF_BLK), BF16), jax.ShapeDtypeStruct((N_DEV, D_MODEL, FF_BLK), BF16),
                   jax.ShapeDtypeStruct((N_DEV, FF_BLK, D_MODEL), BF16)],
        scratch_shapes=[pltpu.VMEM((D_MODEL, FF_BLK), F32), pltpu.VMEM((D_MODEL, FF_BLK), F32),
                        pltpu.VMEM((FF_BLK, D_MODEL), F32)],
        compiler_params=_cp(("arbitrary", "arbitrary"), VMEM_BIG),
    )(hn, dh, a, b, da, db)


def mix_forward(h, norm, wing, tm, name):
    t_rows = h.shape[0]

    def body(h_ref, g_ref, w_ref, hn_ref, p_ref):
        @pl.when(pl.program_id(1) == 0)
        def _():
            hn_ref[...] = _rms_fwd(h_ref[...], g_ref[...]).astype(BF16)

        p_ref[...] = _dot(hn_ref[...], w_ref[...]).astype(BF16)

    row = pl.BlockSpec((tm, D_MODEL), lambda i, j: (i, 0))
    return _pcall(
        body, name=name, grid=(t_rows // tm, N_DEV),
        in_specs=[row, pl.BlockSpec((1, D_MODEL), lambda i, j: (0, 0)),
                  pl.BlockSpec((None, D_MODEL, IN_BLK), lambda i, j: (j, 0, 0))],
        out_specs=[row, pl.BlockSpec((tm, IN_BLK), lambda i, j: (i, j))],
        out_shape=[jax.ShapeDtypeStruct((t_rows, D_MODEL), BF16), jax.ShapeDtypeStruct((t_rows, IN_WIDTH), BF16)],
        compiler_params=_cp(("arbitrary", "arbitrary"), VMEM_BIG),
    )(h, norm, wing)


def mix_backward_act(dh, h, norm, dproj, wing, tm, name):
    t_rows = h.shape[0]

    def body(dh_ref, h_ref, g_ref, dp_ref, w_ref, dhin_ref, dg_ref, acc_ref):
        i, j = pl.program_id(0), pl.program_id(1)

        @pl.when(j == 0)
        def _():
            acc_ref[...] = jnp.zeros_like(acc_ref)

        acc_ref[...] += _dot_nt(dp_ref[...], w_ref[...])

        @pl.when(j == N_DEV - 1)
        def _():
            dx, dg = _rms_bwd(h_ref[...], g_ref[...], acc_ref[...])
            dhin_ref[...] = dh_ref[...] + dx
            _accumulate(dg_ref, dg, i == 0)

    row = pl.BlockSpec((tm, D_MODEL), lambda i, j: (i, 0))
    vec = pl.BlockSpec((1, D_MODEL), lambda i, j: (0, 0))
    return _pcall(
        body, name=name, grid=(t_rows // tm, N_DEV),
        in_specs=[row, row, vec, pl.BlockSpec((tm, IN_BLK), lambda i, j: (i, j)),
                  pl.BlockSpec((None, D_MODEL, IN_BLK), lambda i, j: (j, 0, 0))],
        out_specs=[row, vec],
        out_shape=[jax.ShapeDtypeStruct((t_rows, D_MODEL), F32), jax.ShapeDtypeStruct((1, D_MODEL), F32)],
        scratch_shapes=[pltpu.VMEM((tm, D_MODEL), F32)],
        compiler_params=_cp(("arbitrary", "arbitrary"), VMEM_BIG),
    )(dh, h, norm, dproj, wing)


def mix_backward_weights(hn, dproj, tm, name):
    t_rows = hn.shape[0]
    ni = t_rows // tm

    def body(hn_ref, dp_ref, dw_ref, acc):
        i = pl.program_id(1)

        @pl.when(i == 0)
        def _():
            acc[...] = jnp.zeros_like(acc)

        acc[...] += _dot_tn(hn_ref[...], dp_ref[...])

        @pl.when(i == ni - 1)
        def _():
            dw_ref[...] = acc[...].astype(BF16)

    return _pcall(
        body, name=name, grid=(N_DEV, ni),
        in_specs=[pl.BlockSpec((tm, D_MODEL), lambda j, i: (i, 0)), pl.BlockSpec((tm, IN_BLK), lambda j, i: (i, j))],
        out_specs=pl.BlockSpec((None, D_MODEL, IN_BLK), lambda j, i: (j, 0, 0)),
        out_shape=jax.ShapeDtypeStruct((N_DEV, D_MODEL, IN_BLK), BF16),
        scratch_shapes=[pltpu.VMEM((D_MODEL, IN_BLK), F32)],
        compiler_params=_cp(("arbitrary", "arbitrary"), VMEM_BIG),
    )(hn, dproj)


GELU_C = 0.7978845608028654
GELU_K = 0.044715


def _gelu(x):
    return 0.5 * x * (1.0 + jnp.tanh(GELU_C * (x + GELU_K * (x * x * x))))


def _gelu_and_grad(x):
    th = jnp.tanh(GELU_C * (x + GELU_K * (x * x * x)))
    val = 0.5 * x * (1.0 + th)
    grad = 0.5 * (1.0 + th) + 0.5 * x * (1.0 - th * th) * (GELU_C * (1.0 + 3.0 * GELU_K * (x * x)))
    return val, grad


def merge_forward(h, yraw, attn, proj, glu_a, glu_b, w_out, tm, name):
    t_rows = h.shape[0]

    def body(h_ref, y_ref, at_ref, gate_ref, a_ref, b_ref, wo_ref, out_ref):
        y = _gelu(y_ref[...]).astype(BF16)
        ssm = _dot(y, a_ref[...]) * _sigmoid(_dot(y, b_ref[...]))
        ga = gate_ref[:, :D_MODEL].astype(F32)
        gs = gate_ref[:, D_MODEL:].astype(F32)
        merged = _sigmoid(ga) * at_ref[...].astype(F32) + _sigmoid(gs) * ssm
        out_ref[...] = h_ref[...] + _dot(merged.astype(BF16), wo_ref[...])

    row = pl.BlockSpec((tm, D_MODEL), lambda i: (i, 0))
    glu = pl.BlockSpec((SSM_WIDTH, D_MODEL), lambda i: (0, 0))
    return _pcall(
        body, name=name, grid=(t_rows // tm,),
        in_specs=[row, pl.BlockSpec((tm, SSM_WIDTH), lambda i: (i, 0)), row,
                  pl.BlockSpec((tm, 2 * D_MODEL), lambda i: (i, 1)), glu, glu,
                  pl.BlockSpec((D_MODEL, D_MODEL), lambda i: (0, 0))],
        out_specs=row, out_shape=jax.ShapeDtypeStruct((t_rows, D_MODEL), F32),
        compiler_params=_cp(("arbitrary",), VMEM_BIG),
    )(h, yraw, attn, proj, glu_a, glu_b, w_out)


def merge_backward(dh, yraw, attn, proj, glu_a, glu_b, w_out, tm, name):
    t_rows = dh.shape[0]

    def body(dh_ref, y_ref, at_ref, gate_ref, a_ref, b_ref, wo_ref,
             dat_ref, dy_ref, dgate_ref, dwo_ref, da_ref, db_ref):
        first = pl.program_id(0) == 0
        d16 = dh_ref[...].astype(BF16)
        dmerged = _dot_nt(d16, wo_ref[...])
        gel, dgel = _gelu_and_grad(y_ref[...].astype(F32))
        y16 = gel.astype(BF16)
        ya = _dot(y16, a_ref[...])
        sb = _sigmoid(_dot(y16, b_ref[...]))
        ssm = ya * sb
        sa = _sigmoid(gate_ref[:, :D_MODEL].astype(F32))
        ss = _sigmoid(gate_ref[:, D_MODEL:].astype(F32))
        attn_v = at_ref[...].astype(F32)
        merged = (sa * attn_v + ss * ssm).astype(BF16)
        _accumulate(dwo_ref, _dot_tn(merged, d16), first)
        dat_ref[...] = (dmerged * sa).astype(BF16)
        dgate_ref[:, :D_MODEL] = (dmerged * attn_v * sa * (1.0 - sa)).astype(BF16)
        dgate_ref[:, D_MODEL:] = (dmerged * ssm * ss * (1.0 - ss)).astype(BF16)
        dssm = dmerged * ss
        dya = (dssm * sb).astype(BF16)
        dyb = (dssm * ya * sb * (1.0 - sb)).astype(BF16)
        _accumulate(da_ref, _dot_tn(y16, dya), first)
        _accumulate(db_ref, _dot_tn(y16, dyb), first)
        dy = _dot_nt(dya, a_ref[...]) + _dot_nt(dyb, b_ref[...])
        dy_ref[...] = (dy * dgel).astype(BF16)

    row = pl.BlockSpec((tm, D_MODEL), lambda i: (i, 0))
    ssm_row = pl.BlockSpec((tm, SSM_WIDTH), lambda i: (i, 0))
    gates = pl.BlockSpec((tm, 2 * D_MODEL), lambda i: (i, 1))
    glu = pl.BlockSpec((SSM_WIDTH, D_MODEL), lambda i: (0, 0))
    wo = pl.BlockSpec((D_MODEL, D_MODEL), lambda i: (0, 0))
    return _pcall(
        body, name=name, grid=(t_rows // tm,),
        in_specs=[row, ssm_row, row, gates, glu, glu, wo],
        out_specs=[row, ssm_row, gates, wo, glu, glu],
        out_shape=[jax.ShapeDtypeStruct((t_rows, D_MODEL), BF16), jax.ShapeDtypeStruct((t_rows, SSM_WIDTH), BF16),
                   jax.ShapeDtypeStruct((t_rows, IN_WIDTH), BF16), jax.ShapeDtypeStruct((D_MODEL, D_MODEL), F32),
                   jax.ShapeDtypeStruct((SSM_WIDTH, D_MODEL), F32), jax.ShapeDtypeStruct((SSM_WIDTH, D_MODEL), F32)],
        compiler_params=_cp(("arbitrary",), VMEM_BIG),
    )(dh, yraw, attn, proj, glu_a, glu_b, w_out)


def final_loss_backward(h, target, norm, seq, tm, name):
    t_rows = h.shape[0]
    tiles_per_example = (seq + N_META) // tm

    def body(h_ref, t_ref, g_ref, dh_ref, loss_ref, dg_ref):
        i = pl.program_id(0)
        x = h_ref[...]
        g = g_ref[...]
        r = lax.rsqrt(jnp.mean(x * x, axis=-1, keepdims=True) + NORM_EPS)
        xh = x * r
        pos = lax.broadcasted_iota(jnp.int32, (tm, 1), 0) + (i % tiles_per_example) * tm
        diff = jnp.where(pos < seq, xh * g - t_ref[...], 0.0)
        part = 0.5 * jnp.sum(jnp.sum(diff * diff, axis=-1, keepdims=True), axis=0, keepdims=True) / D_MODEL
        dy = diff / D_MODEL
        t = dy * g
        dh_ref[...] = r * (t - xh * jnp.mean(t * xh, axis=-1, keepdims=True))
        _accumulate(loss_ref, jnp.broadcast_to(part, (1, LANES)), i == 0)
        _accumulate(dg_ref, jnp.sum(dy * xh, axis=0, keepdims=True), i == 0)

    row = pl.BlockSpec((tm, D_MODEL), lambda i: (i, 0))
    vec = pl.BlockSpec((1, D_MODEL), lambda i: (0, 0))
    return _pcall(
        body, name=name, grid=(t_rows // tm,),
        in_specs=[row, row, vec],
        out_specs=[row, pl.BlockSpec((1, LANES), lambda i: (0, 0)), vec],
        out_shape=[jax.ShapeDtypeStruct((t_rows, D_MODEL), F32), jax.ShapeDtypeStruct((1, LANES), F32),
                   jax.ShapeDtypeStruct((1, D_MODEL), F32)],
        compiler_params=_cp(("arbitrary",), VMEM_BIG),
    )(h, target, norm)


ATTN_SCALE = HEAD_DIM ** -0.5


def _lane_half(shape, hf):
    lane = lax.broadcasted_iota(jnp.int32, shape, 1)
    return (lane < HEAD_DIM) if hf == 0 else (lane >= HEAD_DIM)


def _kv_variants(ref, rows, kh):
    tile = kh // 2
    t = ref[rows, tile * LANES:(tile + 1) * LANES].astype(F32)
    swapped = pltpu.roll(t, HEAD_DIM, axis=1)
    at_low, at_high = (t, swapped) if kh % 2 == 0 else (swapped, t)
    lo = jnp.where(_lane_half(t.shape, 0), at_low, 0.0).astype(BF16)
    hi = jnp.where(_lane_half(t.shape, 1), at_high, 0.0).astype(BF16)
    return lo, hi


def _softmax_parts(qt, key_tiles, masks, sink):
    scores = []
    for kt, mask in zip(key_tiles, masks):
        s = _dot_nt(qt, kt) * ATTN_SCALE
        scores.append(s if mask is None else jnp.where(mask, s, NEG_INF))
    m = functools.reduce(jnp.maximum, [jnp.max(s, axis=-1, keepdims=True) for s in scores])
    m = jnp.maximum(m, sink)
    probs = [jnp.exp(s - m) for s in scores]
    e_sink = jnp.exp(sink - m)
    den = functools.reduce(lambda u, w: u + w, [jnp.sum(p, axis=-1, keepdims=True) for p in probs]) + e_sink
    return probs, 1.0 / den, e_sink


def _band_masks(n):
    qi = lax.broadcasted_iota(jnp.int32, (BLOCK, BLOCK), 0)
    kj = lax.broadcasted_iota(jnp.int32, (BLOCK, BLOCK), 1)
    return jnp.logical_and(kj > qi, n > 0), kj <= qi


def _meta_mask():
    qi = lax.broadcasted_iota(jnp.int32, (N_META, N_META), 0)
    kj = lax.broadcasted_iota(jnp.int32, (N_META, N_META), 1)
    return kj <= qi


def attention_forward(proj3, sinks, seq, name):
    n_b, n_l, _ = proj3.shape
    n_blocks = seq // BLOCK
    meta_rows = pl.ds(seq, N_META)

    def body(sink_ref, q_ref, k_ref, v_ref, o_ref):
        def queries(q_rows, key_rows, masks):
            for kh in range(N_KV_HEADS):
                kv = [(_kv_variants(k_ref, r, kh), _kv_variants(v_ref, r, kh)) for r in key_rows]
                for t in range(2):
                    col = kh * 2 * LANES + t * LANES
                    qt = q_ref[q_rows, col:col + LANES]
                    out = None
                    for hf in range(2):
                        sink = sink_ref[0, kh * Q_PER_KV + 2 * t + hf]
                        probs, inv, _ = _softmax_parts(qt, [k[hf] for k, _ in kv], masks, sink)
                        o = functools.reduce(lambda u, w: u + w,
                                             [_dot(p.astype(BF16), v[hf]) for p, (_, v) in zip(probs, kv)]) * inv
                        out = o if out is None else out + o
                    o_ref[q_rows, col:col + LANES] = out.astype(BF16)

        def block(n, carry):
            r0 = pl.multiple_of(n * BLOCK, BLOCK)
            p0 = pl.multiple_of(jnp.maximum(n - 1, 0) * BLOCK, BLOCK)
            prev_ok, cur_ok = _band_masks(n)
            queries(pl.ds(r0, BLOCK), [pl.ds(p0, BLOCK), pl.ds(r0, BLOCK), meta_rows], [prev_ok, cur_ok, None])
            return carry

        lax.fori_loop(0, n_blocks, block, 0)
        queries(meta_rows, [meta_rows], [_meta_mask()])

    return _pcall(
        body, name=name, grid=(n_b,),
        in_specs=[pl.BlockSpec(memory_space=pltpu.SMEM),
                  pl.BlockSpec((None, n_l, D_MODEL), lambda b: (b, 0, 0)),
                  pl.BlockSpec((None, n_l, KV_WIDTH), lambda b: (b, 0, D_MODEL // KV_WIDTH)),
                  pl.BlockSpec((None, n_l, KV_WIDTH), lambda b: (b, 0, D_MODEL // KV_WIDTH + 1))],
        out_specs=pl.BlockSpec((None, n_l, D_MODEL), lambda b: (b, 0, 0)),
        out_shape=jax.ShapeDtypeStruct((n_b, n_l, D_MODEL), BF16),
        compiler_params=_cp(("arbitrary",), VMEM_BIG),
    )(sinks, proj3, proj3, proj3)


def attention_backward(proj3, dattn3, dproj3, sinks, seq, name):
    n_b, n_l, _ = proj3.shape
    n_blocks = seq // BLOCK
    meta_rows = pl.ds(seq, N_META)
    qkv_width = D_MODEL + 2 * KV_WIDTH

    def body(sink_ref, q_ref, k_ref, v_ref, do_ref, _, dqkv_ref, dsink_ref, dk_ref, dv_ref):
        dk_ref[...] = jnp.zeros_like(dk_ref)
        dv_ref[...] = jnp.zeros_like(dv_ref)
        sub = lax.broadcasted_iota(jnp.int32, (SUBLANES, LANES), 0)
        lane = lax.broadcasted_iota(jnp.int32, (SUBLANES, LANES), 1)

        def queries(q_rows, key_rows, masks, dsink):
            for kh in range(N_KV_HEADS):
                kv = [(_kv_variants(k_ref, r, kh), _kv_variants(v_ref, r, kh)) for r in key_rows]
                acc = [[[None, None], [None, None]] for _ in key_rows]
                for t in range(2):
                    col = kh * 2 * LANES + t * LANES
                    qt = q_ref[q_rows, col:col + LANES]
                    dot = do_ref[q_rows, col:col + LANES]
                    dq = None
                    for hf in range(2):
                        head = kh * Q_PER_KV + 2 * t + hf
                        sink = sink_ref[0, head]
                        probs, inv, e_sink = _softmax_parts(qt, [k[hf] for k, _ in kv], masks, sink)
                        probs = [p * inv for p in probs]
                        dps = [_dot_nt(dot, v[hf]) for _, v in kv]
                        delta = functools.reduce(
                            lambda u, w: u + w, [jnp.sum(p * dp, axis=-1, keepdims=True) for p, dp in zip(probs, dps)])
                        d_here = jnp.sum(-(e_sink * inv) * delta, axis=0, keepdims=True)
                        dsink = dsink + jnp.where(jnp.logical_and(sub == 0, lane == head), d_here, 0.0)
                        other = 0 if hf == kh % 2 else 1
                        for x, (p, dp, (k, _)) in enumerate(zip(probs, dps, kv)):
                            ds = (p * (dp - delta)).astype(BF16)
                            dq_x = _dot(ds, k[hf])
                            dq = dq_x if dq is None else dq + dq_x
                            zk = _dot_tn(ds, qt)
                            zv = _dot_tn(p.astype(BF16), dot)
                            half = _lane_half(zk.shape, hf)
                            for which, z in enumerate((zk, zv)):
                                z = jnp.where(half, z, 0.0)
                                old = acc[x][which][other]
                                acc[x][which][other] = z if old is None else old + z
                    dqkv_ref[q_rows, col:col + LANES] = (dq * ATTN_SCALE).astype(BF16)
                tile = slice((kh // 2) * LANES, (kh // 2 + 1) * LANES)
                for x, r in enumerate(key_rows):
                    dk_ref[r, tile] += (acc[x][0][0] + pltpu.roll(acc[x][0][1], HEAD_DIM, axis=1)) * ATTN_SCALE
                    dv_ref[r, tile] += acc[x][1][0] + pltpu.roll(acc[x][1][1], HEAD_DIM, axis=1)
            return dsink

        def block(n, dsink):
            r0 = pl.multiple_of(n * BLOCK, BLOCK)
            p0 = pl.multiple_of(jnp.maximum(n - 1, 0) * BLOCK, BLOCK)
            prev_ok, cur_ok = _band_masks(n)
            return queries(pl.ds(r0, BLOCK), [pl.ds(p0, BLOCK), pl.ds(r0, BLOCK), meta_rows],
                           [prev_ok, cur_ok, None], dsink)

        dsink = lax.fori_loop(0, n_blocks, block, jnp.zeros((SUBLANES, LANES), F32))
        dsink_ref[...] = queries(meta_rows, [meta_rows], [_meta_mask()], dsink)
        dqkv_ref[:, D_MODEL:D_MODEL + KV_WIDTH] = dk_ref[...].astype(BF16)
        dqkv_ref[:, D_MODEL + KV_WIDTH:] = dv_ref[...].astype(BF16)

    return _pcall(
        body, name=name, grid=(n_b,),
        in_specs=[pl.BlockSpec(memory_space=pltpu.SMEM),
                  pl.BlockSpec((None, n_l, D_MODEL), lambda b: (b, 0, 0)),
                  pl.BlockSpec((None, n_l, KV_WIDTH), lambda b: (b, 0, D_MODEL // KV_WIDTH)),
                  pl.BlockSpec((None, n_l, KV_WIDTH), lambda b: (b, 0, D_MODEL // KV_WIDTH + 1)),
                  pl.BlockSpec((None, n_l, D_MODEL), lambda b: (b, 0, 0)),
                  pl.BlockSpec(memory_space=pl.ANY)],
        out_specs=[pl.BlockSpec((None, n_l, qkv_width), lambda b: (b, 0, 0)),
                   pl.BlockSpec((None, SUBLANES, LANES), lambda b: (b, 0, 0))],
        out_shape=[jax.ShapeDtypeStruct(dproj3.shape, BF16), jax.ShapeDtypeStruct((n_b, SUBLANES, LANES), F32)],
        scratch_shapes=[pltpu.VMEM((n_l, KV_WIDTH), F32), pltpu.VMEM((n_l, KV_WIDTH), F32)],
        input_output_aliases={5: 0},
        compiler_params=_cp(("arbitrary",), VMEM_BIG),
    )(sinks, proj3, proj3, proj3, dattn3, dproj3)


TAB_ROWS = 8


def _cmul(ar, ai, br, bi):
    return ar * br - ai * bi, ar * bi + ai * br


def _discretise(ar, ai, ls):
    step = jnp.exp(ls)
    mag = jnp.exp(ar * step)
    ang = ai * step
    cos, sin = jnp.cos(ang), jnp.sin(ang)
    lr, li = mag * cos, mag * sin
    den = ar * ar + ai * ai
    nr, ni = lr - 1.0, li
    cr = (nr * ar + ni * ai) / den
    ci = (ni * ar - nr * ai) / den
    return step, mag, lr, li, den, nr, ni, cr, ci


def _scan_tables(lr, li, reverse):
    n = lr.shape[-1]
    pw = [(lr, li)]
    for _ in range(SUBLANES - 1):
        pw.append(_cmul(pw[-1][0], pw[-1][1], lr, li))
    row = lax.broadcasted_iota(jnp.int32, (SUBLANES, n), 0)
    out = []
    for d in (1, 2, 4):
        ok = (row + d <= SUBLANES - 1) if reverse else (row >= d)
        out += [jnp.where(ok, pw[d - 1][0], 0.0), jnp.where(ok, pw[d - 1][1], 0.0)]
    cr = jnp.zeros((SUBLANES, n), F32)
    ci = jnp.zeros((SUBLANES, n), F32)
    for r in range(SUBLANES):
        e = (SUBLANES - r) if reverse else (r + 1)
        cr = jnp.where(row == r, pw[e - 1][0], cr)
        ci = jnp.where(row == r, pw[e - 1][1], ci)
    return out + [cr, ci]


def ssm_prepare(ar, ai, ls, br_t, bi_t, name):
    def body(ar_ref, ai_ref, ls_ref, br_ref, bi_ref, bbr_ref, bbi_ref, tf_ref, tr_ref):
        _, _, lr, li, _, _, _, cr, ci = _discretise(ar_ref[...], ai_ref[...], ls_ref[...])
        br, bi = br_ref[...], bi_ref[...]
        bbr_ref[...] = cr * br - ci * bi
        bbi_ref[...] = cr * bi + ci * br
        for k, t in enumerate(_scan_tables(lr, li, False)):
            tf_ref[k] = t
        for k, t in enumerate(_scan_tables(lr, -li, True)):
            tr_ref[k] = t

    return _pcall(
        body, name=name,
        out_shape=[jax.ShapeDtypeStruct((SSM_GROUP, N_STATES), F32), jax.ShapeDtypeStruct((SSM_GROUP, N_STATES), F32),
                   jax.ShapeDtypeStruct((TAB_ROWS, SUBLANES, N_STATES), F32),
                   jax.ShapeDtypeStruct((TAB_ROWS, SUBLANES, N_STATES), F32)],
    )(ar, ai, ls, br_t, bi_t)


def ssm_param_backward(ar, ai, ls, br_t, bi_t, dlr_p, dli_p, dbbr, dbbi, group_sum, name):
    def body(ar_ref, ai_ref, ls_ref, br_ref, bi_ref, dlr_ref, dli_ref, dbbr_ref, dbbi_ref, gs_ref,
             dar_ref, dai_ref, dls_ref, dbr_ref, dbi_ref):
        ar, ai = ar_ref[...], ai_ref[...]
        step, mag, lr, li, den, nr, ni, cr, ci = _discretise(ar, ai, ls_ref[...])
        br, bi, dbbr_v, dbbi_v = br_ref[...], bi_ref[...], dbbr_ref[...], dbbi_ref[...]
        dbr_ref[...] = cr * dbbr_v + ci * dbbi_v
        dbi_ref[...] = cr * dbbi_v - ci * dbbr_v
        dcr = jnp.sum(dbbr_v * br + dbbi_v * bi, axis=0, keepdims=True)
        dci = jnp.sum(dbbi_v * br - dbbr_v * bi, axis=0, keepdims=True)
        dnr = (dcr * ar - dci * ai) / den
        dni = (dcr * ai + dci * ar) / den
        dden = -(cr * dcr + ci * dci) / den
        dar = (dcr * nr + dci * ni) / den + dden * 2.0 * ar
        dai = (dcr * ni - dci * nr) / den + dden * 2.0 * ai
        dlr = jnp.sum(dlr_ref[...], axis=0, keepdims=True) + dnr
        dli = jnp.sum(dli_ref[...], axis=0, keepdims=True) + dni
        dmag = (dlr * lr + dli * li) / mag
        dang = dli * lr - dlr * li
        dar_ref[...] = dar + dmag * mag * step
        dai_ref[...] = dai + dang * step
        dstep = dmag * mag * ar + dang * ai
        dls_ref[...] = jnp.dot(dstep * step, gs_ref[...], preferred_element_type=F32, precision=lax.Precision.HIGHEST)

    vec = jax.ShapeDtypeStruct((1, N_STATES), F32)
    mat = jax.ShapeDtypeStruct((SSM_GROUP, N_STATES), F32)
    return _pcall(body, name=name, out_shape=[vec, vec, jax.ShapeDtypeStruct((1, LANES), F32), mat, mat])(
        ar, ai, ls, br_t, bi_t, dlr_p, dli_p, dbbr, dbbi, group_sum)


def _scan_rows(a, b, tabs, carry, reverse):
    for k, d in enumerate((1, 2, 4)):
        shift = SUBLANES - d if reverse else d
        sr, si = pltpu.roll(a, shift, axis=0), pltpu.roll(b, shift, axis=0)
        pr, pi = _cmul(tabs[2 * k], tabs[2 * k + 1], sr, si)
        a, b = a + pr, b + pi
    pr, pi = _cmul(tabs[6], tabs[7], carry[0], carry[1])
    return a + pr, b + pi


def _time_groups(seq, reverse):
    meta = [seq + SUBLANES * g for g in range(N_META // SUBLANES)]
    return meta[::-1] if reverse else meta


def ssm_forward_scan(proj3, b_comb, tabf, c_comb, dvec, seq, name):
    n_b, n_l, _ = proj3.shape
    u_blk = (D_MODEL + 2 * KV_WIDTH) // SSM_WIDTH

    def body(u_ref, b_ref, tab_ref, c_ref, d_ref, x_ref, y_ref, bu, xs):
        j = pl.program_id(1)
        u = u_ref[...]
        bu[...] = _dot(u, b_ref[...])
        tabs = [tab_ref[k] for k in range(TAB_ROWS)]

        def group(r0, carry):
            rows = pl.ds(r0, SUBLANES)
            a, b = _scan_rows(bu[rows, :SCAN_COLS], bu[rows, SCAN_COLS:], tabs, carry, False)
            xs[rows, :SCAN_COLS] = a
            xs[rows, SCAN_COLS:] = b
            return (jnp.broadcast_to(a[SUBLANES - 1:, :], a.shape), jnp.broadcast_to(b[SUBLANES - 1:, :], b.shape))

        zero = jnp.zeros((SUBLANES, SCAN_COLS), F32)
        carry = (zero, zero)
        for r0 in _time_groups(seq, False):
            carry = group(r0, carry)
        lax.fori_loop(0, seq // SUBLANES, lambda g, c: group(pl.multiple_of(g * SUBLANES, SUBLANES), c), carry)
        x16 = xs[...].astype(BF16)
        x_ref[...] = x16
        contrib = _dot(x16, c_ref[...])

        @pl.when(j == 0)
        def _():
            y_ref[...] = contrib + d_ref[...] * u.astype(F32)

        @pl.when(j > 0)
        def _():
            y_ref[...] += contrib

    return _pcall(
        body, name=name, grid=(n_b, N_SCAN_BLK),
        in_specs=[pl.BlockSpec((None, n_l, SSM_WIDTH), lambda b, j: (b, 0, u_blk)),
                  pl.BlockSpec((SSM_WIDTH, 2 * SCAN_COLS), lambda b, j: (0, j)),
                  pl.BlockSpec((TAB_ROWS, SUBLANES, SCAN_COLS), lambda b, j: (0, 0, j)),
                  pl.BlockSpec((2 * SCAN_COLS, SSM_WIDTH), lambda b, j: (j, 0)),
                  pl.BlockSpec((1, SSM_WIDTH), lambda b, j: (0, 0))],
        out_specs=[pl.BlockSpec((None, n_l, 2 * SCAN_COLS), lambda b, j: (b, 0, j)),
                   pl.BlockSpec((None, n_l, SSM_WIDTH), lambda b, j: (b, 0, 0))],
        out_shape=[jax.ShapeDtypeStruct((n_b, n_l, 2 * N_STATES), BF16),
                   jax.ShapeDtypeStruct((n_b, n_l, SSM_WIDTH), F32)],
        scratch_shapes=[pltpu.VMEM((n_l, 2 * SCAN_COLS), F32)] * 2,
        compiler_params=_cp(("arbitrary", "arbitrary"), VMEM_BIG),
    )(proj3, b_comb, tabf, c_comb, dvec)


def ssm_backward_scan(dyraw3, xs3, dproj3, c_comb_t, tabr, b_comb_t, dvec, seq, name):
    n_b, n_l, _ = xs3.shape
    u_blk = (D_MODEL + 2 * KV_WIDTH) // SSM_WIDTH

    def body(dy_ref, x_ref, _, c_ref, tab_ref, b_ref, d_ref, du_ref, g_ref, dlr_ref, dli_ref, dx, gs, xs, du_acc):
        j = pl.program_id(1)
        dy = dy_ref[...]
        dx[...] = _dot(dy, c_ref[...])
        xs[...] = x_ref[...].astype(F32)
        tabs = [tab_ref[k] for k in range(TAB_ROWS)]
        last_row = lax.broadcasted_iota(jnp.int32, (SUBLANES, SCAN_COLS), 0) == SUBLANES - 1

        def group(r0, state):
            cr, ci, acc_r, acc_i = state
            rows = pl.ds(r0, SUBLANES)
            a, b = _scan_rows(dx[rows, :SCAN_COLS], dx[rows, SCAN_COLS:], tabs, (cr, ci), True)
            gs[rows, :SCAN_COLS] = a
            gs[rows, SCAN_COLS:] = b
            na = jnp.where(last_row, cr, pltpu.roll(a, SUBLANES - 1, axis=0))
            nb = jnp.where(last_row, ci, pltpu.roll(b, SUBLANES - 1, axis=0))
            xa, xb = xs[rows, :SCAN_COLS], xs[rows, SCAN_COLS:]
            return (jnp.broadcast_to(a[:1, :], a.shape), jnp.broadcast_to(b[:1, :], b.shape),
                    acc_r + na * xa + nb * xb, acc_i + nb * xa - na * xb)

        zero = jnp.zeros((SUBLANES, SCAN_COLS), F32)
        n_groups = seq // SUBLANES
        state = lax.fori_loop(
            0, n_groups, lambda g, s: group(pl.multiple_of((n_groups - 1 - g) * SUBLANES, SUBLANES), s),
            (zero, zero, zero, zero))
        for r0 in _time_groups(seq, True):
            state = group(r0, state)
        dlr_ref[...] = state[2]
        dli_ref[...] = state[3]
        g16 = gs[...].astype(BF16)
        g_ref[...] = g16
        contrib = _dot(g16, b_ref[...])

        @pl.when(j == 0)
        def _():
            du_acc[...] = contrib + d_ref[...] * dy.astype(F32)

        @pl.when(j > 0)
        def _():
            du_acc[...] += contrib

        @pl.when(j == N_SCAN_BLK - 1)
        def _():
            du_ref[...] = du_acc[...].astype(BF16)

    state_blk = pl.BlockSpec((None, n_l, 2 * SCAN_COLS), lambda b, j: (b, 0, j))
    dl_blk = pl.BlockSpec((None, SUBLANES, SCAN_COLS), lambda b, j: (b, 0, j))
    return _pcall(
        body, name=name, grid=(n_b, N_SCAN_BLK),
        in_specs=[pl.BlockSpec((None, n_l, SSM_WIDTH), lambda b, j: (b, 0, 0)), state_blk,
                  pl.BlockSpec(memory_space=pl.ANY),
                  pl.BlockSpec((SSM_WIDTH, 2 * SCAN_COLS), lambda b, j: (0, j)),
                  pl.BlockSpec((TAB_ROWS, SUBLANES, SCAN_COLS), lambda b, j: (0, 0, j)),
                  pl.BlockSpec((2 * SCAN_COLS, SSM_WIDTH), lambda b, j: (j, 0)),
                  pl.BlockSpec((1, SSM_WIDTH), lambda b, j: (0, 0))],
        out_specs=[pl.BlockSpec((None, n_l, SSM_WIDTH), lambda b, j: (b, 0, u_blk)), state_blk, dl_blk, dl_blk],
        out_shape=[jax.ShapeDtypeStruct(dproj3.shape, BF16), jax.ShapeDtypeStruct((n_b, n_l, 2 * N_STATES), BF16),
                   jax.ShapeDtypeStruct((n_b, SUBLANES, N_STATES), F32), jax.ShapeDtypeStruct((n_b, SUBLANES, N_STATES), F32)],
        scratch_shapes=[pltpu.VMEM((n_l, 2 * SCAN_COLS), F32)] * 3 + [pltpu.VMEM((n_l, SSM_WIDTH), F32)],
        input_output_aliases={2: 0},
        compiler_params=_cp(("arbitrary", "arbitrary"), VMEM_BIG),
    )(dyraw3, xs3, dproj3, c_comb_t, tabr, b_comb_t, dvec)


def ssm_param_grads(proj, gs, xs, dyraw, tm, name):
    t_rows = proj.shape[0]
    ni = t_rows // tm
    n_cb = 2 * N_STATES // SSM_WIDTH
    u_blk = (D_MODEL + 2 * KV_WIDTH) // SSM_WIDTH

    def body(u_ref, g_ref, x_ref, dy_ref, db_ref, dc_ref, dd_ref, acc_b, acc_c):
        cb, i = pl.program_id(0), pl.program_id(1)

        @pl.when(i == 0)
        def _():
            acc_b[...] = jnp.zeros_like(acc_b)
            acc_c[...] = jnp.zeros_like(acc_c)

        u, dy = u_ref[...], dy_ref[...]
        acc_b[...] += _dot_tn(u, g_ref[...])
        acc_c[...] += _dot_tn(x_ref[...], dy)

        @pl.when(i == ni - 1)
        def _():
            db_ref[...] = acc_b[...]
            dc_ref[...] = acc_c[...]

        @pl.when(cb == 0)
        def _():
            _accumulate(dd_ref, jnp.sum(dy.astype(F32) * u.astype(F32), axis=0, keepdims=True), i == 0)

    sq = (SSM_WIDTH, SSM_WIDTH)
    return _pcall(
        body, name=name, grid=(n_cb, ni),
        in_specs=[pl.BlockSpec((tm, SSM_WIDTH), lambda cb, i: (i, u_blk)),
                  pl.BlockSpec((tm, SSM_WIDTH), lambda cb, i: (i, cb)),
                  pl.BlockSpec((tm, SSM_WIDTH), lambda cb, i: (i, cb)),
                  pl.BlockSpec((tm, SSM_WIDTH), lambda cb, i: (i, 0))],
        out_specs=[pl.BlockSpec(sq, lambda cb, i: (0, cb)), pl.BlockSpec(sq, lambda cb, i: (cb, 0)),
                   pl.BlockSpec((1, SSM_WIDTH), lambda cb, i: (0, 0))],
        out_shape=[jax.ShapeDtypeStruct((SSM_WIDTH, 2 * N_STATES), F32), jax.ShapeDtypeStruct((2 * N_STATES, SSM_WIDTH), F32),
                   jax.ShapeDtypeStruct((1, SSM_WIDTH), F32)],
        scratch_shapes=[pltpu.VMEM(sq, F32), pltpu.VMEM(sq, F32)],
        compiler_params=_cp(("arbitrary", "arbitrary"), VMEM_BIG),
    )(proj, gs, xs, dyraw)


def sum_leading(x, name):
    def body(x_ref, o_ref):
        acc = x_ref[0]
        for k in range(1, x.shape[0]):
            acc = acc + x_ref[k]
        o_ref[...] = acc

    return _pcall(body, name=name, out_shape=jax.ShapeDtypeStruct(x.shape[1:], x.dtype))(x)


WEIGHTS = ['meta_tokens', 'ffn1_norm', 'ffn1_w1', 'ffn1_w3', 'ffn1_w2', 'mix_norm', 'w_in', 'attn_sinks', 'ssm_a_re',
           'ssm_a_im', 'ssm_log_step', 'ssm_b_re', 'ssm_b_im', 'ssm_c_re', 'ssm_c_im', 'ssm_d', 'ssm_glu_a', 'ssm_glu_b',
           'w_out', 'ffn2_norm', 'ffn2_w1', 'ffn2_w3', 'ffn2_w2', 'final_norm']
SHARDED = ['ffn1_w1', 'ffn1_w3', 'ffn1_w2', 'ffn2_w1', 'ffn2_w3', 'ffn2_w2', 'w_in', 'ssm_glu_a', 'ssm_glu_b', 'w_out']
REPLICATED = ['ffn1_norm', 'mix_norm', 'ffn2_norm', 'final_norm', 'attn_sinks', 'ssm_a_re', 'ssm_a_im', 'ssm_log_step',
              'ssm_b_re', 'ssm_b_im', 'ssm_c_re', 'ssm_c_im', 'ssm_d']
PACK_COLS = 1024


def _block_diag(blocks):
    g, r, c = blocks.shape
    eye = jnp.eye(g, dtype=blocks.dtype)
    return (blocks[:, :, None, :] * eye[:, None, :, None]).reshape(g * r, g * c)


def _diag_blocks(mat, r, c):
    g = SSM_GROUPS
    eye = jnp.eye(g, dtype=mat.dtype)
    return jnp.sum(mat.reshape(g, r, g, c) * eye[:, None, :, None], axis=2)


def _scan_order(re, im):
    r = re.shape[0]
    return jnp.stack([re.reshape(r, N_SCAN_BLK, SCAN_COLS), im.reshape(r, N_SCAN_BLK, SCAN_COLS)], axis=2).reshape(r, 2 * N_STATES)


def _from_scan_order(comb):
    r = comb.shape[0]
    c4 = comb.reshape(r, N_SCAN_BLK, 2, SCAN_COLS)
    return c4[:, :, 0].reshape(r, N_STATES), c4[:, :, 1].reshape(r, N_STATES)


def _pack(arrays):
    parts = []
    for a in arrays:
        flat = a.reshape(-1)
        chunk = SUBLANES * PACK_COLS
        padded = -(-flat.shape[0] // chunk) * chunk
        parts.append(jnp.pad(flat, (0, padded - flat.shape[0])).reshape(-1, PACK_COLS))
    return jnp.concatenate(parts, axis=0)


def _unpack(packed, shapes):
    out, row = [], 0
    for shape in shapes:
        size = 1
        for s in shape:
            size *= s
        chunk = SUBLANES * PACK_COLS
        rows = -(-size // chunk) * SUBLANES
        out.append(packed[row:row + rows].reshape(-1)[:size].reshape(shape))
        row += rows
    return out


def kernel(x, meta_tokens, ffn1_norm, ffn1_w1, ffn1_w3, ffn1_w2, mix_norm, w_in, attn_sinks, ssm_a_re, ssm_a_im, ssm_log_step, ssm_b_re, ssm_b_im, ssm_c_re, ssm_c_im, ssm_d, ssm_glu_a, ssm_glu_b, w_out, ffn2_norm, ffn2_w1, ffn2_w3, ffn2_w2, final_norm, loss_target, m_meta_tokens, m_ffn1_norm, m_ffn1_w1, m_ffn1_w3, m_ffn1_w2, m_mix_norm, m_w_in, m_attn_sinks, m_ssm_a_re, m_ssm_a_im, m_ssm_log_step, m_ssm_b_re, m_ssm_b_im, m_ssm_c_re, m_ssm_c_im, m_ssm_d, m_ssm_glu_a, m_ssm_glu_b, m_w_out, m_ffn2_norm, m_ffn2_w1, m_ffn2_w3, m_ffn2_w2, m_final_norm, v_meta_tokens, v_ffn1_norm, v_ffn1_w1, v_ffn1_w3, v_ffn1_w2, v_mix_norm, v_w_in, v_attn_sinks, v_ssm_a_re, v_ssm_a_im, v_ssm_log_step, v_ssm_b_re, v_ssm_b_im, v_ssm_c_re, v_ssm_c_im, v_ssm_d, v_ssm_glu_a, v_ssm_glu_b, v_w_out, v_ffn2_norm, v_ffn2_w1, v_ffn2_w3, v_ffn2_w2, v_final_norm):
    given = dict(locals())
    w = {n: given[n] for n in WEIGHTS}
    m = {n: given["m_" + n] for n in WEIGHTS}
    v = {n: given["v_" + n] for n in WEIGHTS}

    n_b, seq, _ = x.shape
    n_l = seq + N_META
    t_rows = n_b * n_l
    tm = _row_tile(n_l, 688)
    px, py, pc = _my_place()
    me = 4 * px + 2 * py + pc
    dest = jnp.stack([4 * qx + 2 * qy + pc for qx, qy in
                      [(px, py), (1 - px, py), (px, 1 - py), (1 - px, 1 - py)]]).astype(jnp.int32)

    w13 = jnp.stack([ffn1_w1[0], ffn1_w3[0], ffn2_w1[0], ffn2_w3[0]]).astype(BF16)
    w2 = jnp.stack([ffn1_w2[0], ffn2_w2[0]]).astype(BF16)
    glu = jnp.stack([ssm_glu_a[0], ssm_glu_b[0]]).astype(BF16)
    w13g, w2g, wing, glug, wog, metag = all_gather_list(
        [w13, w2, w_in[0].astype(BF16), glu, w_out[0].astype(BF16), meta_tokens], "ag_weights")
    glu_a = glug[:, 0].transpose(1, 0, 2).reshape(SSM_WIDTH, D_MODEL)
    glu_b = glug[:, 1].transpose(1, 0, 2).reshape(SSM_WIDTH, D_MODEL)
    w_out_full = wog.reshape(D_MODEL, D_MODEL)
    meta_full = metag.transpose(1, 0, 2).reshape(N_META, D_MODEL)

    h0 = jnp.concatenate([x, jnp.broadcast_to(meta_full[None], (n_b, N_META, D_MODEL))], axis=1).reshape(t_rows, D_MODEL)
    target = jnp.concatenate([loss_target, jnp.zeros((n_b, N_META, D_MODEL), F32)], axis=1).reshape(t_rows, D_MODEL)
    final_g = final_norm.reshape(1, D_MODEL)

    ar = ssm_a_re.reshape(1, N_STATES)
    ai = ssm_a_im.reshape(1, N_STATES)
    ls = jnp.repeat(ssm_log_step.reshape(SSM_GROUPS), SSM_STATE).reshape(1, N_STATES)
    br_t = ssm_b_re[0].transpose(2, 0, 1).reshape(SSM_GROUP, N_STATES)
    bi_t = ssm_b_im[0].transpose(2, 0, 1).reshape(SSM_GROUP, N_STATES)
    bbr, bbi, tabf, tabr = ssm_prepare(ar, ai, ls, br_t, bi_t, "ssm_prepare")
    bbr_g = bbr.reshape(SSM_GROUP, SSM_GROUPS, SSM_STATE).transpose(1, 0, 2)
    bbi_g = bbi.reshape(SSM_GROUP, SSM_GROUPS, SSM_STATE).transpose(1, 0, 2)
    b_comb = _scan_order(_block_diag(bbr_g), _block_diag(bbi_g)).astype(BF16)
    c_comb_t = _scan_order(_block_diag(ssm_c_re[0]), -_block_diag(ssm_c_im[0])).astype(BF16)
    b_comb_t, c_comb = b_comb.T, c_comb_t.T

    h1, hn1, a1, b1 = ffn_forward(h0, ffn1_norm, w13g, w2g, 0, tm, "ffn1_fwd")
    hnm, proj = mix_forward(h1, mix_norm, wing, tm, "mix_fwd")
    proj3 = proj.reshape(n_b, n_l, IN_WIDTH)
    attn3 = attention_forward(proj3, attn_sinks, seq, "attn_fwd")
    attn = attn3.reshape(t_rows, D_MODEL)
    xs3, yraw3 = ssm_forward_scan(proj3, b_comb, tabf, c_comb, ssm_d, seq, "ssm_fwd")
    yraw = yraw3.reshape(t_rows, SSM_WIDTH)
    h2 = merge_forward(h1, yraw, attn, proj, glu_a, glu_b, w_out_full, tm, "merge_fwd")
    h3, hn2, a2, b2 = ffn_forward(h2, ffn2_norm, w13g, w2g, 1, tm, "ffn2_fwd")
    dh3, loss_part, g_final = final_loss_backward(h3, target, final_g, seq, tm, "loss_bwd")
    loss = lax.psum(loss_part[0, 0], ("x", "y", "c"))

    dh2, da2, db2, g_ffn2_norm = ffn_backward_act(dh3, h2, ffn2_norm, a2, b2, w13g, w2g, 1, tm, "ffn2_bwd_act")
    dw = {}
    dw['ffn2_w1'], dw['ffn2_w3'], dw['ffn2_w2'] = ffn_backward_weights(hn2, dh3, a2, b2, da2, db2, tm, "ffn2_bwd_w")
    dattn, dyraw, dproj, d_wo, d_ga, d_gb = merge_backward(dh2, yraw, attn, proj, glu_a, glu_b, w_out_full,
                                                           _row_tile(t_rows, 256), "merge_bwd")
    dproj3 = dproj.reshape(n_b, n_l, IN_WIDTH)
    dproj3, dsink_p = attention_backward(proj3, dattn.reshape(n_b, n_l, D_MODEL), dproj3, attn_sinks, seq, "attn_bwd")
    dproj3, gs3, dlr_p, dli_p = ssm_backward_scan(
        dyraw.reshape(n_b, n_l, SSM_WIDTH), xs3, dproj3, c_comb_t, tabr, b_comb_t, ssm_d, seq, "ssm_bwd")
    dproj = dproj3.reshape(t_rows, IN_WIDTH)
    d_bd, d_cd, g_d = ssm_param_grads(proj, gs3.reshape(t_rows, 2 * N_STATES), xs3.reshape(t_rows, 2 * N_STATES),
                                      dyraw, tm, "ssm_bwd_w")
    dh1, g_mix_norm = mix_backward_act(dh2, h1, mix_norm, dproj, wing, tm, "mix_bwd_act")
    dw['w_in'] = mix_backward_weights(hnm, dproj, tm, "mix_bwd_w")
    dh0, da1, db1, g_ffn1_norm = ffn_backward_act(dh1, h0, ffn1_norm, a1, b1, w13g, w2g, 0, tm, "ffn1_bwd_act")
    dw['ffn1_w1'], dw['ffn1_w3'], dw['ffn1_w2'] = ffn_backward_weights(hn1, dh1, a1, b1, da1, db1, tm, "ffn1_bwd_w")
    dh0_3 = dh0.reshape(n_b, n_l, D_MODEL)
    grad_x = dh0_3[:, :seq]
    g_meta = sum_leading(dh0_3[:, seq:], "meta_sum")

    def blocked_cols(full):
        r = full.shape[0]
        return full.reshape(r, N_DEV, full.shape[1] // N_DEV).transpose(1, 0, 2).astype(BF16)

    dw['ssm_glu_a'] = blocked_cols(d_ga)
    dw['ssm_glu_b'] = blocked_cols(d_gb)
    dw['w_out'] = d_wo.reshape(N_DEV, D_MODEL // N_DEV, D_MODEL).astype(BF16)

    d_b_re, d_b_im = _from_scan_order(d_bd)
    d_c_re, d_c_im = _from_scan_order(d_cd.T)
    dbbr = _diag_blocks(d_b_re, SSM_GROUP, SSM_STATE).transpose(1, 0, 2).reshape(SSM_GROUP, N_STATES)
    dbbi = _diag_blocks(d_b_im, SSM_GROUP, SSM_STATE).transpose(1, 0, 2).reshape(SSM_GROUP, N_STATES)
    g_c_re = _diag_blocks(d_c_re, SSM_GROUP, SSM_STATE)[None]
    g_c_im = -_diag_blocks(d_c_im, SSM_GROUP, SSM_STATE)[None]
    group_sum = (jnp.arange(N_STATES)[:, None] // SSM_STATE == jnp.arange(LANES)[None, :]).astype(F32)
    g_ar, g_ai, g_ls, g_br, g_bi = ssm_param_backward(
        ar, ai, ls, br_t, bi_t, dlr_p.reshape(n_b * SUBLANES, N_STATES), dli_p.reshape(n_b * SUBLANES, N_STATES),
        dbbr, dbbi, group_sum, "ssm_bwd_params")
    g_sinks = sum_leading(dsink_p, "sink_sum")[0:1, :N_KV_HEADS * Q_PER_KV]

    small = {
        'ffn1_norm': g_ffn1_norm, 'mix_norm': g_mix_norm, 'ffn2_norm': g_ffn2_norm, 'final_norm': g_final.reshape(D_MODEL),
        'attn_sinks': g_sinks, 'ssm_a_re': g_ar.reshape(1, SSM_GROUPS, SSM_STATE), 'ssm_a_im': g_ai.reshape(1, SSM_GROUPS, SSM_STATE),
        'ssm_log_step': g_ls[:, :SSM_GROUPS],
        'ssm_b_re': g_br.reshape(SSM_GROUP, SSM_GROUPS, SSM_STATE).transpose(1, 2, 0)[None],
        'ssm_b_im': g_bi.reshape(SSM_GROUP, SSM_GROUPS, SSM_STATE).transpose(1, 2, 0)[None],
        'ssm_c_re': g_c_re, 'ssm_c_im': g_c_im, 'ssm_d': g_d,
    }

    zeros_meta = jnp.zeros((N_META, D_MODEL), F32)
    packed_g = _pack([small[n] for n in REPLICATED] + [g_meta])
    (parts,) = all_gather_list([packed_g], "ag_small_grads")
    packed_out = adamw_small(parts, _pack([w[n] for n in REPLICATED] + [zeros_meta]),
                             _pack([m[n] for n in REPLICATED] + [zeros_meta]),
                             _pack([v[n] for n in REPLICATED] + [zeros_meta]), "adamw_small")
    shapes = [w[n].shape for n in REPLICATED] + [(N_META, D_MODEL)]
    grads, deltas, new_m, new_v = {}, {}, {}, {}
    unpacked = [_unpack(p, shapes) for p in packed_out]
    for k, n in enumerate(REPLICATED):
        grads[n], deltas[n], new_m[n], new_v[n] = (u[k] for u in unpacked)
    g_meta_full = unpacked[0][-1]
    grads['meta_tokens'] = lax.dynamic_index_in_dim(
        g_meta_full.reshape(N_META, N_DEV, D_MODEL // N_DEV), me, axis=1, keepdims=False)
    deltas['meta_tokens'], new_m['meta_tokens'], new_v['meta_tokens'] = adamw_plain(
        grads['meta_tokens'], w['meta_tokens'], m['meta_tokens'], v['meta_tokens'], "adamw_meta")

    g_list = [dw[n] for n in SHARDED]
    r1 = rs_sibling_swap(g_list, "rs_sibling")
    pairs = [pair_sums(dest, g, r, "rs_pair_" + n) for n, g, r in zip(SHARDED, g_list, r1)]
    r2 = rs_chip_exchange(pairs, "rs_chips")
    for n, g, ra, rb in zip(SHARDED, g_list, r1, r2):
        shape = w[n].shape
        two_d = lambda t: t.reshape(shape[1:])
        out = adamw_sharded(dest, g, ra, rb, two_d(w[n]), two_d(m[n]), two_d(v[n]), "adamw_" + n)
        grads[n], deltas[n], new_m[n], new_v[n] = (o.reshape(shape) for o in out)

    return (loss, grad_x, *[grads[n] for n in WEIGHTS], *[deltas[n] for n in WEIGHTS],
            *[new_m[n] for n in WEIGHTS], *[new_v[n] for n in WEIGHTS])
```

```python
import functools

import jax
import jax.numpy as jnp
from jax import lax
from jax.experimental import pallas as pl
from jax.experimental.pallas import tpu as pltpu

F32 = jnp.float32
BF16 = jnp.bfloat16
MESH = pl.DeviceIdType.MESH

N_DEV = 8
D_MODEL = 1024
N_META = 16
HEAD_DIM = 64
N_KV_HEADS = 4
Q_PER_KV = 4
BLOCK = 128
KV_WIDTH = N_KV_HEADS * HEAD_DIM
SSM_GROUP = 16
SSM_WIDTH = 512
SSM_GROUPS = 32
SSM_STATE = 64
N_STATES = SSM_GROUPS * SSM_STATE
D_FF = 2816
FF_BLK = D_FF // N_DEV
IN_WIDTH = 4096
IN_BLK = IN_WIDTH // N_DEV
NORM_EPS = 1e-6
NEG_INF = -1e30
SCAN_COLS = 256
N_SCAN_BLK = N_STATES // SCAN_COLS
SUBLANES = 8
LANES = 128
MXU_WIDTH = 256
FF_FWD_COLS = D_FF // 2
FF_BWD_COLS = MXU_WIDTH

ADAM_LR = 0.001
ADAM_B1 = 0.9
ADAM_B2 = 0.999
ADAM_EPS = 1e-08
ADAM_WD = 0.01
ADAM_STEP = 10

VMEM_BIG = 56 * 1024 * 1024


def _cp(sem=None, vmem=None):
    kw = {}
    if sem is not None:
        kw["dimension_semantics"] = sem
    if vmem is not None:
        kw["vmem_limit_bytes"] = vmem
    return pltpu.CompilerParams(**kw)


def _pcall(body, **kw):
    return pl.pallas_call(body, **kw)


def _dot(a, b):
    return jnp.dot(a, b, preferred_element_type=F32)


def _dot_nt(a, b):
    return lax.dot_general(a, b, (((1,), (1,)), ((), ())), preferred_element_type=F32)


def _dot_tn(a, b):
    return lax.dot_general(a, b, (((0,), (0,)), ((), ())), preferred_element_type=F32)


def _sigmoid(x):
    return 1.0 / (1.0 + jnp.exp(-x))


def _row_tile(rows, cap):
    best = None
    for t in range(16, min(rows, cap) + 1, 16):
        if rows % t == 0:
            best = t
    assert best is not None, rows
    return best


def _my_place():
    return lax.axis_index("x"), lax.axis_index("y"), lax.axis_index("c")


def all_gather_list(shards, name):
    n = len(shards)

    def body(*refs):
        ins, outs = refs[:n], refs[n:2 * n]
        send_sems, recv_sems, local_sems = refs[2 * n:]
        x, y, c = _my_place()
        me, sibling = (x, y, c), (x, y, 1 - c)
        chips = [(1 - x, y), (x, 1 - y), (1 - x, 1 - y)]

        def blk(a, px, py, pc):
            return outs[a].at[4 * px + 2 * py + pc]

        def copy(a, k, block, to, src=None):
            return pltpu.make_async_remote_copy(
                src_ref=blk(a, *block) if src is None else src, dst_ref=blk(a, *block),
                send_sem=send_sems.at[a * 7 + k], recv_sem=recv_sems.at[a * 7 + k],
                device_id=to, device_id_type=MESH)

        mine = [pltpu.make_async_copy(ins[a], blk(a, *me), local_sems.at[a]) for a in range(n)]
        for cp in mine:
            cp.start()
        first = []
        for a in range(n):
            first.append(copy(a, 0, me, sibling, src=ins[a]))
            first += [copy(a, 1 + j, me, (*chip, c), src=ins[a]) for j, chip in enumerate(chips)]
        for cp in first:
            cp.start()
        passed = []
        for j, chip in enumerate(chips):
            for a in range(n):
                copy(a, 1 + j, (*chip, c), me).wait_recv()
                cp = copy(a, 4 + j, (*chip, c), sibling)
                cp.start()
                passed.append(cp)
        for a in range(n):
            copy(a, 0, sibling, me).wait_recv()
            for j, chip in enumerate(chips):
                copy(a, 4 + j, (*chip, 1 - c), me).wait_recv()
        for cp in first + passed:
            cp.wait_send()
        for cp in mine:
            cp.wait()

    any_spec = pl.BlockSpec(memory_space=pl.ANY)
    return _pcall(
        body, name=name,
        out_shape=[jax.ShapeDtypeStruct((N_DEV,) + s.shape, s.dtype) for s in shards],
        in_specs=[any_spec] * n, out_specs=[any_spec] * n,
        scratch_shapes=[pltpu.SemaphoreType.DMA((7 * n,)), pltpu.SemaphoreType.DMA((7 * n,)),
                        pltpu.SemaphoreType.DMA((n,))],
    )(*shards)


def rs_sibling_swap(grads, name):
    n = len(grads)

    def body(*refs):
        ins, outs = refs[:n], refs[n:2 * n]
        send_sems, recv_sems = refs[2 * n:]
        x, y, c = _my_place()
        chips = [(x, y), (1 - x, y), (x, 1 - y), (1 - x, 1 - y)]
        copies = []
        for a in range(n):
            for k, (px, py) in enumerate(chips):
                copies.append(pltpu.make_async_remote_copy(
                    src_ref=ins[a].at[4 * px + 2 * py + (1 - c)], dst_ref=outs[a].at[k],
                    send_sem=send_sems.at[4 * a + k], recv_sem=recv_sems.at[4 * a + k],
                    device_id=(x, y, 1 - c), device_id_type=MESH))
        for cp in copies:
            cp.start()
        for cp in copies:
            cp.wait()

    any_spec = pl.BlockSpec(memory_space=pl.ANY)
    return _pcall(
        body, name=name,
        out_shape=[jax.ShapeDtypeStruct((4,) + g.shape[1:], g.dtype) for g in grads],
        in_specs=[any_spec] * n, out_specs=[any_spec] * n,
        scratch_shapes=[pltpu.SemaphoreType.DMA((4 * n,)), pltpu.SemaphoreType.DMA((4 * n,))],
    )(*grads)


def rs_chip_exchange(parts, name):
    n = len(parts)

    def body(*refs):
        ins, outs = refs[:n], refs[n:2 * n]
        send_sems, recv_sems = refs[2 * n:]
        x, y, c = _my_place()
        chips = [(1 - x, y), (x, 1 - y), (1 - x, 1 - y)]
        copies = []
        for a in range(n):
            for k, (px, py) in enumerate(chips):
                copies.append(pltpu.make_async_remote_copy(
                    src_ref=ins[a].at[k], dst_ref=outs[a].at[k],
                    send_sem=send_sems.at[3 * a + k], recv_sem=recv_sems.at[3 * a + k],
                    device_id=(px, py, c), device_id_type=MESH))
        for cp in copies:
            cp.start()
        for cp in copies:
            cp.wait()

    any_spec = pl.BlockSpec(memory_space=pl.ANY)
    return _pcall(
        body, name=name,
        out_shape=[jax.ShapeDtypeStruct(p.shape, p.dtype) for p in parts],
        in_specs=[any_spec] * n, out_specs=[any_spec] * n,
        scratch_shapes=[pltpu.SemaphoreType.DMA((3 * n,)), pltpu.SemaphoreType.DMA((3 * n,))],
    )(*parts)


def pair_sums(idx, g, r1, name):
    _, rows, cols = g.shape
    tr = _row_tile(rows, 256)

    def body(idx_ref, g_ref, r_ref, o_ref):
        o_ref[...] = (g_ref[...].astype(F32) + r_ref[...].astype(F32)).astype(BF16)

    return _pcall(
        body, name=name,
        out_shape=jax.ShapeDtypeStruct((3, rows, cols), BF16),
        grid_spec=pltpu.PrefetchScalarGridSpec(
            num_scalar_prefetch=1, grid=(3, rows // tr),
            in_specs=[pl.BlockSpec((None, tr, cols), lambda k, r, ix: (ix[k + 1], r, 0)),
                      pl.BlockSpec((None, tr, cols), lambda k, r, ix: (k + 1, r, 0))],
            out_specs=pl.BlockSpec((None, tr, cols), lambda k, r, ix: (k, r, 0))),
        compiler_params=_cp(("arbitrary", "arbitrary")),
    )(idx, g, r1)


def _adam_math(w, g, m, v):
    m = ADAM_B1 * m + (1.0 - ADAM_B1) * g
    v = ADAM_B2 * v + (1.0 - ADAM_B2) * (g * g)
    m_hat = m / (1.0 - ADAM_B1 ** ADAM_STEP)
    v_hat = v / (1.0 - ADAM_B2 ** ADAM_STEP)
    delta = -ADAM_LR * (m_hat / (jnp.sqrt(v_hat) + ADAM_EPS) + ADAM_WD * w)
    return delta, m, v


def adamw_sharded(idx, g, r1, r2, w, m, v, name):
    rows, cols = w.shape
    tr = _row_tile(rows, 256)

    def body(idx_ref, g_ref, r1_ref, r2_ref, w_ref, m_ref, v_ref, go_ref, d_ref, mo_ref, vo_ref):
        grad = g_ref[...].astype(F32) + r1_ref[...].astype(F32)
        for k in range(3):
            grad = grad + r2_ref[k].astype(F32)
        delta, m_new, v_new = _adam_math(w_ref[...], grad, m_ref[...], v_ref[...])
        go_ref[...] = grad
        d_ref[...] = delta
        mo_ref[...] = m_new
        vo_ref[...] = v_new

    tile = pl.BlockSpec((tr, cols), lambda r, ix: (r, 0))
    out = jax.ShapeDtypeStruct((rows, cols), F32)
    return _pcall(
        body, name=name, out_shape=[out] * 4,
        grid_spec=pltpu.PrefetchScalarGridSpec(
            num_scalar_prefetch=1, grid=(rows // tr,),
            in_specs=[pl.BlockSpec((None, tr, cols), lambda r, ix: (ix[0], r, 0)),
                      pl.BlockSpec((None, tr, cols), lambda r, ix: (0, r, 0)),
                      pl.BlockSpec((3, tr, cols), lambda r, ix: (0, r, 0)),
                      tile, tile, tile],
            out_specs=[tile] * 4),
        compiler_params=_cp(("arbitrary",)),
    )(idx, g, r1, r2, w, m, v)


def adamw_small(parts, w, m, v, name):
    _, rows, cols = parts.shape

    def body(p_ref, w_ref, m_ref, v_ref, go_ref, d_ref, mo_ref, vo_ref):
        grad = p_ref[0]
        for k in range(1, N_DEV):
            grad = grad + p_ref[k]
        delta, m_new, v_new = _adam_math(w_ref[...], grad, m_ref[...], v_ref[...])
        go_ref[...] = grad
        d_ref[...] = delta
        mo_ref[...] = m_new
        vo_ref[...] = v_new

    out = jax.ShapeDtypeStruct((rows, cols), F32)
    return _pcall(body, name=name, out_shape=[out] * 4, compiler_params=_cp(vmem=VMEM_BIG))(parts, w, m, v)


def adamw_plain(g, w, m, v, name):
    def body(g_ref, w_ref, m_ref, v_ref, d_ref, mo_ref, vo_ref):
        delta, m_new, v_new = _adam_math(w_ref[...], g_ref[...], m_ref[...], v_ref[...])
        d_ref[...] = delta
        mo_ref[...] = m_new
        vo_ref[...] = v_new

    out = jax.ShapeDtypeStruct(w.shape, F32)
    return _pcall(body, name=name, out_shape=[out] * 3)(g, w, m, v)


def _rms_fwd(x, g):
    r = lax.rsqrt(jnp.mean(x * x, axis=-1, keepdims=True) + NORM_EPS)
    return x * r * g


def _rms_bwd(x, g, dy):
    r = lax.rsqrt(jnp.mean(x * x, axis=-1, keepdims=True) + NORM_EPS)
    xh = x * r
    t = dy * g
    dx = r * (t - xh * jnp.mean(t * xh, axis=-1, keepdims=True))
    return dx, jnp.sum(dy * xh, axis=0, keepdims=True)


def _accumulate(ref, val, first):
    @pl.when(first)
    def _():
        ref[...] = val

    @pl.when(jnp.logical_not(first))
    def _():
        ref[...] += val


def _col_chunks(width):
    return [(c0, min(MXU_WIDTH, width - c0)) for c0 in range(0, width, MXU_WIDTH)]


def ffn_forward(h, norm, w1, w3, w2, tm, tn, name):
    t_rows = h.shape[0]
    nj = D_FF // tn

    def body(h_ref, g_ref, w1_ref, w3_ref, w2_ref, out_ref, hn_ref, a_ref, b_ref, acc_ref):
        j = pl.program_id(1)

        @pl.when(j == 0)
        def _():
            hn_ref[...] = _rms_fwd(h_ref[...], g_ref[...]).astype(BF16)
            acc_ref[...] = jnp.zeros_like(acc_ref)

        hn = hn_ref[...]
        for c0, cw in _col_chunks(tn):
            a = _dot(hn, w1_ref[:, c0:c0 + cw])
            b = _dot(hn, w3_ref[:, c0:c0 + cw])
            a_ref[:, c0:c0 + cw] = a.astype(BF16)
            b_ref[:, c0:c0 + cw] = b.astype(BF16)
            hid = (a * _sigmoid(a) * b).astype(BF16)
            acc_ref[...] += _dot(hid, w2_ref[c0:c0 + cw, :])

        @pl.when(j == nj - 1)
        def _():
            out_ref[...] = h_ref[...] + 0.5 * acc_ref[...]

    row = pl.BlockSpec((tm, D_MODEL), lambda i, j: (i, 0))
    hid_blk = pl.BlockSpec((tm, tn), lambda i, j: (i, j))
    w_col = pl.BlockSpec((D_MODEL, tn), lambda i, j: (0, j))
    return _pcall(
        body, name=name, grid=(t_rows // tm, nj),
        in_specs=[row, pl.BlockSpec((1, D_MODEL), lambda i, j: (0, 0)), w_col, w_col,
                  pl.BlockSpec((tn, D_MODEL), lambda i, j: (j, 0))],
        out_specs=[row, row, hid_blk, hid_blk],
        out_shape=[jax.ShapeDtypeStruct((t_rows, D_MODEL), F32), jax.ShapeDtypeStruct((t_rows, D_MODEL), BF16),
                   jax.ShapeDtypeStruct((t_rows, D_FF), BF16), jax.ShapeDtypeStruct((t_rows, D_FF), BF16)],
        scratch_shapes=[pltpu.VMEM((tm, D_MODEL), F32)],
        compiler_params=_cp(("arbitrary", "arbitrary"), VMEM_BIG),
    )(h, norm, w1, w3, w2)


def ffn_backward_act(dh, h, norm, a, b, w1, w3, w2, tm, tn, name):
    t_rows = h.shape[0]
    nj = D_FF // tn

    def body(dh_ref, h_ref, g_ref, a_ref, b_ref, w1_ref, w3_ref, w2_ref,
             dhin_ref, da_ref, db_ref, dg_ref, dhb_ref, acc_ref):
        i, j = pl.program_id(0), pl.program_id(1)

        @pl.when(j == 0)
        def _():
            dhb_ref[...] = (0.5 * dh_ref[...]).astype(BF16)
            acc_ref[...] = jnp.zeros_like(acc_ref)

        dhb = dhb_ref[...]
        for c0, cw in _col_chunks(tn):
            dhid = _dot_nt(dhb, w2_ref[c0:c0 + cw, :])
            av = a_ref[:, c0:c0 + cw].astype(F32)
            bv = b_ref[:, c0:c0 + cw].astype(F32)
            s = _sigmoid(av)
            da = (dhid * bv * (s * (1.0 + av * (1.0 - s)))).astype(BF16)
            db = (dhid * (av * s)).astype(BF16)
            da_ref[:, c0:c0 + cw] = da
            db_ref[:, c0:c0 + cw] = db
            acc_ref[...] += _dot_nt(da, w1_ref[:, c0:c0 + cw]) + _dot_nt(db, w3_ref[:, c0:c0 + cw])

        @pl.when(j == nj - 1)
        def _():
            dx, dg = _rms_bwd(h_ref[...], g_ref[...], acc_ref[...])
            dhin_ref[...] = dh_ref[...] + dx
            _accumulate(dg_ref, dg, i == 0)

    row = pl.BlockSpec((tm, D_MODEL), lambda i, j: (i, 0))
    vec = pl.BlockSpec((1, D_MODEL), lambda i, j: (0, 0))
    hid_blk = pl.BlockSpec((tm, tn), lambda i, j: (i, j))
    w_col = pl.BlockSpec((D_MODEL, tn), lambda i, j: (0, j))
    return _pcall(
        body, name=name, grid=(t_rows // tm, nj),
        in_specs=[row, row, vec, hid_blk, hid_blk, w_col, w_col, pl.BlockSpec((tn, D_MODEL), lambda i, j: (j, 0))],
        out_specs=[row, hid_blk, hid_blk, vec],
        out_shape=[jax.ShapeDtypeStruct((t_rows, D_MODEL), F32),
                   jax.ShapeDtypeStruct((t_rows, D_FF), BF16), jax.ShapeDtypeStruct((t_rows, D_FF), BF16),
                   jax.ShapeDtypeStruct((1, D_MODEL), F32)],
        scratch_shapes=[pltpu.VMEM((tm, D_MODEL), BF16), pltpu.VMEM((tm, D_MODEL), F32)],
        compiler_params=_cp(("arbitrary", "arbitrary"), VMEM_BIG),
    )(dh, h, norm, a, b, w1, w3, w2)


def ffn_backward_weights(hn, dh, a, b, da, db, tm, tn, name):
    t_rows = hn.shape[0]
    ni = t_rows // tm

    def body(hn_ref, dh_ref, a_ref, b_ref, da_ref, db_ref, dw1_ref, dw3_ref, dw2_ref, acc1, acc3, acc2):
        i = pl.program_id(1)

        @pl.when(i == 0)
        def _():
            acc1[...] = jnp.zeros_like(acc1)
            acc3[...] = jnp.zeros_like(acc3)
            acc2[...] = jnp.zeros_like(acc2)

        hn_v = hn_ref[...]
        acc1[...] += _dot_tn(hn_v, da_ref[...])
        acc3[...] += _dot_tn(hn_v, db_ref[...])
        av = a_ref[...].astype(F32)
        hid = (av * _sigmoid(av) * b_ref[...].astype(F32)).astype(BF16)
        acc2[...] += _dot_tn(hid, (0.5 * dh_ref[...]).astype(BF16))

        @pl.when(i == ni - 1)
        def _():
            dw1_ref[...] = acc1[...].astype(BF16)
            dw3_ref[...] = acc3[...].astype(BF16)
            dw2_ref[...] = acc2[...].astype(BF16)

    row = pl.BlockSpec((tm, D_MODEL), lambda j, i: (i, 0))
    hid_blk = pl.BlockSpec((tm, tn), lambda j, i: (i, j))
    w_col = pl.BlockSpec((D_MODEL, tn), lambda j, i: (0, j))
    w_row = pl.BlockSpec((tn, D_MODEL), lambda j, i: (j, 0))
    return _pcall(
        body, name=name, grid=(D_FF // tn, ni),
        in_specs=[row, row, hid_blk, hid_blk, hid_blk, hid_blk],
        out_specs=[w_col, w_col, w_row],
        out_shape=[jax.ShapeDtypeStruct((D_MODEL, D_FF), BF16), jax.ShapeDtypeStruct((D_MODEL, D_FF), BF16),
                   jax.ShapeDtypeStruct((D_FF, D_MODEL), BF16)],
        scratch_shapes=[pltpu.VMEM((D_MODEL, tn), F32), pltpu.VMEM((D_MODEL, tn), F32), pltpu.VMEM((tn, D_MODEL), F32)],
        compiler_params=_cp(("arbitrary", "arbitrary"), VMEM_BIG),
    )(hn, dh, a, b, da, db)


def mix_forward(h, norm, wing, tm, name):
    t_rows = h.shape[0]

    def body(h_ref, g_ref, w_ref, hn_ref, p_ref):
        hn = _rms_fwd(h_ref[...], g_ref[...]).astype(BF16)
        hn_ref[...] = hn
        for j in range(N_DEV):
            p_ref[:, j * IN_BLK:(j + 1) * IN_BLK] = _dot(hn, w_ref[j]).astype(BF16)

    row = pl.BlockSpec((tm, D_MODEL), lambda i: (i, 0))
    return _pcall(
        body, name=name, grid=(t_rows // tm,),
        in_specs=[row, pl.BlockSpec((1, D_MODEL), lambda i: (0, 0)),
                  pl.BlockSpec((N_DEV, D_MODEL, IN_BLK), lambda i: (0, 0, 0))],
        out_specs=[row, pl.BlockSpec((tm, IN_WIDTH), lambda i: (i, 0))],
        out_shape=[jax.ShapeDtypeStruct((t_rows, D_MODEL), BF16), jax.ShapeDtypeStruct((t_rows, IN_WIDTH), BF16)],
        compiler_params=_cp(("arbitrary",), VMEM_BIG),
    )(h, norm, wing)


def mix_backward_act(dh, h, norm, dproj, wing, tm, name):
    t_rows = h.shape[0]
    per_step = 4
    nj = N_DEV // per_step

    def body(dh_ref, h_ref, g_ref, dp_ref, w_ref, dhin_ref, dg_ref, acc_ref):
        i, j = pl.program_id(0), pl.program_id(1)
        part = functools.reduce(
            lambda u, w: u + w, [_dot_nt(dp_ref[:, k * IN_BLK:(k + 1) * IN_BLK], w_ref[k]) for k in range(per_step)])
        _accumulate(acc_ref, part, j == 0)

        @pl.when(j == nj - 1)
        def _():
            dx, dg = _rms_bwd(h_ref[...], g_ref[...], acc_ref[...])
            dhin_ref[...] = dh_ref[...] + dx
            _accumulate(dg_ref, dg, i == 0)

    row = pl.BlockSpec((tm, D_MODEL), lambda i, j: (i, 0))
    vec = pl.BlockSpec((1, D_MODEL), lambda i, j: (0, 0))
    return _pcall(
        body, name=name, grid=(t_rows // tm, nj),
        in_specs=[row, row, vec, pl.BlockSpec((tm, per_step * IN_BLK), lambda i, j: (i, j)),
                  pl.BlockSpec((per_step, D_MODEL, IN_BLK), lambda i, j: (j, 0, 0))],
        out_specs=[row, vec],
        out_shape=[jax.ShapeDtypeStruct((t_rows, D_MODEL), F32), jax.ShapeDtypeStruct((1, D_MODEL), F32)],
        scratch_shapes=[pltpu.VMEM((tm, D_MODEL), F32)],
        compiler_params=_cp(("arbitrary", "arbitrary"), VMEM_BIG),
    )(dh, h, norm, dproj, wing)


def mix_backward_weights(hn, dproj, tm, name):
    t_rows = hn.shape[0]
    ni = t_rows // tm
    per_step = 2

    def body(hn_ref, dp_ref, dw_ref, acc):
        i = pl.program_id(1)
        _accumulate(acc, _dot_tn(hn_ref[...], dp_ref[...]), i == 0)

        @pl.when(i == ni - 1)
        def _():
            for k in range(per_step):
                dw_ref[k] = acc[:, k * IN_BLK:(k + 1) * IN_BLK].astype(BF16)

    return _pcall(
        body, name=name, grid=(N_DEV // per_step, ni),
        in_specs=[pl.BlockSpec((tm, D_MODEL), lambda j, i: (i, 0)),
                  pl.BlockSpec((tm, per_step * IN_BLK), lambda j, i: (i, j))],
        out_specs=pl.BlockSpec((per_step, D_MODEL, IN_BLK), lambda j, i: (j, 0, 0)),
        out_shape=jax.ShapeDtypeStruct((N_DEV, D_MODEL, IN_BLK), BF16),
        scratch_shapes=[pltpu.VMEM((D_MODEL, per_step * IN_BLK), F32)],
        compiler_params=_cp(("arbitrary", "arbitrary"), VMEM_BIG),
    )(hn, dproj)


GELU_C = 0.7978845608028654
GELU_K = 0.044715


def _gelu(x):
    return 0.5 * x * (1.0 + jnp.tanh(GELU_C * (x + GELU_K * (x * x * x))))


def _gelu_and_grad(x):
    th = jnp.tanh(GELU_C * (x + GELU_K * (x * x * x)))
    val = 0.5 * x * (1.0 + th)
    grad = 0.5 * (1.0 + th) + 0.5 * x * (1.0 - th * th) * (GELU_C * (1.0 + 3.0 * GELU_K * (x * x)))
    return val, grad


def merge_forward(h, yraw, attn, proj, glu_a, glu_b, w_out, tm, name):
    t_rows = h.shape[0]

    def body(h_ref, y_ref, at_ref, gate_ref, a_ref, b_ref, wo_ref, out_ref):
        y = _gelu(y_ref[...]).astype(BF16)
        ssm = _dot(y, a_ref[...]) * _sigmoid(_dot(y, b_ref[...]))
        ga = gate_ref[:, :D_MODEL].astype(F32)
        gs = gate_ref[:, D_MODEL:].astype(F32)
        merged = _sigmoid(ga) * at_ref[...].astype(F32) + _sigmoid(gs) * ssm
        out_ref[...] = h_ref[...] + _dot(merged.astype(BF16), wo_ref[...])

    row = pl.BlockSpec((tm, D_MODEL), lambda i: (i, 0))
    glu = pl.BlockSpec((SSM_WIDTH, D_MODEL), lambda i: (0, 0))
    return _pcall(
        body, name=name, grid=(t_rows // tm,),
        in_specs=[row, pl.BlockSpec((tm, SSM_WIDTH), lambda i: (i, 0)), row,
                  pl.BlockSpec((tm, 2 * D_MODEL), lambda i: (i, 1)), glu, glu,
                  pl.BlockSpec((D_MODEL, D_MODEL), lambda i: (0, 0))],
        out_specs=row, out_shape=jax.ShapeDtypeStruct((t_rows, D_MODEL), F32),
        compiler_params=_cp(("arbitrary",), VMEM_BIG),
    )(h, yraw, attn, proj, glu_a, glu_b, w_out)


def merge_backward(dh, yraw, attn, proj, glu_a, glu_b, w_out, tm, name):
    t_rows = dh.shape[0]

    def body(dh_ref, y_ref, at_ref, gate_ref, a_ref, b_ref, wo_ref,
             dat_ref, dy_ref, dgate_ref, dwo_ref, da_ref, db_ref):
        first = pl.program_id(0) == 0
        d16 = dh_ref[...].astype(BF16)
        dmerged = _dot_nt(d16, wo_ref[...])
        gel, dgel = _gelu_and_grad(y_ref[...].astype(F32))
        y16 = gel.astype(BF16)
        ya = _dot(y16, a_ref[...])
        sb = _sigmoid(_dot(y16, b_ref[...]))
        ssm = ya * sb
        sa = _sigmoid(gate_ref[:, :D_MODEL].astype(F32))
        ss = _sigmoid(gate_ref[:, D_MODEL:].astype(F32))
        attn_v = at_ref[...].astype(F32)
        merged = (sa * attn_v + ss * ssm).astype(BF16)
        _accumulate(dwo_ref, _dot_tn(merged, d16), first)
        dat_ref[...] = (dmerged * sa).astype(BF16)
        dgate_ref[:, :D_MODEL] = (dmerged * attn_v * sa * (1.0 - sa)).astype(BF16)
        dgate_ref[:, D_MODEL:] = (dmerged * ssm * ss * (1.0 - ss)).astype(BF16)
        dssm = dmerged * ss
        dya = (dssm * sb).astype(BF16)
        dyb = (dssm * ya * sb * (1.0 - sb)).astype(BF16)
        _accumulate(da_ref, _dot_tn(y16, dya), first)
        _accumulate(db_ref, _dot_tn(y16, dyb), first)
        dy = _dot_nt(dya, a_ref[...]) + _dot_nt(dyb, b_ref[...])
        dy_ref[...] = (dy * dgel).astype(BF16)

    row = pl.BlockSpec((tm, D_MODEL), lambda i: (i, 0))
    ssm_row = pl.BlockSpec((tm, SSM_WIDTH), lambda i: (i, 0))
    gates = pl.BlockSpec((tm, 2 * D_MODEL), lambda i: (i, 1))
    glu = pl.BlockSpec((SSM_WIDTH, D_MODEL), lambda i: (0, 0))
    wo = pl.BlockSpec((D_MODEL, D_MODEL), lambda i: (0, 0))
    return _pcall(
        body, name=name, grid=(t_rows // tm,),
        in_specs=[row, ssm_row, row, gates, glu, glu, wo],
        out_specs=[row, ssm_row, gates, wo, glu, glu],
        out_shape=[jax.ShapeDtypeStruct((t_rows, D_MODEL), BF16), jax.ShapeDtypeStruct((t_rows, SSM_WIDTH), BF16),
                   jax.ShapeDtypeStruct((t_rows, IN_WIDTH), BF16), jax.ShapeDtypeStruct((D_MODEL, D_MODEL), F32),
                   jax.ShapeDtypeStruct((SSM_WIDTH, D_MODEL), F32), jax.ShapeDtypeStruct((SSM_WIDTH, D_MODEL), F32)],
        compiler_params=_cp(("arbitrary",), VMEM_BIG),
    )(dh, yraw, attn, proj, glu_a, glu_b, w_out)


def final_loss_backward(h, target, norm, seq, tm, name):
    t_rows = h.shape[0]
    tiles_per_example = (seq + N_META) // tm

    def body(h_ref, t_ref, g_ref, dh_ref, loss_ref, dg_ref):
        i = pl.program_id(0)
        x = h_ref[...]
        g = g_ref[...]
        r = lax.rsqrt(jnp.mean(x * x, axis=-1, keepdims=True) + NORM_EPS)
        xh = x * r
        pos = lax.broadcasted_iota(jnp.int32, (tm, 1), 0) + (i % tiles_per_example) * tm
        diff = jnp.where(pos < seq, xh * g - t_ref[...], 0.0)
        part = 0.5 * jnp.sum(jnp.sum(diff * diff, axis=-1, keepdims=True), axis=0, keepdims=True) / D_MODEL
        dy = diff / D_MODEL
        t = dy * g
        dh_ref[...] = r * (t - xh * jnp.mean(t * xh, axis=-1, keepdims=True))
        _accumulate(loss_ref, jnp.broadcast_to(part, (1, LANES)), i == 0)
        _accumulate(dg_ref, jnp.sum(dy * xh, axis=0, keepdims=True), i == 0)

    row = pl.BlockSpec((tm, D_MODEL), lambda i: (i, 0))
    vec = pl.BlockSpec((1, D_MODEL), lambda i: (0, 0))
    return _pcall(
        body, name=name, grid=(t_rows // tm,),
        in_specs=[row, row, vec],
        out_specs=[row, pl.BlockSpec((1, LANES), lambda i: (0, 0)), vec],
        out_shape=[jax.ShapeDtypeStruct((t_rows, D_MODEL), F32), jax.ShapeDtypeStruct((1, LANES), F32),
                   jax.ShapeDtypeStruct((1, D_MODEL), F32)],
        compiler_params=_cp(("arbitrary",), VMEM_BIG),
    )(h, target, norm)


ATTN_SCALE = HEAD_DIM ** -0.5
STACK_HEADS = (0, 2, 1, 3)


def _lane_half(shape, hf):
    lane = lax.broadcasted_iota(jnp.int32, shape, 1)
    return (lane < HEAD_DIM) if hf == 0 else (lane >= HEAD_DIM)


def _kv_variants(ref, rows, kh):
    tile = kh // 2
    t = ref[rows, tile * LANES:(tile + 1) * LANES].astype(F32)
    swapped = pltpu.roll(t, HEAD_DIM, axis=1)
    at_low, at_high = (t, swapped) if kh % 2 == 0 else (swapped, t)
    lo = jnp.where(_lane_half(t.shape, 0), at_low, 0.0).astype(BF16)
    hi = jnp.where(_lane_half(t.shape, 1), at_high, 0.0).astype(BF16)
    return lo, hi


def _to_kv_lanes(lo, hi, kh):
    lo = jnp.where(_lane_half(lo.shape, 0), lo, 0.0)
    hi = jnp.where(_lane_half(hi.shape, 1), hi, 0.0)
    if kh % 2 == 0:
        return lo + pltpu.roll(hi, HEAD_DIM, axis=1)
    return pltpu.roll(lo, HEAD_DIM, axis=1) + hi


def _stacked(ref, rows, kh):
    col = kh * 2 * LANES
    return jnp.concatenate([ref[rows, col:col + LANES], ref[rows, col + LANES:col + 2 * LANES]], axis=0)


def _sink_column(sink_ref, kh, nq):
    row = lax.broadcasted_iota(jnp.int32, (4 * nq, 1), 0)
    col = jnp.zeros((4 * nq, 1), F32)
    for quarter, g in enumerate(STACK_HEADS):
        col = jnp.where(row // nq == quarter, sink_ref[0, kh * Q_PER_KV + g], col)
    return col


def _softmax_parts(qs, key_tiles, masks, sink):
    scores = []
    for (k_lo, k_hi), mask in zip(key_tiles, masks):
        s = jnp.concatenate([_dot_nt(qs, k_lo), _dot_nt(qs, k_hi)], axis=0) * ATTN_SCALE
        scores.append(s if mask is None else jnp.where(mask, s, NEG_INF))
    m = functools.reduce(jnp.maximum, [jnp.max(s, axis=-1, keepdims=True) for s in scores])
    m = jnp.maximum(m, sink)
    probs = [jnp.exp(s - m) for s in scores]
    e_sink = jnp.exp(sink - m)
    den = functools.reduce(lambda u, w: u + w, [jnp.sum(p, axis=-1, keepdims=True) for p in probs]) + e_sink
    return probs, 1.0 / den, e_sink


def _band_mask(nq, first):
    keys = BLOCK if first else 2 * BLOCK
    qi = lax.broadcasted_iota(jnp.int32, (4 * nq, keys), 0) % nq
    kj = lax.broadcasted_iota(jnp.int32, (4 * nq, keys), 1)
    if first:
        return kj <= qi
    return jnp.logical_and(kj > qi, kj <= qi + BLOCK)


def _meta_mask():
    qi = lax.broadcasted_iota(jnp.int32, (4 * N_META, N_META), 0) % N_META
    kj = lax.broadcasted_iota(jnp.int32, (4 * N_META, N_META), 1)
    return kj <= qi


def _attention_schedule(seq, queries, carry):
    meta_rows = pl.ds(seq, N_META)
    carry = queries(pl.ds(0, BLOCK), BLOCK, [pl.ds(0, BLOCK), meta_rows], [_band_mask(BLOCK, True), None], carry)

    def block(n, c):
        r0 = pl.multiple_of(n * BLOCK, BLOCK)
        p0 = pl.multiple_of((n - 1) * BLOCK, BLOCK)
        return queries(pl.ds(r0, BLOCK), BLOCK, [pl.ds(p0, 2 * BLOCK), meta_rows], [_band_mask(BLOCK, False), None], c)

    carry = lax.fori_loop(1, seq // BLOCK, block, carry)
    return queries(meta_rows, N_META, [meta_rows], [_meta_mask()], carry)


def attention_forward(proj3, sinks, seq, name):
    n_b, n_l, _ = proj3.shape

    def body(sink_ref, q_ref, k_ref, v_ref, o_ref):
        def queries(q_rows, nq, key_rows, masks, carry):
            for kh in range(N_KV_HEADS):
                ks = [_kv_variants(k_ref, r, kh) for r in key_rows]
                vs = [_kv_variants(v_ref, r, kh) for r in key_rows]
                qs = _stacked(q_ref, q_rows, kh)
                probs, inv, _ = _softmax_parts(qs, ks, masks, _sink_column(sink_ref, kh, nq))
                probs = [p.astype(BF16) for p in probs]
                o_lo = functools.reduce(lambda u, w: u + w, [_dot(p[:2 * nq], v_lo) for p, (v_lo, _) in zip(probs, vs)])
                o_hi = functools.reduce(lambda u, w: u + w, [_dot(p[2 * nq:], v_hi) for p, (_, v_hi) in zip(probs, vs)])
                out = (o_lo * inv[:2 * nq] + o_hi * inv[2 * nq:]).astype(BF16)
                col = kh * 2 * LANES
                o_ref[q_rows, col:col + LANES] = out[:nq]
                o_ref[q_rows, col + LANES:col + 2 * LANES] = out[nq:]
            return carry

        _attention_schedule(seq, queries, 0)

    return _pcall(
        body, name=name, grid=(n_b,),
        in_specs=[pl.BlockSpec(memory_space=pltpu.SMEM),
                  pl.BlockSpec((None, n_l, D_MODEL), lambda b: (b, 0, 0)),
                  pl.BlockSpec((None, n_l, KV_WIDTH), lambda b: (b, 0, D_MODEL // KV_WIDTH)),
                  pl.BlockSpec((None, n_l, KV_WIDTH), lambda b: (b, 0, D_MODEL // KV_WIDTH + 1))],
        out_specs=pl.BlockSpec((None, n_l, D_MODEL), lambda b: (b, 0, 0)),
        out_shape=jax.ShapeDtypeStruct((n_b, n_l, D_MODEL), BF16),
        compiler_params=_cp(("arbitrary",), VMEM_BIG),
    )(sinks, proj3, proj3, proj3)


def attention_backward(proj3, dattn3, dproj3, sinks, seq, name):
    n_b, n_l, _ = proj3.shape
    qkv_width = D_MODEL + 2 * KV_WIDTH

    def body(sink_ref, q_ref, k_ref, v_ref, do_ref, _, dqkv_ref, dsink_ref, dk_ref, dv_ref):
        dk_ref[...] = jnp.zeros_like(dk_ref)
        dv_ref[...] = jnp.zeros_like(dv_ref)
        sub = lax.broadcasted_iota(jnp.int32, (SUBLANES, LANES), 0)
        lane = lax.broadcasted_iota(jnp.int32, (SUBLANES, LANES), 1)

        def queries(q_rows, nq, key_rows, masks, dsink):
            for kh in range(N_KV_HEADS):
                ks = [_kv_variants(k_ref, r, kh) for r in key_rows]
                vs = [_kv_variants(v_ref, r, kh) for r in key_rows]
                qs = _stacked(q_ref, q_rows, kh)
                dos = _stacked(do_ref, q_rows, kh)
                probs, inv, e_sink = _softmax_parts(qs, ks, masks, _sink_column(sink_ref, kh, nq))
                probs = [p * inv for p in probs]
                dps = [jnp.concatenate([_dot_nt(dos, v_lo), _dot_nt(dos, v_hi)], axis=0) for v_lo, v_hi in vs]
                delta = functools.reduce(
                    lambda u, w: u + w, [jnp.sum(p * dp, axis=-1, keepdims=True) for p, dp in zip(probs, dps)])
                d_sink = -(e_sink * inv) * delta
                for quarter, g in enumerate(STACK_HEADS):
                    d_here = jnp.sum(d_sink[quarter * nq:(quarter + 1) * nq], axis=0, keepdims=True)
                    dsink = dsink + jnp.where(jnp.logical_and(sub == 0, lane == kh * Q_PER_KV + g), d_here, 0.0)
                dq = None
                tile = slice((kh // 2) * LANES, (kh // 2 + 1) * LANES)
                for r, p, dp, (k_lo, k_hi) in zip(key_rows, probs, dps, ks):
                    ds = (p * (dp - delta)).astype(BF16)
                    p16 = p.astype(BF16)
                    dq_x = _dot(ds[:2 * nq], k_lo) + _dot(ds[2 * nq:], k_hi)
                    dq = dq_x if dq is None else dq + dq_x
                    dk_ref[r, tile] += _to_kv_lanes(_dot_tn(ds[:2 * nq], qs), _dot_tn(ds[2 * nq:], qs), kh) * ATTN_SCALE
                    dv_ref[r, tile] += _to_kv_lanes(_dot_tn(p16[:2 * nq], dos), _dot_tn(p16[2 * nq:], dos), kh)
                dq = (dq * ATTN_SCALE).astype(BF16)
                col = kh * 2 * LANES
                dqkv_ref[q_rows, col:col + LANES] = dq[:nq]
                dqkv_ref[q_rows, col + LANES:col + 2 * LANES] = dq[nq:]
            return dsink

        dsink_ref[...] = _attention_schedule(seq, queries, jnp.zeros((SUBLANES, LANES), F32))
        dqkv_ref[:, D_MODEL:D_MODEL + KV_WIDTH] = dk_ref[...].astype(BF16)
        dqkv_ref[:, D_MODEL + KV_WIDTH:] = dv_ref[...].astype(BF16)

    return _pcall(
        body, name=name, grid=(n_b,),
        in_specs=[pl.BlockSpec(memory_space=pltpu.SMEM),
                  pl.BlockSpec((None, n_l, D_MODEL), lambda b: (b, 0, 0)),
                  pl.BlockSpec((None, n_l, KV_WIDTH), lambda b: (b, 0, D_MODEL // KV_WIDTH)),
                  pl.BlockSpec((None, n_l, KV_WIDTH), lambda b: (b, 0, D_MODEL // KV_WIDTH + 1)),
                  pl.BlockSpec((None, n_l, D_MODEL), lambda b: (b, 0, 0)),
                  pl.BlockSpec(memory_space=pl.ANY)],
        out_specs=[pl.BlockSpec((None, n_l, qkv_width), lambda b: (b, 0, 0)),
                   pl.BlockSpec((None, SUBLANES, LANES), lambda b: (b, 0, 0))],
        out_shape=[jax.ShapeDtypeStruct(dproj3.shape, BF16), jax.ShapeDtypeStruct((n_b, SUBLANES, LANES), F32)],
        scratch_shapes=[pltpu.VMEM((n_l, KV_WIDTH), F32), pltpu.VMEM((n_l, KV_WIDTH), F32)],
        input_output_aliases={5: 0},
        compiler_params=_cp(("arbitrary",), VMEM_BIG),
    )(sinks, proj3, proj3, proj3, dattn3, dproj3)


TAB_ROWS = 8


def _cmul(ar, ai, br, bi):
    return ar * br - ai * bi, ar * bi + ai * br


def _discretise(ar, ai, ls):
    step = jnp.exp(ls)
    mag = jnp.exp(ar * step)
    ang = ai * step
    cos, sin = jnp.cos(ang), jnp.sin(ang)
    lr, li = mag * cos, mag * sin
    den = ar * ar + ai * ai
    nr, ni = lr - 1.0, li
    cr = (nr * ar + ni * ai) / den
    ci = (ni * ar - nr * ai) / den
    return step, mag, lr, li, den, nr, ni, cr, ci


def _scan_tables(lr, li, reverse):
    n = lr.shape[-1]
    pw = [(lr, li)]
    for _ in range(SUBLANES - 1):
        pw.append(_cmul(pw[-1][0], pw[-1][1], lr, li))
    row = lax.broadcasted_iota(jnp.int32, (SUBLANES, n), 0)
    out = []
    for d in (1, 2, 4):
        ok = (row + d <= SUBLANES - 1) if reverse else (row >= d)
        out += [jnp.where(ok, pw[d - 1][0], 0.0), jnp.where(ok, pw[d - 1][1], 0.0)]
    cr = jnp.zeros((SUBLANES, n), F32)
    ci = jnp.zeros((SUBLANES, n), F32)
    for r in range(SUBLANES):
        e = (SUBLANES - r) if reverse else (r + 1)
        cr = jnp.where(row == r, pw[e - 1][0], cr)
        ci = jnp.where(row == r, pw[e - 1][1], ci)
    return out + [cr, ci]


def ssm_prepare(ar, ai, ls, br_t, bi_t, name):
    def body(ar_ref, ai_ref, ls_ref, br_ref, bi_ref, bbr_ref, bbi_ref, tf_ref, tr_ref):
        _, _, lr, li, _, _, _, cr, ci = _discretise(ar_ref[...], ai_ref[...], ls_ref[...])
        br, bi = br_ref[...], bi_ref[...]
        bbr_ref[...] = cr * br - ci * bi
        bbi_ref[...] = cr * bi + ci * br
        for k, t in enumerate(_scan_tables(lr, li, False)):
            tf_ref[k] = t
        for k, t in enumerate(_scan_tables(lr, -li, True)):
            tr_ref[k] = t

    return _pcall(
        body, name=name,
        out_shape=[jax.ShapeDtypeStruct((SSM_GROUP, N_STATES), F32), jax.ShapeDtypeStruct((SSM_GROUP, N_STATES), F32),
                   jax.ShapeDtypeStruct((TAB_ROWS, SUBLANES, N_STATES), F32),
                   jax.ShapeDtypeStruct((TAB_ROWS, SUBLANES, N_STATES), F32)],
    )(ar, ai, ls, br_t, bi_t)


def ssm_param_backward(ar, ai, ls, br_t, bi_t, dlr_p, dli_p, dbbr, dbbi, group_sum, name):
    def body(ar_ref, ai_ref, ls_ref, br_ref, bi_ref, dlr_ref, dli_ref, dbbr_ref, dbbi_ref, gs_ref,
             dar_ref, dai_ref, dls_ref, dbr_ref, dbi_ref):
        ar, ai = ar_ref[...], ai_ref[...]
        step, mag, lr, li, den, nr, ni, cr, ci = _discretise(ar, ai, ls_ref[...])
        br, bi, dbbr_v, dbbi_v = br_ref[...], bi_ref[...], dbbr_ref[...], dbbi_ref[...]
        dbr_ref[...] = cr * dbbr_v + ci * dbbi_v
        dbi_ref[...] = cr * dbbi_v - ci * dbbr_v
        dcr = jnp.sum(dbbr_v * br + dbbi_v * bi, axis=0, keepdims=True)
        dci = jnp.sum(dbbi_v * br - dbbr_v * bi, axis=0, keepdims=True)
        dnr = (dcr * ar - dci * ai) / den
        dni = (dcr * ai + dci * ar) / den
        dden = -(cr * dcr + ci * dci) / den
        dar = (dcr * nr + dci * ni) / den + dden * 2.0 * ar
        dai = (dcr * ni - dci * nr) / den + dden * 2.0 * ai
        dlr = jnp.sum(dlr_ref[...], axis=0, keepdims=True) + dnr
        dli = jnp.sum(dli_ref[...], axis=0, keepdims=True) + dni
        dmag = (dlr * lr + dli * li) / mag
        dang = dli * lr - dlr * li
        dar_ref[...] = dar + dmag * mag * step
        dai_ref[...] = dai + dang * step
        dstep = dmag * mag * ar + dang * ai
        dls_ref[...] = jnp.dot(dstep * step, gs_ref[...], preferred_element_type=F32, precision=lax.Precision.HIGHEST)

    vec = jax.ShapeDtypeStruct((1, N_STATES), F32)
    mat = jax.ShapeDtypeStruct((SSM_GROUP, N_STATES), F32)
    return _pcall(body, name=name, out_shape=[vec, vec, jax.ShapeDtypeStruct((1, LANES), F32), mat, mat])(
        ar, ai, ls, br_t, bi_t, dlr_p, dli_p, dbbr, dbbi, group_sum)


def _scan_rows(a, b, tabs, carry, reverse):
    for k, d in enumerate((1, 2, 4)):
        shift = SUBLANES - d if reverse else d
        sr, si = pltpu.roll(a, shift, axis=0), pltpu.roll(b, shift, axis=0)
        pr, pi = _cmul(tabs[2 * k], tabs[2 * k + 1], sr, si)
        a, b = a + pr, b + pi
    pr, pi = _cmul(tabs[6], tabs[7], carry[0], carry[1])
    return a + pr, b + pi


def _time_groups(seq, reverse):
    meta = [seq + SUBLANES * g for g in range(N_META // SUBLANES)]
    return meta[::-1] if reverse else meta


def ssm_forward_scan(proj3, b_comb, tabf, c_comb, dvec, seq, name):
    n_b, n_l, _ = proj3.shape
    u_blk = (D_MODEL + 2 * KV_WIDTH) // SSM_WIDTH

    def body(u_ref, b_ref, tab_ref, c_ref, d_ref, x_ref, y_ref, bu, xs):
        j = pl.program_id(1)
        u = u_ref[...]
        bu[...] = _dot(u, b_ref[...])
        tabs = [tab_ref[k] for k in range(TAB_ROWS)]

        def group(r0, carry):
            rows = pl.ds(r0, SUBLANES)
            a, b = _scan_rows(bu[rows, :SCAN_COLS], bu[rows, SCAN_COLS:], tabs, carry, False)
            xs[rows, :SCAN_COLS] = a
            xs[rows, SCAN_COLS:] = b
            return (jnp.broadcast_to(a[SUBLANES - 1:, :], a.shape), jnp.broadcast_to(b[SUBLANES - 1:, :], b.shape))

        zero = jnp.zeros((SUBLANES, SCAN_COLS), F32)
        carry = (zero, zero)
        for r0 in _time_groups(seq, False):
            carry = group(r0, carry)
        lax.fori_loop(0, seq // SUBLANES, lambda g, c: group(pl.multiple_of(g * SUBLANES, SUBLANES), c), carry)
        x16 = xs[...].astype(BF16)
        x_ref[...] = x16
        contrib = _dot(x16, c_ref[...])

        @pl.when(j == 0)
        def _():
            y_ref[...] = contrib + d_ref[...] * u.astype(F32)

        @pl.when(j > 0)
        def _():
            y_ref[...] += contrib

    return _pcall(
        body, name=name, grid=(n_b, N_SCAN_BLK),
        in_specs=[pl.BlockSpec((None, n_l, SSM_WIDTH), lambda b, j: (b, 0, u_blk)),
                  pl.BlockSpec((SSM_WIDTH, 2 * SCAN_COLS), lambda b, j: (0, j)),
                  pl.BlockSpec((TAB_ROWS, SUBLANES, SCAN_COLS), lambda b, j: (0, 0, j)),
                  pl.BlockSpec((2 * SCAN_COLS, SSM_WIDTH), lambda b, j: (j, 0)),
                  pl.BlockSpec((1, SSM_WIDTH), lambda b, j: (0, 0))],
        out_specs=[pl.BlockSpec((None, n_l, 2 * SCAN_COLS), lambda b, j: (b, 0, j)),
                   pl.BlockSpec((None, n_l, SSM_WIDTH), lambda b, j: (b, 0, 0))],
        out_shape=[jax.ShapeDtypeStruct((n_b, n_l, 2 * N_STATES), BF16),
                   jax.ShapeDtypeStruct((n_b, n_l, SSM_WIDTH), F32)],
        scratch_shapes=[pltpu.VMEM((n_l, 2 * SCAN_COLS), F32)] * 2,
        compiler_params=_cp(("arbitrary", "arbitrary"), VMEM_BIG),
    )(proj3, b_comb, tabf, c_comb, dvec)


def ssm_backward_scan(dyraw3, xs3, dproj3, c_comb_t, tabr, b_comb_t, dvec, seq, name):
    n_b, n_l, _ = xs3.shape
    u_blk = (D_MODEL + 2 * KV_WIDTH) // SSM_WIDTH

    def body(dy_ref, x_ref, _, c_ref, tab_ref, b_ref, d_ref, du_ref, g_ref, dlr_ref, dli_ref, dx, gs, xs, du_acc):
        j = pl.program_id(1)
        dy = dy_ref[...]
        dx[...] = _dot(dy, c_ref[...])
        xs[...] = x_ref[...].astype(F32)
        tabs = [tab_ref[k] for k in range(TAB_ROWS)]
        last_row = lax.broadcasted_iota(jnp.int32, (SUBLANES, SCAN_COLS), 0) == SUBLANES - 1

        def group(r0, state):
            cr, ci, acc_r, acc_i = state
            rows = pl.ds(r0, SUBLANES)
            a, b = _scan_rows(dx[rows, :SCAN_COLS], dx[rows, SCAN_COLS:], tabs, (cr, ci), True)
            gs[rows, :SCAN_COLS] = a
            gs[rows, SCAN_COLS:] = b
            na = jnp.where(last_row, cr, pltpu.roll(a, SUBLANES - 1, axis=0))
            nb = jnp.where(last_row, ci, pltpu.roll(b, SUBLANES - 1, axis=0))
            xa, xb = xs[rows, :SCAN_COLS], xs[rows, SCAN_COLS:]
            return (jnp.broadcast_to(a[:1, :], a.shape), jnp.broadcast_to(b[:1, :], b.shape),
                    acc_r + na * xa + nb * xb, acc_i + nb * xa - na * xb)

        zero = jnp.zeros((SUBLANES, SCAN_COLS), F32)
        n_groups = seq // SUBLANES
        state = lax.fori_loop(
            0, n_groups, lambda g, s: group(pl.multiple_of((n_groups - 1 - g) * SUBLANES, SUBLANES), s),
            (zero, zero, zero, zero))
        for r0 in _time_groups(seq, True):
            state = group(r0, state)
        dlr_ref[...] = state[2]
        dli_ref[...] = state[3]
        g16 = gs[...].astype(BF16)
        g_ref[...] = g16
        contrib = _dot(g16, b_ref[...])

        @pl.when(j == 0)
        def _():
            du_acc[...] = contrib + d_ref[...] * dy.astype(F32)

        @pl.when(j > 0)
        def _():
            du_acc[...] += contrib

        @pl.when(j == N_SCAN_BLK - 1)
        def _():
            du_ref[...] = du_acc[...].astype(BF16)

    state_blk = pl.BlockSpec((None, n_l, 2 * SCAN_COLS), lambda b, j: (b, 0, j))
    dl_blk = pl.BlockSpec((None, SUBLANES, SCAN_COLS), lambda b, j: (b, 0, j))
    return _pcall(
        body, name=name, grid=(n_b, N_SCAN_BLK),
        in_specs=[pl.BlockSpec((None, n_l, SSM_WIDTH), lambda b, j: (b, 0, 0)), state_blk,
                  pl.BlockSpec(memory_space=pl.ANY),
                  pl.BlockSpec((SSM_WIDTH, 2 * SCAN_COLS), lambda b, j: (0, j)),
                  pl.BlockSpec((TAB_ROWS, SUBLANES, SCAN_COLS), lambda b, j: (0, 0, j)),
                  pl.BlockSpec((2 * SCAN_COLS, SSM_WIDTH), lambda b, j: (j, 0)),
                  pl.BlockSpec((1, SSM_WIDTH), lambda b, j: (0, 0))],
        out_specs=[pl.BlockSpec((None, n_l, SSM_WIDTH), lambda b, j: (b, 0, u_blk)), state_blk, dl_blk, dl_blk],
        out_shape=[jax.ShapeDtypeStruct(dproj3.shape, BF16), jax.ShapeDtypeStruct((n_b, n_l, 2 * N_STATES), BF16),
                   jax.ShapeDtypeStruct((n_b, SUBLANES, N_STATES), F32), jax.ShapeDtypeStruct((n_b, SUBLANES, N_STATES), F32)],
        scratch_shapes=[pltpu.VMEM((n_l, 2 * SCAN_COLS), F32)] * 3 + [pltpu.VMEM((n_l, SSM_WIDTH), F32)],
        input_output_aliases={2: 0},
        compiler_params=_cp(("arbitrary", "arbitrary"), VMEM_BIG),
    )(dyraw3, xs3, dproj3, c_comb_t, tabr, b_comb_t, dvec)


def ssm_param_grads(proj, gs, xs, dyraw, tm, name):
    t_rows = proj.shape[0]
    ni = t_rows // tm
    n_cb = 2 * N_STATES // SSM_WIDTH
    u_blk = (D_MODEL + 2 * KV_WIDTH) // SSM_WIDTH

    def body(u_ref, g_ref, x_ref, dy_ref, db_ref, dc_ref, dd_ref, acc_b, acc_c):
        cb, i = pl.program_id(0), pl.program_id(1)

        @pl.when(i == 0)
        def _():
            acc_b[...] = jnp.zeros_like(acc_b)
            acc_c[...] = jnp.zeros_like(acc_c)

        u, dy = u_ref[...], dy_ref[...]
        acc_b[...] += _dot_tn(u, g_ref[...])
        acc_c[...] += _dot_tn(x_ref[...], dy)

        @pl.when(i == ni - 1)
        def _():
            db_ref[...] = acc_b[...]
            dc_ref[...] = acc_c[...]

        @pl.when(cb == 0)
        def _():
            _accumulate(dd_ref, jnp.sum(dy.astype(F32) * u.astype(F32), axis=0, keepdims=True), i == 0)

    sq = (SSM_WIDTH, SSM_WIDTH)
    return _pcall(
        body, name=name, grid=(n_cb, ni),
        in_specs=[pl.BlockSpec((tm, SSM_WIDTH), lambda cb, i: (i, u_blk)),
                  pl.BlockSpec((tm, SSM_WIDTH), lambda cb, i: (i, cb)),
                  pl.BlockSpec((tm, SSM_WIDTH), lambda cb, i: (i, cb)),
                  pl.BlockSpec((tm, SSM_WIDTH), lambda cb, i: (i, 0))],
        out_specs=[pl.BlockSpec(sq, lambda cb, i: (0, cb)), pl.BlockSpec(sq, lambda cb, i: (cb, 0)),
                   pl.BlockSpec((1, SSM_WIDTH), lambda cb, i: (0, 0))],
        out_shape=[jax.ShapeDtypeStruct((SSM_WIDTH, 2 * N_STATES), F32), jax.ShapeDtypeStruct((2 * N_STATES, SSM_WIDTH), F32),
                   jax.ShapeDtypeStruct((1, SSM_WIDTH), F32)],
        scratch_shapes=[pltpu.VMEM(sq, F32), pltpu.VMEM(sq, F32)],
        compiler_params=_cp(("arbitrary", "arbitrary"), VMEM_BIG),
    )(proj, gs, xs, dyraw)


def sum_leading(x, name):
    def body(x_ref, o_ref):
        acc = x_ref[0]
        for k in range(1, x.shape[0]):
            acc = acc + x_ref[k]
        o_ref[...] = acc

    return _pcall(body, name=name, out_shape=jax.ShapeDtypeStruct(x.shape[1:], x.dtype))(x)


WEIGHTS = ['meta_tokens', 'ffn1_norm', 'ffn1_w1', 'ffn1_w3', 'ffn1_w2', 'mix_norm', 'w_in', 'attn_sinks', 'ssm_a_re',
           'ssm_a_im', 'ssm_log_step', 'ssm_b_re', 'ssm_b_im', 'ssm_c_re', 'ssm_c_im', 'ssm_d', 'ssm_glu_a', 'ssm_glu_b',
           'w_out', 'ffn2_norm', 'ffn2_w1', 'ffn2_w3', 'ffn2_w2', 'final_norm']
SHARDED = ['ffn1_w1', 'ffn1_w3', 'ffn1_w2', 'ffn2_w1', 'ffn2_w3', 'ffn2_w2', 'w_in', 'ssm_glu_a', 'ssm_glu_b', 'w_out']
REPLICATED = ['ffn1_norm', 'mix_norm', 'ffn2_norm', 'final_norm', 'attn_sinks', 'ssm_a_re', 'ssm_a_im', 'ssm_log_step',
              'ssm_b_re', 'ssm_b_im', 'ssm_c_re', 'ssm_c_im', 'ssm_d']
PACK_COLS = 1024


def _block_diag(blocks):
    g, r, c = blocks.shape
    eye = jnp.eye(g, dtype=blocks.dtype)
    return (blocks[:, :, None, :] * eye[:, None, :, None]).reshape(g * r, g * c)


def _diag_blocks(mat, r, c):
    g = SSM_GROUPS
    eye = jnp.eye(g, dtype=mat.dtype)
    return jnp.sum(mat.reshape(g, r, g, c) * eye[:, None, :, None], axis=2)


def _scan_order(re, im):
    r = re.shape[0]
    return jnp.stack([re.reshape(r, N_SCAN_BLK, SCAN_COLS), im.reshape(r, N_SCAN_BLK, SCAN_COLS)], axis=2).reshape(r, 2 * N_STATES)


def _from_scan_order(comb):
    r = comb.shape[0]
    c4 = comb.reshape(r, N_SCAN_BLK, 2, SCAN_COLS)
    return c4[:, :, 0].reshape(r, N_STATES), c4[:, :, 1].reshape(r, N_STATES)


def _pack(arrays):
    parts = []
    for a in arrays:
        flat = a.reshape(-1)
        chunk = SUBLANES * PACK_COLS
        padded = -(-flat.shape[0] // chunk) * chunk
        parts.append(jnp.pad(flat, (0, padded - flat.shape[0])).reshape(-1, PACK_COLS))
    return jnp.concatenate(parts, axis=0)


def _unpack(packed, shapes):
    out, row = [], 0
    for shape in shapes:
        size = 1
        for s in shape:
            size *= s
        chunk = SUBLANES * PACK_COLS
        rows = -(-size // chunk) * SUBLANES
        out.append(packed[row:row + rows].reshape(-1)[:size].reshape(shape))
        row += rows
    return out


def kernel(x, meta_tokens, ffn1_norm, ffn1_w1, ffn1_w3, ffn1_w2, mix_norm, w_in, attn_sinks, ssm_a_re, ssm_a_im, ssm_log_step, ssm_b_re, ssm_b_im, ssm_c_re, ssm_c_im, ssm_d, ssm_glu_a, ssm_glu_b, w_out, ffn2_norm, ffn2_w1, ffn2_w3, ffn2_w2, final_norm, loss_target, m_meta_tokens, m_ffn1_norm, m_ffn1_w1, m_ffn1_w3, m_ffn1_w2, m_mix_norm, m_w_in, m_attn_sinks, m_ssm_a_re, m_ssm_a_im, m_ssm_log_step, m_ssm_b_re, m_ssm_b_im, m_ssm_c_re, m_ssm_c_im, m_ssm_d, m_ssm_glu_a, m_ssm_glu_b, m_w_out, m_ffn2_norm, m_ffn2_w1, m_ffn2_w3, m_ffn2_w2, m_final_norm, v_meta_tokens, v_ffn1_norm, v_ffn1_w1, v_ffn1_w3, v_ffn1_w2, v_mix_norm, v_w_in, v_attn_sinks, v_ssm_a_re, v_ssm_a_im, v_ssm_log_step, v_ssm_b_re, v_ssm_b_im, v_ssm_c_re, v_ssm_c_im, v_ssm_d, v_ssm_glu_a, v_ssm_glu_b, v_w_out, v_ffn2_norm, v_ffn2_w1, v_ffn2_w3, v_ffn2_w2, v_final_norm):
    given = dict(locals())
    w = {n: given[n] for n in WEIGHTS}
    m = {n: given["m_" + n] for n in WEIGHTS}
    v = {n: given["v_" + n] for n in WEIGHTS}

    n_b, seq, _ = x.shape
    n_l = seq + N_META
    t_rows = n_b * n_l
    tm = _row_tile(n_l, 688)
    px, py, pc = _my_place()
    me = 4 * px + 2 * py + pc
    dest = jnp.stack([4 * qx + 2 * qy + pc for qx, qy in
                      [(px, py), (1 - px, py), (px, 1 - py), (1 - px, 1 - py)]]).astype(jnp.int32)

    glu = jnp.stack([ssm_glu_a[0], ssm_glu_b[0]]).astype(BF16)
    ffn_names = ['ffn1_w1', 'ffn1_w3', 'ffn1_w2', 'ffn2_w1', 'ffn2_w3', 'ffn2_w2']
    gathered = all_gather_list(
        [w[n][0].astype(BF16) for n in ffn_names] + [w_in[0].astype(BF16), glu, w_out[0].astype(BF16), meta_tokens],
        "ag_weights")
    wing, glug, wog, metag = gathered[len(ffn_names):]
    full = {}
    for n, g in zip(ffn_names, gathered):
        if n.endswith('w2'):
            full[n] = g.reshape(D_FF, D_MODEL)
        else:
            full[n] = g.transpose(1, 0, 2).reshape(D_MODEL, D_FF)
    glu_a = glug[:, 0].transpose(1, 0, 2).reshape(SSM_WIDTH, D_MODEL)
    glu_b = glug[:, 1].transpose(1, 0, 2).reshape(SSM_WIDTH, D_MODEL)
    w_out_full = wog.reshape(D_MODEL, D_MODEL)
    meta_full = metag.transpose(1, 0, 2).reshape(N_META, D_MODEL)

    h0 = jnp.concatenate([x, jnp.broadcast_to(meta_full[None], (n_b, N_META, D_MODEL))], axis=1).reshape(t_rows, D_MODEL)
    target = jnp.concatenate([loss_target, jnp.zeros((n_b, N_META, D_MODEL), F32)], axis=1).reshape(t_rows, D_MODEL)
    final_g = final_norm.reshape(1, D_MODEL)

    ar = ssm_a_re.reshape(1, N_STATES)
    ai = ssm_a_im.reshape(1, N_STATES)
    ls = jnp.repeat(ssm_log_step.reshape(SSM_GROUPS), SSM_STATE).reshape(1, N_STATES)
    br_t = ssm_b_re[0].transpose(2, 0, 1).reshape(SSM_GROUP, N_STATES)
    bi_t = ssm_b_im[0].transpose(2, 0, 1).reshape(SSM_GROUP, N_STATES)
    bbr, bbi, tabf, tabr = ssm_prepare(ar, ai, ls, br_t, bi_t, "ssm_prepare")
    bbr_g = bbr.reshape(SSM_GROUP, SSM_GROUPS, SSM_STATE).transpose(1, 0, 2)
    bbi_g = bbi.reshape(SSM_GROUP, SSM_GROUPS, SSM_STATE).transpose(1, 0, 2)
    b_comb = _scan_order(_block_diag(bbr_g), _block_diag(bbi_g)).astype(BF16)
    c_comb_t = _scan_order(_block_diag(ssm_c_re[0]), -_block_diag(ssm_c_im[0])).astype(BF16)
    b_comb_t, c_comb = b_comb.T, c_comb_t.T

    ffn1_w = (full['ffn1_w1'], full['ffn1_w3'], full['ffn1_w2'])
    ffn2_w = (full['ffn2_w1'], full['ffn2_w3'], full['ffn2_w2'])
    h1, hn1, a1, b1 = ffn_forward(h0, ffn1_norm, *ffn1_w, tm, FF_FWD_COLS, "ffn1_fwd")
    hnm, proj = mix_forward(h1, mix_norm, wing, tm, "mix_fwd")
    proj3 = proj.reshape(n_b, n_l, IN_WIDTH)
    attn3 = attention_forward(proj3, attn_sinks, seq, "attn_fwd")
    attn = attn3.reshape(t_rows, D_MODEL)
    xs3, yraw3 = ssm_forward_scan(proj3, b_comb, tabf, c_comb, ssm_d, seq, "ssm_fwd")
    yraw = yraw3.reshape(t_rows, SSM_WIDTH)
    h2 = merge_forward(h1, yraw, attn, proj, glu_a, glu_b, w_out_full, tm, "merge_fwd")
    h3, hn2, a2, b2 = ffn_forward(h2, ffn2_norm, *ffn2_w, tm, FF_FWD_COLS, "ffn2_fwd")
    dh3, loss_part, g_final = final_loss_backward(h3, target, final_g, seq, tm, "loss_bwd")
    loss = lax.psum(loss_part[0, 0], ("x", "y", "c"))

    def blocked_ffn(d_w1, d_w3, d_w2):
        by_cols = lambda t: t.reshape(D_MODEL, N_DEV, FF_BLK).transpose(1, 0, 2)
        return by_cols(d_w1), by_cols(d_w3), d_w2.reshape(N_DEV, FF_BLK, D_MODEL)

    dh2, da2, db2, g_ffn2_norm = ffn_backward_act(dh3, h2, ffn2_norm, a2, b2, *ffn2_w, tm, FF_BWD_COLS, "ffn2_bwd_act")
    dw = {}
    dw['ffn2_w1'], dw['ffn2_w3'], dw['ffn2_w2'] = blocked_ffn(
        *ffn_backward_weights(hn2, dh3, a2, b2, da2, db2, tm, FF_BWD_COLS, "ffn2_bwd_w"))
    dattn, dyraw, dproj, d_wo, d_ga, d_gb = merge_backward(dh2, yraw, attn, proj, glu_a, glu_b, w_out_full,
                                                           _row_tile(t_rows, 256), "merge_bwd")
    dproj3 = dproj.reshape(n_b, n_l, IN_WIDTH)
    dproj3, dsink_p = attention_backward(proj3, dattn.reshape(n_b, n_l, D_MODEL), dproj3, attn_sinks, seq, "attn_bwd")
    dproj3, gs3, dlr_p, dli_p = ssm_backward_scan(
        dyraw.reshape(n_b, n_l, SSM_WIDTH), xs3, dproj3, c_comb_t, tabr, b_comb_t, ssm_d, seq, "ssm_bwd")
    dproj = dproj3.reshape(t_rows, IN_WIDTH)
    d_bd, d_cd, g_d = ssm_param_grads(proj, gs3.reshape(t_rows, 2 * N_STATES), xs3.reshape(t_rows, 2 * N_STATES),
                                      dyraw, tm, "ssm_bwd_w")
    dh1, g_mix_norm = mix_backward_act(dh2, h1, mix_norm, dproj, wing, tm, "mix_bwd_act")
    dw['w_in'] = mix_backward_weights(hnm, dproj, tm, "mix_bwd_w")
    dh0, da1, db1, g_ffn1_norm = ffn_backward_act(dh1, h0, ffn1_norm, a1, b1, *ffn1_w, tm, FF_BWD_COLS, "ffn1_bwd_act")
    dw['ffn1_w1'], dw['ffn1_w3'], dw['ffn1_w2'] = blocked_ffn(
        *ffn_backward_weights(hn1, dh1, a1, b1, da1, db1, tm, FF_BWD_COLS, "ffn1_bwd_w"))
    dh0_3 = dh0.reshape(n_b, n_l, D_MODEL)
    grad_x = dh0_3[:, :seq]
    g_meta = sum_leading(dh0_3[:, seq:], "meta_sum")

    def blocked_cols(full):
        r = full.shape[0]
        return full.reshape(r, N_DEV, full.shape[1] // N_DEV).transpose(1, 0, 2).astype(BF16)

    dw['ssm_glu_a'] = blocked_cols(d_ga)
    dw['ssm_glu_b'] = blocked_cols(d_gb)
    dw['w_out'] = d_wo.reshape(N_DEV, D_MODEL // N_DEV, D_MODEL).astype(BF16)

    d_b_re, d_b_im = _from_scan_order(d_bd)
    d_c_re, d_c_im = _from_scan_order(d_cd.T)
    dbbr = _diag_blocks(d_b_re, SSM_GROUP, SSM_STATE).transpose(1, 0, 2).reshape(SSM_GROUP, N_STATES)
    dbbi = _diag_blocks(d_b_im, SSM_GROUP, SSM_STATE).transpose(1, 0, 2).reshape(SSM_GROUP, N_STATES)
    g_c_re = _diag_blocks(d_c_re, SSM_GROUP, SSM_STATE)[None]
    g_c_im = -_diag_blocks(d_c_im, SSM_GROUP, SSM_STATE)[None]
    group_sum = (jnp.arange(N_STATES)[:, None] // SSM_STATE == jnp.arange(LANES)[None, :]).astype(F32)
    g_ar, g_ai, g_ls, g_br, g_bi = ssm_param_backward(
        ar, ai, ls, br_t, bi_t, dlr_p.reshape(n_b * SUBLANES, N_STATES), dli_p.reshape(n_b * SUBLANES, N_STATES),
        dbbr, dbbi, group_sum, "ssm_bwd_params")
    g_sinks = sum_leading(dsink_p, "sink_sum")[0:1, :N_KV_HEADS * Q_PER_KV]

    small = {
        'ffn1_norm': g_ffn1_norm, 'mix_norm': g_mix_norm, 'ffn2_norm': g_ffn2_norm, 'final_norm': g_final.reshape(D_MODEL),
        'attn_sinks': g_sinks, 'ssm_a_re': g_ar.reshape(1, SSM_GROUPS, SSM_STATE), 'ssm_a_im': g_ai.reshape(1, SSM_GROUPS, SSM_STATE),
        'ssm_log_step': g_ls[:, :SSM_GROUPS],
        'ssm_b_re': g_br.reshape(SSM_GROUP, SSM_GROUPS, SSM_STATE).transpose(1, 2, 0)[None],
        'ssm_b_im': g_bi.reshape(SSM_GROUP, SSM_GROUPS, SSM_STATE).transpose(1, 2, 0)[None],
        'ssm_c_re': g_c_re, 'ssm_c_im': g_c_im, 'ssm_d': g_d,
    }

    zeros_meta = jnp.zeros((N_META, D_MODEL), F32)
    packed_g = _pack([small[n] for n in REPLICATED] + [g_meta])
    (parts,) = all_gather_list([packed_g], "ag_small_grads")
    packed_out = adamw_small(parts, _pack([w[n] for n in REPLICATED] + [zeros_meta]),
                             _pack([m[n] for n in REPLICATED] + [zeros_meta]),
                             _pack([v[n] for n in REPLICATED] + [zeros_meta]), "adamw_small")
    shapes = [w[n].shape for n in REPLICATED] + [(N_META, D_MODEL)]
    grads, deltas, new_m, new_v = {}, {}, {}, {}
    unpacked = [_unpack(p, shapes) for p in packed_out]
    for k, n in enumerate(REPLICATED):
        grads[n], deltas[n], new_m[n], new_v[n] = (u[k] for u in unpacked)
    g_meta_full = unpacked[0][-1]
    grads['meta_tokens'] = lax.dynamic_index_in_dim(
        g_meta_full.reshape(N_META, N_DEV, D_MODEL // N_DEV), me, axis=1, keepdims=False)
    deltas['meta_tokens'], new_m['meta_tokens'], new_v['meta_tokens'] = adamw_plain(
        grads['meta_tokens'], w['meta_tokens'], m['meta_tokens'], v['meta_tokens'], "adamw_meta")

    g_list = [dw[n] for n in SHARDED]
    r1 = rs_sibling_swap(g_list, "rs_sibling")
    pairs = [pair_sums(dest, g, r, "rs_pair_" + n) for n, g, r in zip(SHARDED, g_list, r1)]
    r2 = rs_chip_exchange(pairs, "rs_chips")
    for n, g, ra, rb in zip(SHARDED, g_list, r1, r2):
        shape = w[n].shape
        two_d = lambda t: t.reshape(shape[1:])
        out = adamw_sharded(dest, g, ra, rb, two_d(w[n]), two_d(m[n]), two_d(v[n]), "adamw_" + n)
        grads[n], deltas[n], new_m[n], new_v[n] = (o.reshape(shape) for o in out)

    return (loss, grad_x, *[grads[n] for n in WEIGHTS], *[deltas[n] for n in WEIGHTS],
            *[new_m[n] for n in WEIGHTS], *[new_v[n] for n in WEIGHTS])
```

```python
import functools

import jax
import jax.numpy as jnp
from jax import lax
from jax.experimental import pallas as pl
from jax.experimental.pallas import tpu as pltpu

F32 = jnp.float32
BF16 = jnp.bfloat16
MESH = pl.DeviceIdType.MESH

N_DEV = 8
D_MODEL = 1024
N_META = 16
HEAD_DIM = 64
N_KV_HEADS = 4
Q_PER_KV = 4
BLOCK = 128
KV_WIDTH = N_KV_HEADS * HEAD_DIM
SSM_GROUP = 16
SSM_WIDTH = 512
SSM_GROUPS = 32
SSM_STATE = 64
N_STATES = SSM_GROUPS * SSM_STATE
D_FF = 2816
FF_BLK = D_FF // N_DEV
IN_WIDTH = 4096
IN_BLK = IN_WIDTH // N_DEV
NORM_EPS = 1e-6
NEG_INF = -1e30
SCAN_COLS = 256
N_SCAN_BLK = N_STATES // SCAN_COLS
SUBLANES = 8
LANES = 128
MXU_WIDTH = 256
FF_FWD_COLS = D_FF // 2
FF_BWD_COLS = D_FF // 2

ADAM_LR = 0.001
ADAM_B1 = 0.9
ADAM_B2 = 0.999
ADAM_EPS = 1e-08
ADAM_WD = 0.01
ADAM_STEP = 10

VMEM_BIG = 56 * 1024 * 1024


def _cp(sem=None, vmem=None):
    kw = {}
    if sem is not None:
        kw["dimension_semantics"] = sem
    if vmem is not None:
        kw["vmem_limit_bytes"] = vmem
    return pltpu.CompilerParams(**kw)


def _pcall(body, **kw):
    return pl.pallas_call(body, **kw)


def _dot(a, b):
    return jnp.dot(a, b, preferred_element_type=F32)


def _dot_nt(a, b):
    return lax.dot_general(a, b, (((1,), (1,)), ((), ())), preferred_element_type=F32)


def _dot_tn(a, b):
    return lax.dot_general(a, b, (((0,), (0,)), ((), ())), preferred_element_type=F32)


def _sigmoid(x):
    return 1.0 / (1.0 + jnp.exp(-x))


def _row_tile(rows, cap):
    best = None
    for t in range(16, min(rows, cap) + 1, 16):
        if rows % t == 0:
            best = t
    assert best is not None, rows
    return best


def _my_place():
    return lax.axis_index("x"), lax.axis_index("y"), lax.axis_index("c")


def all_gather_list(shards, name):
    n = len(shards)

    def body(*refs):
        ins, outs = refs[:n], refs[n:2 * n]
        send_sems, recv_sems, local_sems = refs[2 * n:]
        x, y, c = _my_place()
        me, sibling = (x, y, c), (x, y, 1 - c)
        chips = [(1 - x, y), (x, 1 - y), (1 - x, 1 - y)]

        def blk(a, px, py, pc):
            return outs[a].at[4 * px + 2 * py + pc]

        def copy(a, k, block, to, src=None):
            return pltpu.make_async_remote_copy(
                src_ref=blk(a, *block) if src is None else src, dst_ref=blk(a, *block),
                send_sem=send_sems.at[a * 7 + k], recv_sem=recv_sems.at[a * 7 + k],
                device_id=to, device_id_type=MESH)

        mine = [pltpu.make_async_copy(ins[a], blk(a, *me), local_sems.at[a]) for a in range(n)]
        for cp in mine:
            cp.start()
        first = []
        for a in range(n):
            first.append(copy(a, 0, me, sibling, src=ins[a]))
            first += [copy(a, 1 + j, me, (*chip, c), src=ins[a]) for j, chip in enumerate(chips)]
        for cp in first:
            cp.start()
        passed = []
        for j, chip in enumerate(chips):
            for a in range(n):
                copy(a, 1 + j, (*chip, c), me).wait_recv()
                cp = copy(a, 4 + j, (*chip, c), sibling)
                cp.start()
                passed.append(cp)
        for a in range(n):
            copy(a, 0, sibling, me).wait_recv()
            for j, chip in enumerate(chips):
                copy(a, 4 + j, (*chip, 1 - c), me).wait_recv()
        for cp in first + passed:
            cp.wait_send()
        for cp in mine:
            cp.wait()

    any_spec = pl.BlockSpec(memory_space=pl.ANY)
    return _pcall(
        body, name=name,
        out_shape=[jax.ShapeDtypeStruct((N_DEV,) + s.shape, s.dtype) for s in shards],
        in_specs=[any_spec] * n, out_specs=[any_spec] * n,
        scratch_shapes=[pltpu.SemaphoreType.DMA((7 * n,)), pltpu.SemaphoreType.DMA((7 * n,)),
                        pltpu.SemaphoreType.DMA((n,))],
    )(*shards)


def rs_sibling_swap(grads, name):
    n = len(grads)

    def body(*refs):
        ins, outs = refs[:n], refs[n:2 * n]
        send_sems, recv_sems = refs[2 * n:]
        x, y, c = _my_place()
        chips = [(x, y), (1 - x, y), (x, 1 - y), (1 - x, 1 - y)]
        copies = []
        for a in range(n):
            for k, (px, py) in enumerate(chips):
                copies.append(pltpu.make_async_remote_copy(
                    src_ref=ins[a].at[4 * px + 2 * py + (1 - c)], dst_ref=outs[a].at[k],
                    send_sem=send_sems.at[4 * a + k], recv_sem=recv_sems.at[4 * a + k],
                    device_id=(x, y, 1 - c), device_id_type=MESH))
        for cp in copies:
            cp.start()
        for cp in copies:
            cp.wait()

    any_spec = pl.BlockSpec(memory_space=pl.ANY)
    return _pcall(
        body, name=name,
        out_shape=[jax.ShapeDtypeStruct((4,) + g.shape[1:], g.dtype) for g in grads],
        in_specs=[any_spec] * n, out_specs=[any_spec] * n,
        scratch_shapes=[pltpu.SemaphoreType.DMA((4 * n,)), pltpu.SemaphoreType.DMA((4 * n,))],
    )(*grads)


def rs_chip_exchange(parts, name):
    n = len(parts)

    def body(*refs):
        ins, outs = refs[:n], refs[n:2 * n]
        send_sems, recv_sems = refs[2 * n:]
        x, y, c = _my_place()
        chips = [(1 - x, y), (x, 1 - y), (1 - x, 1 - y)]
        copies = []
        for a in range(n):
            for k, (px, py) in enumerate(chips):
                copies.append(pltpu.make_async_remote_copy(
                    src_ref=ins[a].at[k], dst_ref=outs[a].at[k],
                    send_sem=send_sems.at[3 * a + k], recv_sem=recv_sems.at[3 * a + k],
                    device_id=(px, py, c), device_id_type=MESH))
        for cp in copies:
            cp.start()
        for cp in copies:
            cp.wait()

    any_spec = pl.BlockSpec(memory_space=pl.ANY)
    return _pcall(
        body, name=name,
        out_shape=[jax.ShapeDtypeStruct(p.shape, p.dtype) for p in parts],
        in_specs=[any_spec] * n, out_specs=[any_spec] * n,
        scratch_shapes=[pltpu.SemaphoreType.DMA((3 * n,)), pltpu.SemaphoreType.DMA((3 * n,))],
    )(*parts)


def pair_sums(idx, g, r1, name):
    _, rows, cols = g.shape
    tr = _row_tile(rows, 256)

    def body(idx_ref, g_ref, r_ref, o_ref):
        o_ref[...] = (g_ref[...].astype(F32) + r_ref[...].astype(F32)).astype(BF16)

    return _pcall(
        body, name=name,
        out_shape=jax.ShapeDtypeStruct((3, rows, cols), BF16),
        grid_spec=pltpu.PrefetchScalarGridSpec(
            num_scalar_prefetch=1, grid=(3, rows // tr),
            in_specs=[pl.BlockSpec((None, tr, cols), lambda k, r, ix: (ix[k + 1], r, 0)),
                      pl.BlockSpec((None, tr, cols), lambda k, r, ix: (k + 1, r, 0))],
            out_specs=pl.BlockSpec((None, tr, cols), lambda k, r, ix: (k, r, 0))),
        compiler_params=_cp(("arbitrary", "arbitrary")),
    )(idx, g, r1)


def _adam_math(w, g, m, v):
    m = ADAM_B1 * m + (1.0 - ADAM_B1) * g
    v = ADAM_B2 * v + (1.0 - ADAM_B2) * (g * g)
    m_hat = m / (1.0 - ADAM_B1 ** ADAM_STEP)
    v_hat = v / (1.0 - ADAM_B2 ** ADAM_STEP)
    delta = -ADAM_LR * (m_hat / (jnp.sqrt(v_hat) + ADAM_EPS) + ADAM_WD * w)
    return delta, m, v


def adamw_sharded(idx, g, r1, r2, w, m, v, name):
    rows, cols = w.shape
    tr = _row_tile(rows, 256)

    def body(idx_ref, g_ref, r1_ref, r2_ref, w_ref, m_ref, v_ref, go_ref, d_ref, mo_ref, vo_ref):
        grad = g_ref[...].astype(F32) + r1_ref[...].astype(F32)
        for k in range(3):
            grad = grad + r2_ref[k].astype(F32)
        delta, m_new, v_new = _adam_math(w_ref[...], grad, m_ref[...], v_ref[...])
        go_ref[...] = grad
        d_ref[...] = delta
        mo_ref[...] = m_new
        vo_ref[...] = v_new

    tile = pl.BlockSpec((tr, cols), lambda r, ix: (r, 0))
    out = jax.ShapeDtypeStruct((rows, cols), F32)
    return _pcall(
        body, name=name, out_shape=[out] * 4,
        grid_spec=pltpu.PrefetchScalarGridSpec(
            num_scalar_prefetch=1, grid=(rows // tr,),
            in_specs=[pl.BlockSpec((None, tr, cols), lambda r, ix: (ix[0], r, 0)),
                      pl.BlockSpec((None, tr, cols), lambda r, ix: (0, r, 0)),
                      pl.BlockSpec((3, tr, cols), lambda r, ix: (0, r, 0)),
                      tile, tile, tile],
            out_specs=[tile] * 4),
        compiler_params=_cp(("arbitrary",)),
    )(idx, g, r1, r2, w, m, v)


def adamw_small(parts, w, m, v, name):
    _, rows, cols = parts.shape

    def body(p_ref, w_ref, m_ref, v_ref, go_ref, d_ref, mo_ref, vo_ref):
        grad = p_ref[0]
        for k in range(1, N_DEV):
            grad = grad + p_ref[k]
        delta, m_new, v_new = _adam_math(w_ref[...], grad, m_ref[...], v_ref[...])
        go_ref[...] = grad
        d_ref[...] = delta
        mo_ref[...] = m_new
        vo_ref[...] = v_new

    out = jax.ShapeDtypeStruct((rows, cols), F32)
    return _pcall(body, name=name, out_shape=[out] * 4, compiler_params=_cp(vmem=VMEM_BIG))(parts, w, m, v)


def adamw_plain(g, w, m, v, name):
    def body(g_ref, w_ref, m_ref, v_ref, d_ref, mo_ref, vo_ref):
        delta, m_new, v_new = _adam_math(w_ref[...], g_ref[...], m_ref[...], v_ref[...])
        d_ref[...] = delta
        mo_ref[...] = m_new
        vo_ref[...] = v_new

    out = jax.ShapeDtypeStruct(w.shape, F32)
    return _pcall(body, name=name, out_shape=[out] * 3)(g, w, m, v)


def _rms_fwd(x, g):
    r = lax.rsqrt(jnp.mean(x * x, axis=-1, keepdims=True) + NORM_EPS)
    return x * r * g


def _rms_bwd(x, g, dy):
    r = lax.rsqrt(jnp.mean(x * x, axis=-1, keepdims=True) + NORM_EPS)
    xh = x * r
    t = dy * g
    dx = r * (t - xh * jnp.mean(t * xh, axis=-1, keepdims=True))
    return dx, jnp.sum(dy * xh, axis=0, keepdims=True)


def _accumulate(ref, val, first):
    @pl.when(first)
    def _():
        ref[...] = val

    @pl.when(jnp.logical_not(first))
    def _():
        ref[...] += val


def _col_chunks(width):
    return [(c0, min(MXU_WIDTH, width - c0)) for c0 in range(0, width, MXU_WIDTH)]


def ffn_forward(h, norm, w1, w3, w2, tm, tn, name):
    t_rows = h.shape[0]
    nj = D_FF // tn

    def body(h_ref, g_ref, w1_ref, w3_ref, w2_ref, out_ref, hn_ref, a_ref, b_ref, acc_ref):
        j = pl.program_id(1)

        @pl.when(j == 0)
        def _():
            hn_ref[...] = _rms_fwd(h_ref[...], g_ref[...]).astype(BF16)
            acc_ref[...] = jnp.zeros_like(acc_ref)

        hn = hn_ref[...]
        for c0, cw in _col_chunks(tn):
            a = _dot(hn, w1_ref[:, c0:c0 + cw])
            b = _dot(hn, w3_ref[:, c0:c0 + cw])
            a_ref[:, c0:c0 + cw] = a.astype(BF16)
            b_ref[:, c0:c0 + cw] = b.astype(BF16)
            hid = (a * _sigmoid(a) * b).astype(BF16)
            acc_ref[...] += _dot(hid, w2_ref[c0:c0 + cw, :])

        @pl.when(j == nj - 1)
        def _():
            out_ref[...] = h_ref[...] + 0.5 * acc_ref[...]

    row = pl.BlockSpec((tm, D_MODEL), lambda i, j: (i, 0))
    hid_blk = pl.BlockSpec((tm, tn), lambda i, j: (i, j))
    w_col = pl.BlockSpec((D_MODEL, tn), lambda i, j: (0, j))
    return _pcall(
        body, name=name, grid=(t_rows // tm, nj),
        in_specs=[row, pl.BlockSpec((1, D_MODEL), lambda i, j: (0, 0)), w_col, w_col,
                  pl.BlockSpec((tn, D_MODEL), lambda i, j: (j, 0))],
        out_specs=[row, row, hid_blk, hid_blk],
        out_shape=[jax.ShapeDtypeStruct((t_rows, D_MODEL), F32), jax.ShapeDtypeStruct((t_rows, D_MODEL), BF16),
                   jax.ShapeDtypeStruct((t_rows, D_FF), BF16), jax.ShapeDtypeStruct((t_rows, D_FF), BF16)],
        scratch_shapes=[pltpu.VMEM((tm, D_MODEL), F32)],
        compiler_params=_cp(("arbitrary", "arbitrary"), VMEM_BIG),
    )(h, norm, w1, w3, w2)


def _resident(shape):
    return pl.BlockSpec(shape, lambda *_: (0,) * len(shape), pipeline_mode=pl.Buffered(1))


def ffn_backward_hidden(dh, a, b, w2, tm, name):
    t_rows = dh.shape[0]

    def body(dh_ref, a_ref, b_ref, w2_ref, da_ref, db_ref):
        dhb = (0.5 * dh_ref[...]).astype(BF16)
        for c0, cw in _col_chunks(D_FF):
            dhid = _dot_nt(dhb, w2_ref[c0:c0 + cw, :])
            av = a_ref[:, c0:c0 + cw].astype(F32)
            bv = b_ref[:, c0:c0 + cw].astype(F32)
            s = _sigmoid(av)
            da_ref[:, c0:c0 + cw] = (dhid * bv * (s * (1.0 + av * (1.0 - s)))).astype(BF16)
            db_ref[:, c0:c0 + cw] = (dhid * (av * s)).astype(BF16)

    hid = pl.BlockSpec((tm, D_FF), lambda i: (i, 0))
    return _pcall(
        body, name=name, grid=(t_rows // tm,),
        in_specs=[pl.BlockSpec((tm, D_MODEL), lambda i: (i, 0)), hid, hid, _resident((D_FF, D_MODEL))],
        out_specs=[hid, hid],
        out_shape=[jax.ShapeDtypeStruct((t_rows, D_FF), BF16), jax.ShapeDtypeStruct((t_rows, D_FF), BF16)],
        compiler_params=_cp(("arbitrary",), VMEM_BIG),
    )(dh, a, b, w2)


def ffn_backward_input(dh, h, norm, da, db, w1, w3, tm, name):
    t_rows = h.shape[0]

    def body(dh_ref, h_ref, g_ref, da_ref, db_ref, w1_ref, w3_ref, dhin_ref, dg_ref):
        dhn = _dot_nt(da_ref[...], w1_ref[...]) + _dot_nt(db_ref[...], w3_ref[...])
        dx, dg = _rms_bwd(h_ref[...], g_ref[...], dhn)
        dhin_ref[...] = dh_ref[...] + dx
        _accumulate(dg_ref, dg, pl.program_id(0) == 0)

    row = pl.BlockSpec((tm, D_MODEL), lambda i: (i, 0))
    vec = pl.BlockSpec((1, D_MODEL), lambda i: (0, 0))
    hid = pl.BlockSpec((tm, D_FF), lambda i: (i, 0))
    return _pcall(
        body, name=name, grid=(t_rows // tm,),
        in_specs=[row, row, vec, hid, hid, _resident((D_MODEL, D_FF)), _resident((D_MODEL, D_FF))],
        out_specs=[row, vec],
        out_shape=[jax.ShapeDtypeStruct((t_rows, D_MODEL), F32), jax.ShapeDtypeStruct((1, D_MODEL), F32)],
        compiler_params=_cp(("arbitrary",), VMEM_BIG),
    )(dh, h, norm, da, db, w1, w3)


def ffn_backward_weights(hn, dh, a, b, da, db, tm, tn, name):
    t_rows = hn.shape[0]
    ni = t_rows // tm

    def body(hn_ref, dh_ref, a_ref, b_ref, da_ref, db_ref, dw1_ref, dw3_ref, dw2_ref):
        first = pl.program_id(1) == 0
        hn_t = hn_ref[...].T
        dhb = (0.5 * dh_ref[...]).astype(BF16)
        for c0, cw in _col_chunks(tn):
            cols = slice(c0, c0 + cw)
            _accumulate(dw1_ref.at[:, cols], _dot(hn_t, da_ref[:, cols]), first)
            _accumulate(dw3_ref.at[:, cols], _dot(hn_t, db_ref[:, cols]), first)
            av = a_ref[:, cols].astype(F32)
            hid = (av * _sigmoid(av) * b_ref[:, cols].astype(F32)).astype(BF16)
            _accumulate(dw2_ref.at[cols, :], _dot_tn(hid, dhb), first)

    row = pl.BlockSpec((tm, D_MODEL), lambda j, i: (i, 0))
    hid_blk = pl.BlockSpec((tm, tn), lambda j, i: (i, j))
    w_col = pl.BlockSpec((D_MODEL, tn), lambda j, i: (0, j), pipeline_mode=pl.Buffered(1))
    w_row = pl.BlockSpec((tn, D_MODEL), lambda j, i: (j, 0), pipeline_mode=pl.Buffered(1))
    return _pcall(
        body, name=name, grid=(D_FF // tn, ni),
        in_specs=[row, row, hid_blk, hid_blk, hid_blk, hid_blk],
        out_specs=[w_col, w_col, w_row],
        out_shape=[jax.ShapeDtypeStruct((D_MODEL, D_FF), F32), jax.ShapeDtypeStruct((D_MODEL, D_FF), F32),
                   jax.ShapeDtypeStruct((D_FF, D_MODEL), F32)],
        compiler_params=_cp(("arbitrary", "arbitrary"), VMEM_BIG),
    )(hn, dh, a, b, da, db)


def mix_forward(h, norm, wing, tm, name):
    t_rows = h.shape[0]

    def body(h_ref, g_ref, w_ref, hn_ref, p_ref):
        hn = _rms_fwd(h_ref[...], g_ref[...]).astype(BF16)
        hn_ref[...] = hn
        for j in range(N_DEV):
            p_ref[:, j * IN_BLK:(j + 1) * IN_BLK] = _dot(hn, w_ref[j]).astype(BF16)

    row = pl.BlockSpec((tm, D_MODEL), lambda i: (i, 0))
    return _pcall(
        body, name=name, grid=(t_rows // tm,),
        in_specs=[row, pl.BlockSpec((1, D_MODEL), lambda i: (0, 0)),
                  pl.BlockSpec((N_DEV, D_MODEL, IN_BLK), lambda i: (0, 0, 0))],
        out_specs=[row, pl.BlockSpec((tm, IN_WIDTH), lambda i: (i, 0))],
        out_shape=[jax.ShapeDtypeStruct((t_rows, D_MODEL), BF16), jax.ShapeDtypeStruct((t_rows, IN_WIDTH), BF16)],
        compiler_params=_cp(("arbitrary",), VMEM_BIG),
    )(h, norm, wing)


def mix_backward_act(dh, h, norm, dproj, wing, tm, name):
    t_rows = h.shape[0]
    per_step = 4
    nj = N_DEV // per_step

    def body(dh_ref, h_ref, g_ref, dp_ref, w_ref, dhin_ref, dg_ref, acc_ref):
        i, j = pl.program_id(0), pl.program_id(1)
        part = functools.reduce(
            lambda u, w: u + w, [_dot_nt(dp_ref[:, k * IN_BLK:(k + 1) * IN_BLK], w_ref[k]) for k in range(per_step)])
        _accumulate(acc_ref, part, j == 0)

        @pl.when(j == nj - 1)
        def _():
            dx, dg = _rms_bwd(h_ref[...], g_ref[...], acc_ref[...])
            dhin_ref[...] = dh_ref[...] + dx
            _accumulate(dg_ref, dg, i == 0)

    row = pl.BlockSpec((tm, D_MODEL), lambda i, j: (i, 0))
    vec = pl.BlockSpec((1, D_MODEL), lambda i, j: (0, 0))
    return _pcall(
        body, name=name, grid=(t_rows // tm, nj),
        in_specs=[row, row, vec, pl.BlockSpec((tm, per_step * IN_BLK), lambda i, j: (i, j)),
                  pl.BlockSpec((per_step, D_MODEL, IN_BLK), lambda i, j: (j, 0, 0))],
        out_specs=[row, vec],
        out_shape=[jax.ShapeDtypeStruct((t_rows, D_MODEL), F32), jax.ShapeDtypeStruct((1, D_MODEL), F32)],
        scratch_shapes=[pltpu.VMEM((tm, D_MODEL), F32)],
        compiler_params=_cp(("arbitrary", "arbitrary"), VMEM_BIG),
    )(dh, h, norm, dproj, wing)


def mix_backward_weights(hn, dproj, tm, name):
    t_rows = hn.shape[0]
    ni = t_rows // tm
    per_step = 2

    def body(hn_ref, dp_ref, dw_ref, acc):
        i = pl.program_id(1)
        _accumulate(acc, _dot_tn(hn_ref[...], dp_ref[...]), i == 0)

        @pl.when(i == ni - 1)
        def _():
            for k in range(per_step):
                dw_ref[k] = acc[:, k * IN_BLK:(k + 1) * IN_BLK].astype(BF16)

    return _pcall(
        body, name=name, grid=(N_DEV // per_step, ni),
        in_specs=[pl.BlockSpec((tm, D_MODEL), lambda j, i: (i, 0)),
                  pl.BlockSpec((tm, per_step * IN_BLK), lambda j, i: (i, j))],
        out_specs=pl.BlockSpec((per_step, D_MODEL, IN_BLK), lambda j, i: (j, 0, 0)),
        out_shape=jax.ShapeDtypeStruct((N_DEV, D_MODEL, IN_BLK), BF16),
        scratch_shapes=[pltpu.VMEM((D_MODEL, per_step * IN_BLK), F32)],
        compiler_params=_cp(("arbitrary", "arbitrary"), VMEM_BIG),
    )(hn, dproj)


GELU_C = 0.7978845608028654
GELU_K = 0.044715


def _gelu(x):
    return 0.5 * x * (1.0 + jnp.tanh(GELU_C * (x + GELU_K * (x * x * x))))


def _gelu_and_grad(x):
    th = jnp.tanh(GELU_C * (x + GELU_K * (x * x * x)))
    val = 0.5 * x * (1.0 + th)
    grad = 0.5 * (1.0 + th) + 0.5 * x * (1.0 - th * th) * (GELU_C * (1.0 + 3.0 * GELU_K * (x * x)))
    return val, grad


def merge_forward(h, yraw, attn, proj, glu_a, glu_b, w_out, tm, name):
    t_rows = h.shape[0]

    def body(h_ref, y_ref, at_ref, gate_ref, a_ref, b_ref, wo_ref, out_ref):
        y = _gelu(y_ref[...]).astype(BF16)
        ssm = _dot(y, a_ref[...]) * _sigmoid(_dot(y, b_ref[...]))
        ga = gate_ref[:, :D_MODEL].astype(F32)
        gs = gate_ref[:, D_MODEL:].astype(F32)
        merged = _sigmoid(ga) * at_ref[...].astype(F32) + _sigmoid(gs) * ssm
        out_ref[...] = h_ref[...] + _dot(merged.astype(BF16), wo_ref[...])

    row = pl.BlockSpec((tm, D_MODEL), lambda i: (i, 0))
    glu = pl.BlockSpec((SSM_WIDTH, D_MODEL), lambda i: (0, 0))
    return _pcall(
        body, name=name, grid=(t_rows // tm,),
        in_specs=[row, pl.BlockSpec((tm, SSM_WIDTH), lambda i: (i, 0)), row,
                  pl.BlockSpec((tm, 2 * D_MODEL), lambda i: (i, 1)), glu, glu,
                  pl.BlockSpec((D_MODEL, D_MODEL), lambda i: (0, 0))],
        out_specs=row, out_shape=jax.ShapeDtypeStruct((t_rows, D_MODEL), F32),
        compiler_params=_cp(("arbitrary",), VMEM_BIG),
    )(h, yraw, attn, proj, glu_a, glu_b, w_out)


def merge_backward(dh, yraw, attn, proj, glu_a, glu_b, w_out, tm, name):
    t_rows = dh.shape[0]

    def body(dh_ref, y_ref, at_ref, gate_ref, a_ref, b_ref, wo_ref,
             dat_ref, dy_ref, dgate_ref, dwo_ref, da_ref, db_ref):
        first = pl.program_id(0) == 0
        d16 = dh_ref[...].astype(BF16)
        dmerged = _dot_nt(d16, wo_ref[...])
        gel, dgel = _gelu_and_grad(y_ref[...].astype(F32))
        y16 = gel.astype(BF16)
        ya = _dot(y16, a_ref[...])
        sb = _sigmoid(_dot(y16, b_ref[...]))
        ssm = ya * sb
        sa = _sigmoid(gate_ref[:, :D_MODEL].astype(F32))
        ss = _sigmoid(gate_ref[:, D_MODEL:].astype(F32))
        attn_v = at_ref[...].astype(F32)
        merged = (sa * attn_v + ss * ssm).astype(BF16)
        _accumulate(dwo_ref, _dot_tn(merged, d16), first)
        dat_ref[...] = (dmerged * sa).astype(BF16)
        dgate_ref[:, :D_MODEL] = (dmerged * attn_v * sa * (1.0 - sa)).astype(BF16)
        dgate_ref[:, D_MODEL:] = (dmerged * ssm * ss * (1.0 - ss)).astype(BF16)
        dssm = dmerged * ss
        dya = (dssm * sb).astype(BF16)
        dyb = (dssm * ya * sb * (1.0 - sb)).astype(BF16)
        _accumulate(da_ref, _dot_tn(y16, dya), first)
        _accumulate(db_ref, _dot_tn(y16, dyb), first)
        dy = _dot_nt(dya, a_ref[...]) + _dot_nt(dyb, b_ref[...])
        dy_ref[...] = (dy * dgel).astype(BF16)

    row = pl.BlockSpec((tm, D_MODEL), lambda i: (i, 0))
    ssm_row = pl.BlockSpec((tm, SSM_WIDTH), lambda i: (i, 0))
    gates = pl.BlockSpec((tm, 2 * D_MODEL), lambda i: (i, 1))
    glu = pl.BlockSpec((SSM_WIDTH, D_MODEL), lambda i: (0, 0))
    wo = pl.BlockSpec((D_MODEL, D_MODEL), lambda i: (0, 0))
    return _pcall(
        body, name=name, grid=(t_rows // tm,),
        in_specs=[row, ssm_row, row, gates, glu, glu, wo],
        out_specs=[row, ssm_row, gates, wo, glu, glu],
        out_shape=[jax.ShapeDtypeStruct((t_rows, D_MODEL), BF16), jax.ShapeDtypeStruct((t_rows, SSM_WIDTH), BF16),
                   jax.ShapeDtypeStruct((t_rows, IN_WIDTH), BF16), jax.ShapeDtypeStruct((D_MODEL, D_MODEL), F32),
                   jax.ShapeDtypeStruct((SSM_WIDTH, D_MODEL), F32), jax.ShapeDtypeStruct((SSM_WIDTH, D_MODEL), F32)],
        compiler_params=_cp(("arbitrary",), VMEM_BIG),
    )(dh, yraw, attn, proj, glu_a, glu_b, w_out)


def final_loss_backward(h, target, norm, seq, tm, name):
    t_rows = h.shape[0]
    tiles_per_example = (seq + N_META) // tm

    def body(h_ref, t_ref, g_ref, dh_ref, loss_ref, dg_ref):
        i = pl.program_id(0)
        x = h_ref[...]
        g = g_ref[...]
        r = lax.rsqrt(jnp.mean(x * x, axis=-1, keepdims=True) + NORM_EPS)
        xh = x * r
        pos = lax.broadcasted_iota(jnp.int32, (tm, 1), 0) + (i % tiles_per_example) * tm
        diff = jnp.where(pos < seq, xh * g - t_ref[...], 0.0)
        part = 0.5 * jnp.sum(jnp.sum(diff * diff, axis=-1, keepdims=True), axis=0, keepdims=True) / D_MODEL
        dy = diff / D_MODEL
        t = dy * g
        dh_ref[...] = r * (t - xh * jnp.mean(t * xh, axis=-1, keepdims=True))
        _accumulate(loss_ref, jnp.broadcast_to(part, (1, LANES)), i == 0)
        _accumulate(dg_ref, jnp.sum(dy * xh, axis=0, keepdims=True), i == 0)

    row = pl.BlockSpec((tm, D_MODEL), lambda i: (i, 0))
    vec = pl.BlockSpec((1, D_MODEL), lambda i: (0, 0))
    return _pcall(
        body, name=name, grid=(t_rows // tm,),
        in_specs=[row, row, vec],
        out_specs=[row, pl.BlockSpec((1, LANES), lambda i: (0, 0)), vec],
        out_shape=[jax.ShapeDtypeStruct((t_rows, D_MODEL), F32), jax.ShapeDtypeStruct((1, LANES), F32),
                   jax.ShapeDtypeStruct((1, D_MODEL), F32)],
        compiler_params=_cp(("arbitrary",), VMEM_BIG),
    )(h, target, norm)


ATTN_SCALE = HEAD_DIM ** -0.5
STACK_HEADS = (0, 2, 1, 3)


def _lane_half(shape, hf):
    lane = lax.broadcasted_iota(jnp.int32, shape, 1)
    return (lane < HEAD_DIM) if hf == 0 else (lane >= HEAD_DIM)


def _kv_variants(ref, rows, kh):
    tile = kh // 2
    t = ref[rows, tile * LANES:(tile + 1) * LANES].astype(F32)
    swapped = pltpu.roll(t, HEAD_DIM, axis=1)
    at_low, at_high = (t, swapped) if kh % 2 == 0 else (swapped, t)
    lo = jnp.where(_lane_half(t.shape, 0), at_low, 0.0).astype(BF16)
    hi = jnp.where(_lane_half(t.shape, 1), at_high, 0.0).astype(BF16)
    return lo, hi


def _to_kv_lanes(lo, hi, kh):
    lo = jnp.where(_lane_half(lo.shape, 0), lo, 0.0)
    hi = jnp.where(_lane_half(hi.shape, 1), hi, 0.0)
    if kh % 2 == 0:
        return lo + pltpu.roll(hi, HEAD_DIM, axis=1)
    return pltpu.roll(lo, HEAD_DIM, axis=1) + hi


def _stacked(ref, rows, kh):
    col = kh * 2 * LANES
    return jnp.concatenate([ref[rows, col:col + LANES], ref[rows, col + LANES:col + 2 * LANES]], axis=0)


def _sink_column(sink_ref, kh, nq):
    row = lax.broadcasted_iota(jnp.int32, (4 * nq, 1), 0)
    col = jnp.zeros((4 * nq, 1), F32)
    for quarter, g in enumerate(STACK_HEADS):
        col = jnp.where(row // nq == quarter, sink_ref[0, kh * Q_PER_KV + g], col)
    return col


def _softmax_parts(qs, key_tiles, masks, sink):
    scores = []
    for (k_lo, k_hi), mask in zip(key_tiles, masks):
        s = jnp.concatenate([_dot_nt(qs, k_lo), _dot_nt(qs, k_hi)], axis=0) * ATTN_SCALE
        scores.append(s if mask is None else jnp.where(mask, s, NEG_INF))
    m = functools.reduce(jnp.maximum, [jnp.max(s, axis=-1, keepdims=True) for s in scores])
    m = jnp.maximum(m, sink)
    probs = [jnp.exp(s - m) for s in scores]
    e_sink = jnp.exp(sink - m)
    den = functools.reduce(lambda u, w: u + w, [jnp.sum(p, axis=-1, keepdims=True) for p in probs]) + e_sink
    return probs, 1.0 / den, e_sink


def _band_mask(nq, first):
    keys = BLOCK if first else 2 * BLOCK
    qi = lax.broadcasted_iota(jnp.int32, (4 * nq, keys), 0) % nq
    kj = lax.broadcasted_iota(jnp.int32, (4 * nq, keys), 1)
    if first:
        return kj <= qi
    return jnp.logical_and(kj > qi, kj <= qi + BLOCK)


def _meta_mask():
    qi = lax.broadcasted_iota(jnp.int32, (4 * N_META, N_META), 0) % N_META
    kj = lax.broadcasted_iota(jnp.int32, (4 * N_META, N_META), 1)
    return kj <= qi


def _attention_schedule(seq, queries, carry):
    meta_rows = pl.ds(seq, N_META)
    carry = queries(pl.ds(0, BLOCK), BLOCK, [pl.ds(0, BLOCK), meta_rows], [_band_mask(BLOCK, True), None], carry)

    def block(n, c):
        r0 = pl.multiple_of(n * BLOCK, BLOCK)
        p0 = pl.multiple_of((n - 1) * BLOCK, BLOCK)
        return queries(pl.ds(r0, BLOCK), BLOCK, [pl.ds(p0, 2 * BLOCK), meta_rows], [_band_mask(BLOCK, False), None], c)

    carry = lax.fori_loop(1, seq // BLOCK, block, carry)
    return queries(meta_rows, N_META, [meta_rows], [_meta_mask()], carry)


def attention_forward(proj3, sinks, seq, name):
    n_b, n_l, _ = proj3.shape

    def body(sink_ref, q_ref, k_ref, v_ref, o_ref):
        def queries(q_rows, nq, key_rows, masks, carry):
            for kh in range(N_KV_HEADS):
                ks = [_kv_variants(k_ref, r, kh) for r in key_rows]
                vs = [_kv_variants(v_ref, r, kh) for r in key_rows]
                qs = _stacked(q_ref, q_rows, kh)
                probs, inv, _ = _softmax_parts(qs, ks, masks, _sink_column(sink_ref, kh, nq))
                probs = [p.astype(BF16) for p in probs]
                o_lo = functools.reduce(lambda u, w: u + w, [_dot(p[:2 * nq], v_lo) for p, (v_lo, _) in zip(probs, vs)])
                o_hi = functools.reduce(lambda u, w: u + w, [_dot(p[2 * nq:], v_hi) for p, (_, v_hi) in zip(probs, vs)])
                out = (o_lo * inv[:2 * nq] + o_hi * inv[2 * nq:]).astype(BF16)
                col = kh * 2 * LANES
                o_ref[q_rows, col:col + LANES] = out[:nq]
                o_ref[q_rows, col + LANES:col + 2 * LANES] = out[nq:]
            return carry

        _attention_schedule(seq, queries, 0)

    return _pcall(
        body, name=name, grid=(n_b,),
        in_specs=[pl.BlockSpec(memory_space=pltpu.SMEM),
                  pl.BlockSpec((None, n_l, D_MODEL), lambda b: (b, 0, 0)),
                  pl.BlockSpec((None, n_l, KV_WIDTH), lambda b: (b, 0, D_MODEL // KV_WIDTH)),
                  pl.BlockSpec((None, n_l, KV_WIDTH), lambda b: (b, 0, D_MODEL // KV_WIDTH + 1))],
        out_specs=pl.BlockSpec((None, n_l, D_MODEL), lambda b: (b, 0, 0)),
        out_shape=jax.ShapeDtypeStruct((n_b, n_l, D_MODEL), BF16),
        compiler_params=_cp(("arbitrary",), VMEM_BIG),
    )(sinks, proj3, proj3, proj3)


def attention_backward(proj3, dattn3, dproj3, sinks, seq, name):
    n_b, n_l, _ = proj3.shape
    qkv_width = D_MODEL + 2 * KV_WIDTH

    def body(sink_ref, q_ref, k_ref, v_ref, do_ref, _, dqkv_ref, dsink_ref, dk_ref, dv_ref):
        dk_ref[...] = jnp.zeros_like(dk_ref)
        dv_ref[...] = jnp.zeros_like(dv_ref)
        sub = lax.broadcasted_iota(jnp.int32, (SUBLANES, LANES), 0)
        lane = lax.broadcasted_iota(jnp.int32, (SUBLANES, LANES), 1)

        def queries(q_rows, nq, key_rows, masks, dsink):
            for kh in range(N_KV_HEADS):
                ks = [_kv_variants(k_ref, r, kh) for r in key_rows]
                vs = [_kv_variants(v_ref, r, kh) for r in key_rows]
                qs = _stacked(q_ref, q_rows, kh)
                dos = _stacked(do_ref, q_rows, kh)
                probs, inv, e_sink = _softmax_parts(qs, ks, masks, _sink_column(sink_ref, kh, nq))
                probs = [p * inv for p in probs]
                dps = [jnp.concatenate([_dot_nt(dos, v_lo), _dot_nt(dos, v_hi)], axis=0) for v_lo, v_hi in vs]
                delta = functools.reduce(
                    lambda u, w: u + w, [jnp.sum(p * dp, axis=-1, keepdims=True) for p, dp in zip(probs, dps)])
                d_sink = -(e_sink * inv) * delta
                for quarter, g in enumerate(STACK_HEADS):
                    d_here = jnp.sum(d_sink[quarter * nq:(quarter + 1) * nq], axis=0, keepdims=True)
                    dsink = dsink + jnp.where(jnp.logical_and(sub == 0, lane == kh * Q_PER_KV + g), d_here, 0.0)
                dq = None
                tile = slice((kh // 2) * LANES, (kh // 2 + 1) * LANES)
                for r, p, dp, (k_lo, k_hi) in zip(key_rows, probs, dps, ks):
                    ds = (p * (dp - delta)).astype(BF16)
                    p16 = p.astype(BF16)
                    dq_x = _dot(ds[:2 * nq], k_lo) + _dot(ds[2 * nq:], k_hi)
                    dq = dq_x if dq is None else dq + dq_x
                    dk_ref[r, tile] += _to_kv_lanes(_dot_tn(ds[:2 * nq], qs), _dot_tn(ds[2 * nq:], qs), kh) * ATTN_SCALE
                    dv_ref[r, tile] += _to_kv_lanes(_dot_tn(p16[:2 * nq], dos), _dot_tn(p16[2 * nq:], dos), kh)
                dq = (dq * ATTN_SCALE).astype(BF16)
                col = kh * 2 * LANES
                dqkv_ref[q_rows, col:col + LANES] = dq[:nq]
                dqkv_ref[q_rows, col + LANES:col + 2 * LANES] = dq[nq:]
            return dsink

        dsink_ref[...] = _attention_schedule(seq, queries, jnp.zeros((SUBLANES, LANES), F32))
        dqkv_ref[:, D_MODEL:D_MODEL + KV_WIDTH] = dk_ref[...].astype(BF16)
        dqkv_ref[:, D_MODEL + KV_WIDTH:] = dv_ref[...].astype(BF16)

    return _pcall(
        body, name=name, grid=(n_b,),
        in_specs=[pl.BlockSpec(memory_space=pltpu.SMEM),
                  pl.BlockSpec((None, n_l, D_MODEL), lambda b: (b, 0, 0)),
                  pl.BlockSpec((None, n_l, KV_WIDTH), lambda b: (b, 0, D_MODEL // KV_WIDTH)),
                  pl.BlockSpec((None, n_l, KV_WIDTH), lambda b: (b, 0, D_MODEL // KV_WIDTH + 1)),
                  pl.BlockSpec((None, n_l, D_MODEL), lambda b: (b, 0, 0)),
                  pl.BlockSpec(memory_space=pl.ANY)],
        out_specs=[pl.BlockSpec((None, n_l, qkv_width), lambda b: (b, 0, 0)),
                   pl.BlockSpec((None, SUBLANES, LANES), lambda b: (b, 0, 0))],
        out_shape=[jax.ShapeDtypeStruct(dproj3.shape, BF16), jax.ShapeDtypeStruct((n_b, SUBLANES, LANES), F32)],
        scratch_shapes=[pltpu.VMEM((n_l, KV_WIDTH), F32), pltpu.VMEM((n_l, KV_WIDTH), F32)],
        input_output_aliases={5: 0},
        compiler_params=_cp(("arbitrary",), VMEM_BIG),
    )(sinks, proj3, proj3, proj3, dattn3, dproj3)


TAB_ROWS = 8
SCAN_UNROLL = 4


def _cmul(ar, ai, br, bi):
    return ar * br - ai * bi, ar * bi + ai * br


def _discretise(ar, ai, ls):
    step = jnp.exp(ls)
    mag = jnp.exp(ar * step)
    ang = ai * step
    cos, sin = jnp.cos(ang), jnp.sin(ang)
    lr, li = mag * cos, mag * sin
    den = ar * ar + ai * ai
    nr, ni = lr - 1.0, li
    cr = (nr * ar + ni * ai) / den
    ci = (ni * ar - nr * ai) / den
    return step, mag, lr, li, den, nr, ni, cr, ci


def _scan_tables(lr, li, reverse):
    n = lr.shape[-1]
    pw = [(lr, li)]
    for _ in range(SUBLANES - 1):
        pw.append(_cmul(pw[-1][0], pw[-1][1], lr, li))
    row = lax.broadcasted_iota(jnp.int32, (SUBLANES, n), 0)
    out = []
    for d in (1, 2, 4):
        ok = (row + d <= SUBLANES - 1) if reverse else (row >= d)
        out += [jnp.where(ok, pw[d - 1][0], 0.0), jnp.where(ok, pw[d - 1][1], 0.0)]
    cr = jnp.zeros((SUBLANES, n), F32)
    ci = jnp.zeros((SUBLANES, n), F32)
    for r in range(SUBLANES):
        e = (SUBLANES - r) if reverse else (r + 1)
        cr = jnp.where(row == r, pw[e - 1][0], cr)
        ci = jnp.where(row == r, pw[e - 1][1], ci)
    return out + [cr, ci]


def ssm_prepare(ar, ai, ls, br_t, bi_t, name):
    def body(ar_ref, ai_ref, ls_ref, br_ref, bi_ref, bbr_ref, bbi_ref, tf_ref, tr_ref):
        _, _, lr, li, _, _, _, cr, ci = _discretise(ar_ref[...], ai_ref[...], ls_ref[...])
        br, bi = br_ref[...], bi_ref[...]
        bbr_ref[...] = cr * br - ci * bi
        bbi_ref[...] = cr * bi + ci * br
        for k, t in enumerate(_scan_tables(lr, li, False)):
            tf_ref[k] = t
        for k, t in enumerate(_scan_tables(lr, -li, True)):
            tr_ref[k] = t

    return _pcall(
        body, name=name,
        out_shape=[jax.ShapeDtypeStruct((SSM_GROUP, N_STATES), F32), jax.ShapeDtypeStruct((SSM_GROUP, N_STATES), F32),
                   jax.ShapeDtypeStruct((TAB_ROWS, SUBLANES, N_STATES), F32),
                   jax.ShapeDtypeStruct((TAB_ROWS, SUBLANES, N_STATES), F32)],
    )(ar, ai, ls, br_t, bi_t)


def ssm_param_backward(ar, ai, ls, br_t, bi_t, dlr_p, dli_p, dbbr, dbbi, group_sum, name):
    def body(ar_ref, ai_ref, ls_ref, br_ref, bi_ref, dlr_ref, dli_ref, dbbr_ref, dbbi_ref, gs_ref,
             dar_ref, dai_ref, dls_ref, dbr_ref, dbi_ref):
        ar, ai = ar_ref[...], ai_ref[...]
        step, mag, lr, li, den, nr, ni, cr, ci = _discretise(ar, ai, ls_ref[...])
        br, bi, dbbr_v, dbbi_v = br_ref[...], bi_ref[...], dbbr_ref[...], dbbi_ref[...]
        dbr_ref[...] = cr * dbbr_v + ci * dbbi_v
        dbi_ref[...] = cr * dbbi_v - ci * dbbr_v
        dcr = jnp.sum(dbbr_v * br + dbbi_v * bi, axis=0, keepdims=True)
        dci = jnp.sum(dbbi_v * br - dbbr_v * bi, axis=0, keepdims=True)
        dnr = (dcr * ar - dci * ai) / den
        dni = (dcr * ai + dci * ar) / den
        dden = -(cr * dcr + ci * dci) / den
        dar = (dcr * nr + dci * ni) / den + dden * 2.0 * ar
        dai = (dcr * ni - dci * nr) / den + dden * 2.0 * ai
        dlr = jnp.sum(dlr_ref[...], axis=0, keepdims=True) + dnr
        dli = jnp.sum(dli_ref[...], axis=0, keepdims=True) + dni
        dmag = (dlr * lr + dli * li) / mag
        dang = dli * lr - dlr * li
        dar_ref[...] = dar + dmag * mag * step
        dai_ref[...] = dai + dang * step
        dstep = dmag * mag * ar + dang * ai
        dls_ref[...] = jnp.dot(dstep * step, gs_ref[...], preferred_element_type=F32, precision=lax.Precision.HIGHEST)

    vec = jax.ShapeDtypeStruct((1, N_STATES), F32)
    mat = jax.ShapeDtypeStruct((SSM_GROUP, N_STATES), F32)
    return _pcall(body, name=name, out_shape=[vec, vec, jax.ShapeDtypeStruct((1, LANES), F32), mat, mat])(
        ar, ai, ls, br_t, bi_t, dlr_p, dli_p, dbbr, dbbi, group_sum)


def _scan_rows(a, b, tabs, carry, reverse):
    for k, d in enumerate((1, 2, 4)):
        shift = SUBLANES - d if reverse else d
        sr, si = pltpu.roll(a, shift, axis=0), pltpu.roll(b, shift, axis=0)
        pr, pi = _cmul(tabs[2 * k], tabs[2 * k + 1], sr, si)
        a, b = a + pr, b + pi
    pr, pi = _cmul(tabs[6], tabs[7], carry[0], carry[1])
    return a + pr, b + pi


def _time_groups(seq, reverse):
    meta = [seq + SUBLANES * g for g in range(N_META // SUBLANES)]
    return meta[::-1] if reverse else meta


def ssm_forward_scan(proj3, b_comb, tabf, c_comb, dvec, seq, name):
    n_b, n_l, _ = proj3.shape
    u_blk = (D_MODEL + 2 * KV_WIDTH) // SSM_WIDTH

    def body(u_ref, b_ref, tab_ref, c_ref, d_ref, x_ref, y_ref, bu, xs):
        j = pl.program_id(1)
        u = u_ref[...]
        bu[...] = _dot(u, b_ref[...])
        tabs = [tab_ref[k] for k in range(TAB_ROWS)]

        def group(r0, carry):
            rows = pl.ds(r0, SUBLANES)
            a, b = _scan_rows(bu[rows, :SCAN_COLS], bu[rows, SCAN_COLS:], tabs, carry, False)
            xs[rows, :SCAN_COLS] = a
            xs[rows, SCAN_COLS:] = b
            return (jnp.broadcast_to(a[SUBLANES - 1:, :], a.shape), jnp.broadcast_to(b[SUBLANES - 1:, :], b.shape))

        zero = jnp.zeros((SUBLANES, SCAN_COLS), F32)
        carry = (zero, zero)
        for r0 in _time_groups(seq, False):
            carry = group(r0, carry)
        lax.fori_loop(0, seq // SUBLANES, lambda g, c: group(pl.multiple_of(g * SUBLANES, SUBLANES), c), carry,
                      unroll=SCAN_UNROLL)
        x16 = xs[...].astype(BF16)
        x_ref[...] = x16
        contrib = _dot(x16, c_ref[...])

        @pl.when(j == 0)
        def _():
            y_ref[...] = contrib + d_ref[...] * u.astype(F32)

        @pl.when(j > 0)
        def _():
            y_ref[...] += contrib

    return _pcall(
        body, name=name, grid=(n_b, N_SCAN_BLK),
        in_specs=[pl.BlockSpec((None, n_l, SSM_WIDTH), lambda b, j: (b, 0, u_blk)),
                  pl.BlockSpec((SSM_WIDTH, 2 * SCAN_COLS), lambda b, j: (0, j)),
                  pl.BlockSpec((TAB_ROWS, SUBLANES, SCAN_COLS), lambda b, j: (0, 0, j)),
                  pl.BlockSpec((2 * SCAN_COLS, SSM_WIDTH), lambda b, j: (j, 0)),
                  pl.BlockSpec((1, SSM_WIDTH), lambda b, j: (0, 0))],
        out_specs=[pl.BlockSpec((None, n_l, 2 * SCAN_COLS), lambda b, j: (b, 0, j)),
                   pl.BlockSpec((None, n_l, SSM_WIDTH), lambda b, j: (b, 0, 0))],
        out_shape=[jax.ShapeDtypeStruct((n_b, n_l, 2 * N_STATES), BF16),
                   jax.ShapeDtypeStruct((n_b, n_l, SSM_WIDTH), F32)],
        scratch_shapes=[pltpu.VMEM((n_l, 2 * SCAN_COLS), F32)] * 2,
        compiler_params=_cp(("arbitrary", "arbitrary"), VMEM_BIG),
    )(proj3, b_comb, tabf, c_comb, dvec)


def ssm_backward_scan(dyraw3, xs3, dproj3, c_comb_t, tabr, b_comb_t, dvec, seq, name):
    n_b, n_l, _ = xs3.shape
    u_blk = (D_MODEL + 2 * KV_WIDTH) // SSM_WIDTH

    def body(dy_ref, x_ref, _, c_ref, tab_ref, b_ref, d_ref, du_ref, g_ref, dlr_ref, dli_ref, dx, gs, xs, du_acc):
        j = pl.program_id(1)
        dy = dy_ref[...]
        dx[...] = _dot(dy, c_ref[...])
        xs[...] = x_ref[...].astype(F32)
        tabs = [tab_ref[k] for k in range(TAB_ROWS)]
        last_row = lax.broadcasted_iota(jnp.int32, (SUBLANES, SCAN_COLS), 0) == SUBLANES - 1

        def group(r0, state):
            cr, ci, acc_r, acc_i = state
            rows = pl.ds(r0, SUBLANES)
            a, b = _scan_rows(dx[rows, :SCAN_COLS], dx[rows, SCAN_COLS:], tabs, (cr, ci), True)
            gs[rows, :SCAN_COLS] = a
            gs[rows, SCAN_COLS:] = b
            na = jnp.where(last_row, cr, pltpu.roll(a, SUBLANES - 1, axis=0))
            nb = jnp.where(last_row, ci, pltpu.roll(b, SUBLANES - 1, axis=0))
            xa, xb = xs[rows, :SCAN_COLS], xs[rows, SCAN_COLS:]
            return (jnp.broadcast_to(a[:1, :], a.shape), jnp.broadcast_to(b[:1, :], b.shape),
                    acc_r + na * xa + nb * xb, acc_i + nb * xa - na * xb)

        zero = jnp.zeros((SUBLANES, SCAN_COLS), F32)
        n_groups = seq // SUBLANES
        state = lax.fori_loop(
            0, n_groups, lambda g, s: group(pl.multiple_of((n_groups - 1 - g) * SUBLANES, SUBLANES), s),
            (zero, zero, zero, zero), unroll=SCAN_UNROLL)
        for r0 in _time_groups(seq, True):
            state = group(r0, state)
        dlr_ref[...] = state[2]
        dli_ref[...] = state[3]
        g16 = gs[...].astype(BF16)
        g_ref[...] = g16
        contrib = _dot(g16, b_ref[...])

        @pl.when(j == 0)
        def _():
            du_acc[...] = contrib + d_ref[...] * dy.astype(F32)

        @pl.when(j > 0)
        def _():
            du_acc[...] += contrib

        @pl.when(j == N_SCAN_BLK - 1)
        def _():
            du_ref[...] = du_acc[...].astype(BF16)

    state_blk = pl.BlockSpec((None, n_l, 2 * SCAN_COLS), lambda b, j: (b, 0, j))
    dl_blk = pl.BlockSpec((None, SUBLANES, SCAN_COLS), lambda b, j: (b, 0, j))
    return _pcall(
        body, name=name, grid=(n_b, N_SCAN_BLK),
        in_specs=[pl.BlockSpec((None, n_l, SSM_WIDTH), lambda b, j: (b, 0, 0)), state_blk,
                  pl.BlockSpec(memory_space=pl.ANY),
                  pl.BlockSpec((SSM_WIDTH, 2 * SCAN_COLS), lambda b, j: (0, j)),
                  pl.BlockSpec((TAB_ROWS, SUBLANES, SCAN_COLS), lambda b, j: (0, 0, j)),
                  pl.BlockSpec((2 * SCAN_COLS, SSM_WIDTH), lambda b, j: (j, 0)),
                  pl.BlockSpec((1, SSM_WIDTH), lambda b, j: (0, 0))],
        out_specs=[pl.BlockSpec((None, n_l, SSM_WIDTH), lambda b, j: (b, 0, u_blk)), state_blk, dl_blk, dl_blk],
        out_shape=[jax.ShapeDtypeStruct(dproj3.shape, BF16), jax.ShapeDtypeStruct((n_b, n_l, 2 * N_STATES), BF16),
                   jax.ShapeDtypeStruct((n_b, SUBLANES, N_STATES), F32), jax.ShapeDtypeStruct((n_b, SUBLANES, N_STATES), F32)],
        scratch_shapes=[pltpu.VMEM((n_l, 2 * SCAN_COLS), F32)] * 3 + [pltpu.VMEM((n_l, SSM_WIDTH), F32)],
        input_output_aliases={2: 0},
        compiler_params=_cp(("arbitrary", "arbitrary"), VMEM_BIG),
    )(dyraw3, xs3, dproj3, c_comb_t, tabr, b_comb_t, dvec)


def ssm_param_grads(proj, gs, xs, dyraw, tm, name):
    t_rows = proj.shape[0]
    ni = t_rows // tm
    n_cb = 2 * N_STATES // SSM_WIDTH
    u_blk = (D_MODEL + 2 * KV_WIDTH) // SSM_WIDTH

    def body(u_ref, g_ref, x_ref, dy_ref, db_ref, dc_ref, dd_ref, acc_b, acc_c):
        cb, i = pl.program_id(0), pl.program_id(1)

        @pl.when(i == 0)
        def _():
            acc_b[...] = jnp.zeros_like(acc_b)
            acc_c[...] = jnp.zeros_like(acc_c)

        u, dy = u_ref[...], dy_ref[...]
        acc_b[...] += _dot_tn(u, g_ref[...])
        acc_c[...] += _dot_tn(x_ref[...], dy)

        @pl.when(i == ni - 1)
        def _():
            db_ref[...] = acc_b[...]
            dc_ref[...] = acc_c[...]

        @pl.when(cb == 0)
        def _():
            _accumulate(dd_ref, jnp.sum(dy.astype(F32) * u.astype(F32), axis=0, keepdims=True), i == 0)

    sq = (SSM_WIDTH, SSM_WIDTH)
    return _pcall(
        body, name=name, grid=(n_cb, ni),
        in_specs=[pl.BlockSpec((tm, SSM_WIDTH), lambda cb, i: (i, u_blk)),
                  pl.BlockSpec((tm, SSM_WIDTH), lambda cb, i: (i, cb)),
                  pl.BlockSpec((tm, SSM_WIDTH), lambda cb, i: (i, cb)),
                  pl.BlockSpec((tm, SSM_WIDTH), lambda cb, i: (i, 0))],
        out_specs=[pl.BlockSpec(sq, lambda cb, i: (0, cb)), pl.BlockSpec(sq, lambda cb, i: (cb, 0)),
                   pl.BlockSpec((1, SSM_WIDTH), lambda cb, i: (0, 0))],
        out_shape=[jax.ShapeDtypeStruct((SSM_WIDTH, 2 * N_STATES), F32), jax.ShapeDtypeStruct((2 * N_STATES, SSM_WIDTH), F32),
                   jax.ShapeDtypeStruct((1, SSM_WIDTH), F32)],
        scratch_shapes=[pltpu.VMEM(sq, F32), pltpu.VMEM(sq, F32)],
        compiler_params=_cp(("arbitrary", "arbitrary"), VMEM_BIG),
    )(proj, gs, xs, dyraw)


def sum_leading(x, name):
    def body(x_ref, o_ref):
        acc = x_ref[0]
        for k in range(1, x.shape[0]):
            acc = acc + x_ref[k]
        o_ref[...] = acc

    return _pcall(body, name=name, out_shape=jax.ShapeDtypeStruct(x.shape[1:], x.dtype))(x)


WEIGHTS = ['meta_tokens', 'ffn1_norm', 'ffn1_w1', 'ffn1_w3', 'ffn1_w2', 'mix_norm', 'w_in', 'attn_sinks', 'ssm_a_re',
           'ssm_a_im', 'ssm_log_step', 'ssm_b_re', 'ssm_b_im', 'ssm_c_re', 'ssm_c_im', 'ssm_d', 'ssm_glu_a', 'ssm_glu_b',
           'w_out', 'ffn2_norm', 'ffn2_w1', 'ffn2_w3', 'ffn2_w2', 'final_norm']
SHARDED = ['ffn1_w1', 'ffn1_w3', 'ffn1_w2', 'ffn2_w1', 'ffn2_w3', 'ffn2_w2', 'w_in', 'ssm_glu_a', 'ssm_glu_b', 'w_out']
REPLICATED = ['ffn1_norm', 'mix_norm', 'ffn2_norm', 'final_norm', 'attn_sinks', 'ssm_a_re', 'ssm_a_im', 'ssm_log_step',
              'ssm_b_re', 'ssm_b_im', 'ssm_c_re', 'ssm_c_im', 'ssm_d']
PACK_COLS = 1024


def _block_diag(blocks):
    g, r, c = blocks.shape
    eye = jnp.eye(g, dtype=blocks.dtype)
    return (blocks[:, :, None, :] * eye[:, None, :, None]).reshape(g * r, g * c)


def _diag_blocks(mat, r, c):
    g = SSM_GROUPS
    eye = jnp.eye(g, dtype=mat.dtype)
    return jnp.sum(mat.reshape(g, r, g, c) * eye[:, None, :, None], axis=2)


def _scan_order(re, im):
    r = re.shape[0]
    return jnp.stack([re.reshape(r, N_SCAN_BLK, SCAN_COLS), im.reshape(r, N_SCAN_BLK, SCAN_COLS)], axis=2).reshape(r, 2 * N_STATES)


def _from_scan_order(comb):
    r = comb.shape[0]
    c4 = comb.reshape(r, N_SCAN_BLK, 2, SCAN_COLS)
    return c4[:, :, 0].reshape(r, N_STATES), c4[:, :, 1].reshape(r, N_STATES)


def _pack(arrays):
    parts = []
    for a in arrays:
        flat = a.reshape(-1)
        chunk = SUBLANES * PACK_COLS
        padded = -(-flat.shape[0] // chunk) * chunk
        parts.append(jnp.pad(flat, (0, padded - flat.shape[0])).reshape(-1, PACK_COLS))
    return jnp.concatenate(parts, axis=0)


def _unpack(packed, shapes):
    out, row = [], 0
    for shape in shapes:
        size = 1
        for s in shape:
            size *= s
        chunk = SUBLANES * PACK_COLS
        rows = -(-size // chunk) * SUBLANES
        out.append(packed[row:row + rows].reshape(-1)[:size].reshape(shape))
        row += rows
    return out


def kernel(x, meta_tokens, ffn1_norm, ffn1_w1, ffn1_w3, ffn1_w2, mix_norm, w_in, attn_sinks, ssm_a_re, ssm_a_im, ssm_log_step, ssm_b_re, ssm_b_im, ssm_c_re, ssm_c_im, ssm_d, ssm_glu_a, ssm_glu_b, w_out, ffn2_norm, ffn2_w1, ffn2_w3, ffn2_w2, final_norm, loss_target, m_meta_tokens, m_ffn1_norm, m_ffn1_w1, m_ffn1_w3, m_ffn1_w2, m_mix_norm, m_w_in, m_attn_sinks, m_ssm_a_re, m_ssm_a_im, m_ssm_log_step, m_ssm_b_re, m_ssm_b_im, m_ssm_c_re, m_ssm_c_im, m_ssm_d, m_ssm_glu_a, m_ssm_glu_b, m_w_out, m_ffn2_norm, m_ffn2_w1, m_ffn2_w3, m_ffn2_w2, m_final_norm, v_meta_tokens, v_ffn1_norm, v_ffn1_w1, v_ffn1_w3, v_ffn1_w2, v_mix_norm, v_w_in, v_attn_sinks, v_ssm_a_re, v_ssm_a_im, v_ssm_log_step, v_ssm_b_re, v_ssm_b_im, v_ssm_c_re, v_ssm_c_im, v_ssm_d, v_ssm_glu_a, v_ssm_glu_b, v_w_out, v_ffn2_norm, v_ffn2_w1, v_ffn2_w3, v_ffn2_w2, v_final_norm):
    given = dict(locals())
    w = {n: given[n] for n in WEIGHTS}
    m = {n: given["m_" + n] for n in WEIGHTS}
    v = {n: given["v_" + n] for n in WEIGHTS}

    n_b, seq, _ = x.shape
    n_l = seq + N_META
    t_rows = n_b * n_l
    tm = _row_tile(n_l, 688)
    px, py, pc = _my_place()
    me = 4 * px + 2 * py + pc
    dest = jnp.stack([4 * qx + 2 * qy + pc for qx, qy in
                      [(px, py), (1 - px, py), (px, 1 - py), (1 - px, 1 - py)]]).astype(jnp.int32)

    glu = jnp.stack([ssm_glu_a[0], ssm_glu_b[0]]).astype(BF16)
    ffn_names = ['ffn1_w1', 'ffn1_w3', 'ffn1_w2', 'ffn2_w1', 'ffn2_w3', 'ffn2_w2']
    gathered = all_gather_list(
        [w[n][0].astype(BF16) for n in ffn_names] + [w_in[0].astype(BF16), glu, w_out[0].astype(BF16), meta_tokens],
        "ag_weights")
    wing, glug, wog, metag = gathered[len(ffn_names):]
    full = {}
    for n, g in zip(ffn_names, gathered):
        if n.endswith('w2'):
            full[n] = g.reshape(D_FF, D_MODEL)
        else:
            full[n] = g.transpose(1, 0, 2).reshape(D_MODEL, D_FF)
    glu_a = glug[:, 0].transpose(1, 0, 2).reshape(SSM_WIDTH, D_MODEL)
    glu_b = glug[:, 1].transpose(1, 0, 2).reshape(SSM_WIDTH, D_MODEL)
    w_out_full = wog.reshape(D_MODEL, D_MODEL)
    meta_full = metag.transpose(1, 0, 2).reshape(N_META, D_MODEL)

    h0 = jnp.concatenate([x, jnp.broadcast_to(meta_full[None], (n_b, N_META, D_MODEL))], axis=1).reshape(t_rows, D_MODEL)
    target = jnp.concatenate([loss_target, jnp.zeros((n_b, N_META, D_MODEL), F32)], axis=1).reshape(t_rows, D_MODEL)
    final_g = final_norm.reshape(1, D_MODEL)

    ar = ssm_a_re.reshape(1, N_STATES)
    ai = ssm_a_im.reshape(1, N_STATES)
    ls = jnp.repeat(ssm_log_step.reshape(SSM_GROUPS), SSM_STATE).reshape(1, N_STATES)
    br_t = ssm_b_re[0].transpose(2, 0, 1).reshape(SSM_GROUP, N_STATES)
    bi_t = ssm_b_im[0].transpose(2, 0, 1).reshape(SSM_GROUP, N_STATES)
    bbr, bbi, tabf, tabr = ssm_prepare(ar, ai, ls, br_t, bi_t, "ssm_prepare")
    bbr_g = bbr.reshape(SSM_GROUP, SSM_GROUPS, SSM_STATE).transpose(1, 0, 2)
    bbi_g = bbi.reshape(SSM_GROUP, SSM_GROUPS, SSM_STATE).transpose(1, 0, 2)
    b_comb = _scan_order(_block_diag(bbr_g), _block_diag(bbi_g)).astype(BF16)
    c_comb_t = _scan_order(_block_diag(ssm_c_re[0]), -_block_diag(ssm_c_im[0])).astype(BF16)
    b_comb_t, c_comb = b_comb.T, c_comb_t.T

    ffn1_w = (full['ffn1_w1'], full['ffn1_w3'], full['ffn1_w2'])
    ffn2_w = (full['ffn2_w1'], full['ffn2_w3'], full['ffn2_w2'])
    h1, hn1, a1, b1 = ffn_forward(h0, ffn1_norm, *ffn1_w, tm, FF_FWD_COLS, "ffn1_fwd")
    hnm, proj = mix_forward(h1, mix_norm, wing, tm, "mix_fwd")
    proj3 = proj.reshape(n_b, n_l, IN_WIDTH)
    attn3 = attention_forward(proj3, attn_sinks, seq, "attn_fwd")
    attn = attn3.reshape(t_rows, D_MODEL)
    xs3, yraw3 = ssm_forward_scan(proj3, b_comb, tabf, c_comb, ssm_d, seq, "ssm_fwd")
    yraw = yraw3.reshape(t_rows, SSM_WIDTH)
    h2 = merge_forward(h1, yraw, attn, proj, glu_a, glu_b, w_out_full, tm, "merge_fwd")
    h3, hn2, a2, b2 = ffn_forward(h2, ffn2_norm, *ffn2_w, tm, FF_FWD_COLS, "ffn2_fwd")
    dh3, loss_part, g_final = final_loss_backward(h3, target, final_g, seq, tm, "loss_bwd")
    loss = lax.psum(loss_part[0, 0], ("x", "y", "c"))

    def blocked_ffn(d_w1, d_w3, d_w2):
        by_cols = lambda t: t.reshape(D_MODEL, N_DEV, FF_BLK).transpose(1, 0, 2).astype(BF16)
        return by_cols(d_w1), by_cols(d_w3), d_w2.reshape(N_DEV, FF_BLK, D_MODEL).astype(BF16)

    da2, db2 = ffn_backward_hidden(dh3, a2, b2, ffn2_w[2], tm, "ffn2_bwd_hid")
    dh2, g_ffn2_norm = ffn_backward_input(dh3, h2, ffn2_norm, da2, db2, ffn2_w[0], ffn2_w[1], tm, "ffn2_bwd_in")
    dw = {}
    dw['ffn2_w1'], dw['ffn2_w3'], dw['ffn2_w2'] = blocked_ffn(
        *ffn_backward_weights(hn2, dh3, a2, b2, da2, db2, tm, FF_BWD_COLS, "ffn2_bwd_w"))
    dattn, dyraw, dproj, d_wo, d_ga, d_gb = merge_backward(dh2, yraw, attn, proj, glu_a, glu_b, w_out_full,
                                                           _row_tile(t_rows, 256), "merge_bwd")
    dproj3 = dproj.reshape(n_b, n_l, IN_WIDTH)
    dproj3, dsink_p = attention_backward(proj3, dattn.reshape(n_b, n_l, D_MODEL), dproj3, attn_sinks, seq, "attn_bwd")
    dproj3, gs3, dlr_p, dli_p = ssm_backward_scan(
        dyraw.reshape(n_b, n_l, SSM_WIDTH), xs3, dproj3, c_comb_t, tabr, b_comb_t, ssm_d, seq, "ssm_bwd")
    dproj = dproj3.reshape(t_rows, IN_WIDTH)
    d_bd, d_cd, g_d = ssm_param_grads(proj, gs3.reshape(t_rows, 2 * N_STATES), xs3.reshape(t_rows, 2 * N_STATES),
                                      dyraw, tm, "ssm_bwd_w")
    dh1, g_mix_norm = mix_backward_act(dh2, h1, mix_norm, dproj, wing, tm, "mix_bwd_act")
    dw['w_in'] = mix_backward_weights(hnm, dproj, tm, "mix_bwd_w")
    da1, db1 = ffn_backward_hidden(dh1, a1, b1, ffn1_w[2], tm, "ffn1_bwd_hid")
    dh0, g_ffn1_norm = ffn_backward_input(dh1, h0, ffn1_norm, da1, db1, ffn1_w[0], ffn1_w[1], tm, "ffn1_bwd_in")
    dw['ffn1_w1'], dw['ffn1_w3'], dw['ffn1_w2'] = blocked_ffn(
        *ffn_backward_weights(hn1, dh1, a1, b1, da1, db1, tm, FF_BWD_COLS, "ffn1_bwd_w"))
    dh0_3 = dh0.reshape(n_b, n_l, D_MODEL)
    grad_x = dh0_3[:, :seq]
    g_meta = sum_leading(dh0_3[:, seq:], "meta_sum")

    def blocked_cols(full):
        r = full.shape[0]
        return full.reshape(r, N_DEV, full.shape[1] // N_DEV).transpose(1, 0, 2).astype(BF16)

    dw['ssm_glu_a'] = blocked_cols(d_ga)
    dw['ssm_glu_b'] = blocked_cols(d_gb)
    dw['w_out'] = d_wo.reshape(N_DEV, D_MODEL // N_DEV, D_MODEL).astype(BF16)

    d_b_re, d_b_im = _from_scan_order(d_bd)
    d_c_re, d_c_im = _from_scan_order(d_cd.T)
    dbbr = _diag_blocks(d_b_re, SSM_GROUP, SSM_STATE).transpose(1, 0, 2).reshape(SSM_GROUP, N_STATES)
    dbbi = _diag_blocks(d_b_im, SSM_GROUP, SSM_STATE).transpose(1, 0, 2).reshape(SSM_GROUP, N_STATES)
    g_c_re = _diag_blocks(d_c_re, SSM_GROUP, SSM_STATE)[None]
    g_c_im = -_diag_blocks(d_c_im, SSM_GROUP, SSM_STATE)[None]
    group_sum = (jnp.arange(N_STATES)[:, None] // SSM_STATE == jnp.arange(LANES)[None, :]).astype(F32)
    g_ar, g_ai, g_ls, g_br, g_bi = ssm_param_backward(
        ar, ai, ls, br_t, bi_t, dlr_p.reshape(n_b * SUBLANES, N_STATES), dli_p.reshape(n_b * SUBLANES, N_STATES),
        dbbr, dbbi, group_sum, "ssm_bwd_params")
    g_sinks = sum_leading(dsink_p, "sink_sum")[0:1, :N_KV_HEADS * Q_PER_KV]

    small = {
        'ffn1_norm': g_ffn1_norm, 'mix_norm': g_mix_norm, 'ffn2_norm': g_ffn2_norm, 'final_norm': g_final.reshape(D_MODEL),
        'attn_sinks': g_sinks, 'ssm_a_re': g_ar.reshape(1, SSM_GROUPS, SSM_STATE), 'ssm_a_im': g_ai.reshape(1, SSM_GROUPS, SSM_STATE),
        'ssm_log_step': g_ls[:, :SSM_GROUPS],
        'ssm_b_re': g_br.reshape(SSM_GROUP, SSM_GROUPS, SSM_STATE).transpose(1, 2, 0)[None],
        'ssm_b_im': g_bi.reshape(SSM_GROUP, SSM_GROUPS, SSM_STATE).transpose(1, 2, 0)[None],
        'ssm_c_re': g_c_re, 'ssm_c_im': g_c_im, 'ssm_d': g_d,
    }

    zeros_meta = jnp.zeros((N_META, D_MODEL), F32)
    packed_g = _pack([small[n] for n in REPLICATED] + [g_meta])
    (parts,) = all_gather_list([packed_g], "ag_small_grads")
    packed_out = adamw_small(parts, _pack([w[n] for n in REPLICATED] + [zeros_meta]),
                             _pack([m[n] for n in REPLICATED] + [zeros_meta]),
                             _pack([v[n] for n in REPLICATED] + [zeros_meta]), "adamw_small")
    shapes = [w[n].shape for n in REPLICATED] + [(N_META, D_MODEL)]
    grads, deltas, new_m, new_v = {}, {}, {}, {}
    unpacked = [_unpack(p, shapes) for p in packed_out]
    for k, n in enumerate(REPLICATED):
        grads[n], deltas[n], new_m[n], new_v[n] = (u[k] for u in unpacked)
    g_meta_full = unpacked[0][-1]
    grads['meta_tokens'] = lax.dynamic_index_in_dim(
        g_meta_full.reshape(N_META, N_DEV, D_MODEL // N_DEV), me, axis=1, keepdims=False)
    deltas['meta_tokens'], new_m['meta_tokens'], new_v['meta_tokens'] = adamw_plain(
        grads['meta_tokens'], w['meta_tokens'], m['meta_tokens'], v['meta_tokens'], "adamw_meta")

    g_list = [dw[n] for n in SHARDED]
    r1 = rs_sibling_swap(g_list, "rs_sibling")
    pairs = [pair_sums(dest, g, r, "rs_pair_" + n) for n, g, r in zip(SHARDED, g_list, r1)]
    r2 = rs_chip_exchange(pairs, "rs_chips")
    for n, g, ra, rb in zip(SHARDED, g_list, r1, r2):
        shape = w[n].shape
        two_d = lambda t: t.reshape(shape[1:])
        out = adamw_sharded(dest, g, ra, rb, two_d(w[n]), two_d(m[n]), two_d(v[n]), "adamw_" + n)
        grads[n], deltas[n], new_m[n], new_v[n] = (o.reshape(shape) for o in out)

    return (loss, grad_x, *[grads[n] for n in WEIGHTS], *[deltas[n] for n in WEIGHTS],
            *[new_m[n] for n in WEIGHTS], *[new_v[n] for n in WEIGHTS])
```

```python
import functools

import jax
import jax.numpy as jnp
from jax import lax
from jax.experimental import pallas as pl
from jax.experimental.pallas import tpu as pltpu

F32 = jnp.float32
BF16 = jnp.bfloat16
MESH = pl.DeviceIdType.MESH

N_DEV = 8
D_MODEL = 1024
N_META = 16
HEAD_DIM = 64
N_KV_HEADS = 4
Q_PER_KV = 4
BLOCK = 128
KV_WIDTH = N_KV_HEADS * HEAD_DIM
SSM_GROUP = 16
SSM_WIDTH = 512
SSM_GROUPS = 32
SSM_STATE = 64
N_STATES = SSM_GROUPS * SSM_STATE
D_FF = 2816
FF_BLK = D_FF // N_DEV
IN_WIDTH = 4096
IN_BLK = IN_WIDTH // N_DEV
NORM_EPS = 1e-6
NEG_INF = -1e30
SCAN_COLS = 256
N_SCAN_BLK = N_STATES // SCAN_COLS
SUBLANES = 8
LANES = 128
MXU_WIDTH = 256
FF_FWD_COLS = D_FF // 2
FF_BWD_COLS = MXU_WIDTH

ADAM_LR = 0.001
ADAM_B1 = 0.9
ADAM_B2 = 0.999
ADAM_EPS = 1e-08
ADAM_WD = 0.01
ADAM_STEP = 10

VMEM_BIG = 56 * 1024 * 1024


def _cp(sem=None, vmem=None):
    kw = {}
    if sem is not None:
        kw["dimension_semantics"] = sem
    if vmem is not None:
        kw["vmem_limit_bytes"] = vmem
    return pltpu.CompilerParams(**kw)


def _pcall(body, **kw):
    return pl.pallas_call(body, **kw)


def _dot(a, b):
    return jnp.dot(a, b, preferred_element_type=F32)


def _dot_nt(a, b):
    return lax.dot_general(a, b, (((1,), (1,)), ((), ())), preferred_element_type=F32)


def _dot_tn(a, b):
    return lax.dot_general(a, b, (((0,), (0,)), ((), ())), preferred_element_type=F32)


def _sigmoid(x):
    return 1.0 / (1.0 + jnp.exp(-x))


def _row_tile(rows, cap):
    best = None
    for t in range(16, min(rows, cap) + 1, 16):
        if rows % t == 0:
            best = t
    assert best is not None, rows
    return best


def _my_place():
    return lax.axis_index("x"), lax.axis_index("y"), lax.axis_index("c")


def all_gather_list(shards, name):
    n = len(shards)

    def body(*refs):
        ins, outs = refs[:n], refs[n:2 * n]
        send_sems, recv_sems, local_sems = refs[2 * n:]
        x, y, c = _my_place()
        me, sibling = (x, y, c), (x, y, 1 - c)
        chips = [(1 - x, y), (x, 1 - y), (1 - x, 1 - y)]

        def blk(a, px, py, pc):
            return outs[a].at[4 * px + 2 * py + pc]

        def copy(a, k, block, to, src=None):
            return pltpu.make_async_remote_copy(
                src_ref=blk(a, *block) if src is None else src, dst_ref=blk(a, *block),
                send_sem=send_sems.at[a * 7 + k], recv_sem=recv_sems.at[a * 7 + k],
                device_id=to, device_id_type=MESH)

        mine = [pltpu.make_async_copy(ins[a], blk(a, *me), local_sems.at[a]) for a in range(n)]
        for cp in mine:
            cp.start()
        first = []
        for a in range(n):
            first.append(copy(a, 0, me, sibling, src=ins[a]))
            first += [copy(a, 1 + j, me, (*chip, c), src=ins[a]) for j, chip in enumerate(chips)]
        for cp in first:
            cp.start()
        passed = []
        for j, chip in enumerate(chips):
            for a in range(n):
                copy(a, 1 + j, (*chip, c), me).wait_recv()
                cp = copy(a, 4 + j, (*chip, c), sibling)
                cp.start()
                passed.append(cp)
        for a in range(n):
            copy(a, 0, sibling, me).wait_recv()
            for j, chip in enumerate(chips):
                copy(a, 4 + j, (*chip, 1 - c), me).wait_recv()
        for cp in first + passed:
            cp.wait_send()
        for cp in mine:
            cp.wait()

    any_spec = pl.BlockSpec(memory_space=pl.ANY)
    return _pcall(
        body, name=name,
        out_shape=[jax.ShapeDtypeStruct((N_DEV,) + s.shape, s.dtype) for s in shards],
        in_specs=[any_spec] * n, out_specs=[any_spec] * n,
        scratch_shapes=[pltpu.SemaphoreType.DMA((7 * n,)), pltpu.SemaphoreType.DMA((7 * n,)),
                        pltpu.SemaphoreType.DMA((n,))],
    )(*shards)


def rs_sibling_swap(grads, name):
    n = len(grads)

    def body(*refs):
        ins, outs = refs[:n], refs[n:2 * n]
        send_sems, recv_sems = refs[2 * n:]
        x, y, c = _my_place()
        chips = [(x, y), (1 - x, y), (x, 1 - y), (1 - x, 1 - y)]
        copies = []
        for a in range(n):
            for k, (px, py) in enumerate(chips):
                copies.append(pltpu.make_async_remote_copy(
                    src_ref=ins[a].at[4 * px + 2 * py + (1 - c)], dst_ref=outs[a].at[k],
                    send_sem=send_sems.at[4 * a + k], recv_sem=recv_sems.at[4 * a + k],
                    device_id=(x, y, 1 - c), device_id_type=MESH))
        for cp in copies:
            cp.start()
        for cp in copies:
            cp.wait()

    any_spec = pl.BlockSpec(memory_space=pl.ANY)
    return _pcall(
        body, name=name,
        out_shape=[jax.ShapeDtypeStruct((4,) + g.shape[1:], g.dtype) for g in grads],
        in_specs=[any_spec] * n, out_specs=[any_spec] * n,
        scratch_shapes=[pltpu.SemaphoreType.DMA((4 * n,)), pltpu.SemaphoreType.DMA((4 * n,))],
    )(*grads)


def rs_chip_exchange(parts, name):
    n = len(parts)

    def body(*refs):
        ins, outs = refs[:n], refs[n:2 * n]
        send_sems, recv_sems = refs[2 * n:]
        x, y, c = _my_place()
        chips = [(1 - x, y), (x, 1 - y), (1 - x, 1 - y)]
        copies = []
        for a in range(n):
            for k, (px, py) in enumerate(chips):
                copies.append(pltpu.make_async_remote_copy(
                    src_ref=ins[a].at[k], dst_ref=outs[a].at[k],
                    send_sem=send_sems.at[3 * a + k], recv_sem=recv_sems.at[3 * a + k],
                    device_id=(px, py, c), device_id_type=MESH))
        for cp in copies:
            cp.start()
        for cp in copies:
            cp.wait()

    any_spec = pl.BlockSpec(memory_space=pl.ANY)
    return _pcall(
        body, name=name,
        out_shape=[jax.ShapeDtypeStruct(p.shape, p.dtype) for p in parts],
        in_specs=[any_spec] * n, out_specs=[any_spec] * n,
        scratch_shapes=[pltpu.SemaphoreType.DMA((3 * n,)), pltpu.SemaphoreType.DMA((3 * n,))],
    )(*parts)


def pair_sums(idx, g, r1, name):
    _, rows, cols = g.shape
    tr = _row_tile(rows, 256)

    def body(idx_ref, g_ref, r_ref, o_ref):
        o_ref[...] = (g_ref[...].astype(F32) + r_ref[...].astype(F32)).astype(BF16)

    return _pcall(
        body, name=name,
        out_shape=jax.ShapeDtypeStruct((3, rows, cols), BF16),
        grid_spec=pltpu.PrefetchScalarGridSpec(
            num_scalar_prefetch=1, grid=(3, rows // tr),
            in_specs=[pl.BlockSpec((None, tr, cols), lambda k, r, ix: (ix[k + 1], r, 0)),
                      pl.BlockSpec((None, tr, cols), lambda k, r, ix: (k + 1, r, 0))],
            out_specs=pl.BlockSpec((None, tr, cols), lambda k, r, ix: (k, r, 0))),
        compiler_params=_cp(("arbitrary", "arbitrary")),
    )(idx, g, r1)


def _adam_math(w, g, m, v):
    m = ADAM_B1 * m + (1.0 - ADAM_B1) * g
    v = ADAM_B2 * v + (1.0 - ADAM_B2) * (g * g)
    m_hat = m / (1.0 - ADAM_B1 ** ADAM_STEP)
    v_hat = v / (1.0 - ADAM_B2 ** ADAM_STEP)
    delta = -ADAM_LR * (m_hat / (jnp.sqrt(v_hat) + ADAM_EPS) + ADAM_WD * w)
    return delta, m, v


def adamw_sharded(idx, g, r1, r2, w, m, v, name):
    rows, cols = w.shape
    tr = _row_tile(rows, 256)

    def body(idx_ref, g_ref, r1_ref, r2_ref, w_ref, m_ref, v_ref, go_ref, d_ref, mo_ref, vo_ref):
        grad = g_ref[...].astype(F32) + r1_ref[...].astype(F32)
        for k in range(3):
            grad = grad + r2_ref[k].astype(F32)
        delta, m_new, v_new = _adam_math(w_ref[...], grad, m_ref[...], v_ref[...])
        go_ref[...] = grad
        d_ref[...] = delta
        mo_ref[...] = m_new
        vo_ref[...] = v_new

    tile = pl.BlockSpec((tr, cols), lambda r, ix: (r, 0))
    out = jax.ShapeDtypeStruct((rows, cols), F32)
    return _pcall(
        body, name=name, out_shape=[out] * 4,
        grid_spec=pltpu.PrefetchScalarGridSpec(
            num_scalar_prefetch=1, grid=(rows // tr,),
            in_specs=[pl.BlockSpec((None, tr, cols), lambda r, ix: (ix[0], r, 0)),
                      pl.BlockSpec((None, tr, cols), lambda r, ix: (0, r, 0)),
                      pl.BlockSpec((3, tr, cols), lambda r, ix: (0, r, 0)),
                      tile, tile, tile],
            out_specs=[tile] * 4),
        compiler_params=_cp(("arbitrary",)),
    )(idx, g, r1, r2, w, m, v)


def adamw_small(parts, w, m, v, name):
    _, rows, cols = parts.shape

    def body(p_ref, w_ref, m_ref, v_ref, go_ref, d_ref, mo_ref, vo_ref):
        grad = p_ref[0]
        for k in range(1, N_DEV):
            grad = grad + p_ref[k]
        delta, m_new, v_new = _adam_math(w_ref[...], grad, m_ref[...], v_ref[...])
        go_ref[...] = grad
        d_ref[...] = delta
        mo_ref[...] = m_new
        vo_ref[...] = v_new

    out = jax.ShapeDtypeStruct((rows, cols), F32)
    return _pcall(body, name=name, out_shape=[out] * 4, compiler_params=_cp(vmem=VMEM_BIG))(parts, w, m, v)


def adamw_plain(g, w, m, v, name):
    def body(g_ref, w_ref, m_ref, v_ref, d_ref, mo_ref, vo_ref):
        delta, m_new, v_new = _adam_math(w_ref[...], g_ref[...], m_ref[...], v_ref[...])
        d_ref[...] = delta
        mo_ref[...] = m_new
        vo_ref[...] = v_new

    out = jax.ShapeDtypeStruct(w.shape, F32)
    return _pcall(body, name=name, out_shape=[out] * 3)(g, w, m, v)


def _rms_fwd(x, g):
    r = lax.rsqrt(jnp.mean(x * x, axis=-1, keepdims=True) + NORM_EPS)
    return x * r * g


def _rms_bwd(x, g, dy):
    r = lax.rsqrt(jnp.mean(x * x, axis=-1, keepdims=True) + NORM_EPS)
    xh = x * r
    t = dy * g
    dx = r * (t - xh * jnp.mean(t * xh, axis=-1, keepdims=True))
    return dx, jnp.sum(dy * xh, axis=0, keepdims=True)


def _accumulate(ref, val, first):
    @pl.when(first)
    def _():
        ref[...] = val

    @pl.when(jnp.logical_not(first))
    def _():
        ref[...] += val


def _col_chunks(width):
    return [(c0, min(MXU_WIDTH, width - c0)) for c0 in range(0, width, MXU_WIDTH)]


def ffn_forward(h, norm, w1, w3, w2, tm, tn, name):
    t_rows = h.shape[0]
    nj = D_FF // tn

    def body(h_ref, g_ref, w1_ref, w3_ref, w2_ref, out_ref, hn_ref, a_ref, b_ref, acc_ref):
        j = pl.program_id(1)

        @pl.when(j == 0)
        def _():
            hn_ref[...] = _rms_fwd(h_ref[...], g_ref[...]).astype(BF16)
            acc_ref[...] = jnp.zeros_like(acc_ref)

        hn = hn_ref[...]
        for c0, cw in _col_chunks(tn):
            a = _dot(hn, w1_ref[:, c0:c0 + cw])
            b = _dot(hn, w3_ref[:, c0:c0 + cw])
            a_ref[:, c0:c0 + cw] = a.astype(BF16)
            b_ref[:, c0:c0 + cw] = b.astype(BF16)
            hid = (a * _sigmoid(a) * b).astype(BF16)
            acc_ref[...] += _dot(hid, w2_ref[c0:c0 + cw, :])

        @pl.when(j == nj - 1)
        def _():
            out_ref[...] = h_ref[...] + 0.5 * acc_ref[...]

    row = pl.BlockSpec((tm, D_MODEL), lambda i, j: (i, 0))
    hid_blk = pl.BlockSpec((tm, tn), lambda i, j: (i, j))
    w_col = pl.BlockSpec((D_MODEL, tn), lambda i, j: (0, j))
    return _pcall(
        body, name=name, grid=(t_rows // tm, nj),
        in_specs=[row, pl.BlockSpec((1, D_MODEL), lambda i, j: (0, 0)), w_col, w_col,
                  pl.BlockSpec((tn, D_MODEL), lambda i, j: (j, 0))],
        out_specs=[row, row, hid_blk, hid_blk],
        out_shape=[jax.ShapeDtypeStruct((t_rows, D_MODEL), F32), jax.ShapeDtypeStruct((t_rows, D_MODEL), BF16),
                   jax.ShapeDtypeStruct((t_rows, D_FF), BF16), jax.ShapeDtypeStruct((t_rows, D_FF), BF16)],
        scratch_shapes=[pltpu.VMEM((tm, D_MODEL), F32)],
        compiler_params=_cp(("arbitrary", "arbitrary"), VMEM_BIG),
    )(h, norm, w1, w3, w2)


def _resident(shape):
    return pl.BlockSpec(shape, lambda *_: (0,) * len(shape), pipeline_mode=pl.Buffered(1))


def ffn_backward_hidden(dh, a, b, w2, tm, name):
    t_rows = dh.shape[0]

    def body(dh_ref, a_ref, b_ref, w2_ref, da_ref, db_ref):
        dhb = (0.5 * dh_ref[...]).astype(BF16)
        for c0, cw in _col_chunks(D_FF):
            dhid = _dot_nt(dhb, w2_ref[c0:c0 + cw, :])
            av = a_ref[:, c0:c0 + cw].astype(F32)
            bv = b_ref[:, c0:c0 + cw].astype(F32)
            s = _sigmoid(av)
            da_ref[:, c0:c0 + cw] = (dhid * bv * (s * (1.0 + av * (1.0 - s)))).astype(BF16)
            db_ref[:, c0:c0 + cw] = (dhid * (av * s)).astype(BF16)

    hid = pl.BlockSpec((tm, D_FF), lambda i: (i, 0))
    return _pcall(
        body, name=name, grid=(t_rows // tm,),
        in_specs=[pl.BlockSpec((tm, D_MODEL), lambda i: (i, 0)), hid, hid, _resident((D_FF, D_MODEL))],
        out_specs=[hid, hid],
        out_shape=[jax.ShapeDtypeStruct((t_rows, D_FF), BF16), jax.ShapeDtypeStruct((t_rows, D_FF), BF16)],
        compiler_params=_cp(("arbitrary",), VMEM_BIG),
    )(dh, a, b, w2)


def ffn_backward_input(dh, h, norm, da, db, w1, w3, tm, name):
    t_rows = h.shape[0]

    def body(dh_ref, h_ref, g_ref, da_ref, db_ref, w1_ref, w3_ref, dhin_ref, dg_ref):
        dhn = _dot_nt(da_ref[...], w1_ref[...]) + _dot_nt(db_ref[...], w3_ref[...])
        dx, dg = _rms_bwd(h_ref[...], g_ref[...], dhn)
        dhin_ref[...] = dh_ref[...] + dx
        _accumulate(dg_ref, dg, pl.program_id(0) == 0)

    row = pl.BlockSpec((tm, D_MODEL), lambda i: (i, 0))
    vec = pl.BlockSpec((1, D_MODEL), lambda i: (0, 0))
    hid = pl.BlockSpec((tm, D_FF), lambda i: (i, 0))
    return _pcall(
        body, name=name, grid=(t_rows // tm,),
        in_specs=[row, row, vec, hid, hid, _resident((D_MODEL, D_FF)), _resident((D_MODEL, D_FF))],
        out_specs=[row, vec],
        out_shape=[jax.ShapeDtypeStruct((t_rows, D_MODEL), F32), jax.ShapeDtypeStruct((1, D_MODEL), F32)],
        compiler_params=_cp(("arbitrary",), VMEM_BIG),
    )(dh, h, norm, da, db, w1, w3)


def ffn_backward_weights(hn, dh, a, b, da, db, tm, tn, name):
    t_rows = hn.shape[0]
    ni = t_rows // tm

    def body(hn_ref, dh_ref, a_ref, b_ref, da_ref, db_ref, dw1_ref, dw3_ref, dw2_ref, acc1, acc3, acc2):
        i = pl.program_id(1)
        hn_v = hn_ref[...]
        _accumulate(acc1, _dot_tn(da_ref[...], hn_v), i == 0)
        _accumulate(acc3, _dot_tn(db_ref[...], hn_v), i == 0)
        av = a_ref[...].astype(F32)
        hid = (av * _sigmoid(av) * b_ref[...].astype(F32)).astype(BF16)
        _accumulate(acc2, _dot_tn(hid, (0.5 * dh_ref[...]).astype(BF16)), i == 0)

        @pl.when(i == ni - 1)
        def _():
            dw1_ref[...] = acc1[...].astype(BF16)
            dw3_ref[...] = acc3[...].astype(BF16)
            dw2_ref[...] = acc2[...].astype(BF16)

    row = pl.BlockSpec((tm, D_MODEL), lambda j, i: (i, 0))
    hid_blk = pl.BlockSpec((tm, tn), lambda j, i: (i, j))
    w_row = pl.BlockSpec((tn, D_MODEL), lambda j, i: (j, 0))
    out = jax.ShapeDtypeStruct((D_FF, D_MODEL), BF16)
    return _pcall(
        body, name=name, grid=(D_FF // tn, ni),
        in_specs=[row, row, hid_blk, hid_blk, hid_blk, hid_blk],
        out_specs=[w_row, w_row, w_row], out_shape=[out, out, out],
        scratch_shapes=[pltpu.VMEM((tn, D_MODEL), F32)] * 3,
        compiler_params=_cp(("arbitrary", "arbitrary"), VMEM_BIG),
    )(hn, dh, a, b, da, db)


def mix_forward(h, norm, wing, tm, name):
    t_rows = h.shape[0]

    def body(h_ref, g_ref, w_ref, hn_ref, p_ref):
        hn = _rms_fwd(h_ref[...], g_ref[...]).astype(BF16)
        hn_ref[...] = hn
        for j in range(N_DEV):
            p_ref[:, j * IN_BLK:(j + 1) * IN_BLK] = _dot(hn, w_ref[j]).astype(BF16)

    row = pl.BlockSpec((tm, D_MODEL), lambda i: (i, 0))
    return _pcall(
        body, name=name, grid=(t_rows // tm,),
        in_specs=[row, pl.BlockSpec((1, D_MODEL), lambda i: (0, 0)),
                  pl.BlockSpec((N_DEV, D_MODEL, IN_BLK), lambda i: (0, 0, 0))],
        out_specs=[row, pl.BlockSpec((tm, IN_WIDTH), lambda i: (i, 0))],
        out_shape=[jax.ShapeDtypeStruct((t_rows, D_MODEL), BF16), jax.ShapeDtypeStruct((t_rows, IN_WIDTH), BF16)],
        compiler_params=_cp(("arbitrary",), VMEM_BIG),
    )(h, norm, wing)


def mix_backward_act(dh, h, norm, dproj, wing, tm, name):
    t_rows = h.shape[0]
    per_step = 4
    nj = N_DEV // per_step

    def body(dh_ref, h_ref, g_ref, dp_ref, w_ref, dhin_ref, dg_ref, acc_ref):
        i, j = pl.program_id(0), pl.program_id(1)
        part = functools.reduce(
            lambda u, w: u + w, [_dot_nt(dp_ref[:, k * IN_BLK:(k + 1) * IN_BLK], w_ref[k]) for k in range(per_step)])
        _accumulate(acc_ref, part, j == 0)

        @pl.when(j == nj - 1)
        def _():
            dx, dg = _rms_bwd(h_ref[...], g_ref[...], acc_ref[...])
            dhin_ref[...] = dh_ref[...] + dx
            _accumulate(dg_ref, dg, i == 0)

    row = pl.BlockSpec((tm, D_MODEL), lambda i, j: (i, 0))
    vec = pl.BlockSpec((1, D_MODEL), lambda i, j: (0, 0))
    return _pcall(
        body, name=name, grid=(t_rows // tm, nj),
        in_specs=[row, row, vec, pl.BlockSpec((tm, per_step * IN_BLK), lambda i, j: (i, j)),
                  pl.BlockSpec((per_step, D_MODEL, IN_BLK), lambda i, j: (j, 0, 0))],
        out_specs=[row, vec],
        out_shape=[jax.ShapeDtypeStruct((t_rows, D_MODEL), F32), jax.ShapeDtypeStruct((1, D_MODEL), F32)],
        scratch_shapes=[pltpu.VMEM((tm, D_MODEL), F32)],
        compiler_params=_cp(("arbitrary", "arbitrary"), VMEM_BIG),
    )(dh, h, norm, dproj, wing)


def mix_backward_weights(hn, dproj, tm, name):
    t_rows = hn.shape[0]
    ni = t_rows // tm
    per_step = 2

    def body(hn_ref, dp_ref, dw_ref, acc):
        i = pl.program_id(1)
        _accumulate(acc, _dot_tn(hn_ref[...], dp_ref[...]), i == 0)

        @pl.when(i == ni - 1)
        def _():
            for k in range(per_step):
                dw_ref[k] = acc[:, k * IN_BLK:(k + 1) * IN_BLK].astype(BF16)

    return _pcall(
        body, name=name, grid=(N_DEV // per_step, ni),
        in_specs=[pl.BlockSpec((tm, D_MODEL), lambda j, i: (i, 0)),
                  pl.BlockSpec((tm, per_step * IN_BLK), lambda j, i: (i, j))],
        out_specs=pl.BlockSpec((per_step, D_MODEL, IN_BLK), lambda j, i: (j, 0, 0)),
        out_shape=jax.ShapeDtypeStruct((N_DEV, D_MODEL, IN_BLK), BF16),
        scratch_shapes=[pltpu.VMEM((D_MODEL, per_step * IN_BLK), F32)],
        compiler_params=_cp(("arbitrary", "arbitrary"), VMEM_BIG),
    )(hn, dproj)


GELU_C = 0.7978845608028654
GELU_K = 0.044715


def _gelu(x):
    return 0.5 * x * (1.0 + jnp.tanh(GELU_C * (x + GELU_K * (x * x * x))))


def _gelu_and_grad(x):
    th = jnp.tanh(GELU_C * (x + GELU_K * (x * x * x)))
    val = 0.5 * x * (1.0 + th)
    grad = 0.5 * (1.0 + th) + 0.5 * x * (1.0 - th * th) * (GELU_C * (1.0 + 3.0 * GELU_K * (x * x)))
    return val, grad


def merge_forward(h, yraw, attn, proj, glu_a, glu_b, w_out, tm, name):
    t_rows = h.shape[0]

    def body(h_ref, y_ref, at_ref, gate_ref, a_ref, b_ref, wo_ref, out_ref):
        y = _gelu(y_ref[...]).astype(BF16)
        ssm = _dot(y, a_ref[...]) * _sigmoid(_dot(y, b_ref[...]))
        ga = gate_ref[:, :D_MODEL].astype(F32)
        gs = gate_ref[:, D_MODEL:].astype(F32)
        merged = _sigmoid(ga) * at_ref[...].astype(F32) + _sigmoid(gs) * ssm
        out_ref[...] = h_ref[...] + _dot(merged.astype(BF16), wo_ref[...])

    row = pl.BlockSpec((tm, D_MODEL), lambda i: (i, 0))
    glu = pl.BlockSpec((SSM_WIDTH, D_MODEL), lambda i: (0, 0))
    return _pcall(
        body, name=name, grid=(t_rows // tm,),
        in_specs=[row, pl.BlockSpec((tm, SSM_WIDTH), lambda i: (i, 0)), row,
                  pl.BlockSpec((tm, 2 * D_MODEL), lambda i: (i, 1)), glu, glu,
                  pl.BlockSpec((D_MODEL, D_MODEL), lambda i: (0, 0))],
        out_specs=row, out_shape=jax.ShapeDtypeStruct((t_rows, D_MODEL), F32),
        compiler_params=_cp(("arbitrary",), VMEM_BIG),
    )(h, yraw, attn, proj, glu_a, glu_b, w_out)


def merge_backward(dh, yraw, attn, proj, glu_a, glu_b, w_out, tm, name):
    t_rows = dh.shape[0]

    def body(dh_ref, y_ref, at_ref, gate_ref, a_ref, b_ref, wo_ref,
             dat_ref, dy_ref, dgate_ref, dwo_ref, da_ref, db_ref):
        first = pl.program_id(0) == 0
        d16 = dh_ref[...].astype(BF16)
        dmerged = _dot_nt(d16, wo_ref[...])
        gel, dgel = _gelu_and_grad(y_ref[...].astype(F32))
        y16 = gel.astype(BF16)
        ya = _dot(y16, a_ref[...])
        sb = _sigmoid(_dot(y16, b_ref[...]))
        ssm = ya * sb
        sa = _sigmoid(gate_ref[:, :D_MODEL].astype(F32))
        ss = _sigmoid(gate_ref[:, D_MODEL:].astype(F32))
        attn_v = at_ref[...].astype(F32)
        merged = (sa * attn_v + ss * ssm).astype(BF16)
        _accumulate(dwo_ref, _dot_tn(merged, d16), first)
        dat_ref[...] = (dmerged * sa).astype(BF16)
        dgate_ref[:, :D_MODEL] = (dmerged * attn_v * sa * (1.0 - sa)).astype(BF16)
        dgate_ref[:, D_MODEL:] = (dmerged * ssm * ss * (1.0 - ss)).astype(BF16)
        dssm = dmerged * ss
        dya = (dssm * sb).astype(BF16)
        dyb = (dssm * ya * sb * (1.0 - sb)).astype(BF16)
        _accumulate(da_ref, _dot_tn(y16, dya), first)
        _accumulate(db_ref, _dot_tn(y16, dyb), first)
        dy = _dot_nt(dya, a_ref[...]) + _dot_nt(dyb, b_ref[...])
        dy_ref[...] = (dy * dgel).astype(BF16)

    row = pl.BlockSpec((tm, D_MODEL), lambda i: (i, 0))
    ssm_row = pl.BlockSpec((tm, SSM_WIDTH), lambda i: (i, 0))
    gates = pl.BlockSpec((tm, 2 * D_MODEL), lambda i: (i, 1))
    glu = pl.BlockSpec((SSM_WIDTH, D_MODEL), lambda i: (0, 0))
    wo = pl.BlockSpec((D_MODEL, D_MODEL), lambda i: (0, 0))
    return _pcall(
        body, name=name, grid=(t_rows // tm,),
        in_specs=[row, ssm_row, row, gates, glu, glu, wo],
        out_specs=[row, ssm_row, gates, wo, glu, glu],
        out_shape=[jax.ShapeDtypeStruct((t_rows, D_MODEL), BF16), jax.ShapeDtypeStruct((t_rows, SSM_WIDTH), BF16),
                   jax.ShapeDtypeStruct((t_rows, IN_WIDTH), BF16), jax.ShapeDtypeStruct((D_MODEL, D_MODEL), F32),
                   jax.ShapeDtypeStruct((SSM_WIDTH, D_MODEL), F32), jax.ShapeDtypeStruct((SSM_WIDTH, D_MODEL), F32)],
        compiler_params=_cp(("arbitrary",), VMEM_BIG),
    )(dh, yraw, attn, proj, glu_a, glu_b, w_out)


def final_loss_backward(h, target, norm, seq, tm, name):
    t_rows = h.shape[0]
    tiles_per_example = (seq + N_META) // tm

    def body(h_ref, t_ref, g_ref, dh_ref, loss_ref, dg_ref):
        i = pl.program_id(0)
        x = h_ref[...]
        g = g_ref[...]
        r = lax.rsqrt(jnp.mean(x * x, axis=-1, keepdims=True) + NORM_EPS)
        xh = x * r
        pos = lax.broadcasted_iota(jnp.int32, (tm, 1), 0) + (i % tiles_per_example) * tm
        diff = jnp.where(pos < seq, xh * g - t_ref[...], 0.0)
        part = 0.5 * jnp.sum(jnp.sum(diff * diff, axis=-1, keepdims=True), axis=0, keepdims=True) / D_MODEL
        dy = diff / D_MODEL
        t = dy * g
        dh_ref[...] = r * (t - xh * jnp.mean(t * xh, axis=-1, keepdims=True))
        _accumulate(loss_ref, jnp.broadcast_to(part, (1, LANES)), i == 0)
        _accumulate(dg_ref, jnp.sum(dy * xh, axis=0, keepdims=True), i == 0)

    row = pl.BlockSpec((tm, D_MODEL), lambda i: (i, 0))
    vec = pl.BlockSpec((1, D_MODEL), lambda i: (0, 0))
    return _pcall(
        body, name=name, grid=(t_rows // tm,),
        in_specs=[row, row, vec],
        out_specs=[row, pl.BlockSpec((1, LANES), lambda i: (0, 0)), vec],
        out_shape=[jax.ShapeDtypeStruct((t_rows, D_MODEL), F32), jax.ShapeDtypeStruct((1, LANES), F32),
                   jax.ShapeDtypeStruct((1, D_MODEL), F32)],
        compiler_params=_cp(("arbitrary",), VMEM_BIG),
    )(h, target, norm)


ATTN_SCALE = HEAD_DIM ** -0.5
STACK_HEADS = (0, 2, 1, 3)


def _lane_half(shape, hf):
    lane = lax.broadcasted_iota(jnp.int32, shape, 1)
    return (lane < HEAD_DIM) if hf == 0 else (lane >= HEAD_DIM)


def _kv_variants(ref, rows, kh):
    tile = kh // 2
    t = ref[rows, tile * LANES:(tile + 1) * LANES].astype(F32)
    swapped = pltpu.roll(t, HEAD_DIM, axis=1)
    at_low, at_high = (t, swapped) if kh % 2 == 0 else (swapped, t)
    lo = jnp.where(_lane_half(t.shape, 0), at_low, 0.0).astype(BF16)
    hi = jnp.where(_lane_half(t.shape, 1), at_high, 0.0).astype(BF16)
    return lo, hi


def _to_kv_lanes(lo, hi, kh):
    lo = jnp.where(_lane_half(lo.shape, 0), lo, 0.0)
    hi = jnp.where(_lane_half(hi.shape, 1), hi, 0.0)
    if kh % 2 == 0:
        return lo + pltpu.roll(hi, HEAD_DIM, axis=1)
    return pltpu.roll(lo, HEAD_DIM, axis=1) + hi


def _stacked(ref, rows, kh):
    col = kh * 2 * LANES
    return jnp.concatenate([ref[rows, col:col + LANES], ref[rows, col + LANES:col + 2 * LANES]], axis=0)


def _sink_column(sink_ref, kh, nq):
    row = lax.broadcasted_iota(jnp.int32, (4 * nq, 1), 0)
    col = jnp.zeros((4 * nq, 1), F32)
    for quarter, g in enumerate(STACK_HEADS):
        col = jnp.where(row // nq == quarter, sink_ref[0, kh * Q_PER_KV + g], col)
    return col


def _softmax_parts(qs, key_tiles, masks, sink):
    scores = []
    for (k_lo, k_hi), mask in zip(key_tiles, masks):
        s = jnp.concatenate([_dot_nt(qs, k_lo), _dot_nt(qs, k_hi)], axis=0) * ATTN_SCALE
        scores.append(s if mask is None else jnp.where(mask, s, NEG_INF))
    m = functools.reduce(jnp.maximum, [jnp.max(s, axis=-1, keepdims=True) for s in scores])
    m = jnp.maximum(m, sink)
    probs = [jnp.exp(s - m) for s in scores]
    e_sink = jnp.exp(sink - m)
    den = functools.reduce(lambda u, w: u + w, [jnp.sum(p, axis=-1, keepdims=True) for p in probs]) + e_sink
    return probs, 1.0 / den, e_sink


def _band_mask(nq, first):
    keys = BLOCK if first else 2 * BLOCK
    qi = lax.broadcasted_iota(jnp.int32, (4 * nq, keys), 0) % nq
    kj = lax.broadcasted_iota(jnp.int32, (4 * nq, keys), 1)
    if first:
        return kj <= qi
    return jnp.logical_and(kj > qi, kj <= qi + BLOCK)


def _meta_mask():
    qi = lax.broadcasted_iota(jnp.int32, (4 * N_META, N_META), 0) % N_META
    kj = lax.broadcasted_iota(jnp.int32, (4 * N_META, N_META), 1)
    return kj <= qi


def _attention_schedule(seq, queries, carry):
    meta_rows = pl.ds(seq, N_META)
    carry = queries(pl.ds(0, BLOCK), BLOCK, [pl.ds(0, BLOCK), meta_rows], [_band_mask(BLOCK, True), None], carry)

    def block(n, c):
        r0 = pl.multiple_of(n * BLOCK, BLOCK)
        p0 = pl.multiple_of((n - 1) * BLOCK, BLOCK)
        return queries(pl.ds(r0, BLOCK), BLOCK, [pl.ds(p0, 2 * BLOCK), meta_rows], [_band_mask(BLOCK, False), None], c)

    carry = lax.fori_loop(1, seq // BLOCK, block, carry)
    return queries(meta_rows, N_META, [meta_rows], [_meta_mask()], carry)


def attention_forward(proj3, sinks, seq, name):
    n_b, n_l, _ = proj3.shape

    def body(sink_ref, q_ref, k_ref, v_ref, o_ref):
        def queries(q_rows, nq, key_rows, masks, carry):
            for kh in range(N_KV_HEADS):
                ks = [_kv_variants(k_ref, r, kh) for r in key_rows]
                vs = [_kv_variants(v_ref, r, kh) for r in key_rows]
                qs = _stacked(q_ref, q_rows, kh)
                probs, inv, _ = _softmax_parts(qs, ks, masks, _sink_column(sink_ref, kh, nq))
                probs = [p.astype(BF16) for p in probs]
                o_lo = functools.reduce(lambda u, w: u + w, [_dot(p[:2 * nq], v_lo) for p, (v_lo, _) in zip(probs, vs)])
                o_hi = functools.reduce(lambda u, w: u + w, [_dot(p[2 * nq:], v_hi) for p, (_, v_hi) in zip(probs, vs)])
                out = (o_lo * inv[:2 * nq] + o_hi * inv[2 * nq:]).astype(BF16)
                col = kh * 2 * LANES
                o_ref[q_rows, col:col + LANES] = out[:nq]
                o_ref[q_rows, col + LANES:col + 2 * LANES] = out[nq:]
            return carry

        _attention_schedule(seq, queries, 0)

    return _pcall(
        body, name=name, grid=(n_b,),
        in_specs=[pl.BlockSpec(memory_space=pltpu.SMEM),
                  pl.BlockSpec((None, n_l, D_MODEL), lambda b: (b, 0, 0)),
                  pl.BlockSpec((None, n_l, KV_WIDTH), lambda b: (b, 0, D_MODEL // KV_WIDTH)),
                  pl.BlockSpec((None, n_l, KV_WIDTH), lambda b: (b, 0, D_MODEL // KV_WIDTH + 1))],
        out_specs=pl.BlockSpec((None, n_l, D_MODEL), lambda b: (b, 0, 0)),
        out_shape=jax.ShapeDtypeStruct((n_b, n_l, D_MODEL), BF16),
        compiler_params=_cp(("arbitrary",), VMEM_BIG),
    )(sinks, proj3, proj3, proj3)


def attention_backward(proj3, dattn3, dproj3, sinks, seq, name):
    n_b, n_l, _ = proj3.shape
    qkv_width = D_MODEL + 2 * KV_WIDTH

    def body(sink_ref, q_ref, k_ref, v_ref, do_ref, _, dqkv_ref, dsink_ref, dk_ref, dv_ref):
        dk_ref[...] = jnp.zeros_like(dk_ref)
        dv_ref[...] = jnp.zeros_like(dv_ref)
        sub = lax.broadcasted_iota(jnp.int32, (SUBLANES, LANES), 0)
        lane = lax.broadcasted_iota(jnp.int32, (SUBLANES, LANES), 1)

        def queries(q_rows, nq, key_rows, masks, dsink):
            for kh in range(N_KV_HEADS):
                ks = [_kv_variants(k_ref, r, kh) for r in key_rows]
                vs = [_kv_variants(v_ref, r, kh) for r in key_rows]
                qs = _stacked(q_ref, q_rows, kh)
                dos = _stacked(do_ref, q_rows, kh)
                probs, inv, e_sink = _softmax_parts(qs, ks, masks, _sink_column(sink_ref, kh, nq))
                probs = [p * inv for p in probs]
                dps = [jnp.concatenate([_dot_nt(dos, v_lo), _dot_nt(dos, v_hi)], axis=0) for v_lo, v_hi in vs]
                delta = functools.reduce(
                    lambda u, w: u + w, [jnp.sum(p * dp, axis=-1, keepdims=True) for p, dp in zip(probs, dps)])
                d_sink = -(e_sink * inv) * delta
                for quarter, g in enumerate(STACK_HEADS):
                    d_here = jnp.sum(d_sink[quarter * nq:(quarter + 1) * nq], axis=0, keepdims=True)
                    dsink = dsink + jnp.where(jnp.logical_and(sub == 0, lane == kh * Q_PER_KV + g), d_here, 0.0)
                dq = None
                tile = slice((kh // 2) * LANES, (kh // 2 + 1) * LANES)
                for r, p, dp, (k_lo, k_hi) in zip(key_rows, probs, dps, ks):
                    ds = (p * (dp - delta)).astype(BF16)
                    p16 = p.astype(BF16)
                    dq_x = _dot(ds[:2 * nq], k_lo) + _dot(ds[2 * nq:], k_hi)
                    dq = dq_x if dq is None else dq + dq_x
                    dk_ref[r, tile] += _to_kv_lanes(_dot_tn(ds[:2 * nq], qs), _dot_tn(ds[2 * nq:], qs), kh) * ATTN_SCALE
                    dv_ref[r, tile] += _to_kv_lanes(_dot_tn(p16[:2 * nq], dos), _dot_tn(p16[2 * nq:], dos), kh)
                dq = (dq * ATTN_SCALE).astype(BF16)
                col = kh * 2 * LANES
                dqkv_ref[q_rows, col:col + LANES] = dq[:nq]
                dqkv_ref[q_rows, col + LANES:col + 2 * LANES] = dq[nq:]
            return dsink

        dsink_ref[...] = _attention_schedule(seq, queries, jnp.zeros((SUBLANES, LANES), F32))
        dqkv_ref[:, D_MODEL:D_MODEL + KV_WIDTH] = dk_ref[...].astype(BF16)
        dqkv_ref[:, D_MODEL + KV_WIDTH:] = dv_ref[...].astype(BF16)

    return _pcall(
        body, name=name, grid=(n_b,),
        in_specs=[pl.BlockSpec(memory_space=pltpu.SMEM),
                  pl.BlockSpec((None, n_l, D_MODEL), lambda b: (b, 0, 0)),
                  pl.BlockSpec((None, n_l, KV_WIDTH), lambda b: (b, 0, D_MODEL // KV_WIDTH)),
                  pl.BlockSpec((None, n_l, KV_WIDTH), lambda b: (b, 0, D_MODEL // KV_WIDTH + 1)),
                  pl.BlockSpec((None, n_l, D_MODEL), lambda b: (b, 0, 0)),
                  pl.BlockSpec(memory_space=pl.ANY)],
        out_specs=[pl.BlockSpec((None, n_l, qkv_width), lambda b: (b, 0, 0)),
                   pl.BlockSpec((None, SUBLANES, LANES), lambda b: (b, 0, 0))],
        out_shape=[jax.ShapeDtypeStruct(dproj3.shape, BF16), jax.ShapeDtypeStruct((n_b, SUBLANES, LANES), F32)],
        scratch_shapes=[pltpu.VMEM((n_l, KV_WIDTH), F32), pltpu.VMEM((n_l, KV_WIDTH), F32)],
        input_output_aliases={5: 0},
        compiler_params=_cp(("arbitrary",), VMEM_BIG),
    )(sinks, proj3, proj3, proj3, dattn3, dproj3)


TAB_ROWS = 8
SCAN_UNROLL = 4


def _cmul(ar, ai, br, bi):
    return ar * br - ai * bi, ar * bi + ai * br


def _discretise(ar, ai, ls):
    step = jnp.exp(ls)
    mag = jnp.exp(ar * step)
    ang = ai * step
    cos, sin = jnp.cos(ang), jnp.sin(ang)
    lr, li = mag * cos, mag * sin
    den = ar * ar + ai * ai
    nr, ni = lr - 1.0, li
    cr = (nr * ar + ni * ai) / den
    ci = (ni * ar - nr * ai) / den
    return step, mag, lr, li, den, nr, ni, cr, ci


def _scan_tables(lr, li, reverse):
    n = lr.shape[-1]
    pw = [(lr, li)]
    for _ in range(SUBLANES - 1):
        pw.append(_cmul(pw[-1][0], pw[-1][1], lr, li))
    row = lax.broadcasted_iota(jnp.int32, (SUBLANES, n), 0)
    out = []
    for d in (1, 2, 4):
        ok = (row + d <= SUBLANES - 1) if reverse else (row >= d)
        out += [jnp.where(ok, pw[d - 1][0], 0.0), jnp.where(ok, pw[d - 1][1], 0.0)]
    cr = jnp.zeros((SUBLANES, n), F32)
    ci = jnp.zeros((SUBLANES, n), F32)
    for r in range(SUBLANES):
        e = (SUBLANES - r) if reverse else (r + 1)
        cr = jnp.where(row == r, pw[e - 1][0], cr)
        ci = jnp.where(row == r, pw[e - 1][1], ci)
    return out + [cr, ci]


def ssm_prepare(ar, ai, ls, br_t, bi_t, name):
    def body(ar_ref, ai_ref, ls_ref, br_ref, bi_ref, bbr_ref, bbi_ref, tf_ref, tr_ref):
        _, _, lr, li, _, _, _, cr, ci = _discretise(ar_ref[...], ai_ref[...], ls_ref[...])
        br, bi = br_ref[...], bi_ref[...]
        bbr_ref[...] = cr * br - ci * bi
        bbi_ref[...] = cr * bi + ci * br
        for k, t in enumerate(_scan_tables(lr, li, False)):
            tf_ref[k] = t
        for k, t in enumerate(_scan_tables(lr, -li, True)):
            tr_ref[k] = t

    return _pcall(
        body, name=name,
        out_shape=[jax.ShapeDtypeStruct((SSM_GROUP, N_STATES), F32), jax.ShapeDtypeStruct((SSM_GROUP, N_STATES), F32),
                   jax.ShapeDtypeStruct((TAB_ROWS, SUBLANES, N_STATES), F32),
                   jax.ShapeDtypeStruct((TAB_ROWS, SUBLANES, N_STATES), F32)],
    )(ar, ai, ls, br_t, bi_t)


def ssm_param_backward(ar, ai, ls, br_t, bi_t, dlr_p, dli_p, dbbr, dbbi, group_sum, name):
    def body(ar_ref, ai_ref, ls_ref, br_ref, bi_ref, dlr_ref, dli_ref, dbbr_ref, dbbi_ref, gs_ref,
             dar_ref, dai_ref, dls_ref, dbr_ref, dbi_ref):
        ar, ai = ar_ref[...], ai_ref[...]
        step, mag, lr, li, den, nr, ni, cr, ci = _discretise(ar, ai, ls_ref[...])
        br, bi, dbbr_v, dbbi_v = br_ref[...], bi_ref[...], dbbr_ref[...], dbbi_ref[...]
        dbr_ref[...] = cr * dbbr_v + ci * dbbi_v
        dbi_ref[...] = cr * dbbi_v - ci * dbbr_v
        dcr = jnp.sum(dbbr_v * br + dbbi_v * bi, axis=0, keepdims=True)
        dci = jnp.sum(dbbi_v * br - dbbr_v * bi, axis=0, keepdims=True)
        dnr = (dcr * ar - dci * ai) / den
        dni = (dcr * ai + dci * ar) / den
        dden = -(cr * dcr + ci * dci) / den
        dar = (dcr * nr + dci * ni) / den + dden * 2.0 * ar
        dai = (dcr * ni - dci * nr) / den + dden * 2.0 * ai
        dlr = jnp.sum(dlr_ref[...], axis=0, keepdims=True) + dnr
        dli = jnp.sum(dli_ref[...], axis=0, keepdims=True) + dni
        dmag = (dlr * lr + dli * li) / mag
        dang = dli * lr - dlr * li
        dar_ref[...] = dar + dmag * mag * step
        dai_ref[...] = dai + dang * step
        dstep = dmag * mag * ar + dang * ai
        dls_ref[...] = jnp.dot(dstep * step, gs_ref[...], preferred_element_type=F32, precision=lax.Precision.HIGHEST)

    vec = jax.ShapeDtypeStruct((1, N_STATES), F32)
    mat = jax.ShapeDtypeStruct((SSM_GROUP, N_STATES), F32)
    return _pcall(body, name=name, out_shape=[vec, vec, jax.ShapeDtypeStruct((1, LANES), F32), mat, mat])(
        ar, ai, ls, br_t, bi_t, dlr_p, dli_p, dbbr, dbbi, group_sum)


def _scan_rows(a, b, tabs, carry, reverse):
    for k, d in enumerate((1, 2, 4)):
        shift = SUBLANES - d if reverse else d
        sr, si = pltpu.roll(a, shift, axis=0), pltpu.roll(b, shift, axis=0)
        pr, pi = _cmul(tabs[2 * k], tabs[2 * k + 1], sr, si)
        a, b = a + pr, b + pi
    pr, pi = _cmul(tabs[6], tabs[7], carry[0], carry[1])
    return a + pr, b + pi


def _time_groups(seq, reverse):
    meta = [seq + SUBLANES * g for g in range(N_META // SUBLANES)]
    return meta[::-1] if reverse else meta


def ssm_forward_scan(proj3, b_comb, tabf, c_comb, dvec, seq, name):
    n_b, n_l, _ = proj3.shape
    u_blk = (D_MODEL + 2 * KV_WIDTH) // LANES

    def body(u_ref, b_ref, tab_ref, c_ref, d_ref, x_ref, y_ref, bu, xs):
        j = pl.program_id(1)
        u = u_ref[...]
        bu[...] = _dot(u, b_ref[...])
        tabs = [tab_ref[k] for k in range(TAB_ROWS)]

        def group(r0, carry):
            rows = pl.ds(r0, SUBLANES)
            a, b = _scan_rows(bu[rows, :SCAN_COLS], bu[rows, SCAN_COLS:], tabs, carry, False)
            xs[rows, :SCAN_COLS] = a
            xs[rows, SCAN_COLS:] = b
            return (jnp.broadcast_to(a[SUBLANES - 1:, :], a.shape), jnp.broadcast_to(b[SUBLANES - 1:, :], b.shape))

        zero = jnp.zeros((SUBLANES, SCAN_COLS), F32)
        carry = (zero, zero)
        for r0 in _time_groups(seq, False):
            carry = group(r0, carry)
        span = SCAN_UNROLL * SUBLANES

        def groups(t, c):
            for k in range(SCAN_UNROLL):
                c = group(pl.multiple_of(t * span, span) + k * SUBLANES, c)
            return c

        lax.fori_loop(0, seq // span, groups, carry)
        x16 = xs[...].astype(BF16)
        x_ref[...] = x16
        contrib = _dot(x16, c_ref[...])

        @pl.when(j % 2 == 0)
        def _():
            y_ref[...] = contrib + d_ref[...] * u.astype(F32)

        @pl.when(j % 2 == 1)
        def _():
            y_ref[...] += contrib

    return _pcall(
        body, name=name, grid=(n_b, N_SCAN_BLK),
        in_specs=[pl.BlockSpec((None, n_l, LANES), lambda b, j: (b, 0, u_blk + j // 2)),
                  pl.BlockSpec((LANES, 2 * SCAN_COLS), lambda b, j: (j // 2, j)),
                  pl.BlockSpec((TAB_ROWS, SUBLANES, SCAN_COLS), lambda b, j: (0, 0, j)),
                  pl.BlockSpec((2 * SCAN_COLS, LANES), lambda b, j: (j, j // 2)),
                  pl.BlockSpec((1, LANES), lambda b, j: (0, j // 2))],
        out_specs=[pl.BlockSpec((None, n_l, 2 * SCAN_COLS), lambda b, j: (b, 0, j)),
                   pl.BlockSpec((None, n_l, LANES), lambda b, j: (b, 0, j // 2))],
        out_shape=[jax.ShapeDtypeStruct((n_b, n_l, 2 * N_STATES), BF16),
                   jax.ShapeDtypeStruct((n_b, n_l, SSM_WIDTH), F32)],
        scratch_shapes=[pltpu.VMEM((n_l, 2 * SCAN_COLS), F32)] * 2,
        compiler_params=_cp(("arbitrary", "arbitrary"), VMEM_BIG),
    )(proj3, b_comb, tabf, c_comb, dvec)


def ssm_backward_scan(dyraw3, xs3, dproj3, c_comb_t, tabr, b_comb_t, dvec, seq, name):
    n_b, n_l, _ = xs3.shape
    u_blk = (D_MODEL + 2 * KV_WIDTH) // LANES

    def body(dy_ref, x_ref, _, c_ref, tab_ref, b_ref, d_ref, du_ref, g_ref, dlr_ref, dli_ref, dx, gs, xs, du_acc):
        j = pl.program_id(1)
        dy = dy_ref[...]
        dx[...] = _dot(dy, c_ref[...])
        xs[...] = x_ref[...].astype(F32)
        tabs = [tab_ref[k] for k in range(TAB_ROWS)]
        last_row = lax.broadcasted_iota(jnp.int32, (SUBLANES, SCAN_COLS), 0) == SUBLANES - 1

        def group(r0, state):
            cr, ci, acc_r, acc_i = state
            rows = pl.ds(r0, SUBLANES)
            a, b = _scan_rows(dx[rows, :SCAN_COLS], dx[rows, SCAN_COLS:], tabs, (cr, ci), True)
            gs[rows, :SCAN_COLS] = a
            gs[rows, SCAN_COLS:] = b
            na = jnp.where(last_row, cr, pltpu.roll(a, SUBLANES - 1, axis=0))
            nb = jnp.where(last_row, ci, pltpu.roll(b, SUBLANES - 1, axis=0))
            xa, xb = xs[rows, :SCAN_COLS], xs[rows, SCAN_COLS:]
            return (jnp.broadcast_to(a[:1, :], a.shape), jnp.broadcast_to(b[:1, :], b.shape),
                    acc_r + na * xa + nb * xb, acc_i + nb * xa - na * xb)

        zero = jnp.zeros((SUBLANES, SCAN_COLS), F32)
        span = SCAN_UNROLL * SUBLANES
        n_spans = seq // span

        def groups(t, s):
            for k in reversed(range(SCAN_UNROLL)):
                s = group(pl.multiple_of((n_spans - 1 - t) * span, span) + k * SUBLANES, s)
            return s

        state = lax.fori_loop(0, n_spans, groups, (zero, zero, zero, zero))
        for r0 in _time_groups(seq, True):
            state = group(r0, state)
        dlr_ref[...] = state[2]
        dli_ref[...] = state[3]
        g16 = gs[...].astype(BF16)
        g_ref[...] = g16
        contrib = _dot(g16, b_ref[...])

        @pl.when(j % 2 == 0)
        def _():
            du_acc[...] = contrib + d_ref[...] * dy.astype(F32)

        @pl.when(j % 2 == 1)
        def _():
            du_ref[...] = (du_acc[...] + contrib).astype(BF16)

    state_blk = pl.BlockSpec((None, n_l, 2 * SCAN_COLS), lambda b, j: (b, 0, j))
    dl_blk = pl.BlockSpec((None, SUBLANES, SCAN_COLS), lambda b, j: (b, 0, j))
    return _pcall(
        body, name=name, grid=(n_b, N_SCAN_BLK),
        in_specs=[pl.BlockSpec((None, n_l, LANES), lambda b, j: (b, 0, j // 2)), state_blk,
                  pl.BlockSpec(memory_space=pl.ANY),
                  pl.BlockSpec((LANES, 2 * SCAN_COLS), lambda b, j: (j // 2, j)),
                  pl.BlockSpec((TAB_ROWS, SUBLANES, SCAN_COLS), lambda b, j: (0, 0, j)),
                  pl.BlockSpec((2 * SCAN_COLS, LANES), lambda b, j: (j, j // 2)),
                  pl.BlockSpec((1, LANES), lambda b, j: (0, j // 2))],
        out_specs=[pl.BlockSpec((None, n_l, LANES), lambda b, j: (b, 0, u_blk + j // 2)), state_blk, dl_blk, dl_blk],
        out_shape=[jax.ShapeDtypeStruct(dproj3.shape, BF16), jax.ShapeDtypeStruct((n_b, n_l, 2 * N_STATES), BF16),
                   jax.ShapeDtypeStruct((n_b, SUBLANES, N_STATES), F32), jax.ShapeDtypeStruct((n_b, SUBLANES, N_STATES), F32)],
        scratch_shapes=[pltpu.VMEM((n_l, 2 * SCAN_COLS), F32)] * 3 + [pltpu.VMEM((n_l, LANES), F32)],
        input_output_aliases={2: 0},
        compiler_params=_cp(("arbitrary", "arbitrary"), VMEM_BIG),
    )(dyraw3, xs3, dproj3, c_comb_t, tabr, b_comb_t, dvec)


def ssm_param_grads(proj, gs, xs, dyraw, tm, name):
    t_rows = proj.shape[0]
    ni = t_rows // tm
    n_cb = 2 * N_STATES // SSM_WIDTH
    u_blk = (D_MODEL + 2 * KV_WIDTH) // SSM_WIDTH

    def body(u_ref, g_ref, x_ref, dy_ref, db_ref, dc_ref, dd_ref, acc_b, acc_c):
        cb, i = pl.program_id(0), pl.program_id(1)

        @pl.when(i == 0)
        def _():
            acc_b[...] = jnp.zeros_like(acc_b)
            acc_c[...] = jnp.zeros_like(acc_c)

        u, dy = u_ref[...], dy_ref[...]
        acc_b[...] += _dot_tn(u, g_ref[...])
        acc_c[...] += _dot_tn(x_ref[...], dy)

        @pl.when(i == ni - 1)
        def _():
            db_ref[...] = acc_b[...]
            dc_ref[...] = acc_c[...]

        @pl.when(cb == 0)
        def _():
            _accumulate(dd_ref, jnp.sum(dy.astype(F32) * u.astype(F32), axis=0, keepdims=True), i == 0)

    sq = (SSM_WIDTH, SSM_WIDTH)
    return _pcall(
        body, name=name, grid=(n_cb, ni),
        in_specs=[pl.BlockSpec((tm, SSM_WIDTH), lambda cb, i: (i, u_blk)),
                  pl.BlockSpec((tm, SSM_WIDTH), lambda cb, i: (i, cb)),
                  pl.BlockSpec((tm, SSM_WIDTH), lambda cb, i: (i, cb)),
                  pl.BlockSpec((tm, SSM_WIDTH), lambda cb, i: (i, 0))],
        out_specs=[pl.BlockSpec(sq, lambda cb, i: (0, cb)), pl.BlockSpec(sq, lambda cb, i: (cb, 0)),
                   pl.BlockSpec((1, SSM_WIDTH), lambda cb, i: (0, 0))],
        out_shape=[jax.ShapeDtypeStruct((SSM_WIDTH, 2 * N_STATES), F32), jax.ShapeDtypeStruct((2 * N_STATES, SSM_WIDTH), F32),
                   jax.ShapeDtypeStruct((1, SSM_WIDTH), F32)],
        scratch_shapes=[pltpu.VMEM(sq, F32), pltpu.VMEM(sq, F32)],
        compiler_params=_cp(("arbitrary", "arbitrary"), VMEM_BIG),
    )(proj, gs, xs, dyraw)


def sum_leading(x, name):
    def body(x_ref, o_ref):
        acc = x_ref[0]
        for k in range(1, x.shape[0]):
            acc = acc + x_ref[k]
        o_ref[...] = acc

    return _pcall(body, name=name, out_shape=jax.ShapeDtypeStruct(x.shape[1:], x.dtype))(x)


WEIGHTS = ['meta_tokens', 'ffn1_norm', 'ffn1_w1', 'ffn1_w3', 'ffn1_w2', 'mix_norm', 'w_in', 'attn_sinks', 'ssm_a_re',
           'ssm_a_im', 'ssm_log_step', 'ssm_b_re', 'ssm_b_im', 'ssm_c_re', 'ssm_c_im', 'ssm_d', 'ssm_glu_a', 'ssm_glu_b',
           'w_out', 'ffn2_norm', 'ffn2_w1', 'ffn2_w3', 'ffn2_w2', 'final_norm']
SHARDED = ['ffn1_w1', 'ffn1_w3', 'ffn1_w2', 'ffn2_w1', 'ffn2_w3', 'ffn2_w2', 'w_in', 'ssm_glu_a', 'ssm_glu_b', 'w_out']
REPLICATED = ['ffn1_norm', 'mix_norm', 'ffn2_norm', 'final_norm', 'attn_sinks', 'ssm_a_re', 'ssm_a_im', 'ssm_log_step',
              'ssm_b_re', 'ssm_b_im', 'ssm_c_re', 'ssm_c_im', 'ssm_d']
PACK_COLS = 1024


def _block_diag(blocks):
    g, r, c = blocks.shape
    eye = jnp.eye(g, dtype=blocks.dtype)
    return (blocks[:, :, None, :] * eye[:, None, :, None]).reshape(g * r, g * c)


def _diag_blocks(mat, r, c):
    g = SSM_GROUPS
    eye = jnp.eye(g, dtype=mat.dtype)
    return jnp.sum(mat.reshape(g, r, g, c) * eye[:, None, :, None], axis=2)


def _scan_order(re, im):
    r = re.shape[0]
    return jnp.stack([re.reshape(r, N_SCAN_BLK, SCAN_COLS), im.reshape(r, N_SCAN_BLK, SCAN_COLS)], axis=2).reshape(r, 2 * N_STATES)


def _from_scan_order(comb):
    r = comb.shape[0]
    c4 = comb.reshape(r, N_SCAN_BLK, 2, SCAN_COLS)
    return c4[:, :, 0].reshape(r, N_STATES), c4[:, :, 1].reshape(r, N_STATES)


def _pack(arrays):
    parts = []
    for a in arrays:
        flat = a.reshape(-1)
        chunk = SUBLANES * PACK_COLS
        padded = -(-flat.shape[0] // chunk) * chunk
        parts.append(jnp.pad(flat, (0, padded - flat.shape[0])).reshape(-1, PACK_COLS))
    return jnp.concatenate(parts, axis=0)


def _unpack(packed, shapes):
    out, row = [], 0
    for shape in shapes:
        size = 1
        for s in shape:
            size *= s
        chunk = SUBLANES * PACK_COLS
        rows = -(-size // chunk) * SUBLANES
        out.append(packed[row:row + rows].reshape(-1)[:size].reshape(shape))
        row += rows
    return out


def kernel(x, meta_tokens, ffn1_norm, ffn1_w1, ffn1_w3, ffn1_w2, mix_norm, w_in, attn_sinks, ssm_a_re, ssm_a_im, ssm_log_step, ssm_b_re, ssm_b_im, ssm_c_re, ssm_c_im, ssm_d, ssm_glu_a, ssm_glu_b, w_out, ffn2_norm, ffn2_w1, ffn2_w3, ffn2_w2, final_norm, loss_target, m_meta_tokens, m_ffn1_norm, m_ffn1_w1, m_ffn1_w3, m_ffn1_w2, m_mix_norm, m_w_in, m_attn_sinks, m_ssm_a_re, m_ssm_a_im, m_ssm_log_step, m_ssm_b_re, m_ssm_b_im, m_ssm_c_re, m_ssm_c_im, m_ssm_d, m_ssm_glu_a, m_ssm_glu_b, m_w_out, m_ffn2_norm, m_ffn2_w1, m_ffn2_w3, m_ffn2_w2, m_final_norm, v_meta_tokens, v_ffn1_norm, v_ffn1_w1, v_ffn1_w3, v_ffn1_w2, v_mix_norm, v_w_in, v_attn_sinks, v_ssm_a_re, v_ssm_a_im, v_ssm_log_step, v_ssm_b_re, v_ssm_b_im, v_ssm_c_re, v_ssm_c_im, v_ssm_d, v_ssm_glu_a, v_ssm_glu_b, v_w_out, v_ffn2_norm, v_ffn2_w1, v_ffn2_w3, v_ffn2_w2, v_final_norm):
    given = dict(locals())
    w = {n: given[n] for n in WEIGHTS}
    m = {n: given["m_" + n] for n in WEIGHTS}
    v = {n: given["v_" + n] for n in WEIGHTS}

    n_b, seq, _ = x.shape
    n_l = seq + N_META
    t_rows = n_b * n_l
    tm = _row_tile(n_l, 688)
    px, py, pc = _my_place()
    me = 4 * px + 2 * py + pc
    dest = jnp.stack([4 * qx + 2 * qy + pc for qx, qy in
                      [(px, py), (1 - px, py), (px, 1 - py), (1 - px, 1 - py)]]).astype(jnp.int32)

    glu = jnp.stack([ssm_glu_a[0], ssm_glu_b[0]]).astype(BF16)
    ffn_names = ['ffn1_w1', 'ffn1_w3', 'ffn1_w2', 'ffn2_w1', 'ffn2_w3', 'ffn2_w2']
    gathered = all_gather_list(
        [w[n][0].astype(BF16) for n in ffn_names] + [w_in[0].astype(BF16), glu, w_out[0].astype(BF16), meta_tokens],
        "ag_weights")
    wing, glug, wog, metag = gathered[len(ffn_names):]
    full = {}
    for n, g in zip(ffn_names, gathered):
        if n.endswith('w2'):
            full[n] = g.reshape(D_FF, D_MODEL)
        else:
            full[n] = g.transpose(1, 0, 2).reshape(D_MODEL, D_FF)
    glu_a = glug[:, 0].transpose(1, 0, 2).reshape(SSM_WIDTH, D_MODEL)
    glu_b = glug[:, 1].transpose(1, 0, 2).reshape(SSM_WIDTH, D_MODEL)
    w_out_full = wog.reshape(D_MODEL, D_MODEL)
    meta_full = metag.transpose(1, 0, 2).reshape(N_META, D_MODEL)

    h0 = jnp.concatenate([x, jnp.broadcast_to(meta_full[None], (n_b, N_META, D_MODEL))], axis=1).reshape(t_rows, D_MODEL)
    target = jnp.concatenate([loss_target, jnp.zeros((n_b, N_META, D_MODEL), F32)], axis=1).reshape(t_rows, D_MODEL)
    final_g = final_norm.reshape(1, D_MODEL)

    ar = ssm_a_re.reshape(1, N_STATES)
    ai = ssm_a_im.reshape(1, N_STATES)
    ls = jnp.repeat(ssm_log_step.reshape(SSM_GROUPS), SSM_STATE).reshape(1, N_STATES)
    br_t = ssm_b_re[0].transpose(2, 0, 1).reshape(SSM_GROUP, N_STATES)
    bi_t = ssm_b_im[0].transpose(2, 0, 1).reshape(SSM_GROUP, N_STATES)
    bbr, bbi, tabf, tabr = ssm_prepare(ar, ai, ls, br_t, bi_t, "ssm_prepare")
    bbr_g = bbr.reshape(SSM_GROUP, SSM_GROUPS, SSM_STATE).transpose(1, 0, 2)
    bbi_g = bbi.reshape(SSM_GROUP, SSM_GROUPS, SSM_STATE).transpose(1, 0, 2)
    b_comb = _scan_order(_block_diag(bbr_g), _block_diag(bbi_g)).astype(BF16)
    c_comb_t = _scan_order(_block_diag(ssm_c_re[0]), -_block_diag(ssm_c_im[0])).astype(BF16)
    b_comb_t, c_comb = b_comb.T, c_comb_t.T

    ffn1_w = (full['ffn1_w1'], full['ffn1_w3'], full['ffn1_w2'])
    ffn2_w = (full['ffn2_w1'], full['ffn2_w3'], full['ffn2_w2'])
    h1, hn1, a1, b1 = ffn_forward(h0, ffn1_norm, *ffn1_w, tm, FF_FWD_COLS, "ffn1_fwd")
    hnm, proj = mix_forward(h1, mix_norm, wing, tm, "mix_fwd")
    proj3 = proj.reshape(n_b, n_l, IN_WIDTH)
    attn3 = attention_forward(proj3, attn_sinks, seq, "attn_fwd")
    attn = attn3.reshape(t_rows, D_MODEL)
    xs3, yraw3 = ssm_forward_scan(proj3, b_comb, tabf, c_comb, ssm_d, seq, "ssm_fwd")
    yraw = yraw3.reshape(t_rows, SSM_WIDTH)
    h2 = merge_forward(h1, yraw, attn, proj, glu_a, glu_b, w_out_full, tm, "merge_fwd")
    h3, hn2, a2, b2 = ffn_forward(h2, ffn2_norm, *ffn2_w, tm, FF_FWD_COLS, "ffn2_fwd")
    dh3, loss_part, g_final = final_loss_backward(h3, target, final_g, seq, tm, "loss_bwd")
    loss = lax.psum(loss_part[0, 0], ("x", "y", "c"))

    def blocked_ffn(d_w1t, d_w3t, d_w2):
        by_rows = lambda t: t.reshape(N_DEV, FF_BLK, D_MODEL)
        return by_rows(d_w1t).transpose(0, 2, 1), by_rows(d_w3t).transpose(0, 2, 1), by_rows(d_w2)

    da2, db2 = ffn_backward_hidden(dh3, a2, b2, ffn2_w[2], tm, "ffn2_bwd_hid")
    dh2, g_ffn2_norm = ffn_backward_input(dh3, h2, ffn2_norm, da2, db2, ffn2_w[0], ffn2_w[1], tm, "ffn2_bwd_in")
    dw = {}
    dw['ffn2_w1'], dw['ffn2_w3'], dw['ffn2_w2'] = blocked_ffn(
        *ffn_backward_weights(hn2, dh3, a2, b2, da2, db2, tm, FF_BWD_COLS, "ffn2_bwd_w"))
    dattn, dyraw, dproj, d_wo, d_ga, d_gb = merge_backward(dh2, yraw, attn, proj, glu_a, glu_b, w_out_full,
                                                           _row_tile(t_rows, 256), "merge_bwd")
    dproj3 = dproj.reshape(n_b, n_l, IN_WIDTH)
    dproj3, dsink_p = attention_backward(proj3, dattn.reshape(n_b, n_l, D_MODEL), dproj3, attn_sinks, seq, "attn_bwd")
    dproj3, gs3, dlr_p, dli_p = ssm_backward_scan(
        dyraw.reshape(n_b, n_l, SSM_WIDTH), xs3, dproj3, c_comb_t, tabr, b_comb_t, ssm_d, seq, "ssm_bwd")
    dproj = dproj3.reshape(t_rows, IN_WIDTH)
    d_bd, d_cd, g_d = ssm_param_grads(proj, gs3.reshape(t_rows, 2 * N_STATES), xs3.reshape(t_rows, 2 * N_STATES),
                                      dyraw, tm, "ssm_bwd_w")
    dh1, g_mix_norm = mix_backward_act(dh2, h1, mix_norm, dproj, wing, tm, "mix_bwd_act")
    dw['w_in'] = mix_backward_weights(hnm, dproj, tm, "mix_bwd_w")
    da1, db1 = ffn_backward_hidden(dh1, a1, b1, ffn1_w[2], tm, "ffn1_bwd_hid")
    dh0, g_ffn1_norm = ffn_backward_input(dh1, h0, ffn1_norm, da1, db1, ffn1_w[0], ffn1_w[1], tm, "ffn1_bwd_in")
    dw['ffn1_w1'], dw['ffn1_w3'], dw['ffn1_w2'] = blocked_ffn(
        *ffn_backward_weights(hn1, dh1, a1, b1, da1, db1, tm, FF_BWD_COLS, "ffn1_bwd_w"))
    dh0_3 = dh0.reshape(n_b, n_l, D_MODEL)
    grad_x = dh0_3[:, :seq]
    g_meta = sum_leading(dh0_3[:, seq:], "meta_sum")

    def blocked_cols(full):
        r = full.shape[0]
        return full.reshape(r, N_DEV, full.shape[1] // N_DEV).transpose(1, 0, 2).astype(BF16)

    dw['ssm_glu_a'] = blocked_cols(d_ga)
    dw['ssm_glu_b'] = blocked_cols(d_gb)
    dw['w_out'] = d_wo.reshape(N_DEV, D_MODEL // N_DEV, D_MODEL).astype(BF16)

    d_b_re, d_b_im = _from_scan_order(d_bd)
    d_c_re, d_c_im = _from_scan_order(d_cd.T)
    dbbr = _diag_blocks(d_b_re, SSM_GROUP, SSM_STATE).transpose(1, 0, 2).reshape(SSM_GROUP, N_STATES)
    dbbi = _diag_blocks(d_b_im, SSM_GROUP, SSM_STATE).transpose(1, 0, 2).reshape(SSM_GROUP, N_STATES)
    g_c_re = _diag_blocks(d_c_re, SSM_GROUP, SSM_STATE)[None]
    g_c_im = -_diag_blocks(d_c_im, SSM_GROUP, SSM_STATE)[None]
    group_sum = (jnp.arange(N_STATES)[:, None] // SSM_STATE == jnp.arange(LANES)[None, :]).astype(F32)
    g_ar, g_ai, g_ls, g_br, g_bi = ssm_param_backward(
        ar, ai, ls, br_t, bi_t, dlr_p.reshape(n_b * SUBLANES, N_STATES), dli_p.reshape(n_b * SUBLANES, N_STATES),
        dbbr, dbbi, group_sum, "ssm_bwd_params")
    g_sinks = sum_leading(dsink_p, "sink_sum")[0:1, :N_KV_HEADS * Q_PER_KV]

    small = {
        'ffn1_norm': g_ffn1_norm, 'mix_norm': g_mix_norm, 'ffn2_norm': g_ffn2_norm, 'final_norm': g_final.reshape(D_MODEL),
        'attn_sinks': g_sinks, 'ssm_a_re': g_ar.reshape(1, SSM_GROUPS, SSM_STATE), 'ssm_a_im': g_ai.reshape(1, SSM_GROUPS, SSM_STATE),
        'ssm_log_step': g_ls[:, :SSM_GROUPS],
        'ssm_b_re': g_br.reshape(SSM_GROUP, SSM_GROUPS, SSM_STATE).transpose(1, 2, 0)[None],
        'ssm_b_im': g_bi.reshape(SSM_GROUP, SSM_GROUPS, SSM_STATE).transpose(1, 2, 0)[None],
        'ssm_c_re': g_c_re, 'ssm_c_im': g_c_im, 'ssm_d': g_d,
    }

    zeros_meta = jnp.zeros((N_META, D_MODEL), F32)
    packed_g = _pack([small[n] for n in REPLICATED] + [g_meta])
    (parts,) = all_gather_list([packed_g], "ag_small_grads")
    packed_out = adamw_small(parts, _pack([w[n] for n in REPLICATED] + [zeros_meta]),
                             _pack([m[n] for n in REPLICATED] + [zeros_meta]),
                             _pack([v[n] for n in REPLICATED] + [zeros_meta]), "adamw_small")
    shapes = [w[n].shape for n in REPLICATED] + [(N_META, D_MODEL)]
    grads, deltas, new_m, new_v = {}, {}, {}, {}
    unpacked = [_unpack(p, shapes) for p in packed_out]
    for k, n in enumerate(REPLICATED):
        grads[n], deltas[n], new_m[n], new_v[n] = (u[k] for u in unpacked)
    g_meta_full = unpacked[0][-1]
    grads['meta_tokens'] = lax.dynamic_index_in_dim(
        g_meta_full.reshape(N_META, N_DEV, D_MODEL // N_DEV), me, axis=1, keepdims=False)
    deltas['meta_tokens'], new_m['meta_tokens'], new_v['meta_tokens'] = adamw_plain(
        grads['meta_tokens'], w['meta_tokens'], m['meta_tokens'], v['meta_tokens'], "adamw_meta")

    g_list = [dw[n] for n in SHARDED]
    r1 = rs_sibling_swap(g_list, "rs_sibling")
    pairs = [pair_sums(dest, g, r, "rs_pair_" + n) for n, g, r in zip(SHARDED, g_list, r1)]
    r2 = rs_chip_exchange(pairs, "rs_chips")
    for n, g, ra, rb in zip(SHARDED, g_list, r1, r2):
        shape = w[n].shape
        two_d = lambda t: t.reshape(shape[1:])
        out = adamw_sharded(dest, g, ra, rb, two_d(w[n]), two_d(m[n]), two_d(v[n]), "adamw_" + n)
        grads[n], deltas[n], new_m[n], new_v[n] = (o.reshape(shape) for o in out)

    return (loss, grad_x, *[grads[n] for n in WEIGHTS], *[deltas[n] for n in WEIGHTS],
            *[new_m[n] for n in WEIGHTS], *[new_v[n] for n in WEIGHTS])
```

```python
import functools

import jax
import jax.numpy as jnp
from jax import lax
from jax.experimental import pallas as pl
from jax.experimental.pallas import tpu as pltpu

F32 = jnp.float32
BF16 = jnp.bfloat16
MESH = pl.DeviceIdType.MESH

N_DEV = 8
D_MODEL = 1024
N_META = 16
HEAD_DIM = 64
N_KV_HEADS = 4
Q_PER_KV = 4
BLOCK = 128
KV_WIDTH = N_KV_HEADS * HEAD_DIM
SSM_GROUP = 16
SSM_WIDTH = 512
SSM_GROUPS = 32
SSM_STATE = 64
N_STATES = SSM_GROUPS * SSM_STATE
D_FF = 2816
FF_BLK = D_FF // N_DEV
IN_WIDTH = 4096
IN_BLK = IN_WIDTH // N_DEV
NORM_EPS = 1e-6
NEG_INF = -1e30
SCAN_COLS = 256
N_SCAN_BLK = N_STATES // SCAN_COLS
SUBLANES = 8
LANES = 128
MXU_WIDTH = 256
FF_FWD_COLS = D_FF // 2
FF_BWD_COLS = MXU_WIDTH

ADAM_LR = 0.001
ADAM_B1 = 0.9
ADAM_B2 = 0.999
ADAM_EPS = 1e-08
ADAM_WD = 0.01
ADAM_STEP = 10

VMEM_BIG = 56 * 1024 * 1024


def _cp(sem=None, vmem=None):
    kw = {}
    if sem is not None:
        kw["dimension_semantics"] = sem
    if vmem is not None:
        kw["vmem_limit_bytes"] = vmem
    return pltpu.CompilerParams(**kw)


def _pcall(body, **kw):
    return pl.pallas_call(body, **kw)


def _dot(a, b):
    return jnp.dot(a, b, preferred_element_type=F32)


def _dot_nt(a, b):
    return lax.dot_general(a, b, (((1,), (1,)), ((), ())), preferred_element_type=F32)


def _dot_tn(a, b):
    return lax.dot_general(a, b, (((0,), (0,)), ((), ())), preferred_element_type=F32)


def _sigmoid(x):
    return 1.0 / (1.0 + jnp.exp(-x))


def _row_tile(rows, cap):
    best = None
    for t in range(16, min(rows, cap) + 1, 16):
        if rows % t == 0:
            best = t
    assert best is not None, rows
    return best


def _my_place():
    return lax.axis_index("x"), lax.axis_index("y"), lax.axis_index("c")


def all_gather_list(shards, name):
    n = len(shards)

    def body(*refs):
        ins, outs = refs[:n], refs[n:2 * n]
        send_sems, recv_sems, local_sems = refs[2 * n:]
        x, y, c = _my_place()
        me, sibling = (x, y, c), (x, y, 1 - c)
        chips = [(1 - x, y), (x, 1 - y), (1 - x, 1 - y)]

        def blk(a, px, py, pc):
            return outs[a].at[4 * px + 2 * py + pc]

        def copy(a, k, block, to, src=None):
            return pltpu.make_async_remote_copy(
                src_ref=blk(a, *block) if src is None else src, dst_ref=blk(a, *block),
                send_sem=send_sems.at[a * 7 + k], recv_sem=recv_sems.at[a * 7 + k],
                device_id=to, device_id_type=MESH)

        mine = [pltpu.make_async_copy(ins[a], blk(a, *me), local_sems.at[a]) for a in range(n)]
        for cp in mine:
            cp.start()
        first = []
        for a in range(n):
            first.append(copy(a, 0, me, sibling, src=ins[a]))
            first += [copy(a, 1 + j, me, (*chip, c), src=ins[a]) for j, chip in enumerate(chips)]
        for cp in first:
            cp.start()
        passed = []
        for j, chip in enumerate(chips):
            for a in range(n):
                copy(a, 1 + j, (*chip, c), me).wait_recv()
                cp = copy(a, 4 + j, (*chip, c), sibling)
                cp.start()
                passed.append(cp)
        for a in range(n):
            copy(a, 0, sibling, me).wait_recv()
            for j, chip in enumerate(chips):
                copy(a, 4 + j, (*chip, 1 - c), me).wait_recv()
        for cp in first + passed:
            cp.wait_send()
        for cp in mine:
            cp.wait()

    any_spec = pl.BlockSpec(memory_space=pl.ANY)
    return _pcall(
        body, name=name,
        out_shape=[jax.ShapeDtypeStruct((N_DEV,) + s.shape, s.dtype) for s in shards],
        in_specs=[any_spec] * n, out_specs=[any_spec] * n,
        scratch_shapes=[pltpu.SemaphoreType.DMA((7 * n,)), pltpu.SemaphoreType.DMA((7 * n,)),
                        pltpu.SemaphoreType.DMA((n,))],
    )(*shards)


def rs_sibling_swap(grads, name):
    n = len(grads)

    def body(*refs):
        ins, outs = refs[:n], refs[n:2 * n]
        send_sems, recv_sems = refs[2 * n:]
        x, y, c = _my_place()
        chips = [(x, y), (1 - x, y), (x, 1 - y), (1 - x, 1 - y)]
        copies = []
        for a in range(n):
            for k, (px, py) in enumerate(chips):
                copies.append(pltpu.make_async_remote_copy(
                    src_ref=ins[a].at[4 * px + 2 * py + (1 - c)], dst_ref=outs[a].at[k],
                    send_sem=send_sems.at[4 * a + k], recv_sem=recv_sems.at[4 * a + k],
                    device_id=(x, y, 1 - c), device_id_type=MESH))
        for cp in copies:
            cp.start()
        for cp in copies:
            cp.wait()

    any_spec = pl.BlockSpec(memory_space=pl.ANY)
    return _pcall(
        body, name=name,
        out_shape=[jax.ShapeDtypeStruct((4,) + g.shape[1:], g.dtype) for g in grads],
        in_specs=[any_spec] * n, out_specs=[any_spec] * n,
        scratch_shapes=[pltpu.SemaphoreType.DMA((4 * n,)), pltpu.SemaphoreType.DMA((4 * n,))],
    )(*grads)


def rs_chip_exchange(parts, name):
    n = len(parts)

    def body(*refs):
        ins, outs = refs[:n], refs[n:2 * n]
        send_sems, recv_sems = refs[2 * n:]
        x, y, c = _my_place()
        chips = [(1 - x, y), (x, 1 - y), (1 - x, 1 - y)]
        copies = []
        for a in range(n):
            for k, (px, py) in enumerate(chips):
                copies.append(pltpu.make_async_remote_copy(
                    src_ref=ins[a].at[k], dst_ref=outs[a].at[k],
                    send_sem=send_sems.at[3 * a + k], recv_sem=recv_sems.at[3 * a + k],
                    device_id=(px, py, c), device_id_type=MESH))
        for cp in copies:
            cp.start()
        for cp in copies:
            cp.wait()

    any_spec = pl.BlockSpec(memory_space=pl.ANY)
    return _pcall(
        body, name=name,
        out_shape=[jax.ShapeDtypeStruct(p.shape, p.dtype) for p in parts],
        in_specs=[any_spec] * n, out_specs=[any_spec] * n,
        scratch_shapes=[pltpu.SemaphoreType.DMA((3 * n,)), pltpu.SemaphoreType.DMA((3 * n,))],
    )(*parts)


def pair_sums(idx, g, r1, name):
    _, rows, cols = g.shape
    tr = _row_tile(rows, 256)

    def body(idx_ref, g_ref, r_ref, o_ref):
        o_ref[...] = (g_ref[...].astype(F32) + r_ref[...].astype(F32)).astype(BF16)

    return _pcall(
        body, name=name,
        out_shape=jax.ShapeDtypeStruct((3, rows, cols), BF16),
        grid_spec=pltpu.PrefetchScalarGridSpec(
            num_scalar_prefetch=1, grid=(3, rows // tr),
            in_specs=[pl.BlockSpec((None, tr, cols), lambda k, r, ix: (ix[k + 1], r, 0)),
                      pl.BlockSpec((None, tr, cols), lambda k, r, ix: (k + 1, r, 0))],
            out_specs=pl.BlockSpec((None, tr, cols), lambda k, r, ix: (k, r, 0))),
        compiler_params=_cp(("arbitrary", "arbitrary")),
    )(idx, g, r1)


def _adam_math(w, g, m, v):
    m = ADAM_B1 * m + (1.0 - ADAM_B1) * g
    v = ADAM_B2 * v + (1.0 - ADAM_B2) * (g * g)
    m_hat = m / (1.0 - ADAM_B1 ** ADAM_STEP)
    v_hat = v / (1.0 - ADAM_B2 ** ADAM_STEP)
    delta = -ADAM_LR * (m_hat / (jnp.sqrt(v_hat) + ADAM_EPS) + ADAM_WD * w)
    return delta, m, v


def adamw_sharded(idx, g, r1, r2, w, m, v, name):
    rows, cols = w.shape
    tr = _row_tile(rows, 256)

    def body(idx_ref, g_ref, r1_ref, r2_ref, w_ref, m_ref, v_ref, go_ref, d_ref, mo_ref, vo_ref):
        grad = g_ref[...].astype(F32) + r1_ref[...].astype(F32)
        for k in range(3):
            grad = grad + r2_ref[k].astype(F32)
        delta, m_new, v_new = _adam_math(w_ref[...], grad, m_ref[...], v_ref[...])
        go_ref[...] = grad
        d_ref[...] = delta
        mo_ref[...] = m_new
        vo_ref[...] = v_new

    tile = pl.BlockSpec((tr, cols), lambda r, ix: (r, 0))
    out = jax.ShapeDtypeStruct((rows, cols), F32)
    return _pcall(
        body, name=name, out_shape=[out] * 4,
        grid_spec=pltpu.PrefetchScalarGridSpec(
            num_scalar_prefetch=1, grid=(rows // tr,),
            in_specs=[pl.BlockSpec((None, tr, cols), lambda r, ix: (ix[0], r, 0)),
                      pl.BlockSpec((None, tr, cols), lambda r, ix: (0, r, 0)),
                      pl.BlockSpec((3, tr, cols), lambda r, ix: (0, r, 0)),
                      tile, tile, tile],
            out_specs=[tile] * 4),
        compiler_params=_cp(("arbitrary",)),
    )(idx, g, r1, r2, w, m, v)


def adamw_small(parts, w, m, v, name):
    _, rows, cols = parts.shape

    def body(p_ref, w_ref, m_ref, v_ref, go_ref, d_ref, mo_ref, vo_ref):
        grad = p_ref[0]
        for k in range(1, N_DEV):
            grad = grad + p_ref[k]
        delta, m_new, v_new = _adam_math(w_ref[...], grad, m_ref[...], v_ref[...])
        go_ref[...] = grad
        d_ref[...] = delta
        mo_ref[...] = m_new
        vo_ref[...] = v_new

    out = jax.ShapeDtypeStruct((rows, cols), F32)
    return _pcall(body, name=name, out_shape=[out] * 4, compiler_params=_cp(vmem=VMEM_BIG))(parts, w, m, v)


def adamw_plain(g, w, m, v, name):
    def body(g_ref, w_ref, m_ref, v_ref, d_ref, mo_ref, vo_ref):
        delta, m_new, v_new = _adam_math(w_ref[...], g_ref[...], m_ref[...], v_ref[...])
        d_ref[...] = delta
        mo_ref[...] = m_new
        vo_ref[...] = v_new

    out = jax.ShapeDtypeStruct(w.shape, F32)
    return _pcall(body, name=name, out_shape=[out] * 3)(g, w, m, v)


def _rms_fwd(x, g):
    r = lax.rsqrt(jnp.mean(x * x, axis=-1, keepdims=True) + NORM_EPS)
    return x * r * g


def _rms_bwd(x, g, dy):
    r = lax.rsqrt(jnp.mean(x * x, axis=-1, keepdims=True) + NORM_EPS)
    xh = x * r
    t = dy * g
    dx = r * (t - xh * jnp.mean(t * xh, axis=-1, keepdims=True))
    return dx, jnp.sum(dy * xh, axis=0, keepdims=True)


def _accumulate(ref, val, first):
    @pl.when(first)
    def _():
        ref[...] = val

    @pl.when(jnp.logical_not(first))
    def _():
        ref[...] += val


def _col_chunks(width):
    return [(c0, min(MXU_WIDTH, width - c0)) for c0 in range(0, width, MXU_WIDTH)]


def ffn_forward(h, norm, w1, w3, w2, tm, tn, name):
    t_rows = h.shape[0]
    nj = D_FF // tn

    def body(h_ref, g_ref, w1_ref, w3_ref, w2_ref, out_ref, hn_ref, a_ref, b_ref, acc_ref):
        j = pl.program_id(1)

        @pl.when(j == 0)
        def _():
            hn_ref[...] = _rms_fwd(h_ref[...], g_ref[...]).astype(BF16)
            acc_ref[...] = jnp.zeros_like(acc_ref)

        hn = hn_ref[...]
        for c0, cw in _col_chunks(tn):
            a = _dot(hn, w1_ref[:, c0:c0 + cw])
            b = _dot(hn, w3_ref[:, c0:c0 + cw])
            a_ref[:, c0:c0 + cw] = a.astype(BF16)
            b_ref[:, c0:c0 + cw] = b.astype(BF16)
            hid = (a * _sigmoid(a) * b).astype(BF16)
            acc_ref[...] += _dot(hid, w2_ref[c0:c0 + cw, :])

        @pl.when(j == nj - 1)
        def _():
            out_ref[...] = h_ref[...] + 0.5 * acc_ref[...]

    row = pl.BlockSpec((tm, D_MODEL), lambda i, j: (i, 0))
    hid_blk = pl.BlockSpec((tm, tn), lambda i, j: (i, j))
    w_col = pl.BlockSpec((D_MODEL, tn), lambda i, j: (0, j))
    return _pcall(
        body, name=name, grid=(t_rows // tm, nj),
        in_specs=[row, pl.BlockSpec((1, D_MODEL), lambda i, j: (0, 0)), w_col, w_col,
                  pl.BlockSpec((tn, D_MODEL), lambda i, j: (j, 0))],
        out_specs=[row, row, hid_blk, hid_blk],
        out_shape=[jax.ShapeDtypeStruct((t_rows, D_MODEL), F32), jax.ShapeDtypeStruct((t_rows, D_MODEL), BF16),
                   jax.ShapeDtypeStruct((t_rows, D_FF), BF16), jax.ShapeDtypeStruct((t_rows, D_FF), BF16)],
        scratch_shapes=[pltpu.VMEM((tm, D_MODEL), F32)],
        compiler_params=_cp(("arbitrary", "arbitrary"), VMEM_BIG),
    )(h, norm, w1, w3, w2)


def _resident(shape):
    return pl.BlockSpec(shape, lambda *_: (0,) * len(shape), pipeline_mode=pl.Buffered(1))


def ffn_backward_hidden(dh, a, b, w2, tm, name):
    t_rows = dh.shape[0]

    def body(dh_ref, a_ref, b_ref, w2_ref, da_ref, db_ref, dhb_ref):
        dhb = (0.5 * dh_ref[...]).astype(BF16)
        dhb_ref[...] = dhb
        for c0, cw in _col_chunks(D_FF):
            dhid = _dot_nt(dhb, w2_ref[c0:c0 + cw, :])
            av = a_ref[:, c0:c0 + cw].astype(F32)
            bv = b_ref[:, c0:c0 + cw].astype(F32)
            s = _sigmoid(av)
            da_ref[:, c0:c0 + cw] = (dhid * bv * (s * (1.0 + av * (1.0 - s)))).astype(BF16)
            db_ref[:, c0:c0 + cw] = (dhid * (av * s)).astype(BF16)

    hid = pl.BlockSpec((tm, D_FF), lambda i: (i, 0))
    row = pl.BlockSpec((tm, D_MODEL), lambda i: (i, 0))
    return _pcall(
        body, name=name, grid=(t_rows // tm,),
        in_specs=[row, hid, hid, _resident((D_FF, D_MODEL))],
        out_specs=[hid, hid, row],
        out_shape=[jax.ShapeDtypeStruct((t_rows, D_FF), BF16), jax.ShapeDtypeStruct((t_rows, D_FF), BF16),
                   jax.ShapeDtypeStruct((t_rows, D_MODEL), BF16)],
        compiler_params=_cp(("arbitrary",), VMEM_BIG),
    )(dh, a, b, w2)


def ffn_backward_input(dh, h, norm, da, db, w1, w3, tm, name):
    t_rows = h.shape[0]

    def body(dh_ref, h_ref, g_ref, da_ref, db_ref, w1_ref, w3_ref, dhin_ref, dg_ref):
        dhn = _dot_nt(da_ref[...], w1_ref[...]) + _dot_nt(db_ref[...], w3_ref[...])
        dx, dg = _rms_bwd(h_ref[...], g_ref[...], dhn)
        dhin_ref[...] = dh_ref[...] + dx
        _accumulate(dg_ref, dg, pl.program_id(0) == 0)

    row = pl.BlockSpec((tm, D_MODEL), lambda i: (i, 0))
    vec = pl.BlockSpec((1, D_MODEL), lambda i: (0, 0))
    hid = pl.BlockSpec((tm, D_FF), lambda i: (i, 0))
    return _pcall(
        body, name=name, grid=(t_rows // tm,),
        in_specs=[row, row, vec, hid, hid, _resident((D_MODEL, D_FF)), _resident((D_MODEL, D_FF))],
        out_specs=[row, vec],
        out_shape=[jax.ShapeDtypeStruct((t_rows, D_MODEL), F32), jax.ShapeDtypeStruct((1, D_MODEL), F32)],
        compiler_params=_cp(("arbitrary",), VMEM_BIG),
    )(dh, h, norm, da, db, w1, w3)


def ffn_backward_weights(hn, dh, a, b, da, db, tm, tn, name):
    t_rows = hn.shape[0]
    ni = t_rows // tm

    def body(hn_ref, dh_ref, a_ref, b_ref, da_ref, db_ref, dw1_ref, dw3_ref, dw2_ref, acc1, acc3, acc2):
        i = pl.program_id(1)
        hn_v = hn_ref[...]
        _accumulate(acc1, _dot_tn(hn_v, da_ref[...]), i == 0)
        _accumulate(acc3, _dot_tn(hn_v, db_ref[...]), i == 0)
        av = a_ref[...].astype(F32)
        hid = (av * _sigmoid(av) * b_ref[...].astype(F32)).astype(BF16)
        _accumulate(acc2, _dot_tn(hid, dh_ref[...]), i == 0)

        @pl.when(i == ni - 1)
        def _():
            dw1_ref[...] = acc1[...].astype(BF16)
            dw3_ref[...] = acc3[...].astype(BF16)
            dw2_ref[...] = acc2[...].astype(BF16)

    row = pl.BlockSpec((tm, D_MODEL), lambda j, i: (i, 0))
    hid_blk = pl.BlockSpec((tm, tn), lambda j, i: (i, j))
    w_col = pl.BlockSpec((D_MODEL, tn), lambda j, i: (0, j))
    w_row = pl.BlockSpec((tn, D_MODEL), lambda j, i: (j, 0))
    return _pcall(
        body, name=name, grid=(D_FF // tn, ni),
        in_specs=[row, row, hid_blk, hid_blk, hid_blk, hid_blk],
        out_specs=[w_col, w_col, w_row],
        out_shape=[jax.ShapeDtypeStruct((D_MODEL, D_FF), BF16), jax.ShapeDtypeStruct((D_MODEL, D_FF), BF16),
                   jax.ShapeDtypeStruct((D_FF, D_MODEL), BF16)],
        scratch_shapes=[pltpu.VMEM((D_MODEL, tn), F32), pltpu.VMEM((D_MODEL, tn), F32), pltpu.VMEM((tn, D_MODEL), F32)],
        compiler_params=_cp(("arbitrary", "arbitrary"), VMEM_BIG),
    )(hn, dh, a, b, da, db)


def mix_forward(h, norm, wing, tm, name):
    t_rows = h.shape[0]

    def body(h_ref, g_ref, w_ref, hn_ref, p_ref):
        hn = _rms_fwd(h_ref[...], g_ref[...]).astype(BF16)
        hn_ref[...] = hn
        for j in range(N_DEV):
            p_ref[:, j * IN_BLK:(j + 1) * IN_BLK] = _dot(hn, w_ref[j]).astype(BF16)

    row = pl.BlockSpec((tm, D_MODEL), lambda i: (i, 0))
    return _pcall(
        body, name=name, grid=(t_rows // tm,),
        in_specs=[row, pl.BlockSpec((1, D_MODEL), lambda i: (0, 0)),
                  pl.BlockSpec((N_DEV, D_MODEL, IN_BLK), lambda i: (0, 0, 0))],
        out_specs=[row, pl.BlockSpec((tm, IN_WIDTH), lambda i: (i, 0))],
        out_shape=[jax.ShapeDtypeStruct((t_rows, D_MODEL), BF16), jax.ShapeDtypeStruct((t_rows, IN_WIDTH), BF16)],
        compiler_params=_cp(("arbitrary",), VMEM_BIG),
    )(h, norm, wing)


def mix_backward_act(dh, h, norm, dproj, wing, tm, name):
    t_rows = h.shape[0]
    per_step = 4
    nj = N_DEV // per_step

    def body(dh_ref, h_ref, g_ref, dp_ref, w_ref, dhin_ref, dg_ref, acc_ref):
        i, j = pl.program_id(0), pl.program_id(1)
        part = functools.reduce(
            lambda u, w: u + w, [_dot_nt(dp_ref[:, k * IN_BLK:(k + 1) * IN_BLK], w_ref[k]) for k in range(per_step)])
        _accumulate(acc_ref, part, j == 0)

        @pl.when(j == nj - 1)
        def _():
            dx, dg = _rms_bwd(h_ref[...], g_ref[...], acc_ref[...])
            dhin_ref[...] = dh_ref[...] + dx
            _accumulate(dg_ref, dg, i == 0)

    row = pl.BlockSpec((tm, D_MODEL), lambda i, j: (i, 0))
    vec = pl.BlockSpec((1, D_MODEL), lambda i, j: (0, 0))
    return _pcall(
        body, name=name, grid=(t_rows // tm, nj),
        in_specs=[row, row, vec, pl.BlockSpec((tm, per_step * IN_BLK), lambda i, j: (i, j)),
                  pl.BlockSpec((per_step, D_MODEL, IN_BLK), lambda i, j: (j, 0, 0))],
        out_specs=[row, vec],
        out_shape=[jax.ShapeDtypeStruct((t_rows, D_MODEL), F32), jax.ShapeDtypeStruct((1, D_MODEL), F32)],
        scratch_shapes=[pltpu.VMEM((tm, D_MODEL), F32)],
        compiler_params=_cp(("arbitrary", "arbitrary"), VMEM_BIG),
    )(dh, h, norm, dproj, wing)


def mix_backward_weights(hn, dproj, tm, name):
    t_rows = hn.shape[0]
    ni = t_rows // tm
    per_step = 2

    def body(hn_ref, dp_ref, dw_ref, acc):
        i = pl.program_id(1)
        _accumulate(acc, _dot_tn(hn_ref[...], dp_ref[...]), i == 0)

        @pl.when(i == ni - 1)
        def _():
            for k in range(per_step):
                dw_ref[k] = acc[:, k * IN_BLK:(k + 1) * IN_BLK].astype(BF16)

    return _pcall(
        body, name=name, grid=(N_DEV // per_step, ni),
        in_specs=[pl.BlockSpec((tm, D_MODEL), lambda j, i: (i, 0)),
                  pl.BlockSpec((tm, per_step * IN_BLK), lambda j, i: (i, j))],
        out_specs=pl.BlockSpec((per_step, D_MODEL, IN_BLK), lambda j, i: (j, 0, 0)),
        out_shape=jax.ShapeDtypeStruct((N_DEV, D_MODEL, IN_BLK), BF16),
        scratch_shapes=[pltpu.VMEM((D_MODEL, per_step * IN_BLK), F32)],
        compiler_params=_cp(("arbitrary", "arbitrary"), VMEM_BIG),
    )(hn, dproj)


GELU_C = 0.7978845608028654
GELU_K = 0.044715


def _gelu(x):
    return 0.5 * x * (1.0 + jnp.tanh(GELU_C * (x + GELU_K * (x * x * x))))


def _gelu_and_grad(x):
    th = jnp.tanh(GELU_C * (x + GELU_K * (x * x * x)))
    val = 0.5 * x * (1.0 + th)
    grad = 0.5 * (1.0 + th) + 0.5 * x * (1.0 - th * th) * (GELU_C * (1.0 + 3.0 * GELU_K * (x * x)))
    return val, grad


def merge_forward(h, yraw, attn, proj, glu_a, glu_b, w_out, tm, name):
    t_rows = h.shape[0]

    def body(h_ref, y_ref, at_ref, gate_ref, a_ref, b_ref, wo_ref, out_ref):
        y = _gelu(y_ref[...]).astype(BF16)
        ssm = _dot(y, a_ref[...]) * _sigmoid(_dot(y, b_ref[...]))
        ga = gate_ref[:, :D_MODEL].astype(F32)
        gs = gate_ref[:, D_MODEL:].astype(F32)
        merged = _sigmoid(ga) * at_ref[...].astype(F32) + _sigmoid(gs) * ssm
        out_ref[...] = h_ref[...] + _dot(merged.astype(BF16), wo_ref[...])

    row = pl.BlockSpec((tm, D_MODEL), lambda i: (i, 0))
    glu = pl.BlockSpec((SSM_WIDTH, D_MODEL), lambda i: (0, 0))
    return _pcall(
        body, name=name, grid=(t_rows // tm,),
        in_specs=[row, pl.BlockSpec((tm, SSM_WIDTH), lambda i: (i, 0)), row,
                  pl.BlockSpec((tm, 2 * D_MODEL), lambda i: (i, 1)), glu, glu,
                  pl.BlockSpec((D_MODEL, D_MODEL), lambda i: (0, 0))],
        out_specs=row, out_shape=jax.ShapeDtypeStruct((t_rows, D_MODEL), F32),
        compiler_params=_cp(("arbitrary",), VMEM_BIG),
    )(h, yraw, attn, proj, glu_a, glu_b, w_out)


def merge_backward(dh, yraw, attn, proj, glu_a, glu_b, w_out, tm, name):
    t_rows = dh.shape[0]

    def body(dh_ref, y_ref, at_ref, gate_ref, a_ref, b_ref, wo_ref,
             dat_ref, dy_ref, dgate_ref, d16_ref, mg_ref, y16_ref, dya_ref, dyb_ref):
        d16 = dh_ref[...].astype(BF16)
        d16_ref[...] = d16
        gel, dgel = _gelu_and_grad(y_ref[...].astype(F32))
        y16 = gel.astype(BF16)
        y16_ref[...] = y16
        dy = None
        for c0, cw in _col_chunks(D_MODEL):
            cols = slice(c0, c0 + cw)
            gcols = slice(D_MODEL + c0, D_MODEL + c0 + cw)
            dmerged = _dot_nt(d16, wo_ref[cols, :])
            ya = _dot(y16, a_ref[:, cols])
            sb = _sigmoid(_dot(y16, b_ref[:, cols]))
            ssm = ya * sb
            sa = _sigmoid(gate_ref[:, cols].astype(F32))
            ss = _sigmoid(gate_ref[:, gcols].astype(F32))
            attn_v = at_ref[:, cols].astype(F32)
            mg_ref[:, cols] = (sa * attn_v + ss * ssm).astype(BF16)
            dat_ref[:, cols] = (dmerged * sa).astype(BF16)
            dgate_ref[:, cols] = (dmerged * attn_v * sa * (1.0 - sa)).astype(BF16)
            dgate_ref[:, gcols] = (dmerged * ssm * ss * (1.0 - ss)).astype(BF16)
            dssm = dmerged * ss
            dya = (dssm * sb).astype(BF16)
            dyb = (dssm * ya * sb * (1.0 - sb)).astype(BF16)
            dya_ref[:, cols] = dya
            dyb_ref[:, cols] = dyb
            part = _dot_nt(dya, a_ref[:, cols]) + _dot_nt(dyb, b_ref[:, cols])
            dy = part if dy is None else dy + part
        dy_ref[...] = (dy * dgel).astype(BF16)

    row = pl.BlockSpec((tm, D_MODEL), lambda i: (i, 0))
    ssm_row = pl.BlockSpec((tm, SSM_WIDTH), lambda i: (i, 0))
    gates = pl.BlockSpec((tm, 2 * D_MODEL), lambda i: (i, 1))
    wide = jax.ShapeDtypeStruct((t_rows, D_MODEL), BF16)
    narrow = jax.ShapeDtypeStruct((t_rows, SSM_WIDTH), BF16)
    return _pcall(
        body, name=name, grid=(t_rows // tm,),
        in_specs=[row, ssm_row, row, gates, _resident((SSM_WIDTH, D_MODEL)), _resident((SSM_WIDTH, D_MODEL)),
                  _resident((D_MODEL, D_MODEL))],
        out_specs=[row, ssm_row, gates, row, row, ssm_row, row, row],
        out_shape=[wide, narrow, jax.ShapeDtypeStruct((t_rows, IN_WIDTH), BF16), wide, wide, narrow, wide, wide],
        compiler_params=_cp(("arbitrary",), VMEM_BIG),
    )(dh, yraw, attn, proj, glu_a, glu_b, w_out)


def merge_backward_weights(d16, merged, y16, dya, dyb, tm, name):
    t_rows = d16.shape[0]

    def body(d_ref, mg_ref, y_ref, dya_ref, dyb_ref, dwo_ref, da_ref, db_ref):
        first = pl.program_id(0) == 0
        y16 = y_ref[...]
        _accumulate(dwo_ref, _dot_tn(mg_ref[...], d_ref[...]), first)
        _accumulate(da_ref, _dot_tn(y16, dya_ref[...]), first)
        _accumulate(db_ref, _dot_tn(y16, dyb_ref[...]), first)

    row = pl.BlockSpec((tm, D_MODEL), lambda i: (i, 0))
    ssm_row = pl.BlockSpec((tm, SSM_WIDTH), lambda i: (i, 0))
    glu = pl.BlockSpec((SSM_WIDTH, D_MODEL), lambda i: (0, 0))
    wo = pl.BlockSpec((D_MODEL, D_MODEL), lambda i: (0, 0))
    return _pcall(
        body, name=name, grid=(t_rows // tm,),
        in_specs=[row, row, ssm_row, row, row], out_specs=[wo, glu, glu],
        out_shape=[jax.ShapeDtypeStruct((D_MODEL, D_MODEL), F32), jax.ShapeDtypeStruct((SSM_WIDTH, D_MODEL), F32),
                   jax.ShapeDtypeStruct((SSM_WIDTH, D_MODEL), F32)],
        compiler_params=_cp(("arbitrary",), VMEM_BIG),
    )(d16, merged, y16, dya, dyb)


def final_loss_backward(h, target, norm, seq, tm, name):
    t_rows = h.shape[0]
    tiles_per_example = (seq + N_META) // tm

    def body(h_ref, t_ref, g_ref, dh_ref, loss_ref, dg_ref):
        i = pl.program_id(0)
        x = h_ref[...]
        g = g_ref[...]
        r = lax.rsqrt(jnp.mean(x * x, axis=-1, keepdims=True) + NORM_EPS)
        xh = x * r
        pos = lax.broadcasted_iota(jnp.int32, (tm, 1), 0) + (i % tiles_per_example) * tm
        diff = jnp.where(pos < seq, xh * g - t_ref[...], 0.0)
        part = 0.5 * jnp.sum(jnp.sum(diff * diff, axis=-1, keepdims=True), axis=0, keepdims=True) / D_MODEL
        dy = diff / D_MODEL
        t = dy * g
        dh_ref[...] = r * (t - xh * jnp.mean(t * xh, axis=-1, keepdims=True))
        _accumulate(loss_ref, jnp.broadcast_to(part, (1, LANES)), i == 0)
        _accumulate(dg_ref, jnp.sum(dy * xh, axis=0, keepdims=True), i == 0)

    row = pl.BlockSpec((tm, D_MODEL), lambda i: (i, 0))
    vec = pl.BlockSpec((1, D_MODEL), lambda i: (0, 0))
    return _pcall(
        body, name=name, grid=(t_rows // tm,),
        in_specs=[row, row, vec],
        out_specs=[row, pl.BlockSpec((1, LANES), lambda i: (0, 0)), vec],
        out_shape=[jax.ShapeDtypeStruct((t_rows, D_MODEL), F32), jax.ShapeDtypeStruct((1, LANES), F32),
                   jax.ShapeDtypeStruct((1, D_MODEL), F32)],
        compiler_params=_cp(("arbitrary",), VMEM_BIG),
    )(h, target, norm)


ATTN_SCALE = HEAD_DIM ** -0.5
STACK_HEADS = (0, 2, 1, 3)


def _lane_half(shape, hf):
    lane = lax.broadcasted_iota(jnp.int32, shape, 1)
    return (lane < HEAD_DIM) if hf == 0 else (lane >= HEAD_DIM)


def _kv_variants(ref, rows, kh):
    tile = kh // 2
    t = ref[rows, tile * LANES:(tile + 1) * LANES].astype(F32)
    swapped = pltpu.roll(t, HEAD_DIM, axis=1)
    at_low, at_high = (t, swapped) if kh % 2 == 0 else (swapped, t)
    lo = jnp.where(_lane_half(t.shape, 0), at_low, 0.0).astype(BF16)
    hi = jnp.where(_lane_half(t.shape, 1), at_high, 0.0).astype(BF16)
    return lo, hi


def _to_kv_lanes(lo, hi, kh):
    lo = jnp.where(_lane_half(lo.shape, 0), lo, 0.0)
    hi = jnp.where(_lane_half(hi.shape, 1), hi, 0.0)
    if kh % 2 == 0:
        return lo + pltpu.roll(hi, HEAD_DIM, axis=1)
    return pltpu.roll(lo, HEAD_DIM, axis=1) + hi


def _stacked(ref, rows, kh):
    col = kh * 2 * LANES
    return jnp.concatenate([ref[rows, col:col + LANES], ref[rows, col + LANES:col + 2 * LANES]], axis=0)


def _sink_column(sink_ref, kh, nq):
    row = lax.broadcasted_iota(jnp.int32, (4 * nq, 1), 0)
    col = jnp.zeros((4 * nq, 1), F32)
    for quarter, g in enumerate(STACK_HEADS):
        col = jnp.where(row // nq == quarter, sink_ref[0, kh * Q_PER_KV + g], col)
    return col


def _softmax_parts(qs, key_tiles, masks, sink):
    scores = []
    for (k_lo, k_hi), mask in zip(key_tiles, masks):
        s = jnp.concatenate([_dot_nt(qs, k_lo), _dot_nt(qs, k_hi)], axis=0) * ATTN_SCALE
        scores.append(s if mask is None else jnp.where(mask, s, NEG_INF))
    m = functools.reduce(jnp.maximum, [jnp.max(s, axis=-1, keepdims=True) for s in scores])
    m = jnp.maximum(m, sink)
    probs = [jnp.exp(s - m) for s in scores]
    e_sink = jnp.exp(sink - m)
    den = functools.reduce(lambda u, w: u + w, [jnp.sum(p, axis=-1, keepdims=True) for p in probs]) + e_sink
    return probs, 1.0 / den, e_sink


def _band_mask(nq, first):
    keys = BLOCK if first else 2 * BLOCK
    qi = lax.broadcasted_iota(jnp.int32, (4 * nq, keys), 0) % nq
    kj = lax.broadcasted_iota(jnp.int32, (4 * nq, keys), 1)
    if first:
        return kj <= qi
    return jnp.logical_and(kj > qi, kj <= qi + BLOCK)


def _meta_mask():
    qi = lax.broadcasted_iota(jnp.int32, (4 * N_META, N_META), 0) % N_META
    kj = lax.broadcasted_iota(jnp.int32, (4 * N_META, N_META), 1)
    return kj <= qi


def _attention_schedule(seq, queries, carry):
    meta_rows = pl.ds(seq, N_META)
    carry = queries(pl.ds(0, BLOCK), BLOCK, [pl.ds(0, BLOCK), meta_rows], [_band_mask(BLOCK, True), None], carry)

    def block(n, c):
        r0 = pl.multiple_of(n * BLOCK, BLOCK)
        p0 = pl.multiple_of((n - 1) * BLOCK, BLOCK)
        return queries(pl.ds(r0, BLOCK), BLOCK, [pl.ds(p0, 2 * BLOCK), meta_rows], [_band_mask(BLOCK, False), None], c)

    carry = lax.fori_loop(1, seq // BLOCK, block, carry)
    return queries(meta_rows, N_META, [meta_rows], [_meta_mask()], carry)


def attention_forward(proj3, sinks, seq, name):
    n_b, n_l, _ = proj3.shape

    def body(sink_ref, q_ref, k_ref, v_ref, o_ref):
        def queries(q_rows, nq, key_rows, masks, carry):
            for kh in range(N_KV_HEADS):
                ks = [_kv_variants(k_ref, r, kh) for r in key_rows]
                vs = [_kv_variants(v_ref, r, kh) for r in key_rows]
                qs = _stacked(q_ref, q_rows, kh)
                probs, inv, _ = _softmax_parts(qs, ks, masks, _sink_column(sink_ref, kh, nq))
                probs = [p.astype(BF16) for p in probs]
                o_lo = functools.reduce(lambda u, w: u + w, [_dot(p[:2 * nq], v_lo) for p, (v_lo, _) in zip(probs, vs)])
                o_hi = functools.reduce(lambda u, w: u + w, [_dot(p[2 * nq:], v_hi) for p, (_, v_hi) in zip(probs, vs)])
                out = (o_lo * inv[:2 * nq] + o_hi * inv[2 * nq:]).astype(BF16)
                col = kh * 2 * LANES
                o_ref[q_rows, col:col + LANES] = out[:nq]
                o_ref[q_rows, col + LANES:col + 2 * LANES] = out[nq:]
            return carry

        _attention_schedule(seq, queries, 0)

    return _pcall(
        body, name=name, grid=(n_b,),
        in_specs=[pl.BlockSpec(memory_space=pltpu.SMEM),
                  pl.BlockSpec((None, n_l, D_MODEL), lambda b: (b, 0, 0)),
                  pl.BlockSpec((None, n_l, KV_WIDTH), lambda b: (b, 0, D_MODEL // KV_WIDTH)),
                  pl.BlockSpec((None, n_l, KV_WIDTH), lambda b: (b, 0, D_MODEL // KV_WIDTH + 1))],
        out_specs=pl.BlockSpec((None, n_l, D_MODEL), lambda b: (b, 0, 0)),
        out_shape=jax.ShapeDtypeStruct((n_b, n_l, D_MODEL), BF16),
        compiler_params=_cp(("arbitrary",), VMEM_BIG),
    )(sinks, proj3, proj3, proj3)


def attention_backward(proj3, dattn3, dproj3, sinks, seq, name):
    n_b, n_l, _ = proj3.shape
    qkv_width = D_MODEL + 2 * KV_WIDTH

    def body(sink_ref, q_ref, k_ref, v_ref, do_ref, _, dqkv_ref, dsink_ref, dk_ref, dv_ref):
        dk_ref[...] = jnp.zeros_like(dk_ref)
        dv_ref[...] = jnp.zeros_like(dv_ref)
        sub = lax.broadcasted_iota(jnp.int32, (SUBLANES, LANES), 0)
        lane = lax.broadcasted_iota(jnp.int32, (SUBLANES, LANES), 1)

        def queries(q_rows, nq, key_rows, masks, dsink):
            for kh in range(N_KV_HEADS):
                ks = [_kv_variants(k_ref, r, kh) for r in key_rows]
                vs = [_kv_variants(v_ref, r, kh) for r in key_rows]
                qs = _stacked(q_ref, q_rows, kh)
                dos = _stacked(do_ref, q_rows, kh)
                probs, inv, e_sink = _softmax_parts(qs, ks, masks, _sink_column(sink_ref, kh, nq))
                probs = [p * inv for p in probs]
                dps = [jnp.concatenate([_dot_nt(dos, v_lo), _dot_nt(dos, v_hi)], axis=0) for v_lo, v_hi in vs]
                delta = functools.reduce(
                    lambda u, w: u + w, [jnp.sum(p * dp, axis=-1, keepdims=True) for p, dp in zip(probs, dps)])
                d_sink = -(e_sink * inv) * delta
                for quarter, g in enumerate(STACK_HEADS):
                    d_here = jnp.sum(d_sink[quarter * nq:(quarter + 1) * nq], axis=0, keepdims=True)
                    dsink = dsink + jnp.where(jnp.logical_and(sub == 0, lane == kh * Q_PER_KV + g), d_here, 0.0)
                dq = None
                tile = slice((kh // 2) * LANES, (kh // 2 + 1) * LANES)
                for r, p, dp, (k_lo, k_hi) in zip(key_rows, probs, dps, ks):
                    ds = (p * (dp - delta)).astype(BF16)
                    p16 = p.astype(BF16)
                    dq_x = _dot(ds[:2 * nq], k_lo) + _dot(ds[2 * nq:], k_hi)
                    dq = dq_x if dq is None else dq + dq_x
                    dk_ref[r, tile] += _to_kv_lanes(_dot_tn(ds[:2 * nq], qs), _dot_tn(ds[2 * nq:], qs), kh) * ATTN_SCALE
                    dv_ref[r, tile] += _to_kv_lanes(_dot_tn(p16[:2 * nq], dos), _dot_tn(p16[2 * nq:], dos), kh)
                dq = (dq * ATTN_SCALE).astype(BF16)
                col = kh * 2 * LANES
                dqkv_ref[q_rows, col:col + LANES] = dq[:nq]
                dqkv_ref[q_rows, col + LANES:col + 2 * LANES] = dq[nq:]
            return dsink

        dsink_ref[...] = _attention_schedule(seq, queries, jnp.zeros((SUBLANES, LANES), F32))
        dqkv_ref[:, D_MODEL:D_MODEL + KV_WIDTH] = dk_ref[...].astype(BF16)
        dqkv_ref[:, D_MODEL + KV_WIDTH:] = dv_ref[...].astype(BF16)

    return _pcall(
        body, name=name, grid=(n_b,),
        in_specs=[pl.BlockSpec(memory_space=pltpu.SMEM),
                  pl.BlockSpec((None, n_l, D_MODEL), lambda b: (b, 0, 0)),
                  pl.BlockSpec((None, n_l, KV_WIDTH), lambda b: (b, 0, D_MODEL // KV_WIDTH)),
                  pl.BlockSpec((None, n_l, KV_WIDTH), lambda b: (b, 0, D_MODEL // KV_WIDTH + 1)),
                  pl.BlockSpec((None, n_l, D_MODEL), lambda b: (b, 0, 0)),
                  pl.BlockSpec(memory_space=pl.ANY)],
        out_specs=[pl.BlockSpec((None, n_l, qkv_width), lambda b: (b, 0, 0)),
                   pl.BlockSpec((None, SUBLANES, LANES), lambda b: (b, 0, 0))],
        out_shape=[jax.ShapeDtypeStruct(dproj3.shape, BF16), jax.ShapeDtypeStruct((n_b, SUBLANES, LANES), F32)],
        scratch_shapes=[pltpu.VMEM((n_l, KV_WIDTH), F32), pltpu.VMEM((n_l, KV_WIDTH), F32)],
        input_output_aliases={5: 0},
        compiler_params=_cp(("arbitrary",), VMEM_BIG),
    )(sinks, proj3, proj3, proj3, dattn3, dproj3)


TAB_ROWS = 8
SCAN_UNROLL = 4


def _cmul(ar, ai, br, bi):
    return ar * br - ai * bi, ar * bi + ai * br


def _discretise(ar, ai, ls):
    step = jnp.exp(ls)
    mag = jnp.exp(ar * step)
    ang = ai * step
    cos, sin = jnp.cos(ang), jnp.sin(ang)
    lr, li = mag * cos, mag * sin
    den = ar * ar + ai * ai
    nr, ni = lr - 1.0, li
    cr = (nr * ar + ni * ai) / den
    ci = (ni * ar - nr * ai) / den
    return step, mag, lr, li, den, nr, ni, cr, ci


def _scan_tables(lr, li, reverse):
    n = lr.shape[-1]
    pw = [(lr, li)]
    for _ in range(SUBLANES - 1):
        pw.append(_cmul(pw[-1][0], pw[-1][1], lr, li))
    row = lax.broadcasted_iota(jnp.int32, (SUBLANES, n), 0)
    out = []
    for d in (1, 2, 4):
        ok = (row + d <= SUBLANES - 1) if reverse else (row >= d)
        out += [jnp.where(ok, pw[d - 1][0], 0.0), jnp.where(ok, pw[d - 1][1], 0.0)]
    cr = jnp.zeros((SUBLANES, n), F32)
    ci = jnp.zeros((SUBLANES, n), F32)
    for r in range(SUBLANES):
        e = (SUBLANES - r) if reverse else (r + 1)
        cr = jnp.where(row == r, pw[e - 1][0], cr)
        ci = jnp.where(row == r, pw[e - 1][1], ci)
    return out + [cr, ci]


def ssm_prepare(ar, ai, ls, br_t, bi_t, name):
    def body(ar_ref, ai_ref, ls_ref, br_ref, bi_ref, bbr_ref, bbi_ref, tf_ref, tr_ref):
        _, _, lr, li, _, _, _, cr, ci = _discretise(ar_ref[...], ai_ref[...], ls_ref[...])
        br, bi = br_ref[...], bi_ref[...]
        bbr_ref[...] = cr * br - ci * bi
        bbi_ref[...] = cr * bi + ci * br
        for k, t in enumerate(_scan_tables(lr, li, False)):
            tf_ref[k] = t
        for k, t in enumerate(_scan_tables(lr, -li, True)):
            tr_ref[k] = t

    return _pcall(
        body, name=name,
        out_shape=[jax.ShapeDtypeStruct((SSM_GROUP, N_STATES), F32), jax.ShapeDtypeStruct((SSM_GROUP, N_STATES), F32),
                   jax.ShapeDtypeStruct((TAB_ROWS, SUBLANES, N_STATES), F32),
                   jax.ShapeDtypeStruct((TAB_ROWS, SUBLANES, N_STATES), F32)],
    )(ar, ai, ls, br_t, bi_t)


def ssm_param_backward(ar, ai, ls, br_t, bi_t, dlr_p, dli_p, dbbr, dbbi, group_sum, name):
    def body(ar_ref, ai_ref, ls_ref, br_ref, bi_ref, dlr_ref, dli_ref, dbbr_ref, dbbi_ref, gs_ref,
             dar_ref, dai_ref, dls_ref, dbr_ref, dbi_ref):
        ar, ai = ar_ref[...], ai_ref[...]
        step, mag, lr, li, den, nr, ni, cr, ci = _discretise(ar, ai, ls_ref[...])
        br, bi, dbbr_v, dbbi_v = br_ref[...], bi_ref[...], dbbr_ref[...], dbbi_ref[...]
        dbr_ref[...] = cr * dbbr_v + ci * dbbi_v
        dbi_ref[...] = cr * dbbi_v - ci * dbbr_v
        dcr = jnp.sum(dbbr_v * br + dbbi_v * bi, axis=0, keepdims=True)
        dci = jnp.sum(dbbi_v * br - dbbr_v * bi, axis=0, keepdims=True)
        dnr = (dcr * ar - dci * ai) / den
        dni = (dcr * ai + dci * ar) / den
        dden = -(cr * dcr + ci * dci) / den
        dar = (dcr * nr + dci * ni) / den + dden * 2.0 * ar
        dai = (dcr * ni - dci * nr) / den + dden * 2.0 * ai
        dlr = jnp.sum(dlr_ref[...], axis=0, keepdims=True) + dnr
        dli = jnp.sum(dli_ref[...], axis=0, keepdims=True) + dni
        dmag = (dlr * lr + dli * li) / mag
        dang = dli * lr - dlr * li
        dar_ref[...] = dar + dmag * mag * step
        dai_ref[...] = dai + dang * step
        dstep = dmag * mag * ar + dang * ai
        dls_ref[...] = jnp.dot(dstep * step, gs_ref[...], preferred_element_type=F32, precision=lax.Precision.HIGHEST)

    vec = jax.ShapeDtypeStruct((1, N_STATES), F32)
    mat = jax.ShapeDtypeStruct((SSM_GROUP, N_STATES), F32)
    return _pcall(body, name=name, out_shape=[vec, vec, jax.ShapeDtypeStruct((1, LANES), F32), mat, mat])(
        ar, ai, ls, br_t, bi_t, dlr_p, dli_p, dbbr, dbbi, group_sum)


def _scan_rows(a, b, tabs, carry, reverse):
    for k, d in enumerate((1, 2, 4)):
        shift = SUBLANES - d if reverse else d
        sr, si = pltpu.roll(a, shift, axis=0), pltpu.roll(b, shift, axis=0)
        pr, pi = _cmul(tabs[2 * k], tabs[2 * k + 1], sr, si)
        a, b = a + pr, b + pi
    pr, pi = _cmul(tabs[6], tabs[7], carry[0], carry[1])
    return a + pr, b + pi


def _time_groups(seq, reverse):
    meta = [seq + SUBLANES * g for g in range(N_META // SUBLANES)]
    return meta[::-1] if reverse else meta


def ssm_forward_scan(proj3, b_comb, tabf, c_comb, dvec, seq, name):
    n_b, n_l, _ = proj3.shape
    u_blk = (D_MODEL + 2 * KV_WIDTH) // LANES

    def body(u_ref, b_ref, tab_ref, c_ref, d_ref, x_ref, y_ref, bu, xs):
        j = pl.program_id(1)
        u = u_ref[...]
        bu[...] = _dot(u, b_ref[...])
        tabs = [tab_ref[k] for k in range(TAB_ROWS)]

        def group(r0, carry):
            rows = pl.ds(r0, SUBLANES)
            a, b = _scan_rows(bu[rows, :SCAN_COLS], bu[rows, SCAN_COLS:], tabs, carry, False)
            xs[rows, :SCAN_COLS] = a
            xs[rows, SCAN_COLS:] = b
            return (jnp.broadcast_to(a[SUBLANES - 1:, :], a.shape), jnp.broadcast_to(b[SUBLANES - 1:, :], b.shape))

        zero = jnp.zeros((SUBLANES, SCAN_COLS), F32)
        carry = (zero, zero)
        for r0 in _time_groups(seq, False):
            carry = group(r0, carry)
        span = SCAN_UNROLL * SUBLANES

        def groups(t, c):
            for k in range(SCAN_UNROLL):
                c = group(pl.multiple_of(t * span, span) + k * SUBLANES, c)
            return c

        lax.fori_loop(0, seq // span, groups, carry)
        x16 = xs[...].astype(BF16)
        x_ref[...] = x16
        contrib = _dot(x16, c_ref[...])

        @pl.when(j % 2 == 0)
        def _():
            y_ref[...] = contrib + d_ref[...] * u.astype(F32)

        @pl.when(j % 2 == 1)
        def _():
            y_ref[...] += contrib

    return _pcall(
        body, name=name, grid=(n_b, N_SCAN_BLK),
        in_specs=[pl.BlockSpec((None, n_l, LANES), lambda b, j: (b, 0, u_blk + j // 2)),
                  pl.BlockSpec((LANES, 2 * SCAN_COLS), lambda b, j: (j // 2, j)),
                  pl.BlockSpec((TAB_ROWS, SUBLANES, SCAN_COLS), lambda b, j: (0, 0, j)),
                  pl.BlockSpec((2 * SCAN_COLS, LANES), lambda b, j: (j, j // 2)),
                  pl.BlockSpec((1, LANES), lambda b, j: (0, j // 2))],
        out_specs=[pl.BlockSpec((None, n_l, 2 * SCAN_COLS), lambda b, j: (b, 0, j)),
                   pl.BlockSpec((None, n_l, LANES), lambda b, j: (b, 0, j // 2))],
        out_shape=[jax.ShapeDtypeStruct((n_b, n_l, 2 * N_STATES), BF16),
                   jax.ShapeDtypeStruct((n_b, n_l, SSM_WIDTH), F32)],
        scratch_shapes=[pltpu.VMEM((n_l, 2 * SCAN_COLS), F32)] * 2,
        compiler_params=_cp(("arbitrary", "arbitrary"), VMEM_BIG),
    )(proj3, b_comb, tabf, c_comb, dvec)


def ssm_backward_scan(dyraw3, xs3, dproj3, c_comb_t, tabr, b_comb_t, dvec, seq, name):
    n_b, n_l, _ = xs3.shape
    u_blk = (D_MODEL + 2 * KV_WIDTH) // LANES

    def body(dy_ref, x_ref, _, c_ref, tab_ref, b_ref, d_ref, du_ref, g_ref, dlr_ref, dli_ref, dx, gs, xs, du_acc):
        j = pl.program_id(1)
        dy = dy_ref[...]
        dx[...] = _dot(dy, c_ref[...])
        xs[...] = x_ref[...].astype(F32)
        tabs = [tab_ref[k] for k in range(TAB_ROWS)]
        last_row = lax.broadcasted_iota(jnp.int32, (SUBLANES, SCAN_COLS), 0) == SUBLANES - 1

        def group(r0, state):
            cr, ci, acc_r, acc_i = state
            rows = pl.ds(r0, SUBLANES)
            a, b = _scan_rows(dx[rows, :SCAN_COLS], dx[rows, SCAN_COLS:], tabs, (cr, ci), True)
            gs[rows, :SCAN_COLS] = a
            gs[rows, SCAN_COLS:] = b
            na = jnp.where(last_row, cr, pltpu.roll(a, SUBLANES - 1, axis=0))
            nb = jnp.where(last_row, ci, pltpu.roll(b, SUBLANES - 1, axis=0))
            xa, xb = xs[rows, :SCAN_COLS], xs[rows, SCAN_COLS:]
            return (jnp.broadcast_to(a[:1, :], a.shape), jnp.broadcast_to(b[:1, :], b.shape),
                    acc_r + na * xa + nb * xb, acc_i + nb * xa - na * xb)

        zero = jnp.zeros((SUBLANES, SCAN_COLS), F32)
        span = SCAN_UNROLL * SUBLANES
        n_spans = seq // span

        def groups(t, s):
            for k in reversed(range(SCAN_UNROLL)):
                s = group(pl.multiple_of((n_spans - 1 - t) * span, span) + k * SUBLANES, s)
            return s

        state = lax.fori_loop(0, n_spans, groups, (zero, zero, zero, zero))
        for r0 in _time_groups(seq, True):
            state = group(r0, state)
        dlr_ref[...] = state[2]
        dli_ref[...] = state[3]
        g16 = gs[...].astype(BF16)
        g_ref[...] = g16
        contrib = _dot(g16, b_ref[...])

        @pl.when(j % 2 == 0)
        def _():
            du_acc[...] = contrib + d_ref[...] * dy.astype(F32)

        @pl.when(j % 2 == 1)
        def _():
            du_ref[...] = (du_acc[...] + contrib).astype(BF16)

    state_blk = pl.BlockSpec((None, n_l, 2 * SCAN_COLS), lambda b, j: (b, 0, j))
    dl_blk = pl.BlockSpec((None, SUBLANES, SCAN_COLS), lambda b, j: (b, 0, j))
    return _pcall(
        body, name=name, grid=(n_b, N_SCAN_BLK),
        in_specs=[pl.BlockSpec((None, n_l, LANES), lambda b, j: (b, 0, j // 2)), state_blk,
                  pl.BlockSpec(memory_space=pl.ANY),
                  pl.BlockSpec((LANES, 2 * SCAN_COLS), lambda b, j: (j // 2, j)),
                  pl.BlockSpec((TAB_ROWS, SUBLANES, SCAN_COLS), lambda b, j: (0, 0, j)),
                  pl.BlockSpec((2 * SCAN_COLS, LANES), lambda b, j: (j, j // 2)),
                  pl.BlockSpec((1, LANES), lambda b, j: (0, j // 2))],
        out_specs=[pl.BlockSpec((None, n_l, LANES), lambda b, j: (b, 0, u_blk + j // 2)), state_blk, dl_blk, dl_blk],
        out_shape=[jax.ShapeDtypeStruct(dproj3.shape, BF16), jax.ShapeDtypeStruct((n_b, n_l, 2 * N_STATES), BF16),
                   jax.ShapeDtypeStruct((n_b, SUBLANES, N_STATES), F32), jax.ShapeDtypeStruct((n_b, SUBLANES, N_STATES), F32)],
        scratch_shapes=[pltpu.VMEM((n_l, 2 * SCAN_COLS), F32)] * 3 + [pltpu.VMEM((n_l, LANES), F32)],
        input_output_aliases={2: 0},
        compiler_params=_cp(("arbitrary", "arbitrary"), VMEM_BIG),
    )(dyraw3, xs3, dproj3, c_comb_t, tabr, b_comb_t, dvec)


def ssm_param_grads(proj, gs, xs, dyraw, tm, name):
    t_rows = proj.shape[0]
    ni = t_rows // tm
    n_cb = 2 * N_STATES // SSM_WIDTH
    u_blk = (D_MODEL + 2 * KV_WIDTH) // SSM_WIDTH

    def body(u_ref, g_ref, x_ref, dy_ref, db_ref, dc_ref, dd_ref, acc_b, acc_c):
        cb, i = pl.program_id(0), pl.program_id(1)

        @pl.when(i == 0)
        def _():
            acc_b[...] = jnp.zeros_like(acc_b)
            acc_c[...] = jnp.zeros_like(acc_c)

        u, dy = u_ref[...], dy_ref[...]
        acc_b[...] += _dot_tn(u, g_ref[...])
        acc_c[...] += _dot_tn(x_ref[...], dy)

        @pl.when(i == ni - 1)
        def _():
            db_ref[...] = acc_b[...]
            dc_ref[...] = acc_c[...]

        @pl.when(cb == 0)
        def _():
            _accumulate(dd_ref, jnp.sum(dy.astype(F32) * u.astype(F32), axis=0, keepdims=True), i == 0)

    sq = (SSM_WIDTH, SSM_WIDTH)
    return _pcall(
        body, name=name, grid=(n_cb, ni),
        in_specs=[pl.BlockSpec((tm, SSM_WIDTH), lambda cb, i: (i, u_blk)),
                  pl.BlockSpec((tm, SSM_WIDTH), lambda cb, i: (i, cb)),
                  pl.BlockSpec((tm, SSM_WIDTH), lambda cb, i: (i, cb)),
                  pl.BlockSpec((tm, SSM_WIDTH), lambda cb, i: (i, 0))],
        out_specs=[pl.BlockSpec(sq, lambda cb, i: (0, cb)), pl.BlockSpec(sq, lambda cb, i: (cb, 0)),
                   pl.BlockSpec((1, SSM_WIDTH), lambda cb, i: (0, 0))],
        out_shape=[jax.ShapeDtypeStruct((SSM_WIDTH, 2 * N_STATES), F32), jax.ShapeDtypeStruct((2 * N_STATES, SSM_WIDTH), F32),
                   jax.ShapeDtypeStruct((1, SSM_WIDTH), F32)],
        scratch_shapes=[pltpu.VMEM(sq, F32), pltpu.VMEM(sq, F32)],
        compiler_params=_cp(("arbitrary", "arbitrary"), VMEM_BIG),
    )(proj, gs, xs, dyraw)


def sum_leading(x, name):
    def body(x_ref, o_ref):
        acc = x_ref[0]
        for k in range(1, x.shape[0]):
            acc = acc + x_ref[k]
        o_ref[...] = acc

    return _pcall(body, name=name, out_shape=jax.ShapeDtypeStruct(x.shape[1:], x.dtype))(x)


WEIGHTS = ['meta_tokens', 'ffn1_norm', 'ffn1_w1', 'ffn1_w3', 'ffn1_w2', 'mix_norm', 'w_in', 'attn_sinks', 'ssm_a_re',
           'ssm_a_im', 'ssm_log_step', 'ssm_b_re', 'ssm_b_im', 'ssm_c_re', 'ssm_c_im', 'ssm_d', 'ssm_glu_a', 'ssm_glu_b',
           'w_out', 'ffn2_norm', 'ffn2_w1', 'ffn2_w3', 'ffn2_w2', 'final_norm']
SHARDED = ['ffn1_w1', 'ffn1_w3', 'ffn1_w2', 'ffn2_w1', 'ffn2_w3', 'ffn2_w2', 'w_in', 'ssm_glu_a', 'ssm_glu_b', 'w_out']
REPLICATED = ['ffn1_norm', 'mix_norm', 'ffn2_norm', 'final_norm', 'attn_sinks', 'ssm_a_re', 'ssm_a_im', 'ssm_log_step',
              'ssm_b_re', 'ssm_b_im', 'ssm_c_re', 'ssm_c_im', 'ssm_d']
PACK_COLS = 1024


def _block_diag(blocks):
    g, r, c = blocks.shape
    eye = jnp.eye(g, dtype=blocks.dtype)
    return (blocks[:, :, None, :] * eye[:, None, :, None]).reshape(g * r, g * c)


def _diag_blocks(mat, r, c):
    g = SSM_GROUPS
    eye = jnp.eye(g, dtype=mat.dtype)
    return jnp.sum(mat.reshape(g, r, g, c) * eye[:, None, :, None], axis=2)


def _scan_order(re, im):
    r = re.shape[0]
    return jnp.stack([re.reshape(r, N_SCAN_BLK, SCAN_COLS), im.reshape(r, N_SCAN_BLK, SCAN_COLS)], axis=2).reshape(r, 2 * N_STATES)


def _from_scan_order(comb):
    r = comb.shape[0]
    c4 = comb.reshape(r, N_SCAN_BLK, 2, SCAN_COLS)
    return c4[:, :, 0].reshape(r, N_STATES), c4[:, :, 1].reshape(r, N_STATES)


def _pack(arrays):
    parts = []
    for a in arrays:
        flat = a.reshape(-1)
        chunk = SUBLANES * PACK_COLS
        padded = -(-flat.shape[0] // chunk) * chunk
        parts.append(jnp.pad(flat, (0, padded - flat.shape[0])).reshape(-1, PACK_COLS))
    return jnp.concatenate(parts, axis=0)


def _unpack(packed, shapes):
    out, row = [], 0
    for shape in shapes:
        size = 1
        for s in shape:
            size *= s
        chunk = SUBLANES * PACK_COLS
        rows = -(-size // chunk) * SUBLANES
        out.append(packed[row:row + rows].reshape(-1)[:size].reshape(shape))
        row += rows
    return out


def kernel(x, meta_tokens, ffn1_norm, ffn1_w1, ffn1_w3, ffn1_w2, mix_norm, w_in, attn_sinks, ssm_a_re, ssm_a_im, ssm_log_step, ssm_b_re, ssm_b_im, ssm_c_re, ssm_c_im, ssm_d, ssm_glu_a, ssm_glu_b, w_out, ffn2_norm, ffn2_w1, ffn2_w3, ffn2_w2, final_norm, loss_target, m_meta_tokens, m_ffn1_norm, m_ffn1_w1, m_ffn1_w3, m_ffn1_w2, m_mix_norm, m_w_in, m_attn_sinks, m_ssm_a_re, m_ssm_a_im, m_ssm_log_step, m_ssm_b_re, m_ssm_b_im, m_ssm_c_re, m_ssm_c_im, m_ssm_d, m_ssm_glu_a, m_ssm_glu_b, m_w_out, m_ffn2_norm, m_ffn2_w1, m_ffn2_w3, m_ffn2_w2, m_final_norm, v_meta_tokens, v_ffn1_norm, v_ffn1_w1, v_ffn1_w3, v_ffn1_w2, v_mix_norm, v_w_in, v_attn_sinks, v_ssm_a_re, v_ssm_a_im, v_ssm_log_step, v_ssm_b_re, v_ssm_b_im, v_ssm_c_re, v_ssm_c_im, v_ssm_d, v_ssm_glu_a, v_ssm_glu_b, v_w_out, v_ffn2_norm, v_ffn2_w1, v_ffn2_w3, v_ffn2_w2, v_final_norm):
    given = dict(locals())
    w = {n: given[n] for n in WEIGHTS}
    m = {n: given["m_" + n] for n in WEIGHTS}
    v = {n: given["v_" + n] for n in WEIGHTS}

    n_b, seq, _ = x.shape
    n_l = seq + N_META
    t_rows = n_b * n_l
    tm = _row_tile(n_l, 688)
    px, py, pc = _my_place()
    me = 4 * px + 2 * py + pc
    dest = jnp.stack([4 * qx + 2 * qy + pc for qx, qy in
                      [(px, py), (1 - px, py), (px, 1 - py), (1 - px, 1 - py)]]).astype(jnp.int32)

    glu = jnp.stack([ssm_glu_a[0], ssm_glu_b[0]]).astype(BF16)
    ffn_names = ['ffn1_w1', 'ffn1_w3', 'ffn1_w2', 'ffn2_w1', 'ffn2_w3', 'ffn2_w2']
    gathered = all_gather_list(
        [w[n][0].astype(BF16) for n in ffn_names] + [w_in[0].astype(BF16), glu, w_out[0].astype(BF16), meta_tokens],
        "ag_weights")
    wing, glug, wog, metag = gathered[len(ffn_names):]
    full = {}
    for n, g in zip(ffn_names, gathered):
        if n.endswith('w2'):
            full[n] = g.reshape(D_FF, D_MODEL)
        else:
            full[n] = g.transpose(1, 0, 2).reshape(D_MODEL, D_FF)
    glu_a = glug[:, 0].transpose(1, 0, 2).reshape(SSM_WIDTH, D_MODEL)
    glu_b = glug[:, 1].transpose(1, 0, 2).reshape(SSM_WIDTH, D_MODEL)
    w_out_full = wog.reshape(D_MODEL, D_MODEL)
    meta_full = metag.transpose(1, 0, 2).reshape(N_META, D_MODEL)

    h0 = jnp.concatenate([x, jnp.broadcast_to(meta_full[None], (n_b, N_META, D_MODEL))], axis=1).reshape(t_rows, D_MODEL)
    target = jnp.concatenate([loss_target, jnp.zeros((n_b, N_META, D_MODEL), F32)], axis=1).reshape(t_rows, D_MODEL)
    final_g = final_norm.reshape(1, D_MODEL)

    ar = ssm_a_re.reshape(1, N_STATES)
    ai = ssm_a_im.reshape(1, N_STATES)
    ls = jnp.repeat(ssm_log_step.reshape(SSM_GROUPS), SSM_STATE).reshape(1, N_STATES)
    br_t = ssm_b_re[0].transpose(2, 0, 1).reshape(SSM_GROUP, N_STATES)
    bi_t = ssm_b_im[0].transpose(2, 0, 1).reshape(SSM_GROUP, N_STATES)
    bbr, bbi, tabf, tabr = ssm_prepare(ar, ai, ls, br_t, bi_t, "ssm_prepare")
    bbr_g = bbr.reshape(SSM_GROUP, SSM_GROUPS, SSM_STATE).transpose(1, 0, 2)
    bbi_g = bbi.reshape(SSM_GROUP, SSM_GROUPS, SSM_STATE).transpose(1, 0, 2)
    b_comb = _scan_order(_block_diag(bbr_g), _block_diag(bbi_g)).astype(BF16)
    c_comb_t = _scan_order(_block_diag(ssm_c_re[0]), -_block_diag(ssm_c_im[0])).astype(BF16)
    b_comb_t, c_comb = b_comb.T, c_comb_t.T

    ffn1_w = (full['ffn1_w1'], full['ffn1_w3'], full['ffn1_w2'])
    ffn2_w = (full['ffn2_w1'], full['ffn2_w3'], full['ffn2_w2'])
    h1, hn1, a1, b1 = ffn_forward(h0, ffn1_norm, *ffn1_w, tm, FF_FWD_COLS, "ffn1_fwd")
    hnm, proj = mix_forward(h1, mix_norm, wing, tm, "mix_fwd")
    proj3 = proj.reshape(n_b, n_l, IN_WIDTH)
    attn3 = attention_forward(proj3, attn_sinks, seq, "attn_fwd")
    attn = attn3.reshape(t_rows, D_MODEL)
    xs3, yraw3 = ssm_forward_scan(proj3, b_comb, tabf, c_comb, ssm_d, seq, "ssm_fwd")
    yraw = yraw3.reshape(t_rows, SSM_WIDTH)
    h2 = merge_forward(h1, yraw, attn, proj, glu_a, glu_b, w_out_full, tm, "merge_fwd")
    h3, hn2, a2, b2 = ffn_forward(h2, ffn2_norm, *ffn2_w, tm, FF_FWD_COLS, "ffn2_fwd")
    dh3, loss_part, g_final = final_loss_backward(h3, target, final_g, seq, tm, "loss_bwd")
    loss = lax.psum(loss_part[0, 0], ("x", "y", "c"))

    def blocked_ffn(d_w1, d_w3, d_w2):
        by_cols = lambda t: t.reshape(D_MODEL, N_DEV, FF_BLK).transpose(1, 0, 2)
        return by_cols(d_w1), by_cols(d_w3), d_w2.reshape(N_DEV, FF_BLK, D_MODEL)

    da2, db2, dh3_half = ffn_backward_hidden(dh3, a2, b2, ffn2_w[2], tm, "ffn2_bwd_hid")
    dh2, g_ffn2_norm = ffn_backward_input(dh3, h2, ffn2_norm, da2, db2, ffn2_w[0], ffn2_w[1], tm, "ffn2_bwd_in")
    dw = {}
    dw['ffn2_w1'], dw['ffn2_w3'], dw['ffn2_w2'] = blocked_ffn(
        *ffn_backward_weights(hn2, dh3_half, a2, b2, da2, db2, n_l, FF_BWD_COLS, "ffn2_bwd_w"))
    dattn, dyraw, dproj, *for_weights = merge_backward(dh2, yraw, attn, proj, glu_a, glu_b, w_out_full, tm, "merge_bwd")
    d_wo, d_ga, d_gb = merge_backward_weights(*for_weights, tm, "merge_bwd_w")
    dproj3 = dproj.reshape(n_b, n_l, IN_WIDTH)
    dproj3, dsink_p = attention_backward(proj3, dattn.reshape(n_b, n_l, D_MODEL), dproj3, attn_sinks, seq, "attn_bwd")
    dproj3, gs3, dlr_p, dli_p = ssm_backward_scan(
        dyraw.reshape(n_b, n_l, SSM_WIDTH), xs3, dproj3, c_comb_t, tabr, b_comb_t, ssm_d, seq, "ssm_bwd")
    dproj = dproj3.reshape(t_rows, IN_WIDTH)
    d_bd, d_cd, g_d = ssm_param_grads(proj, gs3.reshape(t_rows, 2 * N_STATES), xs3.reshape(t_rows, 2 * N_STATES),
                                      dyraw, tm, "ssm_bwd_w")
    dh1, g_mix_norm = mix_backward_act(dh2, h1, mix_norm, dproj, wing, tm, "mix_bwd_act")
    dw['w_in'] = mix_backward_weights(hnm, dproj, n_l, "mix_bwd_w")
    da1, db1, dh1_half = ffn_backward_hidden(dh1, a1, b1, ffn1_w[2], tm, "ffn1_bwd_hid")
    dh0, g_ffn1_norm = ffn_backward_input(dh1, h0, ffn1_norm, da1, db1, ffn1_w[0], ffn1_w[1], tm, "ffn1_bwd_in")
    dw['ffn1_w1'], dw['ffn1_w3'], dw['ffn1_w2'] = blocked_ffn(
        *ffn_backward_weights(hn1, dh1_half, a1, b1, da1, db1, n_l, FF_BWD_COLS, "ffn1_bwd_w"))
    dh0_3 = dh0.reshape(n_b, n_l, D_MODEL)
    grad_x = dh0_3[:, :seq]
    g_meta = sum_leading(dh0_3[:, seq:], "meta_sum")

    def blocked_cols(full):
        r = full.shape[0]
        return full.reshape(r, N_DEV, full.shape[1] // N_DEV).transpose(1, 0, 2).astype(BF16)

    dw['ssm_glu_a'] = blocked_cols(d_ga)
    dw['ssm_glu_b'] = blocked_cols(d_gb)
    dw['w_out'] = d_wo.reshape(N_DEV, D_MODEL // N_DEV, D_MODEL).astype(BF16)

    d_b_re, d_b_im = _from_scan_order(d_bd)
    d_c_re, d_c_im = _from_scan_order(d_cd.T)
    dbbr = _diag_blocks(d_b_re, SSM_GROUP, SSM_STATE).transpose(1, 0, 2).reshape(SSM_GROUP, N_STATES)
    dbbi = _diag_blocks(d_b_im, SSM_GROUP, SSM_STATE).transpose(1, 0, 2).reshape(SSM_GROUP, N_STATES)
    g_c_re = _diag_blocks(d_c_re, SSM_GROUP, SSM_STATE)[None]
    g_c_im = -_diag_blocks(d_c_im, SSM_GROUP, SSM_STATE)[None]
    group_sum = (jnp.arange(N_STATES)[:, None] // SSM_STATE == jnp.arange(LANES)[None, :]).astype(F32)
    g_ar, g_ai, g_ls, g_br, g_bi = ssm_param_backward(
        ar, ai, ls, br_t, bi_t, dlr_p.reshape(n_b * SUBLANES, N_STATES), dli_p.reshape(n_b * SUBLANES, N_STATES),
        dbbr, dbbi, group_sum, "ssm_bwd_params")
    g_sinks = sum_leading(dsink_p, "sink_sum")[0:1, :N_KV_HEADS * Q_PER_KV]

    small = {
        'ffn1_norm': g_ffn1_norm, 'mix_norm': g_mix_norm, 'ffn2_norm': g_ffn2_norm, 'final_norm': g_final.reshape(D_MODEL),
        'attn_sinks': g_sinks, 'ssm_a_re': g_ar.reshape(1, SSM_GROUPS, SSM_STATE), 'ssm_a_im': g_ai.reshape(1, SSM_GROUPS, SSM_STATE),
        'ssm_log_step': g_ls[:, :SSM_GROUPS],
        'ssm_b_re': g_br.reshape(SSM_GROUP, SSM_GROUPS, SSM_STATE).transpose(1, 2, 0)[None],
        'ssm_b_im': g_bi.reshape(SSM_GROUP, SSM_GROUPS, SSM_STATE).transpose(1, 2, 0)[None],
        'ssm_c_re': g_c_re, 'ssm_c_im': g_c_im, 'ssm_d': g_d,
    }

    zeros_meta = jnp.zeros((N_META, D_MODEL), F32)
    packed_g = _pack([small[n] for n in REPLICATED] + [g_meta])
    (parts,) = all_gather_list([packed_g], "ag_small_grads")
    packed_out = adamw_small(parts, _pack([w[n] for n in REPLICATED] + [zeros_meta]),
                             _pack([m[n] for n in REPLICATED] + [zeros_meta]),
                             _pack([v[n] for n in REPLICATED] + [zeros_meta]), "adamw_small")
    shapes = [w[n].shape for n in REPLICATED] + [(N_META, D_MODEL)]
    grads, deltas, new_m, new_v = {}, {}, {}, {}
    unpacked = [_unpack(p, shapes) for p in packed_out]
    for k, n in enumerate(REPLICATED):
        grads[n], deltas[n], new_m[n], new_v[n] = (u[k] for u in unpacked)
    g_meta_full = unpacked[0][-1]
    grads['meta_tokens'] = lax.dynamic_index_in_dim(
        g_meta_full.reshape(N_META, N_DEV, D_MODEL // N_DEV), me, axis=1, keepdims=False)
    deltas['meta_tokens'], new_m['meta_tokens'], new_v['meta_tokens'] = adamw_plain(
        grads['meta_tokens'], w['meta_tokens'], m['meta_tokens'], v['meta_tokens'], "adamw_meta")

    g_list = [dw[n] for n in SHARDED]
    r1 = rs_sibling_swap(g_list, "rs_sibling")
    pairs = [pair_sums(dest, g, r, "rs_pair_" + n) for n, g, r in zip(SHARDED, g_list, r1)]
    r2 = rs_chip_exchange(pairs, "rs_chips")
    for n, g, ra, rb in zip(SHARDED, g_list, r1, r2):
        shape = w[n].shape
        two_d = lambda t: t.reshape(shape[1:])
        out = adamw_sharded(dest, g, ra, rb, two_d(w[n]), two_d(m[n]), two_d(v[n]), "adamw_" + n)
        grads[n], deltas[n], new_m[n], new_v[n] = (o.reshape(shape) for o in out)

    return (loss, grad_x, *[grads[n] for n in WEIGHTS], *[deltas[n] for n in WEIGHTS],
            *[new_m[n] for n in WEIGHTS], *[new_v[n] for n in WEIGHTS])
```

```python
import functools

import jax
import jax.numpy as jnp
from jax import lax
from jax.experimental import pallas as pl
from jax.experimental.pallas import tpu as pltpu

F32 = jnp.float32
BF16 = jnp.bfloat16
MESH = pl.DeviceIdType.MESH

N_DEV = 8
D_MODEL = 1024
N_META = 16
HEAD_DIM = 64
N_KV_HEADS = 4
Q_PER_KV = 4
BLOCK = 128
KV_WIDTH = N_KV_HEADS * HEAD_DIM
SSM_GROUP = 16
SSM_WIDTH = 512
SSM_GROUPS = 32
SSM_STATE = 64
N_STATES = SSM_GROUPS * SSM_STATE
D_FF = 2816
FF_BLK = D_FF // N_DEV
IN_WIDTH = 4096
IN_BLK = IN_WIDTH // N_DEV
NORM_EPS = 1e-6
NEG_INF = -1e30
SCAN_COLS = 256
N_SCAN_BLK = N_STATES // SCAN_COLS
SUBLANES = 8
LANES = 128
MXU_WIDTH = 256
FF_FWD_COLS = D_FF // 2
FF_BWD_COLS = MXU_WIDTH

ADAM_LR = 0.001
ADAM_B1 = 0.9
ADAM_B2 = 0.999
ADAM_EPS = 1e-08
ADAM_WD = 0.01
ADAM_STEP = 10

VMEM_BIG = 56 * 1024 * 1024


def _cp(sem=None, vmem=None):
    kw = {}
    if sem is not None:
        kw["dimension_semantics"] = sem
    if vmem is not None:
        kw["vmem_limit_bytes"] = vmem
    return pltpu.CompilerParams(**kw)


def _pcall(body, **kw):
    return pl.pallas_call(body, **kw)


def _dot(a, b):
    return jnp.dot(a, b, preferred_element_type=F32)


def _dot_nt(a, b):
    return lax.dot_general(a, b, (((1,), (1,)), ((), ())), preferred_element_type=F32)


def _dot_tn(a, b):
    return lax.dot_general(a, b, (((0,), (0,)), ((), ())), preferred_element_type=F32)


def _sigmoid(x):
    return 1.0 / (1.0 + jnp.exp(-x))


def _row_tile(rows, cap):
    best = None
    for t in range(16, min(rows, cap) + 1, 16):
        if rows % t == 0:
            best = t
    assert best is not None, rows
    return best


def _my_place():
    return lax.axis_index("x"), lax.axis_index("y"), lax.axis_index("c")


def all_gather_list(shards, name):
    n = len(shards)

    def body(*refs):
        ins, outs = refs[:n], refs[n:2 * n]
        send_sems, recv_sems, local_sems = refs[2 * n:]
        x, y, c = _my_place()
        me, sibling = (x, y, c), (x, y, 1 - c)
        chips = [(1 - x, y), (x, 1 - y), (1 - x, 1 - y)]

        def blk(a, px, py, pc):
            return outs[a].at[4 * px + 2 * py + pc]

        def copy(a, k, block, to, src=None):
            return pltpu.make_async_remote_copy(
                src_ref=blk(a, *block) if src is None else src, dst_ref=blk(a, *block),
                send_sem=send_sems.at[a * 7 + k], recv_sem=recv_sems.at[a * 7 + k],
                device_id=to, device_id_type=MESH)

        mine = [pltpu.make_async_copy(ins[a], blk(a, *me), local_sems.at[a]) for a in range(n)]
        for cp in mine:
            cp.start()
        first = []
        for a in range(n):
            first.append(copy(a, 0, me, sibling, src=ins[a]))
            first += [copy(a, 1 + j, me, (*chip, c), src=ins[a]) for j, chip in enumerate(chips)]
        for cp in first:
            cp.start()
        passed = []
        for j, chip in enumerate(chips):
            for a in range(n):
                copy(a, 1 + j, (*chip, c), me).wait_recv()
                cp = copy(a, 4 + j, (*chip, c), sibling)
                cp.start()
                passed.append(cp)
        for a in range(n):
            copy(a, 0, sibling, me).wait_recv()
            for j, chip in enumerate(chips):
                copy(a, 4 + j, (*chip, 1 - c), me).wait_recv()
        for cp in first + passed:
            cp.wait_send()
        for cp in mine:
            cp.wait()

    any_spec = pl.BlockSpec(memory_space=pl.ANY)
    return _pcall(
        body, name=name,
        out_shape=[jax.ShapeDtypeStruct((N_DEV,) + s.shape, s.dtype) for s in shards],
        in_specs=[any_spec] * n, out_specs=[any_spec] * n,
        scratch_shapes=[pltpu.SemaphoreType.DMA((7 * n,)), pltpu.SemaphoreType.DMA((7 * n,)),
                        pltpu.SemaphoreType.DMA((n,))],
    )(*shards)


def rs_sibling_swap(grads, name):
    n = len(grads)

    def body(*refs):
        ins, outs = refs[:n], refs[n:2 * n]
        send_sems, recv_sems = refs[2 * n:]
        x, y, c = _my_place()
        chips = [(x, y), (1 - x, y), (x, 1 - y), (1 - x, 1 - y)]
        copies = []
        for a in range(n):
            for k, (px, py) in enumerate(chips):
                copies.append(pltpu.make_async_remote_copy(
                    src_ref=ins[a].at[4 * px + 2 * py + (1 - c)], dst_ref=outs[a].at[k],
                    send_sem=send_sems.at[4 * a + k], recv_sem=recv_sems.at[4 * a + k],
                    device_id=(x, y, 1 - c), device_id_type=MESH))
        for cp in copies:
            cp.start()
        for cp in copies:
            cp.wait()

    any_spec = pl.BlockSpec(memory_space=pl.ANY)
    return _pcall(
        body, name=name,
        out_shape=[jax.ShapeDtypeStruct((4,) + g.shape[1:], g.dtype) for g in grads],
        in_specs=[any_spec] * n, out_specs=[any_spec] * n,
        scratch_shapes=[pltpu.SemaphoreType.DMA((4 * n,)), pltpu.SemaphoreType.DMA((4 * n,))],
    )(*grads)


def rs_chip_exchange(parts, name):
    n = len(parts)

    def body(*refs):
        ins, outs = refs[:n], refs[n:2 * n]
        send_sems, recv_sems = refs[2 * n:]
        x, y, c = _my_place()
        chips = [(1 - x, y), (x, 1 - y), (1 - x, 1 - y)]
        copies = []
        for a in range(n):
            for k, (px, py) in enumerate(chips):
                copies.append(pltpu.make_async_remote_copy(
                    src_ref=ins[a].at[k], dst_ref=outs[a].at[k],
                    send_sem=send_sems.at[3 * a + k], recv_sem=recv_sems.at[3 * a + k],
                    device_id=(px, py, c), device_id_type=MESH))
        for cp in copies:
            cp.start()
        for cp in copies:
            cp.wait()

    any_spec = pl.BlockSpec(memory_space=pl.ANY)
    return _pcall(
        body, name=name,
        out_shape=[jax.ShapeDtypeStruct(p.shape, p.dtype) for p in parts],
        in_specs=[any_spec] * n, out_specs=[any_spec] * n,
        scratch_shapes=[pltpu.SemaphoreType.DMA((3 * n,)), pltpu.SemaphoreType.DMA((3 * n,))],
    )(*parts)


def pair_sums(idx, g, r1, name):
    _, rows, cols = g.shape
    tr = _row_tile(rows, 256)

    def body(idx_ref, g_ref, r_ref, o_ref):
        o_ref[...] = (g_ref[...].astype(F32) + r_ref[...].astype(F32)).astype(BF16)

    return _pcall(
        body, name=name,
        out_shape=jax.ShapeDtypeStruct((3, rows, cols), BF16),
        grid_spec=pltpu.PrefetchScalarGridSpec(
            num_scalar_prefetch=1, grid=(3, rows // tr),
            in_specs=[pl.BlockSpec((None, tr, cols), lambda k, r, ix: (ix[k + 1], r, 0)),
                      pl.BlockSpec((None, tr, cols), lambda k, r, ix: (k + 1, r, 0))],
            out_specs=pl.BlockSpec((None, tr, cols), lambda k, r, ix: (k, r, 0))),
        compiler_params=_cp(("arbitrary", "arbitrary")),
    )(idx, g, r1)


def _adam_math(w, g, m, v):
    m = ADAM_B1 * m + (1.0 - ADAM_B1) * g
    v = ADAM_B2 * v + (1.0 - ADAM_B2) * (g * g)
    m_hat = m / (1.0 - ADAM_B1 ** ADAM_STEP)
    v_hat = v / (1.0 - ADAM_B2 ** ADAM_STEP)
    delta = -ADAM_LR * (m_hat / (jnp.sqrt(v_hat) + ADAM_EPS) + ADAM_WD * w)
    return delta, m, v


def adamw_sharded(idx, g, r1, r2, w, m, v, name):
    rows, cols = w.shape
    tr = _row_tile(rows, 256)

    def body(idx_ref, g_ref, r1_ref, r2_ref, w_ref, m_ref, v_ref, go_ref, d_ref, mo_ref, vo_ref):
        grad = g_ref[...].astype(F32) + r1_ref[...].astype(F32)
        for k in range(3):
            grad = grad + r2_ref[k].astype(F32)
        delta, m_new, v_new = _adam_math(w_ref[...], grad, m_ref[...], v_ref[...])
        go_ref[...] = grad
        d_ref[...] = delta
        mo_ref[...] = m_new
        vo_ref[...] = v_new

    tile = pl.BlockSpec((tr, cols), lambda r, ix: (r, 0))
    out = jax.ShapeDtypeStruct((rows, cols), F32)
    return _pcall(
        body, name=name, out_shape=[out] * 4,
        grid_spec=pltpu.PrefetchScalarGridSpec(
            num_scalar_prefetch=1, grid=(rows // tr,),
            in_specs=[pl.BlockSpec((None, tr, cols), lambda r, ix: (ix[0], r, 0)),
                      pl.BlockSpec((None, tr, cols), lambda r, ix: (0, r, 0)),
                      pl.BlockSpec((3, tr, cols), lambda r, ix: (0, r, 0)),
                      tile, tile, tile],
            out_specs=[tile] * 4),
        compiler_params=_cp(("arbitrary",)),
    )(idx, g, r1, r2, w, m, v)


def adamw_small(parts, w, m, v, name):
    _, rows, cols = parts.shape

    def body(p_ref, w_ref, m_ref, v_ref, go_ref, d_ref, mo_ref, vo_ref):
        grad = p_ref[0]
        for k in range(1, N_DEV):
            grad = grad + p_ref[k]
        delta, m_new, v_new = _adam_math(w_ref[...], grad, m_ref[...], v_ref[...])
        go_ref[...] = grad
        d_ref[...] = delta
        mo_ref[...] = m_new
        vo_ref[...] = v_new

    out = jax.ShapeDtypeStruct((rows, cols), F32)
    return _pcall(body, name=name, out_shape=[out] * 4, compiler_params=_cp(vmem=VMEM_BIG))(parts, w, m, v)


def adamw_plain(g, w, m, v, name):
    def body(g_ref, w_ref, m_ref, v_ref, d_ref, mo_ref, vo_ref):
        delta, m_new, v_new = _adam_math(w_ref[...], g_ref[...], m_ref[...], v_ref[...])
        d_ref[...] = delta
        mo_ref[...] = m_new
        vo_ref[...] = v_new

    out = jax.ShapeDtypeStruct(w.shape, F32)
    return _pcall(body, name=name, out_shape=[out] * 3)(g, w, m, v)


def _rms_fwd(x, g):
    r = lax.rsqrt(jnp.mean(x * x, axis=-1, keepdims=True) + NORM_EPS)
    return x * r * g


def _rms_bwd(x, g, dy):
    r = lax.rsqrt(jnp.mean(x * x, axis=-1, keepdims=True) + NORM_EPS)
    xh = x * r
    t = dy * g
    dx = r * (t - xh * jnp.mean(t * xh, axis=-1, keepdims=True))
    return dx, jnp.sum(dy * xh, axis=0, keepdims=True)


def _accumulate(ref, val, first):
    @pl.when(first)
    def _():
        ref[...] = val

    @pl.when(jnp.logical_not(first))
    def _():
        ref[...] += val


def _col_chunks(width):
    return [(c0, min(MXU_WIDTH, width - c0)) for c0 in range(0, width, MXU_WIDTH)]


def ffn_forward(h, norm, w1, w3, w2, tm, tn, name):
    t_rows = h.shape[0]
    nj = D_FF // tn

    def body(h_ref, g_ref, w1_ref, w3_ref, w2_ref, out_ref, hn_ref, a_ref, b_ref, acc_ref):
        j = pl.program_id(1)

        @pl.when(j == 0)
        def _():
            hn_ref[...] = _rms_fwd(h_ref[...], g_ref[...]).astype(BF16)
            acc_ref[...] = jnp.zeros_like(acc_ref)

        hn = hn_ref[...]
        for c0, cw in _col_chunks(tn):
            a = _dot_nt(hn, w1_ref[c0:c0 + cw, :])
            b = _dot_nt(hn, w3_ref[c0:c0 + cw, :])
            a_ref[:, c0:c0 + cw] = a.astype(BF16)
            b_ref[:, c0:c0 + cw] = b.astype(BF16)
            hid = (a * _sigmoid(a) * b).astype(BF16)
            acc_ref[...] += _dot(hid, w2_ref[c0:c0 + cw, :])

        @pl.when(j == nj - 1)
        def _():
            out_ref[...] = h_ref[...] + 0.5 * acc_ref[...]

    row = pl.BlockSpec((tm, D_MODEL), lambda i, j: (i, 0))
    hid_blk = pl.BlockSpec((tm, tn), lambda i, j: (i, j))
    w_row = pl.BlockSpec((tn, D_MODEL), lambda i, j: (j, 0))
    return _pcall(
        body, name=name, grid=(t_rows // tm, nj),
        in_specs=[row, pl.BlockSpec((1, D_MODEL), lambda i, j: (0, 0)), w_row, w_row, w_row],
        out_specs=[row, row, hid_blk, hid_blk],
        out_shape=[jax.ShapeDtypeStruct((t_rows, D_MODEL), F32), jax.ShapeDtypeStruct((t_rows, D_MODEL), BF16),
                   jax.ShapeDtypeStruct((t_rows, D_FF), BF16), jax.ShapeDtypeStruct((t_rows, D_FF), BF16)],
        scratch_shapes=[pltpu.VMEM((tm, D_MODEL), F32)],
        compiler_params=_cp(("arbitrary", "arbitrary"), VMEM_BIG),
    )(h, norm, w1, w3, w2)


def _resident(shape):
    return pl.BlockSpec(shape, lambda *_: (0,) * len(shape), pipeline_mode=pl.Buffered(1))


def ffn_backward_hidden(dh, a, b, w2, tm, name):
    t_rows = dh.shape[0]

    def body(dh_ref, a_ref, b_ref, w2_ref, da_ref, db_ref, dhb_ref):
        dhb = (0.5 * dh_ref[...]).astype(BF16)
        dhb_ref[...] = dhb
        for c0, cw in _col_chunks(D_FF):
            dhid = _dot_nt(dhb, w2_ref[c0:c0 + cw, :])
            av = a_ref[:, c0:c0 + cw].astype(F32)
            bv = b_ref[:, c0:c0 + cw].astype(F32)
            s = _sigmoid(av)
            da_ref[:, c0:c0 + cw] = (dhid * bv * (s * (1.0 + av * (1.0 - s)))).astype(BF16)
            db_ref[:, c0:c0 + cw] = (dhid * (av * s)).astype(BF16)

    hid = pl.BlockSpec((tm, D_FF), lambda i: (i, 0))
    row = pl.BlockSpec((tm, D_MODEL), lambda i: (i, 0))
    return _pcall(
        body, name=name, grid=(t_rows // tm,),
        in_specs=[row, hid, hid, _resident((D_FF, D_MODEL))],
        out_specs=[hid, hid, row],
        out_shape=[jax.ShapeDtypeStruct((t_rows, D_FF), BF16), jax.ShapeDtypeStruct((t_rows, D_FF), BF16),
                   jax.ShapeDtypeStruct((t_rows, D_MODEL), BF16)],
        compiler_params=_cp(("arbitrary",), VMEM_BIG),
    )(dh, a, b, w2)


def ffn_backward_input(dh, h, norm, da, db, w1, w3, tm, name):
    t_rows = h.shape[0]

    def body(dh_ref, h_ref, g_ref, da_ref, db_ref, w1_ref, w3_ref, dhin_ref, dg_ref):
        dhn = _dot(da_ref[...], w1_ref[...]) + _dot(db_ref[...], w3_ref[...])
        dx, dg = _rms_bwd(h_ref[...], g_ref[...], dhn)
        dhin_ref[...] = dh_ref[...] + dx
        _accumulate(dg_ref, dg, pl.program_id(0) == 0)

    row = pl.BlockSpec((tm, D_MODEL), lambda i: (i, 0))
    vec = pl.BlockSpec((1, D_MODEL), lambda i: (0, 0))
    hid = pl.BlockSpec((tm, D_FF), lambda i: (i, 0))
    return _pcall(
        body, name=name, grid=(t_rows // tm,),
        in_specs=[row, row, vec, hid, hid, _resident((D_FF, D_MODEL)), _resident((D_FF, D_MODEL))],
        out_specs=[row, vec],
        out_shape=[jax.ShapeDtypeStruct((t_rows, D_MODEL), F32), jax.ShapeDtypeStruct((1, D_MODEL), F32)],
        compiler_params=_cp(("arbitrary",), VMEM_BIG),
    )(dh, h, norm, da, db, w1, w3)


def ffn_backward_weights(hn, dh, a, b, da, db, tm, tn, name):
    t_rows = hn.shape[0]
    ni = t_rows // tm
    kc = _row_tile(tm, 688)

    def body(hn_ref, dh_ref, a_ref, b_ref, da_ref, db_ref, dw1_ref, dw3_ref, dw2_ref, acc1, acc3, acc2):
        i = pl.program_id(1)
        parts = None
        for r0 in range(0, tm, kc):
            rows = slice(r0, r0 + kc)
            hn_v = hn_ref[rows, :]
            av = a_ref[rows, :].astype(F32)
            hid = (av * _sigmoid(av) * b_ref[rows, :].astype(F32)).astype(BF16)
            new = (_dot_tn(hn_v, da_ref[rows, :]), _dot_tn(hn_v, db_ref[rows, :]), _dot_tn(hid, dh_ref[rows, :]))
            parts = new if parts is None else tuple(p + q for p, q in zip(parts, new))
        _accumulate(acc1, parts[0], i == 0)
        _accumulate(acc3, parts[1], i == 0)
        _accumulate(acc2, parts[2], i == 0)

        @pl.when(i == ni - 1)
        def _():
            dw1_ref[...] = acc1[...].T.astype(BF16)
            dw3_ref[...] = acc3[...].T.astype(BF16)
            dw2_ref[...] = acc2[...].astype(BF16)

    row = pl.BlockSpec((tm, D_MODEL), lambda j, i: (i, 0))
    hid_blk = pl.BlockSpec((tm, tn), lambda j, i: (i, j))
    w_row = pl.BlockSpec((tn, D_MODEL), lambda j, i: (j, 0))
    out = jax.ShapeDtypeStruct((D_FF, D_MODEL), BF16)
    return _pcall(
        body, name=name, grid=(D_FF // tn, ni),
        in_specs=[row, row, hid_blk, hid_blk, hid_blk, hid_blk],
        out_specs=[w_row, w_row, w_row], out_shape=[out, out, out],
        scratch_shapes=[pltpu.VMEM((D_MODEL, tn), F32), pltpu.VMEM((D_MODEL, tn), F32), pltpu.VMEM((tn, D_MODEL), F32)],
        compiler_params=_cp(("arbitrary", "arbitrary"), VMEM_BIG),
    )(hn, dh, a, b, da, db)


def mix_forward(h, norm, wing, tm, name):
    t_rows = h.shape[0]

    def body(h_ref, g_ref, w_ref, hn_ref, p_ref):
        hn = _rms_fwd(h_ref[...], g_ref[...]).astype(BF16)
        hn_ref[...] = hn
        for j in range(N_DEV):
            p_ref[:, j * IN_BLK:(j + 1) * IN_BLK] = _dot(hn, w_ref[j]).astype(BF16)

    row = pl.BlockSpec((tm, D_MODEL), lambda i: (i, 0))
    return _pcall(
        body, name=name, grid=(t_rows // tm,),
        in_specs=[row, pl.BlockSpec((1, D_MODEL), lambda i: (0, 0)),
                  pl.BlockSpec((N_DEV, D_MODEL, IN_BLK), lambda i: (0, 0, 0))],
        out_specs=[row, pl.BlockSpec((tm, IN_WIDTH), lambda i: (i, 0))],
        out_shape=[jax.ShapeDtypeStruct((t_rows, D_MODEL), BF16), jax.ShapeDtypeStruct((t_rows, IN_WIDTH), BF16)],
        compiler_params=_cp(("arbitrary",), VMEM_BIG),
    )(h, norm, wing)


def mix_backward_act(dh, h, norm, dproj, wing, tm, name):
    t_rows = h.shape[0]
    per_step = 4
    nj = N_DEV // per_step

    def body(dh_ref, h_ref, g_ref, dp_ref, w_ref, dhin_ref, dg_ref, acc_ref):
        i, j = pl.program_id(0), pl.program_id(1)
        part = functools.reduce(
            lambda u, w: u + w, [_dot_nt(dp_ref[:, k * IN_BLK:(k + 1) * IN_BLK], w_ref[k]) for k in range(per_step)])
        _accumulate(acc_ref, part, j == 0)

        @pl.when(j == nj - 1)
        def _():
            dx, dg = _rms_bwd(h_ref[...], g_ref[...], acc_ref[...])
            dhin_ref[...] = dh_ref[...] + dx
            _accumulate(dg_ref, dg, i == 0)

    row = pl.BlockSpec((tm, D_MODEL), lambda i, j: (i, 0))
    vec = pl.BlockSpec((1, D_MODEL), lambda i, j: (0, 0))
    return _pcall(
        body, name=name, grid=(t_rows // tm, nj),
        in_specs=[row, row, vec, pl.BlockSpec((tm, per_step * IN_BLK), lambda i, j: (i, j)),
                  pl.BlockSpec((per_step, D_MODEL, IN_BLK), lambda i, j: (j, 0, 0))],
        out_specs=[row, vec],
        out_shape=[jax.ShapeDtypeStruct((t_rows, D_MODEL), F32), jax.ShapeDtypeStruct((1, D_MODEL), F32)],
        scratch_shapes=[pltpu.VMEM((tm, D_MODEL), F32)],
        compiler_params=_cp(("arbitrary", "arbitrary"), VMEM_BIG),
    )(dh, h, norm, dproj, wing)


def mix_backward_weights(hn, dproj, tm, name):
    t_rows = hn.shape[0]
    ni = t_rows // tm
    per_step = 2

    def body(hn_ref, dp_ref, dw_ref, acc):
        i = pl.program_id(1)
        _accumulate(acc, _dot_tn(hn_ref[...], dp_ref[...]), i == 0)

        @pl.when(i == ni - 1)
        def _():
            for k in range(per_step):
                dw_ref[k] = acc[:, k * IN_BLK:(k + 1) * IN_BLK].astype(BF16)

    return _pcall(
        body, name=name, grid=(N_DEV // per_step, ni),
        in_specs=[pl.BlockSpec((tm, D_MODEL), lambda j, i: (i, 0)),
                  pl.BlockSpec((tm, per_step * IN_BLK), lambda j, i: (i, j))],
        out_specs=pl.BlockSpec((per_step, D_MODEL, IN_BLK), lambda j, i: (j, 0, 0)),
        out_shape=jax.ShapeDtypeStruct((N_DEV, D_MODEL, IN_BLK), BF16),
        scratch_shapes=[pltpu.VMEM((D_MODEL, per_step * IN_BLK), F32)],
        compiler_params=_cp(("arbitrary", "arbitrary"), VMEM_BIG),
    )(hn, dproj)


GELU_C = 0.7978845608028654
GELU_K = 0.044715


def _gelu(x):
    return 0.5 * x * (1.0 + jnp.tanh(GELU_C * (x + GELU_K * (x * x * x))))


def _gelu_and_grad(x):
    th = jnp.tanh(GELU_C * (x + GELU_K * (x * x * x)))
    val = 0.5 * x * (1.0 + th)
    grad = 0.5 * (1.0 + th) + 0.5 * x * (1.0 - th * th) * (GELU_C * (1.0 + 3.0 * GELU_K * (x * x)))
    return val, grad


def merge_forward(h, yraw, attn, proj, glu_a, glu_b, w_out, tm, name):
    t_rows = h.shape[0]

    def body(h_ref, y_ref, at_ref, gate_ref, a_ref, b_ref, wo_ref, out_ref):
        y = _gelu(y_ref[...]).astype(BF16)
        ssm = _dot(y, a_ref[...]) * _sigmoid(_dot(y, b_ref[...]))
        ga = gate_ref[:, :D_MODEL].astype(F32)
        gs = gate_ref[:, D_MODEL:].astype(F32)
        merged = _sigmoid(ga) * at_ref[...].astype(F32) + _sigmoid(gs) * ssm
        out_ref[...] = h_ref[...] + _dot(merged.astype(BF16), wo_ref[...])

    row = pl.BlockSpec((tm, D_MODEL), lambda i: (i, 0))
    glu = pl.BlockSpec((SSM_WIDTH, D_MODEL), lambda i: (0, 0))
    return _pcall(
        body, name=name, grid=(t_rows // tm,),
        in_specs=[row, pl.BlockSpec((tm, SSM_WIDTH), lambda i: (i, 0)), row,
                  pl.BlockSpec((tm, 2 * D_MODEL), lambda i: (i, 1)), glu, glu,
                  pl.BlockSpec((D_MODEL, D_MODEL), lambda i: (0, 0))],
        out_specs=row, out_shape=jax.ShapeDtypeStruct((t_rows, D_MODEL), F32),
        compiler_params=_cp(("arbitrary",), VMEM_BIG),
    )(h, yraw, attn, proj, glu_a, glu_b, w_out)


def merge_backward(dh, yraw, attn, proj, glu_a, glu_b, w_out, tm, name):
    t_rows = dh.shape[0]

    def body(dh_ref, y_ref, at_ref, gate_ref, a_ref, b_ref, wo_ref,
             dat_ref, dy_ref, dgate_ref, d16_ref, mg_ref, y16_ref, dya_ref, dyb_ref):
        d16 = dh_ref[...].astype(BF16)
        d16_ref[...] = d16
        gel, dgel = _gelu_and_grad(y_ref[...].astype(F32))
        y16 = gel.astype(BF16)
        y16_ref[...] = y16
        dy = None
        for c0, cw in _col_chunks(D_MODEL):
            cols = slice(c0, c0 + cw)
            gcols = slice(D_MODEL + c0, D_MODEL + c0 + cw)
            dmerged = _dot_nt(d16, wo_ref[cols, :])
            ya = _dot(y16, a_ref[:, cols])
            sb = _sigmoid(_dot(y16, b_ref[:, cols]))
            ssm = ya * sb
            sa = _sigmoid(gate_ref[:, cols].astype(F32))
            ss = _sigmoid(gate_ref[:, gcols].astype(F32))
            attn_v = at_ref[:, cols].astype(F32)
            mg_ref[:, cols] = (sa * attn_v + ss * ssm).astype(BF16)
            dat_ref[:, cols] = (dmerged * sa).astype(BF16)
            dgate_ref[:, cols] = (dmerged * attn_v * sa * (1.0 - sa)).astype(BF16)
            dgate_ref[:, gcols] = (dmerged * ssm * ss * (1.0 - ss)).astype(BF16)
            dssm = dmerged * ss
            dya = (dssm * sb).astype(BF16)
            dyb = (dssm * ya * sb * (1.0 - sb)).astype(BF16)
            dya_ref[:, cols] = dya
            dyb_ref[:, cols] = dyb
            part = _dot_nt(dya, a_ref[:, cols]) + _dot_nt(dyb, b_ref[:, cols])
            dy = part if dy is None else dy + part
        dy_ref[...] = (dy * dgel).astype(BF16)

    row = pl.BlockSpec((tm, D_MODEL), lambda i: (i, 0))
    ssm_row = pl.BlockSpec((tm, SSM_WIDTH), lambda i: (i, 0))
    gates = pl.BlockSpec((tm, 2 * D_MODEL), lambda i: (i, 1))
    wide = jax.ShapeDtypeStruct((t_rows, D_MODEL), BF16)
    narrow = jax.ShapeDtypeStruct((t_rows, SSM_WIDTH), BF16)
    return _pcall(
        body, name=name, grid=(t_rows // tm,),
        in_specs=[row, ssm_row, row, gates, _resident((SSM_WIDTH, D_MODEL)), _resident((SSM_WIDTH, D_MODEL)),
                  _resident((D_MODEL, D_MODEL))],
        out_specs=[row, ssm_row, gates, row, row, ssm_row, row, row],
        out_shape=[wide, narrow, jax.ShapeDtypeStruct((t_rows, IN_WIDTH), BF16), wide, wide, narrow, wide, wide],
        compiler_params=_cp(("arbitrary",), VMEM_BIG),
    )(dh, yraw, attn, proj, glu_a, glu_b, w_out)


def merge_backward_weights(d16, merged, y16, dya, dyb, tm, name):
    t_rows = d16.shape[0]

    def body(d_ref, mg_ref, y_ref, dya_ref, dyb_ref, dwo_ref, da_ref, db_ref):
        first = pl.program_id(0) == 0
        y16 = y_ref[...]
        _accumulate(dwo_ref, _dot_tn(mg_ref[...], d_ref[...]), first)
        _accumulate(da_ref, _dot_tn(y16, dya_ref[...]), first)
        _accumulate(db_ref, _dot_tn(y16, dyb_ref[...]), first)

    row = pl.BlockSpec((tm, D_MODEL), lambda i: (i, 0))
    ssm_row = pl.BlockSpec((tm, SSM_WIDTH), lambda i: (i, 0))
    glu = pl.BlockSpec((SSM_WIDTH, D_MODEL), lambda i: (0, 0))
    wo = pl.BlockSpec((D_MODEL, D_MODEL), lambda i: (0, 0))
    return _pcall(
        body, name=name, grid=(t_rows // tm,),
        in_specs=[row, row, ssm_row, row, row], out_specs=[wo, glu, glu],
        out_shape=[jax.ShapeDtypeStruct((D_MODEL, D_MODEL), F32), jax.ShapeDtypeStruct((SSM_WIDTH, D_MODEL), F32),
                   jax.ShapeDtypeStruct((SSM_WIDTH, D_MODEL), F32)],
        compiler_params=_cp(("arbitrary",), VMEM_BIG),
    )(d16, merged, y16, dya, dyb)


def final_loss_backward(h, target, norm, seq, tm, name):
    t_rows = h.shape[0]
    tiles_per_example = (seq + N_META) // tm

    def body(h_ref, t_ref, g_ref, dh_ref, loss_ref, dg_ref):
        i = pl.program_id(0)
        x = h_ref[...]
        g = g_ref[...]
        r = lax.rsqrt(jnp.mean(x * x, axis=-1, keepdims=True) + NORM_EPS)
        xh = x * r
        pos = lax.broadcasted_iota(jnp.int32, (tm, 1), 0) + (i % tiles_per_example) * tm
        diff = jnp.where(pos < seq, xh * g - t_ref[...], 0.0)
        part = 0.5 * jnp.sum(jnp.sum(diff * diff, axis=-1, keepdims=True), axis=0, keepdims=True) / D_MODEL
        dy = diff / D_MODEL
        t = dy * g
        dh_ref[...] = r * (t - xh * jnp.mean(t * xh, axis=-1, keepdims=True))
        _accumulate(loss_ref, jnp.broadcast_to(part, (1, LANES)), i == 0)
        _accumulate(dg_ref, jnp.sum(dy * xh, axis=0, keepdims=True), i == 0)

    row = pl.BlockSpec((tm, D_MODEL), lambda i: (i, 0))
    vec = pl.BlockSpec((1, D_MODEL), lambda i: (0, 0))
    return _pcall(
        body, name=name, grid=(t_rows // tm,),
        in_specs=[row, row, vec],
        out_specs=[row, pl.BlockSpec((1, LANES), lambda i: (0, 0)), vec],
        out_shape=[jax.ShapeDtypeStruct((t_rows, D_MODEL), F32), jax.ShapeDtypeStruct((1, LANES), F32),
                   jax.ShapeDtypeStruct((1, D_MODEL), F32)],
        compiler_params=_cp(("arbitrary",), VMEM_BIG),
    )(h, target, norm)


ATTN_SCALE = HEAD_DIM ** -0.5
STACK_HEADS = (0, 2, 1, 3)


def _lane_half(shape, hf):
    lane = lax.broadcasted_iota(jnp.int32, shape, 1)
    return (lane < HEAD_DIM) if hf == 0 else (lane >= HEAD_DIM)


def _kv_variants(ref, rows, kh):
    tile = kh // 2
    t = ref[rows, tile * LANES:(tile + 1) * LANES].astype(F32)
    swapped = pltpu.roll(t, HEAD_DIM, axis=1)
    at_low, at_high = (t, swapped) if kh % 2 == 0 else (swapped, t)
    lo = jnp.where(_lane_half(t.shape, 0), at_low, 0.0).astype(BF16)
    hi = jnp.where(_lane_half(t.shape, 1), at_high, 0.0).astype(BF16)
    return lo, hi


def _to_kv_lanes(lo, hi, kh):
    lo = jnp.where(_lane_half(lo.shape, 0), lo, 0.0)
    hi = jnp.where(_lane_half(hi.shape, 1), hi, 0.0)
    if kh % 2 == 0:
        return lo + pltpu.roll(hi, HEAD_DIM, axis=1)
    return pltpu.roll(lo, HEAD_DIM, axis=1) + hi


def _stacked(ref, rows, kh):
    col = kh * 2 * LANES
    return jnp.concatenate([ref[rows, col:col + LANES], ref[rows, col + LANES:col + 2 * LANES]], axis=0)


def _sink_column(sink_ref, kh, nq):
    row = lax.broadcasted_iota(jnp.int32, (4 * nq, 1), 0)
    col = jnp.zeros((4 * nq, 1), F32)
    for quarter, g in enumerate(STACK_HEADS):
        col = jnp.where(row // nq == quarter, sink_ref[0, kh * Q_PER_KV + g], col)
    return col


def _softmax_parts(qs, key_tiles, masks, sink):
    scores = []
    for (k_lo, k_hi), mask in zip(key_tiles, masks):
        s = jnp.concatenate([_dot_nt(qs, k_lo), _dot_nt(qs, k_hi)], axis=0) * ATTN_SCALE
        scores.append(s if mask is None else jnp.where(mask, s, NEG_INF))
    m = functools.reduce(jnp.maximum, [jnp.max(s, axis=-1, keepdims=True) for s in scores])
    m = jnp.maximum(m, sink)
    probs = [jnp.exp(s - m) for s in scores]
    e_sink = jnp.exp(sink - m)
    den = functools.reduce(lambda u, w: u + w, [jnp.sum(p, axis=-1, keepdims=True) for p in probs]) + e_sink
    return probs, 1.0 / den, e_sink


def _band_mask(nq, first):
    keys = BLOCK if first else 2 * BLOCK
    qi = lax.broadcasted_iota(jnp.int32, (4 * nq, keys), 0) % nq
    kj = lax.broadcasted_iota(jnp.int32, (4 * nq, keys), 1)
    if first:
        return kj <= qi
    return jnp.logical_and(kj > qi, kj <= qi + BLOCK)


def _meta_mask():
    qi = lax.broadcasted_iota(jnp.int32, (4 * N_META, N_META), 0) % N_META
    kj = lax.broadcasted_iota(jnp.int32, (4 * N_META, N_META), 1)
    return kj <= qi


def _attention_schedule(seq, queries, carry):
    meta_rows = pl.ds(seq, N_META)
    carry = queries(pl.ds(0, BLOCK), BLOCK, [pl.ds(0, BLOCK), meta_rows], [_band_mask(BLOCK, True), None], carry)

    def block(n, c):
        r0 = pl.multiple_of(n * BLOCK, BLOCK)
        p0 = pl.multiple_of((n - 1) * BLOCK, BLOCK)
        return queries(pl.ds(r0, BLOCK), BLOCK, [pl.ds(p0, 2 * BLOCK), meta_rows], [_band_mask(BLOCK, False), None], c)

    carry = lax.fori_loop(1, seq // BLOCK, block, carry)
    return queries(meta_rows, N_META, [meta_rows], [_meta_mask()], carry)


def attention_forward(proj3, sinks, seq, name):
    n_b, n_l, _ = proj3.shape

    def body(sink_ref, q_ref, k_ref, v_ref, o_ref):
        def queries(q_rows, nq, key_rows, masks, carry):
            for kh in range(N_KV_HEADS):
                ks = [_kv_variants(k_ref, r, kh) for r in key_rows]
                vs = [_kv_variants(v_ref, r, kh) for r in key_rows]
                qs = _stacked(q_ref, q_rows, kh)
                probs, inv, _ = _softmax_parts(qs, ks, masks, _sink_column(sink_ref, kh, nq))
                probs = [p.astype(BF16) for p in probs]
                o_lo = functools.reduce(lambda u, w: u + w, [_dot(p[:2 * nq], v_lo) for p, (v_lo, _) in zip(probs, vs)])
                o_hi = functools.reduce(lambda u, w: u + w, [_dot(p[2 * nq:], v_hi) for p, (_, v_hi) in zip(probs, vs)])
                out = (o_lo * inv[:2 * nq] + o_hi * inv[2 * nq:]).astype(BF16)
                col = kh * 2 * LANES
                o_ref[q_rows, col:col + LANES] = out[:nq]
                o_ref[q_rows, col + LANES:col + 2 * LANES] = out[nq:]
            return carry

        _attention_schedule(seq, queries, 0)

    return _pcall(
        body, name=name, grid=(n_b,),
        in_specs=[pl.BlockSpec(memory_space=pltpu.SMEM),
                  pl.BlockSpec((None, n_l, D_MODEL), lambda b: (b, 0, 0)),
                  pl.BlockSpec((None, n_l, KV_WIDTH), lambda b: (b, 0, D_MODEL // KV_WIDTH)),
                  pl.BlockSpec((None, n_l, KV_WIDTH), lambda b: (b, 0, D_MODEL // KV_WIDTH + 1))],
        out_specs=pl.BlockSpec((None, n_l, D_MODEL), lambda b: (b, 0, 0)),
        out_shape=jax.ShapeDtypeStruct((n_b, n_l, D_MODEL), BF16),
        compiler_params=_cp(("arbitrary",), VMEM_BIG),
    )(sinks, proj3, proj3, proj3)


def attention_backward(proj3, dattn3, dproj3, sinks, seq, name):
    n_b, n_l, _ = proj3.shape
    qkv_width = D_MODEL + 2 * KV_WIDTH

    def body(sink_ref, q_ref, k_ref, v_ref, do_ref, _, dqkv_ref, dsink_ref, dk_ref, dv_ref):
        dk_ref[...] = jnp.zeros_like(dk_ref)
        dv_ref[...] = jnp.zeros_like(dv_ref)
        sub = lax.broadcasted_iota(jnp.int32, (SUBLANES, LANES), 0)
        lane = lax.broadcasted_iota(jnp.int32, (SUBLANES, LANES), 1)

        def queries(q_rows, nq, key_rows, masks, dsink):
            for kh in range(N_KV_HEADS):
                ks = [_kv_variants(k_ref, r, kh) for r in key_rows]
                vs = [_kv_variants(v_ref, r, kh) for r in key_rows]
                qs = _stacked(q_ref, q_rows, kh)
                dos = _stacked(do_ref, q_rows, kh)
                probs, inv, e_sink = _softmax_parts(qs, ks, masks, _sink_column(sink_ref, kh, nq))
                probs = [p * inv for p in probs]
                dps = [jnp.concatenate([_dot_nt(dos, v_lo), _dot_nt(dos, v_hi)], axis=0) for v_lo, v_hi in vs]
                delta = functools.reduce(
                    lambda u, w: u + w, [jnp.sum(p * dp, axis=-1, keepdims=True) for p, dp in zip(probs, dps)])
                d_sink = -(e_sink * inv) * delta
                for quarter, g in enumerate(STACK_HEADS):
                    d_here = jnp.sum(d_sink[quarter * nq:(quarter + 1) * nq], axis=0, keepdims=True)
                    dsink = dsink + jnp.where(jnp.logical_and(sub == 0, lane == kh * Q_PER_KV + g), d_here, 0.0)
                dq = None
                tile = slice((kh // 2) * LANES, (kh // 2 + 1) * LANES)
                for r, p, dp, (k_lo, k_hi) in zip(key_rows, probs, dps, ks):
                    ds = (p * (dp - delta)).astype(BF16)
                    p16 = p.astype(BF16)
                    dq_x = _dot(ds[:2 * nq], k_lo) + _dot(ds[2 * nq:], k_hi)
                    dq = dq_x if dq is None else dq + dq_x
                    dk_ref[r, tile] += _to_kv_lanes(_dot_tn(ds[:2 * nq], qs), _dot_tn(ds[2 * nq:], qs), kh) * ATTN_SCALE
                    dv_ref[r, tile] += _to_kv_lanes(_dot_tn(p16[:2 * nq], dos), _dot_tn(p16[2 * nq:], dos), kh)
                dq = (dq * ATTN_SCALE).astype(BF16)
                col = kh * 2 * LANES
                dqkv_ref[q_rows, col:col + LANES] = dq[:nq]
                dqkv_ref[q_rows, col + LANES:col + 2 * LANES] = dq[nq:]
            return dsink

        dsink_ref[...] = _attention_schedule(seq, queries, jnp.zeros((SUBLANES, LANES), F32))
        dqkv_ref[:, D_MODEL:D_MODEL + KV_WIDTH] = dk_ref[...].astype(BF16)
        dqkv_ref[:, D_MODEL + KV_WIDTH:] = dv_ref[...].astype(BF16)

    return _pcall(
        body, name=name, grid=(n_b,),
        in_specs=[pl.BlockSpec(memory_space=pltpu.SMEM),
                  pl.BlockSpec((None, n_l, D_MODEL), lambda b: (b, 0, 0)),
                  pl.BlockSpec((None, n_l, KV_WIDTH), lambda b: (b, 0, D_MODEL // KV_WIDTH)),
                  pl.BlockSpec((None, n_l, KV_WIDTH), lambda b: (b, 0, D_MODEL // KV_WIDTH + 1)),
                  pl.BlockSpec((None, n_l, D_MODEL), lambda b: (b, 0, 0)),
                  pl.BlockSpec(memory_space=pl.ANY)],
        out_specs=[pl.BlockSpec((None, n_l, qkv_width), lambda b: (b, 0, 0)),
                   pl.BlockSpec((None, SUBLANES, LANES), lambda b: (b, 0, 0))],
        out_shape=[jax.ShapeDtypeStruct(dproj3.shape, BF16), jax.ShapeDtypeStruct((n_b, SUBLANES, LANES), F32)],
        scratch_shapes=[pltpu.VMEM((n_l, KV_WIDTH), F32), pltpu.VMEM((n_l, KV_WIDTH), F32)],
        input_output_aliases={5: 0},
        compiler_params=_cp(("arbitrary",), VMEM_BIG),
    )(sinks, proj3, proj3, proj3, dattn3, dproj3)


TAB_ROWS = 8
SCAN_UNROLL = 4


def _cmul(ar, ai, br, bi):
    return ar * br - ai * bi, ar * bi + ai * br


def _discretise(ar, ai, ls):
    step = jnp.exp(ls)
    mag = jnp.exp(ar * step)
    ang = ai * step
    cos, sin = jnp.cos(ang), jnp.sin(ang)
    lr, li = mag * cos, mag * sin
    den = ar * ar + ai * ai
    nr, ni = lr - 1.0, li
    cr = (nr * ar + ni * ai) / den
    ci = (ni * ar - nr * ai) / den
    return step, mag, lr, li, den, nr, ni, cr, ci


def _scan_tables(lr, li, reverse):
    n = lr.shape[-1]
    pw = [(lr, li)]
    for _ in range(SUBLANES - 1):
        pw.append(_cmul(pw[-1][0], pw[-1][1], lr, li))
    row = lax.broadcasted_iota(jnp.int32, (SUBLANES, n), 0)
    out = []
    for d in (1, 2, 4):
        ok = (row + d <= SUBLANES - 1) if reverse else (row >= d)
        out += [jnp.where(ok, pw[d - 1][0], 0.0), jnp.where(ok, pw[d - 1][1], 0.0)]
    cr = jnp.zeros((SUBLANES, n), F32)
    ci = jnp.zeros((SUBLANES, n), F32)
    for r in range(SUBLANES):
        e = (SUBLANES - r) if reverse else (r + 1)
        cr = jnp.where(row == r, pw[e - 1][0], cr)
        ci = jnp.where(row == r, pw[e - 1][1], ci)
    return out + [cr, ci]


def ssm_prepare(ar, ai, ls, br_t, bi_t, name):
    def body(ar_ref, ai_ref, ls_ref, br_ref, bi_ref, bbr_ref, bbi_ref, tf_ref, tr_ref):
        _, _, lr, li, _, _, _, cr, ci = _discretise(ar_ref[...], ai_ref[...], ls_ref[...])
        br, bi = br_ref[...], bi_ref[...]
        bbr_ref[...] = cr * br - ci * bi
        bbi_ref[...] = cr * bi + ci * br
        for k, t in enumerate(_scan_tables(lr, li, False)):
            tf_ref[k] = t
        for k, t in enumerate(_scan_tables(lr, -li, True)):
            tr_ref[k] = t

    return _pcall(
        body, name=name,
        out_shape=[jax.ShapeDtypeStruct((SSM_GROUP, N_STATES), F32), jax.ShapeDtypeStruct((SSM_GROUP, N_STATES), F32),
                   jax.ShapeDtypeStruct((TAB_ROWS, SUBLANES, N_STATES), F32),
                   jax.ShapeDtypeStruct((TAB_ROWS, SUBLANES, N_STATES), F32)],
    )(ar, ai, ls, br_t, bi_t)


def ssm_param_backward(ar, ai, ls, br_t, bi_t, dlr_p, dli_p, dbbr, dbbi, group_sum, name):
    def body(ar_ref, ai_ref, ls_ref, br_ref, bi_ref, dlr_ref, dli_ref, dbbr_ref, dbbi_ref, gs_ref,
             dar_ref, dai_ref, dls_ref, dbr_ref, dbi_ref):
        ar, ai = ar_ref[...], ai_ref[...]
        step, mag, lr, li, den, nr, ni, cr, ci = _discretise(ar, ai, ls_ref[...])
        br, bi, dbbr_v, dbbi_v = br_ref[...], bi_ref[...], dbbr_ref[...], dbbi_ref[...]
        dbr_ref[...] = cr * dbbr_v + ci * dbbi_v
        dbi_ref[...] = cr * dbbi_v - ci * dbbr_v
        dcr = jnp.sum(dbbr_v * br + dbbi_v * bi, axis=0, keepdims=True)
        dci = jnp.sum(dbbi_v * br - dbbr_v * bi, axis=0, keepdims=True)
        dnr = (dcr * ar - dci * ai) / den
        dni = (dcr * ai + dci * ar) / den
        dden = -(cr * dcr + ci * dci) / den
        dar = (dcr * nr + dci * ni) / den + dden * 2.0 * ar
        dai = (dcr * ni - dci * nr) / den + dden * 2.0 * ai
        dlr = jnp.sum(dlr_ref[...], axis=0, keepdims=True) + dnr
        dli = jnp.sum(dli_ref[...], axis=0, keepdims=True) + dni
        dmag = (dlr * lr + dli * li) / mag
        dang = dli * lr - dlr * li
        dar_ref[...] = dar + dmag * mag * step
        dai_ref[...] = dai + dang * step
        dstep = dmag * mag * ar + dang * ai
        dls_ref[...] = jnp.dot(dstep * step, gs_ref[...], preferred_element_type=F32, precision=lax.Precision.HIGHEST)

    vec = jax.ShapeDtypeStruct((1, N_STATES), F32)
    mat = jax.ShapeDtypeStruct((SSM_GROUP, N_STATES), F32)
    return _pcall(body, name=name, out_shape=[vec, vec, jax.ShapeDtypeStruct((1, LANES), F32), mat, mat])(
        ar, ai, ls, br_t, bi_t, dlr_p, dli_p, dbbr, dbbi, group_sum)


def _scan_rows(a, b, tabs, carry, reverse):
    for k, d in enumerate((1, 2, 4)):
        shift = SUBLANES - d if reverse else d
        sr, si = pltpu.roll(a, shift, axis=0), pltpu.roll(b, shift, axis=0)
        pr, pi = _cmul(tabs[2 * k], tabs[2 * k + 1], sr, si)
        a, b = a + pr, b + pi
    pr, pi = _cmul(tabs[6], tabs[7], carry[0], carry[1])
    return a + pr, b + pi


def _time_groups(seq, reverse):
    meta = [seq + SUBLANES * g for g in range(N_META // SUBLANES)]
    return meta[::-1] if reverse else meta


def ssm_forward_scan(proj3, b_comb, tabf, c_comb, dvec, seq, name):
    n_b, n_l, _ = proj3.shape
    u_blk = (D_MODEL + 2 * KV_WIDTH) // LANES

    def body(u_ref, b_ref, tab_ref, c_ref, d_ref, x_ref, y_ref, bu, xs):
        j = pl.program_id(1)
        u = u_ref[...]
        bu[...] = _dot(u, b_ref[...])
        tabs = [tab_ref[k] for k in range(TAB_ROWS)]

        def group(r0, carry):
            rows = pl.ds(r0, SUBLANES)
            a, b = _scan_rows(bu[rows, :SCAN_COLS], bu[rows, SCAN_COLS:], tabs, carry, False)
            xs[rows, :SCAN_COLS] = a
            xs[rows, SCAN_COLS:] = b
            return (jnp.broadcast_to(a[SUBLANES - 1:, :], a.shape), jnp.broadcast_to(b[SUBLANES - 1:, :], b.shape))

        zero = jnp.zeros((SUBLANES, SCAN_COLS), F32)
        carry = (zero, zero)
        for r0 in _time_groups(seq, False):
            carry = group(r0, carry)
        span = SCAN_UNROLL * SUBLANES

        def groups(t, c):
            for k in range(SCAN_UNROLL):
                c = group(pl.multiple_of(t * span, span) + k * SUBLANES, c)
            return c

        lax.fori_loop(0, seq // span, groups, carry)
        x16 = xs[...].astype(BF16)
        x_ref[...] = x16
        contrib = _dot(x16, c_ref[...])

        @pl.when(j % 2 == 0)
        def _():
            y_ref[...] = contrib + d_ref[...] * u.astype(F32)

        @pl.when(j % 2 == 1)
        def _():
            y_ref[...] += contrib

    return _pcall(
        body, name=name, grid=(n_b, N_SCAN_BLK),
        in_specs=[pl.BlockSpec((None, n_l, LANES), lambda b, j: (b, 0, u_blk + j // 2)),
                  pl.BlockSpec((LANES, 2 * SCAN_COLS), lambda b, j: (j // 2, j)),
                  pl.BlockSpec((TAB_ROWS, SUBLANES, SCAN_COLS), lambda b, j: (0, 0, j)),
                  pl.BlockSpec((2 * SCAN_COLS, LANES), lambda b, j: (j, j // 2)),
                  pl.BlockSpec((1, LANES), lambda b, j: (0, j // 2))],
        out_specs=[pl.BlockSpec((None, n_l, 2 * SCAN_COLS), lambda b, j: (b, 0, j)),
                   pl.BlockSpec((None, n_l, LANES), lambda b, j: (b, 0, j // 2))],
        out_shape=[jax.ShapeDtypeStruct((n_b, n_l, 2 * N_STATES), BF16),
                   jax.ShapeDtypeStruct((n_b, n_l, SSM_WIDTH), F32)],
        scratch_shapes=[pltpu.VMEM((n_l, 2 * SCAN_COLS), F32)] * 2,
        compiler_params=_cp(("arbitrary", "arbitrary"), VMEM_BIG),
    )(proj3, b_comb, tabf, c_comb, dvec)


def ssm_backward_scan(dyraw3, xs3, dproj3, c_comb_t, tabr, b_comb_t, dvec, seq, name):
    n_b, n_l, _ = xs3.shape
    u_blk = (D_MODEL + 2 * KV_WIDTH) // LANES

    def body(dy_ref, x_ref, _, c_ref, tab_ref, b_ref, d_ref, du_ref, g_ref, dlr_ref, dli_ref, dx, gs, xs, du_acc):
        j = pl.program_id(1)
        dy = dy_ref[...]
        dx[...] = _dot(dy, c_ref[...])
        xs[...] = x_ref[...].astype(F32)
        tabs = [tab_ref[k] for k in range(TAB_ROWS)]
        last_row = lax.broadcasted_iota(jnp.int32, (SUBLANES, SCAN_COLS), 0) == SUBLANES - 1

        def group(r0, state):
            cr, ci, acc_r, acc_i = state
            rows = pl.ds(r0, SUBLANES)
            a, b = _scan_rows(dx[rows, :SCAN_COLS], dx[rows, SCAN_COLS:], tabs, (cr, ci), True)
            gs[rows, :SCAN_COLS] = a
            gs[rows, SCAN_COLS:] = b
            na = jnp.where(last_row, cr, pltpu.roll(a, SUBLANES - 1, axis=0))
            nb = jnp.where(last_row, ci, pltpu.roll(b, SUBLANES - 1, axis=0))
            xa, xb = xs[rows, :SCAN_COLS], xs[rows, SCAN_COLS:]
            return (jnp.broadcast_to(a[:1, :], a.shape), jnp.broadcast_to(b[:1, :], b.shape),
                    acc_r + na * xa + nb * xb, acc_i + nb * xa - na * xb)

        zero = jnp.zeros((SUBLANES, SCAN_COLS), F32)
        span = SCAN_UNROLL * SUBLANES
        n_spans = seq // span

        def groups(t, s):
            for k in reversed(range(SCAN_UNROLL)):
                s = group(pl.multiple_of((n_spans - 1 - t) * span, span) + k * SUBLANES, s)
            return s

        state = lax.fori_loop(0, n_spans, groups, (zero, zero, zero, zero))
        for r0 in _time_groups(seq, True):
            state = group(r0, state)
        dlr_ref[...] = state[2]
        dli_ref[...] = state[3]
        g16 = gs[...].astype(BF16)
        g_ref[...] = g16
        contrib = _dot(g16, b_ref[...])

        @pl.when(j % 2 == 0)
        def _():
            du_acc[...] = contrib + d_ref[...] * dy.astype(F32)

        @pl.when(j % 2 == 1)
        def _():
            du_ref[...] = (du_acc[...] + contrib).astype(BF16)

    state_blk = pl.BlockSpec((None, n_l, 2 * SCAN_COLS), lambda b, j: (b, 0, j))
    dl_blk = pl.BlockSpec((None, SUBLANES, SCAN_COLS), lambda b, j: (b, 0, j))
    return _pcall(
        body, name=name, grid=(n_b, N_SCAN_BLK),
        in_specs=[pl.BlockSpec((None, n_l, LANES), lambda b, j: (b, 0, j // 2)), state_blk,
                  pl.BlockSpec(memory_space=pl.ANY),
                  pl.BlockSpec((LANES, 2 * SCAN_COLS), lambda b, j: (j // 2, j)),
                  pl.BlockSpec((TAB_ROWS, SUBLANES, SCAN_COLS), lambda b, j: (0, 0, j)),
                  pl.BlockSpec((2 * SCAN_COLS, LANES), lambda b, j: (j, j // 2)),
                  pl.BlockSpec((1, LANES), lambda b, j: (0, j // 2))],
        out_specs=[pl.BlockSpec((None, n_l, LANES), lambda b, j: (b, 0, u_blk + j // 2)), state_blk, dl_blk, dl_blk],
        out_shape=[jax.ShapeDtypeStruct(dproj3.shape, BF16), jax.ShapeDtypeStruct((n_b, n_l, 2 * N_STATES), BF16),
                   jax.ShapeDtypeStruct((n_b, SUBLANES, N_STATES), F32), jax.ShapeDtypeStruct((n_b, SUBLANES, N_STATES), F32)],
        scratch_shapes=[pltpu.VMEM((n_l, 2 * SCAN_COLS), F32)] * 3 + [pltpu.VMEM((n_l, LANES), F32)],
        input_output_aliases={2: 0},
        compiler_params=_cp(("arbitrary", "arbitrary"), VMEM_BIG),
    )(dyraw3, xs3, dproj3, c_comb_t, tabr, b_comb_t, dvec)


def ssm_param_grads(proj, gs, xs, dyraw, tm, name):
    t_rows = proj.shape[0]
    ni = t_rows // tm
    u_blk = (D_MODEL + 2 * KV_WIDTH) // LANES
    width = 2 * SCAN_COLS

    def body(u_ref, g_ref, x_ref, dy_ref, db_ref, dc_ref, dd_ref):
        cb, i = pl.program_id(0), pl.program_id(1)
        u, dy = u_ref[...], dy_ref[...]
        _accumulate(db_ref, _dot_tn(u, g_ref[...]), i == 0)
        _accumulate(dc_ref, _dot_tn(x_ref[...], dy), i == 0)

        @pl.when(cb % 2 == 0)
        def _():
            _accumulate(dd_ref, jnp.sum(dy.astype(F32) * u.astype(F32), axis=0, keepdims=True), i == 0)

    return _pcall(
        body, name=name, grid=(N_SCAN_BLK, ni),
        in_specs=[pl.BlockSpec((tm, LANES), lambda cb, i: (i, u_blk + cb // 2)),
                  pl.BlockSpec((tm, width), lambda cb, i: (i, cb)),
                  pl.BlockSpec((tm, width), lambda cb, i: (i, cb)),
                  pl.BlockSpec((tm, LANES), lambda cb, i: (i, cb // 2))],
        out_specs=[pl.BlockSpec((None, LANES, width), lambda cb, i: (cb, 0, 0)),
                   pl.BlockSpec((None, width, LANES), lambda cb, i: (cb, 0, 0)),
                   pl.BlockSpec((1, LANES), lambda cb, i: (0, cb // 2))],
        out_shape=[jax.ShapeDtypeStruct((N_SCAN_BLK, LANES, width), F32),
                   jax.ShapeDtypeStruct((N_SCAN_BLK, width, LANES), F32), jax.ShapeDtypeStruct((1, SSM_WIDTH), F32)],
        compiler_params=_cp(("arbitrary", "arbitrary"), VMEM_BIG),
    )(proj, gs, xs, dyraw)


def sum_leading(x, name):
    def body(x_ref, o_ref):
        acc = x_ref[0]
        for k in range(1, x.shape[0]):
            acc = acc + x_ref[k]
        o_ref[...] = acc

    return _pcall(body, name=name, out_shape=jax.ShapeDtypeStruct(x.shape[1:], x.dtype))(x)


WEIGHTS = ['meta_tokens', 'ffn1_norm', 'ffn1_w1', 'ffn1_w3', 'ffn1_w2', 'mix_norm', 'w_in', 'attn_sinks', 'ssm_a_re',
           'ssm_a_im', 'ssm_log_step', 'ssm_b_re', 'ssm_b_im', 'ssm_c_re', 'ssm_c_im', 'ssm_d', 'ssm_glu_a', 'ssm_glu_b',
           'w_out', 'ffn2_norm', 'ffn2_w1', 'ffn2_w3', 'ffn2_w2', 'final_norm']
SHARDED = ['ffn1_w1', 'ffn1_w3', 'ffn1_w2', 'ffn2_w1', 'ffn2_w3', 'ffn2_w2', 'w_in', 'ssm_glu_a', 'ssm_glu_b', 'w_out']
REPLICATED = ['ffn1_norm', 'mix_norm', 'ffn2_norm', 'final_norm', 'attn_sinks', 'ssm_a_re', 'ssm_a_im', 'ssm_log_step',
              'ssm_b_re', 'ssm_b_im', 'ssm_c_re', 'ssm_c_im', 'ssm_d']
PACK_COLS = 1024


def _block_diag(blocks):
    g, r, c = blocks.shape
    eye = jnp.eye(g, dtype=blocks.dtype)
    return (blocks[:, :, None, :] * eye[:, None, :, None]).reshape(g * r, g * c)


def _diag_blocks(mat, r, c):
    g = SSM_GROUPS
    eye = jnp.eye(g, dtype=mat.dtype)
    return jnp.sum(mat.reshape(g, r, g, c) * eye[:, None, :, None], axis=2)


def _scan_order(re, im):
    r = re.shape[0]
    return jnp.stack([re.reshape(r, N_SCAN_BLK, SCAN_COLS), im.reshape(r, N_SCAN_BLK, SCAN_COLS)], axis=2).reshape(r, 2 * N_STATES)


def _from_scan_order(comb):
    r = comb.shape[0]
    c4 = comb.reshape(r, N_SCAN_BLK, 2, SCAN_COLS)
    return c4[:, :, 0].reshape(r, N_STATES), c4[:, :, 1].reshape(r, N_STATES)


def _pack(arrays):
    parts = []
    for a in arrays:
        flat = a.reshape(-1)
        chunk = SUBLANES * PACK_COLS
        padded = -(-flat.shape[0] // chunk) * chunk
        parts.append(jnp.pad(flat, (0, padded - flat.shape[0])).reshape(-1, PACK_COLS))
    return jnp.concatenate(parts, axis=0)


def _unpack(packed, shapes):
    out, row = [], 0
    for shape in shapes:
        size = 1
        for s in shape:
            size *= s
        chunk = SUBLANES * PACK_COLS
        rows = -(-size // chunk) * SUBLANES
        out.append(packed[row:row + rows].reshape(-1)[:size].reshape(shape))
        row += rows
    return out


def kernel(x, meta_tokens, ffn1_norm, ffn1_w1, ffn1_w3, ffn1_w2, mix_norm, w_in, attn_sinks, ssm_a_re, ssm_a_im, ssm_log_step, ssm_b_re, ssm_b_im, ssm_c_re, ssm_c_im, ssm_d, ssm_glu_a, ssm_glu_b, w_out, ffn2_norm, ffn2_w1, ffn2_w3, ffn2_w2, final_norm, loss_target, m_meta_tokens, m_ffn1_norm, m_ffn1_w1, m_ffn1_w3, m_ffn1_w2, m_mix_norm, m_w_in, m_attn_sinks, m_ssm_a_re, m_ssm_a_im, m_ssm_log_step, m_ssm_b_re, m_ssm_b_im, m_ssm_c_re, m_ssm_c_im, m_ssm_d, m_ssm_glu_a, m_ssm_glu_b, m_w_out, m_ffn2_norm, m_ffn2_w1, m_ffn2_w3, m_ffn2_w2, m_final_norm, v_meta_tokens, v_ffn1_norm, v_ffn1_w1, v_ffn1_w3, v_ffn1_w2, v_mix_norm, v_w_in, v_attn_sinks, v_ssm_a_re, v_ssm_a_im, v_ssm_log_step, v_ssm_b_re, v_ssm_b_im, v_ssm_c_re, v_ssm_c_im, v_ssm_d, v_ssm_glu_a, v_ssm_glu_b, v_w_out, v_ffn2_norm, v_ffn2_w1, v_ffn2_w3, v_ffn2_w2, v_final_norm):
    given = dict(locals())
    w = {n: given[n] for n in WEIGHTS}
    m = {n: given["m_" + n] for n in WEIGHTS}
    v = {n: given["v_" + n] for n in WEIGHTS}

    n_b, seq, _ = x.shape
    n_l = seq + N_META
    t_rows = n_b * n_l
    tm = _row_tile(n_l, 688)
    px, py, pc = _my_place()
    me = 4 * px + 2 * py + pc
    dest = jnp.stack([4 * qx + 2 * qy + pc for qx, qy in
                      [(px, py), (1 - px, py), (px, 1 - py), (1 - px, 1 - py)]]).astype(jnp.int32)

    glu = jnp.stack([ssm_glu_a[0], ssm_glu_b[0]]).astype(BF16)
    ffn_names = ['ffn1_w1', 'ffn1_w3', 'ffn1_w2', 'ffn2_w1', 'ffn2_w3', 'ffn2_w2']

    def hidden_on_rows(n, t):
        return t[0] if n.endswith('w2') else t[0].T

    def hidden_on_rows_back(n, t):
        return t[None] if n.endswith('w2') else t.T[None]

    gathered = all_gather_list(
        [hidden_on_rows(n, w[n]).astype(BF16) for n in ffn_names]
        + [w_in[0].astype(BF16), glu, w_out[0].astype(BF16), meta_tokens], "ag_weights")
    wing, glug, wog, metag = gathered[len(ffn_names):]
    full = {n: g.reshape(D_FF, D_MODEL) for n, g in zip(ffn_names, gathered)}
    glu_a = glug[:, 0].transpose(1, 0, 2).reshape(SSM_WIDTH, D_MODEL)
    glu_b = glug[:, 1].transpose(1, 0, 2).reshape(SSM_WIDTH, D_MODEL)
    w_out_full = wog.reshape(D_MODEL, D_MODEL)
    meta_full = metag.transpose(1, 0, 2).reshape(N_META, D_MODEL)

    h0 = jnp.concatenate([x, jnp.broadcast_to(meta_full[None], (n_b, N_META, D_MODEL))], axis=1).reshape(t_rows, D_MODEL)
    target = jnp.concatenate([loss_target, jnp.zeros((n_b, N_META, D_MODEL), F32)], axis=1).reshape(t_rows, D_MODEL)
    final_g = final_norm.reshape(1, D_MODEL)

    ar = ssm_a_re.reshape(1, N_STATES)
    ai = ssm_a_im.reshape(1, N_STATES)
    ls = jnp.repeat(ssm_log_step.reshape(SSM_GROUPS), SSM_STATE).reshape(1, N_STATES)
    br_t = ssm_b_re[0].transpose(2, 0, 1).reshape(SSM_GROUP, N_STATES)
    bi_t = ssm_b_im[0].transpose(2, 0, 1).reshape(SSM_GROUP, N_STATES)
    bbr, bbi, tabf, tabr = ssm_prepare(ar, ai, ls, br_t, bi_t, "ssm_prepare")
    bbr_g = bbr.reshape(SSM_GROUP, SSM_GROUPS, SSM_STATE).transpose(1, 0, 2)
    bbi_g = bbi.reshape(SSM_GROUP, SSM_GROUPS, SSM_STATE).transpose(1, 0, 2)
    b_comb = _scan_order(_block_diag(bbr_g), _block_diag(bbi_g)).astype(BF16)
    c_comb_t = _scan_order(_block_diag(ssm_c_re[0]), -_block_diag(ssm_c_im[0])).astype(BF16)
    b_comb_t, c_comb = b_comb.T, c_comb_t.T

    ffn1_w = (full['ffn1_w1'], full['ffn1_w3'], full['ffn1_w2'])
    ffn2_w = (full['ffn2_w1'], full['ffn2_w3'], full['ffn2_w2'])
    h1, hn1, a1, b1 = ffn_forward(h0, ffn1_norm, *ffn1_w, tm, FF_FWD_COLS, "ffn1_fwd")
    hnm, proj = mix_forward(h1, mix_norm, wing, tm, "mix_fwd")
    proj3 = proj.reshape(n_b, n_l, IN_WIDTH)
    attn3 = attention_forward(proj3, attn_sinks, seq, "attn_fwd")
    attn = attn3.reshape(t_rows, D_MODEL)
    xs3, yraw3 = ssm_forward_scan(proj3, b_comb, tabf, c_comb, ssm_d, seq, "ssm_fwd")
    yraw = yraw3.reshape(t_rows, SSM_WIDTH)
    h2 = merge_forward(h1, yraw, attn, proj, glu_a, glu_b, w_out_full, tm, "merge_fwd")
    h3, hn2, a2, b2 = ffn_forward(h2, ffn2_norm, *ffn2_w, tm, FF_FWD_COLS, "ffn2_fwd")
    dh3, loss_part, g_final = final_loss_backward(h3, target, final_g, seq, tm, "loss_bwd")
    loss = lax.psum(loss_part[0, 0], ("x", "y", "c"))

    def blocked_ffn(d_w1t, d_w3t, d_w2):
        return tuple(t.reshape(N_DEV, FF_BLK, D_MODEL) for t in (d_w1t, d_w3t, d_w2))

    da2, db2, dh3_half = ffn_backward_hidden(dh3, a2, b2, ffn2_w[2], tm, "ffn2_bwd_hid")
    dh2, g_ffn2_norm = ffn_backward_input(dh3, h2, ffn2_norm, da2, db2, ffn2_w[0], ffn2_w[1], tm, "ffn2_bwd_in")
    dw = {}
    dw['ffn2_w1'], dw['ffn2_w3'], dw['ffn2_w2'] = blocked_ffn(
        *ffn_backward_weights(hn2, dh3_half, a2, b2, da2, db2, n_l, FF_BWD_COLS, "ffn2_bwd_w"))
    dattn, dyraw, dproj, *for_weights = merge_backward(dh2, yraw, attn, proj, glu_a, glu_b, w_out_full, tm, "merge_bwd")
    d_wo, d_ga, d_gb = merge_backward_weights(*for_weights, tm, "merge_bwd_w")
    dproj3 = dproj.reshape(n_b, n_l, IN_WIDTH)
    dproj3, dsink_p = attention_backward(proj3, dattn.reshape(n_b, n_l, D_MODEL), dproj3, attn_sinks, seq, "attn_bwd")
    dproj3, gs3, dlr_p, dli_p = ssm_backward_scan(
        dyraw.reshape(n_b, n_l, SSM_WIDTH), xs3, dproj3, c_comb_t, tabr, b_comb_t, ssm_d, seq, "ssm_bwd")
    dproj = dproj3.reshape(t_rows, IN_WIDTH)
    d_bd, d_cd, g_d = ssm_param_grads(proj, gs3.reshape(t_rows, 2 * N_STATES), xs3.reshape(t_rows, 2 * N_STATES),
                                      dyraw, n_l, "ssm_bwd_w")
    dh1, g_mix_norm = mix_backward_act(dh2, h1, mix_norm, dproj, wing, tm, "mix_bwd_act")
    dw['w_in'] = mix_backward_weights(hnm, dproj, n_l, "mix_bwd_w")
    da1, db1, dh1_half = ffn_backward_hidden(dh1, a1, b1, ffn1_w[2], tm, "ffn1_bwd_hid")
    dh0, g_ffn1_norm = ffn_backward_input(dh1, h0, ffn1_norm, da1, db1, ffn1_w[0], ffn1_w[1], tm, "ffn1_bwd_in")
    dw['ffn1_w1'], dw['ffn1_w3'], dw['ffn1_w2'] = blocked_ffn(
        *ffn_backward_weights(hn1, dh1_half, a1, b1, da1, db1, n_l, FF_BWD_COLS, "ffn1_bwd_w"))
    dh0_3 = dh0.reshape(n_b, n_l, D_MODEL)
    grad_x = dh0_3[:, :seq]
    g_meta = sum_leading(dh0_3[:, seq:], "meta_sum")

    def blocked_cols(full):
        r = full.shape[0]
        return full.reshape(r, N_DEV, full.shape[1] // N_DEV).transpose(1, 0, 2).astype(BF16)

    dw['ssm_glu_a'] = blocked_cols(d_ga)
    dw['ssm_glu_b'] = blocked_cols(d_gb)
    dw['w_out'] = d_wo.reshape(N_DEV, D_MODEL // N_DEV, D_MODEL).astype(BF16)

    groups_per_blk = SCAN_COLS // SSM_STATE
    half = ((jnp.arange(N_SCAN_BLK) % 2)[:, None] == jnp.arange(2)[None, :]).astype(F32)
    eye = jnp.eye(groups_per_blk, dtype=F32)

    def group_blocks(part, channels_first):
        if channels_first:
            t = jnp.sum(part.reshape(N_SCAN_BLK, 2, LANES // 2, SCAN_COLS) * half[:, :, None, None], axis=1)
            t = t.reshape(N_SCAN_BLK, groups_per_blk, SSM_GROUP, groups_per_blk, SSM_STATE)
            t = jnp.sum(t * eye[None, :, None, :, None], axis=3)
            return t.reshape(SSM_GROUPS, SSM_GROUP, SSM_STATE)
        t = jnp.sum(part.reshape(N_SCAN_BLK, SCAN_COLS, 2, LANES // 2) * half[:, None, :, None], axis=2)
        t = t.reshape(N_SCAN_BLK, groups_per_blk, SSM_STATE, groups_per_blk, SSM_GROUP)
        t = jnp.sum(t * eye[None, :, None, :, None], axis=3)
        return t.reshape(SSM_GROUPS, SSM_STATE, SSM_GROUP).transpose(0, 2, 1)

    dbbr = group_blocks(d_bd[:, :, :SCAN_COLS], True).transpose(1, 0, 2).reshape(SSM_GROUP, N_STATES)
    dbbi = group_blocks(d_bd[:, :, SCAN_COLS:], True).transpose(1, 0, 2).reshape(SSM_GROUP, N_STATES)
    g_c_re = group_blocks(d_cd[:, :SCAN_COLS, :], False)[None]
    g_c_im = -group_blocks(d_cd[:, SCAN_COLS:, :], False)[None]
    group_sum = (jnp.arange(N_STATES)[:, None] // SSM_STATE == jnp.arange(LANES)[None, :]).astype(F32)
    g_ar, g_ai, g_ls, g_br, g_bi = ssm_param_backward(
        ar, ai, ls, br_t, bi_t, dlr_p.reshape(n_b * SUBLANES, N_STATES), dli_p.reshape(n_b * SUBLANES, N_STATES),
        dbbr, dbbi, group_sum, "ssm_bwd_params")
    g_sinks = sum_leading(dsink_p, "sink_sum")[0:1, :N_KV_HEADS * Q_PER_KV]

    small = {
        'ffn1_norm': g_ffn1_norm, 'mix_norm': g_mix_norm, 'ffn2_norm': g_ffn2_norm, 'final_norm': g_final.reshape(D_MODEL),
        'attn_sinks': g_sinks, 'ssm_a_re': g_ar.reshape(1, SSM_GROUPS, SSM_STATE), 'ssm_a_im': g_ai.reshape(1, SSM_GROUPS, SSM_STATE),
        'ssm_log_step': g_ls[:, :SSM_GROUPS],
        'ssm_b_re': g_br.reshape(SSM_GROUP, SSM_GROUPS, SSM_STATE).transpose(1, 2, 0)[None],
        'ssm_b_im': g_bi.reshape(SSM_GROUP, SSM_GROUPS, SSM_STATE).transpose(1, 2, 0)[None],
        'ssm_c_re': g_c_re, 'ssm_c_im': g_c_im, 'ssm_d': g_d,
    }

    zeros_meta = jnp.zeros((N_META, D_MODEL), F32)
    packed_g = _pack([small[n] for n in REPLICATED] + [g_meta])
    (parts,) = all_gather_list([packed_g], "ag_small_grads")
    packed_out = adamw_small(parts, _pack([w[n] for n in REPLICATED] + [zeros_meta]),
                             _pack([m[n] for n in REPLICATED] + [zeros_meta]),
                             _pack([v[n] for n in REPLICATED] + [zeros_meta]), "adamw_small")
    shapes = [w[n].shape for n in REPLICATED] + [(N_META, D_MODEL)]
    grads, deltas, new_m, new_v = {}, {}, {}, {}
    unpacked = [_unpack(p, shapes) for p in packed_out]
    for k, n in enumerate(REPLICATED):
        grads[n], deltas[n], new_m[n], new_v[n] = (u[k] for u in unpacked)
    g_meta_full = unpacked[0][-1]
    grads['meta_tokens'] = lax.dynamic_index_in_dim(
        g_meta_full.reshape(N_META, N_DEV, D_MODEL // N_DEV), me, axis=1, keepdims=False)
    deltas['meta_tokens'], new_m['meta_tokens'], new_v['meta_tokens'] = adamw_plain(
        grads['meta_tokens'], w['meta_tokens'], m['meta_tokens'], v['meta_tokens'], "adamw_meta")

    g_list = [dw[n] for n in SHARDED]
    r1 = rs_sibling_swap(g_list, "rs_sibling")
    pairs = [pair_sums(dest, g, r, "rs_pair_" + n) for n, g, r in zip(SHARDED, g_list, r1)]
    r2 = rs_chip_exchange(pairs, "rs_chips")
    for n, g, ra, rb in zip(SHARDED, g_list, r1, r2):
        if n in ffn_names:
            two_d, back = functools.partial(hidden_on_rows, n), functools.partial(hidden_on_rows_back, n)
        else:
            two_d, back = (lambda t: t[0]), (lambda t: t[None])
        out = adamw_sharded(dest, g, ra, rb, two_d(w[n]), two_d(m[n]), two_d(v[n]), "adamw_" + n)
        grads[n], deltas[n], new_m[n], new_v[n] = (back(o) for o in out)

    return (loss, grad_x, *[grads[n] for n in WEIGHTS], *[deltas[n] for n in WEIGHTS],
            *[new_m[n] for n in WEIGHTS], *[new_v[n] for n in WEIGHTS])
```

```python
import functools

import jax
import jax.numpy as jnp
from jax import lax
from jax.experimental import pallas as pl
from jax.experimental.pallas import tpu as pltpu

F32 = jnp.float32
BF16 = jnp.bfloat16
MESH = pl.DeviceIdType.MESH

N_DEV = 8
D_MODEL = 1024
N_META = 16
HEAD_DIM = 64
N_KV_HEADS = 4
Q_PER_KV = 4
BLOCK = 128
KV_WIDTH = N_KV_HEADS * HEAD_DIM
SSM_GROUP = 16
SSM_WIDTH = 512
SSM_GROUPS = 32
SSM_STATE = 64
N_STATES = SSM_GROUPS * SSM_STATE
D_FF = 2816
FF_BLK = D_FF // N_DEV
IN_WIDTH = 4096
IN_BLK = IN_WIDTH // N_DEV
NORM_EPS = 1e-6
NEG_INF = -1e30
SCAN_COLS = 256
N_SCAN_BLK = N_STATES // SCAN_COLS
SUBLANES = 8
LANES = 128
MXU_WIDTH = 256
FF_FWD_COLS = D_FF // 2
FF_BWD_COLS = MXU_WIDTH

ADAM_LR = 0.001
ADAM_B1 = 0.9
ADAM_B2 = 0.999
ADAM_EPS = 1e-08
ADAM_WD = 0.01
ADAM_STEP = 10

VMEM_BIG = 56 * 1024 * 1024


def _cp(sem=None, vmem=None):
    kw = {}
    if sem is not None:
        kw["dimension_semantics"] = sem
    if vmem is not None:
        kw["vmem_limit_bytes"] = vmem
    return pltpu.CompilerParams(**kw)


def _pcall(body, **kw):
    return pl.pallas_call(body, **kw)


def _dot(a, b):
    return jnp.dot(a, b, preferred_element_type=F32)


def _dot_nt(a, b):
    return lax.dot_general(a, b, (((1,), (1,)), ((), ())), preferred_element_type=F32)


def _dot_tn(a, b):
    return lax.dot_general(a, b, (((0,), (0,)), ((), ())), preferred_element_type=F32)


def _sigmoid(x):
    return 1.0 / (1.0 + jnp.exp(-x))


def _row_tile(rows, cap):
    best = None
    for t in range(16, min(rows, cap) + 1, 16):
        if rows % t == 0:
            best = t
    assert best is not None, rows
    return best


def _my_place():
    return lax.axis_index("x"), lax.axis_index("y"), lax.axis_index("c")


def all_gather_list(shards, name):
    n = len(shards)

    def body(*refs):
        ins, outs = refs[:n], refs[n:2 * n]
        send_sems, recv_sems, local_sems = refs[2 * n:]
        x, y, c = _my_place()
        me, sibling = (x, y, c), (x, y, 1 - c)
        chips = [(1 - x, y), (x, 1 - y), (1 - x, 1 - y)]

        def blk(a, px, py, pc):
            return outs[a].at[4 * px + 2 * py + pc]

        def copy(a, k, block, to, src=None):
            return pltpu.make_async_remote_copy(
                src_ref=blk(a, *block) if src is None else src, dst_ref=blk(a, *block),
                send_sem=send_sems.at[a * 7 + k], recv_sem=recv_sems.at[a * 7 + k],
                device_id=to, device_id_type=MESH)

        mine = [pltpu.make_async_copy(ins[a], blk(a, *me), local_sems.at[a]) for a in range(n)]
        for cp in mine:
            cp.start()
        first = []
        for a in range(n):
            first.append(copy(a, 0, me, sibling, src=ins[a]))
            first += [copy(a, 1 + j, me, (*chip, c), src=ins[a]) for j, chip in enumerate(chips)]
        for cp in first:
            cp.start()
        passed = []
        for j, chip in enumerate(chips):
            for a in range(n):
                copy(a, 1 + j, (*chip, c), me).wait_recv()
                cp = copy(a, 4 + j, (*chip, c), sibling)
                cp.start()
                passed.append(cp)
        for a in range(n):
            copy(a, 0, sibling, me).wait_recv()
            for j, chip in enumerate(chips):
                copy(a, 4 + j, (*chip, 1 - c), me).wait_recv()
        for cp in first + passed:
            cp.wait_send()
        for cp in mine:
            cp.wait()

    any_spec = pl.BlockSpec(memory_space=pl.ANY)
    return _pcall(
        body, name=name,
        out_shape=[jax.ShapeDtypeStruct((N_DEV,) + s.shape, s.dtype) for s in shards],
        in_specs=[any_spec] * n, out_specs=[any_spec] * n,
        scratch_shapes=[pltpu.SemaphoreType.DMA((7 * n,)), pltpu.SemaphoreType.DMA((7 * n,)),
                        pltpu.SemaphoreType.DMA((n,))],
    )(*shards)


def rs_sibling_swap(grads, name):
    n = len(grads)

    def body(*refs):
        ins, outs = refs[:n], refs[n:2 * n]
        send_sems, recv_sems = refs[2 * n:]
        x, y, c = _my_place()
        chips = [(x, y), (1 - x, y), (x, 1 - y), (1 - x, 1 - y)]
        copies = []
        for a in range(n):
            for k, (px, py) in enumerate(chips):
                copies.append(pltpu.make_async_remote_copy(
                    src_ref=ins[a].at[4 * px + 2 * py + (1 - c)], dst_ref=outs[a].at[k],
                    send_sem=send_sems.at[4 * a + k], recv_sem=recv_sems.at[4 * a + k],
                    device_id=(x, y, 1 - c), device_id_type=MESH))
        for cp in copies:
            cp.start()
        for cp in copies:
            cp.wait()

    any_spec = pl.BlockSpec(memory_space=pl.ANY)
    return _pcall(
        body, name=name,
        out_shape=[jax.ShapeDtypeStruct((4,) + g.shape[1:], g.dtype) for g in grads],
        in_specs=[any_spec] * n, out_specs=[any_spec] * n,
        scratch_shapes=[pltpu.SemaphoreType.DMA((4 * n,)), pltpu.SemaphoreType.DMA((4 * n,))],
    )(*grads)


def rs_chip_exchange(parts, name):
    n = len(parts)

    def body(*refs):
        ins, outs = refs[:n], refs[n:2 * n]
        send_sems, recv_sems = refs[2 * n:]
        x, y, c = _my_place()
        chips = [(1 - x, y), (x, 1 - y), (1 - x, 1 - y)]
        copies = []
        for a in range(n):
            for k, (px, py) in enumerate(chips):
                copies.append(pltpu.make_async_remote_copy(
                    src_ref=ins[a].at[k], dst_ref=outs[a].at[k],
                    send_sem=send_sems.at[3 * a + k], recv_sem=recv_sems.at[3 * a + k],
                    device_id=(px, py, c), device_id_type=MESH))
        for cp in copies:
            cp.start()
        for cp in copies:
            cp.wait()

    any_spec = pl.BlockSpec(memory_space=pl.ANY)
    return _pcall(
        body, name=name,
        out_shape=[jax.ShapeDtypeStruct(p.shape, p.dtype) for p in parts],
        in_specs=[any_spec] * n, out_specs=[any_spec] * n,
        scratch_shapes=[pltpu.SemaphoreType.DMA((3 * n,)), pltpu.SemaphoreType.DMA((3 * n,))],
    )(*parts)


HBM_SPEC = pl.BlockSpec(memory_space=pltpu.HBM)
SEM_SPEC = pl.BlockSpec(memory_space=pltpu.SEMAPHORE)
N_PEERS = N_DEV - 1


def _related(k):
    x, y, c = _my_place()
    px = 1 - x if k & 4 else x
    py = 1 - y if k & 2 else y
    pc = 1 - c if k & 1 else c
    return (px, py, pc), 4 * px + 2 * py + pc


def _exchange_copies(srcs, lands, send_sems, recv_sems, gather):
    x, y, c = _my_place()
    me = 4 * x + 2 * y + c
    copies = []
    for a, (src, land) in enumerate(zip(srcs, lands)):
        for k in range(1, N_DEV):
            peer, d = _related(k)
            copies.append(pltpu.make_async_remote_copy(
                src_ref=src if gather else src.at[d], dst_ref=land.at[me] if gather else land.at[k],
                send_sem=send_sems.at[a * N_PEERS + k - 1], recv_sem=recv_sems.at[a * N_PEERS + k - 1],
                device_id=peer, device_id_type=MESH))
    return copies


def exchange_start(srcs, gather, name):
    n = len(srcs)
    land_shapes = [((N_DEV,) + s.shape) if gather else s.shape for s in srcs]

    def body(*refs):
        send_sems, recv_sems = refs[2 * n], refs[2 * n + 1]
        for cp in _exchange_copies(refs[:n], refs[n:2 * n], send_sems, recv_sems, gather):
            cp.start()
        token = refs[-1]
        token[...] = jnp.zeros_like(token)

    sems = pltpu.SemaphoreType.DMA((n * N_PEERS,))
    lands = [pltpu.with_memory_space_constraint(lax.empty(shape, s.dtype), pltpu.HBM) for shape, s in zip(land_shapes, srcs)]
    out = _pcall(
        body, name=name,
        out_shape=(sems, sems, *[pltpu.HBM(s.shape, s.dtype) for s in srcs],
                   *[pltpu.HBM(shape, s.dtype) for shape, s in zip(land_shapes, srcs)],
                   jax.ShapeDtypeStruct((SUBLANES, LANES), F32)),
        in_specs=[HBM_SPEC] * (2 * n),
        out_specs=(SEM_SPEC, SEM_SPEC, *[HBM_SPEC] * (2 * n), pl.BlockSpec(memory_space=pltpu.VMEM)),
        input_output_aliases={i: 2 + i for i in range(2 * n)},
        compiler_params=pltpu.CompilerParams(has_side_effects=pltpu.SideEffectType.DATAFLOW_SIDE_EFFECTING),
    )(*[pltpu.with_memory_space_constraint(s, pltpu.HBM) for s in srcs], *lands)
    return out[0], out[1], list(out[2:2 + n]), list(out[2 + n:2 + 2 * n]), out[-1]


def exchange_wait(send_sems, recv_sems, srcs, lands, after, gather, name):
    n = len(srcs)

    def body(*refs):
        for cp in _exchange_copies(refs[:n], refs[n:2 * n], refs[2 * n], refs[2 * n + 1], gather):
            cp.wait_send()
            cp.wait_recv()

    out = _pcall(
        body, name=name,
        out_shape=(*[pltpu.HBM(s.shape, s.dtype) for s in srcs], *[pltpu.HBM(z.shape, z.dtype) for z in lands]),
        in_specs=[HBM_SPEC] * (2 * n) + [SEM_SPEC, SEM_SPEC, pl.BlockSpec(memory_space=pl.ANY)],
        out_specs=tuple([HBM_SPEC] * (2 * n)),
        input_output_aliases={i: i for i in range(2 * n)},
        compiler_params=pltpu.CompilerParams(has_side_effects=pltpu.SideEffectType.DATAFLOW_SIDE_EFFECTING),
    )(*srcs, *lands, send_sems, recv_sems, after)
    return list(out[n:])


def adamw_exchanged(me, g, land, w, m, v, name):
    rows, cols = w.shape
    tr = _row_tile(rows, 256)

    def body(me_ref, g_ref, land_ref, w_ref, m_ref, v_ref, go_ref, d_ref, mo_ref, vo_ref):
        grad = g_ref[...].astype(F32)
        for k in range(1, N_DEV):
            grad = grad + land_ref[k].astype(F32)
        delta, m_new, v_new = _adam_math(w_ref[...], grad, m_ref[...], v_ref[...])
        go_ref[...] = grad
        d_ref[...] = delta
        mo_ref[...] = m_new
        vo_ref[...] = v_new

    tile = pl.BlockSpec((tr, cols), lambda r, ix: (r, 0))
    out = jax.ShapeDtypeStruct((rows, cols), F32)
    return _pcall(
        body, name=name, out_shape=[out] * 4,
        grid_spec=pltpu.PrefetchScalarGridSpec(
            num_scalar_prefetch=1, grid=(rows // tr,),
            in_specs=[pl.BlockSpec((None, tr, cols), lambda r, ix: (ix[0], r, 0)),
                      pl.BlockSpec((N_DEV, tr, cols), lambda r, ix: (0, r, 0)), tile, tile, tile],
            out_specs=[tile] * 4),
        compiler_params=_cp(("arbitrary",)),
    )(me, g, land, w, m, v)


def pair_sums(idx, g, r1, name):
    _, rows, cols = g.shape
    tr = _row_tile(rows, 256)

    def body(idx_ref, g_ref, r_ref, o_ref):
        o_ref[...] = (g_ref[...].astype(F32) + r_ref[...].astype(F32)).astype(BF16)

    return _pcall(
        body, name=name,
        out_shape=jax.ShapeDtypeStruct((3, rows, cols), BF16),
        grid_spec=pltpu.PrefetchScalarGridSpec(
            num_scalar_prefetch=1, grid=(3, rows // tr),
            in_specs=[pl.BlockSpec((None, tr, cols), lambda k, r, ix: (ix[k + 1], r, 0)),
                      pl.BlockSpec((None, tr, cols), lambda k, r, ix: (k + 1, r, 0))],
            out_specs=pl.BlockSpec((None, tr, cols), lambda k, r, ix: (k, r, 0))),
        compiler_params=_cp(("arbitrary", "arbitrary")),
    )(idx, g, r1)


def _adam_math(w, g, m, v):
    m = ADAM_B1 * m + (1.0 - ADAM_B1) * g
    v = ADAM_B2 * v + (1.0 - ADAM_B2) * (g * g)
    m_hat = m / (1.0 - ADAM_B1 ** ADAM_STEP)
    v_hat = v / (1.0 - ADAM_B2 ** ADAM_STEP)
    delta = -ADAM_LR * (m_hat / (jnp.sqrt(v_hat) + ADAM_EPS) + ADAM_WD * w)
    return delta, m, v


def adamw_sharded(idx, g, r1, r2, w, m, v, name):
    rows, cols = w.shape
    tr = _row_tile(rows, 256)

    def body(idx_ref, g_ref, r1_ref, r2_ref, w_ref, m_ref, v_ref, go_ref, d_ref, mo_ref, vo_ref):
        grad = g_ref[...].astype(F32) + r1_ref[...].astype(F32)
        for k in range(3):
            grad = grad + r2_ref[k].astype(F32)
        delta, m_new, v_new = _adam_math(w_ref[...], grad, m_ref[...], v_ref[...])
        go_ref[...] = grad
        d_ref[...] = delta
        mo_ref[...] = m_new
        vo_ref[...] = v_new

    tile = pl.BlockSpec((tr, cols), lambda r, ix: (r, 0))
    out = jax.ShapeDtypeStruct((rows, cols), F32)
    return _pcall(
        body, name=name, out_shape=[out] * 4,
        grid_spec=pltpu.PrefetchScalarGridSpec(
            num_scalar_prefetch=1, grid=(rows // tr,),
            in_specs=[pl.BlockSpec((None, tr, cols), lambda r, ix: (ix[0], r, 0)),
                      pl.BlockSpec((None, tr, cols), lambda r, ix: (0, r, 0)),
                      pl.BlockSpec((3, tr, cols), lambda r, ix: (0, r, 0)),
                      tile, tile, tile],
            out_specs=[tile] * 4),
        compiler_params=_cp(("arbitrary",)),
    )(idx, g, r1, r2, w, m, v)


def adamw_small(parts, w, m, v, name):
    _, rows, cols = parts.shape

    def body(p_ref, w_ref, m_ref, v_ref, go_ref, d_ref, mo_ref, vo_ref):
        grad = p_ref[0]
        for k in range(1, N_DEV):
            grad = grad + p_ref[k]
        delta, m_new, v_new = _adam_math(w_ref[...], grad, m_ref[...], v_ref[...])
        go_ref[...] = grad
        d_ref[...] = delta
        mo_ref[...] = m_new
        vo_ref[...] = v_new

    out = jax.ShapeDtypeStruct((rows, cols), F32)
    return _pcall(body, name=name, out_shape=[out] * 4, compiler_params=_cp(vmem=VMEM_BIG))(parts, w, m, v)


def adamw_plain(g, w, m, v, name):
    def body(g_ref, w_ref, m_ref, v_ref, d_ref, mo_ref, vo_ref):
        delta, m_new, v_new = _adam_math(w_ref[...], g_ref[...], m_ref[...], v_ref[...])
        d_ref[...] = delta
        mo_ref[...] = m_new
        vo_ref[...] = v_new

    out = jax.ShapeDtypeStruct(w.shape, F32)
    return _pcall(body, name=name, out_shape=[out] * 3)(g, w, m, v)


def _rms_fwd(x, g):
    r = lax.rsqrt(jnp.mean(x * x, axis=-1, keepdims=True) + NORM_EPS)
    return x * r * g


def _rms_bwd(x, g, dy):
    r = lax.rsqrt(jnp.mean(x * x, axis=-1, keepdims=True) + NORM_EPS)
    xh = x * r
    t = dy * g
    dx = r * (t - xh * jnp.mean(t * xh, axis=-1, keepdims=True))
    return dx, jnp.sum(dy * xh, axis=0, keepdims=True)


def _accumulate(ref, val, first):
    @pl.when(first)
    def _():
        ref[...] = val

    @pl.when(jnp.logical_not(first))
    def _():
        ref[...] += val


def _col_chunks(width):
    return [(c0, min(MXU_WIDTH, width - c0)) for c0 in range(0, width, MXU_WIDTH)]


def ffn_forward(h, norm, w1, w3, w2, tm, tn, name):
    t_rows = h.shape[0]
    nj = D_FF // tn

    def body(h_ref, g_ref, w1_ref, w3_ref, w2_ref, out_ref, hn_ref, a_ref, b_ref, acc_ref):
        j = pl.program_id(1)

        @pl.when(j == 0)
        def _():
            hn_ref[...] = _rms_fwd(h_ref[...], g_ref[...]).astype(BF16)
            acc_ref[...] = jnp.zeros_like(acc_ref)

        hn = hn_ref[...]
        for c0, cw in _col_chunks(tn):
            a = _dot_nt(hn, w1_ref[c0:c0 + cw, :])
            b = _dot_nt(hn, w3_ref[c0:c0 + cw, :])
            a_ref[:, c0:c0 + cw] = a.astype(BF16)
            b_ref[:, c0:c0 + cw] = b.astype(BF16)
            hid = (a * _sigmoid(a) * b).astype(BF16)
            acc_ref[...] += _dot(hid, w2_ref[c0:c0 + cw, :])

        @pl.when(j == nj - 1)
        def _():
            out_ref[...] = h_ref[...] + 0.5 * acc_ref[...]

    row = pl.BlockSpec((tm, D_MODEL), lambda i, j: (i, 0))
    hid_blk = pl.BlockSpec((tm, tn), lambda i, j: (i, j))
    w_row = pl.BlockSpec((tn, D_MODEL), lambda i, j: (j, 0))
    return _pcall(
        body, name=name, grid=(t_rows // tm, nj),
        in_specs=[row, pl.BlockSpec((1, D_MODEL), lambda i, j: (0, 0)), w_row, w_row, w_row],
        out_specs=[row, row, hid_blk, hid_blk],
        out_shape=[jax.ShapeDtypeStruct((t_rows, D_MODEL), F32), jax.ShapeDtypeStruct((t_rows, D_MODEL), BF16),
                   jax.ShapeDtypeStruct((t_rows, D_FF), BF16), jax.ShapeDtypeStruct((t_rows, D_FF), BF16)],
        scratch_shapes=[pltpu.VMEM((tm, D_MODEL), F32)],
        compiler_params=_cp(("arbitrary", "arbitrary"), VMEM_BIG),
    )(h, norm, w1, w3, w2)


def _resident(shape):
    return pl.BlockSpec(shape, lambda *_: (0,) * len(shape), pipeline_mode=pl.Buffered(1))


def ffn_backward_hidden(dh, a, b, w2, tm, name):
    t_rows = dh.shape[0]

    def body(dh_ref, a_ref, b_ref, w2_ref, da_ref, db_ref, dhb_ref):
        dhb = (0.5 * dh_ref[...]).astype(BF16)
        dhb_ref[...] = dhb
        for c0, cw in _col_chunks(D_FF):
            dhid = _dot_nt(dhb, w2_ref[c0:c0 + cw, :])
            av = a_ref[:, c0:c0 + cw].astype(F32)
            bv = b_ref[:, c0:c0 + cw].astype(F32)
            s = _sigmoid(av)
            da_ref[:, c0:c0 + cw] = (dhid * bv * (s * (1.0 + av * (1.0 - s)))).astype(BF16)
            db_ref[:, c0:c0 + cw] = (dhid * (av * s)).astype(BF16)

    hid = pl.BlockSpec((tm, D_FF), lambda i: (i, 0))
    row = pl.BlockSpec((tm, D_MODEL), lambda i: (i, 0))
    return _pcall(
        body, name=name, grid=(t_rows // tm,),
        in_specs=[row, hid, hid, _resident((D_FF, D_MODEL))],
        out_specs=[hid, hid, row],
        out_shape=[jax.ShapeDtypeStruct((t_rows, D_FF), BF16), jax.ShapeDtypeStruct((t_rows, D_FF), BF16),
                   jax.ShapeDtypeStruct((t_rows, D_MODEL), BF16)],
        compiler_params=_cp(("arbitrary",), VMEM_BIG),
    )(dh, a, b, w2)


def ffn_backward_input(dh, h, norm, da, db, w1, w3, tm, name):
    t_rows = h.shape[0]

    def body(dh_ref, h_ref, g_ref, da_ref, db_ref, w1_ref, w3_ref, dhin_ref, dg_ref):
        dhn = _dot(da_ref[...], w1_ref[...]) + _dot(db_ref[...], w3_ref[...])
        dx, dg = _rms_bwd(h_ref[...], g_ref[...], dhn)
        dhin_ref[...] = dh_ref[...] + dx
        _accumulate(dg_ref, dg, pl.program_id(0) == 0)

    row = pl.BlockSpec((tm, D_MODEL), lambda i: (i, 0))
    vec = pl.BlockSpec((1, D_MODEL), lambda i: (0, 0))
    hid = pl.BlockSpec((tm, D_FF), lambda i: (i, 0))
    return _pcall(
        body, name=name, grid=(t_rows // tm,),
        in_specs=[row, row, vec, hid, hid, _resident((D_FF, D_MODEL)), _resident((D_FF, D_MODEL))],
        out_specs=[row, vec],
        out_shape=[jax.ShapeDtypeStruct((t_rows, D_MODEL), F32), jax.ShapeDtypeStruct((1, D_MODEL), F32)],
        compiler_params=_cp(("arbitrary",), VMEM_BIG),
    )(dh, h, norm, da, db, w1, w3)


def ffn_backward_weights(hn, dh, a, b, da, db, tm, tn, name):
    t_rows = hn.shape[0]
    ni = t_rows // tm
    kc = _row_tile(tm, 688)

    def body(hn_ref, dh_ref, a_ref, b_ref, da_ref, db_ref, dw1_ref, dw3_ref, dw2_ref, acc1, acc3, acc2):
        i = pl.program_id(1)
        parts = None
        for r0 in range(0, tm, kc):
            rows = slice(r0, r0 + kc)
            hn_v = hn_ref[rows, :]
            av = a_ref[rows, :].astype(F32)
            hid = (av * _sigmoid(av) * b_ref[rows, :].astype(F32)).astype(BF16)
            new = (_dot_tn(hn_v, da_ref[rows, :]), _dot_tn(hn_v, db_ref[rows, :]), _dot_tn(hid, dh_ref[rows, :]))
            parts = new if parts is None else tuple(p + q for p, q in zip(parts, new))
        _accumulate(acc1, parts[0], i == 0)
        _accumulate(acc3, parts[1], i == 0)
        _accumulate(acc2, parts[2], i == 0)

        @pl.when(i == ni - 1)
        def _():
            dw1_ref[...] = acc1[...].T.astype(BF16)
            dw3_ref[...] = acc3[...].T.astype(BF16)
            dw2_ref[...] = acc2[...].astype(BF16)

    row = pl.BlockSpec((tm, D_MODEL), lambda j, i: (i, 0))
    hid_blk = pl.BlockSpec((tm, tn), lambda j, i: (i, j))
    w_row = pl.BlockSpec((tn, D_MODEL), lambda j, i: (j, 0))
    out = jax.ShapeDtypeStruct((D_FF, D_MODEL), BF16)
    return _pcall(
        body, name=name, grid=(D_FF // tn, ni),
        in_specs=[row, row, hid_blk, hid_blk, hid_blk, hid_blk],
        out_specs=[w_row, w_row, w_row], out_shape=[out, out, out],
        scratch_shapes=[pltpu.VMEM((D_MODEL, tn), F32), pltpu.VMEM((D_MODEL, tn), F32), pltpu.VMEM((tn, D_MODEL), F32)],
        compiler_params=_cp(("arbitrary", "arbitrary"), VMEM_BIG),
    )(hn, dh, a, b, da, db)


def mix_forward(h, norm, wing, tm, name):
    t_rows = h.shape[0]

    def body(h_ref, g_ref, w_ref, hn_ref, p_ref):
        hn = _rms_fwd(h_ref[...], g_ref[...]).astype(BF16)
        hn_ref[...] = hn
        for j in range(N_DEV):
            p_ref[:, j * IN_BLK:(j + 1) * IN_BLK] = _dot(hn, w_ref[j]).astype(BF16)

    row = pl.BlockSpec((tm, D_MODEL), lambda i: (i, 0))
    return _pcall(
        body, name=name, grid=(t_rows // tm,),
        in_specs=[row, pl.BlockSpec((1, D_MODEL), lambda i: (0, 0)),
                  pl.BlockSpec((N_DEV, D_MODEL, IN_BLK), lambda i: (0, 0, 0))],
        out_specs=[row, pl.BlockSpec((tm, IN_WIDTH), lambda i: (i, 0))],
        out_shape=[jax.ShapeDtypeStruct((t_rows, D_MODEL), BF16), jax.ShapeDtypeStruct((t_rows, IN_WIDTH), BF16)],
        compiler_params=_cp(("arbitrary",), VMEM_BIG),
    )(h, norm, wing)


def mix_backward_act(dh, h, norm, dproj, wing, tm, name):
    t_rows = h.shape[0]
    per_step = 4
    nj = N_DEV // per_step

    def body(dh_ref, h_ref, g_ref, dp_ref, w_ref, dhin_ref, dg_ref, acc_ref):
        i, j = pl.program_id(0), pl.program_id(1)
        part = functools.reduce(
            lambda u, w: u + w, [_dot_nt(dp_ref[:, k * IN_BLK:(k + 1) * IN_BLK], w_ref[k]) for k in range(per_step)])
        _accumulate(acc_ref, part, j == 0)

        @pl.when(j == nj - 1)
        def _():
            dx, dg = _rms_bwd(h_ref[...], g_ref[...], acc_ref[...])
            dhin_ref[...] = dh_ref[...] + dx
            _accumulate(dg_ref, dg, i == 0)

    row = pl.BlockSpec((tm, D_MODEL), lambda i, j: (i, 0))
    vec = pl.BlockSpec((1, D_MODEL), lambda i, j: (0, 0))
    return _pcall(
        body, name=name, grid=(t_rows // tm, nj),
        in_specs=[row, row, vec, pl.BlockSpec((tm, per_step * IN_BLK), lambda i, j: (i, j)),
                  pl.BlockSpec((per_step, D_MODEL, IN_BLK), lambda i, j: (j, 0, 0))],
        out_specs=[row, vec],
        out_shape=[jax.ShapeDtypeStruct((t_rows, D_MODEL), F32), jax.ShapeDtypeStruct((1, D_MODEL), F32)],
        scratch_shapes=[pltpu.VMEM((tm, D_MODEL), F32)],
        compiler_params=_cp(("arbitrary", "arbitrary"), VMEM_BIG),
    )(dh, h, norm, dproj, wing)


def mix_backward_weights(hn, dproj, tm, name):
    t_rows = hn.shape[0]
    ni = t_rows // tm
    per_step = 2

    def body(hn_ref, dp_ref, dw_ref, acc):
        i = pl.program_id(1)
        _accumulate(acc, _dot_tn(hn_ref[...], dp_ref[...]), i == 0)

        @pl.when(i == ni - 1)
        def _():
            for k in range(per_step):
                dw_ref[k] = acc[:, k * IN_BLK:(k + 1) * IN_BLK].astype(BF16)

    return _pcall(
        body, name=name, grid=(N_DEV // per_step, ni),
        in_specs=[pl.BlockSpec((tm, D_MODEL), lambda j, i: (i, 0)),
                  pl.BlockSpec((tm, per_step * IN_BLK), lambda j, i: (i, j))],
        out_specs=pl.BlockSpec((per_step, D_MODEL, IN_BLK), lambda j, i: (j, 0, 0)),
        out_shape=jax.ShapeDtypeStruct((N_DEV, D_MODEL, IN_BLK), BF16),
        scratch_shapes=[pltpu.VMEM((D_MODEL, per_step * IN_BLK), F32)],
        compiler_params=_cp(("arbitrary", "arbitrary"), VMEM_BIG),
    )(hn, dproj)


GELU_C = 0.7978845608028654
GELU_K = 0.044715


def _gelu(x):
    return 0.5 * x * (1.0 + jnp.tanh(GELU_C * (x + GELU_K * (x * x * x))))


def _gelu_and_grad(x):
    th = jnp.tanh(GELU_C * (x + GELU_K * (x * x * x)))
    val = 0.5 * x * (1.0 + th)
    grad = 0.5 * (1.0 + th) + 0.5 * x * (1.0 - th * th) * (GELU_C * (1.0 + 3.0 * GELU_K * (x * x)))
    return val, grad


def merge_forward(h, yraw, attn, proj, glu_a, glu_b, w_out, tm, name):
    t_rows = h.shape[0]

    def body(h_ref, y_ref, at_ref, gate_ref, a_ref, b_ref, wo_ref, out_ref):
        y = _gelu(y_ref[...]).astype(BF16)
        ssm = _dot(y, a_ref[...]) * _sigmoid(_dot(y, b_ref[...]))
        ga = gate_ref[:, :D_MODEL].astype(F32)
        gs = gate_ref[:, D_MODEL:].astype(F32)
        merged = _sigmoid(ga) * at_ref[...].astype(F32) + _sigmoid(gs) * ssm
        out_ref[...] = h_ref[...] + _dot(merged.astype(BF16), wo_ref[...])

    row = pl.BlockSpec((tm, D_MODEL), lambda i: (i, 0))
    glu = pl.BlockSpec((SSM_WIDTH, D_MODEL), lambda i: (0, 0))
    return _pcall(
        body, name=name, grid=(t_rows // tm,),
        in_specs=[row, pl.BlockSpec((tm, SSM_WIDTH), lambda i: (i, 0)), row,
                  pl.BlockSpec((tm, 2 * D_MODEL), lambda i: (i, 1)), glu, glu,
                  pl.BlockSpec((D_MODEL, D_MODEL), lambda i: (0, 0))],
        out_specs=row, out_shape=jax.ShapeDtypeStruct((t_rows, D_MODEL), F32),
        compiler_params=_cp(("arbitrary",), VMEM_BIG),
    )(h, yraw, attn, proj, glu_a, glu_b, w_out)


def merge_backward(dh, yraw, attn, proj, glu_a, glu_b, w_out, tm, name):
    t_rows = dh.shape[0]

    def body(dh_ref, y_ref, at_ref, gate_ref, a_ref, b_ref, wo_ref,
             dat_ref, dy_ref, dgate_ref, d16_ref, mg_ref, y16_ref, dya_ref, dyb_ref):
        d16 = dh_ref[...].astype(BF16)
        d16_ref[...] = d16
        gel, dgel = _gelu_and_grad(y_ref[...].astype(F32))
        y16 = gel.astype(BF16)
        y16_ref[...] = y16
        dy = None
        for c0, cw in _col_chunks(D_MODEL):
            cols = slice(c0, c0 + cw)
            gcols = slice(D_MODEL + c0, D_MODEL + c0 + cw)
            dmerged = _dot_nt(d16, wo_ref[cols, :])
            ya = _dot(y16, a_ref[:, cols])
            sb = _sigmoid(_dot(y16, b_ref[:, cols]))
            ssm = ya * sb
            sa = _sigmoid(gate_ref[:, cols].astype(F32))
            ss = _sigmoid(gate_ref[:, gcols].astype(F32))
            attn_v = at_ref[:, cols].astype(F32)
            mg_ref[:, cols] = (sa * attn_v + ss * ssm).astype(BF16)
            dat_ref[:, cols] = (dmerged * sa).astype(BF16)
            dgate_ref[:, cols] = (dmerged * attn_v * sa * (1.0 - sa)).astype(BF16)
            dgate_ref[:, gcols] = (dmerged * ssm * ss * (1.0 - ss)).astype(BF16)
            dssm = dmerged * ss
            dya = (dssm * sb).astype(BF16)
            dyb = (dssm * ya * sb * (1.0 - sb)).astype(BF16)
            dya_ref[:, cols] = dya
            dyb_ref[:, cols] = dyb
            part = _dot_nt(dya, a_ref[:, cols]) + _dot_nt(dyb, b_ref[:, cols])
            dy = part if dy is None else dy + part
        dy_ref[...] = (dy * dgel).astype(BF16)

    row = pl.BlockSpec((tm, D_MODEL), lambda i: (i, 0))
    ssm_row = pl.BlockSpec((tm, SSM_WIDTH), lambda i: (i, 0))
    gates = pl.BlockSpec((tm, 2 * D_MODEL), lambda i: (i, 1))
    wide = jax.ShapeDtypeStruct((t_rows, D_MODEL), BF16)
    narrow = jax.ShapeDtypeStruct((t_rows, SSM_WIDTH), BF16)
    return _pcall(
        body, name=name, grid=(t_rows // tm,),
        in_specs=[row, ssm_row, row, gates, _resident((SSM_WIDTH, D_MODEL)), _resident((SSM_WIDTH, D_MODEL)),
                  _resident((D_MODEL, D_MODEL))],
        out_specs=[row, ssm_row, gates, row, row, ssm_row, row, row],
        out_shape=[wide, narrow, jax.ShapeDtypeStruct((t_rows, IN_WIDTH), BF16), wide, wide, narrow, wide, wide],
        compiler_params=_cp(("arbitrary",), VMEM_BIG),
    )(dh, yraw, attn, proj, glu_a, glu_b, w_out)


def merge_backward_weights(d16, merged, y16, dya, dyb, tm, name):
    t_rows = d16.shape[0]

    def body(d_ref, mg_ref, y_ref, dya_ref, dyb_ref, dwo_ref, da_ref, db_ref):
        first = pl.program_id(0) == 0
        y16 = y_ref[...]
        _accumulate(dwo_ref, _dot_tn(mg_ref[...], d_ref[...]), first)
        _accumulate(da_ref, _dot_tn(y16, dya_ref[...]), first)
        _accumulate(db_ref, _dot_tn(y16, dyb_ref[...]), first)

    row = pl.BlockSpec((tm, D_MODEL), lambda i: (i, 0))
    ssm_row = pl.BlockSpec((tm, SSM_WIDTH), lambda i: (i, 0))
    glu = pl.BlockSpec((SSM_WIDTH, D_MODEL), lambda i: (0, 0))
    wo = pl.BlockSpec((D_MODEL, D_MODEL), lambda i: (0, 0))
    return _pcall(
        body, name=name, grid=(t_rows // tm,),
        in_specs=[row, row, ssm_row, row, row], out_specs=[wo, glu, glu],
        out_shape=[jax.ShapeDtypeStruct((D_MODEL, D_MODEL), F32), jax.ShapeDtypeStruct((SSM_WIDTH, D_MODEL), F32),
                   jax.ShapeDtypeStruct((SSM_WIDTH, D_MODEL), F32)],
        compiler_params=_cp(("arbitrary",), VMEM_BIG),
    )(d16, merged, y16, dya, dyb)


def final_loss_backward(h, target, norm, seq, tm, name):
    t_rows = h.shape[0]
    tiles_per_example = (seq + N_META) // tm

    def body(h_ref, t_ref, g_ref, dh_ref, loss_ref, dg_ref):
        i = pl.program_id(0)
        x = h_ref[...]
        g = g_ref[...]
        r = lax.rsqrt(jnp.mean(x * x, axis=-1, keepdims=True) + NORM_EPS)
        xh = x * r
        pos = lax.broadcasted_iota(jnp.int32, (tm, 1), 0) + (i % tiles_per_example) * tm
        diff = jnp.where(pos < seq, xh * g - t_ref[...], 0.0)
        part = 0.5 * jnp.sum(jnp.sum(diff * diff, axis=-1, keepdims=True), axis=0, keepdims=True) / D_MODEL
        dy = diff / D_MODEL
        t = dy * g
        dh_ref[...] = r * (t - xh * jnp.mean(t * xh, axis=-1, keepdims=True))
        _accumulate(loss_ref, jnp.broadcast_to(part, (1, LANES)), i == 0)
        _accumulate(dg_ref, jnp.sum(dy * xh, axis=0, keepdims=True), i == 0)

    row = pl.BlockSpec((tm, D_MODEL), lambda i: (i, 0))
    vec = pl.BlockSpec((1, D_MODEL), lambda i: (0, 0))
    return _pcall(
        body, name=name, grid=(t_rows // tm,),
        in_specs=[row, row, vec],
        out_specs=[row, pl.BlockSpec((1, LANES), lambda i: (0, 0)), vec],
        out_shape=[jax.ShapeDtypeStruct((t_rows, D_MODEL), F32), jax.ShapeDtypeStruct((1, LANES), F32),
                   jax.ShapeDtypeStruct((1, D_MODEL), F32)],
        compiler_params=_cp(("arbitrary",), VMEM_BIG),
    )(h, target, norm)


ATTN_SCALE = HEAD_DIM ** -0.5
STACK_HEADS = (0, 2, 1, 3)


def _lane_half(shape, hf):
    lane = lax.broadcasted_iota(jnp.int32, shape, 1)
    return (lane < HEAD_DIM) if hf == 0 else (lane >= HEAD_DIM)


def _kv_variants(ref, rows, kh):
    tile = kh // 2
    t = ref[rows, tile * LANES:(tile + 1) * LANES].astype(F32)
    swapped = pltpu.roll(t, HEAD_DIM, axis=1)
    at_low, at_high = (t, swapped) if kh % 2 == 0 else (swapped, t)
    lo = jnp.where(_lane_half(t.shape, 0), at_low, 0.0).astype(BF16)
    hi = jnp.where(_lane_half(t.shape, 1), at_high, 0.0).astype(BF16)
    return lo, hi


def _to_kv_lanes(lo, hi, kh):
    lo = jnp.where(_lane_half(lo.shape, 0), lo, 0.0)
    hi = jnp.where(_lane_half(hi.shape, 1), hi, 0.0)
    if kh % 2 == 0:
        return lo + pltpu.roll(hi, HEAD_DIM, axis=1)
    return pltpu.roll(lo, HEAD_DIM, axis=1) + hi


def _stacked(ref, rows, kh):
    col = kh * 2 * LANES
    return jnp.concatenate([ref[rows, col:col + LANES], ref[rows, col + LANES:col + 2 * LANES]], axis=0)


def _sink_column(sink_ref, kh, nq):
    row = lax.broadcasted_iota(jnp.int32, (4 * nq, 1), 0)
    col = jnp.zeros((4 * nq, 1), F32)
    for quarter, g in enumerate(STACK_HEADS):
        col = jnp.where(row // nq == quarter, sink_ref[0, kh * Q_PER_KV + g], col)
    return col


def _softmax_parts(qs, key_tiles, masks, sink):
    scores = []
    for (k_lo, k_hi), mask in zip(key_tiles, masks):
        s = jnp.concatenate([_dot_nt(qs, k_lo), _dot_nt(qs, k_hi)], axis=0) * ATTN_SCALE
        scores.append(s if mask is None else jnp.where(mask, s, NEG_INF))
    m = functools.reduce(jnp.maximum, [jnp.max(s, axis=-1, keepdims=True) for s in scores])
    m = jnp.maximum(m, sink)
    probs = [jnp.exp(s - m) for s in scores]
    e_sink = jnp.exp(sink - m)
    den = functools.reduce(lambda u, w: u + w, [jnp.sum(p, axis=-1, keepdims=True) for p in probs]) + e_sink
    return probs, 1.0 / den, e_sink


def _band_mask(nq, first):
    keys = BLOCK if first else 2 * BLOCK
    qi = lax.broadcasted_iota(jnp.int32, (4 * nq, keys), 0) % nq
    kj = lax.broadcasted_iota(jnp.int32, (4 * nq, keys), 1)
    if first:
        return kj <= qi
    return jnp.logical_and(kj > qi, kj <= qi + BLOCK)


def _meta_mask():
    qi = lax.broadcasted_iota(jnp.int32, (4 * N_META, N_META), 0) % N_META
    kj = lax.broadcasted_iota(jnp.int32, (4 * N_META, N_META), 1)
    return kj <= qi


def _attention_schedule(seq, queries, carry):
    meta_rows = pl.ds(seq, N_META)
    carry = queries(pl.ds(0, BLOCK), BLOCK, [pl.ds(0, BLOCK), meta_rows], [_band_mask(BLOCK, True), None], carry)

    def block(n, c):
        r0 = pl.multiple_of(n * BLOCK, BLOCK)
        p0 = pl.multiple_of((n - 1) * BLOCK, BLOCK)
        return queries(pl.ds(r0, BLOCK), BLOCK, [pl.ds(p0, 2 * BLOCK), meta_rows], [_band_mask(BLOCK, False), None], c)

    carry = lax.fori_loop(1, seq // BLOCK, block, carry)
    return queries(meta_rows, N_META, [meta_rows], [_meta_mask()], carry)


def attention_forward(proj3, sinks, seq, name):
    n_b, n_l, _ = proj3.shape

    def body(sink_ref, q_ref, k_ref, v_ref, o_ref):
        def queries(q_rows, nq, key_rows, masks, carry):
            for kh in range(N_KV_HEADS):
                ks = [_kv_variants(k_ref, r, kh) for r in key_rows]
                vs = [_kv_variants(v_ref, r, kh) for r in key_rows]
                qs = _stacked(q_ref, q_rows, kh)
                probs, inv, _ = _softmax_parts(qs, ks, masks, _sink_column(sink_ref, kh, nq))
                probs = [p.astype(BF16) for p in probs]
                o_lo = functools.reduce(lambda u, w: u + w, [_dot(p[:2 * nq], v_lo) for p, (v_lo, _) in zip(probs, vs)])
                o_hi = functools.reduce(lambda u, w: u + w, [_dot(p[2 * nq:], v_hi) for p, (_, v_hi) in zip(probs, vs)])
                out = (o_lo * inv[:2 * nq] + o_hi * inv[2 * nq:]).astype(BF16)
                col = kh * 2 * LANES
                o_ref[q_rows, col:col + LANES] = out[:nq]
                o_ref[q_rows, col + LANES:col + 2 * LANES] = out[nq:]
            return carry

        _attention_schedule(seq, queries, 0)

    return _pcall(
        body, name=name, grid=(n_b,),
        in_specs=[pl.BlockSpec(memory_space=pltpu.SMEM),
                  pl.BlockSpec((None, n_l, D_MODEL), lambda b: (b, 0, 0)),
                  pl.BlockSpec((None, n_l, KV_WIDTH), lambda b: (b, 0, D_MODEL // KV_WIDTH)),
                  pl.BlockSpec((None, n_l, KV_WIDTH), lambda b: (b, 0, D_MODEL // KV_WIDTH + 1))],
        out_specs=pl.BlockSpec((None, n_l, D_MODEL), lambda b: (b, 0, 0)),
        out_shape=jax.ShapeDtypeStruct((n_b, n_l, D_MODEL), BF16),
        compiler_params=_cp(("arbitrary",), VMEM_BIG),
    )(sinks, proj3, proj3, proj3)


def attention_backward(proj3, dattn3, dproj3, sinks, seq, name):
    n_b, n_l, _ = proj3.shape
    qkv_width = D_MODEL + 2 * KV_WIDTH

    def body(sink_ref, q_ref, k_ref, v_ref, do_ref, _, dqkv_ref, dsink_ref, dk_ref, dv_ref):
        dk_ref[...] = jnp.zeros_like(dk_ref)
        dv_ref[...] = jnp.zeros_like(dv_ref)
        sub = lax.broadcasted_iota(jnp.int32, (SUBLANES, LANES), 0)
        lane = lax.broadcasted_iota(jnp.int32, (SUBLANES, LANES), 1)

        def queries(q_rows, nq, key_rows, masks, dsink):
            for kh in range(N_KV_HEADS):
                ks = [_kv_variants(k_ref, r, kh) for r in key_rows]
                vs = [_kv_variants(v_ref, r, kh) for r in key_rows]
                qs = _stacked(q_ref, q_rows, kh)
                dos = _stacked(do_ref, q_rows, kh)
                probs, inv, e_sink = _softmax_parts(qs, ks, masks, _sink_column(sink_ref, kh, nq))
                probs = [p * inv for p in probs]
                dps = [jnp.concatenate([_dot_nt(dos, v_lo), _dot_nt(dos, v_hi)], axis=0) for v_lo, v_hi in vs]
                delta = functools.reduce(
                    lambda u, w: u + w, [jnp.sum(p * dp, axis=-1, keepdims=True) for p, dp in zip(probs, dps)])
                d_sink = -(e_sink * inv) * delta
                for quarter, g in enumerate(STACK_HEADS):
                    d_here = jnp.sum(d_sink[quarter * nq:(quarter + 1) * nq], axis=0, keepdims=True)
                    dsink = dsink + jnp.where(jnp.logical_and(sub == 0, lane == kh * Q_PER_KV + g), d_here, 0.0)
                dq = None
                tile = slice((kh // 2) * LANES, (kh // 2 + 1) * LANES)
                for r, p, dp, (k_lo, k_hi) in zip(key_rows, probs, dps, ks):
                    ds = (p * (dp - delta)).astype(BF16)
                    p16 = p.astype(BF16)
                    dq_x = _dot(ds[:2 * nq], k_lo) + _dot(ds[2 * nq:], k_hi)
                    dq = dq_x if dq is None else dq + dq_x
                    dk_ref[r, tile] += _to_kv_lanes(_dot_tn(ds[:2 * nq], qs), _dot_tn(ds[2 * nq:], qs), kh) * ATTN_SCALE
                    dv_ref[r, tile] += _to_kv_lanes(_dot_tn(p16[:2 * nq], dos), _dot_tn(p16[2 * nq:], dos), kh)
                dq = (dq * ATTN_SCALE).astype(BF16)
                col = kh * 2 * LANES
                dqkv_ref[q_rows, col:col + LANES] = dq[:nq]
                dqkv_ref[q_rows, col + LANES:col + 2 * LANES] = dq[nq:]
            return dsink

        dsink_ref[...] = _attention_schedule(seq, queries, jnp.zeros((SUBLANES, LANES), F32))
        dqkv_ref[:, D_MODEL:D_MODEL + KV_WIDTH] = dk_ref[...].astype(BF16)
        dqkv_ref[:, D_MODEL + KV_WIDTH:] = dv_ref[...].astype(BF16)

    return _pcall(
        body, name=name, grid=(n_b,),
        in_specs=[pl.BlockSpec(memory_space=pltpu.SMEM),
                  pl.BlockSpec((None, n_l, D_MODEL), lambda b: (b, 0, 0)),
                  pl.BlockSpec((None, n_l, KV_WIDTH), lambda b: (b, 0, D_MODEL // KV_WIDTH)),
                  pl.BlockSpec((None, n_l, KV_WIDTH), lambda b: (b, 0, D_MODEL // KV_WIDTH + 1)),
                  pl.BlockSpec((None, n_l, D_MODEL), lambda b: (b, 0, 0)),
                  pl.BlockSpec(memory_space=pl.ANY)],
        out_specs=[pl.BlockSpec((None, n_l, qkv_width), lambda b: (b, 0, 0)),
                   pl.BlockSpec((None, SUBLANES, LANES), lambda b: (b, 0, 0))],
        out_shape=[jax.ShapeDtypeStruct(dproj3.shape, BF16), jax.ShapeDtypeStruct((n_b, SUBLANES, LANES), F32)],
        scratch_shapes=[pltpu.VMEM((n_l, KV_WIDTH), F32), pltpu.VMEM((n_l, KV_WIDTH), F32)],
        input_output_aliases={5: 0},
        compiler_params=_cp(("arbitrary",), VMEM_BIG),
    )(sinks, proj3, proj3, proj3, dattn3, dproj3)


TAB_ROWS = 8
SCAN_UNROLL = 4


def _cmul(ar, ai, br, bi):
    return ar * br - ai * bi, ar * bi + ai * br


def _discretise(ar, ai, ls):
    step = jnp.exp(ls)
    mag = jnp.exp(ar * step)
    ang = ai * step
    cos, sin = jnp.cos(ang), jnp.sin(ang)
    lr, li = mag * cos, mag * sin
    den = ar * ar + ai * ai
    nr, ni = lr - 1.0, li
    cr = (nr * ar + ni * ai) / den
    ci = (ni * ar - nr * ai) / den
    return step, mag, lr, li, den, nr, ni, cr, ci


def _scan_tables(lr, li, reverse):
    n = lr.shape[-1]
    pw = [(lr, li)]
    for _ in range(SUBLANES - 1):
        pw.append(_cmul(pw[-1][0], pw[-1][1], lr, li))
    row = lax.broadcasted_iota(jnp.int32, (SUBLANES, n), 0)
    out = []
    for d in (1, 2, 4):
        ok = (row + d <= SUBLANES - 1) if reverse else (row >= d)
        out += [jnp.where(ok, pw[d - 1][0], 0.0), jnp.where(ok, pw[d - 1][1], 0.0)]
    cr = jnp.zeros((SUBLANES, n), F32)
    ci = jnp.zeros((SUBLANES, n), F32)
    for r in range(SUBLANES):
        e = (SUBLANES - r) if reverse else (r + 1)
        cr = jnp.where(row == r, pw[e - 1][0], cr)
        ci = jnp.where(row == r, pw[e - 1][1], ci)
    return out + [cr, ci]


def ssm_prepare(ar, ai, ls, br_t, bi_t, name):
    def body(ar_ref, ai_ref, ls_ref, br_ref, bi_ref, bbr_ref, bbi_ref, tf_ref, tr_ref):
        _, _, lr, li, _, _, _, cr, ci = _discretise(ar_ref[...], ai_ref[...], ls_ref[...])
        br, bi = br_ref[...], bi_ref[...]
        bbr_ref[...] = cr * br - ci * bi
        bbi_ref[...] = cr * bi + ci * br
        for k, t in enumerate(_scan_tables(lr, li, False)):
            tf_ref[k] = t
        for k, t in enumerate(_scan_tables(lr, -li, True)):
            tr_ref[k] = t

    return _pcall(
        body, name=name,
        out_shape=[jax.ShapeDtypeStruct((SSM_GROUP, N_STATES), F32), jax.ShapeDtypeStruct((SSM_GROUP, N_STATES), F32),
                   jax.ShapeDtypeStruct((TAB_ROWS, SUBLANES, N_STATES), F32),
                   jax.ShapeDtypeStruct((TAB_ROWS, SUBLANES, N_STATES), F32)],
    )(ar, ai, ls, br_t, bi_t)


def ssm_param_backward(ar, ai, ls, br_t, bi_t, dlr_p, dli_p, dbbr, dbbi, group_sum, name):
    def body(ar_ref, ai_ref, ls_ref, br_ref, bi_ref, dlr_ref, dli_ref, dbbr_ref, dbbi_ref, gs_ref,
             dar_ref, dai_ref, dls_ref, dbr_ref, dbi_ref):
        ar, ai = ar_ref[...], ai_ref[...]
        step, mag, lr, li, den, nr, ni, cr, ci = _discretise(ar, ai, ls_ref[...])
        br, bi, dbbr_v, dbbi_v = br_ref[...], bi_ref[...], dbbr_ref[...], dbbi_ref[...]
        dbr_ref[...] = cr * dbbr_v + ci * dbbi_v
        dbi_ref[...] = cr * dbbi_v - ci * dbbr_v
        dcr = jnp.sum(dbbr_v * br + dbbi_v * bi, axis=0, keepdims=True)
        dci = jnp.sum(dbbi_v * br - dbbr_v * bi, axis=0, keepdims=True)
        dnr = (dcr * ar - dci * ai) / den
        dni = (dcr * ai + dci * ar) / den
        dden = -(cr * dcr + ci * dci) / den
        dar = (dcr * nr + dci * ni) / den + dden * 2.0 * ar
        dai = (dcr * ni - dci * nr) / den + dden * 2.0 * ai
        dlr = jnp.sum(dlr_ref[...], axis=0, keepdims=True) + dnr
        dli = jnp.sum(dli_ref[...], axis=0, keepdims=True) + dni
        dmag = (dlr * lr + dli * li) / mag
        dang = dli * lr - dlr * li
        dar_ref[...] = dar + dmag * mag * step
        dai_ref[...] = dai + dang * step
        dstep = dmag * mag * ar + dang * ai
        dls_ref[...] = jnp.dot(dstep * step, gs_ref[...], preferred_element_type=F32, precision=lax.Precision.HIGHEST)

    vec = jax.ShapeDtypeStruct((1, N_STATES), F32)
    mat = jax.ShapeDtypeStruct((SSM_GROUP, N_STATES), F32)
    return _pcall(body, name=name, out_shape=[vec, vec, jax.ShapeDtypeStruct((1, LANES), F32), mat, mat])(
        ar, ai, ls, br_t, bi_t, dlr_p, dli_p, dbbr, dbbi, group_sum)


def _scan_rows(a, b, tabs, carry, reverse):
    for k, d in enumerate((1, 2, 4)):
        shift = SUBLANES - d if reverse else d
        sr, si = pltpu.roll(a, shift, axis=0), pltpu.roll(b, shift, axis=0)
        pr, pi = _cmul(tabs[2 * k], tabs[2 * k + 1], sr, si)
        a, b = a + pr, b + pi
    pr, pi = _cmul(tabs[6], tabs[7], carry[0], carry[1])
    return a + pr, b + pi


def _time_groups(seq, reverse):
    meta = [seq + SUBLANES * g for g in range(N_META // SUBLANES)]
    return meta[::-1] if reverse else meta


def ssm_forward_scan(proj3, b_comb, tabf, c_comb, dvec, seq, name):
    n_b, n_l, _ = proj3.shape
    u_blk = (D_MODEL + 2 * KV_WIDTH) // LANES

    def body(u_ref, b_ref, tab_ref, c_ref, d_ref, x_ref, y_ref, bu, xs):
        j = pl.program_id(1)
        u = u_ref[...]
        bu[...] = _dot(u, b_ref[...])
        tabs = [tab_ref[k] for k in range(TAB_ROWS)]

        def group(r0, carry):
            rows = pl.ds(r0, SUBLANES)
            a, b = _scan_rows(bu[rows, :SCAN_COLS], bu[rows, SCAN_COLS:], tabs, carry, False)
            xs[rows, :SCAN_COLS] = a
            xs[rows, SCAN_COLS:] = b
            return (jnp.broadcast_to(a[SUBLANES - 1:, :], a.shape), jnp.broadcast_to(b[SUBLANES - 1:, :], b.shape))

        zero = jnp.zeros((SUBLANES, SCAN_COLS), F32)
        carry = (zero, zero)
        for r0 in _time_groups(seq, False):
            carry = group(r0, carry)
        span = SCAN_UNROLL * SUBLANES

        def groups(t, c):
            for k in range(SCAN_UNROLL):
                c = group(pl.multiple_of(t * span, span) + k * SUBLANES, c)
            return c

        lax.fori_loop(0, seq // span, groups, carry)
        x16 = xs[...].astype(BF16)
        x_ref[...] = x16
        contrib = _dot(x16, c_ref[...])

        @pl.when(j % 2 == 0)
        def _():
            y_ref[...] = contrib + d_ref[...] * u.astype(F32)

        @pl.when(j % 2 == 1)
        def _():
            y_ref[...] += contrib

    return _pcall(
        body, name=name, grid=(n_b, N_SCAN_BLK),
        in_specs=[pl.BlockSpec((None, n_l, LANES), lambda b, j: (b, 0, u_blk + j // 2)),
                  pl.BlockSpec((LANES, 2 * SCAN_COLS), lambda b, j: (j // 2, j)),
                  pl.BlockSpec((TAB_ROWS, SUBLANES, SCAN_COLS), lambda b, j: (0, 0, j)),
                  pl.BlockSpec((2 * SCAN_COLS, LANES), lambda b, j: (j, j // 2)),
                  pl.BlockSpec((1, LANES), lambda b, j: (0, j // 2))],
        out_specs=[pl.BlockSpec((None, n_l, 2 * SCAN_COLS), lambda b, j: (b, 0, j)),
                   pl.BlockSpec((None, n_l, LANES), lambda b, j: (b, 0, j // 2))],
        out_shape=[jax.ShapeDtypeStruct((n_b, n_l, 2 * N_STATES), BF16),
                   jax.ShapeDtypeStruct((n_b, n_l, SSM_WIDTH), F32)],
        scratch_shapes=[pltpu.VMEM((n_l, 2 * SCAN_COLS), F32)] * 2,
        compiler_params=_cp(("arbitrary", "arbitrary"), VMEM_BIG),
    )(proj3, b_comb, tabf, c_comb, dvec)


def ssm_backward_scan(dyraw3, xs3, dproj3, c_comb_t, tabr, b_comb_t, dvec, seq, name):
    n_b, n_l, _ = xs3.shape
    u_blk = (D_MODEL + 2 * KV_WIDTH) // LANES

    def body(dy_ref, x_ref, _, c_ref, tab_ref, b_ref, d_ref, du_ref, g_ref, dlr_ref, dli_ref, dx, gs, xs, du_acc):
        j = pl.program_id(1)
        dy = dy_ref[...]
        dx[...] = _dot(dy, c_ref[...])
        xs[...] = x_ref[...].astype(F32)
        tabs = [tab_ref[k] for k in range(TAB_ROWS)]
        last_row = lax.broadcasted_iota(jnp.int32, (SUBLANES, SCAN_COLS), 0) == SUBLANES - 1

        def group(r0, state):
            cr, ci, acc_r, acc_i = state
            rows = pl.ds(r0, SUBLANES)
            a, b = _scan_rows(dx[rows, :SCAN_COLS], dx[rows, SCAN_COLS:], tabs, (cr, ci), True)
            gs[rows, :SCAN_COLS] = a
            gs[rows, SCAN_COLS:] = b
            na = jnp.where(last_row, cr, pltpu.roll(a, SUBLANES - 1, axis=0))
            nb = jnp.where(last_row, ci, pltpu.roll(b, SUBLANES - 1, axis=0))
            xa, xb = xs[rows, :SCAN_COLS], xs[rows, SCAN_COLS:]
            return (jnp.broadcast_to(a[:1, :], a.shape), jnp.broadcast_to(b[:1, :], b.shape),
                    acc_r + na * xa + nb * xb, acc_i + nb * xa - na * xb)

        zero = jnp.zeros((SUBLANES, SCAN_COLS), F32)
        span = SCAN_UNROLL * SUBLANES
        n_spans = seq // span

        def groups(t, s):
            for k in reversed(range(SCAN_UNROLL)):
                s = group(pl.multiple_of((n_spans - 1 - t) * span, span) + k * SUBLANES, s)
            return s

        state = lax.fori_loop(0, n_spans, groups, (zero, zero, zero, zero))
        for r0 in _time_groups(seq, True):
            state = group(r0, state)
        dlr_ref[...] = state[2]
        dli_ref[...] = state[3]
        g16 = gs[...].astype(BF16)
        g_ref[...] = g16
        contrib = _dot(g16, b_ref[...])

        @pl.when(j % 2 == 0)
        def _():
            du_acc[...] = contrib + d_ref[...] * dy.astype(F32)

        @pl.when(j % 2 == 1)
        def _():
            du_ref[...] = (du_acc[...] + contrib).astype(BF16)

    state_blk = pl.BlockSpec((None, n_l, 2 * SCAN_COLS), lambda b, j: (b, 0, j))
    dl_blk = pl.BlockSpec((None, SUBLANES, SCAN_COLS), lambda b, j: (b, 0, j))
    return _pcall(
        body, name=name, grid=(n_b, N_SCAN_BLK),
        in_specs=[pl.BlockSpec((None, n_l, LANES), lambda b, j: (b, 0, j // 2)), state_blk,
                  pl.BlockSpec(memory_space=pl.ANY),
                  pl.BlockSpec((LANES, 2 * SCAN_COLS), lambda b, j: (j // 2, j)),
                  pl.BlockSpec((TAB_ROWS, SUBLANES, SCAN_COLS), lambda b, j: (0, 0, j)),
                  pl.BlockSpec((2 * SCAN_COLS, LANES), lambda b, j: (j, j // 2)),
                  pl.BlockSpec((1, LANES), lambda b, j: (0, j // 2))],
        out_specs=[pl.BlockSpec((None, n_l, LANES), lambda b, j: (b, 0, u_blk + j // 2)), state_blk, dl_blk, dl_blk],
        out_shape=[jax.ShapeDtypeStruct(dproj3.shape, BF16), jax.ShapeDtypeStruct((n_b, n_l, 2 * N_STATES), BF16),
                   jax.ShapeDtypeStruct((n_b, SUBLANES, N_STATES), F32), jax.ShapeDtypeStruct((n_b, SUBLANES, N_STATES), F32)],
        scratch_shapes=[pltpu.VMEM((n_l, 2 * SCAN_COLS), F32)] * 3 + [pltpu.VMEM((n_l, LANES), F32)],
        input_output_aliases={2: 0},
        compiler_params=_cp(("arbitrary", "arbitrary"), VMEM_BIG),
    )(dyraw3, xs3, dproj3, c_comb_t, tabr, b_comb_t, dvec)


def ssm_param_grads(proj, gs, xs, dyraw, tm, name):
    t_rows = proj.shape[0]
    ni = t_rows // tm
    u_blk = (D_MODEL + 2 * KV_WIDTH) // LANES
    width = 2 * SCAN_COLS

    def body(u_ref, g_ref, x_ref, dy_ref, db_ref, dc_ref, dd_ref):
        cb, i = pl.program_id(0), pl.program_id(1)
        u, dy = u_ref[...], dy_ref[...]
        _accumulate(db_ref, _dot_tn(u, g_ref[...]), i == 0)
        _accumulate(dc_ref, _dot_tn(x_ref[...], dy), i == 0)

        @pl.when(cb % 2 == 0)
        def _():
            _accumulate(dd_ref, jnp.sum(dy.astype(F32) * u.astype(F32), axis=0, keepdims=True), i == 0)

    return _pcall(
        body, name=name, grid=(N_SCAN_BLK, ni),
        in_specs=[pl.BlockSpec((tm, LANES), lambda cb, i: (i, u_blk + cb // 2)),
                  pl.BlockSpec((tm, width), lambda cb, i: (i, cb)),
                  pl.BlockSpec((tm, width), lambda cb, i: (i, cb)),
                  pl.BlockSpec((tm, LANES), lambda cb, i: (i, cb // 2))],
        out_specs=[pl.BlockSpec((None, LANES, width), lambda cb, i: (cb, 0, 0)),
                   pl.BlockSpec((None, width, LANES), lambda cb, i: (cb, 0, 0)),
                   pl.BlockSpec((1, LANES), lambda cb, i: (0, cb // 2))],
        out_shape=[jax.ShapeDtypeStruct((N_SCAN_BLK, LANES, width), F32),
                   jax.ShapeDtypeStruct((N_SCAN_BLK, width, LANES), F32), jax.ShapeDtypeStruct((1, SSM_WIDTH), F32)],
        compiler_params=_cp(("arbitrary", "arbitrary"), VMEM_BIG),
    )(proj, gs, xs, dyraw)


def sum_leading(x, name):
    def body(x_ref, o_ref):
        acc = x_ref[0]
        for k in range(1, x.shape[0]):
            acc = acc + x_ref[k]
        o_ref[...] = acc

    return _pcall(body, name=name, out_shape=jax.ShapeDtypeStruct(x.shape[1:], x.dtype))(x)


WEIGHTS = ['meta_tokens', 'ffn1_norm', 'ffn1_w1', 'ffn1_w3', 'ffn1_w2', 'mix_norm', 'w_in', 'attn_sinks', 'ssm_a_re',
           'ssm_a_im', 'ssm_log_step', 'ssm_b_re', 'ssm_b_im', 'ssm_c_re', 'ssm_c_im', 'ssm_d', 'ssm_glu_a', 'ssm_glu_b',
           'w_out', 'ffn2_norm', 'ffn2_w1', 'ffn2_w3', 'ffn2_w2', 'final_norm']
SHARDED = ['ffn1_w1', 'ffn1_w3', 'ffn1_w2', 'ffn2_w1', 'ffn2_w3', 'ffn2_w2', 'w_in', 'ssm_glu_a', 'ssm_glu_b', 'w_out']
REPLICATED = ['ffn1_norm', 'mix_norm', 'ffn2_norm', 'final_norm', 'attn_sinks', 'ssm_a_re', 'ssm_a_im', 'ssm_log_step',
              'ssm_b_re', 'ssm_b_im', 'ssm_c_re', 'ssm_c_im', 'ssm_d']
PACK_COLS = 1024


def _block_diag(blocks):
    g, r, c = blocks.shape
    eye = jnp.eye(g, dtype=blocks.dtype)
    return (blocks[:, :, None, :] * eye[:, None, :, None]).reshape(g * r, g * c)


def _diag_blocks(mat, r, c):
    g = SSM_GROUPS
    eye = jnp.eye(g, dtype=mat.dtype)
    return jnp.sum(mat.reshape(g, r, g, c) * eye[:, None, :, None], axis=2)


def _scan_order(re, im):
    r = re.shape[0]
    return jnp.stack([re.reshape(r, N_SCAN_BLK, SCAN_COLS), im.reshape(r, N_SCAN_BLK, SCAN_COLS)], axis=2).reshape(r, 2 * N_STATES)


def _from_scan_order(comb):
    r = comb.shape[0]
    c4 = comb.reshape(r, N_SCAN_BLK, 2, SCAN_COLS)
    return c4[:, :, 0].reshape(r, N_STATES), c4[:, :, 1].reshape(r, N_STATES)


def _pack(arrays):
    parts = []
    for a in arrays:
        flat = a.reshape(-1)
        chunk = SUBLANES * PACK_COLS
        padded = -(-flat.shape[0] // chunk) * chunk
        parts.append(jnp.pad(flat, (0, padded - flat.shape[0])).reshape(-1, PACK_COLS))
    return jnp.concatenate(parts, axis=0)


def _unpack(packed, shapes):
    out, row = [], 0
    for shape in shapes:
        size = 1
        for s in shape:
            size *= s
        chunk = SUBLANES * PACK_COLS
        rows = -(-size // chunk) * SUBLANES
        out.append(packed[row:row + rows].reshape(-1)[:size].reshape(shape))
        row += rows
    return out


def kernel(x, meta_tokens, ffn1_norm, ffn1_w1, ffn1_w3, ffn1_w2, mix_norm, w_in, attn_sinks, ssm_a_re, ssm_a_im, ssm_log_step, ssm_b_re, ssm_b_im, ssm_c_re, ssm_c_im, ssm_d, ssm_glu_a, ssm_glu_b, w_out, ffn2_norm, ffn2_w1, ffn2_w3, ffn2_w2, final_norm, loss_target, m_meta_tokens, m_ffn1_norm, m_ffn1_w1, m_ffn1_w3, m_ffn1_w2, m_mix_norm, m_w_in, m_attn_sinks, m_ssm_a_re, m_ssm_a_im, m_ssm_log_step, m_ssm_b_re, m_ssm_b_im, m_ssm_c_re, m_ssm_c_im, m_ssm_d, m_ssm_glu_a, m_ssm_glu_b, m_w_out, m_ffn2_norm, m_ffn2_w1, m_ffn2_w3, m_ffn2_w2, m_final_norm, v_meta_tokens, v_ffn1_norm, v_ffn1_w1, v_ffn1_w3, v_ffn1_w2, v_mix_norm, v_w_in, v_attn_sinks, v_ssm_a_re, v_ssm_a_im, v_ssm_log_step, v_ssm_b_re, v_ssm_b_im, v_ssm_c_re, v_ssm_c_im, v_ssm_d, v_ssm_glu_a, v_ssm_glu_b, v_w_out, v_ffn2_norm, v_ffn2_w1, v_ffn2_w3, v_ffn2_w2, v_final_norm):
    given = dict(locals())
    w = {n: given[n] for n in WEIGHTS}
    m = {n: given["m_" + n] for n in WEIGHTS}
    v = {n: given["v_" + n] for n in WEIGHTS}

    n_b, seq, _ = x.shape
    n_l = seq + N_META
    t_rows = n_b * n_l
    tm = _row_tile(n_l, 688)
    px, py, pc = _my_place()
    me = 4 * px + 2 * py + pc
    dest = jnp.stack([4 * qx + 2 * qy + pc for qx, qy in
                      [(px, py), (1 - px, py), (px, 1 - py), (1 - px, 1 - py)]]).astype(jnp.int32)

    glu = jnp.stack([ssm_glu_a[0], ssm_glu_b[0]]).astype(BF16)
    ffn_names = ['ffn1_w1', 'ffn1_w3', 'ffn1_w2', 'ffn2_w1', 'ffn2_w3', 'ffn2_w2']

    def hidden_on_rows(n, t):
        return t[0] if n.endswith('w2') else t[0].T

    def hidden_on_rows_back(n, t):
        return t[None] if n.endswith('w2') else t.T[None]

    me_idx = jnp.reshape(me, (1,)).astype(jnp.int32)
    first_names, later_names = ffn_names[:3], ffn_names[3:]
    *first, wing, metag = all_gather_list(
        [hidden_on_rows(n, w[n]).astype(BF16) for n in first_names] + [w_in[0].astype(BF16), meta_tokens], "ag_first")
    later_shards = [hidden_on_rows(n, w[n]).astype(BF16) for n in later_names] + [glu, w_out[0].astype(BF16)]
    *later_shards, wing = lax.optimization_barrier((*later_shards, wing))
    ag_send, ag_recv, later_shards, later_lands, ag_token = exchange_start(later_shards, True, "ag_later_start")
    full = {n: g.reshape(D_FF, D_MODEL) for n, g in zip(first_names, first)}
    meta_full = metag.transpose(1, 0, 2).reshape(N_META, D_MODEL)

    h0 = jnp.concatenate([x, jnp.broadcast_to(meta_full[None], (n_b, N_META, D_MODEL))], axis=1).reshape(t_rows, D_MODEL)
    target = jnp.concatenate([loss_target, jnp.zeros((n_b, N_META, D_MODEL), F32)], axis=1).reshape(t_rows, D_MODEL)
    final_g = final_norm.reshape(1, D_MODEL)

    ar = ssm_a_re.reshape(1, N_STATES)
    ai = ssm_a_im.reshape(1, N_STATES)
    ls = jnp.repeat(ssm_log_step.reshape(SSM_GROUPS), SSM_STATE).reshape(1, N_STATES)
    br_t = ssm_b_re[0].transpose(2, 0, 1).reshape(SSM_GROUP, N_STATES)
    bi_t = ssm_b_im[0].transpose(2, 0, 1).reshape(SSM_GROUP, N_STATES)
    bbr, bbi, tabf, tabr = ssm_prepare(ar, ai, ls, br_t, bi_t, "ssm_prepare")
    bbr_g = bbr.reshape(SSM_GROUP, SSM_GROUPS, SSM_STATE).transpose(1, 0, 2)
    bbi_g = bbi.reshape(SSM_GROUP, SSM_GROUPS, SSM_STATE).transpose(1, 0, 2)
    b_comb = _scan_order(_block_diag(bbr_g), _block_diag(bbi_g)).astype(BF16)
    c_comb_t = _scan_order(_block_diag(ssm_c_re[0]), -_block_diag(ssm_c_im[0])).astype(BF16)
    b_comb_t, c_comb = b_comb.T, c_comb_t.T

    ffn1_w = (full['ffn1_w1'], full['ffn1_w3'], full['ffn1_w2'])
    h0, _ = lax.optimization_barrier((h0, ag_token))
    h1, hn1, a1, b1 = ffn_forward(h0, ffn1_norm, *ffn1_w, tm, FF_FWD_COLS, "ffn1_fwd")
    hnm, proj = mix_forward(h1, mix_norm, wing, tm, "mix_fwd")
    proj3 = proj.reshape(n_b, n_l, IN_WIDTH)
    attn3 = attention_forward(proj3, attn_sinks, seq, "attn_fwd")
    attn = attn3.reshape(t_rows, D_MODEL)
    xs3, yraw3 = ssm_forward_scan(proj3, b_comb, tabf, c_comb, ssm_d, seq, "ssm_fwd")
    yraw = yraw3.reshape(t_rows, SSM_WIDTH)
    later = exchange_wait(ag_send, ag_recv, later_shards, later_lands, yraw3, True, "ag_later_wait")
    later = [lax.dynamic_update_slice_in_dim(z, s[None], me, axis=0) for z, s in zip(later, later_shards)]
    for n, g in zip(later_names, later):
        full[n] = g.reshape(D_FF, D_MODEL)
    ffn2_w = (full['ffn2_w1'], full['ffn2_w3'], full['ffn2_w2'])
    glug, wog = later[len(later_names):]
    glu_a = glug[:, 0].transpose(1, 0, 2).reshape(SSM_WIDTH, D_MODEL)
    glu_b = glug[:, 1].transpose(1, 0, 2).reshape(SSM_WIDTH, D_MODEL)
    w_out_full = wog.reshape(D_MODEL, D_MODEL)
    h2 = merge_forward(h1, yraw, attn, proj, glu_a, glu_b, w_out_full, tm, "merge_fwd")
    h3, hn2, a2, b2 = ffn_forward(h2, ffn2_norm, *ffn2_w, tm, FF_FWD_COLS, "ffn2_fwd")
    dh3, loss_part, g_final = final_loss_backward(h3, target, final_g, seq, tm, "loss_bwd")
    loss = lax.psum(loss_part[0, 0], ("x", "y", "c"))

    def blocked_ffn(d_w1t, d_w3t, d_w2):
        return tuple(t.reshape(N_DEV, FF_BLK, D_MODEL) for t in (d_w1t, d_w3t, d_w2))

    da2, db2, dh3_half = ffn_backward_hidden(dh3, a2, b2, ffn2_w[2], tm, "ffn2_bwd_hid")
    dh2, g_ffn2_norm = ffn_backward_input(dh3, h2, ffn2_norm, da2, db2, ffn2_w[0], ffn2_w[1], tm, "ffn2_bwd_in")
    dw = {}
    dw['ffn2_w1'], dw['ffn2_w3'], dw['ffn2_w2'] = blocked_ffn(
        *ffn_backward_weights(hn2, dh3_half, a2, b2, da2, db2, n_l, FF_BWD_COLS, "ffn2_bwd_w"))

    def blocked_cols(full_grad):
        r = full_grad.shape[0]
        return full_grad.reshape(r, N_DEV, full_grad.shape[1] // N_DEV).transpose(1, 0, 2).astype(BF16)

    early = {}

    def start_reduce(names, tag, tied):
        send, recv, srcs, lands, token = exchange_start([dw[n] for n in names], False, "rs_" + tag + "_start")
        early[tag] = (names, send, recv, srcs, lands)
        return lax.optimization_barrier((tied, token))[0]

    dh2 = start_reduce(['ffn2_w1', 'ffn2_w3', 'ffn2_w2'], "ffn2", dh2)
    dattn, dyraw, dproj, *for_weights = merge_backward(dh2, yraw, attn, proj, glu_a, glu_b, w_out_full, tm, "merge_bwd")
    d_wo, d_ga, d_gb = merge_backward_weights(*for_weights, tm, "merge_bwd_w")
    dw['ssm_glu_a'] = blocked_cols(d_ga)
    dw['ssm_glu_b'] = blocked_cols(d_gb)
    dw['w_out'] = d_wo.reshape(N_DEV, D_MODEL // N_DEV, D_MODEL).astype(BF16)
    dproj = start_reduce(['ssm_glu_a', 'ssm_glu_b', 'w_out'], "mix", dproj)
    dproj3 = dproj.reshape(n_b, n_l, IN_WIDTH)
    dproj3, dsink_p = attention_backward(proj3, dattn.reshape(n_b, n_l, D_MODEL), dproj3, attn_sinks, seq, "attn_bwd")
    dproj3, gs3, dlr_p, dli_p = ssm_backward_scan(
        dyraw.reshape(n_b, n_l, SSM_WIDTH), xs3, dproj3, c_comb_t, tabr, b_comb_t, ssm_d, seq, "ssm_bwd")
    dproj = dproj3.reshape(t_rows, IN_WIDTH)
    d_bd, d_cd, g_d = ssm_param_grads(proj, gs3.reshape(t_rows, 2 * N_STATES), xs3.reshape(t_rows, 2 * N_STATES),
                                      dyraw, n_l, "ssm_bwd_w")
    dh1, g_mix_norm = mix_backward_act(dh2, h1, mix_norm, dproj, wing, tm, "mix_bwd_act")
    dw['w_in'] = mix_backward_weights(hnm, dproj, n_l, "mix_bwd_w")
    dh1 = start_reduce(['w_in'], "w_in", dh1)
    da1, db1, dh1_half = ffn_backward_hidden(dh1, a1, b1, ffn1_w[2], tm, "ffn1_bwd_hid")
    dh0, g_ffn1_norm = ffn_backward_input(dh1, h0, ffn1_norm, da1, db1, ffn1_w[0], ffn1_w[1], tm, "ffn1_bwd_in")
    dw['ffn1_w1'], dw['ffn1_w3'], dw['ffn1_w2'] = blocked_ffn(
        *ffn_backward_weights(hn1, dh1_half, a1, b1, da1, db1, n_l, FF_BWD_COLS, "ffn1_bwd_w"))
    dh0_3 = dh0.reshape(n_b, n_l, D_MODEL)
    grad_x = dh0_3[:, :seq]
    g_meta = sum_leading(dh0_3[:, seq:], "meta_sum")

    groups_per_blk = SCAN_COLS // SSM_STATE
    half = ((jnp.arange(N_SCAN_BLK) % 2)[:, None] == jnp.arange(2)[None, :]).astype(F32)
    eye = jnp.eye(groups_per_blk, dtype=F32)

    def group_blocks(part, channels_first):
        if channels_first:
            t = jnp.sum(part.reshape(N_SCAN_BLK, 2, LANES // 2, SCAN_COLS) * half[:, :, None, None], axis=1)
            t = t.reshape(N_SCAN_BLK, groups_per_blk, SSM_GROUP, groups_per_blk, SSM_STATE)
            t = jnp.sum(t * eye[None, :, None, :, None], axis=3)
            return t.reshape(SSM_GROUPS, SSM_GROUP, SSM_STATE)
        t = jnp.sum(part.reshape(N_SCAN_BLK, SCAN_COLS, 2, LANES // 2) * half[:, None, :, None], axis=2)
        t = t.reshape(N_SCAN_BLK, groups_per_blk, SSM_STATE, groups_per_blk, SSM_GROUP)
        t = jnp.sum(t * eye[None, :, None, :, None], axis=3)
        return t.reshape(SSM_GROUPS, SSM_STATE, SSM_GROUP).transpose(0, 2, 1)

    dbbr = group_blocks(d_bd[:, :, :SCAN_COLS], True).transpose(1, 0, 2).reshape(SSM_GROUP, N_STATES)
    dbbi = group_blocks(d_bd[:, :, SCAN_COLS:], True).transpose(1, 0, 2).reshape(SSM_GROUP, N_STATES)
    g_c_re = group_blocks(d_cd[:, :SCAN_COLS, :], False)[None]
    g_c_im = -group_blocks(d_cd[:, SCAN_COLS:, :], False)[None]
    group_sum = (jnp.arange(N_STATES)[:, None] // SSM_STATE == jnp.arange(LANES)[None, :]).astype(F32)
    g_ar, g_ai, g_ls, g_br, g_bi = ssm_param_backward(
        ar, ai, ls, br_t, bi_t, dlr_p.reshape(n_b * SUBLANES, N_STATES), dli_p.reshape(n_b * SUBLANES, N_STATES),
        dbbr, dbbi, group_sum, "ssm_bwd_params")
    g_sinks = sum_leading(dsink_p, "sink_sum")[0:1, :N_KV_HEADS * Q_PER_KV]

    small = {
        'ffn1_norm': g_ffn1_norm, 'mix_norm': g_mix_norm, 'ffn2_norm': g_ffn2_norm, 'final_norm': g_final.reshape(D_MODEL),
        'attn_sinks': g_sinks, 'ssm_a_re': g_ar.reshape(1, SSM_GROUPS, SSM_STATE), 'ssm_a_im': g_ai.reshape(1, SSM_GROUPS, SSM_STATE),
        'ssm_log_step': g_ls[:, :SSM_GROUPS],
        'ssm_b_re': g_br.reshape(SSM_GROUP, SSM_GROUPS, SSM_STATE).transpose(1, 2, 0)[None],
        'ssm_b_im': g_bi.reshape(SSM_GROUP, SSM_GROUPS, SSM_STATE).transpose(1, 2, 0)[None],
        'ssm_c_re': g_c_re, 'ssm_c_im': g_c_im, 'ssm_d': g_d,
    }

    zeros_meta = jnp.zeros((N_META, D_MODEL), F32)
    packed_g = _pack([small[n] for n in REPLICATED] + [g_meta])
    (parts,) = all_gather_list([packed_g], "ag_small_grads")
    packed_out = adamw_small(parts, _pack([w[n] for n in REPLICATED] + [zeros_meta]),
                             _pack([m[n] for n in REPLICATED] + [zeros_meta]),
                             _pack([v[n] for n in REPLICATED] + [zeros_meta]), "adamw_small")
    shapes = [w[n].shape for n in REPLICATED] + [(N_META, D_MODEL)]
    grads, deltas, new_m, new_v = {}, {}, {}, {}
    unpacked = [_unpack(p, shapes) for p in packed_out]
    for k, n in enumerate(REPLICATED):
        grads[n], deltas[n], new_m[n], new_v[n] = (u[k] for u in unpacked)
    g_meta_full = unpacked[0][-1]
    grads['meta_tokens'] = lax.dynamic_index_in_dim(
        g_meta_full.reshape(N_META, N_DEV, D_MODEL // N_DEV), me, axis=1, keepdims=False)
    deltas['meta_tokens'], new_m['meta_tokens'], new_v['meta_tokens'] = adamw_plain(
        grads['meta_tokens'], w['meta_tokens'], m['meta_tokens'], v['meta_tokens'], "adamw_meta")

    def views(n):
        if n in ffn_names:
            return functools.partial(hidden_on_rows, n), functools.partial(hidden_on_rows_back, n)
        return (lambda t: t[0]), (lambda t: t[None])

    g_list = [dw[n] for n in first_names]
    r1 = rs_sibling_swap(g_list, "rs_sibling")
    pairs = [pair_sums(dest, g, r, "rs_pair_" + n) for n, g, r in zip(first_names, g_list, r1)]
    r2 = rs_chip_exchange(pairs, "rs_chips")
    for n, g, ra, rb in zip(first_names, g_list, r1, r2):
        two_d, back = views(n)
        out = adamw_sharded(dest, g, ra, rb, two_d(w[n]), two_d(m[n]), two_d(v[n]), "adamw_" + n)
        grads[n], deltas[n], new_m[n], new_v[n] = (back(o) for o in out)
    for tag, (names, send, recv, srcs, lands) in early.items():
        lands = exchange_wait(send, recv, srcs, lands, r2[0], False, "rs_" + tag + "_wait")
        for n, g, land in zip(names, srcs, lands):
            two_d, back = views(n)
            out = adamw_exchanged(me_idx, g, land, two_d(w[n]), two_d(m[n]), two_d(v[n]), "adamw_" + n)
            grads[n], deltas[n], new_m[n], new_v[n] = (back(o) for o in out)

    return (loss, grad_x, *[grads[n] for n in WEIGHTS], *[deltas[n] for n in WEIGHTS],
            *[new_m[n] for n in WEIGHTS], *[new_v[n] for n in WEIGHTS])
```

```python
import functools

import jax
import jax.numpy as jnp
from jax import lax
from jax.experimental import pallas as pl
from jax.experimental.pallas import tpu as pltpu

F32 = jnp.float32
BF16 = jnp.bfloat16
MESH = pl.DeviceIdType.MESH

N_DEV = 8
D_MODEL = 1024
N_META = 16
HEAD_DIM = 64
N_KV_HEADS = 4
Q_PER_KV = 4
BLOCK = 128
KV_WIDTH = N_KV_HEADS * HEAD_DIM
SSM_GROUP = 16
SSM_WIDTH = 512
SSM_GROUPS = 32
SSM_STATE = 64
N_STATES = SSM_GROUPS * SSM_STATE
D_FF = 2816
FF_BLK = D_FF // N_DEV
IN_WIDTH = 4096
IN_BLK = IN_WIDTH // N_DEV
NORM_EPS = 1e-6
NEG_INF = -1e30
SCAN_COLS = 256
N_SCAN_BLK = N_STATES // SCAN_COLS
SUBLANES = 8
LANES = 128
MXU_WIDTH = 256
FF_FWD_COLS = D_FF // 2
FF_BWD_COLS = MXU_WIDTH

ADAM_LR = 0.001
ADAM_B1 = 0.9
ADAM_B2 = 0.999
ADAM_EPS = 1e-08
ADAM_WD = 0.01
ADAM_STEP = 10

VMEM_BIG = 56 * 1024 * 1024


def _cp(sem=None, vmem=None):
    kw = {}
    if sem is not None:
        kw["dimension_semantics"] = sem
    if vmem is not None:
        kw["vmem_limit_bytes"] = vmem
    return pltpu.CompilerParams(**kw)


def _pcall(body, **kw):
    return pl.pallas_call(body, **kw)


def _dot(a, b):
    return jnp.dot(a, b, preferred_element_type=F32)


def _dot_nt(a, b):
    return lax.dot_general(a, b, (((1,), (1,)), ((), ())), preferred_element_type=F32)


def _dot_tn(a, b):
    return lax.dot_general(a, b, (((0,), (0,)), ((), ())), preferred_element_type=F32)


def _sigmoid(x):
    return 1.0 / (1.0 + jnp.exp(-x))


def _row_tile(rows, cap):
    best = None
    for t in range(16, min(rows, cap) + 1, 16):
        if rows % t == 0:
            best = t
    assert best is not None, rows
    return best


def _my_place():
    return lax.axis_index("x"), lax.axis_index("y"), lax.axis_index("c")


def all_gather_list(shards, name):
    n = len(shards)

    def body(*refs):
        ins, outs = refs[:n], refs[n:2 * n]
        send_sems, recv_sems, local_sems = refs[2 * n:]
        x, y, c = _my_place()
        me, sibling = (x, y, c), (x, y, 1 - c)
        chips = [(1 - x, y), (x, 1 - y), (1 - x, 1 - y)]

        def blk(a, px, py, pc):
            return outs[a].at[4 * px + 2 * py + pc]

        def copy(a, k, block, to, src=None):
            return pltpu.make_async_remote_copy(
                src_ref=blk(a, *block) if src is None else src, dst_ref=blk(a, *block),
                send_sem=send_sems.at[a * 7 + k], recv_sem=recv_sems.at[a * 7 + k],
                device_id=to, device_id_type=MESH)

        mine = [pltpu.make_async_copy(ins[a], blk(a, *me), local_sems.at[a]) for a in range(n)]
        for cp in mine:
            cp.start()
        first = []
        for a in range(n):
            first.append(copy(a, 0, me, sibling, src=ins[a]))
            first += [copy(a, 1 + j, me, (*chip, c), src=ins[a]) for j, chip in enumerate(chips)]
        for cp in first:
            cp.start()
        passed = []
        for j, chip in enumerate(chips):
            for a in range(n):
                copy(a, 1 + j, (*chip, c), me).wait_recv()
                cp = copy(a, 4 + j, (*chip, c), sibling)
                cp.start()
                passed.append(cp)
        for a in range(n):
            copy(a, 0, sibling, me).wait_recv()
            for j, chip in enumerate(chips):
                copy(a, 4 + j, (*chip, 1 - c), me).wait_recv()
        for cp in first + passed:
            cp.wait_send()
        for cp in mine:
            cp.wait()

    any_spec = pl.BlockSpec(memory_space=pl.ANY)
    return _pcall(
        body, name=name,
        out_shape=[jax.ShapeDtypeStruct((N_DEV,) + s.shape, s.dtype) for s in shards],
        in_specs=[any_spec] * n, out_specs=[any_spec] * n,
        scratch_shapes=[pltpu.SemaphoreType.DMA((7 * n,)), pltpu.SemaphoreType.DMA((7 * n,)),
                        pltpu.SemaphoreType.DMA((n,))],
    )(*shards)


def rs_sibling_swap(grads, name):
    n = len(grads)

    def body(*refs):
        ins, outs = refs[:n], refs[n:2 * n]
        send_sems, recv_sems = refs[2 * n:]
        x, y, c = _my_place()
        chips = [(x, y), (1 - x, y), (x, 1 - y), (1 - x, 1 - y)]
        copies = []
        for a in range(n):
            for k, (px, py) in enumerate(chips):
                copies.append(pltpu.make_async_remote_copy(
                    src_ref=ins[a].at[4 * px + 2 * py + (1 - c)], dst_ref=outs[a].at[k],
                    send_sem=send_sems.at[4 * a + k], recv_sem=recv_sems.at[4 * a + k],
                    device_id=(x, y, 1 - c), device_id_type=MESH))
        for cp in copies:
            cp.start()
        for cp in copies:
            cp.wait()

    any_spec = pl.BlockSpec(memory_space=pl.ANY)
    return _pcall(
        body, name=name,
        out_shape=[jax.ShapeDtypeStruct((4,) + g.shape[1:], g.dtype) for g in grads],
        in_specs=[any_spec] * n, out_specs=[any_spec] * n,
        scratch_shapes=[pltpu.SemaphoreType.DMA((4 * n,)), pltpu.SemaphoreType.DMA((4 * n,))],
    )(*grads)


def rs_chip_exchange(parts, name):
    n = len(parts)

    def body(*refs):
        ins, outs = refs[:n], refs[n:2 * n]
        send_sems, recv_sems = refs[2 * n:]
        x, y, c = _my_place()
        chips = [(1 - x, y), (x, 1 - y), (1 - x, 1 - y)]
        copies = []
        for a in range(n):
            for k, (px, py) in enumerate(chips):
                copies.append(pltpu.make_async_remote_copy(
                    src_ref=ins[a].at[k], dst_ref=outs[a].at[k],
                    send_sem=send_sems.at[3 * a + k], recv_sem=recv_sems.at[3 * a + k],
                    device_id=(px, py, c), device_id_type=MESH))
        for cp in copies:
            cp.start()
        for cp in copies:
            cp.wait()

    any_spec = pl.BlockSpec(memory_space=pl.ANY)
    return _pcall(
        body, name=name,
        out_shape=[jax.ShapeDtypeStruct(p.shape, p.dtype) for p in parts],
        in_specs=[any_spec] * n, out_specs=[any_spec] * n,
        scratch_shapes=[pltpu.SemaphoreType.DMA((3 * n,)), pltpu.SemaphoreType.DMA((3 * n,))],
    )(*parts)


HBM_SPEC = pl.BlockSpec(memory_space=pltpu.HBM)
SEM_SPEC = pl.BlockSpec(memory_space=pltpu.SEMAPHORE)
N_PEERS = N_DEV - 1


def _related(k):
    x, y, c = _my_place()
    px = 1 - x if k & 4 else x
    py = 1 - y if k & 2 else y
    pc = 1 - c if k & 1 else c
    return (px, py, pc), 4 * px + 2 * py + pc


def _exchange_copies(srcs, lands, send_sems, recv_sems, gather):
    x, y, c = _my_place()
    me = 4 * x + 2 * y + c
    copies = []
    for a, (src, land) in enumerate(zip(srcs, lands)):
        for k in range(1, N_DEV):
            peer, d = _related(k)
            copies.append(pltpu.make_async_remote_copy(
                src_ref=src if gather else src.at[d], dst_ref=land.at[me] if gather else land.at[k],
                send_sem=send_sems.at[a * N_PEERS + k - 1], recv_sem=recv_sems.at[a * N_PEERS + k - 1],
                device_id=peer, device_id_type=MESH))
    return copies


def exchange_start(srcs, after, gather, name):
    n = len(srcs)
    land_shapes = [((N_DEV,) + s.shape) if gather else s.shape for s in srcs]

    def body(*refs):
        send_sems, recv_sems = refs[2 * n + 1], refs[2 * n + 2]
        for cp in _exchange_copies(refs[:n], refs[n:2 * n], send_sems, recv_sems, gather):
            cp.start()
        token = refs[-1]
        token[...] = jnp.zeros_like(token)

    sems = pltpu.SemaphoreType.DMA((n * N_PEERS,))
    lands = [pltpu.with_memory_space_constraint(lax.empty(shape, s.dtype), pltpu.HBM) for shape, s in zip(land_shapes, srcs)]
    out = _pcall(
        body, name=name,
        out_shape=(sems, sems, *[pltpu.HBM(s.shape, s.dtype) for s in srcs],
                   *[pltpu.HBM(shape, s.dtype) for shape, s in zip(land_shapes, srcs)],
                   jax.ShapeDtypeStruct((SUBLANES, LANES), F32)),
        in_specs=[HBM_SPEC] * (2 * n) + [pl.BlockSpec(memory_space=pl.ANY)],
        out_specs=(SEM_SPEC, SEM_SPEC, *[HBM_SPEC] * (2 * n), pl.BlockSpec(memory_space=pltpu.VMEM)),
        input_output_aliases={i: 2 + i for i in range(2 * n)},
        compiler_params=pltpu.CompilerParams(has_side_effects=pltpu.SideEffectType.DATAFLOW_SIDE_EFFECTING),
    )(*[pltpu.with_memory_space_constraint(s, pltpu.HBM) for s in srcs], *lands, after)
    return out[0], out[1], list(out[2:2 + n]), list(out[2 + n:2 + 2 * n]), out[-1]


def exchange_wait(send_sems, recv_sems, srcs, lands, after, gather, name):
    n = len(srcs)

    def body(*refs):
        for cp in _exchange_copies(refs[:n], refs[n:2 * n], refs[2 * n], refs[2 * n + 1], gather):
            cp.wait_send()
            cp.wait_recv()

    out = _pcall(
        body, name=name,
        out_shape=(*[pltpu.HBM(s.shape, s.dtype) for s in srcs], *[pltpu.HBM(z.shape, z.dtype) for z in lands]),
        in_specs=[HBM_SPEC] * (2 * n) + [SEM_SPEC, SEM_SPEC, pl.BlockSpec(memory_space=pl.ANY)],
        out_specs=tuple([HBM_SPEC] * (2 * n)),
        input_output_aliases={i: i for i in range(2 * n)},
        compiler_params=pltpu.CompilerParams(has_side_effects=pltpu.SideEffectType.DATAFLOW_SIDE_EFFECTING),
    )(*srcs, *lands, send_sems, recv_sems, after)
    return list(out[n:])


def adamw_exchanged(me, g, land, w, m, v, name):
    rows, cols = w.shape
    tr = _row_tile(rows, 256)

    def body(me_ref, g_ref, land_ref, w_ref, m_ref, v_ref, go_ref, d_ref, mo_ref, vo_ref):
        grad = g_ref[...].astype(F32)
        for k in range(1, N_DEV):
            grad = grad + land_ref[k].astype(F32)
        delta, m_new, v_new = _adam_math(w_ref[...], grad, m_ref[...], v_ref[...])
        go_ref[...] = grad
        d_ref[...] = delta
        mo_ref[...] = m_new
        vo_ref[...] = v_new

    tile = pl.BlockSpec((tr, cols), lambda r, ix: (r, 0))
    out = jax.ShapeDtypeStruct((rows, cols), F32)
    return _pcall(
        body, name=name, out_shape=[out] * 4,
        grid_spec=pltpu.PrefetchScalarGridSpec(
            num_scalar_prefetch=1, grid=(rows // tr,),
            in_specs=[pl.BlockSpec((None, tr, cols), lambda r, ix: (ix[0], r, 0)),
                      pl.BlockSpec((N_DEV, tr, cols), lambda r, ix: (0, r, 0)), tile, tile, tile],
            out_specs=[tile] * 4),
        compiler_params=_cp(("arbitrary",)),
    )(me, g, land, w, m, v)


def pair_sums(idx, g, r1, name):
    _, rows, cols = g.shape
    tr = _row_tile(rows, 256)

    def body(idx_ref, g_ref, r_ref, o_ref):
        o_ref[...] = (g_ref[...].astype(F32) + r_ref[...].astype(F32)).astype(BF16)

    return _pcall(
        body, name=name,
        out_shape=jax.ShapeDtypeStruct((3, rows, cols), BF16),
        grid_spec=pltpu.PrefetchScalarGridSpec(
            num_scalar_prefetch=1, grid=(3, rows // tr),
            in_specs=[pl.BlockSpec((None, tr, cols), lambda k, r, ix: (ix[k + 1], r, 0)),
                      pl.BlockSpec((None, tr, cols), lambda k, r, ix: (k + 1, r, 0))],
            out_specs=pl.BlockSpec((None, tr, cols), lambda k, r, ix: (k, r, 0))),
        compiler_params=_cp(("arbitrary", "arbitrary")),
    )(idx, g, r1)


def _adam_math(w, g, m, v):
    m = ADAM_B1 * m + (1.0 - ADAM_B1) * g
    v = ADAM_B2 * v + (1.0 - ADAM_B2) * (g * g)
    m_hat = m / (1.0 - ADAM_B1 ** ADAM_STEP)
    v_hat = v / (1.0 - ADAM_B2 ** ADAM_STEP)
    delta = -ADAM_LR * (m_hat / (jnp.sqrt(v_hat) + ADAM_EPS) + ADAM_WD * w)
    return delta, m, v


def adamw_sharded(idx, g, r1, r2, w, m, v, name):
    rows, cols = w.shape
    tr = _row_tile(rows, 256)

    def body(idx_ref, g_ref, r1_ref, r2_ref, w_ref, m_ref, v_ref, go_ref, d_ref, mo_ref, vo_ref):
        grad = g_ref[...].astype(F32) + r1_ref[...].astype(F32)
        for k in range(3):
            grad = grad + r2_ref[k].astype(F32)
        delta, m_new, v_new = _adam_math(w_ref[...], grad, m_ref[...], v_ref[...])
        go_ref[...] = grad
        d_ref[...] = delta
        mo_ref[...] = m_new
        vo_ref[...] = v_new

    tile = pl.BlockSpec((tr, cols), lambda r, ix: (r, 0))
    out = jax.ShapeDtypeStruct((rows, cols), F32)
    return _pcall(
        body, name=name, out_shape=[out] * 4,
        grid_spec=pltpu.PrefetchScalarGridSpec(
            num_scalar_prefetch=1, grid=(rows // tr,),
            in_specs=[pl.BlockSpec((None, tr, cols), lambda r, ix: (ix[0], r, 0)),
                      pl.BlockSpec((None, tr, cols), lambda r, ix: (0, r, 0)),
                      pl.BlockSpec((3, tr, cols), lambda r, ix: (0, r, 0)),
                      tile, tile, tile],
            out_specs=[tile] * 4),
        compiler_params=_cp(("arbitrary",)),
    )(idx, g, r1, r2, w, m, v)


def adamw_small(parts, w, m, v, name):
    _, rows, cols = parts.shape

    def body(p_ref, w_ref, m_ref, v_ref, go_ref, d_ref, mo_ref, vo_ref):
        grad = p_ref[0]
        for k in range(1, N_DEV):
            grad = grad + p_ref[k]
        delta, m_new, v_new = _adam_math(w_ref[...], grad, m_ref[...], v_ref[...])
        go_ref[...] = grad
        d_ref[...] = delta
        mo_ref[...] = m_new
        vo_ref[...] = v_new

    out = jax.ShapeDtypeStruct((rows, cols), F32)
    return _pcall(body, name=name, out_shape=[out] * 4, compiler_params=_cp(vmem=VMEM_BIG))(parts, w, m, v)


def adamw_plain(g, w, m, v, name):
    def body(g_ref, w_ref, m_ref, v_ref, d_ref, mo_ref, vo_ref):
        delta, m_new, v_new = _adam_math(w_ref[...], g_ref[...], m_ref[...], v_ref[...])
        d_ref[...] = delta
        mo_ref[...] = m_new
        vo_ref[...] = v_new

    out = jax.ShapeDtypeStruct(w.shape, F32)
    return _pcall(body, name=name, out_shape=[out] * 3)(g, w, m, v)


def _rms_fwd(x, g):
    r = lax.rsqrt(jnp.mean(x * x, axis=-1, keepdims=True) + NORM_EPS)
    return x * r * g


def _rms_bwd(x, g, dy):
    r = lax.rsqrt(jnp.mean(x * x, axis=-1, keepdims=True) + NORM_EPS)
    xh = x * r
    t = dy * g
    dx = r * (t - xh * jnp.mean(t * xh, axis=-1, keepdims=True))
    return dx, jnp.sum(dy * xh, axis=0, keepdims=True)


def _accumulate(ref, val, first):
    @pl.when(first)
    def _():
        ref[...] = val

    @pl.when(jnp.logical_not(first))
    def _():
        ref[...] += val


def _col_chunks(width):
    return [(c0, min(MXU_WIDTH, width - c0)) for c0 in range(0, width, MXU_WIDTH)]


ANY_SPEC = pl.BlockSpec(memory_space=pl.ANY)


def ffn_forward(h, norm, w1, w3, w2, after, tm, tn, name):
    t_rows = h.shape[0]
    nj = D_FF // tn

    def body(h_ref, g_ref, w1_ref, w3_ref, w2_ref, _, out_ref, hn_ref, a_ref, b_ref, acc_ref):
        j = pl.program_id(1)

        @pl.when(j == 0)
        def _():
            hn_ref[...] = _rms_fwd(h_ref[...], g_ref[...]).astype(BF16)
            acc_ref[...] = jnp.zeros_like(acc_ref)

        hn = hn_ref[...]
        for c0, cw in _col_chunks(tn):
            a = _dot_nt(hn, w1_ref[c0:c0 + cw, :])
            b = _dot_nt(hn, w3_ref[c0:c0 + cw, :])
            a_ref[:, c0:c0 + cw] = a.astype(BF16)
            b_ref[:, c0:c0 + cw] = b.astype(BF16)
            hid = (a * _sigmoid(a) * b).astype(BF16)
            acc_ref[...] += _dot(hid, w2_ref[c0:c0 + cw, :])

        @pl.when(j == nj - 1)
        def _():
            out_ref[...] = h_ref[...] + 0.5 * acc_ref[...]

    row = pl.BlockSpec((tm, D_MODEL), lambda i, j: (i, 0))
    hid_blk = pl.BlockSpec((tm, tn), lambda i, j: (i, j))
    w_row = pl.BlockSpec((tn, D_MODEL), lambda i, j: (j, 0))
    return _pcall(
        body, name=name, grid=(t_rows // tm, nj),
        in_specs=[row, pl.BlockSpec((1, D_MODEL), lambda i, j: (0, 0)), w_row, w_row, w_row, ANY_SPEC],
        out_specs=[row, row, hid_blk, hid_blk],
        out_shape=[jax.ShapeDtypeStruct((t_rows, D_MODEL), F32), jax.ShapeDtypeStruct((t_rows, D_MODEL), BF16),
                   jax.ShapeDtypeStruct((t_rows, D_FF), BF16), jax.ShapeDtypeStruct((t_rows, D_FF), BF16)],
        scratch_shapes=[pltpu.VMEM((tm, D_MODEL), F32)],
        compiler_params=_cp(("arbitrary", "arbitrary"), VMEM_BIG),
    )(h, norm, w1, w3, w2, after)


def _resident(shape):
    return pl.BlockSpec(shape, lambda *_: (0,) * len(shape), pipeline_mode=pl.Buffered(1))


def ffn_backward_hidden(dh, a, b, w2, after, tm, name):
    t_rows = dh.shape[0]

    def body(dh_ref, a_ref, b_ref, w2_ref, _, da_ref, db_ref, dhb_ref):
        dhb = (0.5 * dh_ref[...]).astype(BF16)
        dhb_ref[...] = dhb
        for c0, cw in _col_chunks(D_FF):
            dhid = _dot_nt(dhb, w2_ref[c0:c0 + cw, :])
            av = a_ref[:, c0:c0 + cw].astype(F32)
            bv = b_ref[:, c0:c0 + cw].astype(F32)
            s = _sigmoid(av)
            da_ref[:, c0:c0 + cw] = (dhid * bv * (s * (1.0 + av * (1.0 - s)))).astype(BF16)
            db_ref[:, c0:c0 + cw] = (dhid * (av * s)).astype(BF16)

    hid = pl.BlockSpec((tm, D_FF), lambda i: (i, 0))
    row = pl.BlockSpec((tm, D_MODEL), lambda i: (i, 0))
    return _pcall(
        body, name=name, grid=(t_rows // tm,),
        in_specs=[row, hid, hid, _resident((D_FF, D_MODEL)), ANY_SPEC],
        out_specs=[hid, hid, row],
        out_shape=[jax.ShapeDtypeStruct((t_rows, D_FF), BF16), jax.ShapeDtypeStruct((t_rows, D_FF), BF16),
                   jax.ShapeDtypeStruct((t_rows, D_MODEL), BF16)],
        compiler_params=_cp(("arbitrary",), VMEM_BIG),
    )(dh, a, b, w2, after)


def ffn_backward_input(dh, h, norm, da, db, w1, w3, tm, name):
    t_rows = h.shape[0]

    def body(dh_ref, h_ref, g_ref, da_ref, db_ref, w1_ref, w3_ref, dhin_ref, dg_ref):
        dhn = _dot(da_ref[...], w1_ref[...]) + _dot(db_ref[...], w3_ref[...])
        dx, dg = _rms_bwd(h_ref[...], g_ref[...], dhn)
        dhin_ref[...] = dh_ref[...] + dx
        _accumulate(dg_ref, dg, pl.program_id(0) == 0)

    row = pl.BlockSpec((tm, D_MODEL), lambda i: (i, 0))
    vec = pl.BlockSpec((1, D_MODEL), lambda i: (0, 0))
    hid = pl.BlockSpec((tm, D_FF), lambda i: (i, 0))
    return _pcall(
        body, name=name, grid=(t_rows // tm,),
        in_specs=[row, row, vec, hid, hid, _resident((D_FF, D_MODEL)), _resident((D_FF, D_MODEL))],
        out_specs=[row, vec],
        out_shape=[jax.ShapeDtypeStruct((t_rows, D_MODEL), F32), jax.ShapeDtypeStruct((1, D_MODEL), F32)],
        compiler_params=_cp(("arbitrary",), VMEM_BIG),
    )(dh, h, norm, da, db, w1, w3)


def ffn_backward_weights(hn, dh, a, b, da, db, tm, tn, name):
    t_rows = hn.shape[0]
    ni = t_rows // tm
    kc = _row_tile(tm, 688)

    def body(hn_ref, dh_ref, a_ref, b_ref, da_ref, db_ref, dw1_ref, dw3_ref, dw2_ref, acc1, acc3, acc2):
        i = pl.program_id(1)
        parts = None
        for r0 in range(0, tm, kc):
            rows = slice(r0, r0 + kc)
            hn_v = hn_ref[rows, :]
            av = a_ref[rows, :].astype(F32)
            hid = (av * _sigmoid(av) * b_ref[rows, :].astype(F32)).astype(BF16)
            new = (_dot_tn(hn_v, da_ref[rows, :]), _dot_tn(hn_v, db_ref[rows, :]), _dot_tn(hid, dh_ref[rows, :]))
            parts = new if parts is None else tuple(p + q for p, q in zip(parts, new))
        _accumulate(acc1, parts[0], i == 0)
        _accumulate(acc3, parts[1], i == 0)
        _accumulate(acc2, parts[2], i == 0)

        @pl.when(i == ni - 1)
        def _():
            dw1_ref[...] = acc1[...].T.astype(BF16)
            dw3_ref[...] = acc3[...].T.astype(BF16)
            dw2_ref[...] = acc2[...].astype(BF16)

    row = pl.BlockSpec((tm, D_MODEL), lambda j, i: (i, 0))
    hid_blk = pl.BlockSpec((tm, tn), lambda j, i: (i, j))
    w_row = pl.BlockSpec((tn, D_MODEL), lambda j, i: (j, 0))
    out = jax.ShapeDtypeStruct((D_FF, D_MODEL), BF16)
    return _pcall(
        body, name=name, grid=(D_FF // tn, ni),
        in_specs=[row, row, hid_blk, hid_blk, hid_blk, hid_blk],
        out_specs=[w_row, w_row, w_row], out_shape=[out, out, out],
        scratch_shapes=[pltpu.VMEM((D_MODEL, tn), F32), pltpu.VMEM((D_MODEL, tn), F32), pltpu.VMEM((tn, D_MODEL), F32)],
        compiler_params=_cp(("arbitrary", "arbitrary"), VMEM_BIG),
    )(hn, dh, a, b, da, db)


def mix_forward(h, norm, wing, tm, name):
    t_rows = h.shape[0]

    def body(h_ref, g_ref, w_ref, hn_ref, p_ref):
        hn = _rms_fwd(h_ref[...], g_ref[...]).astype(BF16)
        hn_ref[...] = hn
        for j in range(N_DEV):
            p_ref[:, j * IN_BLK:(j + 1) * IN_BLK] = _dot(hn, w_ref[j]).astype(BF16)

    row = pl.BlockSpec((tm, D_MODEL), lambda i: (i, 0))
    return _pcall(
        body, name=name, grid=(t_rows // tm,),
        in_specs=[row, pl.BlockSpec((1, D_MODEL), lambda i: (0, 0)),
                  pl.BlockSpec((N_DEV, D_MODEL, IN_BLK), lambda i: (0, 0, 0))],
        out_specs=[row, pl.BlockSpec((tm, IN_WIDTH), lambda i: (i, 0))],
        out_shape=[jax.ShapeDtypeStruct((t_rows, D_MODEL), BF16), jax.ShapeDtypeStruct((t_rows, IN_WIDTH), BF16)],
        compiler_params=_cp(("arbitrary",), VMEM_BIG),
    )(h, norm, wing)


def mix_backward_act(dh, h, norm, dproj, wing, tm, name):
    t_rows = h.shape[0]
    per_step = 4
    nj = N_DEV // per_step

    def body(dh_ref, h_ref, g_ref, dp_ref, w_ref, dhin_ref, dg_ref, acc_ref):
        i, j = pl.program_id(0), pl.program_id(1)
        part = functools.reduce(
            lambda u, w: u + w, [_dot_nt(dp_ref[:, k * IN_BLK:(k + 1) * IN_BLK], w_ref[k]) for k in range(per_step)])
        _accumulate(acc_ref, part, j == 0)

        @pl.when(j == nj - 1)
        def _():
            dx, dg = _rms_bwd(h_ref[...], g_ref[...], acc_ref[...])
            dhin_ref[...] = dh_ref[...] + dx
            _accumulate(dg_ref, dg, i == 0)

    row = pl.BlockSpec((tm, D_MODEL), lambda i, j: (i, 0))
    vec = pl.BlockSpec((1, D_MODEL), lambda i, j: (0, 0))
    return _pcall(
        body, name=name, grid=(t_rows // tm, nj),
        in_specs=[row, row, vec, pl.BlockSpec((tm, per_step * IN_BLK), lambda i, j: (i, j)),
                  pl.BlockSpec((per_step, D_MODEL, IN_BLK), lambda i, j: (j, 0, 0))],
        out_specs=[row, vec],
        out_shape=[jax.ShapeDtypeStruct((t_rows, D_MODEL), F32), jax.ShapeDtypeStruct((1, D_MODEL), F32)],
        scratch_shapes=[pltpu.VMEM((tm, D_MODEL), F32)],
        compiler_params=_cp(("arbitrary", "arbitrary"), VMEM_BIG),
    )(dh, h, norm, dproj, wing)


def mix_backward_weights(hn, dproj, tm, name):
    t_rows = hn.shape[0]
    ni = t_rows // tm
    per_step = 2

    def body(hn_ref, dp_ref, dw_ref, acc):
        i = pl.program_id(1)
        _accumulate(acc, _dot_tn(hn_ref[...], dp_ref[...]), i == 0)

        @pl.when(i == ni - 1)
        def _():
            for k in range(per_step):
                dw_ref[k] = acc[:, k * IN_BLK:(k + 1) * IN_BLK].astype(BF16)

    return _pcall(
        body, name=name, grid=(N_DEV // per_step, ni),
        in_specs=[pl.BlockSpec((tm, D_MODEL), lambda j, i: (i, 0)),
                  pl.BlockSpec((tm, per_step * IN_BLK), lambda j, i: (i, j))],
        out_specs=pl.BlockSpec((per_step, D_MODEL, IN_BLK), lambda j, i: (j, 0, 0)),
        out_shape=jax.ShapeDtypeStruct((N_DEV, D_MODEL, IN_BLK), BF16),
        scratch_shapes=[pltpu.VMEM((D_MODEL, per_step * IN_BLK), F32)],
        compiler_params=_cp(("arbitrary", "arbitrary"), VMEM_BIG),
    )(hn, dproj)


GELU_C = 0.7978845608028654
GELU_K = 0.044715


def _gelu(x):
    return 0.5 * x * (1.0 + jnp.tanh(GELU_C * (x + GELU_K * (x * x * x))))


def _gelu_and_grad(x):
    th = jnp.tanh(GELU_C * (x + GELU_K * (x * x * x)))
    val = 0.5 * x * (1.0 + th)
    grad = 0.5 * (1.0 + th) + 0.5 * x * (1.0 - th * th) * (GELU_C * (1.0 + 3.0 * GELU_K * (x * x)))
    return val, grad


def merge_forward(h, yraw, attn, proj, glu_a, glu_b, w_out, tm, name):
    t_rows = h.shape[0]

    def body(h_ref, y_ref, at_ref, gate_ref, a_ref, b_ref, wo_ref, out_ref):
        y = _gelu(y_ref[...]).astype(BF16)
        ssm = _dot(y, a_ref[...]) * _sigmoid(_dot(y, b_ref[...]))
        ga = gate_ref[:, :D_MODEL].astype(F32)
        gs = gate_ref[:, D_MODEL:].astype(F32)
        merged = _sigmoid(ga) * at_ref[...].astype(F32) + _sigmoid(gs) * ssm
        out_ref[...] = h_ref[...] + _dot(merged.astype(BF16), wo_ref[...])

    row = pl.BlockSpec((tm, D_MODEL), lambda i: (i, 0))
    glu = pl.BlockSpec((SSM_WIDTH, D_MODEL), lambda i: (0, 0))
    return _pcall(
        body, name=name, grid=(t_rows // tm,),
        in_specs=[row, pl.BlockSpec((tm, SSM_WIDTH), lambda i: (i, 0)), row,
                  pl.BlockSpec((tm, 2 * D_MODEL), lambda i: (i, 1)), glu, glu,
                  pl.BlockSpec((D_MODEL, D_MODEL), lambda i: (0, 0))],
        out_specs=row, out_shape=jax.ShapeDtypeStruct((t_rows, D_MODEL), F32),
        compiler_params=_cp(("arbitrary",), VMEM_BIG),
    )(h, yraw, attn, proj, glu_a, glu_b, w_out)


def merge_backward(dh, yraw, attn, proj, glu_a, glu_b, w_out, after, tm, name):
    t_rows = dh.shape[0]

    def body(dh_ref, y_ref, at_ref, gate_ref, a_ref, b_ref, wo_ref, _,
             dat_ref, dy_ref, dgate_ref, d16_ref, mg_ref, y16_ref, dya_ref, dyb_ref):
        d16 = dh_ref[...].astype(BF16)
        d16_ref[...] = d16
        gel, dgel = _gelu_and_grad(y_ref[...].astype(F32))
        y16 = gel.astype(BF16)
        y16_ref[...] = y16
        dy = None
        for c0, cw in _col_chunks(D_MODEL):
            cols = slice(c0, c0 + cw)
            gcols = slice(D_MODEL + c0, D_MODEL + c0 + cw)
            dmerged = _dot_nt(d16, wo_ref[cols, :])
            ya = _dot(y16, a_ref[:, cols])
            sb = _sigmoid(_dot(y16, b_ref[:, cols]))
            ssm = ya * sb
            sa = _sigmoid(gate_ref[:, cols].astype(F32))
            ss = _sigmoid(gate_ref[:, gcols].astype(F32))
            attn_v = at_ref[:, cols].astype(F32)
            mg_ref[:, cols] = (sa * attn_v + ss * ssm).astype(BF16)
            dat_ref[:, cols] = (dmerged * sa).astype(BF16)
            dgate_ref[:, cols] = (dmerged * attn_v * sa * (1.0 - sa)).astype(BF16)
            dgate_ref[:, gcols] = (dmerged * ssm * ss * (1.0 - ss)).astype(BF16)
            dssm = dmerged * ss
            dya = (dssm * sb).astype(BF16)
            dyb = (dssm * ya * sb * (1.0 - sb)).astype(BF16)
            dya_ref[:, cols] = dya
            dyb_ref[:, cols] = dyb
            part = _dot_nt(dya, a_ref[:, cols]) + _dot_nt(dyb, b_ref[:, cols])
            dy = part if dy is None else dy + part
        dy_ref[...] = (dy * dgel).astype(BF16)

    row = pl.BlockSpec((tm, D_MODEL), lambda i: (i, 0))
    ssm_row = pl.BlockSpec((tm, SSM_WIDTH), lambda i: (i, 0))
    gates = pl.BlockSpec((tm, 2 * D_MODEL), lambda i: (i, 1))
    wide = jax.ShapeDtypeStruct((t_rows, D_MODEL), BF16)
    narrow = jax.ShapeDtypeStruct((t_rows, SSM_WIDTH), BF16)
    return _pcall(
        body, name=name, grid=(t_rows // tm,),
        in_specs=[row, ssm_row, row, gates, _resident((SSM_WIDTH, D_MODEL)), _resident((SSM_WIDTH, D_MODEL)),
                  _resident((D_MODEL, D_MODEL)), ANY_SPEC],
        out_specs=[row, ssm_row, gates, row, row, ssm_row, row, row],
        out_shape=[wide, narrow, jax.ShapeDtypeStruct((t_rows, IN_WIDTH), BF16), wide, wide, narrow, wide, wide],
        compiler_params=_cp(("arbitrary",), VMEM_BIG),
    )(dh, yraw, attn, proj, glu_a, glu_b, w_out, after)


def merge_backward_weights(d16, merged, y16, dya, dyb, tm, name):
    t_rows = d16.shape[0]

    def body(d_ref, mg_ref, y_ref, dya_ref, dyb_ref, dwo_ref, da_ref, db_ref):
        first = pl.program_id(0) == 0
        y16 = y_ref[...]
        _accumulate(dwo_ref, _dot_tn(mg_ref[...], d_ref[...]), first)
        _accumulate(da_ref, _dot_tn(y16, dya_ref[...]), first)
        _accumulate(db_ref, _dot_tn(y16, dyb_ref[...]), first)

    row = pl.BlockSpec((tm, D_MODEL), lambda i: (i, 0))
    ssm_row = pl.BlockSpec((tm, SSM_WIDTH), lambda i: (i, 0))
    glu = pl.BlockSpec((SSM_WIDTH, D_MODEL), lambda i: (0, 0))
    wo = pl.BlockSpec((D_MODEL, D_MODEL), lambda i: (0, 0))
    return _pcall(
        body, name=name, grid=(t_rows // tm,),
        in_specs=[row, row, ssm_row, row, row], out_specs=[wo, glu, glu],
        out_shape=[jax.ShapeDtypeStruct((D_MODEL, D_MODEL), F32), jax.ShapeDtypeStruct((SSM_WIDTH, D_MODEL), F32),
                   jax.ShapeDtypeStruct((SSM_WIDTH, D_MODEL), F32)],
        compiler_params=_cp(("arbitrary",), VMEM_BIG),
    )(d16, merged, y16, dya, dyb)


def final_loss_backward(h, target, norm, seq, tm, name):
    t_rows = h.shape[0]
    tiles_per_example = (seq + N_META) // tm

    def body(h_ref, t_ref, g_ref, dh_ref, loss_ref, dg_ref):
        i = pl.program_id(0)
        x = h_ref[...]
        g = g_ref[...]
        r = lax.rsqrt(jnp.mean(x * x, axis=-1, keepdims=True) + NORM_EPS)
        xh = x * r
        pos = lax.broadcasted_iota(jnp.int32, (tm, 1), 0) + (i % tiles_per_example) * tm
        diff = jnp.where(pos < seq, xh * g - t_ref[...], 0.0)
        part = 0.5 * jnp.sum(jnp.sum(diff * diff, axis=-1, keepdims=True), axis=0, keepdims=True) / D_MODEL
        dy = diff / D_MODEL
        t = dy * g
        dh_ref[...] = r * (t - xh * jnp.mean(t * xh, axis=-1, keepdims=True))
        _accumulate(loss_ref, jnp.broadcast_to(part, (1, LANES)), i == 0)
        _accumulate(dg_ref, jnp.sum(dy * xh, axis=0, keepdims=True), i == 0)

    row = pl.BlockSpec((tm, D_MODEL), lambda i: (i, 0))
    vec = pl.BlockSpec((1, D_MODEL), lambda i: (0, 0))
    return _pcall(
        body, name=name, grid=(t_rows // tm,),
        in_specs=[row, row, vec],
        out_specs=[row, pl.BlockSpec((1, LANES), lambda i: (0, 0)), vec],
        out_shape=[jax.ShapeDtypeStruct((t_rows, D_MODEL), F32), jax.ShapeDtypeStruct((1, LANES), F32),
                   jax.ShapeDtypeStruct((1, D_MODEL), F32)],
        compiler_params=_cp(("arbitrary",), VMEM_BIG),
    )(h, target, norm)


ATTN_SCALE = HEAD_DIM ** -0.5
STACK_HEADS = (0, 2, 1, 3)


def _lane_half(shape, hf):
    lane = lax.broadcasted_iota(jnp.int32, shape, 1)
    return (lane < HEAD_DIM) if hf == 0 else (lane >= HEAD_DIM)


def _kv_variants(ref, rows, kh):
    tile = kh // 2
    t = ref[rows, tile * LANES:(tile + 1) * LANES].astype(F32)
    swapped = pltpu.roll(t, HEAD_DIM, axis=1)
    at_low, at_high = (t, swapped) if kh % 2 == 0 else (swapped, t)
    lo = jnp.where(_lane_half(t.shape, 0), at_low, 0.0).astype(BF16)
    hi = jnp.where(_lane_half(t.shape, 1), at_high, 0.0).astype(BF16)
    return lo, hi


def _to_kv_lanes(lo, hi, kh):
    lo = jnp.where(_lane_half(lo.shape, 0), lo, 0.0)
    hi = jnp.where(_lane_half(hi.shape, 1), hi, 0.0)
    if kh % 2 == 0:
        return lo + pltpu.roll(hi, HEAD_DIM, axis=1)
    return pltpu.roll(lo, HEAD_DIM, axis=1) + hi


def _stacked(ref, rows, kh):
    col = kh * 2 * LANES
    return jnp.concatenate([ref[rows, col:col + LANES], ref[rows, col + LANES:col + 2 * LANES]], axis=0)


def _sink_column(sink_ref, kh, nq):
    row = lax.broadcasted_iota(jnp.int32, (4 * nq, 1), 0)
    col = jnp.zeros((4 * nq, 1), F32)
    for quarter, g in enumerate(STACK_HEADS):
        col = jnp.where(row // nq == quarter, sink_ref[0, kh * Q_PER_KV + g], col)
    return col


def _softmax_parts(qs, key_tiles, masks, sink):
    scores = []
    for (k_lo, k_hi), mask in zip(key_tiles, masks):
        s = jnp.concatenate([_dot_nt(qs, k_lo), _dot_nt(qs, k_hi)], axis=0) * ATTN_SCALE
        scores.append(s if mask is None else jnp.where(mask, s, NEG_INF))
    m = functools.reduce(jnp.maximum, [jnp.max(s, axis=-1, keepdims=True) for s in scores])
    m = jnp.maximum(m, sink)
    probs = [jnp.exp(s - m) for s in scores]
    e_sink = jnp.exp(sink - m)
    den = functools.reduce(lambda u, w: u + w, [jnp.sum(p, axis=-1, keepdims=True) for p in probs]) + e_sink
    return probs, 1.0 / den, e_sink


def _band_mask(nq, first):
    keys = BLOCK if first else 2 * BLOCK
    qi = lax.broadcasted_iota(jnp.int32, (4 * nq, keys), 0) % nq
    kj = lax.broadcasted_iota(jnp.int32, (4 * nq, keys), 1)
    if first:
        return kj <= qi
    return jnp.logical_and(kj > qi, kj <= qi + BLOCK)


def _meta_mask():
    qi = lax.broadcasted_iota(jnp.int32, (4 * N_META, N_META), 0) % N_META
    kj = lax.broadcasted_iota(jnp.int32, (4 * N_META, N_META), 1)
    return kj <= qi


def _attention_schedule(seq, queries, carry):
    meta_rows = pl.ds(seq, N_META)
    carry = queries(pl.ds(0, BLOCK), BLOCK, [pl.ds(0, BLOCK), meta_rows], [_band_mask(BLOCK, True), None], carry)

    def block(n, c):
        r0 = pl.multiple_of(n * BLOCK, BLOCK)
        p0 = pl.multiple_of((n - 1) * BLOCK, BLOCK)
        return queries(pl.ds(r0, BLOCK), BLOCK, [pl.ds(p0, 2 * BLOCK), meta_rows], [_band_mask(BLOCK, False), None], c)

    carry = lax.fori_loop(1, seq // BLOCK, block, carry)
    return queries(meta_rows, N_META, [meta_rows], [_meta_mask()], carry)


def attention_forward(proj3, sinks, seq, name):
    n_b, n_l, _ = proj3.shape

    def body(sink_ref, q_ref, k_ref, v_ref, o_ref):
        def queries(q_rows, nq, key_rows, masks, carry):
            for kh in range(N_KV_HEADS):
                ks = [_kv_variants(k_ref, r, kh) for r in key_rows]
                vs = [_kv_variants(v_ref, r, kh) for r in key_rows]
                qs = _stacked(q_ref, q_rows, kh)
                probs, inv, _ = _softmax_parts(qs, ks, masks, _sink_column(sink_ref, kh, nq))
                probs = [p.astype(BF16) for p in probs]
                o_lo = functools.reduce(lambda u, w: u + w, [_dot(p[:2 * nq], v_lo) for p, (v_lo, _) in zip(probs, vs)])
                o_hi = functools.reduce(lambda u, w: u + w, [_dot(p[2 * nq:], v_hi) for p, (_, v_hi) in zip(probs, vs)])
                out = (o_lo * inv[:2 * nq] + o_hi * inv[2 * nq:]).astype(BF16)
                col = kh * 2 * LANES
                o_ref[q_rows, col:col + LANES] = out[:nq]
                o_ref[q_rows, col + LANES:col + 2 * LANES] = out[nq:]
            return carry

        _attention_schedule(seq, queries, 0)

    return _pcall(
        body, name=name, grid=(n_b,),
        in_specs=[pl.BlockSpec(memory_space=pltpu.SMEM),
                  pl.BlockSpec((None, n_l, D_MODEL), lambda b: (b, 0, 0)),
                  pl.BlockSpec((None, n_l, KV_WIDTH), lambda b: (b, 0, D_MODEL // KV_WIDTH)),
                  pl.BlockSpec((None, n_l, KV_WIDTH), lambda b: (b, 0, D_MODEL // KV_WIDTH + 1))],
        out_specs=pl.BlockSpec((None, n_l, D_MODEL), lambda b: (b, 0, 0)),
        out_shape=jax.ShapeDtypeStruct((n_b, n_l, D_MODEL), BF16),
        compiler_params=_cp(("arbitrary",), VMEM_BIG),
    )(sinks, proj3, proj3, proj3)


def attention_backward(proj3, dattn3, dproj3, sinks, after, seq, name):
    n_b, n_l, _ = proj3.shape
    qkv_width = D_MODEL + 2 * KV_WIDTH

    def body(sink_ref, q_ref, k_ref, v_ref, do_ref, _, __, dqkv_ref, dsink_ref, dk_ref, dv_ref):
        dk_ref[...] = jnp.zeros_like(dk_ref)
        dv_ref[...] = jnp.zeros_like(dv_ref)
        sub = lax.broadcasted_iota(jnp.int32, (SUBLANES, LANES), 0)
        lane = lax.broadcasted_iota(jnp.int32, (SUBLANES, LANES), 1)

        def queries(q_rows, nq, key_rows, masks, dsink):
            for kh in range(N_KV_HEADS):
                ks = [_kv_variants(k_ref, r, kh) for r in key_rows]
                vs = [_kv_variants(v_ref, r, kh) for r in key_rows]
                qs = _stacked(q_ref, q_rows, kh)
                dos = _stacked(do_ref, q_rows, kh)
                probs, inv, e_sink = _softmax_parts(qs, ks, masks, _sink_column(sink_ref, kh, nq))
                probs = [p * inv for p in probs]
                dps = [jnp.concatenate([_dot_nt(dos, v_lo), _dot_nt(dos, v_hi)], axis=0) for v_lo, v_hi in vs]
                delta = functools.reduce(
                    lambda u, w: u + w, [jnp.sum(p * dp, axis=-1, keepdims=True) for p, dp in zip(probs, dps)])
                d_sink = -(e_sink * inv) * delta
                for quarter, g in enumerate(STACK_HEADS):
                    d_here = jnp.sum(d_sink[quarter * nq:(quarter + 1) * nq], axis=0, keepdims=True)
                    dsink = dsink + jnp.where(jnp.logical_and(sub == 0, lane == kh * Q_PER_KV + g), d_here, 0.0)
                dq = None
                tile = slice((kh // 2) * LANES, (kh // 2 + 1) * LANES)
                for r, p, dp, (k_lo, k_hi) in zip(key_rows, probs, dps, ks):
                    ds = (p * (dp - delta)).astype(BF16)
                    p16 = p.astype(BF16)
                    dq_x = _dot(ds[:2 * nq], k_lo) + _dot(ds[2 * nq:], k_hi)
                    dq = dq_x if dq is None else dq + dq_x
                    dk_ref[r, tile] += _to_kv_lanes(_dot_tn(ds[:2 * nq], qs), _dot_tn(ds[2 * nq:], qs), kh) * ATTN_SCALE
                    dv_ref[r, tile] += _to_kv_lanes(_dot_tn(p16[:2 * nq], dos), _dot_tn(p16[2 * nq:], dos), kh)
                dq = (dq * ATTN_SCALE).astype(BF16)
                col = kh * 2 * LANES
                dqkv_ref[q_rows, col:col + LANES] = dq[:nq]
                dqkv_ref[q_rows, col + LANES:col + 2 * LANES] = dq[nq:]
            return dsink

        dsink_ref[...] = _attention_schedule(seq, queries, jnp.zeros((SUBLANES, LANES), F32))
        dqkv_ref[:, D_MODEL:D_MODEL + KV_WIDTH] = dk_ref[...].astype(BF16)
        dqkv_ref[:, D_MODEL + KV_WIDTH:] = dv_ref[...].astype(BF16)

    return _pcall(
        body, name=name, grid=(n_b,),
        in_specs=[pl.BlockSpec(memory_space=pltpu.SMEM),
                  pl.BlockSpec((None, n_l, D_MODEL), lambda b: (b, 0, 0)),
                  pl.BlockSpec((None, n_l, KV_WIDTH), lambda b: (b, 0, D_MODEL // KV_WIDTH)),
                  pl.BlockSpec((None, n_l, KV_WIDTH), lambda b: (b, 0, D_MODEL // KV_WIDTH + 1)),
                  pl.BlockSpec((None, n_l, D_MODEL), lambda b: (b, 0, 0)),
                  ANY_SPEC, ANY_SPEC],
        out_specs=[pl.BlockSpec((None, n_l, qkv_width), lambda b: (b, 0, 0)),
                   pl.BlockSpec((None, SUBLANES, LANES), lambda b: (b, 0, 0))],
        out_shape=[jax.ShapeDtypeStruct(dproj3.shape, BF16), jax.ShapeDtypeStruct((n_b, SUBLANES, LANES), F32)],
        scratch_shapes=[pltpu.VMEM((n_l, KV_WIDTH), F32), pltpu.VMEM((n_l, KV_WIDTH), F32)],
        input_output_aliases={5: 0},
        compiler_params=_cp(("arbitrary",), VMEM_BIG),
    )(sinks, proj3, proj3, proj3, dattn3, dproj3, after)


TAB_ROWS = 8
SCAN_UNROLL = 4


def _cmul(ar, ai, br, bi):
    return ar * br - ai * bi, ar * bi + ai * br


def _discretise(ar, ai, ls):
    step = jnp.exp(ls)
    mag = jnp.exp(ar * step)
    ang = ai * step
    cos, sin = jnp.cos(ang), jnp.sin(ang)
    lr, li = mag * cos, mag * sin
    den = ar * ar + ai * ai
    nr, ni = lr - 1.0, li
    cr = (nr * ar + ni * ai) / den
    ci = (ni * ar - nr * ai) / den
    return step, mag, lr, li, den, nr, ni, cr, ci


def _scan_tables(lr, li, reverse):
    n = lr.shape[-1]
    pw = [(lr, li)]
    for _ in range(SUBLANES - 1):
        pw.append(_cmul(pw[-1][0], pw[-1][1], lr, li))
    row = lax.broadcasted_iota(jnp.int32, (SUBLANES, n), 0)
    out = []
    for d in (1, 2, 4):
        ok = (row + d <= SUBLANES - 1) if reverse else (row >= d)
        out += [jnp.where(ok, pw[d - 1][0], 0.0), jnp.where(ok, pw[d - 1][1], 0.0)]
    cr = jnp.zeros((SUBLANES, n), F32)
    ci = jnp.zeros((SUBLANES, n), F32)
    for r in range(SUBLANES):
        e = (SUBLANES - r) if reverse else (r + 1)
        cr = jnp.where(row == r, pw[e - 1][0], cr)
        ci = jnp.where(row == r, pw[e - 1][1], ci)
    return out + [cr, ci]


def ssm_prepare(ar, ai, ls, br_t, bi_t, name):
    def body(ar_ref, ai_ref, ls_ref, br_ref, bi_ref, bbr_ref, bbi_ref, tf_ref, tr_ref):
        _, _, lr, li, _, _, _, cr, ci = _discretise(ar_ref[...], ai_ref[...], ls_ref[...])
        br, bi = br_ref[...], bi_ref[...]
        bbr_ref[...] = cr * br - ci * bi
        bbi_ref[...] = cr * bi + ci * br
        for k, t in enumerate(_scan_tables(lr, li, False)):
            tf_ref[k] = t
        for k, t in enumerate(_scan_tables(lr, -li, True)):
            tr_ref[k] = t

    return _pcall(
        body, name=name,
        out_shape=[jax.ShapeDtypeStruct((SSM_GROUP, N_STATES), F32), jax.ShapeDtypeStruct((SSM_GROUP, N_STATES), F32),
                   jax.ShapeDtypeStruct((TAB_ROWS, SUBLANES, N_STATES), F32),
                   jax.ShapeDtypeStruct((TAB_ROWS, SUBLANES, N_STATES), F32)],
    )(ar, ai, ls, br_t, bi_t)


def ssm_param_backward(ar, ai, ls, br_t, bi_t, dlr_p, dli_p, dbbr, dbbi, group_sum, name):
    def body(ar_ref, ai_ref, ls_ref, br_ref, bi_ref, dlr_ref, dli_ref, dbbr_ref, dbbi_ref, gs_ref,
             dar_ref, dai_ref, dls_ref, dbr_ref, dbi_ref):
        ar, ai = ar_ref[...], ai_ref[...]
        step, mag, lr, li, den, nr, ni, cr, ci = _discretise(ar, ai, ls_ref[...])
        br, bi, dbbr_v, dbbi_v = br_ref[...], bi_ref[...], dbbr_ref[...], dbbi_ref[...]
        dbr_ref[...] = cr * dbbr_v + ci * dbbi_v
        dbi_ref[...] = cr * dbbi_v - ci * dbbr_v
        dcr = jnp.sum(dbbr_v * br + dbbi_v * bi, axis=0, keepdims=True)
        dci = jnp.sum(dbbi_v * br - dbbr_v * bi, axis=0, keepdims=True)
        dnr = (dcr * ar - dci * ai) / den
        dni = (dcr * ai + dci * ar) / den
        dden = -(cr * dcr + ci * dci) / den
        dar = (dcr * nr + dci * ni) / den + dden * 2.0 * ar
        dai = (dcr * ni - dci * nr) / den + dden * 2.0 * ai
        dlr = jnp.sum(dlr_ref[...], axis=0, keepdims=True) + dnr
        dli = jnp.sum(dli_ref[...], axis=0, keepdims=True) + dni
        dmag = (dlr * lr + dli * li) / mag
        dang = dli * lr - dlr * li
        dar_ref[...] = dar + dmag * mag * step
        dai_ref[...] = dai + dang * step
        dstep = dmag * mag * ar + dang * ai
        dls_ref[...] = jnp.dot(dstep * step, gs_ref[...], preferred_element_type=F32, precision=lax.Precision.HIGHEST)

    vec = jax.ShapeDtypeStruct((1, N_STATES), F32)
    mat = jax.ShapeDtypeStruct((SSM_GROUP, N_STATES), F32)
    return _pcall(body, name=name, out_shape=[vec, vec, jax.ShapeDtypeStruct((1, LANES), F32), mat, mat])(
        ar, ai, ls, br_t, bi_t, dlr_p, dli_p, dbbr, dbbi, group_sum)


def _scan_rows(a, b, tabs, carry, reverse):
    for k, d in enumerate((1, 2, 4)):
        shift = SUBLANES - d if reverse else d
        sr, si = pltpu.roll(a, shift, axis=0), pltpu.roll(b, shift, axis=0)
        pr, pi = _cmul(tabs[2 * k], tabs[2 * k + 1], sr, si)
        a, b = a + pr, b + pi
    pr, pi = _cmul(tabs[6], tabs[7], carry[0], carry[1])
    return a + pr, b + pi


def _time_groups(seq, reverse):
    meta = [seq + SUBLANES * g for g in range(N_META // SUBLANES)]
    return meta[::-1] if reverse else meta


def ssm_forward_scan(proj3, b_comb, tabf, c_comb, dvec, seq, name):
    n_b, n_l, _ = proj3.shape
    u_blk = (D_MODEL + 2 * KV_WIDTH) // LANES

    def body(u_ref, b_ref, tab_ref, c_ref, d_ref, x_ref, y_ref, bu, xs):
        j = pl.program_id(1)
        u = u_ref[...]
        bu[...] = _dot(u, b_ref[...])
        tabs = [tab_ref[k] for k in range(TAB_ROWS)]

        def group(r0, carry):
            rows = pl.ds(r0, SUBLANES)
            a, b = _scan_rows(bu[rows, :SCAN_COLS], bu[rows, SCAN_COLS:], tabs, carry, False)
            xs[rows, :SCAN_COLS] = a
            xs[rows, SCAN_COLS:] = b
            return (jnp.broadcast_to(a[SUBLANES - 1:, :], a.shape), jnp.broadcast_to(b[SUBLANES - 1:, :], b.shape))

        zero = jnp.zeros((SUBLANES, SCAN_COLS), F32)
        carry = (zero, zero)
        for r0 in _time_groups(seq, False):
            carry = group(r0, carry)
        span = SCAN_UNROLL * SUBLANES

        def groups(t, c):
            for k in range(SCAN_UNROLL):
                c = group(pl.multiple_of(t * span, span) + k * SUBLANES, c)
            return c

        lax.fori_loop(0, seq // span, groups, carry)
        x16 = xs[...].astype(BF16)
        x_ref[...] = x16
        contrib = _dot(x16, c_ref[...])

        @pl.when(j % 2 == 0)
        def _():
            y_ref[...] = contrib + d_ref[...] * u.astype(F32)

        @pl.when(j % 2 == 1)
        def _():
            y_ref[...] += contrib

    return _pcall(
        body, name=name, grid=(n_b, N_SCAN_BLK),
        in_specs=[pl.BlockSpec((None, n_l, LANES), lambda b, j: (b, 0, u_blk + j // 2)),
                  pl.BlockSpec((LANES, 2 * SCAN_COLS), lambda b, j: (j // 2, j)),
                  pl.BlockSpec((TAB_ROWS, SUBLANES, SCAN_COLS), lambda b, j: (0, 0, j)),
                  pl.BlockSpec((2 * SCAN_COLS, LANES), lambda b, j: (j, j // 2)),
                  pl.BlockSpec((1, LANES), lambda b, j: (0, j // 2))],
        out_specs=[pl.BlockSpec((None, n_l, 2 * SCAN_COLS), lambda b, j: (b, 0, j)),
                   pl.BlockSpec((None, n_l, LANES), lambda b, j: (b, 0, j // 2))],
        out_shape=[jax.ShapeDtypeStruct((n_b, n_l, 2 * N_STATES), BF16),
                   jax.ShapeDtypeStruct((n_b, n_l, SSM_WIDTH), F32)],
        scratch_shapes=[pltpu.VMEM((n_l, 2 * SCAN_COLS), F32)] * 2,
        compiler_params=_cp(("arbitrary", "arbitrary"), VMEM_BIG),
    )(proj3, b_comb, tabf, c_comb, dvec)


def ssm_backward_scan(dyraw3, xs3, dproj3, c_comb_t, tabr, b_comb_t, dvec, seq, name):
    n_b, n_l, _ = xs3.shape
    u_blk = (D_MODEL + 2 * KV_WIDTH) // LANES

    def body(dy_ref, x_ref, _, c_ref, tab_ref, b_ref, d_ref, du_ref, g_ref, dlr_ref, dli_ref, dx, gs, xs, du_acc):
        j = pl.program_id(1)
        dy = dy_ref[...]
        dx[...] = _dot(dy, c_ref[...])
        xs[...] = x_ref[...].astype(F32)
        tabs = [tab_ref[k] for k in range(TAB_ROWS)]
        last_row = lax.broadcasted_iota(jnp.int32, (SUBLANES, SCAN_COLS), 0) == SUBLANES - 1

        def group(r0, state):
            cr, ci, acc_r, acc_i = state
            rows = pl.ds(r0, SUBLANES)
            a, b = _scan_rows(dx[rows, :SCAN_COLS], dx[rows, SCAN_COLS:], tabs, (cr, ci), True)
            gs[rows, :SCAN_COLS] = a
            gs[rows, SCAN_COLS:] = b
            na = jnp.where(last_row, cr, pltpu.roll(a, SUBLANES - 1, axis=0))
            nb = jnp.where(last_row, ci, pltpu.roll(b, SUBLANES - 1, axis=0))
            xa, xb = xs[rows, :SCAN_COLS], xs[rows, SCAN_COLS:]
            return (jnp.broadcast_to(a[:1, :], a.shape), jnp.broadcast_to(b[:1, :], b.shape),
                    acc_r + na * xa + nb * xb, acc_i + nb * xa - na * xb)

        zero = jnp.zeros((SUBLANES, SCAN_COLS), F32)
        span = SCAN_UNROLL * SUBLANES
        n_spans = seq // span

        def groups(t, s):
            for k in reversed(range(SCAN_UNROLL)):
                s = group(pl.multiple_of((n_spans - 1 - t) * span, span) + k * SUBLANES, s)
            return s

        state = lax.fori_loop(0, n_spans, groups, (zero, zero, zero, zero))
        for r0 in _time_groups(seq, True):
            state = group(r0, state)
        dlr_ref[...] = state[2]
        dli_ref[...] = state[3]
        g16 = gs[...].astype(BF16)
        g_ref[...] = g16
        contrib = _dot(g16, b_ref[...])

        @pl.when(j % 2 == 0)
        def _():
            du_acc[...] = contrib + d_ref[...] * dy.astype(F32)

        @pl.when(j % 2 == 1)
        def _():
            du_ref[...] = (du_acc[...] + contrib).astype(BF16)

    state_blk = pl.BlockSpec((None, n_l, 2 * SCAN_COLS), lambda b, j: (b, 0, j))
    dl_blk = pl.BlockSpec((None, SUBLANES, SCAN_COLS), lambda b, j: (b, 0, j))
    return _pcall(
        body, name=name, grid=(n_b, N_SCAN_BLK),
        in_specs=[pl.BlockSpec((None, n_l, LANES), lambda b, j: (b, 0, j // 2)), state_blk,
                  pl.BlockSpec(memory_space=pl.ANY),
                  pl.BlockSpec((LANES, 2 * SCAN_COLS), lambda b, j: (j // 2, j)),
                  pl.BlockSpec((TAB_ROWS, SUBLANES, SCAN_COLS), lambda b, j: (0, 0, j)),
                  pl.BlockSpec((2 * SCAN_COLS, LANES), lambda b, j: (j, j // 2)),
                  pl.BlockSpec((1, LANES), lambda b, j: (0, j // 2))],
        out_specs=[pl.BlockSpec((None, n_l, LANES), lambda b, j: (b, 0, u_blk + j // 2)), state_blk, dl_blk, dl_blk],
        out_shape=[jax.ShapeDtypeStruct(dproj3.shape, BF16), jax.ShapeDtypeStruct((n_b, n_l, 2 * N_STATES), BF16),
                   jax.ShapeDtypeStruct((n_b, SUBLANES, N_STATES), F32), jax.ShapeDtypeStruct((n_b, SUBLANES, N_STATES), F32)],
        scratch_shapes=[pltpu.VMEM((n_l, 2 * SCAN_COLS), F32)] * 3 + [pltpu.VMEM((n_l, LANES), F32)],
        input_output_aliases={2: 0},
        compiler_params=_cp(("arbitrary", "arbitrary"), VMEM_BIG),
    )(dyraw3, xs3, dproj3, c_comb_t, tabr, b_comb_t, dvec)


def ssm_param_grads(proj, gs, xs, dyraw, tm, name):
    t_rows = proj.shape[0]
    ni = t_rows // tm
    u_blk = (D_MODEL + 2 * KV_WIDTH) // LANES
    width = 2 * SCAN_COLS

    def body(u_ref, g_ref, x_ref, dy_ref, db_ref, dc_ref, dd_ref):
        cb, i = pl.program_id(0), pl.program_id(1)
        u, dy = u_ref[...], dy_ref[...]
        _accumulate(db_ref, _dot_tn(u, g_ref[...]), i == 0)
        _accumulate(dc_ref, _dot_tn(x_ref[...], dy), i == 0)

        @pl.when(cb % 2 == 0)
        def _():
            _accumulate(dd_ref, jnp.sum(dy.astype(F32) * u.astype(F32), axis=0, keepdims=True), i == 0)

    return _pcall(
        body, name=name, grid=(N_SCAN_BLK, ni),
        in_specs=[pl.BlockSpec((tm, LANES), lambda cb, i: (i, u_blk + cb // 2)),
                  pl.BlockSpec((tm, width), lambda cb, i: (i, cb)),
                  pl.BlockSpec((tm, width), lambda cb, i: (i, cb)),
                  pl.BlockSpec((tm, LANES), lambda cb, i: (i, cb // 2))],
        out_specs=[pl.BlockSpec((None, LANES, width), lambda cb, i: (cb, 0, 0)),
                   pl.BlockSpec((None, width, LANES), lambda cb, i: (cb, 0, 0)),
                   pl.BlockSpec((1, LANES), lambda cb, i: (0, cb // 2))],
        out_shape=[jax.ShapeDtypeStruct((N_SCAN_BLK, LANES, width), F32),
                   jax.ShapeDtypeStruct((N_SCAN_BLK, width, LANES), F32), jax.ShapeDtypeStruct((1, SSM_WIDTH), F32)],
        compiler_params=_cp(("arbitrary", "arbitrary"), VMEM_BIG),
    )(proj, gs, xs, dyraw)


def sum_leading(x, name):
    def body(x_ref, o_ref):
        acc = x_ref[0]
        for k in range(1, x.shape[0]):
            acc = acc + x_ref[k]
        o_ref[...] = acc

    return _pcall(body, name=name, out_shape=jax.ShapeDtypeStruct(x.shape[1:], x.dtype))(x)


WEIGHTS = ['meta_tokens', 'ffn1_norm', 'ffn1_w1', 'ffn1_w3', 'ffn1_w2', 'mix_norm', 'w_in', 'attn_sinks', 'ssm_a_re',
           'ssm_a_im', 'ssm_log_step', 'ssm_b_re', 'ssm_b_im', 'ssm_c_re', 'ssm_c_im', 'ssm_d', 'ssm_glu_a', 'ssm_glu_b',
           'w_out', 'ffn2_norm', 'ffn2_w1', 'ffn2_w3', 'ffn2_w2', 'final_norm']
SHARDED = ['ffn1_w1', 'ffn1_w3', 'ffn1_w2', 'ffn2_w1', 'ffn2_w3', 'ffn2_w2', 'w_in', 'ssm_glu_a', 'ssm_glu_b', 'w_out']
REPLICATED = ['ffn1_norm', 'mix_norm', 'ffn2_norm', 'final_norm', 'attn_sinks', 'ssm_a_re', 'ssm_a_im', 'ssm_log_step',
              'ssm_b_re', 'ssm_b_im', 'ssm_c_re', 'ssm_c_im', 'ssm_d']
PACK_COLS = 1024


def _block_diag(blocks):
    g, r, c = blocks.shape
    eye = jnp.eye(g, dtype=blocks.dtype)
    return (blocks[:, :, None, :] * eye[:, None, :, None]).reshape(g * r, g * c)


def _diag_blocks(mat, r, c):
    g = SSM_GROUPS
    eye = jnp.eye(g, dtype=mat.dtype)
    return jnp.sum(mat.reshape(g, r, g, c) * eye[:, None, :, None], axis=2)


def _scan_order(re, im):
    r = re.shape[0]
    return jnp.stack([re.reshape(r, N_SCAN_BLK, SCAN_COLS), im.reshape(r, N_SCAN_BLK, SCAN_COLS)], axis=2).reshape(r, 2 * N_STATES)


def _from_scan_order(comb):
    r = comb.shape[0]
    c4 = comb.reshape(r, N_SCAN_BLK, 2, SCAN_COLS)
    return c4[:, :, 0].reshape(r, N_STATES), c4[:, :, 1].reshape(r, N_STATES)


def _pack(arrays):
    parts = []
    for a in arrays:
        flat = a.reshape(-1)
        chunk = SUBLANES * PACK_COLS
        padded = -(-flat.shape[0] // chunk) * chunk
        parts.append(jnp.pad(flat, (0, padded - flat.shape[0])).reshape(-1, PACK_COLS))
    return jnp.concatenate(parts, axis=0)


def _unpack(packed, shapes):
    out, row = [], 0
    for shape in shapes:
        size = 1
        for s in shape:
            size *= s
        chunk = SUBLANES * PACK_COLS
        rows = -(-size // chunk) * SUBLANES
        out.append(packed[row:row + rows].reshape(-1)[:size].reshape(shape))
        row += rows
    return out


def kernel(x, meta_tokens, ffn1_norm, ffn1_w1, ffn1_w3, ffn1_w2, mix_norm, w_in, attn_sinks, ssm_a_re, ssm_a_im, ssm_log_step, ssm_b_re, ssm_b_im, ssm_c_re, ssm_c_im, ssm_d, ssm_glu_a, ssm_glu_b, w_out, ffn2_norm, ffn2_w1, ffn2_w3, ffn2_w2, final_norm, loss_target, m_meta_tokens, m_ffn1_norm, m_ffn1_w1, m_ffn1_w3, m_ffn1_w2, m_mix_norm, m_w_in, m_attn_sinks, m_ssm_a_re, m_ssm_a_im, m_ssm_log_step, m_ssm_b_re, m_ssm_b_im, m_ssm_c_re, m_ssm_c_im, m_ssm_d, m_ssm_glu_a, m_ssm_glu_b, m_w_out, m_ffn2_norm, m_ffn2_w1, m_ffn2_w3, m_ffn2_w2, m_final_norm, v_meta_tokens, v_ffn1_norm, v_ffn1_w1, v_ffn1_w3, v_ffn1_w2, v_mix_norm, v_w_in, v_attn_sinks, v_ssm_a_re, v_ssm_a_im, v_ssm_log_step, v_ssm_b_re, v_ssm_b_im, v_ssm_c_re, v_ssm_c_im, v_ssm_d, v_ssm_glu_a, v_ssm_glu_b, v_w_out, v_ffn2_norm, v_ffn2_w1, v_ffn2_w3, v_ffn2_w2, v_final_norm):
    given = dict(locals())
    w = {n: given[n] for n in WEIGHTS}
    m = {n: given["m_" + n] for n in WEIGHTS}
    v = {n: given["v_" + n] for n in WEIGHTS}

    n_b, seq, _ = x.shape
    n_l = seq + N_META
    t_rows = n_b * n_l
    tm = _row_tile(n_l, 688)
    px, py, pc = _my_place()
    me = 4 * px + 2 * py + pc
    dest = jnp.stack([4 * qx + 2 * qy + pc for qx, qy in
                      [(px, py), (1 - px, py), (px, 1 - py), (1 - px, 1 - py)]]).astype(jnp.int32)

    glu = jnp.stack([ssm_glu_a[0], ssm_glu_b[0]]).astype(BF16)
    ffn_names = ['ffn1_w1', 'ffn1_w3', 'ffn1_w2', 'ffn2_w1', 'ffn2_w3', 'ffn2_w2']

    def hidden_on_rows(n, t):
        return t[0] if n.endswith('w2') else t[0].T

    def hidden_on_rows_back(n, t):
        return t[None] if n.endswith('w2') else t.T[None]

    me_idx = jnp.reshape(me, (1,)).astype(jnp.int32)
    first_names, later_names = ffn_names[:3], ffn_names[3:]
    *first, wing, metag = all_gather_list(
        [hidden_on_rows(n, w[n]).astype(BF16) for n in first_names] + [w_in[0].astype(BF16), meta_tokens], "ag_first")
    later_shards = [hidden_on_rows(n, w[n]).astype(BF16) for n in later_names] + [glu, w_out[0].astype(BF16)]
    ag_send, ag_recv, later_shards, later_lands, ag_token = exchange_start(later_shards, wing, True, "ag_later_start")
    full = {n: g.reshape(D_FF, D_MODEL) for n, g in zip(first_names, first)}
    meta_full = metag.transpose(1, 0, 2).reshape(N_META, D_MODEL)

    h0 = jnp.concatenate([x, jnp.broadcast_to(meta_full[None], (n_b, N_META, D_MODEL))], axis=1).reshape(t_rows, D_MODEL)
    target = jnp.concatenate([loss_target, jnp.zeros((n_b, N_META, D_MODEL), F32)], axis=1).reshape(t_rows, D_MODEL)
    final_g = final_norm.reshape(1, D_MODEL)

    ar = ssm_a_re.reshape(1, N_STATES)
    ai = ssm_a_im.reshape(1, N_STATES)
    ls = jnp.repeat(ssm_log_step.reshape(SSM_GROUPS), SSM_STATE).reshape(1, N_STATES)
    br_t = ssm_b_re[0].transpose(2, 0, 1).reshape(SSM_GROUP, N_STATES)
    bi_t = ssm_b_im[0].transpose(2, 0, 1).reshape(SSM_GROUP, N_STATES)
    bbr, bbi, tabf, tabr = ssm_prepare(ar, ai, ls, br_t, bi_t, "ssm_prepare")
    bbr_g = bbr.reshape(SSM_GROUP, SSM_GROUPS, SSM_STATE).transpose(1, 0, 2)
    bbi_g = bbi.reshape(SSM_GROUP, SSM_GROUPS, SSM_STATE).transpose(1, 0, 2)
    b_comb = _scan_order(_block_diag(bbr_g), _block_diag(bbi_g)).astype(BF16)
    c_comb_t = _scan_order(_block_diag(ssm_c_re[0]), -_block_diag(ssm_c_im[0])).astype(BF16)
    b_comb_t, c_comb = b_comb.T, c_comb_t.T

    ffn1_w = (full['ffn1_w1'], full['ffn1_w3'], full['ffn1_w2'])
    h1, hn1, a1, b1 = ffn_forward(h0, ffn1_norm, *ffn1_w, ag_token, tm, FF_FWD_COLS, "ffn1_fwd")
    hnm, proj = mix_forward(h1, mix_norm, wing, tm, "mix_fwd")
    proj3 = proj.reshape(n_b, n_l, IN_WIDTH)
    attn3 = attention_forward(proj3, attn_sinks, seq, "attn_fwd")
    attn = attn3.reshape(t_rows, D_MODEL)
    xs3, yraw3 = ssm_forward_scan(proj3, b_comb, tabf, c_comb, ssm_d, seq, "ssm_fwd")
    yraw = yraw3.reshape(t_rows, SSM_WIDTH)
    later = exchange_wait(ag_send, ag_recv, later_shards, later_lands, yraw3, True, "ag_later_wait")
    later = [lax.dynamic_update_slice_in_dim(z, s[None], me, axis=0) for z, s in zip(later, later_shards)]
    for n, g in zip(later_names, later):
        full[n] = g.reshape(D_FF, D_MODEL)
    ffn2_w = (full['ffn2_w1'], full['ffn2_w3'], full['ffn2_w2'])
    glug, wog = later[len(later_names):]
    glu_a = glug[:, 0].transpose(1, 0, 2).reshape(SSM_WIDTH, D_MODEL)
    glu_b = glug[:, 1].transpose(1, 0, 2).reshape(SSM_WIDTH, D_MODEL)
    w_out_full = wog.reshape(D_MODEL, D_MODEL)
    h2 = merge_forward(h1, yraw, attn, proj, glu_a, glu_b, w_out_full, tm, "merge_fwd")
    h3, hn2, a2, b2 = ffn_forward(h2, ffn2_norm, *ffn2_w, ag_token, tm, FF_FWD_COLS, "ffn2_fwd")
    dh3, loss_part, g_final = final_loss_backward(h3, target, final_g, seq, tm, "loss_bwd")
    loss = lax.psum(loss_part[0, 0], ("x", "y", "c"))

    def blocked_ffn(d_w1t, d_w3t, d_w2):
        return tuple(t.reshape(N_DEV, FF_BLK, D_MODEL) for t in (d_w1t, d_w3t, d_w2))

    da2, db2, dh3_half = ffn_backward_hidden(dh3, a2, b2, ffn2_w[2], g_final, tm, "ffn2_bwd_hid")
    dh2, g_ffn2_norm = ffn_backward_input(dh3, h2, ffn2_norm, da2, db2, ffn2_w[0], ffn2_w[1], tm, "ffn2_bwd_in")
    dw = {}
    dw['ffn2_w1'], dw['ffn2_w3'], dw['ffn2_w2'] = blocked_ffn(
        *ffn_backward_weights(hn2, dh3_half, a2, b2, da2, db2, n_l, FF_BWD_COLS, "ffn2_bwd_w"))

    def blocked_cols(full_grad):
        r = full_grad.shape[0]
        return full_grad.reshape(r, N_DEV, full_grad.shape[1] // N_DEV).transpose(1, 0, 2).astype(BF16)

    early = {}

    def start_reduce(names, tag):
        srcs = [dw[n] for n in names]
        send, recv, srcs, lands, token = exchange_start(srcs, srcs[0], False, "rs_" + tag + "_start")
        early[tag] = (names, send, recv, srcs, lands)
        return token

    token = start_reduce(['ffn2_w1', 'ffn2_w3', 'ffn2_w2'], "ffn2")
    dattn, dyraw, dproj, *for_weights = merge_backward(dh2, yraw, attn, proj, glu_a, glu_b, w_out_full, token, tm,
                                                       "merge_bwd")
    d_wo, d_ga, d_gb = merge_backward_weights(*for_weights, tm, "merge_bwd_w")
    dw['ssm_glu_a'] = blocked_cols(d_ga)
    dw['ssm_glu_b'] = blocked_cols(d_gb)
    dw['w_out'] = d_wo.reshape(N_DEV, D_MODEL // N_DEV, D_MODEL).astype(BF16)
    token = start_reduce(['ssm_glu_a', 'ssm_glu_b', 'w_out'], "mix")
    dproj3 = dproj.reshape(n_b, n_l, IN_WIDTH)
    dproj3, dsink_p = attention_backward(proj3, dattn.reshape(n_b, n_l, D_MODEL), dproj3, attn_sinks, token, seq,
                                         "attn_bwd")
    dproj3, gs3, dlr_p, dli_p = ssm_backward_scan(
        dyraw.reshape(n_b, n_l, SSM_WIDTH), xs3, dproj3, c_comb_t, tabr, b_comb_t, ssm_d, seq, "ssm_bwd")
    dproj = dproj3.reshape(t_rows, IN_WIDTH)
    d_bd, d_cd, g_d = ssm_param_grads(proj, gs3.reshape(t_rows, 2 * N_STATES), xs3.reshape(t_rows, 2 * N_STATES),
                                      dyraw, n_l, "ssm_bwd_w")
    dh1, g_mix_norm = mix_backward_act(dh2, h1, mix_norm, dproj, wing, tm, "mix_bwd_act")
    dw['w_in'] = mix_backward_weights(hnm, dproj, n_l, "mix_bwd_w")
    token = start_reduce(['w_in'], "w_in")
    da1, db1, dh1_half = ffn_backward_hidden(dh1, a1, b1, ffn1_w[2], token, tm, "ffn1_bwd_hid")
    dh0, g_ffn1_norm = ffn_backward_input(dh1, h0, ffn1_norm, da1, db1, ffn1_w[0], ffn1_w[1], tm, "ffn1_bwd_in")
    dw['ffn1_w1'], dw['ffn1_w3'], dw['ffn1_w2'] = blocked_ffn(
        *ffn_backward_weights(hn1, dh1_half, a1, b1, da1, db1, n_l, FF_BWD_COLS, "ffn1_bwd_w"))
    dh0_3 = dh0.reshape(n_b, n_l, D_MODEL)
    grad_x = dh0_3[:, :seq]
    g_meta = sum_leading(dh0_3[:, seq:], "meta_sum")

    groups_per_blk = SCAN_COLS // SSM_STATE
    half = ((jnp.arange(N_SCAN_BLK) % 2)[:, None] == jnp.arange(2)[None, :]).astype(F32)
    eye = jnp.eye(groups_per_blk, dtype=F32)

    def group_blocks(part, channels_first):
        if channels_first:
            t = jnp.sum(part.reshape(N_SCAN_BLK, 2, LANES // 2, SCAN_COLS) * half[:, :, None, None], axis=1)
            t = t.reshape(N_SCAN_BLK, groups_per_blk, SSM_GROUP, groups_per_blk, SSM_STATE)
            t = jnp.sum(t * eye[None, :, None, :, None], axis=3)
            return t.reshape(SSM_GROUPS, SSM_GROUP, SSM_STATE)
        t = jnp.sum(part.reshape(N_SCAN_BLK, SCAN_COLS, 2, LANES // 2) * half[:, None, :, None], axis=2)
        t = t.reshape(N_SCAN_BLK, groups_per_blk, SSM_STATE, groups_per_blk, SSM_GROUP)
        t = jnp.sum(t * eye[None, :, None, :, None], axis=3)
        return t.reshape(SSM_GROUPS, SSM_STATE, SSM_GROUP).transpose(0, 2, 1)

    dbbr = group_blocks(d_bd[:, :, :SCAN_COLS], True).transpose(1, 0, 2).reshape(SSM_GROUP, N_STATES)
    dbbi = group_blocks(d_bd[:, :, SCAN_COLS:], True).transpose(1, 0, 2).reshape(SSM_GROUP, N_STATES)
    g_c_re = group_blocks(d_cd[:, :SCAN_COLS, :], False)[None]
    g_c_im = -group_blocks(d_cd[:, SCAN_COLS:, :], False)[None]
    group_sum = (jnp.arange(N_STATES)[:, None] // SSM_STATE == jnp.arange(LANES)[None, :]).astype(F32)
    g_ar, g_ai, g_ls, g_br, g_bi = ssm_param_backward(
        ar, ai, ls, br_t, bi_t, dlr_p.reshape(n_b * SUBLANES, N_STATES), dli_p.reshape(n_b * SUBLANES, N_STATES),
        dbbr, dbbi, group_sum, "ssm_bwd_params")
    g_sinks = sum_leading(dsink_p, "sink_sum")[0:1, :N_KV_HEADS * Q_PER_KV]

    small = {
        'ffn1_norm': g_ffn1_norm, 'mix_norm': g_mix_norm, 'ffn2_norm': g_ffn2_norm, 'final_norm': g_final.reshape(D_MODEL),
        'attn_sinks': g_sinks, 'ssm_a_re': g_ar.reshape(1, SSM_GROUPS, SSM_STATE), 'ssm_a_im': g_ai.reshape(1, SSM_GROUPS, SSM_STATE),
        'ssm_log_step': g_ls[:, :SSM_GROUPS],
        'ssm_b_re': g_br.reshape(SSM_GROUP, SSM_GROUPS, SSM_STATE).transpose(1, 2, 0)[None],
        'ssm_b_im': g_bi.reshape(SSM_GROUP, SSM_GROUPS, SSM_STATE).transpose(1, 2, 0)[None],
        'ssm_c_re': g_c_re, 'ssm_c_im': g_c_im, 'ssm_d': g_d,
    }

    zeros_meta = jnp.zeros((N_META, D_MODEL), F32)
    packed_g = _pack([small[n] for n in REPLICATED] + [g_meta])
    (parts,) = all_gather_list([packed_g], "ag_small_grads")
    packed_out = adamw_small(parts, _pack([w[n] for n in REPLICATED] + [zeros_meta]),
                             _pack([m[n] for n in REPLICATED] + [zeros_meta]),
                             _pack([v[n] for n in REPLICATED] + [zeros_meta]), "adamw_small")
    shapes = [w[n].shape for n in REPLICATED] + [(N_META, D_MODEL)]
    grads, deltas, new_m, new_v = {}, {}, {}, {}
    unpacked = [_unpack(p, shapes) for p in packed_out]
    for k, n in enumerate(REPLICATED):
        grads[n], deltas[n], new_m[n], new_v[n] = (u[k] for u in unpacked)
    g_meta_full = unpacked[0][-1]
    grads['meta_tokens'] = lax.dynamic_index_in_dim(
        g_meta_full.reshape(N_META, N_DEV, D_MODEL // N_DEV), me, axis=1, keepdims=False)
    deltas['meta_tokens'], new_m['meta_tokens'], new_v['meta_tokens'] = adamw_plain(
        grads['meta_tokens'], w['meta_tokens'], m['meta_tokens'], v['meta_tokens'], "adamw_meta")

    def views(n):
        if n in ffn_names:
            return functools.partial(hidden_on_rows, n), functools.partial(hidden_on_rows_back, n)
        return (lambda t: t[0]), (lambda t: t[None])

    g_list = [dw[n] for n in first_names]
    r1 = rs_sibling_swap(g_list, "rs_sibling")
    pairs = [pair_sums(dest, g, r, "rs_pair_" + n) for n, g, r in zip(first_names, g_list, r1)]
    r2 = rs_chip_exchange(pairs, "rs_chips")
    for n, g, ra, rb in zip(first_names, g_list, r1, r2):
        two_d, back = views(n)
        out = adamw_sharded(dest, g, ra, rb, two_d(w[n]), two_d(m[n]), two_d(v[n]), "adamw_" + n)
        grads[n], deltas[n], new_m[n], new_v[n] = (back(o) for o in out)
    for tag, (names, send, recv, srcs, lands) in early.items():
        lands = exchange_wait(send, recv, srcs, lands, r2[0], False, "rs_" + tag + "_wait")
        for n, g, land in zip(names, srcs, lands):
            two_d, back = views(n)
            out = adamw_exchanged(me_idx, g, land, two_d(w[n]), two_d(m[n]), two_d(v[n]), "adamw_" + n)
            grads[n], deltas[n], new_m[n], new_v[n] = (back(o) for o in out)

    return (loss, grad_x, *[grads[n] for n in WEIGHTS], *[deltas[n] for n in WEIGHTS],
            *[new_m[n] for n in WEIGHTS], *[new_v[n] for n in WEIGHTS])
```

```python
import functools

import jax
import jax.numpy as jnp
from jax import lax
from jax.experimental import pallas as pl
from jax.experimental.pallas import tpu as pltpu

F32 = jnp.float32
BF16 = jnp.bfloat16
MESH = pl.DeviceIdType.MESH

N_DEV = 8
D_MODEL = 1024
N_META = 16
HEAD_DIM = 64
N_KV_HEADS = 4
Q_PER_KV = 4
BLOCK = 128
KV_WIDTH = N_KV_HEADS * HEAD_DIM
SSM_GROUP = 16
SSM_WIDTH = 512
SSM_GROUPS = 32
SSM_STATE = 64
N_STATES = SSM_GROUPS * SSM_STATE
D_FF = 2816
FF_BLK = D_FF // N_DEV
IN_WIDTH = 4096
IN_BLK = IN_WIDTH // N_DEV
NORM_EPS = 1e-6
NEG_INF = -1e30
SCAN_COLS = 256
N_SCAN_BLK = N_STATES // SCAN_COLS
SUBLANES = 8
LANES = 128
MXU_WIDTH = 256
FF_FWD_COLS = D_FF // 2
FF_BWD_COLS = MXU_WIDTH

ADAM_LR = 0.001
ADAM_B1 = 0.9
ADAM_B2 = 0.999
ADAM_EPS = 1e-08
ADAM_WD = 0.01
ADAM_STEP = 10

VMEM_BIG = 56 * 1024 * 1024


def _cp(sem=None, vmem=None):
    kw = {}
    if sem is not None:
        kw["dimension_semantics"] = sem
    if vmem is not None:
        kw["vmem_limit_bytes"] = vmem
    return pltpu.CompilerParams(**kw)


def _pcall(body, **kw):
    return pl.pallas_call(body, **kw)


def _dot(a, b):
    return jnp.dot(a, b, preferred_element_type=F32)


def _dot_nt(a, b):
    return lax.dot_general(a, b, (((1,), (1,)), ((), ())), preferred_element_type=F32)


def _dot_tn(a, b):
    return lax.dot_general(a, b, (((0,), (0,)), ((), ())), preferred_element_type=F32)


def _sigmoid(x):
    return 1.0 / (1.0 + jnp.exp(-x))


def _row_tile(rows, cap):
    best = None
    for t in range(16, min(rows, cap) + 1, 16):
        if rows % t == 0:
            best = t
    assert best is not None, rows
    return best


def _my_place():
    return lax.axis_index("x"), lax.axis_index("y"), lax.axis_index("c")


def all_gather_list(shards, name):
    n = len(shards)

    def body(*refs):
        ins, outs = refs[:n], refs[n:2 * n]
        send_sems, recv_sems, local_sems = refs[2 * n:]
        x, y, c = _my_place()
        me, sibling = (x, y, c), (x, y, 1 - c)
        chips = [(1 - x, y), (x, 1 - y), (1 - x, 1 - y)]

        def blk(a, px, py, pc):
            return outs[a].at[4 * px + 2 * py + pc]

        def copy(a, k, block, to, src=None):
            return pltpu.make_async_remote_copy(
                src_ref=blk(a, *block) if src is None else src, dst_ref=blk(a, *block),
                send_sem=send_sems.at[a * 7 + k], recv_sem=recv_sems.at[a * 7 + k],
                device_id=to, device_id_type=MESH)

        mine = [pltpu.make_async_copy(ins[a], blk(a, *me), local_sems.at[a]) for a in range(n)]
        for cp in mine:
            cp.start()
        first = []
        for a in range(n):
            first.append(copy(a, 0, me, sibling, src=ins[a]))
            first += [copy(a, 1 + j, me, (*chip, c), src=ins[a]) for j, chip in enumerate(chips)]
        for cp in first:
            cp.start()
        passed = []
        for j, chip in enumerate(chips):
            for a in range(n):
                copy(a, 1 + j, (*chip, c), me).wait_recv()
                cp = copy(a, 4 + j, (*chip, c), sibling)
                cp.start()
                passed.append(cp)
        for a in range(n):
            copy(a, 0, sibling, me).wait_recv()
            for j, chip in enumerate(chips):
                copy(a, 4 + j, (*chip, 1 - c), me).wait_recv()
        for cp in first + passed:
            cp.wait_send()
        for cp in mine:
            cp.wait()

    any_spec = pl.BlockSpec(memory_space=pl.ANY)
    return _pcall(
        body, name=name,
        out_shape=[jax.ShapeDtypeStruct((N_DEV,) + s.shape, s.dtype) for s in shards],
        in_specs=[any_spec] * n, out_specs=[any_spec] * n,
        scratch_shapes=[pltpu.SemaphoreType.DMA((7 * n,)), pltpu.SemaphoreType.DMA((7 * n,)),
                        pltpu.SemaphoreType.DMA((n,))],
    )(*shards)


HBM_SPEC = pl.BlockSpec(memory_space=pltpu.HBM)
SEM_SPEC = pl.BlockSpec(memory_space=pltpu.SEMAPHORE)
N_PEERS = N_DEV - 1


def _related(k):
    x, y, c = _my_place()
    px = 1 - x if k & 4 else x
    py = 1 - y if k & 2 else y
    pc = 1 - c if k & 1 else c
    return (px, py, pc), 4 * px + 2 * py + pc


def _exchange_copies(srcs, lands, send_sems, recv_sems, gather):
    x, y, c = _my_place()
    me = 4 * x + 2 * y + c
    copies = []
    for a, (src, land) in enumerate(zip(srcs, lands)):
        for k in range(1, N_DEV):
            peer, d = _related(k)
            copies.append(pltpu.make_async_remote_copy(
                src_ref=src if gather else src.at[d], dst_ref=land.at[me] if gather else land.at[k],
                send_sem=send_sems.at[a * N_PEERS + k - 1], recv_sem=recv_sems.at[a * N_PEERS + k - 1],
                device_id=peer, device_id_type=MESH))
    return copies


def exchange_start(srcs, after, gather, name):
    n = len(srcs)
    land_shapes = [((N_DEV,) + s.shape) if gather else s.shape for s in srcs]

    def body(*refs):
        send_sems, recv_sems = refs[2 * n + 1], refs[2 * n + 2]
        for cp in _exchange_copies(refs[:n], refs[n:2 * n], send_sems, recv_sems, gather):
            cp.start()
        token = refs[-1]
        token[...] = jnp.zeros_like(token)

    sems = pltpu.SemaphoreType.DMA((n * N_PEERS,))
    lands = [pltpu.with_memory_space_constraint(lax.empty(shape, s.dtype), pltpu.HBM) for shape, s in zip(land_shapes, srcs)]
    out = _pcall(
        body, name=name,
        out_shape=(sems, sems, *[pltpu.HBM(s.shape, s.dtype) for s in srcs],
                   *[pltpu.HBM(shape, s.dtype) for shape, s in zip(land_shapes, srcs)],
                   jax.ShapeDtypeStruct((SUBLANES, LANES), F32)),
        in_specs=[HBM_SPEC] * (2 * n) + [pl.BlockSpec(memory_space=pl.ANY)],
        out_specs=(SEM_SPEC, SEM_SPEC, *[HBM_SPEC] * (2 * n), pl.BlockSpec(memory_space=pltpu.VMEM)),
        input_output_aliases={i: 2 + i for i in range(2 * n)},
        compiler_params=pltpu.CompilerParams(has_side_effects=pltpu.SideEffectType.DATAFLOW_SIDE_EFFECTING),
    )(*[pltpu.with_memory_space_constraint(s, pltpu.HBM) for s in srcs], *lands, after)
    return out[0], out[1], list(out[2:2 + n]), list(out[2 + n:2 + 2 * n]), out[-1]


def exchange_wait(send_sems, recv_sems, srcs, lands, after, gather, name):
    n = len(srcs)

    def body(*refs):
        for cp in _exchange_copies(refs[:n], refs[n:2 * n], refs[2 * n], refs[2 * n + 1], gather):
            cp.wait_send()
            cp.wait_recv()

    out = _pcall(
        body, name=name,
        out_shape=(*[pltpu.HBM(s.shape, s.dtype) for s in srcs], *[pltpu.HBM(z.shape, z.dtype) for z in lands]),
        in_specs=[HBM_SPEC] * (2 * n) + [SEM_SPEC, SEM_SPEC, pl.BlockSpec(memory_space=pl.ANY)],
        out_specs=tuple([HBM_SPEC] * (2 * n)),
        input_output_aliases={i: i for i in range(2 * n)},
        compiler_params=pltpu.CompilerParams(has_side_effects=pltpu.SideEffectType.DATAFLOW_SIDE_EFFECTING),
    )(*srcs, *lands, send_sems, recv_sems, after)
    return list(out[n:])


def adamw_exchanged(me, g, land, w, m, v, name):
    rows, cols = w.shape
    tr = _row_tile(rows, 256)

    def body(me_ref, g_ref, land_ref, w_ref, m_ref, v_ref, go_ref, d_ref, mo_ref, vo_ref):
        grad = g_ref[...].astype(F32)
        for k in range(1, N_DEV):
            grad = grad + land_ref[k].astype(F32)
        delta, m_new, v_new = _adam_math(w_ref[...], grad, m_ref[...], v_ref[...])
        go_ref[...] = grad
        d_ref[...] = delta
        mo_ref[...] = m_new
        vo_ref[...] = v_new

    tile = pl.BlockSpec((tr, cols), lambda r, ix: (r, 0))
    out = jax.ShapeDtypeStruct((rows, cols), F32)
    return _pcall(
        body, name=name, out_shape=[out] * 4,
        grid_spec=pltpu.PrefetchScalarGridSpec(
            num_scalar_prefetch=1, grid=(rows // tr,),
            in_specs=[pl.BlockSpec((None, tr, cols), lambda r, ix: (ix[0], r, 0)),
                      pl.BlockSpec((N_DEV, tr, cols), lambda r, ix: (0, r, 0)), tile, tile, tile],
            out_specs=[tile] * 4),
        compiler_params=_cp(("arbitrary",)),
    )(me, g, land, w, m, v)


def _adam_math(w, g, m, v):
    m = ADAM_B1 * m + (1.0 - ADAM_B1) * g
    v = ADAM_B2 * v + (1.0 - ADAM_B2) * (g * g)
    m_hat = m / (1.0 - ADAM_B1 ** ADAM_STEP)
    v_hat = v / (1.0 - ADAM_B2 ** ADAM_STEP)
    delta = -ADAM_LR * (m_hat / (jnp.sqrt(v_hat) + ADAM_EPS) + ADAM_WD * w)
    return delta, m, v


def adamw_small(parts, w, m, v, name):
    _, rows, cols = parts.shape

    def body(p_ref, w_ref, m_ref, v_ref, go_ref, d_ref, mo_ref, vo_ref):
        grad = p_ref[0]
        for k in range(1, N_DEV):
            grad = grad + p_ref[k]
        delta, m_new, v_new = _adam_math(w_ref[...], grad, m_ref[...], v_ref[...])
        go_ref[...] = grad
        d_ref[...] = delta
        mo_ref[...] = m_new
        vo_ref[...] = v_new

    out = jax.ShapeDtypeStruct((rows, cols), F32)
    return _pcall(body, name=name, out_shape=[out] * 4, compiler_params=_cp(vmem=VMEM_BIG))(parts, w, m, v)


def adamw_plain(g, w, m, v, name):
    def body(g_ref, w_ref, m_ref, v_ref, d_ref, mo_ref, vo_ref):
        delta, m_new, v_new = _adam_math(w_ref[...], g_ref[...], m_ref[...], v_ref[...])
        d_ref[...] = delta
        mo_ref[...] = m_new
        vo_ref[...] = v_new

    out = jax.ShapeDtypeStruct(w.shape, F32)
    return _pcall(body, name=name, out_shape=[out] * 3)(g, w, m, v)


def _rms_fwd(x, g):
    r = lax.rsqrt(jnp.mean(x * x, axis=-1, keepdims=True) + NORM_EPS)
    return x * r * g


def _rms_bwd(x, g, dy):
    r = lax.rsqrt(jnp.mean(x * x, axis=-1, keepdims=True) + NORM_EPS)
    xh = x * r
    t = dy * g
    dx = r * (t - xh * jnp.mean(t * xh, axis=-1, keepdims=True))
    return dx, jnp.sum(dy * xh, axis=0, keepdims=True)


def _accumulate(ref, val, first):
    @pl.when(first)
    def _():
        ref[...] = val

    @pl.when(jnp.logical_not(first))
    def _():
        ref[...] += val


def _col_chunks(width):
    return [(c0, min(MXU_WIDTH, width - c0)) for c0 in range(0, width, MXU_WIDTH)]


ANY_SPEC = pl.BlockSpec(memory_space=pl.ANY)


def ffn_forward(h, norm, w1, w3, w2, after, tm, tn, name):
    t_rows = h.shape[0]
    nj = D_FF // tn

    def body(h_ref, g_ref, w1_ref, w3_ref, w2_ref, _, out_ref, hn_ref, a_ref, b_ref, acc_ref):
        j = pl.program_id(1)

        @pl.when(j == 0)
        def _():
            hn_ref[...] = _rms_fwd(h_ref[...], g_ref[...]).astype(BF16)
            acc_ref[...] = jnp.zeros_like(acc_ref)

        hn = hn_ref[...]
        for c0, cw in _col_chunks(tn):
            a = _dot_nt(hn, w1_ref[c0:c0 + cw, :])
            b = _dot_nt(hn, w3_ref[c0:c0 + cw, :])
            a_ref[:, c0:c0 + cw] = a.astype(BF16)
            b_ref[:, c0:c0 + cw] = b.astype(BF16)
            hid = (a * _sigmoid(a) * b).astype(BF16)
            acc_ref[...] += _dot(hid, w2_ref[c0:c0 + cw, :])

        @pl.when(j == nj - 1)
        def _():
            out_ref[...] = h_ref[...] + 0.5 * acc_ref[...]

    row = pl.BlockSpec((tm, D_MODEL), lambda i, j: (i, 0))
    hid_blk = pl.BlockSpec((tm, tn), lambda i, j: (i, j))
    w_row = pl.BlockSpec((tn, D_MODEL), lambda i, j: (j, 0))
    return _pcall(
        body, name=name, grid=(t_rows // tm, nj),
        in_specs=[row, pl.BlockSpec((1, D_MODEL), lambda i, j: (0, 0)), w_row, w_row, w_row, ANY_SPEC],
        out_specs=[row, row, hid_blk, hid_blk],
        out_shape=[jax.ShapeDtypeStruct((t_rows, D_MODEL), F32), jax.ShapeDtypeStruct((t_rows, D_MODEL), BF16),
                   jax.ShapeDtypeStruct((t_rows, D_FF), BF16), jax.ShapeDtypeStruct((t_rows, D_FF), BF16)],
        scratch_shapes=[pltpu.VMEM((tm, D_MODEL), F32)],
        compiler_params=_cp(("arbitrary", "arbitrary"), VMEM_BIG),
    )(h, norm, w1, w3, w2, after)


def _resident(shape):
    return pl.BlockSpec(shape, lambda *_: (0,) * len(shape), pipeline_mode=pl.Buffered(1))


def ffn_backward_hidden(dh, a, b, w2, after, tm, name):
    t_rows = dh.shape[0]

    def body(dh_ref, a_ref, b_ref, w2_ref, _, da_ref, db_ref, dhb_ref):
        dhb = (0.5 * dh_ref[...]).astype(BF16)
        dhb_ref[...] = dhb
        for c0, cw in _col_chunks(D_FF):
            dhid = _dot_nt(dhb, w2_ref[c0:c0 + cw, :])
            av = a_ref[:, c0:c0 + cw].astype(F32)
            bv = b_ref[:, c0:c0 + cw].astype(F32)
            s = _sigmoid(av)
            da_ref[:, c0:c0 + cw] = (dhid * bv * (s * (1.0 + av * (1.0 - s)))).astype(BF16)
            db_ref[:, c0:c0 + cw] = (dhid * (av * s)).astype(BF16)

    hid = pl.BlockSpec((tm, D_FF), lambda i: (i, 0))
    row = pl.BlockSpec((tm, D_MODEL), lambda i: (i, 0))
    return _pcall(
        body, name=name, grid=(t_rows // tm,),
        in_specs=[row, hid, hid, _resident((D_FF, D_MODEL)), ANY_SPEC],
        out_specs=[hid, hid, row],
        out_shape=[jax.ShapeDtypeStruct((t_rows, D_FF), BF16), jax.ShapeDtypeStruct((t_rows, D_FF), BF16),
                   jax.ShapeDtypeStruct((t_rows, D_MODEL), BF16)],
        compiler_params=_cp(("arbitrary",), VMEM_BIG),
    )(dh, a, b, w2, after)


def ffn_backward_input(dh, h, norm, da, db, w1, w3, after, tm, name):
    t_rows = h.shape[0]

    def body(dh_ref, h_ref, g_ref, da_ref, db_ref, w1_ref, w3_ref, _, dhin_ref, dg_ref):
        dhn = _dot(da_ref[...], w1_ref[...]) + _dot(db_ref[...], w3_ref[...])
        dx, dg = _rms_bwd(h_ref[...], g_ref[...], dhn)
        dhin_ref[...] = dh_ref[...] + dx
        _accumulate(dg_ref, dg, pl.program_id(0) == 0)

    row = pl.BlockSpec((tm, D_MODEL), lambda i: (i, 0))
    vec = pl.BlockSpec((1, D_MODEL), lambda i: (0, 0))
    hid = pl.BlockSpec((tm, D_FF), lambda i: (i, 0))
    return _pcall(
        body, name=name, grid=(t_rows // tm,),
        in_specs=[row, row, vec, hid, hid, _resident((D_FF, D_MODEL)), _resident((D_FF, D_MODEL)), ANY_SPEC],
        out_specs=[row, vec],
        out_shape=[jax.ShapeDtypeStruct((t_rows, D_MODEL), F32), jax.ShapeDtypeStruct((1, D_MODEL), F32)],
        compiler_params=_cp(("arbitrary",), VMEM_BIG),
    )(dh, h, norm, da, db, w1, w3, after)


def ffn_backward_weights(hn, dh, a, b, da, db, tm, tn, name):
    t_rows = hn.shape[0]
    ni = t_rows // tm
    kc = _row_tile(tm, 688)

    def body(hn_ref, dh_ref, a_ref, b_ref, da_ref, db_ref, dw1_ref, dw3_ref, dw2_ref, acc1, acc3, acc2):
        i = pl.program_id(1)
        parts = None
        for r0 in range(0, tm, kc):
            rows = slice(r0, r0 + kc)
            hn_v = hn_ref[rows, :]
            av = a_ref[rows, :].astype(F32)
            hid = (av * _sigmoid(av) * b_ref[rows, :].astype(F32)).astype(BF16)
            new = (_dot_tn(hn_v, da_ref[rows, :]), _dot_tn(hn_v, db_ref[rows, :]), _dot_tn(hid, dh_ref[rows, :]))
            parts = new if parts is None else tuple(p + q for p, q in zip(parts, new))
        _accumulate(acc1, parts[0], i == 0)
        _accumulate(acc3, parts[1], i == 0)
        _accumulate(acc2, parts[2], i == 0)

        @pl.when(i == ni - 1)
        def _():
            dw1_ref[...] = acc1[...].T.astype(BF16)
            dw3_ref[...] = acc3[...].T.astype(BF16)
            dw2_ref[...] = acc2[...].astype(BF16)

    row = pl.BlockSpec((tm, D_MODEL), lambda j, i: (i, 0))
    hid_blk = pl.BlockSpec((tm, tn), lambda j, i: (i, j))
    w_row = pl.BlockSpec((tn, D_MODEL), lambda j, i: (j, 0))
    out = jax.ShapeDtypeStruct((D_FF, D_MODEL), BF16)
    return _pcall(
        body, name=name, grid=(D_FF // tn, ni),
        in_specs=[row, row, hid_blk, hid_blk, hid_blk, hid_blk],
        out_specs=[w_row, w_row, w_row], out_shape=[out, out, out],
        scratch_shapes=[pltpu.VMEM((D_MODEL, tn), F32), pltpu.VMEM((D_MODEL, tn), F32), pltpu.VMEM((tn, D_MODEL), F32)],
        compiler_params=_cp(("arbitrary", "arbitrary"), VMEM_BIG),
    )(hn, dh, a, b, da, db)


def mix_forward(h, norm, wing, tm, name):
    t_rows = h.shape[0]

    def body(h_ref, g_ref, w_ref, hn_ref, p_ref):
        hn = _rms_fwd(h_ref[...], g_ref[...]).astype(BF16)
        hn_ref[...] = hn
        for j in range(N_DEV):
            p_ref[:, j * IN_BLK:(j + 1) * IN_BLK] = _dot(hn, w_ref[j]).astype(BF16)

    row = pl.BlockSpec((tm, D_MODEL), lambda i: (i, 0))
    return _pcall(
        body, name=name, grid=(t_rows // tm,),
        in_specs=[row, pl.BlockSpec((1, D_MODEL), lambda i: (0, 0)),
                  pl.BlockSpec((N_DEV, D_MODEL, IN_BLK), lambda i: (0, 0, 0))],
        out_specs=[row, pl.BlockSpec((tm, IN_WIDTH), lambda i: (i, 0))],
        out_shape=[jax.ShapeDtypeStruct((t_rows, D_MODEL), BF16), jax.ShapeDtypeStruct((t_rows, IN_WIDTH), BF16)],
        compiler_params=_cp(("arbitrary",), VMEM_BIG),
    )(h, norm, wing)


def mix_backward_act(dh, h, norm, dproj, wing, tm, name):
    t_rows = h.shape[0]
    per_step = 4
    nj = N_DEV // per_step

    def body(dh_ref, h_ref, g_ref, dp_ref, w_ref, dhin_ref, dg_ref, acc_ref):
        i, j = pl.program_id(0), pl.program_id(1)
        part = functools.reduce(
            lambda u, w: u + w, [_dot_nt(dp_ref[:, k * IN_BLK:(k + 1) * IN_BLK], w_ref[k]) for k in range(per_step)])
        _accumulate(acc_ref, part, j == 0)

        @pl.when(j == nj - 1)
        def _():
            dx, dg = _rms_bwd(h_ref[...], g_ref[...], acc_ref[...])
            dhin_ref[...] = dh_ref[...] + dx
            _accumulate(dg_ref, dg, i == 0)

    row = pl.BlockSpec((tm, D_MODEL), lambda i, j: (i, 0))
    vec = pl.BlockSpec((1, D_MODEL), lambda i, j: (0, 0))
    return _pcall(
        body, name=name, grid=(t_rows // tm, nj),
        in_specs=[row, row, vec, pl.BlockSpec((tm, per_step * IN_BLK), lambda i, j: (i, j)),
                  pl.BlockSpec((per_step, D_MODEL, IN_BLK), lambda i, j: (j, 0, 0))],
        out_specs=[row, vec],
        out_shape=[jax.ShapeDtypeStruct((t_rows, D_MODEL), F32), jax.ShapeDtypeStruct((1, D_MODEL), F32)],
        scratch_shapes=[pltpu.VMEM((tm, D_MODEL), F32)],
        compiler_params=_cp(("arbitrary", "arbitrary"), VMEM_BIG),
    )(dh, h, norm, dproj, wing)


def mix_backward_weights(hn, dproj, tm, name):
    t_rows = hn.shape[0]
    ni = t_rows // tm
    per_step = 2

    kc = _row_tile(tm, 688)

    def body(hn_ref, dp_ref, dw_ref, acc):
        i = pl.program_id(1)
        part = functools.reduce(lambda u, w: u + w, [_dot_tn(hn_ref[r0:r0 + kc, :], dp_ref[r0:r0 + kc, :])
                                                    for r0 in range(0, tm, kc)])
        _accumulate(acc, part, i == 0)

        @pl.when(i == ni - 1)
        def _():
            for k in range(per_step):
                dw_ref[k] = acc[:, k * IN_BLK:(k + 1) * IN_BLK].astype(BF16)

    return _pcall(
        body, name=name, grid=(N_DEV // per_step, ni),
        in_specs=[pl.BlockSpec((tm, D_MODEL), lambda j, i: (i, 0)),
                  pl.BlockSpec((tm, per_step * IN_BLK), lambda j, i: (i, j))],
        out_specs=pl.BlockSpec((per_step, D_MODEL, IN_BLK), lambda j, i: (j, 0, 0)),
        out_shape=jax.ShapeDtypeStruct((N_DEV, D_MODEL, IN_BLK), BF16),
        scratch_shapes=[pltpu.VMEM((D_MODEL, per_step * IN_BLK), F32)],
        compiler_params=_cp(("arbitrary", "arbitrary"), VMEM_BIG),
    )(hn, dproj)


GELU_C = 0.7978845608028654
GELU_K = 0.044715


def _gelu(x):
    return 0.5 * x * (1.0 + jnp.tanh(GELU_C * (x + GELU_K * (x * x * x))))


def _gelu_and_grad(x):
    th = jnp.tanh(GELU_C * (x + GELU_K * (x * x * x)))
    val = 0.5 * x * (1.0 + th)
    grad = 0.5 * (1.0 + th) + 0.5 * x * (1.0 - th * th) * (GELU_C * (1.0 + 3.0 * GELU_K * (x * x)))
    return val, grad


def merge_forward(h, yraw, attn, proj, glu_a, glu_b, w_out, tm, name):
    t_rows = h.shape[0]

    def body(h_ref, y_ref, at_ref, gate_ref, a_ref, b_ref, wo_ref, out_ref):
        y = _gelu(y_ref[...]).astype(BF16)
        ssm = _dot(y, a_ref[...]) * _sigmoid(_dot(y, b_ref[...]))
        ga = gate_ref[:, :D_MODEL].astype(F32)
        gs = gate_ref[:, D_MODEL:].astype(F32)
        merged = _sigmoid(ga) * at_ref[...].astype(F32) + _sigmoid(gs) * ssm
        out_ref[...] = h_ref[...] + _dot(merged.astype(BF16), wo_ref[...])

    row = pl.BlockSpec((tm, D_MODEL), lambda i: (i, 0))
    glu = pl.BlockSpec((SSM_WIDTH, D_MODEL), lambda i: (0, 0))
    return _pcall(
        body, name=name, grid=(t_rows // tm,),
        in_specs=[row, pl.BlockSpec((tm, SSM_WIDTH), lambda i: (i, 0)), row,
                  pl.BlockSpec((tm, 2 * D_MODEL), lambda i: (i, 1)), glu, glu,
                  pl.BlockSpec((D_MODEL, D_MODEL), lambda i: (0, 0))],
        out_specs=row, out_shape=jax.ShapeDtypeStruct((t_rows, D_MODEL), F32),
        compiler_params=_cp(("arbitrary",), VMEM_BIG),
    )(h, yraw, attn, proj, glu_a, glu_b, w_out)


def merge_backward(dh, yraw, attn, proj, glu_a, glu_b, w_out, after, tm, name):
    t_rows = dh.shape[0]

    def body(dh_ref, y_ref, at_ref, gate_ref, a_ref, b_ref, wo_ref, _,
             dat_ref, dy_ref, dgate_ref, d16_ref, mg_ref, y16_ref, dya_ref, dyb_ref):
        d16 = dh_ref[...].astype(BF16)
        d16_ref[...] = d16
        gel, dgel = _gelu_and_grad(y_ref[...].astype(F32))
        y16 = gel.astype(BF16)
        y16_ref[...] = y16
        dy = None
        for c0, cw in _col_chunks(D_MODEL):
            cols = slice(c0, c0 + cw)
            gcols = slice(D_MODEL + c0, D_MODEL + c0 + cw)
            dmerged = _dot_nt(d16, wo_ref[cols, :])
            ya = _dot(y16, a_ref[:, cols])
            sb = _sigmoid(_dot(y16, b_ref[:, cols]))
            ssm = ya * sb
            sa = _sigmoid(gate_ref[:, cols].astype(F32))
            ss = _sigmoid(gate_ref[:, gcols].astype(F32))
            attn_v = at_ref[:, cols].astype(F32)
            mg_ref[:, cols] = (sa * attn_v + ss * ssm).astype(BF16)
            dat_ref[:, cols] = (dmerged * sa).astype(BF16)
            dgate_ref[:, cols] = (dmerged * attn_v * sa * (1.0 - sa)).astype(BF16)
            dgate_ref[:, gcols] = (dmerged * ssm * ss * (1.0 - ss)).astype(BF16)
            dssm = dmerged * ss
            dya = (dssm * sb).astype(BF16)
            dyb = (dssm * ya * sb * (1.0 - sb)).astype(BF16)
            dya_ref[:, cols] = dya
            dyb_ref[:, cols] = dyb
            part = _dot_nt(dya, a_ref[:, cols]) + _dot_nt(dyb, b_ref[:, cols])
            dy = part if dy is None else dy + part
        dy_ref[...] = (dy * dgel).astype(BF16)

    row = pl.BlockSpec((tm, D_MODEL), lambda i: (i, 0))
    ssm_row = pl.BlockSpec((tm, SSM_WIDTH), lambda i: (i, 0))
    gates = pl.BlockSpec((tm, 2 * D_MODEL), lambda i: (i, 1))
    wide = jax.ShapeDtypeStruct((t_rows, D_MODEL), BF16)
    narrow = jax.ShapeDtypeStruct((t_rows, SSM_WIDTH), BF16)
    return _pcall(
        body, name=name, grid=(t_rows // tm,),
        in_specs=[row, ssm_row, row, gates, _resident((SSM_WIDTH, D_MODEL)), _resident((SSM_WIDTH, D_MODEL)),
                  _resident((D_MODEL, D_MODEL)), ANY_SPEC],
        out_specs=[row, ssm_row, gates, row, row, ssm_row, row, row],
        out_shape=[wide, narrow, jax.ShapeDtypeStruct((t_rows, IN_WIDTH), BF16), wide, wide, narrow, wide, wide],
        compiler_params=_cp(("arbitrary",), VMEM_BIG),
    )(dh, yraw, attn, proj, glu_a, glu_b, w_out, after)


def merge_backward_weights(d16, merged, y16, dya, dyb, tm, name):
    t_rows = d16.shape[0]

    def body(d_ref, mg_ref, y_ref, dya_ref, dyb_ref, dwo_ref, da_ref, db_ref):
        first = pl.program_id(0) == 0
        y16 = y_ref[...]
        _accumulate(dwo_ref, _dot_tn(mg_ref[...], d_ref[...]), first)
        _accumulate(da_ref, _dot_tn(y16, dya_ref[...]), first)
        _accumulate(db_ref, _dot_tn(y16, dyb_ref[...]), first)

    row = pl.BlockSpec((tm, D_MODEL), lambda i: (i, 0))
    ssm_row = pl.BlockSpec((tm, SSM_WIDTH), lambda i: (i, 0))
    glu = pl.BlockSpec((SSM_WIDTH, D_MODEL), lambda i: (0, 0))
    wo = pl.BlockSpec((D_MODEL, D_MODEL), lambda i: (0, 0))
    return _pcall(
        body, name=name, grid=(t_rows // tm,),
        in_specs=[row, row, ssm_row, row, row], out_specs=[wo, glu, glu],
        out_shape=[jax.ShapeDtypeStruct((D_MODEL, D_MODEL), F32), jax.ShapeDtypeStruct((SSM_WIDTH, D_MODEL), F32),
                   jax.ShapeDtypeStruct((SSM_WIDTH, D_MODEL), F32)],
        compiler_params=_cp(("arbitrary",), VMEM_BIG),
    )(d16, merged, y16, dya, dyb)


def final_loss_backward(h, target, norm, seq, tm, name):
    t_rows = h.shape[0]
    tiles_per_example = (seq + N_META) // tm

    def body(h_ref, t_ref, g_ref, dh_ref, loss_ref, dg_ref):
        i = pl.program_id(0)
        x = h_ref[...]
        g = g_ref[...]
        r = lax.rsqrt(jnp.mean(x * x, axis=-1, keepdims=True) + NORM_EPS)
        xh = x * r
        pos = lax.broadcasted_iota(jnp.int32, (tm, 1), 0) + (i % tiles_per_example) * tm
        diff = jnp.where(pos < seq, xh * g - t_ref[...], 0.0)
        part = 0.5 * jnp.sum(jnp.sum(diff * diff, axis=-1, keepdims=True), axis=0, keepdims=True) / D_MODEL
        dy = diff / D_MODEL
        t = dy * g
        dh_ref[...] = r * (t - xh * jnp.mean(t * xh, axis=-1, keepdims=True))
        _accumulate(loss_ref, jnp.broadcast_to(part, (1, LANES)), i == 0)
        _accumulate(dg_ref, jnp.sum(dy * xh, axis=0, keepdims=True), i == 0)

    row = pl.BlockSpec((tm, D_MODEL), lambda i: (i, 0))
    vec = pl.BlockSpec((1, D_MODEL), lambda i: (0, 0))
    return _pcall(
        body, name=name, grid=(t_rows // tm,),
        in_specs=[row, row, vec],
        out_specs=[row, pl.BlockSpec((1, LANES), lambda i: (0, 0)), vec],
        out_shape=[jax.ShapeDtypeStruct((t_rows, D_MODEL), F32), jax.ShapeDtypeStruct((1, LANES), F32),
                   jax.ShapeDtypeStruct((1, D_MODEL), F32)],
        compiler_params=_cp(("arbitrary",), VMEM_BIG),
    )(h, target, norm)


ATTN_SCALE = HEAD_DIM ** -0.5
STACK_HEADS = (0, 2, 1, 3)


def _lane_half(shape, hf):
    lane = lax.broadcasted_iota(jnp.int32, shape, 1)
    return (lane < HEAD_DIM) if hf == 0 else (lane >= HEAD_DIM)


def _kv_variants(ref, rows, kh):
    tile = kh // 2
    t = ref[rows, tile * LANES:(tile + 1) * LANES].astype(F32)
    swapped = pltpu.roll(t, HEAD_DIM, axis=1)
    at_low, at_high = (t, swapped) if kh % 2 == 0 else (swapped, t)
    lo = jnp.where(_lane_half(t.shape, 0), at_low, 0.0).astype(BF16)
    hi = jnp.where(_lane_half(t.shape, 1), at_high, 0.0).astype(BF16)
    return lo, hi


def _to_kv_lanes(lo, hi, kh):
    lo = jnp.where(_lane_half(lo.shape, 0), lo, 0.0)
    hi = jnp.where(_lane_half(hi.shape, 1), hi, 0.0)
    if kh % 2 == 0:
        return lo + pltpu.roll(hi, HEAD_DIM, axis=1)
    return pltpu.roll(lo, HEAD_DIM, axis=1) + hi


def _stacked(ref, rows, kh):
    col = kh * 2 * LANES
    return jnp.concatenate([ref[rows, col:col + LANES], ref[rows, col + LANES:col + 2 * LANES]], axis=0)


def _sink_column(sink_ref, kh, nq):
    row = lax.broadcasted_iota(jnp.int32, (4 * nq, 1), 0)
    col = jnp.zeros((4 * nq, 1), F32)
    for quarter, g in enumerate(STACK_HEADS):
        col = jnp.where(row // nq == quarter, sink_ref[0, kh * Q_PER_KV + g], col)
    return col


def _softmax_parts(qs, key_tiles, masks, sink):
    scores = []
    for (k_lo, k_hi), mask in zip(key_tiles, masks):
        s = jnp.concatenate([_dot_nt(qs, k_lo), _dot_nt(qs, k_hi)], axis=0) * ATTN_SCALE
        scores.append(s if mask is None else jnp.where(mask, s, NEG_INF))
    m = functools.reduce(jnp.maximum, [jnp.max(s, axis=-1, keepdims=True) for s in scores])
    m = jnp.maximum(m, sink)
    probs = [jnp.exp(s - m) for s in scores]
    e_sink = jnp.exp(sink - m)
    den = functools.reduce(lambda u, w: u + w, [jnp.sum(p, axis=-1, keepdims=True) for p in probs]) + e_sink
    return probs, 1.0 / den, e_sink


def _band_mask(nq, first):
    keys = BLOCK if first else 2 * BLOCK
    qi = lax.broadcasted_iota(jnp.int32, (4 * nq, keys), 0) % nq
    kj = lax.broadcasted_iota(jnp.int32, (4 * nq, keys), 1)
    if first:
        return kj <= qi
    return jnp.logical_and(kj > qi, kj <= qi + BLOCK)


def _meta_mask():
    qi = lax.broadcasted_iota(jnp.int32, (4 * N_META, N_META), 0) % N_META
    kj = lax.broadcasted_iota(jnp.int32, (4 * N_META, N_META), 1)
    return kj <= qi


def _attention_schedule(seq, queries, carry):
    meta_rows = pl.ds(seq, N_META)
    carry = queries(pl.ds(0, BLOCK), BLOCK, [pl.ds(0, BLOCK), meta_rows], [_band_mask(BLOCK, True), None], carry)

    def block(n, c):
        r0 = pl.multiple_of(n * BLOCK, BLOCK)
        p0 = pl.multiple_of((n - 1) * BLOCK, BLOCK)
        return queries(pl.ds(r0, BLOCK), BLOCK, [pl.ds(p0, 2 * BLOCK), meta_rows], [_band_mask(BLOCK, False), None], c)

    carry = lax.fori_loop(1, seq // BLOCK, block, carry)
    return queries(meta_rows, N_META, [meta_rows], [_meta_mask()], carry)


def attention_forward(proj3, sinks, seq, name):
    n_b, n_l, _ = proj3.shape

    def body(sink_ref, q_ref, k_ref, v_ref, o_ref):
        def queries(q_rows, nq, key_rows, masks, carry):
            for kh in range(N_KV_HEADS):
                ks = [_kv_variants(k_ref, r, kh) for r in key_rows]
                vs = [_kv_variants(v_ref, r, kh) for r in key_rows]
                qs = _stacked(q_ref, q_rows, kh)
                probs, inv, _ = _softmax_parts(qs, ks, masks, _sink_column(sink_ref, kh, nq))
                probs = [p.astype(BF16) for p in probs]
                o_lo = functools.reduce(lambda u, w: u + w, [_dot(p[:2 * nq], v_lo) for p, (v_lo, _) in zip(probs, vs)])
                o_hi = functools.reduce(lambda u, w: u + w, [_dot(p[2 * nq:], v_hi) for p, (_, v_hi) in zip(probs, vs)])
                out = (o_lo * inv[:2 * nq] + o_hi * inv[2 * nq:]).astype(BF16)
                col = kh * 2 * LANES
                o_ref[q_rows, col:col + LANES] = out[:nq]
                o_ref[q_rows, col + LANES:col + 2 * LANES] = out[nq:]
            return carry

        _attention_schedule(seq, queries, 0)

    return _pcall(
        body, name=name, grid=(n_b,),
        in_specs=[pl.BlockSpec(memory_space=pltpu.SMEM),
                  pl.BlockSpec((None, n_l, D_MODEL), lambda b: (b, 0, 0)),
                  pl.BlockSpec((None, n_l, KV_WIDTH), lambda b: (b, 0, D_MODEL // KV_WIDTH)),
                  pl.BlockSpec((None, n_l, KV_WIDTH), lambda b: (b, 0, D_MODEL // KV_WIDTH + 1))],
        out_specs=pl.BlockSpec((None, n_l, D_MODEL), lambda b: (b, 0, 0)),
        out_shape=jax.ShapeDtypeStruct((n_b, n_l, D_MODEL), BF16),
        compiler_params=_cp(("arbitrary",), VMEM_BIG),
    )(sinks, proj3, proj3, proj3)


def attention_backward(proj3, dattn3, dproj3, sinks, after, seq, name):
    n_b, n_l, _ = proj3.shape
    qkv_width = D_MODEL + 2 * KV_WIDTH

    def body(sink_ref, q_ref, k_ref, v_ref, do_ref, _, __, dqkv_ref, dsink_ref, dk_ref, dv_ref):
        dk_ref[...] = jnp.zeros_like(dk_ref)
        dv_ref[...] = jnp.zeros_like(dv_ref)
        sub = lax.broadcasted_iota(jnp.int32, (SUBLANES, LANES), 0)
        lane = lax.broadcasted_iota(jnp.int32, (SUBLANES, LANES), 1)

        def queries(q_rows, nq, key_rows, masks, dsink):
            for kh in range(N_KV_HEADS):
                ks = [_kv_variants(k_ref, r, kh) for r in key_rows]
                vs = [_kv_variants(v_ref, r, kh) for r in key_rows]
                qs = _stacked(q_ref, q_rows, kh)
                dos = _stacked(do_ref, q_rows, kh)
                probs, inv, e_sink = _softmax_parts(qs, ks, masks, _sink_column(sink_ref, kh, nq))
                probs = [p * inv for p in probs]
                dps = [jnp.concatenate([_dot_nt(dos, v_lo), _dot_nt(dos, v_hi)], axis=0) for v_lo, v_hi in vs]
                delta = functools.reduce(
                    lambda u, w: u + w, [jnp.sum(p * dp, axis=-1, keepdims=True) for p, dp in zip(probs, dps)])
                d_sink = -(e_sink * inv) * delta
                for quarter, g in enumerate(STACK_HEADS):
                    d_here = jnp.sum(d_sink[quarter * nq:(quarter + 1) * nq], axis=0, keepdims=True)
                    dsink = dsink + jnp.where(jnp.logical_and(sub == 0, lane == kh * Q_PER_KV + g), d_here, 0.0)
                dq = None
                tile = slice((kh // 2) * LANES, (kh // 2 + 1) * LANES)
                for r, p, dp, (k_lo, k_hi) in zip(key_rows, probs, dps, ks):
                    ds = (p * (dp - delta)).astype(BF16)
                    p16 = p.astype(BF16)
                    dq_x = _dot(ds[:2 * nq], k_lo) + _dot(ds[2 * nq:], k_hi)
                    dq = dq_x if dq is None else dq + dq_x
                    dk_ref[r, tile] += _to_kv_lanes(_dot_tn(ds[:2 * nq], qs), _dot_tn(ds[2 * nq:], qs), kh) * ATTN_SCALE
                    dv_ref[r, tile] += _to_kv_lanes(_dot_tn(p16[:2 * nq], dos), _dot_tn(p16[2 * nq:], dos), kh)
                dq = (dq * ATTN_SCALE).astype(BF16)
                col = kh * 2 * LANES
                dqkv_ref[q_rows, col:col + LANES] = dq[:nq]
                dqkv_ref[q_rows, col + LANES:col + 2 * LANES] = dq[nq:]
            return dsink

        dsink_ref[...] = _attention_schedule(seq, queries, jnp.zeros((SUBLANES, LANES), F32))
        dqkv_ref[:, D_MODEL:D_MODEL + KV_WIDTH] = dk_ref[...].astype(BF16)
        dqkv_ref[:, D_MODEL + KV_WIDTH:] = dv_ref[...].astype(BF16)

    return _pcall(
        body, name=name, grid=(n_b,),
        in_specs=[pl.BlockSpec(memory_space=pltpu.SMEM),
                  pl.BlockSpec((None, n_l, D_MODEL), lambda b: (b, 0, 0)),
                  pl.BlockSpec((None, n_l, KV_WIDTH), lambda b: (b, 0, D_MODEL // KV_WIDTH)),
                  pl.BlockSpec((None, n_l, KV_WIDTH), lambda b: (b, 0, D_MODEL // KV_WIDTH + 1)),
                  pl.BlockSpec((None, n_l, D_MODEL), lambda b: (b, 0, 0)),
                  ANY_SPEC, ANY_SPEC],
        out_specs=[pl.BlockSpec((None, n_l, qkv_width), lambda b: (b, 0, 0)),
                   pl.BlockSpec((None, SUBLANES, LANES), lambda b: (b, 0, 0))],
        out_shape=[jax.ShapeDtypeStruct(dproj3.shape, BF16), jax.ShapeDtypeStruct((n_b, SUBLANES, LANES), F32)],
        scratch_shapes=[pltpu.VMEM((n_l, KV_WIDTH), F32), pltpu.VMEM((n_l, KV_WIDTH), F32)],
        input_output_aliases={5: 0},
        compiler_params=_cp(("arbitrary",), VMEM_BIG),
    )(sinks, proj3, proj3, proj3, dattn3, dproj3, after)


TAB_ROWS = 8
SCAN_UNROLL = 4


def _cmul(ar, ai, br, bi):
    return ar * br - ai * bi, ar * bi + ai * br


def _discretise(ar, ai, ls):
    step = jnp.exp(ls)
    mag = jnp.exp(ar * step)
    ang = ai * step
    cos, sin = jnp.cos(ang), jnp.sin(ang)
    lr, li = mag * cos, mag * sin
    den = ar * ar + ai * ai
    nr, ni = lr - 1.0, li
    cr = (nr * ar + ni * ai) / den
    ci = (ni * ar - nr * ai) / den
    return step, mag, lr, li, den, nr, ni, cr, ci


def _scan_tables(lr, li, reverse):
    n = lr.shape[-1]
    pw = [(lr, li)]
    for _ in range(SUBLANES - 1):
        pw.append(_cmul(pw[-1][0], pw[-1][1], lr, li))
    row = lax.broadcasted_iota(jnp.int32, (SUBLANES, n), 0)
    out = []
    for d in (1, 2, 4):
        ok = (row + d <= SUBLANES - 1) if reverse else (row >= d)
        out += [jnp.where(ok, pw[d - 1][0], 0.0), jnp.where(ok, pw[d - 1][1], 0.0)]
    cr = jnp.zeros((SUBLANES, n), F32)
    ci = jnp.zeros((SUBLANES, n), F32)
    for r in range(SUBLANES):
        e = (SUBLANES - r) if reverse else (r + 1)
        cr = jnp.where(row == r, pw[e - 1][0], cr)
        ci = jnp.where(row == r, pw[e - 1][1], ci)
    return out + [cr, ci]


def ssm_prepare(ar, ai, ls, br_t, bi_t, name):
    def body(ar_ref, ai_ref, ls_ref, br_ref, bi_ref, bbr_ref, bbi_ref, tf_ref, tr_ref):
        _, _, lr, li, _, _, _, cr, ci = _discretise(ar_ref[...], ai_ref[...], ls_ref[...])
        br, bi = br_ref[...], bi_ref[...]
        bbr_ref[...] = cr * br - ci * bi
        bbi_ref[...] = cr * bi + ci * br
        for k, t in enumerate(_scan_tables(lr, li, False)):
            tf_ref[k] = t
        for k, t in enumerate(_scan_tables(lr, -li, True)):
            tr_ref[k] = t

    return _pcall(
        body, name=name,
        out_shape=[jax.ShapeDtypeStruct((SSM_GROUP, N_STATES), F32), jax.ShapeDtypeStruct((SSM_GROUP, N_STATES), F32),
                   jax.ShapeDtypeStruct((TAB_ROWS, SUBLANES, N_STATES), F32),
                   jax.ShapeDtypeStruct((TAB_ROWS, SUBLANES, N_STATES), F32)],
    )(ar, ai, ls, br_t, bi_t)


def ssm_param_backward(ar, ai, ls, br_t, bi_t, dlr_p, dli_p, dbbr, dbbi, group_sum, name):
    def body(ar_ref, ai_ref, ls_ref, br_ref, bi_ref, dlr_ref, dli_ref, dbbr_ref, dbbi_ref, gs_ref,
             dar_ref, dai_ref, dls_ref, dbr_ref, dbi_ref):
        ar, ai = ar_ref[...], ai_ref[...]
        step, mag, lr, li, den, nr, ni, cr, ci = _discretise(ar, ai, ls_ref[...])
        br, bi, dbbr_v, dbbi_v = br_ref[...], bi_ref[...], dbbr_ref[...], dbbi_ref[...]
        dbr_ref[...] = cr * dbbr_v + ci * dbbi_v
        dbi_ref[...] = cr * dbbi_v - ci * dbbr_v
        dcr = jnp.sum(dbbr_v * br + dbbi_v * bi, axis=0, keepdims=True)
        dci = jnp.sum(dbbi_v * br - dbbr_v * bi, axis=0, keepdims=True)
        dnr = (dcr * ar - dci * ai) / den
        dni = (dcr * ai + dci * ar) / den
        dden = -(cr * dcr + ci * dci) / den
        dar = (dcr * nr + dci * ni) / den + dden * 2.0 * ar
        dai = (dcr * ni - dci * nr) / den + dden * 2.0 * ai
        dlr = jnp.sum(dlr_ref[...], axis=0, keepdims=True) + dnr
        dli = jnp.sum(dli_ref[...], axis=0, keepdims=True) + dni
        dmag = (dlr * lr + dli * li) / mag
        dang = dli * lr - dlr * li
        dar_ref[...] = dar + dmag * mag * step
        dai_ref[...] = dai + dang * step
        dstep = dmag * mag * ar + dang * ai
        dls_ref[...] = jnp.dot(dstep * step, gs_ref[...], preferred_element_type=F32, precision=lax.Precision.HIGHEST)

    vec = jax.ShapeDtypeStruct((1, N_STATES), F32)
    mat = jax.ShapeDtypeStruct((SSM_GROUP, N_STATES), F32)
    return _pcall(body, name=name, out_shape=[vec, vec, jax.ShapeDtypeStruct((1, LANES), F32), mat, mat])(
        ar, ai, ls, br_t, bi_t, dlr_p, dli_p, dbbr, dbbi, group_sum)


def _scan_rows(a, b, tabs, carry, reverse):
    for k, d in enumerate((1, 2, 4)):
        shift = SUBLANES - d if reverse else d
        sr, si = pltpu.roll(a, shift, axis=0), pltpu.roll(b, shift, axis=0)
        pr, pi = _cmul(tabs[2 * k], tabs[2 * k + 1], sr, si)
        a, b = a + pr, b + pi
    pr, pi = _cmul(tabs[6], tabs[7], carry[0], carry[1])
    return a + pr, b + pi


def _time_groups(seq, reverse):
    meta = [seq + SUBLANES * g for g in range(N_META // SUBLANES)]
    return meta[::-1] if reverse else meta


def ssm_forward_scan(proj3, b_comb, tabf, c_comb, dvec, seq, name):
    n_b, n_l, _ = proj3.shape
    u_blk = (D_MODEL + 2 * KV_WIDTH) // LANES

    def body(u_ref, b_ref, tab_ref, c_ref, d_ref, x_ref, y_ref, bu, xs):
        j = pl.program_id(1)
        u = u_ref[...]
        bu[...] = _dot(u, b_ref[...])
        tabs = [tab_ref[k] for k in range(TAB_ROWS)]

        def group(r0, carry):
            rows = pl.ds(r0, SUBLANES)
            a, b = _scan_rows(bu[rows, :SCAN_COLS], bu[rows, SCAN_COLS:], tabs, carry, False)
            xs[rows, :SCAN_COLS] = a
            xs[rows, SCAN_COLS:] = b
            return (jnp.broadcast_to(a[SUBLANES - 1:, :], a.shape), jnp.broadcast_to(b[SUBLANES - 1:, :], b.shape))

        zero = jnp.zeros((SUBLANES, SCAN_COLS), F32)
        carry = (zero, zero)
        for r0 in _time_groups(seq, False):
            carry = group(r0, carry)
        span = SCAN_UNROLL * SUBLANES

        def groups(t, c):
            for k in range(SCAN_UNROLL):
                c = group(pl.multiple_of(t * span, span) + k * SUBLANES, c)
            return c

        lax.fori_loop(0, seq // span, groups, carry)
        x16 = xs[...].astype(BF16)
        x_ref[...] = x16
        contrib = _dot(x16, c_ref[...])

        @pl.when(j % 2 == 0)
        def _():
            y_ref[...] = contrib + d_ref[...] * u.astype(F32)

        @pl.when(j % 2 == 1)
        def _():
            y_ref[...] += contrib

    return _pcall(
        body, name=name, grid=(n_b, N_SCAN_BLK),
        in_specs=[pl.BlockSpec((None, n_l, LANES), lambda b, j: (b, 0, u_blk + j // 2)),
                  pl.BlockSpec((LANES, 2 * SCAN_COLS), lambda b, j: (j // 2, j)),
                  pl.BlockSpec((TAB_ROWS, SUBLANES, SCAN_COLS), lambda b, j: (0, 0, j)),
                  pl.BlockSpec((2 * SCAN_COLS, LANES), lambda b, j: (j, j // 2)),
                  pl.BlockSpec((1, LANES), lambda b, j: (0, j // 2))],
        out_specs=[pl.BlockSpec((None, n_l, 2 * SCAN_COLS), lambda b, j: (b, 0, j)),
                   pl.BlockSpec((None, n_l, LANES), lambda b, j: (b, 0, j // 2))],
        out_shape=[jax.ShapeDtypeStruct((n_b, n_l, 2 * N_STATES), BF16),
                   jax.ShapeDtypeStruct((n_b, n_l, SSM_WIDTH), F32)],
        scratch_shapes=[pltpu.VMEM((n_l, 2 * SCAN_COLS), F32)] * 2,
        compiler_params=_cp(("arbitrary", "arbitrary"), VMEM_BIG),
    )(proj3, b_comb, tabf, c_comb, dvec)


def ssm_backward_scan(dyraw3, xs3, dproj3, c_comb_t, tabr, b_comb_t, dvec, seq, name):
    n_b, n_l, _ = xs3.shape
    u_blk = (D_MODEL + 2 * KV_WIDTH) // LANES

    def body(dy_ref, x_ref, _, c_ref, tab_ref, b_ref, d_ref, du_ref, g_ref, dlr_ref, dli_ref, dx, gs, xs, du_acc):
        j = pl.program_id(1)
        dy = dy_ref[...]
        dx[...] = _dot(dy, c_ref[...])
        xs[...] = x_ref[...].astype(F32)
        tabs = [tab_ref[k] for k in range(TAB_ROWS)]
        last_row = lax.broadcasted_iota(jnp.int32, (SUBLANES, SCAN_COLS), 0) == SUBLANES - 1

        def group(r0, state):
            cr, ci, acc_r, acc_i = state
            rows = pl.ds(r0, SUBLANES)
            a, b = _scan_rows(dx[rows, :SCAN_COLS], dx[rows, SCAN_COLS:], tabs, (cr, ci), True)
            gs[rows, :SCAN_COLS] = a
            gs[rows, SCAN_COLS:] = b
            na = jnp.where(last_row, cr, pltpu.roll(a, SUBLANES - 1, axis=0))
            nb = jnp.where(last_row, ci, pltpu.roll(b, SUBLANES - 1, axis=0))
            xa, xb = xs[rows, :SCAN_COLS], xs[rows, SCAN_COLS:]
            return (jnp.broadcast_to(a[:1, :], a.shape), jnp.broadcast_to(b[:1, :], b.shape),
                    acc_r + na * xa + nb * xb, acc_i + nb * xa - na * xb)

        zero = jnp.zeros((SUBLANES, SCAN_COLS), F32)
        span = SCAN_UNROLL * SUBLANES
        n_spans = seq // span

        def groups(t, s):
            for k in reversed(range(SCAN_UNROLL)):
                s = group(pl.multiple_of((n_spans - 1 - t) * span, span) + k * SUBLANES, s)
            return s

        state = lax.fori_loop(0, n_spans, groups, (zero, zero, zero, zero))
        for r0 in _time_groups(seq, True):
            state = group(r0, state)
        dlr_ref[...] = state[2]
        dli_ref[...] = state[3]
        g16 = gs[...].astype(BF16)
        g_ref[...] = g16
        contrib = _dot(g16, b_ref[...])

        @pl.when(j % 2 == 0)
        def _():
            du_acc[...] = contrib + d_ref[...] * dy.astype(F32)

        @pl.when(j % 2 == 1)
        def _():
            du_ref[...] = (du_acc[...] + contrib).astype(BF16)

    state_blk = pl.BlockSpec((None, n_l, 2 * SCAN_COLS), lambda b, j: (b, 0, j))
    dl_blk = pl.BlockSpec((None, SUBLANES, SCAN_COLS), lambda b, j: (b, 0, j))
    return _pcall(
        body, name=name, grid=(n_b, N_SCAN_BLK),
        in_specs=[pl.BlockSpec((None, n_l, LANES), lambda b, j: (b, 0, j // 2)), state_blk,
                  pl.BlockSpec(memory_space=pl.ANY),
                  pl.BlockSpec((LANES, 2 * SCAN_COLS), lambda b, j: (j // 2, j)),
                  pl.BlockSpec((TAB_ROWS, SUBLANES, SCAN_COLS), lambda b, j: (0, 0, j)),
                  pl.BlockSpec((2 * SCAN_COLS, LANES), lambda b, j: (j, j // 2)),
                  pl.BlockSpec((1, LANES), lambda b, j: (0, j // 2))],
        out_specs=[pl.BlockSpec((None, n_l, LANES), lambda b, j: (b, 0, u_blk + j // 2)), state_blk, dl_blk, dl_blk],
        out_shape=[jax.ShapeDtypeStruct(dproj3.shape, BF16), jax.ShapeDtypeStruct((n_b, n_l, 2 * N_STATES), BF16),
                   jax.ShapeDtypeStruct((n_b, SUBLANES, N_STATES), F32), jax.ShapeDtypeStruct((n_b, SUBLANES, N_STATES), F32)],
        scratch_shapes=[pltpu.VMEM((n_l, 2 * SCAN_COLS), F32)] * 3 + [pltpu.VMEM((n_l, LANES), F32)],
        input_output_aliases={2: 0},
        compiler_params=_cp(("arbitrary", "arbitrary"), VMEM_BIG),
    )(dyraw3, xs3, dproj3, c_comb_t, tabr, b_comb_t, dvec)


def ssm_param_grads(proj, gs, xs, dyraw, tm, name):
    t_rows = proj.shape[0]
    ni = t_rows // tm
    u_blk = (D_MODEL + 2 * KV_WIDTH) // LANES
    width = 2 * SCAN_COLS

    def body(u_ref, g_ref, x_ref, dy_ref, db_ref, dc_ref, dd_ref):
        cb, i = pl.program_id(0), pl.program_id(1)
        u, dy = u_ref[...], dy_ref[...]
        _accumulate(db_ref, _dot_tn(u, g_ref[...]), i == 0)
        _accumulate(dc_ref, _dot_tn(x_ref[...], dy), i == 0)

        @pl.when(cb % 2 == 0)
        def _():
            _accumulate(dd_ref, jnp.sum(dy.astype(F32) * u.astype(F32), axis=0, keepdims=True), i == 0)

    return _pcall(
        body, name=name, grid=(N_SCAN_BLK, ni),
        in_specs=[pl.BlockSpec((tm, LANES), lambda cb, i: (i, u_blk + cb // 2)),
                  pl.BlockSpec((tm, width), lambda cb, i: (i, cb)),
                  pl.BlockSpec((tm, width), lambda cb, i: (i, cb)),
                  pl.BlockSpec((tm, LANES), lambda cb, i: (i, cb // 2))],
        out_specs=[pl.BlockSpec((None, LANES, width), lambda cb, i: (cb, 0, 0)),
                   pl.BlockSpec((None, width, LANES), lambda cb, i: (cb, 0, 0)),
                   pl.BlockSpec((1, LANES), lambda cb, i: (0, cb // 2))],
        out_shape=[jax.ShapeDtypeStruct((N_SCAN_BLK, LANES, width), F32),
                   jax.ShapeDtypeStruct((N_SCAN_BLK, width, LANES), F32), jax.ShapeDtypeStruct((1, SSM_WIDTH), F32)],
        compiler_params=_cp(("arbitrary", "arbitrary"), VMEM_BIG),
    )(proj, gs, xs, dyraw)


def sum_leading(x, name):
    def body(x_ref, o_ref):
        acc = x_ref[0]
        for k in range(1, x.shape[0]):
            acc = acc + x_ref[k]
        o_ref[...] = acc

    return _pcall(body, name=name, out_shape=jax.ShapeDtypeStruct(x.shape[1:], x.dtype))(x)


WEIGHTS = ['meta_tokens', 'ffn1_norm', 'ffn1_w1', 'ffn1_w3', 'ffn1_w2', 'mix_norm', 'w_in', 'attn_sinks', 'ssm_a_re',
           'ssm_a_im', 'ssm_log_step', 'ssm_b_re', 'ssm_b_im', 'ssm_c_re', 'ssm_c_im', 'ssm_d', 'ssm_glu_a', 'ssm_glu_b',
           'w_out', 'ffn2_norm', 'ffn2_w1', 'ffn2_w3', 'ffn2_w2', 'final_norm']
SHARDED = ['ffn1_w1', 'ffn1_w3', 'ffn1_w2', 'ffn2_w1', 'ffn2_w3', 'ffn2_w2', 'w_in', 'ssm_glu_a', 'ssm_glu_b', 'w_out']
REPLICATED = ['ffn1_norm', 'mix_norm', 'ffn2_norm', 'final_norm', 'attn_sinks', 'ssm_a_re', 'ssm_a_im', 'ssm_log_step',
              'ssm_b_re', 'ssm_b_im', 'ssm_c_re', 'ssm_c_im', 'ssm_d']
PACK_COLS = 1024


def _block_diag(blocks):
    g, r, c = blocks.shape
    eye = jnp.eye(g, dtype=blocks.dtype)
    return (blocks[:, :, None, :] * eye[:, None, :, None]).reshape(g * r, g * c)


def _scan_order(re, im):
    r = re.shape[0]
    return jnp.stack([re.reshape(r, N_SCAN_BLK, SCAN_COLS), im.reshape(r, N_SCAN_BLK, SCAN_COLS)], axis=2).reshape(r, 2 * N_STATES)


def _pack(arrays):
    parts = []
    for a in arrays:
        flat = a.reshape(-1)
        chunk = SUBLANES * PACK_COLS
        padded = -(-flat.shape[0] // chunk) * chunk
        parts.append(jnp.pad(flat, (0, padded - flat.shape[0])).reshape(-1, PACK_COLS))
    return jnp.concatenate(parts, axis=0)


def _unpack(packed, shapes):
    out, row = [], 0
    for shape in shapes:
        size = 1
        for s in shape:
            size *= s
        chunk = SUBLANES * PACK_COLS
        rows = -(-size // chunk) * SUBLANES
        out.append(packed[row:row + rows].reshape(-1)[:size].reshape(shape))
        row += rows
    return out


def kernel(x, meta_tokens, ffn1_norm, ffn1_w1, ffn1_w3, ffn1_w2, mix_norm, w_in, attn_sinks, ssm_a_re, ssm_a_im, ssm_log_step, ssm_b_re, ssm_b_im, ssm_c_re, ssm_c_im, ssm_d, ssm_glu_a, ssm_glu_b, w_out, ffn2_norm, ffn2_w1, ffn2_w3, ffn2_w2, final_norm, loss_target, m_meta_tokens, m_ffn1_norm, m_ffn1_w1, m_ffn1_w3, m_ffn1_w2, m_mix_norm, m_w_in, m_attn_sinks, m_ssm_a_re, m_ssm_a_im, m_ssm_log_step, m_ssm_b_re, m_ssm_b_im, m_ssm_c_re, m_ssm_c_im, m_ssm_d, m_ssm_glu_a, m_ssm_glu_b, m_w_out, m_ffn2_norm, m_ffn2_w1, m_ffn2_w3, m_ffn2_w2, m_final_norm, v_meta_tokens, v_ffn1_norm, v_ffn1_w1, v_ffn1_w3, v_ffn1_w2, v_mix_norm, v_w_in, v_attn_sinks, v_ssm_a_re, v_ssm_a_im, v_ssm_log_step, v_ssm_b_re, v_ssm_b_im, v_ssm_c_re, v_ssm_c_im, v_ssm_d, v_ssm_glu_a, v_ssm_glu_b, v_w_out, v_ffn2_norm, v_ffn2_w1, v_ffn2_w3, v_ffn2_w2, v_final_norm):
    given = dict(locals())
    w = {n: given[n] for n in WEIGHTS}
    m = {n: given["m_" + n] for n in WEIGHTS}
    v = {n: given["v_" + n] for n in WEIGHTS}

    n_b, seq, _ = x.shape
    n_l = seq + N_META
    t_rows = n_b * n_l
    tm = _row_tile(n_l, 688)
    px, py, pc = _my_place()
    me = 4 * px + 2 * py + pc

    glu = jnp.stack([ssm_glu_a[0], ssm_glu_b[0]]).astype(BF16)
    ffn_names = ['ffn1_w1', 'ffn1_w3', 'ffn1_w2', 'ffn2_w1', 'ffn2_w3', 'ffn2_w2']

    def hidden_on_rows(n, t):
        return t[0] if n.endswith('w2') else t[0].T

    def hidden_on_rows_back(n, t):
        return t[None] if n.endswith('w2') else t.T[None]

    me_idx = jnp.reshape(me, (1,)).astype(jnp.int32)
    first_names, later_names = ffn_names[:3], ffn_names[3:]
    *first, metag = all_gather_list(
        [hidden_on_rows(n, w[n]).astype(BF16) for n in first_names] + [meta_tokens], "ag_first")
    win_send, win_recv, win_shard, win_land, win_token = exchange_start(
        [w_in[0].astype(BF16)], first[0], True, "ag_w_in_start")
    later_shards = [hidden_on_rows(n, w[n]).astype(BF16) for n in later_names] + [glu, w_out[0].astype(BF16)]
    ag_send, ag_recv, later_shards, later_lands, ag_token = exchange_start(later_shards, win_token, True, "ag_later_start")
    full = {n: g.reshape(D_FF, D_MODEL) for n, g in zip(first_names, first)}
    meta_full = metag.transpose(1, 0, 2).reshape(N_META, D_MODEL)

    h0 = jnp.concatenate([x, jnp.broadcast_to(meta_full[None], (n_b, N_META, D_MODEL))], axis=1).reshape(t_rows, D_MODEL)
    target = jnp.concatenate([loss_target, jnp.zeros((n_b, N_META, D_MODEL), F32)], axis=1).reshape(t_rows, D_MODEL)
    final_g = final_norm.reshape(1, D_MODEL)

    ar = ssm_a_re.reshape(1, N_STATES)
    ai = ssm_a_im.reshape(1, N_STATES)
    ls = jnp.repeat(ssm_log_step.reshape(SSM_GROUPS), SSM_STATE).reshape(1, N_STATES)
    br_t = ssm_b_re[0].transpose(2, 0, 1).reshape(SSM_GROUP, N_STATES)
    bi_t = ssm_b_im[0].transpose(2, 0, 1).reshape(SSM_GROUP, N_STATES)
    bbr, bbi, tabf, tabr = ssm_prepare(ar, ai, ls, br_t, bi_t, "ssm_prepare")
    bbr_g = bbr.reshape(SSM_GROUP, SSM_GROUPS, SSM_STATE).transpose(1, 0, 2)
    bbi_g = bbi.reshape(SSM_GROUP, SSM_GROUPS, SSM_STATE).transpose(1, 0, 2)
    b_comb = _scan_order(_block_diag(bbr_g), _block_diag(bbi_g)).astype(BF16)
    c_comb_t = _scan_order(_block_diag(ssm_c_re[0]), -_block_diag(ssm_c_im[0])).astype(BF16)
    b_comb_t, c_comb = b_comb.T, c_comb_t.T

    ffn1_w = (full['ffn1_w1'], full['ffn1_w3'], full['ffn1_w2'])
    h1, hn1, a1, b1 = ffn_forward(h0, ffn1_norm, *ffn1_w, ag_token, tm, FF_FWD_COLS, "ffn1_fwd")
    (wing,) = exchange_wait(win_send, win_recv, win_shard, win_land, h1, True, "ag_w_in_wait")
    wing = lax.dynamic_update_slice_in_dim(wing, win_shard[0][None], me, axis=0)
    hnm, proj = mix_forward(h1, mix_norm, wing, tm, "mix_fwd")
    proj3 = proj.reshape(n_b, n_l, IN_WIDTH)
    attn3 = attention_forward(proj3, attn_sinks, seq, "attn_fwd")
    attn = attn3.reshape(t_rows, D_MODEL)
    xs3, yraw3 = ssm_forward_scan(proj3, b_comb, tabf, c_comb, ssm_d, seq, "ssm_fwd")
    yraw = yraw3.reshape(t_rows, SSM_WIDTH)
    later = exchange_wait(ag_send, ag_recv, later_shards, later_lands, yraw3, True, "ag_later_wait")
    later = [lax.dynamic_update_slice_in_dim(z, s[None], me, axis=0) for z, s in zip(later, later_shards)]
    for n, g in zip(later_names, later):
        full[n] = g.reshape(D_FF, D_MODEL)
    ffn2_w = (full['ffn2_w1'], full['ffn2_w3'], full['ffn2_w2'])
    glug, wog = later[len(later_names):]
    glu_a = glug[:, 0].transpose(1, 0, 2).reshape(SSM_WIDTH, D_MODEL)
    glu_b = glug[:, 1].transpose(1, 0, 2).reshape(SSM_WIDTH, D_MODEL)
    w_out_full = wog.reshape(D_MODEL, D_MODEL)
    h2 = merge_forward(h1, yraw, attn, proj, glu_a, glu_b, w_out_full, tm, "merge_fwd")
    h3, hn2, a2, b2 = ffn_forward(h2, ffn2_norm, *ffn2_w, ag_token, tm, FF_FWD_COLS, "ffn2_fwd")
    dh3, loss_part, g_final = final_loss_backward(h3, target, final_g, seq, tm, "loss_bwd")
    loss = lax.psum(loss_part[0, 0], ("x", "y", "c"))

    def blocked_ffn(d_w1t, d_w3t, d_w2):
        return tuple(t.reshape(N_DEV, FF_BLK, D_MODEL) for t in (d_w1t, d_w3t, d_w2))

    def blocked_cols(full_grad):
        r = full_grad.shape[0]
        return full_grad.reshape(r, N_DEV, full_grad.shape[1] // N_DEV).transpose(1, 0, 2).astype(BF16)

    early = {}

    def start_reduce(names, tag):
        srcs = [dw[n] for n in names]
        send, recv, srcs, lands, token = exchange_start(srcs, srcs[0], False, "rs_" + tag + "_start")
        early[tag] = (names, send, recv, srcs, lands)
        return token

    dw = {}
    da2, db2, dh3_half = ffn_backward_hidden(dh3, a2, b2, ffn2_w[2], g_final, tm, "ffn2_bwd_hid")
    dw['ffn2_w1'], dw['ffn2_w3'], dw['ffn2_w2'] = blocked_ffn(
        *ffn_backward_weights(hn2, dh3_half, a2, b2, da2, db2, n_l, FF_BWD_COLS, "ffn2_bwd_w"))
    token = start_reduce(later_names, "ffn2")
    dh2, g_ffn2_norm = ffn_backward_input(dh3, h2, ffn2_norm, da2, db2, ffn2_w[0], ffn2_w[1], token, tm, "ffn2_bwd_in")
    dattn, dyraw, dproj, *for_weights = merge_backward(dh2, yraw, attn, proj, glu_a, glu_b, w_out_full, token, tm,
                                                       "merge_bwd")
    d_wo, d_ga, d_gb = merge_backward_weights(*for_weights, tm, "merge_bwd_w")
    dw['ssm_glu_a'] = blocked_cols(d_ga)
    dw['ssm_glu_b'] = blocked_cols(d_gb)
    dw['w_out'] = d_wo.reshape(N_DEV, D_MODEL // N_DEV, D_MODEL).astype(BF16)
    token = start_reduce(['ssm_glu_a', 'ssm_glu_b', 'w_out'], "mix")
    dproj3 = dproj.reshape(n_b, n_l, IN_WIDTH)
    dproj3, dsink_p = attention_backward(proj3, dattn.reshape(n_b, n_l, D_MODEL), dproj3, attn_sinks, token, seq,
                                         "attn_bwd")
    dproj3, gs3, dlr_p, dli_p = ssm_backward_scan(
        dyraw.reshape(n_b, n_l, SSM_WIDTH), xs3, dproj3, c_comb_t, tabr, b_comb_t, ssm_d, seq, "ssm_bwd")
    dproj = dproj3.reshape(t_rows, IN_WIDTH)
    d_bd, d_cd, g_d = ssm_param_grads(proj, gs3.reshape(t_rows, 2 * N_STATES), xs3.reshape(t_rows, 2 * N_STATES),
                                      dyraw, n_l, "ssm_bwd_w")
    dh1, g_mix_norm = mix_backward_act(dh2, h1, mix_norm, dproj, wing, tm, "mix_bwd_act")
    dw['w_in'] = mix_backward_weights(hnm, dproj, n_l, "mix_bwd_w")
    token = start_reduce(['w_in'], "w_in")
    da1, db1, dh1_half = ffn_backward_hidden(dh1, a1, b1, ffn1_w[2], token, tm, "ffn1_bwd_hid")
    dw['ffn1_w1'], dw['ffn1_w3'], dw['ffn1_w2'] = blocked_ffn(
        *ffn_backward_weights(hn1, dh1_half, a1, b1, da1, db1, n_l, FF_BWD_COLS, "ffn1_bwd_w"))
    token = start_reduce(first_names, "ffn1")
    dh0, g_ffn1_norm = ffn_backward_input(dh1, h0, ffn1_norm, da1, db1, ffn1_w[0], ffn1_w[1], token, tm, "ffn1_bwd_in")
    dh0_3 = dh0.reshape(n_b, n_l, D_MODEL)
    grad_x = dh0_3[:, :seq]
    g_meta = sum_leading(dh0_3[:, seq:], "meta_sum")

    groups_per_blk = SCAN_COLS // SSM_STATE
    half = ((jnp.arange(N_SCAN_BLK) % 2)[:, None] == jnp.arange(2)[None, :]).astype(F32)
    eye = jnp.eye(groups_per_blk, dtype=F32)

    def group_blocks(part, channels_first):
        if channels_first:
            t = jnp.sum(part.reshape(N_SCAN_BLK, 2, LANES // 2, SCAN_COLS) * half[:, :, None, None], axis=1)
            t = t.reshape(N_SCAN_BLK, groups_per_blk, SSM_GROUP, groups_per_blk, SSM_STATE)
            t = jnp.sum(t * eye[None, :, None, :, None], axis=3)
            return t.reshape(SSM_GROUPS, SSM_GROUP, SSM_STATE)
        t = jnp.sum(part.reshape(N_SCAN_BLK, SCAN_COLS, 2, LANES // 2) * half[:, None, :, None], axis=2)
        t = t.reshape(N_SCAN_BLK, groups_per_blk, SSM_STATE, groups_per_blk, SSM_GROUP)
        t = jnp.sum(t * eye[None, :, None, :, None], axis=3)
        return t.reshape(SSM_GROUPS, SSM_STATE, SSM_GROUP).transpose(0, 2, 1)

    dbbr = group_blocks(d_bd[:, :, :SCAN_COLS], True).transpose(1, 0, 2).reshape(SSM_GROUP, N_STATES)
    dbbi = group_blocks(d_bd[:, :, SCAN_COLS:], True).transpose(1, 0, 2).reshape(SSM_GROUP, N_STATES)
    g_c_re = group_blocks(d_cd[:, :SCAN_COLS, :], False)[None]
    g_c_im = -group_blocks(d_cd[:, SCAN_COLS:, :], False)[None]
    group_sum = (jnp.arange(N_STATES)[:, None] // SSM_STATE == jnp.arange(LANES)[None, :]).astype(F32)
    g_ar, g_ai, g_ls, g_br, g_bi = ssm_param_backward(
        ar, ai, ls, br_t, bi_t, dlr_p.reshape(n_b * SUBLANES, N_STATES), dli_p.reshape(n_b * SUBLANES, N_STATES),
        dbbr, dbbi, group_sum, "ssm_bwd_params")
    g_sinks = sum_leading(dsink_p, "sink_sum")[0:1, :N_KV_HEADS * Q_PER_KV]

    small = {
        'ffn1_norm': g_ffn1_norm, 'mix_norm': g_mix_norm, 'ffn2_norm': g_ffn2_norm, 'final_norm': g_final.reshape(D_MODEL),
        'attn_sinks': g_sinks, 'ssm_a_re': g_ar.reshape(1, SSM_GROUPS, SSM_STATE), 'ssm_a_im': g_ai.reshape(1, SSM_GROUPS, SSM_STATE),
        'ssm_log_step': g_ls[:, :SSM_GROUPS],
        'ssm_b_re': g_br.reshape(SSM_GROUP, SSM_GROUPS, SSM_STATE).transpose(1, 2, 0)[None],
        'ssm_b_im': g_bi.reshape(SSM_GROUP, SSM_GROUPS, SSM_STATE).transpose(1, 2, 0)[None],
        'ssm_c_re': g_c_re, 'ssm_c_im': g_c_im, 'ssm_d': g_d,
    }

    zeros_meta = jnp.zeros((N_META, D_MODEL), F32)
    packed_g = _pack([small[n] for n in REPLICATED] + [g_meta])
    (parts,) = all_gather_list([packed_g], "ag_small_grads")
    packed_out = adamw_small(parts, _pack([w[n] for n in REPLICATED] + [zeros_meta]),
                             _pack([m[n] for n in REPLICATED] + [zeros_meta]),
                             _pack([v[n] for n in REPLICATED] + [zeros_meta]), "adamw_small")
    shapes = [w[n].shape for n in REPLICATED] + [(N_META, D_MODEL)]
    grads, deltas, new_m, new_v = {}, {}, {}, {}
    unpacked = [_unpack(p, shapes) for p in packed_out]
    for k, n in enumerate(REPLICATED):
        grads[n], deltas[n], new_m[n], new_v[n] = (u[k] for u in unpacked)
    g_meta_full = unpacked[0][-1]
    grads['meta_tokens'] = lax.dynamic_index_in_dim(
        g_meta_full.reshape(N_META, N_DEV, D_MODEL // N_DEV), me, axis=1, keepdims=False)
    deltas['meta_tokens'], new_m['meta_tokens'], new_v['meta_tokens'] = adamw_plain(
        grads['meta_tokens'], w['meta_tokens'], m['meta_tokens'], v['meta_tokens'], "adamw_meta")

    def views(n):
        if n in ffn_names:
            return functools.partial(hidden_on_rows, n), functools.partial(hidden_on_rows_back, n)
        return (lambda t: t[0]), (lambda t: t[None])

    previous = packed_out[0]
    for tag, (names, send, recv, srcs, lands) in early.items():
        lands = exchange_wait(send, recv, srcs, lands, previous, False, "rs_" + tag + "_wait")
        for n, g, land in zip(names, srcs, lands):
            two_d, back = views(n)
            out = adamw_exchanged(me_idx, g, land, two_d(w[n]), two_d(m[n]), two_d(v[n]), "adamw_" + n)
            grads[n], deltas[n], new_m[n], new_v[n] = (back(o) for o in out)
            previous = out[1]

    return (loss, grad_x, *[grads[n] for n in WEIGHTS], *[deltas[n] for n in WEIGHTS],
            *[new_m[n] for n in WEIGHTS], *[new_v[n] for n in WEIGHTS])
```

```python
import functools

import jax
import jax.numpy as jnp
from jax import lax
from jax.experimental import pallas as pl
from jax.experimental.pallas import tpu as pltpu

F32 = jnp.float32
BF16 = jnp.bfloat16
MESH = pl.DeviceIdType.MESH

N_DEV = 8
D_MODEL = 1024
N_META = 16
HEAD_DIM = 64
N_KV_HEADS = 4
Q_PER_KV = 4
BLOCK = 128
KV_WIDTH = N_KV_HEADS * HEAD_DIM
SSM_GROUP = 16
SSM_WIDTH = 512
SSM_GROUPS = 32
SSM_STATE = 64
N_STATES = SSM_GROUPS * SSM_STATE
D_FF = 2816
FF_BLK = D_FF // N_DEV
IN_WIDTH = 4096
IN_BLK = IN_WIDTH // N_DEV
NORM_EPS = 1e-6
NEG_INF = -1e30
SCAN_COLS = 256
N_SCAN_BLK = N_STATES // SCAN_COLS
SUBLANES = 8
LANES = 128
MXU_WIDTH = 256
FF_BWD_COLS = MXU_WIDTH

ADAM_LR = 0.001
ADAM_B1 = 0.9
ADAM_B2 = 0.999
ADAM_EPS = 1e-08
ADAM_WD = 0.01
ADAM_STEP = 10

VMEM_BIG = 56 * 1024 * 1024


def _cp(sem=None, vmem=None):
    kw = {}
    if sem is not None:
        kw["dimension_semantics"] = sem
    if vmem is not None:
        kw["vmem_limit_bytes"] = vmem
    return pltpu.CompilerParams(**kw)


def _pcall(body, **kw):
    return pl.pallas_call(body, **kw)


def _dot(a, b):
    return jnp.dot(a, b, preferred_element_type=F32)


def _dot_nt(a, b):
    return lax.dot_general(a, b, (((1,), (1,)), ((), ())), preferred_element_type=F32)


def _dot_tn(a, b):
    return lax.dot_general(a, b, (((0,), (0,)), ((), ())), preferred_element_type=F32)


def _sigmoid(x):
    return 1.0 / (1.0 + jnp.exp(-x))


def _row_tile(rows, cap):
    best = None
    for t in range(16, min(rows, cap) + 1, 16):
        if rows % t == 0:
            best = t
    assert best is not None, rows
    return best


def _my_place():
    return lax.axis_index("x"), lax.axis_index("y"), lax.axis_index("c")


def all_gather_list(shards, name):
    n = len(shards)

    def body(*refs):
        ins, outs = refs[:n], refs[n:2 * n]
        send_sems, recv_sems, local_sems = refs[2 * n:]
        x, y, c = _my_place()
        me, sibling = (x, y, c), (x, y, 1 - c)
        chips = [(1 - x, y), (x, 1 - y), (1 - x, 1 - y)]

        def blk(a, px, py, pc):
            return outs[a].at[4 * px + 2 * py + pc]

        def copy(a, k, block, to, src=None):
            return pltpu.make_async_remote_copy(
                src_ref=blk(a, *block) if src is None else src, dst_ref=blk(a, *block),
                send_sem=send_sems.at[a * 7 + k], recv_sem=recv_sems.at[a * 7 + k],
                device_id=to, device_id_type=MESH)

        mine = [pltpu.make_async_copy(ins[a], blk(a, *me), local_sems.at[a]) for a in range(n)]
        for cp in mine:
            cp.start()
        first = []
        for a in range(n):
            first.append(copy(a, 0, me, sibling, src=ins[a]))
            first += [copy(a, 1 + j, me, (*chip, c), src=ins[a]) for j, chip in enumerate(chips)]
        for cp in first:
            cp.start()
        passed = []
        for j, chip in enumerate(chips):
            for a in range(n):
                copy(a, 1 + j, (*chip, c), me).wait_recv()
                cp = copy(a, 4 + j, (*chip, c), sibling)
                cp.start()
                passed.append(cp)
        for a in range(n):
            copy(a, 0, sibling, me).wait_recv()
            for j, chip in enumerate(chips):
                copy(a, 4 + j, (*chip, 1 - c), me).wait_recv()
        for cp in first + passed:
            cp.wait_send()
        for cp in mine:
            cp.wait()

    any_spec = pl.BlockSpec(memory_space=pl.ANY)
    return _pcall(
        body, name=name,
        out_shape=[jax.ShapeDtypeStruct((N_DEV,) + s.shape, s.dtype) for s in shards],
        in_specs=[any_spec] * n, out_specs=[any_spec] * n,
        scratch_shapes=[pltpu.SemaphoreType.DMA((7 * n,)), pltpu.SemaphoreType.DMA((7 * n,)),
                        pltpu.SemaphoreType.DMA((n,))],
    )(*shards)


HBM_SPEC = pl.BlockSpec(memory_space=pltpu.HBM)
SEM_SPEC = pl.BlockSpec(memory_space=pltpu.SEMAPHORE)
N_PEERS = N_DEV - 1


def _related(k):
    x, y, c = _my_place()
    px = 1 - x if k & 4 else x
    py = 1 - y if k & 2 else y
    pc = 1 - c if k & 1 else c
    return (px, py, pc), 4 * px + 2 * py + pc


def _exchange_copies(srcs, lands, send_sems, recv_sems, gather):
    x, y, c = _my_place()
    me = 4 * x + 2 * y + c
    copies = []
    for a, (src, land) in enumerate(zip(srcs, lands)):
        for k in range(1, N_DEV):
            peer, d = _related(k)
            copies.append(pltpu.make_async_remote_copy(
                src_ref=src if gather else src.at[d], dst_ref=land.at[me] if gather else land.at[k],
                send_sem=send_sems.at[a * N_PEERS + k - 1], recv_sem=recv_sems.at[a * N_PEERS + k - 1],
                device_id=peer, device_id_type=MESH))
    return copies


def exchange_start(srcs, after, gather, name):
    n = len(srcs)
    land_shapes = [((N_DEV,) + s.shape) if gather else s.shape for s in srcs]

    def body(*refs):
        send_sems, recv_sems = refs[2 * n + 1], refs[2 * n + 2]
        for cp in _exchange_copies(refs[:n], refs[n:2 * n], send_sems, recv_sems, gather):
            cp.start()
        token = refs[-1]
        token[...] = jnp.zeros_like(token)

    sems = pltpu.SemaphoreType.DMA((n * N_PEERS,))
    lands = [pltpu.with_memory_space_constraint(lax.empty(shape, s.dtype), pltpu.HBM) for shape, s in zip(land_shapes, srcs)]
    out = _pcall(
        body, name=name,
        out_shape=(sems, sems, *[pltpu.HBM(s.shape, s.dtype) for s in srcs],
                   *[pltpu.HBM(shape, s.dtype) for shape, s in zip(land_shapes, srcs)],
                   jax.ShapeDtypeStruct((SUBLANES, LANES), F32)),
        in_specs=[HBM_SPEC] * (2 * n) + [pl.BlockSpec(memory_space=pl.ANY)],
        out_specs=(SEM_SPEC, SEM_SPEC, *[HBM_SPEC] * (2 * n), pl.BlockSpec(memory_space=pltpu.VMEM)),
        input_output_aliases={i: 2 + i for i in range(2 * n)},
        compiler_params=pltpu.CompilerParams(has_side_effects=pltpu.SideEffectType.DATAFLOW_SIDE_EFFECTING),
    )(*[pltpu.with_memory_space_constraint(s, pltpu.HBM) for s in srcs], *lands, after)
    return out[0], out[1], list(out[2:2 + n]), list(out[2 + n:2 + 2 * n]), out[-1]


def exchange_wait(send_sems, recv_sems, srcs, lands, after, gather, name):
    n = len(srcs)

    def body(*refs):
        for cp in _exchange_copies(refs[:n], refs[n:2 * n], refs[2 * n], refs[2 * n + 1], gather):
            cp.wait_send()
            cp.wait_recv()

    out = _pcall(
        body, name=name,
        out_shape=(*[pltpu.HBM(s.shape, s.dtype) for s in srcs], *[pltpu.HBM(z.shape, z.dtype) for z in lands]),
        in_specs=[HBM_SPEC] * (2 * n) + [SEM_SPEC, SEM_SPEC, pl.BlockSpec(memory_space=pl.ANY)],
        out_specs=tuple([HBM_SPEC] * (2 * n)),
        input_output_aliases={i: i for i in range(2 * n)},
        compiler_params=pltpu.CompilerParams(has_side_effects=pltpu.SideEffectType.DATAFLOW_SIDE_EFFECTING),
    )(*srcs, *lands, send_sems, recv_sems, after)
    return list(out[n:])


def adamw_exchanged(me, g, land, w, m, v, name):
    rows, cols = w.shape
    tr = _row_tile(rows, 256)

    def body(me_ref, g_ref, land_ref, w_ref, m_ref, v_ref, go_ref, d_ref, mo_ref, vo_ref):
        grad = g_ref[...].astype(F32)
        for k in range(1, N_DEV):
            grad = grad + land_ref[k].astype(F32)
        delta, m_new, v_new = _adam_math(w_ref[...], grad, m_ref[...], v_ref[...])
        go_ref[...] = grad
        d_ref[...] = delta
        mo_ref[...] = m_new
        vo_ref[...] = v_new

    tile = pl.BlockSpec((tr, cols), lambda r, ix: (r, 0))
    out = jax.ShapeDtypeStruct((rows, cols), F32)
    return _pcall(
        body, name=name, out_shape=[out] * 4,
        grid_spec=pltpu.PrefetchScalarGridSpec(
            num_scalar_prefetch=1, grid=(rows // tr,),
            in_specs=[pl.BlockSpec((None, tr, cols), lambda r, ix: (ix[0], r, 0)),
                      pl.BlockSpec((N_DEV, tr, cols), lambda r, ix: (0, r, 0)), tile, tile, tile],
            out_specs=[tile] * 4),
        compiler_params=_cp(("arbitrary",)),
    )(me, g, land, w, m, v)


def _adam_math(w, g, m, v):
    m = ADAM_B1 * m + (1.0 - ADAM_B1) * g
    v = ADAM_B2 * v + (1.0 - ADAM_B2) * (g * g)
    m_hat = m / (1.0 - ADAM_B1 ** ADAM_STEP)
    v_hat = v / (1.0 - ADAM_B2 ** ADAM_STEP)
    delta = -ADAM_LR * (m_hat / (jnp.sqrt(v_hat) + ADAM_EPS) + ADAM_WD * w)
    return delta, m, v


def adamw_small(parts, w, m, v, name):
    _, rows, cols = parts.shape

    def body(p_ref, w_ref, m_ref, v_ref, go_ref, d_ref, mo_ref, vo_ref):
        grad = p_ref[0]
        for k in range(1, N_DEV):
            grad = grad + p_ref[k]
        delta, m_new, v_new = _adam_math(w_ref[...], grad, m_ref[...], v_ref[...])
        go_ref[...] = grad
        d_ref[...] = delta
        mo_ref[...] = m_new
        vo_ref[...] = v_new

    out = jax.ShapeDtypeStruct((rows, cols), F32)
    return _pcall(body, name=name, out_shape=[out] * 4, compiler_params=_cp(vmem=VMEM_BIG))(parts, w, m, v)


def adamw_plain(g, w, m, v, name):
    def body(g_ref, w_ref, m_ref, v_ref, d_ref, mo_ref, vo_ref):
        delta, m_new, v_new = _adam_math(w_ref[...], g_ref[...], m_ref[...], v_ref[...])
        d_ref[...] = delta
        mo_ref[...] = m_new
        vo_ref[...] = v_new

    out = jax.ShapeDtypeStruct(w.shape, F32)
    return _pcall(body, name=name, out_shape=[out] * 3)(g, w, m, v)


def _rms_fwd(x, g):
    r = lax.rsqrt(jnp.mean(x * x, axis=-1, keepdims=True) + NORM_EPS)
    return x * r * g


def _rms_bwd(x, g, dy):
    r = lax.rsqrt(jnp.mean(x * x, axis=-1, keepdims=True) + NORM_EPS)
    xh = x * r
    t = dy * g
    dx = r * (t - xh * jnp.mean(t * xh, axis=-1, keepdims=True))
    return dx, jnp.sum(dy * xh, axis=0, keepdims=True)


def _accumulate(ref, val, first):
    @pl.when(first)
    def _():
        ref[...] = val

    @pl.when(jnp.logical_not(first))
    def _():
        ref[...] += val


def _col_chunks(width):
    return [(c0, min(MXU_WIDTH, width - c0)) for c0 in range(0, width, MXU_WIDTH)]


ANY_SPEC = pl.BlockSpec(memory_space=pl.ANY)


def ffn_forward(h, norm, w1, w3, w2, after, tm, name):
    t_rows = h.shape[0]

    def body(h_ref, g_ref, w1_ref, w3_ref, w2_ref, _, out_ref, hn_ref, a_ref, b_ref, hid_ref):
        hn = _rms_fwd(h_ref[...], g_ref[...]).astype(BF16)
        hn_ref[...] = hn
        for c0, cw in _col_chunks(D_FF):
            a = _dot_nt(hn, w1_ref[c0:c0 + cw, :])
            b = _dot_nt(hn, w3_ref[c0:c0 + cw, :])
            a_ref[:, c0:c0 + cw] = a.astype(BF16)
            b_ref[:, c0:c0 + cw] = b.astype(BF16)
            hid_ref[:, c0:c0 + cw] = (a * _sigmoid(a) * b).astype(BF16)
        out_ref[...] = h_ref[...] + 0.5 * _dot(hid_ref[...], w2_ref[...])

    row = pl.BlockSpec((tm, D_MODEL), lambda i: (i, 0))
    hid_blk = pl.BlockSpec((tm, D_FF), lambda i: (i, 0))
    weight = _resident((D_FF, D_MODEL))
    return _pcall(
        body, name=name, grid=(t_rows // tm,),
        in_specs=[row, pl.BlockSpec((1, D_MODEL), lambda i: (0, 0)), weight, weight, weight, ANY_SPEC],
        out_specs=[row, row, hid_blk, hid_blk],
        out_shape=[jax.ShapeDtypeStruct((t_rows, D_MODEL), F32), jax.ShapeDtypeStruct((t_rows, D_MODEL), BF16),
                   jax.ShapeDtypeStruct((t_rows, D_FF), BF16), jax.ShapeDtypeStruct((t_rows, D_FF), BF16)],
        scratch_shapes=[pltpu.VMEM((tm, D_FF), BF16)],
        compiler_params=_cp(("arbitrary",), VMEM_BIG),
    )(h, norm, w1, w3, w2, after)


def _resident(shape):
    return pl.BlockSpec(shape, lambda *_: (0,) * len(shape), pipeline_mode=pl.Buffered(1))


def ffn_backward_hidden(dh, a, b, w2, after, tm, name):
    t_rows = dh.shape[0]

    def body(dh_ref, a_ref, b_ref, w2_ref, _, da_ref, db_ref, dhb_ref):
        dhb = (0.5 * dh_ref[...]).astype(BF16)
        dhb_ref[...] = dhb
        for c0, cw in _col_chunks(D_FF):
            dhid = _dot_nt(dhb, w2_ref[c0:c0 + cw, :])
            av = a_ref[:, c0:c0 + cw].astype(F32)
            bv = b_ref[:, c0:c0 + cw].astype(F32)
            s = _sigmoid(av)
            da_ref[:, c0:c0 + cw] = (dhid * bv * (s * (1.0 + av * (1.0 - s)))).astype(BF16)
            db_ref[:, c0:c0 + cw] = (dhid * (av * s)).astype(BF16)

    hid = pl.BlockSpec((tm, D_FF), lambda i: (i, 0))
    row = pl.BlockSpec((tm, D_MODEL), lambda i: (i, 0))
    return _pcall(
        body, name=name, grid=(t_rows // tm,),
        in_specs=[row, hid, hid, _resident((D_FF, D_MODEL)), ANY_SPEC],
        out_specs=[hid, hid, row],
        out_shape=[jax.ShapeDtypeStruct((t_rows, D_FF), BF16), jax.ShapeDtypeStruct((t_rows, D_FF), BF16),
                   jax.ShapeDtypeStruct((t_rows, D_MODEL), BF16)],
        compiler_params=_cp(("arbitrary",), VMEM_BIG),
    )(dh, a, b, w2, after)


def ffn_backward_input(dh, h, norm, da, db, w1, w3, after, tm, name):
    t_rows = h.shape[0]

    def body(dh_ref, h_ref, g_ref, da_ref, db_ref, w1_ref, w3_ref, _, dhin_ref, dg_ref):
        dhn = _dot(da_ref[...], w1_ref[...]) + _dot(db_ref[...], w3_ref[...])
        dx, dg = _rms_bwd(h_ref[...], g_ref[...], dhn)
        dhin_ref[...] = dh_ref[...] + dx
        _accumulate(dg_ref, dg, pl.program_id(0) == 0)

    row = pl.BlockSpec((tm, D_MODEL), lambda i: (i, 0))
    vec = pl.BlockSpec((1, D_MODEL), lambda i: (0, 0))
    hid = pl.BlockSpec((tm, D_FF), lambda i: (i, 0))
    return _pcall(
        body, name=name, grid=(t_rows // tm,),
        in_specs=[row, row, vec, hid, hid, _resident((D_FF, D_MODEL)), _resident((D_FF, D_MODEL)), ANY_SPEC],
        out_specs=[row, vec],
        out_shape=[jax.ShapeDtypeStruct((t_rows, D_MODEL), F32), jax.ShapeDtypeStruct((1, D_MODEL), F32)],
        compiler_params=_cp(("arbitrary",), VMEM_BIG),
    )(dh, h, norm, da, db, w1, w3, after)


def ffn_backward_weights(hn, dh, a, b, da, db, tm, tn, name):
    t_rows = hn.shape[0]
    ni = t_rows // tm
    kc = _row_tile(tm, 688)

    def body(hn_ref, dh_ref, a_ref, b_ref, da_ref, db_ref, dw1_ref, dw3_ref, dw2_ref, acc1, acc3, acc2):
        i = pl.program_id(1)
        parts = None
        for r0 in range(0, tm, kc):
            rows = slice(r0, r0 + kc)
            hn_v = hn_ref[rows, :]
            av = a_ref[rows, :].astype(F32)
            hid = (av * _sigmoid(av) * b_ref[rows, :].astype(F32)).astype(BF16)
            new = (_dot_tn(hn_v, da_ref[rows, :]), _dot_tn(hn_v, db_ref[rows, :]), _dot_tn(hid, dh_ref[rows, :]))
            parts = new if parts is None else tuple(p + q for p, q in zip(parts, new))
        _accumulate(acc1, parts[0], i == 0)
        _accumulate(acc3, parts[1], i == 0)
        _accumulate(acc2, parts[2], i == 0)

        @pl.when(i == ni - 1)
        def _():
            dw1_ref[...] = acc1[...].T.astype(BF16)
            dw3_ref[...] = acc3[...].T.astype(BF16)
            dw2_ref[...] = acc2[...].astype(BF16)

    row = pl.BlockSpec((tm, D_MODEL), lambda j, i: (i, 0))
    hid_blk = pl.BlockSpec((tm, tn), lambda j, i: (i, j))
    w_row = pl.BlockSpec((tn, D_MODEL), lambda j, i: (j, 0))
    out = jax.ShapeDtypeStruct((D_FF, D_MODEL), BF16)
    return _pcall(
        body, name=name, grid=(D_FF // tn, ni),
        in_specs=[row, row, hid_blk, hid_blk, hid_blk, hid_blk],
        out_specs=[w_row, w_row, w_row], out_shape=[out, out, out],
        scratch_shapes=[pltpu.VMEM((D_MODEL, tn), F32), pltpu.VMEM((D_MODEL, tn), F32), pltpu.VMEM((tn, D_MODEL), F32)],
        compiler_params=_cp(("arbitrary", "arbitrary"), VMEM_BIG),
    )(hn, dh, a, b, da, db)


def mix_forward(h, norm, wing, tm, name):
    t_rows = h.shape[0]

    def body(h_ref, g_ref, w_ref, hn_ref, p_ref):
        hn = _rms_fwd(h_ref[...], g_ref[...]).astype(BF16)
        hn_ref[...] = hn
        for j in range(N_DEV):
            p_ref[:, j * IN_BLK:(j + 1) * IN_BLK] = _dot(hn, w_ref[j]).astype(BF16)

    row = pl.BlockSpec((tm, D_MODEL), lambda i: (i, 0))
    return _pcall(
        body, name=name, grid=(t_rows // tm,),
        in_specs=[row, pl.BlockSpec((1, D_MODEL), lambda i: (0, 0)),
                  pl.BlockSpec((N_DEV, D_MODEL, IN_BLK), lambda i: (0, 0, 0))],
        out_specs=[row, pl.BlockSpec((tm, IN_WIDTH), lambda i: (i, 0))],
        out_shape=[jax.ShapeDtypeStruct((t_rows, D_MODEL), BF16), jax.ShapeDtypeStruct((t_rows, IN_WIDTH), BF16)],
        compiler_params=_cp(("arbitrary",), VMEM_BIG),
    )(h, norm, wing)


def mix_backward_act(dh, h, norm, dproj, wing, tm, name):
    t_rows = h.shape[0]
    per_step = 4
    nj = N_DEV // per_step

    def body(dh_ref, h_ref, g_ref, dp_ref, w_ref, dhin_ref, dg_ref, acc_ref):
        i, j = pl.program_id(0), pl.program_id(1)
        part = functools.reduce(
            lambda u, w: u + w, [_dot_nt(dp_ref[:, k * IN_BLK:(k + 1) * IN_BLK], w_ref[k]) for k in range(per_step)])
        _accumulate(acc_ref, part, j == 0)

        @pl.when(j == nj - 1)
        def _():
            dx, dg = _rms_bwd(h_ref[...], g_ref[...], acc_ref[...])
            dhin_ref[...] = dh_ref[...] + dx
            _accumulate(dg_ref, dg, i == 0)

    row = pl.BlockSpec((tm, D_MODEL), lambda i, j: (i, 0))
    vec = pl.BlockSpec((1, D_MODEL), lambda i, j: (0, 0))
    return _pcall(
        body, name=name, grid=(t_rows // tm, nj),
        in_specs=[row, row, vec, pl.BlockSpec((tm, per_step * IN_BLK), lambda i, j: (i, j)),
                  pl.BlockSpec((per_step, D_MODEL, IN_BLK), lambda i, j: (j, 0, 0))],
        out_specs=[row, vec],
        out_shape=[jax.ShapeDtypeStruct((t_rows, D_MODEL), F32), jax.ShapeDtypeStruct((1, D_MODEL), F32)],
        scratch_shapes=[pltpu.VMEM((tm, D_MODEL), F32)],
        compiler_params=_cp(("arbitrary", "arbitrary"), VMEM_BIG),
    )(dh, h, norm, dproj, wing)


def mix_backward_weights(hn, dproj, tm, name):
    t_rows = hn.shape[0]
    ni = t_rows // tm
    per_step = 2

    kc = _row_tile(tm, 688)

    def body(hn_ref, dp_ref, dw_ref, acc):
        i = pl.program_id(1)
        part = functools.reduce(lambda u, w: u + w, [_dot_tn(hn_ref[r0:r0 + kc, :], dp_ref[r0:r0 + kc, :])
                                                    for r0 in range(0, tm, kc)])
        _accumulate(acc, part, i == 0)

        @pl.when(i == ni - 1)
        def _():
            for k in range(per_step):
                dw_ref[k] = acc[:, k * IN_BLK:(k + 1) * IN_BLK].astype(BF16)

    return _pcall(
        body, name=name, grid=(N_DEV // per_step, ni),
        in_specs=[pl.BlockSpec((tm, D_MODEL), lambda j, i: (i, 0)),
                  pl.BlockSpec((tm, per_step * IN_BLK), lambda j, i: (i, j))],
        out_specs=pl.BlockSpec((per_step, D_MODEL, IN_BLK), lambda j, i: (j, 0, 0)),
        out_shape=jax.ShapeDtypeStruct((N_DEV, D_MODEL, IN_BLK), BF16),
        scratch_shapes=[pltpu.VMEM((D_MODEL, per_step * IN_BLK), F32)],
        compiler_params=_cp(("arbitrary", "arbitrary"), VMEM_BIG),
    )(hn, dproj)


GELU_C = 0.7978845608028654
GELU_K = 0.044715


def _gelu(x):
    return 0.5 * x * (1.0 + jnp.tanh(GELU_C * (x + GELU_K * (x * x * x))))


def _gelu_and_grad(x):
    th = jnp.tanh(GELU_C * (x + GELU_K * (x * x * x)))
    val = 0.5 * x * (1.0 + th)
    grad = 0.5 * (1.0 + th) + 0.5 * x * (1.0 - th * th) * (GELU_C * (1.0 + 3.0 * GELU_K * (x * x)))
    return val, grad


def merge_forward(h, yraw, attn, proj, glu_a, glu_b, w_out, tm, name):
    t_rows = h.shape[0]

    def body(h_ref, y_ref, at_ref, gate_ref, a_ref, b_ref, wo_ref, out_ref):
        y = _gelu(y_ref[...]).astype(BF16)
        ssm = _dot(y, a_ref[...]) * _sigmoid(_dot(y, b_ref[...]))
        ga = gate_ref[:, :D_MODEL].astype(F32)
        gs = gate_ref[:, D_MODEL:].astype(F32)
        merged = _sigmoid(ga) * at_ref[...].astype(F32) + _sigmoid(gs) * ssm
        out_ref[...] = h_ref[...] + _dot(merged.astype(BF16), wo_ref[...])

    row = pl.BlockSpec((tm, D_MODEL), lambda i: (i, 0))
    glu = pl.BlockSpec((SSM_WIDTH, D_MODEL), lambda i: (0, 0))
    return _pcall(
        body, name=name, grid=(t_rows // tm,),
        in_specs=[row, pl.BlockSpec((tm, SSM_WIDTH), lambda i: (i, 0)), row,
                  pl.BlockSpec((tm, 2 * D_MODEL), lambda i: (i, 1)), glu, glu,
                  pl.BlockSpec((D_MODEL, D_MODEL), lambda i: (0, 0))],
        out_specs=row, out_shape=jax.ShapeDtypeStruct((t_rows, D_MODEL), F32),
        compiler_params=_cp(("arbitrary",), VMEM_BIG),
    )(h, yraw, attn, proj, glu_a, glu_b, w_out)


def merge_backward(dh, yraw, attn, proj, glu_a, glu_b, w_out, after, tm, name):
    t_rows = dh.shape[0]

    def body(dh_ref, y_ref, at_ref, gate_ref, a_ref, b_ref, wo_ref, _,
             dat_ref, dy_ref, dgate_ref, d16_ref, mg_ref, y16_ref, dya_ref, dyb_ref):
        d16 = dh_ref[...].astype(BF16)
        d16_ref[...] = d16
        gel, dgel = _gelu_and_grad(y_ref[...].astype(F32))
        y16 = gel.astype(BF16)
        y16_ref[...] = y16
        dy = None
        for c0, cw in _col_chunks(D_MODEL):
            cols = slice(c0, c0 + cw)
            gcols = slice(D_MODEL + c0, D_MODEL + c0 + cw)
            dmerged = _dot_nt(d16, wo_ref[cols, :])
            ya = _dot(y16, a_ref[:, cols])
            sb = _sigmoid(_dot(y16, b_ref[:, cols]))
            ssm = ya * sb
            sa = _sigmoid(gate_ref[:, cols].astype(F32))
            ss = _sigmoid(gate_ref[:, gcols].astype(F32))
            attn_v = at_ref[:, cols].astype(F32)
            mg_ref[:, cols] = (sa * attn_v + ss * ssm).astype(BF16)
            dat_ref[:, cols] = (dmerged * sa).astype(BF16)
            dgate_ref[:, cols] = (dmerged * attn_v * sa * (1.0 - sa)).astype(BF16)
            dgate_ref[:, gcols] = (dmerged * ssm * ss * (1.0 - ss)).astype(BF16)
            dssm = dmerged * ss
            dya = (dssm * sb).astype(BF16)
            dyb = (dssm * ya * sb * (1.0 - sb)).astype(BF16)
            dya_ref[:, cols] = dya
            dyb_ref[:, cols] = dyb
            part = _dot_nt(dya, a_ref[:, cols]) + _dot_nt(dyb, b_ref[:, cols])
            dy = part if dy is None else dy + part
        dy_ref[...] = (dy * dgel).astype(BF16)

    row = pl.BlockSpec((tm, D_MODEL), lambda i: (i, 0))
    ssm_row = pl.BlockSpec((tm, SSM_WIDTH), lambda i: (i, 0))
    gates = pl.BlockSpec((tm, 2 * D_MODEL), lambda i: (i, 1))
    wide = jax.ShapeDtypeStruct((t_rows, D_MODEL), BF16)
    narrow = jax.ShapeDtypeStruct((t_rows, SSM_WIDTH), BF16)
    return _pcall(
        body, name=name, grid=(t_rows // tm,),
        in_specs=[row, ssm_row, row, gates, _resident((SSM_WIDTH, D_MODEL)), _resident((SSM_WIDTH, D_MODEL)),
                  _resident((D_MODEL, D_MODEL)), ANY_SPEC],
        out_specs=[row, ssm_row, gates, row, row, ssm_row, row, row],
        out_shape=[wide, narrow, jax.ShapeDtypeStruct((t_rows, IN_WIDTH), BF16), wide, wide, narrow, wide, wide],
        compiler_params=_cp(("arbitrary",), VMEM_BIG),
    )(dh, yraw, attn, proj, glu_a, glu_b, w_out, after)


def merge_backward_weights(d16, merged, y16, dya, dyb, tm, name):
    t_rows = d16.shape[0]

    def body(d_ref, mg_ref, y_ref, dya_ref, dyb_ref, dwo_ref, da_ref, db_ref):
        first = pl.program_id(0) == 0
        y16 = y_ref[...]
        _accumulate(dwo_ref, _dot_tn(mg_ref[...], d_ref[...]), first)
        _accumulate(da_ref, _dot_tn(y16, dya_ref[...]), first)
        _accumulate(db_ref, _dot_tn(y16, dyb_ref[...]), first)

    row = pl.BlockSpec((tm, D_MODEL), lambda i: (i, 0))
    ssm_row = pl.BlockSpec((tm, SSM_WIDTH), lambda i: (i, 0))
    glu = pl.BlockSpec((SSM_WIDTH, D_MODEL), lambda i: (0, 0))
    wo = pl.BlockSpec((D_MODEL, D_MODEL), lambda i: (0, 0))
    return _pcall(
        body, name=name, grid=(t_rows // tm,),
        in_specs=[row, row, ssm_row, row, row], out_specs=[wo, glu, glu],
        out_shape=[jax.ShapeDtypeStruct((D_MODEL, D_MODEL), F32), jax.ShapeDtypeStruct((SSM_WIDTH, D_MODEL), F32),
                   jax.ShapeDtypeStruct((SSM_WIDTH, D_MODEL), F32)],
        compiler_params=_cp(("arbitrary",), VMEM_BIG),
    )(d16, merged, y16, dya, dyb)


def final_loss_backward(h, target, norm, seq, tm, name):
    t_rows = h.shape[0]
    tiles_per_example = (seq + N_META) // tm

    def body(h_ref, t_ref, g_ref, dh_ref, loss_ref, dg_ref):
        i = pl.program_id(0)
        x = h_ref[...]
        g = g_ref[...]
        r = lax.rsqrt(jnp.mean(x * x, axis=-1, keepdims=True) + NORM_EPS)
        xh = x * r
        pos = lax.broadcasted_iota(jnp.int32, (tm, 1), 0) + (i % tiles_per_example) * tm
        diff = jnp.where(pos < seq, xh * g - t_ref[...], 0.0)
        part = 0.5 * jnp.sum(jnp.sum(diff * diff, axis=-1, keepdims=True), axis=0, keepdims=True) / D_MODEL
        dy = diff / D_MODEL
        t = dy * g
        dh_ref[...] = r * (t - xh * jnp.mean(t * xh, axis=-1, keepdims=True))
        _accumulate(loss_ref, jnp.broadcast_to(part, (1, LANES)), i == 0)
        _accumulate(dg_ref, jnp.sum(dy * xh, axis=0, keepdims=True), i == 0)

    row = pl.BlockSpec((tm, D_MODEL), lambda i: (i, 0))
    vec = pl.BlockSpec((1, D_MODEL), lambda i: (0, 0))
    return _pcall(
        body, name=name, grid=(t_rows // tm,),
        in_specs=[row, row, vec],
        out_specs=[row, pl.BlockSpec((1, LANES), lambda i: (0, 0)), vec],
        out_shape=[jax.ShapeDtypeStruct((t_rows, D_MODEL), F32), jax.ShapeDtypeStruct((1, LANES), F32),
                   jax.ShapeDtypeStruct((1, D_MODEL), F32)],
        compiler_params=_cp(("arbitrary",), VMEM_BIG),
    )(h, target, norm)


ATTN_SCALE = HEAD_DIM ** -0.5
STACK_HEADS = (0, 2, 1, 3)


def _lane_half(shape, hf):
    lane = lax.broadcasted_iota(jnp.int32, shape, 1)
    return (lane < HEAD_DIM) if hf == 0 else (lane >= HEAD_DIM)


def _kv_variants(ref, rows, kh):
    tile = kh // 2
    t = ref[rows, tile * LANES:(tile + 1) * LANES].astype(F32)
    swapped = pltpu.roll(t, HEAD_DIM, axis=1)
    at_low, at_high = (t, swapped) if kh % 2 == 0 else (swapped, t)
    lo = jnp.where(_lane_half(t.shape, 0), at_low, 0.0).astype(BF16)
    hi = jnp.where(_lane_half(t.shape, 1), at_high, 0.0).astype(BF16)
    return lo, hi


def _to_kv_lanes(lo, hi, kh):
    lo = jnp.where(_lane_half(lo.shape, 0), lo, 0.0)
    hi = jnp.where(_lane_half(hi.shape, 1), hi, 0.0)
    if kh % 2 == 0:
        return lo + pltpu.roll(hi, HEAD_DIM, axis=1)
    return pltpu.roll(lo, HEAD_DIM, axis=1) + hi


def _stacked(ref, rows, kh):
    col = kh * 2 * LANES
    return jnp.concatenate([ref[rows, col:col + LANES], ref[rows, col + LANES:col + 2 * LANES]], axis=0)


def _sink_column(sink_ref, kh, nq):
    row = lax.broadcasted_iota(jnp.int32, (4 * nq, 1), 0)
    col = jnp.zeros((4 * nq, 1), F32)
    for quarter, g in enumerate(STACK_HEADS):
        col = jnp.where(row // nq == quarter, sink_ref[0, kh * Q_PER_KV + g], col)
    return col


def _softmax_parts(qs, key_tiles, masks, sink):
    scores = []
    for (k_lo, k_hi), mask in zip(key_tiles, masks):
        s = jnp.concatenate([_dot_nt(qs, k_lo), _dot_nt(qs, k_hi)], axis=0) * ATTN_SCALE
        scores.append(s if mask is None else jnp.where(mask, s, NEG_INF))
    m = functools.reduce(jnp.maximum, [jnp.max(s, axis=-1, keepdims=True) for s in scores])
    m = jnp.maximum(m, sink)
    probs = [jnp.exp(s - m) for s in scores]
    e_sink = jnp.exp(sink - m)
    den = _row_sums(probs) + e_sink
    return probs, 1.0 / den, e_sink


def _row_sums(tiles):
    total = functools.reduce(
        lambda u, w: u + w, [_dot(t.astype(BF16), jnp.ones((t.shape[-1], LANES), BF16)) for t in tiles])
    return total[:, :1]


def _band_mask(nq, first):
    keys = BLOCK if first else 2 * BLOCK
    qi = lax.broadcasted_iota(jnp.int32, (4 * nq, keys), 0) % nq
    kj = lax.broadcasted_iota(jnp.int32, (4 * nq, keys), 1)
    if first:
        return kj <= qi
    return jnp.logical_and(kj > qi, kj <= qi + BLOCK)


def _meta_mask():
    qi = lax.broadcasted_iota(jnp.int32, (4 * N_META, N_META), 0) % N_META
    kj = lax.broadcasted_iota(jnp.int32, (4 * N_META, N_META), 1)
    return kj <= qi


def _attention_schedule(seq, queries, carry):
    meta_rows = pl.ds(seq, N_META)
    carry = queries(pl.ds(0, BLOCK), BLOCK, [pl.ds(0, BLOCK), meta_rows], [_band_mask(BLOCK, True), None], carry)

    def block(n, c):
        r0 = pl.multiple_of(n * BLOCK, BLOCK)
        p0 = pl.multiple_of((n - 1) * BLOCK, BLOCK)
        return queries(pl.ds(r0, BLOCK), BLOCK, [pl.ds(p0, 2 * BLOCK), meta_rows], [_band_mask(BLOCK, False), None], c)

    carry = lax.fori_loop(1, seq // BLOCK, block, carry)
    return queries(meta_rows, N_META, [meta_rows], [_meta_mask()], carry)


def attention_forward(proj3, sinks, seq, name):
    n_b, n_l, _ = proj3.shape

    def body(sink_ref, q_ref, k_ref, v_ref, o_ref):
        def queries(q_rows, nq, key_rows, masks, carry):
            for kh in range(N_KV_HEADS):
                ks = [_kv_variants(k_ref, r, kh) for r in key_rows]
                vs = [_kv_variants(v_ref, r, kh) for r in key_rows]
                qs = _stacked(q_ref, q_rows, kh)
                probs, inv, _ = _softmax_parts(qs, ks, masks, _sink_column(sink_ref, kh, nq))
                probs = [p.astype(BF16) for p in probs]
                o_lo = functools.reduce(lambda u, w: u + w, [_dot(p[:2 * nq], v_lo) for p, (v_lo, _) in zip(probs, vs)])
                o_hi = functools.reduce(lambda u, w: u + w, [_dot(p[2 * nq:], v_hi) for p, (_, v_hi) in zip(probs, vs)])
                out = (o_lo * inv[:2 * nq] + o_hi * inv[2 * nq:]).astype(BF16)
                col = kh * 2 * LANES
                o_ref[q_rows, col:col + LANES] = out[:nq]
                o_ref[q_rows, col + LANES:col + 2 * LANES] = out[nq:]
            return carry

        _attention_schedule(seq, queries, 0)

    return _pcall(
        body, name=name, grid=(n_b,),
        in_specs=[pl.BlockSpec(memory_space=pltpu.SMEM),
                  pl.BlockSpec((None, n_l, D_MODEL), lambda b: (b, 0, 0)),
                  pl.BlockSpec((None, n_l, KV_WIDTH), lambda b: (b, 0, D_MODEL // KV_WIDTH)),
                  pl.BlockSpec((None, n_l, KV_WIDTH), lambda b: (b, 0, D_MODEL // KV_WIDTH + 1))],
        out_specs=pl.BlockSpec((None, n_l, D_MODEL), lambda b: (b, 0, 0)),
        out_shape=jax.ShapeDtypeStruct((n_b, n_l, D_MODEL), BF16),
        compiler_params=_cp(("arbitrary",), VMEM_BIG),
    )(sinks, proj3, proj3, proj3)


def attention_backward(proj3, dattn3, dproj3, sinks, after, seq, name):
    n_b, n_l, _ = proj3.shape
    qkv_width = D_MODEL + 2 * KV_WIDTH

    def body(sink_ref, q_ref, k_ref, v_ref, do_ref, _, __, dqkv_ref, dsink_ref, dk_ref, dv_ref):
        dk_ref[...] = jnp.zeros_like(dk_ref)
        dv_ref[...] = jnp.zeros_like(dv_ref)
        sub = lax.broadcasted_iota(jnp.int32, (SUBLANES, LANES), 0)
        lane = lax.broadcasted_iota(jnp.int32, (SUBLANES, LANES), 1)

        def queries(q_rows, nq, key_rows, masks, dsink):
            for kh in range(N_KV_HEADS):
                ks = [_kv_variants(k_ref, r, kh) for r in key_rows]
                vs = [_kv_variants(v_ref, r, kh) for r in key_rows]
                qs = _stacked(q_ref, q_rows, kh)
                dos = _stacked(do_ref, q_rows, kh)
                probs, inv, e_sink = _softmax_parts(qs, ks, masks, _sink_column(sink_ref, kh, nq))
                probs = [p * inv for p in probs]
                dps = [jnp.concatenate([_dot_nt(dos, v_lo), _dot_nt(dos, v_hi)], axis=0) for v_lo, v_hi in vs]
                delta = _row_sums([p * dp for p, dp in zip(probs, dps)])
                d_sink = -(e_sink * inv) * delta
                for quarter, g in enumerate(STACK_HEADS):
                    d_here = jnp.sum(d_sink[quarter * nq:(quarter + 1) * nq], axis=0, keepdims=True)
                    dsink = dsink + jnp.where(jnp.logical_and(sub == 0, lane == kh * Q_PER_KV + g), d_here, 0.0)
                dq = None
                tile = slice((kh // 2) * LANES, (kh // 2 + 1) * LANES)
                for r, p, dp, (k_lo, k_hi) in zip(key_rows, probs, dps, ks):
                    ds = (p * (dp - delta)).astype(BF16)
                    p16 = p.astype(BF16)
                    dq_x = _dot(ds[:2 * nq], k_lo) + _dot(ds[2 * nq:], k_hi)
                    dq = dq_x if dq is None else dq + dq_x
                    dk_ref[r, tile] += _to_kv_lanes(_dot_tn(ds[:2 * nq], qs), _dot_tn(ds[2 * nq:], qs), kh) * ATTN_SCALE
                    dv_ref[r, tile] += _to_kv_lanes(_dot_tn(p16[:2 * nq], dos), _dot_tn(p16[2 * nq:], dos), kh)
                dq = (dq * ATTN_SCALE).astype(BF16)
                col = kh * 2 * LANES
                dqkv_ref[q_rows, col:col + LANES] = dq[:nq]
                dqkv_ref[q_rows, col + LANES:col + 2 * LANES] = dq[nq:]
            return dsink

        dsink_ref[...] = _attention_schedule(seq, queries, jnp.zeros((SUBLANES, LANES), F32))
        dqkv_ref[:, D_MODEL:D_MODEL + KV_WIDTH] = dk_ref[...].astype(BF16)
        dqkv_ref[:, D_MODEL + KV_WIDTH:] = dv_ref[...].astype(BF16)

    return _pcall(
        body, name=name, grid=(n_b,),
        in_specs=[pl.BlockSpec(memory_space=pltpu.SMEM),
                  pl.BlockSpec((None, n_l, D_MODEL), lambda b: (b, 0, 0)),
                  pl.BlockSpec((None, n_l, KV_WIDTH), lambda b: (b, 0, D_MODEL // KV_WIDTH)),
                  pl.BlockSpec((None, n_l, KV_WIDTH), lambda b: (b, 0, D_MODEL // KV_WIDTH + 1)),
                  pl.BlockSpec((None, n_l, D_MODEL), lambda b: (b, 0, 0)),
                  ANY_SPEC, ANY_SPEC],
        out_specs=[pl.BlockSpec((None, n_l, qkv_width), lambda b: (b, 0, 0)),
                   pl.BlockSpec((None, SUBLANES, LANES), lambda b: (b, 0, 0))],
        out_shape=[jax.ShapeDtypeStruct(dproj3.shape, BF16), jax.ShapeDtypeStruct((n_b, SUBLANES, LANES), F32)],
        scratch_shapes=[pltpu.VMEM((n_l, KV_WIDTH), F32), pltpu.VMEM((n_l, KV_WIDTH), F32)],
        input_output_aliases={5: 0},
        compiler_params=_cp(("arbitrary",), VMEM_BIG),
    )(sinks, proj3, proj3, proj3, dattn3, dproj3, after)


TAB_ROWS = 8
SCAN_UNROLL = 4


def _cmul(ar, ai, br, bi):
    return ar * br - ai * bi, ar * bi + ai * br


def _discretise(ar, ai, ls):
    step = jnp.exp(ls)
    mag = jnp.exp(ar * step)
    ang = ai * step
    cos, sin = jnp.cos(ang), jnp.sin(ang)
    lr, li = mag * cos, mag * sin
    den = ar * ar + ai * ai
    nr, ni = lr - 1.0, li
    cr = (nr * ar + ni * ai) / den
    ci = (ni * ar - nr * ai) / den
    return step, mag, lr, li, den, nr, ni, cr, ci


def _scan_tables(lr, li, reverse):
    n = lr.shape[-1]
    pw = [(lr, li)]
    for _ in range(SUBLANES - 1):
        pw.append(_cmul(pw[-1][0], pw[-1][1], lr, li))
    row = lax.broadcasted_iota(jnp.int32, (SUBLANES, n), 0)
    out = []
    for d in (1, 2, 4):
        ok = (row + d <= SUBLANES - 1) if reverse else (row >= d)
        out += [jnp.where(ok, pw[d - 1][0], 0.0), jnp.where(ok, pw[d - 1][1], 0.0)]
    cr = jnp.zeros((SUBLANES, n), F32)
    ci = jnp.zeros((SUBLANES, n), F32)
    for r in range(SUBLANES):
        e = (SUBLANES - r) if reverse else (r + 1)
        cr = jnp.where(row == r, pw[e - 1][0], cr)
        ci = jnp.where(row == r, pw[e - 1][1], ci)
    return out + [cr, ci]


def ssm_prepare(ar, ai, ls, br_t, bi_t, name):
    def body(ar_ref, ai_ref, ls_ref, br_ref, bi_ref, bbr_ref, bbi_ref, tf_ref, tr_ref):
        _, _, lr, li, _, _, _, cr, ci = _discretise(ar_ref[...], ai_ref[...], ls_ref[...])
        br, bi = br_ref[...], bi_ref[...]
        bbr_ref[...] = cr * br - ci * bi
        bbi_ref[...] = cr * bi + ci * br
        for k, t in enumerate(_scan_tables(lr, li, False)):
            tf_ref[k] = t
        for k, t in enumerate(_scan_tables(lr, -li, True)):
            tr_ref[k] = t

    return _pcall(
        body, name=name,
        out_shape=[jax.ShapeDtypeStruct((SSM_GROUP, N_STATES), F32), jax.ShapeDtypeStruct((SSM_GROUP, N_STATES), F32),
                   jax.ShapeDtypeStruct((TAB_ROWS, SUBLANES, N_STATES), F32),
                   jax.ShapeDtypeStruct((TAB_ROWS, SUBLANES, N_STATES), F32)],
    )(ar, ai, ls, br_t, bi_t)


def ssm_param_backward(ar, ai, ls, br_t, bi_t, dlr_p, dli_p, dbbr, dbbi, group_sum, name):
    def body(ar_ref, ai_ref, ls_ref, br_ref, bi_ref, dlr_ref, dli_ref, dbbr_ref, dbbi_ref, gs_ref,
             dar_ref, dai_ref, dls_ref, dbr_ref, dbi_ref):
        ar, ai = ar_ref[...], ai_ref[...]
        step, mag, lr, li, den, nr, ni, cr, ci = _discretise(ar, ai, ls_ref[...])
        br, bi, dbbr_v, dbbi_v = br_ref[...], bi_ref[...], dbbr_ref[...], dbbi_ref[...]
        dbr_ref[...] = cr * dbbr_v + ci * dbbi_v
        dbi_ref[...] = cr * dbbi_v - ci * dbbr_v
        dcr = jnp.sum(dbbr_v * br + dbbi_v * bi, axis=0, keepdims=True)
        dci = jnp.sum(dbbi_v * br - dbbr_v * bi, axis=0, keepdims=True)
        dnr = (dcr * ar - dci * ai) / den
        dni = (dcr * ai + dci * ar) / den
        dden = -(cr * dcr + ci * dci) / den
        dar = (dcr * nr + dci * ni) / den + dden * 2.0 * ar
        dai = (dcr * ni - dci * nr) / den + dden * 2.0 * ai
        dlr = jnp.sum(dlr_ref[...], axis=0, keepdims=True) + dnr
        dli = jnp.sum(dli_ref[...], axis=0, keepdims=True) + dni
        dmag = (dlr * lr + dli * li) / mag
        dang = dli * lr - dlr * li
        dar_ref[...] = dar + dmag * mag * step
        dai_ref[...] = dai + dang * step
        dstep = dmag * mag * ar + dang * ai
        dls_ref[...] = jnp.dot(dstep * step, gs_ref[...], preferred_element_type=F32, precision=lax.Precision.HIGHEST)

    vec = jax.ShapeDtypeStruct((1, N_STATES), F32)
    mat = jax.ShapeDtypeStruct((SSM_GROUP, N_STATES), F32)
    return _pcall(body, name=name, out_shape=[vec, vec, jax.ShapeDtypeStruct((1, LANES), F32), mat, mat])(
        ar, ai, ls, br_t, bi_t, dlr_p, dli_p, dbbr, dbbi, group_sum)


def _scan_rows(a, b, tabs, carry, reverse):
    for k, d in enumerate((1, 2, 4)):
        shift = SUBLANES - d if reverse else d
        sr, si = pltpu.roll(a, shift, axis=0), pltpu.roll(b, shift, axis=0)
        pr, pi = _cmul(tabs[2 * k], tabs[2 * k + 1], sr, si)
        a, b = a + pr, b + pi
    pr, pi = _cmul(tabs[6], tabs[7], carry[0], carry[1])
    return a + pr, b + pi


def _time_groups(seq, reverse):
    meta = [seq + SUBLANES * g for g in range(N_META // SUBLANES)]
    return meta[::-1] if reverse else meta


def ssm_forward_scan(proj3, b_comb, tabf, c_comb, dvec, seq, name):
    n_b, n_l, _ = proj3.shape
    u_blk = (D_MODEL + 2 * KV_WIDTH) // LANES

    def body(u_ref, b_ref, tab_ref, c_ref, d_ref, x_ref, y_ref, bu, xs):
        j = pl.program_id(1)
        u = u_ref[...]
        bu[...] = _dot(u, b_ref[...])
        tabs = [tab_ref[k] for k in range(TAB_ROWS)]

        def group(r0, carry):
            rows = pl.ds(r0, SUBLANES)
            a, b = _scan_rows(bu[rows, :SCAN_COLS], bu[rows, SCAN_COLS:], tabs, carry, False)
            xs[rows, :SCAN_COLS] = a
            xs[rows, SCAN_COLS:] = b
            return (jnp.broadcast_to(a[SUBLANES - 1:, :], a.shape), jnp.broadcast_to(b[SUBLANES - 1:, :], b.shape))

        zero = jnp.zeros((SUBLANES, SCAN_COLS), F32)
        carry = (zero, zero)
        for r0 in _time_groups(seq, False):
            carry = group(r0, carry)
        span = SCAN_UNROLL * SUBLANES

        def groups(t, c):
            for k in range(SCAN_UNROLL):
                c = group(pl.multiple_of(t * span, span) + k * SUBLANES, c)
            return c

        lax.fori_loop(0, seq // span, groups, carry)
        x16 = xs[...].astype(BF16)
        x_ref[...] = x16
        contrib = _dot(x16, c_ref[...])

        @pl.when(j % 2 == 0)
        def _():
            y_ref[...] = contrib + d_ref[...] * u.astype(F32)

        @pl.when(j % 2 == 1)
        def _():
            y_ref[...] += contrib

    return _pcall(
        body, name=name, grid=(n_b, N_SCAN_BLK),
        in_specs=[pl.BlockSpec((None, n_l, LANES), lambda b, j: (b, 0, u_blk + j // 2)),
                  pl.BlockSpec((LANES, 2 * SCAN_COLS), lambda b, j: (j // 2, j)),
                  pl.BlockSpec((TAB_ROWS, SUBLANES, SCAN_COLS), lambda b, j: (0, 0, j)),
                  pl.BlockSpec((2 * SCAN_COLS, LANES), lambda b, j: (j, j // 2)),
                  pl.BlockSpec((1, LANES), lambda b, j: (0, j // 2))],
        out_specs=[pl.BlockSpec((None, n_l, 2 * SCAN_COLS), lambda b, j: (b, 0, j)),
                   pl.BlockSpec((None, n_l, LANES), lambda b, j: (b, 0, j // 2))],
        out_shape=[jax.ShapeDtypeStruct((n_b, n_l, 2 * N_STATES), BF16),
                   jax.ShapeDtypeStruct((n_b, n_l, SSM_WIDTH), F32)],
        scratch_shapes=[pltpu.VMEM((n_l, 2 * SCAN_COLS), F32)] * 2,
        compiler_params=_cp(("arbitrary", "arbitrary"), VMEM_BIG),
    )(proj3, b_comb, tabf, c_comb, dvec)


def ssm_backward_scan(dyraw3, xs3, dproj3, c_comb_t, tabr, b_comb_t, dvec, seq, name):
    n_b, n_l, _ = xs3.shape
    u_blk = (D_MODEL + 2 * KV_WIDTH) // LANES

    def body(dy_ref, x_ref, _, c_ref, tab_ref, b_ref, d_ref, du_ref, g_ref, dlr_ref, dli_ref, dx, gs, xs, du_acc):
        j = pl.program_id(1)
        dy = dy_ref[...]
        dx[...] = _dot(dy, c_ref[...])
        xs[...] = x_ref[...].astype(F32)
        tabs = [tab_ref[k] for k in range(TAB_ROWS)]
        last_row = lax.broadcasted_iota(jnp.int32, (SUBLANES, SCAN_COLS), 0) == SUBLANES - 1

        def group(r0, state):
            cr, ci, acc_r, acc_i = state
            rows = pl.ds(r0, SUBLANES)
            a, b = _scan_rows(dx[rows, :SCAN_COLS], dx[rows, SCAN_COLS:], tabs, (cr, ci), True)
            gs[rows, :SCAN_COLS] = a
            gs[rows, SCAN_COLS:] = b
            na = jnp.where(last_row, cr, pltpu.roll(a, SUBLANES - 1, axis=0))
            nb = jnp.where(last_row, ci, pltpu.roll(b, SUBLANES - 1, axis=0))
            xa, xb = xs[rows, :SCAN_COLS], xs[rows, SCAN_COLS:]
            return (jnp.broadcast_to(a[:1, :], a.shape), jnp.broadcast_to(b[:1, :], b.shape),
                    acc_r + na * xa + nb * xb, acc_i + nb * xa - na * xb)

        zero = jnp.zeros((SUBLANES, SCAN_COLS), F32)
        span = SCAN_UNROLL * SUBLANES
        n_spans = seq // span

        def groups(t, s):
            for k in reversed(range(SCAN_UNROLL)):
                s = group(pl.multiple_of((n_spans - 1 - t) * span, span) + k * SUBLANES, s)
            return s

        state = lax.fori_loop(0, n_spans, groups, (zero, zero, zero, zero))
        for r0 in _time_groups(seq, True):
            state = group(r0, state)
        dlr_ref[...] = state[2]
        dli_ref[...] = state[3]
        g16 = gs[...].astype(BF16)
        g_ref[...] = g16
        contrib = _dot(g16, b_ref[...])

        @pl.when(j % 2 == 0)
        def _():
            du_acc[...] = contrib + d_ref[...] * dy.astype(F32)

        @pl.when(j % 2 == 1)
        def _():
            du_ref[...] = (du_acc[...] + contrib).astype(BF16)

    state_blk = pl.BlockSpec((None, n_l, 2 * SCAN_COLS), lambda b, j: (b, 0, j))
    dl_blk = pl.BlockSpec((None, SUBLANES, SCAN_COLS), lambda b, j: (b, 0, j))
    return _pcall(
        body, name=name, grid=(n_b, N_SCAN_BLK),
        in_specs=[pl.BlockSpec((None, n_l, LANES), lambda b, j: (b, 0, j // 2)), state_blk,
                  pl.BlockSpec(memory_space=pl.ANY),
                  pl.BlockSpec((LANES, 2 * SCAN_COLS), lambda b, j: (j // 2, j)),
                  pl.BlockSpec((TAB_ROWS, SUBLANES, SCAN_COLS), lambda b, j: (0, 0, j)),
                  pl.BlockSpec((2 * SCAN_COLS, LANES), lambda b, j: (j, j // 2)),
                  pl.BlockSpec((1, LANES), lambda b, j: (0, j // 2))],
        out_specs=[pl.BlockSpec((None, n_l, LANES), lambda b, j: (b, 0, u_blk + j // 2)), state_blk, dl_blk, dl_blk],
        out_shape=[jax.ShapeDtypeStruct(dproj3.shape, BF16), jax.ShapeDtypeStruct((n_b, n_l, 2 * N_STATES), BF16),
                   jax.ShapeDtypeStruct((n_b, SUBLANES, N_STATES), F32), jax.ShapeDtypeStruct((n_b, SUBLANES, N_STATES), F32)],
        scratch_shapes=[pltpu.VMEM((n_l, 2 * SCAN_COLS), F32)] * 3 + [pltpu.VMEM((n_l, LANES), F32)],
        input_output_aliases={2: 0},
        compiler_params=_cp(("arbitrary", "arbitrary"), VMEM_BIG),
    )(dyraw3, xs3, dproj3, c_comb_t, tabr, b_comb_t, dvec)


def ssm_param_grads(proj, gs, xs, dyraw, tm, name):
    t_rows = proj.shape[0]
    ni = t_rows // tm
    u_blk = (D_MODEL + 2 * KV_WIDTH) // LANES
    width = 2 * SCAN_COLS

    def body(u_ref, g_ref, x_ref, dy_ref, db_ref, dc_ref, dd_ref):
        cb, i = pl.program_id(0), pl.program_id(1)
        u, dy = u_ref[...], dy_ref[...]
        _accumulate(db_ref, _dot_tn(u, g_ref[...]), i == 0)
        _accumulate(dc_ref, _dot_tn(x_ref[...], dy), i == 0)

        @pl.when(cb % 2 == 0)
        def _():
            _accumulate(dd_ref, jnp.sum(dy.astype(F32) * u.astype(F32), axis=0, keepdims=True), i == 0)

    return _pcall(
        body, name=name, grid=(N_SCAN_BLK, ni),
        in_specs=[pl.BlockSpec((tm, LANES), lambda cb, i: (i, u_blk + cb // 2)),
                  pl.BlockSpec((tm, width), lambda cb, i: (i, cb)),
                  pl.BlockSpec((tm, width), lambda cb, i: (i, cb)),
                  pl.BlockSpec((tm, LANES), lambda cb, i: (i, cb // 2))],
        out_specs=[pl.BlockSpec((None, LANES, width), lambda cb, i: (cb, 0, 0)),
                   pl.BlockSpec((None, width, LANES), lambda cb, i: (cb, 0, 0)),
                   pl.BlockSpec((1, LANES), lambda cb, i: (0, cb // 2))],
        out_shape=[jax.ShapeDtypeStruct((N_SCAN_BLK, LANES, width), F32),
                   jax.ShapeDtypeStruct((N_SCAN_BLK, width, LANES), F32), jax.ShapeDtypeStruct((1, SSM_WIDTH), F32)],
        compiler_params=_cp(("arbitrary", "arbitrary"), VMEM_BIG),
    )(proj, gs, xs, dyraw)


def sum_leading(x, name):
    def body(x_ref, o_ref):
        acc = x_ref[0]
        for k in range(1, x.shape[0]):
            acc = acc + x_ref[k]
        o_ref[...] = acc

    return _pcall(body, name=name, out_shape=jax.ShapeDtypeStruct(x.shape[1:], x.dtype))(x)


WEIGHTS = ['meta_tokens', 'ffn1_norm', 'ffn1_w1', 'ffn1_w3', 'ffn1_w2', 'mix_norm', 'w_in', 'attn_sinks', 'ssm_a_re',
           'ssm_a_im', 'ssm_log_step', 'ssm_b_re', 'ssm_b_im', 'ssm_c_re', 'ssm_c_im', 'ssm_d', 'ssm_glu_a', 'ssm_glu_b',
           'w_out', 'ffn2_norm', 'ffn2_w1', 'ffn2_w3', 'ffn2_w2', 'final_norm']
SHARDED = ['ffn1_w1', 'ffn1_w3', 'ffn1_w2', 'ffn2_w1', 'ffn2_w3', 'ffn2_w2', 'w_in', 'ssm_glu_a', 'ssm_glu_b', 'w_out']
REPLICATED = ['ffn1_norm', 'mix_norm', 'ffn2_norm', 'final_norm', 'attn_sinks', 'ssm_a_re', 'ssm_a_im', 'ssm_log_step',
              'ssm_b_re', 'ssm_b_im', 'ssm_c_re', 'ssm_c_im', 'ssm_d']
PACK_COLS = 1024


def _block_diag(blocks):
    g, r, c = blocks.shape
    eye = jnp.eye(g, dtype=blocks.dtype)
    return (blocks[:, :, None, :] * eye[:, None, :, None]).reshape(g * r, g * c)


def _scan_order(re, im):
    r = re.shape[0]
    return jnp.stack([re.reshape(r, N_SCAN_BLK, SCAN_COLS), im.reshape(r, N_SCAN_BLK, SCAN_COLS)], axis=2).reshape(r, 2 * N_STATES)


def _pack(arrays):
    parts = []
    for a in arrays:
        flat = a.reshape(-1)
        chunk = SUBLANES * PACK_COLS
        padded = -(-flat.shape[0] // chunk) * chunk
        parts.append(jnp.pad(flat, (0, padded - flat.shape[0])).reshape(-1, PACK_COLS))
    return jnp.concatenate(parts, axis=0)


def _unpack(packed, shapes):
    out, row = [], 0
    for shape in shapes:
        size = 1
        for s in shape:
            size *= s
        chunk = SUBLANES * PACK_COLS
        rows = -(-size // chunk) * SUBLANES
        out.append(packed[row:row + rows].reshape(-1)[:size].reshape(shape))
        row += rows
    return out


def kernel(x, meta_tokens, ffn1_norm, ffn1_w1, ffn1_w3, ffn1_w2, mix_norm, w_in, attn_sinks, ssm_a_re, ssm_a_im, ssm_log_step, ssm_b_re, ssm_b_im, ssm_c_re, ssm_c_im, ssm_d, ssm_glu_a, ssm_glu_b, w_out, ffn2_norm, ffn2_w1, ffn2_w3, ffn2_w2, final_norm, loss_target, m_meta_tokens, m_ffn1_norm, m_ffn1_w1, m_ffn1_w3, m_ffn1_w2, m_mix_norm, m_w_in, m_attn_sinks, m_ssm_a_re, m_ssm_a_im, m_ssm_log_step, m_ssm_b_re, m_ssm_b_im, m_ssm_c_re, m_ssm_c_im, m_ssm_d, m_ssm_glu_a, m_ssm_glu_b, m_w_out, m_ffn2_norm, m_ffn2_w1, m_ffn2_w3, m_ffn2_w2, m_final_norm, v_meta_tokens, v_ffn1_norm, v_ffn1_w1, v_ffn1_w3, v_ffn1_w2, v_mix_norm, v_w_in, v_attn_sinks, v_ssm_a_re, v_ssm_a_im, v_ssm_log_step, v_ssm_b_re, v_ssm_b_im, v_ssm_c_re, v_ssm_c_im, v_ssm_d, v_ssm_glu_a, v_ssm_glu_b, v_w_out, v_ffn2_norm, v_ffn2_w1, v_ffn2_w3, v_ffn2_w2, v_final_norm):
    given = dict(locals())
    w = {n: given[n] for n in WEIGHTS}
    m = {n: given["m_" + n] for n in WEIGHTS}
    v = {n: given["v_" + n] for n in WEIGHTS}

    n_b, seq, _ = x.shape
    n_l = seq + N_META
    t_rows = n_b * n_l
    tm = _row_tile(n_l, 688)
    px, py, pc = _my_place()
    me = 4 * px + 2 * py + pc

    glu = jnp.stack([ssm_glu_a[0], ssm_glu_b[0]]).astype(BF16)
    ffn_names = ['ffn1_w1', 'ffn1_w3', 'ffn1_w2', 'ffn2_w1', 'ffn2_w3', 'ffn2_w2']

    def hidden_on_rows(n, t):
        return t[0] if n.endswith('w2') else t[0].T

    def hidden_on_rows_back(n, t):
        return t[None] if n.endswith('w2') else t.T[None]

    me_idx = jnp.reshape(me, (1,)).astype(jnp.int32)
    first_names, later_names = ffn_names[:3], ffn_names[3:]
    *first, metag = all_gather_list(
        [hidden_on_rows(n, w[n]).astype(BF16) for n in first_names] + [meta_tokens], "ag_first")
    win_send, win_recv, win_shard, win_land, win_token = exchange_start(
        [w_in[0].astype(BF16)], first[0], True, "ag_w_in_start")
    later_shards = [hidden_on_rows(n, w[n]).astype(BF16) for n in later_names] + [glu, w_out[0].astype(BF16)]
    ag_send, ag_recv, later_shards, later_lands, ag_token = exchange_start(later_shards, win_token, True, "ag_later_start")
    full = {n: g.reshape(D_FF, D_MODEL) for n, g in zip(first_names, first)}
    meta_full = metag.transpose(1, 0, 2).reshape(N_META, D_MODEL)

    h0 = jnp.concatenate([x, jnp.broadcast_to(meta_full[None], (n_b, N_META, D_MODEL))], axis=1).reshape(t_rows, D_MODEL)
    target = jnp.concatenate([loss_target, jnp.zeros((n_b, N_META, D_MODEL), F32)], axis=1).reshape(t_rows, D_MODEL)
    final_g = final_norm.reshape(1, D_MODEL)

    ar = ssm_a_re.reshape(1, N_STATES)
    ai = ssm_a_im.reshape(1, N_STATES)
    ls = jnp.repeat(ssm_log_step.reshape(SSM_GROUPS), SSM_STATE).reshape(1, N_STATES)
    br_t = ssm_b_re[0].transpose(2, 0, 1).reshape(SSM_GROUP, N_STATES)
    bi_t = ssm_b_im[0].transpose(2, 0, 1).reshape(SSM_GROUP, N_STATES)
    bbr, bbi, tabf, tabr = ssm_prepare(ar, ai, ls, br_t, bi_t, "ssm_prepare")
    bbr_g = bbr.reshape(SSM_GROUP, SSM_GROUPS, SSM_STATE).transpose(1, 0, 2)
    bbi_g = bbi.reshape(SSM_GROUP, SSM_GROUPS, SSM_STATE).transpose(1, 0, 2)
    b_comb = _scan_order(_block_diag(bbr_g), _block_diag(bbi_g)).astype(BF16)
    c_comb_t = _scan_order(_block_diag(ssm_c_re[0]), -_block_diag(ssm_c_im[0])).astype(BF16)
    b_comb_t, c_comb = b_comb.T, c_comb_t.T

    ffn1_w = (full['ffn1_w1'], full['ffn1_w3'], full['ffn1_w2'])
    h1, hn1, a1, b1 = ffn_forward(h0, ffn1_norm, *ffn1_w, ag_token, tm, "ffn1_fwd")
    (wing,) = exchange_wait(win_send, win_recv, win_shard, win_land, h1, True, "ag_w_in_wait")
    wing = lax.dynamic_update_slice_in_dim(wing, win_shard[0][None], me, axis=0)
    hnm, proj = mix_forward(h1, mix_norm, wing, tm, "mix_fwd")
    proj3 = proj.reshape(n_b, n_l, IN_WIDTH)
    attn3 = attention_forward(proj3, attn_sinks, seq, "attn_fwd")
    attn = attn3.reshape(t_rows, D_MODEL)
    xs3, yraw3 = ssm_forward_scan(proj3, b_comb, tabf, c_comb, ssm_d, seq, "ssm_fwd")
    yraw = yraw3.reshape(t_rows, SSM_WIDTH)
    later = exchange_wait(ag_send, ag_recv, later_shards, later_lands, yraw3, True, "ag_later_wait")
    later = [lax.dynamic_update_slice_in_dim(z, s[None], me, axis=0) for z, s in zip(later, later_shards)]
    for n, g in zip(later_names, later):
        full[n] = g.reshape(D_FF, D_MODEL)
    ffn2_w = (full['ffn2_w1'], full['ffn2_w3'], full['ffn2_w2'])
    glug, wog = later[len(later_names):]
    glu_a = glug[:, 0].transpose(1, 0, 2).reshape(SSM_WIDTH, D_MODEL)
    glu_b = glug[:, 1].transpose(1, 0, 2).reshape(SSM_WIDTH, D_MODEL)
    w_out_full = wog.reshape(D_MODEL, D_MODEL)
    h2 = merge_forward(h1, yraw, attn, proj, glu_a, glu_b, w_out_full, tm, "merge_fwd")
    h3, hn2, a2, b2 = ffn_forward(h2, ffn2_norm, *ffn2_w, ag_token, tm, "ffn2_fwd")
    dh3, loss_part, g_final = final_loss_backward(h3, target, final_g, seq, tm, "loss_bwd")
    loss = lax.psum(loss_part[0, 0], ("x", "y", "c"))

    def blocked_ffn(d_w1t, d_w3t, d_w2):
        return tuple(t.reshape(N_DEV, FF_BLK, D_MODEL) for t in (d_w1t, d_w3t, d_w2))

    def blocked_cols(full_grad):
        r = full_grad.shape[0]
        return full_grad.reshape(r, N_DEV, full_grad.shape[1] // N_DEV).transpose(1, 0, 2).astype(BF16)

    early = {}

    def start_reduce(names, tag):
        srcs = [dw[n] for n in names]
        send, recv, srcs, lands, token = exchange_start(srcs, srcs[0], False, "rs_" + tag + "_start")
        early[tag] = (names, send, recv, srcs, lands)
        return token

    dw = {}
    da2, db2, dh3_half = ffn_backward_hidden(dh3, a2, b2, ffn2_w[2], g_final, tm, "ffn2_bwd_hid")
    dw['ffn2_w1'], dw['ffn2_w3'], dw['ffn2_w2'] = blocked_ffn(
        *ffn_backward_weights(hn2, dh3_half, a2, b2, da2, db2, n_l, FF_BWD_COLS, "ffn2_bwd_w"))
    token = start_reduce(later_names, "ffn2")
    dh2, g_ffn2_norm = ffn_backward_input(dh3, h2, ffn2_norm, da2, db2, ffn2_w[0], ffn2_w[1], token, tm, "ffn2_bwd_in")
    dattn, dyraw, dproj, *for_weights = merge_backward(dh2, yraw, attn, proj, glu_a, glu_b, w_out_full, token, tm,
                                                       "merge_bwd")
    d_wo, d_ga, d_gb = merge_backward_weights(*for_weights, tm, "merge_bwd_w")
    dw['ssm_glu_a'] = blocked_cols(d_ga)
    dw['ssm_glu_b'] = blocked_cols(d_gb)
    dw['w_out'] = d_wo.reshape(N_DEV, D_MODEL // N_DEV, D_MODEL).astype(BF16)
    token = start_reduce(['ssm_glu_a', 'ssm_glu_b', 'w_out'], "mix")
    dproj3 = dproj.reshape(n_b, n_l, IN_WIDTH)
    dproj3, dsink_p = attention_backward(proj3, dattn.reshape(n_b, n_l, D_MODEL), dproj3, attn_sinks, token, seq,
                                         "attn_bwd")
    dproj3, gs3, dlr_p, dli_p = ssm_backward_scan(
        dyraw.reshape(n_b, n_l, SSM_WIDTH), xs3, dproj3, c_comb_t, tabr, b_comb_t, ssm_d, seq, "ssm_bwd")
    dproj = dproj3.reshape(t_rows, IN_WIDTH)
    d_bd, d_cd, g_d = ssm_param_grads(proj, gs3.reshape(t_rows, 2 * N_STATES), xs3.reshape(t_rows, 2 * N_STATES),
                                      dyraw, n_l, "ssm_bwd_w")
    dh1, g_mix_norm = mix_backward_act(dh2, h1, mix_norm, dproj, wing, tm, "mix_bwd_act")
    dw['w_in'] = mix_backward_weights(hnm, dproj, n_l, "mix_bwd_w")
    token = start_reduce(['w_in'], "w_in")
    da1, db1, dh1_half = ffn_backward_hidden(dh1, a1, b1, ffn1_w[2], token, tm, "ffn1_bwd_hid")
    dw['ffn1_w1'], dw['ffn1_w3'], dw['ffn1_w2'] = blocked_ffn(
        *ffn_backward_weights(hn1, dh1_half, a1, b1, da1, db1, n_l, FF_BWD_COLS, "ffn1_bwd_w"))
    token = start_reduce(first_names, "ffn1")
    dh0, g_ffn1_norm = ffn_backward_input(dh1, h0, ffn1_norm, da1, db1, ffn1_w[0], ffn1_w[1], token, tm, "ffn1_bwd_in")
    dh0_3 = dh0.reshape(n_b, n_l, D_MODEL)
    grad_x = dh0_3[:, :seq]
    g_meta = sum_leading(dh0_3[:, seq:], "meta_sum")

    groups_per_blk = SCAN_COLS // SSM_STATE
    half = ((jnp.arange(N_SCAN_BLK) % 2)[:, None] == jnp.arange(2)[None, :]).astype(F32)
    eye = jnp.eye(groups_per_blk, dtype=F32)

    def group_blocks(part, channels_first):
        if channels_first:
            t = jnp.sum(part.reshape(N_SCAN_BLK, 2, LANES // 2, SCAN_COLS) * half[:, :, None, None], axis=1)
            t = t.reshape(N_SCAN_BLK, groups_per_blk, SSM_GROUP, groups_per_blk, SSM_STATE)
            t = jnp.sum(t * eye[None, :, None, :, None], axis=3)
            return t.reshape(SSM_GROUPS, SSM_GROUP, SSM_STATE)
        t = jnp.sum(part.reshape(N_SCAN_BLK, SCAN_COLS, 2, LANES // 2) * half[:, None, :, None], axis=2)
        t = t.reshape(N_SCAN_BLK, groups_per_blk, SSM_STATE, groups_per_blk, SSM_GROUP)
        t = jnp.sum(t * eye[None, :, None, :, None], axis=3)
        return t.reshape(SSM_GROUPS, SSM_STATE, SSM_GROUP).transpose(0, 2, 1)

    dbbr = group_blocks(d_bd[:, :, :SCAN_COLS], True).transpose(1, 0, 2).reshape(SSM_GROUP, N_STATES)
    dbbi = group_blocks(d_bd[:, :, SCAN_COLS:], True).transpose(1, 0, 2).reshape(SSM_GROUP, N_STATES)
    g_c_re = group_blocks(d_cd[:, :SCAN_COLS, :], False)[None]
    g_c_im = -group_blocks(d_cd[:, SCAN_COLS:, :], False)[None]
    group_sum = (jnp.arange(N_STATES)[:, None] // SSM_STATE == jnp.arange(LANES)[None, :]).astype(F32)
    g_ar, g_ai, g_ls, g_br, g_bi = ssm_param_backward(
        ar, ai, ls, br_t, bi_t, dlr_p.reshape(n_b * SUBLANES, N_STATES), dli_p.reshape(n_b * SUBLANES, N_STATES),
        dbbr, dbbi, group_sum, "ssm_bwd_params")
    g_sinks = sum_leading(dsink_p, "sink_sum")[0:1, :N_KV_HEADS * Q_PER_KV]

    small = {
        'ffn1_norm': g_ffn1_norm, 'mix_norm': g_mix_norm, 'ffn2_norm': g_ffn2_norm, 'final_norm': g_final.reshape(D_MODEL),
        'attn_sinks': g_sinks, 'ssm_a_re': g_ar.reshape(1, SSM_GROUPS, SSM_STATE), 'ssm_a_im': g_ai.reshape(1, SSM_GROUPS, SSM_STATE),
        'ssm_log_step': g_ls[:, :SSM_GROUPS],
        'ssm_b_re': g_br.reshape(SSM_GROUP, SSM_GROUPS, SSM_STATE).transpose(1, 2, 0)[None],
        'ssm_b_im': g_bi.reshape(SSM_GROUP, SSM_GROUPS, SSM_STATE).transpose(1, 2, 0)[None],
        'ssm_c_re': g_c_re, 'ssm_c_im': g_c_im, 'ssm_d': g_d,
    }

    zeros_meta = jnp.zeros((N_META, D_MODEL), F32)
    packed_g = _pack([small[n] for n in REPLICATED] + [g_meta])
    (parts,) = all_gather_list([packed_g], "ag_small_grads")
    packed_out = adamw_small(parts, _pack([w[n] for n in REPLICATED] + [zeros_meta]),
                             _pack([m[n] for n in REPLICATED] + [zeros_meta]),
                             _pack([v[n] for n in REPLICATED] + [zeros_meta]), "adamw_small")
    shapes = [w[n].shape for n in REPLICATED] + [(N_META, D_MODEL)]
    grads, deltas, new_m, new_v = {}, {}, {}, {}
    unpacked = [_unpack(p, shapes) for p in packed_out]
    for k, n in enumerate(REPLICATED):
        grads[n], deltas[n], new_m[n], new_v[n] = (u[k] for u in unpacked)
    g_meta_full = unpacked[0][-1]
    grads['meta_tokens'] = lax.dynamic_index_in_dim(
        g_meta_full.reshape(N_META, N_DEV, D_MODEL // N_DEV), me, axis=1, keepdims=False)
    deltas['meta_tokens'], new_m['meta_tokens'], new_v['meta_tokens'] = adamw_plain(
        grads['meta_tokens'], w['meta_tokens'], m['meta_tokens'], v['meta_tokens'], "adamw_meta")

    def views(n):
        if n in ffn_names:
            return functools.partial(hidden_on_rows, n), functools.partial(hidden_on_rows_back, n)
        return (lambda t: t[0]), (lambda t: t[None])

    previous = packed_out[0]
    for tag, (names, send, recv, srcs, lands) in early.items():
        lands = exchange_wait(send, recv, srcs, lands, previous, False, "rs_" + tag + "_wait")
        for n, g, land in zip(names, srcs, lands):
            two_d, back = views(n)
            out = adamw_exchanged(me_idx, g, land, two_d(w[n]), two_d(m[n]), two_d(v[n]), "adamw_" + n)
            grads[n], deltas[n], new_m[n], new_v[n] = (back(o) for o in out)
            previous = out[1]

    return (loss, grad_x, *[grads[n] for n in WEIGHTS], *[deltas[n] for n in WEIGHTS],
            *[new_m[n] for n in WEIGHTS], *[new_v[n] for n in WEIGHTS])
```

```python
import functools

import jax
import jax.numpy as jnp
from jax import lax
from jax.experimental import pallas as pl
from jax.experimental.pallas import tpu as pltpu

F32 = jnp.float32
BF16 = jnp.bfloat16
MESH = pl.DeviceIdType.MESH

N_DEV = 8
D_MODEL = 1024
N_META = 16
HEAD_DIM = 64
N_KV_HEADS = 4
Q_PER_KV = 4
BLOCK = 128
KV_WIDTH = N_KV_HEADS * HEAD_DIM
SSM_GROUP = 16
SSM_WIDTH = 512
SSM_GROUPS = 32
SSM_STATE = 64
N_STATES = SSM_GROUPS * SSM_STATE
D_FF = 2816
FF_BLK = D_FF // N_DEV
IN_WIDTH = 4096
IN_BLK = IN_WIDTH // N_DEV
NORM_EPS = 1e-6
NEG_INF = -1e30
SCAN_COLS = 256
N_SCAN_BLK = N_STATES // SCAN_COLS
SUBLANES = 8
LANES = 128
MXU_WIDTH = 256
FF_BWD_COLS = MXU_WIDTH

ADAM_LR = 0.001
ADAM_B1 = 0.9
ADAM_B2 = 0.999
ADAM_EPS = 1e-08
ADAM_WD = 0.01
ADAM_STEP = 10

VMEM_BIG = 56 * 1024 * 1024


def _cp(sem=None, vmem=None):
    kw = {}
    if sem is not None:
        kw["dimension_semantics"] = sem
    if vmem is not None:
        kw["vmem_limit_bytes"] = vmem
    return pltpu.CompilerParams(**kw)


def _pcall(body, **kw):
    return pl.pallas_call(body, **kw)


def _dot(a, b):
    return jnp.dot(a, b, preferred_element_type=F32)


def _dot_nt(a, b):
    return lax.dot_general(a, b, (((1,), (1,)), ((), ())), preferred_element_type=F32)


def _dot_tn(a, b):
    return lax.dot_general(a, b, (((0,), (0,)), ((), ())), preferred_element_type=F32)


def _sigmoid(x):
    return 1.0 / (1.0 + jnp.exp(-x))


def _row_tile(rows, cap):
    best = None
    for t in range(16, min(rows, cap) + 1, 16):
        if rows % t == 0:
            best = t
    assert best is not None, rows
    return best


def _my_place():
    return lax.axis_index("x"), lax.axis_index("y"), lax.axis_index("c")


def all_gather_list(shards, name):
    n = len(shards)

    def body(*refs):
        ins, outs = refs[:n], refs[n:2 * n]
        send_sems, recv_sems, local_sems = refs[2 * n:]
        x, y, c = _my_place()
        me, sibling = (x, y, c), (x, y, 1 - c)
        chips = [(1 - x, y), (x, 1 - y), (1 - x, 1 - y)]

        def blk(a, px, py, pc):
            return outs[a].at[4 * px + 2 * py + pc]

        def copy(a, k, block, to, src=None):
            return pltpu.make_async_remote_copy(
                src_ref=blk(a, *block) if src is None else src, dst_ref=blk(a, *block),
                send_sem=send_sems.at[a * 7 + k], recv_sem=recv_sems.at[a * 7 + k],
                device_id=to, device_id_type=MESH)

        mine = [pltpu.make_async_copy(ins[a], blk(a, *me), local_sems.at[a]) for a in range(n)]
        for cp in mine:
            cp.start()
        first = []
        for a in range(n):
            first.append(copy(a, 0, me, sibling, src=ins[a]))
            first += [copy(a, 1 + j, me, (*chip, c), src=ins[a]) for j, chip in enumerate(chips)]
        for cp in first:
            cp.start()
        passed = []
        for j, chip in enumerate(chips):
            for a in range(n):
                copy(a, 1 + j, (*chip, c), me).wait_recv()
                cp = copy(a, 4 + j, (*chip, c), sibling)
                cp.start()
                passed.append(cp)
        for a in range(n):
            copy(a, 0, sibling, me).wait_recv()
            for j, chip in enumerate(chips):
                copy(a, 4 + j, (*chip, 1 - c), me).wait_recv()
        for cp in first + passed:
            cp.wait_send()
        for cp in mine:
            cp.wait()

    any_spec = pl.BlockSpec(memory_space=pl.ANY)
    return _pcall(
        body, name=name,
        out_shape=[jax.ShapeDtypeStruct((N_DEV,) + s.shape, s.dtype) for s in shards],
        in_specs=[any_spec] * n, out_specs=[any_spec] * n,
        scratch_shapes=[pltpu.SemaphoreType.DMA((7 * n,)), pltpu.SemaphoreType.DMA((7 * n,)),
                        pltpu.SemaphoreType.DMA((n,))],
    )(*shards)


HBM_SPEC = pl.BlockSpec(memory_space=pltpu.HBM)
SEM_SPEC = pl.BlockSpec(memory_space=pltpu.SEMAPHORE)
N_PEERS = N_DEV - 1


def _related(k):
    x, y, c = _my_place()
    px = 1 - x if k & 4 else x
    py = 1 - y if k & 2 else y
    pc = 1 - c if k & 1 else c
    return (px, py, pc), 4 * px + 2 * py + pc


def _exchange_copies(srcs, lands, send_sems, recv_sems, gather):
    x, y, c = _my_place()
    me = 4 * x + 2 * y + c
    copies = []
    for a, (src, land) in enumerate(zip(srcs, lands)):
        for k in range(1, N_DEV):
            peer, d = _related(k)
            copies.append(pltpu.make_async_remote_copy(
                src_ref=src if gather else src.at[d], dst_ref=land.at[me] if gather else land.at[k],
                send_sem=send_sems.at[a * N_PEERS + k - 1], recv_sem=recv_sems.at[a * N_PEERS + k - 1],
                device_id=peer, device_id_type=MESH))
    return copies


def exchange_start(srcs, after, gather, name):
    n = len(srcs)
    land_shapes = [((N_DEV,) + s.shape) if gather else s.shape for s in srcs]

    def body(*refs):
        send_sems, recv_sems = refs[2 * n + 1], refs[2 * n + 2]
        for cp in _exchange_copies(refs[:n], refs[n:2 * n], send_sems, recv_sems, gather):
            cp.start()
        token = refs[-1]
        token[...] = jnp.zeros_like(token)

    sems = pltpu.SemaphoreType.DMA((n * N_PEERS,))
    lands = [pltpu.with_memory_space_constraint(lax.empty(shape, s.dtype), pltpu.HBM) for shape, s in zip(land_shapes, srcs)]
    out = _pcall(
        body, name=name,
        out_shape=(sems, sems, *[pltpu.HBM(s.shape, s.dtype) for s in srcs],
                   *[pltpu.HBM(shape, s.dtype) for shape, s in zip(land_shapes, srcs)],
                   jax.ShapeDtypeStruct((SUBLANES, LANES), F32)),
        in_specs=[HBM_SPEC] * (2 * n) + [pl.BlockSpec(memory_space=pl.ANY)],
        out_specs=(SEM_SPEC, SEM_SPEC, *[HBM_SPEC] * (2 * n), pl.BlockSpec(memory_space=pltpu.VMEM)),
        input_output_aliases={i: 2 + i for i in range(2 * n)},
        compiler_params=pltpu.CompilerParams(has_side_effects=pltpu.SideEffectType.DATAFLOW_SIDE_EFFECTING),
    )(*[pltpu.with_memory_space_constraint(s, pltpu.HBM) for s in srcs], *lands, after)
    return out[0], out[1], list(out[2:2 + n]), list(out[2 + n:2 + 2 * n]), out[-1]


def exchange_wait(send_sems, recv_sems, srcs, lands, after, gather, name):
    n = len(srcs)

    def body(*refs):
        for cp in _exchange_copies(refs[:n], refs[n:2 * n], refs[2 * n], refs[2 * n + 1], gather):
            cp.wait_send()
            cp.wait_recv()

    out = _pcall(
        body, name=name,
        out_shape=(*[pltpu.HBM(s.shape, s.dtype) for s in srcs], *[pltpu.HBM(z.shape, z.dtype) for z in lands]),
        in_specs=[HBM_SPEC] * (2 * n) + [SEM_SPEC, SEM_SPEC, pl.BlockSpec(memory_space=pl.ANY)],
        out_specs=tuple([HBM_SPEC] * (2 * n)),
        input_output_aliases={i: i for i in range(2 * n)},
        compiler_params=pltpu.CompilerParams(has_side_effects=pltpu.SideEffectType.DATAFLOW_SIDE_EFFECTING),
    )(*srcs, *lands, send_sems, recv_sems, after)
    return list(out[n:])


def adamw_exchanged(me, g, land, w, m, v, name):
    rows, cols = w.shape
    tr = _row_tile(rows, 256)

    def body(me_ref, g_ref, land_ref, w_ref, m_ref, v_ref, go_ref, d_ref, mo_ref, vo_ref):
        grad = g_ref[...].astype(F32)
        for k in range(1, N_DEV):
            grad = grad + land_ref[k].astype(F32)
        delta, m_new, v_new = _adam_math(w_ref[...], grad, m_ref[...], v_ref[...])
        go_ref[...] = grad
        d_ref[...] = delta
        mo_ref[...] = m_new
        vo_ref[...] = v_new

    tile = pl.BlockSpec((tr, cols), lambda r, ix: (r, 0))
    out = jax.ShapeDtypeStruct((rows, cols), F32)
    return _pcall(
        body, name=name, out_shape=[out] * 4,
        grid_spec=pltpu.PrefetchScalarGridSpec(
            num_scalar_prefetch=1, grid=(rows // tr,),
            in_specs=[pl.BlockSpec((None, tr, cols), lambda r, ix: (ix[0], r, 0)),
                      pl.BlockSpec((N_DEV, tr, cols), lambda r, ix: (0, r, 0)), tile, tile, tile],
            out_specs=[tile] * 4),
        compiler_params=_cp(("arbitrary",)),
    )(me, g, land, w, m, v)


def _adam_math(w, g, m, v):
    m = ADAM_B1 * m + (1.0 - ADAM_B1) * g
    v = ADAM_B2 * v + (1.0 - ADAM_B2) * (g * g)
    m_hat = m / (1.0 - ADAM_B1 ** ADAM_STEP)
    v_hat = v / (1.0 - ADAM_B2 ** ADAM_STEP)
    delta = -ADAM_LR * (m_hat / (jnp.sqrt(v_hat) + ADAM_EPS) + ADAM_WD * w)
    return delta, m, v


def adamw_small(parts, w, m, v, name):
    _, rows, cols = parts.shape

    def body(p_ref, w_ref, m_ref, v_ref, go_ref, d_ref, mo_ref, vo_ref):
        grad = p_ref[0]
        for k in range(1, N_DEV):
            grad = grad + p_ref[k]
        delta, m_new, v_new = _adam_math(w_ref[...], grad, m_ref[...], v_ref[...])
        go_ref[...] = grad
        d_ref[...] = delta
        mo_ref[...] = m_new
        vo_ref[...] = v_new

    out = jax.ShapeDtypeStruct((rows, cols), F32)
    return _pcall(body, name=name, out_shape=[out] * 4, compiler_params=_cp(vmem=VMEM_BIG))(parts, w, m, v)


def adamw_plain(g, w, m, v, name):
    def body(g_ref, w_ref, m_ref, v_ref, d_ref, mo_ref, vo_ref):
        delta, m_new, v_new = _adam_math(w_ref[...], g_ref[...], m_ref[...], v_ref[...])
        d_ref[...] = delta
        mo_ref[...] = m_new
        vo_ref[...] = v_new

    out = jax.ShapeDtypeStruct(w.shape, F32)
    return _pcall(body, name=name, out_shape=[out] * 3)(g, w, m, v)


def _rms_fwd(x, g):
    r = lax.rsqrt(jnp.mean(x * x, axis=-1, keepdims=True) + NORM_EPS)
    return x * r * g


def _rms_bwd(x, g, dy):
    r = lax.rsqrt(jnp.mean(x * x, axis=-1, keepdims=True) + NORM_EPS)
    xh = x * r
    t = dy * g
    dx = r * (t - xh * jnp.mean(t * xh, axis=-1, keepdims=True))
    return dx, jnp.sum(dy * xh, axis=0, keepdims=True)


def _accumulate(ref, val, first):
    @pl.when(first)
    def _():
        ref[...] = val

    @pl.when(jnp.logical_not(first))
    def _():
        ref[...] += val


def _col_chunks(width):
    return [(c0, min(MXU_WIDTH, width - c0)) for c0 in range(0, width, MXU_WIDTH)]


ANY_SPEC = pl.BlockSpec(memory_space=pl.ANY)


def ffn_forward(h, norm, w1, w3, w2, after, tm, name):
    t_rows = h.shape[0]

    def body(h_ref, g_ref, w1_ref, w3_ref, w2_ref, _, out_ref, hn_ref, a_ref, b_ref, hid_ref):
        hn = _rms_fwd(h_ref[...], g_ref[...]).astype(BF16)
        hn_ref[...] = hn
        for c0, cw in _col_chunks(D_FF):
            a = _dot_nt(hn, w1_ref[c0:c0 + cw, :])
            b = _dot_nt(hn, w3_ref[c0:c0 + cw, :])
            a_ref[:, c0:c0 + cw] = a.astype(BF16)
            b_ref[:, c0:c0 + cw] = b.astype(BF16)
            hid_ref[:, c0:c0 + cw] = (a * _sigmoid(a) * b).astype(BF16)
        out_ref[...] = h_ref[...] + 0.5 * _dot(hid_ref[...], w2_ref[...])

    row = pl.BlockSpec((tm, D_MODEL), lambda i: (i, 0))
    hid_blk = pl.BlockSpec((tm, D_FF), lambda i: (i, 0))
    weight = _resident((D_FF, D_MODEL))
    return _pcall(
        body, name=name, grid=(t_rows // tm,),
        in_specs=[row, pl.BlockSpec((1, D_MODEL), lambda i: (0, 0)), weight, weight, weight, ANY_SPEC],
        out_specs=[row, row, hid_blk, hid_blk],
        out_shape=[jax.ShapeDtypeStruct((t_rows, D_MODEL), F32), jax.ShapeDtypeStruct((t_rows, D_MODEL), BF16),
                   jax.ShapeDtypeStruct((t_rows, D_FF), BF16), jax.ShapeDtypeStruct((t_rows, D_FF), BF16)],
        scratch_shapes=[pltpu.VMEM((tm, D_FF), BF16)],
        compiler_params=_cp(("arbitrary",), VMEM_BIG),
    )(h, norm, w1, w3, w2, after)


def _resident(shape):
    return pl.BlockSpec(shape, lambda *_: (0,) * len(shape), pipeline_mode=pl.Buffered(1))


def ffn_backward_hidden(dh, a, b, w2, after, tm, name):
    t_rows = dh.shape[0]

    def body(dh_ref, a_ref, b_ref, w2_ref, _, da_ref, db_ref, dhb_ref):
        dhb = (0.5 * dh_ref[...]).astype(BF16)
        dhb_ref[...] = dhb
        for c0, cw in _col_chunks(D_FF):
            dhid = _dot_nt(dhb, w2_ref[c0:c0 + cw, :])
            av = a_ref[:, c0:c0 + cw].astype(F32)
            bv = b_ref[:, c0:c0 + cw].astype(F32)
            s = _sigmoid(av)
            da_ref[:, c0:c0 + cw] = (dhid * bv * (s * (1.0 + av * (1.0 - s)))).astype(BF16)
            db_ref[:, c0:c0 + cw] = (dhid * (av * s)).astype(BF16)

    hid = pl.BlockSpec((tm, D_FF), lambda i: (i, 0))
    row = pl.BlockSpec((tm, D_MODEL), lambda i: (i, 0))
    return _pcall(
        body, name=name, grid=(t_rows // tm,),
        in_specs=[row, hid, hid, _resident((D_FF, D_MODEL)), ANY_SPEC],
        out_specs=[hid, hid, row],
        out_shape=[jax.ShapeDtypeStruct((t_rows, D_FF), BF16), jax.ShapeDtypeStruct((t_rows, D_FF), BF16),
                   jax.ShapeDtypeStruct((t_rows, D_MODEL), BF16)],
        compiler_params=_cp(("arbitrary",), VMEM_BIG),
    )(dh, a, b, w2, after)


def ffn_backward_input(dh, h, norm, da, db, w1, w3, after, tm, name):
    t_rows = h.shape[0]

    def body(dh_ref, h_ref, g_ref, da_ref, db_ref, w1_ref, w3_ref, _, dhin_ref, dg_ref):
        dhn = _dot(da_ref[...], w1_ref[...]) + _dot(db_ref[...], w3_ref[...])
        dx, dg = _rms_bwd(h_ref[...], g_ref[...], dhn)
        dhin_ref[...] = dh_ref[...] + dx
        _accumulate(dg_ref, dg, pl.program_id(0) == 0)

    row = pl.BlockSpec((tm, D_MODEL), lambda i: (i, 0))
    vec = pl.BlockSpec((1, D_MODEL), lambda i: (0, 0))
    hid = pl.BlockSpec((tm, D_FF), lambda i: (i, 0))
    return _pcall(
        body, name=name, grid=(t_rows // tm,),
        in_specs=[row, row, vec, hid, hid, _resident((D_FF, D_MODEL)), _resident((D_FF, D_MODEL)), ANY_SPEC],
        out_specs=[row, vec],
        out_shape=[jax.ShapeDtypeStruct((t_rows, D_MODEL), F32), jax.ShapeDtypeStruct((1, D_MODEL), F32)],
        compiler_params=_cp(("arbitrary",), VMEM_BIG),
    )(dh, h, norm, da, db, w1, w3, after)


def ffn_backward_weights(hn, dh, a, b, da, db, tm, tn, name):
    t_rows = hn.shape[0]
    ni = t_rows // tm
    kc = _row_tile(tm, 688)

    def body(hn_ref, dh_ref, a_ref, b_ref, da_ref, db_ref, dw1_ref, dw3_ref, dw2_ref, acc1, acc3, acc2):
        i = pl.program_id(1)
        parts = None
        for r0 in range(0, tm, kc):
            rows = slice(r0, r0 + kc)
            hn_v = hn_ref[rows, :]
            av = a_ref[rows, :].astype(F32)
            hid = (av * _sigmoid(av) * b_ref[rows, :].astype(F32)).astype(BF16)
            new = (_dot_tn(hn_v, da_ref[rows, :]), _dot_tn(hn_v, db_ref[rows, :]), _dot_tn(hid, dh_ref[rows, :]))
            parts = new if parts is None else tuple(p + q for p, q in zip(parts, new))
        _accumulate(acc1, parts[0], i == 0)
        _accumulate(acc3, parts[1], i == 0)
        _accumulate(acc2, parts[2], i == 0)

        @pl.when(i == ni - 1)
        def _():
            dw1_ref[...] = acc1[...].T.astype(BF16)
            dw3_ref[...] = acc3[...].T.astype(BF16)
            dw2_ref[...] = acc2[...].astype(BF16)

    row = pl.BlockSpec((tm, D_MODEL), lambda j, i: (i, 0))
    hid_blk = pl.BlockSpec((tm, tn), lambda j, i: (i, j))
    w_row = pl.BlockSpec((tn, D_MODEL), lambda j, i: (j, 0))
    out = jax.ShapeDtypeStruct((D_FF, D_MODEL), BF16)
    return _pcall(
        body, name=name, grid=(D_FF // tn, ni),
        in_specs=[row, row, hid_blk, hid_blk, hid_blk, hid_blk],
        out_specs=[w_row, w_row, w_row], out_shape=[out, out, out],
        scratch_shapes=[pltpu.VMEM((D_MODEL, tn), F32), pltpu.VMEM((D_MODEL, tn), F32), pltpu.VMEM((tn, D_MODEL), F32)],
        compiler_params=_cp(("arbitrary", "arbitrary"), VMEM_BIG),
    )(hn, dh, a, b, da, db)


def mix_forward(h, norm, wing, tm, name):
    t_rows = h.shape[0]

    def body(h_ref, g_ref, w_ref, hn_ref, p_ref):
        hn = _rms_fwd(h_ref[...], g_ref[...]).astype(BF16)
        hn_ref[...] = hn
        for j in range(N_DEV):
            p_ref[:, j * IN_BLK:(j + 1) * IN_BLK] = _dot(hn, w_ref[j]).astype(BF16)

    row = pl.BlockSpec((tm, D_MODEL), lambda i: (i, 0))
    return _pcall(
        body, name=name, grid=(t_rows // tm,),
        in_specs=[row, pl.BlockSpec((1, D_MODEL), lambda i: (0, 0)),
                  pl.BlockSpec((N_DEV, D_MODEL, IN_BLK), lambda i: (0, 0, 0))],
        out_specs=[row, pl.BlockSpec((tm, IN_WIDTH), lambda i: (i, 0))],
        out_shape=[jax.ShapeDtypeStruct((t_rows, D_MODEL), BF16), jax.ShapeDtypeStruct((t_rows, IN_WIDTH), BF16)],
        compiler_params=_cp(("arbitrary",), VMEM_BIG),
    )(h, norm, wing)


def mix_backward_act(dh, h, norm, dproj, w_in_full, tm, name):
    t_rows = h.shape[0]

    def body(dh_ref, h_ref, g_ref, dp_ref, w_ref, dhin_ref, dg_ref):
        dx, dg = _rms_bwd(h_ref[...], g_ref[...], _dot_nt(dp_ref[...], w_ref[...]))
        dhin_ref[...] = dh_ref[...] + dx
        _accumulate(dg_ref, dg, pl.program_id(0) == 0)

    row = pl.BlockSpec((tm, D_MODEL), lambda i: (i, 0))
    vec = pl.BlockSpec((1, D_MODEL), lambda i: (0, 0))
    return _pcall(
        body, name=name, grid=(t_rows // tm,),
        in_specs=[row, row, vec, pl.BlockSpec((tm, IN_WIDTH), lambda i: (i, 0)), _resident((D_MODEL, IN_WIDTH))],
        out_specs=[row, vec],
        out_shape=[jax.ShapeDtypeStruct((t_rows, D_MODEL), F32), jax.ShapeDtypeStruct((1, D_MODEL), F32)],
        compiler_params=_cp(("arbitrary",), VMEM_BIG),
    )(dh, h, norm, dproj, w_in_full)


def mix_backward_weights(hn, dproj, tm, name):
    t_rows = hn.shape[0]
    ni = t_rows // tm
    per_step = 2

    kc = _row_tile(tm, 688)

    def body(hn_ref, dp_ref, dw_ref, acc):
        i = pl.program_id(1)
        part = functools.reduce(lambda u, w: u + w, [_dot_tn(hn_ref[r0:r0 + kc, :], dp_ref[r0:r0 + kc, :])
                                                    for r0 in range(0, tm, kc)])
        _accumulate(acc, part, i == 0)

        @pl.when(i == ni - 1)
        def _():
            for k in range(per_step):
                dw_ref[k] = acc[:, k * IN_BLK:(k + 1) * IN_BLK].astype(BF16)

    return _pcall(
        body, name=name, grid=(N_DEV // per_step, ni),
        in_specs=[pl.BlockSpec((tm, D_MODEL), lambda j, i: (i, 0)),
                  pl.BlockSpec((tm, per_step * IN_BLK), lambda j, i: (i, j))],
        out_specs=pl.BlockSpec((per_step, D_MODEL, IN_BLK), lambda j, i: (j, 0, 0)),
        out_shape=jax.ShapeDtypeStruct((N_DEV, D_MODEL, IN_BLK), BF16),
        scratch_shapes=[pltpu.VMEM((D_MODEL, per_step * IN_BLK), F32)],
        compiler_params=_cp(("arbitrary", "arbitrary"), VMEM_BIG),
    )(hn, dproj)


GELU_C = 0.7978845608028654
GELU_K = 0.044715


def _gelu(x):
    return 0.5 * x * (1.0 + jnp.tanh(GELU_C * (x + GELU_K * (x * x * x))))


def _gelu_and_grad(x):
    th = jnp.tanh(GELU_C * (x + GELU_K * (x * x * x)))
    val = 0.5 * x * (1.0 + th)
    grad = 0.5 * (1.0 + th) + 0.5 * x * (1.0 - th * th) * (GELU_C * (1.0 + 3.0 * GELU_K * (x * x)))
    return val, grad


def merge_forward(h, yraw, attn, proj, glu_a, glu_b, w_out, tm, name):
    t_rows = h.shape[0]

    def body(h_ref, y_ref, at_ref, gate_ref, a_ref, b_ref, wo_ref, out_ref):
        y = _gelu(y_ref[...]).astype(BF16)
        ssm = _dot(y, a_ref[...]) * _sigmoid(_dot(y, b_ref[...]))
        ga = gate_ref[:, :D_MODEL].astype(F32)
        gs = gate_ref[:, D_MODEL:].astype(F32)
        merged = _sigmoid(ga) * at_ref[...].astype(F32) + _sigmoid(gs) * ssm
        out_ref[...] = h_ref[...] + _dot(merged.astype(BF16), wo_ref[...])

    row = pl.BlockSpec((tm, D_MODEL), lambda i: (i, 0))
    glu = pl.BlockSpec((SSM_WIDTH, D_MODEL), lambda i: (0, 0))
    return _pcall(
        body, name=name, grid=(t_rows // tm,),
        in_specs=[row, pl.BlockSpec((tm, SSM_WIDTH), lambda i: (i, 0)), row,
                  pl.BlockSpec((tm, 2 * D_MODEL), lambda i: (i, 1)), glu, glu,
                  pl.BlockSpec((D_MODEL, D_MODEL), lambda i: (0, 0))],
        out_specs=row, out_shape=jax.ShapeDtypeStruct((t_rows, D_MODEL), F32),
        compiler_params=_cp(("arbitrary",), VMEM_BIG),
    )(h, yraw, attn, proj, glu_a, glu_b, w_out)


def merge_backward(dh, yraw, attn, proj, glu_a, glu_b, w_out, after, tm, name):
    t_rows = dh.shape[0]

    def body(dh_ref, y_ref, at_ref, gate_ref, a_ref, b_ref, wo_ref, _,
             dat_ref, dy_ref, dgate_ref, d16_ref, mg_ref, y16_ref, dya_ref, dyb_ref):
        d16 = dh_ref[...].astype(BF16)
        d16_ref[...] = d16
        gel, dgel = _gelu_and_grad(y_ref[...].astype(F32))
        y16 = gel.astype(BF16)
        y16_ref[...] = y16
        dy = None
        for c0, cw in _col_chunks(D_MODEL):
            cols = slice(c0, c0 + cw)
            gcols = slice(D_MODEL + c0, D_MODEL + c0 + cw)
            dmerged = _dot_nt(d16, wo_ref[cols, :])
            ya = _dot(y16, a_ref[:, cols])
            sb = _sigmoid(_dot(y16, b_ref[:, cols]))
            ssm = ya * sb
            sa = _sigmoid(gate_ref[:, cols].astype(F32))
            ss = _sigmoid(gate_ref[:, gcols].astype(F32))
            attn_v = at_ref[:, cols].astype(F32)
            mg_ref[:, cols] = (sa * attn_v + ss * ssm).astype(BF16)
            dat_ref[:, cols] = (dmerged * sa).astype(BF16)
            dgate_ref[:, cols] = (dmerged * attn_v * sa * (1.0 - sa)).astype(BF16)
            dgate_ref[:, gcols] = (dmerged * ssm * ss * (1.0 - ss)).astype(BF16)
            dssm = dmerged * ss
            dya = (dssm * sb).astype(BF16)
            dyb = (dssm * ya * sb * (1.0 - sb)).astype(BF16)
            dya_ref[:, cols] = dya
            dyb_ref[:, cols] = dyb
            part = _dot_nt(dya, a_ref[:, cols]) + _dot_nt(dyb, b_ref[:, cols])
            dy = part if dy is None else dy + part
        dy_ref[...] = (dy * dgel).astype(BF16)

    row = pl.BlockSpec((tm, D_MODEL), lambda i: (i, 0))
    ssm_row = pl.BlockSpec((tm, SSM_WIDTH), lambda i: (i, 0))
    gates = pl.BlockSpec((tm, 2 * D_MODEL), lambda i: (i, 1))
    wide = jax.ShapeDtypeStruct((t_rows, D_MODEL), BF16)
    narrow = jax.ShapeDtypeStruct((t_rows, SSM_WIDTH), BF16)
    return _pcall(
        body, name=name, grid=(t_rows // tm,),
        in_specs=[row, ssm_row, row, gates, _resident((SSM_WIDTH, D_MODEL)), _resident((SSM_WIDTH, D_MODEL)),
                  _resident((D_MODEL, D_MODEL)), ANY_SPEC],
        out_specs=[row, ssm_row, gates, row, row, ssm_row, row, row],
        out_shape=[wide, narrow, jax.ShapeDtypeStruct((t_rows, IN_WIDTH), BF16), wide, wide, narrow, wide, wide],
        compiler_params=_cp(("arbitrary",), VMEM_BIG),
    )(dh, yraw, attn, proj, glu_a, glu_b, w_out, after)


def merge_backward_weights(d16, merged, y16, dya, dyb, tm, name):
    t_rows = d16.shape[0]

    def body(d_ref, mg_ref, y_ref, dya_ref, dyb_ref, dwo_ref, da_ref, db_ref):
        first = pl.program_id(0) == 0
        y16 = y_ref[...]
        _accumulate(dwo_ref, _dot_tn(mg_ref[...], d_ref[...]), first)
        _accumulate(da_ref, _dot_tn(y16, dya_ref[...]), first)
        _accumulate(db_ref, _dot_tn(y16, dyb_ref[...]), first)

    row = pl.BlockSpec((tm, D_MODEL), lambda i: (i, 0))
    ssm_row = pl.BlockSpec((tm, SSM_WIDTH), lambda i: (i, 0))
    glu = pl.BlockSpec((SSM_WIDTH, D_MODEL), lambda i: (0, 0))
    wo = pl.BlockSpec((D_MODEL, D_MODEL), lambda i: (0, 0))
    return _pcall(
        body, name=name, grid=(t_rows // tm,),
        in_specs=[row, row, ssm_row, row, row], out_specs=[wo, glu, glu],
        out_shape=[jax.ShapeDtypeStruct((D_MODEL, D_MODEL), F32), jax.ShapeDtypeStruct((SSM_WIDTH, D_MODEL), F32),
                   jax.ShapeDtypeStruct((SSM_WIDTH, D_MODEL), F32)],
        compiler_params=_cp(("arbitrary",), VMEM_BIG),
    )(d16, merged, y16, dya, dyb)


def final_loss_backward(h, target, norm, seq, tm, name):
    t_rows = h.shape[0]
    tiles_per_example = (seq + N_META) // tm

    def body(h_ref, t_ref, g_ref, dh_ref, loss_ref, dg_ref):
        i = pl.program_id(0)
        x = h_ref[...]
        g = g_ref[...]
        r = lax.rsqrt(jnp.mean(x * x, axis=-1, keepdims=True) + NORM_EPS)
        xh = x * r
        pos = lax.broadcasted_iota(jnp.int32, (tm, 1), 0) + (i % tiles_per_example) * tm
        diff = jnp.where(pos < seq, xh * g - t_ref[...], 0.0)
        part = 0.5 * jnp.sum(jnp.sum(diff * diff, axis=-1, keepdims=True), axis=0, keepdims=True) / D_MODEL
        dy = diff / D_MODEL
        t = dy * g
        dh_ref[...] = r * (t - xh * jnp.mean(t * xh, axis=-1, keepdims=True))
        _accumulate(loss_ref, jnp.broadcast_to(part, (1, LANES)), i == 0)
        _accumulate(dg_ref, jnp.sum(dy * xh, axis=0, keepdims=True), i == 0)

    row = pl.BlockSpec((tm, D_MODEL), lambda i: (i, 0))
    vec = pl.BlockSpec((1, D_MODEL), lambda i: (0, 0))
    return _pcall(
        body, name=name, grid=(t_rows // tm,),
        in_specs=[row, row, vec],
        out_specs=[row, pl.BlockSpec((1, LANES), lambda i: (0, 0)), vec],
        out_shape=[jax.ShapeDtypeStruct((t_rows, D_MODEL), F32), jax.ShapeDtypeStruct((1, LANES), F32),
                   jax.ShapeDtypeStruct((1, D_MODEL), F32)],
        compiler_params=_cp(("arbitrary",), VMEM_BIG),
    )(h, target, norm)


ATTN_SCALE = HEAD_DIM ** -0.5
STACK_HEADS = (0, 2, 1, 3)


def _lane_half(shape, hf):
    lane = lax.broadcasted_iota(jnp.int32, shape, 1)
    return (lane < HEAD_DIM) if hf == 0 else (lane >= HEAD_DIM)


def _kv_variants(ref, rows, kh):
    tile = kh // 2
    t = ref[rows, tile * LANES:(tile + 1) * LANES].astype(F32)
    swapped = pltpu.roll(t, HEAD_DIM, axis=1)
    at_low, at_high = (t, swapped) if kh % 2 == 0 else (swapped, t)
    lo = jnp.where(_lane_half(t.shape, 0), at_low, 0.0).astype(BF16)
    hi = jnp.where(_lane_half(t.shape, 1), at_high, 0.0).astype(BF16)
    return lo, hi


def _to_kv_lanes(lo, hi, kh):
    lo = jnp.where(_lane_half(lo.shape, 0), lo, 0.0)
    hi = jnp.where(_lane_half(hi.shape, 1), hi, 0.0)
    if kh % 2 == 0:
        return lo + pltpu.roll(hi, HEAD_DIM, axis=1)
    return pltpu.roll(lo, HEAD_DIM, axis=1) + hi


def _stacked(ref, rows, kh):
    col = kh * 2 * LANES
    return jnp.concatenate([ref[rows, col:col + LANES], ref[rows, col + LANES:col + 2 * LANES]], axis=0)


def _sink_column(sink_ref, kh, nq):
    row = lax.broadcasted_iota(jnp.int32, (4 * nq, 1), 0)
    col = jnp.zeros((4 * nq, 1), F32)
    for quarter, g in enumerate(STACK_HEADS):
        col = jnp.where(row // nq == quarter, sink_ref[0, kh * Q_PER_KV + g], col)
    return col


def _softmax_parts(qs, key_tiles, masks, sink):
    scores = []
    for (k_lo, k_hi), mask in zip(key_tiles, masks):
        s = jnp.concatenate([_dot_nt(qs, k_lo), _dot_nt(qs, k_hi)], axis=0) * ATTN_SCALE
        scores.append(s if mask is None else jnp.where(mask, s, NEG_INF))
    m = functools.reduce(jnp.maximum, [jnp.max(s, axis=-1, keepdims=True) for s in scores])
    m = jnp.maximum(m, sink)
    probs = [jnp.exp(s - m) for s in scores]
    e_sink = jnp.exp(sink - m)
    den = _row_sums(probs) + e_sink
    return probs, 1.0 / den, e_sink


def _row_sums(tiles):
    return functools.reduce(lambda u, w: u + w, [jnp.sum(t, axis=-1, keepdims=True) for t in tiles])


def _band_mask(nq, first):
    keys = BLOCK if first else 2 * BLOCK
    qi = lax.broadcasted_iota(jnp.int32, (4 * nq, keys), 0) % nq
    kj = lax.broadcasted_iota(jnp.int32, (4 * nq, keys), 1)
    if first:
        return kj <= qi
    return jnp.logical_and(kj > qi, kj <= qi + BLOCK)


def _meta_mask():
    qi = lax.broadcasted_iota(jnp.int32, (4 * N_META, N_META), 0) % N_META
    kj = lax.broadcasted_iota(jnp.int32, (4 * N_META, N_META), 1)
    return kj <= qi


def _attention_schedule(seq, queries, carry):
    meta_rows = pl.ds(seq, N_META)
    carry = queries(pl.ds(0, BLOCK), BLOCK, [pl.ds(0, BLOCK), meta_rows], [_band_mask(BLOCK, True), None], carry)

    def block(n, c):
        r0 = pl.multiple_of(n * BLOCK, BLOCK)
        p0 = pl.multiple_of((n - 1) * BLOCK, BLOCK)
        return queries(pl.ds(r0, BLOCK), BLOCK, [pl.ds(p0, 2 * BLOCK), meta_rows], [_band_mask(BLOCK, False), None], c)

    carry = lax.fori_loop(1, seq // BLOCK, block, carry)
    return queries(meta_rows, N_META, [meta_rows], [_meta_mask()], carry)


def attention_forward(proj3, sinks, seq, name):
    n_b, n_l, _ = proj3.shape

    def body(sink_ref, q_ref, k_ref, v_ref, o_ref):
        def queries(q_rows, nq, key_rows, masks, carry):
            for kh in range(N_KV_HEADS):
                ks = [_kv_variants(k_ref, r, kh) for r in key_rows]
                vs = [_kv_variants(v_ref, r, kh) for r in key_rows]
                qs = _stacked(q_ref, q_rows, kh)
                probs, inv, _ = _softmax_parts(qs, ks, masks, _sink_column(sink_ref, kh, nq))
                probs = [p.astype(BF16) for p in probs]
                o_lo = functools.reduce(lambda u, w: u + w, [_dot(p[:2 * nq], v_lo) for p, (v_lo, _) in zip(probs, vs)])
                o_hi = functools.reduce(lambda u, w: u + w, [_dot(p[2 * nq:], v_hi) for p, (_, v_hi) in zip(probs, vs)])
                out = (o_lo * inv[:2 * nq] + o_hi * inv[2 * nq:]).astype(BF16)
                col = kh * 2 * LANES
                o_ref[q_rows, col:col + LANES] = out[:nq]
                o_ref[q_rows, col + LANES:col + 2 * LANES] = out[nq:]
            return carry

        _attention_schedule(seq, queries, 0)

    return _pcall(
        body, name=name, grid=(n_b,),
        in_specs=[pl.BlockSpec(memory_space=pltpu.SMEM),
                  pl.BlockSpec((None, n_l, D_MODEL), lambda b: (b, 0, 0)),
                  pl.BlockSpec((None, n_l, KV_WIDTH), lambda b: (b, 0, D_MODEL // KV_WIDTH)),
                  pl.BlockSpec((None, n_l, KV_WIDTH), lambda b: (b, 0, D_MODEL // KV_WIDTH + 1))],
        out_specs=pl.BlockSpec((None, n_l, D_MODEL), lambda b: (b, 0, 0)),
        out_shape=jax.ShapeDtypeStruct((n_b, n_l, D_MODEL), BF16),
        compiler_params=_cp(("arbitrary",), VMEM_BIG),
    )(sinks, proj3, proj3, proj3)


def attention_backward(proj3, dattn3, dproj3, sinks, after, seq, name):
    n_b, n_l, _ = proj3.shape
    qkv_width = D_MODEL + 2 * KV_WIDTH

    def body(sink_ref, q_ref, k_ref, v_ref, do_ref, _, __, dqkv_ref, dsink_ref, dk_ref, dv_ref):
        dk_ref[...] = jnp.zeros_like(dk_ref)
        dv_ref[...] = jnp.zeros_like(dv_ref)
        sub = lax.broadcasted_iota(jnp.int32, (SUBLANES, LANES), 0)
        lane = lax.broadcasted_iota(jnp.int32, (SUBLANES, LANES), 1)

        def queries(q_rows, nq, key_rows, masks, dsink):
            for kh in range(N_KV_HEADS):
                ks = [_kv_variants(k_ref, r, kh) for r in key_rows]
                vs = [_kv_variants(v_ref, r, kh) for r in key_rows]
                qs = _stacked(q_ref, q_rows, kh)
                dos = _stacked(do_ref, q_rows, kh)
                probs, inv, e_sink = _softmax_parts(qs, ks, masks, _sink_column(sink_ref, kh, nq))
                probs = [p * inv for p in probs]
                dps = [jnp.concatenate([_dot_nt(dos, v_lo), _dot_nt(dos, v_hi)], axis=0) for v_lo, v_hi in vs]
                delta = _row_sums([p * dp for p, dp in zip(probs, dps)])
                d_sink = -(e_sink * inv) * delta
                for quarter, g in enumerate(STACK_HEADS):
                    d_here = jnp.sum(d_sink[quarter * nq:(quarter + 1) * nq], axis=0, keepdims=True)
                    dsink = dsink + jnp.where(jnp.logical_and(sub == 0, lane == kh * Q_PER_KV + g), d_here, 0.0)
                dq = None
                tile = slice((kh // 2) * LANES, (kh // 2 + 1) * LANES)
                for r, p, dp, (k_lo, k_hi) in zip(key_rows, probs, dps, ks):
                    ds = (p * (dp - delta)).astype(BF16)
                    p16 = p.astype(BF16)
                    dq_x = _dot(ds[:2 * nq], k_lo) + _dot(ds[2 * nq:], k_hi)
                    dq = dq_x if dq is None else dq + dq_x
                    dk_ref[r, tile] += _to_kv_lanes(_dot_tn(ds[:2 * nq], qs), _dot_tn(ds[2 * nq:], qs), kh) * ATTN_SCALE
                    dv_ref[r, tile] += _to_kv_lanes(_dot_tn(p16[:2 * nq], dos), _dot_tn(p16[2 * nq:], dos), kh)
                dq = (dq * ATTN_SCALE).astype(BF16)
                col = kh * 2 * LANES
                dqkv_ref[q_rows, col:col + LANES] = dq[:nq]
                dqkv_ref[q_rows, col + LANES:col + 2 * LANES] = dq[nq:]
            return dsink

        dsink_ref[...] = _attention_schedule(seq, queries, jnp.zeros((SUBLANES, LANES), F32))
        dqkv_ref[:, D_MODEL:D_MODEL + KV_WIDTH] = dk_ref[...].astype(BF16)
        dqkv_ref[:, D_MODEL + KV_WIDTH:] = dv_ref[...].astype(BF16)

    return _pcall(
        body, name=name, grid=(n_b,),
        in_specs=[pl.BlockSpec(memory_space=pltpu.SMEM),
                  pl.BlockSpec((None, n_l, D_MODEL), lambda b: (b, 0, 0)),
                  pl.BlockSpec((None, n_l, KV_WIDTH), lambda b: (b, 0, D_MODEL // KV_WIDTH)),
                  pl.BlockSpec((None, n_l, KV_WIDTH), lambda b: (b, 0, D_MODEL // KV_WIDTH + 1)),
                  pl.BlockSpec((None, n_l, D_MODEL), lambda b: (b, 0, 0)),
                  ANY_SPEC, ANY_SPEC],
        out_specs=[pl.BlockSpec((None, n_l, qkv_width), lambda b: (b, 0, 0)),
                   pl.BlockSpec((None, SUBLANES, LANES), lambda b: (b, 0, 0))],
        out_shape=[jax.ShapeDtypeStruct(dproj3.shape, BF16), jax.ShapeDtypeStruct((n_b, SUBLANES, LANES), F32)],
        scratch_shapes=[pltpu.VMEM((n_l, KV_WIDTH), F32), pltpu.VMEM((n_l, KV_WIDTH), F32)],
        input_output_aliases={5: 0},
        compiler_params=_cp(("arbitrary",), VMEM_BIG),
    )(sinks, proj3, proj3, proj3, dattn3, dproj3, after)


TAB_ROWS = 8
SCAN_UNROLL = 4


def _cmul(ar, ai, br, bi):
    return ar * br - ai * bi, ar * bi + ai * br


def _discretise(ar, ai, ls):
    step = jnp.exp(ls)
    mag = jnp.exp(ar * step)
    ang = ai * step
    cos, sin = jnp.cos(ang), jnp.sin(ang)
    lr, li = mag * cos, mag * sin
    den = ar * ar + ai * ai
    nr, ni = lr - 1.0, li
    cr = (nr * ar + ni * ai) / den
    ci = (ni * ar - nr * ai) / den
    return step, mag, lr, li, den, nr, ni, cr, ci


def _scan_tables(lr, li, reverse):
    n = lr.shape[-1]
    pw = [(lr, li)]
    for _ in range(SUBLANES - 1):
        pw.append(_cmul(pw[-1][0], pw[-1][1], lr, li))
    row = lax.broadcasted_iota(jnp.int32, (SUBLANES, n), 0)
    out = []
    for d in (1, 2, 4):
        ok = (row + d <= SUBLANES - 1) if reverse else (row >= d)
        out += [jnp.where(ok, pw[d - 1][0], 0.0), jnp.where(ok, pw[d - 1][1], 0.0)]
    cr = jnp.zeros((SUBLANES, n), F32)
    ci = jnp.zeros((SUBLANES, n), F32)
    for r in range(SUBLANES):
        e = (SUBLANES - r) if reverse else (r + 1)
        cr = jnp.where(row == r, pw[e - 1][0], cr)
        ci = jnp.where(row == r, pw[e - 1][1], ci)
    return out + [cr, ci]


def ssm_prepare(ar, ai, ls, br_t, bi_t, name):
    def body(ar_ref, ai_ref, ls_ref, br_ref, bi_ref, bbr_ref, bbi_ref, tf_ref, tr_ref):
        _, _, lr, li, _, _, _, cr, ci = _discretise(ar_ref[...], ai_ref[...], ls_ref[...])
        br, bi = br_ref[...], bi_ref[...]
        bbr_ref[...] = cr * br - ci * bi
        bbi_ref[...] = cr * bi + ci * br
        for k, t in enumerate(_scan_tables(lr, li, False)):
            tf_ref[k] = t
        for k, t in enumerate(_scan_tables(lr, -li, True)):
            tr_ref[k] = t

    return _pcall(
        body, name=name,
        out_shape=[jax.ShapeDtypeStruct((SSM_GROUP, N_STATES), F32), jax.ShapeDtypeStruct((SSM_GROUP, N_STATES), F32),
                   jax.ShapeDtypeStruct((TAB_ROWS, SUBLANES, N_STATES), F32),
                   jax.ShapeDtypeStruct((TAB_ROWS, SUBLANES, N_STATES), F32)],
    )(ar, ai, ls, br_t, bi_t)


def ssm_param_backward(ar, ai, ls, br_t, bi_t, dlr_p, dli_p, dbbr, dbbi, group_sum, name):
    def body(ar_ref, ai_ref, ls_ref, br_ref, bi_ref, dlr_ref, dli_ref, dbbr_ref, dbbi_ref, gs_ref,
             dar_ref, dai_ref, dls_ref, dbr_ref, dbi_ref):
        ar, ai = ar_ref[...], ai_ref[...]
        step, mag, lr, li, den, nr, ni, cr, ci = _discretise(ar, ai, ls_ref[...])
        br, bi, dbbr_v, dbbi_v = br_ref[...], bi_ref[...], dbbr_ref[...], dbbi_ref[...]
        dbr_ref[...] = cr * dbbr_v + ci * dbbi_v
        dbi_ref[...] = cr * dbbi_v - ci * dbbr_v
        dcr = jnp.sum(dbbr_v * br + dbbi_v * bi, axis=0, keepdims=True)
        dci = jnp.sum(dbbi_v * br - dbbr_v * bi, axis=0, keepdims=True)
        dnr = (dcr * ar - dci * ai) / den
        dni = (dcr * ai + dci * ar) / den
        dden = -(cr * dcr + ci * dci) / den
        dar = (dcr * nr + dci * ni) / den + dden * 2.0 * ar
        dai = (dcr * ni - dci * nr) / den + dden * 2.0 * ai
        dlr = jnp.sum(dlr_ref[...], axis=0, keepdims=True) + dnr
        dli = jnp.sum(dli_ref[...], axis=0, keepdims=True) + dni
        dmag = (dlr * lr + dli * li) / mag
        dang = dli * lr - dlr * li
        dar_ref[...] = dar + dmag * mag * step
        dai_ref[...] = dai + dang * step
        dstep = dmag * mag * ar + dang * ai
        dls_ref[...] = jnp.dot(dstep * step, gs_ref[...], preferred_element_type=F32, precision=lax.Precision.HIGHEST)

    vec = jax.ShapeDtypeStruct((1, N_STATES), F32)
    mat = jax.ShapeDtypeStruct((SSM_GROUP, N_STATES), F32)
    return _pcall(body, name=name, out_shape=[vec, vec, jax.ShapeDtypeStruct((1, LANES), F32), mat, mat])(
        ar, ai, ls, br_t, bi_t, dlr_p, dli_p, dbbr, dbbi, group_sum)


def _scan_rows(a, b, tabs, carry, reverse):
    for k, d in enumerate((1, 2, 4)):
        shift = SUBLANES - d if reverse else d
        sr, si = pltpu.roll(a, shift, axis=0), pltpu.roll(b, shift, axis=0)
        pr, pi = _cmul(tabs[2 * k], tabs[2 * k + 1], sr, si)
        a, b = a + pr, b + pi
    pr, pi = _cmul(tabs[6], tabs[7], carry[0], carry[1])
    return a + pr, b + pi


def _time_groups(seq, reverse):
    meta = [seq + SUBLANES * g for g in range(N_META // SUBLANES)]
    return meta[::-1] if reverse else meta


def ssm_forward_scan(proj3, b_comb, tabf, c_comb, dvec, seq, name):
    n_b, n_l, _ = proj3.shape
    u_blk = (D_MODEL + 2 * KV_WIDTH) // LANES

    def body(u_ref, b_ref, tab_ref, c_ref, d_ref, x_ref, y_ref, bu, xs):
        j = pl.program_id(1)
        u = u_ref[...]
        bu[...] = _dot(u, b_ref[...])
        tabs = [tab_ref[k] for k in range(TAB_ROWS)]

        def group(r0, carry):
            rows = pl.ds(r0, SUBLANES)
            a, b = _scan_rows(bu[rows, :SCAN_COLS], bu[rows, SCAN_COLS:], tabs, carry, False)
            xs[rows, :SCAN_COLS] = a
            xs[rows, SCAN_COLS:] = b
            return (jnp.broadcast_to(a[SUBLANES - 1:, :], a.shape), jnp.broadcast_to(b[SUBLANES - 1:, :], b.shape))

        zero = jnp.zeros((SUBLANES, SCAN_COLS), F32)
        carry = (zero, zero)
        for r0 in _time_groups(seq, False):
            carry = group(r0, carry)
        span = SCAN_UNROLL * SUBLANES

        def groups(t, c):
            for k in range(SCAN_UNROLL):
                c = group(pl.multiple_of(t * span, span) + k * SUBLANES, c)
            return c

        lax.fori_loop(0, seq // span, groups, carry)
        x16 = xs[...].astype(BF16)
        x_ref[...] = x16
        contrib = _dot(x16, c_ref[...])

        @pl.when(j % 2 == 0)
        def _():
            y_ref[...] = contrib + d_ref[...] * u.astype(F32)

        @pl.when(j % 2 == 1)
        def _():
            y_ref[...] += contrib

    return _pcall(
        body, name=name, grid=(n_b, N_SCAN_BLK),
        in_specs=[pl.BlockSpec((None, n_l, LANES), lambda b, j: (b, 0, u_blk + j // 2)),
                  pl.BlockSpec((LANES, 2 * SCAN_COLS), lambda b, j: (j // 2, j)),
                  pl.BlockSpec((TAB_ROWS, SUBLANES, SCAN_COLS), lambda b, j: (0, 0, j)),
                  pl.BlockSpec((2 * SCAN_COLS, LANES), lambda b, j: (j, j // 2)),
                  pl.BlockSpec((1, LANES), lambda b, j: (0, j // 2))],
        out_specs=[pl.BlockSpec((None, n_l, 2 * SCAN_COLS), lambda b, j: (b, 0, j)),
                   pl.BlockSpec((None, n_l, LANES), lambda b, j: (b, 0, j // 2))],
        out_shape=[jax.ShapeDtypeStruct((n_b, n_l, 2 * N_STATES), BF16),
                   jax.ShapeDtypeStruct((n_b, n_l, SSM_WIDTH), F32)],
        scratch_shapes=[pltpu.VMEM((n_l, 2 * SCAN_COLS), F32)] * 2,
        compiler_params=_cp(("arbitrary", "arbitrary"), VMEM_BIG),
    )(proj3, b_comb, tabf, c_comb, dvec)


def ssm_backward_scan(dyraw3, xs3, dproj3, c_comb_t, tabr, b_comb_t, dvec, seq, name):
    n_b, n_l, _ = xs3.shape
    u_blk = (D_MODEL + 2 * KV_WIDTH) // LANES

    def body(dy_ref, x_ref, _, c_ref, tab_ref, b_ref, d_ref, du_ref, g_ref, dlr_ref, dli_ref, dx, gs, xs, du_acc):
        j = pl.program_id(1)
        dy = dy_ref[...]
        dx[...] = _dot(dy, c_ref[...])
        xs[...] = x_ref[...].astype(F32)
        tabs = [tab_ref[k] for k in range(TAB_ROWS)]
        last_row = lax.broadcasted_iota(jnp.int32, (SUBLANES, SCAN_COLS), 0) == SUBLANES - 1

        def group(r0, state):
            cr, ci, acc_r, acc_i = state
            rows = pl.ds(r0, SUBLANES)
            a, b = _scan_rows(dx[rows, :SCAN_COLS], dx[rows, SCAN_COLS:], tabs, (cr, ci), True)
            gs[rows, :SCAN_COLS] = a
            gs[rows, SCAN_COLS:] = b
            na = jnp.where(last_row, cr, pltpu.roll(a, SUBLANES - 1, axis=0))
            nb = jnp.where(last_row, ci, pltpu.roll(b, SUBLANES - 1, axis=0))
            xa, xb = xs[rows, :SCAN_COLS], xs[rows, SCAN_COLS:]
            return (jnp.broadcast_to(a[:1, :], a.shape), jnp.broadcast_to(b[:1, :], b.shape),
                    acc_r + na * xa + nb * xb, acc_i + nb * xa - na * xb)

        zero = jnp.zeros((SUBLANES, SCAN_COLS), F32)
        span = SCAN_UNROLL * SUBLANES
        n_spans = seq // span

        def groups(t, s):
            for k in reversed(range(SCAN_UNROLL)):
                s = group(pl.multiple_of((n_spans - 1 - t) * span, span) + k * SUBLANES, s)
            return s

        state = lax.fori_loop(0, n_spans, groups, (zero, zero, zero, zero))
        for r0 in _time_groups(seq, True):
            state = group(r0, state)
        dlr_ref[...] = state[2]
        dli_ref[...] = state[3]
        g16 = gs[...].astype(BF16)
        g_ref[...] = g16
        contrib = _dot(g16, b_ref[...])

        @pl.when(j % 2 == 0)
        def _():
            du_acc[...] = contrib + d_ref[...] * dy.astype(F32)

        @pl.when(j % 2 == 1)
        def _():
            du_ref[...] = (du_acc[...] + contrib).astype(BF16)

    state_blk = pl.BlockSpec((None, n_l, 2 * SCAN_COLS), lambda b, j: (b, 0, j))
    dl_blk = pl.BlockSpec((None, SUBLANES, SCAN_COLS), lambda b, j: (b, 0, j))
    return _pcall(
        body, name=name, grid=(n_b, N_SCAN_BLK),
        in_specs=[pl.BlockSpec((None, n_l, LANES), lambda b, j: (b, 0, j // 2)), state_blk,
                  pl.BlockSpec(memory_space=pl.ANY),
                  pl.BlockSpec((LANES, 2 * SCAN_COLS), lambda b, j: (j // 2, j)),
                  pl.BlockSpec((TAB_ROWS, SUBLANES, SCAN_COLS), lambda b, j: (0, 0, j)),
                  pl.BlockSpec((2 * SCAN_COLS, LANES), lambda b, j: (j, j // 2)),
                  pl.BlockSpec((1, LANES), lambda b, j: (0, j // 2))],
        out_specs=[pl.BlockSpec((None, n_l, LANES), lambda b, j: (b, 0, u_blk + j // 2)), state_blk, dl_blk, dl_blk],
        out_shape=[jax.ShapeDtypeStruct(dproj3.shape, BF16), jax.ShapeDtypeStruct((n_b, n_l, 2 * N_STATES), BF16),
                   jax.ShapeDtypeStruct((n_b, SUBLANES, N_STATES), F32), jax.ShapeDtypeStruct((n_b, SUBLANES, N_STATES), F32)],
        scratch_shapes=[pltpu.VMEM((n_l, 2 * SCAN_COLS), F32)] * 3 + [pltpu.VMEM((n_l, LANES), F32)],
        input_output_aliases={2: 0},
        compiler_params=_cp(("arbitrary", "arbitrary"), VMEM_BIG),
    )(dyraw3, xs3, dproj3, c_comb_t, tabr, b_comb_t, dvec)


def ssm_param_grads(proj, gs, xs, dyraw, tm, name):
    t_rows = proj.shape[0]
    ni = t_rows // tm
    u_blk = (D_MODEL + 2 * KV_WIDTH) // LANES
    width = 2 * SCAN_COLS

    def body(u_ref, g_ref, x_ref, dy_ref, db_ref, dc_ref, dd_ref):
        cb, i = pl.program_id(0), pl.program_id(1)
        u, dy = u_ref[...], dy_ref[...]
        _accumulate(db_ref, _dot_tn(u, g_ref[...]), i == 0)
        _accumulate(dc_ref, _dot_tn(x_ref[...], dy), i == 0)

        @pl.when(cb % 2 == 0)
        def _():
            _accumulate(dd_ref, jnp.sum(dy.astype(F32) * u.astype(F32), axis=0, keepdims=True), i == 0)

    return _pcall(
        body, name=name, grid=(N_SCAN_BLK, ni),
        in_specs=[pl.BlockSpec((tm, LANES), lambda cb, i: (i, u_blk + cb // 2)),
                  pl.BlockSpec((tm, width), lambda cb, i: (i, cb)),
                  pl.BlockSpec((tm, width), lambda cb, i: (i, cb)),
                  pl.BlockSpec((tm, LANES), lambda cb, i: (i, cb // 2))],
        out_specs=[pl.BlockSpec((None, LANES, width), lambda cb, i: (cb, 0, 0)),
                   pl.BlockSpec((None, width, LANES), lambda cb, i: (cb, 0, 0)),
                   pl.BlockSpec((1, LANES), lambda cb, i: (0, cb // 2))],
        out_shape=[jax.ShapeDtypeStruct((N_SCAN_BLK, LANES, width), F32),
                   jax.ShapeDtypeStruct((N_SCAN_BLK, width, LANES), F32), jax.ShapeDtypeStruct((1, SSM_WIDTH), F32)],
        compiler_params=_cp(("arbitrary", "arbitrary"), VMEM_BIG),
    )(proj, gs, xs, dyraw)


def sum_leading(x, name):
    def body(x_ref, o_ref):
        acc = x_ref[0]
        for k in range(1, x.shape[0]):
            acc = acc + x_ref[k]
        o_ref[...] = acc

    return _pcall(body, name=name, out_shape=jax.ShapeDtypeStruct(x.shape[1:], x.dtype))(x)


WEIGHTS = ['meta_tokens', 'ffn1_norm', 'ffn1_w1', 'ffn1_w3', 'ffn1_w2', 'mix_norm', 'w_in', 'attn_sinks', 'ssm_a_re',
           'ssm_a_im', 'ssm_log_step', 'ssm_b_re', 'ssm_b_im', 'ssm_c_re', 'ssm_c_im', 'ssm_d', 'ssm_glu_a', 'ssm_glu_b',
           'w_out', 'ffn2_norm', 'ffn2_w1', 'ffn2_w3', 'ffn2_w2', 'final_norm']
SHARDED = ['ffn1_w1', 'ffn1_w3', 'ffn1_w2', 'ffn2_w1', 'ffn2_w3', 'ffn2_w2', 'w_in', 'ssm_glu_a', 'ssm_glu_b', 'w_out']
REPLICATED = ['ffn1_norm', 'mix_norm', 'ffn2_norm', 'final_norm', 'attn_sinks', 'ssm_a_re', 'ssm_a_im', 'ssm_log_step',
              'ssm_b_re', 'ssm_b_im', 'ssm_c_re', 'ssm_c_im', 'ssm_d']
PACK_COLS = 1024


def _block_diag(blocks):
    g, r, c = blocks.shape
    eye = jnp.eye(g, dtype=blocks.dtype)
    return (blocks[:, :, None, :] * eye[:, None, :, None]).reshape(g * r, g * c)


def _scan_order(re, im):
    r = re.shape[0]
    return jnp.stack([re.reshape(r, N_SCAN_BLK, SCAN_COLS), im.reshape(r, N_SCAN_BLK, SCAN_COLS)], axis=2).reshape(r, 2 * N_STATES)


def _pack(arrays):
    parts = []
    for a in arrays:
        flat = a.reshape(-1)
        chunk = SUBLANES * PACK_COLS
        padded = -(-flat.shape[0] // chunk) * chunk
        parts.append(jnp.pad(flat, (0, padded - flat.shape[0])).reshape(-1, PACK_COLS))
    return jnp.concatenate(parts, axis=0)


def _unpack(packed, shapes):
    out, row = [], 0
    for shape in shapes:
        size = 1
        for s in shape:
            size *= s
        chunk = SUBLANES * PACK_COLS
        rows = -(-size // chunk) * SUBLANES
        out.append(packed[row:row + rows].reshape(-1)[:size].reshape(shape))
        row += rows
    return out


def kernel(x, meta_tokens, ffn1_norm, ffn1_w1, ffn1_w3, ffn1_w2, mix_norm, w_in, attn_sinks, ssm_a_re, ssm_a_im, ssm_log_step, ssm_b_re, ssm_b_im, ssm_c_re, ssm_c_im, ssm_d, ssm_glu_a, ssm_glu_b, w_out, ffn2_norm, ffn2_w1, ffn2_w3, ffn2_w2, final_norm, loss_target, m_meta_tokens, m_ffn1_norm, m_ffn1_w1, m_ffn1_w3, m_ffn1_w2, m_mix_norm, m_w_in, m_attn_sinks, m_ssm_a_re, m_ssm_a_im, m_ssm_log_step, m_ssm_b_re, m_ssm_b_im, m_ssm_c_re, m_ssm_c_im, m_ssm_d, m_ssm_glu_a, m_ssm_glu_b, m_w_out, m_ffn2_norm, m_ffn2_w1, m_ffn2_w3, m_ffn2_w2, m_final_norm, v_meta_tokens, v_ffn1_norm, v_ffn1_w1, v_ffn1_w3, v_ffn1_w2, v_mix_norm, v_w_in, v_attn_sinks, v_ssm_a_re, v_ssm_a_im, v_ssm_log_step, v_ssm_b_re, v_ssm_b_im, v_ssm_c_re, v_ssm_c_im, v_ssm_d, v_ssm_glu_a, v_ssm_glu_b, v_w_out, v_ffn2_norm, v_ffn2_w1, v_ffn2_w3, v_ffn2_w2, v_final_norm):
    given = dict(locals())
    w = {n: given[n] for n in WEIGHTS}
    m = {n: given["m_" + n] for n in WEIGHTS}
    v = {n: given["v_" + n] for n in WEIGHTS}

    n_b, seq, _ = x.shape
    n_l = seq + N_META
    t_rows = n_b * n_l
    tm = _row_tile(n_l, 688)
    px, py, pc = _my_place()
    me = 4 * px + 2 * py + pc

    glu = jnp.stack([ssm_glu_a[0], ssm_glu_b[0]]).astype(BF16)
    ffn_names = ['ffn1_w1', 'ffn1_w3', 'ffn1_w2', 'ffn2_w1', 'ffn2_w3', 'ffn2_w2']

    def hidden_on_rows(n, t):
        return t[0] if n.endswith('w2') else t[0].T

    def hidden_on_rows_back(n, t):
        return t[None] if n.endswith('w2') else t.T[None]

    me_idx = jnp.reshape(me, (1,)).astype(jnp.int32)
    first_names, later_names = ffn_names[:3], ffn_names[3:]
    *first, metag = all_gather_list(
        [hidden_on_rows(n, w[n]).astype(BF16) for n in first_names] + [meta_tokens], "ag_first")
    win_send, win_recv, win_shard, win_land, win_token = exchange_start(
        [w_in[0].astype(BF16)], first[0], True, "ag_w_in_start")
    later_shards = [hidden_on_rows(n, w[n]).astype(BF16) for n in later_names] + [glu, w_out[0].astype(BF16)]
    ag_send, ag_recv, later_shards, later_lands, ag_token = exchange_start(later_shards, win_token, True, "ag_later_start")
    full = {n: g.reshape(D_FF, D_MODEL) for n, g in zip(first_names, first)}
    meta_full = metag.transpose(1, 0, 2).reshape(N_META, D_MODEL)

    h0 = jnp.concatenate([x, jnp.broadcast_to(meta_full[None], (n_b, N_META, D_MODEL))], axis=1).reshape(t_rows, D_MODEL)
    target = jnp.concatenate([loss_target, jnp.zeros((n_b, N_META, D_MODEL), F32)], axis=1).reshape(t_rows, D_MODEL)
    final_g = final_norm.reshape(1, D_MODEL)

    ar = ssm_a_re.reshape(1, N_STATES)
    ai = ssm_a_im.reshape(1, N_STATES)
    ls = jnp.repeat(ssm_log_step.reshape(SSM_GROUPS), SSM_STATE).reshape(1, N_STATES)
    br_t = ssm_b_re[0].transpose(2, 0, 1).reshape(SSM_GROUP, N_STATES)
    bi_t = ssm_b_im[0].transpose(2, 0, 1).reshape(SSM_GROUP, N_STATES)
    bbr, bbi, tabf, tabr = ssm_prepare(ar, ai, ls, br_t, bi_t, "ssm_prepare")
    bbr_g = bbr.reshape(SSM_GROUP, SSM_GROUPS, SSM_STATE).transpose(1, 0, 2)
    bbi_g = bbi.reshape(SSM_GROUP, SSM_GROUPS, SSM_STATE).transpose(1, 0, 2)
    b_comb = _scan_order(_block_diag(bbr_g), _block_diag(bbi_g)).astype(BF16)
    c_comb_t = _scan_order(_block_diag(ssm_c_re[0]), -_block_diag(ssm_c_im[0])).astype(BF16)
    b_comb_t, c_comb = b_comb.T, c_comb_t.T

    ffn1_w = (full['ffn1_w1'], full['ffn1_w3'], full['ffn1_w2'])
    h1, hn1, a1, b1 = ffn_forward(h0, ffn1_norm, *ffn1_w, ag_token, tm, "ffn1_fwd")
    (wing,) = exchange_wait(win_send, win_recv, win_shard, win_land, h1, True, "ag_w_in_wait")
    wing = lax.dynamic_update_slice_in_dim(wing, win_shard[0][None], me, axis=0)
    hnm, proj = mix_forward(h1, mix_norm, wing, tm, "mix_fwd")
    proj3 = proj.reshape(n_b, n_l, IN_WIDTH)
    attn3 = attention_forward(proj3, attn_sinks, seq, "attn_fwd")
    attn = attn3.reshape(t_rows, D_MODEL)
    xs3, yraw3 = ssm_forward_scan(proj3, b_comb, tabf, c_comb, ssm_d, seq, "ssm_fwd")
    yraw = yraw3.reshape(t_rows, SSM_WIDTH)
    later = exchange_wait(ag_send, ag_recv, later_shards, later_lands, yraw3, True, "ag_later_wait")
    later = [lax.dynamic_update_slice_in_dim(z, s[None], me, axis=0) for z, s in zip(later, later_shards)]
    for n, g in zip(later_names, later):
        full[n] = g.reshape(D_FF, D_MODEL)
    ffn2_w = (full['ffn2_w1'], full['ffn2_w3'], full['ffn2_w2'])
    glug, wog = later[len(later_names):]
    glu_a = glug[:, 0].transpose(1, 0, 2).reshape(SSM_WIDTH, D_MODEL)
    glu_b = glug[:, 1].transpose(1, 0, 2).reshape(SSM_WIDTH, D_MODEL)
    w_out_full = wog.reshape(D_MODEL, D_MODEL)
    h2 = merge_forward(h1, yraw, attn, proj, glu_a, glu_b, w_out_full, tm, "merge_fwd")
    h3, hn2, a2, b2 = ffn_forward(h2, ffn2_norm, *ffn2_w, ag_token, tm, "ffn2_fwd")
    dh3, loss_part, g_final = final_loss_backward(h3, target, final_g, seq, tm, "loss_bwd")
    loss = lax.psum(loss_part[0, 0], ("x", "y", "c"))

    def blocked_ffn(d_w1t, d_w3t, d_w2):
        return tuple(t.reshape(N_DEV, FF_BLK, D_MODEL) for t in (d_w1t, d_w3t, d_w2))

    def blocked_cols(full_grad):
        r = full_grad.shape[0]
        return full_grad.reshape(r, N_DEV, full_grad.shape[1] // N_DEV).transpose(1, 0, 2).astype(BF16)

    early = {}

    def start_reduce(names, tag):
        srcs = [dw[n] for n in names]
        send, recv, srcs, lands, token = exchange_start(srcs, srcs[0], False, "rs_" + tag + "_start")
        early[tag] = (names, send, recv, srcs, lands)
        return token

    dw = {}
    da2, db2, dh3_half = ffn_backward_hidden(dh3, a2, b2, ffn2_w[2], g_final, tm, "ffn2_bwd_hid")
    dw['ffn2_w1'], dw['ffn2_w3'], dw['ffn2_w2'] = blocked_ffn(
        *ffn_backward_weights(hn2, dh3_half, a2, b2, da2, db2, n_l, FF_BWD_COLS, "ffn2_bwd_w"))
    token = start_reduce(later_names, "ffn2")
    dh2, g_ffn2_norm = ffn_backward_input(dh3, h2, ffn2_norm, da2, db2, ffn2_w[0], ffn2_w[1], token, tm, "ffn2_bwd_in")
    dattn, dyraw, dproj, *for_weights = merge_backward(dh2, yraw, attn, proj, glu_a, glu_b, w_out_full, token, tm,
                                                       "merge_bwd")
    d_wo, d_ga, d_gb = merge_backward_weights(*for_weights, tm, "merge_bwd_w")
    dw['ssm_glu_a'] = blocked_cols(d_ga)
    dw['ssm_glu_b'] = blocked_cols(d_gb)
    dw['w_out'] = d_wo.reshape(N_DEV, D_MODEL // N_DEV, D_MODEL).astype(BF16)
    token = start_reduce(['ssm_glu_a', 'ssm_glu_b', 'w_out'], "mix")
    dproj3 = dproj.reshape(n_b, n_l, IN_WIDTH)
    dproj3, dsink_p = attention_backward(proj3, dattn.reshape(n_b, n_l, D_MODEL), dproj3, attn_sinks, token, seq,
                                         "attn_bwd")
    dproj3, gs3, dlr_p, dli_p = ssm_backward_scan(
        dyraw.reshape(n_b, n_l, SSM_WIDTH), xs3, dproj3, c_comb_t, tabr, b_comb_t, ssm_d, seq, "ssm_bwd")
    dproj = dproj3.reshape(t_rows, IN_WIDTH)
    d_bd, d_cd, g_d = ssm_param_grads(proj, gs3.reshape(t_rows, 2 * N_STATES), xs3.reshape(t_rows, 2 * N_STATES),
                                      dyraw, n_l, "ssm_bwd_w")
    w_in_full = wing.transpose(1, 0, 2).reshape(D_MODEL, IN_WIDTH)
    dh1, g_mix_norm = mix_backward_act(dh2, h1, mix_norm, dproj, w_in_full, tm, "mix_bwd_act")
    dw['w_in'] = mix_backward_weights(hnm, dproj, n_l, "mix_bwd_w")
    token = start_reduce(['w_in'], "w_in")
    da1, db1, dh1_half = ffn_backward_hidden(dh1, a1, b1, ffn1_w[2], token, tm, "ffn1_bwd_hid")
    dw['ffn1_w1'], dw['ffn1_w3'], dw['ffn1_w2'] = blocked_ffn(
        *ffn_backward_weights(hn1, dh1_half, a1, b1, da1, db1, n_l, FF_BWD_COLS, "ffn1_bwd_w"))
    token = start_reduce(first_names, "ffn1")
    dh0, g_ffn1_norm = ffn_backward_input(dh1, h0, ffn1_norm, da1, db1, ffn1_w[0], ffn1_w[1], token, tm, "ffn1_bwd_in")
    dh0_3 = dh0.reshape(n_b, n_l, D_MODEL)
    grad_x = dh0_3[:, :seq]
    g_meta = sum_leading(dh0_3[:, seq:], "meta_sum")

    groups_per_blk = SCAN_COLS // SSM_STATE
    half = ((jnp.arange(N_SCAN_BLK) % 2)[:, None] == jnp.arange(2)[None, :]).astype(F32)
    eye = jnp.eye(groups_per_blk, dtype=F32)

    def group_blocks(part, channels_first):
        if channels_first:
            t = jnp.sum(part.reshape(N_SCAN_BLK, 2, LANES // 2, SCAN_COLS) * half[:, :, None, None], axis=1)
            t = t.reshape(N_SCAN_BLK, groups_per_blk, SSM_GROUP, groups_per_blk, SSM_STATE)
            t = jnp.sum(t * eye[None, :, None, :, None], axis=3)
            return t.reshape(SSM_GROUPS, SSM_GROUP, SSM_STATE)
        t = jnp.sum(part.reshape(N_SCAN_BLK, SCAN_COLS, 2, LANES // 2) * half[:, None, :, None], axis=2)
        t = t.reshape(N_SCAN_BLK, groups_per_blk, SSM_STATE, groups_per_blk, SSM_GROUP)
        t = jnp.sum(t * eye[None, :, None, :, None], axis=3)
        return t.reshape(SSM_GROUPS, SSM_STATE, SSM_GROUP).transpose(0, 2, 1)

    dbbr = group_blocks(d_bd[:, :, :SCAN_COLS], True).transpose(1, 0, 2).reshape(SSM_GROUP, N_STATES)
    dbbi = group_blocks(d_bd[:, :, SCAN_COLS:], True).transpose(1, 0, 2).reshape(SSM_GROUP, N_STATES)
    g_c_re = group_blocks(d_cd[:, :SCAN_COLS, :], False)[None]
    g_c_im = -group_blocks(d_cd[:, SCAN_COLS:, :], False)[None]
    group_sum = (jnp.arange(N_STATES)[:, None] // SSM_STATE == jnp.arange(LANES)[None, :]).astype(F32)
    g_ar, g_ai, g_ls, g_br, g_bi = ssm_param_backward(
        ar, ai, ls, br_t, bi_t, dlr_p.reshape(n_b * SUBLANES, N_STATES), dli_p.reshape(n_b * SUBLANES, N_STATES),
        dbbr, dbbi, group_sum, "ssm_bwd_params")
    g_sinks = sum_leading(dsink_p, "sink_sum")[0:1, :N_KV_HEADS * Q_PER_KV]

    small = {
        'ffn1_norm': g_ffn1_norm, 'mix_norm': g_mix_norm, 'ffn2_norm': g_ffn2_norm, 'final_norm': g_final.reshape(D_MODEL),
        'attn_sinks': g_sinks, 'ssm_a_re': g_ar.reshape(1, SSM_GROUPS, SSM_STATE), 'ssm_a_im': g_ai.reshape(1, SSM_GROUPS, SSM_STATE),
        'ssm_log_step': g_ls[:, :SSM_GROUPS],
        'ssm_b_re': g_br.reshape(SSM_GROUP, SSM_GROUPS, SSM_STATE).transpose(1, 2, 0)[None],
        'ssm_b_im': g_bi.reshape(SSM_GROUP, SSM_GROUPS, SSM_STATE).transpose(1, 2, 0)[None],
        'ssm_c_re': g_c_re, 'ssm_c_im': g_c_im, 'ssm_d': g_d,
    }

    zeros_meta = jnp.zeros((N_META, D_MODEL), F32)
    packed_g = _pack([small[n] for n in REPLICATED] + [g_meta])
    (parts,) = all_gather_list([packed_g], "ag_small_grads")
    packed_out = adamw_small(parts, _pack([w[n] for n in REPLICATED] + [zeros_meta]),
                             _pack([m[n] for n in REPLICATED] + [zeros_meta]),
                             _pack([v[n] for n in REPLICATED] + [zeros_meta]), "adamw_small")
    shapes = [w[n].shape for n in REPLICATED] + [(N_META, D_MODEL)]
    grads, deltas, new_m, new_v = {}, {}, {}, {}
    unpacked = [_unpack(p, shapes) for p in packed_out]
    for k, n in enumerate(REPLICATED):
        grads[n], deltas[n], new_m[n], new_v[n] = (u[k] for u in unpacked)
    g_meta_full = unpacked[0][-1]
    grads['meta_tokens'] = lax.dynamic_index_in_dim(
        g_meta_full.reshape(N_META, N_DEV, D_MODEL // N_DEV), me, axis=1, keepdims=False)
    deltas['meta_tokens'], new_m['meta_tokens'], new_v['meta_tokens'] = adamw_plain(
        grads['meta_tokens'], w['meta_tokens'], m['meta_tokens'], v['meta_tokens'], "adamw_meta")

    def views(n):
        if n in ffn_names:
            return functools.partial(hidden_on_rows, n), functools.partial(hidden_on_rows_back, n)
        return (lambda t: t[0]), (lambda t: t[None])

    previous = packed_out[0]
    for tag, (names, send, recv, srcs, lands) in early.items():
        lands = exchange_wait(send, recv, srcs, lands, previous, False, "rs_" + tag + "_wait")
        for n, g, land in zip(names, srcs, lands):
            two_d, back = views(n)
            out = adamw_exchanged(me_idx, g, land, two_d(w[n]), two_d(m[n]), two_d(v[n]), "adamw_" + n)
            grads[n], deltas[n], new_m[n], new_v[n] = (back(o) for o in out)
            previous = out[1]

    return (loss, grad_x, *[grads[n] for n in WEIGHTS], *[deltas[n] for n in WEIGHTS],
            *[new_m[n] for n in WEIGHTS], *[new_v[n] for n in WEIGHTS])
```

```python
import functools

import jax
import jax.numpy as jnp
from jax import lax
from jax.experimental import pallas as pl
from jax.experimental.pallas import tpu as pltpu

F32 = jnp.float32
BF16 = jnp.bfloat16
MESH = pl.DeviceIdType.MESH

N_DEV = 8
D_MODEL = 1024
N_META = 16
HEAD_DIM = 64
N_KV_HEADS = 4
Q_PER_KV = 4
BLOCK = 128
KV_WIDTH = N_KV_HEADS * HEAD_DIM
SSM_GROUP = 16
SSM_WIDTH = 512
SSM_GROUPS = 32
SSM_STATE = 64
N_STATES = SSM_GROUPS * SSM_STATE
D_FF = 2816
FF_BLK = D_FF // N_DEV
IN_WIDTH = 4096
IN_BLK = IN_WIDTH // N_DEV
NORM_EPS = 1e-6
NEG_INF = -1e30
SCAN_COLS = 256
N_SCAN_BLK = N_STATES // SCAN_COLS
SUBLANES = 8
LANES = 128
MXU_WIDTH = 256
FF_BWD_COLS = MXU_WIDTH

ADAM_LR = 0.001
ADAM_B1 = 0.9
ADAM_B2 = 0.999
ADAM_EPS = 1e-08
ADAM_WD = 0.01
ADAM_STEP = 10

VMEM_BIG = 56 * 1024 * 1024


def _cp(sem=None, vmem=None):
    kw = {}
    if sem is not None:
        kw["dimension_semantics"] = sem
    if vmem is not None:
        kw["vmem_limit_bytes"] = vmem
    return pltpu.CompilerParams(**kw)


def _pcall(body, **kw):
    return pl.pallas_call(body, **kw)


def _dot(a, b):
    return jnp.dot(a, b, preferred_element_type=F32)


def _dot_nt(a, b):
    return lax.dot_general(a, b, (((1,), (1,)), ((), ())), preferred_element_type=F32)


def _dot_tn(a, b):
    return lax.dot_general(a, b, (((0,), (0,)), ((), ())), preferred_element_type=F32)


def _sigmoid(x):
    return 1.0 / (1.0 + jnp.exp(-x))


def _row_tile(rows, cap):
    best = None
    for t in range(16, min(rows, cap) + 1, 16):
        if rows % t == 0:
            best = t
    assert best is not None, rows
    return best


def _my_place():
    return lax.axis_index("x"), lax.axis_index("y"), lax.axis_index("c")


def all_gather_list(shards, name):
    n = len(shards)

    def body(*refs):
        ins, outs = refs[:n], refs[n:2 * n]
        send_sems, recv_sems, local_sems = refs[2 * n:]
        x, y, c = _my_place()
        me, sibling = (x, y, c), (x, y, 1 - c)
        chips = [(1 - x, y), (x, 1 - y), (1 - x, 1 - y)]

        def blk(a, px, py, pc):
            return outs[a].at[4 * px + 2 * py + pc]

        def copy(a, k, block, to, src=None):
            return pltpu.make_async_remote_copy(
                src_ref=blk(a, *block) if src is None else src, dst_ref=blk(a, *block),
                send_sem=send_sems.at[a * 7 + k], recv_sem=recv_sems.at[a * 7 + k],
                device_id=to, device_id_type=MESH)

        mine = [pltpu.make_async_copy(ins[a], blk(a, *me), local_sems.at[a]) for a in range(n)]
        for cp in mine:
            cp.start()
        first = []
        for a in range(n):
            first.append(copy(a, 0, me, sibling, src=ins[a]))
            first += [copy(a, 1 + j, me, (*chip, c), src=ins[a]) for j, chip in enumerate(chips)]
        for cp in first:
            cp.start()
        passed = []
        for j, chip in enumerate(chips):
            for a in range(n):
                copy(a, 1 + j, (*chip, c), me).wait_recv()
                cp = copy(a, 4 + j, (*chip, c), sibling)
                cp.start()
                passed.append(cp)
        for a in range(n):
            copy(a, 0, sibling, me).wait_recv()
            for j, chip in enumerate(chips):
                copy(a, 4 + j, (*chip, 1 - c), me).wait_recv()
        for cp in first + passed:
            cp.wait_send()
        for cp in mine:
            cp.wait()

    any_spec = pl.BlockSpec(memory_space=pl.ANY)
    return _pcall(
        body, name=name,
        out_shape=[jax.ShapeDtypeStruct((N_DEV,) + s.shape, s.dtype) for s in shards],
        in_specs=[any_spec] * n, out_specs=[any_spec] * n,
        scratch_shapes=[pltpu.SemaphoreType.DMA((7 * n,)), pltpu.SemaphoreType.DMA((7 * n,)),
                        pltpu.SemaphoreType.DMA((n,))],
    )(*shards)


HBM_SPEC = pl.BlockSpec(memory_space=pltpu.HBM)
SEM_SPEC = pl.BlockSpec(memory_space=pltpu.SEMAPHORE)
N_PEERS = N_DEV - 1


def _related(k):
    x, y, c = _my_place()
    px = 1 - x if k & 4 else x
    py = 1 - y if k & 2 else y
    pc = 1 - c if k & 1 else c
    return (px, py, pc), 4 * px + 2 * py + pc


def _exchange_copies(srcs, lands, send_sems, recv_sems, gather):
    x, y, c = _my_place()
    me = 4 * x + 2 * y + c
    copies = []
    for a, (src, land) in enumerate(zip(srcs, lands)):
        for k in range(1, N_DEV):
            peer, d = _related(k)
            copies.append(pltpu.make_async_remote_copy(
                src_ref=src if gather else src.at[d], dst_ref=land.at[me] if gather else land.at[k],
                send_sem=send_sems.at[a * N_PEERS + k - 1], recv_sem=recv_sems.at[a * N_PEERS + k - 1],
                device_id=peer, device_id_type=MESH))
    return copies


def exchange_start(srcs, after, gather, name):
    n = len(srcs)
    land_shapes = [((N_DEV,) + s.shape) if gather else s.shape for s in srcs]

    def body(*refs):
        send_sems, recv_sems = refs[2 * n + 1], refs[2 * n + 2]
        for cp in _exchange_copies(refs[:n], refs[n:2 * n], send_sems, recv_sems, gather):
            cp.start()
        token = refs[-1]
        token[...] = jnp.zeros_like(token)

    sems = pltpu.SemaphoreType.DMA((n * N_PEERS,))
    lands = [pltpu.with_memory_space_constraint(lax.empty(shape, s.dtype), pltpu.HBM) for shape, s in zip(land_shapes, srcs)]
    out = _pcall(
        body, name=name,
        out_shape=(sems, sems, *[pltpu.HBM(s.shape, s.dtype) for s in srcs],
                   *[pltpu.HBM(shape, s.dtype) for shape, s in zip(land_shapes, srcs)],
                   jax.ShapeDtypeStruct((SUBLANES, LANES), F32)),
        in_specs=[HBM_SPEC] * (2 * n) + [pl.BlockSpec(memory_space=pl.ANY)],
        out_specs=(SEM_SPEC, SEM_SPEC, *[HBM_SPEC] * (2 * n), pl.BlockSpec(memory_space=pltpu.VMEM)),
        input_output_aliases={i: 2 + i for i in range(2 * n)},
        compiler_params=pltpu.CompilerParams(has_side_effects=pltpu.SideEffectType.DATAFLOW_SIDE_EFFECTING),
    )(*[pltpu.with_memory_space_constraint(s, pltpu.HBM) for s in srcs], *lands, after)
    return out[0], out[1], list(out[2:2 + n]), list(out[2 + n:2 + 2 * n]), out[-1]


def exchange_wait(send_sems, recv_sems, srcs, lands, after, gather, name):
    n = len(srcs)

    def body(*refs):
        for cp in _exchange_copies(refs[:n], refs[n:2 * n], refs[2 * n], refs[2 * n + 1], gather):
            cp.wait_send()
            cp.wait_recv()

    out = _pcall(
        body, name=name,
        out_shape=(*[pltpu.HBM(s.shape, s.dtype) for s in srcs], *[pltpu.HBM(z.shape, z.dtype) for z in lands]),
        in_specs=[HBM_SPEC] * (2 * n) + [SEM_SPEC, SEM_SPEC, pl.BlockSpec(memory_space=pl.ANY)],
        out_specs=tuple([HBM_SPEC] * (2 * n)),
        input_output_aliases={i: i for i in range(2 * n)},
        compiler_params=pltpu.CompilerParams(has_side_effects=pltpu.SideEffectType.DATAFLOW_SIDE_EFFECTING),
    )(*srcs, *lands, send_sems, recv_sems, after)
    return list(out[n:])


def adamw_exchanged(me, g, land, w, m, v, name):
    rows, cols = w.shape
    tr = _row_tile(rows, 256)

    def body(me_ref, g_ref, land_ref, w_ref, m_ref, v_ref, go_ref, d_ref, mo_ref, vo_ref):
        grad = g_ref[...].astype(F32)
        for k in range(1, N_DEV):
            grad = grad + land_ref[k].astype(F32)
        delta, m_new, v_new = _adam_math(w_ref[...], grad, m_ref[...], v_ref[...])
        go_ref[...] = grad
        d_ref[...] = delta
        mo_ref[...] = m_new
        vo_ref[...] = v_new

    tile = pl.BlockSpec((tr, cols), lambda r, ix: (r, 0))
    out = jax.ShapeDtypeStruct((rows, cols), F32)
    return _pcall(
        body, name=name, out_shape=[out] * 4,
        grid_spec=pltpu.PrefetchScalarGridSpec(
            num_scalar_prefetch=1, grid=(rows // tr,),
            in_specs=[pl.BlockSpec((None, tr, cols), lambda r, ix: (ix[0], r, 0)),
                      pl.BlockSpec((N_DEV, tr, cols), lambda r, ix: (0, r, 0)), tile, tile, tile],
            out_specs=[tile] * 4),
        compiler_params=_cp(("arbitrary",)),
    )(me, g, land, w, m, v)


def _adam_math(w, g, m, v):
    m = ADAM_B1 * m + (1.0 - ADAM_B1) * g
    v = ADAM_B2 * v + (1.0 - ADAM_B2) * (g * g)
    m_hat = m / (1.0 - ADAM_B1 ** ADAM_STEP)
    v_hat = v / (1.0 - ADAM_B2 ** ADAM_STEP)
    delta = -ADAM_LR * (m_hat / (jnp.sqrt(v_hat) + ADAM_EPS) + ADAM_WD * w)
    return delta, m, v


def adamw_small(parts, w, m, v, name):
    _, rows, cols = parts.shape

    def body(p_ref, w_ref, m_ref, v_ref, go_ref, d_ref, mo_ref, vo_ref):
        grad = p_ref[0]
        for k in range(1, N_DEV):
            grad = grad + p_ref[k]
        delta, m_new, v_new = _adam_math(w_ref[...], grad, m_ref[...], v_ref[...])
        go_ref[...] = grad
        d_ref[...] = delta
        mo_ref[...] = m_new
        vo_ref[...] = v_new

    out = jax.ShapeDtypeStruct((rows, cols), F32)
    return _pcall(body, name=name, out_shape=[out] * 4, compiler_params=_cp(vmem=VMEM_BIG))(parts, w, m, v)


def adamw_plain(g, w, m, v, name):
    def body(g_ref, w_ref, m_ref, v_ref, d_ref, mo_ref, vo_ref):
        delta, m_new, v_new = _adam_math(w_ref[...], g_ref[...], m_ref[...], v_ref[...])
        d_ref[...] = delta
        mo_ref[...] = m_new
        vo_ref[...] = v_new

    out = jax.ShapeDtypeStruct(w.shape, F32)
    return _pcall(body, name=name, out_shape=[out] * 3)(g, w, m, v)


def _rms_fwd(x, g):
    r = lax.rsqrt(jnp.mean(x * x, axis=-1, keepdims=True) + NORM_EPS)
    return x * r * g


def _rms_bwd(x, g, dy):
    r = lax.rsqrt(jnp.mean(x * x, axis=-1, keepdims=True) + NORM_EPS)
    xh = x * r
    t = dy * g
    dx = r * (t - xh * jnp.mean(t * xh, axis=-1, keepdims=True))
    return dx, jnp.sum(dy * xh, axis=0, keepdims=True)


def _accumulate(ref, val, first):
    @pl.when(first)
    def _():
        ref[...] = val

    @pl.when(jnp.logical_not(first))
    def _():
        ref[...] += val


def _col_chunks(width):
    return [(c0, min(MXU_WIDTH, width - c0)) for c0 in range(0, width, MXU_WIDTH)]


ANY_SPEC = pl.BlockSpec(memory_space=pl.ANY)


def ffn_forward(h, norm, w1, w3, w2, after, tm, name):
    t_rows = h.shape[0]

    def body(h_ref, g_ref, w1_ref, w3_ref, w2_ref, _, out_ref, hn_ref, a_ref, b_ref, hid_ref):
        hn = _rms_fwd(h_ref[...], g_ref[...]).astype(BF16)
        hn_ref[...] = hn
        for c0, cw in _col_chunks(D_FF):
            a = _dot_nt(hn, w1_ref[c0:c0 + cw, :])
            b = _dot_nt(hn, w3_ref[c0:c0 + cw, :])
            a_ref[:, c0:c0 + cw] = a.astype(BF16)
            b_ref[:, c0:c0 + cw] = b.astype(BF16)
            hid_ref[:, c0:c0 + cw] = (a * _sigmoid(a) * b).astype(BF16)
        out_ref[...] = h_ref[...] + 0.5 * _dot(hid_ref[...], w2_ref[...])

    row = pl.BlockSpec((tm, D_MODEL), lambda i: (i, 0))
    hid_blk = pl.BlockSpec((tm, D_FF), lambda i: (i, 0))
    weight = _resident((D_FF, D_MODEL))
    return _pcall(
        body, name=name, grid=(t_rows // tm,),
        in_specs=[row, pl.BlockSpec((1, D_MODEL), lambda i: (0, 0)), weight, weight, weight, ANY_SPEC],
        out_specs=[row, row, hid_blk, hid_blk],
        out_shape=[jax.ShapeDtypeStruct((t_rows, D_MODEL), F32), jax.ShapeDtypeStruct((t_rows, D_MODEL), BF16),
                   jax.ShapeDtypeStruct((t_rows, D_FF), BF16), jax.ShapeDtypeStruct((t_rows, D_FF), BF16)],
        scratch_shapes=[pltpu.VMEM((tm, D_FF), BF16)],
        compiler_params=_cp(("arbitrary",), VMEM_BIG),
    )(h, norm, w1, w3, w2, after)


def _resident(shape):
    return pl.BlockSpec(shape, lambda *_: (0,) * len(shape), pipeline_mode=pl.Buffered(1))


def ffn_backward_hidden(dh, a, b, w2, after, tm, name):
    t_rows = dh.shape[0]

    def body(dh_ref, a_ref, b_ref, w2_ref, _, da_ref, db_ref, dhb_ref):
        dhb = (0.5 * dh_ref[...]).astype(BF16)
        dhb_ref[...] = dhb
        for c0, cw in _col_chunks(D_FF):
            dhid = _dot_nt(dhb, w2_ref[c0:c0 + cw, :])
            av = a_ref[:, c0:c0 + cw].astype(F32)
            bv = b_ref[:, c0:c0 + cw].astype(F32)
            s = _sigmoid(av)
            da_ref[:, c0:c0 + cw] = (dhid * bv * (s * (1.0 + av * (1.0 - s)))).astype(BF16)
            db_ref[:, c0:c0 + cw] = (dhid * (av * s)).astype(BF16)

    hid = pl.BlockSpec((tm, D_FF), lambda i: (i, 0))
    row = pl.BlockSpec((tm, D_MODEL), lambda i: (i, 0))
    return _pcall(
        body, name=name, grid=(t_rows // tm,),
        in_specs=[row, hid, hid, _resident((D_FF, D_MODEL)), ANY_SPEC],
        out_specs=[hid, hid, row],
        out_shape=[jax.ShapeDtypeStruct((t_rows, D_FF), BF16), jax.ShapeDtypeStruct((t_rows, D_FF), BF16),
                   jax.ShapeDtypeStruct((t_rows, D_MODEL), BF16)],
        compiler_params=_cp(("arbitrary",), VMEM_BIG),
    )(dh, a, b, w2, after)


def ffn_backward_input(dh, h, norm, da, db, w1, w3, after, tm, name):
    t_rows = h.shape[0]

    def body(dh_ref, h_ref, g_ref, da_ref, db_ref, w1_ref, w3_ref, _, dhin_ref, dg_ref):
        dhn = _dot(da_ref[...], w1_ref[...]) + _dot(db_ref[...], w3_ref[...])
        dx, dg = _rms_bwd(h_ref[...], g_ref[...], dhn)
        dhin_ref[...] = dh_ref[...] + dx
        _accumulate(dg_ref, dg, pl.program_id(0) == 0)

    row = pl.BlockSpec((tm, D_MODEL), lambda i: (i, 0))
    vec = pl.BlockSpec((1, D_MODEL), lambda i: (0, 0))
    hid = pl.BlockSpec((tm, D_FF), lambda i: (i, 0))
    return _pcall(
        body, name=name, grid=(t_rows // tm,),
        in_specs=[row, row, vec, hid, hid, _resident((D_FF, D_MODEL)), _resident((D_FF, D_MODEL)), ANY_SPEC],
        out_specs=[row, vec],
        out_shape=[jax.ShapeDtypeStruct((t_rows, D_MODEL), F32), jax.ShapeDtypeStruct((1, D_MODEL), F32)],
        compiler_params=_cp(("arbitrary",), VMEM_BIG),
    )(dh, h, norm, da, db, w1, w3, after)


def ffn_backward_weights(hn, dh, a, b, da, db, tm, tn, name):
    t_rows = hn.shape[0]
    ni = t_rows // tm
    kc = _row_tile(tm, 688)

    def body(hn_ref, dh_ref, a_ref, b_ref, da_ref, db_ref, dw1_ref, dw3_ref, dw2_ref, acc1, acc3, acc2):
        i = pl.program_id(1)
        parts = None
        for r0 in range(0, tm, kc):
            rows = slice(r0, r0 + kc)
            hn_v = hn_ref[rows, :]
            av = a_ref[rows, :].astype(F32)
            hid = (av * _sigmoid(av) * b_ref[rows, :].astype(F32)).astype(BF16)
            new = (_dot_tn(hn_v, da_ref[rows, :]), _dot_tn(hn_v, db_ref[rows, :]), _dot_tn(hid, dh_ref[rows, :]))
            parts = new if parts is None else tuple(p + q for p, q in zip(parts, new))
        _accumulate(acc1, parts[0], i == 0)
        _accumulate(acc3, parts[1], i == 0)
        _accumulate(acc2, parts[2], i == 0)

        @pl.when(i == ni - 1)
        def _():
            dw1_ref[...] = acc1[...].T.astype(BF16)
            dw3_ref[...] = acc3[...].T.astype(BF16)
            dw2_ref[...] = acc2[...].astype(BF16)

    row = pl.BlockSpec((tm, D_MODEL), lambda j, i: (i, 0))
    hid_blk = pl.BlockSpec((tm, tn), lambda j, i: (i, j))
    w_row = pl.BlockSpec((tn, D_MODEL), lambda j, i: (j, 0))
    out = jax.ShapeDtypeStruct((D_FF, D_MODEL), BF16)
    return _pcall(
        body, name=name, grid=(D_FF // tn, ni),
        in_specs=[row, row, hid_blk, hid_blk, hid_blk, hid_blk],
        out_specs=[w_row, w_row, w_row], out_shape=[out, out, out],
        scratch_shapes=[pltpu.VMEM((D_MODEL, tn), F32), pltpu.VMEM((D_MODEL, tn), F32), pltpu.VMEM((tn, D_MODEL), F32)],
        compiler_params=_cp(("arbitrary", "arbitrary"), VMEM_BIG),
    )(hn, dh, a, b, da, db)


def mix_forward(h, norm, wing, tm, name):
    t_rows = h.shape[0]

    def body(h_ref, g_ref, w_ref, hn_ref, p_ref):
        hn = _rms_fwd(h_ref[...], g_ref[...]).astype(BF16)
        hn_ref[...] = hn
        for j in range(N_DEV):
            p_ref[:, j * IN_BLK:(j + 1) * IN_BLK] = _dot(hn, w_ref[j]).astype(BF16)

    row = pl.BlockSpec((tm, D_MODEL), lambda i: (i, 0))
    return _pcall(
        body, name=name, grid=(t_rows // tm,),
        in_specs=[row, pl.BlockSpec((1, D_MODEL), lambda i: (0, 0)),
                  pl.BlockSpec((N_DEV, D_MODEL, IN_BLK), lambda i: (0, 0, 0))],
        out_specs=[row, pl.BlockSpec((tm, IN_WIDTH), lambda i: (i, 0))],
        out_shape=[jax.ShapeDtypeStruct((t_rows, D_MODEL), BF16), jax.ShapeDtypeStruct((t_rows, IN_WIDTH), BF16)],
        compiler_params=_cp(("arbitrary",), VMEM_BIG),
    )(h, norm, wing)


def mix_backward_act(dh, h, norm, dproj, w_in_full, tm, name):
    t_rows = h.shape[0]

    def body(dh_ref, h_ref, g_ref, dp_ref, w_ref, dhin_ref, dg_ref):
        dx, dg = _rms_bwd(h_ref[...], g_ref[...], _dot_nt(dp_ref[...], w_ref[...]))
        dhin_ref[...] = dh_ref[...] + dx
        _accumulate(dg_ref, dg, pl.program_id(0) == 0)

    row = pl.BlockSpec((tm, D_MODEL), lambda i: (i, 0))
    vec = pl.BlockSpec((1, D_MODEL), lambda i: (0, 0))
    return _pcall(
        body, name=name, grid=(t_rows // tm,),
        in_specs=[row, row, vec, pl.BlockSpec((tm, IN_WIDTH), lambda i: (i, 0)), _resident((D_MODEL, IN_WIDTH))],
        out_specs=[row, vec],
        out_shape=[jax.ShapeDtypeStruct((t_rows, D_MODEL), F32), jax.ShapeDtypeStruct((1, D_MODEL), F32)],
        compiler_params=_cp(("arbitrary",), VMEM_BIG),
    )(dh, h, norm, dproj, w_in_full)


def mix_backward_weights(hn, dproj, tm, name):
    t_rows = hn.shape[0]
    ni = t_rows // tm
    per_step = 2

    kc = _row_tile(tm, 688)

    def body(hn_ref, dp_ref, dw_ref, acc):
        i = pl.program_id(1)
        part = functools.reduce(lambda u, w: u + w, [_dot_tn(hn_ref[r0:r0 + kc, :], dp_ref[r0:r0 + kc, :])
                                                    for r0 in range(0, tm, kc)])
        _accumulate(acc, part, i == 0)

        @pl.when(i == ni - 1)
        def _():
            for k in range(per_step):
                dw_ref[k] = acc[:, k * IN_BLK:(k + 1) * IN_BLK].astype(BF16)

    return _pcall(
        body, name=name, grid=(N_DEV // per_step, ni),
        in_specs=[pl.BlockSpec((tm, D_MODEL), lambda j, i: (i, 0)),
                  pl.BlockSpec((tm, per_step * IN_BLK), lambda j, i: (i, j))],
        out_specs=pl.BlockSpec((per_step, D_MODEL, IN_BLK), lambda j, i: (j, 0, 0)),
        out_shape=jax.ShapeDtypeStruct((N_DEV, D_MODEL, IN_BLK), BF16),
        scratch_shapes=[pltpu.VMEM((D_MODEL, per_step * IN_BLK), F32)],
        compiler_params=_cp(("arbitrary", "arbitrary"), VMEM_BIG),
    )(hn, dproj)


GELU_C = 0.7978845608028654
GELU_K = 0.044715


def _gelu(x):
    return 0.5 * x * (1.0 + jnp.tanh(GELU_C * (x + GELU_K * (x * x * x))))


def _gelu_and_grad(x):
    th = jnp.tanh(GELU_C * (x + GELU_K * (x * x * x)))
    val = 0.5 * x * (1.0 + th)
    grad = 0.5 * (1.0 + th) + 0.5 * x * (1.0 - th * th) * (GELU_C * (1.0 + 3.0 * GELU_K * (x * x)))
    return val, grad


def merge_forward(h, yraw, attn, proj, glu_a, glu_b, w_out, tm, name):
    t_rows = h.shape[0]

    def body(h_ref, y_ref, at_ref, gate_ref, a_ref, b_ref, wo_ref, out_ref):
        y = _gelu(y_ref[...]).astype(BF16)
        ssm = _dot(y, a_ref[...]) * _sigmoid(_dot(y, b_ref[...]))
        ga = gate_ref[:, :D_MODEL].astype(F32)
        gs = gate_ref[:, D_MODEL:].astype(F32)
        merged = _sigmoid(ga) * at_ref[...].astype(F32) + _sigmoid(gs) * ssm
        out_ref[...] = h_ref[...] + _dot(merged.astype(BF16), wo_ref[...])

    row = pl.BlockSpec((tm, D_MODEL), lambda i: (i, 0))
    glu = pl.BlockSpec((SSM_WIDTH, D_MODEL), lambda i: (0, 0))
    return _pcall(
        body, name=name, grid=(t_rows // tm,),
        in_specs=[row, pl.BlockSpec((tm, SSM_WIDTH), lambda i: (i, 0)), row,
                  pl.BlockSpec((tm, 2 * D_MODEL), lambda i: (i, 1)), glu, glu,
                  pl.BlockSpec((D_MODEL, D_MODEL), lambda i: (0, 0))],
        out_specs=row, out_shape=jax.ShapeDtypeStruct((t_rows, D_MODEL), F32),
        compiler_params=_cp(("arbitrary",), VMEM_BIG),
    )(h, yraw, attn, proj, glu_a, glu_b, w_out)


def merge_backward(dh, yraw, attn, proj, glu_a, glu_b, w_out, after, tm, name):
    t_rows = dh.shape[0]

    def body(dh_ref, y_ref, at_ref, gate_ref, a_ref, b_ref, wo_ref, _,
             dat_ref, dy_ref, dgate_ref, d16_ref, mg_ref, y16_ref, dya_ref, dyb_ref):
        d16 = dh_ref[...].astype(BF16)
        d16_ref[...] = d16
        gel, dgel = _gelu_and_grad(y_ref[...].astype(F32))
        y16 = gel.astype(BF16)
        y16_ref[...] = y16
        dy = None
        for c0, cw in _col_chunks(D_MODEL):
            cols = slice(c0, c0 + cw)
            gcols = slice(D_MODEL + c0, D_MODEL + c0 + cw)
            dmerged = _dot_nt(d16, wo_ref[cols, :])
            ya = _dot(y16, a_ref[:, cols])
            sb = _sigmoid(_dot(y16, b_ref[:, cols]))
            ssm = ya * sb
            sa = _sigmoid(gate_ref[:, cols].astype(F32))
            ss = _sigmoid(gate_ref[:, gcols].astype(F32))
            attn_v = at_ref[:, cols].astype(F32)
            mg_ref[:, cols] = (sa * attn_v + ss * ssm).astype(BF16)
            dat_ref[:, cols] = (dmerged * sa).astype(BF16)
            dgate_ref[:, cols] = (dmerged * attn_v * sa * (1.0 - sa)).astype(BF16)
            dgate_ref[:, gcols] = (dmerged * ssm * ss * (1.0 - ss)).astype(BF16)
            dssm = dmerged * ss
            dya = (dssm * sb).astype(BF16)
            dyb = (dssm * ya * sb * (1.0 - sb)).astype(BF16)
            dya_ref[:, cols] = dya
            dyb_ref[:, cols] = dyb
            part = _dot_nt(dya, a_ref[:, cols]) + _dot_nt(dyb, b_ref[:, cols])
            dy = part if dy is None else dy + part
        dy_ref[...] = (dy * dgel).astype(BF16)

    row = pl.BlockSpec((tm, D_MODEL), lambda i: (i, 0))
    ssm_row = pl.BlockSpec((tm, SSM_WIDTH), lambda i: (i, 0))
    gates = pl.BlockSpec((tm, 2 * D_MODEL), lambda i: (i, 1))
    wide = jax.ShapeDtypeStruct((t_rows, D_MODEL), BF16)
    narrow = jax.ShapeDtypeStruct((t_rows, SSM_WIDTH), BF16)
    return _pcall(
        body, name=name, grid=(t_rows // tm,),
        in_specs=[row, ssm_row, row, gates, _resident((SSM_WIDTH, D_MODEL)), _resident((SSM_WIDTH, D_MODEL)),
                  _resident((D_MODEL, D_MODEL)), ANY_SPEC],
        out_specs=[row, ssm_row, gates, row, row, ssm_row, row, row],
        out_shape=[wide, narrow, jax.ShapeDtypeStruct((t_rows, IN_WIDTH), BF16), wide, wide, narrow, wide, wide],
        compiler_params=_cp(("arbitrary",), VMEM_BIG),
    )(dh, yraw, attn, proj, glu_a, glu_b, w_out, after)


def merge_backward_weights(d16, merged, y16, dya, dyb, tm, name):
    t_rows = d16.shape[0]

    def body(d_ref, mg_ref, y_ref, dya_ref, dyb_ref, dwo_ref, da_ref, db_ref):
        first = pl.program_id(0) == 0
        y16 = y_ref[...]
        _accumulate(dwo_ref, _dot_tn(mg_ref[...], d_ref[...]), first)
        _accumulate(da_ref, _dot_tn(y16, dya_ref[...]), first)
        _accumulate(db_ref, _dot_tn(y16, dyb_ref[...]), first)

    row = pl.BlockSpec((tm, D_MODEL), lambda i: (i, 0))
    ssm_row = pl.BlockSpec((tm, SSM_WIDTH), lambda i: (i, 0))
    glu = pl.BlockSpec((SSM_WIDTH, D_MODEL), lambda i: (0, 0))
    wo = pl.BlockSpec((D_MODEL, D_MODEL), lambda i: (0, 0))
    return _pcall(
        body, name=name, grid=(t_rows // tm,),
        in_specs=[row, row, ssm_row, row, row], out_specs=[wo, glu, glu],
        out_shape=[jax.ShapeDtypeStruct((D_MODEL, D_MODEL), F32), jax.ShapeDtypeStruct((SSM_WIDTH, D_MODEL), F32),
                   jax.ShapeDtypeStruct((SSM_WIDTH, D_MODEL), F32)],
        compiler_params=_cp(("arbitrary",), VMEM_BIG),
    )(d16, merged, y16, dya, dyb)


def final_loss_backward(h, target, norm, seq, tm, name):
    t_rows = h.shape[0]
    tiles_per_example = (seq + N_META) // tm

    def body(h_ref, t_ref, g_ref, dh_ref, loss_ref, dg_ref):
        i = pl.program_id(0)
        x = h_ref[...]
        g = g_ref[...]
        r = lax.rsqrt(jnp.mean(x * x, axis=-1, keepdims=True) + NORM_EPS)
        xh = x * r
        pos = lax.broadcasted_iota(jnp.int32, (tm, 1), 0) + (i % tiles_per_example) * tm
        diff = jnp.where(pos < seq, xh * g - t_ref[...], 0.0)
        part = 0.5 * jnp.sum(jnp.sum(diff * diff, axis=-1, keepdims=True), axis=0, keepdims=True) / D_MODEL
        dy = diff / D_MODEL
        t = dy * g
        dh_ref[...] = r * (t - xh * jnp.mean(t * xh, axis=-1, keepdims=True))
        _accumulate(loss_ref, jnp.broadcast_to(part, (1, LANES)), i == 0)
        _accumulate(dg_ref, jnp.sum(dy * xh, axis=0, keepdims=True), i == 0)

    row = pl.BlockSpec((tm, D_MODEL), lambda i: (i, 0))
    vec = pl.BlockSpec((1, D_MODEL), lambda i: (0, 0))
    return _pcall(
        body, name=name, grid=(t_rows // tm,),
        in_specs=[row, row, vec],
        out_specs=[row, pl.BlockSpec((1, LANES), lambda i: (0, 0)), vec],
        out_shape=[jax.ShapeDtypeStruct((t_rows, D_MODEL), F32), jax.ShapeDtypeStruct((1, LANES), F32),
                   jax.ShapeDtypeStruct((1, D_MODEL), F32)],
        compiler_params=_cp(("arbitrary",), VMEM_BIG),
    )(h, target, norm)


ATTN_SCALE = HEAD_DIM ** -0.5
STACK_HEADS = (0, 2, 1, 3)


def _lane_half(shape, hf):
    lane = lax.broadcasted_iota(jnp.int32, shape, 1)
    return (lane < HEAD_DIM) if hf == 0 else (lane >= HEAD_DIM)


def _kv_variants(ref, rows, kh):
    tile = kh // 2
    t = ref[rows, tile * LANES:(tile + 1) * LANES].astype(F32)
    swapped = pltpu.roll(t, HEAD_DIM, axis=1)
    at_low, at_high = (t, swapped) if kh % 2 == 0 else (swapped, t)
    lo = jnp.where(_lane_half(t.shape, 0), at_low, 0.0).astype(BF16)
    hi = jnp.where(_lane_half(t.shape, 1), at_high, 0.0).astype(BF16)
    return lo, hi


def _to_kv_lanes(lo, hi, kh):
    lo = jnp.where(_lane_half(lo.shape, 0), lo, 0.0)
    hi = jnp.where(_lane_half(hi.shape, 1), hi, 0.0)
    if kh % 2 == 0:
        return lo + pltpu.roll(hi, HEAD_DIM, axis=1)
    return pltpu.roll(lo, HEAD_DIM, axis=1) + hi


def _stacked(ref, rows, kh):
    col = kh * 2 * LANES
    return jnp.concatenate([ref[rows, col:col + LANES], ref[rows, col + LANES:col + 2 * LANES]], axis=0)


def _sink_column(sink_ref, kh, nq):
    row = lax.broadcasted_iota(jnp.int32, (4 * nq, 1), 0)
    col = jnp.zeros((4 * nq, 1), F32)
    for quarter, g in enumerate(STACK_HEADS):
        col = jnp.where(row // nq == quarter, sink_ref[0, kh * Q_PER_KV + g], col)
    return col


def _softmax_parts(qs, key_tiles, masks, sink):
    scores = []
    for (k_lo, k_hi), mask in zip(key_tiles, masks):
        s = jnp.concatenate([_dot_nt(qs, k_lo), _dot_nt(qs, k_hi)], axis=0) * ATTN_SCALE
        scores.append(s if mask is None else jnp.where(mask, s, NEG_INF))
    m = functools.reduce(jnp.maximum, [jnp.max(s, axis=-1, keepdims=True) for s in scores])
    m = jnp.maximum(m, sink)
    probs = [jnp.exp(s - m) for s in scores]
    e_sink = jnp.exp(sink - m)
    den = _row_sums(probs) + e_sink
    return probs, 1.0 / den, e_sink


def _row_sums(tiles):
    return functools.reduce(lambda u, w: u + w, [jnp.sum(t, axis=-1, keepdims=True) for t in tiles])


def _band_mask(nq, first):
    keys = BLOCK if first else 2 * BLOCK
    qi = lax.broadcasted_iota(jnp.int32, (4 * nq, keys), 0) % nq
    kj = lax.broadcasted_iota(jnp.int32, (4 * nq, keys), 1)
    if first:
        return kj <= qi
    return jnp.logical_and(kj > qi, kj <= qi + BLOCK)


def _meta_mask():
    qi = lax.broadcasted_iota(jnp.int32, (4 * N_META, N_META), 0) % N_META
    kj = lax.broadcasted_iota(jnp.int32, (4 * N_META, N_META), 1)
    return kj <= qi


def _attention_schedule(seq, queries, carry):
    meta_rows = pl.ds(seq, N_META)
    carry = queries(pl.ds(0, BLOCK), BLOCK, [pl.ds(0, BLOCK), meta_rows], [_band_mask(BLOCK, True), None], carry)

    def block(n, c):
        r0 = pl.multiple_of(n * BLOCK, BLOCK)
        p0 = pl.multiple_of((n - 1) * BLOCK, BLOCK)
        return queries(pl.ds(r0, BLOCK), BLOCK, [pl.ds(p0, 2 * BLOCK), meta_rows], [_band_mask(BLOCK, False), None], c)

    carry = lax.fori_loop(1, seq // BLOCK, block, carry)
    return queries(meta_rows, N_META, [meta_rows], [_meta_mask()], carry)


def attention_forward(proj3, sinks, seq, name):
    n_b, n_l, _ = proj3.shape

    def body(sink_ref, q_ref, k_ref, v_ref, o_ref):
        def queries(q_rows, nq, key_rows, masks, carry):
            for kh in range(N_KV_HEADS):
                ks = [_kv_variants(k_ref, r, kh) for r in key_rows]
                vs = [_kv_variants(v_ref, r, kh) for r in key_rows]
                qs = _stacked(q_ref, q_rows, kh)
                probs, inv, _ = _softmax_parts(qs, ks, masks, _sink_column(sink_ref, kh, nq))
                probs = [p.astype(BF16) for p in probs]
                o_lo = functools.reduce(lambda u, w: u + w, [_dot(p[:2 * nq], v_lo) for p, (v_lo, _) in zip(probs, vs)])
                o_hi = functools.reduce(lambda u, w: u + w, [_dot(p[2 * nq:], v_hi) for p, (_, v_hi) in zip(probs, vs)])
                out = (o_lo * inv[:2 * nq] + o_hi * inv[2 * nq:]).astype(BF16)
                col = kh * 2 * LANES
                o_ref[q_rows, col:col + LANES] = out[:nq]
                o_ref[q_rows, col + LANES:col + 2 * LANES] = out[nq:]
            return carry

        _attention_schedule(seq, queries, 0)

    return _pcall(
        body, name=name, grid=(n_b,),
        in_specs=[pl.BlockSpec(memory_space=pltpu.SMEM),
                  pl.BlockSpec((None, n_l, D_MODEL), lambda b: (b, 0, 0)),
                  pl.BlockSpec((None, n_l, KV_WIDTH), lambda b: (b, 0, D_MODEL // KV_WIDTH)),
                  pl.BlockSpec((None, n_l, KV_WIDTH), lambda b: (b, 0, D_MODEL // KV_WIDTH + 1))],
        out_specs=pl.BlockSpec((None, n_l, D_MODEL), lambda b: (b, 0, 0)),
        out_shape=jax.ShapeDtypeStruct((n_b, n_l, D_MODEL), BF16),
        compiler_params=_cp(("arbitrary",), VMEM_BIG),
    )(sinks, proj3, proj3, proj3)


def attention_backward(proj3, dattn3, dproj3, sinks, after, seq, name):
    n_b, n_l, _ = proj3.shape
    qkv_width = D_MODEL + 2 * KV_WIDTH

    def body(sink_ref, q_ref, k_ref, v_ref, do_ref, _, __, dqkv_ref, dsink_ref, dk_ref, dv_ref):
        dk_ref[...] = jnp.zeros_like(dk_ref)
        dv_ref[...] = jnp.zeros_like(dv_ref)
        sub = lax.broadcasted_iota(jnp.int32, (SUBLANES, LANES), 0)
        lane = lax.broadcasted_iota(jnp.int32, (SUBLANES, LANES), 1)

        def queries(q_rows, nq, key_rows, masks, dsink):
            for kh in range(N_KV_HEADS):
                ks = [_kv_variants(k_ref, r, kh) for r in key_rows]
                vs = [_kv_variants(v_ref, r, kh) for r in key_rows]
                qs = _stacked(q_ref, q_rows, kh)
                dos = _stacked(do_ref, q_rows, kh)
                probs, inv, e_sink = _softmax_parts(qs, ks, masks, _sink_column(sink_ref, kh, nq))
                probs = [p * inv for p in probs]
                dps = [jnp.concatenate([_dot_nt(dos, v_lo), _dot_nt(dos, v_hi)], axis=0) for v_lo, v_hi in vs]
                delta = _row_sums([p * dp for p, dp in zip(probs, dps)])
                d_sink = -(e_sink * inv) * delta
                for quarter, g in enumerate(STACK_HEADS):
                    d_here = jnp.sum(d_sink[quarter * nq:(quarter + 1) * nq], axis=0, keepdims=True)
                    dsink = dsink + jnp.where(jnp.logical_and(sub == 0, lane == kh * Q_PER_KV + g), d_here, 0.0)
                dq = None
                tile = slice((kh // 2) * LANES, (kh // 2 + 1) * LANES)
                for r, p, dp, (k_lo, k_hi) in zip(key_rows, probs, dps, ks):
                    ds = (p * (dp - delta)).astype(BF16)
                    p16 = p.astype(BF16)
                    dq_x = _dot(ds[:2 * nq], k_lo) + _dot(ds[2 * nq:], k_hi)
                    dq = dq_x if dq is None else dq + dq_x
                    dk_ref[r, tile] += _to_kv_lanes(_dot_tn(ds[:2 * nq], qs), _dot_tn(ds[2 * nq:], qs), kh) * ATTN_SCALE
                    dv_ref[r, tile] += _to_kv_lanes(_dot_tn(p16[:2 * nq], dos), _dot_tn(p16[2 * nq:], dos), kh)
                dq = (dq * ATTN_SCALE).astype(BF16)
                col = kh * 2 * LANES
                dqkv_ref[q_rows, col:col + LANES] = dq[:nq]
                dqkv_ref[q_rows, col + LANES:col + 2 * LANES] = dq[nq:]
            return dsink

        dsink_ref[...] = _attention_schedule(seq, queries, jnp.zeros((SUBLANES, LANES), F32))
        dqkv_ref[:, D_MODEL:D_MODEL + KV_WIDTH] = dk_ref[...].astype(BF16)
        dqkv_ref[:, D_MODEL + KV_WIDTH:] = dv_ref[...].astype(BF16)

    return _pcall(
        body, name=name, grid=(n_b,),
        in_specs=[pl.BlockSpec(memory_space=pltpu.SMEM),
                  pl.BlockSpec((None, n_l, D_MODEL), lambda b: (b, 0, 0)),
                  pl.BlockSpec((None, n_l, KV_WIDTH), lambda b: (b, 0, D_MODEL // KV_WIDTH)),
                  pl.BlockSpec((None, n_l, KV_WIDTH), lambda b: (b, 0, D_MODEL // KV_WIDTH + 1)),
                  pl.BlockSpec((None, n_l, D_MODEL), lambda b: (b, 0, 0)),
                  ANY_SPEC, ANY_SPEC],
        out_specs=[pl.BlockSpec((None, n_l, qkv_width), lambda b: (b, 0, 0)),
                   pl.BlockSpec((None, SUBLANES, LANES), lambda b: (b, 0, 0))],
        out_shape=[jax.ShapeDtypeStruct(dproj3.shape, BF16), jax.ShapeDtypeStruct((n_b, SUBLANES, LANES), F32)],
        scratch_shapes=[pltpu.VMEM((n_l, KV_WIDTH), F32), pltpu.VMEM((n_l, KV_WIDTH), F32)],
        input_output_aliases={5: 0},
        compiler_params=_cp(("arbitrary",), VMEM_BIG),
    )(sinks, proj3, proj3, proj3, dattn3, dproj3, after)


TAB_ROWS = 8
SCAN_UNROLL = 4


def _cmul(ar, ai, br, bi):
    return ar * br - ai * bi, ar * bi + ai * br


def _discretise(ar, ai, ls):
    step = jnp.exp(ls)
    mag = jnp.exp(ar * step)
    ang = ai * step
    cos, sin = jnp.cos(ang), jnp.sin(ang)
    lr, li = mag * cos, mag * sin
    den = ar * ar + ai * ai
    nr, ni = lr - 1.0, li
    cr = (nr * ar + ni * ai) / den
    ci = (ni * ar - nr * ai) / den
    return step, mag, lr, li, den, nr, ni, cr, ci


def _scan_tables(lr, li, reverse):
    n = lr.shape[-1]
    pw = [(lr, li)]
    for _ in range(SUBLANES - 1):
        pw.append(_cmul(pw[-1][0], pw[-1][1], lr, li))
    row = lax.broadcasted_iota(jnp.int32, (SUBLANES, n), 0)
    out = []
    for d in (1, 2, 4):
        ok = (row + d <= SUBLANES - 1) if reverse else (row >= d)
        out += [jnp.where(ok, pw[d - 1][0], 0.0), jnp.where(ok, pw[d - 1][1], 0.0)]
    cr = jnp.zeros((SUBLANES, n), F32)
    ci = jnp.zeros((SUBLANES, n), F32)
    for r in range(SUBLANES):
        e = (SUBLANES - r) if reverse else (r + 1)
        cr = jnp.where(row == r, pw[e - 1][0], cr)
        ci = jnp.where(row == r, pw[e - 1][1], ci)
    return out + [cr, ci]


def ssm_prepare(ar, ai, ls, br_t, bi_t, name):
    def body(ar_ref, ai_ref, ls_ref, br_ref, bi_ref, bbr_ref, bbi_ref, tf_ref, tr_ref):
        _, _, lr, li, _, _, _, cr, ci = _discretise(ar_ref[...], ai_ref[...], ls_ref[...])
        br, bi = br_ref[...], bi_ref[...]
        bbr_ref[...] = cr * br - ci * bi
        bbi_ref[...] = cr * bi + ci * br
        for k, t in enumerate(_scan_tables(lr, li, False)):
            tf_ref[k] = t
        for k, t in enumerate(_scan_tables(lr, -li, True)):
            tr_ref[k] = t

    return _pcall(
        body, name=name,
        out_shape=[jax.ShapeDtypeStruct((SSM_GROUP, N_STATES), F32), jax.ShapeDtypeStruct((SSM_GROUP, N_STATES), F32),
                   jax.ShapeDtypeStruct((TAB_ROWS, SUBLANES, N_STATES), F32),
                   jax.ShapeDtypeStruct((TAB_ROWS, SUBLANES, N_STATES), F32)],
    )(ar, ai, ls, br_t, bi_t)


def ssm_param_backward(ar, ai, ls, br_t, bi_t, dlr_p, dli_p, dbbr, dbbi, group_sum, name):
    def body(ar_ref, ai_ref, ls_ref, br_ref, bi_ref, dlr_ref, dli_ref, dbbr_ref, dbbi_ref, gs_ref,
             dar_ref, dai_ref, dls_ref, dbr_ref, dbi_ref):
        ar, ai = ar_ref[...], ai_ref[...]
        step, mag, lr, li, den, nr, ni, cr, ci = _discretise(ar, ai, ls_ref[...])
        br, bi, dbbr_v, dbbi_v = br_ref[...], bi_ref[...], dbbr_ref[...], dbbi_ref[...]
        dbr_ref[...] = cr * dbbr_v + ci * dbbi_v
        dbi_ref[...] = cr * dbbi_v - ci * dbbr_v
        dcr = jnp.sum(dbbr_v * br + dbbi_v * bi, axis=0, keepdims=True)
        dci = jnp.sum(dbbi_v * br - dbbr_v * bi, axis=0, keepdims=True)
        dnr = (dcr * ar - dci * ai) / den
        dni = (dcr * ai + dci * ar) / den
        dden = -(cr * dcr + ci * dci) / den
        dar = (dcr * nr + dci * ni) / den + dden * 2.0 * ar
        dai = (dcr * ni - dci * nr) / den + dden * 2.0 * ai
        dlr = jnp.sum(dlr_ref[...], axis=0, keepdims=True) + dnr
        dli = jnp.sum(dli_ref[...], axis=0, keepdims=True) + dni
        dmag = (dlr * lr + dli * li) / mag
        dang = dli * lr - dlr * li
        dar_ref[...] = dar + dmag * mag * step
        dai_ref[...] = dai + dang * step
        dstep = dmag * mag * ar + dang * ai
        dls_ref[...] = jnp.dot(dstep * step, gs_ref[...], preferred_element_type=F32, precision=lax.Precision.HIGHEST)

    vec = jax.ShapeDtypeStruct((1, N_STATES), F32)
    mat = jax.ShapeDtypeStruct((SSM_GROUP, N_STATES), F32)
    return _pcall(body, name=name, out_shape=[vec, vec, jax.ShapeDtypeStruct((1, LANES), F32), mat, mat])(
        ar, ai, ls, br_t, bi_t, dlr_p, dli_p, dbbr, dbbi, group_sum)


def _scan_rows(a, b, tabs, carry, reverse):
    for k, d in enumerate((1, 2, 4)):
        shift = SUBLANES - d if reverse else d
        sr, si = pltpu.roll(a, shift, axis=0), pltpu.roll(b, shift, axis=0)
        pr, pi = _cmul(tabs[2 * k], tabs[2 * k + 1], sr, si)
        a, b = a + pr, b + pi
    pr, pi = _cmul(tabs[6], tabs[7], carry[0], carry[1])
    return a + pr, b + pi


def _time_groups(seq, reverse):
    meta = [seq + SUBLANES * g for g in range(N_META // SUBLANES)]
    return meta[::-1] if reverse else meta


def ssm_forward_scan(proj3, b_comb, tabf, c_comb, dvec, seq, name):
    n_b, n_l, _ = proj3.shape
    u_blk = (D_MODEL + 2 * KV_WIDTH) // LANES

    def body(u_ref, b_ref, tab_ref, c_ref, d_ref, x_ref, y_ref, bu, xs):
        j = pl.program_id(1)
        u = u_ref[...]
        bu[...] = _dot(u, b_ref[...])
        tabs = [tab_ref[k] for k in range(TAB_ROWS)]

        def group(r0, carry):
            rows = pl.ds(r0, SUBLANES)
            a, b = _scan_rows(bu[rows, :SCAN_COLS], bu[rows, SCAN_COLS:], tabs, carry, False)
            xs[rows, :SCAN_COLS] = a
            xs[rows, SCAN_COLS:] = b
            return (jnp.broadcast_to(a[SUBLANES - 1:, :], a.shape), jnp.broadcast_to(b[SUBLANES - 1:, :], b.shape))

        zero = jnp.zeros((SUBLANES, SCAN_COLS), F32)
        carry = (zero, zero)
        for r0 in _time_groups(seq, False):
            carry = group(r0, carry)
        span = SCAN_UNROLL * SUBLANES

        def groups(t, c):
            for k in range(SCAN_UNROLL):
                c = group(pl.multiple_of(t * span, span) + k * SUBLANES, c)
            return c

        lax.fori_loop(0, seq // span, groups, carry)
        x16 = xs[...].astype(BF16)
        x_ref[...] = x16
        contrib = _dot(x16, c_ref[...])

        @pl.when(j % 2 == 0)
        def _():
            y_ref[...] = contrib + d_ref[...] * u.astype(F32)

        @pl.when(j % 2 == 1)
        def _():
            y_ref[...] += contrib

    return _pcall(
        body, name=name, grid=(n_b, N_SCAN_BLK),
        in_specs=[pl.BlockSpec((None, n_l, LANES), lambda b, j: (b, 0, u_blk + j // 2)),
                  pl.BlockSpec((None, LANES, 2 * SCAN_COLS), lambda b, j: (j, 0, 0)),
                  pl.BlockSpec((TAB_ROWS, SUBLANES, SCAN_COLS), lambda b, j: (0, 0, j)),
                  pl.BlockSpec((None, 2 * SCAN_COLS, LANES), lambda b, j: (j, 0, 0)),
                  pl.BlockSpec((1, LANES), lambda b, j: (0, j // 2))],
        out_specs=[pl.BlockSpec((None, n_l, 2 * SCAN_COLS), lambda b, j: (b, 0, j)),
                   pl.BlockSpec((None, n_l, LANES), lambda b, j: (b, 0, j // 2))],
        out_shape=[jax.ShapeDtypeStruct((n_b, n_l, 2 * N_STATES), BF16),
                   jax.ShapeDtypeStruct((n_b, n_l, SSM_WIDTH), F32)],
        scratch_shapes=[pltpu.VMEM((n_l, 2 * SCAN_COLS), F32)] * 2,
        compiler_params=_cp(("arbitrary", "arbitrary"), VMEM_BIG),
    )(proj3, b_comb, tabf, c_comb, dvec)


def ssm_backward_scan(dyraw3, xs3, dproj3, c_comb_t, tabr, b_comb_t, dvec, seq, name):
    n_b, n_l, _ = xs3.shape
    u_blk = (D_MODEL + 2 * KV_WIDTH) // LANES

    def body(dy_ref, x_ref, _, c_ref, tab_ref, b_ref, d_ref, du_ref, g_ref, dlr_ref, dli_ref, dx, gs, xs, du_acc):
        j = pl.program_id(1)
        dy = dy_ref[...]
        dx[...] = _dot(dy, c_ref[...])
        xs[...] = x_ref[...].astype(F32)
        tabs = [tab_ref[k] for k in range(TAB_ROWS)]
        last_row = lax.broadcasted_iota(jnp.int32, (SUBLANES, SCAN_COLS), 0) == SUBLANES - 1

        def group(r0, state):
            cr, ci, acc_r, acc_i = state
            rows = pl.ds(r0, SUBLANES)
            a, b = _scan_rows(dx[rows, :SCAN_COLS], dx[rows, SCAN_COLS:], tabs, (cr, ci), True)
            gs[rows, :SCAN_COLS] = a
            gs[rows, SCAN_COLS:] = b
            na = jnp.where(last_row, cr, pltpu.roll(a, SUBLANES - 1, axis=0))
            nb = jnp.where(last_row, ci, pltpu.roll(b, SUBLANES - 1, axis=0))
            xa, xb = xs[rows, :SCAN_COLS], xs[rows, SCAN_COLS:]
            return (jnp.broadcast_to(a[:1, :], a.shape), jnp.broadcast_to(b[:1, :], b.shape),
                    acc_r + na * xa + nb * xb, acc_i + nb * xa - na * xb)

        zero = jnp.zeros((SUBLANES, SCAN_COLS), F32)
        span = SCAN_UNROLL * SUBLANES
        n_spans = seq // span

        def groups(t, s):
            for k in reversed(range(SCAN_UNROLL)):
                s = group(pl.multiple_of((n_spans - 1 - t) * span, span) + k * SUBLANES, s)
            return s

        state = lax.fori_loop(0, n_spans, groups, (zero, zero, zero, zero))
        for r0 in _time_groups(seq, True):
            state = group(r0, state)
        dlr_ref[...] = state[2]
        dli_ref[...] = state[3]
        g16 = gs[...].astype(BF16)
        g_ref[...] = g16
        contrib = _dot(g16, b_ref[...])

        @pl.when(j % 2 == 0)
        def _():
            du_acc[...] = contrib + d_ref[...] * dy.astype(F32)

        @pl.when(j % 2 == 1)
        def _():
            du_ref[...] = (du_acc[...] + contrib).astype(BF16)

    state_blk = pl.BlockSpec((None, n_l, 2 * SCAN_COLS), lambda b, j: (b, 0, j))
    dl_blk = pl.BlockSpec((None, SUBLANES, SCAN_COLS), lambda b, j: (b, 0, j))
    return _pcall(
        body, name=name, grid=(n_b, N_SCAN_BLK),
        in_specs=[pl.BlockSpec((None, n_l, LANES), lambda b, j: (b, 0, j // 2)), state_blk,
                  pl.BlockSpec(memory_space=pl.ANY),
                  pl.BlockSpec((None, LANES, 2 * SCAN_COLS), lambda b, j: (j, 0, 0)),
                  pl.BlockSpec((TAB_ROWS, SUBLANES, SCAN_COLS), lambda b, j: (0, 0, j)),
                  pl.BlockSpec((None, 2 * SCAN_COLS, LANES), lambda b, j: (j, 0, 0)),
                  pl.BlockSpec((1, LANES), lambda b, j: (0, j // 2))],
        out_specs=[pl.BlockSpec((None, n_l, LANES), lambda b, j: (b, 0, u_blk + j // 2)), state_blk, dl_blk, dl_blk],
        out_shape=[jax.ShapeDtypeStruct(dproj3.shape, BF16), jax.ShapeDtypeStruct((n_b, n_l, 2 * N_STATES), BF16),
                   jax.ShapeDtypeStruct((n_b, SUBLANES, N_STATES), F32), jax.ShapeDtypeStruct((n_b, SUBLANES, N_STATES), F32)],
        scratch_shapes=[pltpu.VMEM((n_l, 2 * SCAN_COLS), F32)] * 3 + [pltpu.VMEM((n_l, LANES), F32)],
        input_output_aliases={2: 0},
        compiler_params=_cp(("arbitrary", "arbitrary"), VMEM_BIG),
    )(dyraw3, xs3, dproj3, c_comb_t, tabr, b_comb_t, dvec)


def ssm_param_grads(proj, gs, xs, dyraw, tm, name):
    t_rows = proj.shape[0]
    ni = t_rows // tm
    u_blk = (D_MODEL + 2 * KV_WIDTH) // LANES
    width = 2 * SCAN_COLS

    def body(u_ref, g_ref, x_ref, dy_ref, db_ref, dc_ref, dd_ref):
        cb, i = pl.program_id(0), pl.program_id(1)
        u, dy = u_ref[...], dy_ref[...]
        _accumulate(db_ref, _dot_tn(u, g_ref[...]), i == 0)
        _accumulate(dc_ref, _dot_tn(x_ref[...], dy), i == 0)

        @pl.when(cb % 2 == 0)
        def _():
            _accumulate(dd_ref, jnp.sum(dy.astype(F32) * u.astype(F32), axis=0, keepdims=True), i == 0)

    return _pcall(
        body, name=name, grid=(N_SCAN_BLK, ni),
        in_specs=[pl.BlockSpec((tm, LANES), lambda cb, i: (i, u_blk + cb // 2)),
                  pl.BlockSpec((tm, width), lambda cb, i: (i, cb)),
                  pl.BlockSpec((tm, width), lambda cb, i: (i, cb)),
                  pl.BlockSpec((tm, LANES), lambda cb, i: (i, cb // 2))],
        out_specs=[pl.BlockSpec((None, LANES, width), lambda cb, i: (cb, 0, 0)),
                   pl.BlockSpec((None, width, LANES), lambda cb, i: (cb, 0, 0)),
                   pl.BlockSpec((1, LANES), lambda cb, i: (0, cb // 2))],
        out_shape=[jax.ShapeDtypeStruct((N_SCAN_BLK, LANES, width), F32),
                   jax.ShapeDtypeStruct((N_SCAN_BLK, width, LANES), F32), jax.ShapeDtypeStruct((1, SSM_WIDTH), F32)],
        compiler_params=_cp(("arbitrary", "arbitrary"), VMEM_BIG),
    )(proj, gs, xs, dyraw)


def sum_leading(x, name):
    def body(x_ref, o_ref):
        acc = x_ref[0]
        for k in range(1, x.shape[0]):
            acc = acc + x_ref[k]
        o_ref[...] = acc

    return _pcall(body, name=name, out_shape=jax.ShapeDtypeStruct(x.shape[1:], x.dtype))(x)


WEIGHTS = ['meta_tokens', 'ffn1_norm', 'ffn1_w1', 'ffn1_w3', 'ffn1_w2', 'mix_norm', 'w_in', 'attn_sinks', 'ssm_a_re',
           'ssm_a_im', 'ssm_log_step', 'ssm_b_re', 'ssm_b_im', 'ssm_c_re', 'ssm_c_im', 'ssm_d', 'ssm_glu_a', 'ssm_glu_b',
           'w_out', 'ffn2_norm', 'ffn2_w1', 'ffn2_w3', 'ffn2_w2', 'final_norm']
SHARDED = ['ffn1_w1', 'ffn1_w3', 'ffn1_w2', 'ffn2_w1', 'ffn2_w3', 'ffn2_w2', 'w_in', 'ssm_glu_a', 'ssm_glu_b', 'w_out']
REPLICATED = ['ffn1_norm', 'mix_norm', 'ffn2_norm', 'final_norm', 'attn_sinks', 'ssm_a_re', 'ssm_a_im', 'ssm_log_step',
              'ssm_b_re', 'ssm_b_im', 'ssm_c_re', 'ssm_c_im', 'ssm_d']
PACK_COLS = 1024


def _pack(arrays):
    parts = []
    for a in arrays:
        flat = a.reshape(-1)
        chunk = SUBLANES * PACK_COLS
        padded = -(-flat.shape[0] // chunk) * chunk
        parts.append(jnp.pad(flat, (0, padded - flat.shape[0])).reshape(-1, PACK_COLS))
    return jnp.concatenate(parts, axis=0)


def _unpack(packed, shapes):
    out, row = [], 0
    for shape in shapes:
        size = 1
        for s in shape:
            size *= s
        chunk = SUBLANES * PACK_COLS
        rows = -(-size // chunk) * SUBLANES
        out.append(packed[row:row + rows].reshape(-1)[:size].reshape(shape))
        row += rows
    return out


def kernel(x, meta_tokens, ffn1_norm, ffn1_w1, ffn1_w3, ffn1_w2, mix_norm, w_in, attn_sinks, ssm_a_re, ssm_a_im, ssm_log_step, ssm_b_re, ssm_b_im, ssm_c_re, ssm_c_im, ssm_d, ssm_glu_a, ssm_glu_b, w_out, ffn2_norm, ffn2_w1, ffn2_w3, ffn2_w2, final_norm, loss_target, m_meta_tokens, m_ffn1_norm, m_ffn1_w1, m_ffn1_w3, m_ffn1_w2, m_mix_norm, m_w_in, m_attn_sinks, m_ssm_a_re, m_ssm_a_im, m_ssm_log_step, m_ssm_b_re, m_ssm_b_im, m_ssm_c_re, m_ssm_c_im, m_ssm_d, m_ssm_glu_a, m_ssm_glu_b, m_w_out, m_ffn2_norm, m_ffn2_w1, m_ffn2_w3, m_ffn2_w2, m_final_norm, v_meta_tokens, v_ffn1_norm, v_ffn1_w1, v_ffn1_w3, v_ffn1_w2, v_mix_norm, v_w_in, v_attn_sinks, v_ssm_a_re, v_ssm_a_im, v_ssm_log_step, v_ssm_b_re, v_ssm_b_im, v_ssm_c_re, v_ssm_c_im, v_ssm_d, v_ssm_glu_a, v_ssm_glu_b, v_w_out, v_ffn2_norm, v_ffn2_w1, v_ffn2_w3, v_ffn2_w2, v_final_norm):
    given = dict(locals())
    w = {n: given[n] for n in WEIGHTS}
    m = {n: given["m_" + n] for n in WEIGHTS}
    v = {n: given["v_" + n] for n in WEIGHTS}

    n_b, seq, _ = x.shape
    n_l = seq + N_META
    t_rows = n_b * n_l
    tm = _row_tile(n_l, 688)
    px, py, pc = _my_place()
    me = 4 * px + 2 * py + pc

    glu = jnp.stack([ssm_glu_a[0], ssm_glu_b[0]]).astype(BF16)
    ffn_names = ['ffn1_w1', 'ffn1_w3', 'ffn1_w2', 'ffn2_w1', 'ffn2_w3', 'ffn2_w2']

    def hidden_on_rows(n, t):
        return t[0] if n.endswith('w2') else t[0].T

    def hidden_on_rows_back(n, t):
        return t[None] if n.endswith('w2') else t.T[None]

    me_idx = jnp.reshape(me, (1,)).astype(jnp.int32)
    first_names, later_names = ffn_names[:3], ffn_names[3:]
    *first, metag = all_gather_list(
        [hidden_on_rows(n, w[n]).astype(BF16) for n in first_names] + [meta_tokens], "ag_first")
    win_send, win_recv, win_shard, win_land, win_token = exchange_start(
        [w_in[0].astype(BF16)], first[0], True, "ag_w_in_start")
    later_shards = [hidden_on_rows(n, w[n]).astype(BF16) for n in later_names] + [glu, w_out[0].astype(BF16)]
    ag_send, ag_recv, later_shards, later_lands, ag_token = exchange_start(later_shards, win_token, True, "ag_later_start")
    full = {n: g.reshape(D_FF, D_MODEL) for n, g in zip(first_names, first)}
    meta_full = metag.transpose(1, 0, 2).reshape(N_META, D_MODEL)

    h0 = jnp.concatenate([x, jnp.broadcast_to(meta_full[None], (n_b, N_META, D_MODEL))], axis=1).reshape(t_rows, D_MODEL)
    target = jnp.concatenate([loss_target, jnp.zeros((n_b, N_META, D_MODEL), F32)], axis=1).reshape(t_rows, D_MODEL)
    final_g = final_norm.reshape(1, D_MODEL)

    ar = ssm_a_re.reshape(1, N_STATES)
    ai = ssm_a_im.reshape(1, N_STATES)
    ls = jnp.repeat(ssm_log_step.reshape(SSM_GROUPS), SSM_STATE).reshape(1, N_STATES)
    br_t = ssm_b_re[0].transpose(2, 0, 1).reshape(SSM_GROUP, N_STATES)
    bi_t = ssm_b_im[0].transpose(2, 0, 1).reshape(SSM_GROUP, N_STATES)
    bbr, bbi, tabf, tabr = ssm_prepare(ar, ai, ls, br_t, bi_t, "ssm_prepare")
    bbr_g = bbr.reshape(SSM_GROUP, SSM_GROUPS, SSM_STATE).transpose(1, 0, 2)
    bbi_g = bbi.reshape(SSM_GROUP, SSM_GROUPS, SSM_STATE).transpose(1, 0, 2)
    groups_per_blk = SCAN_COLS // SSM_STATE
    half = ((jnp.arange(N_SCAN_BLK) % 2)[:, None] == jnp.arange(2)[None, :]).astype(F32)
    eye = jnp.eye(groups_per_blk, dtype=F32)

    def scan_blocks(re_g, im_g):
        def one(t):
            t = t.reshape(N_SCAN_BLK, groups_per_blk, SSM_GROUP, SSM_STATE)
            t = t[:, :, :, None, :] * eye[None, :, None, :, None]
            t = t.reshape(N_SCAN_BLK, LANES // 2, SCAN_COLS)
            return (t[:, None] * half[:, :, None, None]).reshape(N_SCAN_BLK, LANES, SCAN_COLS)
        return jnp.concatenate([one(re_g), one(im_g)], axis=-1).astype(BF16)

    b_comb = scan_blocks(bbr_g, bbi_g)
    c_comb_t = scan_blocks(ssm_c_re[0], -ssm_c_im[0])
    b_comb_t, c_comb = b_comb.transpose(0, 2, 1), c_comb_t.transpose(0, 2, 1)

    ffn1_w = (full['ffn1_w1'], full['ffn1_w3'], full['ffn1_w2'])
    h1, hn1, a1, b1 = ffn_forward(h0, ffn1_norm, *ffn1_w, ag_token, tm, "ffn1_fwd")
    (wing,) = exchange_wait(win_send, win_recv, win_shard, win_land, h1, True, "ag_w_in_wait")
    wing = lax.dynamic_update_slice_in_dim(wing, win_shard[0][None], me, axis=0)
    hnm, proj = mix_forward(h1, mix_norm, wing, tm, "mix_fwd")
    proj3 = proj.reshape(n_b, n_l, IN_WIDTH)
    attn3 = attention_forward(proj3, attn_sinks, seq, "attn_fwd")
    attn = attn3.reshape(t_rows, D_MODEL)
    xs3, yraw3 = ssm_forward_scan(proj3, b_comb, tabf, c_comb, ssm_d, seq, "ssm_fwd")
    yraw = yraw3.reshape(t_rows, SSM_WIDTH)
    later = exchange_wait(ag_send, ag_recv, later_shards, later_lands, yraw3, True, "ag_later_wait")
    later = [lax.dynamic_update_slice_in_dim(z, s[None], me, axis=0) for z, s in zip(later, later_shards)]
    for n, g in zip(later_names, later):
        full[n] = g.reshape(D_FF, D_MODEL)
    ffn2_w = (full['ffn2_w1'], full['ffn2_w3'], full['ffn2_w2'])
    glug, wog = later[len(later_names):]
    glu_a = glug[:, 0].transpose(1, 0, 2).reshape(SSM_WIDTH, D_MODEL)
    glu_b = glug[:, 1].transpose(1, 0, 2).reshape(SSM_WIDTH, D_MODEL)
    w_out_full = wog.reshape(D_MODEL, D_MODEL)
    h2 = merge_forward(h1, yraw, attn, proj, glu_a, glu_b, w_out_full, tm, "merge_fwd")
    h3, hn2, a2, b2 = ffn_forward(h2, ffn2_norm, *ffn2_w, ag_token, tm, "ffn2_fwd")
    dh3, loss_part, g_final = final_loss_backward(h3, target, final_g, seq, tm, "loss_bwd")
    loss = lax.psum(loss_part[0, 0], ("x", "y", "c"))

    def blocked_ffn(d_w1t, d_w3t, d_w2):
        return tuple(t.reshape(N_DEV, FF_BLK, D_MODEL) for t in (d_w1t, d_w3t, d_w2))

    def blocked_cols(full_grad):
        r = full_grad.shape[0]
        return full_grad.reshape(r, N_DEV, full_grad.shape[1] // N_DEV).transpose(1, 0, 2).astype(BF16)

    early = {}

    def start_reduce(names, tag):
        srcs = [dw[n] for n in names]
        send, recv, srcs, lands, token = exchange_start(srcs, srcs[0], False, "rs_" + tag + "_start")
        early[tag] = (names, send, recv, srcs, lands)
        return token

    dw = {}
    da2, db2, dh3_half = ffn_backward_hidden(dh3, a2, b2, ffn2_w[2], g_final, tm, "ffn2_bwd_hid")
    dw['ffn2_w1'], dw['ffn2_w3'], dw['ffn2_w2'] = blocked_ffn(
        *ffn_backward_weights(hn2, dh3_half, a2, b2, da2, db2, n_l, FF_BWD_COLS, "ffn2_bwd_w"))
    token = start_reduce(later_names, "ffn2")
    dh2, g_ffn2_norm = ffn_backward_input(dh3, h2, ffn2_norm, da2, db2, ffn2_w[0], ffn2_w[1], token, tm, "ffn2_bwd_in")
    dattn, dyraw, dproj, *for_weights = merge_backward(dh2, yraw, attn, proj, glu_a, glu_b, w_out_full, token, tm,
                                                       "merge_bwd")
    d_wo, d_ga, d_gb = merge_backward_weights(*for_weights, tm, "merge_bwd_w")
    dw['ssm_glu_a'] = blocked_cols(d_ga)
    dw['ssm_glu_b'] = blocked_cols(d_gb)
    dw['w_out'] = d_wo.reshape(N_DEV, D_MODEL // N_DEV, D_MODEL).astype(BF16)
    token = start_reduce(['ssm_glu_a', 'ssm_glu_b', 'w_out'], "mix")
    dproj3 = dproj.reshape(n_b, n_l, IN_WIDTH)
    dproj3, dsink_p = attention_backward(proj3, dattn.reshape(n_b, n_l, D_MODEL), dproj3, attn_sinks, token, seq,
                                         "attn_bwd")
    dproj3, gs3, dlr_p, dli_p = ssm_backward_scan(
        dyraw.reshape(n_b, n_l, SSM_WIDTH), xs3, dproj3, c_comb_t, tabr, b_comb_t, ssm_d, seq, "ssm_bwd")
    dproj = dproj3.reshape(t_rows, IN_WIDTH)
    d_bd, d_cd, g_d = ssm_param_grads(proj, gs3.reshape(t_rows, 2 * N_STATES), xs3.reshape(t_rows, 2 * N_STATES),
                                      dyraw, n_l, "ssm_bwd_w")
    w_in_full = wing.transpose(1, 0, 2).reshape(D_MODEL, IN_WIDTH)
    dh1, g_mix_norm = mix_backward_act(dh2, h1, mix_norm, dproj, w_in_full, tm, "mix_bwd_act")
    dw['w_in'] = mix_backward_weights(hnm, dproj, n_l, "mix_bwd_w")
    token = start_reduce(['w_in'], "w_in")
    da1, db1, dh1_half = ffn_backward_hidden(dh1, a1, b1, ffn1_w[2], token, tm, "ffn1_bwd_hid")
    dw['ffn1_w1'], dw['ffn1_w3'], dw['ffn1_w2'] = blocked_ffn(
        *ffn_backward_weights(hn1, dh1_half, a1, b1, da1, db1, n_l, FF_BWD_COLS, "ffn1_bwd_w"))
    token = start_reduce(first_names, "ffn1")
    dh0, g_ffn1_norm = ffn_backward_input(dh1, h0, ffn1_norm, da1, db1, ffn1_w[0], ffn1_w[1], token, tm, "ffn1_bwd_in")
    dh0_3 = dh0.reshape(n_b, n_l, D_MODEL)
    grad_x = dh0_3[:, :seq]
    g_meta = sum_leading(dh0_3[:, seq:], "meta_sum")

    def group_blocks(part, channels_first):
        if channels_first:
            t = jnp.sum(part.reshape(N_SCAN_BLK, 2, LANES // 2, SCAN_COLS) * half[:, :, None, None], axis=1)
            t = t.reshape(N_SCAN_BLK, groups_per_blk, SSM_GROUP, groups_per_blk, SSM_STATE)
            t = jnp.sum(t * eye[None, :, None, :, None], axis=3)
            return t.reshape(SSM_GROUPS, SSM_GROUP, SSM_STATE)
        t = jnp.sum(part.reshape(N_SCAN_BLK, SCAN_COLS, 2, LANES // 2) * half[:, None, :, None], axis=2)
        t = t.reshape(N_SCAN_BLK, groups_per_blk, SSM_STATE, groups_per_blk, SSM_GROUP)
        t = jnp.sum(t * eye[None, :, None, :, None], axis=3)
        return t.reshape(SSM_GROUPS, SSM_STATE, SSM_GROUP).transpose(0, 2, 1)

    dbbr = group_blocks(d_bd[:, :, :SCAN_COLS], True).transpose(1, 0, 2).reshape(SSM_GROUP, N_STATES)
    dbbi = group_blocks(d_bd[:, :, SCAN_COLS:], True).transpose(1, 0, 2).reshape(SSM_GROUP, N_STATES)
    g_c_re = group_blocks(d_cd[:, :SCAN_COLS, :], False)[None]
    g_c_im = -group_blocks(d_cd[:, SCAN_COLS:, :], False)[None]
    group_sum = (jnp.arange(N_STATES)[:, None] // SSM_STATE == jnp.arange(LANES)[None, :]).astype(F32)
    g_ar, g_ai, g_ls, g_br, g_bi = ssm_param_backward(
        ar, ai, ls, br_t, bi_t, dlr_p.reshape(n_b * SUBLANES, N_STATES), dli_p.reshape(n_b * SUBLANES, N_STATES),
        dbbr, dbbi, group_sum, "ssm_bwd_params")
    g_sinks = sum_leading(dsink_p, "sink_sum")[0:1, :N_KV_HEADS * Q_PER_KV]

    small = {
        'ffn1_norm': g_ffn1_norm, 'mix_norm': g_mix_norm, 'ffn2_norm': g_ffn2_norm, 'final_norm': g_final.reshape(D_MODEL),
        'attn_sinks': g_sinks, 'ssm_a_re': g_ar.reshape(1, SSM_GROUPS, SSM_STATE), 'ssm_a_im': g_ai.reshape(1, SSM_GROUPS, SSM_STATE),
        'ssm_log_step': g_ls[:, :SSM_GROUPS],
        'ssm_b_re': g_br.reshape(SSM_GROUP, SSM_GROUPS, SSM_STATE).transpose(1, 2, 0)[None],
        'ssm_b_im': g_bi.reshape(SSM_GROUP, SSM_GROUPS, SSM_STATE).transpose(1, 2, 0)[None],
        'ssm_c_re': g_c_re, 'ssm_c_im': g_c_im, 'ssm_d': g_d,
    }

    zeros_meta = jnp.zeros((N_META, D_MODEL), F32)
    packed_g = _pack([small[n] for n in REPLICATED] + [g_meta])
    (parts,) = all_gather_list([packed_g], "ag_small_grads")
    packed_out = adamw_small(parts, _pack([w[n] for n in REPLICATED] + [zeros_meta]),
                             _pack([m[n] for n in REPLICATED] + [zeros_meta]),
                             _pack([v[n] for n in REPLICATED] + [zeros_meta]), "adamw_small")
    shapes = [w[n].shape for n in REPLICATED] + [(N_META, D_MODEL)]
    grads, deltas, new_m, new_v = {}, {}, {}, {}
    unpacked = [_unpack(p, shapes) for p in packed_out]
    for k, n in enumerate(REPLICATED):
        grads[n], deltas[n], new_m[n], new_v[n] = (u[k] for u in unpacked)
    g_meta_full = unpacked[0][-1]
    grads['meta_tokens'] = lax.dynamic_index_in_dim(
        g_meta_full.reshape(N_META, N_DEV, D_MODEL // N_DEV), me, axis=1, keepdims=False)
    deltas['meta_tokens'], new_m['meta_tokens'], new_v['meta_tokens'] = adamw_plain(
        grads['meta_tokens'], w['meta_tokens'], m['meta_tokens'], v['meta_tokens'], "adamw_meta")

    def views(n):
        if n in ffn_names:
            return functools.partial(hidden_on_rows, n), functools.partial(hidden_on_rows_back, n)
        return (lambda t: t[0]), (lambda t: t[None])

    previous = packed_out[0]
    for tag, (names, send, recv, srcs, lands) in early.items():
        lands = exchange_wait(send, recv, srcs, lands, previous, False, "rs_" + tag + "_wait")
        for n, g, land in zip(names, srcs, lands):
            two_d, back = views(n)
            out = adamw_exchanged(me_idx, g, land, two_d(w[n]), two_d(m[n]), two_d(v[n]), "adamw_" + n)
            grads[n], deltas[n], new_m[n], new_v[n] = (back(o) for o in out)
            previous = out[1]

    return (loss, grad_x, *[grads[n] for n in WEIGHTS], *[deltas[n] for n in WEIGHTS],
            *[new_m[n] for n in WEIGHTS], *[new_v[n] for n in WEIGHTS])
```

```python
import functools

import jax
import jax.numpy as jnp
from jax import lax
from jax.experimental import pallas as pl
from jax.experimental.pallas import tpu as pltpu

F32 = jnp.float32
BF16 = jnp.bfloat16
MESH = pl.DeviceIdType.MESH

N_DEV = 8
D_MODEL = 1024
N_META = 16
HEAD_DIM = 64
N_KV_HEADS = 4
Q_PER_KV = 4
BLOCK = 128
KV_WIDTH = N_KV_HEADS * HEAD_DIM
SSM_GROUP = 16
SSM_WIDTH = 512
SSM_GROUPS = 32
SSM_STATE = 64
N_STATES = SSM_GROUPS * SSM_STATE
D_FF = 2816
FF_BLK = D_FF // N_DEV
IN_WIDTH = 4096
IN_BLK = IN_WIDTH // N_DEV
NORM_EPS = 1e-6
NEG_INF = -1e30
SCAN_COLS = 256
N_SCAN_BLK = N_STATES // SCAN_COLS
SUBLANES = 8
LANES = 128
MXU_WIDTH = 256
FF_BWD_COLS = MXU_WIDTH

ADAM_LR = 0.001
ADAM_B1 = 0.9
ADAM_B2 = 0.999
ADAM_EPS = 1e-08
ADAM_WD = 0.01
ADAM_STEP = 10

VMEM_BIG = 56 * 1024 * 1024


def _cp(sem=None, vmem=None):
    kw = {}
    if sem is not None:
        kw["dimension_semantics"] = sem
    if vmem is not None:
        kw["vmem_limit_bytes"] = vmem
    return pltpu.CompilerParams(**kw)


def _pcall(body, **kw):
    return pl.pallas_call(body, **kw)


def _dot(a, b):
    return jnp.dot(a, b, preferred_element_type=F32)


def _dot_nt(a, b):
    return lax.dot_general(a, b, (((1,), (1,)), ((), ())), preferred_element_type=F32)


def _dot_tn(a, b):
    return lax.dot_general(a, b, (((0,), (0,)), ((), ())), preferred_element_type=F32)


def _sigmoid(x):
    return 1.0 / (1.0 + jnp.exp(-x))


def _row_tile(rows, cap):
    best = None
    for t in range(16, min(rows, cap) + 1, 16):
        if rows % t == 0:
            best = t
    assert best is not None, rows
    return best


def _my_place():
    return lax.axis_index("x"), lax.axis_index("y"), lax.axis_index("c")


def all_gather_list(shards, name):
    n = len(shards)

    def body(*refs):
        ins, outs = refs[:n], refs[n:2 * n]
        send_sems, recv_sems, local_sems = refs[2 * n:]
        x, y, c = _my_place()
        me, sibling = (x, y, c), (x, y, 1 - c)
        chips = [(1 - x, y), (x, 1 - y), (1 - x, 1 - y)]

        def blk(a, px, py, pc):
            return outs[a].at[4 * px + 2 * py + pc]

        def copy(a, k, block, to, src=None):
            return pltpu.make_async_remote_copy(
                src_ref=blk(a, *block) if src is None else src, dst_ref=blk(a, *block),
                send_sem=send_sems.at[a * 7 + k], recv_sem=recv_sems.at[a * 7 + k],
                device_id=to, device_id_type=MESH)

        mine = [pltpu.make_async_copy(ins[a], blk(a, *me), local_sems.at[a]) for a in range(n)]
        for cp in mine:
            cp.start()
        first = []
        for a in range(n):
            first.append(copy(a, 0, me, sibling, src=ins[a]))
            first += [copy(a, 1 + j, me, (*chip, c), src=ins[a]) for j, chip in enumerate(chips)]
        for cp in first:
            cp.start()
        passed = []
        for j, chip in enumerate(chips):
            for a in range(n):
                copy(a, 1 + j, (*chip, c), me).wait_recv()
                cp = copy(a, 4 + j, (*chip, c), sibling)
                cp.start()
                passed.append(cp)
        for a in range(n):
            copy(a, 0, sibling, me).wait_recv()
            for j, chip in enumerate(chips):
                copy(a, 4 + j, (*chip, 1 - c), me).wait_recv()
        for cp in first + passed:
            cp.wait_send()
        for cp in mine:
            cp.wait()

    any_spec = pl.BlockSpec(memory_space=pl.ANY)
    return _pcall(
        body, name=name,
        out_shape=[jax.ShapeDtypeStruct((N_DEV,) + s.shape, s.dtype) for s in shards],
        in_specs=[any_spec] * n, out_specs=[any_spec] * n,
        scratch_shapes=[pltpu.SemaphoreType.DMA((7 * n,)), pltpu.SemaphoreType.DMA((7 * n,)),
                        pltpu.SemaphoreType.DMA((n,))],
    )(*shards)


HBM_SPEC = pl.BlockSpec(memory_space=pltpu.HBM)
SEM_SPEC = pl.BlockSpec(memory_space=pltpu.SEMAPHORE)
N_PEERS = N_DEV - 1


def _related(k):
    x, y, c = _my_place()
    px = 1 - x if k & 4 else x
    py = 1 - y if k & 2 else y
    pc = 1 - c if k & 1 else c
    return (px, py, pc), 4 * px + 2 * py + pc


def _exchange_copies(srcs, lands, send_sems, recv_sems, gather):
    x, y, c = _my_place()
    me = 4 * x + 2 * y + c
    copies = []
    for a, (src, land) in enumerate(zip(srcs, lands)):
        for k in range(1, N_DEV):
            peer, d = _related(k)
            copies.append(pltpu.make_async_remote_copy(
                src_ref=src if gather else src.at[d], dst_ref=land.at[me] if gather else land.at[k],
                send_sem=send_sems.at[a * N_PEERS + k - 1], recv_sem=recv_sems.at[a * N_PEERS + k - 1],
                device_id=peer, device_id_type=MESH))
    return copies


def exchange_start(srcs, after, gather, name):
    n = len(srcs)
    land_shapes = [((N_DEV,) + s.shape) if gather else s.shape for s in srcs]

    def body(*refs):
        send_sems, recv_sems = refs[2 * n + 1], refs[2 * n + 2]
        for cp in _exchange_copies(refs[:n], refs[n:2 * n], send_sems, recv_sems, gather):
            cp.start()
        token = refs[-1]
        token[...] = jnp.zeros_like(token)

    sems = pltpu.SemaphoreType.DMA((n * N_PEERS,))
    lands = [pltpu.with_memory_space_constraint(lax.empty(shape, s.dtype), pltpu.HBM) for shape, s in zip(land_shapes, srcs)]
    out = _pcall(
        body, name=name,
        out_shape=(sems, sems, *[pltpu.HBM(s.shape, s.dtype) for s in srcs],
                   *[pltpu.HBM(shape, s.dtype) for shape, s in zip(land_shapes, srcs)],
                   jax.ShapeDtypeStruct((SUBLANES, LANES), F32)),
        in_specs=[HBM_SPEC] * (2 * n) + [pl.BlockSpec(memory_space=pl.ANY)],
        out_specs=(SEM_SPEC, SEM_SPEC, *[HBM_SPEC] * (2 * n), pl.BlockSpec(memory_space=pltpu.VMEM)),
        input_output_aliases={i: 2 + i for i in range(2 * n)},
        compiler_params=pltpu.CompilerParams(has_side_effects=pltpu.SideEffectType.DATAFLOW_SIDE_EFFECTING),
    )(*[pltpu.with_memory_space_constraint(s, pltpu.HBM) for s in srcs], *lands, after)
    return out[0], out[1], list(out[2:2 + n]), list(out[2 + n:2 + 2 * n]), out[-1]


def exchange_wait(send_sems, recv_sems, srcs, lands, after, gather, name):
    n = len(srcs)

    def body(*refs):
        for cp in _exchange_copies(refs[:n], refs[n:2 * n], refs[2 * n], refs[2 * n + 1], gather):
            cp.wait_send()
            cp.wait_recv()

    out = _pcall(
        body, name=name,
        out_shape=(*[pltpu.HBM(s.shape, s.dtype) for s in srcs], *[pltpu.HBM(z.shape, z.dtype) for z in lands]),
        in_specs=[HBM_SPEC] * (2 * n) + [SEM_SPEC, SEM_SPEC, pl.BlockSpec(memory_space=pl.ANY)],
        out_specs=tuple([HBM_SPEC] * (2 * n)),
        input_output_aliases={i: i for i in range(2 * n)},
        compiler_params=pltpu.CompilerParams(has_side_effects=pltpu.SideEffectType.DATAFLOW_SIDE_EFFECTING),
    )(*srcs, *lands, send_sems, recv_sems, after)
    return list(out[n:])


def adamw_exchanged(me, g, land, w, m, v, name):
    rows, cols = w.shape
    tr = _row_tile(rows, 256)

    def body(me_ref, g_ref, land_ref, w_ref, m_ref, v_ref, go_ref, d_ref, mo_ref, vo_ref):
        grad = g_ref[...].astype(F32)
        for k in range(1, N_DEV):
            grad = grad + land_ref[k].astype(F32)
        delta, m_new, v_new = _adam_math(w_ref[...], grad, m_ref[...], v_ref[...])
        go_ref[...] = grad
        d_ref[...] = delta
        mo_ref[...] = m_new
        vo_ref[...] = v_new

    tile = pl.BlockSpec((tr, cols), lambda r, ix: (r, 0))
    out = jax.ShapeDtypeStruct((rows, cols), F32)
    return _pcall(
        body, name=name, out_shape=[out] * 4,
        grid_spec=pltpu.PrefetchScalarGridSpec(
            num_scalar_prefetch=1, grid=(rows // tr,),
            in_specs=[pl.BlockSpec((None, tr, cols), lambda r, ix: (ix[0], r, 0)),
                      pl.BlockSpec((N_DEV, tr, cols), lambda r, ix: (0, r, 0)), tile, tile, tile],
            out_specs=[tile] * 4),
        compiler_params=_cp(("arbitrary",)),
    )(me, g, land, w, m, v)


def _adam_math(w, g, m, v):
    m = ADAM_B1 * m + (1.0 - ADAM_B1) * g
    v = ADAM_B2 * v + (1.0 - ADAM_B2) * (g * g)
    m_hat = m / (1.0 - ADAM_B1 ** ADAM_STEP)
    v_hat = v / (1.0 - ADAM_B2 ** ADAM_STEP)
    delta = -ADAM_LR * (m_hat / (jnp.sqrt(v_hat) + ADAM_EPS) + ADAM_WD * w)
    return delta, m, v


def adamw_small(parts, w, m, v, name):
    _, rows, cols = parts.shape

    def body(p_ref, w_ref, m_ref, v_ref, go_ref, d_ref, mo_ref, vo_ref):
        grad = p_ref[0]
        for k in range(1, N_DEV):
            grad = grad + p_ref[k]
        delta, m_new, v_new = _adam_math(w_ref[...], grad, m_ref[...], v_ref[...])
        go_ref[...] = grad
        d_ref[...] = delta
        mo_ref[...] = m_new
        vo_ref[...] = v_new

    out = jax.ShapeDtypeStruct((rows, cols), F32)
    return _pcall(body, name=name, out_shape=[out] * 4, compiler_params=_cp(vmem=VMEM_BIG))(parts, w, m, v)


def adamw_plain(g, w, m, v, name):
    def body(g_ref, w_ref, m_ref, v_ref, d_ref, mo_ref, vo_ref):
        delta, m_new, v_new = _adam_math(w_ref[...], g_ref[...], m_ref[...], v_ref[...])
        d_ref[...] = delta
        mo_ref[...] = m_new
        vo_ref[...] = v_new

    out = jax.ShapeDtypeStruct(w.shape, F32)
    return _pcall(body, name=name, out_shape=[out] * 3)(g, w, m, v)


def _rms_fwd(x, g):
    r = lax.rsqrt(jnp.mean(x * x, axis=-1, keepdims=True) + NORM_EPS)
    return x * r * g


def _rms_bwd(x, g, dy):
    r = lax.rsqrt(jnp.mean(x * x, axis=-1, keepdims=True) + NORM_EPS)
    xh = x * r
    t = dy * g
    dx = r * (t - xh * jnp.mean(t * xh, axis=-1, keepdims=True))
    return dx, jnp.sum(dy * xh, axis=0, keepdims=True)


def _accumulate(ref, val, first):
    @pl.when(first)
    def _():
        ref[...] = val

    @pl.when(jnp.logical_not(first))
    def _():
        ref[...] += val


def _col_chunks(width):
    return [(c0, min(MXU_WIDTH, width - c0)) for c0 in range(0, width, MXU_WIDTH)]


ANY_SPEC = pl.BlockSpec(memory_space=pl.ANY)


def ffn_forward(h, norm, w1, w3, w2, after, tm, name):
    t_rows = h.shape[0]

    def body(h_ref, g_ref, w1_ref, w3_ref, w2_ref, _, out_ref, hn_ref, a_ref, b_ref, hid_ref):
        hn = _rms_fwd(h_ref[...], g_ref[...]).astype(BF16)
        hn_ref[...] = hn
        for c0, cw in _col_chunks(D_FF):
            a = _dot_nt(hn, w1_ref[c0:c0 + cw, :])
            b = _dot_nt(hn, w3_ref[c0:c0 + cw, :])
            a_ref[:, c0:c0 + cw] = a.astype(BF16)
            b_ref[:, c0:c0 + cw] = b.astype(BF16)
            hid_ref[:, c0:c0 + cw] = (a * _sigmoid(a) * b).astype(BF16)
        out_ref[...] = h_ref[...] + 0.5 * _dot(hid_ref[...], w2_ref[...])

    row = pl.BlockSpec((tm, D_MODEL), lambda i: (i, 0))
    hid_blk = pl.BlockSpec((tm, D_FF), lambda i: (i, 0))
    weight = _resident((D_FF, D_MODEL))
    return _pcall(
        body, name=name, grid=(t_rows // tm,),
        in_specs=[row, pl.BlockSpec((1, D_MODEL), lambda i: (0, 0)), weight, weight, weight, ANY_SPEC],
        out_specs=[row, row, hid_blk, hid_blk],
        out_shape=[jax.ShapeDtypeStruct((t_rows, D_MODEL), F32), jax.ShapeDtypeStruct((t_rows, D_MODEL), BF16),
                   jax.ShapeDtypeStruct((t_rows, D_FF), BF16), jax.ShapeDtypeStruct((t_rows, D_FF), BF16)],
        scratch_shapes=[pltpu.VMEM((tm, D_FF), BF16)],
        compiler_params=_cp(("arbitrary",), VMEM_BIG),
    )(h, norm, w1, w3, w2, after)


def _resident(shape):
    return pl.BlockSpec(shape, lambda *_: (0,) * len(shape), pipeline_mode=pl.Buffered(1))


def ffn_backward_hidden(dh, a, b, w2, after, tm, name):
    t_rows = dh.shape[0]

    def body(dh_ref, a_ref, b_ref, w2_ref, _, da_ref, db_ref, dhb_ref):
        dhb = (0.5 * dh_ref[...]).astype(BF16)
        dhb_ref[...] = dhb
        for c0, cw in _col_chunks(D_FF):
            dhid = _dot_nt(dhb, w2_ref[c0:c0 + cw, :])
            av = a_ref[:, c0:c0 + cw].astype(F32)
            bv = b_ref[:, c0:c0 + cw].astype(F32)
            s = _sigmoid(av)
            da_ref[:, c0:c0 + cw] = (dhid * bv * (s * (1.0 + av * (1.0 - s)))).astype(BF16)
            db_ref[:, c0:c0 + cw] = (dhid * (av * s)).astype(BF16)

    hid = pl.BlockSpec((tm, D_FF), lambda i: (i, 0))
    row = pl.BlockSpec((tm, D_MODEL), lambda i: (i, 0))
    return _pcall(
        body, name=name, grid=(t_rows // tm,),
        in_specs=[row, hid, hid, _resident((D_FF, D_MODEL)), ANY_SPEC],
        out_specs=[hid, hid, row],
        out_shape=[jax.ShapeDtypeStruct((t_rows, D_FF), BF16), jax.ShapeDtypeStruct((t_rows, D_FF), BF16),
                   jax.ShapeDtypeStruct((t_rows, D_MODEL), BF16)],
        compiler_params=_cp(("arbitrary",), VMEM_BIG),
    )(dh, a, b, w2, after)


def ffn_backward_input(dh, h, norm, da, db, w1, w3, after, tm, name):
    t_rows = h.shape[0]

    def body(dh_ref, h_ref, g_ref, da_ref, db_ref, w1_ref, w3_ref, _, dhin_ref, dg_ref):
        dhn = _dot(da_ref[...], w1_ref[...]) + _dot(db_ref[...], w3_ref[...])
        dx, dg = _rms_bwd(h_ref[...], g_ref[...], dhn)
        dhin_ref[...] = dh_ref[...] + dx
        _accumulate(dg_ref, dg, pl.program_id(0) == 0)

    row = pl.BlockSpec((tm, D_MODEL), lambda i: (i, 0))
    vec = pl.BlockSpec((1, D_MODEL), lambda i: (0, 0))
    hid = pl.BlockSpec((tm, D_FF), lambda i: (i, 0))
    return _pcall(
        body, name=name, grid=(t_rows // tm,),
        in_specs=[row, row, vec, hid, hid, _resident((D_FF, D_MODEL)), _resident((D_FF, D_MODEL)), ANY_SPEC],
        out_specs=[row, vec],
        out_shape=[jax.ShapeDtypeStruct((t_rows, D_MODEL), F32), jax.ShapeDtypeStruct((1, D_MODEL), F32)],
        compiler_params=_cp(("arbitrary",), VMEM_BIG),
    )(dh, h, norm, da, db, w1, w3, after)


def ffn_backward_weights(hn, dh, a, b, da, db, tm, tn, name):
    t_rows = hn.shape[0]
    ni = t_rows // tm
    kc = _row_tile(tm, 688)

    def body(hn_ref, dh_ref, a_ref, b_ref, da_ref, db_ref, dw1_ref, dw3_ref, dw2_ref, acc1, acc3, acc2):
        i = pl.program_id(1)
        parts = None
        for r0 in range(0, tm, kc):
            rows = slice(r0, r0 + kc)
            hn_v = hn_ref[rows, :]
            av = a_ref[rows, :].astype(F32)
            hid = (av * _sigmoid(av) * b_ref[rows, :].astype(F32)).astype(BF16)
            new = (_dot_tn(hn_v, da_ref[rows, :]), _dot_tn(hn_v, db_ref[rows, :]), _dot_tn(hid, dh_ref[rows, :]))
            parts = new if parts is None else tuple(p + q for p, q in zip(parts, new))
        _accumulate(acc1, parts[0], i == 0)
        _accumulate(acc3, parts[1], i == 0)
        _accumulate(acc2, parts[2], i == 0)

        @pl.when(i == ni - 1)
        def _():
            dw1_ref[...] = acc1[...].T.astype(BF16)
            dw3_ref[...] = acc3[...].T.astype(BF16)
            dw2_ref[...] = acc2[...].astype(BF16)

    row = pl.BlockSpec((tm, D_MODEL), lambda j, i: (i, 0))
    hid_blk = pl.BlockSpec((tm, tn), lambda j, i: (i, j))
    w_row = pl.BlockSpec((tn, D_MODEL), lambda j, i: (j, 0))
    out = jax.ShapeDtypeStruct((D_FF, D_MODEL), BF16)
    return _pcall(
        body, name=name, grid=(D_FF // tn, ni),
        in_specs=[row, row, hid_blk, hid_blk, hid_blk, hid_blk],
        out_specs=[w_row, w_row, w_row], out_shape=[out, out, out],
        scratch_shapes=[pltpu.VMEM((D_MODEL, tn), F32), pltpu.VMEM((D_MODEL, tn), F32), pltpu.VMEM((tn, D_MODEL), F32)],
        compiler_params=_cp(("arbitrary", "arbitrary"), VMEM_BIG),
    )(hn, dh, a, b, da, db)


def mix_forward(h, norm, wing, tm, name):
    t_rows = h.shape[0]

    def body(h_ref, g_ref, w_ref, hn_ref, p_ref):
        hn = _rms_fwd(h_ref[...], g_ref[...]).astype(BF16)
        hn_ref[...] = hn
        for j in range(N_DEV):
            p_ref[:, j * IN_BLK:(j + 1) * IN_BLK] = _dot(hn, w_ref[j]).astype(BF16)

    row = pl.BlockSpec((tm, D_MODEL), lambda i: (i, 0))
    return _pcall(
        body, name=name, grid=(t_rows // tm,),
        in_specs=[row, pl.BlockSpec((1, D_MODEL), lambda i: (0, 0)),
                  pl.BlockSpec((N_DEV, D_MODEL, IN_BLK), lambda i: (0, 0, 0))],
        out_specs=[row, pl.BlockSpec((tm, IN_WIDTH), lambda i: (i, 0))],
        out_shape=[jax.ShapeDtypeStruct((t_rows, D_MODEL), BF16), jax.ShapeDtypeStruct((t_rows, IN_WIDTH), BF16)],
        compiler_params=_cp(("arbitrary",), VMEM_BIG),
    )(h, norm, wing)


def mix_backward_act(dh, h, norm, dproj, w_in_full, tm, name):
    t_rows = h.shape[0]

    def body(dh_ref, h_ref, g_ref, dp_ref, w_ref, dhin_ref, dg_ref):
        dx, dg = _rms_bwd(h_ref[...], g_ref[...], _dot_nt(dp_ref[...], w_ref[...]))
        dhin_ref[...] = dh_ref[...] + dx
        _accumulate(dg_ref, dg, pl.program_id(0) == 0)

    row = pl.BlockSpec((tm, D_MODEL), lambda i: (i, 0))
    vec = pl.BlockSpec((1, D_MODEL), lambda i: (0, 0))
    return _pcall(
        body, name=name, grid=(t_rows // tm,),
        in_specs=[row, row, vec, pl.BlockSpec((tm, IN_WIDTH), lambda i: (i, 0)), _resident((D_MODEL, IN_WIDTH))],
        out_specs=[row, vec],
        out_shape=[jax.ShapeDtypeStruct((t_rows, D_MODEL), F32), jax.ShapeDtypeStruct((1, D_MODEL), F32)],
        compiler_params=_cp(("arbitrary",), VMEM_BIG),
    )(dh, h, norm, dproj, w_in_full)


def mix_backward_weights(hn, dproj, tm, name):
    t_rows = hn.shape[0]
    ni = t_rows // tm
    per_step = 2

    kc = _row_tile(tm, 688)

    def body(hn_ref, dp_ref, dw_ref, acc):
        i = pl.program_id(1)
        part = functools.reduce(lambda u, w: u + w, [_dot_tn(hn_ref[r0:r0 + kc, :], dp_ref[r0:r0 + kc, :])
                                                    for r0 in range(0, tm, kc)])
        _accumulate(acc, part, i == 0)

        @pl.when(i == ni - 1)
        def _():
            for k in range(per_step):
                dw_ref[k] = acc[:, k * IN_BLK:(k + 1) * IN_BLK].astype(BF16)

    return _pcall(
        body, name=name, grid=(N_DEV // per_step, ni),
        in_specs=[pl.BlockSpec((tm, D_MODEL), lambda j, i: (i, 0)),
                  pl.BlockSpec((tm, per_step * IN_BLK), lambda j, i: (i, j))],
        out_specs=pl.BlockSpec((per_step, D_MODEL, IN_BLK), lambda j, i: (j, 0, 0)),
        out_shape=jax.ShapeDtypeStruct((N_DEV, D_MODEL, IN_BLK), BF16),
        scratch_shapes=[pltpu.VMEM((D_MODEL, per_step * IN_BLK), F32)],
        compiler_params=_cp(("arbitrary", "arbitrary"), VMEM_BIG),
    )(hn, dproj)


GELU_C = 0.7978845608028654
GELU_K = 0.044715


def _gelu(x):
    return 0.5 * x * (1.0 + jnp.tanh(GELU_C * (x + GELU_K * (x * x * x))))


def _gelu_and_grad(x):
    th = jnp.tanh(GELU_C * (x + GELU_K * (x * x * x)))
    val = 0.5 * x * (1.0 + th)
    grad = 0.5 * (1.0 + th) + 0.5 * x * (1.0 - th * th) * (GELU_C * (1.0 + 3.0 * GELU_K * (x * x)))
    return val, grad


def merge_forward(h, yraw, attn, proj, glu_a, glu_b, w_out, tm, name):
    t_rows = h.shape[0]

    def body(h_ref, y_ref, at_ref, gate_ref, a_ref, b_ref, wo_ref, out_ref):
        y = _gelu(y_ref[...]).astype(BF16)
        ssm = _dot(y, a_ref[...]) * _sigmoid(_dot(y, b_ref[...]))
        ga = gate_ref[:, :D_MODEL].astype(F32)
        gs = gate_ref[:, D_MODEL:].astype(F32)
        merged = _sigmoid(ga) * at_ref[...].astype(F32) + _sigmoid(gs) * ssm
        out_ref[...] = h_ref[...] + _dot(merged.astype(BF16), wo_ref[...])

    row = pl.BlockSpec((tm, D_MODEL), lambda i: (i, 0))
    glu = pl.BlockSpec((SSM_WIDTH, D_MODEL), lambda i: (0, 0))
    return _pcall(
        body, name=name, grid=(t_rows // tm,),
        in_specs=[row, pl.BlockSpec((tm, SSM_WIDTH), lambda i: (i, 0)), row,
                  pl.BlockSpec((tm, 2 * D_MODEL), lambda i: (i, 1)), glu, glu,
                  pl.BlockSpec((D_MODEL, D_MODEL), lambda i: (0, 0))],
        out_specs=row, out_shape=jax.ShapeDtypeStruct((t_rows, D_MODEL), F32),
        compiler_params=_cp(("arbitrary",), VMEM_BIG),
    )(h, yraw, attn, proj, glu_a, glu_b, w_out)


def merge_backward(dh, yraw, attn, proj, glu_a, glu_b, w_out, after, tm, name):
    t_rows = dh.shape[0]

    def body(dh_ref, y_ref, at_ref, gate_ref, a_ref, b_ref, wo_ref, _,
             dat_ref, dy_ref, dgate_ref, d16_ref, mg_ref, y16_ref, dya_ref, dyb_ref):
        d16 = dh_ref[...].astype(BF16)
        d16_ref[...] = d16
        gel, dgel = _gelu_and_grad(y_ref[...].astype(F32))
        y16 = gel.astype(BF16)
        y16_ref[...] = y16
        dy = None
        for c0, cw in _col_chunks(D_MODEL):
            cols = slice(c0, c0 + cw)
            gcols = slice(D_MODEL + c0, D_MODEL + c0 + cw)
            dmerged = _dot_nt(d16, wo_ref[cols, :])
            ya = _dot(y16, a_ref[:, cols])
            sb = _sigmoid(_dot(y16, b_ref[:, cols]))
            ssm = ya * sb
            sa = _sigmoid(gate_ref[:, cols].astype(F32))
            ss = _sigmoid(gate_ref[:, gcols].astype(F32))
            attn_v = at_ref[:, cols].astype(F32)
            mg_ref[:, cols] = (sa * attn_v + ss * ssm).astype(BF16)
            dat_ref[:, cols] = (dmerged * sa).astype(BF16)
            dgate_ref[:, cols] = (dmerged * attn_v * sa * (1.0 - sa)).astype(BF16)
            dgate_ref[:, gcols] = (dmerged * ssm * ss * (1.0 - ss)).astype(BF16)
            dssm = dmerged * ss
            dya = (dssm * sb).astype(BF16)
            dyb = (dssm * ya * sb * (1.0 - sb)).astype(BF16)
            dya_ref[:, cols] = dya
            dyb_ref[:, cols] = dyb
            part = _dot_nt(dya, a_ref[:, cols]) + _dot_nt(dyb, b_ref[:, cols])
            dy = part if dy is None else dy + part
        dy_ref[...] = (dy * dgel).astype(BF16)

    row = pl.BlockSpec((tm, D_MODEL), lambda i: (i, 0))
    ssm_row = pl.BlockSpec((tm, SSM_WIDTH), lambda i: (i, 0))
    gates = pl.BlockSpec((tm, 2 * D_MODEL), lambda i: (i, 1))
    wide = jax.ShapeDtypeStruct((t_rows, D_MODEL), BF16)
    narrow = jax.ShapeDtypeStruct((t_rows, SSM_WIDTH), BF16)
    return _pcall(
        body, name=name, grid=(t_rows // tm,),
        in_specs=[row, ssm_row, row, gates, _resident((SSM_WIDTH, D_MODEL)), _resident((SSM_WIDTH, D_MODEL)),
                  _resident((D_MODEL, D_MODEL)), ANY_SPEC],
        out_specs=[row, ssm_row, gates, row, row, ssm_row, row, row],
        out_shape=[wide, narrow, jax.ShapeDtypeStruct((t_rows, IN_WIDTH), BF16), wide, wide, narrow, wide, wide],
        compiler_params=_cp(("arbitrary",), VMEM_BIG),
    )(dh, yraw, attn, proj, glu_a, glu_b, w_out, after)


def merge_backward_weights(d16, merged, y16, dya, dyb, tm, name):
    t_rows = d16.shape[0]

    def body(d_ref, mg_ref, y_ref, dya_ref, dyb_ref, dwo_ref, da_ref, db_ref):
        first = pl.program_id(0) == 0
        y16 = y_ref[...]
        _accumulate(dwo_ref, _dot_tn(mg_ref[...], d_ref[...]), first)
        _accumulate(da_ref, _dot_tn(y16, dya_ref[...]), first)
        _accumulate(db_ref, _dot_tn(y16, dyb_ref[...]), first)

    row = pl.BlockSpec((tm, D_MODEL), lambda i: (i, 0))
    ssm_row = pl.BlockSpec((tm, SSM_WIDTH), lambda i: (i, 0))
    glu = pl.BlockSpec((SSM_WIDTH, D_MODEL), lambda i: (0, 0))
    wo = pl.BlockSpec((D_MODEL, D_MODEL), lambda i: (0, 0))
    return _pcall(
        body, name=name, grid=(t_rows // tm,),
        in_specs=[row, row, ssm_row, row, row], out_specs=[wo, glu, glu],
        out_shape=[jax.ShapeDtypeStruct((D_MODEL, D_MODEL), F32), jax.ShapeDtypeStruct((SSM_WIDTH, D_MODEL), F32),
                   jax.ShapeDtypeStruct((SSM_WIDTH, D_MODEL), F32)],
        compiler_params=_cp(("arbitrary",), VMEM_BIG),
    )(d16, merged, y16, dya, dyb)


def final_loss_backward(h, target, norm, seq, tm, name):
    t_rows = h.shape[0]
    tiles_per_example = (seq + N_META) // tm

    def body(h_ref, t_ref, g_ref, dh_ref, loss_ref, dg_ref):
        i = pl.program_id(0)
        x = h_ref[...]
        g = g_ref[...]
        r = lax.rsqrt(jnp.mean(x * x, axis=-1, keepdims=True) + NORM_EPS)
        xh = x * r
        pos = lax.broadcasted_iota(jnp.int32, (tm, 1), 0) + (i % tiles_per_example) * tm
        diff = jnp.where(pos < seq, xh * g - t_ref[...], 0.0)
        part = 0.5 * jnp.sum(jnp.sum(diff * diff, axis=-1, keepdims=True), axis=0, keepdims=True) / D_MODEL
        dy = diff / D_MODEL
        t = dy * g
        dh_ref[...] = r * (t - xh * jnp.mean(t * xh, axis=-1, keepdims=True))
        _accumulate(loss_ref, jnp.broadcast_to(part, (1, LANES)), i == 0)
        _accumulate(dg_ref, jnp.sum(dy * xh, axis=0, keepdims=True), i == 0)

    row = pl.BlockSpec((tm, D_MODEL), lambda i: (i, 0))
    vec = pl.BlockSpec((1, D_MODEL), lambda i: (0, 0))
    return _pcall(
        body, name=name, grid=(t_rows // tm,),
        in_specs=[row, row, vec],
        out_specs=[row, pl.BlockSpec((1, LANES), lambda i: (0, 0)), vec],
        out_shape=[jax.ShapeDtypeStruct((t_rows, D_MODEL), F32), jax.ShapeDtypeStruct((1, LANES), F32),
                   jax.ShapeDtypeStruct((1, D_MODEL), F32)],
        compiler_params=_cp(("arbitrary",), VMEM_BIG),
    )(h, target, norm)


ATTN_SCALE = HEAD_DIM ** -0.5
STACK_HEADS = (0, 2, 1, 3)
META_PAD = LANES - N_META


def _lane_half(shape, hf):
    lane = lax.broadcasted_iota(jnp.int32, shape, 1)
    return (lane < HEAD_DIM) if hf == 0 else (lane >= HEAD_DIM)


def _kv_variants(ref, rows, kh, pad_rows=0):
    tile = kh // 2
    t = ref[rows, tile * LANES:(tile + 1) * LANES].astype(F32)
    swapped = pltpu.roll(t, HEAD_DIM, axis=1)
    at_low, at_high = (t, swapped) if kh % 2 == 0 else (swapped, t)
    lo = jnp.where(_lane_half(t.shape, 0), at_low, 0.0).astype(BF16)
    hi = jnp.where(_lane_half(t.shape, 1), at_high, 0.0).astype(BF16)
    if pad_rows:
        zeros = jnp.zeros((pad_rows, LANES), BF16)
        lo, hi = jnp.concatenate([lo, zeros], axis=0), jnp.concatenate([hi, zeros], axis=0)
    return lo, hi


def _key_tiles(ref, key_rows, kh):
    return [_kv_variants(ref, r, kh, META_PAD if i == len(key_rows) - 1 else 0) for i, r in enumerate(key_rows)]


def _to_kv_lanes(lo, hi, kh):
    lo = jnp.where(_lane_half(lo.shape, 0), lo, 0.0)
    hi = jnp.where(_lane_half(hi.shape, 1), hi, 0.0)
    if kh % 2 == 0:
        return lo + pltpu.roll(hi, HEAD_DIM, axis=1)
    return pltpu.roll(lo, HEAD_DIM, axis=1) + hi


def _stacked(ref, rows, kh):
    col = kh * 2 * LANES
    return jnp.concatenate([ref[rows, col:col + LANES], ref[rows, col + LANES:col + 2 * LANES]], axis=0)


def _sink_column(sink_ref, kh, nq):
    row = lax.broadcasted_iota(jnp.int32, (4 * nq, 1), 0)
    col = jnp.zeros((4 * nq, 1), F32)
    for quarter, g in enumerate(STACK_HEADS):
        col = jnp.where(row // nq == quarter, sink_ref[0, kh * Q_PER_KV + g], col)
    return col


def _softmax_parts(qs, key_tiles, masks, sink):
    scores = []
    for (k_lo, k_hi), mask in zip(key_tiles, masks):
        s = jnp.concatenate([_dot_nt(qs, k_lo), _dot_nt(qs, k_hi)], axis=0) * ATTN_SCALE
        scores.append(s if mask is None else jnp.where(mask, s, NEG_INF))
    m = jnp.maximum(_row_reduce(scores, jnp.maximum, jnp.max), sink)
    probs = [jnp.exp(s - m) for s in scores]
    e_sink = jnp.exp(sink - m)
    den = _row_sums(probs) + e_sink
    return probs, 1.0 / den, e_sink


def _row_reduce(tiles, combine, reduce):
    chunks = [t[:, c:c + LANES] for t in tiles for c in range(0, t.shape[-1], LANES)]
    return reduce(functools.reduce(combine, chunks), axis=-1, keepdims=True)


def _row_sums(tiles):
    return _row_reduce(tiles, lambda u, w: u + w, jnp.sum)


def _band_mask(nq, first):
    keys = BLOCK if first else 2 * BLOCK
    qi = lax.broadcasted_iota(jnp.int32, (4 * nq, keys), 0) % nq
    kj = lax.broadcasted_iota(jnp.int32, (4 * nq, keys), 1)
    if first:
        return kj <= qi
    return jnp.logical_and(kj > qi, kj <= qi + BLOCK)


def _meta_mask(nq, causal):
    qi = lax.broadcasted_iota(jnp.int32, (4 * nq, LANES), 0) % nq
    kj = lax.broadcasted_iota(jnp.int32, (4 * nq, LANES), 1)
    return jnp.logical_and(kj < N_META, kj <= qi) if causal else kj < N_META


def _attention_schedule(seq, queries, carry):
    meta_rows = pl.ds(seq, N_META)
    meta_ok = _meta_mask(BLOCK, False)
    carry = queries(pl.ds(0, BLOCK), BLOCK, [pl.ds(0, BLOCK), meta_rows], [_band_mask(BLOCK, True), meta_ok], carry)

    def block(n, c):
        r0 = pl.multiple_of(n * BLOCK, BLOCK)
        p0 = pl.multiple_of((n - 1) * BLOCK, BLOCK)
        return queries(pl.ds(r0, BLOCK), BLOCK, [pl.ds(p0, 2 * BLOCK), meta_rows], [_band_mask(BLOCK, False), meta_ok], c)

    carry = lax.fori_loop(1, seq // BLOCK, block, carry)
    return queries(meta_rows, N_META, [meta_rows], [_meta_mask(N_META, True)], carry)


def attention_forward(proj3, sinks, seq, name):
    n_b, n_l, _ = proj3.shape

    def body(sink_ref, q_ref, k_ref, v_ref, o_ref):
        def queries(q_rows, nq, key_rows, masks, carry):
            for kh in range(N_KV_HEADS):
                ks = _key_tiles(k_ref, key_rows, kh)
                vs = _key_tiles(v_ref, key_rows, kh)
                qs = _stacked(q_ref, q_rows, kh)
                probs, inv, _ = _softmax_parts(qs, ks, masks, _sink_column(sink_ref, kh, nq))
                probs = [p.astype(BF16) for p in probs]
                o_lo = functools.reduce(lambda u, w: u + w, [_dot(p[:2 * nq], v_lo) for p, (v_lo, _) in zip(probs, vs)])
                o_hi = functools.reduce(lambda u, w: u + w, [_dot(p[2 * nq:], v_hi) for p, (_, v_hi) in zip(probs, vs)])
                out = (o_lo * inv[:2 * nq] + o_hi * inv[2 * nq:]).astype(BF16)
                col = kh * 2 * LANES
                o_ref[q_rows, col:col + LANES] = out[:nq]
                o_ref[q_rows, col + LANES:col + 2 * LANES] = out[nq:]
            return carry

        _attention_schedule(seq, queries, 0)

    return _pcall(
        body, name=name, grid=(n_b,),
        in_specs=[pl.BlockSpec(memory_space=pltpu.SMEM),
                  pl.BlockSpec((None, n_l, D_MODEL), lambda b: (b, 0, 0)),
                  pl.BlockSpec((None, n_l, KV_WIDTH), lambda b: (b, 0, D_MODEL // KV_WIDTH)),
                  pl.BlockSpec((None, n_l, KV_WIDTH), lambda b: (b, 0, D_MODEL // KV_WIDTH + 1))],
        out_specs=pl.BlockSpec((None, n_l, D_MODEL), lambda b: (b, 0, 0)),
        out_shape=jax.ShapeDtypeStruct((n_b, n_l, D_MODEL), BF16),
        compiler_params=_cp(("arbitrary",), VMEM_BIG),
    )(sinks, proj3, proj3, proj3)


def attention_backward(proj3, dattn3, dproj3, sinks, after, seq, name):
    n_b, n_l, _ = proj3.shape
    qkv_width = D_MODEL + 2 * KV_WIDTH

    def body(sink_ref, q_ref, k_ref, v_ref, do_ref, _, __, dqkv_ref, dsink_ref, dk_ref, dv_ref):
        dk_ref[...] = jnp.zeros_like(dk_ref)
        dv_ref[...] = jnp.zeros_like(dv_ref)
        sub = lax.broadcasted_iota(jnp.int32, (SUBLANES, LANES), 0)
        lane = lax.broadcasted_iota(jnp.int32, (SUBLANES, LANES), 1)

        def queries(q_rows, nq, key_rows, masks, dsink):
            for kh in range(N_KV_HEADS):
                ks = _key_tiles(k_ref, key_rows, kh)
                vs = _key_tiles(v_ref, key_rows, kh)
                qs = _stacked(q_ref, q_rows, kh)
                dos = _stacked(do_ref, q_rows, kh)
                probs, inv, e_sink = _softmax_parts(qs, ks, masks, _sink_column(sink_ref, kh, nq))
                probs = [p * inv for p in probs]
                dps = [jnp.concatenate([_dot_nt(dos, v_lo), _dot_nt(dos, v_hi)], axis=0) for v_lo, v_hi in vs]
                delta = _row_sums([p * dp for p, dp in zip(probs, dps)])
                d_sink = -(e_sink * inv) * delta
                for quarter, g in enumerate(STACK_HEADS):
                    d_here = jnp.sum(d_sink[quarter * nq:(quarter + 1) * nq], axis=0, keepdims=True)
                    dsink = dsink + jnp.where(jnp.logical_and(sub == 0, lane == kh * Q_PER_KV + g), d_here, 0.0)
                dq = None
                tile = slice((kh // 2) * LANES, (kh // 2 + 1) * LANES)
                for r, p, dp, (k_lo, k_hi) in zip(key_rows, probs, dps, ks):
                    ds = (p * (dp - delta)).astype(BF16)
                    p16 = p.astype(BF16)
                    dq_x = _dot(ds[:2 * nq], k_lo) + _dot(ds[2 * nq:], k_hi)
                    dq = dq_x if dq is None else dq + dq_x
                    d_k = _to_kv_lanes(_dot_tn(ds[:2 * nq], qs), _dot_tn(ds[2 * nq:], qs), kh) * ATTN_SCALE
                    d_v = _to_kv_lanes(_dot_tn(p16[:2 * nq], dos), _dot_tn(p16[2 * nq:], dos), kh)
                    n_keys = r.size
                    dk_ref[r, tile] += d_k[:n_keys]
                    dv_ref[r, tile] += d_v[:n_keys]
                dq = (dq * ATTN_SCALE).astype(BF16)
                col = kh * 2 * LANES
                dqkv_ref[q_rows, col:col + LANES] = dq[:nq]
                dqkv_ref[q_rows, col + LANES:col + 2 * LANES] = dq[nq:]
            return dsink

        dsink_ref[...] = _attention_schedule(seq, queries, jnp.zeros((SUBLANES, LANES), F32))
        dqkv_ref[:, D_MODEL:D_MODEL + KV_WIDTH] = dk_ref[...].astype(BF16)
        dqkv_ref[:, D_MODEL + KV_WIDTH:] = dv_ref[...].astype(BF16)

    return _pcall(
        body, name=name, grid=(n_b,),
        in_specs=[pl.BlockSpec(memory_space=pltpu.SMEM),
                  pl.BlockSpec((None, n_l, D_MODEL), lambda b: (b, 0, 0)),
                  pl.BlockSpec((None, n_l, KV_WIDTH), lambda b: (b, 0, D_MODEL // KV_WIDTH)),
                  pl.BlockSpec((None, n_l, KV_WIDTH), lambda b: (b, 0, D_MODEL // KV_WIDTH + 1)),
                  pl.BlockSpec((None, n_l, D_MODEL), lambda b: (b, 0, 0)),
                  ANY_SPEC, ANY_SPEC],
        out_specs=[pl.BlockSpec((None, n_l, qkv_width), lambda b: (b, 0, 0)),
                   pl.BlockSpec((None, SUBLANES, LANES), lambda b: (b, 0, 0))],
        out_shape=[jax.ShapeDtypeStruct(dproj3.shape, BF16), jax.ShapeDtypeStruct((n_b, SUBLANES, LANES), F32)],
        scratch_shapes=[pltpu.VMEM((n_l, KV_WIDTH), F32), pltpu.VMEM((n_l, KV_WIDTH), F32)],
        input_output_aliases={5: 0},
        compiler_params=_cp(("arbitrary",), VMEM_BIG),
    )(sinks, proj3, proj3, proj3, dattn3, dproj3, after)


TAB_ROWS = 8
SCAN_UNROLL = 4


def _cmul(ar, ai, br, bi):
    return ar * br - ai * bi, ar * bi + ai * br


def _discretise(ar, ai, ls):
    step = jnp.exp(ls)
    mag = jnp.exp(ar * step)
    ang = ai * step
    cos, sin = jnp.cos(ang), jnp.sin(ang)
    lr, li = mag * cos, mag * sin
    den = ar * ar + ai * ai
    nr, ni = lr - 1.0, li
    cr = (nr * ar + ni * ai) / den
    ci = (ni * ar - nr * ai) / den
    return step, mag, lr, li, den, nr, ni, cr, ci


def _scan_tables(lr, li, reverse):
    n = lr.shape[-1]
    pw = [(lr, li)]
    for _ in range(SUBLANES - 1):
        pw.append(_cmul(pw[-1][0], pw[-1][1], lr, li))
    row = lax.broadcasted_iota(jnp.int32, (SUBLANES, n), 0)
    out = []
    for d in (1, 2, 4):
        ok = (row + d <= SUBLANES - 1) if reverse else (row >= d)
        out += [jnp.where(ok, pw[d - 1][0], 0.0), jnp.where(ok, pw[d - 1][1], 0.0)]
    cr = jnp.zeros((SUBLANES, n), F32)
    ci = jnp.zeros((SUBLANES, n), F32)
    for r in range(SUBLANES):
        e = (SUBLANES - r) if reverse else (r + 1)
        cr = jnp.where(row == r, pw[e - 1][0], cr)
        ci = jnp.where(row == r, pw[e - 1][1], ci)
    return out + [cr, ci]


def ssm_prepare(ar, ai, ls, br_t, bi_t, name):
    def body(ar_ref, ai_ref, ls_ref, br_ref, bi_ref, bbr_ref, bbi_ref, tf_ref, tr_ref):
        _, _, lr, li, _, _, _, cr, ci = _discretise(ar_ref[...], ai_ref[...], ls_ref[...])
        br, bi = br_ref[...], bi_ref[...]
        bbr_ref[...] = cr * br - ci * bi
        bbi_ref[...] = cr * bi + ci * br
        for k, t in enumerate(_scan_tables(lr, li, False)):
            tf_ref[k] = t
        for k, t in enumerate(_scan_tables(lr, -li, True)):
            tr_ref[k] = t

    return _pcall(
        body, name=name,
        out_shape=[jax.ShapeDtypeStruct((SSM_GROUP, N_STATES), F32), jax.ShapeDtypeStruct((SSM_GROUP, N_STATES), F32),
                   jax.ShapeDtypeStruct((TAB_ROWS, SUBLANES, N_STATES), F32),
                   jax.ShapeDtypeStruct((TAB_ROWS, SUBLANES, N_STATES), F32)],
    )(ar, ai, ls, br_t, bi_t)


def ssm_param_backward(ar, ai, ls, br_t, bi_t, dlr_p, dli_p, dbbr, dbbi, group_sum, name):
    def body(ar_ref, ai_ref, ls_ref, br_ref, bi_ref, dlr_ref, dli_ref, dbbr_ref, dbbi_ref, gs_ref,
             dar_ref, dai_ref, dls_ref, dbr_ref, dbi_ref):
        ar, ai = ar_ref[...], ai_ref[...]
        step, mag, lr, li, den, nr, ni, cr, ci = _discretise(ar, ai, ls_ref[...])
        br, bi, dbbr_v, dbbi_v = br_ref[...], bi_ref[...], dbbr_ref[...], dbbi_ref[...]
        dbr_ref[...] = cr * dbbr_v + ci * dbbi_v
        dbi_ref[...] = cr * dbbi_v - ci * dbbr_v
        dcr = jnp.sum(dbbr_v * br + dbbi_v * bi, axis=0, keepdims=True)
        dci = jnp.sum(dbbi_v * br - dbbr_v * bi, axis=0, keepdims=True)
        dnr = (dcr * ar - dci * ai) / den
        dni = (dcr * ai + dci * ar) / den
        dden = -(cr * dcr + ci * dci) / den
        dar = (dcr * nr + dci * ni) / den + dden * 2.0 * ar
        dai = (dcr * ni - dci * nr) / den + dden * 2.0 * ai
        dlr = jnp.sum(dlr_ref[...], axis=0, keepdims=True) + dnr
        dli = jnp.sum(dli_ref[...], axis=0, keepdims=True) + dni
        dmag = (dlr * lr + dli * li) / mag
        dang = dli * lr - dlr * li
        dar_ref[...] = dar + dmag * mag * step
        dai_ref[...] = dai + dang * step
        dstep = dmag * mag * ar + dang * ai
        dls_ref[...] = jnp.dot(dstep * step, gs_ref[...], preferred_element_type=F32, precision=lax.Precision.HIGHEST)

    vec = jax.ShapeDtypeStruct((1, N_STATES), F32)
    mat = jax.ShapeDtypeStruct((SSM_GROUP, N_STATES), F32)
    return _pcall(body, name=name, out_shape=[vec, vec, jax.ShapeDtypeStruct((1, LANES), F32), mat, mat])(
        ar, ai, ls, br_t, bi_t, dlr_p, dli_p, dbbr, dbbi, group_sum)


def _scan_rows(a, b, tabs, carry, reverse):
    for k, d in enumerate((1, 2, 4)):
        shift = SUBLANES - d if reverse else d
        sr, si = pltpu.roll(a, shift, axis=0), pltpu.roll(b, shift, axis=0)
        pr, pi = _cmul(tabs[2 * k], tabs[2 * k + 1], sr, si)
        a, b = a + pr, b + pi
    pr, pi = _cmul(tabs[6], tabs[7], carry[0], carry[1])
    return a + pr, b + pi


def _time_groups(seq, reverse):
    meta = [seq + SUBLANES * g for g in range(N_META // SUBLANES)]
    return meta[::-1] if reverse else meta


def ssm_forward_scan(proj3, b_comb, tabf, c_comb, dvec, seq, name):
    n_b, n_l, _ = proj3.shape
    u_blk = (D_MODEL + 2 * KV_WIDTH) // LANES

    def body(u_ref, b_ref, tab_ref, c_ref, d_ref, x_ref, y_ref, bu, xs):
        j = pl.program_id(1)
        u = u_ref[...]
        bu[...] = _dot(u, b_ref[...])
        tabs = [tab_ref[k] for k in range(TAB_ROWS)]

        def group(r0, carry):
            rows = pl.ds(r0, SUBLANES)
            a, b = _scan_rows(bu[rows, :SCAN_COLS], bu[rows, SCAN_COLS:], tabs, carry, False)
            xs[rows, :SCAN_COLS] = a
            xs[rows, SCAN_COLS:] = b
            return (jnp.broadcast_to(a[SUBLANES - 1:, :], a.shape), jnp.broadcast_to(b[SUBLANES - 1:, :], b.shape))

        zero = jnp.zeros((SUBLANES, SCAN_COLS), F32)
        carry = (zero, zero)
        for r0 in _time_groups(seq, False):
            carry = group(r0, carry)
        span = SCAN_UNROLL * SUBLANES

        def groups(t, c):
            for k in range(SCAN_UNROLL):
                c = group(pl.multiple_of(t * span, span) + k * SUBLANES, c)
            return c

        lax.fori_loop(0, seq // span, groups, carry)
        x16 = xs[...].astype(BF16)
        x_ref[...] = x16
        contrib = _dot(x16, c_ref[...])

        @pl.when(j % 2 == 0)
        def _():
            y_ref[...] = contrib + d_ref[...] * u.astype(F32)

        @pl.when(j % 2 == 1)
        def _():
            y_ref[...] += contrib

    return _pcall(
        body, name=name, grid=(n_b, N_SCAN_BLK),
        in_specs=[pl.BlockSpec((None, n_l, LANES), lambda b, j: (b, 0, u_blk + j // 2)),
                  pl.BlockSpec((None, LANES, 2 * SCAN_COLS), lambda b, j: (j, 0, 0)),
                  pl.BlockSpec((TAB_ROWS, SUBLANES, SCAN_COLS), lambda b, j: (0, 0, j)),
                  pl.BlockSpec((None, 2 * SCAN_COLS, LANES), lambda b, j: (j, 0, 0)),
                  pl.BlockSpec((1, LANES), lambda b, j: (0, j // 2))],
        out_specs=[pl.BlockSpec((None, n_l, 2 * SCAN_COLS), lambda b, j: (b, 0, j)),
                   pl.BlockSpec((None, n_l, LANES), lambda b, j: (b, 0, j // 2))],
        out_shape=[jax.ShapeDtypeStruct((n_b, n_l, 2 * N_STATES), BF16),
                   jax.ShapeDtypeStruct((n_b, n_l, SSM_WIDTH), F32)],
        scratch_shapes=[pltpu.VMEM((n_l, 2 * SCAN_COLS), F32)] * 2,
        compiler_params=_cp(("arbitrary", "arbitrary"), VMEM_BIG),
    )(proj3, b_comb, tabf, c_comb, dvec)


def ssm_backward_scan(dyraw3, xs3, dproj3, c_comb_t, tabr, b_comb_t, dvec, seq, name):
    n_b, n_l, _ = xs3.shape
    u_blk = (D_MODEL + 2 * KV_WIDTH) // LANES

    def body(dy_ref, x_ref, _, c_ref, tab_ref, b_ref, d_ref, du_ref, g_ref, dlr_ref, dli_ref, dx, gs, xs, du_acc):
        j = pl.program_id(1)
        dy = dy_ref[...]
        dx[...] = _dot(dy, c_ref[...])
        xs[...] = x_ref[...].astype(F32)
        tabs = [tab_ref[k] for k in range(TAB_ROWS)]
        last_row = lax.broadcasted_iota(jnp.int32, (SUBLANES, SCAN_COLS), 0) == SUBLANES - 1

        def group(r0, state):
            cr, ci, acc_r, acc_i = state
            rows = pl.ds(r0, SUBLANES)
            a, b = _scan_rows(dx[rows, :SCAN_COLS], dx[rows, SCAN_COLS:], tabs, (cr, ci), True)
            gs[rows, :SCAN_COLS] = a
            gs[rows, SCAN_COLS:] = b
            na = jnp.where(last_row, cr, pltpu.roll(a, SUBLANES - 1, axis=0))
            nb = jnp.where(last_row, ci, pltpu.roll(b, SUBLANES - 1, axis=0))
            xa, xb = xs[rows, :SCAN_COLS], xs[rows, SCAN_COLS:]
            return (jnp.broadcast_to(a[:1, :], a.shape), jnp.broadcast_to(b[:1, :], b.shape),
                    acc_r + na * xa + nb * xb, acc_i + nb * xa - na * xb)

        zero = jnp.zeros((SUBLANES, SCAN_COLS), F32)
        span = SCAN_UNROLL * SUBLANES
        n_spans = seq // span

        def groups(t, s):
            for k in reversed(range(SCAN_UNROLL)):
                s = group(pl.multiple_of((n_spans - 1 - t) * span, span) + k * SUBLANES, s)
            return s

        state = lax.fori_loop(0, n_spans, groups, (zero, zero, zero, zero))
        for r0 in _time_groups(seq, True):
            state = group(r0, state)
        dlr_ref[...] = state[2]
        dli_ref[...] = state[3]
        g16 = gs[...].astype(BF16)
        g_ref[...] = g16
        contrib = _dot(g16, b_ref[...])

        @pl.when(j % 2 == 0)
        def _():
            du_acc[...] = contrib + d_ref[...] * dy.astype(F32)

        @pl.when(j % 2 == 1)
        def _():
            du_ref[...] = (du_acc[...] + contrib).astype(BF16)

    state_blk = pl.BlockSpec((None, n_l, 2 * SCAN_COLS), lambda b, j: (b, 0, j))
    dl_blk = pl.BlockSpec((None, SUBLANES, SCAN_COLS), lambda b, j: (b, 0, j))
    return _pcall(
        body, name=name, grid=(n_b, N_SCAN_BLK),
        in_specs=[pl.BlockSpec((None, n_l, LANES), lambda b, j: (b, 0, j // 2)), state_blk,
                  pl.BlockSpec(memory_space=pl.ANY),
                  pl.BlockSpec((None, LANES, 2 * SCAN_COLS), lambda b, j: (j, 0, 0)),
                  pl.BlockSpec((TAB_ROWS, SUBLANES, SCAN_COLS), lambda b, j: (0, 0, j)),
                  pl.BlockSpec((None, 2 * SCAN_COLS, LANES), lambda b, j: (j, 0, 0)),
                  pl.BlockSpec((1, LANES), lambda b, j: (0, j // 2))],
        out_specs=[pl.BlockSpec((None, n_l, LANES), lambda b, j: (b, 0, u_blk + j // 2)), state_blk, dl_blk, dl_blk],
        out_shape=[jax.ShapeDtypeStruct(dproj3.shape, BF16), jax.ShapeDtypeStruct((n_b, n_l, 2 * N_STATES), BF16),
                   jax.ShapeDtypeStruct((n_b, SUBLANES, N_STATES), F32), jax.ShapeDtypeStruct((n_b, SUBLANES, N_STATES), F32)],
        scratch_shapes=[pltpu.VMEM((n_l, 2 * SCAN_COLS), F32)] * 3 + [pltpu.VMEM((n_l, LANES), F32)],
        input_output_aliases={2: 0},
        compiler_params=_cp(("arbitrary", "arbitrary"), VMEM_BIG),
    )(dyraw3, xs3, dproj3, c_comb_t, tabr, b_comb_t, dvec)


def ssm_param_grads(proj, gs, xs, dyraw, tm, name):
    t_rows = proj.shape[0]
    ni = t_rows // tm
    u_blk = (D_MODEL + 2 * KV_WIDTH) // LANES
    width = 2 * SCAN_COLS

    def body(u_ref, g_ref, x_ref, dy_ref, db_ref, dc_ref, dd_ref):
        cb, i = pl.program_id(0), pl.program_id(1)
        u, dy = u_ref[...], dy_ref[...]
        _accumulate(db_ref, _dot_tn(u, g_ref[...]), i == 0)
        _accumulate(dc_ref, _dot_tn(x_ref[...], dy), i == 0)

        @pl.when(cb % 2 == 0)
        def _():
            _accumulate(dd_ref, jnp.sum(dy.astype(F32) * u.astype(F32), axis=0, keepdims=True), i == 0)

    return _pcall(
        body, name=name, grid=(N_SCAN_BLK, ni),
        in_specs=[pl.BlockSpec((tm, LANES), lambda cb, i: (i, u_blk + cb // 2)),
                  pl.BlockSpec((tm, width), lambda cb, i: (i, cb)),
                  pl.BlockSpec((tm, width), lambda cb, i: (i, cb)),
                  pl.BlockSpec((tm, LANES), lambda cb, i: (i, cb // 2))],
        out_specs=[pl.BlockSpec((None, LANES, width), lambda cb, i: (cb, 0, 0)),
                   pl.BlockSpec((None, width, LANES), lambda cb, i: (cb, 0, 0)),
                   pl.BlockSpec((1, LANES), lambda cb, i: (0, cb // 2))],
        out_shape=[jax.ShapeDtypeStruct((N_SCAN_BLK, LANES, width), F32),
                   jax.ShapeDtypeStruct((N_SCAN_BLK, width, LANES), F32), jax.ShapeDtypeStruct((1, SSM_WIDTH), F32)],
        compiler_params=_cp(("arbitrary", "arbitrary"), VMEM_BIG),
    )(proj, gs, xs, dyraw)


def sum_leading(x, name):
    def body(x_ref, o_ref):
        acc = x_ref[0]
        for k in range(1, x.shape[0]):
            acc = acc + x_ref[k]
        o_ref[...] = acc

    return _pcall(body, name=name, out_shape=jax.ShapeDtypeStruct(x.shape[1:], x.dtype))(x)


WEIGHTS = ['meta_tokens', 'ffn1_norm', 'ffn1_w1', 'ffn1_w3', 'ffn1_w2', 'mix_norm', 'w_in', 'attn_sinks', 'ssm_a_re',
           'ssm_a_im', 'ssm_log_step', 'ssm_b_re', 'ssm_b_im', 'ssm_c_re', 'ssm_c_im', 'ssm_d', 'ssm_glu_a', 'ssm_glu_b',
           'w_out', 'ffn2_norm', 'ffn2_w1', 'ffn2_w3', 'ffn2_w2', 'final_norm']
SHARDED = ['ffn1_w1', 'ffn1_w3', 'ffn1_w2', 'ffn2_w1', 'ffn2_w3', 'ffn2_w2', 'w_in', 'ssm_glu_a', 'ssm_glu_b', 'w_out']
REPLICATED = ['ffn1_norm', 'mix_norm', 'ffn2_norm', 'final_norm', 'attn_sinks', 'ssm_a_re', 'ssm_a_im', 'ssm_log_step',
              'ssm_b_re', 'ssm_b_im', 'ssm_c_re', 'ssm_c_im', 'ssm_d']
PACK_COLS = 1024


def _pack(arrays):
    parts = []
    for a in arrays:
        flat = a.reshape(-1)
        chunk = SUBLANES * PACK_COLS
        padded = -(-flat.shape[0] // chunk) * chunk
        parts.append(jnp.pad(flat, (0, padded - flat.shape[0])).reshape(-1, PACK_COLS))
    return jnp.concatenate(parts, axis=0)


def _unpack(packed, shapes):
    out, row = [], 0
    for shape in shapes:
        size = 1
        for s in shape:
            size *= s
        chunk = SUBLANES * PACK_COLS
        rows = -(-size // chunk) * SUBLANES
        out.append(packed[row:row + rows].reshape(-1)[:size].reshape(shape))
        row += rows
    return out


def kernel(x, meta_tokens, ffn1_norm, ffn1_w1, ffn1_w3, ffn1_w2, mix_norm, w_in, attn_sinks, ssm_a_re, ssm_a_im, ssm_log_step, ssm_b_re, ssm_b_im, ssm_c_re, ssm_c_im, ssm_d, ssm_glu_a, ssm_glu_b, w_out, ffn2_norm, ffn2_w1, ffn2_w3, ffn2_w2, final_norm, loss_target, m_meta_tokens, m_ffn1_norm, m_ffn1_w1, m_ffn1_w3, m_ffn1_w2, m_mix_norm, m_w_in, m_attn_sinks, m_ssm_a_re, m_ssm_a_im, m_ssm_log_step, m_ssm_b_re, m_ssm_b_im, m_ssm_c_re, m_ssm_c_im, m_ssm_d, m_ssm_glu_a, m_ssm_glu_b, m_w_out, m_ffn2_norm, m_ffn2_w1, m_ffn2_w3, m_ffn2_w2, m_final_norm, v_meta_tokens, v_ffn1_norm, v_ffn1_w1, v_ffn1_w3, v_ffn1_w2, v_mix_norm, v_w_in, v_attn_sinks, v_ssm_a_re, v_ssm_a_im, v_ssm_log_step, v_ssm_b_re, v_ssm_b_im, v_ssm_c_re, v_ssm_c_im, v_ssm_d, v_ssm_glu_a, v_ssm_glu_b, v_w_out, v_ffn2_norm, v_ffn2_w1, v_ffn2_w3, v_ffn2_w2, v_final_norm):
    given = dict(locals())
    w = {n: given[n] for n in WEIGHTS}
    m = {n: given["m_" + n] for n in WEIGHTS}
    v = {n: given["v_" + n] for n in WEIGHTS}

    n_b, seq, _ = x.shape
    n_l = seq + N_META
    t_rows = n_b * n_l
    tm = _row_tile(n_l, 688)
    px, py, pc = _my_place()
    me = 4 * px + 2 * py + pc

    glu = jnp.stack([ssm_glu_a[0], ssm_glu_b[0]]).astype(BF16)
    ffn_names = ['ffn1_w1', 'ffn1_w3', 'ffn1_w2', 'ffn2_w1', 'ffn2_w3', 'ffn2_w2']

    def hidden_on_rows(n, t):
        return t[0] if n.endswith('w2') else t[0].T

    def hidden_on_rows_back(n, t):
        return t[None] if n.endswith('w2') else t.T[None]

    me_idx = jnp.reshape(me, (1,)).astype(jnp.int32)
    first_names, later_names = ffn_names[:3], ffn_names[3:]
    *first, metag = all_gather_list(
        [hidden_on_rows(n, w[n]).astype(BF16) for n in first_names] + [meta_tokens], "ag_first")
    win_send, win_recv, win_shard, win_land, win_token = exchange_start(
        [w_in[0].astype(BF16)], first[0], True, "ag_w_in_start")
    later_shards = [hidden_on_rows(n, w[n]).astype(BF16) for n in later_names] + [glu, w_out[0].astype(BF16)]
    ag_send, ag_recv, later_shards, later_lands, ag_token = exchange_start(later_shards, win_token, True, "ag_later_start")
    full = {n: g.reshape(D_FF, D_MODEL) for n, g in zip(first_names, first)}
    meta_full = metag.transpose(1, 0, 2).reshape(N_META, D_MODEL)

    h0 = jnp.concatenate([x, jnp.broadcast_to(meta_full[None], (n_b, N_META, D_MODEL))], axis=1).reshape(t_rows, D_MODEL)
    target = jnp.concatenate([loss_target, jnp.zeros((n_b, N_META, D_MODEL), F32)], axis=1).reshape(t_rows, D_MODEL)
    final_g = final_norm.reshape(1, D_MODEL)

    ar = ssm_a_re.reshape(1, N_STATES)
    ai = ssm_a_im.reshape(1, N_STATES)
    ls = jnp.repeat(ssm_log_step.reshape(SSM_GROUPS), SSM_STATE).reshape(1, N_STATES)
    br_t = ssm_b_re[0].transpose(2, 0, 1).reshape(SSM_GROUP, N_STATES)
    bi_t = ssm_b_im[0].transpose(2, 0, 1).reshape(SSM_GROUP, N_STATES)
    bbr, bbi, tabf, tabr = ssm_prepare(ar, ai, ls, br_t, bi_t, "ssm_prepare")
    bbr_g = bbr.reshape(SSM_GROUP, SSM_GROUPS, SSM_STATE).transpose(1, 0, 2)
    bbi_g = bbi.reshape(SSM_GROUP, SSM_GROUPS, SSM_STATE).transpose(1, 0, 2)
    groups_per_blk = SCAN_COLS // SSM_STATE
    half = ((jnp.arange(N_SCAN_BLK) % 2)[:, None] == jnp.arange(2)[None, :]).astype(F32)
    eye = jnp.eye(groups_per_blk, dtype=F32)

    def scan_blocks(re_g, im_g):
        def one(t):
            t = t.reshape(N_SCAN_BLK, groups_per_blk, SSM_GROUP, SSM_STATE)
            t = t[:, :, :, None, :] * eye[None, :, None, :, None]
            t = t.reshape(N_SCAN_BLK, LANES // 2, SCAN_COLS)
            return (t[:, None] * half[:, :, None, None]).reshape(N_SCAN_BLK, LANES, SCAN_COLS)
        return jnp.concatenate([one(re_g), one(im_g)], axis=-1).astype(BF16)

    b_comb = scan_blocks(bbr_g, bbi_g)
    c_comb_t = scan_blocks(ssm_c_re[0], -ssm_c_im[0])
    b_comb_t, c_comb = b_comb.transpose(0, 2, 1), c_comb_t.transpose(0, 2, 1)

    ffn1_w = (full['ffn1_w1'], full['ffn1_w3'], full['ffn1_w2'])
    h1, hn1, a1, b1 = ffn_forward(h0, ffn1_norm, *ffn1_w, ag_token, tm, "ffn1_fwd")
    (wing,) = exchange_wait(win_send, win_recv, win_shard, win_land, h1, True, "ag_w_in_wait")
    wing = lax.dynamic_update_slice_in_dim(wing, win_shard[0][None], me, axis=0)
    hnm, proj = mix_forward(h1, mix_norm, wing, tm, "mix_fwd")
    proj3 = proj.reshape(n_b, n_l, IN_WIDTH)
    attn3 = attention_forward(proj3, attn_sinks, seq, "attn_fwd")
    attn = attn3.reshape(t_rows, D_MODEL)
    xs3, yraw3 = ssm_forward_scan(proj3, b_comb, tabf, c_comb, ssm_d, seq, "ssm_fwd")
    yraw = yraw3.reshape(t_rows, SSM_WIDTH)
    later = exchange_wait(ag_send, ag_recv, later_shards, later_lands, yraw3, True, "ag_later_wait")
    later = [lax.dynamic_update_slice_in_dim(z, s[None], me, axis=0) for z, s in zip(later, later_shards)]
    for n, g in zip(later_names, later):
        full[n] = g.reshape(D_FF, D_MODEL)
    ffn2_w = (full['ffn2_w1'], full['ffn2_w3'], full['ffn2_w2'])
    glug, wog = later[len(later_names):]
    glu_a = glug[:, 0].transpose(1, 0, 2).reshape(SSM_WIDTH, D_MODEL)
    glu_b = glug[:, 1].transpose(1, 0, 2).reshape(SSM_WIDTH, D_MODEL)
    w_out_full = wog.reshape(D_MODEL, D_MODEL)
    h2 = merge_forward(h1, yraw, attn, proj, glu_a, glu_b, w_out_full, tm, "merge_fwd")
    h3, hn2, a2, b2 = ffn_forward(h2, ffn2_norm, *ffn2_w, ag_token, tm, "ffn2_fwd")
    dh3, loss_part, g_final = final_loss_backward(h3, target, final_g, seq, tm, "loss_bwd")
    loss = lax.psum(loss_part[0, 0], ("x", "y", "c"))

    def blocked_ffn(d_w1t, d_w3t, d_w2):
        return tuple(t.reshape(N_DEV, FF_BLK, D_MODEL) for t in (d_w1t, d_w3t, d_w2))

    def blocked_cols(full_grad):
        r = full_grad.shape[0]
        return full_grad.reshape(r, N_DEV, full_grad.shape[1] // N_DEV).transpose(1, 0, 2).astype(BF16)

    early = {}

    def start_reduce(names, tag):
        srcs = [dw[n] for n in names]
        send, recv, srcs, lands, token = exchange_start(srcs, srcs[0], False, "rs_" + tag + "_start")
        early[tag] = (names, send, recv, srcs, lands)
        return token

    dw = {}
    da2, db2, dh3_half = ffn_backward_hidden(dh3, a2, b2, ffn2_w[2], g_final, tm, "ffn2_bwd_hid")
    dw['ffn2_w1'], dw['ffn2_w3'], dw['ffn2_w2'] = blocked_ffn(
        *ffn_backward_weights(hn2, dh3_half, a2, b2, da2, db2, n_l, FF_BWD_COLS, "ffn2_bwd_w"))
    token = start_reduce(later_names, "ffn2")
    dh2, g_ffn2_norm = ffn_backward_input(dh3, h2, ffn2_norm, da2, db2, ffn2_w[0], ffn2_w[1], token, tm, "ffn2_bwd_in")
    dattn, dyraw, dproj, *for_weights = merge_backward(dh2, yraw, attn, proj, glu_a, glu_b, w_out_full, token, tm,
                                                       "merge_bwd")
    d_wo, d_ga, d_gb = merge_backward_weights(*for_weights, tm, "merge_bwd_w")
    dw['ssm_glu_a'] = blocked_cols(d_ga)
    dw['ssm_glu_b'] = blocked_cols(d_gb)
    dw['w_out'] = d_wo.reshape(N_DEV, D_MODEL // N_DEV, D_MODEL).astype(BF16)
    token = start_reduce(['ssm_glu_a', 'ssm_glu_b', 'w_out'], "mix")
    dproj3 = dproj.reshape(n_b, n_l, IN_WIDTH)
    dproj3, dsink_p = attention_backward(proj3, dattn.reshape(n_b, n_l, D_MODEL), dproj3, attn_sinks, token, seq,
                                         "attn_bwd")
    dproj3, gs3, dlr_p, dli_p = ssm_backward_scan(
        dyraw.reshape(n_b, n_l, SSM_WIDTH), xs3, dproj3, c_comb_t, tabr, b_comb_t, ssm_d, seq, "ssm_bwd")
    dproj = dproj3.reshape(t_rows, IN_WIDTH)
    d_bd, d_cd, g_d = ssm_param_grads(proj, gs3.reshape(t_rows, 2 * N_STATES), xs3.reshape(t_rows, 2 * N_STATES),
                                      dyraw, n_l, "ssm_bwd_w")
    w_in_full = wing.transpose(1, 0, 2).reshape(D_MODEL, IN_WIDTH)
    dh1, g_mix_norm = mix_backward_act(dh2, h1, mix_norm, dproj, w_in_full, tm, "mix_bwd_act")
    dw['w_in'] = mix_backward_weights(hnm, dproj, n_l, "mix_bwd_w")
    token = start_reduce(['w_in'], "w_in")

    def group_blocks(part, channels_first):
        if channels_first:
            t = jnp.sum(part.reshape(N_SCAN_BLK, 2, LANES // 2, SCAN_COLS) * half[:, :, None, None], axis=1)
            t = t.reshape(N_SCAN_BLK, groups_per_blk, SSM_GROUP, groups_per_blk, SSM_STATE)
            t = jnp.sum(t * eye[None, :, None, :, None], axis=3)
            return t.reshape(SSM_GROUPS, SSM_GROUP, SSM_STATE)
        t = jnp.sum(part.reshape(N_SCAN_BLK, SCAN_COLS, 2, LANES // 2) * half[:, None, :, None], axis=2)
        t = t.reshape(N_SCAN_BLK, groups_per_blk, SSM_STATE, groups_per_blk, SSM_GROUP)
        t = jnp.sum(t * eye[None, :, None, :, None], axis=3)
        return t.reshape(SSM_GROUPS, SSM_STATE, SSM_GROUP).transpose(0, 2, 1)

    dbbr = group_blocks(d_bd[:, :, :SCAN_COLS], True).transpose(1, 0, 2).reshape(SSM_GROUP, N_STATES)
    dbbi = group_blocks(d_bd[:, :, SCAN_COLS:], True).transpose(1, 0, 2).reshape(SSM_GROUP, N_STATES)
    g_c_re = group_blocks(d_cd[:, :SCAN_COLS, :], False)[None]
    g_c_im = -group_blocks(d_cd[:, SCAN_COLS:, :], False)[None]
    group_sum = (jnp.arange(N_STATES)[:, None] // SSM_STATE == jnp.arange(LANES)[None, :]).astype(F32)
    g_ar, g_ai, g_ls, g_br, g_bi = ssm_param_backward(
        ar, ai, ls, br_t, bi_t, dlr_p.reshape(n_b * SUBLANES, N_STATES), dli_p.reshape(n_b * SUBLANES, N_STATES),
        dbbr, dbbi, group_sum, "ssm_bwd_params")
    g_sinks = sum_leading(dsink_p, "sink_sum")[0:1, :N_KV_HEADS * Q_PER_KV]

    small = {
        'mix_norm': g_mix_norm, 'ffn2_norm': g_ffn2_norm, 'final_norm': g_final.reshape(D_MODEL),
        'attn_sinks': g_sinks, 'ssm_a_re': g_ar.reshape(1, SSM_GROUPS, SSM_STATE), 'ssm_a_im': g_ai.reshape(1, SSM_GROUPS, SSM_STATE),
        'ssm_log_step': g_ls[:, :SSM_GROUPS],
        'ssm_b_re': g_br.reshape(SSM_GROUP, SSM_GROUPS, SSM_STATE).transpose(1, 2, 0)[None],
        'ssm_b_im': g_bi.reshape(SSM_GROUP, SSM_GROUPS, SSM_STATE).transpose(1, 2, 0)[None],
        'ssm_c_re': g_c_re, 'ssm_c_im': g_c_im, 'ssm_d': g_d,
    }
    early_small = [n for n in REPLICATED if n in small]
    sg_send, sg_recv, sg_src, sg_land, token = exchange_start(
        [_pack([small[n] for n in early_small])], token, True, "ag_small_start")
    da1, db1, dh1_half = ffn_backward_hidden(dh1, a1, b1, ffn1_w[2], token, tm, "ffn1_bwd_hid")
    dw['ffn1_w1'], dw['ffn1_w3'], dw['ffn1_w2'] = blocked_ffn(
        *ffn_backward_weights(hn1, dh1_half, a1, b1, da1, db1, n_l, FF_BWD_COLS, "ffn1_bwd_w"))
    token = start_reduce(first_names, "ffn1")
    dh0, g_ffn1_norm = ffn_backward_input(dh1, h0, ffn1_norm, da1, db1, ffn1_w[0], ffn1_w[1], token, tm, "ffn1_bwd_in")
    dh0_3 = dh0.reshape(n_b, n_l, D_MODEL)
    grad_x = dh0_3[:, :seq]
    g_meta = sum_leading(dh0_3[:, seq:], "meta_sum")

    grads, deltas, new_m, new_v = {}, {}, {}, {}
    zeros_meta = jnp.zeros((N_META, D_MODEL), F32)
    (late_parts,) = all_gather_list([_pack([g_ffn1_norm, g_meta])], "ag_small_late")
    (early_parts,) = exchange_wait(sg_send, sg_recv, sg_src, sg_land, late_parts, True, "ag_small_wait")
    early_parts = lax.dynamic_update_slice_in_dim(early_parts, sg_src[0][None], me, axis=0)

    def small_update(parts, names, extra, tag):
        pack_of = lambda d: _pack([d[n] for n in names] + extra)
        packed = adamw_small(parts, pack_of(w), pack_of(m), pack_of(v), "adamw_small_" + tag)
        unpacked = [_unpack(p, [w[n].shape for n in names] + [e.shape for e in extra]) for p in packed]
        for k, n in enumerate(names):
            grads[n], deltas[n], new_m[n], new_v[n] = (u[k] for u in unpacked)
        return packed, unpacked

    small_update(early_parts, early_small, [], "early")
    packed_out, unpacked = small_update(late_parts, ['ffn1_norm'], [zeros_meta], "late")
    g_meta_full = unpacked[0][-1]
    grads['meta_tokens'] = lax.dynamic_index_in_dim(
        g_meta_full.reshape(N_META, N_DEV, D_MODEL // N_DEV), me, axis=1, keepdims=False)
    deltas['meta_tokens'], new_m['meta_tokens'], new_v['meta_tokens'] = adamw_plain(
        grads['meta_tokens'], w['meta_tokens'], m['meta_tokens'], v['meta_tokens'], "adamw_meta")

    def views(n):
        if n in ffn_names:
            return functools.partial(hidden_on_rows, n), functools.partial(hidden_on_rows_back, n)
        return (lambda t: t[0]), (lambda t: t[None])

    previous = packed_out[0]
    for tag, (names, send, recv, srcs, lands) in early.items():
        lands = exchange_wait(send, recv, srcs, lands, previous, False, "rs_" + tag + "_wait")
        for n, g, land in zip(names, srcs, lands):
            two_d, back = views(n)
            out = adamw_exchanged(me_idx, g, land, two_d(w[n]), two_d(m[n]), two_d(v[n]), "adamw_" + n)
            grads[n], deltas[n], new_m[n], new_v[n] = (back(o) for o in out)
            previous = out[1]

    return (loss, grad_x, *[grads[n] for n in WEIGHTS], *[deltas[n] for n in WEIGHTS],
            *[new_m[n] for n in WEIGHTS], *[new_v[n] for n in WEIGHTS])
```

```python
import functools

import jax
import jax.numpy as jnp
from jax import lax
from jax.experimental import pallas as pl
from jax.experimental.pallas import tpu as pltpu

F32 = jnp.float32
BF16 = jnp.bfloat16
MESH = pl.DeviceIdType.MESH

N_DEV = 8
D_MODEL = 1024
N_META = 16
HEAD_DIM = 64
N_KV_HEADS = 4
Q_PER_KV = 4
BLOCK = 128
KV_WIDTH = N_KV_HEADS * HEAD_DIM
SSM_GROUP = 16
SSM_WIDTH = 512
SSM_GROUPS = 32
SSM_STATE = 64
N_STATES = SSM_GROUPS * SSM_STATE
D_FF = 2816
FF_BLK = D_FF // N_DEV
IN_WIDTH = 4096
IN_BLK = IN_WIDTH // N_DEV
NORM_EPS = 1e-6
NEG_INF = -1e30
SCAN_COLS = 256
N_SCAN_BLK = N_STATES // SCAN_COLS
SUBLANES = 8
LANES = 128
MXU_WIDTH = 256
FF_BWD_COLS = MXU_WIDTH

ADAM_LR = 0.001
ADAM_B1 = 0.9
ADAM_B2 = 0.999
ADAM_EPS = 1e-08
ADAM_WD = 0.01
ADAM_STEP = 10

VMEM_BIG = 56 * 1024 * 1024


def _cp(sem=None, vmem=None):
    kw = {}
    if sem is not None:
        kw["dimension_semantics"] = sem
    if vmem is not None:
        kw["vmem_limit_bytes"] = vmem
    return pltpu.CompilerParams(**kw)


def _pcall(body, **kw):
    return pl.pallas_call(body, **kw)


def _dot(a, b):
    return jnp.dot(a, b, preferred_element_type=F32)


def _dot_nt(a, b):
    return lax.dot_general(a, b, (((1,), (1,)), ((), ())), preferred_element_type=F32)


def _dot_tn(a, b):
    return lax.dot_general(a, b, (((0,), (0,)), ((), ())), preferred_element_type=F32)


def _sigmoid(x):
    return 1.0 / (1.0 + jnp.exp(-x))


def _row_tile(rows, cap):
    best = None
    for t in range(16, min(rows, cap) + 1, 16):
        if rows % t == 0:
            best = t
    assert best is not None, rows
    return best


def _my_place():
    return lax.axis_index("x"), lax.axis_index("y"), lax.axis_index("c")


def all_gather_list(shards, after, name):
    n = len(shards)

    def body(*refs):
        ins, outs = refs[:n], refs[n + 1:2 * n + 1]
        send_sems, recv_sems, local_sems = refs[2 * n + 1:]
        x, y, c = _my_place()
        me, sibling = (x, y, c), (x, y, 1 - c)
        chips = [(1 - x, y), (x, 1 - y), (1 - x, 1 - y)]

        def blk(a, px, py, pc):
            return outs[a].at[4 * px + 2 * py + pc]

        def copy(a, k, block, to, src=None):
            return pltpu.make_async_remote_copy(
                src_ref=blk(a, *block) if src is None else src, dst_ref=blk(a, *block),
                send_sem=send_sems.at[a * 7 + k], recv_sem=recv_sems.at[a * 7 + k],
                device_id=to, device_id_type=MESH)

        mine = [pltpu.make_async_copy(ins[a], blk(a, *me), local_sems.at[a]) for a in range(n)]
        for cp in mine:
            cp.start()
        first = []
        for a in range(n):
            first.append(copy(a, 0, me, sibling, src=ins[a]))
            first += [copy(a, 1 + j, me, (*chip, c), src=ins[a]) for j, chip in enumerate(chips)]
        for cp in first:
            cp.start()
        passed = []
        for j, chip in enumerate(chips):
            for a in range(n):
                copy(a, 1 + j, (*chip, c), me).wait_recv()
                cp = copy(a, 4 + j, (*chip, c), sibling)
                cp.start()
                passed.append(cp)
        for a in range(n):
            copy(a, 0, sibling, me).wait_recv()
            for j, chip in enumerate(chips):
                copy(a, 4 + j, (*chip, 1 - c), me).wait_recv()
        for cp in first + passed:
            cp.wait_send()
        for cp in mine:
            cp.wait()

    any_spec = pl.BlockSpec(memory_space=pl.ANY)
    return _pcall(
        body, name=name,
        out_shape=[jax.ShapeDtypeStruct((N_DEV,) + s.shape, s.dtype) for s in shards],
        in_specs=[any_spec] * (n + 1), out_specs=[any_spec] * n,
        scratch_shapes=[pltpu.SemaphoreType.DMA((7 * n,)), pltpu.SemaphoreType.DMA((7 * n,)),
                        pltpu.SemaphoreType.DMA((n,))],
    )(*shards, after)


HBM_SPEC = pl.BlockSpec(memory_space=pltpu.HBM)
SEM_SPEC = pl.BlockSpec(memory_space=pltpu.SEMAPHORE)
N_PEERS = N_DEV - 1


def _related(k):
    x, y, c = _my_place()
    px = 1 - x if k & 4 else x
    py = 1 - y if k & 2 else y
    pc = 1 - c if k & 1 else c
    return (px, py, pc), 4 * px + 2 * py + pc


def _exchange_copies(srcs, lands, send_sems, recv_sems, gather):
    x, y, c = _my_place()
    me = 4 * x + 2 * y + c
    copies = []
    for a, (src, land) in enumerate(zip(srcs, lands)):
        for k in range(1, N_DEV):
            peer, d = _related(k)
            copies.append(pltpu.make_async_remote_copy(
                src_ref=src if gather else src.at[d], dst_ref=land.at[me] if gather else land.at[k],
                send_sem=send_sems.at[a * N_PEERS + k - 1], recv_sem=recv_sems.at[a * N_PEERS + k - 1],
                device_id=peer, device_id_type=MESH))
    return copies


def exchange_start(srcs, after, gather, name):
    n = len(srcs)
    land_shapes = [((N_DEV,) + s.shape) if gather else s.shape for s in srcs]

    def body(*refs):
        send_sems, recv_sems = refs[2 * n + 1], refs[2 * n + 2]
        for cp in _exchange_copies(refs[:n], refs[n:2 * n], send_sems, recv_sems, gather):
            cp.start()
        token = refs[-1]
        token[...] = jnp.zeros_like(token)

    sems = pltpu.SemaphoreType.DMA((n * N_PEERS,))
    lands = [pltpu.with_memory_space_constraint(lax.empty(shape, s.dtype), pltpu.HBM) for shape, s in zip(land_shapes, srcs)]
    out = _pcall(
        body, name=name,
        out_shape=(sems, sems, *[pltpu.HBM(s.shape, s.dtype) for s in srcs],
                   *[pltpu.HBM(shape, s.dtype) for shape, s in zip(land_shapes, srcs)],
                   jax.ShapeDtypeStruct((SUBLANES, LANES), F32)),
        in_specs=[HBM_SPEC] * (2 * n) + [pl.BlockSpec(memory_space=pl.ANY)],
        out_specs=(SEM_SPEC, SEM_SPEC, *[HBM_SPEC] * (2 * n), pl.BlockSpec(memory_space=pltpu.VMEM)),
        input_output_aliases={i: 2 + i for i in range(2 * n)},
        compiler_params=pltpu.CompilerParams(has_side_effects=pltpu.SideEffectType.DATAFLOW_SIDE_EFFECTING),
    )(*[pltpu.with_memory_space_constraint(s, pltpu.HBM) for s in srcs], *lands, after)
    return out[0], out[1], list(out[2:2 + n]), list(out[2 + n:2 + 2 * n]), out[-1]


def exchange_wait(send_sems, recv_sems, srcs, lands, after, gather, name):
    n = len(srcs)

    def body(*refs):
        for cp in _exchange_copies(refs[:n], refs[n:2 * n], refs[2 * n], refs[2 * n + 1], gather):
            cp.wait_send()
            cp.wait_recv()

    out = _pcall(
        body, name=name,
        out_shape=(*[pltpu.HBM(s.shape, s.dtype) for s in srcs], *[pltpu.HBM(z.shape, z.dtype) for z in lands]),
        in_specs=[HBM_SPEC] * (2 * n) + [SEM_SPEC, SEM_SPEC, pl.BlockSpec(memory_space=pl.ANY)],
        out_specs=tuple([HBM_SPEC] * (2 * n)),
        input_output_aliases={i: i for i in range(2 * n)},
        compiler_params=pltpu.CompilerParams(has_side_effects=pltpu.SideEffectType.DATAFLOW_SIDE_EFFECTING),
    )(*srcs, *lands, send_sems, recv_sems, after)
    return list(out[n:])


def adamw_exchanged(me, g, land, w, m, v, name):
    rows, cols = w.shape
    tr = _row_tile(rows, 256)

    def body(me_ref, g_ref, land_ref, w_ref, m_ref, v_ref, go_ref, d_ref, mo_ref, vo_ref):
        grad = g_ref[...].astype(F32)
        for k in range(1, N_DEV):
            grad = grad + land_ref[k].astype(F32)
        delta, m_new, v_new = _adam_math(w_ref[...], grad, m_ref[...], v_ref[...])
        go_ref[...] = grad
        d_ref[...] = delta
        mo_ref[...] = m_new
        vo_ref[...] = v_new

    tile = pl.BlockSpec((tr, cols), lambda r, ix: (r, 0))
    out = jax.ShapeDtypeStruct((rows, cols), F32)
    return _pcall(
        body, name=name, out_shape=[out] * 4,
        grid_spec=pltpu.PrefetchScalarGridSpec(
            num_scalar_prefetch=1, grid=(rows // tr,),
            in_specs=[pl.BlockSpec((None, tr, cols), lambda r, ix: (ix[0], r, 0)),
                      pl.BlockSpec((N_DEV, tr, cols), lambda r, ix: (0, r, 0)), tile, tile, tile],
            out_specs=[tile] * 4),
        compiler_params=_cp(("arbitrary",)),
    )(me, g, land, w, m, v)


def _adam_math(w, g, m, v):
    m = ADAM_B1 * m + (1.0 - ADAM_B1) * g
    v = ADAM_B2 * v + (1.0 - ADAM_B2) * (g * g)
    m_hat = m / (1.0 - ADAM_B1 ** ADAM_STEP)
    v_hat = v / (1.0 - ADAM_B2 ** ADAM_STEP)
    delta = -ADAM_LR * (m_hat / (jnp.sqrt(v_hat) + ADAM_EPS) + ADAM_WD * w)
    return delta, m, v


def adamw_small(parts, w, m, v, name):
    _, rows, cols = parts.shape

    def body(p_ref, w_ref, m_ref, v_ref, go_ref, d_ref, mo_ref, vo_ref):
        grad = p_ref[0]
        for k in range(1, N_DEV):
            grad = grad + p_ref[k]
        delta, m_new, v_new = _adam_math(w_ref[...], grad, m_ref[...], v_ref[...])
        go_ref[...] = grad
        d_ref[...] = delta
        mo_ref[...] = m_new
        vo_ref[...] = v_new

    out = jax.ShapeDtypeStruct((rows, cols), F32)
    return _pcall(body, name=name, out_shape=[out] * 4, compiler_params=_cp(vmem=VMEM_BIG))(parts, w, m, v)


def adamw_plain(g, w, m, v, name):
    def body(g_ref, w_ref, m_ref, v_ref, d_ref, mo_ref, vo_ref):
        delta, m_new, v_new = _adam_math(w_ref[...], g_ref[...], m_ref[...], v_ref[...])
        d_ref[...] = delta
        mo_ref[...] = m_new
        vo_ref[...] = v_new

    out = jax.ShapeDtypeStruct(w.shape, F32)
    return _pcall(body, name=name, out_shape=[out] * 3)(g, w, m, v)


def _rms_fwd(x, g):
    r = lax.rsqrt(jnp.mean(x * x, axis=-1, keepdims=True) + NORM_EPS)
    return x * r * g


def _rms_bwd(x, g, dy):
    r = lax.rsqrt(jnp.mean(x * x, axis=-1, keepdims=True) + NORM_EPS)
    xh = x * r
    t = dy * g
    dx = r * (t - xh * jnp.mean(t * xh, axis=-1, keepdims=True))
    return dx, jnp.sum(dy * xh, axis=0, keepdims=True)


def _accumulate(ref, val, first):
    @pl.when(first)
    def _():
        ref[...] = val

    @pl.when(jnp.logical_not(first))
    def _():
        ref[...] += val


def _col_chunks(width):
    return [(c0, min(MXU_WIDTH, width - c0)) for c0 in range(0, width, MXU_WIDTH)]


ANY_SPEC = pl.BlockSpec(memory_space=pl.ANY)


def ffn_forward(h, norm, w1, w3, w2, after, tm, name):
    t_rows = h.shape[0]

    def body(h_ref, g_ref, w1_ref, w3_ref, w2_ref, _, out_ref, hn_ref, a_ref, b_ref, hid_ref):
        hn = _rms_fwd(h_ref[...], g_ref[...]).astype(BF16)
        hn_ref[...] = hn
        for c0, cw in _col_chunks(D_FF):
            a = _dot_nt(hn, w1_ref[c0:c0 + cw, :])
            b = _dot_nt(hn, w3_ref[c0:c0 + cw, :])
            a_ref[:, c0:c0 + cw] = a.astype(BF16)
            b_ref[:, c0:c0 + cw] = b.astype(BF16)
            hid_ref[:, c0:c0 + cw] = (a * _sigmoid(a) * b).astype(BF16)
        out_ref[...] = h_ref[...] + 0.5 * _dot(hid_ref[...], w2_ref[...])

    row = pl.BlockSpec((tm, D_MODEL), lambda i: (i, 0))
    hid_blk = pl.BlockSpec((tm, D_FF), lambda i: (i, 0))
    weight = _resident((D_FF, D_MODEL))
    return _pcall(
        body, name=name, grid=(t_rows // tm,),
        in_specs=[row, pl.BlockSpec((1, D_MODEL), lambda i: (0, 0)), weight, weight, weight, ANY_SPEC],
        out_specs=[row, row, hid_blk, hid_blk],
        out_shape=[jax.ShapeDtypeStruct((t_rows, D_MODEL), F32), jax.ShapeDtypeStruct((t_rows, D_MODEL), BF16),
                   jax.ShapeDtypeStruct((t_rows, D_FF), BF16), jax.ShapeDtypeStruct((t_rows, D_FF), BF16)],
        scratch_shapes=[pltpu.VMEM((tm, D_FF), BF16)],
        compiler_params=_cp(("arbitrary",), VMEM_BIG),
    )(h, norm, w1, w3, w2, after)


def _resident(shape):
    return pl.BlockSpec(shape, lambda *_: (0,) * len(shape), pipeline_mode=pl.Buffered(1))


def ffn_backward_hidden(dh, a, b, w2, after, tm, name):
    t_rows = dh.shape[0]

    def body(dh_ref, a_ref, b_ref, w2_ref, _, da_ref, db_ref, dhb_ref):
        dhb = (0.5 * dh_ref[...]).astype(BF16)
        dhb_ref[...] = dhb
        for c0, cw in _col_chunks(D_FF):
            dhid = _dot_nt(dhb, w2_ref[c0:c0 + cw, :])
            av = a_ref[:, c0:c0 + cw].astype(F32)
            bv = b_ref[:, c0:c0 + cw].astype(F32)
            s = _sigmoid(av)
            da_ref[:, c0:c0 + cw] = (dhid * bv * (s * (1.0 + av * (1.0 - s)))).astype(BF16)
            db_ref[:, c0:c0 + cw] = (dhid * (av * s)).astype(BF16)

    hid = pl.BlockSpec((tm, D_FF), lambda i: (i, 0))
    row = pl.BlockSpec((tm, D_MODEL), lambda i: (i, 0))
    return _pcall(
        body, name=name, grid=(t_rows // tm,),
        in_specs=[row, hid, hid, _resident((D_FF, D_MODEL)), ANY_SPEC],
        out_specs=[hid, hid, row],
        out_shape=[jax.ShapeDtypeStruct((t_rows, D_FF), BF16), jax.ShapeDtypeStruct((t_rows, D_FF), BF16),
                   jax.ShapeDtypeStruct((t_rows, D_MODEL), BF16)],
        compiler_params=_cp(("arbitrary",), VMEM_BIG),
    )(dh, a, b, w2, after)


def ffn_backward_input(dh, h, norm, da, db, w1, w3, after, tm, name):
    t_rows = h.shape[0]

    def body(dh_ref, h_ref, g_ref, da_ref, db_ref, w1_ref, w3_ref, _, dhin_ref, dg_ref):
        dhn = _dot(da_ref[...], w1_ref[...]) + _dot(db_ref[...], w3_ref[...])
        dx, dg = _rms_bwd(h_ref[...], g_ref[...], dhn)
        dhin_ref[...] = dh_ref[...] + dx
        _accumulate(dg_ref, dg, pl.program_id(0) == 0)

    row = pl.BlockSpec((tm, D_MODEL), lambda i: (i, 0))
    vec = pl.BlockSpec((1, D_MODEL), lambda i: (0, 0))
    hid = pl.BlockSpec((tm, D_FF), lambda i: (i, 0))
    return _pcall(
        body, name=name, grid=(t_rows // tm,),
        in_specs=[row, row, vec, hid, hid, _resident((D_FF, D_MODEL)), _resident((D_FF, D_MODEL)), ANY_SPEC],
        out_specs=[row, vec],
        out_shape=[jax.ShapeDtypeStruct((t_rows, D_MODEL), F32), jax.ShapeDtypeStruct((1, D_MODEL), F32)],
        compiler_params=_cp(("arbitrary",), VMEM_BIG),
    )(dh, h, norm, da, db, w1, w3, after)


def ffn_backward_weights(hn, dh, a, b, da, db, tm, tn, name):
    t_rows = hn.shape[0]
    ni = t_rows // tm
    kc = _row_tile(tm, 688)

    def body(hn_ref, dh_ref, a_ref, b_ref, da_ref, db_ref, dw1_ref, dw3_ref, dw2_ref, acc1, acc3, acc2):
        i = pl.program_id(1)
        parts = None
        for r0 in range(0, tm, kc):
            rows = slice(r0, r0 + kc)
            hn_v = hn_ref[rows, :]
            av = a_ref[rows, :].astype(F32)
            hid = (av * _sigmoid(av) * b_ref[rows, :].astype(F32)).astype(BF16)
            new = (_dot_tn(hn_v, da_ref[rows, :]), _dot_tn(hn_v, db_ref[rows, :]), _dot_tn(hid, dh_ref[rows, :]))
            parts = new if parts is None else tuple(p + q for p, q in zip(parts, new))
        _accumulate(acc1, parts[0], i == 0)
        _accumulate(acc3, parts[1], i == 0)
        _accumulate(acc2, parts[2], i == 0)

        @pl.when(i == ni - 1)
        def _():
            dw1_ref[...] = acc1[...].T.astype(BF16)
            dw3_ref[...] = acc3[...].T.astype(BF16)
            dw2_ref[...] = acc2[...].astype(BF16)

    row = pl.BlockSpec((tm, D_MODEL), lambda j, i: (i, 0))
    hid_blk = pl.BlockSpec((tm, tn), lambda j, i: (i, j))
    w_row = pl.BlockSpec((tn, D_MODEL), lambda j, i: (j, 0))
    out = jax.ShapeDtypeStruct((D_FF, D_MODEL), BF16)
    return _pcall(
        body, name=name, grid=(D_FF // tn, ni),
        in_specs=[row, row, hid_blk, hid_blk, hid_blk, hid_blk],
        out_specs=[w_row, w_row, w_row], out_shape=[out, out, out],
        scratch_shapes=[pltpu.VMEM((D_MODEL, tn), F32), pltpu.VMEM((D_MODEL, tn), F32), pltpu.VMEM((tn, D_MODEL), F32)],
        compiler_params=_cp(("arbitrary", "arbitrary"), VMEM_BIG),
    )(hn, dh, a, b, da, db)


def mix_forward(h, norm, wing, tm, name):
    t_rows = h.shape[0]

    def body(h_ref, g_ref, w_ref, hn_ref, p_ref):
        hn = _rms_fwd(h_ref[...], g_ref[...]).astype(BF16)
        hn_ref[...] = hn
        for j in range(N_DEV):
            p_ref[:, j * IN_BLK:(j + 1) * IN_BLK] = _dot(hn, w_ref[j]).astype(BF16)

    row = pl.BlockSpec((tm, D_MODEL), lambda i: (i, 0))
    return _pcall(
        body, name=name, grid=(t_rows // tm,),
        in_specs=[row, pl.BlockSpec((1, D_MODEL), lambda i: (0, 0)),
                  pl.BlockSpec((N_DEV, D_MODEL, IN_BLK), lambda i: (0, 0, 0))],
        out_specs=[row, pl.BlockSpec((tm, IN_WIDTH), lambda i: (i, 0))],
        out_shape=[jax.ShapeDtypeStruct((t_rows, D_MODEL), BF16), jax.ShapeDtypeStruct((t_rows, IN_WIDTH), BF16)],
        compiler_params=_cp(("arbitrary",), VMEM_BIG),
    )(h, norm, wing)


def mix_backward_act(dh, h, norm, dproj, w_in_full, tm, name):
    t_rows = h.shape[0]

    def body(dh_ref, h_ref, g_ref, dp_ref, w_ref, dhin_ref, dg_ref):
        dx, dg = _rms_bwd(h_ref[...], g_ref[...], _dot_nt(dp_ref[...], w_ref[...]))
        dhin_ref[...] = dh_ref[...] + dx
        _accumulate(dg_ref, dg, pl.program_id(0) == 0)

    row = pl.BlockSpec((tm, D_MODEL), lambda i: (i, 0))
    vec = pl.BlockSpec((1, D_MODEL), lambda i: (0, 0))
    return _pcall(
        body, name=name, grid=(t_rows // tm,),
        in_specs=[row, row, vec, pl.BlockSpec((tm, IN_WIDTH), lambda i: (i, 0)), _resident((D_MODEL, IN_WIDTH))],
        out_specs=[row, vec],
        out_shape=[jax.ShapeDtypeStruct((t_rows, D_MODEL), F32), jax.ShapeDtypeStruct((1, D_MODEL), F32)],
        compiler_params=_cp(("arbitrary",), VMEM_BIG),
    )(dh, h, norm, dproj, w_in_full)


def mix_backward_weights(hn, dproj, tm, name):
    t_rows = hn.shape[0]
    ni = t_rows // tm
    per_step = 2

    kc = _row_tile(tm, 688)

    def body(hn_ref, dp_ref, dw_ref, acc):
        i = pl.program_id(1)
        part = functools.reduce(lambda u, w: u + w, [_dot_tn(hn_ref[r0:r0 + kc, :], dp_ref[r0:r0 + kc, :])
                                                    for r0 in range(0, tm, kc)])
        _accumulate(acc, part, i == 0)

        @pl.when(i == ni - 1)
        def _():
            for k in range(per_step):
                dw_ref[k] = acc[:, k * IN_BLK:(k + 1) * IN_BLK].astype(BF16)

    return _pcall(
        body, name=name, grid=(N_DEV // per_step, ni),
        in_specs=[pl.BlockSpec((tm, D_MODEL), lambda j, i: (i, 0)),
                  pl.BlockSpec((tm, per_step * IN_BLK), lambda j, i: (i, j))],
        out_specs=pl.BlockSpec((per_step, D_MODEL, IN_BLK), lambda j, i: (j, 0, 0)),
        out_shape=jax.ShapeDtypeStruct((N_DEV, D_MODEL, IN_BLK), BF16),
        scratch_shapes=[pltpu.VMEM((D_MODEL, per_step * IN_BLK), F32)],
        compiler_params=_cp(("arbitrary", "arbitrary"), VMEM_BIG),
    )(hn, dproj)


GELU_C = 0.7978845608028654
GELU_K = 0.044715


def _gelu(x):
    return 0.5 * x * (1.0 + jnp.tanh(GELU_C * (x + GELU_K * (x * x * x))))


def _gelu_and_grad(x):
    th = jnp.tanh(GELU_C * (x + GELU_K * (x * x * x)))
    val = 0.5 * x * (1.0 + th)
    grad = 0.5 * (1.0 + th) + 0.5 * x * (1.0 - th * th) * (GELU_C * (1.0 + 3.0 * GELU_K * (x * x)))
    return val, grad


def merge_forward(h, yraw, attn, proj, glu_a, glu_b, w_out, tm, name):
    t_rows = h.shape[0]

    def body(h_ref, y_ref, at_ref, gate_ref, a_ref, b_ref, wo_ref, out_ref):
        y = _gelu(y_ref[...]).astype(BF16)
        ssm = _dot(y, a_ref[...]) * _sigmoid(_dot(y, b_ref[...]))
        ga = gate_ref[:, :D_MODEL].astype(F32)
        gs = gate_ref[:, D_MODEL:].astype(F32)
        merged = _sigmoid(ga) * at_ref[...].astype(F32) + _sigmoid(gs) * ssm
        out_ref[...] = h_ref[...] + _dot(merged.astype(BF16), wo_ref[...])

    row = pl.BlockSpec((tm, D_MODEL), lambda i: (i, 0))
    glu = pl.BlockSpec((SSM_WIDTH, D_MODEL), lambda i: (0, 0))
    return _pcall(
        body, name=name, grid=(t_rows // tm,),
        in_specs=[row, pl.BlockSpec((tm, SSM_WIDTH), lambda i: (i, 0)), row,
                  pl.BlockSpec((tm, 2 * D_MODEL), lambda i: (i, 1)), glu, glu,
                  pl.BlockSpec((D_MODEL, D_MODEL), lambda i: (0, 0))],
        out_specs=row, out_shape=jax.ShapeDtypeStruct((t_rows, D_MODEL), F32),
        compiler_params=_cp(("arbitrary",), VMEM_BIG),
    )(h, yraw, attn, proj, glu_a, glu_b, w_out)


def merge_backward(dh, yraw, attn, proj, glu_a, glu_b, w_out, after, tm, name):
    t_rows = dh.shape[0]

    def body(dh_ref, y_ref, at_ref, gate_ref, a_ref, b_ref, wo_ref, _,
             dat_ref, dy_ref, dgate_ref, d16_ref, mg_ref, y16_ref, dya_ref, dyb_ref):
        d16 = dh_ref[...].astype(BF16)
        d16_ref[...] = d16
        gel, dgel = _gelu_and_grad(y_ref[...].astype(F32))
        y16 = gel.astype(BF16)
        y16_ref[...] = y16
        dy = None
        for c0, cw in _col_chunks(D_MODEL):
            cols = slice(c0, c0 + cw)
            gcols = slice(D_MODEL + c0, D_MODEL + c0 + cw)
            dmerged = _dot_nt(d16, wo_ref[cols, :])
            ya = _dot(y16, a_ref[:, cols])
            sb = _sigmoid(_dot(y16, b_ref[:, cols]))
            ssm = ya * sb
            sa = _sigmoid(gate_ref[:, cols].astype(F32))
            ss = _sigmoid(gate_ref[:, gcols].astype(F32))
            attn_v = at_ref[:, cols].astype(F32)
            mg_ref[:, cols] = (sa * attn_v + ss * ssm).astype(BF16)
            dat_ref[:, cols] = (dmerged * sa).astype(BF16)
            dgate_ref[:, cols] = (dmerged * attn_v * sa * (1.0 - sa)).astype(BF16)
            dgate_ref[:, gcols] = (dmerged * ssm * ss * (1.0 - ss)).astype(BF16)
            dssm = dmerged * ss
            dya = (dssm * sb).astype(BF16)
            dyb = (dssm * ya * sb * (1.0 - sb)).astype(BF16)
            dya_ref[:, cols] = dya
            dyb_ref[:, cols] = dyb
            part = _dot_nt(dya, a_ref[:, cols]) + _dot_nt(dyb, b_ref[:, cols])
            dy = part if dy is None else dy + part
        dy_ref[...] = (dy * dgel).astype(BF16)

    row = pl.BlockSpec((tm, D_MODEL), lambda i: (i, 0))
    ssm_row = pl.BlockSpec((tm, SSM_WIDTH), lambda i: (i, 0))
    gates = pl.BlockSpec((tm, 2 * D_MODEL), lambda i: (i, 1))
    wide = jax.ShapeDtypeStruct((t_rows, D_MODEL), BF16)
    narrow = jax.ShapeDtypeStruct((t_rows, SSM_WIDTH), BF16)
    return _pcall(
        body, name=name, grid=(t_rows // tm,),
        in_specs=[row, ssm_row, row, gates, _resident((SSM_WIDTH, D_MODEL)), _resident((SSM_WIDTH, D_MODEL)),
                  _resident((D_MODEL, D_MODEL)), ANY_SPEC],
        out_specs=[row, ssm_row, gates, row, row, ssm_row, row, row],
        out_shape=[wide, narrow, jax.ShapeDtypeStruct((t_rows, IN_WIDTH), BF16), wide, wide, narrow, wide, wide],
        compiler_params=_cp(("arbitrary",), VMEM_BIG),
    )(dh, yraw, attn, proj, glu_a, glu_b, w_out, after)


def merge_backward_weights(d16, merged, y16, dya, dyb, tm, name):
    t_rows = d16.shape[0]

    def body(d_ref, mg_ref, y_ref, dya_ref, dyb_ref, dwo_ref, da_ref, db_ref):
        first = pl.program_id(0) == 0
        y16 = y_ref[...]
        _accumulate(dwo_ref, _dot_tn(mg_ref[...], d_ref[...]), first)
        _accumulate(da_ref, _dot_tn(y16, dya_ref[...]), first)
        _accumulate(db_ref, _dot_tn(y16, dyb_ref[...]), first)

    row = pl.BlockSpec((tm, D_MODEL), lambda i: (i, 0))
    ssm_row = pl.BlockSpec((tm, SSM_WIDTH), lambda i: (i, 0))
    glu = pl.BlockSpec((SSM_WIDTH, D_MODEL), lambda i: (0, 0))
    wo = pl.BlockSpec((D_MODEL, D_MODEL), lambda i: (0, 0))
    return _pcall(
        body, name=name, grid=(t_rows // tm,),
        in_specs=[row, row, ssm_row, row, row], out_specs=[wo, glu, glu],
        out_shape=[jax.ShapeDtypeStruct((D_MODEL, D_MODEL), F32), jax.ShapeDtypeStruct((SSM_WIDTH, D_MODEL), F32),
                   jax.ShapeDtypeStruct((SSM_WIDTH, D_MODEL), F32)],
        compiler_params=_cp(("arbitrary",), VMEM_BIG),
    )(d16, merged, y16, dya, dyb)


def final_loss_backward(h, target, norm, seq, tm, name):
    t_rows = h.shape[0]
    tiles_per_example = (seq + N_META) // tm

    def body(h_ref, t_ref, g_ref, dh_ref, loss_ref, dg_ref):
        i = pl.program_id(0)
        x = h_ref[...]
        g = g_ref[...]
        r = lax.rsqrt(jnp.mean(x * x, axis=-1, keepdims=True) + NORM_EPS)
        xh = x * r
        pos = lax.broadcasted_iota(jnp.int32, (tm, 1), 0) + (i % tiles_per_example) * tm
        diff = jnp.where(pos < seq, xh * g - t_ref[...], 0.0)
        part = 0.5 * jnp.sum(jnp.sum(diff * diff, axis=-1, keepdims=True), axis=0, keepdims=True) / D_MODEL
        dy = diff / D_MODEL
        t = dy * g
        dh_ref[...] = r * (t - xh * jnp.mean(t * xh, axis=-1, keepdims=True))
        _accumulate(loss_ref, jnp.broadcast_to(part, (1, LANES)), i == 0)
        _accumulate(dg_ref, jnp.sum(dy * xh, axis=0, keepdims=True), i == 0)

    row = pl.BlockSpec((tm, D_MODEL), lambda i: (i, 0))
    vec = pl.BlockSpec((1, D_MODEL), lambda i: (0, 0))
    return _pcall(
        body, name=name, grid=(t_rows // tm,),
        in_specs=[row, row, vec],
        out_specs=[row, pl.BlockSpec((1, LANES), lambda i: (0, 0)), vec],
        out_shape=[jax.ShapeDtypeStruct((t_rows, D_MODEL), F32), jax.ShapeDtypeStruct((1, LANES), F32),
                   jax.ShapeDtypeStruct((1, D_MODEL), F32)],
        compiler_params=_cp(("arbitrary",), VMEM_BIG),
    )(h, target, norm)


ATTN_SCALE = HEAD_DIM ** -0.5
STACK_HEADS = (0, 2, 1, 3)
META_PAD = LANES - N_META


def _lane_half(shape, hf):
    lane = lax.broadcasted_iota(jnp.int32, shape, 1)
    return (lane < HEAD_DIM) if hf == 0 else (lane >= HEAD_DIM)


def _kv_variants(ref, rows, kh, pad_rows=0):
    tile = kh // 2
    t = ref[rows, tile * LANES:(tile + 1) * LANES].astype(F32)
    swapped = pltpu.roll(t, HEAD_DIM, axis=1)
    at_low, at_high = (t, swapped) if kh % 2 == 0 else (swapped, t)
    lo = jnp.where(_lane_half(t.shape, 0), at_low, 0.0).astype(BF16)
    hi = jnp.where(_lane_half(t.shape, 1), at_high, 0.0).astype(BF16)
    if pad_rows:
        zeros = jnp.zeros((pad_rows, LANES), BF16)
        lo, hi = jnp.concatenate([lo, zeros], axis=0), jnp.concatenate([hi, zeros], axis=0)
    return lo, hi


def _key_tiles(ref, key_rows, kh):
    return [_kv_variants(ref, r, kh, META_PAD if i == len(key_rows) - 1 else 0) for i, r in enumerate(key_rows)]


def _to_kv_lanes(lo, hi, kh):
    lo = jnp.where(_lane_half(lo.shape, 0), lo, 0.0)
    hi = jnp.where(_lane_half(hi.shape, 1), hi, 0.0)
    if kh % 2 == 0:
        return lo + pltpu.roll(hi, HEAD_DIM, axis=1)
    return pltpu.roll(lo, HEAD_DIM, axis=1) + hi


def _stacked(ref, rows, kh):
    col = kh * 2 * LANES
    return jnp.concatenate([ref[rows, col:col + LANES], ref[rows, col + LANES:col + 2 * LANES]], axis=0)


def _sink_column(sink_ref, kh, nq):
    row = lax.broadcasted_iota(jnp.int32, (4 * nq, 1), 0)
    col = jnp.zeros((4 * nq, 1), F32)
    for quarter, g in enumerate(STACK_HEADS):
        col = jnp.where(row // nq == quarter, sink_ref[0, kh * Q_PER_KV + g], col)
    return col


def _softmax_parts(qs, key_tiles, masks, sink):
    scores = []
    for (k_lo, k_hi), mask in zip(key_tiles, masks):
        s = jnp.concatenate([_dot_nt(qs, k_lo), _dot_nt(qs, k_hi)], axis=0) * ATTN_SCALE
        scores.append(s if mask is None else jnp.where(mask, s, NEG_INF))
    m = jnp.maximum(_row_reduce(scores, jnp.maximum, jnp.max), sink)
    probs = [jnp.exp(s - m) for s in scores]
    e_sink = jnp.exp(sink - m)
    den = _row_sums(probs) + e_sink
    return probs, 1.0 / den, e_sink


def _row_reduce(tiles, combine, reduce):
    chunks = [t[:, c:c + LANES] for t in tiles for c in range(0, t.shape[-1], LANES)]
    return reduce(functools.reduce(combine, chunks), axis=-1, keepdims=True)


def _row_sums(tiles):
    return _row_reduce(tiles, lambda u, w: u + w, jnp.sum)


def _band_mask(nq, first):
    keys = BLOCK if first else 2 * BLOCK
    qi = lax.broadcasted_iota(jnp.int32, (4 * nq, keys), 0) % nq
    kj = lax.broadcasted_iota(jnp.int32, (4 * nq, keys), 1)
    if first:
        return kj <= qi
    return jnp.logical_and(kj > qi, kj <= qi + BLOCK)


def _meta_mask(nq, causal):
    qi = lax.broadcasted_iota(jnp.int32, (4 * nq, LANES), 0) % nq
    kj = lax.broadcasted_iota(jnp.int32, (4 * nq, LANES), 1)
    return jnp.logical_and(kj < N_META, kj <= qi) if causal else kj < N_META


def _attention_schedule(seq, queries, carry):
    meta_rows = pl.ds(seq, N_META)
    meta_ok = _meta_mask(BLOCK, False)
    carry = queries(pl.ds(0, BLOCK), BLOCK, [pl.ds(0, BLOCK), meta_rows], [_band_mask(BLOCK, True), meta_ok], carry)

    def block(n, c):
        r0 = pl.multiple_of(n * BLOCK, BLOCK)
        p0 = pl.multiple_of((n - 1) * BLOCK, BLOCK)
        return queries(pl.ds(r0, BLOCK), BLOCK, [pl.ds(p0, 2 * BLOCK), meta_rows], [_band_mask(BLOCK, False), meta_ok], c)

    carry = lax.fori_loop(1, seq // BLOCK, block, carry)
    return queries(meta_rows, N_META, [meta_rows], [_meta_mask(N_META, True)], carry)


def attention_forward(proj3, sinks, seq, name):
    n_b, n_l, _ = proj3.shape

    def body(sink_ref, q_ref, k_ref, v_ref, o_ref):
        def queries(q_rows, nq, key_rows, masks, carry):
            for kh in range(N_KV_HEADS):
                ks = _key_tiles(k_ref, key_rows, kh)
                vs = _key_tiles(v_ref, key_rows, kh)
                qs = _stacked(q_ref, q_rows, kh)
                probs, inv, _ = _softmax_parts(qs, ks, masks, _sink_column(sink_ref, kh, nq))
                probs = [p.astype(BF16) for p in probs]
                o_lo = functools.reduce(lambda u, w: u + w, [_dot(p[:2 * nq], v_lo) for p, (v_lo, _) in zip(probs, vs)])
                o_hi = functools.reduce(lambda u, w: u + w, [_dot(p[2 * nq:], v_hi) for p, (_, v_hi) in zip(probs, vs)])
                out = (o_lo * inv[:2 * nq] + o_hi * inv[2 * nq:]).astype(BF16)
                col = kh * 2 * LANES
                o_ref[q_rows, col:col + LANES] = out[:nq]
                o_ref[q_rows, col + LANES:col + 2 * LANES] = out[nq:]
            return carry

        _attention_schedule(seq, queries, 0)

    return _pcall(
        body, name=name, grid=(n_b,),
        in_specs=[pl.BlockSpec(memory_space=pltpu.SMEM),
                  pl.BlockSpec((None, n_l, D_MODEL), lambda b: (b, 0, 0)),
                  pl.BlockSpec((None, n_l, KV_WIDTH), lambda b: (b, 0, D_MODEL // KV_WIDTH)),
                  pl.BlockSpec((None, n_l, KV_WIDTH), lambda b: (b, 0, D_MODEL // KV_WIDTH + 1))],
        out_specs=pl.BlockSpec((None, n_l, D_MODEL), lambda b: (b, 0, 0)),
        out_shape=jax.ShapeDtypeStruct((n_b, n_l, D_MODEL), BF16),
        compiler_params=_cp(("arbitrary",), VMEM_BIG),
    )(sinks, proj3, proj3, proj3)


def attention_backward(proj3, dattn3, dproj3, sinks, after, seq, name):
    n_b, n_l, _ = proj3.shape
    qkv_width = D_MODEL + 2 * KV_WIDTH

    def body(sink_ref, q_ref, k_ref, v_ref, do_ref, _, __, dqkv_ref, dsink_ref, dk_ref, dv_ref):
        dk_ref[...] = jnp.zeros_like(dk_ref)
        dv_ref[...] = jnp.zeros_like(dv_ref)
        sub = lax.broadcasted_iota(jnp.int32, (SUBLANES, LANES), 0)
        lane = lax.broadcasted_iota(jnp.int32, (SUBLANES, LANES), 1)

        def queries(q_rows, nq, key_rows, masks, dsink):
            for kh in range(N_KV_HEADS):
                ks = _key_tiles(k_ref, key_rows, kh)
                vs = _key_tiles(v_ref, key_rows, kh)
                qs = _stacked(q_ref, q_rows, kh)
                dos = _stacked(do_ref, q_rows, kh)
                probs, inv, e_sink = _softmax_parts(qs, ks, masks, _sink_column(sink_ref, kh, nq))
                probs = [p * inv for p in probs]
                dps = [jnp.concatenate([_dot_nt(dos, v_lo), _dot_nt(dos, v_hi)], axis=0) for v_lo, v_hi in vs]
                delta = _row_sums([p * dp for p, dp in zip(probs, dps)])
                d_sink = -(e_sink * inv) * delta
                for quarter, g in enumerate(STACK_HEADS):
                    d_here = jnp.sum(d_sink[quarter * nq:(quarter + 1) * nq], axis=0, keepdims=True)
                    dsink = dsink + jnp.where(jnp.logical_and(sub == 0, lane == kh * Q_PER_KV + g), d_here, 0.0)
                dq = None
                tile = slice((kh // 2) * LANES, (kh // 2 + 1) * LANES)
                for r, p, dp, (k_lo, k_hi) in zip(key_rows, probs, dps, ks):
                    ds = (p * (dp - delta)).astype(BF16)
                    p16 = p.astype(BF16)
                    dq_x = _dot(ds[:2 * nq], k_lo) + _dot(ds[2 * nq:], k_hi)
                    dq = dq_x if dq is None else dq + dq_x
                    d_k = _to_kv_lanes(_dot_tn(ds[:2 * nq], qs), _dot_tn(ds[2 * nq:], qs), kh) * ATTN_SCALE
                    d_v = _to_kv_lanes(_dot_tn(p16[:2 * nq], dos), _dot_tn(p16[2 * nq:], dos), kh)
                    n_keys = r.size
                    dk_ref[r, tile] += d_k[:n_keys]
                    dv_ref[r, tile] += d_v[:n_keys]
                dq = (dq * ATTN_SCALE).astype(BF16)
                col = kh * 2 * LANES
                dqkv_ref[q_rows, col:col + LANES] = dq[:nq]
                dqkv_ref[q_rows, col + LANES:col + 2 * LANES] = dq[nq:]
            return dsink

        dsink_ref[...] = _attention_schedule(seq, queries, jnp.zeros((SUBLANES, LANES), F32))
        dqkv_ref[:, D_MODEL:D_MODEL + KV_WIDTH] = dk_ref[...].astype(BF16)
        dqkv_ref[:, D_MODEL + KV_WIDTH:] = dv_ref[...].astype(BF16)

    return _pcall(
        body, name=name, grid=(n_b,),
        in_specs=[pl.BlockSpec(memory_space=pltpu.SMEM),
                  pl.BlockSpec((None, n_l, D_MODEL), lambda b: (b, 0, 0)),
                  pl.BlockSpec((None, n_l, KV_WIDTH), lambda b: (b, 0, D_MODEL // KV_WIDTH)),
                  pl.BlockSpec((None, n_l, KV_WIDTH), lambda b: (b, 0, D_MODEL // KV_WIDTH + 1)),
                  pl.BlockSpec((None, n_l, D_MODEL), lambda b: (b, 0, 0)),
                  ANY_SPEC, ANY_SPEC],
        out_specs=[pl.BlockSpec((None, n_l, qkv_width), lambda b: (b, 0, 0)),
                   pl.BlockSpec((None, SUBLANES, LANES), lambda b: (b, 0, 0))],
        out_shape=[jax.ShapeDtypeStruct(dproj3.shape, BF16), jax.ShapeDtypeStruct((n_b, SUBLANES, LANES), F32)],
        scratch_shapes=[pltpu.VMEM((n_l, KV_WIDTH), F32), pltpu.VMEM((n_l, KV_WIDTH), F32)],
        input_output_aliases={5: 0},
        compiler_params=_cp(("arbitrary",), VMEM_BIG),
    )(sinks, proj3, proj3, proj3, dattn3, dproj3, after)


TAB_ROWS = 8
SCAN_UNROLL = 4


def _cmul(ar, ai, br, bi):
    return ar * br - ai * bi, ar * bi + ai * br


def _discretise(ar, ai, ls):
    step = jnp.exp(ls)
    mag = jnp.exp(ar * step)
    ang = ai * step
    cos, sin = jnp.cos(ang), jnp.sin(ang)
    lr, li = mag * cos, mag * sin
    den = ar * ar + ai * ai
    nr, ni = lr - 1.0, li
    cr = (nr * ar + ni * ai) / den
    ci = (ni * ar - nr * ai) / den
    return step, mag, lr, li, den, nr, ni, cr, ci


def _scan_tables(lr, li, reverse):
    n = lr.shape[-1]
    pw = [(lr, li)]
    for _ in range(SUBLANES - 1):
        pw.append(_cmul(pw[-1][0], pw[-1][1], lr, li))
    row = lax.broadcasted_iota(jnp.int32, (SUBLANES, n), 0)
    out = []
    for d in (1, 2, 4):
        ok = (row + d <= SUBLANES - 1) if reverse else (row >= d)
        out += [jnp.where(ok, pw[d - 1][0], 0.0), jnp.where(ok, pw[d - 1][1], 0.0)]
    cr = jnp.zeros((SUBLANES, n), F32)
    ci = jnp.zeros((SUBLANES, n), F32)
    for r in range(SUBLANES):
        e = (SUBLANES - r) if reverse else (r + 1)
        cr = jnp.where(row == r, pw[e - 1][0], cr)
        ci = jnp.where(row == r, pw[e - 1][1], ci)
    return out + [cr, ci]


def ssm_prepare(ar, ai, ls, br_t, bi_t, name):
    def body(ar_ref, ai_ref, ls_ref, br_ref, bi_ref, bbr_ref, bbi_ref, tf_ref, tr_ref):
        _, _, lr, li, _, _, _, cr, ci = _discretise(ar_ref[...], ai_ref[...], ls_ref[...])
        br, bi = br_ref[...], bi_ref[...]
        bbr_ref[...] = cr * br - ci * bi
        bbi_ref[...] = cr * bi + ci * br
        for k, t in enumerate(_scan_tables(lr, li, False)):
            tf_ref[k] = t
        for k, t in enumerate(_scan_tables(lr, -li, True)):
            tr_ref[k] = t

    return _pcall(
        body, name=name,
        out_shape=[jax.ShapeDtypeStruct((SSM_GROUP, N_STATES), F32), jax.ShapeDtypeStruct((SSM_GROUP, N_STATES), F32),
                   jax.ShapeDtypeStruct((TAB_ROWS, SUBLANES, N_STATES), F32),
                   jax.ShapeDtypeStruct((TAB_ROWS, SUBLANES, N_STATES), F32)],
    )(ar, ai, ls, br_t, bi_t)


def ssm_param_backward(ar, ai, ls, br_t, bi_t, dlr_p, dli_p, dbbr, dbbi, group_sum, name):
    def body(ar_ref, ai_ref, ls_ref, br_ref, bi_ref, dlr_ref, dli_ref, dbbr_ref, dbbi_ref, gs_ref,
             dar_ref, dai_ref, dls_ref, dbr_ref, dbi_ref):
        ar, ai = ar_ref[...], ai_ref[...]
        step, mag, lr, li, den, nr, ni, cr, ci = _discretise(ar, ai, ls_ref[...])
        br, bi, dbbr_v, dbbi_v = br_ref[...], bi_ref[...], dbbr_ref[...], dbbi_ref[...]
        dbr_ref[...] = cr * dbbr_v + ci * dbbi_v
        dbi_ref[...] = cr * dbbi_v - ci * dbbr_v
        dcr = jnp.sum(dbbr_v * br + dbbi_v * bi, axis=0, keepdims=True)
        dci = jnp.sum(dbbi_v * br - dbbr_v * bi, axis=0, keepdims=True)
        dnr = (dcr * ar - dci * ai) / den
        dni = (dcr * ai + dci * ar) / den
        dden = -(cr * dcr + ci * dci) / den
        dar = (dcr * nr + dci * ni) / den + dden * 2.0 * ar
        dai = (dcr * ni - dci * nr) / den + dden * 2.0 * ai
        dlr = jnp.sum(dlr_ref[...], axis=0, keepdims=True) + dnr
        dli = jnp.sum(dli_ref[...], axis=0, keepdims=True) + dni
        dmag = (dlr * lr + dli * li) / mag
        dang = dli * lr - dlr * li
        dar_ref[...] = dar + dmag * mag * step
        dai_ref[...] = dai + dang * step
        dstep = dmag * mag * ar + dang * ai
        dls_ref[...] = jnp.dot(dstep * step, gs_ref[...], preferred_element_type=F32, precision=lax.Precision.HIGHEST)

    vec = jax.ShapeDtypeStruct((1, N_STATES), F32)
    mat = jax.ShapeDtypeStruct((SSM_GROUP, N_STATES), F32)
    return _pcall(body, name=name, out_shape=[vec, vec, jax.ShapeDtypeStruct((1, LANES), F32), mat, mat])(
        ar, ai, ls, br_t, bi_t, dlr_p, dli_p, dbbr, dbbi, group_sum)


def _scan_rows(a, b, tabs, carry, reverse):
    for k, d in enumerate((1, 2, 4)):
        shift = SUBLANES - d if reverse else d
        sr, si = pltpu.roll(a, shift, axis=0), pltpu.roll(b, shift, axis=0)
        pr, pi = _cmul(tabs[2 * k], tabs[2 * k + 1], sr, si)
        a, b = a + pr, b + pi
    pr, pi = _cmul(tabs[6], tabs[7], carry[0], carry[1])
    return a + pr, b + pi


def _time_groups(seq, reverse):
    meta = [seq + SUBLANES * g for g in range(N_META // SUBLANES)]
    return meta[::-1] if reverse else meta


def ssm_forward_scan(proj3, b_comb, tabf, c_comb, dvec, seq, name):
    n_b, n_l, _ = proj3.shape
    u_blk = (D_MODEL + 2 * KV_WIDTH) // LANES

    def body(u_ref, b_ref, tab_ref, c_ref, d_ref, x_ref, y_ref, bu, xs):
        j = pl.program_id(1)
        u = u_ref[...]
        bu[...] = _dot(u, b_ref[...])
        tabs = [tab_ref[k] for k in range(TAB_ROWS)]

        def group(r0, carry):
            rows = pl.ds(r0, SUBLANES)
            a, b = _scan_rows(bu[rows, :SCAN_COLS], bu[rows, SCAN_COLS:], tabs, carry, False)
            xs[rows, :SCAN_COLS] = a
            xs[rows, SCAN_COLS:] = b
            return (jnp.broadcast_to(a[SUBLANES - 1:, :], a.shape), jnp.broadcast_to(b[SUBLANES - 1:, :], b.shape))

        zero = jnp.zeros((SUBLANES, SCAN_COLS), F32)
        carry = (zero, zero)
        for r0 in _time_groups(seq, False):
            carry = group(r0, carry)
        span = SCAN_UNROLL * SUBLANES

        def groups(t, c):
            for k in range(SCAN_UNROLL):
                c = group(pl.multiple_of(t * span, span) + k * SUBLANES, c)
            return c

        lax.fori_loop(0, seq // span, groups, carry)
        x16 = xs[...].astype(BF16)
        x_ref[...] = x16
        contrib = _dot(x16, c_ref[...])

        @pl.when(j % 2 == 0)
        def _():
            y_ref[...] = contrib + d_ref[...] * u.astype(F32)

        @pl.when(j % 2 == 1)
        def _():
            y_ref[...] += contrib

    return _pcall(
        body, name=name, grid=(n_b, N_SCAN_BLK),
        in_specs=[pl.BlockSpec((None, n_l, LANES), lambda b, j: (b, 0, u_blk + j // 2)),
                  pl.BlockSpec((None, LANES, 2 * SCAN_COLS), lambda b, j: (j, 0, 0)),
                  pl.BlockSpec((TAB_ROWS, SUBLANES, SCAN_COLS), lambda b, j: (0, 0, j)),
                  pl.BlockSpec((None, 2 * SCAN_COLS, LANES), lambda b, j: (j, 0, 0)),
                  pl.BlockSpec((1, LANES), lambda b, j: (0, j // 2))],
        out_specs=[pl.BlockSpec((None, n_l, 2 * SCAN_COLS), lambda b, j: (b, 0, j)),
                   pl.BlockSpec((None, n_l, LANES), lambda b, j: (b, 0, j // 2))],
        out_shape=[jax.ShapeDtypeStruct((n_b, n_l, 2 * N_STATES), BF16),
                   jax.ShapeDtypeStruct((n_b, n_l, SSM_WIDTH), F32)],
        scratch_shapes=[pltpu.VMEM((n_l, 2 * SCAN_COLS), F32)] * 2,
        compiler_params=_cp(("arbitrary", "arbitrary"), VMEM_BIG),
    )(proj3, b_comb, tabf, c_comb, dvec)


def ssm_backward_scan(dyraw3, xs3, dproj3, c_comb_t, tabr, b_comb_t, dvec, seq, name):
    n_b, n_l, _ = xs3.shape
    u_blk = (D_MODEL + 2 * KV_WIDTH) // LANES

    def body(dy_ref, x_ref, _, c_ref, tab_ref, b_ref, d_ref, du_ref, g_ref, dlr_ref, dli_ref, dx, gs, xs, du_acc):
        j = pl.program_id(1)
        dy = dy_ref[...]
        dx[...] = _dot(dy, c_ref[...])
        xs[...] = x_ref[...].astype(F32)
        tabs = [tab_ref[k] for k in range(TAB_ROWS)]
        last_row = lax.broadcasted_iota(jnp.int32, (SUBLANES, SCAN_COLS), 0) == SUBLANES - 1

        def group(r0, state):
            cr, ci, acc_r, acc_i = state
            rows = pl.ds(r0, SUBLANES)
            a, b = _scan_rows(dx[rows, :SCAN_COLS], dx[rows, SCAN_COLS:], tabs, (cr, ci), True)
            gs[rows, :SCAN_COLS] = a
            gs[rows, SCAN_COLS:] = b
            na = jnp.where(last_row, cr, pltpu.roll(a, SUBLANES - 1, axis=0))
            nb = jnp.where(last_row, ci, pltpu.roll(b, SUBLANES - 1, axis=0))
            xa, xb = xs[rows, :SCAN_COLS], xs[rows, SCAN_COLS:]
            return (jnp.broadcast_to(a[:1, :], a.shape), jnp.broadcast_to(b[:1, :], b.shape),
                    acc_r + na * xa + nb * xb, acc_i + nb * xa - na * xb)

        zero = jnp.zeros((SUBLANES, SCAN_COLS), F32)
        span = SCAN_UNROLL * SUBLANES
        n_spans = seq // span

        def groups(t, s):
            for k in reversed(range(SCAN_UNROLL)):
                s = group(pl.multiple_of((n_spans - 1 - t) * span, span) + k * SUBLANES, s)
            return s

        state = lax.fori_loop(0, n_spans, groups, (zero, zero, zero, zero))
        for r0 in _time_groups(seq, True):
            state = group(r0, state)
        dlr_ref[...] = state[2]
        dli_ref[...] = state[3]
        g16 = gs[...].astype(BF16)
        g_ref[...] = g16
        contrib = _dot(g16, b_ref[...])

        @pl.when(j % 2 == 0)
        def _():
            du_acc[...] = contrib + d_ref[...] * dy.astype(F32)

        @pl.when(j % 2 == 1)
        def _():
            du_ref[...] = (du_acc[...] + contrib).astype(BF16)

    state_blk = pl.BlockSpec((None, n_l, 2 * SCAN_COLS), lambda b, j: (b, 0, j))
    dl_blk = pl.BlockSpec((None, SUBLANES, SCAN_COLS), lambda b, j: (b, 0, j))
    return _pcall(
        body, name=name, grid=(n_b, N_SCAN_BLK),
        in_specs=[pl.BlockSpec((None, n_l, LANES), lambda b, j: (b, 0, j // 2)), state_blk,
                  pl.BlockSpec(memory_space=pl.ANY),
                  pl.BlockSpec((None, LANES, 2 * SCAN_COLS), lambda b, j: (j, 0, 0)),
                  pl.BlockSpec((TAB_ROWS, SUBLANES, SCAN_COLS), lambda b, j: (0, 0, j)),
                  pl.BlockSpec((None, 2 * SCAN_COLS, LANES), lambda b, j: (j, 0, 0)),
                  pl.BlockSpec((1, LANES), lambda b, j: (0, j // 2))],
        out_specs=[pl.BlockSpec((None, n_l, LANES), lambda b, j: (b, 0, u_blk + j // 2)), state_blk, dl_blk, dl_blk],
        out_shape=[jax.ShapeDtypeStruct(dproj3.shape, BF16), jax.ShapeDtypeStruct((n_b, n_l, 2 * N_STATES), BF16),
                   jax.ShapeDtypeStruct((n_b, SUBLANES, N_STATES), F32), jax.ShapeDtypeStruct((n_b, SUBLANES, N_STATES), F32)],
        scratch_shapes=[pltpu.VMEM((n_l, 2 * SCAN_COLS), F32)] * 3 + [pltpu.VMEM((n_l, LANES), F32)],
        input_output_aliases={2: 0},
        compiler_params=_cp(("arbitrary", "arbitrary"), VMEM_BIG),
    )(dyraw3, xs3, dproj3, c_comb_t, tabr, b_comb_t, dvec)


def ssm_param_grads(proj, gs, xs, dyraw, tm, name):
    t_rows = proj.shape[0]
    ni = t_rows // tm
    u_blk = (D_MODEL + 2 * KV_WIDTH) // LANES
    width = 2 * SCAN_COLS

    def body(u_ref, g_ref, x_ref, dy_ref, db_ref, dc_ref, dd_ref):
        cb, i = pl.program_id(0), pl.program_id(1)
        u, dy = u_ref[...], dy_ref[...]
        _accumulate(db_ref, _dot_tn(u, g_ref[...]), i == 0)
        _accumulate(dc_ref, _dot_tn(x_ref[...], dy), i == 0)

        @pl.when(cb % 2 == 0)
        def _():
            _accumulate(dd_ref, jnp.sum(dy.astype(F32) * u.astype(F32), axis=0, keepdims=True), i == 0)

    return _pcall(
        body, name=name, grid=(N_SCAN_BLK, ni),
        in_specs=[pl.BlockSpec((tm, LANES), lambda cb, i: (i, u_blk + cb // 2)),
                  pl.BlockSpec((tm, width), lambda cb, i: (i, cb)),
                  pl.BlockSpec((tm, width), lambda cb, i: (i, cb)),
                  pl.BlockSpec((tm, LANES), lambda cb, i: (i, cb // 2))],
        out_specs=[pl.BlockSpec((None, LANES, width), lambda cb, i: (cb, 0, 0)),
                   pl.BlockSpec((None, width, LANES), lambda cb, i: (cb, 0, 0)),
                   pl.BlockSpec((1, LANES), lambda cb, i: (0, cb // 2))],
        out_shape=[jax.ShapeDtypeStruct((N_SCAN_BLK, LANES, width), F32),
                   jax.ShapeDtypeStruct((N_SCAN_BLK, width, LANES), F32), jax.ShapeDtypeStruct((1, SSM_WIDTH), F32)],
        compiler_params=_cp(("arbitrary", "arbitrary"), VMEM_BIG),
    )(proj, gs, xs, dyraw)


def sum_leading(x, name):
    def body(x_ref, o_ref):
        acc = x_ref[0]
        for k in range(1, x.shape[0]):
            acc = acc + x_ref[k]
        o_ref[...] = acc

    return _pcall(body, name=name, out_shape=jax.ShapeDtypeStruct(x.shape[1:], x.dtype))(x)


WEIGHTS = ['meta_tokens', 'ffn1_norm', 'ffn1_w1', 'ffn1_w3', 'ffn1_w2', 'mix_norm', 'w_in', 'attn_sinks', 'ssm_a_re',
           'ssm_a_im', 'ssm_log_step', 'ssm_b_re', 'ssm_b_im', 'ssm_c_re', 'ssm_c_im', 'ssm_d', 'ssm_glu_a', 'ssm_glu_b',
           'w_out', 'ffn2_norm', 'ffn2_w1', 'ffn2_w3', 'ffn2_w2', 'final_norm']
SHARDED = ['ffn1_w1', 'ffn1_w3', 'ffn1_w2', 'ffn2_w1', 'ffn2_w3', 'ffn2_w2', 'w_in', 'ssm_glu_a', 'ssm_glu_b', 'w_out']
REPLICATED = ['ffn1_norm', 'mix_norm', 'ffn2_norm', 'final_norm', 'attn_sinks', 'ssm_a_re', 'ssm_a_im', 'ssm_log_step',
              'ssm_b_re', 'ssm_b_im', 'ssm_c_re', 'ssm_c_im', 'ssm_d']
PACK_COLS = 1024


def _pack(arrays):
    parts = []
    for a in arrays:
        flat = a.reshape(-1)
        chunk = SUBLANES * PACK_COLS
        padded = -(-flat.shape[0] // chunk) * chunk
        parts.append(jnp.pad(flat, (0, padded - flat.shape[0])).reshape(-1, PACK_COLS))
    return jnp.concatenate(parts, axis=0)


def _unpack(packed, shapes):
    out, row = [], 0
    for shape in shapes:
        size = 1
        for s in shape:
            size *= s
        chunk = SUBLANES * PACK_COLS
        rows = -(-size // chunk) * SUBLANES
        out.append(packed[row:row + rows].reshape(-1)[:size].reshape(shape))
        row += rows
    return out


def kernel(x, meta_tokens, ffn1_norm, ffn1_w1, ffn1_w3, ffn1_w2, mix_norm, w_in, attn_sinks, ssm_a_re, ssm_a_im, ssm_log_step, ssm_b_re, ssm_b_im, ssm_c_re, ssm_c_im, ssm_d, ssm_glu_a, ssm_glu_b, w_out, ffn2_norm, ffn2_w1, ffn2_w3, ffn2_w2, final_norm, loss_target, m_meta_tokens, m_ffn1_norm, m_ffn1_w1, m_ffn1_w3, m_ffn1_w2, m_mix_norm, m_w_in, m_attn_sinks, m_ssm_a_re, m_ssm_a_im, m_ssm_log_step, m_ssm_b_re, m_ssm_b_im, m_ssm_c_re, m_ssm_c_im, m_ssm_d, m_ssm_glu_a, m_ssm_glu_b, m_w_out, m_ffn2_norm, m_ffn2_w1, m_ffn2_w3, m_ffn2_w2, m_final_norm, v_meta_tokens, v_ffn1_norm, v_ffn1_w1, v_ffn1_w3, v_ffn1_w2, v_mix_norm, v_w_in, v_attn_sinks, v_ssm_a_re, v_ssm_a_im, v_ssm_log_step, v_ssm_b_re, v_ssm_b_im, v_ssm_c_re, v_ssm_c_im, v_ssm_d, v_ssm_glu_a, v_ssm_glu_b, v_w_out, v_ffn2_norm, v_ffn2_w1, v_ffn2_w3, v_ffn2_w2, v_final_norm):
    given = dict(locals())
    w = {n: given[n] for n in WEIGHTS}
    m = {n: given["m_" + n] for n in WEIGHTS}
    v = {n: given["v_" + n] for n in WEIGHTS}

    n_b, seq, _ = x.shape
    n_l = seq + N_META
    t_rows = n_b * n_l
    tm = _row_tile(n_l, 688)
    px, py, pc = _my_place()
    me = 4 * px + 2 * py + pc

    glu = jnp.stack([ssm_glu_a[0], ssm_glu_b[0]]).astype(BF16)
    ffn_names = ['ffn1_w1', 'ffn1_w3', 'ffn1_w2', 'ffn2_w1', 'ffn2_w3', 'ffn2_w2']

    def hidden_on_rows(n, t):
        return t[0] if n.endswith('w2') else t[0].T

    def hidden_on_rows_back(n, t):
        return t[None] if n.endswith('w2') else t.T[None]

    me_idx = jnp.reshape(me, (1,)).astype(jnp.int32)
    first_names, later_names = ffn_names[:3], ffn_names[3:]
    *first, metag = all_gather_list(
        [hidden_on_rows(n, w[n]).astype(BF16) for n in first_names] + [meta_tokens], meta_tokens, "ag_first")
    win_send, win_recv, win_shard, win_land, win_token = exchange_start(
        [w_in[0].astype(BF16)], first[0], True, "ag_w_in_start")
    later_shards = [hidden_on_rows(n, w[n]).astype(BF16) for n in later_names] + [glu, w_out[0].astype(BF16)]
    ag_send, ag_recv, later_shards, later_lands, ag_token = exchange_start(later_shards, win_token, True, "ag_later_start")
    full = {n: g.reshape(D_FF, D_MODEL) for n, g in zip(first_names, first)}
    meta_full = metag.transpose(1, 0, 2).reshape(N_META, D_MODEL)

    h0 = jnp.concatenate([x, jnp.broadcast_to(meta_full[None], (n_b, N_META, D_MODEL))], axis=1).reshape(t_rows, D_MODEL)
    target = jnp.concatenate([loss_target, jnp.zeros((n_b, N_META, D_MODEL), F32)], axis=1).reshape(t_rows, D_MODEL)
    final_g = final_norm.reshape(1, D_MODEL)

    ar = ssm_a_re.reshape(1, N_STATES)
    ai = ssm_a_im.reshape(1, N_STATES)
    ls = jnp.repeat(ssm_log_step.reshape(SSM_GROUPS), SSM_STATE).reshape(1, N_STATES)
    br_t = ssm_b_re[0].transpose(2, 0, 1).reshape(SSM_GROUP, N_STATES)
    bi_t = ssm_b_im[0].transpose(2, 0, 1).reshape(SSM_GROUP, N_STATES)
    bbr, bbi, tabf, tabr = ssm_prepare(ar, ai, ls, br_t, bi_t, "ssm_prepare")
    bbr_g = bbr.reshape(SSM_GROUP, SSM_GROUPS, SSM_STATE).transpose(1, 0, 2)
    bbi_g = bbi.reshape(SSM_GROUP, SSM_GROUPS, SSM_STATE).transpose(1, 0, 2)
    groups_per_blk = SCAN_COLS // SSM_STATE
    half = ((jnp.arange(N_SCAN_BLK) % 2)[:, None] == jnp.arange(2)[None, :]).astype(F32)
    eye = jnp.eye(groups_per_blk, dtype=F32)

    def scan_blocks(re_g, im_g):
        def one(t):
            t = t.reshape(N_SCAN_BLK, groups_per_blk, SSM_GROUP, SSM_STATE)
            t = t[:, :, :, None, :] * eye[None, :, None, :, None]
            t = t.reshape(N_SCAN_BLK, LANES // 2, SCAN_COLS)
            return (t[:, None] * half[:, :, None, None]).reshape(N_SCAN_BLK, LANES, SCAN_COLS)
        return jnp.concatenate([one(re_g), one(im_g)], axis=-1).astype(BF16)

    b_comb = scan_blocks(bbr_g, bbi_g)
    c_comb_t = scan_blocks(ssm_c_re[0], -ssm_c_im[0])
    b_comb_t, c_comb = b_comb.transpose(0, 2, 1), c_comb_t.transpose(0, 2, 1)

    ffn1_w = (full['ffn1_w1'], full['ffn1_w3'], full['ffn1_w2'])
    h1, hn1, a1, b1 = ffn_forward(h0, ffn1_norm, *ffn1_w, ag_token, tm, "ffn1_fwd")
    (wing,) = exchange_wait(win_send, win_recv, win_shard, win_land, h1, True, "ag_w_in_wait")
    wing = lax.dynamic_update_slice_in_dim(wing, win_shard[0][None], me, axis=0)
    hnm, proj = mix_forward(h1, mix_norm, wing, tm, "mix_fwd")
    proj3 = proj.reshape(n_b, n_l, IN_WIDTH)
    attn3 = attention_forward(proj3, attn_sinks, seq, "attn_fwd")
    attn = attn3.reshape(t_rows, D_MODEL)
    xs3, yraw3 = ssm_forward_scan(proj3, b_comb, tabf, c_comb, ssm_d, seq, "ssm_fwd")
    yraw = yraw3.reshape(t_rows, SSM_WIDTH)
    later = exchange_wait(ag_send, ag_recv, later_shards, later_lands, yraw3, True, "ag_later_wait")
    later = [lax.dynamic_update_slice_in_dim(z, s[None], me, axis=0) for z, s in zip(later, later_shards)]
    for n, g in zip(later_names, later):
        full[n] = g.reshape(D_FF, D_MODEL)
    ffn2_w = (full['ffn2_w1'], full['ffn2_w3'], full['ffn2_w2'])
    glug, wog = later[len(later_names):]
    glu_a = glug[:, 0].transpose(1, 0, 2).reshape(SSM_WIDTH, D_MODEL)
    glu_b = glug[:, 1].transpose(1, 0, 2).reshape(SSM_WIDTH, D_MODEL)
    w_out_full = wog.reshape(D_MODEL, D_MODEL)
    h2 = merge_forward(h1, yraw, attn, proj, glu_a, glu_b, w_out_full, tm, "merge_fwd")
    h3, hn2, a2, b2 = ffn_forward(h2, ffn2_norm, *ffn2_w, ag_token, tm, "ffn2_fwd")
    dh3, loss_part, g_final = final_loss_backward(h3, target, final_g, seq, tm, "loss_bwd")
    loss = lax.psum(loss_part[0, 0], ("x", "y", "c"))

    def blocked_ffn(d_w1t, d_w3t, d_w2):
        return tuple(t.reshape(N_DEV, FF_BLK, D_MODEL) for t in (d_w1t, d_w3t, d_w2))

    def blocked_cols(full_grad):
        r = full_grad.shape[0]
        return full_grad.reshape(r, N_DEV, full_grad.shape[1] // N_DEV).transpose(1, 0, 2).astype(BF16)

    early = {}

    def start_reduce(names, tag):
        srcs = [dw[n] for n in names]
        send, recv, srcs, lands, token = exchange_start(srcs, srcs[0], False, "rs_" + tag + "_start")
        early[tag] = (names, send, recv, srcs, lands)
        return token

    dw = {}
    da2, db2, dh3_half = ffn_backward_hidden(dh3, a2, b2, ffn2_w[2], g_final, tm, "ffn2_bwd_hid")
    dw['ffn2_w1'], dw['ffn2_w3'], dw['ffn2_w2'] = blocked_ffn(
        *ffn_backward_weights(hn2, dh3_half, a2, b2, da2, db2, n_l, FF_BWD_COLS, "ffn2_bwd_w"))
    token = start_reduce(later_names, "ffn2")
    dh2, g_ffn2_norm = ffn_backward_input(dh3, h2, ffn2_norm, da2, db2, ffn2_w[0], ffn2_w[1], token, tm, "ffn2_bwd_in")
    dattn, dyraw, dproj, *for_weights = merge_backward(dh2, yraw, attn, proj, glu_a, glu_b, w_out_full, token, tm,
                                                       "merge_bwd")
    d_wo, d_ga, d_gb = merge_backward_weights(*for_weights, tm, "merge_bwd_w")
    dw['ssm_glu_a'] = blocked_cols(d_ga)
    dw['ssm_glu_b'] = blocked_cols(d_gb)
    dw['w_out'] = d_wo.reshape(N_DEV, D_MODEL // N_DEV, D_MODEL).astype(BF16)
    token = start_reduce(['ssm_glu_a', 'ssm_glu_b', 'w_out'], "mix")
    dproj3 = dproj.reshape(n_b, n_l, IN_WIDTH)
    dproj3, dsink_p = attention_backward(proj3, dattn.reshape(n_b, n_l, D_MODEL), dproj3, attn_sinks, token, seq,
                                         "attn_bwd")
    dproj3, gs3, dlr_p, dli_p = ssm_backward_scan(
        dyraw.reshape(n_b, n_l, SSM_WIDTH), xs3, dproj3, c_comb_t, tabr, b_comb_t, ssm_d, seq, "ssm_bwd")
    dproj = dproj3.reshape(t_rows, IN_WIDTH)
    d_bd, d_cd, g_d = ssm_param_grads(proj, gs3.reshape(t_rows, 2 * N_STATES), xs3.reshape(t_rows, 2 * N_STATES),
                                      dyraw, n_l, "ssm_bwd_w")
    w_in_full = wing.transpose(1, 0, 2).reshape(D_MODEL, IN_WIDTH)
    dh1, g_mix_norm = mix_backward_act(dh2, h1, mix_norm, dproj, w_in_full, tm, "mix_bwd_act")
    dw['w_in'] = mix_backward_weights(hnm, dproj, n_l, "mix_bwd_w")
    token = start_reduce(['w_in'], "w_in")

    def group_blocks(part, channels_first):
        if channels_first:
            t = jnp.sum(part.reshape(N_SCAN_BLK, 2, LANES // 2, SCAN_COLS) * half[:, :, None, None], axis=1)
            t = t.reshape(N_SCAN_BLK, groups_per_blk, SSM_GROUP, groups_per_blk, SSM_STATE)
            t = jnp.sum(t * eye[None, :, None, :, None], axis=3)
            return t.reshape(SSM_GROUPS, SSM_GROUP, SSM_STATE)
        t = jnp.sum(part.reshape(N_SCAN_BLK, SCAN_COLS, 2, LANES // 2) * half[:, None, :, None], axis=2)
        t = t.reshape(N_SCAN_BLK, groups_per_blk, SSM_STATE, groups_per_blk, SSM_GROUP)
        t = jnp.sum(t * eye[None, :, None, :, None], axis=3)
        return t.reshape(SSM_GROUPS, SSM_STATE, SSM_GROUP).transpose(0, 2, 1)

    dbbr = group_blocks(d_bd[:, :, :SCAN_COLS], True).transpose(1, 0, 2).reshape(SSM_GROUP, N_STATES)
    dbbi = group_blocks(d_bd[:, :, SCAN_COLS:], True).transpose(1, 0, 2).reshape(SSM_GROUP, N_STATES)
    g_c_re = group_blocks(d_cd[:, :SCAN_COLS, :], False)[None]
    g_c_im = -group_blocks(d_cd[:, SCAN_COLS:, :], False)[None]
    group_sum = (jnp.arange(N_STATES)[:, None] // SSM_STATE == jnp.arange(LANES)[None, :]).astype(F32)
    g_ar, g_ai, g_ls, g_br, g_bi = ssm_param_backward(
        ar, ai, ls, br_t, bi_t, dlr_p.reshape(n_b * SUBLANES, N_STATES), dli_p.reshape(n_b * SUBLANES, N_STATES),
        dbbr, dbbi, group_sum, "ssm_bwd_params")
    g_sinks = sum_leading(dsink_p, "sink_sum")[0:1, :N_KV_HEADS * Q_PER_KV]

    small = {
        'mix_norm': g_mix_norm, 'ffn2_norm': g_ffn2_norm, 'final_norm': g_final.reshape(D_MODEL),
        'attn_sinks': g_sinks, 'ssm_a_re': g_ar.reshape(1, SSM_GROUPS, SSM_STATE), 'ssm_a_im': g_ai.reshape(1, SSM_GROUPS, SSM_STATE),
        'ssm_log_step': g_ls[:, :SSM_GROUPS],
        'ssm_b_re': g_br.reshape(SSM_GROUP, SSM_GROUPS, SSM_STATE).transpose(1, 2, 0)[None],
        'ssm_b_im': g_bi.reshape(SSM_GROUP, SSM_GROUPS, SSM_STATE).transpose(1, 2, 0)[None],
        'ssm_c_re': g_c_re, 'ssm_c_im': g_c_im, 'ssm_d': g_d,
    }
    early_small = [n for n in REPLICATED if n in small]
    sg_send, sg_recv, sg_src, sg_land, token = exchange_start(
        [_pack([small[n] for n in early_small])], token, True, "ag_small_start")
    da1, db1, dh1_half = ffn_backward_hidden(dh1, a1, b1, ffn1_w[2], token, tm, "ffn1_bwd_hid")
    dw['ffn1_w1'], dw['ffn1_w3'], dw['ffn1_w2'] = blocked_ffn(
        *ffn_backward_weights(hn1, dh1_half, a1, b1, da1, db1, n_l, FF_BWD_COLS, "ffn1_bwd_w"))
    token = start_reduce(first_names, "ffn1")
    dh0, g_ffn1_norm = ffn_backward_input(dh1, h0, ffn1_norm, da1, db1, ffn1_w[0], ffn1_w[1], token, tm, "ffn1_bwd_in")
    dh0_3 = dh0.reshape(n_b, n_l, D_MODEL)
    grad_x = dh0_3[:, :seq]
    g_meta = sum_leading(dh0_3[:, seq:], "meta_sum")

    grads, deltas, new_m, new_v = {}, {}, {}, {}

    def views(n):
        if n in ffn_names:
            return functools.partial(hidden_on_rows, n), functools.partial(hidden_on_rows_back, n)
        return (lambda t: t[0]), (lambda t: t[None])

    def finish_reduce(tag, previous):
        names, send, recv, srcs, lands = early[tag]
        lands = exchange_wait(send, recv, srcs, lands, previous, False, "rs_" + tag + "_wait")
        for n, g, land in zip(names, srcs, lands):
            two_d, back = views(n)
            out = adamw_exchanged(me_idx, g, land, two_d(w[n]), two_d(m[n]), two_d(v[n]), "adamw_" + n)
            grads[n], deltas[n], new_m[n], new_v[n] = (back(o) for o in out)
            previous = out[1]
        return previous

    previous = g_meta
    for tag in ("ffn2", "mix", "w_in"):
        previous = finish_reduce(tag, previous)

    zeros_meta = jnp.zeros((N_META, D_MODEL), F32)
    (late_parts,) = all_gather_list([_pack([g_ffn1_norm, g_meta])], previous, "ag_small_late")
    (early_parts,) = exchange_wait(sg_send, sg_recv, sg_src, sg_land, late_parts, True, "ag_small_wait")
    early_parts = lax.dynamic_update_slice_in_dim(early_parts, sg_src[0][None], me, axis=0)

    def small_update(parts, names, extra, tag):
        pack_of = lambda d: _pack([d[n] for n in names] + extra)
        packed = adamw_small(parts, pack_of(w), pack_of(m), pack_of(v), "adamw_small_" + tag)
        unpacked = [_unpack(p, [w[n].shape for n in names] + [e.shape for e in extra]) for p in packed]
        for k, n in enumerate(names):
            grads[n], deltas[n], new_m[n], new_v[n] = (u[k] for u in unpacked)
        return packed, unpacked

    small_update(early_parts, early_small, [], "early")
    packed_out, unpacked = small_update(late_parts, ['ffn1_norm'], [zeros_meta], "late")
    g_meta_full = unpacked[0][-1]
    grads['meta_tokens'] = lax.dynamic_index_in_dim(
        g_meta_full.reshape(N_META, N_DEV, D_MODEL // N_DEV), me, axis=1, keepdims=False)
    deltas['meta_tokens'], new_m['meta_tokens'], new_v['meta_tokens'] = adamw_plain(
        grads['meta_tokens'], w['meta_tokens'], m['meta_tokens'], v['meta_tokens'], "adamw_meta")

    finish_reduce("ffn1", packed_out[0])

    return (loss, grad_x, *[grads[n] for n in WEIGHTS], *[deltas[n] for n in WEIGHTS],
            *[new_m[n] for n in WEIGHTS], *[new_v[n] for n in WEIGHTS])
```

```python
import functools

import jax
import jax.numpy as jnp
from jax import lax
from jax.experimental import pallas as pl
from jax.experimental.pallas import tpu as pltpu

F32 = jnp.float32
BF16 = jnp.bfloat16
MESH = pl.DeviceIdType.MESH

N_DEV = 8
D_MODEL = 1024
N_META = 16
HEAD_DIM = 64
N_KV_HEADS = 4
Q_PER_KV = 4
BLOCK = 128
KV_WIDTH = N_KV_HEADS * HEAD_DIM
SSM_GROUP = 16
SSM_WIDTH = 512
SSM_GROUPS = 32
SSM_STATE = 64
N_STATES = SSM_GROUPS * SSM_STATE
D_FF = 2816
FF_BLK = D_FF // N_DEV
IN_WIDTH = 4096
IN_BLK = IN_WIDTH // N_DEV
NORM_EPS = 1e-6
NEG_INF = -1e30
SCAN_COLS = 256
N_SCAN_BLK = N_STATES // SCAN_COLS
SUBLANES = 8
LANES = 128
MXU_WIDTH = 256
FF_BWD_COLS = MXU_WIDTH

ADAM_LR = 0.001
ADAM_B1 = 0.9
ADAM_B2 = 0.999
ADAM_EPS = 1e-08
ADAM_WD = 0.01
ADAM_STEP = 10

VMEM_BIG = 56 * 1024 * 1024


def _cp(sem=None, vmem=None):
    kw = {}
    if sem is not None:
        kw["dimension_semantics"] = sem
    if vmem is not None:
        kw["vmem_limit_bytes"] = vmem
    return pltpu.CompilerParams(**kw)


def _pcall(body, **kw):
    return pl.pallas_call(body, **kw)


def _dot(a, b):
    return jnp.dot(a, b, preferred_element_type=F32)


def _dot_nt(a, b):
    return lax.dot_general(a, b, (((1,), (1,)), ((), ())), preferred_element_type=F32)


def _dot_tn(a, b):
    return lax.dot_general(a, b, (((0,), (0,)), ((), ())), preferred_element_type=F32)


def _sigmoid(x):
    return 1.0 / (1.0 + jnp.exp(-x))


def _row_tile(rows, cap):
    best = None
    for t in range(16, min(rows, cap) + 1, 16):
        if rows % t == 0:
            best = t
    assert best is not None, rows
    return best


def _my_place():
    return lax.axis_index("x"), lax.axis_index("y"), lax.axis_index("c")


def all_gather_list(shards, after, name):
    n = len(shards)

    def body(*refs):
        ins, outs = refs[:n], refs[n + 1:2 * n + 1]
        send_sems, recv_sems, local_sems = refs[2 * n + 1:]
        x, y, c = _my_place()
        me, sibling = (x, y, c), (x, y, 1 - c)
        chips = [(1 - x, y), (x, 1 - y), (1 - x, 1 - y)]

        def blk(a, px, py, pc):
            return outs[a].at[4 * px + 2 * py + pc]

        def copy(a, k, block, to, src=None):
            return pltpu.make_async_remote_copy(
                src_ref=blk(a, *block) if src is None else src, dst_ref=blk(a, *block),
                send_sem=send_sems.at[a * 7 + k], recv_sem=recv_sems.at[a * 7 + k],
                device_id=to, device_id_type=MESH)

        mine = [pltpu.make_async_copy(ins[a], blk(a, *me), local_sems.at[a]) for a in range(n)]
        for cp in mine:
            cp.start()
        first = []
        for a in range(n):
            first.append(copy(a, 0, me, sibling, src=ins[a]))
            first += [copy(a, 1 + j, me, (*chip, c), src=ins[a]) for j, chip in enumerate(chips)]
        for cp in first:
            cp.start()
        passed = []
        for j, chip in enumerate(chips):
            for a in range(n):
                copy(a, 1 + j, (*chip, c), me).wait_recv()
                cp = copy(a, 4 + j, (*chip, c), sibling)
                cp.start()
                passed.append(cp)
        for a in range(n):
            copy(a, 0, sibling, me).wait_recv()
            for j, chip in enumerate(chips):
                copy(a, 4 + j, (*chip, 1 - c), me).wait_recv()
        for cp in first + passed:
            cp.wait_send()
        for cp in mine:
            cp.wait()

    any_spec = pl.BlockSpec(memory_space=pl.ANY)
    return _pcall(
        body, name=name,
        out_shape=[jax.ShapeDtypeStruct((N_DEV,) + s.shape, s.dtype) for s in shards],
        in_specs=[any_spec] * (n + 1), out_specs=[any_spec] * n,
        scratch_shapes=[pltpu.SemaphoreType.DMA((7 * n,)), pltpu.SemaphoreType.DMA((7 * n,)),
                        pltpu.SemaphoreType.DMA((n,))],
    )(*shards, after)


HBM_SPEC = pl.BlockSpec(memory_space=pltpu.HBM)
SEM_SPEC = pl.BlockSpec(memory_space=pltpu.SEMAPHORE)
N_PEERS = N_DEV - 1


def _related(k):
    x, y, c = _my_place()
    px = 1 - x if k & 4 else x
    py = 1 - y if k & 2 else y
    pc = 1 - c if k & 1 else c
    return (px, py, pc), 4 * px + 2 * py + pc


def _exchange_copies(srcs, lands, send_sems, recv_sems, gather):
    x, y, c = _my_place()
    me = 4 * x + 2 * y + c
    copies = []
    for a, (src, land) in enumerate(zip(srcs, lands)):
        for k in range(1, N_DEV):
            peer, d = _related(k)
            copies.append(pltpu.make_async_remote_copy(
                src_ref=src if gather else src.at[d], dst_ref=land.at[me] if gather else land.at[k],
                send_sem=send_sems.at[a * N_PEERS + k - 1], recv_sem=recv_sems.at[a * N_PEERS + k - 1],
                device_id=peer, device_id_type=MESH))
    return copies


def exchange_start(srcs, after, gather, name):
    n = len(srcs)
    land_shapes = [((N_DEV,) + s.shape) if gather else s.shape for s in srcs]

    def body(*refs):
        send_sems, recv_sems = refs[2 * n + 1], refs[2 * n + 2]
        for cp in _exchange_copies(refs[:n], refs[n:2 * n], send_sems, recv_sems, gather):
            cp.start()
        token = refs[-1]
        token[...] = jnp.zeros_like(token)

    sems = pltpu.SemaphoreType.DMA((n * N_PEERS,))
    lands = [pltpu.with_memory_space_constraint(lax.empty(shape, s.dtype), pltpu.HBM) for shape, s in zip(land_shapes, srcs)]
    out = _pcall(
        body, name=name,
        out_shape=(sems, sems, *[pltpu.HBM(s.shape, s.dtype) for s in srcs],
                   *[pltpu.HBM(shape, s.dtype) for shape, s in zip(land_shapes, srcs)],
                   jax.ShapeDtypeStruct((SUBLANES, LANES), F32)),
        in_specs=[HBM_SPEC] * (2 * n) + [pl.BlockSpec(memory_space=pl.ANY)],
        out_specs=(SEM_SPEC, SEM_SPEC, *[HBM_SPEC] * (2 * n), pl.BlockSpec(memory_space=pltpu.VMEM)),
        input_output_aliases={i: 2 + i for i in range(2 * n)},
        compiler_params=pltpu.CompilerParams(has_side_effects=pltpu.SideEffectType.DATAFLOW_SIDE_EFFECTING),
    )(*[pltpu.with_memory_space_constraint(s, pltpu.HBM) for s in srcs], *lands, after)
    return out[0], out[1], list(out[2:2 + n]), list(out[2 + n:2 + 2 * n]), out[-1]


def exchange_wait(send_sems, recv_sems, srcs, lands, after, gather, name):
    n = len(srcs)

    def body(*refs):
        for cp in _exchange_copies(refs[:n], refs[n:2 * n], refs[2 * n], refs[2 * n + 1], gather):
            cp.wait_send()
            cp.wait_recv()

    out = _pcall(
        body, name=name,
        out_shape=(*[pltpu.HBM(s.shape, s.dtype) for s in srcs], *[pltpu.HBM(z.shape, z.dtype) for z in lands]),
        in_specs=[HBM_SPEC] * (2 * n) + [SEM_SPEC, SEM_SPEC, pl.BlockSpec(memory_space=pl.ANY)],
        out_specs=tuple([HBM_SPEC] * (2 * n)),
        input_output_aliases={i: i for i in range(2 * n)},
        compiler_params=pltpu.CompilerParams(has_side_effects=pltpu.SideEffectType.DATAFLOW_SIDE_EFFECTING),
    )(*srcs, *lands, send_sems, recv_sems, after)
    return list(out[:n]), list(out[n:])


def adamw_exchanged(me, g, land, w, m, v, name):
    rows, cols = w.shape
    tr = _row_tile(rows, 256)

    def body(me_ref, g_ref, land_ref, w_ref, m_ref, v_ref, go_ref, d_ref, mo_ref, vo_ref):
        grad = g_ref[...].astype(F32)
        for k in range(1, N_DEV):
            grad = grad + land_ref[k].astype(F32)
        delta, m_new, v_new = _adam_math(w_ref[...], grad, m_ref[...], v_ref[...])
        go_ref[...] = grad
        d_ref[...] = delta
        mo_ref[...] = m_new
        vo_ref[...] = v_new

    tile = pl.BlockSpec((tr, cols), lambda r, ix: (r, 0))
    out = jax.ShapeDtypeStruct((rows, cols), F32)
    return _pcall(
        body, name=name, out_shape=[out] * 4,
        grid_spec=pltpu.PrefetchScalarGridSpec(
            num_scalar_prefetch=1, grid=(rows // tr,),
            in_specs=[pl.BlockSpec((None, tr, cols), lambda r, ix: (ix[0], r, 0)),
                      pl.BlockSpec((N_DEV, tr, cols), lambda r, ix: (0, r, 0)), tile, tile, tile],
            out_specs=[tile] * 4),
        compiler_params=_cp(("arbitrary",)),
    )(me, g, land, w, m, v)


def _adam_math(w, g, m, v):
    m = ADAM_B1 * m + (1.0 - ADAM_B1) * g
    v = ADAM_B2 * v + (1.0 - ADAM_B2) * (g * g)
    m_hat = m / (1.0 - ADAM_B1 ** ADAM_STEP)
    v_hat = v / (1.0 - ADAM_B2 ** ADAM_STEP)
    delta = -ADAM_LR * (m_hat / (jnp.sqrt(v_hat) + ADAM_EPS) + ADAM_WD * w)
    return delta, m, v


def adamw_small(parts, w, m, v, name):
    _, rows, cols = parts.shape

    def body(p_ref, w_ref, m_ref, v_ref, go_ref, d_ref, mo_ref, vo_ref):
        grad = p_ref[0]
        for k in range(1, N_DEV):
            grad = grad + p_ref[k]
        delta, m_new, v_new = _adam_math(w_ref[...], grad, m_ref[...], v_ref[...])
        go_ref[...] = grad
        d_ref[...] = delta
        mo_ref[...] = m_new
        vo_ref[...] = v_new

    out = jax.ShapeDtypeStruct((rows, cols), F32)
    return _pcall(body, name=name, out_shape=[out] * 4, compiler_params=_cp(vmem=VMEM_BIG))(parts, w, m, v)


def adamw_plain(g, w, m, v, name):
    def body(g_ref, w_ref, m_ref, v_ref, d_ref, mo_ref, vo_ref):
        delta, m_new, v_new = _adam_math(w_ref[...], g_ref[...], m_ref[...], v_ref[...])
        d_ref[...] = delta
        mo_ref[...] = m_new
        vo_ref[...] = v_new

    out = jax.ShapeDtypeStruct(w.shape, F32)
    return _pcall(body, name=name, out_shape=[out] * 3)(g, w, m, v)


def _rms_fwd(x, g):
    r = lax.rsqrt(jnp.mean(x * x, axis=-1, keepdims=True) + NORM_EPS)
    return x * r * g


def _rms_bwd(x, g, dy):
    r = lax.rsqrt(jnp.mean(x * x, axis=-1, keepdims=True) + NORM_EPS)
    xh = x * r
    t = dy * g
    dx = r * (t - xh * jnp.mean(t * xh, axis=-1, keepdims=True))
    return dx, jnp.sum(dy * xh, axis=0, keepdims=True)


def _accumulate(ref, val, first):
    @pl.when(first)
    def _():
        ref[...] = val

    @pl.when(jnp.logical_not(first))
    def _():
        ref[...] += val


def _col_chunks(width):
    return [(c0, min(MXU_WIDTH, width - c0)) for c0 in range(0, width, MXU_WIDTH)]


ANY_SPEC = pl.BlockSpec(memory_space=pl.ANY)


def ffn_forward(h, norm, w1, w3, w2, after, tm, name):
    t_rows = h.shape[0]

    def body(h_ref, g_ref, w1_ref, w3_ref, w2_ref, _, out_ref, hn_ref, a_ref, b_ref, hid_ref):
        hn = _rms_fwd(h_ref[...], g_ref[...]).astype(BF16)
        hn_ref[...] = hn
        for c0, cw in _col_chunks(D_FF):
            a = _dot_nt(hn, w1_ref[c0:c0 + cw, :])
            b = _dot_nt(hn, w3_ref[c0:c0 + cw, :])
            a_ref[:, c0:c0 + cw] = a.astype(BF16)
            b_ref[:, c0:c0 + cw] = b.astype(BF16)
            hid_ref[:, c0:c0 + cw] = (a * _sigmoid(a) * b).astype(BF16)
        out_ref[...] = h_ref[...] + 0.5 * _dot(hid_ref[...], w2_ref[...])

    row = pl.BlockSpec((tm, D_MODEL), lambda i: (i, 0))
    hid_blk = pl.BlockSpec((tm, D_FF), lambda i: (i, 0))
    weight = _resident((D_FF, D_MODEL))
    return _pcall(
        body, name=name, grid=(t_rows // tm,),
        in_specs=[row, pl.BlockSpec((1, D_MODEL), lambda i: (0, 0)), weight, weight, weight, ANY_SPEC],
        out_specs=[row, row, hid_blk, hid_blk],
        out_shape=[jax.ShapeDtypeStruct((t_rows, D_MODEL), F32), jax.ShapeDtypeStruct((t_rows, D_MODEL), BF16),
                   jax.ShapeDtypeStruct((t_rows, D_FF), BF16), jax.ShapeDtypeStruct((t_rows, D_FF), BF16)],
        scratch_shapes=[pltpu.VMEM((tm, D_FF), BF16)],
        compiler_params=_cp(("arbitrary",), VMEM_BIG),
    )(h, norm, w1, w3, w2, after)


def _resident(shape):
    return pl.BlockSpec(shape, lambda *_: (0,) * len(shape), pipeline_mode=pl.Buffered(1))


def ffn_backward_hidden(dh, a, b, w2, after, tm, name):
    t_rows = dh.shape[0]

    def body(dh_ref, a_ref, b_ref, w2_ref, _, da_ref, db_ref, dhb_ref):
        dhb = (0.5 * dh_ref[...]).astype(BF16)
        dhb_ref[...] = dhb
        for c0, cw in _col_chunks(D_FF):
            dhid = _dot_nt(dhb, w2_ref[c0:c0 + cw, :])
            av = a_ref[:, c0:c0 + cw].astype(F32)
            bv = b_ref[:, c0:c0 + cw].astype(F32)
            s = _sigmoid(av)
            da_ref[:, c0:c0 + cw] = (dhid * bv * (s * (1.0 + av * (1.0 - s)))).astype(BF16)
            db_ref[:, c0:c0 + cw] = (dhid * (av * s)).astype(BF16)

    hid = pl.BlockSpec((tm, D_FF), lambda i: (i, 0))
    row = pl.BlockSpec((tm, D_MODEL), lambda i: (i, 0))
    return _pcall(
        body, name=name, grid=(t_rows // tm,),
        in_specs=[row, hid, hid, _resident((D_FF, D_MODEL)), ANY_SPEC],
        out_specs=[hid, hid, row],
        out_shape=[jax.ShapeDtypeStruct((t_rows, D_FF), BF16), jax.ShapeDtypeStruct((t_rows, D_FF), BF16),
                   jax.ShapeDtypeStruct((t_rows, D_MODEL), BF16)],
        compiler_params=_cp(("arbitrary",), VMEM_BIG),
    )(dh, a, b, w2, after)


def ffn_backward_input(dh, h, norm, da, db, w1, w3, after, tm, name):
    t_rows = h.shape[0]

    def body(dh_ref, h_ref, g_ref, da_ref, db_ref, w1_ref, w3_ref, _, dhin_ref, dg_ref):
        dhn = _dot(da_ref[...], w1_ref[...]) + _dot(db_ref[...], w3_ref[...])
        dx, dg = _rms_bwd(h_ref[...], g_ref[...], dhn)
        dhin_ref[...] = dh_ref[...] + dx
        _accumulate(dg_ref, dg, pl.program_id(0) == 0)

    row = pl.BlockSpec((tm, D_MODEL), lambda i: (i, 0))
    vec = pl.BlockSpec((1, D_MODEL), lambda i: (0, 0))
    hid = pl.BlockSpec((tm, D_FF), lambda i: (i, 0))
    return _pcall(
        body, name=name, grid=(t_rows // tm,),
        in_specs=[row, row, vec, hid, hid, _resident((D_FF, D_MODEL)), _resident((D_FF, D_MODEL)), ANY_SPEC],
        out_specs=[row, vec],
        out_shape=[jax.ShapeDtypeStruct((t_rows, D_MODEL), F32), jax.ShapeDtypeStruct((1, D_MODEL), F32)],
        compiler_params=_cp(("arbitrary",), VMEM_BIG),
    )(dh, h, norm, da, db, w1, w3, after)


def ffn_backward_weights(hn, dh, a, b, da, db, tm, tn, name):
    t_rows = hn.shape[0]
    ni = t_rows // tm
    kc = _row_tile(tm, 688)

    def body(hn_ref, dh_ref, a_ref, b_ref, da_ref, db_ref, dw1_ref, dw3_ref, dw2_ref, acc1, acc3, acc2):
        i = pl.program_id(1)
        parts = None
        for r0 in range(0, tm, kc):
            rows = slice(r0, r0 + kc)
            hn_v = hn_ref[rows, :]
            av = a_ref[rows, :].astype(F32)
            hid = (av * _sigmoid(av) * b_ref[rows, :].astype(F32)).astype(BF16)
            new = (_dot_tn(hn_v, da_ref[rows, :]), _dot_tn(hn_v, db_ref[rows, :]), _dot_tn(dh_ref[rows, :], hid))
            parts = new if parts is None else tuple(p + q for p, q in zip(parts, new))
        _accumulate(acc1, parts[0], i == 0)
        _accumulate(acc3, parts[1], i == 0)
        _accumulate(acc2, parts[2], i == 0)

        @pl.when(i == ni - 1)
        def _():
            dw1_ref[...] = acc1[...].T.astype(BF16)
            dw3_ref[...] = acc3[...].T.astype(BF16)
            dw2_ref[...] = acc2[...].T.astype(BF16)

    row = pl.BlockSpec((tm, D_MODEL), lambda j, i: (i, 0))
    hid_blk = pl.BlockSpec((tm, tn), lambda j, i: (i, j))
    w_row = pl.BlockSpec((tn, D_MODEL), lambda j, i: (j, 0))
    out = jax.ShapeDtypeStruct((D_FF, D_MODEL), BF16)
    return _pcall(
        body, name=name, grid=(D_FF // tn, ni),
        in_specs=[row, row, hid_blk, hid_blk, hid_blk, hid_blk],
        out_specs=[w_row, w_row, w_row], out_shape=[out, out, out],
        scratch_shapes=[pltpu.VMEM((D_MODEL, tn), F32)] * 3,
        compiler_params=_cp(("arbitrary", "arbitrary"), VMEM_BIG),
    )(hn, dh, a, b, da, db)


def mix_forward(h, norm, wing, tm, name):
    t_rows = h.shape[0]

    def body(h_ref, g_ref, w_ref, hn_ref, p_ref):
        hn = _rms_fwd(h_ref[...], g_ref[...]).astype(BF16)
        hn_ref[...] = hn
        for j in range(N_DEV):
            p_ref[:, j * IN_BLK:(j + 1) * IN_BLK] = _dot(hn, w_ref[j]).astype(BF16)

    row = pl.BlockSpec((tm, D_MODEL), lambda i: (i, 0))
    return _pcall(
        body, name=name, grid=(t_rows // tm,),
        in_specs=[row, pl.BlockSpec((1, D_MODEL), lambda i: (0, 0)),
                  pl.BlockSpec((N_DEV, D_MODEL, IN_BLK), lambda i: (0, 0, 0))],
        out_specs=[row, pl.BlockSpec((tm, IN_WIDTH), lambda i: (i, 0))],
        out_shape=[jax.ShapeDtypeStruct((t_rows, D_MODEL), BF16), jax.ShapeDtypeStruct((t_rows, IN_WIDTH), BF16)],
        compiler_params=_cp(("arbitrary",), VMEM_BIG),
    )(h, norm, wing)


def mix_backward_act(dh, h, norm, dproj, w_in_full, tm, name):
    t_rows = h.shape[0]

    def body(dh_ref, h_ref, g_ref, dp_ref, w_ref, dhin_ref, dg_ref):
        dx, dg = _rms_bwd(h_ref[...], g_ref[...], _dot_nt(dp_ref[...], w_ref[...]))
        dhin_ref[...] = dh_ref[...] + dx
        _accumulate(dg_ref, dg, pl.program_id(0) == 0)

    row = pl.BlockSpec((tm, D_MODEL), lambda i: (i, 0))
    vec = pl.BlockSpec((1, D_MODEL), lambda i: (0, 0))
    return _pcall(
        body, name=name, grid=(t_rows // tm,),
        in_specs=[row, row, vec, pl.BlockSpec((tm, IN_WIDTH), lambda i: (i, 0)), _resident((D_MODEL, IN_WIDTH))],
        out_specs=[row, vec],
        out_shape=[jax.ShapeDtypeStruct((t_rows, D_MODEL), F32), jax.ShapeDtypeStruct((1, D_MODEL), F32)],
        compiler_params=_cp(("arbitrary",), VMEM_BIG),
    )(dh, h, norm, dproj, w_in_full)


def mix_backward_weights(hn, dproj, tm, name):
    t_rows = hn.shape[0]
    ni = t_rows // tm
    per_step = 2

    kc = _row_tile(tm, 688)

    def body(hn_ref, dp_ref, dw_ref, acc):
        i = pl.program_id(1)
        part = functools.reduce(lambda u, w: u + w, [_dot_tn(hn_ref[r0:r0 + kc, :], dp_ref[r0:r0 + kc, :])
                                                    for r0 in range(0, tm, kc)])
        _accumulate(acc, part, i == 0)

        @pl.when(i == ni - 1)
        def _():
            for k in range(per_step):
                dw_ref[k] = acc[:, k * IN_BLK:(k + 1) * IN_BLK].astype(BF16)

    return _pcall(
        body, name=name, grid=(N_DEV // per_step, ni),
        in_specs=[pl.BlockSpec((tm, D_MODEL), lambda j, i: (i, 0)),
                  pl.BlockSpec((tm, per_step * IN_BLK), lambda j, i: (i, j))],
        out_specs=pl.BlockSpec((per_step, D_MODEL, IN_BLK), lambda j, i: (j, 0, 0)),
        out_shape=jax.ShapeDtypeStruct((N_DEV, D_MODEL, IN_BLK), BF16),
        scratch_shapes=[pltpu.VMEM((D_MODEL, per_step * IN_BLK), F32)],
        compiler_params=_cp(("arbitrary", "arbitrary"), VMEM_BIG),
    )(hn, dproj)


GELU_C = 0.7978845608028654
GELU_K = 0.044715


def _gelu(x):
    return 0.5 * x * (1.0 + jnp.tanh(GELU_C * (x + GELU_K * (x * x * x))))


def _gelu_and_grad(x):
    th = jnp.tanh(GELU_C * (x + GELU_K * (x * x * x)))
    val = 0.5 * x * (1.0 + th)
    grad = 0.5 * (1.0 + th) + 0.5 * x * (1.0 - th * th) * (GELU_C * (1.0 + 3.0 * GELU_K * (x * x)))
    return val, grad


def merge_forward(h, yraw, attn, proj, glu_a, glu_b, w_out, tm, name):
    t_rows = h.shape[0]

    def body(h_ref, y_ref, at_ref, gate_ref, a_ref, b_ref, wo_ref, out_ref):
        y = _gelu(y_ref[...]).astype(BF16)
        ssm = _dot(y, a_ref[...]) * _sigmoid(_dot(y, b_ref[...]))
        ga = gate_ref[:, :D_MODEL].astype(F32)
        gs = gate_ref[:, D_MODEL:].astype(F32)
        merged = _sigmoid(ga) * at_ref[...].astype(F32) + _sigmoid(gs) * ssm
        out_ref[...] = h_ref[...] + _dot(merged.astype(BF16), wo_ref[...])

    row = pl.BlockSpec((tm, D_MODEL), lambda i: (i, 0))
    glu = pl.BlockSpec((SSM_WIDTH, D_MODEL), lambda i: (0, 0))
    return _pcall(
        body, name=name, grid=(t_rows // tm,),
        in_specs=[row, pl.BlockSpec((tm, SSM_WIDTH), lambda i: (i, 0)), row,
                  pl.BlockSpec((tm, 2 * D_MODEL), lambda i: (i, 1)), glu, glu,
                  pl.BlockSpec((D_MODEL, D_MODEL), lambda i: (0, 0))],
        out_specs=row, out_shape=jax.ShapeDtypeStruct((t_rows, D_MODEL), F32),
        compiler_params=_cp(("arbitrary",), VMEM_BIG),
    )(h, yraw, attn, proj, glu_a, glu_b, w_out)


def merge_backward(dh, yraw, attn, proj, glu_a, glu_b, w_out, after, tm, name):
    t_rows = dh.shape[0]

    def body(dh_ref, y_ref, at_ref, gate_ref, a_ref, b_ref, wo_ref, _,
             dat_ref, dy_ref, dgate_ref, d16_ref, mg_ref, y16_ref, dya_ref, dyb_ref):
        d16 = dh_ref[...].astype(BF16)
        d16_ref[...] = d16
        gel, dgel = _gelu_and_grad(y_ref[...].astype(F32))
        y16 = gel.astype(BF16)
        y16_ref[...] = y16
        dy = None
        for c0, cw in _col_chunks(D_MODEL):
            cols = slice(c0, c0 + cw)
            gcols = slice(D_MODEL + c0, D_MODEL + c0 + cw)
            dmerged = _dot_nt(d16, wo_ref[cols, :])
            ya = _dot(y16, a_ref[:, cols])
            sb = _sigmoid(_dot(y16, b_ref[:, cols]))
            ssm = ya * sb
            sa = _sigmoid(gate_ref[:, cols].astype(F32))
            ss = _sigmoid(gate_ref[:, gcols].astype(F32))
            attn_v = at_ref[:, cols].astype(F32)
            mg_ref[:, cols] = (sa * attn_v + ss * ssm).astype(BF16)
            dat_ref[:, cols] = (dmerged * sa).astype(BF16)
            dgate_ref[:, cols] = (dmerged * attn_v * sa * (1.0 - sa)).astype(BF16)
            dgate_ref[:, gcols] = (dmerged * ssm * ss * (1.0 - ss)).astype(BF16)
            dssm = dmerged * ss
            dya = (dssm * sb).astype(BF16)
            dyb = (dssm * ya * sb * (1.0 - sb)).astype(BF16)
            dya_ref[:, cols] = dya
            dyb_ref[:, cols] = dyb
            part = _dot_nt(dya, a_ref[:, cols]) + _dot_nt(dyb, b_ref[:, cols])
            dy = part if dy is None else dy + part
        dy_ref[...] = (dy * dgel).astype(BF16)

    row = pl.BlockSpec((tm, D_MODEL), lambda i: (i, 0))
    ssm_row = pl.BlockSpec((tm, SSM_WIDTH), lambda i: (i, 0))
    gates = pl.BlockSpec((tm, 2 * D_MODEL), lambda i: (i, 1))
    wide = jax.ShapeDtypeStruct((t_rows, D_MODEL), BF16)
    narrow = jax.ShapeDtypeStruct((t_rows, SSM_WIDTH), BF16)
    return _pcall(
        body, name=name, grid=(t_rows // tm,),
        in_specs=[row, ssm_row, row, gates, _resident((SSM_WIDTH, D_MODEL)), _resident((SSM_WIDTH, D_MODEL)),
                  _resident((D_MODEL, D_MODEL)), ANY_SPEC],
        out_specs=[row, ssm_row, gates, row, row, ssm_row, row, row],
        out_shape=[wide, narrow, jax.ShapeDtypeStruct((t_rows, IN_WIDTH), BF16), wide, wide, narrow, wide, wide],
        compiler_params=_cp(("arbitrary",), VMEM_BIG),
    )(dh, yraw, attn, proj, glu_a, glu_b, w_out, after)


def merge_backward_weights(d16, merged, y16, dya, dyb, tm, name):
    t_rows = d16.shape[0]

    def body(d_ref, mg_ref, y_ref, dya_ref, dyb_ref, dwo_ref, da_ref, db_ref):
        first = pl.program_id(0) == 0
        y16 = y_ref[...]
        _accumulate(dwo_ref, _dot_tn(mg_ref[...], d_ref[...]), first)
        _accumulate(da_ref, _dot_tn(y16, dya_ref[...]), first)
        _accumulate(db_ref, _dot_tn(y16, dyb_ref[...]), first)

    row = pl.BlockSpec((tm, D_MODEL), lambda i: (i, 0))
    ssm_row = pl.BlockSpec((tm, SSM_WIDTH), lambda i: (i, 0))
    glu = pl.BlockSpec((SSM_WIDTH, D_MODEL), lambda i: (0, 0))
    wo = pl.BlockSpec((D_MODEL, D_MODEL), lambda i: (0, 0))
    return _pcall(
        body, name=name, grid=(t_rows // tm,),
        in_specs=[row, row, ssm_row, row, row], out_specs=[wo, glu, glu],
        out_shape=[jax.ShapeDtypeStruct((D_MODEL, D_MODEL), F32), jax.ShapeDtypeStruct((SSM_WIDTH, D_MODEL), F32),
                   jax.ShapeDtypeStruct((SSM_WIDTH, D_MODEL), F32)],
        compiler_params=_cp(("arbitrary",), VMEM_BIG),
    )(d16, merged, y16, dya, dyb)


def final_loss_backward(h, target, norm, seq, tm, name):
    t_rows = h.shape[0]
    tiles_per_example = (seq + N_META) // tm

    def body(h_ref, t_ref, g_ref, dh_ref, loss_ref, dg_ref):
        i = pl.program_id(0)
        x = h_ref[...]
        g = g_ref[...]
        r = lax.rsqrt(jnp.mean(x * x, axis=-1, keepdims=True) + NORM_EPS)
        xh = x * r
        pos = lax.broadcasted_iota(jnp.int32, (tm, 1), 0) + (i % tiles_per_example) * tm
        diff = jnp.where(pos < seq, xh * g - t_ref[...], 0.0)
        part = 0.5 * jnp.sum(jnp.sum(diff * diff, axis=-1, keepdims=True), axis=0, keepdims=True) / D_MODEL
        dy = diff / D_MODEL
        t = dy * g
        dh_ref[...] = r * (t - xh * jnp.mean(t * xh, axis=-1, keepdims=True))
        _accumulate(loss_ref, jnp.broadcast_to(part, (1, LANES)), i == 0)
        _accumulate(dg_ref, jnp.sum(dy * xh, axis=0, keepdims=True), i == 0)

    row = pl.BlockSpec((tm, D_MODEL), lambda i: (i, 0))
    vec = pl.BlockSpec((1, D_MODEL), lambda i: (0, 0))
    return _pcall(
        body, name=name, grid=(t_rows // tm,),
        in_specs=[row, row, vec],
        out_specs=[row, pl.BlockSpec((1, LANES), lambda i: (0, 0)), vec],
        out_shape=[jax.ShapeDtypeStruct((t_rows, D_MODEL), F32), jax.ShapeDtypeStruct((1, LANES), F32),
                   jax.ShapeDtypeStruct((1, D_MODEL), F32)],
        compiler_params=_cp(("arbitrary",), VMEM_BIG),
    )(h, target, norm)


ATTN_SCALE = HEAD_DIM ** -0.5
STACK_HEADS = (0, 2, 1, 3)
META_PAD = LANES - N_META


def _lane_half(shape, hf):
    lane = lax.broadcasted_iota(jnp.int32, shape, 1)
    return (lane < HEAD_DIM) if hf == 0 else (lane >= HEAD_DIM)


def _kv_variants(ref, rows, kh, pad_rows=0):
    tile = kh // 2
    t = ref[rows, tile * LANES:(tile + 1) * LANES].astype(F32)
    swapped = pltpu.roll(t, HEAD_DIM, axis=1)
    at_low, at_high = (t, swapped) if kh % 2 == 0 else (swapped, t)
    lo = jnp.where(_lane_half(t.shape, 0), at_low, 0.0).astype(BF16)
    hi = jnp.where(_lane_half(t.shape, 1), at_high, 0.0).astype(BF16)
    if pad_rows:
        zeros = jnp.zeros((pad_rows, LANES), BF16)
        lo, hi = jnp.concatenate([lo, zeros], axis=0), jnp.concatenate([hi, zeros], axis=0)
    return lo, hi


def _key_tiles(ref, key_rows, kh):
    return [_kv_variants(ref, r, kh, META_PAD if i == len(key_rows) - 1 else 0) for i, r in enumerate(key_rows)]


def _to_kv_lanes(lo, hi, kh):
    lo = jnp.where(_lane_half(lo.shape, 0), lo, 0.0)
    hi = jnp.where(_lane_half(hi.shape, 1), hi, 0.0)
    if kh % 2 == 0:
        return lo + pltpu.roll(hi, HEAD_DIM, axis=1)
    return pltpu.roll(lo, HEAD_DIM, axis=1) + hi


def _stacked(ref, rows, kh):
    col = kh * 2 * LANES
    return jnp.concatenate([ref[rows, col:col + LANES], ref[rows, col + LANES:col + 2 * LANES]], axis=0)


def _sink_column(sink_ref, kh, nq):
    row = lax.broadcasted_iota(jnp.int32, (4 * nq, 1), 0)
    col = jnp.zeros((4 * nq, 1), F32)
    for quarter, g in enumerate(STACK_HEADS):
        col = jnp.where(row // nq == quarter, sink_ref[0, kh * Q_PER_KV + g], col)
    return col


def _softmax_parts(qs, key_tiles, masks, sink):
    scores = []
    for (k_lo, k_hi), mask in zip(key_tiles, masks):
        s = jnp.concatenate([_dot_nt(qs, k_lo), _dot_nt(qs, k_hi)], axis=0) * ATTN_SCALE
        scores.append(s if mask is None else jnp.where(mask, s, NEG_INF))
    m = jnp.maximum(_row_reduce(scores, jnp.maximum, jnp.max), sink)
    probs = [jnp.exp(s - m) for s in scores]
    e_sink = jnp.exp(sink - m)
    den = _row_sums(probs) + e_sink
    return probs, 1.0 / den, e_sink


def _row_reduce(tiles, combine, reduce):
    chunks = [t[:, c:c + LANES] for t in tiles for c in range(0, t.shape[-1], LANES)]
    return reduce(functools.reduce(combine, chunks), axis=-1, keepdims=True)


def _row_sums(tiles):
    return _row_reduce(tiles, lambda u, w: u + w, jnp.sum)


def _band_mask(nq, first):
    keys = BLOCK if first else 2 * BLOCK
    qi = lax.broadcasted_iota(jnp.int32, (4 * nq, keys), 0) % nq
    kj = lax.broadcasted_iota(jnp.int32, (4 * nq, keys), 1)
    if first:
        return kj <= qi
    return jnp.logical_and(kj > qi, kj <= qi + BLOCK)


def _meta_mask(nq, causal):
    qi = lax.broadcasted_iota(jnp.int32, (4 * nq, LANES), 0) % nq
    kj = lax.broadcasted_iota(jnp.int32, (4 * nq, LANES), 1)
    return jnp.logical_and(kj < N_META, kj <= qi) if causal else kj < N_META


def _attention_schedule(seq, queries, carry):
    meta_rows = pl.ds(seq, N_META)
    meta_ok = _meta_mask(BLOCK, False)
    carry = queries(pl.ds(0, BLOCK), BLOCK, [pl.ds(0, BLOCK), meta_rows], [_band_mask(BLOCK, True), meta_ok], carry)

    def block(n, c):
        r0 = pl.multiple_of(n * BLOCK, BLOCK)
        p0 = pl.multiple_of((n - 1) * BLOCK, BLOCK)
        return queries(pl.ds(r0, BLOCK), BLOCK, [pl.ds(p0, 2 * BLOCK), meta_rows], [_band_mask(BLOCK, False), meta_ok], c)

    carry = lax.fori_loop(1, seq // BLOCK, block, carry)
    return queries(meta_rows, N_META, [meta_rows], [_meta_mask(N_META, True)], carry)


def attention_forward(proj3, sinks, seq, name):
    n_b, n_l, _ = proj3.shape

    def body(sink_ref, q_ref, k_ref, v_ref, o_ref):
        def queries(q_rows, nq, key_rows, masks, carry):
            for kh in range(N_KV_HEADS):
                ks = _key_tiles(k_ref, key_rows, kh)
                vs = _key_tiles(v_ref, key_rows, kh)
                qs = _stacked(q_ref, q_rows, kh)
                probs, inv, _ = _softmax_parts(qs, ks, masks, _sink_column(sink_ref, kh, nq))
                probs = [p.astype(BF16) for p in probs]
                o_lo = functools.reduce(lambda u, w: u + w, [_dot(p[:2 * nq], v_lo) for p, (v_lo, _) in zip(probs, vs)])
                o_hi = functools.reduce(lambda u, w: u + w, [_dot(p[2 * nq:], v_hi) for p, (_, v_hi) in zip(probs, vs)])
                out = (o_lo * inv[:2 * nq] + o_hi * inv[2 * nq:]).astype(BF16)
                col = kh * 2 * LANES
                o_ref[q_rows, col:col + LANES] = out[:nq]
                o_ref[q_rows, col + LANES:col + 2 * LANES] = out[nq:]
            return carry

        _attention_schedule(seq, queries, 0)

    return _pcall(
        body, name=name, grid=(n_b,),
        in_specs=[pl.BlockSpec(memory_space=pltpu.SMEM),
                  pl.BlockSpec((None, n_l, D_MODEL), lambda b: (b, 0, 0)),
                  pl.BlockSpec((None, n_l, KV_WIDTH), lambda b: (b, 0, D_MODEL // KV_WIDTH)),
                  pl.BlockSpec((None, n_l, KV_WIDTH), lambda b: (b, 0, D_MODEL // KV_WIDTH + 1))],
        out_specs=pl.BlockSpec((None, n_l, D_MODEL), lambda b: (b, 0, 0)),
        out_shape=jax.ShapeDtypeStruct((n_b, n_l, D_MODEL), BF16),
        compiler_params=_cp(("arbitrary",), VMEM_BIG),
    )(sinks, proj3, proj3, proj3)


def attention_backward(proj3, dattn3, dproj3, sinks, after, seq, name):
    n_b, n_l, _ = proj3.shape
    qkv_width = D_MODEL + 2 * KV_WIDTH

    def body(sink_ref, q_ref, k_ref, v_ref, do_ref, _, __, dqkv_ref, dsink_ref, dk_ref, dv_ref):
        dk_ref[...] = jnp.zeros_like(dk_ref)
        dv_ref[...] = jnp.zeros_like(dv_ref)
        sub = lax.broadcasted_iota(jnp.int32, (SUBLANES, LANES), 0)
        lane = lax.broadcasted_iota(jnp.int32, (SUBLANES, LANES), 1)

        def queries(q_rows, nq, key_rows, masks, dsink):
            for kh in range(N_KV_HEADS):
                ks = _key_tiles(k_ref, key_rows, kh)
                vs = _key_tiles(v_ref, key_rows, kh)
                qs = _stacked(q_ref, q_rows, kh)
                dos = _stacked(do_ref, q_rows, kh)
                probs, inv, e_sink = _softmax_parts(qs, ks, masks, _sink_column(sink_ref, kh, nq))
                probs = [p * inv for p in probs]
                dps = [jnp.concatenate([_dot_nt(dos, v_lo), _dot_nt(dos, v_hi)], axis=0) for v_lo, v_hi in vs]
                delta = _row_sums([p * dp for p, dp in zip(probs, dps)])
                d_sink = -(e_sink * inv) * delta
                for quarter, g in enumerate(STACK_HEADS):
                    d_here = jnp.sum(d_sink[quarter * nq:(quarter + 1) * nq], axis=0, keepdims=True)
                    dsink = dsink + jnp.where(jnp.logical_and(sub == 0, lane == kh * Q_PER_KV + g), d_here, 0.0)
                dq = None
                tile = slice((kh // 2) * LANES, (kh // 2 + 1) * LANES)
                for r, p, dp, (k_lo, k_hi) in zip(key_rows, probs, dps, ks):
                    ds = (p * (dp - delta)).astype(BF16)
                    p16 = p.astype(BF16)
                    dq_x = _dot(ds[:2 * nq], k_lo) + _dot(ds[2 * nq:], k_hi)
                    dq = dq_x if dq is None else dq + dq_x
                    d_k = _to_kv_lanes(_dot_tn(ds[:2 * nq], qs), _dot_tn(ds[2 * nq:], qs), kh) * ATTN_SCALE
                    d_v = _to_kv_lanes(_dot_tn(p16[:2 * nq], dos), _dot_tn(p16[2 * nq:], dos), kh)
                    n_keys = r.size
                    dk_ref[r, tile] += d_k[:n_keys]
                    dv_ref[r, tile] += d_v[:n_keys]
                dq = (dq * ATTN_SCALE).astype(BF16)
                col = kh * 2 * LANES
                dqkv_ref[q_rows, col:col + LANES] = dq[:nq]
                dqkv_ref[q_rows, col + LANES:col + 2 * LANES] = dq[nq:]
            return dsink

        dsink_ref[...] = _attention_schedule(seq, queries, jnp.zeros((SUBLANES, LANES), F32))
        dqkv_ref[:, D_MODEL:D_MODEL + KV_WIDTH] = dk_ref[...].astype(BF16)
        dqkv_ref[:, D_MODEL + KV_WIDTH:] = dv_ref[...].astype(BF16)

    return _pcall(
        body, name=name, grid=(n_b,),
        in_specs=[pl.BlockSpec(memory_space=pltpu.SMEM),
                  pl.BlockSpec((None, n_l, D_MODEL), lambda b: (b, 0, 0)),
                  pl.BlockSpec((None, n_l, KV_WIDTH), lambda b: (b, 0, D_MODEL // KV_WIDTH)),
                  pl.BlockSpec((None, n_l, KV_WIDTH), lambda b: (b, 0, D_MODEL // KV_WIDTH + 1)),
                  pl.BlockSpec((None, n_l, D_MODEL), lambda b: (b, 0, 0)),
                  ANY_SPEC, ANY_SPEC],
        out_specs=[pl.BlockSpec((None, n_l, qkv_width), lambda b: (b, 0, 0)),
                   pl.BlockSpec((None, SUBLANES, LANES), lambda b: (b, 0, 0))],
        out_shape=[jax.ShapeDtypeStruct(dproj3.shape, BF16), jax.ShapeDtypeStruct((n_b, SUBLANES, LANES), F32)],
        scratch_shapes=[pltpu.VMEM((n_l, KV_WIDTH), F32), pltpu.VMEM((n_l, KV_WIDTH), F32)],
        input_output_aliases={5: 0},
        compiler_params=_cp(("arbitrary",), VMEM_BIG),
    )(sinks, proj3, proj3, proj3, dattn3, dproj3, after)


TAB_ROWS = 8
SCAN_UNROLL = 4


def _cmul(ar, ai, br, bi):
    return ar * br - ai * bi, ar * bi + ai * br


def _discretise(ar, ai, ls):
    step = jnp.exp(ls)
    mag = jnp.exp(ar * step)
    ang = ai * step
    cos, sin = jnp.cos(ang), jnp.sin(ang)
    lr, li = mag * cos, mag * sin
    den = ar * ar + ai * ai
    nr, ni = lr - 1.0, li
    cr = (nr * ar + ni * ai) / den
    ci = (ni * ar - nr * ai) / den
    return step, mag, lr, li, den, nr, ni, cr, ci


def _scan_tables(lr, li, reverse):
    n = lr.shape[-1]
    pw = [(lr, li)]
    for _ in range(SUBLANES - 1):
        pw.append(_cmul(pw[-1][0], pw[-1][1], lr, li))
    row = lax.broadcasted_iota(jnp.int32, (SUBLANES, n), 0)
    out = []
    for d in (1, 2, 4):
        ok = (row + d <= SUBLANES - 1) if reverse else (row >= d)
        out += [jnp.where(ok, pw[d - 1][0], 0.0), jnp.where(ok, pw[d - 1][1], 0.0)]
    cr = jnp.zeros((SUBLANES, n), F32)
    ci = jnp.zeros((SUBLANES, n), F32)
    for r in range(SUBLANES):
        e = (SUBLANES - r) if reverse else (r + 1)
        cr = jnp.where(row == r, pw[e - 1][0], cr)
        ci = jnp.where(row == r, pw[e - 1][1], ci)
    return out + [cr, ci]


def ssm_prepare(ar, ai, ls, br_t, bi_t, name):
    def body(ar_ref, ai_ref, ls_ref, br_ref, bi_ref, bbr_ref, bbi_ref, tf_ref, tr_ref):
        _, _, lr, li, _, _, _, cr, ci = _discretise(ar_ref[...], ai_ref[...], ls_ref[...])
        br, bi = br_ref[...], bi_ref[...]
        bbr_ref[...] = cr * br - ci * bi
        bbi_ref[...] = cr * bi + ci * br
        for k, t in enumerate(_scan_tables(lr, li, False)):
            tf_ref[k] = t
        for k, t in enumerate(_scan_tables(lr, -li, True)):
            tr_ref[k] = t

    return _pcall(
        body, name=name,
        out_shape=[jax.ShapeDtypeStruct((SSM_GROUP, N_STATES), F32), jax.ShapeDtypeStruct((SSM_GROUP, N_STATES), F32),
                   jax.ShapeDtypeStruct((TAB_ROWS, SUBLANES, N_STATES), F32),
                   jax.ShapeDtypeStruct((TAB_ROWS, SUBLANES, N_STATES), F32)],
    )(ar, ai, ls, br_t, bi_t)


def ssm_param_backward(ar, ai, ls, br_t, bi_t, dlr_p, dli_p, dbbr, dbbi, group_sum, name):
    def body(ar_ref, ai_ref, ls_ref, br_ref, bi_ref, dlr_ref, dli_ref, dbbr_ref, dbbi_ref, gs_ref,
             dar_ref, dai_ref, dls_ref, dbr_ref, dbi_ref):
        ar, ai = ar_ref[...], ai_ref[...]
        step, mag, lr, li, den, nr, ni, cr, ci = _discretise(ar, ai, ls_ref[...])
        br, bi, dbbr_v, dbbi_v = br_ref[...], bi_ref[...], dbbr_ref[...], dbbi_ref[...]
        dbr_ref[...] = cr * dbbr_v + ci * dbbi_v
        dbi_ref[...] = cr * dbbi_v - ci * dbbr_v
        dcr = jnp.sum(dbbr_v * br + dbbi_v * bi, axis=0, keepdims=True)
        dci = jnp.sum(dbbi_v * br - dbbr_v * bi, axis=0, keepdims=True)
        dnr = (dcr * ar - dci * ai) / den
        dni = (dcr * ai + dci * ar) / den
        dden = -(cr * dcr + ci * dci) / den
        dar = (dcr * nr + dci * ni) / den + dden * 2.0 * ar
        dai = (dcr * ni - dci * nr) / den + dden * 2.0 * ai
        dlr = jnp.sum(dlr_ref[...], axis=0, keepdims=True) + dnr
        dli = jnp.sum(dli_ref[...], axis=0, keepdims=True) + dni
        dmag = (dlr * lr + dli * li) / mag
        dang = dli * lr - dlr * li
        dar_ref[...] = dar + dmag * mag * step
        dai_ref[...] = dai + dang * step
        dstep = dmag * mag * ar + dang * ai
        dls_ref[...] = jnp.dot(dstep * step, gs_ref[...], preferred_element_type=F32, precision=lax.Precision.HIGHEST)

    vec = jax.ShapeDtypeStruct((1, N_STATES), F32)
    mat = jax.ShapeDtypeStruct((SSM_GROUP, N_STATES), F32)
    return _pcall(body, name=name, out_shape=[vec, vec, jax.ShapeDtypeStruct((1, LANES), F32), mat, mat])(
        ar, ai, ls, br_t, bi_t, dlr_p, dli_p, dbbr, dbbi, group_sum)


def _scan_rows(a, b, tabs, carry, reverse):
    for k, d in enumerate((1, 2, 4)):
        shift = SUBLANES - d if reverse else d
        sr, si = pltpu.roll(a, shift, axis=0), pltpu.roll(b, shift, axis=0)
        pr, pi = _cmul(tabs[2 * k], tabs[2 * k + 1], sr, si)
        a, b = a + pr, b + pi
    pr, pi = _cmul(tabs[6], tabs[7], carry[0], carry[1])
    return a + pr, b + pi


def _time_groups(seq, reverse):
    meta = [seq + SUBLANES * g for g in range(N_META // SUBLANES)]
    return meta[::-1] if reverse else meta


def ssm_forward_scan(proj3, b_comb, tabf, c_comb, dvec, seq, name):
    n_b, n_l, _ = proj3.shape
    u_blk = (D_MODEL + 2 * KV_WIDTH) // LANES

    def body(u_ref, b_ref, tab_ref, c_ref, d_ref, x_ref, y_ref, bu, xs):
        j = pl.program_id(1)
        u = u_ref[...]
        bu[...] = _dot(u, b_ref[...])
        tabs = [tab_ref[k] for k in range(TAB_ROWS)]

        def group(r0, carry):
            rows = pl.ds(r0, SUBLANES)
            a, b = _scan_rows(bu[rows, :SCAN_COLS], bu[rows, SCAN_COLS:], tabs, carry, False)
            xs[rows, :SCAN_COLS] = a
            xs[rows, SCAN_COLS:] = b
            return (jnp.broadcast_to(a[SUBLANES - 1:, :], a.shape), jnp.broadcast_to(b[SUBLANES - 1:, :], b.shape))

        zero = jnp.zeros((SUBLANES, SCAN_COLS), F32)
        carry = (zero, zero)
        for r0 in _time_groups(seq, False):
            carry = group(r0, carry)
        span = SCAN_UNROLL * SUBLANES

        def groups(t, c):
            for k in range(SCAN_UNROLL):
                c = group(pl.multiple_of(t * span, span) + k * SUBLANES, c)
            return c

        lax.fori_loop(0, seq // span, groups, carry)
        x16 = xs[...].astype(BF16)
        x_ref[...] = x16
        contrib = _dot(x16, c_ref[...])

        @pl.when(j % 2 == 0)
        def _():
            y_ref[...] = contrib + d_ref[...] * u.astype(F32)

        @pl.when(j % 2 == 1)
        def _():
            y_ref[...] += contrib

    return _pcall(
        body, name=name, grid=(n_b, N_SCAN_BLK),
        in_specs=[pl.BlockSpec((None, n_l, LANES), lambda b, j: (b, 0, u_blk + j // 2)),
                  pl.BlockSpec((None, LANES, 2 * SCAN_COLS), lambda b, j: (j, 0, 0)),
                  pl.BlockSpec((TAB_ROWS, SUBLANES, SCAN_COLS), lambda b, j: (0, 0, j)),
                  pl.BlockSpec((None, 2 * SCAN_COLS, LANES), lambda b, j: (j, 0, 0)),
                  pl.BlockSpec((1, LANES), lambda b, j: (0, j // 2))],
        out_specs=[pl.BlockSpec((None, n_l, 2 * SCAN_COLS), lambda b, j: (b, 0, j)),
                   pl.BlockSpec((None, n_l, LANES), lambda b, j: (b, 0, j // 2))],
        out_shape=[jax.ShapeDtypeStruct((n_b, n_l, 2 * N_STATES), BF16),
                   jax.ShapeDtypeStruct((n_b, n_l, SSM_WIDTH), F32)],
        scratch_shapes=[pltpu.VMEM((n_l, 2 * SCAN_COLS), F32)] * 2,
        compiler_params=_cp(("arbitrary", "arbitrary"), VMEM_BIG),
    )(proj3, b_comb, tabf, c_comb, dvec)


def ssm_backward_scan(dyraw3, xs3, dproj3, c_comb_t, tabr, b_comb_t, dvec, seq, name):
    n_b, n_l, _ = xs3.shape
    u_blk = (D_MODEL + 2 * KV_WIDTH) // LANES

    def body(dy_ref, x_ref, _, c_ref, tab_ref, b_ref, d_ref, du_ref, g_ref, dlr_ref, dli_ref, dx, gs, xs, du_acc):
        j = pl.program_id(1)
        dy = dy_ref[...]
        dx[...] = _dot(dy, c_ref[...])
        xs[...] = x_ref[...].astype(F32)
        tabs = [tab_ref[k] for k in range(TAB_ROWS)]
        last_row = lax.broadcasted_iota(jnp.int32, (SUBLANES, SCAN_COLS), 0) == SUBLANES - 1

        def group(r0, state):
            cr, ci, acc_r, acc_i = state
            rows = pl.ds(r0, SUBLANES)
            a, b = _scan_rows(dx[rows, :SCAN_COLS], dx[rows, SCAN_COLS:], tabs, (cr, ci), True)
            gs[rows, :SCAN_COLS] = a
            gs[rows, SCAN_COLS:] = b
            na = jnp.where(last_row, cr, pltpu.roll(a, SUBLANES - 1, axis=0))
            nb = jnp.where(last_row, ci, pltpu.roll(b, SUBLANES - 1, axis=0))
            xa, xb = xs[rows, :SCAN_COLS], xs[rows, SCAN_COLS:]
            return (jnp.broadcast_to(a[:1, :], a.shape), jnp.broadcast_to(b[:1, :], b.shape),
                    acc_r + na * xa + nb * xb, acc_i + nb * xa - na * xb)

        zero = jnp.zeros((SUBLANES, SCAN_COLS), F32)
        span = SCAN_UNROLL * SUBLANES
        n_spans = seq // span

        def groups(t, s):
            for k in reversed(range(SCAN_UNROLL)):
                s = group(pl.multiple_of((n_spans - 1 - t) * span, span) + k * SUBLANES, s)
            return s

        state = lax.fori_loop(0, n_spans, groups, (zero, zero, zero, zero))
        for r0 in _time_groups(seq, True):
            state = group(r0, state)
        dlr_ref[...] = state[2]
        dli_ref[...] = state[3]
        g16 = gs[...].astype(BF16)
        g_ref[...] = g16
        contrib = _dot(g16, b_ref[...])

        @pl.when(j % 2 == 0)
        def _():
            du_acc[...] = contrib + d_ref[...] * dy.astype(F32)

        @pl.when(j % 2 == 1)
        def _():
            du_ref[...] = (du_acc[...] + contrib).astype(BF16)

    state_blk = pl.BlockSpec((None, n_l, 2 * SCAN_COLS), lambda b, j: (b, 0, j))
    dl_blk = pl.BlockSpec((None, SUBLANES, SCAN_COLS), lambda b, j: (b, 0, j))
    return _pcall(
        body, name=name, grid=(n_b, N_SCAN_BLK),
        in_specs=[pl.BlockSpec((None, n_l, LANES), lambda b, j: (b, 0, j // 2)), state_blk,
                  pl.BlockSpec(memory_space=pl.ANY),
                  pl.BlockSpec((None, LANES, 2 * SCAN_COLS), lambda b, j: (j, 0, 0)),
                  pl.BlockSpec((TAB_ROWS, SUBLANES, SCAN_COLS), lambda b, j: (0, 0, j)),
                  pl.BlockSpec((None, 2 * SCAN_COLS, LANES), lambda b, j: (j, 0, 0)),
                  pl.BlockSpec((1, LANES), lambda b, j: (0, j // 2))],
        out_specs=[pl.BlockSpec((None, n_l, LANES), lambda b, j: (b, 0, u_blk + j // 2)), state_blk, dl_blk, dl_blk],
        out_shape=[jax.ShapeDtypeStruct(dproj3.shape, BF16), jax.ShapeDtypeStruct((n_b, n_l, 2 * N_STATES), BF16),
                   jax.ShapeDtypeStruct((n_b, SUBLANES, N_STATES), F32), jax.ShapeDtypeStruct((n_b, SUBLANES, N_STATES), F32)],
        scratch_shapes=[pltpu.VMEM((n_l, 2 * SCAN_COLS), F32)] * 3 + [pltpu.VMEM((n_l, LANES), F32)],
        input_output_aliases={2: 0},
        compiler_params=_cp(("arbitrary", "arbitrary"), VMEM_BIG),
    )(dyraw3, xs3, dproj3, c_comb_t, tabr, b_comb_t, dvec)


def ssm_param_grads(proj, gs, xs, dyraw, tm, name):
    t_rows = proj.shape[0]
    ni = t_rows // tm
    u_blk = (D_MODEL + 2 * KV_WIDTH) // LANES
    width = 2 * SCAN_COLS

    def body(u_ref, g_ref, x_ref, dy_ref, db_ref, dc_ref, dd_ref):
        cb, i = pl.program_id(0), pl.program_id(1)
        u, dy = u_ref[...], dy_ref[...]
        _accumulate(db_ref, _dot_tn(u, g_ref[...]), i == 0)
        _accumulate(dc_ref, _dot_tn(x_ref[...], dy), i == 0)

        @pl.when(cb % 2 == 0)
        def _():
            _accumulate(dd_ref, jnp.sum(dy.astype(F32) * u.astype(F32), axis=0, keepdims=True), i == 0)

    return _pcall(
        body, name=name, grid=(N_SCAN_BLK, ni),
        in_specs=[pl.BlockSpec((tm, LANES), lambda cb, i: (i, u_blk + cb // 2)),
                  pl.BlockSpec((tm, width), lambda cb, i: (i, cb)),
                  pl.BlockSpec((tm, width), lambda cb, i: (i, cb)),
                  pl.BlockSpec((tm, LANES), lambda cb, i: (i, cb // 2))],
        out_specs=[pl.BlockSpec((None, LANES, width), lambda cb, i: (cb, 0, 0)),
                   pl.BlockSpec((None, width, LANES), lambda cb, i: (cb, 0, 0)),
                   pl.BlockSpec((1, LANES), lambda cb, i: (0, cb // 2))],
        out_shape=[jax.ShapeDtypeStruct((N_SCAN_BLK, LANES, width), F32),
                   jax.ShapeDtypeStruct((N_SCAN_BLK, width, LANES), F32), jax.ShapeDtypeStruct((1, SSM_WIDTH), F32)],
        compiler_params=_cp(("arbitrary", "arbitrary"), VMEM_BIG),
    )(proj, gs, xs, dyraw)


def sum_leading(x, name):
    def body(x_ref, o_ref):
        acc = x_ref[0]
        for k in range(1, x.shape[0]):
            acc = acc + x_ref[k]
        o_ref[...] = acc

    return _pcall(body, name=name, out_shape=jax.ShapeDtypeStruct(x.shape[1:], x.dtype))(x)


WEIGHTS = ['meta_tokens', 'ffn1_norm', 'ffn1_w1', 'ffn1_w3', 'ffn1_w2', 'mix_norm', 'w_in', 'attn_sinks', 'ssm_a_re',
           'ssm_a_im', 'ssm_log_step', 'ssm_b_re', 'ssm_b_im', 'ssm_c_re', 'ssm_c_im', 'ssm_d', 'ssm_glu_a', 'ssm_glu_b',
           'w_out', 'ffn2_norm', 'ffn2_w1', 'ffn2_w3', 'ffn2_w2', 'final_norm']
SHARDED = ['ffn1_w1', 'ffn1_w3', 'ffn1_w2', 'ffn2_w1', 'ffn2_w3', 'ffn2_w2', 'w_in', 'ssm_glu_a', 'ssm_glu_b', 'w_out']
REPLICATED = ['ffn1_norm', 'mix_norm', 'ffn2_norm', 'final_norm', 'attn_sinks', 'ssm_a_re', 'ssm_a_im', 'ssm_log_step',
              'ssm_b_re', 'ssm_b_im', 'ssm_c_re', 'ssm_c_im', 'ssm_d']
PACK_COLS = 1024


def _pack(arrays):
    parts = []
    for a in arrays:
        flat = a.reshape(-1)
        chunk = SUBLANES * PACK_COLS
        padded = -(-flat.shape[0] // chunk) * chunk
        parts.append(jnp.pad(flat, (0, padded - flat.shape[0])).reshape(-1, PACK_COLS))
    return jnp.concatenate(parts, axis=0)


def _unpack(packed, shapes):
    out, row = [], 0
    for shape in shapes:
        size = 1
        for s in shape:
            size *= s
        chunk = SUBLANES * PACK_COLS
        rows = -(-size // chunk) * SUBLANES
        out.append(packed[row:row + rows].reshape(-1)[:size].reshape(shape))
        row += rows
    return out


def kernel(x, meta_tokens, ffn1_norm, ffn1_w1, ffn1_w3, ffn1_w2, mix_norm, w_in, attn_sinks, ssm_a_re, ssm_a_im, ssm_log_step, ssm_b_re, ssm_b_im, ssm_c_re, ssm_c_im, ssm_d, ssm_glu_a, ssm_glu_b, w_out, ffn2_norm, ffn2_w1, ffn2_w3, ffn2_w2, final_norm, loss_target, m_meta_tokens, m_ffn1_norm, m_ffn1_w1, m_ffn1_w3, m_ffn1_w2, m_mix_norm, m_w_in, m_attn_sinks, m_ssm_a_re, m_ssm_a_im, m_ssm_log_step, m_ssm_b_re, m_ssm_b_im, m_ssm_c_re, m_ssm_c_im, m_ssm_d, m_ssm_glu_a, m_ssm_glu_b, m_w_out, m_ffn2_norm, m_ffn2_w1, m_ffn2_w3, m_ffn2_w2, m_final_norm, v_meta_tokens, v_ffn1_norm, v_ffn1_w1, v_ffn1_w3, v_ffn1_w2, v_mix_norm, v_w_in, v_attn_sinks, v_ssm_a_re, v_ssm_a_im, v_ssm_log_step, v_ssm_b_re, v_ssm_b_im, v_ssm_c_re, v_ssm_c_im, v_ssm_d, v_ssm_glu_a, v_ssm_glu_b, v_w_out, v_ffn2_norm, v_ffn2_w1, v_ffn2_w3, v_ffn2_w2, v_final_norm):
    given = dict(locals())
    w = {n: given[n] for n in WEIGHTS}
    m = {n: given["m_" + n] for n in WEIGHTS}
    v = {n: given["v_" + n] for n in WEIGHTS}

    n_b, seq, _ = x.shape
    n_l = seq + N_META
    t_rows = n_b * n_l
    tm = _row_tile(n_l, 688)
    px, py, pc = _my_place()
    me = 4 * px + 2 * py + pc

    glu = jnp.stack([ssm_glu_a[0], ssm_glu_b[0]]).astype(BF16)
    ffn_names = ['ffn1_w1', 'ffn1_w3', 'ffn1_w2', 'ffn2_w1', 'ffn2_w3', 'ffn2_w2']

    def hidden_on_rows(n, t):
        return t[0] if n.endswith('w2') else t[0].T

    def hidden_on_rows_back(n, t):
        return t[None] if n.endswith('w2') else t.T[None]

    me_idx = jnp.reshape(me, (1,)).astype(jnp.int32)
    first_names, later_names = ffn_names[:3], ffn_names[3:]
    *first, metag = all_gather_list(
        [hidden_on_rows(n, w[n]).astype(BF16) for n in first_names] + [meta_tokens], meta_tokens, "ag_first")
    win_send, win_recv, win_shard, win_land, win_token = exchange_start(
        [w_in[0].astype(BF16)], first[0], True, "ag_w_in_start")
    later_shards = [hidden_on_rows(n, w[n]).astype(BF16) for n in later_names] + [glu, w_out[0].astype(BF16)]
    ag_send, ag_recv, later_shards, later_lands, ag_token = exchange_start(later_shards, win_token, True, "ag_later_start")
    full = {n: g.reshape(D_FF, D_MODEL) for n, g in zip(first_names, first)}
    meta_full = metag.transpose(1, 0, 2).reshape(N_META, D_MODEL)

    h0 = jnp.concatenate([x, jnp.broadcast_to(meta_full[None], (n_b, N_META, D_MODEL))], axis=1).reshape(t_rows, D_MODEL)
    target = jnp.concatenate([loss_target, jnp.zeros((n_b, N_META, D_MODEL), F32)], axis=1).reshape(t_rows, D_MODEL)
    final_g = final_norm.reshape(1, D_MODEL)

    ar = ssm_a_re.reshape(1, N_STATES)
    ai = ssm_a_im.reshape(1, N_STATES)
    ls = jnp.repeat(ssm_log_step.reshape(SSM_GROUPS), SSM_STATE).reshape(1, N_STATES)
    br_t = ssm_b_re[0].transpose(2, 0, 1).reshape(SSM_GROUP, N_STATES)
    bi_t = ssm_b_im[0].transpose(2, 0, 1).reshape(SSM_GROUP, N_STATES)
    bbr, bbi, tabf, tabr = ssm_prepare(ar, ai, ls, br_t, bi_t, "ssm_prepare")
    bbr_g = bbr.reshape(SSM_GROUP, SSM_GROUPS, SSM_STATE).transpose(1, 0, 2)
    bbi_g = bbi.reshape(SSM_GROUP, SSM_GROUPS, SSM_STATE).transpose(1, 0, 2)
    groups_per_blk = SCAN_COLS // SSM_STATE
    half = ((jnp.arange(N_SCAN_BLK) % 2)[:, None] == jnp.arange(2)[None, :]).astype(F32)
    eye = jnp.eye(groups_per_blk, dtype=F32)

    def scan_blocks(re_g, im_g):
        def one(t):
            t = t.reshape(N_SCAN_BLK, groups_per_blk, SSM_GROUP, SSM_STATE)
            t = t[:, :, :, None, :] * eye[None, :, None, :, None]
            t = t.reshape(N_SCAN_BLK, LANES // 2, SCAN_COLS)
            return (t[:, None] * half[:, :, None, None]).reshape(N_SCAN_BLK, LANES, SCAN_COLS)
        return jnp.concatenate([one(re_g), one(im_g)], axis=-1).astype(BF16)

    b_comb = scan_blocks(bbr_g, bbi_g)
    c_comb_t = scan_blocks(ssm_c_re[0], -ssm_c_im[0])
    b_comb_t, c_comb = b_comb.transpose(0, 2, 1), c_comb_t.transpose(0, 2, 1)

    ffn1_w = (full['ffn1_w1'], full['ffn1_w3'], full['ffn1_w2'])
    h1, hn1, a1, b1 = ffn_forward(h0, ffn1_norm, *ffn1_w, ag_token, tm, "ffn1_fwd")
    win_shard, (wing,) = exchange_wait(win_send, win_recv, win_shard, win_land, h1, True, "ag_w_in_wait")
    wing = lax.dynamic_update_slice_in_dim(wing, win_shard[0][None], me, axis=0)
    hnm, proj = mix_forward(h1, mix_norm, wing, tm, "mix_fwd")
    proj3 = proj.reshape(n_b, n_l, IN_WIDTH)
    attn3 = attention_forward(proj3, attn_sinks, seq, "attn_fwd")
    attn = attn3.reshape(t_rows, D_MODEL)
    xs3, yraw3 = ssm_forward_scan(proj3, b_comb, tabf, c_comb, ssm_d, seq, "ssm_fwd")
    yraw = yraw3.reshape(t_rows, SSM_WIDTH)
    later_shards, later = exchange_wait(ag_send, ag_recv, later_shards, later_lands, yraw3, True, "ag_later_wait")
    later = [lax.dynamic_update_slice_in_dim(z, s[None], me, axis=0) for z, s in zip(later, later_shards)]
    for n, g in zip(later_names, later):
        full[n] = g.reshape(D_FF, D_MODEL)
    ffn2_w = (full['ffn2_w1'], full['ffn2_w3'], full['ffn2_w2'])
    glug, wog = later[len(later_names):]
    glu_a = glug[:, 0].transpose(1, 0, 2).reshape(SSM_WIDTH, D_MODEL)
    glu_b = glug[:, 1].transpose(1, 0, 2).reshape(SSM_WIDTH, D_MODEL)
    w_out_full = wog.reshape(D_MODEL, D_MODEL)
    h2 = merge_forward(h1, yraw, attn, proj, glu_a, glu_b, w_out_full, tm, "merge_fwd")
    h3, hn2, a2, b2 = ffn_forward(h2, ffn2_norm, *ffn2_w, ag_token, tm, "ffn2_fwd")
    dh3, loss_part, g_final = final_loss_backward(h3, target, final_g, seq, tm, "loss_bwd")
    loss = lax.psum(loss_part[0, 0], ("x", "y", "c"))

    def blocked_ffn(d_w1t, d_w3t, d_w2):
        return tuple(t.reshape(N_DEV, FF_BLK, D_MODEL) for t in (d_w1t, d_w3t, d_w2))

    def blocked_cols(full_grad):
        r = full_grad.shape[0]
        return full_grad.reshape(r, N_DEV, full_grad.shape[1] // N_DEV).transpose(1, 0, 2).astype(BF16)

    early = {}

    def start_reduce(names, tag):
        srcs = [dw[n] for n in names]
        send, recv, srcs, lands, token = exchange_start(srcs, srcs[0], False, "rs_" + tag + "_start")
        early[tag] = (names, send, recv, srcs, lands)
        return token

    dw = {}
    da2, db2, dh3_half = ffn_backward_hidden(dh3, a2, b2, ffn2_w[2], g_final, tm, "ffn2_bwd_hid")
    dw['ffn2_w1'], dw['ffn2_w3'], dw['ffn2_w2'] = blocked_ffn(
        *ffn_backward_weights(hn2, dh3_half, a2, b2, da2, db2, n_l, FF_BWD_COLS, "ffn2_bwd_w"))
    token = start_reduce(later_names, "ffn2")
    dh2, g_ffn2_norm = ffn_backward_input(dh3, h2, ffn2_norm, da2, db2, ffn2_w[0], ffn2_w[1], token, tm, "ffn2_bwd_in")
    dattn, dyraw, dproj, *for_weights = merge_backward(dh2, yraw, attn, proj, glu_a, glu_b, w_out_full, token, tm,
                                                       "merge_bwd")
    d_wo, d_ga, d_gb = merge_backward_weights(*for_weights, tm, "merge_bwd_w")
    dw['ssm_glu_a'] = blocked_cols(d_ga)
    dw['ssm_glu_b'] = blocked_cols(d_gb)
    dw['w_out'] = d_wo.reshape(N_DEV, D_MODEL // N_DEV, D_MODEL).astype(BF16)
    token = start_reduce(['ssm_glu_a', 'ssm_glu_b', 'w_out'], "mix")
    dproj3 = dproj.reshape(n_b, n_l, IN_WIDTH)
    dproj3, dsink_p = attention_backward(proj3, dattn.reshape(n_b, n_l, D_MODEL), dproj3, attn_sinks, token, seq,
                                         "attn_bwd")
    dproj3, gs3, dlr_p, dli_p = ssm_backward_scan(
        dyraw.reshape(n_b, n_l, SSM_WIDTH), xs3, dproj3, c_comb_t, tabr, b_comb_t, ssm_d, seq, "ssm_bwd")
    dproj = dproj3.reshape(t_rows, IN_WIDTH)
    d_bd, d_cd, g_d = ssm_param_grads(proj, gs3.reshape(t_rows, 2 * N_STATES), xs3.reshape(t_rows, 2 * N_STATES),
                                      dyraw, n_l, "ssm_bwd_w")
    w_in_full = wing.transpose(1, 0, 2).reshape(D_MODEL, IN_WIDTH)
    dh1, g_mix_norm = mix_backward_act(dh2, h1, mix_norm, dproj, w_in_full, tm, "mix_bwd_act")
    dw['w_in'] = mix_backward_weights(hnm, dproj, n_l, "mix_bwd_w")
    token = start_reduce(['w_in'], "w_in")

    def group_blocks(part, channels_first):
        if channels_first:
            t = jnp.sum(part.reshape(N_SCAN_BLK, 2, LANES // 2, SCAN_COLS) * half[:, :, None, None], axis=1)
            t = t.reshape(N_SCAN_BLK, groups_per_blk, SSM_GROUP, groups_per_blk, SSM_STATE)
            t = jnp.sum(t * eye[None, :, None, :, None], axis=3)
            return t.reshape(SSM_GROUPS, SSM_GROUP, SSM_STATE)
        t = jnp.sum(part.reshape(N_SCAN_BLK, SCAN_COLS, 2, LANES // 2) * half[:, None, :, None], axis=2)
        t = t.reshape(N_SCAN_BLK, groups_per_blk, SSM_STATE, groups_per_blk, SSM_GROUP)
        t = jnp.sum(t * eye[None, :, None, :, None], axis=3)
        return t.reshape(SSM_GROUPS, SSM_STATE, SSM_GROUP).transpose(0, 2, 1)

    dbbr = group_blocks(d_bd[:, :, :SCAN_COLS], True).transpose(1, 0, 2).reshape(SSM_GROUP, N_STATES)
    dbbi = group_blocks(d_bd[:, :, SCAN_COLS:], True).transpose(1, 0, 2).reshape(SSM_GROUP, N_STATES)
    g_c_re = group_blocks(d_cd[:, :SCAN_COLS, :], False)[None]
    g_c_im = -group_blocks(d_cd[:, SCAN_COLS:, :], False)[None]
    group_sum = (jnp.arange(N_STATES)[:, None] // SSM_STATE == jnp.arange(LANES)[None, :]).astype(F32)
    g_ar, g_ai, g_ls, g_br, g_bi = ssm_param_backward(
        ar, ai, ls, br_t, bi_t, dlr_p.reshape(n_b * SUBLANES, N_STATES), dli_p.reshape(n_b * SUBLANES, N_STATES),
        dbbr, dbbi, group_sum, "ssm_bwd_params")
    g_sinks = sum_leading(dsink_p, "sink_sum")[0:1, :N_KV_HEADS * Q_PER_KV]

    small = {
        'mix_norm': g_mix_norm, 'ffn2_norm': g_ffn2_norm, 'final_norm': g_final.reshape(D_MODEL),
        'attn_sinks': g_sinks, 'ssm_a_re': g_ar.reshape(1, SSM_GROUPS, SSM_STATE), 'ssm_a_im': g_ai.reshape(1, SSM_GROUPS, SSM_STATE),
        'ssm_log_step': g_ls[:, :SSM_GROUPS],
        'ssm_b_re': g_br.reshape(SSM_GROUP, SSM_GROUPS, SSM_STATE).transpose(1, 2, 0)[None],
        'ssm_b_im': g_bi.reshape(SSM_GROUP, SSM_GROUPS, SSM_STATE).transpose(1, 2, 0)[None],
        'ssm_c_re': g_c_re, 'ssm_c_im': g_c_im, 'ssm_d': g_d,
    }
    early_small = [n for n in REPLICATED if n in small]
    sg_send, sg_recv, sg_src, sg_land, token = exchange_start(
        [_pack([small[n] for n in early_small])], token, True, "ag_small_start")
    da1, db1, dh1_half = ffn_backward_hidden(dh1, a1, b1, ffn1_w[2], token, tm, "ffn1_bwd_hid")
    dw['ffn1_w1'], dw['ffn1_w3'], dw['ffn1_w2'] = blocked_ffn(
        *ffn_backward_weights(hn1, dh1_half, a1, b1, da1, db1, n_l, FF_BWD_COLS, "ffn1_bwd_w"))
    token = start_reduce(first_names, "ffn1")
    dh0, g_ffn1_norm = ffn_backward_input(dh1, h0, ffn1_norm, da1, db1, ffn1_w[0], ffn1_w[1], token, tm, "ffn1_bwd_in")
    dh0_3 = dh0.reshape(n_b, n_l, D_MODEL)
    grad_x = dh0_3[:, :seq]
    g_meta = sum_leading(dh0_3[:, seq:], "meta_sum")

    grads, deltas, new_m, new_v = {}, {}, {}, {}

    def views(n):
        if n in ffn_names:
            return functools.partial(hidden_on_rows, n), functools.partial(hidden_on_rows_back, n)
        return (lambda t: t[0]), (lambda t: t[None])

    def finish_reduce(tag, previous):
        names, send, recv, srcs, lands = early[tag]
        srcs, lands = exchange_wait(send, recv, srcs, lands, previous, False, "rs_" + tag + "_wait")
        for n, g, land in zip(names, srcs, lands):
            two_d, back = views(n)
            out = adamw_exchanged(me_idx, g, land, two_d(w[n]), two_d(m[n]), two_d(v[n]), "adamw_" + n)
            grads[n], deltas[n], new_m[n], new_v[n] = (back(o) for o in out)
            previous = out[1]
        return previous

    previous = g_meta
    for tag in ("ffn2", "mix", "w_in"):
        previous = finish_reduce(tag, previous)

    zeros_meta = jnp.zeros((N_META, D_MODEL), F32)
    (late_parts,) = all_gather_list([_pack([g_ffn1_norm, g_meta])], previous, "ag_small_late")
    sg_src, (early_parts,) = exchange_wait(sg_send, sg_recv, sg_src, sg_land, late_parts, True, "ag_small_wait")
    early_parts = lax.dynamic_update_slice_in_dim(early_parts, sg_src[0][None], me, axis=0)

    def small_update(parts, names, extra, tag):
        pack_of = lambda d: _pack([d[n] for n in names] + extra)
        packed = adamw_small(parts, pack_of(w), pack_of(m), pack_of(v), "adamw_small_" + tag)
        unpacked = [_unpack(p, [w[n].shape for n in names] + [e.shape for e in extra]) for p in packed]
        for k, n in enumerate(names):
            grads[n], deltas[n], new_m[n], new_v[n] = (u[k] for u in unpacked)
        return packed, unpacked

    small_update(early_parts, early_small, [], "early")
    packed_out, unpacked = small_update(late_parts, ['ffn1_norm'], [zeros_meta], "late")
    g_meta_full = unpacked[0][-1]
    grads['meta_tokens'] = lax.dynamic_index_in_dim(
        g_meta_full.reshape(N_META, N_DEV, D_MODEL // N_DEV), me, axis=1, keepdims=False)
    deltas['meta_tokens'], new_m['meta_tokens'], new_v['meta_tokens'] = adamw_plain(
        grads['meta_tokens'], w['meta_tokens'], m['meta_tokens'], v['meta_tokens'], "adamw_meta")

    finish_reduce("ffn1", packed_out[0])

    return (loss, grad_x, *[grads[n] for n in WEIGHTS], *[deltas[n] for n in WEIGHTS],
            *[new_m[n] for n in WEIGHTS], *[new_v[n] for n in WEIGHTS])
```

```python
import functools

import jax
import jax.numpy as jnp
from jax import lax
from jax.experimental import pallas as pl
from jax.experimental.pallas import tpu as pltpu

F32 = jnp.float32
BF16 = jnp.bfloat16
MESH = pl.DeviceIdType.MESH

N_DEV = 8
D_MODEL = 1024
N_META = 16
HEAD_DIM = 64
N_KV_HEADS = 4
Q_PER_KV = 4
BLOCK = 128
KV_WIDTH = N_KV_HEADS * HEAD_DIM
SSM_GROUP = 16
SSM_WIDTH = 512
SSM_GROUPS = 32
SSM_STATE = 64
N_STATES = SSM_GROUPS * SSM_STATE
D_FF = 2816
FF_BLK = D_FF // N_DEV
IN_WIDTH = 4096
IN_BLK = IN_WIDTH // N_DEV
NORM_EPS = 1e-6
NEG_INF = -1e30
SCAN_COLS = 256
N_SCAN_BLK = N_STATES // SCAN_COLS
SUBLANES = 8
LANES = 128
MXU_WIDTH = 256
FF_BWD_COLS = MXU_WIDTH

ADAM_LR = 0.001
ADAM_B1 = 0.9
ADAM_B2 = 0.999
ADAM_EPS = 1e-08
ADAM_WD = 0.01
ADAM_STEP = 10

VMEM_BIG = 56 * 1024 * 1024


def _cp(sem=None, vmem=None):
    kw = {}
    if sem is not None:
        kw["dimension_semantics"] = sem
    if vmem is not None:
        kw["vmem_limit_bytes"] = vmem
    return pltpu.CompilerParams(**kw)


def _pcall(body, **kw):
    return pl.pallas_call(body, **kw)


def _dot(a, b):
    return jnp.dot(a, b, preferred_element_type=F32)


def _dot_nt(a, b):
    return lax.dot_general(a, b, (((1,), (1,)), ((), ())), preferred_element_type=F32)


def _dot_tn(a, b):
    return lax.dot_general(a, b, (((0,), (0,)), ((), ())), preferred_element_type=F32)


def _sigmoid(x):
    return 1.0 / (1.0 + jnp.exp(-x))


def _row_tile(rows, cap):
    best = None
    for t in range(16, min(rows, cap) + 1, 16):
        if rows % t == 0:
            best = t
    assert best is not None, rows
    return best


def _my_place():
    return lax.axis_index("x"), lax.axis_index("y"), lax.axis_index("c")


def all_gather_list(shards, after, name):
    n = len(shards)

    def body(*refs):
        ins, outs = refs[:n], refs[n + 1:2 * n + 1]
        send_sems, recv_sems, local_sems = refs[2 * n + 1:]
        x, y, c = _my_place()
        me, sibling = (x, y, c), (x, y, 1 - c)
        chips = [(1 - x, y), (x, 1 - y), (1 - x, 1 - y)]

        def blk(a, px, py, pc):
            return outs[a].at[4 * px + 2 * py + pc]

        def copy(a, k, block, to, src=None):
            return pltpu.make_async_remote_copy(
                src_ref=blk(a, *block) if src is None else src, dst_ref=blk(a, *block),
                send_sem=send_sems.at[a * 7 + k], recv_sem=recv_sems.at[a * 7 + k],
                device_id=to, device_id_type=MESH)

        mine = [pltpu.make_async_copy(ins[a], blk(a, *me), local_sems.at[a]) for a in range(n)]
        for cp in mine:
            cp.start()
        first = []
        for a in range(n):
            first.append(copy(a, 0, me, sibling, src=ins[a]))
            first += [copy(a, 1 + j, me, (*chip, c), src=ins[a]) for j, chip in enumerate(chips)]
        for cp in first:
            cp.start()
        passed = []
        for j, chip in enumerate(chips):
            for a in range(n):
                copy(a, 1 + j, (*chip, c), me).wait_recv()
                cp = copy(a, 4 + j, (*chip, c), sibling)
                cp.start()
                passed.append(cp)
        for a in range(n):
            copy(a, 0, sibling, me).wait_recv()
            for j, chip in enumerate(chips):
                copy(a, 4 + j, (*chip, 1 - c), me).wait_recv()
        for cp in first + passed:
            cp.wait_send()
        for cp in mine:
            cp.wait()

    any_spec = pl.BlockSpec(memory_space=pl.ANY)
    return _pcall(
        body, name=name,
        out_shape=[jax.ShapeDtypeStruct((N_DEV,) + s.shape, s.dtype) for s in shards],
        in_specs=[any_spec] * (n + 1), out_specs=[any_spec] * n,
        scratch_shapes=[pltpu.SemaphoreType.DMA((7 * n,)), pltpu.SemaphoreType.DMA((7 * n,)),
                        pltpu.SemaphoreType.DMA((n,))],
    )(*shards, after)


HBM_SPEC = pl.BlockSpec(memory_space=pltpu.HBM)
SEM_SPEC = pl.BlockSpec(memory_space=pltpu.SEMAPHORE)
N_PEERS = N_DEV - 1


def _related(k):
    x, y, c = _my_place()
    px = 1 - x if k & 4 else x
    py = 1 - y if k & 2 else y
    pc = 1 - c if k & 1 else c
    return (px, py, pc), 4 * px + 2 * py + pc


def _exchange_copies(srcs, lands, send_sems, recv_sems, gather):
    x, y, c = _my_place()
    me = 4 * x + 2 * y + c
    copies = []
    for a, (src, land) in enumerate(zip(srcs, lands)):
        for k in range(1, N_DEV):
            peer, d = _related(k)
            copies.append(pltpu.make_async_remote_copy(
                src_ref=src if gather else src.at[d], dst_ref=land.at[me] if gather else land.at[k],
                send_sem=send_sems.at[a * N_PEERS + k - 1], recv_sem=recv_sems.at[a * N_PEERS + k - 1],
                device_id=peer, device_id_type=MESH))
    return copies


def exchange_start(srcs, after, gather, name):
    n = len(srcs)
    land_shapes = [((N_DEV,) + s.shape) if gather else s.shape for s in srcs]

    def body(*refs):
        send_sems, recv_sems = refs[2 * n + 1], refs[2 * n + 2]
        for cp in _exchange_copies(refs[:n], refs[n:2 * n], send_sems, recv_sems, gather):
            cp.start()
        token = refs[-1]
        token[...] = jnp.zeros_like(token)

    sems = pltpu.SemaphoreType.DMA((n * N_PEERS,))
    lands = [pltpu.with_memory_space_constraint(lax.empty(shape, s.dtype), pltpu.HBM) for shape, s in zip(land_shapes, srcs)]
    out = _pcall(
        body, name=name,
        out_shape=(sems, sems, *[pltpu.HBM(s.shape, s.dtype) for s in srcs],
                   *[pltpu.HBM(shape, s.dtype) for shape, s in zip(land_shapes, srcs)],
                   jax.ShapeDtypeStruct((SUBLANES, LANES), F32)),
        in_specs=[HBM_SPEC] * (2 * n) + [pl.BlockSpec(memory_space=pl.ANY)],
        out_specs=(SEM_SPEC, SEM_SPEC, *[HBM_SPEC] * (2 * n), pl.BlockSpec(memory_space=pltpu.VMEM)),
        input_output_aliases={i: 2 + i for i in range(2 * n)},
        compiler_params=pltpu.CompilerParams(has_side_effects=pltpu.SideEffectType.DATAFLOW_SIDE_EFFECTING),
    )(*[pltpu.with_memory_space_constraint(s, pltpu.HBM) for s in srcs], *lands, after)
    return out[0], out[1], list(out[2:2 + n]), list(out[2 + n:2 + 2 * n]), out[-1]


def exchange_wait(send_sems, recv_sems, srcs, lands, after, gather, name):
    n = len(srcs)

    def body(*refs):
        for cp in _exchange_copies(refs[:n], refs[n:2 * n], refs[2 * n], refs[2 * n + 1], gather):
            cp.wait_send()
            cp.wait_recv()

    out = _pcall(
        body, name=name,
        out_shape=(*[pltpu.HBM(s.shape, s.dtype) for s in srcs], *[pltpu.HBM(z.shape, z.dtype) for z in lands]),
        in_specs=[HBM_SPEC] * (2 * n) + [SEM_SPEC, SEM_SPEC, pl.BlockSpec(memory_space=pl.ANY)],
        out_specs=tuple([HBM_SPEC] * (2 * n)),
        input_output_aliases={i: i for i in range(2 * n)},
        compiler_params=pltpu.CompilerParams(has_side_effects=pltpu.SideEffectType.DATAFLOW_SIDE_EFFECTING),
    )(*srcs, *lands, send_sems, recv_sems, after)
    return list(out[:n]), list(out[n:])


def adamw_exchanged(me, g, land, w, m, v, name):
    rows, cols = w.shape
    tr = _row_tile(rows, 256)

    def body(me_ref, g_ref, land_ref, w_ref, m_ref, v_ref, go_ref, d_ref, mo_ref, vo_ref):
        grad = g_ref[...].astype(F32)
        for k in range(1, N_DEV):
            grad = grad + land_ref[k].astype(F32)
        delta, m_new, v_new = _adam_math(w_ref[...], grad, m_ref[...], v_ref[...])
        go_ref[...] = grad
        d_ref[...] = delta
        mo_ref[...] = m_new
        vo_ref[...] = v_new

    tile = pl.BlockSpec((tr, cols), lambda r, ix: (r, 0))
    out = jax.ShapeDtypeStruct((rows, cols), F32)
    return _pcall(
        body, name=name, out_shape=[out] * 4,
        grid_spec=pltpu.PrefetchScalarGridSpec(
            num_scalar_prefetch=1, grid=(rows // tr,),
            in_specs=[pl.BlockSpec((None, tr, cols), lambda r, ix: (ix[0], r, 0)),
                      pl.BlockSpec((N_DEV, tr, cols), lambda r, ix: (0, r, 0)), tile, tile, tile],
            out_specs=[tile] * 4),
        compiler_params=_cp(("arbitrary",)),
    )(me, g, land, w, m, v)


def _adam_math(w, g, m, v):
    m = ADAM_B1 * m + (1.0 - ADAM_B1) * g
    v = ADAM_B2 * v + (1.0 - ADAM_B2) * (g * g)
    m_hat = m / (1.0 - ADAM_B1 ** ADAM_STEP)
    v_hat = v / (1.0 - ADAM_B2 ** ADAM_STEP)
    delta = -ADAM_LR * (m_hat / (jnp.sqrt(v_hat) + ADAM_EPS) + ADAM_WD * w)
    return delta, m, v


def adamw_small(parts, w, m, v, name):
    _, rows, cols = parts.shape

    def body(p_ref, w_ref, m_ref, v_ref, go_ref, d_ref, mo_ref, vo_ref):
        grad = p_ref[0]
        for k in range(1, N_DEV):
            grad = grad + p_ref[k]
        delta, m_new, v_new = _adam_math(w_ref[...], grad, m_ref[...], v_ref[...])
        go_ref[...] = grad
        d_ref[...] = delta
        mo_ref[...] = m_new
        vo_ref[...] = v_new

    out = jax.ShapeDtypeStruct((rows, cols), F32)
    return _pcall(body, name=name, out_shape=[out] * 4, compiler_params=_cp(vmem=VMEM_BIG))(parts, w, m, v)


def adamw_plain(g, w, m, v, name):
    def body(g_ref, w_ref, m_ref, v_ref, d_ref, mo_ref, vo_ref):
        delta, m_new, v_new = _adam_math(w_ref[...], g_ref[...], m_ref[...], v_ref[...])
        d_ref[...] = delta
        mo_ref[...] = m_new
        vo_ref[...] = v_new

    out = jax.ShapeDtypeStruct(w.shape, F32)
    return _pcall(body, name=name, out_shape=[out] * 3)(g, w, m, v)


def _rms_fwd(x, g):
    r = lax.rsqrt(jnp.mean(x * x, axis=-1, keepdims=True) + NORM_EPS)
    return x * r * g


def _rms_bwd(x, g, dy):
    r = lax.rsqrt(jnp.mean(x * x, axis=-1, keepdims=True) + NORM_EPS)
    xh = x * r
    t = dy * g
    dx = r * (t - xh * jnp.mean(t * xh, axis=-1, keepdims=True))
    return dx, jnp.sum(dy * xh, axis=0, keepdims=True)


def _accumulate(ref, val, first):
    @pl.when(first)
    def _():
        ref[...] = val

    @pl.when(jnp.logical_not(first))
    def _():
        ref[...] += val


def _col_chunks(width):
    return [(c0, min(MXU_WIDTH, width - c0)) for c0 in range(0, width, MXU_WIDTH)]


ANY_SPEC = pl.BlockSpec(memory_space=pl.ANY)


def ffn_forward(h, norm, w1, w3, w2, after, tm, name):
    t_rows = h.shape[0]

    def body(h_ref, g_ref, w1_ref, w3_ref, w2_ref, _, out_ref, hn_ref, a_ref, b_ref, hid_ref):
        hn = _rms_fwd(h_ref[...], g_ref[...]).astype(BF16)
        hn_ref[...] = hn
        for c0, cw in _col_chunks(D_FF):
            a = _dot_nt(hn, w1_ref[c0:c0 + cw, :])
            b = _dot_nt(hn, w3_ref[c0:c0 + cw, :])
            a_ref[:, c0:c0 + cw] = a.astype(BF16)
            b_ref[:, c0:c0 + cw] = b.astype(BF16)
            hid_ref[:, c0:c0 + cw] = (a * _sigmoid(a) * b).astype(BF16)
        out_ref[...] = h_ref[...] + 0.5 * _dot(hid_ref[...], w2_ref[...])

    row = pl.BlockSpec((tm, D_MODEL), lambda i: (i, 0))
    hid_blk = pl.BlockSpec((tm, D_FF), lambda i: (i, 0))
    weight = _resident((D_FF, D_MODEL))
    return _pcall(
        body, name=name, grid=(t_rows // tm,),
        in_specs=[row, pl.BlockSpec((1, D_MODEL), lambda i: (0, 0)), weight, weight, weight, ANY_SPEC],
        out_specs=[row, row, hid_blk, hid_blk],
        out_shape=[jax.ShapeDtypeStruct((t_rows, D_MODEL), F32), jax.ShapeDtypeStruct((t_rows, D_MODEL), BF16),
                   jax.ShapeDtypeStruct((t_rows, D_FF), BF16), jax.ShapeDtypeStruct((t_rows, D_FF), BF16)],
        scratch_shapes=[pltpu.VMEM((tm, D_FF), BF16)],
        compiler_params=_cp(("arbitrary",), VMEM_BIG),
    )(h, norm, w1, w3, w2, after)


def _resident(shape):
    return pl.BlockSpec(shape, lambda *_: (0,) * len(shape), pipeline_mode=pl.Buffered(1))


def ffn_backward_hidden(dh, a, b, w2, after, tm, name):
    t_rows = dh.shape[0]

    def body(dh_ref, a_ref, b_ref, w2_ref, _, da_ref, db_ref, dhb_ref):
        dhb = (0.5 * dh_ref[...]).astype(BF16)
        dhb_ref[...] = dhb
        for c0, cw in _col_chunks(D_FF):
            dhid = _dot_nt(dhb, w2_ref[c0:c0 + cw, :])
            av = a_ref[:, c0:c0 + cw].astype(F32)
            bv = b_ref[:, c0:c0 + cw].astype(F32)
            s = _sigmoid(av)
            da_ref[:, c0:c0 + cw] = (dhid * bv * (s * (1.0 + av * (1.0 - s)))).astype(BF16)
            db_ref[:, c0:c0 + cw] = (dhid * (av * s)).astype(BF16)

    hid = pl.BlockSpec((tm, D_FF), lambda i: (i, 0))
    row = pl.BlockSpec((tm, D_MODEL), lambda i: (i, 0))
    return _pcall(
        body, name=name, grid=(t_rows // tm,),
        in_specs=[row, hid, hid, _resident((D_FF, D_MODEL)), ANY_SPEC],
        out_specs=[hid, hid, row],
        out_shape=[jax.ShapeDtypeStruct((t_rows, D_FF), BF16), jax.ShapeDtypeStruct((t_rows, D_FF), BF16),
                   jax.ShapeDtypeStruct((t_rows, D_MODEL), BF16)],
        compiler_params=_cp(("arbitrary",), VMEM_BIG),
    )(dh, a, b, w2, after)


def ffn_backward_input(dh, h, norm, da, db, w1, w3, after, tm, name, examples=None):
    t_rows = h.shape[0]

    def body(dh_ref, h_ref, g_ref, da_ref, db_ref, w1_ref, w3_ref, _, dhin_ref, *rest):
        dg_ref = rest[-1]
        dhn = _dot(da_ref[...], w1_ref[...]) + _dot(db_ref[...], w3_ref[...])
        dx, dg = _rms_bwd(h_ref[...], g_ref[...], dhn)
        dhin = dh_ref[...] + dx
        dhin_ref[...] = dhin
        _accumulate(dg_ref, dg, pl.program_id(0) == 0)
        if examples is not None:
            @pl.when(pl.program_id(0) % tiles == tiles - 1)
            def _():
                rest[0][...] = dhin[tm - N_META:, :]

    row = pl.BlockSpec((tm, D_MODEL), lambda i: (i, 0))
    vec = pl.BlockSpec((1, D_MODEL), lambda i: (0, 0))
    hid = pl.BlockSpec((tm, D_FF), lambda i: (i, 0))
    if examples is None:
        out_specs = [row, vec]
        out_shape = [jax.ShapeDtypeStruct((t_rows, D_MODEL), F32), jax.ShapeDtypeStruct((1, D_MODEL), F32)]
    else:
        n_b, seq = examples
        tiles = (seq + N_META) // tm
        out_specs = [pl.BlockSpec((None, tm, D_MODEL), lambda i: (i // tiles, i % tiles, 0)),
                     pl.BlockSpec((None, N_META, D_MODEL), lambda i: (i // tiles, 0, 0)), vec]
        out_shape = [jax.ShapeDtypeStruct((n_b, seq, D_MODEL), F32), jax.ShapeDtypeStruct((n_b, N_META, D_MODEL), F32),
                     jax.ShapeDtypeStruct((1, D_MODEL), F32)]
    return _pcall(
        body, name=name, grid=(t_rows // tm,),
        in_specs=[row, row, vec, hid, hid, _resident((D_FF, D_MODEL)), _resident((D_FF, D_MODEL)), ANY_SPEC],
        out_specs=out_specs, out_shape=out_shape,
        compiler_params=_cp(("arbitrary",), VMEM_BIG),
    )(dh, h, norm, da, db, w1, w3, after)


def ffn_backward_weights(hn, dh, a, b, da, db, tm, tn, name):
    t_rows = hn.shape[0]
    ni = t_rows // tm
    kc = _row_tile(tm, 688)

    def body(hn_ref, dh_ref, a_ref, b_ref, da_ref, db_ref, dw1_ref, dw3_ref, dw2_ref, acc1, acc3, acc2):
        i = pl.program_id(1)
        parts = None
        for r0 in range(0, tm, kc):
            rows = slice(r0, r0 + kc)
            hn_v = hn_ref[rows, :]
            av = a_ref[rows, :].astype(F32)
            hid = (av * _sigmoid(av) * b_ref[rows, :].astype(F32)).astype(BF16)
            new = (_dot_tn(hn_v, da_ref[rows, :]), _dot_tn(hn_v, db_ref[rows, :]), _dot_tn(dh_ref[rows, :], hid))
            parts = new if parts is None else tuple(p + q for p, q in zip(parts, new))
        _accumulate(acc1, parts[0], i == 0)
        _accumulate(acc3, parts[1], i == 0)
        _accumulate(acc2, parts[2], i == 0)

        @pl.when(i == ni - 1)
        def _():
            dw1_ref[...] = acc1[...].T.astype(BF16)
            dw3_ref[...] = acc3[...].T.astype(BF16)
            dw2_ref[...] = acc2[...].T.astype(BF16)

    row = pl.BlockSpec((tm, D_MODEL), lambda j, i: (i, 0))
    hid_blk = pl.BlockSpec((tm, tn), lambda j, i: (i, j))
    w_row = pl.BlockSpec((tn, D_MODEL), lambda j, i: (j, 0))
    out = jax.ShapeDtypeStruct((D_FF, D_MODEL), BF16)
    return _pcall(
        body, name=name, grid=(D_FF // tn, ni),
        in_specs=[row, row, hid_blk, hid_blk, hid_blk, hid_blk],
        out_specs=[w_row, w_row, w_row], out_shape=[out, out, out],
        scratch_shapes=[pltpu.VMEM((D_MODEL, tn), F32)] * 3,
        compiler_params=_cp(("arbitrary", "arbitrary"), VMEM_BIG),
    )(hn, dh, a, b, da, db)


def mix_forward(h, norm, wing, tm, name):
    t_rows = h.shape[0]

    def body(h_ref, g_ref, w_ref, hn_ref, p_ref):
        hn = _rms_fwd(h_ref[...], g_ref[...]).astype(BF16)
        hn_ref[...] = hn
        for j in range(N_DEV):
            p_ref[:, j * IN_BLK:(j + 1) * IN_BLK] = _dot(hn, w_ref[j]).astype(BF16)

    row = pl.BlockSpec((tm, D_MODEL), lambda i: (i, 0))
    return _pcall(
        body, name=name, grid=(t_rows // tm,),
        in_specs=[row, pl.BlockSpec((1, D_MODEL), lambda i: (0, 0)),
                  pl.BlockSpec((N_DEV, D_MODEL, IN_BLK), lambda i: (0, 0, 0))],
        out_specs=[row, pl.BlockSpec((tm, IN_WIDTH), lambda i: (i, 0))],
        out_shape=[jax.ShapeDtypeStruct((t_rows, D_MODEL), BF16), jax.ShapeDtypeStruct((t_rows, IN_WIDTH), BF16)],
        compiler_params=_cp(("arbitrary",), VMEM_BIG),
    )(h, norm, wing)


def mix_backward_act(dh, h, norm, dproj, w_in_full, tm, name):
    t_rows = h.shape[0]

    def body(dh_ref, h_ref, g_ref, dp_ref, w_ref, dhin_ref, dg_ref):
        dx, dg = _rms_bwd(h_ref[...], g_ref[...], _dot_nt(dp_ref[...], w_ref[...]))
        dhin_ref[...] = dh_ref[...] + dx
        _accumulate(dg_ref, dg, pl.program_id(0) == 0)

    row = pl.BlockSpec((tm, D_MODEL), lambda i: (i, 0))
    vec = pl.BlockSpec((1, D_MODEL), lambda i: (0, 0))
    return _pcall(
        body, name=name, grid=(t_rows // tm,),
        in_specs=[row, row, vec, pl.BlockSpec((tm, IN_WIDTH), lambda i: (i, 0)), _resident((D_MODEL, IN_WIDTH))],
        out_specs=[row, vec],
        out_shape=[jax.ShapeDtypeStruct((t_rows, D_MODEL), F32), jax.ShapeDtypeStruct((1, D_MODEL), F32)],
        compiler_params=_cp(("arbitrary",), VMEM_BIG),
    )(dh, h, norm, dproj, w_in_full)


def mix_backward_weights(hn, dproj, tm, name):
    t_rows = hn.shape[0]
    ni = t_rows // tm
    per_step = 2

    kc = _row_tile(tm, 688)

    def body(hn_ref, dp_ref, dw_ref, acc):
        i = pl.program_id(1)
        part = functools.reduce(lambda u, w: u + w, [_dot_tn(hn_ref[r0:r0 + kc, :], dp_ref[r0:r0 + kc, :])
                                                    for r0 in range(0, tm, kc)])
        _accumulate(acc, part, i == 0)

        @pl.when(i == ni - 1)
        def _():
            for k in range(per_step):
                dw_ref[k] = acc[:, k * IN_BLK:(k + 1) * IN_BLK].astype(BF16)

    return _pcall(
        body, name=name, grid=(N_DEV // per_step, ni),
        in_specs=[pl.BlockSpec((tm, D_MODEL), lambda j, i: (i, 0)),
                  pl.BlockSpec((tm, per_step * IN_BLK), lambda j, i: (i, j))],
        out_specs=pl.BlockSpec((per_step, D_MODEL, IN_BLK), lambda j, i: (j, 0, 0)),
        out_shape=jax.ShapeDtypeStruct((N_DEV, D_MODEL, IN_BLK), BF16),
        scratch_shapes=[pltpu.VMEM((D_MODEL, per_step * IN_BLK), F32)],
        compiler_params=_cp(("arbitrary", "arbitrary"), VMEM_BIG),
    )(hn, dproj)


GELU_C = 0.7978845608028654
GELU_K = 0.044715


def _gelu(x):
    return 0.5 * x * (1.0 + jnp.tanh(GELU_C * (x + GELU_K * (x * x * x))))


def _gelu_and_grad(x):
    th = jnp.tanh(GELU_C * (x + GELU_K * (x * x * x)))
    val = 0.5 * x * (1.0 + th)
    grad = 0.5 * (1.0 + th) + 0.5 * x * (1.0 - th * th) * (GELU_C * (1.0 + 3.0 * GELU_K * (x * x)))
    return val, grad


def merge_forward(h, yraw, attn, proj, glu_a, glu_b, w_out, tm, name):
    t_rows = h.shape[0]

    def body(h_ref, y_ref, at_ref, gate_ref, a_ref, b_ref, wo_ref, out_ref):
        y = _gelu(y_ref[...]).astype(BF16)
        ssm = _dot(y, a_ref[...]) * _sigmoid(_dot(y, b_ref[...]))
        ga = gate_ref[:, :D_MODEL].astype(F32)
        gs = gate_ref[:, D_MODEL:].astype(F32)
        merged = _sigmoid(ga) * at_ref[...].astype(F32) + _sigmoid(gs) * ssm
        out_ref[...] = h_ref[...] + _dot(merged.astype(BF16), wo_ref[...])

    row = pl.BlockSpec((tm, D_MODEL), lambda i: (i, 0))
    glu = pl.BlockSpec((SSM_WIDTH, D_MODEL), lambda i: (0, 0))
    return _pcall(
        body, name=name, grid=(t_rows // tm,),
        in_specs=[row, pl.BlockSpec((tm, SSM_WIDTH), lambda i: (i, 0)), row,
                  pl.BlockSpec((tm, 2 * D_MODEL), lambda i: (i, 1)), glu, glu,
                  pl.BlockSpec((D_MODEL, D_MODEL), lambda i: (0, 0))],
        out_specs=row, out_shape=jax.ShapeDtypeStruct((t_rows, D_MODEL), F32),
        compiler_params=_cp(("arbitrary",), VMEM_BIG),
    )(h, yraw, attn, proj, glu_a, glu_b, w_out)


def merge_backward(dh, yraw, attn, proj, glu_a, glu_b, w_out, after, tm, name):
    t_rows = dh.shape[0]

    def body(dh_ref, y_ref, at_ref, gate_ref, a_ref, b_ref, wo_ref, _,
             dat_ref, dy_ref, dgate_ref, d16_ref, mg_ref, y16_ref, dya_ref, dyb_ref):
        d16 = dh_ref[...].astype(BF16)
        d16_ref[...] = d16
        gel, dgel = _gelu_and_grad(y_ref[...].astype(F32))
        y16 = gel.astype(BF16)
        y16_ref[...] = y16
        dy = None
        for c0, cw in _col_chunks(D_MODEL):
            cols = slice(c0, c0 + cw)
            gcols = slice(D_MODEL + c0, D_MODEL + c0 + cw)
            dmerged = _dot_nt(d16, wo_ref[cols, :])
            ya = _dot(y16, a_ref[:, cols])
            sb = _sigmoid(_dot(y16, b_ref[:, cols]))
            ssm = ya * sb
            sa = _sigmoid(gate_ref[:, cols].astype(F32))
            ss = _sigmoid(gate_ref[:, gcols].astype(F32))
            attn_v = at_ref[:, cols].astype(F32)
            mg_ref[:, cols] = (sa * attn_v + ss * ssm).astype(BF16)
            dat_ref[:, cols] = (dmerged * sa).astype(BF16)
            dgate_ref[:, cols] = (dmerged * attn_v * sa * (1.0 - sa)).astype(BF16)
            dgate_ref[:, gcols] = (dmerged * ssm * ss * (1.0 - ss)).astype(BF16)
            dssm = dmerged * ss
            dya = (dssm * sb).astype(BF16)
            dyb = (dssm * ya * sb * (1.0 - sb)).astype(BF16)
            dya_ref[:, cols] = dya
            dyb_ref[:, cols] = dyb
            part = _dot_nt(dya, a_ref[:, cols]) + _dot_nt(dyb, b_ref[:, cols])
            dy = part if dy is None else dy + part
        dy_ref[...] = (dy * dgel).astype(BF16)

    row = pl.BlockSpec((tm, D_MODEL), lambda i: (i, 0))
    ssm_row = pl.BlockSpec((tm, SSM_WIDTH), lambda i: (i, 0))
    gates = pl.BlockSpec((tm, 2 * D_MODEL), lambda i: (i, 1))
    wide = jax.ShapeDtypeStruct((t_rows, D_MODEL), BF16)
    narrow = jax.ShapeDtypeStruct((t_rows, SSM_WIDTH), BF16)
    return _pcall(
        body, name=name, grid=(t_rows // tm,),
        in_specs=[row, ssm_row, row, gates, _resident((SSM_WIDTH, D_MODEL)), _resident((SSM_WIDTH, D_MODEL)),
                  _resident((D_MODEL, D_MODEL)), ANY_SPEC],
        out_specs=[row, ssm_row, gates, row, row, ssm_row, row, row],
        out_shape=[wide, narrow, jax.ShapeDtypeStruct((t_rows, IN_WIDTH), BF16), wide, wide, narrow, wide, wide],
        compiler_params=_cp(("arbitrary",), VMEM_BIG),
    )(dh, yraw, attn, proj, glu_a, glu_b, w_out, after)


def merge_backward_weights(d16, merged, y16, dya, dyb, tm, name):
    t_rows = d16.shape[0]

    def body(d_ref, mg_ref, y_ref, dya_ref, dyb_ref, dwo_ref, da_ref, db_ref):
        first = pl.program_id(0) == 0
        y16 = y_ref[...]
        _accumulate(dwo_ref, _dot_tn(mg_ref[...], d_ref[...]), first)
        _accumulate(da_ref, _dot_tn(y16, dya_ref[...]), first)
        _accumulate(db_ref, _dot_tn(y16, dyb_ref[...]), first)

    row = pl.BlockSpec((tm, D_MODEL), lambda i: (i, 0))
    ssm_row = pl.BlockSpec((tm, SSM_WIDTH), lambda i: (i, 0))
    glu = pl.BlockSpec((SSM_WIDTH, D_MODEL), lambda i: (0, 0))
    wo = pl.BlockSpec((D_MODEL, D_MODEL), lambda i: (0, 0))
    return _pcall(
        body, name=name, grid=(t_rows // tm,),
        in_specs=[row, row, ssm_row, row, row], out_specs=[wo, glu, glu],
        out_shape=[jax.ShapeDtypeStruct((D_MODEL, D_MODEL), F32), jax.ShapeDtypeStruct((SSM_WIDTH, D_MODEL), F32),
                   jax.ShapeDtypeStruct((SSM_WIDTH, D_MODEL), F32)],
        compiler_params=_cp(("arbitrary",), VMEM_BIG),
    )(d16, merged, y16, dya, dyb)


def final_loss_backward(h, target, norm, seq, tm, name):
    t_rows = h.shape[0]
    tiles_per_example = (seq + N_META) // tm

    def body(h_ref, t_ref, g_ref, dh_ref, loss_ref, dg_ref):
        i = pl.program_id(0)
        x = h_ref[...]
        g = g_ref[...]
        r = lax.rsqrt(jnp.mean(x * x, axis=-1, keepdims=True) + NORM_EPS)
        xh = x * r
        pos = lax.broadcasted_iota(jnp.int32, (tm, 1), 0) + (i % tiles_per_example) * tm
        diff = jnp.where(pos < seq, xh * g - t_ref[...], 0.0)
        part = 0.5 * jnp.sum(jnp.sum(diff * diff, axis=-1, keepdims=True), axis=0, keepdims=True) / D_MODEL
        dy = diff / D_MODEL
        t = dy * g
        dh_ref[...] = r * (t - xh * jnp.mean(t * xh, axis=-1, keepdims=True))
        _accumulate(loss_ref, jnp.broadcast_to(part, (1, LANES)), i == 0)
        _accumulate(dg_ref, jnp.sum(dy * xh, axis=0, keepdims=True), i == 0)

    row = pl.BlockSpec((tm, D_MODEL), lambda i: (i, 0))
    vec = pl.BlockSpec((1, D_MODEL), lambda i: (0, 0))
    per_example = pl.BlockSpec((None, tm, D_MODEL), lambda i: (i // tiles_per_example, i % tiles_per_example, 0))
    return _pcall(
        body, name=name, grid=(t_rows // tm,),
        in_specs=[row, per_example, vec],
        out_specs=[row, pl.BlockSpec((1, LANES), lambda i: (0, 0)), vec],
        out_shape=[jax.ShapeDtypeStruct((t_rows, D_MODEL), F32), jax.ShapeDtypeStruct((1, LANES), F32),
                   jax.ShapeDtypeStruct((1, D_MODEL), F32)],
        compiler_params=_cp(("arbitrary",), VMEM_BIG),
    )(h, target, norm)


ATTN_SCALE = HEAD_DIM ** -0.5
STACK_HEADS = (0, 2, 1, 3)
META_PAD = LANES - N_META


def _lane_half(shape, hf):
    lane = lax.broadcasted_iota(jnp.int32, shape, 1)
    return (lane < HEAD_DIM) if hf == 0 else (lane >= HEAD_DIM)


def _kv_variants(ref, rows, kh, pad_rows=0):
    tile = kh // 2
    t = ref[rows, tile * LANES:(tile + 1) * LANES].astype(F32)
    swapped = pltpu.roll(t, HEAD_DIM, axis=1)
    at_low, at_high = (t, swapped) if kh % 2 == 0 else (swapped, t)
    lo = jnp.where(_lane_half(t.shape, 0), at_low, 0.0).astype(BF16)
    hi = jnp.where(_lane_half(t.shape, 1), at_high, 0.0).astype(BF16)
    if pad_rows:
        zeros = jnp.zeros((pad_rows, LANES), BF16)
        lo, hi = jnp.concatenate([lo, zeros], axis=0), jnp.concatenate([hi, zeros], axis=0)
    return lo, hi


def _key_tiles(ref, key_rows, kh):
    return [_kv_variants(ref, r, kh, META_PAD if i == len(key_rows) - 1 else 0) for i, r in enumerate(key_rows)]


def _to_kv_lanes(lo, hi, kh):
    lo = jnp.where(_lane_half(lo.shape, 0), lo, 0.0)
    hi = jnp.where(_lane_half(hi.shape, 1), hi, 0.0)
    if kh % 2 == 0:
        return lo + pltpu.roll(hi, HEAD_DIM, axis=1)
    return pltpu.roll(lo, HEAD_DIM, axis=1) + hi


def _stacked(ref, rows, kh):
    col = kh * 2 * LANES
    return jnp.concatenate([ref[rows, col:col + LANES], ref[rows, col + LANES:col + 2 * LANES]], axis=0)


def _sink_column(sink_ref, kh, nq):
    row = lax.broadcasted_iota(jnp.int32, (4 * nq, 1), 0)
    col = jnp.zeros((4 * nq, 1), F32)
    for quarter, g in enumerate(STACK_HEADS):
        col = jnp.where(row // nq == quarter, sink_ref[0, kh * Q_PER_KV + g], col)
    return col


def _softmax_parts(qs, key_tiles, masks, sink):
    scores = []
    for (k_lo, k_hi), mask in zip(key_tiles, masks):
        s = jnp.concatenate([_dot_nt(qs, k_lo), _dot_nt(qs, k_hi)], axis=0) * ATTN_SCALE
        scores.append(s if mask is None else jnp.where(mask, s, NEG_INF))
    m = jnp.maximum(_row_reduce(scores, jnp.maximum, jnp.max), sink)
    probs = [jnp.exp(s - m) for s in scores]
    e_sink = jnp.exp(sink - m)
    den = _row_sums(probs) + e_sink
    return probs, 1.0 / den, e_sink


def _row_reduce(tiles, combine, reduce):
    chunks = [t[:, c:c + LANES] for t in tiles for c in range(0, t.shape[-1], LANES)]
    return reduce(functools.reduce(combine, chunks), axis=-1, keepdims=True)


def _row_sums(tiles):
    return _row_reduce(tiles, lambda u, w: u + w, jnp.sum)


def _band_mask(nq, first):
    keys = BLOCK if first else 2 * BLOCK
    qi = lax.broadcasted_iota(jnp.int32, (4 * nq, keys), 0) % nq
    kj = lax.broadcasted_iota(jnp.int32, (4 * nq, keys), 1)
    if first:
        return kj <= qi
    return jnp.logical_and(kj > qi, kj <= qi + BLOCK)


def _meta_mask(nq, causal):
    qi = lax.broadcasted_iota(jnp.int32, (4 * nq, LANES), 0) % nq
    kj = lax.broadcasted_iota(jnp.int32, (4 * nq, LANES), 1)
    return jnp.logical_and(kj < N_META, kj <= qi) if causal else kj < N_META


def _attention_schedule(seq, queries, carry):
    meta_rows = pl.ds(seq, N_META)
    meta_ok = _meta_mask(BLOCK, False)
    carry = queries(pl.ds(0, BLOCK), BLOCK, [pl.ds(0, BLOCK), meta_rows], [_band_mask(BLOCK, True), meta_ok], carry)

    def block(n, c):
        r0 = pl.multiple_of(n * BLOCK, BLOCK)
        p0 = pl.multiple_of((n - 1) * BLOCK, BLOCK)
        return queries(pl.ds(r0, BLOCK), BLOCK, [pl.ds(p0, 2 * BLOCK), meta_rows], [_band_mask(BLOCK, False), meta_ok], c)

    carry = lax.fori_loop(1, seq // BLOCK, block, carry)
    return queries(meta_rows, N_META, [meta_rows], [_meta_mask(N_META, True)], carry)


def attention_forward(proj3, sinks, seq, name):
    n_b, n_l, _ = proj3.shape

    def body(sink_ref, q_ref, k_ref, v_ref, o_ref):
        def queries(q_rows, nq, key_rows, masks, carry):
            for kh in range(N_KV_HEADS):
                ks = _key_tiles(k_ref, key_rows, kh)
                vs = _key_tiles(v_ref, key_rows, kh)
                qs = _stacked(q_ref, q_rows, kh)
                probs, inv, _ = _softmax_parts(qs, ks, masks, _sink_column(sink_ref, kh, nq))
                probs = [p.astype(BF16) for p in probs]
                o_lo = functools.reduce(lambda u, w: u + w, [_dot(p[:2 * nq], v_lo) for p, (v_lo, _) in zip(probs, vs)])
                o_hi = functools.reduce(lambda u, w: u + w, [_dot(p[2 * nq:], v_hi) for p, (_, v_hi) in zip(probs, vs)])
                out = (o_lo * inv[:2 * nq] + o_hi * inv[2 * nq:]).astype(BF16)
                col = kh * 2 * LANES
                o_ref[q_rows, col:col + LANES] = out[:nq]
                o_ref[q_rows, col + LANES:col + 2 * LANES] = out[nq:]
            return carry

        _attention_schedule(seq, queries, 0)

    return _pcall(
        body, name=name, grid=(n_b,),
        in_specs=[pl.BlockSpec(memory_space=pltpu.SMEM),
                  pl.BlockSpec((None, n_l, D_MODEL), lambda b: (b, 0, 0)),
                  pl.BlockSpec((None, n_l, KV_WIDTH), lambda b: (b, 0, D_MODEL // KV_WIDTH)),
                  pl.BlockSpec((None, n_l, KV_WIDTH), lambda b: (b, 0, D_MODEL // KV_WIDTH + 1))],
        out_specs=pl.BlockSpec((None, n_l, D_MODEL), lambda b: (b, 0, 0)),
        out_shape=jax.ShapeDtypeStruct((n_b, n_l, D_MODEL), BF16),
        compiler_params=_cp(("arbitrary",), VMEM_BIG),
    )(sinks, proj3, proj3, proj3)


def attention_backward(proj3, dattn3, dproj3, sinks, after, seq, name):
    n_b, n_l, _ = proj3.shape
    qkv_width = D_MODEL + 2 * KV_WIDTH

    def body(sink_ref, q_ref, k_ref, v_ref, do_ref, _, __, dqkv_ref, dsink_ref, dk_ref, dv_ref):
        dk_ref[...] = jnp.zeros_like(dk_ref)
        dv_ref[...] = jnp.zeros_like(dv_ref)
        sub = lax.broadcasted_iota(jnp.int32, (SUBLANES, LANES), 0)
        lane = lax.broadcasted_iota(jnp.int32, (SUBLANES, LANES), 1)

        def queries(q_rows, nq, key_rows, masks, dsink):
            for kh in range(N_KV_HEADS):
                ks = _key_tiles(k_ref, key_rows, kh)
                vs = _key_tiles(v_ref, key_rows, kh)
                qs = _stacked(q_ref, q_rows, kh)
                dos = _stacked(do_ref, q_rows, kh)
                probs, inv, e_sink = _softmax_parts(qs, ks, masks, _sink_column(sink_ref, kh, nq))
                probs = [p * inv for p in probs]
                dps = [jnp.concatenate([_dot_nt(dos, v_lo), _dot_nt(dos, v_hi)], axis=0) for v_lo, v_hi in vs]
                delta = _row_sums([p * dp for p, dp in zip(probs, dps)])
                d_sink = -(e_sink * inv) * delta
                for quarter, g in enumerate(STACK_HEADS):
                    d_here = jnp.sum(d_sink[quarter * nq:(quarter + 1) * nq], axis=0, keepdims=True)
                    dsink = dsink + jnp.where(jnp.logical_and(sub == 0, lane == kh * Q_PER_KV + g), d_here, 0.0)
                dq = None
                tile = slice((kh // 2) * LANES, (kh // 2 + 1) * LANES)
                for r, p, dp, (k_lo, k_hi) in zip(key_rows, probs, dps, ks):
                    ds = (p * (dp - delta)).astype(BF16)
                    p16 = p.astype(BF16)
                    dq_x = _dot(ds[:2 * nq], k_lo) + _dot(ds[2 * nq:], k_hi)
                    dq = dq_x if dq is None else dq + dq_x
                    d_k = _to_kv_lanes(_dot_tn(ds[:2 * nq], qs), _dot_tn(ds[2 * nq:], qs), kh) * ATTN_SCALE
                    d_v = _to_kv_lanes(_dot_tn(p16[:2 * nq], dos), _dot_tn(p16[2 * nq:], dos), kh)
                    n_keys = r.size
                    dk_ref[r, tile] += d_k[:n_keys]
                    dv_ref[r, tile] += d_v[:n_keys]
                dq = (dq * ATTN_SCALE).astype(BF16)
                col = kh * 2 * LANES
                dqkv_ref[q_rows, col:col + LANES] = dq[:nq]
                dqkv_ref[q_rows, col + LANES:col + 2 * LANES] = dq[nq:]
            return dsink

        dsink_ref[...] = _attention_schedule(seq, queries, jnp.zeros((SUBLANES, LANES), F32))
        dqkv_ref[:, D_MODEL:D_MODEL + KV_WIDTH] = dk_ref[...].astype(BF16)
        dqkv_ref[:, D_MODEL + KV_WIDTH:] = dv_ref[...].astype(BF16)

    return _pcall(
        body, name=name, grid=(n_b,),
        in_specs=[pl.BlockSpec(memory_space=pltpu.SMEM),
                  pl.BlockSpec((None, n_l, D_MODEL), lambda b: (b, 0, 0)),
                  pl.BlockSpec((None, n_l, KV_WIDTH), lambda b: (b, 0, D_MODEL // KV_WIDTH)),
                  pl.BlockSpec((None, n_l, KV_WIDTH), lambda b: (b, 0, D_MODEL // KV_WIDTH + 1)),
                  pl.BlockSpec((None, n_l, D_MODEL), lambda b: (b, 0, 0)),
                  ANY_SPEC, ANY_SPEC],
        out_specs=[pl.BlockSpec((None, n_l, qkv_width), lambda b: (b, 0, 0)),
                   pl.BlockSpec((None, SUBLANES, LANES), lambda b: (b, 0, 0))],
        out_shape=[jax.ShapeDtypeStruct(dproj3.shape, BF16), jax.ShapeDtypeStruct((n_b, SUBLANES, LANES), F32)],
        scratch_shapes=[pltpu.VMEM((n_l, KV_WIDTH), F32), pltpu.VMEM((n_l, KV_WIDTH), F32)],
        input_output_aliases={5: 0},
        compiler_params=_cp(("arbitrary",), VMEM_BIG),
    )(sinks, proj3, proj3, proj3, dattn3, dproj3, after)


TAB_ROWS = 8
SCAN_UNROLL = 4


def _cmul(ar, ai, br, bi):
    return ar * br - ai * bi, ar * bi + ai * br


def _discretise(ar, ai, ls):
    step = jnp.exp(ls)
    mag = jnp.exp(ar * step)
    ang = ai * step
    cos, sin = jnp.cos(ang), jnp.sin(ang)
    lr, li = mag * cos, mag * sin
    den = ar * ar + ai * ai
    nr, ni = lr - 1.0, li
    cr = (nr * ar + ni * ai) / den
    ci = (ni * ar - nr * ai) / den
    return step, mag, lr, li, den, nr, ni, cr, ci


def _scan_tables(lr, li, reverse):
    n = lr.shape[-1]
    pw = [(lr, li)]
    for _ in range(SUBLANES - 1):
        pw.append(_cmul(pw[-1][0], pw[-1][1], lr, li))
    row = lax.broadcasted_iota(jnp.int32, (SUBLANES, n), 0)
    out = []
    for d in (1, 2, 4):
        ok = (row + d <= SUBLANES - 1) if reverse else (row >= d)
        out += [jnp.where(ok, pw[d - 1][0], 0.0), jnp.where(ok, pw[d - 1][1], 0.0)]
    cr = jnp.zeros((SUBLANES, n), F32)
    ci = jnp.zeros((SUBLANES, n), F32)
    for r in range(SUBLANES):
        e = (SUBLANES - r) if reverse else (r + 1)
        cr = jnp.where(row == r, pw[e - 1][0], cr)
        ci = jnp.where(row == r, pw[e - 1][1], ci)
    return out + [cr, ci]


def ssm_prepare(ar, ai, ls, br_t, bi_t, name):
    def body(ar_ref, ai_ref, ls_ref, br_ref, bi_ref, bbr_ref, bbi_ref, tf_ref, tr_ref):
        _, _, lr, li, _, _, _, cr, ci = _discretise(ar_ref[...], ai_ref[...], ls_ref[...])
        br, bi = br_ref[...], bi_ref[...]
        bbr_ref[...] = cr * br - ci * bi
        bbi_ref[...] = cr * bi + ci * br
        for k, t in enumerate(_scan_tables(lr, li, False)):
            tf_ref[k] = t
        for k, t in enumerate(_scan_tables(lr, -li, True)):
            tr_ref[k] = t

    return _pcall(
        body, name=name,
        out_shape=[jax.ShapeDtypeStruct((SSM_GROUP, N_STATES), F32), jax.ShapeDtypeStruct((SSM_GROUP, N_STATES), F32),
                   jax.ShapeDtypeStruct((TAB_ROWS, SUBLANES, N_STATES), F32),
                   jax.ShapeDtypeStruct((TAB_ROWS, SUBLANES, N_STATES), F32)],
    )(ar, ai, ls, br_t, bi_t)


def ssm_param_backward(ar, ai, ls, br_t, bi_t, dlr_p, dli_p, dbbr, dbbi, group_sum, name):
    def body(ar_ref, ai_ref, ls_ref, br_ref, bi_ref, dlr_ref, dli_ref, dbbr_ref, dbbi_ref, gs_ref,
             dar_ref, dai_ref, dls_ref, dbr_ref, dbi_ref):
        ar, ai = ar_ref[...], ai_ref[...]
        step, mag, lr, li, den, nr, ni, cr, ci = _discretise(ar, ai, ls_ref[...])
        br, bi, dbbr_v, dbbi_v = br_ref[...], bi_ref[...], dbbr_ref[...], dbbi_ref[...]
        dbr_ref[...] = cr * dbbr_v + ci * dbbi_v
        dbi_ref[...] = cr * dbbi_v - ci * dbbr_v
        dcr = jnp.sum(dbbr_v * br + dbbi_v * bi, axis=0, keepdims=True)
        dci = jnp.sum(dbbi_v * br - dbbr_v * bi, axis=0, keepdims=True)
        dnr = (dcr * ar - dci * ai) / den
        dni = (dcr * ai + dci * ar) / den
        dden = -(cr * dcr + ci * dci) / den
        dar = (dcr * nr + dci * ni) / den + dden * 2.0 * ar
        dai = (dcr * ni - dci * nr) / den + dden * 2.0 * ai
        dlr = jnp.sum(dlr_ref[...], axis=0, keepdims=True) + dnr
        dli = jnp.sum(dli_ref[...], axis=0, keepdims=True) + dni
        dmag = (dlr * lr + dli * li) / mag
        dang = dli * lr - dlr * li
        dar_ref[...] = dar + dmag * mag * step
        dai_ref[...] = dai + dang * step
        dstep = dmag * mag * ar + dang * ai
        dls_ref[...] = jnp.dot(dstep * step, gs_ref[...], preferred_element_type=F32, precision=lax.Precision.HIGHEST)

    vec = jax.ShapeDtypeStruct((1, N_STATES), F32)
    mat = jax.ShapeDtypeStruct((SSM_GROUP, N_STATES), F32)
    return _pcall(body, name=name, out_shape=[vec, vec, jax.ShapeDtypeStruct((1, LANES), F32), mat, mat])(
        ar, ai, ls, br_t, bi_t, dlr_p, dli_p, dbbr, dbbi, group_sum)


def _scan_rows(a, b, tabs, carry, reverse):
    for k, d in enumerate((1, 2, 4)):
        shift = SUBLANES - d if reverse else d
        sr, si = pltpu.roll(a, shift, axis=0), pltpu.roll(b, shift, axis=0)
        pr, pi = _cmul(tabs[2 * k], tabs[2 * k + 1], sr, si)
        a, b = a + pr, b + pi
    pr, pi = _cmul(tabs[6], tabs[7], carry[0], carry[1])
    return a + pr, b + pi


def _time_groups(seq, reverse):
    meta = [seq + SUBLANES * g for g in range(N_META // SUBLANES)]
    return meta[::-1] if reverse else meta


def ssm_forward_scan(proj3, b_comb, tabf, c_comb, dvec, seq, name):
    n_b, n_l, _ = proj3.shape
    u_blk = (D_MODEL + 2 * KV_WIDTH) // LANES

    def body(u_ref, b_ref, tab_ref, c_ref, d_ref, x_ref, y_ref, bu, xs):
        j = pl.program_id(1)
        u = u_ref[...]
        bu[...] = _dot(u, b_ref[...])
        tabs = [tab_ref[k] for k in range(TAB_ROWS)]

        def group(r0, carry):
            rows = pl.ds(r0, SUBLANES)
            a, b = _scan_rows(bu[rows, :SCAN_COLS], bu[rows, SCAN_COLS:], tabs, carry, False)
            xs[rows, :SCAN_COLS] = a
            xs[rows, SCAN_COLS:] = b
            return (jnp.broadcast_to(a[SUBLANES - 1:, :], a.shape), jnp.broadcast_to(b[SUBLANES - 1:, :], b.shape))

        zero = jnp.zeros((SUBLANES, SCAN_COLS), F32)
        carry = (zero, zero)
        for r0 in _time_groups(seq, False):
            carry = group(r0, carry)
        span = SCAN_UNROLL * SUBLANES

        def groups(t, c):
            for k in range(SCAN_UNROLL):
                c = group(pl.multiple_of(t * span, span) + k * SUBLANES, c)
            return c

        lax.fori_loop(0, seq // span, groups, carry)
        x16 = xs[...].astype(BF16)
        x_ref[...] = x16
        contrib = _dot(x16, c_ref[...])

        @pl.when(j % 2 == 0)
        def _():
            y_ref[...] = contrib + d_ref[...] * u.astype(F32)

        @pl.when(j % 2 == 1)
        def _():
            y_ref[...] += contrib

    return _pcall(
        body, name=name, grid=(n_b, N_SCAN_BLK),
        in_specs=[pl.BlockSpec((None, n_l, LANES), lambda b, j: (b, 0, u_blk + j // 2)),
                  pl.BlockSpec((None, LANES, 2 * SCAN_COLS), lambda b, j: (j, 0, 0)),
                  pl.BlockSpec((TAB_ROWS, SUBLANES, SCAN_COLS), lambda b, j: (0, 0, j)),
                  pl.BlockSpec((None, 2 * SCAN_COLS, LANES), lambda b, j: (j, 0, 0)),
                  pl.BlockSpec((1, LANES), lambda b, j: (0, j // 2))],
        out_specs=[pl.BlockSpec((None, n_l, 2 * SCAN_COLS), lambda b, j: (b, 0, j)),
                   pl.BlockSpec((None, n_l, LANES), lambda b, j: (b, 0, j // 2))],
        out_shape=[jax.ShapeDtypeStruct((n_b, n_l, 2 * N_STATES), BF16),
                   jax.ShapeDtypeStruct((n_b, n_l, SSM_WIDTH), F32)],
        scratch_shapes=[pltpu.VMEM((n_l, 2 * SCAN_COLS), F32)] * 2,
        compiler_params=_cp(("arbitrary", "arbitrary"), VMEM_BIG),
    )(proj3, b_comb, tabf, c_comb, dvec)


def ssm_backward_scan(dyraw3, xs3, dproj3, c_comb_t, tabr, b_comb_t, dvec, seq, name):
    n_b, n_l, _ = xs3.shape
    u_blk = (D_MODEL + 2 * KV_WIDTH) // LANES

    def body(dy_ref, x_ref, _, c_ref, tab_ref, b_ref, d_ref, du_ref, g_ref, dlr_ref, dli_ref, dx, gs, xs, du_acc):
        j = pl.program_id(1)
        dy = dy_ref[...]
        dx[...] = _dot(dy, c_ref[...])
        xs[...] = x_ref[...].astype(F32)
        tabs = [tab_ref[k] for k in range(TAB_ROWS)]
        last_row = lax.broadcasted_iota(jnp.int32, (SUBLANES, SCAN_COLS), 0) == SUBLANES - 1

        def group(r0, state):
            cr, ci, acc_r, acc_i = state
            rows = pl.ds(r0, SUBLANES)
            a, b = _scan_rows(dx[rows, :SCAN_COLS], dx[rows, SCAN_COLS:], tabs, (cr, ci), True)
            gs[rows, :SCAN_COLS] = a
            gs[rows, SCAN_COLS:] = b
            na = jnp.where(last_row, cr, pltpu.roll(a, SUBLANES - 1, axis=0))
            nb = jnp.where(last_row, ci, pltpu.roll(b, SUBLANES - 1, axis=0))
            xa, xb = xs[rows, :SCAN_COLS], xs[rows, SCAN_COLS:]
            return (jnp.broadcast_to(a[:1, :], a.shape), jnp.broadcast_to(b[:1, :], b.shape),
                    acc_r + na * xa + nb * xb, acc_i + nb * xa - na * xb)

        zero = jnp.zeros((SUBLANES, SCAN_COLS), F32)
        span = SCAN_UNROLL * SUBLANES
        n_spans = seq // span

        def groups(t, s):
            for k in reversed(range(SCAN_UNROLL)):
                s = group(pl.multiple_of((n_spans - 1 - t) * span, span) + k * SUBLANES, s)
            return s

        state = lax.fori_loop(0, n_spans, groups, (zero, zero, zero, zero))
        for r0 in _time_groups(seq, True):
            state = group(r0, state)
        dlr_ref[...] = state[2]
        dli_ref[...] = state[3]
        g16 = gs[...].astype(BF16)
        g_ref[...] = g16
        contrib = _dot(g16, b_ref[...])

        @pl.when(j % 2 == 0)
        def _():
            du_acc[...] = contrib + d_ref[...] * dy.astype(F32)

        @pl.when(j % 2 == 1)
        def _():
            du_ref[...] = (du_acc[...] + contrib).astype(BF16)

    state_blk = pl.BlockSpec((None, n_l, 2 * SCAN_COLS), lambda b, j: (b, 0, j))
    dl_blk = pl.BlockSpec((None, SUBLANES, SCAN_COLS), lambda b, j: (b, 0, j))
    return _pcall(
        body, name=name, grid=(n_b, N_SCAN_BLK),
        in_specs=[pl.BlockSpec((None, n_l, LANES), lambda b, j: (b, 0, j // 2)), state_blk,
                  pl.BlockSpec(memory_space=pl.ANY),
                  pl.BlockSpec((None, LANES, 2 * SCAN_COLS), lambda b, j: (j, 0, 0)),
                  pl.BlockSpec((TAB_ROWS, SUBLANES, SCAN_COLS), lambda b, j: (0, 0, j)),
                  pl.BlockSpec((None, 2 * SCAN_COLS, LANES), lambda b, j: (j, 0, 0)),
                  pl.BlockSpec((1, LANES), lambda b, j: (0, j // 2))],
        out_specs=[pl.BlockSpec((None, n_l, LANES), lambda b, j: (b, 0, u_blk + j // 2)), state_blk, dl_blk, dl_blk],
        out_shape=[jax.ShapeDtypeStruct(dproj3.shape, BF16), jax.ShapeDtypeStruct((n_b, n_l, 2 * N_STATES), BF16),
                   jax.ShapeDtypeStruct((n_b, SUBLANES, N_STATES), F32), jax.ShapeDtypeStruct((n_b, SUBLANES, N_STATES), F32)],
        scratch_shapes=[pltpu.VMEM((n_l, 2 * SCAN_COLS), F32)] * 3 + [pltpu.VMEM((n_l, LANES), F32)],
        input_output_aliases={2: 0},
        compiler_params=_cp(("arbitrary", "arbitrary"), VMEM_BIG),
    )(dyraw3, xs3, dproj3, c_comb_t, tabr, b_comb_t, dvec)


def ssm_param_grads(proj, gs, xs, dyraw, tm, name):
    t_rows = proj.shape[0]
    ni = t_rows // tm
    u_blk = (D_MODEL + 2 * KV_WIDTH) // LANES
    width = 2 * SCAN_COLS

    def body(u_ref, g_ref, x_ref, dy_ref, db_ref, dc_ref, dd_ref):
        cb, i = pl.program_id(0), pl.program_id(1)
        u, dy = u_ref[...], dy_ref[...]
        _accumulate(db_ref, _dot_tn(u, g_ref[...]), i == 0)
        _accumulate(dc_ref, _dot_tn(x_ref[...], dy), i == 0)

        @pl.when(cb % 2 == 0)
        def _():
            _accumulate(dd_ref, jnp.sum(dy.astype(F32) * u.astype(F32), axis=0, keepdims=True), i == 0)

    return _pcall(
        body, name=name, grid=(N_SCAN_BLK, ni),
        in_specs=[pl.BlockSpec((tm, LANES), lambda cb, i: (i, u_blk + cb // 2)),
                  pl.BlockSpec((tm, width), lambda cb, i: (i, cb)),
                  pl.BlockSpec((tm, width), lambda cb, i: (i, cb)),
                  pl.BlockSpec((tm, LANES), lambda cb, i: (i, cb // 2))],
        out_specs=[pl.BlockSpec((None, LANES, width), lambda cb, i: (cb, 0, 0)),
                   pl.BlockSpec((None, width, LANES), lambda cb, i: (cb, 0, 0)),
                   pl.BlockSpec((1, LANES), lambda cb, i: (0, cb // 2))],
        out_shape=[jax.ShapeDtypeStruct((N_SCAN_BLK, LANES, width), F32),
                   jax.ShapeDtypeStruct((N_SCAN_BLK, width, LANES), F32), jax.ShapeDtypeStruct((1, SSM_WIDTH), F32)],
        compiler_params=_cp(("arbitrary", "arbitrary"), VMEM_BIG),
    )(proj, gs, xs, dyraw)


def sum_leading(x, name):
    def body(x_ref, o_ref):
        acc = x_ref[0]
        for k in range(1, x.shape[0]):
            acc = acc + x_ref[k]
        o_ref[...] = acc

    return _pcall(body, name=name, out_shape=jax.ShapeDtypeStruct(x.shape[1:], x.dtype))(x)


WEIGHTS = ['meta_tokens', 'ffn1_norm', 'ffn1_w1', 'ffn1_w3', 'ffn1_w2', 'mix_norm', 'w_in', 'attn_sinks', 'ssm_a_re',
           'ssm_a_im', 'ssm_log_step', 'ssm_b_re', 'ssm_b_im', 'ssm_c_re', 'ssm_c_im', 'ssm_d', 'ssm_glu_a', 'ssm_glu_b',
           'w_out', 'ffn2_norm', 'ffn2_w1', 'ffn2_w3', 'ffn2_w2', 'final_norm']
SHARDED = ['ffn1_w1', 'ffn1_w3', 'ffn1_w2', 'ffn2_w1', 'ffn2_w3', 'ffn2_w2', 'w_in', 'ssm_glu_a', 'ssm_glu_b', 'w_out']
REPLICATED = ['ffn1_norm', 'mix_norm', 'ffn2_norm', 'final_norm', 'attn_sinks', 'ssm_a_re', 'ssm_a_im', 'ssm_log_step',
              'ssm_b_re', 'ssm_b_im', 'ssm_c_re', 'ssm_c_im', 'ssm_d']
PACK_COLS = 1024


def _pack(arrays):
    parts = []
    for a in arrays:
        flat = a.reshape(-1)
        chunk = SUBLANES * PACK_COLS
        padded = -(-flat.shape[0] // chunk) * chunk
        parts.append(jnp.pad(flat, (0, padded - flat.shape[0])).reshape(-1, PACK_COLS))
    return jnp.concatenate(parts, axis=0)


def _unpack(packed, shapes):
    out, row = [], 0
    for shape in shapes:
        size = 1
        for s in shape:
            size *= s
        chunk = SUBLANES * PACK_COLS
        rows = -(-size // chunk) * SUBLANES
        out.append(packed[row:row + rows].reshape(-1)[:size].reshape(shape))
        row += rows
    return out


def kernel(x, meta_tokens, ffn1_norm, ffn1_w1, ffn1_w3, ffn1_w2, mix_norm, w_in, attn_sinks, ssm_a_re, ssm_a_im, ssm_log_step, ssm_b_re, ssm_b_im, ssm_c_re, ssm_c_im, ssm_d, ssm_glu_a, ssm_glu_b, w_out, ffn2_norm, ffn2_w1, ffn2_w3, ffn2_w2, final_norm, loss_target, m_meta_tokens, m_ffn1_norm, m_ffn1_w1, m_ffn1_w3, m_ffn1_w2, m_mix_norm, m_w_in, m_attn_sinks, m_ssm_a_re, m_ssm_a_im, m_ssm_log_step, m_ssm_b_re, m_ssm_b_im, m_ssm_c_re, m_ssm_c_im, m_ssm_d, m_ssm_glu_a, m_ssm_glu_b, m_w_out, m_ffn2_norm, m_ffn2_w1, m_ffn2_w3, m_ffn2_w2, m_final_norm, v_meta_tokens, v_ffn1_norm, v_ffn1_w1, v_ffn1_w3, v_ffn1_w2, v_mix_norm, v_w_in, v_attn_sinks, v_ssm_a_re, v_ssm_a_im, v_ssm_log_step, v_ssm_b_re, v_ssm_b_im, v_ssm_c_re, v_ssm_c_im, v_ssm_d, v_ssm_glu_a, v_ssm_glu_b, v_w_out, v_ffn2_norm, v_ffn2_w1, v_ffn2_w3, v_ffn2_w2, v_final_norm):
    given = dict(locals())
    w = {n: given[n] for n in WEIGHTS}
    m = {n: given["m_" + n] for n in WEIGHTS}
    v = {n: given["v_" + n] for n in WEIGHTS}

    n_b, seq, _ = x.shape
    n_l = seq + N_META
    t_rows = n_b * n_l
    tm = _row_tile(n_l, 688)
    px, py, pc = _my_place()
    me = 4 * px + 2 * py + pc

    glu = jnp.stack([ssm_glu_a[0], ssm_glu_b[0]]).astype(BF16)
    ffn_names = ['ffn1_w1', 'ffn1_w3', 'ffn1_w2', 'ffn2_w1', 'ffn2_w3', 'ffn2_w2']

    def hidden_on_rows(n, t):
        return t[0] if n.endswith('w2') else t[0].T

    def hidden_on_rows_back(n, t):
        return t[None] if n.endswith('w2') else t.T[None]

    me_idx = jnp.reshape(me, (1,)).astype(jnp.int32)
    first_names, later_names = ffn_names[:3], ffn_names[3:]
    *first, metag = all_gather_list(
        [hidden_on_rows(n, w[n]).astype(BF16) for n in first_names] + [meta_tokens], meta_tokens, "ag_first")
    win_send, win_recv, win_shard, win_land, win_token = exchange_start(
        [w_in[0].astype(BF16)], first[0], True, "ag_w_in_start")
    later_shards = [hidden_on_rows(n, w[n]).astype(BF16) for n in later_names] + [glu, w_out[0].astype(BF16)]
    ag_send, ag_recv, later_shards, later_lands, ag_token = exchange_start(later_shards, win_token, True, "ag_later_start")
    full = {n: g.reshape(D_FF, D_MODEL) for n, g in zip(first_names, first)}
    meta_full = metag.transpose(1, 0, 2).reshape(N_META, D_MODEL)

    h0 = jnp.concatenate([x, jnp.broadcast_to(meta_full[None], (n_b, N_META, D_MODEL))], axis=1).reshape(t_rows, D_MODEL)
    final_g = final_norm.reshape(1, D_MODEL)

    ar = ssm_a_re.reshape(1, N_STATES)
    ai = ssm_a_im.reshape(1, N_STATES)
    ls = jnp.repeat(ssm_log_step.reshape(SSM_GROUPS), SSM_STATE).reshape(1, N_STATES)
    br_t = ssm_b_re[0].transpose(2, 0, 1).reshape(SSM_GROUP, N_STATES)
    bi_t = ssm_b_im[0].transpose(2, 0, 1).reshape(SSM_GROUP, N_STATES)
    bbr, bbi, tabf, tabr = ssm_prepare(ar, ai, ls, br_t, bi_t, "ssm_prepare")
    bbr_g = bbr.reshape(SSM_GROUP, SSM_GROUPS, SSM_STATE).transpose(1, 0, 2)
    bbi_g = bbi.reshape(SSM_GROUP, SSM_GROUPS, SSM_STATE).transpose(1, 0, 2)
    groups_per_blk = SCAN_COLS // SSM_STATE
    half = ((jnp.arange(N_SCAN_BLK) % 2)[:, None] == jnp.arange(2)[None, :]).astype(F32)
    eye = jnp.eye(groups_per_blk, dtype=F32)

    def scan_blocks(re_g, im_g):
        def one(t):
            t = t.reshape(N_SCAN_BLK, groups_per_blk, SSM_GROUP, SSM_STATE)
            t = t[:, :, :, None, :] * eye[None, :, None, :, None]
            t = t.reshape(N_SCAN_BLK, LANES // 2, SCAN_COLS)
            return (t[:, None] * half[:, :, None, None]).reshape(N_SCAN_BLK, LANES, SCAN_COLS)
        return jnp.concatenate([one(re_g), one(im_g)], axis=-1).astype(BF16)

    b_comb = scan_blocks(bbr_g, bbi_g)
    c_comb_t = scan_blocks(ssm_c_re[0], -ssm_c_im[0])
    b_comb_t, c_comb = b_comb.transpose(0, 2, 1), c_comb_t.transpose(0, 2, 1)

    ffn1_w = (full['ffn1_w1'], full['ffn1_w3'], full['ffn1_w2'])
    h1, hn1, a1, b1 = ffn_forward(h0, ffn1_norm, *ffn1_w, ag_token, tm, "ffn1_fwd")
    win_shard, (wing,) = exchange_wait(win_send, win_recv, win_shard, win_land, h1, True, "ag_w_in_wait")
    wing = lax.dynamic_update_slice_in_dim(wing, win_shard[0][None], me, axis=0)
    hnm, proj = mix_forward(h1, mix_norm, wing, tm, "mix_fwd")
    proj3 = proj.reshape(n_b, n_l, IN_WIDTH)
    attn3 = attention_forward(proj3, attn_sinks, seq, "attn_fwd")
    attn = attn3.reshape(t_rows, D_MODEL)
    xs3, yraw3 = ssm_forward_scan(proj3, b_comb, tabf, c_comb, ssm_d, seq, "ssm_fwd")
    yraw = yraw3.reshape(t_rows, SSM_WIDTH)
    later_shards, later = exchange_wait(ag_send, ag_recv, later_shards, later_lands, yraw3, True, "ag_later_wait")
    later = [lax.dynamic_update_slice_in_dim(z, s[None], me, axis=0) for z, s in zip(later, later_shards)]
    for n, g in zip(later_names, later):
        full[n] = g.reshape(D_FF, D_MODEL)
    ffn2_w = (full['ffn2_w1'], full['ffn2_w3'], full['ffn2_w2'])
    glug, wog = later[len(later_names):]
    glu_a = glug[:, 0].transpose(1, 0, 2).reshape(SSM_WIDTH, D_MODEL)
    glu_b = glug[:, 1].transpose(1, 0, 2).reshape(SSM_WIDTH, D_MODEL)
    w_out_full = wog.reshape(D_MODEL, D_MODEL)
    h2 = merge_forward(h1, yraw, attn, proj, glu_a, glu_b, w_out_full, tm, "merge_fwd")
    h3, hn2, a2, b2 = ffn_forward(h2, ffn2_norm, *ffn2_w, ag_token, tm, "ffn2_fwd")
    dh3, loss_part, g_final = final_loss_backward(h3, loss_target, final_g, seq, tm, "loss_bwd")
    loss = lax.psum(loss_part[0, 0], ("x", "y", "c"))

    def blocked_ffn(d_w1t, d_w3t, d_w2):
        return tuple(t.reshape(N_DEV, FF_BLK, D_MODEL) for t in (d_w1t, d_w3t, d_w2))

    def blocked_cols(full_grad):
        r = full_grad.shape[0]
        return full_grad.reshape(r, N_DEV, full_grad.shape[1] // N_DEV).transpose(1, 0, 2).astype(BF16)

    early = {}

    def start_reduce(names, tag):
        srcs = [dw[n] for n in names]
        send, recv, srcs, lands, token = exchange_start(srcs, srcs[0], False, "rs_" + tag + "_start")
        early[tag] = (names, send, recv, srcs, lands)
        return token

    dw = {}
    da2, db2, dh3_half = ffn_backward_hidden(dh3, a2, b2, ffn2_w[2], g_final, tm, "ffn2_bwd_hid")
    dw['ffn2_w1'], dw['ffn2_w3'], dw['ffn2_w2'] = blocked_ffn(
        *ffn_backward_weights(hn2, dh3_half, a2, b2, da2, db2, n_l, FF_BWD_COLS, "ffn2_bwd_w"))
    token = start_reduce(later_names, "ffn2")
    dh2, g_ffn2_norm = ffn_backward_input(dh3, h2, ffn2_norm, da2, db2, ffn2_w[0], ffn2_w[1], token, tm, "ffn2_bwd_in")
    dattn, dyraw, dproj, *for_weights = merge_backward(dh2, yraw, attn, proj, glu_a, glu_b, w_out_full, token, tm,
                                                       "merge_bwd")
    d_wo, d_ga, d_gb = merge_backward_weights(*for_weights, tm, "merge_bwd_w")
    dw['ssm_glu_a'] = blocked_cols(d_ga)
    dw['ssm_glu_b'] = blocked_cols(d_gb)
    dw['w_out'] = d_wo.reshape(N_DEV, D_MODEL // N_DEV, D_MODEL).astype(BF16)
    token = start_reduce(['ssm_glu_a', 'ssm_glu_b', 'w_out'], "mix")
    dproj3 = dproj.reshape(n_b, n_l, IN_WIDTH)
    dproj3, dsink_p = attention_backward(proj3, dattn.reshape(n_b, n_l, D_MODEL), dproj3, attn_sinks, token, seq,
                                         "attn_bwd")
    dproj3, gs3, dlr_p, dli_p = ssm_backward_scan(
        dyraw.reshape(n_b, n_l, SSM_WIDTH), xs3, dproj3, c_comb_t, tabr, b_comb_t, ssm_d, seq, "ssm_bwd")
    dproj = dproj3.reshape(t_rows, IN_WIDTH)
    d_bd, d_cd, g_d = ssm_param_grads(proj, gs3.reshape(t_rows, 2 * N_STATES), xs3.reshape(t_rows, 2 * N_STATES),
                                      dyraw, n_l, "ssm_bwd_w")
    w_in_full = wing.transpose(1, 0, 2).reshape(D_MODEL, IN_WIDTH)
    dh1, g_mix_norm = mix_backward_act(dh2, h1, mix_norm, dproj, w_in_full, tm, "mix_bwd_act")
    dw['w_in'] = mix_backward_weights(hnm, dproj, n_l, "mix_bwd_w")
    token = start_reduce(['w_in'], "w_in")

    def group_blocks(part, channels_first):
        if channels_first:
            t = jnp.sum(part.reshape(N_SCAN_BLK, 2, LANES // 2, SCAN_COLS) * half[:, :, None, None], axis=1)
            t = t.reshape(N_SCAN_BLK, groups_per_blk, SSM_GROUP, groups_per_blk, SSM_STATE)
            t = jnp.sum(t * eye[None, :, None, :, None], axis=3)
            return t.reshape(SSM_GROUPS, SSM_GROUP, SSM_STATE)
        t = jnp.sum(part.reshape(N_SCAN_BLK, SCAN_COLS, 2, LANES // 2) * half[:, None, :, None], axis=2)
        t = t.reshape(N_SCAN_BLK, groups_per_blk, SSM_STATE, groups_per_blk, SSM_GROUP)
        t = jnp.sum(t * eye[None, :, None, :, None], axis=3)
        return t.reshape(SSM_GROUPS, SSM_STATE, SSM_GROUP).transpose(0, 2, 1)

    dbbr = group_blocks(d_bd[:, :, :SCAN_COLS], True).transpose(1, 0, 2).reshape(SSM_GROUP, N_STATES)
    dbbi = group_blocks(d_bd[:, :, SCAN_COLS:], True).transpose(1, 0, 2).reshape(SSM_GROUP, N_STATES)
    g_c_re = group_blocks(d_cd[:, :SCAN_COLS, :], False)[None]
    g_c_im = -group_blocks(d_cd[:, SCAN_COLS:, :], False)[None]
    group_sum = (jnp.arange(N_STATES)[:, None] // SSM_STATE == jnp.arange(LANES)[None, :]).astype(F32)
    g_ar, g_ai, g_ls, g_br, g_bi = ssm_param_backward(
        ar, ai, ls, br_t, bi_t, dlr_p.reshape(n_b * SUBLANES, N_STATES), dli_p.reshape(n_b * SUBLANES, N_STATES),
        dbbr, dbbi, group_sum, "ssm_bwd_params")
    g_sinks = sum_leading(dsink_p, "sink_sum")[0:1, :N_KV_HEADS * Q_PER_KV]

    small = {
        'mix_norm': g_mix_norm, 'ffn2_norm': g_ffn2_norm, 'final_norm': g_final.reshape(D_MODEL),
        'attn_sinks': g_sinks, 'ssm_a_re': g_ar.reshape(1, SSM_GROUPS, SSM_STATE), 'ssm_a_im': g_ai.reshape(1, SSM_GROUPS, SSM_STATE),
        'ssm_log_step': g_ls[:, :SSM_GROUPS],
        'ssm_b_re': g_br.reshape(SSM_GROUP, SSM_GROUPS, SSM_STATE).transpose(1, 2, 0)[None],
        'ssm_b_im': g_bi.reshape(SSM_GROUP, SSM_GROUPS, SSM_STATE).transpose(1, 2, 0)[None],
        'ssm_c_re': g_c_re, 'ssm_c_im': g_c_im, 'ssm_d': g_d,
    }
    early_small = [n for n in REPLICATED if n in small]
    sg_send, sg_recv, sg_src, sg_land, token = exchange_start(
        [_pack([small[n] for n in early_small])], token, True, "ag_small_start")
    da1, db1, dh1_half = ffn_backward_hidden(dh1, a1, b1, ffn1_w[2], token, tm, "ffn1_bwd_hid")
    dw['ffn1_w1'], dw['ffn1_w3'], dw['ffn1_w2'] = blocked_ffn(
        *ffn_backward_weights(hn1, dh1_half, a1, b1, da1, db1, n_l, FF_BWD_COLS, "ffn1_bwd_w"))
    token = start_reduce(first_names, "ffn1")
    grad_x, meta_rows_grad, g_ffn1_norm = ffn_backward_input(
        dh1, h0, ffn1_norm, da1, db1, ffn1_w[0], ffn1_w[1], token, tm, "ffn1_bwd_in", examples=(n_b, seq))
    g_meta = sum_leading(meta_rows_grad, "meta_sum")

    grads, deltas, new_m, new_v = {}, {}, {}, {}

    def views(n):
        if n in ffn_names:
            return functools.partial(hidden_on_rows, n), functools.partial(hidden_on_rows_back, n)
        return (lambda t: t[0]), (lambda t: t[None])

    def finish_reduce(tag, previous):
        names, send, recv, srcs, lands = early[tag]
        srcs, lands = exchange_wait(send, recv, srcs, lands, previous, False, "rs_" + tag + "_wait")
        for n, g, land in zip(names, srcs, lands):
            two_d, back = views(n)
            out = adamw_exchanged(me_idx, g, land, two_d(w[n]), two_d(m[n]), two_d(v[n]), "adamw_" + n)
            grads[n], deltas[n], new_m[n], new_v[n] = (back(o) for o in out)
            previous = out[1]
        return previous

    previous = g_meta
    for tag in ("ffn2", "mix", "w_in"):
        previous = finish_reduce(tag, previous)

    zeros_meta = jnp.zeros((N_META, D_MODEL), F32)
    (late_parts,) = all_gather_list([_pack([g_ffn1_norm, g_meta])], previous, "ag_small_late")
    sg_src, (early_parts,) = exchange_wait(sg_send, sg_recv, sg_src, sg_land, late_parts, True, "ag_small_wait")
    early_parts = lax.dynamic_update_slice_in_dim(early_parts, sg_src[0][None], me, axis=0)

    def small_update(parts, names, extra, tag):
        pack_of = lambda d: _pack([d[n] for n in names] + extra)
        packed = adamw_small(parts, pack_of(w), pack_of(m), pack_of(v), "adamw_small_" + tag)
        unpacked = [_unpack(p, [w[n].shape for n in names] + [e.shape for e in extra]) for p in packed]
        for k, n in enumerate(names):
            grads[n], deltas[n], new_m[n], new_v[n] = (u[k] for u in unpacked)
        return packed, unpacked

    small_update(early_parts, early_small, [], "early")
    packed_out, unpacked = small_update(late_parts, ['ffn1_norm'], [zeros_meta], "late")
    g_meta_full = unpacked[0][-1]
    grads['meta_tokens'] = lax.dynamic_index_in_dim(
        g_meta_full.reshape(N_META, N_DEV, D_MODEL // N_DEV), me, axis=1, keepdims=False)
    deltas['meta_tokens'], new_m['meta_tokens'], new_v['meta_tokens'] = adamw_plain(
        grads['meta_tokens'], w['meta_tokens'], m['meta_tokens'], v['meta_tokens'], "adamw_meta")

    finish_reduce("ffn1", packed_out[0])

    return (loss, grad_x, *[grads[n] for n in WEIGHTS], *[deltas[n] for n in WEIGHTS],
            *[new_m[n] for n in WEIGHTS], *[new_v[n] for n in WEIGHTS])
```

```python
import functools

import jax
import jax.numpy as jnp
from jax import lax
from jax.experimental import pallas as pl
from jax.experimental.pallas import tpu as pltpu

F32 = jnp.float32
BF16 = jnp.bfloat16
MESH = pl.DeviceIdType.MESH

N_DEV = 8
D_MODEL = 1024
N_META = 16
HEAD_DIM = 64
N_KV_HEADS = 4
Q_PER_KV = 4
BLOCK = 128
KV_WIDTH = N_KV_HEADS * HEAD_DIM
SSM_GROUP = 16
SSM_WIDTH = 512
SSM_GROUPS = 32
SSM_STATE = 64
N_STATES = SSM_GROUPS * SSM_STATE
D_FF = 2816
FF_BLK = D_FF // N_DEV
IN_WIDTH = 4096
IN_BLK = IN_WIDTH // N_DEV
NORM_EPS = 1e-6
NEG_INF = -1e30
SCAN_COLS = 256
N_SCAN_BLK = N_STATES // SCAN_COLS
SUBLANES = 8
LANES = 128
MXU_WIDTH = 256
FF_BWD_COLS = MXU_WIDTH

ADAM_LR = 0.001
ADAM_B1 = 0.9
ADAM_B2 = 0.999
ADAM_EPS = 1e-08
ADAM_WD = 0.01
ADAM_STEP = 10

VMEM_BIG = 56 * 1024 * 1024


def _cp(sem=None, vmem=None):
    kw = {}
    if sem is not None:
        kw["dimension_semantics"] = sem
    if vmem is not None:
        kw["vmem_limit_bytes"] = vmem
    return pltpu.CompilerParams(**kw)


def _pcall(body, **kw):
    return pl.pallas_call(body, **kw)


def _dot(a, b):
    return jnp.dot(a, b, preferred_element_type=F32)


def _dot_nt(a, b):
    return lax.dot_general(a, b, (((1,), (1,)), ((), ())), preferred_element_type=F32)


def _dot_tn(a, b):
    return lax.dot_general(a, b, (((0,), (0,)), ((), ())), preferred_element_type=F32)


def _sigmoid(x):
    return 1.0 / (1.0 + jnp.exp(-x))


def _row_tile(rows, cap):
    best = None
    for t in range(16, min(rows, cap) + 1, 16):
        if rows % t == 0:
            best = t
    assert best is not None, rows
    return best


def _my_place():
    return lax.axis_index("x"), lax.axis_index("y"), lax.axis_index("c")


def all_gather_list(shards, after, name):
    n = len(shards)

    def body(*refs):
        ins, outs = refs[:n], refs[n + 1:2 * n + 1]
        send_sems, recv_sems, local_sems = refs[2 * n + 1:]
        x, y, c = _my_place()
        me, sibling = (x, y, c), (x, y, 1 - c)
        chips = [(1 - x, y), (x, 1 - y), (1 - x, 1 - y)]

        def blk(a, px, py, pc):
            return outs[a].at[4 * px + 2 * py + pc]

        def copy(a, k, block, to, src=None):
            return pltpu.make_async_remote_copy(
                src_ref=blk(a, *block) if src is None else src, dst_ref=blk(a, *block),
                send_sem=send_sems.at[a * 7 + k], recv_sem=recv_sems.at[a * 7 + k],
                device_id=to, device_id_type=MESH)

        mine = [pltpu.make_async_copy(ins[a], blk(a, *me), local_sems.at[a]) for a in range(n)]
        for cp in mine:
            cp.start()
        first = []
        for a in range(n):
            first.append(copy(a, 0, me, sibling, src=ins[a]))
            first += [copy(a, 1 + j, me, (*chip, c), src=ins[a]) for j, chip in enumerate(chips)]
        for cp in first:
            cp.start()
        passed = []
        for j, chip in enumerate(chips):
            for a in range(n):
                copy(a, 1 + j, (*chip, c), me).wait_recv()
                cp = copy(a, 4 + j, (*chip, c), sibling)
                cp.start()
                passed.append(cp)
        for a in range(n):
            copy(a, 0, sibling, me).wait_recv()
            for j, chip in enumerate(chips):
                copy(a, 4 + j, (*chip, 1 - c), me).wait_recv()
        for cp in first + passed:
            cp.wait_send()
        for cp in mine:
            cp.wait()

    any_spec = pl.BlockSpec(memory_space=pl.ANY)
    return _pcall(
        body, name=name,
        out_shape=[jax.ShapeDtypeStruct((N_DEV,) + s.shape, s.dtype) for s in shards],
        in_specs=[any_spec] * (n + 1), out_specs=[any_spec] * n,
        scratch_shapes=[pltpu.SemaphoreType.DMA((7 * n,)), pltpu.SemaphoreType.DMA((7 * n,)),
                        pltpu.SemaphoreType.DMA((n,))],
    )(*shards, after)


HBM_SPEC = pl.BlockSpec(memory_space=pltpu.HBM)
SEM_SPEC = pl.BlockSpec(memory_space=pltpu.SEMAPHORE)
N_PEERS = N_DEV - 1


def _related(k):
    x, y, c = _my_place()
    px = 1 - x if k & 4 else x
    py = 1 - y if k & 2 else y
    pc = 1 - c if k & 1 else c
    return (px, py, pc), 4 * px + 2 * py + pc


def _exchange_copies(srcs, lands, send_sems, recv_sems, gather):
    x, y, c = _my_place()
    me = 4 * x + 2 * y + c
    copies = []
    for a, (src, land) in enumerate(zip(srcs, lands)):
        for k in range(1, N_DEV):
            peer, d = _related(k)
            copies.append(pltpu.make_async_remote_copy(
                src_ref=src if gather else src.at[d], dst_ref=land.at[me] if gather else land.at[k],
                send_sem=send_sems.at[a * N_PEERS + k - 1], recv_sem=recv_sems.at[a * N_PEERS + k - 1],
                device_id=peer, device_id_type=MESH))
    return copies


def exchange_start(srcs, after, gather, name):
    n = len(srcs)
    land_shapes = [((N_DEV,) + s.shape) if gather else s.shape for s in srcs]

    def body(*refs):
        send_sems, recv_sems = refs[2 * n + 1], refs[2 * n + 2]
        for cp in _exchange_copies(refs[:n], refs[n:2 * n], send_sems, recv_sems, gather):
            cp.start()
        token = refs[-1]
        token[...] = jnp.zeros_like(token)

    sems = pltpu.SemaphoreType.DMA((n * N_PEERS,))
    lands = [pltpu.with_memory_space_constraint(lax.empty(shape, s.dtype), pltpu.HBM) for shape, s in zip(land_shapes, srcs)]
    out = _pcall(
        body, name=name,
        out_shape=(sems, sems, *[pltpu.HBM(s.shape, s.dtype) for s in srcs],
                   *[pltpu.HBM(shape, s.dtype) for shape, s in zip(land_shapes, srcs)],
                   jax.ShapeDtypeStruct((SUBLANES, LANES), F32)),
        in_specs=[HBM_SPEC] * (2 * n) + [pl.BlockSpec(memory_space=pl.ANY)],
        out_specs=(SEM_SPEC, SEM_SPEC, *[HBM_SPEC] * (2 * n), pl.BlockSpec(memory_space=pltpu.VMEM)),
        input_output_aliases={i: 2 + i for i in range(2 * n)},
        compiler_params=pltpu.CompilerParams(has_side_effects=pltpu.SideEffectType.DATAFLOW_SIDE_EFFECTING),
    )(*[pltpu.with_memory_space_constraint(s, pltpu.HBM) for s in srcs], *lands, after)
    return out[0], out[1], list(out[2:2 + n]), list(out[2 + n:2 + 2 * n]), out[-1]


def exchange_wait(send_sems, recv_sems, srcs, lands, after, gather, name):
    n = len(srcs)

    def body(*refs):
        for cp in _exchange_copies(refs[:n], refs[n:2 * n], refs[2 * n], refs[2 * n + 1], gather):
            cp.wait_send()
            cp.wait_recv()

    out = _pcall(
        body, name=name,
        out_shape=(*[pltpu.HBM(s.shape, s.dtype) for s in srcs], *[pltpu.HBM(z.shape, z.dtype) for z in lands]),
        in_specs=[HBM_SPEC] * (2 * n) + [SEM_SPEC, SEM_SPEC, pl.BlockSpec(memory_space=pl.ANY)],
        out_specs=tuple([HBM_SPEC] * (2 * n)),
        input_output_aliases={i: i for i in range(2 * n)},
        compiler_params=pltpu.CompilerParams(has_side_effects=pltpu.SideEffectType.DATAFLOW_SIDE_EFFECTING),
    )(*srcs, *lands, send_sems, recv_sems, after)
    return list(out[:n]), list(out[n:])


def adamw_exchanged(me, g, land, w, m, v, name):
    rows, cols = w.shape
    tr = _row_tile(rows, 256)

    def body(me_ref, g_ref, land_ref, w_ref, m_ref, v_ref, go_ref, d_ref, mo_ref, vo_ref):
        grad = g_ref[...].astype(F32)
        for k in range(1, N_DEV):
            grad = grad + land_ref[k].astype(F32)
        delta, m_new, v_new = _adam_math(w_ref[...], grad, m_ref[...], v_ref[...])
        go_ref[...] = grad
        d_ref[...] = delta
        mo_ref[...] = m_new
        vo_ref[...] = v_new

    tile = pl.BlockSpec((tr, cols), lambda r, ix: (r, 0))
    out = jax.ShapeDtypeStruct((rows, cols), F32)
    return _pcall(
        body, name=name, out_shape=[out] * 4,
        grid_spec=pltpu.PrefetchScalarGridSpec(
            num_scalar_prefetch=1, grid=(rows // tr,),
            in_specs=[pl.BlockSpec((None, tr, cols), lambda r, ix: (ix[0], r, 0)),
                      pl.BlockSpec((N_DEV, tr, cols), lambda r, ix: (0, r, 0)), tile, tile, tile],
            out_specs=[tile] * 4),
        compiler_params=_cp(("arbitrary",)),
    )(me, g, land, w, m, v)


def _adam_math(w, g, m, v):
    m = ADAM_B1 * m + (1.0 - ADAM_B1) * g
    v = ADAM_B2 * v + (1.0 - ADAM_B2) * (g * g)
    m_hat = m / (1.0 - ADAM_B1 ** ADAM_STEP)
    v_hat = v / (1.0 - ADAM_B2 ** ADAM_STEP)
    delta = -ADAM_LR * (m_hat / (jnp.sqrt(v_hat) + ADAM_EPS) + ADAM_WD * w)
    return delta, m, v


def adamw_small(parts, w, m, v, name):
    _, rows, cols = parts.shape

    def body(p_ref, w_ref, m_ref, v_ref, go_ref, d_ref, mo_ref, vo_ref):
        grad = p_ref[0]
        for k in range(1, N_DEV):
            grad = grad + p_ref[k]
        delta, m_new, v_new = _adam_math(w_ref[...], grad, m_ref[...], v_ref[...])
        go_ref[...] = grad
        d_ref[...] = delta
        mo_ref[...] = m_new
        vo_ref[...] = v_new

    out = jax.ShapeDtypeStruct((rows, cols), F32)
    return _pcall(body, name=name, out_shape=[out] * 4, compiler_params=_cp(vmem=VMEM_BIG))(parts, w, m, v)


def adamw_plain(g, w, m, v, name):
    def body(g_ref, w_ref, m_ref, v_ref, d_ref, mo_ref, vo_ref):
        delta, m_new, v_new = _adam_math(w_ref[...], g_ref[...], m_ref[...], v_ref[...])
        d_ref[...] = delta
        mo_ref[...] = m_new
        vo_ref[...] = v_new

    out = jax.ShapeDtypeStruct(w.shape, F32)
    return _pcall(body, name=name, out_shape=[out] * 3)(g, w, m, v)


def _rms_fwd(x, g):
    r = lax.rsqrt(jnp.mean(x * x, axis=-1, keepdims=True) + NORM_EPS)
    return x * r * g


def _rms_bwd(x, g, dy):
    r = lax.rsqrt(jnp.mean(x * x, axis=-1, keepdims=True) + NORM_EPS)
    xh = x * r
    t = dy * g
    dx = r * (t - xh * jnp.mean(t * xh, axis=-1, keepdims=True))
    return dx, jnp.sum(dy * xh, axis=0, keepdims=True)


def _accumulate(ref, val, first):
    @pl.when(first)
    def _():
        ref[...] = val

    @pl.when(jnp.logical_not(first))
    def _():
        ref[...] += val


def _col_chunks(width):
    return [(c0, min(MXU_WIDTH, width - c0)) for c0 in range(0, width, MXU_WIDTH)]


ANY_SPEC = pl.BlockSpec(memory_space=pl.ANY)


def ffn_forward(h, norm, w1, w3, w2, after, tm, name, meta=None):
    if meta is None:
        t_rows = h.shape[0]
        h_spec = pl.BlockSpec((tm, D_MODEL), lambda i: (i, 0))
    else:
        tiles = (h.shape[1] + N_META) // tm
        t_rows = h.shape[0] * tiles * tm
        h_spec = pl.BlockSpec((None, tm, D_MODEL), lambda i: (i // tiles, i % tiles, 0))

    def body(h_ref, g_ref, w1_ref, w3_ref, w2_ref, _, *rest):
        if meta is None:
            out_ref, hn_ref, a_ref, b_ref, hid_ref = rest
            h_in = h_ref[...]
        else:
            meta_ref, out_ref, hn_ref, a_ref, b_ref, h0_ref, hid_ref = rest
            h_in = h_ref[...]
            with_meta = jnp.concatenate([h_in[:tm - N_META], meta_ref[...]], axis=0)
            h_in = jnp.where(pl.program_id(0) % tiles == tiles - 1, with_meta, h_in)
            h0_ref[...] = h_in
        hn = _rms_fwd(h_in, g_ref[...]).astype(BF16)
        hn_ref[...] = hn
        for c0, cw in _col_chunks(D_FF):
            a = _dot_nt(hn, w1_ref[c0:c0 + cw, :])
            b = _dot_nt(hn, w3_ref[c0:c0 + cw, :])
            a_ref[:, c0:c0 + cw] = a.astype(BF16)
            b_ref[:, c0:c0 + cw] = b.astype(BF16)
            hid_ref[:, c0:c0 + cw] = (a * _sigmoid(a) * b).astype(BF16)
        out_ref[...] = h_in + 0.5 * _dot(hid_ref[...], w2_ref[...])

    row = pl.BlockSpec((tm, D_MODEL), lambda i: (i, 0))
    hid_blk = pl.BlockSpec((tm, D_FF), lambda i: (i, 0))
    weight = _resident((D_FF, D_MODEL))
    wide = jax.ShapeDtypeStruct((t_rows, D_MODEL), F32)
    extra_in = [] if meta is None else [meta]
    return _pcall(
        body, name=name, grid=(t_rows // tm,),
        in_specs=[h_spec, pl.BlockSpec((1, D_MODEL), lambda i: (0, 0)), weight, weight, weight, ANY_SPEC]
        + [pl.BlockSpec((N_META, D_MODEL), lambda i: (0, 0))] * len(extra_in),
        out_specs=[row, row, hid_blk, hid_blk] + [row] * len(extra_in),
        out_shape=[wide, jax.ShapeDtypeStruct((t_rows, D_MODEL), BF16),
                   jax.ShapeDtypeStruct((t_rows, D_FF), BF16), jax.ShapeDtypeStruct((t_rows, D_FF), BF16)]
        + [wide] * len(extra_in),
        scratch_shapes=[pltpu.VMEM((tm, D_FF), BF16)],
        compiler_params=_cp(("arbitrary",), VMEM_BIG),
    )(h, norm, w1, w3, w2, after, *extra_in)


def _resident(shape):
    return pl.BlockSpec(shape, lambda *_: (0,) * len(shape), pipeline_mode=pl.Buffered(1))


def ffn_backward_hidden(dh, a, b, w2, after, tm, name):
    t_rows = dh.shape[0]

    def body(dh_ref, a_ref, b_ref, w2_ref, _, da_ref, db_ref, dhb_ref):
        dhb = (0.5 * dh_ref[...]).astype(BF16)
        dhb_ref[...] = dhb
        for c0, cw in _col_chunks(D_FF):
            dhid = _dot_nt(dhb, w2_ref[c0:c0 + cw, :])
            av = a_ref[:, c0:c0 + cw].astype(F32)
            bv = b_ref[:, c0:c0 + cw].astype(F32)
            s = _sigmoid(av)
            da_ref[:, c0:c0 + cw] = (dhid * bv * (s * (1.0 + av * (1.0 - s)))).astype(BF16)
            db_ref[:, c0:c0 + cw] = (dhid * (av * s)).astype(BF16)

    hid = pl.BlockSpec((tm, D_FF), lambda i: (i, 0))
    row = pl.BlockSpec((tm, D_MODEL), lambda i: (i, 0))
    return _pcall(
        body, name=name, grid=(t_rows // tm,),
        in_specs=[row, hid, hid, _resident((D_FF, D_MODEL)), ANY_SPEC],
        out_specs=[hid, hid, row],
        out_shape=[jax.ShapeDtypeStruct((t_rows, D_FF), BF16), jax.ShapeDtypeStruct((t_rows, D_FF), BF16),
                   jax.ShapeDtypeStruct((t_rows, D_MODEL), BF16)],
        compiler_params=_cp(("arbitrary",), VMEM_BIG),
    )(dh, a, b, w2, after)


def ffn_backward_input(dh, h, norm, da, db, w1, w3, after, tm, name, examples=None):
    t_rows = h.shape[0]

    def body(dh_ref, h_ref, g_ref, da_ref, db_ref, w1_ref, w3_ref, _, dhin_ref, *rest):
        dg_ref = rest[-1]
        dhn = _dot(da_ref[...], w1_ref[...]) + _dot(db_ref[...], w3_ref[...])
        dx, dg = _rms_bwd(h_ref[...], g_ref[...], dhn)
        dhin = dh_ref[...] + dx
        dhin_ref[...] = dhin
        _accumulate(dg_ref, dg, pl.program_id(0) == 0)
        if examples is not None:
            @pl.when(pl.program_id(0) % tiles == tiles - 1)
            def _():
                rest[0][...] = dhin[tm - N_META:, :]

    row = pl.BlockSpec((tm, D_MODEL), lambda i: (i, 0))
    vec = pl.BlockSpec((1, D_MODEL), lambda i: (0, 0))
    hid = pl.BlockSpec((tm, D_FF), lambda i: (i, 0))
    if examples is None:
        out_specs = [row, vec]
        out_shape = [jax.ShapeDtypeStruct((t_rows, D_MODEL), F32), jax.ShapeDtypeStruct((1, D_MODEL), F32)]
    else:
        n_b, seq = examples
        tiles = (seq + N_META) // tm
        out_specs = [pl.BlockSpec((None, tm, D_MODEL), lambda i: (i // tiles, i % tiles, 0)),
                     pl.BlockSpec((None, N_META, D_MODEL), lambda i: (i // tiles, 0, 0)), vec]
        out_shape = [jax.ShapeDtypeStruct((n_b, seq, D_MODEL), F32), jax.ShapeDtypeStruct((n_b, N_META, D_MODEL), F32),
                     jax.ShapeDtypeStruct((1, D_MODEL), F32)]
    return _pcall(
        body, name=name, grid=(t_rows // tm,),
        in_specs=[row, row, vec, hid, hid, _resident((D_FF, D_MODEL)), _resident((D_FF, D_MODEL)), ANY_SPEC],
        out_specs=out_specs, out_shape=out_shape,
        compiler_params=_cp(("arbitrary",), VMEM_BIG),
    )(dh, h, norm, da, db, w1, w3, after)


def ffn_backward_weights(hn, dh, a, b, da, db, tm, tn, name):
    t_rows = hn.shape[0]
    ni = t_rows // tm
    kc = _row_tile(tm, 688)

    def body(hn_ref, dh_ref, a_ref, b_ref, da_ref, db_ref, dw1_ref, dw3_ref, dw2_ref, acc1, acc3, acc2):
        i = pl.program_id(1)
        parts = None
        for r0 in range(0, tm, kc):
            rows = slice(r0, r0 + kc)
            hn_v = hn_ref[rows, :]
            av = a_ref[rows, :].astype(F32)
            hid = (av * _sigmoid(av) * b_ref[rows, :].astype(F32)).astype(BF16)
            new = (_dot_tn(hn_v, da_ref[rows, :]), _dot_tn(hn_v, db_ref[rows, :]), _dot_tn(dh_ref[rows, :], hid))
            parts = new if parts is None else tuple(p + q for p, q in zip(parts, new))
        _accumulate(acc1, parts[0], i == 0)
        _accumulate(acc3, parts[1], i == 0)
        _accumulate(acc2, parts[2], i == 0)

        @pl.when(i == ni - 1)
        def _():
            dw1_ref[...] = acc1[...].T.astype(BF16)
            dw3_ref[...] = acc3[...].T.astype(BF16)
            dw2_ref[...] = acc2[...].T.astype(BF16)

    row = pl.BlockSpec((tm, D_MODEL), lambda j, i: (i, 0))
    hid_blk = pl.BlockSpec((tm, tn), lambda j, i: (i, j))
    w_row = pl.BlockSpec((tn, D_MODEL), lambda j, i: (j, 0))
    out = jax.ShapeDtypeStruct((D_FF, D_MODEL), BF16)
    return _pcall(
        body, name=name, grid=(D_FF // tn, ni),
        in_specs=[row, row, hid_blk, hid_blk, hid_blk, hid_blk],
        out_specs=[w_row, w_row, w_row], out_shape=[out, out, out],
        scratch_shapes=[pltpu.VMEM((D_MODEL, tn), F32)] * 3,
        compiler_params=_cp(("arbitrary", "arbitrary"), VMEM_BIG),
    )(hn, dh, a, b, da, db)


def mix_forward(h, norm, wing, tm, name):
    t_rows = h.shape[0]

    def body(h_ref, g_ref, w_ref, hn_ref, p_ref):
        hn = _rms_fwd(h_ref[...], g_ref[...]).astype(BF16)
        hn_ref[...] = hn
        for j in range(N_DEV):
            p_ref[:, j * IN_BLK:(j + 1) * IN_BLK] = _dot(hn, w_ref[j]).astype(BF16)

    row = pl.BlockSpec((tm, D_MODEL), lambda i: (i, 0))
    return _pcall(
        body, name=name, grid=(t_rows // tm,),
        in_specs=[row, pl.BlockSpec((1, D_MODEL), lambda i: (0, 0)),
                  pl.BlockSpec((N_DEV, D_MODEL, IN_BLK), lambda i: (0, 0, 0))],
        out_specs=[row, pl.BlockSpec((tm, IN_WIDTH), lambda i: (i, 0))],
        out_shape=[jax.ShapeDtypeStruct((t_rows, D_MODEL), BF16), jax.ShapeDtypeStruct((t_rows, IN_WIDTH), BF16)],
        compiler_params=_cp(("arbitrary",), VMEM_BIG),
    )(h, norm, wing)


def mix_backward_act(dh, h, norm, dproj, w_in_full, tm, name):
    t_rows = h.shape[0]

    def body(dh_ref, h_ref, g_ref, dp_ref, w_ref, dhin_ref, dg_ref):
        dx, dg = _rms_bwd(h_ref[...], g_ref[...], _dot_nt(dp_ref[...], w_ref[...]))
        dhin_ref[...] = dh_ref[...] + dx
        _accumulate(dg_ref, dg, pl.program_id(0) == 0)

    row = pl.BlockSpec((tm, D_MODEL), lambda i: (i, 0))
    vec = pl.BlockSpec((1, D_MODEL), lambda i: (0, 0))
    return _pcall(
        body, name=name, grid=(t_rows // tm,),
        in_specs=[row, row, vec, pl.BlockSpec((tm, IN_WIDTH), lambda i: (i, 0)), _resident((D_MODEL, IN_WIDTH))],
        out_specs=[row, vec],
        out_shape=[jax.ShapeDtypeStruct((t_rows, D_MODEL), F32), jax.ShapeDtypeStruct((1, D_MODEL), F32)],
        compiler_params=_cp(("arbitrary",), VMEM_BIG),
    )(dh, h, norm, dproj, w_in_full)


def mix_backward_weights(hn, dproj, tm, name):
    t_rows = hn.shape[0]
    ni = t_rows // tm
    per_step = 2

    kc = _row_tile(tm, 688)

    def body(hn_ref, dp_ref, dw_ref, acc):
        i = pl.program_id(1)
        part = functools.reduce(lambda u, w: u + w, [_dot_tn(hn_ref[r0:r0 + kc, :], dp_ref[r0:r0 + kc, :])
                                                    for r0 in range(0, tm, kc)])
        _accumulate(acc, part, i == 0)

        @pl.when(i == ni - 1)
        def _():
            for k in range(per_step):
                dw_ref[k] = acc[:, k * IN_BLK:(k + 1) * IN_BLK].astype(BF16)

    return _pcall(
        body, name=name, grid=(N_DEV // per_step, ni),
        in_specs=[pl.BlockSpec((tm, D_MODEL), lambda j, i: (i, 0)),
                  pl.BlockSpec((tm, per_step * IN_BLK), lambda j, i: (i, j))],
        out_specs=pl.BlockSpec((per_step, D_MODEL, IN_BLK), lambda j, i: (j, 0, 0)),
        out_shape=jax.ShapeDtypeStruct((N_DEV, D_MODEL, IN_BLK), BF16),
        scratch_shapes=[pltpu.VMEM((D_MODEL, per_step * IN_BLK), F32)],
        compiler_params=_cp(("arbitrary", "arbitrary"), VMEM_BIG),
    )(hn, dproj)


GELU_C = 0.7978845608028654
GELU_K = 0.044715


def _gelu(x):
    return 0.5 * x * (1.0 + jnp.tanh(GELU_C * (x + GELU_K * (x * x * x))))


def _gelu_and_grad(x):
    th = jnp.tanh(GELU_C * (x + GELU_K * (x * x * x)))
    val = 0.5 * x * (1.0 + th)
    grad = 0.5 * (1.0 + th) + 0.5 * x * (1.0 - th * th) * (GELU_C * (1.0 + 3.0 * GELU_K * (x * x)))
    return val, grad


def merge_forward(h, yraw, attn, proj, glu_a, glu_b, w_out, tm, name):
    t_rows = h.shape[0]

    def body(h_ref, y_ref, at_ref, gate_ref, a_ref, b_ref, wo_ref, out_ref):
        y = _gelu(y_ref[...]).astype(BF16)
        ssm = _dot(y, a_ref[...]) * _sigmoid(_dot(y, b_ref[...]))
        ga = gate_ref[:, :D_MODEL].astype(F32)
        gs = gate_ref[:, D_MODEL:].astype(F32)
        merged = _sigmoid(ga) * at_ref[...].astype(F32) + _sigmoid(gs) * ssm
        out_ref[...] = h_ref[...] + _dot(merged.astype(BF16), wo_ref[...])

    row = pl.BlockSpec((tm, D_MODEL), lambda i: (i, 0))
    glu = pl.BlockSpec((SSM_WIDTH, D_MODEL), lambda i: (0, 0))
    return _pcall(
        body, name=name, grid=(t_rows // tm,),
        in_specs=[row, pl.BlockSpec((tm, SSM_WIDTH), lambda i: (i, 0)), row,
                  pl.BlockSpec((tm, 2 * D_MODEL), lambda i: (i, 1)), glu, glu,
                  pl.BlockSpec((D_MODEL, D_MODEL), lambda i: (0, 0))],
        out_specs=row, out_shape=jax.ShapeDtypeStruct((t_rows, D_MODEL), F32),
        compiler_params=_cp(("arbitrary",), VMEM_BIG),
    )(h, yraw, attn, proj, glu_a, glu_b, w_out)


def merge_backward(dh, yraw, attn, proj, glu_a, glu_b, w_out, after, tm, name):
    t_rows = dh.shape[0]

    def body(dh_ref, y_ref, at_ref, gate_ref, a_ref, b_ref, wo_ref, _,
             dat_ref, dy_ref, dgate_ref, d16_ref, mg_ref, y16_ref, dya_ref, dyb_ref):
        d16 = dh_ref[...].astype(BF16)
        d16_ref[...] = d16
        gel, dgel = _gelu_and_grad(y_ref[...].astype(F32))
        y16 = gel.astype(BF16)
        y16_ref[...] = y16
        dy = None
        for c0, cw in _col_chunks(D_MODEL):
            cols = slice(c0, c0 + cw)
            gcols = slice(D_MODEL + c0, D_MODEL + c0 + cw)
            dmerged = _dot_nt(d16, wo_ref[cols, :])
            ya = _dot(y16, a_ref[:, cols])
            sb = _sigmoid(_dot(y16, b_ref[:, cols]))
            ssm = ya * sb
            sa = _sigmoid(gate_ref[:, cols].astype(F32))
            ss = _sigmoid(gate_ref[:, gcols].astype(F32))
            attn_v = at_ref[:, cols].astype(F32)
            mg_ref[:, cols] = (sa * attn_v + ss * ssm).astype(BF16)
            dat_ref[:, cols] = (dmerged * sa).astype(BF16)
            dgate_ref[:, cols] = (dmerged * attn_v * sa * (1.0 - sa)).astype(BF16)
            dgate_ref[:, gcols] = (dmerged * ssm * ss * (1.0 - ss)).astype(BF16)
            dssm = dmerged * ss
            dya = (dssm * sb).astype(BF16)
            dyb = (dssm * ya * sb * (1.0 - sb)).astype(BF16)
            dya_ref[:, cols] = dya
            dyb_ref[:, cols] = dyb
            part = _dot_nt(dya, a_ref[:, cols]) + _dot_nt(dyb, b_ref[:, cols])
            dy = part if dy is None else dy + part
        dy_ref[...] = (dy * dgel).astype(BF16)

    row = pl.BlockSpec((tm, D_MODEL), lambda i: (i, 0))
    ssm_row = pl.BlockSpec((tm, SSM_WIDTH), lambda i: (i, 0))
    gates = pl.BlockSpec((tm, 2 * D_MODEL), lambda i: (i, 1))
    wide = jax.ShapeDtypeStruct((t_rows, D_MODEL), BF16)
    narrow = jax.ShapeDtypeStruct((t_rows, SSM_WIDTH), BF16)
    return _pcall(
        body, name=name, grid=(t_rows // tm,),
        in_specs=[row, ssm_row, row, gates, _resident((SSM_WIDTH, D_MODEL)), _resident((SSM_WIDTH, D_MODEL)),
                  _resident((D_MODEL, D_MODEL)), ANY_SPEC],
        out_specs=[row, ssm_row, gates, row, row, ssm_row, row, row],
        out_shape=[wide, narrow, jax.ShapeDtypeStruct((t_rows, IN_WIDTH), BF16), wide, wide, narrow, wide, wide],
        compiler_params=_cp(("arbitrary",), VMEM_BIG),
    )(dh, yraw, attn, proj, glu_a, glu_b, w_out, after)


def merge_backward_weights(d16, merged, y16, dya, dyb, tm, name):
    t_rows = d16.shape[0]

    def body(d_ref, mg_ref, y_ref, dya_ref, dyb_ref, dwo_ref, da_ref, db_ref):
        first = pl.program_id(0) == 0
        y16 = y_ref[...]
        _accumulate(dwo_ref, _dot_tn(mg_ref[...], d_ref[...]), first)
        _accumulate(da_ref, _dot_tn(y16, dya_ref[...]), first)
        _accumulate(db_ref, _dot_tn(y16, dyb_ref[...]), first)

    row = pl.BlockSpec((tm, D_MODEL), lambda i: (i, 0))
    ssm_row = pl.BlockSpec((tm, SSM_WIDTH), lambda i: (i, 0))
    glu = pl.BlockSpec((SSM_WIDTH, D_MODEL), lambda i: (0, 0))
    wo = pl.BlockSpec((D_MODEL, D_MODEL), lambda i: (0, 0))
    return _pcall(
        body, name=name, grid=(t_rows // tm,),
        in_specs=[row, row, ssm_row, row, row], out_specs=[wo, glu, glu],
        out_shape=[jax.ShapeDtypeStruct((D_MODEL, D_MODEL), F32), jax.ShapeDtypeStruct((SSM_WIDTH, D_MODEL), F32),
                   jax.ShapeDtypeStruct((SSM_WIDTH, D_MODEL), F32)],
        compiler_params=_cp(("arbitrary",), VMEM_BIG),
    )(d16, merged, y16, dya, dyb)


def final_loss_backward(h, target, norm, seq, tm, name):
    t_rows = h.shape[0]
    tiles_per_example = (seq + N_META) // tm

    def body(h_ref, t_ref, g_ref, dh_ref, loss_ref, dg_ref):
        i = pl.program_id(0)
        x = h_ref[...]
        g = g_ref[...]
        r = lax.rsqrt(jnp.mean(x * x, axis=-1, keepdims=True) + NORM_EPS)
        xh = x * r
        pos = lax.broadcasted_iota(jnp.int32, (tm, 1), 0) + (i % tiles_per_example) * tm
        diff = jnp.where(pos < seq, xh * g - t_ref[...], 0.0)
        part = 0.5 * jnp.sum(jnp.sum(diff * diff, axis=-1, keepdims=True), axis=0, keepdims=True) / D_MODEL
        dy = diff / D_MODEL
        t = dy * g
        dh_ref[...] = r * (t - xh * jnp.mean(t * xh, axis=-1, keepdims=True))
        _accumulate(loss_ref, jnp.broadcast_to(part, (1, LANES)), i == 0)
        _accumulate(dg_ref, jnp.sum(dy * xh, axis=0, keepdims=True), i == 0)

    row = pl.BlockSpec((tm, D_MODEL), lambda i: (i, 0))
    vec = pl.BlockSpec((1, D_MODEL), lambda i: (0, 0))
    per_example = pl.BlockSpec((None, tm, D_MODEL), lambda i: (i // tiles_per_example, i % tiles_per_example, 0))
    return _pcall(
        body, name=name, grid=(t_rows // tm,),
        in_specs=[row, per_example, vec],
        out_specs=[row, pl.BlockSpec((1, LANES), lambda i: (0, 0)), vec],
        out_shape=[jax.ShapeDtypeStruct((t_rows, D_MODEL), F32), jax.ShapeDtypeStruct((1, LANES), F32),
                   jax.ShapeDtypeStruct((1, D_MODEL), F32)],
        compiler_params=_cp(("arbitrary",), VMEM_BIG),
    )(h, target, norm)


ATTN_SCALE = HEAD_DIM ** -0.5
STACK_HEADS = (0, 2, 1, 3)
META_PAD = LANES - N_META


def _lane_half(shape, hf):
    lane = lax.broadcasted_iota(jnp.int32, shape, 1)
    return (lane < HEAD_DIM) if hf == 0 else (lane >= HEAD_DIM)


def _kv_variants(ref, rows, kh, pad_rows=0):
    tile = kh // 2
    t = ref[rows, tile * LANES:(tile + 1) * LANES].astype(F32)
    swapped = pltpu.roll(t, HEAD_DIM, axis=1)
    at_low, at_high = (t, swapped) if kh % 2 == 0 else (swapped, t)
    lo = jnp.where(_lane_half(t.shape, 0), at_low, 0.0).astype(BF16)
    hi = jnp.where(_lane_half(t.shape, 1), at_high, 0.0).astype(BF16)
    if pad_rows:
        zeros = jnp.zeros((pad_rows, LANES), BF16)
        lo, hi = jnp.concatenate([lo, zeros], axis=0), jnp.concatenate([hi, zeros], axis=0)
    return lo, hi


def _key_tiles(ref, key_rows, kh):
    return [_kv_variants(ref, r, kh, META_PAD if i == len(key_rows) - 1 else 0) for i, r in enumerate(key_rows)]


def _to_kv_lanes(lo, hi, kh):
    lo = jnp.where(_lane_half(lo.shape, 0), lo, 0.0)
    hi = jnp.where(_lane_half(hi.shape, 1), hi, 0.0)
    if kh % 2 == 0:
        return lo + pltpu.roll(hi, HEAD_DIM, axis=1)
    return pltpu.roll(lo, HEAD_DIM, axis=1) + hi


def _stacked(ref, rows, kh):
    col = kh * 2 * LANES
    return jnp.concatenate([ref[rows, col:col + LANES], ref[rows, col + LANES:col + 2 * LANES]], axis=0)


def _sink_column(sink_ref, kh, nq):
    row = lax.broadcasted_iota(jnp.int32, (4 * nq, 1), 0)
    col = jnp.zeros((4 * nq, 1), F32)
    for quarter, g in enumerate(STACK_HEADS):
        col = jnp.where(row // nq == quarter, sink_ref[0, kh * Q_PER_KV + g], col)
    return col


def _softmax_parts(qs, key_tiles, masks, sink):
    scores = []
    for (k_lo, k_hi), mask in zip(key_tiles, masks):
        s = jnp.concatenate([_dot_nt(qs, k_lo), _dot_nt(qs, k_hi)], axis=0) * ATTN_SCALE
        scores.append(s if mask is None else jnp.where(mask, s, NEG_INF))
    m = jnp.maximum(_row_reduce(scores, jnp.maximum, jnp.max), sink)
    probs = [jnp.exp(s - m) for s in scores]
    e_sink = jnp.exp(sink - m)
    den = _row_sums(probs) + e_sink
    return probs, 1.0 / den, e_sink


def _row_reduce(tiles, combine, reduce):
    chunks = [t[:, c:c + LANES] for t in tiles for c in range(0, t.shape[-1], LANES)]
    return reduce(functools.reduce(combine, chunks), axis=-1, keepdims=True)


def _row_sums(tiles):
    return _row_reduce(tiles, lambda u, w: u + w, jnp.sum)


def _band_mask(nq, first):
    keys = BLOCK if first else 2 * BLOCK
    qi = lax.broadcasted_iota(jnp.int32, (4 * nq, keys), 0) % nq
    kj = lax.broadcasted_iota(jnp.int32, (4 * nq, keys), 1)
    if first:
        return kj <= qi
    return jnp.logical_and(kj > qi, kj <= qi + BLOCK)


def _meta_mask(nq, causal):
    qi = lax.broadcasted_iota(jnp.int32, (4 * nq, LANES), 0) % nq
    kj = lax.broadcasted_iota(jnp.int32, (4 * nq, LANES), 1)
    return jnp.logical_and(kj < N_META, kj <= qi) if causal else kj < N_META


def _attention_schedule(seq, queries, carry):
    meta_rows = pl.ds(seq, N_META)
    meta_ok = _meta_mask(BLOCK, False)
    carry = queries(pl.ds(0, BLOCK), BLOCK, [pl.ds(0, BLOCK), meta_rows], [_band_mask(BLOCK, True), meta_ok], carry)

    def block(n, c):
        r0 = pl.multiple_of(n * BLOCK, BLOCK)
        p0 = pl.multiple_of((n - 1) * BLOCK, BLOCK)
        return queries(pl.ds(r0, BLOCK), BLOCK, [pl.ds(p0, 2 * BLOCK), meta_rows], [_band_mask(BLOCK, False), meta_ok], c)

    carry = lax.fori_loop(1, seq // BLOCK, block, carry)
    return queries(meta_rows, N_META, [meta_rows], [_meta_mask(N_META, True)], carry)


def attention_forward(proj3, sinks, seq, name):
    n_b, n_l, _ = proj3.shape

    def body(sink_ref, q_ref, k_ref, v_ref, o_ref):
        def queries(q_rows, nq, key_rows, masks, carry):
            for kh in range(N_KV_HEADS):
                ks = _key_tiles(k_ref, key_rows, kh)
                vs = _key_tiles(v_ref, key_rows, kh)
                qs = _stacked(q_ref, q_rows, kh)
                probs, inv, _ = _softmax_parts(qs, ks, masks, _sink_column(sink_ref, kh, nq))
                probs = [p.astype(BF16) for p in probs]
                o_lo = functools.reduce(lambda u, w: u + w, [_dot(p[:2 * nq], v_lo) for p, (v_lo, _) in zip(probs, vs)])
                o_hi = functools.reduce(lambda u, w: u + w, [_dot(p[2 * nq:], v_hi) for p, (_, v_hi) in zip(probs, vs)])
                out = (o_lo * inv[:2 * nq] + o_hi * inv[2 * nq:]).astype(BF16)
                col = kh * 2 * LANES
                o_ref[q_rows, col:col + LANES] = out[:nq]
                o_ref[q_rows, col + LANES:col + 2 * LANES] = out[nq:]
            return carry

        _attention_schedule(seq, queries, 0)

    return _pcall(
        body, name=name, grid=(n_b,),
        in_specs=[pl.BlockSpec(memory_space=pltpu.SMEM),
                  pl.BlockSpec((None, n_l, D_MODEL), lambda b: (b, 0, 0)),
                  pl.BlockSpec((None, n_l, KV_WIDTH), lambda b: (b, 0, D_MODEL // KV_WIDTH)),
                  pl.BlockSpec((None, n_l, KV_WIDTH), lambda b: (b, 0, D_MODEL // KV_WIDTH + 1))],
        out_specs=pl.BlockSpec((None, n_l, D_MODEL), lambda b: (b, 0, 0)),
        out_shape=jax.ShapeDtypeStruct((n_b, n_l, D_MODEL), BF16),
        compiler_params=_cp(("arbitrary",), VMEM_BIG),
    )(sinks, proj3, proj3, proj3)


def attention_backward(proj3, dattn3, dproj3, sinks, after, seq, name):
    n_b, n_l, _ = proj3.shape
    qkv_width = D_MODEL + 2 * KV_WIDTH

    def body(sink_ref, q_ref, k_ref, v_ref, do_ref, _, __, dqkv_ref, dsink_ref, dk_ref, dv_ref):
        dk_ref[...] = jnp.zeros_like(dk_ref)
        dv_ref[...] = jnp.zeros_like(dv_ref)
        sub = lax.broadcasted_iota(jnp.int32, (SUBLANES, LANES), 0)
        lane = lax.broadcasted_iota(jnp.int32, (SUBLANES, LANES), 1)

        def queries(q_rows, nq, key_rows, masks, dsink):
            for kh in range(N_KV_HEADS):
                ks = _key_tiles(k_ref, key_rows, kh)
                vs = _key_tiles(v_ref, key_rows, kh)
                qs = _stacked(q_ref, q_rows, kh)
                dos = _stacked(do_ref, q_rows, kh)
                probs, inv, e_sink = _softmax_parts(qs, ks, masks, _sink_column(sink_ref, kh, nq))
                probs = [p * inv for p in probs]
                dps = [jnp.concatenate([_dot_nt(dos, v_lo), _dot_nt(dos, v_hi)], axis=0) for v_lo, v_hi in vs]
                delta = _row_sums([p * dp for p, dp in zip(probs, dps)])
                d_sink = -(e_sink * inv) * delta
                for quarter, g in enumerate(STACK_HEADS):
                    d_here = jnp.sum(d_sink[quarter * nq:(quarter + 1) * nq], axis=0, keepdims=True)
                    dsink = dsink + jnp.where(jnp.logical_and(sub == 0, lane == kh * Q_PER_KV + g), d_here, 0.0)
                dq = None
                tile = slice((kh // 2) * LANES, (kh // 2 + 1) * LANES)
                for r, p, dp, (k_lo, k_hi) in zip(key_rows, probs, dps, ks):
                    ds = (p * (dp - delta)).astype(BF16)
                    p16 = p.astype(BF16)
                    dq_x = _dot(ds[:2 * nq], k_lo) + _dot(ds[2 * nq:], k_hi)
                    dq = dq_x if dq is None else dq + dq_x
                    d_k = _to_kv_lanes(_dot_tn(ds[:2 * nq], qs), _dot_tn(ds[2 * nq:], qs), kh) * ATTN_SCALE
                    d_v = _to_kv_lanes(_dot_tn(p16[:2 * nq], dos), _dot_tn(p16[2 * nq:], dos), kh)
                    n_keys = r.size
                    dk_ref[r, tile] += d_k[:n_keys]
                    dv_ref[r, tile] += d_v[:n_keys]
                dq = (dq * ATTN_SCALE).astype(BF16)
                col = kh * 2 * LANES
                dqkv_ref[q_rows, col:col + LANES] = dq[:nq]
                dqkv_ref[q_rows, col + LANES:col + 2 * LANES] = dq[nq:]
            return dsink

        dsink_ref[...] = _attention_schedule(seq, queries, jnp.zeros((SUBLANES, LANES), F32))
        dqkv_ref[:, D_MODEL:D_MODEL + KV_WIDTH] = dk_ref[...].astype(BF16)
        dqkv_ref[:, D_MODEL + KV_WIDTH:] = dv_ref[...].astype(BF16)

    return _pcall(
        body, name=name, grid=(n_b,),
        in_specs=[pl.BlockSpec(memory_space=pltpu.SMEM),
                  pl.BlockSpec((None, n_l, D_MODEL), lambda b: (b, 0, 0)),
                  pl.BlockSpec((None, n_l, KV_WIDTH), lambda b: (b, 0, D_MODEL // KV_WIDTH)),
                  pl.BlockSpec((None, n_l, KV_WIDTH), lambda b: (b, 0, D_MODEL // KV_WIDTH + 1)),
                  pl.BlockSpec((None, n_l, D_MODEL), lambda b: (b, 0, 0)),
                  ANY_SPEC, ANY_SPEC],
        out_specs=[pl.BlockSpec((None, n_l, qkv_width), lambda b: (b, 0, 0)),
                   pl.BlockSpec((None, SUBLANES, LANES), lambda b: (b, 0, 0))],
        out_shape=[jax.ShapeDtypeStruct(dproj3.shape, BF16), jax.ShapeDtypeStruct((n_b, SUBLANES, LANES), F32)],
        scratch_shapes=[pltpu.VMEM((n_l, KV_WIDTH), F32), pltpu.VMEM((n_l, KV_WIDTH), F32)],
        input_output_aliases={5: 0},
        compiler_params=_cp(("arbitrary",), VMEM_BIG),
    )(sinks, proj3, proj3, proj3, dattn3, dproj3, after)


TAB_ROWS = 8
SCAN_UNROLL = 4


def _cmul(ar, ai, br, bi):
    return ar * br - ai * bi, ar * bi + ai * br


def _discretise(ar, ai, ls):
    step = jnp.exp(ls)
    mag = jnp.exp(ar * step)
    ang = ai * step
    cos, sin = jnp.cos(ang), jnp.sin(ang)
    lr, li = mag * cos, mag * sin
    den = ar * ar + ai * ai
    nr, ni = lr - 1.0, li
    cr = (nr * ar + ni * ai) / den
    ci = (ni * ar - nr * ai) / den
    return step, mag, lr, li, den, nr, ni, cr, ci


def _scan_tables(lr, li, reverse):
    n = lr.shape[-1]
    pw = [(lr, li)]
    for _ in range(SUBLANES - 1):
        pw.append(_cmul(pw[-1][0], pw[-1][1], lr, li))
    row = lax.broadcasted_iota(jnp.int32, (SUBLANES, n), 0)
    out = []
    for d in (1, 2, 4):
        ok = (row + d <= SUBLANES - 1) if reverse else (row >= d)
        out += [jnp.where(ok, pw[d - 1][0], 0.0), jnp.where(ok, pw[d - 1][1], 0.0)]
    cr = jnp.zeros((SUBLANES, n), F32)
    ci = jnp.zeros((SUBLANES, n), F32)
    for r in range(SUBLANES):
        e = (SUBLANES - r) if reverse else (r + 1)
        cr = jnp.where(row == r, pw[e - 1][0], cr)
        ci = jnp.where(row == r, pw[e - 1][1], ci)
    return out + [cr, ci]


def ssm_prepare(ar, ai, ls, br_t, bi_t, name):
    def body(ar_ref, ai_ref, ls_ref, br_ref, bi_ref, bbr_ref, bbi_ref, tf_ref, tr_ref):
        _, _, lr, li, _, _, _, cr, ci = _discretise(ar_ref[...], ai_ref[...], ls_ref[...])
        br, bi = br_ref[...], bi_ref[...]
        bbr_ref[...] = cr * br - ci * bi
        bbi_ref[...] = cr * bi + ci * br
        for k, t in enumerate(_scan_tables(lr, li, False)):
            tf_ref[k] = t
        for k, t in enumerate(_scan_tables(lr, -li, True)):
            tr_ref[k] = t

    return _pcall(
        body, name=name,
        out_shape=[jax.ShapeDtypeStruct((SSM_GROUP, N_STATES), F32), jax.ShapeDtypeStruct((SSM_GROUP, N_STATES), F32),
                   jax.ShapeDtypeStruct((TAB_ROWS, SUBLANES, N_STATES), F32),
                   jax.ShapeDtypeStruct((TAB_ROWS, SUBLANES, N_STATES), F32)],
    )(ar, ai, ls, br_t, bi_t)


def ssm_param_backward(ar, ai, ls, br_t, bi_t, dlr_p, dli_p, dbbr, dbbi, group_sum, name):
    def body(ar_ref, ai_ref, ls_ref, br_ref, bi_ref, dlr_ref, dli_ref, dbbr_ref, dbbi_ref, gs_ref,
             dar_ref, dai_ref, dls_ref, dbr_ref, dbi_ref):
        ar, ai = ar_ref[...], ai_ref[...]
        step, mag, lr, li, den, nr, ni, cr, ci = _discretise(ar, ai, ls_ref[...])
        br, bi, dbbr_v, dbbi_v = br_ref[...], bi_ref[...], dbbr_ref[...], dbbi_ref[...]
        dbr_ref[...] = cr * dbbr_v + ci * dbbi_v
        dbi_ref[...] = cr * dbbi_v - ci * dbbr_v
        dcr = jnp.sum(dbbr_v * br + dbbi_v * bi, axis=0, keepdims=True)
        dci = jnp.sum(dbbi_v * br - dbbr_v * bi, axis=0, keepdims=True)
        dnr = (dcr * ar - dci * ai) / den
        dni = (dcr * ai + dci * ar) / den
        dden = -(cr * dcr + ci * dci) / den
        dar = (dcr * nr + dci * ni) / den + dden * 2.0 * ar
        dai = (dcr * ni - dci * nr) / den + dden * 2.0 * ai
        dlr = jnp.sum(dlr_ref[...], axis=0, keepdims=True) + dnr
        dli = jnp.sum(dli_ref[...], axis=0, keepdims=True) + dni
        dmag = (dlr * lr + dli * li) / mag
        dang = dli * lr - dlr * li
        dar_ref[...] = dar + dmag * mag * step
        dai_ref[...] = dai + dang * step
        dstep = dmag * mag * ar + dang * ai
        dls_ref[...] = jnp.dot(dstep * step, gs_ref[...], preferred_element_type=F32, precision=lax.Precision.HIGHEST)

    vec = jax.ShapeDtypeStruct((1, N_STATES), F32)
    mat = jax.ShapeDtypeStruct((SSM_GROUP, N_STATES), F32)
    return _pcall(body, name=name, out_shape=[vec, vec, jax.ShapeDtypeStruct((1, LANES), F32), mat, mat])(
        ar, ai, ls, br_t, bi_t, dlr_p, dli_p, dbbr, dbbi, group_sum)


def _scan_rows(a, b, tabs, carry, reverse):
    for k, d in enumerate((1, 2, 4)):
        shift = SUBLANES - d if reverse else d
        sr, si = pltpu.roll(a, shift, axis=0), pltpu.roll(b, shift, axis=0)
        pr, pi = _cmul(tabs[2 * k], tabs[2 * k + 1], sr, si)
        a, b = a + pr, b + pi
    pr, pi = _cmul(tabs[6], tabs[7], carry[0], carry[1])
    return a + pr, b + pi


def _time_groups(seq, reverse):
    meta = [seq + SUBLANES * g for g in range(N_META // SUBLANES)]
    return meta[::-1] if reverse else meta


def ssm_forward_scan(proj3, b_comb, tabf, c_comb, dvec, seq, name):
    n_b, n_l, _ = proj3.shape
    u_blk = (D_MODEL + 2 * KV_WIDTH) // LANES

    def body(u_ref, b_ref, tab_ref, c_ref, d_ref, x_ref, y_ref, bu, xs):
        j = pl.program_id(1)
        u = u_ref[...]
        bu[...] = _dot(u, b_ref[...])
        tabs = [tab_ref[k] for k in range(TAB_ROWS)]

        def group(r0, carry):
            rows = pl.ds(r0, SUBLANES)
            a, b = _scan_rows(bu[rows, :SCAN_COLS], bu[rows, SCAN_COLS:], tabs, carry, False)
            xs[rows, :SCAN_COLS] = a
            xs[rows, SCAN_COLS:] = b
            return (jnp.broadcast_to(a[SUBLANES - 1:, :], a.shape), jnp.broadcast_to(b[SUBLANES - 1:, :], b.shape))

        zero = jnp.zeros((SUBLANES, SCAN_COLS), F32)
        carry = (zero, zero)
        for r0 in _time_groups(seq, False):
            carry = group(r0, carry)
        span = SCAN_UNROLL * SUBLANES

        def groups(t, c):
            for k in range(SCAN_UNROLL):
                c = group(pl.multiple_of(t * span, span) + k * SUBLANES, c)
            return c

        lax.fori_loop(0, seq // span, groups, carry)
        x16 = xs[...].astype(BF16)
        x_ref[...] = x16
        contrib = _dot(x16, c_ref[...])

        @pl.when(j % 2 == 0)
        def _():
            y_ref[...] = contrib + d_ref[...] * u.astype(F32)

        @pl.when(j % 2 == 1)
        def _():
            y_ref[...] += contrib

    return _pcall(
        body, name=name, grid=(n_b, N_SCAN_BLK),
        in_specs=[pl.BlockSpec((None, n_l, LANES), lambda b, j: (b, 0, u_blk + j // 2)),
                  pl.BlockSpec((None, LANES, 2 * SCAN_COLS), lambda b, j: (j, 0, 0)),
                  pl.BlockSpec((TAB_ROWS, SUBLANES, SCAN_COLS), lambda b, j: (0, 0, j)),
                  pl.BlockSpec((None, 2 * SCAN_COLS, LANES), lambda b, j: (j, 0, 0)),
                  pl.BlockSpec((1, LANES), lambda b, j: (0, j // 2))],
        out_specs=[pl.BlockSpec((None, n_l, 2 * SCAN_COLS), lambda b, j: (b, 0, j)),
                   pl.BlockSpec((None, n_l, LANES), lambda b, j: (b, 0, j // 2))],
        out_shape=[jax.ShapeDtypeStruct((n_b, n_l, 2 * N_STATES), BF16),
                   jax.ShapeDtypeStruct((n_b, n_l, SSM_WIDTH), F32)],
        scratch_shapes=[pltpu.VMEM((n_l, 2 * SCAN_COLS), F32)] * 2,
        compiler_params=_cp(("arbitrary", "arbitrary"), VMEM_BIG),
    )(proj3, b_comb, tabf, c_comb, dvec)


def ssm_backward_scan(dyraw3, xs3, dproj3, c_comb_t, tabr, b_comb_t, dvec, seq, name):
    n_b, n_l, _ = xs3.shape
    u_blk = (D_MODEL + 2 * KV_WIDTH) // LANES

    def body(dy_ref, x_ref, _, c_ref, tab_ref, b_ref, d_ref, du_ref, g_ref, dlr_ref, dli_ref, dx, gs, xs, du_acc):
        j = pl.program_id(1)
        dy = dy_ref[...]
        dx[...] = _dot(dy, c_ref[...])
        xs[...] = x_ref[...].astype(F32)
        tabs = [tab_ref[k] for k in range(TAB_ROWS)]
        last_row = lax.broadcasted_iota(jnp.int32, (SUBLANES, SCAN_COLS), 0) == SUBLANES - 1

        def group(r0, state):
            cr, ci, acc_r, acc_i = state
            rows = pl.ds(r0, SUBLANES)
            a, b = _scan_rows(dx[rows, :SCAN_COLS], dx[rows, SCAN_COLS:], tabs, (cr, ci), True)
            gs[rows, :SCAN_COLS] = a
            gs[rows, SCAN_COLS:] = b
            na = jnp.where(last_row, cr, pltpu.roll(a, SUBLANES - 1, axis=0))
            nb = jnp.where(last_row, ci, pltpu.roll(b, SUBLANES - 1, axis=0))
            xa, xb = xs[rows, :SCAN_COLS], xs[rows, SCAN_COLS:]
            return (jnp.broadcast_to(a[:1, :], a.shape), jnp.broadcast_to(b[:1, :], b.shape),
                    acc_r + na * xa + nb * xb, acc_i + nb * xa - na * xb)

        zero = jnp.zeros((SUBLANES, SCAN_COLS), F32)
        span = SCAN_UNROLL * SUBLANES
        n_spans = seq // span

        def groups(t, s):
            for k in reversed(range(SCAN_UNROLL)):
                s = group(pl.multiple_of((n_spans - 1 - t) * span, span) + k * SUBLANES, s)
            return s

        state = lax.fori_loop(0, n_spans, groups, (zero, zero, zero, zero))
        for r0 in _time_groups(seq, True):
            state = group(r0, state)
        dlr_ref[...] = state[2]
        dli_ref[...] = state[3]
        g16 = gs[...].astype(BF16)
        g_ref[...] = g16
        contrib = _dot(g16, b_ref[...])

        @pl.when(j % 2 == 0)
        def _():
            du_acc[...] = contrib + d_ref[...] * dy.astype(F32)

        @pl.when(j % 2 == 1)
        def _():
            du_ref[...] = (du_acc[...] + contrib).astype(BF16)

    state_blk = pl.BlockSpec((None, n_l, 2 * SCAN_COLS), lambda b, j: (b, 0, j))
    dl_blk = pl.BlockSpec((None, SUBLANES, SCAN_COLS), lambda b, j: (b, 0, j))
    return _pcall(
        body, name=name, grid=(n_b, N_SCAN_BLK),
        in_specs=[pl.BlockSpec((None, n_l, LANES), lambda b, j: (b, 0, j // 2)), state_blk,
                  pl.BlockSpec(memory_space=pl.ANY),
                  pl.BlockSpec((None, LANES, 2 * SCAN_COLS), lambda b, j: (j, 0, 0)),
                  pl.BlockSpec((TAB_ROWS, SUBLANES, SCAN_COLS), lambda b, j: (0, 0, j)),
                  pl.BlockSpec((None, 2 * SCAN_COLS, LANES), lambda b, j: (j, 0, 0)),
                  pl.BlockSpec((1, LANES), lambda b, j: (0, j // 2))],
        out_specs=[pl.BlockSpec((None, n_l, LANES), lambda b, j: (b, 0, u_blk + j // 2)), state_blk, dl_blk, dl_blk],
        out_shape=[jax.ShapeDtypeStruct(dproj3.shape, BF16), jax.ShapeDtypeStruct((n_b, n_l, 2 * N_STATES), BF16),
                   jax.ShapeDtypeStruct((n_b, SUBLANES, N_STATES), F32), jax.ShapeDtypeStruct((n_b, SUBLANES, N_STATES), F32)],
        scratch_shapes=[pltpu.VMEM((n_l, 2 * SCAN_COLS), F32)] * 3 + [pltpu.VMEM((n_l, LANES), F32)],
        input_output_aliases={2: 0},
        compiler_params=_cp(("arbitrary", "arbitrary"), VMEM_BIG),
    )(dyraw3, xs3, dproj3, c_comb_t, tabr, b_comb_t, dvec)


def ssm_param_grads(proj, gs, xs, dyraw, tm, name):
    t_rows = proj.shape[0]
    ni = t_rows // tm
    u_blk = (D_MODEL + 2 * KV_WIDTH) // LANES
    width = 2 * SCAN_COLS

    def body(u_ref, g_ref, x_ref, dy_ref, db_ref, dc_ref, dd_ref):
        cb, i = pl.program_id(0), pl.program_id(1)
        u, dy = u_ref[...], dy_ref[...]
        _accumulate(db_ref, _dot_tn(u, g_ref[...]), i == 0)
        _accumulate(dc_ref, _dot_tn(x_ref[...], dy), i == 0)

        @pl.when(cb % 2 == 0)
        def _():
            _accumulate(dd_ref, jnp.sum(dy.astype(F32) * u.astype(F32), axis=0, keepdims=True), i == 0)

    return _pcall(
        body, name=name, grid=(N_SCAN_BLK, ni),
        in_specs=[pl.BlockSpec((tm, LANES), lambda cb, i: (i, u_blk + cb // 2)),
                  pl.BlockSpec((tm, width), lambda cb, i: (i, cb)),
                  pl.BlockSpec((tm, width), lambda cb, i: (i, cb)),
                  pl.BlockSpec((tm, LANES), lambda cb, i: (i, cb // 2))],
        out_specs=[pl.BlockSpec((None, LANES, width), lambda cb, i: (cb, 0, 0)),
                   pl.BlockSpec((None, width, LANES), lambda cb, i: (cb, 0, 0)),
                   pl.BlockSpec((1, LANES), lambda cb, i: (0, cb // 2))],
        out_shape=[jax.ShapeDtypeStruct((N_SCAN_BLK, LANES, width), F32),
                   jax.ShapeDtypeStruct((N_SCAN_BLK, width, LANES), F32), jax.ShapeDtypeStruct((1, SSM_WIDTH), F32)],
        compiler_params=_cp(("arbitrary", "arbitrary"), VMEM_BIG),
    )(proj, gs, xs, dyraw)


def sum_leading(x, name):
    def body(x_ref, o_ref):
        acc = x_ref[0]
        for k in range(1, x.shape[0]):
            acc = acc + x_ref[k]
        o_ref[...] = acc

    return _pcall(body, name=name, out_shape=jax.ShapeDtypeStruct(x.shape[1:], x.dtype))(x)


WEIGHTS = ['meta_tokens', 'ffn1_norm', 'ffn1_w1', 'ffn1_w3', 'ffn1_w2', 'mix_norm', 'w_in', 'attn_sinks', 'ssm_a_re',
           'ssm_a_im', 'ssm_log_step', 'ssm_b_re', 'ssm_b_im', 'ssm_c_re', 'ssm_c_im', 'ssm_d', 'ssm_glu_a', 'ssm_glu_b',
           'w_out', 'ffn2_norm', 'ffn2_w1', 'ffn2_w3', 'ffn2_w2', 'final_norm']
SHARDED = ['ffn1_w1', 'ffn1_w3', 'ffn1_w2', 'ffn2_w1', 'ffn2_w3', 'ffn2_w2', 'w_in', 'ssm_glu_a', 'ssm_glu_b', 'w_out']
REPLICATED = ['ffn1_norm', 'mix_norm', 'ffn2_norm', 'final_norm', 'attn_sinks', 'ssm_a_re', 'ssm_a_im', 'ssm_log_step',
              'ssm_b_re', 'ssm_b_im', 'ssm_c_re', 'ssm_c_im', 'ssm_d']
PACK_COLS = 1024


def _pack(arrays):
    parts = []
    for a in arrays:
        flat = a.reshape(-1)
        chunk = SUBLANES * PACK_COLS
        padded = -(-flat.shape[0] // chunk) * chunk
        parts.append(jnp.pad(flat, (0, padded - flat.shape[0])).reshape(-1, PACK_COLS))
    return jnp.concatenate(parts, axis=0)


def _unpack(packed, shapes):
    out, row = [], 0
    for shape in shapes:
        size = 1
        for s in shape:
            size *= s
        chunk = SUBLANES * PACK_COLS
        rows = -(-size // chunk) * SUBLANES
        out.append(packed[row:row + rows].reshape(-1)[:size].reshape(shape))
        row += rows
    return out


def kernel(x, meta_tokens, ffn1_norm, ffn1_w1, ffn1_w3, ffn1_w2, mix_norm, w_in, attn_sinks, ssm_a_re, ssm_a_im, ssm_log_step, ssm_b_re, ssm_b_im, ssm_c_re, ssm_c_im, ssm_d, ssm_glu_a, ssm_glu_b, w_out, ffn2_norm, ffn2_w1, ffn2_w3, ffn2_w2, final_norm, loss_target, m_meta_tokens, m_ffn1_norm, m_ffn1_w1, m_ffn1_w3, m_ffn1_w2, m_mix_norm, m_w_in, m_attn_sinks, m_ssm_a_re, m_ssm_a_im, m_ssm_log_step, m_ssm_b_re, m_ssm_b_im, m_ssm_c_re, m_ssm_c_im, m_ssm_d, m_ssm_glu_a, m_ssm_glu_b, m_w_out, m_ffn2_norm, m_ffn2_w1, m_ffn2_w3, m_ffn2_w2, m_final_norm, v_meta_tokens, v_ffn1_norm, v_ffn1_w1, v_ffn1_w3, v_ffn1_w2, v_mix_norm, v_w_in, v_attn_sinks, v_ssm_a_re, v_ssm_a_im, v_ssm_log_step, v_ssm_b_re, v_ssm_b_im, v_ssm_c_re, v_ssm_c_im, v_ssm_d, v_ssm_glu_a, v_ssm_glu_b, v_w_out, v_ffn2_norm, v_ffn2_w1, v_ffn2_w3, v_ffn2_w2, v_final_norm):
    given = dict(locals())
    w = {n: given[n] for n in WEIGHTS}
    m = {n: given["m_" + n] for n in WEIGHTS}
    v = {n: given["v_" + n] for n in WEIGHTS}

    n_b, seq, _ = x.shape
    n_l = seq + N_META
    t_rows = n_b * n_l
    tm = _row_tile(n_l, 688)
    px, py, pc = _my_place()
    me = 4 * px + 2 * py + pc

    glu = jnp.stack([ssm_glu_a[0], ssm_glu_b[0]]).astype(BF16)
    ffn_names = ['ffn1_w1', 'ffn1_w3', 'ffn1_w2', 'ffn2_w1', 'ffn2_w3', 'ffn2_w2']

    def hidden_on_rows(n, t):
        return t[0] if n.endswith('w2') else t[0].T

    def hidden_on_rows_back(n, t):
        return t[None] if n.endswith('w2') else t.T[None]

    me_idx = jnp.reshape(me, (1,)).astype(jnp.int32)
    first_names, later_names = ffn_names[:3], ffn_names[3:]
    *first, metag = all_gather_list(
        [hidden_on_rows(n, w[n]).astype(BF16) for n in first_names] + [meta_tokens], meta_tokens, "ag_first")
    win_send, win_recv, win_shard, win_land, win_token = exchange_start(
        [w_in[0].astype(BF16)], first[0], True, "ag_w_in_start")
    later_shards = [hidden_on_rows(n, w[n]).astype(BF16) for n in later_names] + [glu, w_out[0].astype(BF16)]
    ag_send, ag_recv, later_shards, later_lands, ag_token = exchange_start(later_shards, win_token, True, "ag_later_start")
    full = {n: g.reshape(D_FF, D_MODEL) for n, g in zip(first_names, first)}
    meta_full = metag.transpose(1, 0, 2).reshape(N_META, D_MODEL)

    final_g = final_norm.reshape(1, D_MODEL)

    ar = ssm_a_re.reshape(1, N_STATES)
    ai = ssm_a_im.reshape(1, N_STATES)
    ls = jnp.repeat(ssm_log_step.reshape(SSM_GROUPS), SSM_STATE).reshape(1, N_STATES)
    br_t = ssm_b_re[0].transpose(2, 0, 1).reshape(SSM_GROUP, N_STATES)
    bi_t = ssm_b_im[0].transpose(2, 0, 1).reshape(SSM_GROUP, N_STATES)
    bbr, bbi, tabf, tabr = ssm_prepare(ar, ai, ls, br_t, bi_t, "ssm_prepare")
    bbr_g = bbr.reshape(SSM_GROUP, SSM_GROUPS, SSM_STATE).transpose(1, 0, 2)
    bbi_g = bbi.reshape(SSM_GROUP, SSM_GROUPS, SSM_STATE).transpose(1, 0, 2)
    groups_per_blk = SCAN_COLS // SSM_STATE
    half = ((jnp.arange(N_SCAN_BLK) % 2)[:, None] == jnp.arange(2)[None, :]).astype(F32)
    eye = jnp.eye(groups_per_blk, dtype=F32)

    def scan_blocks(re_g, im_g):
        def one(t):
            t = t.reshape(N_SCAN_BLK, groups_per_blk, SSM_GROUP, SSM_STATE)
            t = t[:, :, :, None, :] * eye[None, :, None, :, None]
            t = t.reshape(N_SCAN_BLK, LANES // 2, SCAN_COLS)
            return (t[:, None] * half[:, :, None, None]).reshape(N_SCAN_BLK, LANES, SCAN_COLS)
        return jnp.concatenate([one(re_g), one(im_g)], axis=-1).astype(BF16)

    b_comb = scan_blocks(bbr_g, bbi_g)
    c_comb_t = scan_blocks(ssm_c_re[0], -ssm_c_im[0])
    b_comb_t, c_comb = b_comb.transpose(0, 2, 1), c_comb_t.transpose(0, 2, 1)

    ffn1_w = (full['ffn1_w1'], full['ffn1_w3'], full['ffn1_w2'])
    h1, hn1, a1, b1, h0 = ffn_forward(x, ffn1_norm, *ffn1_w, ag_token, tm, "ffn1_fwd", meta=meta_full)
    win_shard, (wing,) = exchange_wait(win_send, win_recv, win_shard, win_land, h1, True, "ag_w_in_wait")
    wing = lax.dynamic_update_slice_in_dim(wing, win_shard[0][None], me, axis=0)
    hnm, proj = mix_forward(h1, mix_norm, wing, tm, "mix_fwd")
    proj3 = proj.reshape(n_b, n_l, IN_WIDTH)
    attn3 = attention_forward(proj3, attn_sinks, seq, "attn_fwd")
    attn = attn3.reshape(t_rows, D_MODEL)
    xs3, yraw3 = ssm_forward_scan(proj3, b_comb, tabf, c_comb, ssm_d, seq, "ssm_fwd")
    yraw = yraw3.reshape(t_rows, SSM_WIDTH)
    later_shards, later = exchange_wait(ag_send, ag_recv, later_shards, later_lands, yraw3, True, "ag_later_wait")
    later = [lax.dynamic_update_slice_in_dim(z, s[None], me, axis=0) for z, s in zip(later, later_shards)]
    for n, g in zip(later_names, later):
        full[n] = g.reshape(D_FF, D_MODEL)
    ffn2_w = (full['ffn2_w1'], full['ffn2_w3'], full['ffn2_w2'])
    glug, wog = later[len(later_names):]
    glu_a = glug[:, 0].transpose(1, 0, 2).reshape(SSM_WIDTH, D_MODEL)
    glu_b = glug[:, 1].transpose(1, 0, 2).reshape(SSM_WIDTH, D_MODEL)
    w_out_full = wog.reshape(D_MODEL, D_MODEL)
    h2 = merge_forward(h1, yraw, attn, proj, glu_a, glu_b, w_out_full, tm, "merge_fwd")
    h3, hn2, a2, b2 = ffn_forward(h2, ffn2_norm, *ffn2_w, ag_token, tm, "ffn2_fwd")
    dh3, loss_part, g_final = final_loss_backward(h3, loss_target, final_g, seq, tm, "loss_bwd")
    loss = lax.psum(loss_part[0, 0], ("x", "y", "c"))

    def blocked_ffn(d_w1t, d_w3t, d_w2):
        return tuple(t.reshape(N_DEV, FF_BLK, D_MODEL) for t in (d_w1t, d_w3t, d_w2))

    def blocked_cols(full_grad):
        r = full_grad.shape[0]
        return full_grad.reshape(r, N_DEV, full_grad.shape[1] // N_DEV).transpose(1, 0, 2).astype(BF16)

    early = {}

    def start_reduce(names, tag):
        srcs = [dw[n] for n in names]
        send, recv, srcs, lands, token = exchange_start(srcs, srcs[0], False, "rs_" + tag + "_start")
        early[tag] = (names, send, recv, srcs, lands)
        return token

    dw = {}
    da2, db2, dh3_half = ffn_backward_hidden(dh3, a2, b2, ffn2_w[2], g_final, tm, "ffn2_bwd_hid")
    dw['ffn2_w1'], dw['ffn2_w3'], dw['ffn2_w2'] = blocked_ffn(
        *ffn_backward_weights(hn2, dh3_half, a2, b2, da2, db2, n_l, FF_BWD_COLS, "ffn2_bwd_w"))
    token = start_reduce(later_names, "ffn2")
    dh2, g_ffn2_norm = ffn_backward_input(dh3, h2, ffn2_norm, da2, db2, ffn2_w[0], ffn2_w[1], token, tm, "ffn2_bwd_in")
    dattn, dyraw, dproj, *for_weights = merge_backward(dh2, yraw, attn, proj, glu_a, glu_b, w_out_full, token, tm,
                                                       "merge_bwd")
    d_wo, d_ga, d_gb = merge_backward_weights(*for_weights, tm, "merge_bwd_w")
    dw['ssm_glu_a'] = blocked_cols(d_ga)
    dw['ssm_glu_b'] = blocked_cols(d_gb)
    dw['w_out'] = d_wo.reshape(N_DEV, D_MODEL // N_DEV, D_MODEL).astype(BF16)
    token = start_reduce(['ssm_glu_a', 'ssm_glu_b', 'w_out'], "mix")
    dproj3 = dproj.reshape(n_b, n_l, IN_WIDTH)
    dproj3, dsink_p = attention_backward(proj3, dattn.reshape(n_b, n_l, D_MODEL), dproj3, attn_sinks, token, seq,
                                         "attn_bwd")
    dproj3, gs3, dlr_p, dli_p = ssm_backward_scan(
        dyraw.reshape(n_b, n_l, SSM_WIDTH), xs3, dproj3, c_comb_t, tabr, b_comb_t, ssm_d, seq, "ssm_bwd")
    dproj = dproj3.reshape(t_rows, IN_WIDTH)
    d_bd, d_cd, g_d = ssm_param_grads(proj, gs3.reshape(t_rows, 2 * N_STATES), xs3.reshape(t_rows, 2 * N_STATES),
                                      dyraw, n_l, "ssm_bwd_w")
    w_in_full = wing.transpose(1, 0, 2).reshape(D_MODEL, IN_WIDTH)
    dh1, g_mix_norm = mix_backward_act(dh2, h1, mix_norm, dproj, w_in_full, tm, "mix_bwd_act")
    dw['w_in'] = mix_backward_weights(hnm, dproj, n_l, "mix_bwd_w")
    token = start_reduce(['w_in'], "w_in")

    def group_blocks(part, channels_first):
        if channels_first:
            t = jnp.sum(part.reshape(N_SCAN_BLK, 2, LANES // 2, SCAN_COLS) * half[:, :, None, None], axis=1)
            t = t.reshape(N_SCAN_BLK, groups_per_blk, SSM_GROUP, groups_per_blk, SSM_STATE)
            t = jnp.sum(t * eye[None, :, None, :, None], axis=3)
            return t.reshape(SSM_GROUPS, SSM_GROUP, SSM_STATE)
        t = jnp.sum(part.reshape(N_SCAN_BLK, SCAN_COLS, 2, LANES // 2) * half[:, None, :, None], axis=2)
        t = t.reshape(N_SCAN_BLK, groups_per_blk, SSM_STATE, groups_per_blk, SSM_GROUP)
        t = jnp.sum(t * eye[None, :, None, :, None], axis=3)
        return t.reshape(SSM_GROUPS, SSM_STATE, SSM_GROUP).transpose(0, 2, 1)

    dbbr = group_blocks(d_bd[:, :, :SCAN_COLS], True).transpose(1, 0, 2).reshape(SSM_GROUP, N_STATES)
    dbbi = group_blocks(d_bd[:, :, SCAN_COLS:], True).transpose(1, 0, 2).reshape(SSM_GROUP, N_STATES)
    g_c_re = group_blocks(d_cd[:, :SCAN_COLS, :], False)[None]
    g_c_im = -group_blocks(d_cd[:, SCAN_COLS:, :], False)[None]
    group_sum = (jnp.arange(N_STATES)[:, None] // SSM_STATE == jnp.arange(LANES)[None, :]).astype(F32)
    g_ar, g_ai, g_ls, g_br, g_bi = ssm_param_backward(
        ar, ai, ls, br_t, bi_t, dlr_p.reshape(n_b * SUBLANES, N_STATES), dli_p.reshape(n_b * SUBLANES, N_STATES),
        dbbr, dbbi, group_sum, "ssm_bwd_params")
    g_sinks = sum_leading(dsink_p, "sink_sum")[0:1, :N_KV_HEADS * Q_PER_KV]

    small = {
        'mix_norm': g_mix_norm, 'ffn2_norm': g_ffn2_norm, 'final_norm': g_final.reshape(D_MODEL),
        'attn_sinks': g_sinks, 'ssm_a_re': g_ar.reshape(1, SSM_GROUPS, SSM_STATE), 'ssm_a_im': g_ai.reshape(1, SSM_GROUPS, SSM_STATE),
        'ssm_log_step': g_ls[:, :SSM_GROUPS],
        'ssm_b_re': g_br.reshape(SSM_GROUP, SSM_GROUPS, SSM_STATE).transpose(1, 2, 0)[None],
        'ssm_b_im': g_bi.reshape(SSM_GROUP, SSM_GROUPS, SSM_STATE).transpose(1, 2, 0)[None],
        'ssm_c_re': g_c_re, 'ssm_c_im': g_c_im, 'ssm_d': g_d,
    }
    early_small = [n for n in REPLICATED if n in small]
    sg_send, sg_recv, sg_src, sg_land, token = exchange_start(
        [_pack([small[n] for n in early_small])], token, True, "ag_small_start")
    da1, db1, dh1_half = ffn_backward_hidden(dh1, a1, b1, ffn1_w[2], token, tm, "ffn1_bwd_hid")
    dw['ffn1_w1'], dw['ffn1_w3'], dw['ffn1_w2'] = blocked_ffn(
        *ffn_backward_weights(hn1, dh1_half, a1, b1, da1, db1, n_l, FF_BWD_COLS, "ffn1_bwd_w"))
    token = start_reduce(first_names, "ffn1")
    grad_x, meta_rows_grad, g_ffn1_norm = ffn_backward_input(
        dh1, h0, ffn1_norm, da1, db1, ffn1_w[0], ffn1_w[1], token, tm, "ffn1_bwd_in", examples=(n_b, seq))
    g_meta = sum_leading(meta_rows_grad, "meta_sum")

    grads, deltas, new_m, new_v = {}, {}, {}, {}

    def views(n):
        if n in ffn_names:
            return functools.partial(hidden_on_rows, n), functools.partial(hidden_on_rows_back, n)
        return (lambda t: t[0]), (lambda t: t[None])

    def finish_reduce(tag, previous):
        names, send, recv, srcs, lands = early[tag]
        srcs, lands = exchange_wait(send, recv, srcs, lands, previous, False, "rs_" + tag + "_wait")
        for n, g, land in zip(names, srcs, lands):
            two_d, back = views(n)
            out = adamw_exchanged(me_idx, g, land, two_d(w[n]), two_d(m[n]), two_d(v[n]), "adamw_" + n)
            grads[n], deltas[n], new_m[n], new_v[n] = (back(o) for o in out)
            previous = out[1]
        return previous

    previous = g_meta
    for tag in ("ffn2", "mix", "w_in"):
        previous = finish_reduce(tag, previous)

    zeros_meta = jnp.zeros((N_META, D_MODEL), F32)
    (late_parts,) = all_gather_list([_pack([g_ffn1_norm, g_meta])], previous, "ag_small_late")
    sg_src, (early_parts,) = exchange_wait(sg_send, sg_recv, sg_src, sg_land, late_parts, True, "ag_small_wait")
    early_parts = lax.dynamic_update_slice_in_dim(early_parts, sg_src[0][None], me, axis=0)

    def small_update(parts, names, extra, tag):
        pack_of = lambda d: _pack([d[n] for n in names] + extra)
        packed = adamw_small(parts, pack_of(w), pack_of(m), pack_of(v), "adamw_small_" + tag)
        unpacked = [_unpack(p, [w[n].shape for n in names] + [e.shape for e in extra]) for p in packed]
        for k, n in enumerate(names):
            grads[n], deltas[n], new_m[n], new_v[n] = (u[k] for u in unpacked)
        return packed, unpacked

    small_update(early_parts, early_small, [], "early")
    packed_out, unpacked = small_update(late_parts, ['ffn1_norm'], [zeros_meta], "late")
    g_meta_full = unpacked[0][-1]
    grads['meta_tokens'] = lax.dynamic_index_in_dim(
        g_meta_full.reshape(N_META, N_DEV, D_MODEL // N_DEV), me, axis=1, keepdims=False)
    deltas['meta_tokens'], new_m['meta_tokens'], new_v['meta_tokens'] = adamw_plain(
        grads['meta_tokens'], w['meta_tokens'], m['meta_tokens'], v['meta_tokens'], "adamw_meta")

    finish_reduce("ffn1", packed_out[0])

    return (loss, grad_x, *[grads[n] for n in WEIGHTS], *[deltas[n] for n in WEIGHTS],
            *[new_m[n] for n in WEIGHTS], *[new_v[n] for n in WEIGHTS])
```

```python
import functools

import jax
import jax.numpy as jnp
from jax import lax
from jax.experimental import pallas as pl
from jax.experimental.pallas import tpu as pltpu

F32 = jnp.float32
BF16 = jnp.bfloat16
MESH = pl.DeviceIdType.MESH

N_DEV = 8
D_MODEL = 1024
N_META = 16
HEAD_DIM = 64
N_KV_HEADS = 4
Q_PER_KV = 4
BLOCK = 128
KV_WIDTH = N_KV_HEADS * HEAD_DIM
SSM_GROUP = 16
SSM_WIDTH = 512
SSM_GROUPS = 32
SSM_STATE = 64
N_STATES = SSM_GROUPS * SSM_STATE
D_FF = 2816
FF_BLK = D_FF // N_DEV
IN_WIDTH = 4096
IN_BLK = IN_WIDTH // N_DEV
NORM_EPS = 1e-6
NEG_INF = -1e30
SCAN_COLS = 256
N_SCAN_BLK = N_STATES // SCAN_COLS
SUBLANES = 8
LANES = 128
MXU_WIDTH = 256
FF_BWD_COLS = MXU_WIDTH

ADAM_LR = 0.001
ADAM_B1 = 0.9
ADAM_B2 = 0.999
ADAM_EPS = 1e-08
ADAM_WD = 0.01
ADAM_STEP = 10

VMEM_BIG = 56 * 1024 * 1024


def _cp(sem=None, vmem=None):
    kw = {}
    if sem is not None:
        kw["dimension_semantics"] = sem
    if vmem is not None:
        kw["vmem_limit_bytes"] = vmem
    return pltpu.CompilerParams(**kw)


def _pcall(body, **kw):
    return pl.pallas_call(body, **kw)


def _dot(a, b):
    return jnp.dot(a, b, preferred_element_type=F32)


def _dot_nt(a, b):
    return lax.dot_general(a, b, (((1,), (1,)), ((), ())), preferred_element_type=F32)


def _dot_tn(a, b):
    return lax.dot_general(a, b, (((0,), (0,)), ((), ())), preferred_element_type=F32)


def _sigmoid(x):
    return 1.0 / (1.0 + jnp.exp(-x))


def _row_tile(rows, cap):
    best = None
    for t in range(16, min(rows, cap) + 1, 16):
        if rows % t == 0:
            best = t
    assert best is not None, rows
    return best


def _my_place():
    return lax.axis_index("x"), lax.axis_index("y"), lax.axis_index("c")


def all_gather_list(shards, after, name):
    n = len(shards)

    def body(*refs):
        ins, outs = refs[:n], refs[n + 1:2 * n + 1]
        send_sems, recv_sems, local_sems = refs[2 * n + 1:]
        x, y, c = _my_place()
        me, sibling = (x, y, c), (x, y, 1 - c)
        chips = [(1 - x, y), (x, 1 - y), (1 - x, 1 - y)]

        def blk(a, px, py, pc):
            return outs[a].at[4 * px + 2 * py + pc]

        def copy(a, k, block, to, src=None):
            return pltpu.make_async_remote_copy(
                src_ref=blk(a, *block) if src is None else src, dst_ref=blk(a, *block),
                send_sem=send_sems.at[a * 7 + k], recv_sem=recv_sems.at[a * 7 + k],
                device_id=to, device_id_type=MESH)

        mine = [pltpu.make_async_copy(ins[a], blk(a, *me), local_sems.at[a]) for a in range(n)]
        for cp in mine:
            cp.start()
        first = []
        for a in range(n):
            first.append(copy(a, 0, me, sibling, src=ins[a]))
            first += [copy(a, 1 + j, me, (*chip, c), src=ins[a]) for j, chip in enumerate(chips)]
        for cp in first:
            cp.start()
        passed = []
        for j, chip in enumerate(chips):
            for a in range(n):
                copy(a, 1 + j, (*chip, c), me).wait_recv()
                cp = copy(a, 4 + j, (*chip, c), sibling)
                cp.start()
                passed.append(cp)
        for a in range(n):
            copy(a, 0, sibling, me).wait_recv()
            for j, chip in enumerate(chips):
                copy(a, 4 + j, (*chip, 1 - c), me).wait_recv()
        for cp in first + passed:
            cp.wait_send()
        for cp in mine:
            cp.wait()

    any_spec = pl.BlockSpec(memory_space=pl.ANY)
    return _pcall(
        body, name=name,
        out_shape=[jax.ShapeDtypeStruct((N_DEV,) + s.shape, s.dtype) for s in shards],
        in_specs=[any_spec] * (n + 1), out_specs=[any_spec] * n,
        scratch_shapes=[pltpu.SemaphoreType.DMA((7 * n,)), pltpu.SemaphoreType.DMA((7 * n,)),
                        pltpu.SemaphoreType.DMA((n,))],
    )(*shards, after)


HBM_SPEC = pl.BlockSpec(memory_space=pltpu.HBM)
SEM_SPEC = pl.BlockSpec(memory_space=pltpu.SEMAPHORE)
N_PEERS = N_DEV - 1


def _related(k):
    x, y, c = _my_place()
    px = 1 - x if k & 4 else x
    py = 1 - y if k & 2 else y
    pc = 1 - c if k & 1 else c
    return (px, py, pc), 4 * px + 2 * py + pc


def _exchange_copies(srcs, lands, send_sems, recv_sems, gather):
    x, y, c = _my_place()
    me = 4 * x + 2 * y + c
    copies = []
    for a, (src, land) in enumerate(zip(srcs, lands)):
        for k in range(1, N_DEV):
            peer, d = _related(k)
            copies.append(pltpu.make_async_remote_copy(
                src_ref=src if gather else src.at[d], dst_ref=land.at[me] if gather else land.at[k],
                send_sem=send_sems.at[a * N_PEERS + k - 1], recv_sem=recv_sems.at[a * N_PEERS + k - 1],
                device_id=peer, device_id_type=MESH))
    return copies


def exchange_start(srcs, after, gather, name):
    n = len(srcs)
    land_shapes = [((N_DEV,) + s.shape) if gather else s.shape for s in srcs]

    def body(*refs):
        send_sems, recv_sems = refs[2 * n + 1], refs[2 * n + 2]
        for cp in _exchange_copies(refs[:n], refs[n:2 * n], send_sems, recv_sems, gather):
            cp.start()
        token = refs[-1]
        token[...] = jnp.zeros_like(token)

    sems = pltpu.SemaphoreType.DMA((n * N_PEERS,))
    lands = [pltpu.with_memory_space_constraint(lax.empty(shape, s.dtype), pltpu.HBM) for shape, s in zip(land_shapes, srcs)]
    out = _pcall(
        body, name=name,
        out_shape=(sems, sems, *[pltpu.HBM(s.shape, s.dtype) for s in srcs],
                   *[pltpu.HBM(shape, s.dtype) for shape, s in zip(land_shapes, srcs)],
                   jax.ShapeDtypeStruct((SUBLANES, LANES), F32)),
        in_specs=[HBM_SPEC] * (2 * n) + [pl.BlockSpec(memory_space=pl.ANY)],
        out_specs=(SEM_SPEC, SEM_SPEC, *[HBM_SPEC] * (2 * n), pl.BlockSpec(memory_space=pltpu.VMEM)),
        input_output_aliases={i: 2 + i for i in range(2 * n)},
        compiler_params=pltpu.CompilerParams(has_side_effects=pltpu.SideEffectType.DATAFLOW_SIDE_EFFECTING),
    )(*[pltpu.with_memory_space_constraint(s, pltpu.HBM) for s in srcs], *lands, after)
    return out[0], out[1], list(out[2:2 + n]), list(out[2 + n:2 + 2 * n]), out[-1]


def exchange_wait(send_sems, recv_sems, srcs, lands, after, gather, name):
    n = len(srcs)

    def body(*refs):
        for cp in _exchange_copies(refs[:n], refs[n:2 * n], refs[2 * n], refs[2 * n + 1], gather):
            cp.wait_send()
            cp.wait_recv()

    out = _pcall(
        body, name=name,
        out_shape=(*[pltpu.HBM(s.shape, s.dtype) for s in srcs], *[pltpu.HBM(z.shape, z.dtype) for z in lands]),
        in_specs=[HBM_SPEC] * (2 * n) + [SEM_SPEC, SEM_SPEC, pl.BlockSpec(memory_space=pl.ANY)],
        out_specs=tuple([HBM_SPEC] * (2 * n)),
        input_output_aliases={i: i for i in range(2 * n)},
        compiler_params=pltpu.CompilerParams(has_side_effects=pltpu.SideEffectType.DATAFLOW_SIDE_EFFECTING),
    )(*srcs, *lands, send_sems, recv_sems, after)
    return list(out[:n]), list(out[n:])


def adamw_exchanged(me, g, land, w, m, v, name):
    rows, cols = w.shape
    tr = _row_tile(rows, 256)

    def body(me_ref, g_ref, land_ref, w_ref, m_ref, v_ref, go_ref, d_ref, mo_ref, vo_ref):
        grad = g_ref[...].astype(F32)
        for k in range(1, N_DEV):
            grad = grad + land_ref[k].astype(F32)
        delta, m_new, v_new = _adam_math(w_ref[...], grad, m_ref[...], v_ref[...])
        go_ref[...] = grad
        d_ref[...] = delta
        mo_ref[...] = m_new
        vo_ref[...] = v_new

    tile = pl.BlockSpec((tr, cols), lambda r, ix: (r, 0))
    out = jax.ShapeDtypeStruct((rows, cols), F32)
    return _pcall(
        body, name=name, out_shape=[out] * 4,
        grid_spec=pltpu.PrefetchScalarGridSpec(
            num_scalar_prefetch=1, grid=(rows // tr,),
            in_specs=[pl.BlockSpec((None, tr, cols), lambda r, ix: (ix[0], r, 0)),
                      pl.BlockSpec((N_DEV, tr, cols), lambda r, ix: (0, r, 0)), tile, tile, tile],
            out_specs=[tile] * 4),
        compiler_params=_cp(("arbitrary",)),
    )(me, g, land, w, m, v)


def _adam_math(w, g, m, v):
    m = ADAM_B1 * m + (1.0 - ADAM_B1) * g
    v = ADAM_B2 * v + (1.0 - ADAM_B2) * (g * g)
    m_hat = m / (1.0 - ADAM_B1 ** ADAM_STEP)
    v_hat = v / (1.0 - ADAM_B2 ** ADAM_STEP)
    delta = -ADAM_LR * (m_hat / (jnp.sqrt(v_hat) + ADAM_EPS) + ADAM_WD * w)
    return delta, m, v


def adamw_small(parts, w, m, v, name):
    _, rows, cols = parts.shape

    def body(p_ref, w_ref, m_ref, v_ref, go_ref, d_ref, mo_ref, vo_ref):
        grad = p_ref[0]
        for k in range(1, N_DEV):
            grad = grad + p_ref[k]
        delta, m_new, v_new = _adam_math(w_ref[...], grad, m_ref[...], v_ref[...])
        go_ref[...] = grad
        d_ref[...] = delta
        mo_ref[...] = m_new
        vo_ref[...] = v_new

    out = jax.ShapeDtypeStruct((rows, cols), F32)
    return _pcall(body, name=name, out_shape=[out] * 4, compiler_params=_cp(vmem=VMEM_BIG))(parts, w, m, v)


def adamw_plain(g, w, m, v, name):
    def body(g_ref, w_ref, m_ref, v_ref, d_ref, mo_ref, vo_ref):
        delta, m_new, v_new = _adam_math(w_ref[...], g_ref[...], m_ref[...], v_ref[...])
        d_ref[...] = delta
        mo_ref[...] = m_new
        vo_ref[...] = v_new

    out = jax.ShapeDtypeStruct(w.shape, F32)
    return _pcall(body, name=name, out_shape=[out] * 3)(g, w, m, v)


def _rms_fwd(x, g):
    r = lax.rsqrt(jnp.mean(x * x, axis=-1, keepdims=True) + NORM_EPS)
    return x * r * g


def _rms_bwd(x, g, dy):
    r = lax.rsqrt(jnp.mean(x * x, axis=-1, keepdims=True) + NORM_EPS)
    xh = x * r
    t = dy * g
    dx = r * (t - xh * jnp.mean(t * xh, axis=-1, keepdims=True))
    return dx, jnp.sum(dy * xh, axis=0, keepdims=True)


def _accumulate(ref, val, first):
    @pl.when(first)
    def _():
        ref[...] = val

    @pl.when(jnp.logical_not(first))
    def _():
        ref[...] += val


def _col_chunks(width):
    return [(c0, min(MXU_WIDTH, width - c0)) for c0 in range(0, width, MXU_WIDTH)]


ANY_SPEC = pl.BlockSpec(memory_space=pl.ANY)


def ffn_forward(h, norm, w1, w3, w2, after, tm, name, meta=None):
    if meta is None:
        t_rows = h.shape[0]
        h_spec = pl.BlockSpec((tm, D_MODEL), lambda i: (i, 0))
    else:
        tiles = (h.shape[1] + N_META) // tm
        t_rows = h.shape[0] * tiles * tm
        h_spec = pl.BlockSpec((None, tm, D_MODEL), lambda i: (i // tiles, i % tiles, 0))

    def body(h_ref, g_ref, w1_ref, w3_ref, w2_ref, _, *rest):
        if meta is None:
            out_ref, hn_ref, a_ref, b_ref, hid_ref = rest
            h_in = h_ref[...]
        else:
            meta_ref, out_ref, hn_ref, a_ref, b_ref, h0_ref, hid_ref = rest
            h_in = h_ref[...]
            with_meta = jnp.concatenate([h_in[:tm - N_META], meta_ref[...]], axis=0)
            h_in = jnp.where(pl.program_id(0) % tiles == tiles - 1, with_meta, h_in)
            h0_ref[...] = h_in
        hn = _rms_fwd(h_in, g_ref[...]).astype(BF16)
        hn_ref[...] = hn
        for c0, cw in _col_chunks(D_FF):
            a = _dot_nt(hn, w1_ref[c0:c0 + cw, :])
            b = _dot_nt(hn, w3_ref[c0:c0 + cw, :])
            a_ref[:, c0:c0 + cw] = a.astype(BF16)
            b_ref[:, c0:c0 + cw] = b.astype(BF16)
            hid_ref[:, c0:c0 + cw] = (a * _sigmoid(a) * b).astype(BF16)
        out_ref[...] = h_in + 0.5 * _dot(hid_ref[...], w2_ref[...])

    row = pl.BlockSpec((tm, D_MODEL), lambda i: (i, 0))
    hid_blk = pl.BlockSpec((tm, D_FF), lambda i: (i, 0))
    weight = _resident((D_FF, D_MODEL))
    wide = jax.ShapeDtypeStruct((t_rows, D_MODEL), F32)
    extra_in = [] if meta is None else [meta]
    return _pcall(
        body, name=name, grid=(t_rows // tm,),
        in_specs=[h_spec, pl.BlockSpec((1, D_MODEL), lambda i: (0, 0)), weight, weight, weight, ANY_SPEC]
        + [pl.BlockSpec((N_META, D_MODEL), lambda i: (0, 0))] * len(extra_in),
        out_specs=[row, row, hid_blk, hid_blk] + [row] * len(extra_in),
        out_shape=[wide, jax.ShapeDtypeStruct((t_rows, D_MODEL), BF16),
                   jax.ShapeDtypeStruct((t_rows, D_FF), BF16), jax.ShapeDtypeStruct((t_rows, D_FF), BF16)]
        + [wide] * len(extra_in),
        scratch_shapes=[pltpu.VMEM((tm, D_FF), BF16)],
        compiler_params=_cp(("arbitrary",), VMEM_BIG),
    )(h, norm, w1, w3, w2, after, *extra_in)


def _resident(shape):
    return pl.BlockSpec(shape, lambda *_: (0,) * len(shape), pipeline_mode=pl.Buffered(1))


def ffn_backward_hidden(dh, a, b, w2, after, tm, name):
    t_rows = dh.shape[0]

    def body(dh_ref, a_ref, b_ref, w2_ref, _, da_ref, db_ref, dhb_ref):
        dhb = (0.5 * dh_ref[...]).astype(BF16)
        dhb_ref[...] = dhb
        for c0, cw in _col_chunks(D_FF):
            dhid = _dot_nt(dhb, w2_ref[c0:c0 + cw, :])
            av = a_ref[:, c0:c0 + cw].astype(F32)
            bv = b_ref[:, c0:c0 + cw].astype(F32)
            s = _sigmoid(av)
            da_ref[:, c0:c0 + cw] = (dhid * bv * (s * (1.0 + av * (1.0 - s)))).astype(BF16)
            db_ref[:, c0:c0 + cw] = (dhid * (av * s)).astype(BF16)

    hid = pl.BlockSpec((tm, D_FF), lambda i: (i, 0))
    row = pl.BlockSpec((tm, D_MODEL), lambda i: (i, 0))
    return _pcall(
        body, name=name, grid=(t_rows // tm,),
        in_specs=[row, hid, hid, _resident((D_FF, D_MODEL)), ANY_SPEC],
        out_specs=[hid, hid, row],
        out_shape=[jax.ShapeDtypeStruct((t_rows, D_FF), BF16), jax.ShapeDtypeStruct((t_rows, D_FF), BF16),
                   jax.ShapeDtypeStruct((t_rows, D_MODEL), BF16)],
        compiler_params=_cp(("arbitrary",), VMEM_BIG),
    )(dh, a, b, w2, after)


def ffn_backward_input(dh, h, norm, da, db, w1, w3, after, tm, name, examples=None):
    t_rows = h.shape[0]

    def body(dh_ref, h_ref, g_ref, da_ref, db_ref, w1_ref, w3_ref, _, dhin_ref, *rest):
        dg_ref = rest[-1]
        dhn = _dot(da_ref[...], w1_ref[...]) + _dot(db_ref[...], w3_ref[...])
        dx, dg = _rms_bwd(h_ref[...], g_ref[...], dhn)
        dhin = dh_ref[...] + dx
        dhin_ref[...] = dhin
        _accumulate(dg_ref, dg, pl.program_id(0) == 0)
        if examples is not None:
            @pl.when(pl.program_id(0) % tiles == tiles - 1)
            def _():
                rest[0][...] = dhin[tm - N_META:, :]

    row = pl.BlockSpec((tm, D_MODEL), lambda i: (i, 0))
    vec = pl.BlockSpec((1, D_MODEL), lambda i: (0, 0))
    hid = pl.BlockSpec((tm, D_FF), lambda i: (i, 0))
    if examples is None:
        out_specs = [row, vec]
        out_shape = [jax.ShapeDtypeStruct((t_rows, D_MODEL), F32), jax.ShapeDtypeStruct((1, D_MODEL), F32)]
    else:
        n_b, seq = examples
        tiles = (seq + N_META) // tm
        out_specs = [pl.BlockSpec((None, tm, D_MODEL), lambda i: (i // tiles, i % tiles, 0)),
                     pl.BlockSpec((None, N_META, D_MODEL), lambda i: (i // tiles, 0, 0)), vec]
        out_shape = [jax.ShapeDtypeStruct((n_b, seq, D_MODEL), F32), jax.ShapeDtypeStruct((n_b, N_META, D_MODEL), F32),
                     jax.ShapeDtypeStruct((1, D_MODEL), F32)]
    return _pcall(
        body, name=name, grid=(t_rows // tm,),
        in_specs=[row, row, vec, hid, hid, _resident((D_FF, D_MODEL)), _resident((D_FF, D_MODEL)), ANY_SPEC],
        out_specs=out_specs, out_shape=out_shape,
        compiler_params=_cp(("arbitrary",), VMEM_BIG),
    )(dh, h, norm, da, db, w1, w3, after)


def ffn_backward_weights(hn, dh, a, b, da, db, tm, tn, name):
    t_rows = hn.shape[0]
    ni = t_rows // tm
    kc = _row_tile(tm, 688)

    def body(hn_ref, dh_ref, a_ref, b_ref, da_ref, db_ref, dw1_ref, dw3_ref, dw2_ref, acc1, acc3, acc2):
        i = pl.program_id(1)
        parts = None
        for r0 in range(0, tm, kc):
            rows = slice(r0, r0 + kc)
            hn_v = hn_ref[rows, :]
            av = a_ref[rows, :].astype(F32)
            hid = (av * _sigmoid(av) * b_ref[rows, :].astype(F32)).astype(BF16)
            new = (_dot_tn(hn_v, da_ref[rows, :]), _dot_tn(hn_v, db_ref[rows, :]), _dot_tn(dh_ref[rows, :], hid))
            parts = new if parts is None else tuple(p + q for p, q in zip(parts, new))
        _accumulate(acc1, parts[0], i == 0)
        _accumulate(acc3, parts[1], i == 0)
        _accumulate(acc2, parts[2], i == 0)

        @pl.when(i == ni - 1)
        def _():
            dw1_ref[...] = acc1[...].T.astype(BF16)
            dw3_ref[...] = acc3[...].T.astype(BF16)
            dw2_ref[...] = acc2[...].T.astype(BF16)

    row = pl.BlockSpec((tm, D_MODEL), lambda j, i: (i, 0))
    hid_blk = pl.BlockSpec((tm, tn), lambda j, i: (i, j))
    w_row = pl.BlockSpec((tn, D_MODEL), lambda j, i: (j, 0))
    out = jax.ShapeDtypeStruct((D_FF, D_MODEL), BF16)
    return _pcall(
        body, name=name, grid=(D_FF // tn, ni),
        in_specs=[row, row, hid_blk, hid_blk, hid_blk, hid_blk],
        out_specs=[w_row, w_row, w_row], out_shape=[out, out, out],
        scratch_shapes=[pltpu.VMEM((D_MODEL, tn), F32)] * 3,
        compiler_params=_cp(("arbitrary", "arbitrary"), VMEM_BIG),
    )(hn, dh, a, b, da, db)


def mix_forward(h, norm, wing, tm, name):
    t_rows = h.shape[0]

    def body(h_ref, g_ref, w_ref, hn_ref, p_ref):
        hn = _rms_fwd(h_ref[...], g_ref[...]).astype(BF16)
        hn_ref[...] = hn
        for j in range(N_DEV):
            p_ref[:, j * IN_BLK:(j + 1) * IN_BLK] = _dot(hn, w_ref[j]).astype(BF16)

    row = pl.BlockSpec((tm, D_MODEL), lambda i: (i, 0))
    return _pcall(
        body, name=name, grid=(t_rows // tm,),
        in_specs=[row, pl.BlockSpec((1, D_MODEL), lambda i: (0, 0)),
                  pl.BlockSpec((N_DEV, D_MODEL, IN_BLK), lambda i: (0, 0, 0))],
        out_specs=[row, pl.BlockSpec((tm, IN_WIDTH), lambda i: (i, 0))],
        out_shape=[jax.ShapeDtypeStruct((t_rows, D_MODEL), BF16), jax.ShapeDtypeStruct((t_rows, IN_WIDTH), BF16)],
        compiler_params=_cp(("arbitrary",), VMEM_BIG),
    )(h, norm, wing)


def mix_backward_act(dh, h, norm, dproj, w_in_full, tm, name):
    t_rows = h.shape[0]

    def body(dh_ref, h_ref, g_ref, dp_ref, w_ref, dhin_ref, dg_ref):
        dx, dg = _rms_bwd(h_ref[...], g_ref[...], _dot_nt(dp_ref[...], w_ref[...]))
        dhin_ref[...] = dh_ref[...] + dx
        _accumulate(dg_ref, dg, pl.program_id(0) == 0)

    row = pl.BlockSpec((tm, D_MODEL), lambda i: (i, 0))
    vec = pl.BlockSpec((1, D_MODEL), lambda i: (0, 0))
    return _pcall(
        body, name=name, grid=(t_rows // tm,),
        in_specs=[row, row, vec, pl.BlockSpec((tm, IN_WIDTH), lambda i: (i, 0)), _resident((D_MODEL, IN_WIDTH))],
        out_specs=[row, vec],
        out_shape=[jax.ShapeDtypeStruct((t_rows, D_MODEL), F32), jax.ShapeDtypeStruct((1, D_MODEL), F32)],
        compiler_params=_cp(("arbitrary",), VMEM_BIG),
    )(dh, h, norm, dproj, w_in_full)


def mix_backward_weights(hn, dproj, tm, name):
    t_rows = hn.shape[0]
    ni = t_rows // tm
    per_step = 2

    kc = _row_tile(tm, 688)

    def body(hn_ref, dp_ref, dw_ref, acc):
        i = pl.program_id(1)
        part = functools.reduce(lambda u, w: u + w, [_dot_tn(hn_ref[r0:r0 + kc, :], dp_ref[r0:r0 + kc, :])
                                                    for r0 in range(0, tm, kc)])
        _accumulate(acc, part, i == 0)

        @pl.when(i == ni - 1)
        def _():
            for k in range(per_step):
                dw_ref[k] = acc[:, k * IN_BLK:(k + 1) * IN_BLK].astype(BF16)

    return _pcall(
        body, name=name, grid=(N_DEV // per_step, ni),
        in_specs=[pl.BlockSpec((tm, D_MODEL), lambda j, i: (i, 0)),
                  pl.BlockSpec((tm, per_step * IN_BLK), lambda j, i: (i, j))],
        out_specs=pl.BlockSpec((per_step, D_MODEL, IN_BLK), lambda j, i: (j, 0, 0)),
        out_shape=jax.ShapeDtypeStruct((N_DEV, D_MODEL, IN_BLK), BF16),
        scratch_shapes=[pltpu.VMEM((D_MODEL, per_step * IN_BLK), F32)],
        compiler_params=_cp(("arbitrary", "arbitrary"), VMEM_BIG),
    )(hn, dproj)


GELU_C = 0.7978845608028654
GELU_K = 0.044715


def _gelu(x):
    return 0.5 * x * (1.0 + jnp.tanh(GELU_C * (x + GELU_K * (x * x * x))))


def _gelu_and_grad(x):
    th = jnp.tanh(GELU_C * (x + GELU_K * (x * x * x)))
    val = 0.5 * x * (1.0 + th)
    grad = 0.5 * (1.0 + th) + 0.5 * x * (1.0 - th * th) * (GELU_C * (1.0 + 3.0 * GELU_K * (x * x)))
    return val, grad


def merge_forward(h, yraw, attn, proj, glu_a, glu_b, w_out, tm, name):
    t_rows = h.shape[0]

    def body(h_ref, y_ref, at_ref, gate_ref, a_ref, b_ref, wo_ref, out_ref):
        y = _gelu(y_ref[...]).astype(BF16)
        ssm = _dot(y, a_ref[...]) * _sigmoid(_dot(y, b_ref[...]))
        ga = gate_ref[:, :D_MODEL].astype(F32)
        gs = gate_ref[:, D_MODEL:].astype(F32)
        merged = _sigmoid(ga) * at_ref[...].astype(F32) + _sigmoid(gs) * ssm
        out_ref[...] = h_ref[...] + _dot(merged.astype(BF16), wo_ref[...])

    row = pl.BlockSpec((tm, D_MODEL), lambda i: (i, 0))
    glu = pl.BlockSpec((SSM_WIDTH, D_MODEL), lambda i: (0, 0))
    return _pcall(
        body, name=name, grid=(t_rows // tm,),
        in_specs=[row, pl.BlockSpec((tm, SSM_WIDTH), lambda i: (i, 0)), row,
                  pl.BlockSpec((tm, 2 * D_MODEL), lambda i: (i, 1)), glu, glu,
                  pl.BlockSpec((D_MODEL, D_MODEL), lambda i: (0, 0))],
        out_specs=row, out_shape=jax.ShapeDtypeStruct((t_rows, D_MODEL), F32),
        compiler_params=_cp(("arbitrary",), VMEM_BIG),
    )(h, yraw, attn, proj, glu_a, glu_b, w_out)


def merge_backward(dh, yraw, attn, proj, glu_a, glu_b, w_out, after, tm, name):
    t_rows = dh.shape[0]

    def body(dh_ref, y_ref, at_ref, gate_ref, a_ref, b_ref, wo_ref, _,
             dat_ref, dy_ref, dgate_ref, d16_ref, mg_ref, y16_ref, dya_ref, dyb_ref):
        d16 = dh_ref[...].astype(BF16)
        d16_ref[...] = d16
        gel, dgel = _gelu_and_grad(y_ref[...].astype(F32))
        y16 = gel.astype(BF16)
        y16_ref[...] = y16
        dy = None
        for c0, cw in _col_chunks(D_MODEL):
            cols = slice(c0, c0 + cw)
            gcols = slice(D_MODEL + c0, D_MODEL + c0 + cw)
            dmerged = _dot_nt(d16, wo_ref[cols, :])
            ya = _dot(y16, a_ref[:, cols])
            sb = _sigmoid(_dot(y16, b_ref[:, cols]))
            ssm = ya * sb
            sa = _sigmoid(gate_ref[:, cols].astype(F32))
            ss = _sigmoid(gate_ref[:, gcols].astype(F32))
            attn_v = at_ref[:, cols].astype(F32)
            mg_ref[:, cols] = (sa * attn_v + ss * ssm).astype(BF16)
            dat_ref[:, cols] = (dmerged * sa).astype(BF16)
            dgate_ref[:, cols] = (dmerged * attn_v * sa * (1.0 - sa)).astype(BF16)
            dgate_ref[:, gcols] = (dmerged * ssm * ss * (1.0 - ss)).astype(BF16)
            dssm = dmerged * ss
            dya = (dssm * sb).astype(BF16)
            dyb = (dssm * ya * sb * (1.0 - sb)).astype(BF16)
            dya_ref[:, cols] = dya
            dyb_ref[:, cols] = dyb
            part = _dot_nt(dya, a_ref[:, cols]) + _dot_nt(dyb, b_ref[:, cols])
            dy = part if dy is None else dy + part
        dy_ref[...] = (dy * dgel).astype(BF16)

    row = pl.BlockSpec((tm, D_MODEL), lambda i: (i, 0))
    ssm_row = pl.BlockSpec((tm, SSM_WIDTH), lambda i: (i, 0))
    gates = pl.BlockSpec((tm, 2 * D_MODEL), lambda i: (i, 1))
    wide = jax.ShapeDtypeStruct((t_rows, D_MODEL), BF16)
    narrow = jax.ShapeDtypeStruct((t_rows, SSM_WIDTH), BF16)
    return _pcall(
        body, name=name, grid=(t_rows // tm,),
        in_specs=[row, ssm_row, row, gates, _resident((SSM_WIDTH, D_MODEL)), _resident((SSM_WIDTH, D_MODEL)),
                  _resident((D_MODEL, D_MODEL)), ANY_SPEC],
        out_specs=[row, ssm_row, gates, row, row, ssm_row, row, row],
        out_shape=[wide, narrow, jax.ShapeDtypeStruct((t_rows, IN_WIDTH), BF16), wide, wide, narrow, wide, wide],
        compiler_params=_cp(("arbitrary",), VMEM_BIG),
    )(dh, yraw, attn, proj, glu_a, glu_b, w_out, after)


def merge_backward_weights(d16, merged, y16, dya, dyb, tm, name):
    t_rows = d16.shape[0]

    def body(d_ref, mg_ref, y_ref, dya_ref, dyb_ref, dwo_ref, da_ref, db_ref):
        first = pl.program_id(0) == 0
        y16 = y_ref[...]
        _accumulate(dwo_ref, _dot_tn(mg_ref[...], d_ref[...]), first)
        _accumulate(da_ref, _dot_tn(y16, dya_ref[...]), first)
        _accumulate(db_ref, _dot_tn(y16, dyb_ref[...]), first)

    row = pl.BlockSpec((tm, D_MODEL), lambda i: (i, 0))
    ssm_row = pl.BlockSpec((tm, SSM_WIDTH), lambda i: (i, 0))
    glu = pl.BlockSpec((SSM_WIDTH, D_MODEL), lambda i: (0, 0))
    wo = pl.BlockSpec((D_MODEL, D_MODEL), lambda i: (0, 0))
    return _pcall(
        body, name=name, grid=(t_rows // tm,),
        in_specs=[row, row, ssm_row, row, row], out_specs=[wo, glu, glu],
        out_shape=[jax.ShapeDtypeStruct((D_MODEL, D_MODEL), F32), jax.ShapeDtypeStruct((SSM_WIDTH, D_MODEL), F32),
                   jax.ShapeDtypeStruct((SSM_WIDTH, D_MODEL), F32)],
        compiler_params=_cp(("arbitrary",), VMEM_BIG),
    )(d16, merged, y16, dya, dyb)


def final_loss_backward(h, target, norm, seq, tm, name):
    t_rows = h.shape[0]
    tiles_per_example = (seq + N_META) // tm

    def body(h_ref, t_ref, g_ref, dh_ref, loss_ref, dg_ref):
        i = pl.program_id(0)
        x = h_ref[...]
        g = g_ref[...]
        r = lax.rsqrt(jnp.mean(x * x, axis=-1, keepdims=True) + NORM_EPS)
        xh = x * r
        pos = lax.broadcasted_iota(jnp.int32, (tm, 1), 0) + (i % tiles_per_example) * tm
        diff = jnp.where(pos < seq, xh * g - t_ref[...], 0.0)
        part = 0.5 * jnp.sum(jnp.sum(diff * diff, axis=-1, keepdims=True), axis=0, keepdims=True) / D_MODEL
        dy = diff / D_MODEL
        t = dy * g
        dh_ref[...] = r * (t - xh * jnp.mean(t * xh, axis=-1, keepdims=True))
        _accumulate(loss_ref, jnp.broadcast_to(part, (1, LANES)), i == 0)
        _accumulate(dg_ref, jnp.sum(dy * xh, axis=0, keepdims=True), i == 0)

    row = pl.BlockSpec((tm, D_MODEL), lambda i: (i, 0))
    vec = pl.BlockSpec((1, D_MODEL), lambda i: (0, 0))
    per_example = pl.BlockSpec((None, tm, D_MODEL), lambda i: (i // tiles_per_example, i % tiles_per_example, 0))
    return _pcall(
        body, name=name, grid=(t_rows // tm,),
        in_specs=[row, per_example, vec],
        out_specs=[row, pl.BlockSpec((1, LANES), lambda i: (0, 0)), vec],
        out_shape=[jax.ShapeDtypeStruct((t_rows, D_MODEL), F32), jax.ShapeDtypeStruct((1, LANES), F32),
                   jax.ShapeDtypeStruct((1, D_MODEL), F32)],
        compiler_params=_cp(("arbitrary",), VMEM_BIG),
    )(h, target, norm)


ATTN_SCALE = HEAD_DIM ** -0.5
STACK_HEADS = (0, 2, 1, 3)
META_PAD = LANES - N_META


def _lane_half(shape, hf):
    lane = lax.broadcasted_iota(jnp.int32, shape, 1)
    return (lane < HEAD_DIM) if hf == 0 else (lane >= HEAD_DIM)


def _kv_variants(ref, rows, kh, pad_rows=0):
    tile = kh // 2
    t = ref[rows, tile * LANES:(tile + 1) * LANES].astype(F32)
    swapped = pltpu.roll(t, HEAD_DIM, axis=1)
    at_low, at_high = (t, swapped) if kh % 2 == 0 else (swapped, t)
    lo = jnp.where(_lane_half(t.shape, 0), at_low, 0.0).astype(BF16)
    hi = jnp.where(_lane_half(t.shape, 1), at_high, 0.0).astype(BF16)
    if pad_rows:
        zeros = jnp.zeros((pad_rows, LANES), BF16)
        lo, hi = jnp.concatenate([lo, zeros], axis=0), jnp.concatenate([hi, zeros], axis=0)
    return lo, hi


def _key_tiles(ref, key_rows, kh):
    return [_kv_variants(ref, r, kh, META_PAD if i == len(key_rows) - 1 else 0) for i, r in enumerate(key_rows)]


def _to_kv_lanes(lo, hi, kh):
    lo = jnp.where(_lane_half(lo.shape, 0), lo, 0.0)
    hi = jnp.where(_lane_half(hi.shape, 1), hi, 0.0)
    if kh % 2 == 0:
        return lo + pltpu.roll(hi, HEAD_DIM, axis=1)
    return pltpu.roll(lo, HEAD_DIM, axis=1) + hi


def _stacked(ref, rows, kh):
    col = kh * 2 * LANES
    return jnp.concatenate([ref[rows, col:col + LANES], ref[rows, col + LANES:col + 2 * LANES]], axis=0)


def _sink_column(sink_ref, kh, nq):
    row = lax.broadcasted_iota(jnp.int32, (4 * nq, 1), 0)
    col = jnp.zeros((4 * nq, 1), F32)
    for quarter, g in enumerate(STACK_HEADS):
        col = jnp.where(row // nq == quarter, sink_ref[0, kh * Q_PER_KV + g], col)
    return col


def _softmax_parts(qs, key_tiles, masks, sink):
    scores = []
    for (k_lo, k_hi), mask in zip(key_tiles, masks):
        s = jnp.concatenate([_dot_nt(qs, k_lo), _dot_nt(qs, k_hi)], axis=0) * ATTN_SCALE
        scores.append(s if mask is None else jnp.where(mask, s, NEG_INF))
    m = jnp.maximum(_row_reduce(scores, jnp.maximum, jnp.max), sink)
    probs = [jnp.exp(s - m) for s in scores]
    e_sink = jnp.exp(sink - m)
    den = _row_sums(probs) + e_sink
    return probs, 1.0 / den, e_sink


def _row_reduce(tiles, combine, reduce):
    chunks = [t[:, c:c + LANES] for t in tiles for c in range(0, t.shape[-1], LANES)]
    return reduce(functools.reduce(combine, chunks), axis=-1, keepdims=True)


def _row_sums(tiles):
    return _row_reduce(tiles, lambda u, w: u + w, jnp.sum)


def _band_mask(nq, first):
    keys = BLOCK if first else 2 * BLOCK
    qi = lax.broadcasted_iota(jnp.int32, (4 * nq, keys), 0) % nq
    kj = lax.broadcasted_iota(jnp.int32, (4 * nq, keys), 1)
    if first:
        return kj <= qi
    return jnp.logical_and(kj > qi, kj <= qi + BLOCK)


def _meta_mask(nq, causal):
    qi = lax.broadcasted_iota(jnp.int32, (4 * nq, LANES), 0) % nq
    kj = lax.broadcasted_iota(jnp.int32, (4 * nq, LANES), 1)
    return jnp.logical_and(kj < N_META, kj <= qi) if causal else kj < N_META


def _attention_schedule(seq, queries, carry):
    meta_rows = pl.ds(seq, N_META)
    meta_ok = _meta_mask(BLOCK, False)
    carry = queries(pl.ds(0, BLOCK), BLOCK, [pl.ds(0, BLOCK), meta_rows], [_band_mask(BLOCK, True), meta_ok], carry)

    def block(n, c):
        r0 = pl.multiple_of(n * BLOCK, BLOCK)
        p0 = pl.multiple_of((n - 1) * BLOCK, BLOCK)
        return queries(pl.ds(r0, BLOCK), BLOCK, [pl.ds(p0, 2 * BLOCK), meta_rows], [_band_mask(BLOCK, False), meta_ok], c)

    carry = lax.fori_loop(1, seq // BLOCK, block, carry)
    return queries(meta_rows, N_META, [meta_rows], [_meta_mask(N_META, True)], carry)


def attention_forward(proj3, sinks, seq, name):
    n_b, n_l, _ = proj3.shape

    def body(sink_ref, q_ref, k_ref, v_ref, o_ref):
        def queries(q_rows, nq, key_rows, masks, carry):
            for kh in range(N_KV_HEADS):
                ks = _key_tiles(k_ref, key_rows, kh)
                vs = _key_tiles(v_ref, key_rows, kh)
                qs = _stacked(q_ref, q_rows, kh)
                probs, inv, _ = _softmax_parts(qs, ks, masks, _sink_column(sink_ref, kh, nq))
                probs = [p.astype(BF16) for p in probs]
                o_lo = functools.reduce(lambda u, w: u + w, [_dot(p[:2 * nq], v_lo) for p, (v_lo, _) in zip(probs, vs)])
                o_hi = functools.reduce(lambda u, w: u + w, [_dot(p[2 * nq:], v_hi) for p, (_, v_hi) in zip(probs, vs)])
                out = (o_lo * inv[:2 * nq] + o_hi * inv[2 * nq:]).astype(BF16)
                col = kh * 2 * LANES
                o_ref[q_rows, col:col + LANES] = out[:nq]
                o_ref[q_rows, col + LANES:col + 2 * LANES] = out[nq:]
            return carry

        _attention_schedule(seq, queries, 0)

    return _pcall(
        body, name=name, grid=(n_b,),
        in_specs=[pl.BlockSpec(memory_space=pltpu.SMEM),
                  pl.BlockSpec((None, n_l, D_MODEL), lambda b: (b, 0, 0)),
                  pl.BlockSpec((None, n_l, KV_WIDTH), lambda b: (b, 0, D_MODEL // KV_WIDTH)),
                  pl.BlockSpec((None, n_l, KV_WIDTH), lambda b: (b, 0, D_MODEL // KV_WIDTH + 1))],
        out_specs=pl.BlockSpec((None, n_l, D_MODEL), lambda b: (b, 0, 0)),
        out_shape=jax.ShapeDtypeStruct((n_b, n_l, D_MODEL), BF16),
        compiler_params=_cp(("arbitrary",), VMEM_BIG),
    )(sinks, proj3, proj3, proj3)


def attention_backward(proj3, dattn3, dproj3, sinks, after, seq, name):
    n_b, n_l, _ = proj3.shape
    qkv_width = D_MODEL + 2 * KV_WIDTH

    def body(sink_ref, q_ref, k_ref, v_ref, do_ref, _, __, dqkv_ref, dsink_ref, dk_ref, dv_ref):
        dk_ref[...] = jnp.zeros_like(dk_ref)
        dv_ref[...] = jnp.zeros_like(dv_ref)
        sub = lax.broadcasted_iota(jnp.int32, (SUBLANES, LANES), 0)
        lane = lax.broadcasted_iota(jnp.int32, (SUBLANES, LANES), 1)

        def queries(q_rows, nq, key_rows, masks, dsink):
            for kh in range(N_KV_HEADS):
                ks = _key_tiles(k_ref, key_rows, kh)
                vs = _key_tiles(v_ref, key_rows, kh)
                qs = _stacked(q_ref, q_rows, kh)
                dos = _stacked(do_ref, q_rows, kh)
                probs, inv, e_sink = _softmax_parts(qs, ks, masks, _sink_column(sink_ref, kh, nq))
                probs = [p * inv for p in probs]
                dps = [jnp.concatenate([_dot_nt(dos, v_lo), _dot_nt(dos, v_hi)], axis=0) for v_lo, v_hi in vs]
                delta = _row_sums([p * dp for p, dp in zip(probs, dps)])
                d_sink = -(e_sink * inv) * delta
                for quarter, g in enumerate(STACK_HEADS):
                    d_here = jnp.sum(d_sink[quarter * nq:(quarter + 1) * nq], axis=0, keepdims=True)
                    dsink = dsink + jnp.where(jnp.logical_and(sub == 0, lane == kh * Q_PER_KV + g), d_here, 0.0)
                dq = None
                tile = slice((kh // 2) * LANES, (kh // 2 + 1) * LANES)
                for r, p, dp, (k_lo, k_hi) in zip(key_rows, probs, dps, ks):
                    ds = (p * (dp - delta)).astype(BF16)
                    p16 = p.astype(BF16)
                    dq_x = _dot(ds[:2 * nq], k_lo) + _dot(ds[2 * nq:], k_hi)
                    dq = dq_x if dq is None else dq + dq_x
                    d_k = _to_kv_lanes(_dot_tn(ds[:2 * nq], qs), _dot_tn(ds[2 * nq:], qs), kh) * ATTN_SCALE
                    d_v = _to_kv_lanes(_dot_tn(p16[:2 * nq], dos), _dot_tn(p16[2 * nq:], dos), kh)
                    n_keys = r.size
                    dk_ref[r, tile] += d_k[:n_keys]
                    dv_ref[r, tile] += d_v[:n_keys]
                dq = (dq * ATTN_SCALE).astype(BF16)
                col = kh * 2 * LANES
                dqkv_ref[q_rows, col:col + LANES] = dq[:nq]
                dqkv_ref[q_rows, col + LANES:col + 2 * LANES] = dq[nq:]
            return dsink

        dsink_ref[...] = _attention_schedule(seq, queries, jnp.zeros((SUBLANES, LANES), F32))
        dqkv_ref[:, D_MODEL:D_MODEL + KV_WIDTH] = dk_ref[...].astype(BF16)
        dqkv_ref[:, D_MODEL + KV_WIDTH:] = dv_ref[...].astype(BF16)

    return _pcall(
        body, name=name, grid=(n_b,),
        in_specs=[pl.BlockSpec(memory_space=pltpu.SMEM),
                  pl.BlockSpec((None, n_l, D_MODEL), lambda b: (b, 0, 0)),
                  pl.BlockSpec((None, n_l, KV_WIDTH), lambda b: (b, 0, D_MODEL // KV_WIDTH)),
                  pl.BlockSpec((None, n_l, KV_WIDTH), lambda b: (b, 0, D_MODEL // KV_WIDTH + 1)),
                  pl.BlockSpec((None, n_l, D_MODEL), lambda b: (b, 0, 0)),
                  ANY_SPEC, ANY_SPEC],
        out_specs=[pl.BlockSpec((None, n_l, qkv_width), lambda b: (b, 0, 0)),
                   pl.BlockSpec((None, SUBLANES, LANES), lambda b: (b, 0, 0))],
        out_shape=[jax.ShapeDtypeStruct(dproj3.shape, BF16), jax.ShapeDtypeStruct((n_b, SUBLANES, LANES), F32)],
        scratch_shapes=[pltpu.VMEM((n_l, KV_WIDTH), F32), pltpu.VMEM((n_l, KV_WIDTH), F32)],
        input_output_aliases={5: 0},
        compiler_params=_cp(("arbitrary",), VMEM_BIG),
    )(sinks, proj3, proj3, proj3, dattn3, dproj3, after)


TAB_ROWS = 8
SCAN_UNROLL = 4


def _cmul(ar, ai, br, bi):
    return ar * br - ai * bi, ar * bi + ai * br


def _discretise(ar, ai, ls):
    step = jnp.exp(ls)
    mag = jnp.exp(ar * step)
    ang = ai * step
    cos, sin = jnp.cos(ang), jnp.sin(ang)
    lr, li = mag * cos, mag * sin
    den = ar * ar + ai * ai
    nr, ni = lr - 1.0, li
    cr = (nr * ar + ni * ai) / den
    ci = (ni * ar - nr * ai) / den
    return step, mag, lr, li, den, nr, ni, cr, ci


def _scan_tables(lr, li, reverse):
    n = lr.shape[-1]
    pw = [(lr, li)]
    for _ in range(SUBLANES - 1):
        pw.append(_cmul(pw[-1][0], pw[-1][1], lr, li))
    row = lax.broadcasted_iota(jnp.int32, (SUBLANES, n), 0)
    out = []
    for d in (1, 2, 4):
        ok = (row + d <= SUBLANES - 1) if reverse else (row >= d)
        out += [jnp.where(ok, pw[d - 1][0], 0.0), jnp.where(ok, pw[d - 1][1], 0.0)]
    cr = jnp.zeros((SUBLANES, n), F32)
    ci = jnp.zeros((SUBLANES, n), F32)
    for r in range(SUBLANES):
        e = (SUBLANES - r) if reverse else (r + 1)
        cr = jnp.where(row == r, pw[e - 1][0], cr)
        ci = jnp.where(row == r, pw[e - 1][1], ci)
    return out + [cr, ci]


def ssm_prepare(ar, ai, ls, br_t, bi_t, name):
    def body(ar_ref, ai_ref, ls_ref, br_ref, bi_ref, bbr_ref, bbi_ref, tf_ref, tr_ref):
        _, _, lr, li, _, _, _, cr, ci = _discretise(ar_ref[...], ai_ref[...], ls_ref[...])
        br, bi = br_ref[...], bi_ref[...]
        bbr_ref[...] = cr * br - ci * bi
        bbi_ref[...] = cr * bi + ci * br
        for k, t in enumerate(_scan_tables(lr, li, False)):
            tf_ref[k] = t
        for k, t in enumerate(_scan_tables(lr, -li, True)):
            tr_ref[k] = t

    return _pcall(
        body, name=name,
        out_shape=[jax.ShapeDtypeStruct((SSM_GROUP, N_STATES), F32), jax.ShapeDtypeStruct((SSM_GROUP, N_STATES), F32),
                   jax.ShapeDtypeStruct((TAB_ROWS, SUBLANES, N_STATES), F32),
                   jax.ShapeDtypeStruct((TAB_ROWS, SUBLANES, N_STATES), F32)],
    )(ar, ai, ls, br_t, bi_t)


def ssm_param_backward(ar, ai, ls, br_t, bi_t, dlr_p, dli_p, dbbr, dbbi, group_sum, name):
    def body(ar_ref, ai_ref, ls_ref, br_ref, bi_ref, dlr_ref, dli_ref, dbbr_ref, dbbi_ref, gs_ref,
             dar_ref, dai_ref, dls_ref, dbr_ref, dbi_ref):
        ar, ai = ar_ref[...], ai_ref[...]
        step, mag, lr, li, den, nr, ni, cr, ci = _discretise(ar, ai, ls_ref[...])
        br, bi, dbbr_v, dbbi_v = br_ref[...], bi_ref[...], dbbr_ref[...], dbbi_ref[...]
        dbr_ref[...] = cr * dbbr_v + ci * dbbi_v
        dbi_ref[...] = cr * dbbi_v - ci * dbbr_v
        dcr = jnp.sum(dbbr_v * br + dbbi_v * bi, axis=0, keepdims=True)
        dci = jnp.sum(dbbi_v * br - dbbr_v * bi, axis=0, keepdims=True)
        dnr = (dcr * ar - dci * ai) / den
        dni = (dcr * ai + dci * ar) / den
        dden = -(cr * dcr + ci * dci) / den
        dar = (dcr * nr + dci * ni) / den + dden * 2.0 * ar
        dai = (dcr * ni - dci * nr) / den + dden * 2.0 * ai
        dlr = jnp.sum(dlr_ref[...], axis=0, keepdims=True) + dnr
        dli = jnp.sum(dli_ref[...], axis=0, keepdims=True) + dni
        dmag = (dlr * lr + dli * li) / mag
        dang = dli * lr - dlr * li
        dar_ref[...] = dar + dmag * mag * step
        dai_ref[...] = dai + dang * step
        dstep = dmag * mag * ar + dang * ai
        dls_ref[...] = jnp.dot(dstep * step, gs_ref[...], preferred_element_type=F32, precision=lax.Precision.HIGHEST)

    vec = jax.ShapeDtypeStruct((1, N_STATES), F32)
    mat = jax.ShapeDtypeStruct((SSM_GROUP, N_STATES), F32)
    return _pcall(body, name=name, out_shape=[vec, vec, jax.ShapeDtypeStruct((1, LANES), F32), mat, mat])(
        ar, ai, ls, br_t, bi_t, dlr_p, dli_p, dbbr, dbbi, group_sum)


def _scan_rows(a, b, tabs, carry, reverse):
    for k, d in enumerate((1, 2, 4)):
        shift = SUBLANES - d if reverse else d
        sr, si = pltpu.roll(a, shift, axis=0), pltpu.roll(b, shift, axis=0)
        pr, pi = _cmul(tabs[2 * k], tabs[2 * k + 1], sr, si)
        a, b = a + pr, b + pi
    pr, pi = _cmul(tabs[6], tabs[7], carry[0], carry[1])
    return a + pr, b + pi


def _time_groups(seq, reverse):
    meta = [seq + SUBLANES * g for g in range(N_META // SUBLANES)]
    return meta[::-1] if reverse else meta


def ssm_forward_scan(proj3, b_comb, tabf, c_comb, dvec, seq, name):
    n_b, n_l, _ = proj3.shape
    u_blk = (D_MODEL + 2 * KV_WIDTH) // LANES

    def body(u_ref, b_ref, tab_ref, c_ref, d_ref, x_ref, y_ref, bu, xs):
        j = pl.program_id(1)
        u = u_ref[...]
        bu[...] = _dot(u, b_ref[...])
        tabs = [tab_ref[k] for k in range(TAB_ROWS)]

        def group(r0, carry):
            rows = pl.ds(r0, SUBLANES)
            a, b = _scan_rows(bu[rows, :SCAN_COLS], bu[rows, SCAN_COLS:], tabs, carry, False)
            xs[rows, :SCAN_COLS] = a
            xs[rows, SCAN_COLS:] = b
            return (jnp.broadcast_to(a[SUBLANES - 1:, :], a.shape), jnp.broadcast_to(b[SUBLANES - 1:, :], b.shape))

        zero = jnp.zeros((SUBLANES, SCAN_COLS), F32)
        carry = (zero, zero)
        for r0 in _time_groups(seq, False):
            carry = group(r0, carry)
        span = SCAN_UNROLL * SUBLANES

        def groups(t, c):
            for k in range(SCAN_UNROLL):
                c = group(pl.multiple_of(t * span, span) + k * SUBLANES, c)
            return c

        lax.fori_loop(0, seq // span, groups, carry)
        x16 = xs[...].astype(BF16)
        x_ref[...] = x16
        contrib = _dot(x16, c_ref[...])

        @pl.when(j % 2 == 0)
        def _():
            y_ref[...] = contrib + d_ref[...] * u.astype(F32)

        @pl.when(j % 2 == 1)
        def _():
            y_ref[...] += contrib

    return _pcall(
        body, name=name, grid=(n_b, N_SCAN_BLK),
        in_specs=[pl.BlockSpec((None, n_l, LANES), lambda b, j: (b, 0, u_blk + j // 2)),
                  pl.BlockSpec((None, LANES, 2 * SCAN_COLS), lambda b, j: (j, 0, 0)),
                  pl.BlockSpec((TAB_ROWS, SUBLANES, SCAN_COLS), lambda b, j: (0, 0, j)),
                  pl.BlockSpec((None, 2 * SCAN_COLS, LANES), lambda b, j: (j, 0, 0)),
                  pl.BlockSpec((1, LANES), lambda b, j: (0, j // 2))],
        out_specs=[pl.BlockSpec((None, n_l, 2 * SCAN_COLS), lambda b, j: (b, 0, j)),
                   pl.BlockSpec((None, n_l, LANES), lambda b, j: (b, 0, j // 2))],
        out_shape=[jax.ShapeDtypeStruct((n_b, n_l, 2 * N_STATES), BF16),
                   jax.ShapeDtypeStruct((n_b, n_l, SSM_WIDTH), F32)],
        scratch_shapes=[pltpu.VMEM((n_l, 2 * SCAN_COLS), F32)] * 2,
        compiler_params=_cp(("arbitrary", "arbitrary"), VMEM_BIG),
    )(proj3, b_comb, tabf, c_comb, dvec)


def ssm_backward_scan(dyraw3, xs3, dproj3, c_comb_t, tabr, b_comb_t, dvec, seq, name):
    n_b, n_l, _ = xs3.shape
    u_blk = (D_MODEL + 2 * KV_WIDTH) // LANES

    def body(dy_ref, x_ref, _, c_ref, tab_ref, b_ref, d_ref, du_ref, g_ref, dlr_ref, dli_ref, dx, gs, xs, du_acc):
        j = pl.program_id(1)
        dy = dy_ref[...]
        dx[...] = _dot(dy, c_ref[...])
        xs[...] = x_ref[...].astype(F32)
        tabs = [tab_ref[k] for k in range(TAB_ROWS)]
        last_row = lax.broadcasted_iota(jnp.int32, (SUBLANES, SCAN_COLS), 0) == SUBLANES - 1

        def group(r0, state):
            cr, ci, acc_r, acc_i = state
            rows = pl.ds(r0, SUBLANES)
            a, b = _scan_rows(dx[rows, :SCAN_COLS], dx[rows, SCAN_COLS:], tabs, (cr, ci), True)
            gs[rows, :SCAN_COLS] = a
            gs[rows, SCAN_COLS:] = b
            na = jnp.where(last_row, cr, pltpu.roll(a, SUBLANES - 1, axis=0))
            nb = jnp.where(last_row, ci, pltpu.roll(b, SUBLANES - 1, axis=0))
            xa, xb = xs[rows, :SCAN_COLS], xs[rows, SCAN_COLS:]
            return (jnp.broadcast_to(a[:1, :], a.shape), jnp.broadcast_to(b[:1, :], b.shape),
                    acc_r + na * xa + nb * xb, acc_i + nb * xa - na * xb)

        zero = jnp.zeros((SUBLANES, SCAN_COLS), F32)
        span = SCAN_UNROLL * SUBLANES
        n_spans = seq // span

        def groups(t, s):
            for k in reversed(range(SCAN_UNROLL)):
                s = group(pl.multiple_of((n_spans - 1 - t) * span, span) + k * SUBLANES, s)
            return s

        state = lax.fori_loop(0, n_spans, groups, (zero, zero, zero, zero))
        for r0 in _time_groups(seq, True):
            state = group(r0, state)
        dlr_ref[...] = state[2]
        dli_ref[...] = state[3]
        g16 = gs[...].astype(BF16)
        g_ref[...] = g16
        contrib = _dot(g16, b_ref[...])

        @pl.when(j % 2 == 0)
        def _():
            du_acc[...] = contrib + d_ref[...] * dy.astype(F32)

        @pl.when(j % 2 == 1)
        def _():
            du_ref[...] = (du_acc[...] + contrib).astype(BF16)

    state_blk = pl.BlockSpec((None, n_l, 2 * SCAN_COLS), lambda b, j: (b, 0, j))
    dl_blk = pl.BlockSpec((None, SUBLANES, SCAN_COLS), lambda b, j: (b, 0, j))
    return _pcall(
        body, name=name, grid=(n_b, N_SCAN_BLK),
        in_specs=[pl.BlockSpec((None, n_l, LANES), lambda b, j: (b, 0, j // 2)), state_blk,
                  pl.BlockSpec(memory_space=pl.ANY),
                  pl.BlockSpec((None, LANES, 2 * SCAN_COLS), lambda b, j: (j, 0, 0)),
                  pl.BlockSpec((TAB_ROWS, SUBLANES, SCAN_COLS), lambda b, j: (0, 0, j)),
                  pl.BlockSpec((None, 2 * SCAN_COLS, LANES), lambda b, j: (j, 0, 0)),
                  pl.BlockSpec((1, LANES), lambda b, j: (0, j // 2))],
        out_specs=[pl.BlockSpec((None, n_l, LANES), lambda b, j: (b, 0, u_blk + j // 2)), state_blk, dl_blk, dl_blk],
        out_shape=[jax.ShapeDtypeStruct(dproj3.shape, BF16), jax.ShapeDtypeStruct((n_b, n_l, 2 * N_STATES), BF16),
                   jax.ShapeDtypeStruct((n_b, SUBLANES, N_STATES), F32), jax.ShapeDtypeStruct((n_b, SUBLANES, N_STATES), F32)],
        scratch_shapes=[pltpu.VMEM((n_l, 2 * SCAN_COLS), F32)] * 3 + [pltpu.VMEM((n_l, LANES), F32)],
        input_output_aliases={2: 0},
        compiler_params=_cp(("arbitrary", "arbitrary"), VMEM_BIG),
    )(dyraw3, xs3, dproj3, c_comb_t, tabr, b_comb_t, dvec)


def ssm_param_grads(proj, gs, xs, dyraw, tm, name):
    t_rows = proj.shape[0]
    ni = t_rows // tm
    u_blk = (D_MODEL + 2 * KV_WIDTH) // LANES
    width = 2 * SCAN_COLS

    def body(u_ref, g_ref, x_ref, dy_ref, db_ref, dc_ref, dd_ref):
        cb, i = pl.program_id(0), pl.program_id(1)
        u, dy = u_ref[...], dy_ref[...]
        _accumulate(db_ref, _dot_tn(u, g_ref[...]), i == 0)
        _accumulate(dc_ref, _dot_tn(x_ref[...], dy), i == 0)

        @pl.when(cb % 2 == 0)
        def _():
            _accumulate(dd_ref, jnp.sum(dy.astype(F32) * u.astype(F32), axis=0, keepdims=True), i == 0)

    return _pcall(
        body, name=name, grid=(N_SCAN_BLK, ni),
        in_specs=[pl.BlockSpec((tm, LANES), lambda cb, i: (i, u_blk + cb // 2)),
                  pl.BlockSpec((tm, width), lambda cb, i: (i, cb)),
                  pl.BlockSpec((tm, width), lambda cb, i: (i, cb)),
                  pl.BlockSpec((tm, LANES), lambda cb, i: (i, cb // 2))],
        out_specs=[pl.BlockSpec((None, LANES, width), lambda cb, i: (cb, 0, 0)),
                   pl.BlockSpec((None, width, LANES), lambda cb, i: (cb, 0, 0)),
                   pl.BlockSpec((1, LANES), lambda cb, i: (0, cb // 2))],
        out_shape=[jax.ShapeDtypeStruct((N_SCAN_BLK, LANES, width), F32),
                   jax.ShapeDtypeStruct((N_SCAN_BLK, width, LANES), F32), jax.ShapeDtypeStruct((1, SSM_WIDTH), F32)],
        compiler_params=_cp(("arbitrary", "arbitrary"), VMEM_BIG),
    )(proj, gs, xs, dyraw)


def sum_leading(x, name):
    def body(x_ref, o_ref):
        acc = x_ref[0]
        for k in range(1, x.shape[0]):
            acc = acc + x_ref[k]
        o_ref[...] = acc

    return _pcall(body, name=name, out_shape=jax.ShapeDtypeStruct(x.shape[1:], x.dtype))(x)


WEIGHTS = ['meta_tokens', 'ffn1_norm', 'ffn1_w1', 'ffn1_w3', 'ffn1_w2', 'mix_norm', 'w_in', 'attn_sinks', 'ssm_a_re',
           'ssm_a_im', 'ssm_log_step', 'ssm_b_re', 'ssm_b_im', 'ssm_c_re', 'ssm_c_im', 'ssm_d', 'ssm_glu_a', 'ssm_glu_b',
           'w_out', 'ffn2_norm', 'ffn2_w1', 'ffn2_w3', 'ffn2_w2', 'final_norm']
SHARDED = ['ffn1_w1', 'ffn1_w3', 'ffn1_w2', 'ffn2_w1', 'ffn2_w3', 'ffn2_w2', 'w_in', 'ssm_glu_a', 'ssm_glu_b', 'w_out']
REPLICATED = ['ffn1_norm', 'mix_norm', 'ffn2_norm', 'final_norm', 'attn_sinks', 'ssm_a_re', 'ssm_a_im', 'ssm_log_step',
              'ssm_b_re', 'ssm_b_im', 'ssm_c_re', 'ssm_c_im', 'ssm_d']
PACK_COLS = 1024


def _pack(arrays):
    parts = []
    for a in arrays:
        flat = a.reshape(-1)
        chunk = SUBLANES * PACK_COLS
        padded = -(-flat.shape[0] // chunk) * chunk
        parts.append(jnp.pad(flat, (0, padded - flat.shape[0])).reshape(-1, PACK_COLS))
    return jnp.concatenate(parts, axis=0)


def _unpack(packed, shapes):
    out, row = [], 0
    for shape in shapes:
        size = 1
        for s in shape:
            size *= s
        chunk = SUBLANES * PACK_COLS
        rows = -(-size // chunk) * SUBLANES
        out.append(packed[row:row + rows].reshape(-1)[:size].reshape(shape))
        row += rows
    return out


def kernel(x, meta_tokens, ffn1_norm, ffn1_w1, ffn1_w3, ffn1_w2, mix_norm, w_in, attn_sinks, ssm_a_re, ssm_a_im, ssm_log_step, ssm_b_re, ssm_b_im, ssm_c_re, ssm_c_im, ssm_d, ssm_glu_a, ssm_glu_b, w_out, ffn2_norm, ffn2_w1, ffn2_w3, ffn2_w2, final_norm, loss_target, m_meta_tokens, m_ffn1_norm, m_ffn1_w1, m_ffn1_w3, m_ffn1_w2, m_mix_norm, m_w_in, m_attn_sinks, m_ssm_a_re, m_ssm_a_im, m_ssm_log_step, m_ssm_b_re, m_ssm_b_im, m_ssm_c_re, m_ssm_c_im, m_ssm_d, m_ssm_glu_a, m_ssm_glu_b, m_w_out, m_ffn2_norm, m_ffn2_w1, m_ffn2_w3, m_ffn2_w2, m_final_norm, v_meta_tokens, v_ffn1_norm, v_ffn1_w1, v_ffn1_w3, v_ffn1_w2, v_mix_norm, v_w_in, v_attn_sinks, v_ssm_a_re, v_ssm_a_im, v_ssm_log_step, v_ssm_b_re, v_ssm_b_im, v_ssm_c_re, v_ssm_c_im, v_ssm_d, v_ssm_glu_a, v_ssm_glu_b, v_w_out, v_ffn2_norm, v_ffn2_w1, v_ffn2_w3, v_ffn2_w2, v_final_norm):
    given = dict(locals())
    w = {n: given[n] for n in WEIGHTS}
    m = {n: given["m_" + n] for n in WEIGHTS}
    v = {n: given["v_" + n] for n in WEIGHTS}

    n_b, seq, _ = x.shape
    n_l = seq + N_META
    t_rows = n_b * n_l
    tm = _row_tile(n_l, 688)
    px, py, pc = _my_place()
    me = 4 * px + 2 * py + pc

    glu = jnp.stack([ssm_glu_a[0], ssm_glu_b[0]]).astype(BF16)
    ffn_names = ['ffn1_w1', 'ffn1_w3', 'ffn1_w2', 'ffn2_w1', 'ffn2_w3', 'ffn2_w2']

    def hidden_on_rows(n, t):
        return t[0] if n.endswith('w2') else t[0].T

    def hidden_on_rows_back(n, t):
        return t[None] if n.endswith('w2') else t.T[None]

    me_idx = jnp.reshape(me, (1,)).astype(jnp.int32)
    first_names, later_names = ffn_names[:3], ffn_names[3:]
    *first, metag = all_gather_list(
        [hidden_on_rows(n, w[n]).astype(BF16) for n in first_names] + [meta_tokens], meta_tokens, "ag_first")
    win_send, win_recv, win_shard, win_land, win_token = exchange_start(
        [w_in[0].astype(BF16)], first[0], True, "ag_w_in_start")
    later_shards = [hidden_on_rows(n, w[n]).astype(BF16) for n in later_names] + [glu, w_out[0].astype(BF16)]
    ag_send, ag_recv, later_shards, later_lands, ag_token = exchange_start(later_shards, win_token, True, "ag_later_start")
    full = {n: g.reshape(D_FF, D_MODEL) for n, g in zip(first_names, first)}
    meta_full = metag.transpose(1, 0, 2).reshape(N_META, D_MODEL)

    final_g = final_norm.reshape(1, D_MODEL)

    ar = ssm_a_re.reshape(1, N_STATES)
    ai = ssm_a_im.reshape(1, N_STATES)
    ls = jnp.repeat(ssm_log_step.reshape(SSM_GROUPS), SSM_STATE).reshape(1, N_STATES)
    br_t = ssm_b_re[0].transpose(2, 0, 1).reshape(SSM_GROUP, N_STATES)
    bi_t = ssm_b_im[0].transpose(2, 0, 1).reshape(SSM_GROUP, N_STATES)
    bbr, bbi, tabf, tabr = ssm_prepare(ar, ai, ls, br_t, bi_t, "ssm_prepare")
    bbr_g = bbr.reshape(SSM_GROUP, SSM_GROUPS, SSM_STATE).transpose(1, 0, 2)
    bbi_g = bbi.reshape(SSM_GROUP, SSM_GROUPS, SSM_STATE).transpose(1, 0, 2)
    groups_per_blk = SCAN_COLS // SSM_STATE
    half = ((jnp.arange(N_SCAN_BLK) % 2)[:, None] == jnp.arange(2)[None, :]).astype(F32)
    eye = jnp.eye(groups_per_blk, dtype=F32)

    def scan_blocks(re_g, im_g):
        def one(t):
            t = t.reshape(N_SCAN_BLK, groups_per_blk, SSM_GROUP, SSM_STATE)
            t = t[:, :, :, None, :] * eye[None, :, None, :, None]
            t = t.reshape(N_SCAN_BLK, LANES // 2, SCAN_COLS)
            return (t[:, None] * half[:, :, None, None]).reshape(N_SCAN_BLK, LANES, SCAN_COLS)
        return jnp.concatenate([one(re_g), one(im_g)], axis=-1).astype(BF16)

    b_comb = scan_blocks(bbr_g, bbi_g)
    c_comb_t = scan_blocks(ssm_c_re[0], -ssm_c_im[0])
    b_comb_t, c_comb = b_comb.transpose(0, 2, 1), c_comb_t.transpose(0, 2, 1)

    ffn1_w = (full['ffn1_w1'], full['ffn1_w3'], full['ffn1_w2'])
    h1, hn1, a1, b1, h0 = ffn_forward(x, ffn1_norm, *ffn1_w, ag_token, tm, "ffn1_fwd", meta=meta_full)
    win_shard, (wing,) = exchange_wait(win_send, win_recv, win_shard, win_land, h1, True, "ag_w_in_wait")
    wing = lax.dynamic_update_slice_in_dim(wing, win_shard[0][None], me, axis=0)
    hnm, proj = mix_forward(h1, mix_norm, wing, tm, "mix_fwd")
    proj3 = proj.reshape(n_b, n_l, IN_WIDTH)
    attn3 = attention_forward(proj3, attn_sinks, seq, "attn_fwd")
    attn = attn3.reshape(t_rows, D_MODEL)
    xs3, yraw3 = ssm_forward_scan(proj3, b_comb, tabf, c_comb, ssm_d, seq, "ssm_fwd")
    yraw = yraw3.reshape(t_rows, SSM_WIDTH)
    later_shards, later = exchange_wait(ag_send, ag_recv, later_shards, later_lands, yraw3, True, "ag_later_wait")
    later = [lax.dynamic_update_slice_in_dim(z, s[None], me, axis=0) for z, s in zip(later, later_shards)]
    for n, g in zip(later_names, later):
        full[n] = g.reshape(D_FF, D_MODEL)
    ffn2_w = (full['ffn2_w1'], full['ffn2_w3'], full['ffn2_w2'])
    glug, wog = later[len(later_names):]
    glu_a = glug[:, 0].transpose(1, 0, 2).reshape(SSM_WIDTH, D_MODEL)
    glu_b = glug[:, 1].transpose(1, 0, 2).reshape(SSM_WIDTH, D_MODEL)
    w_out_full = wog.reshape(D_MODEL, D_MODEL)
    h2 = merge_forward(h1, yraw, attn, proj, glu_a, glu_b, w_out_full, tm, "merge_fwd")
    h3, hn2, a2, b2 = ffn_forward(h2, ffn2_norm, *ffn2_w, ag_token, tm, "ffn2_fwd")
    dh3, loss_part, g_final = final_loss_backward(h3, loss_target, final_g, seq, tm, "loss_bwd")
    loss = lax.psum(loss_part[0, 0], ("x", "y", "c"))

    def blocked_ffn(d_w1t, d_w3t, d_w2):
        return tuple(t.reshape(N_DEV, FF_BLK, D_MODEL) for t in (d_w1t, d_w3t, d_w2))

    def blocked_cols(full_grad):
        r = full_grad.shape[0]
        return full_grad.reshape(r, N_DEV, full_grad.shape[1] // N_DEV).transpose(1, 0, 2).astype(BF16)

    early = {}

    def start_reduce(names, tag):
        srcs = [dw[n] for n in names]
        send, recv, srcs, lands, token = exchange_start(srcs, srcs[0], False, "rs_" + tag + "_start")
        early[tag] = (names, send, recv, srcs, lands)
        return token

    dw = {}
    da2, db2, dh3_half = ffn_backward_hidden(dh3, a2, b2, ffn2_w[2], g_final, tm, "ffn2_bwd_hid")
    dw['ffn2_w1'], dw['ffn2_w3'], dw['ffn2_w2'] = blocked_ffn(
        *ffn_backward_weights(hn2, dh3_half, a2, b2, da2, db2, n_l, FF_BWD_COLS, "ffn2_bwd_w"))
    token = start_reduce(later_names, "ffn2")
    dh2, g_ffn2_norm = ffn_backward_input(dh3, h2, ffn2_norm, da2, db2, ffn2_w[0], ffn2_w[1], token, tm, "ffn2_bwd_in")
    dattn, dyraw, dproj, *for_weights = merge_backward(dh2, yraw, attn, proj, glu_a, glu_b, w_out_full, token, tm,
                                                       "merge_bwd")
    d_wo, d_ga, d_gb = merge_backward_weights(*for_weights, tm, "merge_bwd_w")
    dw['ssm_glu_a'] = blocked_cols(d_ga)
    dw['ssm_glu_b'] = blocked_cols(d_gb)
    dw['w_out'] = d_wo.reshape(N_DEV, D_MODEL // N_DEV, D_MODEL).astype(BF16)
    token = start_reduce(['ssm_glu_a', 'ssm_glu_b', 'w_out'], "mix")
    dproj3 = dproj.reshape(n_b, n_l, IN_WIDTH)
    dproj3, dsink_p = attention_backward(proj3, dattn.reshape(n_b, n_l, D_MODEL), dproj3, attn_sinks, token, seq,
                                         "attn_bwd")
    dproj3, gs3, dlr_p, dli_p = ssm_backward_scan(
        dyraw.reshape(n_b, n_l, SSM_WIDTH), xs3, dproj3, c_comb_t, tabr, b_comb_t, ssm_d, seq, "ssm_bwd")
    dproj = dproj3.reshape(t_rows, IN_WIDTH)
    d_bd, d_cd, g_d = ssm_param_grads(proj, gs3.reshape(t_rows, 2 * N_STATES), xs3.reshape(t_rows, 2 * N_STATES),
                                      dyraw, n_l, "ssm_bwd_w")
    w_in_full = wing.transpose(1, 0, 2).reshape(D_MODEL, IN_WIDTH)
    dh1, g_mix_norm = mix_backward_act(dh2, h1, mix_norm, dproj, w_in_full, tm, "mix_bwd_act")
    dw['w_in'] = mix_backward_weights(hnm, dproj, n_l, "mix_bwd_w")
    token = start_reduce(['w_in'], "w_in")

    def group_blocks(part, channels_first):
        if channels_first:
            t = jnp.sum(part.reshape(N_SCAN_BLK, 2, LANES // 2, SCAN_COLS) * half[:, :, None, None], axis=1)
            t = t.reshape(N_SCAN_BLK, groups_per_blk, SSM_GROUP, groups_per_blk, SSM_STATE)
            t = jnp.sum(t * eye[None, :, None, :, None], axis=3)
            return t.reshape(SSM_GROUPS, SSM_GROUP, SSM_STATE)
        t = jnp.sum(part.reshape(N_SCAN_BLK, SCAN_COLS, 2, LANES // 2) * half[:, None, :, None], axis=2)
        t = t.reshape(N_SCAN_BLK, groups_per_blk, SSM_STATE, groups_per_blk, SSM_GROUP)
        t = jnp.sum(t * eye[None, :, None, :, None], axis=3)
        return t.reshape(SSM_GROUPS, SSM_STATE, SSM_GROUP).transpose(0, 2, 1)

    dbbr = group_blocks(d_bd[:, :, :SCAN_COLS], True).transpose(1, 0, 2).reshape(SSM_GROUP, N_STATES)
    dbbi = group_blocks(d_bd[:, :, SCAN_COLS:], True).transpose(1, 0, 2).reshape(SSM_GROUP, N_STATES)
    g_c_re = group_blocks(d_cd[:, :SCAN_COLS, :], False)[None]
    g_c_im = -group_blocks(d_cd[:, SCAN_COLS:, :], False)[None]
    group_sum = (jnp.arange(N_STATES)[:, None] // SSM_STATE == jnp.arange(LANES)[None, :]).astype(F32)
    g_ar, g_ai, g_ls, g_br, g_bi = ssm_param_backward(
        ar, ai, ls, br_t, bi_t, dlr_p.reshape(n_b * SUBLANES, N_STATES), dli_p.reshape(n_b * SUBLANES, N_STATES),
        dbbr, dbbi, group_sum, "ssm_bwd_params")
    g_sinks = sum_leading(dsink_p, "sink_sum")[0:1, :N_KV_HEADS * Q_PER_KV]

    small = {
        'mix_norm': g_mix_norm, 'ffn2_norm': g_ffn2_norm, 'final_norm': g_final.reshape(D_MODEL),
        'attn_sinks': g_sinks, 'ssm_a_re': g_ar.reshape(1, SSM_GROUPS, SSM_STATE), 'ssm_a_im': g_ai.reshape(1, SSM_GROUPS, SSM_STATE),
        'ssm_log_step': g_ls[:, :SSM_GROUPS],
        'ssm_b_re': g_br.reshape(SSM_GROUP, SSM_GROUPS, SSM_STATE).transpose(1, 2, 0)[None],
        'ssm_b_im': g_bi.reshape(SSM_GROUP, SSM_GROUPS, SSM_STATE).transpose(1, 2, 0)[None],
        'ssm_c_re': g_c_re, 'ssm_c_im': g_c_im, 'ssm_d': g_d,
    }
    early_small = [n for n in REPLICATED if n in small]
    sg_send, sg_recv, sg_src, sg_land, token = exchange_start(
        [_pack([small[n] for n in early_small])], token, True, "ag_small_start")
    da1, db1, dh1_half = ffn_backward_hidden(dh1, a1, b1, ffn1_w[2], token, tm, "ffn1_bwd_hid")
    dw['ffn1_w1'], dw['ffn1_w3'], dw['ffn1_w2'] = blocked_ffn(
        *ffn_backward_weights(hn1, dh1_half, a1, b1, da1, db1, n_l, FF_BWD_COLS, "ffn1_bwd_w"))
    token = start_reduce(first_names, "ffn1")
    grad_x, meta_rows_grad, g_ffn1_norm = ffn_backward_input(
        dh1, h0, ffn1_norm, da1, db1, ffn1_w[0], ffn1_w[1], token, tm, "ffn1_bwd_in", examples=(n_b, seq))
    g_meta = sum_leading(meta_rows_grad, "meta_sum")

    grads, deltas, new_m, new_v = {}, {}, {}, {}

    def views(n):
        if n in ffn_names:
            return functools.partial(hidden_on_rows, n), functools.partial(hidden_on_rows_back, n)
        return (lambda t: t[0]), (lambda t: t[None])

    def finish_reduce(tag, previous):
        names, send, recv, srcs, lands = early[tag]
        srcs, lands = exchange_wait(send, recv, srcs, lands, previous, False, "rs_" + tag + "_wait")
        for n, g, land in zip(names, srcs, lands):
            two_d, back = views(n)
            out = adamw_exchanged(me_idx, g, land, two_d(w[n]), two_d(m[n]), two_d(v[n]), "adamw_" + n)
            grads[n], deltas[n], new_m[n], new_v[n] = (back(o) for o in out)
            previous = out[1]
        return previous

    sl_send, sl_recv, sl_src, sl_land, previous = exchange_start(
        [_pack([g_ffn1_norm, g_meta])], g_meta, True, "ag_small_late_start")
    for tag in ("ffn2", "mix", "w_in"):
        previous = finish_reduce(tag, previous)

    zeros_meta = jnp.zeros((N_META, D_MODEL), F32)
    sl_src, (late_parts,) = exchange_wait(sl_send, sl_recv, sl_src, sl_land, previous, True, "ag_small_late_wait")
    late_parts = lax.dynamic_update_slice_in_dim(late_parts, sl_src[0][None], me, axis=0)
    sg_src, (early_parts,) = exchange_wait(sg_send, sg_recv, sg_src, sg_land, late_parts, True, "ag_small_wait")
    early_parts = lax.dynamic_update_slice_in_dim(early_parts, sg_src[0][None], me, axis=0)

    def small_update(parts, names, extra, tag):
        pack_of = lambda d: _pack([d[n] for n in names] + extra)
        packed = adamw_small(parts, pack_of(w), pack_of(m), pack_of(v), "adamw_small_" + tag)
        unpacked = [_unpack(p, [w[n].shape for n in names] + [e.shape for e in extra]) for p in packed]
        for k, n in enumerate(names):
            grads[n], deltas[n], new_m[n], new_v[n] = (u[k] for u in unpacked)
        return packed, unpacked

    small_update(early_parts, early_small, [], "early")
    packed_out, unpacked = small_update(late_parts, ['ffn1_norm'], [zeros_meta], "late")
    g_meta_full = unpacked[0][-1]
    grads['meta_tokens'] = lax.dynamic_index_in_dim(
        g_meta_full.reshape(N_META, N_DEV, D_MODEL // N_DEV), me, axis=1, keepdims=False)
    deltas['meta_tokens'], new_m['meta_tokens'], new_v['meta_tokens'] = adamw_plain(
        grads['meta_tokens'], w['meta_tokens'], m['meta_tokens'], v['meta_tokens'], "adamw_meta")

    finish_reduce("ffn1", packed_out[0])

    return (loss, grad_x, *[grads[n] for n in WEIGHTS], *[deltas[n] for n in WEIGHTS],
            *[new_m[n] for n in WEIGHTS], *[new_v[n] for n in WEIGHTS])
```

```python
import functools

import jax
import jax.numpy as jnp
from jax import lax
from jax.experimental import pallas as pl
from jax.experimental.pallas import tpu as pltpu

F32 = jnp.float32
BF16 = jnp.bfloat16
MESH = pl.DeviceIdType.MESH

N_DEV = 8
D_MODEL = 1024
N_META = 16
HEAD_DIM = 64
N_KV_HEADS = 4
Q_PER_KV = 4
BLOCK = 128
KV_WIDTH = N_KV_HEADS * HEAD_DIM
SSM_GROUP = 16
SSM_WIDTH = 512
SSM_GROUPS = 32
SSM_STATE = 64
N_STATES = SSM_GROUPS * SSM_STATE
D_FF = 2816
FF_BLK = D_FF // N_DEV
IN_WIDTH = 4096
IN_BLK = IN_WIDTH // N_DEV
NORM_EPS = 1e-6
NEG_INF = -1e30
SCAN_COLS = 256
N_SCAN_BLK = N_STATES // SCAN_COLS
SUBLANES = 8
LANES = 128
MXU_WIDTH = 256
FF_BWD_COLS = MXU_WIDTH

ADAM_LR = 0.001
ADAM_B1 = 0.9
ADAM_B2 = 0.999
ADAM_EPS = 1e-08
ADAM_WD = 0.01
ADAM_STEP = 10

VMEM_BIG = 56 * 1024 * 1024


def _cp(sem=None, vmem=None):
    kw = {}
    if sem is not None:
        kw["dimension_semantics"] = sem
    if vmem is not None:
        kw["vmem_limit_bytes"] = vmem
    return pltpu.CompilerParams(**kw)


def _pcall(body, **kw):
    return pl.pallas_call(body, **kw)


def _dot(a, b):
    return jnp.dot(a, b, preferred_element_type=F32)


def _dot_nt(a, b):
    return lax.dot_general(a, b, (((1,), (1,)), ((), ())), preferred_element_type=F32)


def _dot_tn(a, b):
    return lax.dot_general(a, b, (((0,), (0,)), ((), ())), preferred_element_type=F32)


def _sigmoid(x):
    return 1.0 / (1.0 + jnp.exp(-x))


def _row_tile(rows, cap):
    best = None
    for t in range(16, min(rows, cap) + 1, 16):
        if rows % t == 0:
            best = t
    assert best is not None, rows
    return best


def _my_place():
    return lax.axis_index("x"), lax.axis_index("y"), lax.axis_index("c")


def all_gather_list(shards, after, name):
    n = len(shards)

    def body(*refs):
        ins, outs = refs[:n], refs[n + 1:2 * n + 1]
        send_sems, recv_sems, local_sems = refs[2 * n + 1:]
        x, y, c = _my_place()
        me, sibling = (x, y, c), (x, y, 1 - c)
        chips = [(1 - x, y), (x, 1 - y), (1 - x, 1 - y)]

        def blk(a, px, py, pc):
            return outs[a].at[4 * px + 2 * py + pc]

        def copy(a, k, block, to, src=None):
            return pltpu.make_async_remote_copy(
                src_ref=blk(a, *block) if src is None else src, dst_ref=blk(a, *block),
                send_sem=send_sems.at[a * 7 + k], recv_sem=recv_sems.at[a * 7 + k],
                device_id=to, device_id_type=MESH)

        mine = [pltpu.make_async_copy(ins[a], blk(a, *me), local_sems.at[a]) for a in range(n)]
        for cp in mine:
            cp.start()
        first = []
        for a in range(n):
            first.append(copy(a, 0, me, sibling, src=ins[a]))
            first += [copy(a, 1 + j, me, (*chip, c), src=ins[a]) for j, chip in enumerate(chips)]
        for cp in first:
            cp.start()
        passed = []
        for j, chip in enumerate(chips):
            for a in range(n):
                copy(a, 1 + j, (*chip, c), me).wait_recv()
                cp = copy(a, 4 + j, (*chip, c), sibling)
                cp.start()
                passed.append(cp)
        for a in range(n):
            copy(a, 0, sibling, me).wait_recv()
            for j, chip in enumerate(chips):
                copy(a, 4 + j, (*chip, 1 - c), me).wait_recv()
        for cp in first + passed:
            cp.wait_send()
        for cp in mine:
            cp.wait()

    any_spec = pl.BlockSpec(memory_space=pl.ANY)
    return _pcall(
        body, name=name,
        out_shape=[jax.ShapeDtypeStruct((N_DEV,) + s.shape, s.dtype) for s in shards],
        in_specs=[any_spec] * (n + 1), out_specs=[any_spec] * n,
        scratch_shapes=[pltpu.SemaphoreType.DMA((7 * n,)), pltpu.SemaphoreType.DMA((7 * n,)),
                        pltpu.SemaphoreType.DMA((n,))],
    )(*shards, after)


HBM_SPEC = pl.BlockSpec(memory_space=pltpu.HBM)
SEM_SPEC = pl.BlockSpec(memory_space=pltpu.SEMAPHORE)
N_PEERS = N_DEV - 1


def _related(k):
    x, y, c = _my_place()
    px = 1 - x if k & 4 else x
    py = 1 - y if k & 2 else y
    pc = 1 - c if k & 1 else c
    return (px, py, pc), 4 * px + 2 * py + pc


def _exchange_copies(srcs, lands, send_sems, recv_sems, gather):
    x, y, c = _my_place()
    me = 4 * x + 2 * y + c
    copies = []
    for a, (src, land) in enumerate(zip(srcs, lands)):
        for k in range(1, N_DEV):
            peer, d = _related(k)
            copies.append(pltpu.make_async_remote_copy(
                src_ref=src if gather else src.at[d], dst_ref=land.at[me] if gather else land.at[k],
                send_sem=send_sems.at[a * N_PEERS + k - 1], recv_sem=recv_sems.at[a * N_PEERS + k - 1],
                device_id=peer, device_id_type=MESH))
    return copies


def exchange_start(srcs, after, gather, name):
    n = len(srcs)
    land_shapes = [((N_DEV,) + s.shape) if gather else s.shape for s in srcs]

    def body(*refs):
        send_sems, recv_sems = refs[2 * n + 1], refs[2 * n + 2]
        for cp in _exchange_copies(refs[:n], refs[n:2 * n], send_sems, recv_sems, gather):
            cp.start()
        token = refs[-1]
        token[...] = jnp.zeros_like(token)

    sems = pltpu.SemaphoreType.DMA((n * N_PEERS,))
    lands = [pltpu.with_memory_space_constraint(lax.empty(shape, s.dtype), pltpu.HBM) for shape, s in zip(land_shapes, srcs)]
    out = _pcall(
        body, name=name,
        out_shape=(sems, sems, *[pltpu.HBM(s.shape, s.dtype) for s in srcs],
                   *[pltpu.HBM(shape, s.dtype) for shape, s in zip(land_shapes, srcs)],
                   jax.ShapeDtypeStruct((SUBLANES, LANES), F32)),
        in_specs=[HBM_SPEC] * (2 * n) + [pl.BlockSpec(memory_space=pl.ANY)],
        out_specs=(SEM_SPEC, SEM_SPEC, *[HBM_SPEC] * (2 * n), pl.BlockSpec(memory_space=pltpu.VMEM)),
        input_output_aliases={i: 2 + i for i in range(2 * n)},
        compiler_params=pltpu.CompilerParams(has_side_effects=pltpu.SideEffectType.DATAFLOW_SIDE_EFFECTING),
    )(*[pltpu.with_memory_space_constraint(s, pltpu.HBM) for s in srcs], *lands, after)
    return out[0], out[1], list(out[2:2 + n]), list(out[2 + n:2 + 2 * n]), out[-1]


def exchange_wait(send_sems, recv_sems, srcs, lands, after, gather, name):
    n = len(srcs)

    def body(*refs):
        for cp in _exchange_copies(refs[:n], refs[n:2 * n], refs[2 * n], refs[2 * n + 1], gather):
            cp.wait_send()
            cp.wait_recv()

    out = _pcall(
        body, name=name,
        out_shape=(*[pltpu.HBM(s.shape, s.dtype) for s in srcs], *[pltpu.HBM(z.shape, z.dtype) for z in lands]),
        in_specs=[HBM_SPEC] * (2 * n) + [SEM_SPEC, SEM_SPEC, pl.BlockSpec(memory_space=pl.ANY)],
        out_specs=tuple([HBM_SPEC] * (2 * n)),
        input_output_aliases={i: i for i in range(2 * n)},
        compiler_params=pltpu.CompilerParams(has_side_effects=pltpu.SideEffectType.DATAFLOW_SIDE_EFFECTING),
    )(*srcs, *lands, send_sems, recv_sems, after)
    return list(out[:n]), list(out[n:])


def adamw_exchanged(me, g, land, w, m, v, name):
    rows, cols = w.shape
    tr = _row_tile(rows, 256)

    def body(me_ref, g_ref, land_ref, w_ref, m_ref, v_ref, go_ref, d_ref, mo_ref, vo_ref):
        grad = g_ref[...].astype(F32)
        for k in range(1, N_DEV):
            grad = grad + land_ref[k].astype(F32)
        delta, m_new, v_new = _adam_math(w_ref[...], grad, m_ref[...], v_ref[...])
        go_ref[...] = grad
        d_ref[...] = delta
        mo_ref[...] = m_new
        vo_ref[...] = v_new

    tile = pl.BlockSpec((tr, cols), lambda r, ix: (r, 0))
    out = jax.ShapeDtypeStruct((rows, cols), F32)
    return _pcall(
        body, name=name, out_shape=[out] * 4,
        grid_spec=pltpu.PrefetchScalarGridSpec(
            num_scalar_prefetch=1, grid=(rows // tr,),
            in_specs=[pl.BlockSpec((None, tr, cols), lambda r, ix: (ix[0], r, 0)),
                      pl.BlockSpec((N_DEV, tr, cols), lambda r, ix: (0, r, 0)), tile, tile, tile],
            out_specs=[tile] * 4),
        compiler_params=_cp(("arbitrary",)),
    )(me, g, land, w, m, v)


def _adam_math(w, g, m, v):
    m = ADAM_B1 * m + (1.0 - ADAM_B1) * g
    v = ADAM_B2 * v + (1.0 - ADAM_B2) * (g * g)
    m_hat = m / (1.0 - ADAM_B1 ** ADAM_STEP)
    v_hat = v / (1.0 - ADAM_B2 ** ADAM_STEP)
    delta = -ADAM_LR * (m_hat / (jnp.sqrt(v_hat) + ADAM_EPS) + ADAM_WD * w)
    return delta, m, v


def adamw_small(parts, w, m, v, name):
    _, rows, cols = parts.shape

    def body(p_ref, w_ref, m_ref, v_ref, go_ref, d_ref, mo_ref, vo_ref):
        grad = p_ref[0]
        for k in range(1, N_DEV):
            grad = grad + p_ref[k]
        delta, m_new, v_new = _adam_math(w_ref[...], grad, m_ref[...], v_ref[...])
        go_ref[...] = grad
        d_ref[...] = delta
        mo_ref[...] = m_new
        vo_ref[...] = v_new

    out = jax.ShapeDtypeStruct((rows, cols), F32)
    return _pcall(body, name=name, out_shape=[out] * 4, compiler_params=_cp(vmem=VMEM_BIG))(parts, w, m, v)


def adamw_plain(g, w, m, v, name):
    def body(g_ref, w_ref, m_ref, v_ref, d_ref, mo_ref, vo_ref):
        delta, m_new, v_new = _adam_math(w_ref[...], g_ref[...], m_ref[...], v_ref[...])
        d_ref[...] = delta
        mo_ref[...] = m_new
        vo_ref[...] = v_new

    out = jax.ShapeDtypeStruct(w.shape, F32)
    return _pcall(body, name=name, out_shape=[out] * 3)(g, w, m, v)


def _rms_fwd(x, g):
    r = lax.rsqrt(jnp.mean(x * x, axis=-1, keepdims=True) + NORM_EPS)
    return x * r * g


def _rms_bwd(x, g, dy):
    r = lax.rsqrt(jnp.mean(x * x, axis=-1, keepdims=True) + NORM_EPS)
    xh = x * r
    t = dy * g
    dx = r * (t - xh * jnp.mean(t * xh, axis=-1, keepdims=True))
    return dx, jnp.sum(dy * xh, axis=0, keepdims=True)


def _accumulate(ref, val, first):
    @pl.when(first)
    def _():
        ref[...] = val

    @pl.when(jnp.logical_not(first))
    def _():
        ref[...] += val


def _col_chunks(width):
    return [(c0, min(MXU_WIDTH, width - c0)) for c0 in range(0, width, MXU_WIDTH)]


ANY_SPEC = pl.BlockSpec(memory_space=pl.ANY)


def ffn_forward(h, norm, w1, w3, w2, after, tm, name, meta=None):
    if meta is None:
        t_rows = h.shape[0]
        h_spec = pl.BlockSpec((tm, D_MODEL), lambda i: (i, 0))
    else:
        tiles = (h.shape[1] + N_META) // tm
        t_rows = h.shape[0] * tiles * tm
        h_spec = pl.BlockSpec((None, tm, D_MODEL), lambda i: (i // tiles, i % tiles, 0))

    def body(h_ref, g_ref, w1_ref, w3_ref, w2_ref, _, *rest):
        if meta is None:
            out_ref, hn_ref, a_ref, b_ref, hid_ref = rest
            h_in = h_ref[...]
        else:
            meta_ref, out_ref, hn_ref, a_ref, b_ref, h0_ref, hid_ref = rest
            h_in = h_ref[...]
            with_meta = jnp.concatenate([h_in[:tm - N_META], meta_ref[...]], axis=0)
            h_in = jnp.where(pl.program_id(0) % tiles == tiles - 1, with_meta, h_in)
            h0_ref[...] = h_in
        hn = _rms_fwd(h_in, g_ref[...]).astype(BF16)
        hn_ref[...] = hn
        for c0, cw in _col_chunks(D_FF):
            a = _dot_nt(hn, w1_ref[c0:c0 + cw, :])
            b = _dot_nt(hn, w3_ref[c0:c0 + cw, :])
            a_ref[:, c0:c0 + cw] = a.astype(BF16)
            b_ref[:, c0:c0 + cw] = b.astype(BF16)
            hid_ref[:, c0:c0 + cw] = (a * _sigmoid(a) * b).astype(BF16)
        out_ref[...] = h_in + 0.5 * _dot(hid_ref[...], w2_ref[...])

    row = pl.BlockSpec((tm, D_MODEL), lambda i: (i, 0))
    hid_blk = pl.BlockSpec((tm, D_FF), lambda i: (i, 0))
    weight = _resident((D_FF, D_MODEL))
    wide = jax.ShapeDtypeStruct((t_rows, D_MODEL), F32)
    extra_in = [] if meta is None else [meta]
    return _pcall(
        body, name=name, grid=(t_rows // tm,),
        in_specs=[h_spec, pl.BlockSpec((1, D_MODEL), lambda i: (0, 0)), weight, weight, weight, ANY_SPEC]
        + [pl.BlockSpec((N_META, D_MODEL), lambda i: (0, 0))] * len(extra_in),
        out_specs=[row, row, hid_blk, hid_blk] + [row] * len(extra_in),
        out_shape=[wide, jax.ShapeDtypeStruct((t_rows, D_MODEL), BF16),
                   jax.ShapeDtypeStruct((t_rows, D_FF), BF16), jax.ShapeDtypeStruct((t_rows, D_FF), BF16)]
        + [wide] * len(extra_in),
        scratch_shapes=[pltpu.VMEM((tm, D_FF), BF16)],
        compiler_params=_cp(("arbitrary",), VMEM_BIG),
    )(h, norm, w1, w3, w2, after, *extra_in)


def _resident(shape):
    return pl.BlockSpec(shape, lambda *_: (0,) * len(shape), pipeline_mode=pl.Buffered(1))


def ffn_backward_hidden(dh, a, b, w2, after, tm, name):
    t_rows = dh.shape[0]

    def body(dh_ref, a_ref, b_ref, w2_ref, _, da_ref, db_ref, dhb_ref):
        dhb = (0.5 * dh_ref[...]).astype(BF16)
        dhb_ref[...] = dhb
        for c0, cw in _col_chunks(D_FF):
            dhid = _dot_nt(dhb, w2_ref[c0:c0 + cw, :])
            av = a_ref[:, c0:c0 + cw].astype(F32)
            bv = b_ref[:, c0:c0 + cw].astype(F32)
            s = _sigmoid(av)
            da_ref[:, c0:c0 + cw] = (dhid * bv * (s * (1.0 + av * (1.0 - s)))).astype(BF16)
            db_ref[:, c0:c0 + cw] = (dhid * (av * s)).astype(BF16)

    hid = pl.BlockSpec((tm, D_FF), lambda i: (i, 0))
    row = pl.BlockSpec((tm, D_MODEL), lambda i: (i, 0))
    return _pcall(
        body, name=name, grid=(t_rows // tm,),
        in_specs=[row, hid, hid, _resident((D_FF, D_MODEL)), ANY_SPEC],
        out_specs=[hid, hid, row],
        out_shape=[jax.ShapeDtypeStruct((t_rows, D_FF), BF16), jax.ShapeDtypeStruct((t_rows, D_FF), BF16),
                   jax.ShapeDtypeStruct((t_rows, D_MODEL), BF16)],
        compiler_params=_cp(("arbitrary",), VMEM_BIG),
    )(dh, a, b, w2, after)


def ffn_backward_input(dh, h, norm, da, db, w1, w3, after, tm, name, examples=None):
    t_rows = h.shape[0]

    def body(dh_ref, h_ref, g_ref, da_ref, db_ref, w1_ref, w3_ref, _, dhin_ref, *rest):
        dg_ref = rest[-1]
        dhn = _dot(da_ref[...], w1_ref[...]) + _dot(db_ref[...], w3_ref[...])
        dx, dg = _rms_bwd(h_ref[...], g_ref[...], dhn)
        dhin = dh_ref[...] + dx
        dhin_ref[...] = dhin
        _accumulate(dg_ref, dg, pl.program_id(0) == 0)
        if examples is not None:
            @pl.when(pl.program_id(0) % tiles == tiles - 1)
            def _():
                rest[0][...] = dhin[tm - N_META:, :]

    row = pl.BlockSpec((tm, D_MODEL), lambda i: (i, 0))
    vec = pl.BlockSpec((1, D_MODEL), lambda i: (0, 0))
    hid = pl.BlockSpec((tm, D_FF), lambda i: (i, 0))
    if examples is None:
        out_specs = [row, vec]
        out_shape = [jax.ShapeDtypeStruct((t_rows, D_MODEL), F32), jax.ShapeDtypeStruct((1, D_MODEL), F32)]
    else:
        n_b, seq = examples
        tiles = (seq + N_META) // tm
        out_specs = [pl.BlockSpec((None, tm, D_MODEL), lambda i: (i // tiles, i % tiles, 0)),
                     pl.BlockSpec((None, N_META, D_MODEL), lambda i: (i // tiles, 0, 0)), vec]
        out_shape = [jax.ShapeDtypeStruct((n_b, seq, D_MODEL), F32), jax.ShapeDtypeStruct((n_b, N_META, D_MODEL), F32),
                     jax.ShapeDtypeStruct((1, D_MODEL), F32)]
    return _pcall(
        body, name=name, grid=(t_rows // tm,),
        in_specs=[row, row, vec, hid, hid, _resident((D_FF, D_MODEL)), _resident((D_FF, D_MODEL)), ANY_SPEC],
        out_specs=out_specs, out_shape=out_shape,
        compiler_params=_cp(("arbitrary",), VMEM_BIG),
    )(dh, h, norm, da, db, w1, w3, after)


def _weight_gradient_call(body, wide, hidden, after, n_out, tm, tn, name):
    ni = wide.shape[0] // tm
    row = pl.BlockSpec((tm, D_MODEL), lambda j, i: (i, 0))
    hid_blk = pl.BlockSpec((tm, tn), lambda j, i: (i, j))
    w_row = pl.BlockSpec((tn, D_MODEL), lambda j, i: (j, 0))
    out = jax.ShapeDtypeStruct((D_FF, D_MODEL), BF16)
    return _pcall(
        body, name=name, grid=(D_FF // tn, ni),
        in_specs=[row] + [hid_blk] * len(hidden) + [ANY_SPEC],
        out_specs=[w_row] * n_out, out_shape=[out] * n_out,
        scratch_shapes=[pltpu.VMEM((D_MODEL, tn), F32)] * n_out,
        compiler_params=_cp(("arbitrary", "arbitrary"), VMEM_BIG),
    )(wide, *hidden, after)


def ffn_backward_weights_in(hn, da, db, tm, tn, name):
    ni = hn.shape[0] // tm
    kc = _row_tile(tm, 688)

    def body(hn_ref, da_ref, db_ref, _, dw1_ref, dw3_ref, acc1, acc3):
        i = pl.program_id(1)
        parts = None
        for r0 in range(0, tm, kc):
            rows = slice(r0, r0 + kc)
            hn_v = hn_ref[rows, :]
            new = (_dot_tn(hn_v, da_ref[rows, :]), _dot_tn(hn_v, db_ref[rows, :]))
            parts = new if parts is None else tuple(p + q for p, q in zip(parts, new))
        _accumulate(acc1, parts[0], i == 0)
        _accumulate(acc3, parts[1], i == 0)

        @pl.when(i == ni - 1)
        def _():
            dw1_ref[...] = acc1[...].T.astype(BF16)
            dw3_ref[...] = acc3[...].T.astype(BF16)

    return _weight_gradient_call(body, hn, [da, db], da, 2, tm, tn, name)


def ffn_backward_weights_out(dh, a, b, after, tm, tn, name):
    ni = dh.shape[0] // tm
    kc = _row_tile(tm, 688)

    def body(dh_ref, a_ref, b_ref, _, dw2_ref, acc2):
        i = pl.program_id(1)
        part = None
        for r0 in range(0, tm, kc):
            rows = slice(r0, r0 + kc)
            av = a_ref[rows, :].astype(F32)
            hid = (av * _sigmoid(av) * b_ref[rows, :].astype(F32)).astype(BF16)
            new = _dot_tn(dh_ref[rows, :], hid)
            part = new if part is None else part + new
        _accumulate(acc2, part, i == 0)

        @pl.when(i == ni - 1)
        def _():
            dw2_ref[...] = acc2[...].T.astype(BF16)

    return _weight_gradient_call(body, dh, [a, b], after, 1, tm, tn, name)[0]


def mix_forward(h, norm, wing, tm, name):
    t_rows = h.shape[0]

    def body(h_ref, g_ref, w_ref, hn_ref, p_ref):
        hn = _rms_fwd(h_ref[...], g_ref[...]).astype(BF16)
        hn_ref[...] = hn
        for j in range(N_DEV):
            p_ref[:, j * IN_BLK:(j + 1) * IN_BLK] = _dot(hn, w_ref[j]).astype(BF16)

    row = pl.BlockSpec((tm, D_MODEL), lambda i: (i, 0))
    return _pcall(
        body, name=name, grid=(t_rows // tm,),
        in_specs=[row, pl.BlockSpec((1, D_MODEL), lambda i: (0, 0)),
                  pl.BlockSpec((N_DEV, D_MODEL, IN_BLK), lambda i: (0, 0, 0))],
        out_specs=[row, pl.BlockSpec((tm, IN_WIDTH), lambda i: (i, 0))],
        out_shape=[jax.ShapeDtypeStruct((t_rows, D_MODEL), BF16), jax.ShapeDtypeStruct((t_rows, IN_WIDTH), BF16)],
        compiler_params=_cp(("arbitrary",), VMEM_BIG),
    )(h, norm, wing)


def mix_backward_act(dh, h, norm, dproj, w_in_full, tm, name):
    t_rows = h.shape[0]

    def body(dh_ref, h_ref, g_ref, dp_ref, w_ref, dhin_ref, dg_ref):
        dx, dg = _rms_bwd(h_ref[...], g_ref[...], _dot_nt(dp_ref[...], w_ref[...]))
        dhin_ref[...] = dh_ref[...] + dx
        _accumulate(dg_ref, dg, pl.program_id(0) == 0)

    row = pl.BlockSpec((tm, D_MODEL), lambda i: (i, 0))
    vec = pl.BlockSpec((1, D_MODEL), lambda i: (0, 0))
    return _pcall(
        body, name=name, grid=(t_rows // tm,),
        in_specs=[row, row, vec, pl.BlockSpec((tm, IN_WIDTH), lambda i: (i, 0)), _resident((D_MODEL, IN_WIDTH))],
        out_specs=[row, vec],
        out_shape=[jax.ShapeDtypeStruct((t_rows, D_MODEL), F32), jax.ShapeDtypeStruct((1, D_MODEL), F32)],
        compiler_params=_cp(("arbitrary",), VMEM_BIG),
    )(dh, h, norm, dproj, w_in_full)


def mix_backward_weights(hn, dproj, tm, name):
    t_rows = hn.shape[0]
    ni = t_rows // tm
    per_step = 2

    kc = _row_tile(tm, 688)

    def body(hn_ref, dp_ref, dw_ref, acc):
        i = pl.program_id(1)
        part = functools.reduce(lambda u, w: u + w, [_dot_tn(hn_ref[r0:r0 + kc, :], dp_ref[r0:r0 + kc, :])
                                                    for r0 in range(0, tm, kc)])
        _accumulate(acc, part, i == 0)

        @pl.when(i == ni - 1)
        def _():
            for k in range(per_step):
                dw_ref[k] = acc[:, k * IN_BLK:(k + 1) * IN_BLK].astype(BF16)

    return _pcall(
        body, name=name, grid=(N_DEV // per_step, ni),
        in_specs=[pl.BlockSpec((tm, D_MODEL), lambda j, i: (i, 0)),
                  pl.BlockSpec((tm, per_step * IN_BLK), lambda j, i: (i, j))],
        out_specs=pl.BlockSpec((per_step, D_MODEL, IN_BLK), lambda j, i: (j, 0, 0)),
        out_shape=jax.ShapeDtypeStruct((N_DEV, D_MODEL, IN_BLK), BF16),
        scratch_shapes=[pltpu.VMEM((D_MODEL, per_step * IN_BLK), F32)],
        compiler_params=_cp(("arbitrary", "arbitrary"), VMEM_BIG),
    )(hn, dproj)


GELU_C = 0.7978845608028654
GELU_K = 0.044715


def _gelu(x):
    return 0.5 * x * (1.0 + jnp.tanh(GELU_C * (x + GELU_K * (x * x * x))))


def _gelu_and_grad(x):
    th = jnp.tanh(GELU_C * (x + GELU_K * (x * x * x)))
    val = 0.5 * x * (1.0 + th)
    grad = 0.5 * (1.0 + th) + 0.5 * x * (1.0 - th * th) * (GELU_C * (1.0 + 3.0 * GELU_K * (x * x)))
    return val, grad


def merge_forward(h, yraw, attn, proj, glu_a, glu_b, w_out, tm, name):
    t_rows = h.shape[0]

    def body(h_ref, y_ref, at_ref, gate_ref, a_ref, b_ref, wo_ref, out_ref):
        y = _gelu(y_ref[...]).astype(BF16)
        ssm = _dot(y, a_ref[...]) * _sigmoid(_dot(y, b_ref[...]))
        ga = gate_ref[:, :D_MODEL].astype(F32)
        gs = gate_ref[:, D_MODEL:].astype(F32)
        merged = _sigmoid(ga) * at_ref[...].astype(F32) + _sigmoid(gs) * ssm
        out_ref[...] = h_ref[...] + _dot(merged.astype(BF16), wo_ref[...])

    row = pl.BlockSpec((tm, D_MODEL), lambda i: (i, 0))
    glu = pl.BlockSpec((SSM_WIDTH, D_MODEL), lambda i: (0, 0))
    return _pcall(
        body, name=name, grid=(t_rows // tm,),
        in_specs=[row, pl.BlockSpec((tm, SSM_WIDTH), lambda i: (i, 0)), row,
                  pl.BlockSpec((tm, 2 * D_MODEL), lambda i: (i, 1)), glu, glu,
                  pl.BlockSpec((D_MODEL, D_MODEL), lambda i: (0, 0))],
        out_specs=row, out_shape=jax.ShapeDtypeStruct((t_rows, D_MODEL), F32),
        compiler_params=_cp(("arbitrary",), VMEM_BIG),
    )(h, yraw, attn, proj, glu_a, glu_b, w_out)


def merge_backward(dh, yraw, attn, proj, glu_a, glu_b, w_out, after, tm, name):
    t_rows = dh.shape[0]

    def body(dh_ref, y_ref, at_ref, gate_ref, a_ref, b_ref, wo_ref, _,
             dat_ref, dy_ref, dgate_ref, d16_ref, mg_ref, y16_ref, dya_ref, dyb_ref):
        d16 = dh_ref[...].astype(BF16)
        d16_ref[...] = d16
        gel, dgel = _gelu_and_grad(y_ref[...].astype(F32))
        y16 = gel.astype(BF16)
        y16_ref[...] = y16
        dy = None
        for c0, cw in _col_chunks(D_MODEL):
            cols = slice(c0, c0 + cw)
            gcols = slice(D_MODEL + c0, D_MODEL + c0 + cw)
            dmerged = _dot_nt(d16, wo_ref[cols, :])
            ya = _dot(y16, a_ref[:, cols])
            sb = _sigmoid(_dot(y16, b_ref[:, cols]))
            ssm = ya * sb
            sa = _sigmoid(gate_ref[:, cols].astype(F32))
            ss = _sigmoid(gate_ref[:, gcols].astype(F32))
            attn_v = at_ref[:, cols].astype(F32)
            mg_ref[:, cols] = (sa * attn_v + ss * ssm).astype(BF16)
            dat_ref[:, cols] = (dmerged * sa).astype(BF16)
            dgate_ref[:, cols] = (dmerged * attn_v * sa * (1.0 - sa)).astype(BF16)
            dgate_ref[:, gcols] = (dmerged * ssm * ss * (1.0 - ss)).astype(BF16)
            dssm = dmerged * ss
            dya = (dssm * sb).astype(BF16)
            dyb = (dssm * ya * sb * (1.0 - sb)).astype(BF16)
            dya_ref[:, cols] = dya
            dyb_ref[:, cols] = dyb
            part = _dot_nt(dya, a_ref[:, cols]) + _dot_nt(dyb, b_ref[:, cols])
            dy = part if dy is None else dy + part
        dy_ref[...] = (dy * dgel).astype(BF16)

    row = pl.BlockSpec((tm, D_MODEL), lambda i: (i, 0))
    ssm_row = pl.BlockSpec((tm, SSM_WIDTH), lambda i: (i, 0))
    gates = pl.BlockSpec((tm, 2 * D_MODEL), lambda i: (i, 1))
    wide = jax.ShapeDtypeStruct((t_rows, D_MODEL), BF16)
    narrow = jax.ShapeDtypeStruct((t_rows, SSM_WIDTH), BF16)
    return _pcall(
        body, name=name, grid=(t_rows // tm,),
        in_specs=[row, ssm_row, row, gates, _resident((SSM_WIDTH, D_MODEL)), _resident((SSM_WIDTH, D_MODEL)),
                  _resident((D_MODEL, D_MODEL)), ANY_SPEC],
        out_specs=[row, ssm_row, gates, row, row, ssm_row, row, row],
        out_shape=[wide, narrow, jax.ShapeDtypeStruct((t_rows, IN_WIDTH), BF16), wide, wide, narrow, wide, wide],
        compiler_params=_cp(("arbitrary",), VMEM_BIG),
    )(dh, yraw, attn, proj, glu_a, glu_b, w_out, after)


def merge_backward_weights(d16, merged, y16, dya, dyb, tm, name):
    t_rows = d16.shape[0]

    def body(d_ref, mg_ref, y_ref, dya_ref, dyb_ref, dwo_ref, da_ref, db_ref):
        first = pl.program_id(0) == 0
        y16 = y_ref[...]
        _accumulate(dwo_ref, _dot_tn(mg_ref[...], d_ref[...]), first)
        _accumulate(da_ref, _dot_tn(y16, dya_ref[...]), first)
        _accumulate(db_ref, _dot_tn(y16, dyb_ref[...]), first)

    row = pl.BlockSpec((tm, D_MODEL), lambda i: (i, 0))
    ssm_row = pl.BlockSpec((tm, SSM_WIDTH), lambda i: (i, 0))
    glu = pl.BlockSpec((SSM_WIDTH, D_MODEL), lambda i: (0, 0))
    wo = pl.BlockSpec((D_MODEL, D_MODEL), lambda i: (0, 0))
    return _pcall(
        body, name=name, grid=(t_rows // tm,),
        in_specs=[row, row, ssm_row, row, row], out_specs=[wo, glu, glu],
        out_shape=[jax.ShapeDtypeStruct((D_MODEL, D_MODEL), F32), jax.ShapeDtypeStruct((SSM_WIDTH, D_MODEL), F32),
                   jax.ShapeDtypeStruct((SSM_WIDTH, D_MODEL), F32)],
        compiler_params=_cp(("arbitrary",), VMEM_BIG),
    )(d16, merged, y16, dya, dyb)


def final_loss_backward(h, target, norm, seq, tm, name):
    t_rows = h.shape[0]
    tiles_per_example = (seq + N_META) // tm

    def body(h_ref, t_ref, g_ref, dh_ref, loss_ref, dg_ref):
        i = pl.program_id(0)
        x = h_ref[...]
        g = g_ref[...]
        r = lax.rsqrt(jnp.mean(x * x, axis=-1, keepdims=True) + NORM_EPS)
        xh = x * r
        pos = lax.broadcasted_iota(jnp.int32, (tm, 1), 0) + (i % tiles_per_example) * tm
        diff = jnp.where(pos < seq, xh * g - t_ref[...], 0.0)
        part = 0.5 * jnp.sum(jnp.sum(diff * diff, axis=-1, keepdims=True), axis=0, keepdims=True) / D_MODEL
        dy = diff / D_MODEL
        t = dy * g
        dh_ref[...] = r * (t - xh * jnp.mean(t * xh, axis=-1, keepdims=True))
        _accumulate(loss_ref, jnp.broadcast_to(part, (1, LANES)), i == 0)
        _accumulate(dg_ref, jnp.sum(dy * xh, axis=0, keepdims=True), i == 0)

    row = pl.BlockSpec((tm, D_MODEL), lambda i: (i, 0))
    vec = pl.BlockSpec((1, D_MODEL), lambda i: (0, 0))
    per_example = pl.BlockSpec((None, tm, D_MODEL), lambda i: (i // tiles_per_example, i % tiles_per_example, 0))
    return _pcall(
        body, name=name, grid=(t_rows // tm,),
        in_specs=[row, per_example, vec],
        out_specs=[row, pl.BlockSpec((1, LANES), lambda i: (0, 0)), vec],
        out_shape=[jax.ShapeDtypeStruct((t_rows, D_MODEL), F32), jax.ShapeDtypeStruct((1, LANES), F32),
                   jax.ShapeDtypeStruct((1, D_MODEL), F32)],
        compiler_params=_cp(("arbitrary",), VMEM_BIG),
    )(h, target, norm)


ATTN_SCALE = HEAD_DIM ** -0.5
STACK_HEADS = (0, 2, 1, 3)
META_PAD = LANES - N_META


def _lane_half(shape, hf):
    lane = lax.broadcasted_iota(jnp.int32, shape, 1)
    return (lane < HEAD_DIM) if hf == 0 else (lane >= HEAD_DIM)


def _kv_variants(ref, rows, kh, pad_rows=0):
    tile = kh // 2
    t = ref[rows, tile * LANES:(tile + 1) * LANES].astype(F32)
    swapped = pltpu.roll(t, HEAD_DIM, axis=1)
    at_low, at_high = (t, swapped) if kh % 2 == 0 else (swapped, t)
    lo = jnp.where(_lane_half(t.shape, 0), at_low, 0.0).astype(BF16)
    hi = jnp.where(_lane_half(t.shape, 1), at_high, 0.0).astype(BF16)
    if pad_rows:
        zeros = jnp.zeros((pad_rows, LANES), BF16)
        lo, hi = jnp.concatenate([lo, zeros], axis=0), jnp.concatenate([hi, zeros], axis=0)
    return lo, hi


def _key_tiles(ref, key_rows, kh):
    return [_kv_variants(ref, r, kh, META_PAD if i == len(key_rows) - 1 else 0) for i, r in enumerate(key_rows)]


def _to_kv_lanes(lo, hi, kh):
    lo = jnp.where(_lane_half(lo.shape, 0), lo, 0.0)
    hi = jnp.where(_lane_half(hi.shape, 1), hi, 0.0)
    if kh % 2 == 0:
        return lo + pltpu.roll(hi, HEAD_DIM, axis=1)
    return pltpu.roll(lo, HEAD_DIM, axis=1) + hi


def _stacked(ref, rows, kh):
    col = kh * 2 * LANES
    return jnp.concatenate([ref[rows, col:col + LANES], ref[rows, col + LANES:col + 2 * LANES]], axis=0)


def _sink_column(sink_ref, kh, nq):
    row = lax.broadcasted_iota(jnp.int32, (4 * nq, 1), 0)
    col = jnp.zeros((4 * nq, 1), F32)
    for quarter, g in enumerate(STACK_HEADS):
        col = jnp.where(row // nq == quarter, sink_ref[0, kh * Q_PER_KV + g], col)
    return col


def _softmax_parts(qs, key_tiles, masks, sink):
    scores = []
    for (k_lo, k_hi), mask in zip(key_tiles, masks):
        s = jnp.concatenate([_dot_nt(qs, k_lo), _dot_nt(qs, k_hi)], axis=0) * ATTN_SCALE
        scores.append(s if mask is None else jnp.where(mask, s, NEG_INF))
    m = jnp.maximum(_row_reduce(scores, jnp.maximum, jnp.max), sink)
    probs = [jnp.exp(s - m) for s in scores]
    e_sink = jnp.exp(sink - m)
    den = _row_sums(probs) + e_sink
    return probs, 1.0 / den, e_sink


def _row_reduce(tiles, combine, reduce):
    chunks = [t[:, c:c + LANES] for t in tiles for c in range(0, t.shape[-1], LANES)]
    return reduce(functools.reduce(combine, chunks), axis=-1, keepdims=True)


def _row_sums(tiles):
    return _row_reduce(tiles, lambda u, w: u + w, jnp.sum)


def _band_mask(nq, first):
    keys = BLOCK if first else 2 * BLOCK
    qi = lax.broadcasted_iota(jnp.int32, (4 * nq, keys), 0) % nq
    kj = lax.broadcasted_iota(jnp.int32, (4 * nq, keys), 1)
    if first:
        return kj <= qi
    return jnp.logical_and(kj > qi, kj <= qi + BLOCK)


def _meta_mask(nq, causal):
    qi = lax.broadcasted_iota(jnp.int32, (4 * nq, LANES), 0) % nq
    kj = lax.broadcasted_iota(jnp.int32, (4 * nq, LANES), 1)
    return jnp.logical_and(kj < N_META, kj <= qi) if causal else kj < N_META


def _attention_schedule(seq, queries, carry):
    meta_rows = pl.ds(seq, N_META)
    meta_ok = _meta_mask(BLOCK, False)
    carry = queries(pl.ds(0, BLOCK), BLOCK, [pl.ds(0, BLOCK), meta_rows], [_band_mask(BLOCK, True), meta_ok], carry)

    def block(n, c):
        r0 = pl.multiple_of(n * BLOCK, BLOCK)
        p0 = pl.multiple_of((n - 1) * BLOCK, BLOCK)
        return queries(pl.ds(r0, BLOCK), BLOCK, [pl.ds(p0, 2 * BLOCK), meta_rows], [_band_mask(BLOCK, False), meta_ok], c)

    carry = lax.fori_loop(1, seq // BLOCK, block, carry)
    return queries(meta_rows, N_META, [meta_rows], [_meta_mask(N_META, True)], carry)


def attention_forward(proj3, sinks, seq, name):
    n_b, n_l, _ = proj3.shape

    def body(sink_ref, q_ref, k_ref, v_ref, o_ref):
        def queries(q_rows, nq, key_rows, masks, carry):
            for kh in range(N_KV_HEADS):
                ks = _key_tiles(k_ref, key_rows, kh)
                vs = _key_tiles(v_ref, key_rows, kh)
                qs = _stacked(q_ref, q_rows, kh)
                probs, inv, _ = _softmax_parts(qs, ks, masks, _sink_column(sink_ref, kh, nq))
                probs = [p.astype(BF16) for p in probs]
                o_lo = functools.reduce(lambda u, w: u + w, [_dot(p[:2 * nq], v_lo) for p, (v_lo, _) in zip(probs, vs)])
                o_hi = functools.reduce(lambda u, w: u + w, [_dot(p[2 * nq:], v_hi) for p, (_, v_hi) in zip(probs, vs)])
                out = (o_lo * inv[:2 * nq] + o_hi * inv[2 * nq:]).astype(BF16)
                col = kh * 2 * LANES
                o_ref[q_rows, col:col + LANES] = out[:nq]
                o_ref[q_rows, col + LANES:col + 2 * LANES] = out[nq:]
            return carry

        _attention_schedule(seq, queries, 0)

    return _pcall(
        body, name=name, grid=(n_b,),
        in_specs=[pl.BlockSpec(memory_space=pltpu.SMEM),
                  pl.BlockSpec((None, n_l, D_MODEL), lambda b: (b, 0, 0)),
                  pl.BlockSpec((None, n_l, KV_WIDTH), lambda b: (b, 0, D_MODEL // KV_WIDTH)),
                  pl.BlockSpec((None, n_l, KV_WIDTH), lambda b: (b, 0, D_MODEL // KV_WIDTH + 1))],
        out_specs=pl.BlockSpec((None, n_l, D_MODEL), lambda b: (b, 0, 0)),
        out_shape=jax.ShapeDtypeStruct((n_b, n_l, D_MODEL), BF16),
        compiler_params=_cp(("arbitrary",), VMEM_BIG),
    )(sinks, proj3, proj3, proj3)


def attention_backward(proj3, dattn3, dproj3, sinks, after, seq, name):
    n_b, n_l, _ = proj3.shape
    qkv_width = D_MODEL + 2 * KV_WIDTH

    def body(sink_ref, q_ref, k_ref, v_ref, do_ref, _, __, dqkv_ref, dsink_ref, dk_ref, dv_ref):
        dk_ref[...] = jnp.zeros_like(dk_ref)
        dv_ref[...] = jnp.zeros_like(dv_ref)
        sub = lax.broadcasted_iota(jnp.int32, (SUBLANES, LANES), 0)
        lane = lax.broadcasted_iota(jnp.int32, (SUBLANES, LANES), 1)

        def queries(q_rows, nq, key_rows, masks, dsink):
            for kh in range(N_KV_HEADS):
                ks = _key_tiles(k_ref, key_rows, kh)
                vs = _key_tiles(v_ref, key_rows, kh)
                qs = _stacked(q_ref, q_rows, kh)
                dos = _stacked(do_ref, q_rows, kh)
                probs, inv, e_sink = _softmax_parts(qs, ks, masks, _sink_column(sink_ref, kh, nq))
                probs = [p * inv for p in probs]
                dps = [jnp.concatenate([_dot_nt(dos, v_lo), _dot_nt(dos, v_hi)], axis=0) for v_lo, v_hi in vs]
                delta = _row_sums([p * dp for p, dp in zip(probs, dps)])
                d_sink = -(e_sink * inv) * delta
                for quarter, g in enumerate(STACK_HEADS):
                    d_here = jnp.sum(d_sink[quarter * nq:(quarter + 1) * nq], axis=0, keepdims=True)
                    dsink = dsink + jnp.where(jnp.logical_and(sub == 0, lane == kh * Q_PER_KV + g), d_here, 0.0)
                dq = None
                tile = slice((kh // 2) * LANES, (kh // 2 + 1) * LANES)
                for r, p, dp, (k_lo, k_hi) in zip(key_rows, probs, dps, ks):
                    ds = (p * (dp - delta)).astype(BF16)
                    p16 = p.astype(BF16)
                    dq_x = _dot(ds[:2 * nq], k_lo) + _dot(ds[2 * nq:], k_hi)
                    dq = dq_x if dq is None else dq + dq_x
                    d_k = _to_kv_lanes(_dot_tn(ds[:2 * nq], qs), _dot_tn(ds[2 * nq:], qs), kh) * ATTN_SCALE
                    d_v = _to_kv_lanes(_dot_tn(p16[:2 * nq], dos), _dot_tn(p16[2 * nq:], dos), kh)
                    n_keys = r.size
                    dk_ref[r, tile] += d_k[:n_keys]
                    dv_ref[r, tile] += d_v[:n_keys]
                dq = (dq * ATTN_SCALE).astype(BF16)
                col = kh * 2 * LANES
                dqkv_ref[q_rows, col:col + LANES] = dq[:nq]
                dqkv_ref[q_rows, col + LANES:col + 2 * LANES] = dq[nq:]
            return dsink

        dsink_ref[...] = _attention_schedule(seq, queries, jnp.zeros((SUBLANES, LANES), F32))
        dqkv_ref[:, D_MODEL:D_MODEL + KV_WIDTH] = dk_ref[...].astype(BF16)
        dqkv_ref[:, D_MODEL + KV_WIDTH:] = dv_ref[...].astype(BF16)

    return _pcall(
        body, name=name, grid=(n_b,),
        in_specs=[pl.BlockSpec(memory_space=pltpu.SMEM),
                  pl.BlockSpec((None, n_l, D_MODEL), lambda b: (b, 0, 0)),
                  pl.BlockSpec((None, n_l, KV_WIDTH), lambda b: (b, 0, D_MODEL // KV_WIDTH)),
                  pl.BlockSpec((None, n_l, KV_WIDTH), lambda b: (b, 0, D_MODEL // KV_WIDTH + 1)),
                  pl.BlockSpec((None, n_l, D_MODEL), lambda b: (b, 0, 0)),
                  ANY_SPEC, ANY_SPEC],
        out_specs=[pl.BlockSpec((None, n_l, qkv_width), lambda b: (b, 0, 0)),
                   pl.BlockSpec((None, SUBLANES, LANES), lambda b: (b, 0, 0))],
        out_shape=[jax.ShapeDtypeStruct(dproj3.shape, BF16), jax.ShapeDtypeStruct((n_b, SUBLANES, LANES), F32)],
        scratch_shapes=[pltpu.VMEM((n_l, KV_WIDTH), F32), pltpu.VMEM((n_l, KV_WIDTH), F32)],
        input_output_aliases={5: 0},
        compiler_params=_cp(("arbitrary",), VMEM_BIG),
    )(sinks, proj3, proj3, proj3, dattn3, dproj3, after)


TAB_ROWS = 8
SCAN_UNROLL = 4


def _cmul(ar, ai, br, bi):
    return ar * br - ai * bi, ar * bi + ai * br


def _discretise(ar, ai, ls):
    step = jnp.exp(ls)
    mag = jnp.exp(ar * step)
    ang = ai * step
    cos, sin = jnp.cos(ang), jnp.sin(ang)
    lr, li = mag * cos, mag * sin
    den = ar * ar + ai * ai
    nr, ni = lr - 1.0, li
    cr = (nr * ar + ni * ai) / den
    ci = (ni * ar - nr * ai) / den
    return step, mag, lr, li, den, nr, ni, cr, ci


def _scan_tables(lr, li, reverse):
    n = lr.shape[-1]
    pw = [(lr, li)]
    for _ in range(SUBLANES - 1):
        pw.append(_cmul(pw[-1][0], pw[-1][1], lr, li))
    row = lax.broadcasted_iota(jnp.int32, (SUBLANES, n), 0)
    out = []
    for d in (1, 2, 4):
        ok = (row + d <= SUBLANES - 1) if reverse else (row >= d)
        out += [jnp.where(ok, pw[d - 1][0], 0.0), jnp.where(ok, pw[d - 1][1], 0.0)]
    cr = jnp.zeros((SUBLANES, n), F32)
    ci = jnp.zeros((SUBLANES, n), F32)
    for r in range(SUBLANES):
        e = (SUBLANES - r) if reverse else (r + 1)
        cr = jnp.where(row == r, pw[e - 1][0], cr)
        ci = jnp.where(row == r, pw[e - 1][1], ci)
    return out + [cr, ci]


def ssm_prepare(ar, ai, ls, br_t, bi_t, name):
    def body(ar_ref, ai_ref, ls_ref, br_ref, bi_ref, bbr_ref, bbi_ref, tf_ref, tr_ref):
        _, _, lr, li, _, _, _, cr, ci = _discretise(ar_ref[...], ai_ref[...], ls_ref[...])
        br, bi = br_ref[...], bi_ref[...]
        bbr_ref[...] = cr * br - ci * bi
        bbi_ref[...] = cr * bi + ci * br
        for k, t in enumerate(_scan_tables(lr, li, False)):
            tf_ref[k] = t
        for k, t in enumerate(_scan_tables(lr, -li, True)):
            tr_ref[k] = t

    return _pcall(
        body, name=name,
        out_shape=[jax.ShapeDtypeStruct((SSM_GROUP, N_STATES), F32), jax.ShapeDtypeStruct((SSM_GROUP, N_STATES), F32),
                   jax.ShapeDtypeStruct((TAB_ROWS, SUBLANES, N_STATES), F32),
                   jax.ShapeDtypeStruct((TAB_ROWS, SUBLANES, N_STATES), F32)],
    )(ar, ai, ls, br_t, bi_t)


def ssm_param_backward(ar, ai, ls, br_t, bi_t, dlr_p, dli_p, dbbr, dbbi, group_sum, name):
    def body(ar_ref, ai_ref, ls_ref, br_ref, bi_ref, dlr_ref, dli_ref, dbbr_ref, dbbi_ref, gs_ref,
             dar_ref, dai_ref, dls_ref, dbr_ref, dbi_ref):
        ar, ai = ar_ref[...], ai_ref[...]
        step, mag, lr, li, den, nr, ni, cr, ci = _discretise(ar, ai, ls_ref[...])
        br, bi, dbbr_v, dbbi_v = br_ref[...], bi_ref[...], dbbr_ref[...], dbbi_ref[...]
        dbr_ref[...] = cr * dbbr_v + ci * dbbi_v
        dbi_ref[...] = cr * dbbi_v - ci * dbbr_v
        dcr = jnp.sum(dbbr_v * br + dbbi_v * bi, axis=0, keepdims=True)
        dci = jnp.sum(dbbi_v * br - dbbr_v * bi, axis=0, keepdims=True)
        dnr = (dcr * ar - dci * ai) / den
        dni = (dcr * ai + dci * ar) / den
        dden = -(cr * dcr + ci * dci) / den
        dar = (dcr * nr + dci * ni) / den + dden * 2.0 * ar
        dai = (dcr * ni - dci * nr) / den + dden * 2.0 * ai
        dlr = jnp.sum(dlr_ref[...], axis=0, keepdims=True) + dnr
        dli = jnp.sum(dli_ref[...], axis=0, keepdims=True) + dni
        dmag = (dlr * lr + dli * li) / mag
        dang = dli * lr - dlr * li
        dar_ref[...] = dar + dmag * mag * step
        dai_ref[...] = dai + dang * step
        dstep = dmag * mag * ar + dang * ai
        dls_ref[...] = jnp.dot(dstep * step, gs_ref[...], preferred_element_type=F32, precision=lax.Precision.HIGHEST)

    vec = jax.ShapeDtypeStruct((1, N_STATES), F32)
    mat = jax.ShapeDtypeStruct((SSM_GROUP, N_STATES), F32)
    return _pcall(body, name=name, out_shape=[vec, vec, jax.ShapeDtypeStruct((1, LANES), F32), mat, mat])(
        ar, ai, ls, br_t, bi_t, dlr_p, dli_p, dbbr, dbbi, group_sum)


def _scan_rows(a, b, tabs, carry, reverse):
    for k, d in enumerate((1, 2, 4)):
        shift = SUBLANES - d if reverse else d
        sr, si = pltpu.roll(a, shift, axis=0), pltpu.roll(b, shift, axis=0)
        pr, pi = _cmul(tabs[2 * k], tabs[2 * k + 1], sr, si)
        a, b = a + pr, b + pi
    pr, pi = _cmul(tabs[6], tabs[7], carry[0], carry[1])
    return a + pr, b + pi


def _time_groups(seq, reverse):
    meta = [seq + SUBLANES * g for g in range(N_META // SUBLANES)]
    return meta[::-1] if reverse else meta


def ssm_forward_scan(proj3, b_comb, tabf, c_comb, dvec, seq, name):
    n_b, n_l, _ = proj3.shape
    u_blk = (D_MODEL + 2 * KV_WIDTH) // LANES

    def body(u_ref, b_ref, tab_ref, c_ref, d_ref, x_ref, y_ref, bu, xs):
        j = pl.program_id(1)
        u = u_ref[...]
        bu[...] = _dot(u, b_ref[...])
        tabs = [tab_ref[k] for k in range(TAB_ROWS)]

        def group(r0, carry):
            rows = pl.ds(r0, SUBLANES)
            a, b = _scan_rows(bu[rows, :SCAN_COLS], bu[rows, SCAN_COLS:], tabs, carry, False)
            xs[rows, :SCAN_COLS] = a
            xs[rows, SCAN_COLS:] = b
            return (jnp.broadcast_to(a[SUBLANES - 1:, :], a.shape), jnp.broadcast_to(b[SUBLANES - 1:, :], b.shape))

        zero = jnp.zeros((SUBLANES, SCAN_COLS), F32)
        carry = (zero, zero)
        for r0 in _time_groups(seq, False):
            carry = group(r0, carry)
        span = SCAN_UNROLL * SUBLANES

        def groups(t, c):
            for k in range(SCAN_UNROLL):
                c = group(pl.multiple_of(t * span, span) + k * SUBLANES, c)
            return c

        lax.fori_loop(0, seq // span, groups, carry)
        x16 = xs[...].astype(BF16)
        x_ref[...] = x16
        contrib = _dot(x16, c_ref[...])

        @pl.when(j % 2 == 0)
        def _():
            y_ref[...] = contrib + d_ref[...] * u.astype(F32)

        @pl.when(j % 2 == 1)
        def _():
            y_ref[...] += contrib

    return _pcall(
        body, name=name, grid=(n_b, N_SCAN_BLK),
        in_specs=[pl.BlockSpec((None, n_l, LANES), lambda b, j: (b, 0, u_blk + j // 2)),
                  pl.BlockSpec((None, LANES, 2 * SCAN_COLS), lambda b, j: (j, 0, 0)),
                  pl.BlockSpec((TAB_ROWS, SUBLANES, SCAN_COLS), lambda b, j: (0, 0, j)),
                  pl.BlockSpec((None, 2 * SCAN_COLS, LANES), lambda b, j: (j, 0, 0)),
                  pl.BlockSpec((1, LANES), lambda b, j: (0, j // 2))],
        out_specs=[pl.BlockSpec((None, n_l, 2 * SCAN_COLS), lambda b, j: (b, 0, j)),
                   pl.BlockSpec((None, n_l, LANES), lambda b, j: (b, 0, j // 2))],
        out_shape=[jax.ShapeDtypeStruct((n_b, n_l, 2 * N_STATES), BF16),
                   jax.ShapeDtypeStruct((n_b, n_l, SSM_WIDTH), F32)],
        scratch_shapes=[pltpu.VMEM((n_l, 2 * SCAN_COLS), F32)] * 2,
        compiler_params=_cp(("arbitrary", "arbitrary"), VMEM_BIG),
    )(proj3, b_comb, tabf, c_comb, dvec)


def ssm_backward_scan(dyraw3, xs3, dproj3, c_comb_t, tabr, b_comb_t, dvec, seq, name):
    n_b, n_l, _ = xs3.shape
    u_blk = (D_MODEL + 2 * KV_WIDTH) // LANES

    def body(dy_ref, x_ref, _, c_ref, tab_ref, b_ref, d_ref, du_ref, g_ref, dlr_ref, dli_ref, dx, gs, xs, du_acc):
        j = pl.program_id(1)
        dy = dy_ref[...]
        dx[...] = _dot(dy, c_ref[...])
        xs[...] = x_ref[...].astype(F32)
        tabs = [tab_ref[k] for k in range(TAB_ROWS)]
        last_row = lax.broadcasted_iota(jnp.int32, (SUBLANES, SCAN_COLS), 0) == SUBLANES - 1

        def group(r0, state):
            cr, ci, acc_r, acc_i = state
            rows = pl.ds(r0, SUBLANES)
            a, b = _scan_rows(dx[rows, :SCAN_COLS], dx[rows, SCAN_COLS:], tabs, (cr, ci), True)
            gs[rows, :SCAN_COLS] = a
            gs[rows, SCAN_COLS:] = b
            na = jnp.where(last_row, cr, pltpu.roll(a, SUBLANES - 1, axis=0))
            nb = jnp.where(last_row, ci, pltpu.roll(b, SUBLANES - 1, axis=0))
            xa, xb = xs[rows, :SCAN_COLS], xs[rows, SCAN_COLS:]
            return (jnp.broadcast_to(a[:1, :], a.shape), jnp.broadcast_to(b[:1, :], b.shape),
                    acc_r + na * xa + nb * xb, acc_i + nb * xa - na * xb)

        zero = jnp.zeros((SUBLANES, SCAN_COLS), F32)
        span = SCAN_UNROLL * SUBLANES
        n_spans = seq // span

        def groups(t, s):
            for k in reversed(range(SCAN_UNROLL)):
                s = group(pl.multiple_of((n_spans - 1 - t) * span, span) + k * SUBLANES, s)
            return s

        state = lax.fori_loop(0, n_spans, groups, (zero, zero, zero, zero))
        for r0 in _time_groups(seq, True):
            state = group(r0, state)
        dlr_ref[...] = state[2]
        dli_ref[...] = state[3]
        g16 = gs[...].astype(BF16)
        g_ref[...] = g16
        contrib = _dot(g16, b_ref[...])

        @pl.when(j % 2 == 0)
        def _():
            du_acc[...] = contrib + d_ref[...] * dy.astype(F32)

        @pl.when(j % 2 == 1)
        def _():
            du_ref[...] = (du_acc[...] + contrib).astype(BF16)

    state_blk = pl.BlockSpec((None, n_l, 2 * SCAN_COLS), lambda b, j: (b, 0, j))
    dl_blk = pl.BlockSpec((None, SUBLANES, SCAN_COLS), lambda b, j: (b, 0, j))
    return _pcall(
        body, name=name, grid=(n_b, N_SCAN_BLK),
        in_specs=[pl.BlockSpec((None, n_l, LANES), lambda b, j: (b, 0, j // 2)), state_blk,
                  pl.BlockSpec(memory_space=pl.ANY),
                  pl.BlockSpec((None, LANES, 2 * SCAN_COLS), lambda b, j: (j, 0, 0)),
                  pl.BlockSpec((TAB_ROWS, SUBLANES, SCAN_COLS), lambda b, j: (0, 0, j)),
                  pl.BlockSpec((None, 2 * SCAN_COLS, LANES), lambda b, j: (j, 0, 0)),
                  pl.BlockSpec((1, LANES), lambda b, j: (0, j // 2))],
        out_specs=[pl.BlockSpec((None, n_l, LANES), lambda b, j: (b, 0, u_blk + j // 2)), state_blk, dl_blk, dl_blk],
        out_shape=[jax.ShapeDtypeStruct(dproj3.shape, BF16), jax.ShapeDtypeStruct((n_b, n_l, 2 * N_STATES), BF16),
                   jax.ShapeDtypeStruct((n_b, SUBLANES, N_STATES), F32), jax.ShapeDtypeStruct((n_b, SUBLANES, N_STATES), F32)],
        scratch_shapes=[pltpu.VMEM((n_l, 2 * SCAN_COLS), F32)] * 3 + [pltpu.VMEM((n_l, LANES), F32)],
        input_output_aliases={2: 0},
        compiler_params=_cp(("arbitrary", "arbitrary"), VMEM_BIG),
    )(dyraw3, xs3, dproj3, c_comb_t, tabr, b_comb_t, dvec)


def ssm_param_grads(proj, gs, xs, dyraw, tm, name):
    t_rows = proj.shape[0]
    ni = t_rows // tm
    u_blk = (D_MODEL + 2 * KV_WIDTH) // LANES
    width = 2 * SCAN_COLS

    def body(u_ref, g_ref, x_ref, dy_ref, db_ref, dc_ref, dd_ref):
        cb, i = pl.program_id(0), pl.program_id(1)
        u, dy = u_ref[...], dy_ref[...]
        _accumulate(db_ref, _dot_tn(u, g_ref[...]), i == 0)
        _accumulate(dc_ref, _dot_tn(x_ref[...], dy), i == 0)

        @pl.when(cb % 2 == 0)
        def _():
            _accumulate(dd_ref, jnp.sum(dy.astype(F32) * u.astype(F32), axis=0, keepdims=True), i == 0)

    return _pcall(
        body, name=name, grid=(N_SCAN_BLK, ni),
        in_specs=[pl.BlockSpec((tm, LANES), lambda cb, i: (i, u_blk + cb // 2)),
                  pl.BlockSpec((tm, width), lambda cb, i: (i, cb)),
                  pl.BlockSpec((tm, width), lambda cb, i: (i, cb)),
                  pl.BlockSpec((tm, LANES), lambda cb, i: (i, cb // 2))],
        out_specs=[pl.BlockSpec((None, LANES, width), lambda cb, i: (cb, 0, 0)),
                   pl.BlockSpec((None, width, LANES), lambda cb, i: (cb, 0, 0)),
                   pl.BlockSpec((1, LANES), lambda cb, i: (0, cb // 2))],
        out_shape=[jax.ShapeDtypeStruct((N_SCAN_BLK, LANES, width), F32),
                   jax.ShapeDtypeStruct((N_SCAN_BLK, width, LANES), F32), jax.ShapeDtypeStruct((1, SSM_WIDTH), F32)],
        compiler_params=_cp(("arbitrary", "arbitrary"), VMEM_BIG),
    )(proj, gs, xs, dyraw)


def sum_leading(x, name):
    def body(x_ref, o_ref):
        acc = x_ref[0]
        for k in range(1, x.shape[0]):
            acc = acc + x_ref[k]
        o_ref[...] = acc

    return _pcall(body, name=name, out_shape=jax.ShapeDtypeStruct(x.shape[1:], x.dtype))(x)


WEIGHTS = ['meta_tokens', 'ffn1_norm', 'ffn1_w1', 'ffn1_w3', 'ffn1_w2', 'mix_norm', 'w_in', 'attn_sinks', 'ssm_a_re',
           'ssm_a_im', 'ssm_log_step', 'ssm_b_re', 'ssm_b_im', 'ssm_c_re', 'ssm_c_im', 'ssm_d', 'ssm_glu_a', 'ssm_glu_b',
           'w_out', 'ffn2_norm', 'ffn2_w1', 'ffn2_w3', 'ffn2_w2', 'final_norm']
SHARDED = ['ffn1_w1', 'ffn1_w3', 'ffn1_w2', 'ffn2_w1', 'ffn2_w3', 'ffn2_w2', 'w_in', 'ssm_glu_a', 'ssm_glu_b', 'w_out']
REPLICATED = ['ffn1_norm', 'mix_norm', 'ffn2_norm', 'final_norm', 'attn_sinks', 'ssm_a_re', 'ssm_a_im', 'ssm_log_step',
              'ssm_b_re', 'ssm_b_im', 'ssm_c_re', 'ssm_c_im', 'ssm_d']
PACK_COLS = 1024


def _pack(arrays):
    parts = []
    for a in arrays:
        flat = a.reshape(-1)
        chunk = SUBLANES * PACK_COLS
        padded = -(-flat.shape[0] // chunk) * chunk
        parts.append(jnp.pad(flat, (0, padded - flat.shape[0])).reshape(-1, PACK_COLS))
    return jnp.concatenate(parts, axis=0)


def _unpack(packed, shapes):
    out, row = [], 0
    for shape in shapes:
        size = 1
        for s in shape:
            size *= s
        chunk = SUBLANES * PACK_COLS
        rows = -(-size // chunk) * SUBLANES
        out.append(packed[row:row + rows].reshape(-1)[:size].reshape(shape))
        row += rows
    return out


def kernel(x, meta_tokens, ffn1_norm, ffn1_w1, ffn1_w3, ffn1_w2, mix_norm, w_in, attn_sinks, ssm_a_re, ssm_a_im, ssm_log_step, ssm_b_re, ssm_b_im, ssm_c_re, ssm_c_im, ssm_d, ssm_glu_a, ssm_glu_b, w_out, ffn2_norm, ffn2_w1, ffn2_w3, ffn2_w2, final_norm, loss_target, m_meta_tokens, m_ffn1_norm, m_ffn1_w1, m_ffn1_w3, m_ffn1_w2, m_mix_norm, m_w_in, m_attn_sinks, m_ssm_a_re, m_ssm_a_im, m_ssm_log_step, m_ssm_b_re, m_ssm_b_im, m_ssm_c_re, m_ssm_c_im, m_ssm_d, m_ssm_glu_a, m_ssm_glu_b, m_w_out, m_ffn2_norm, m_ffn2_w1, m_ffn2_w3, m_ffn2_w2, m_final_norm, v_meta_tokens, v_ffn1_norm, v_ffn1_w1, v_ffn1_w3, v_ffn1_w2, v_mix_norm, v_w_in, v_attn_sinks, v_ssm_a_re, v_ssm_a_im, v_ssm_log_step, v_ssm_b_re, v_ssm_b_im, v_ssm_c_re, v_ssm_c_im, v_ssm_d, v_ssm_glu_a, v_ssm_glu_b, v_w_out, v_ffn2_norm, v_ffn2_w1, v_ffn2_w3, v_ffn2_w2, v_final_norm):
    given = dict(locals())
    w = {n: given[n] for n in WEIGHTS}
    m = {n: given["m_" + n] for n in WEIGHTS}
    v = {n: given["v_" + n] for n in WEIGHTS}

    n_b, seq, _ = x.shape
    n_l = seq + N_META
    t_rows = n_b * n_l
    tm = _row_tile(n_l, 688)
    px, py, pc = _my_place()
    me = 4 * px + 2 * py + pc

    glu = jnp.stack([ssm_glu_a[0], ssm_glu_b[0]]).astype(BF16)
    ffn_names = ['ffn1_w1', 'ffn1_w3', 'ffn1_w2', 'ffn2_w1', 'ffn2_w3', 'ffn2_w2']

    def hidden_on_rows(n, t):
        return t[0] if n.endswith('w2') else t[0].T

    def hidden_on_rows_back(n, t):
        return t[None] if n.endswith('w2') else t.T[None]

    me_idx = jnp.reshape(me, (1,)).astype(jnp.int32)
    first_names, later_names = ffn_names[:3], ffn_names[3:]
    *first, metag = all_gather_list(
        [hidden_on_rows(n, w[n]).astype(BF16) for n in first_names] + [meta_tokens], meta_tokens, "ag_first")
    win_send, win_recv, win_shard, win_land, win_token = exchange_start(
        [w_in[0].astype(BF16)], first[0], True, "ag_w_in_start")
    later_shards = [hidden_on_rows(n, w[n]).astype(BF16) for n in later_names] + [glu, w_out[0].astype(BF16)]
    ag_send, ag_recv, later_shards, later_lands, ag_token = exchange_start(later_shards, win_token, True, "ag_later_start")
    full = {n: g.reshape(D_FF, D_MODEL) for n, g in zip(first_names, first)}
    meta_full = metag.transpose(1, 0, 2).reshape(N_META, D_MODEL)

    final_g = final_norm.reshape(1, D_MODEL)

    ar = ssm_a_re.reshape(1, N_STATES)
    ai = ssm_a_im.reshape(1, N_STATES)
    ls = jnp.repeat(ssm_log_step.reshape(SSM_GROUPS), SSM_STATE).reshape(1, N_STATES)
    br_t = ssm_b_re[0].transpose(2, 0, 1).reshape(SSM_GROUP, N_STATES)
    bi_t = ssm_b_im[0].transpose(2, 0, 1).reshape(SSM_GROUP, N_STATES)
    bbr, bbi, tabf, tabr = ssm_prepare(ar, ai, ls, br_t, bi_t, "ssm_prepare")
    bbr_g = bbr.reshape(SSM_GROUP, SSM_GROUPS, SSM_STATE).transpose(1, 0, 2)
    bbi_g = bbi.reshape(SSM_GROUP, SSM_GROUPS, SSM_STATE).transpose(1, 0, 2)
    groups_per_blk = SCAN_COLS // SSM_STATE
    half = ((jnp.arange(N_SCAN_BLK) % 2)[:, None] == jnp.arange(2)[None, :]).astype(F32)
    eye = jnp.eye(groups_per_blk, dtype=F32)

    def scan_blocks(re_g, im_g):
        def one(t):
            t = t.reshape(N_SCAN_BLK, groups_per_blk, SSM_GROUP, SSM_STATE)
            t = t[:, :, :, None, :] * eye[None, :, None, :, None]
            t = t.reshape(N_SCAN_BLK, LANES // 2, SCAN_COLS)
            return (t[:, None] * half[:, :, None, None]).reshape(N_SCAN_BLK, LANES, SCAN_COLS)
        return jnp.concatenate([one(re_g), one(im_g)], axis=-1).astype(BF16)

    b_comb = scan_blocks(bbr_g, bbi_g)
    c_comb_t = scan_blocks(ssm_c_re[0], -ssm_c_im[0])
    b_comb_t, c_comb = b_comb.transpose(0, 2, 1), c_comb_t.transpose(0, 2, 1)

    ffn1_w = (full['ffn1_w1'], full['ffn1_w3'], full['ffn1_w2'])
    h1, hn1, a1, b1, h0 = ffn_forward(x, ffn1_norm, *ffn1_w, ag_token, tm, "ffn1_fwd", meta=meta_full)
    win_shard, (wing,) = exchange_wait(win_send, win_recv, win_shard, win_land, h1, True, "ag_w_in_wait")
    wing = lax.dynamic_update_slice_in_dim(wing, win_shard[0][None], me, axis=0)
    hnm, proj = mix_forward(h1, mix_norm, wing, tm, "mix_fwd")
    proj3 = proj.reshape(n_b, n_l, IN_WIDTH)
    attn3 = attention_forward(proj3, attn_sinks, seq, "attn_fwd")
    attn = attn3.reshape(t_rows, D_MODEL)
    xs3, yraw3 = ssm_forward_scan(proj3, b_comb, tabf, c_comb, ssm_d, seq, "ssm_fwd")
    yraw = yraw3.reshape(t_rows, SSM_WIDTH)
    later_shards, later = exchange_wait(ag_send, ag_recv, later_shards, later_lands, yraw3, True, "ag_later_wait")
    later = [lax.dynamic_update_slice_in_dim(z, s[None], me, axis=0) for z, s in zip(later, later_shards)]
    for n, g in zip(later_names, later):
        full[n] = g.reshape(D_FF, D_MODEL)
    ffn2_w = (full['ffn2_w1'], full['ffn2_w3'], full['ffn2_w2'])
    glug, wog = later[len(later_names):]
    glu_a = glug[:, 0].transpose(1, 0, 2).reshape(SSM_WIDTH, D_MODEL)
    glu_b = glug[:, 1].transpose(1, 0, 2).reshape(SSM_WIDTH, D_MODEL)
    w_out_full = wog.reshape(D_MODEL, D_MODEL)
    h2 = merge_forward(h1, yraw, attn, proj, glu_a, glu_b, w_out_full, tm, "merge_fwd")
    h3, hn2, a2, b2 = ffn_forward(h2, ffn2_norm, *ffn2_w, ag_token, tm, "ffn2_fwd")
    dh3, loss_part, g_final = final_loss_backward(h3, loss_target, final_g, seq, tm, "loss_bwd")
    loss = lax.psum(loss_part[0, 0], ("x", "y", "c"))

    def blocked_ffn(*grads_by_hidden_row):
        return tuple(t.reshape(N_DEV, FF_BLK, D_MODEL) for t in grads_by_hidden_row)

    def blocked_cols(full_grad):
        r = full_grad.shape[0]
        return full_grad.reshape(r, N_DEV, full_grad.shape[1] // N_DEV).transpose(1, 0, 2).astype(BF16)

    early = {}

    def start_reduce(names, tag):
        srcs = [dw[n] for n in names]
        send, recv, srcs, lands, token = exchange_start(srcs, srcs[0], False, "rs_" + tag + "_start")
        early[tag] = (names, send, recv, srcs, lands)
        return token

    dw = {}
    da2, db2, dh3_half = ffn_backward_hidden(dh3, a2, b2, ffn2_w[2], g_final, tm, "ffn2_bwd_hid")
    dw['ffn2_w1'], dw['ffn2_w3'], dw['ffn2_w2'] = blocked_ffn(
        *ffn_backward_weights_in(hn2, da2, db2, n_l, FF_BWD_COLS, "ffn2_bwd_w_in"),
        ffn_backward_weights_out(dh3_half, a2, b2, da2, n_l, FF_BWD_COLS, "ffn2_bwd_w_out"))
    token = start_reduce(later_names, "ffn2")
    dh2, g_ffn2_norm = ffn_backward_input(dh3, h2, ffn2_norm, da2, db2, ffn2_w[0], ffn2_w[1], token, tm, "ffn2_bwd_in")
    dattn, dyraw, dproj, *for_weights = merge_backward(dh2, yraw, attn, proj, glu_a, glu_b, w_out_full, token, tm,
                                                       "merge_bwd")
    d_wo, d_ga, d_gb = merge_backward_weights(*for_weights, tm, "merge_bwd_w")
    dw['ssm_glu_a'] = blocked_cols(d_ga)
    dw['ssm_glu_b'] = blocked_cols(d_gb)
    dw['w_out'] = d_wo.reshape(N_DEV, D_MODEL // N_DEV, D_MODEL).astype(BF16)
    token = start_reduce(['ssm_glu_a', 'ssm_glu_b', 'w_out'], "mix")
    dproj3 = dproj.reshape(n_b, n_l, IN_WIDTH)
    dproj3, dsink_p = attention_backward(proj3, dattn.reshape(n_b, n_l, D_MODEL), dproj3, attn_sinks, token, seq,
                                         "attn_bwd")
    dproj3, gs3, dlr_p, dli_p = ssm_backward_scan(
        dyraw.reshape(n_b, n_l, SSM_WIDTH), xs3, dproj3, c_comb_t, tabr, b_comb_t, ssm_d, seq, "ssm_bwd")
    dproj = dproj3.reshape(t_rows, IN_WIDTH)
    d_bd, d_cd, g_d = ssm_param_grads(proj, gs3.reshape(t_rows, 2 * N_STATES), xs3.reshape(t_rows, 2 * N_STATES),
                                      dyraw, n_l, "ssm_bwd_w")
    w_in_full = wing.transpose(1, 0, 2).reshape(D_MODEL, IN_WIDTH)
    dh1, g_mix_norm = mix_backward_act(dh2, h1, mix_norm, dproj, w_in_full, tm, "mix_bwd_act")
    dw['w_in'] = mix_backward_weights(hnm, dproj, n_l, "mix_bwd_w")
    token = start_reduce(['w_in'], "w_in")

    def group_blocks(part, channels_first):
        if channels_first:
            t = jnp.sum(part.reshape(N_SCAN_BLK, 2, LANES // 2, SCAN_COLS) * half[:, :, None, None], axis=1)
            t = t.reshape(N_SCAN_BLK, groups_per_blk, SSM_GROUP, groups_per_blk, SSM_STATE)
            t = jnp.sum(t * eye[None, :, None, :, None], axis=3)
            return t.reshape(SSM_GROUPS, SSM_GROUP, SSM_STATE)
        t = jnp.sum(part.reshape(N_SCAN_BLK, SCAN_COLS, 2, LANES // 2) * half[:, None, :, None], axis=2)
        t = t.reshape(N_SCAN_BLK, groups_per_blk, SSM_STATE, groups_per_blk, SSM_GROUP)
        t = jnp.sum(t * eye[None, :, None, :, None], axis=3)
        return t.reshape(SSM_GROUPS, SSM_STATE, SSM_GROUP).transpose(0, 2, 1)

    dbbr = group_blocks(d_bd[:, :, :SCAN_COLS], True).transpose(1, 0, 2).reshape(SSM_GROUP, N_STATES)
    dbbi = group_blocks(d_bd[:, :, SCAN_COLS:], True).transpose(1, 0, 2).reshape(SSM_GROUP, N_STATES)
    g_c_re = group_blocks(d_cd[:, :SCAN_COLS, :], False)[None]
    g_c_im = -group_blocks(d_cd[:, SCAN_COLS:, :], False)[None]
    group_sum = (jnp.arange(N_STATES)[:, None] // SSM_STATE == jnp.arange(LANES)[None, :]).astype(F32)
    g_ar, g_ai, g_ls, g_br, g_bi = ssm_param_backward(
        ar, ai, ls, br_t, bi_t, dlr_p.reshape(n_b * SUBLANES, N_STATES), dli_p.reshape(n_b * SUBLANES, N_STATES),
        dbbr, dbbi, group_sum, "ssm_bwd_params")
    g_sinks = sum_leading(dsink_p, "sink_sum")[0:1, :N_KV_HEADS * Q_PER_KV]

    small = {
        'mix_norm': g_mix_norm, 'ffn2_norm': g_ffn2_norm, 'final_norm': g_final.reshape(D_MODEL),
        'attn_sinks': g_sinks, 'ssm_a_re': g_ar.reshape(1, SSM_GROUPS, SSM_STATE), 'ssm_a_im': g_ai.reshape(1, SSM_GROUPS, SSM_STATE),
        'ssm_log_step': g_ls[:, :SSM_GROUPS],
        'ssm_b_re': g_br.reshape(SSM_GROUP, SSM_GROUPS, SSM_STATE).transpose(1, 2, 0)[None],
        'ssm_b_im': g_bi.reshape(SSM_GROUP, SSM_GROUPS, SSM_STATE).transpose(1, 2, 0)[None],
        'ssm_c_re': g_c_re, 'ssm_c_im': g_c_im, 'ssm_d': g_d,
    }
    early_small = [n for n in REPLICATED if n in small]
    sg_send, sg_recv, sg_src, sg_land, token = exchange_start(
        [_pack([small[n] for n in early_small])], token, True, "ag_small_start")
    da1, db1, dh1_half = ffn_backward_hidden(dh1, a1, b1, ffn1_w[2], token, tm, "ffn1_bwd_hid")
    d_w1t, d_w3t = ffn_backward_weights_in(hn1, da1, db1, n_l, FF_BWD_COLS, "ffn1_bwd_w_in")
    dw['ffn1_w1'], dw['ffn1_w3'] = blocked_ffn(d_w1t, d_w3t)
    token = start_reduce(first_names[:2], "ffn1_in")
    (dw['ffn1_w2'],) = blocked_ffn(
        ffn_backward_weights_out(dh1_half, a1, b1, token, n_l, FF_BWD_COLS, "ffn1_bwd_w_out"))
    token = start_reduce(first_names[2:], "ffn1_out")
    grad_x, meta_rows_grad, g_ffn1_norm = ffn_backward_input(
        dh1, h0, ffn1_norm, da1, db1, ffn1_w[0], ffn1_w[1], token, tm, "ffn1_bwd_in", examples=(n_b, seq))
    g_meta = sum_leading(meta_rows_grad, "meta_sum")

    grads, deltas, new_m, new_v = {}, {}, {}, {}

    def views(n):
        if n in ffn_names:
            return functools.partial(hidden_on_rows, n), functools.partial(hidden_on_rows_back, n)
        return (lambda t: t[0]), (lambda t: t[None])

    def finish_reduce(tag, previous):
        names, send, recv, srcs, lands = early[tag]
        srcs, lands = exchange_wait(send, recv, srcs, lands, previous, False, "rs_" + tag + "_wait")
        for n, g, land in zip(names, srcs, lands):
            two_d, back = views(n)
            out = adamw_exchanged(me_idx, g, land, two_d(w[n]), two_d(m[n]), two_d(v[n]), "adamw_" + n)
            grads[n], deltas[n], new_m[n], new_v[n] = (back(o) for o in out)
            previous = out[1]
        return previous

    sl_send, sl_recv, sl_src, sl_land, previous = exchange_start(
        [_pack([g_ffn1_norm, g_meta])], g_meta, True, "ag_small_late_start")
    for tag in ("ffn2", "mix", "w_in"):
        previous = finish_reduce(tag, previous)

    zeros_meta = jnp.zeros((N_META, D_MODEL), F32)
    sl_src, (late_parts,) = exchange_wait(sl_send, sl_recv, sl_src, sl_land, previous, True, "ag_small_late_wait")
    late_parts = lax.dynamic_update_slice_in_dim(late_parts, sl_src[0][None], me, axis=0)
    sg_src, (early_parts,) = exchange_wait(sg_send, sg_recv, sg_src, sg_land, late_parts, True, "ag_small_wait")
    early_parts = lax.dynamic_update_slice_in_dim(early_parts, sg_src[0][None], me, axis=0)

    def small_update(parts, names, extra, tag):
        pack_of = lambda d: _pack([d[n] for n in names] + extra)
        packed = adamw_small(parts, pack_of(w), pack_of(m), pack_of(v), "adamw_small_" + tag)
        unpacked = [_unpack(p, [w[n].shape for n in names] + [e.shape for e in extra]) for p in packed]
        for k, n in enumerate(names):
            grads[n], deltas[n], new_m[n], new_v[n] = (u[k] for u in unpacked)
        return packed, unpacked

    small_update(early_parts, early_small, [], "early")
    packed_out, unpacked = small_update(late_parts, ['ffn1_norm'], [zeros_meta], "late")
    g_meta_full = unpacked[0][-1]
    grads['meta_tokens'] = lax.dynamic_index_in_dim(
        g_meta_full.reshape(N_META, N_DEV, D_MODEL // N_DEV), me, axis=1, keepdims=False)
    deltas['meta_tokens'], new_m['meta_tokens'], new_v['meta_tokens'] = adamw_plain(
        grads['meta_tokens'], w['meta_tokens'], m['meta_tokens'], v['meta_tokens'], "adamw_meta")

    finish_reduce("ffn1_out", finish_reduce("ffn1_in", packed_out[0]))

    return (loss, grad_x, *[grads[n] for n in WEIGHTS], *[deltas[n] for n in WEIGHTS],
            *[new_m[n] for n in WEIGHTS], *[new_v[n] for n in WEIGHTS])
```

```python
import functools

import jax
import jax.numpy as jnp
from jax import lax
from jax.experimental import pallas as pl
from jax.experimental.pallas import tpu as pltpu

F32 = jnp.float32
BF16 = jnp.bfloat16
MESH = pl.DeviceIdType.MESH

N_DEV = 8
D_MODEL = 1024
N_META = 16
HEAD_DIM = 64
N_KV_HEADS = 4
Q_PER_KV = 4
BLOCK = 128
KV_WIDTH = N_KV_HEADS * HEAD_DIM
SSM_GROUP = 16
SSM_WIDTH = 512
SSM_GROUPS = 32
SSM_STATE = 64
N_STATES = SSM_GROUPS * SSM_STATE
D_FF = 2816
FF_BLK = D_FF // N_DEV
IN_WIDTH = 4096
IN_BLK = IN_WIDTH // N_DEV
NORM_EPS = 1e-6
NEG_INF = -1e30
SCAN_COLS = 256
N_SCAN_BLK = N_STATES // SCAN_COLS
SUBLANES = 8
LANES = 128
MXU_WIDTH = 256
FF_BWD_COLS = MXU_WIDTH

ADAM_LR = 0.001
ADAM_B1 = 0.9
ADAM_B2 = 0.999
ADAM_EPS = 1e-08
ADAM_WD = 0.01
ADAM_STEP = 10

VMEM_BIG = 56 * 1024 * 1024


def _cp(sem=None, vmem=None):
    kw = {}
    if sem is not None:
        kw["dimension_semantics"] = sem
    if vmem is not None:
        kw["vmem_limit_bytes"] = vmem
    return pltpu.CompilerParams(**kw)


def _pcall(body, **kw):
    return pl.pallas_call(body, **kw)


def _dot(a, b):
    return jnp.dot(a, b, preferred_element_type=F32)


def _dot_nt(a, b):
    return lax.dot_general(a, b, (((1,), (1,)), ((), ())), preferred_element_type=F32)


def _dot_tn(a, b):
    return lax.dot_general(a, b, (((0,), (0,)), ((), ())), preferred_element_type=F32)


def _sigmoid(x):
    return 1.0 / (1.0 + jnp.exp(-x))


def _row_tile(rows, cap):
    best = None
    for t in range(16, min(rows, cap) + 1, 16):
        if rows % t == 0:
            best = t
    assert best is not None, rows
    return best


def _my_place():
    return lax.axis_index("x"), lax.axis_index("y"), lax.axis_index("c")


def all_gather_list(shards, after, name):
    n = len(shards)

    def body(*refs):
        ins, outs = refs[:n], refs[n + 1:2 * n + 1]
        send_sems, recv_sems, local_sems = refs[2 * n + 1:]
        x, y, c = _my_place()
        me, sibling = (x, y, c), (x, y, 1 - c)
        chips = [(1 - x, y), (x, 1 - y), (1 - x, 1 - y)]

        def blk(a, px, py, pc):
            return outs[a].at[4 * px + 2 * py + pc]

        def copy(a, k, block, to, src=None):
            return pltpu.make_async_remote_copy(
                src_ref=blk(a, *block) if src is None else src, dst_ref=blk(a, *block),
                send_sem=send_sems.at[a * 7 + k], recv_sem=recv_sems.at[a * 7 + k],
                device_id=to, device_id_type=MESH)

        mine = [pltpu.make_async_copy(ins[a], blk(a, *me), local_sems.at[a]) for a in range(n)]
        for cp in mine:
            cp.start()
        first = []
        for a in range(n):
            first.append(copy(a, 0, me, sibling, src=ins[a]))
            first += [copy(a, 1 + j, me, (*chip, c), src=ins[a]) for j, chip in enumerate(chips)]
        for cp in first:
            cp.start()
        passed = []
        for j, chip in enumerate(chips):
            for a in range(n):
                copy(a, 1 + j, (*chip, c), me).wait_recv()
                cp = copy(a, 4 + j, (*chip, c), sibling)
                cp.start()
                passed.append(cp)
        for a in range(n):
            copy(a, 0, sibling, me).wait_recv()
            for j, chip in enumerate(chips):
                copy(a, 4 + j, (*chip, 1 - c), me).wait_recv()
        for cp in first + passed:
            cp.wait_send()
        for cp in mine:
            cp.wait()

    any_spec = pl.BlockSpec(memory_space=pl.ANY)
    return _pcall(
        body, name=name,
        out_shape=[jax.ShapeDtypeStruct((N_DEV,) + s.shape, s.dtype) for s in shards],
        in_specs=[any_spec] * (n + 1), out_specs=[any_spec] * n,
        scratch_shapes=[pltpu.SemaphoreType.DMA((7 * n,)), pltpu.SemaphoreType.DMA((7 * n,)),
                        pltpu.SemaphoreType.DMA((n,))],
    )(*shards, after)


HBM_SPEC = pl.BlockSpec(memory_space=pltpu.HBM)
SEM_SPEC = pl.BlockSpec(memory_space=pltpu.SEMAPHORE)
N_PEERS = N_DEV - 1


def _related(k):
    x, y, c = _my_place()
    px = 1 - x if k & 4 else x
    py = 1 - y if k & 2 else y
    pc = 1 - c if k & 1 else c
    return (px, py, pc), 4 * px + 2 * py + pc


def _exchange_copies(srcs, lands, send_sems, recv_sems, gather):
    x, y, c = _my_place()
    me = 4 * x + 2 * y + c
    copies = []
    for a, (src, land) in enumerate(zip(srcs, lands)):
        for k in range(1, N_DEV):
            peer, d = _related(k)
            copies.append(pltpu.make_async_remote_copy(
                src_ref=src if gather else src.at[d], dst_ref=land.at[me] if gather else land.at[k],
                send_sem=send_sems.at[a * N_PEERS + k - 1], recv_sem=recv_sems.at[a * N_PEERS + k - 1],
                device_id=peer, device_id_type=MESH))
    return copies


def exchange_start(srcs, after, gather, name):
    n = len(srcs)
    land_shapes = [((N_DEV,) + s.shape) if gather else s.shape for s in srcs]

    def body(*refs):
        send_sems, recv_sems = refs[2 * n + 1], refs[2 * n + 2]
        for cp in _exchange_copies(refs[:n], refs[n:2 * n], send_sems, recv_sems, gather):
            cp.start()
        token = refs[-1]
        token[...] = jnp.zeros_like(token)

    sems = pltpu.SemaphoreType.DMA((n * N_PEERS,))
    lands = [pltpu.with_memory_space_constraint(lax.empty(shape, s.dtype), pltpu.HBM) for shape, s in zip(land_shapes, srcs)]
    out = _pcall(
        body, name=name,
        out_shape=(sems, sems, *[pltpu.HBM(s.shape, s.dtype) for s in srcs],
                   *[pltpu.HBM(shape, s.dtype) for shape, s in zip(land_shapes, srcs)],
                   jax.ShapeDtypeStruct((SUBLANES, LANES), F32)),
        in_specs=[HBM_SPEC] * (2 * n) + [pl.BlockSpec(memory_space=pl.ANY)],
        out_specs=(SEM_SPEC, SEM_SPEC, *[HBM_SPEC] * (2 * n), pl.BlockSpec(memory_space=pltpu.VMEM)),
        input_output_aliases={i: 2 + i for i in range(2 * n)},
        compiler_params=pltpu.CompilerParams(has_side_effects=pltpu.SideEffectType.DATAFLOW_SIDE_EFFECTING),
    )(*[pltpu.with_memory_space_constraint(s, pltpu.HBM) for s in srcs], *lands, after)
    return out[0], out[1], list(out[2:2 + n]), list(out[2 + n:2 + 2 * n]), out[-1]


def exchange_wait(send_sems, recv_sems, srcs, lands, after, gather, name):
    n = len(srcs)

    def body(*refs):
        for cp in _exchange_copies(refs[:n], refs[n:2 * n], refs[2 * n], refs[2 * n + 1], gather):
            cp.wait_send()
            cp.wait_recv()

    out = _pcall(
        body, name=name,
        out_shape=(*[pltpu.HBM(s.shape, s.dtype) for s in srcs], *[pltpu.HBM(z.shape, z.dtype) for z in lands]),
        in_specs=[HBM_SPEC] * (2 * n) + [SEM_SPEC, SEM_SPEC, pl.BlockSpec(memory_space=pl.ANY)],
        out_specs=tuple([HBM_SPEC] * (2 * n)),
        input_output_aliases={i: i for i in range(2 * n)},
        compiler_params=pltpu.CompilerParams(has_side_effects=pltpu.SideEffectType.DATAFLOW_SIDE_EFFECTING),
    )(*srcs, *lands, send_sems, recv_sems, after)
    return list(out[:n]), list(out[n:])


def adamw_exchanged(me, g, land, w, m, v, name):
    rows, cols = w.shape
    tr = _row_tile(rows, 256)

    def body(me_ref, g_ref, land_ref, w_ref, m_ref, v_ref, go_ref, d_ref, mo_ref, vo_ref):
        grad = g_ref[...].astype(F32)
        for k in range(1, N_DEV):
            grad = grad + land_ref[k].astype(F32)
        delta, m_new, v_new = _adam_math(w_ref[...], grad, m_ref[...], v_ref[...])
        go_ref[...] = grad
        d_ref[...] = delta
        mo_ref[...] = m_new
        vo_ref[...] = v_new

    tile = pl.BlockSpec((tr, cols), lambda r, ix: (r, 0))
    out = jax.ShapeDtypeStruct((rows, cols), F32)
    return _pcall(
        body, name=name, out_shape=[out] * 4,
        grid_spec=pltpu.PrefetchScalarGridSpec(
            num_scalar_prefetch=1, grid=(rows // tr,),
            in_specs=[pl.BlockSpec((None, tr, cols), lambda r, ix: (ix[0], r, 0)),
                      pl.BlockSpec((N_DEV, tr, cols), lambda r, ix: (0, r, 0)), tile, tile, tile],
            out_specs=[tile] * 4),
        compiler_params=_cp(("arbitrary",)),
    )(me, g, land, w, m, v)


def _adam_math(w, g, m, v):
    m = ADAM_B1 * m + (1.0 - ADAM_B1) * g
    v = ADAM_B2 * v + (1.0 - ADAM_B2) * (g * g)
    m_hat = m / (1.0 - ADAM_B1 ** ADAM_STEP)
    v_hat = v / (1.0 - ADAM_B2 ** ADAM_STEP)
    delta = -ADAM_LR * (m_hat / (jnp.sqrt(v_hat) + ADAM_EPS) + ADAM_WD * w)
    return delta, m, v


def adamw_small(parts, w, m, v, name):
    _, rows, cols = parts.shape

    def body(p_ref, w_ref, m_ref, v_ref, go_ref, d_ref, mo_ref, vo_ref):
        grad = p_ref[0]
        for k in range(1, N_DEV):
            grad = grad + p_ref[k]
        delta, m_new, v_new = _adam_math(w_ref[...], grad, m_ref[...], v_ref[...])
        go_ref[...] = grad
        d_ref[...] = delta
        mo_ref[...] = m_new
        vo_ref[...] = v_new

    out = jax.ShapeDtypeStruct((rows, cols), F32)
    return _pcall(body, name=name, out_shape=[out] * 4, compiler_params=_cp(vmem=VMEM_BIG))(parts, w, m, v)


def adamw_plain(g, w, m, v, name):
    def body(g_ref, w_ref, m_ref, v_ref, d_ref, mo_ref, vo_ref):
        delta, m_new, v_new = _adam_math(w_ref[...], g_ref[...], m_ref[...], v_ref[...])
        d_ref[...] = delta
        mo_ref[...] = m_new
        vo_ref[...] = v_new

    out = jax.ShapeDtypeStruct(w.shape, F32)
    return _pcall(body, name=name, out_shape=[out] * 3)(g, w, m, v)


def _rms_fwd(x, g):
    r = lax.rsqrt(jnp.mean(x * x, axis=-1, keepdims=True) + NORM_EPS)
    return x * r * g


def _rms_bwd(x, g, dy):
    r = lax.rsqrt(jnp.mean(x * x, axis=-1, keepdims=True) + NORM_EPS)
    xh = x * r
    t = dy * g
    dx = r * (t - xh * jnp.mean(t * xh, axis=-1, keepdims=True))
    return dx, jnp.sum(dy * xh, axis=0, keepdims=True)


def _accumulate(ref, val, first):
    @pl.when(first)
    def _():
        ref[...] = val

    @pl.when(jnp.logical_not(first))
    def _():
        ref[...] += val


def _col_chunks(width):
    return [(c0, min(MXU_WIDTH, width - c0)) for c0 in range(0, width, MXU_WIDTH)]


ANY_SPEC = pl.BlockSpec(memory_space=pl.ANY)


def ffn_forward(h, norm, w1, w3, w2, after, tm, name, meta=None):
    if meta is None:
        t_rows = h.shape[0]
        h_spec = pl.BlockSpec((tm, D_MODEL), lambda i: (i, 0))
    else:
        tiles = (h.shape[1] + N_META) // tm
        t_rows = h.shape[0] * tiles * tm
        h_spec = pl.BlockSpec((None, tm, D_MODEL), lambda i: (i // tiles, i % tiles, 0))

    def body(h_ref, g_ref, w1_ref, w3_ref, w2_ref, _, *rest):
        if meta is None:
            out_ref, hn_ref, a_ref, b_ref, hid_ref = rest
            h_in = h_ref[...]
        else:
            meta_ref, out_ref, hn_ref, a_ref, b_ref, h0_ref, hid_ref = rest
            h_in = h_ref[...]
            with_meta = jnp.concatenate([h_in[:tm - N_META], meta_ref[...]], axis=0)
            h_in = jnp.where(pl.program_id(0) % tiles == tiles - 1, with_meta, h_in)
            h0_ref[...] = h_in
        hn = _rms_fwd(h_in, g_ref[...]).astype(BF16)
        hn_ref[...] = hn
        for c0, cw in _col_chunks(D_FF):
            a = _dot_nt(hn, w1_ref[c0:c0 + cw, :])
            b = _dot_nt(hn, w3_ref[c0:c0 + cw, :])
            a_ref[:, c0:c0 + cw] = a.astype(BF16)
            b_ref[:, c0:c0 + cw] = b.astype(BF16)
            hid_ref[:, c0:c0 + cw] = (a * _sigmoid(a) * b).astype(BF16)
        out_ref[...] = h_in + 0.5 * _dot(hid_ref[...], w2_ref[...])

    row = pl.BlockSpec((tm, D_MODEL), lambda i: (i, 0))
    hid_blk = pl.BlockSpec((tm, D_FF), lambda i: (i, 0))
    weight = _resident((D_FF, D_MODEL))
    wide = jax.ShapeDtypeStruct((t_rows, D_MODEL), F32)
    extra_in = [] if meta is None else [meta]
    return _pcall(
        body, name=name, grid=(t_rows // tm,),
        in_specs=[h_spec, pl.BlockSpec((1, D_MODEL), lambda i: (0, 0)), weight, weight, weight, ANY_SPEC]
        + [pl.BlockSpec((N_META, D_MODEL), lambda i: (0, 0))] * len(extra_in),
        out_specs=[row, row, hid_blk, hid_blk] + [row] * len(extra_in),
        out_shape=[wide, jax.ShapeDtypeStruct((t_rows, D_MODEL), BF16),
                   jax.ShapeDtypeStruct((t_rows, D_FF), BF16), jax.ShapeDtypeStruct((t_rows, D_FF), BF16)]
        + [wide] * len(extra_in),
        scratch_shapes=[pltpu.VMEM((tm, D_FF), BF16)],
        compiler_params=_cp(("arbitrary",), VMEM_BIG),
    )(h, norm, w1, w3, w2, after, *extra_in)


def _resident(shape):
    return pl.BlockSpec(shape, lambda *_: (0,) * len(shape), pipeline_mode=pl.Buffered(1))


def ffn_backward_hidden(dh, a, b, w2, after, tm, name):
    t_rows = dh.shape[0]

    def body(dh_ref, a_ref, b_ref, w2_ref, _, da_ref, db_ref, dhb_ref):
        dhb = (0.5 * dh_ref[...]).astype(BF16)
        dhb_ref[...] = dhb
        for c0, cw in _col_chunks(D_FF):
            dhid = _dot_nt(dhb, w2_ref[c0:c0 + cw, :])
            av = a_ref[:, c0:c0 + cw].astype(F32)
            bv = b_ref[:, c0:c0 + cw].astype(F32)
            s = _sigmoid(av)
            da_ref[:, c0:c0 + cw] = (dhid * bv * (s * (1.0 + av * (1.0 - s)))).astype(BF16)
            db_ref[:, c0:c0 + cw] = (dhid * (av * s)).astype(BF16)

    hid = pl.BlockSpec((tm, D_FF), lambda i: (i, 0))
    row = pl.BlockSpec((tm, D_MODEL), lambda i: (i, 0))
    return _pcall(
        body, name=name, grid=(t_rows // tm,),
        in_specs=[row, hid, hid, _resident((D_FF, D_MODEL)), ANY_SPEC],
        out_specs=[hid, hid, row],
        out_shape=[jax.ShapeDtypeStruct((t_rows, D_FF), BF16), jax.ShapeDtypeStruct((t_rows, D_FF), BF16),
                   jax.ShapeDtypeStruct((t_rows, D_MODEL), BF16)],
        compiler_params=_cp(("arbitrary",), VMEM_BIG),
    )(dh, a, b, w2, after)


def ffn_backward_input(dh, h, norm, da, db, w1, w3, after, tm, name, examples=None):
    t_rows = h.shape[0]

    def body(dh_ref, h_ref, g_ref, da_ref, db_ref, w1_ref, w3_ref, _, dhin_ref, *rest):
        dg_ref = rest[-1]
        dhn = _dot(da_ref[...], w1_ref[...]) + _dot(db_ref[...], w3_ref[...])
        dx, dg = _rms_bwd(h_ref[...], g_ref[...], dhn)
        dhin = dh_ref[...] + dx
        dhin_ref[...] = dhin
        _accumulate(dg_ref, dg, pl.program_id(0) == 0)
        if examples is not None:
            @pl.when(pl.program_id(0) % tiles == tiles - 1)
            def _():
                rest[0][...] = dhin[tm - N_META:, :]

    row = pl.BlockSpec((tm, D_MODEL), lambda i: (i, 0))
    vec = pl.BlockSpec((1, D_MODEL), lambda i: (0, 0))
    hid = pl.BlockSpec((tm, D_FF), lambda i: (i, 0))
    if examples is None:
        out_specs = [row, vec]
        out_shape = [jax.ShapeDtypeStruct((t_rows, D_MODEL), F32), jax.ShapeDtypeStruct((1, D_MODEL), F32)]
    else:
        n_b, seq = examples
        tiles = (seq + N_META) // tm
        out_specs = [pl.BlockSpec((None, tm, D_MODEL), lambda i: (i // tiles, i % tiles, 0)),
                     pl.BlockSpec((None, N_META, D_MODEL), lambda i: (i // tiles, 0, 0)), vec]
        out_shape = [jax.ShapeDtypeStruct((n_b, seq, D_MODEL), F32), jax.ShapeDtypeStruct((n_b, N_META, D_MODEL), F32),
                     jax.ShapeDtypeStruct((1, D_MODEL), F32)]
    return _pcall(
        body, name=name, grid=(t_rows // tm,),
        in_specs=[row, row, vec, hid, hid, _resident((D_FF, D_MODEL)), _resident((D_FF, D_MODEL)), ANY_SPEC],
        out_specs=out_specs, out_shape=out_shape,
        compiler_params=_cp(("arbitrary",), VMEM_BIG),
    )(dh, h, norm, da, db, w1, w3, after)


def ffn_backward_weights(hn, dh, a, b, da, db, tm, tn, name):
    t_rows = hn.shape[0]
    ni = t_rows // tm
    kc = _row_tile(tm, 688)

    def body(hn_ref, dh_ref, a_ref, b_ref, da_ref, db_ref, dw1_ref, dw3_ref, dw2_ref, acc1, acc3, acc2):
        i = pl.program_id(1)
        parts = None
        for r0 in range(0, tm, kc):
            rows = slice(r0, r0 + kc)
            hn_v = hn_ref[rows, :]
            av = a_ref[rows, :].astype(F32)
            hid = (av * _sigmoid(av) * b_ref[rows, :].astype(F32)).astype(BF16)
            new = (_dot_tn(hn_v, da_ref[rows, :]), _dot_tn(hn_v, db_ref[rows, :]), _dot_tn(dh_ref[rows, :], hid))
            parts = new if parts is None else tuple(p + q for p, q in zip(parts, new))
        _accumulate(acc1, parts[0], i == 0)
        _accumulate(acc3, parts[1], i == 0)
        _accumulate(acc2, parts[2], i == 0)

        @pl.when(i == ni - 1)
        def _():
            dw1_ref[...] = acc1[...].T.astype(BF16)
            dw3_ref[...] = acc3[...].T.astype(BF16)
            dw2_ref[...] = acc2[...].T.astype(BF16)

    row = pl.BlockSpec((tm, D_MODEL), lambda j, i: (i, 0))
    hid_blk = pl.BlockSpec((tm, tn), lambda j, i: (i, j))
    w_row = pl.BlockSpec((tn, D_MODEL), lambda j, i: (j, 0))
    out = jax.ShapeDtypeStruct((D_FF, D_MODEL), BF16)
    return _pcall(
        body, name=name, grid=(D_FF // tn, ni),
        in_specs=[row, row, hid_blk, hid_blk, hid_blk, hid_blk],
        out_specs=[w_row, w_row, w_row], out_shape=[out, out, out],
        scratch_shapes=[pltpu.VMEM((D_MODEL, tn), F32)] * 3,
        compiler_params=_cp(("arbitrary", "arbitrary"), VMEM_BIG),
    )(hn, dh, a, b, da, db)


def _weight_gradient_call(body, wide, hidden, after, n_out, tm, tn, name):
    ni = wide.shape[0] // tm
    row = pl.BlockSpec((tm, D_MODEL), lambda j, i: (i, 0))
    hid_blk = pl.BlockSpec((tm, tn), lambda j, i: (i, j))
    w_row = pl.BlockSpec((tn, D_MODEL), lambda j, i: (j, 0))
    out = jax.ShapeDtypeStruct((D_FF, D_MODEL), BF16)
    return _pcall(
        body, name=name, grid=(D_FF // tn, ni),
        in_specs=[row] + [hid_blk] * len(hidden) + [ANY_SPEC],
        out_specs=[w_row] * n_out, out_shape=[out] * n_out,
        scratch_shapes=[pltpu.VMEM((D_MODEL, tn), F32)] * n_out,
        compiler_params=_cp(("arbitrary", "arbitrary"), VMEM_BIG),
    )(wide, *hidden, after)


def ffn_backward_weights_in(hn, da, db, tm, tn, name):
    ni = hn.shape[0] // tm
    kc = _row_tile(tm, 688)

    def body(hn_ref, da_ref, db_ref, _, dw1_ref, dw3_ref, acc1, acc3):
        i = pl.program_id(1)
        parts = None
        for r0 in range(0, tm, kc):
            rows = slice(r0, r0 + kc)
            hn_v = hn_ref[rows, :]
            new = (_dot_tn(hn_v, da_ref[rows, :]), _dot_tn(hn_v, db_ref[rows, :]))
            parts = new if parts is None else tuple(p + q for p, q in zip(parts, new))
        _accumulate(acc1, parts[0], i == 0)
        _accumulate(acc3, parts[1], i == 0)

        @pl.when(i == ni - 1)
        def _():
            dw1_ref[...] = acc1[...].T.astype(BF16)
            dw3_ref[...] = acc3[...].T.astype(BF16)

    return _weight_gradient_call(body, hn, [da, db], da, 2, tm, tn, name)


def ffn_backward_weights_out(dh, a, b, after, tm, tn, name):
    ni = dh.shape[0] // tm
    kc = _row_tile(tm, 688)

    def body(dh_ref, a_ref, b_ref, _, dw2_ref, acc2):
        i = pl.program_id(1)
        part = None
        for r0 in range(0, tm, kc):
            rows = slice(r0, r0 + kc)
            av = a_ref[rows, :].astype(F32)
            hid = (av * _sigmoid(av) * b_ref[rows, :].astype(F32)).astype(BF16)
            new = _dot_tn(dh_ref[rows, :], hid)
            part = new if part is None else part + new
        _accumulate(acc2, part, i == 0)

        @pl.when(i == ni - 1)
        def _():
            dw2_ref[...] = acc2[...].T.astype(BF16)

    return _weight_gradient_call(body, dh, [a, b], after, 1, tm, tn, name)[0]


def mix_forward(h, norm, wing, tm, name):
    t_rows = h.shape[0]

    def body(h_ref, g_ref, w_ref, hn_ref, p_ref):
        hn = _rms_fwd(h_ref[...], g_ref[...]).astype(BF16)
        hn_ref[...] = hn
        for j in range(N_DEV):
            p_ref[:, j * IN_BLK:(j + 1) * IN_BLK] = _dot(hn, w_ref[j]).astype(BF16)

    row = pl.BlockSpec((tm, D_MODEL), lambda i: (i, 0))
    return _pcall(
        body, name=name, grid=(t_rows // tm,),
        in_specs=[row, pl.BlockSpec((1, D_MODEL), lambda i: (0, 0)),
                  pl.BlockSpec((N_DEV, D_MODEL, IN_BLK), lambda i: (0, 0, 0))],
        out_specs=[row, pl.BlockSpec((tm, IN_WIDTH), lambda i: (i, 0))],
        out_shape=[jax.ShapeDtypeStruct((t_rows, D_MODEL), BF16), jax.ShapeDtypeStruct((t_rows, IN_WIDTH), BF16)],
        compiler_params=_cp(("arbitrary",), VMEM_BIG),
    )(h, norm, wing)


def mix_backward_act(dh, h, norm, dproj, w_in_full, tm, name):
    t_rows = h.shape[0]

    def body(dh_ref, h_ref, g_ref, dp_ref, w_ref, dhin_ref, dg_ref):
        dx, dg = _rms_bwd(h_ref[...], g_ref[...], _dot_nt(dp_ref[...], w_ref[...]))
        dhin_ref[...] = dh_ref[...] + dx
        _accumulate(dg_ref, dg, pl.program_id(0) == 0)

    row = pl.BlockSpec((tm, D_MODEL), lambda i: (i, 0))
    vec = pl.BlockSpec((1, D_MODEL), lambda i: (0, 0))
    return _pcall(
        body, name=name, grid=(t_rows // tm,),
        in_specs=[row, row, vec, pl.BlockSpec((tm, IN_WIDTH), lambda i: (i, 0)), _resident((D_MODEL, IN_WIDTH))],
        out_specs=[row, vec],
        out_shape=[jax.ShapeDtypeStruct((t_rows, D_MODEL), F32), jax.ShapeDtypeStruct((1, D_MODEL), F32)],
        compiler_params=_cp(("arbitrary",), VMEM_BIG),
    )(dh, h, norm, dproj, w_in_full)


def mix_backward_weights(hn, dproj, tm, name):
    t_rows = hn.shape[0]
    ni = t_rows // tm
    per_step = 2

    kc = _row_tile(tm, 688)

    def body(hn_ref, dp_ref, dw_ref, acc):
        i = pl.program_id(1)
        part = functools.reduce(lambda u, w: u + w, [_dot_tn(hn_ref[r0:r0 + kc, :], dp_ref[r0:r0 + kc, :])
                                                    for r0 in range(0, tm, kc)])
        _accumulate(acc, part, i == 0)

        @pl.when(i == ni - 1)
        def _():
            for k in range(per_step):
                dw_ref[k] = acc[:, k * IN_BLK:(k + 1) * IN_BLK].astype(BF16)

    return _pcall(
        body, name=name, grid=(N_DEV // per_step, ni),
        in_specs=[pl.BlockSpec((tm, D_MODEL), lambda j, i: (i, 0)),
                  pl.BlockSpec((tm, per_step * IN_BLK), lambda j, i: (i, j))],
        out_specs=pl.BlockSpec((per_step, D_MODEL, IN_BLK), lambda j, i: (j, 0, 0)),
        out_shape=jax.ShapeDtypeStruct((N_DEV, D_MODEL, IN_BLK), BF16),
        scratch_shapes=[pltpu.VMEM((D_MODEL, per_step * IN_BLK), F32)],
        compiler_params=_cp(("arbitrary", "arbitrary"), VMEM_BIG),
    )(hn, dproj)


GELU_C = 0.7978845608028654
GELU_K = 0.044715


def _gelu(x):
    return 0.5 * x * (1.0 + jnp.tanh(GELU_C * (x + GELU_K * (x * x * x))))


def _gelu_and_grad(x):
    th = jnp.tanh(GELU_C * (x + GELU_K * (x * x * x)))
    val = 0.5 * x * (1.0 + th)
    grad = 0.5 * (1.0 + th) + 0.5 * x * (1.0 - th * th) * (GELU_C * (1.0 + 3.0 * GELU_K * (x * x)))
    return val, grad


def merge_forward(h, yraw, attn, proj, glu_a, glu_b, w_out, tm, name):
    t_rows = h.shape[0]

    def body(h_ref, y_ref, at_ref, gate_ref, a_ref, b_ref, wo_ref, out_ref):
        y = _gelu(y_ref[...]).astype(BF16)
        ssm = _dot(y, a_ref[...]) * _sigmoid(_dot(y, b_ref[...]))
        ga = gate_ref[:, :D_MODEL].astype(F32)
        gs = gate_ref[:, D_MODEL:].astype(F32)
        merged = _sigmoid(ga) * at_ref[...].astype(F32) + _sigmoid(gs) * ssm
        out_ref[...] = h_ref[...] + _dot(merged.astype(BF16), wo_ref[...])

    row = pl.BlockSpec((tm, D_MODEL), lambda i: (i, 0))
    glu = pl.BlockSpec((SSM_WIDTH, D_MODEL), lambda i: (0, 0))
    return _pcall(
        body, name=name, grid=(t_rows // tm,),
        in_specs=[row, pl.BlockSpec((tm, SSM_WIDTH), lambda i: (i, 0)), row,
                  pl.BlockSpec((tm, 2 * D_MODEL), lambda i: (i, 1)), glu, glu,
                  pl.BlockSpec((D_MODEL, D_MODEL), lambda i: (0, 0))],
        out_specs=row, out_shape=jax.ShapeDtypeStruct((t_rows, D_MODEL), F32),
        compiler_params=_cp(("arbitrary",), VMEM_BIG),
    )(h, yraw, attn, proj, glu_a, glu_b, w_out)


def merge_backward(dh, yraw, attn, proj, glu_a, glu_b, w_out, after, tm, name):
    t_rows = dh.shape[0]

    def body(dh_ref, y_ref, at_ref, gate_ref, a_ref, b_ref, wo_ref, _,
             dat_ref, dy_ref, dgate_ref, d16_ref, mg_ref, y16_ref, dya_ref, dyb_ref):
        d16 = dh_ref[...].astype(BF16)
        d16_ref[...] = d16
        gel, dgel = _gelu_and_grad(y_ref[...].astype(F32))
        y16 = gel.astype(BF16)
        y16_ref[...] = y16
        dy = None
        for c0, cw in _col_chunks(D_MODEL):
            cols = slice(c0, c0 + cw)
            gcols = slice(D_MODEL + c0, D_MODEL + c0 + cw)
            dmerged = _dot_nt(d16, wo_ref[cols, :])
            ya = _dot(y16, a_ref[:, cols])
            sb = _sigmoid(_dot(y16, b_ref[:, cols]))
            ssm = ya * sb
            sa = _sigmoid(gate_ref[:, cols].astype(F32))
            ss = _sigmoid(gate_ref[:, gcols].astype(F32))
            attn_v = at_ref[:, cols].astype(F32)
            mg_ref[:, cols] = (sa * attn_v + ss * ssm).astype(BF16)
            dat_ref[:, cols] = (dmerged * sa).astype(BF16)
            dgate_ref[:, cols] = (dmerged * attn_v * sa * (1.0 - sa)).astype(BF16)
            dgate_ref[:, gcols] = (dmerged * ssm * ss * (1.0 - ss)).astype(BF16)
            dssm = dmerged * ss
            dya = (dssm * sb).astype(BF16)
            dyb = (dssm * ya * sb * (1.0 - sb)).astype(BF16)
            dya_ref[:, cols] = dya
            dyb_ref[:, cols] = dyb
            part = _dot_nt(dya, a_ref[:, cols]) + _dot_nt(dyb, b_ref[:, cols])
            dy = part if dy is None else dy + part
        dy_ref[...] = (dy * dgel).astype(BF16)

    row = pl.BlockSpec((tm, D_MODEL), lambda i: (i, 0))
    ssm_row = pl.BlockSpec((tm, SSM_WIDTH), lambda i: (i, 0))
    gates = pl.BlockSpec((tm, 2 * D_MODEL), lambda i: (i, 1))
    wide = jax.ShapeDtypeStruct((t_rows, D_MODEL), BF16)
    narrow = jax.ShapeDtypeStruct((t_rows, SSM_WIDTH), BF16)
    return _pcall(
        body, name=name, grid=(t_rows // tm,),
        in_specs=[row, ssm_row, row, gates, _resident((SSM_WIDTH, D_MODEL)), _resident((SSM_WIDTH, D_MODEL)),
                  _resident((D_MODEL, D_MODEL)), ANY_SPEC],
        out_specs=[row, ssm_row, gates, row, row, ssm_row, row, row],
        out_shape=[wide, narrow, jax.ShapeDtypeStruct((t_rows, IN_WIDTH), BF16), wide, wide, narrow, wide, wide],
        compiler_params=_cp(("arbitrary",), VMEM_BIG),
    )(dh, yraw, attn, proj, glu_a, glu_b, w_out, after)


def merge_backward_weights(d16, merged, y16, dya, dyb, tm, name):
    t_rows = d16.shape[0]

    def body(d_ref, mg_ref, y_ref, dya_ref, dyb_ref, dwo_ref, da_ref, db_ref):
        first = pl.program_id(0) == 0
        y16 = y_ref[...]
        _accumulate(dwo_ref, _dot_tn(mg_ref[...], d_ref[...]), first)
        _accumulate(da_ref, _dot_tn(y16, dya_ref[...]), first)
        _accumulate(db_ref, _dot_tn(y16, dyb_ref[...]), first)

    row = pl.BlockSpec((tm, D_MODEL), lambda i: (i, 0))
    ssm_row = pl.BlockSpec((tm, SSM_WIDTH), lambda i: (i, 0))
    glu = pl.BlockSpec((SSM_WIDTH, D_MODEL), lambda i: (0, 0))
    wo = pl.BlockSpec((D_MODEL, D_MODEL), lambda i: (0, 0))
    return _pcall(
        body, name=name, grid=(t_rows // tm,),
        in_specs=[row, row, ssm_row, row, row], out_specs=[wo, glu, glu],
        out_shape=[jax.ShapeDtypeStruct((D_MODEL, D_MODEL), F32), jax.ShapeDtypeStruct((SSM_WIDTH, D_MODEL), F32),
                   jax.ShapeDtypeStruct((SSM_WIDTH, D_MODEL), F32)],
        compiler_params=_cp(("arbitrary",), VMEM_BIG),
    )(d16, merged, y16, dya, dyb)


def final_loss_backward(h, target, norm, seq, tm, name):
    t_rows = h.shape[0]
    tiles_per_example = (seq + N_META) // tm

    def body(h_ref, t_ref, g_ref, dh_ref, loss_ref, dg_ref):
        i = pl.program_id(0)
        x = h_ref[...]
        g = g_ref[...]
        r = lax.rsqrt(jnp.mean(x * x, axis=-1, keepdims=True) + NORM_EPS)
        xh = x * r
        pos = lax.broadcasted_iota(jnp.int32, (tm, 1), 0) + (i % tiles_per_example) * tm
        diff = jnp.where(pos < seq, xh * g - t_ref[...], 0.0)
        part = 0.5 * jnp.sum(jnp.sum(diff * diff, axis=-1, keepdims=True), axis=0, keepdims=True) / D_MODEL
        dy = diff / D_MODEL
        t = dy * g
        dh_ref[...] = r * (t - xh * jnp.mean(t * xh, axis=-1, keepdims=True))
        _accumulate(loss_ref, jnp.broadcast_to(part, (1, LANES)), i == 0)
        _accumulate(dg_ref, jnp.sum(dy * xh, axis=0, keepdims=True), i == 0)

    row = pl.BlockSpec((tm, D_MODEL), lambda i: (i, 0))
    vec = pl.BlockSpec((1, D_MODEL), lambda i: (0, 0))
    per_example = pl.BlockSpec((None, tm, D_MODEL), lambda i: (i // tiles_per_example, i % tiles_per_example, 0))
    return _pcall(
        body, name=name, grid=(t_rows // tm,),
        in_specs=[row, per_example, vec],
        out_specs=[row, pl.BlockSpec((1, LANES), lambda i: (0, 0)), vec],
        out_shape=[jax.ShapeDtypeStruct((t_rows, D_MODEL), F32), jax.ShapeDtypeStruct((1, LANES), F32),
                   jax.ShapeDtypeStruct((1, D_MODEL), F32)],
        compiler_params=_cp(("arbitrary",), VMEM_BIG),
    )(h, target, norm)


ATTN_SCALE = HEAD_DIM ** -0.5
STACK_HEADS = (0, 2, 1, 3)
META_PAD = LANES - N_META


def _lane_half(shape, hf):
    lane = lax.broadcasted_iota(jnp.int32, shape, 1)
    return (lane < HEAD_DIM) if hf == 0 else (lane >= HEAD_DIM)


def _kv_variants(ref, rows, kh, pad_rows=0):
    tile = kh // 2
    t = ref[rows, tile * LANES:(tile + 1) * LANES].astype(F32)
    swapped = pltpu.roll(t, HEAD_DIM, axis=1)
    at_low, at_high = (t, swapped) if kh % 2 == 0 else (swapped, t)
    lo = jnp.where(_lane_half(t.shape, 0), at_low, 0.0).astype(BF16)
    hi = jnp.where(_lane_half(t.shape, 1), at_high, 0.0).astype(BF16)
    if pad_rows:
        zeros = jnp.zeros((pad_rows, LANES), BF16)
        lo, hi = jnp.concatenate([lo, zeros], axis=0), jnp.concatenate([hi, zeros], axis=0)
    return lo, hi


def _key_tiles(ref, key_rows, kh):
    return [_kv_variants(ref, r, kh, META_PAD if i == len(key_rows) - 1 else 0) for i, r in enumerate(key_rows)]


def _to_kv_lanes(lo, hi, kh):
    lo = jnp.where(_lane_half(lo.shape, 0), lo, 0.0)
    hi = jnp.where(_lane_half(hi.shape, 1), hi, 0.0)
    if kh % 2 == 0:
        return lo + pltpu.roll(hi, HEAD_DIM, axis=1)
    return pltpu.roll(lo, HEAD_DIM, axis=1) + hi


def _stacked(ref, rows, kh):
    col = kh * 2 * LANES
    return jnp.concatenate([ref[rows, col:col + LANES], ref[rows, col + LANES:col + 2 * LANES]], axis=0)


def _sink_column(sink_ref, kh, nq):
    row = lax.broadcasted_iota(jnp.int32, (4 * nq, 1), 0)
    col = jnp.zeros((4 * nq, 1), F32)
    for quarter, g in enumerate(STACK_HEADS):
        col = jnp.where(row // nq == quarter, sink_ref[0, kh * Q_PER_KV + g], col)
    return col


def _softmax_parts(qs, key_tiles, masks, sink):
    scores = []
    for (k_lo, k_hi), mask in zip(key_tiles, masks):
        s = jnp.concatenate([_dot_nt(qs, k_lo), _dot_nt(qs, k_hi)], axis=0) * ATTN_SCALE
        scores.append(s if mask is None else jnp.where(mask, s, NEG_INF))
    m = jnp.maximum(_row_reduce(scores, jnp.maximum, jnp.max), sink)
    probs = [jnp.exp(s - m) for s in scores]
    e_sink = jnp.exp(sink - m)
    den = _row_sums(probs) + e_sink
    return probs, 1.0 / den, e_sink


def _row_reduce(tiles, combine, reduce):
    chunks = [t[:, c:c + LANES] for t in tiles for c in range(0, t.shape[-1], LANES)]
    return reduce(functools.reduce(combine, chunks), axis=-1, keepdims=True)


def _row_sums(tiles):
    return _row_reduce(tiles, lambda u, w: u + w, jnp.sum)


def _band_mask(nq, first):
    keys = BLOCK if first else 2 * BLOCK
    qi = lax.broadcasted_iota(jnp.int32, (4 * nq, keys), 0) % nq
    kj = lax.broadcasted_iota(jnp.int32, (4 * nq, keys), 1)
    if first:
        return kj <= qi
    return jnp.logical_and(kj > qi, kj <= qi + BLOCK)


def _meta_mask(nq, causal):
    qi = lax.broadcasted_iota(jnp.int32, (4 * nq, LANES), 0) % nq
    kj = lax.broadcasted_iota(jnp.int32, (4 * nq, LANES), 1)
    return jnp.logical_and(kj < N_META, kj <= qi) if causal else kj < N_META


def _attention_schedule(seq, queries, carry):
    meta_rows = pl.ds(seq, N_META)
    meta_ok = _meta_mask(BLOCK, False)
    carry = queries(pl.ds(0, BLOCK), BLOCK, [pl.ds(0, BLOCK), meta_rows], [_band_mask(BLOCK, True), meta_ok], carry)

    def block(n, c):
        r0 = pl.multiple_of(n * BLOCK, BLOCK)
        p0 = pl.multiple_of((n - 1) * BLOCK, BLOCK)
        return queries(pl.ds(r0, BLOCK), BLOCK, [pl.ds(p0, 2 * BLOCK), meta_rows], [_band_mask(BLOCK, False), meta_ok], c)

    carry = lax.fori_loop(1, seq // BLOCK, block, carry)
    return queries(meta_rows, N_META, [meta_rows], [_meta_mask(N_META, True)], carry)


def attention_forward(proj3, sinks, seq, name):
    n_b, n_l, _ = proj3.shape

    def body(sink_ref, q_ref, k_ref, v_ref, o_ref):
        def queries(q_rows, nq, key_rows, masks, carry):
            for kh in range(N_KV_HEADS):
                ks = _key_tiles(k_ref, key_rows, kh)
                vs = _key_tiles(v_ref, key_rows, kh)
                qs = _stacked(q_ref, q_rows, kh)
                probs, inv, _ = _softmax_parts(qs, ks, masks, _sink_column(sink_ref, kh, nq))
                probs = [p.astype(BF16) for p in probs]
                o_lo = functools.reduce(lambda u, w: u + w, [_dot(p[:2 * nq], v_lo) for p, (v_lo, _) in zip(probs, vs)])
                o_hi = functools.reduce(lambda u, w: u + w, [_dot(p[2 * nq:], v_hi) for p, (_, v_hi) in zip(probs, vs)])
                out = (o_lo * inv[:2 * nq] + o_hi * inv[2 * nq:]).astype(BF16)
                col = kh * 2 * LANES
                o_ref[q_rows, col:col + LANES] = out[:nq]
                o_ref[q_rows, col + LANES:col + 2 * LANES] = out[nq:]
            return carry

        _attention_schedule(seq, queries, 0)

    return _pcall(
        body, name=name, grid=(n_b,),
        in_specs=[pl.BlockSpec(memory_space=pltpu.SMEM),
                  pl.BlockSpec((None, n_l, D_MODEL), lambda b: (b, 0, 0)),
                  pl.BlockSpec((None, n_l, KV_WIDTH), lambda b: (b, 0, D_MODEL // KV_WIDTH)),
                  pl.BlockSpec((None, n_l, KV_WIDTH), lambda b: (b, 0, D_MODEL // KV_WIDTH + 1))],
        out_specs=pl.BlockSpec((None, n_l, D_MODEL), lambda b: (b, 0, 0)),
        out_shape=jax.ShapeDtypeStruct((n_b, n_l, D_MODEL), BF16),
        compiler_params=_cp(("arbitrary",), VMEM_BIG),
    )(sinks, proj3, proj3, proj3)


def attention_backward(proj3, dattn3, dproj3, sinks, after, seq, name):
    n_b, n_l, _ = proj3.shape
    qkv_width = D_MODEL + 2 * KV_WIDTH

    def body(sink_ref, q_ref, k_ref, v_ref, do_ref, _, __, dqkv_ref, dsink_ref, dk_ref, dv_ref):
        dk_ref[...] = jnp.zeros_like(dk_ref)
        dv_ref[...] = jnp.zeros_like(dv_ref)
        sub = lax.broadcasted_iota(jnp.int32, (SUBLANES, LANES), 0)
        lane = lax.broadcasted_iota(jnp.int32, (SUBLANES, LANES), 1)

        def queries(q_rows, nq, key_rows, masks, dsink):
            for kh in range(N_KV_HEADS):
                ks = _key_tiles(k_ref, key_rows, kh)
                vs = _key_tiles(v_ref, key_rows, kh)
                qs = _stacked(q_ref, q_rows, kh)
                dos = _stacked(do_ref, q_rows, kh)
                probs, inv, e_sink = _softmax_parts(qs, ks, masks, _sink_column(sink_ref, kh, nq))
                probs = [p * inv for p in probs]
                dps = [jnp.concatenate([_dot_nt(dos, v_lo), _dot_nt(dos, v_hi)], axis=0) for v_lo, v_hi in vs]
                delta = _row_sums([p * dp for p, dp in zip(probs, dps)])
                d_sink = -(e_sink * inv) * delta
                for quarter, g in enumerate(STACK_HEADS):
                    d_here = jnp.sum(d_sink[quarter * nq:(quarter + 1) * nq], axis=0, keepdims=True)
                    dsink = dsink + jnp.where(jnp.logical_and(sub == 0, lane == kh * Q_PER_KV + g), d_here, 0.0)
                dq = None
                tile = slice((kh // 2) * LANES, (kh // 2 + 1) * LANES)
                for r, p, dp, (k_lo, k_hi) in zip(key_rows, probs, dps, ks):
                    ds = (p * (dp - delta)).astype(BF16)
                    p16 = p.astype(BF16)
                    dq_x = _dot(ds[:2 * nq], k_lo) + _dot(ds[2 * nq:], k_hi)
                    dq = dq_x if dq is None else dq + dq_x
                    d_k = _to_kv_lanes(_dot_tn(ds[:2 * nq], qs), _dot_tn(ds[2 * nq:], qs), kh) * ATTN_SCALE
                    d_v = _to_kv_lanes(_dot_tn(p16[:2 * nq], dos), _dot_tn(p16[2 * nq:], dos), kh)
                    n_keys = r.size
                    dk_ref[r, tile] += d_k[:n_keys]
                    dv_ref[r, tile] += d_v[:n_keys]
                dq = (dq * ATTN_SCALE).astype(BF16)
                col = kh * 2 * LANES
                dqkv_ref[q_rows, col:col + LANES] = dq[:nq]
                dqkv_ref[q_rows, col + LANES:col + 2 * LANES] = dq[nq:]
            return dsink

        dsink_ref[...] = _attention_schedule(seq, queries, jnp.zeros((SUBLANES, LANES), F32))
        dqkv_ref[:, D_MODEL:D_MODEL + KV_WIDTH] = dk_ref[...].astype(BF16)
        dqkv_ref[:, D_MODEL + KV_WIDTH:] = dv_ref[...].astype(BF16)

    return _pcall(
        body, name=name, grid=(n_b,),
        in_specs=[pl.BlockSpec(memory_space=pltpu.SMEM),
                  pl.BlockSpec((None, n_l, D_MODEL), lambda b: (b, 0, 0)),
                  pl.BlockSpec((None, n_l, KV_WIDTH), lambda b: (b, 0, D_MODEL // KV_WIDTH)),
                  pl.BlockSpec((None, n_l, KV_WIDTH), lambda b: (b, 0, D_MODEL // KV_WIDTH + 1)),
                  pl.BlockSpec((None, n_l, D_MODEL), lambda b: (b, 0, 0)),
                  ANY_SPEC, ANY_SPEC],
        out_specs=[pl.BlockSpec((None, n_l, qkv_width), lambda b: (b, 0, 0)),
                   pl.BlockSpec((None, SUBLANES, LANES), lambda b: (b, 0, 0))],
        out_shape=[jax.ShapeDtypeStruct(dproj3.shape, BF16), jax.ShapeDtypeStruct((n_b, SUBLANES, LANES), F32)],
        scratch_shapes=[pltpu.VMEM((n_l, KV_WIDTH), F32), pltpu.VMEM((n_l, KV_WIDTH), F32)],
        input_output_aliases={5: 0},
        compiler_params=_cp(("arbitrary",), VMEM_BIG),
    )(sinks, proj3, proj3, proj3, dattn3, dproj3, after)


TAB_ROWS = 8
SCAN_UNROLL = 4


def _cmul(ar, ai, br, bi):
    return ar * br - ai * bi, ar * bi + ai * br


def _discretise(ar, ai, ls):
    step = jnp.exp(ls)
    mag = jnp.exp(ar * step)
    ang = ai * step
    cos, sin = jnp.cos(ang), jnp.sin(ang)
    lr, li = mag * cos, mag * sin
    den = ar * ar + ai * ai
    nr, ni = lr - 1.0, li
    cr = (nr * ar + ni * ai) / den
    ci = (ni * ar - nr * ai) / den
    return step, mag, lr, li, den, nr, ni, cr, ci


def _scan_tables(lr, li, reverse):
    n = lr.shape[-1]
    pw = [(lr, li)]
    for _ in range(SUBLANES - 1):
        pw.append(_cmul(pw[-1][0], pw[-1][1], lr, li))
    row = lax.broadcasted_iota(jnp.int32, (SUBLANES, n), 0)
    out = []
    for d in (1, 2, 4):
        ok = (row + d <= SUBLANES - 1) if reverse else (row >= d)
        out += [jnp.where(ok, pw[d - 1][0], 0.0), jnp.where(ok, pw[d - 1][1], 0.0)]
    cr = jnp.zeros((SUBLANES, n), F32)
    ci = jnp.zeros((SUBLANES, n), F32)
    for r in range(SUBLANES):
        e = (SUBLANES - r) if reverse else (r + 1)
        cr = jnp.where(row == r, pw[e - 1][0], cr)
        ci = jnp.where(row == r, pw[e - 1][1], ci)
    return out + [cr, ci]


def ssm_prepare(ar, ai, ls, br_t, bi_t, name):
    def body(ar_ref, ai_ref, ls_ref, br_ref, bi_ref, bbr_ref, bbi_ref, tf_ref, tr_ref):
        _, _, lr, li, _, _, _, cr, ci = _discretise(ar_ref[...], ai_ref[...], ls_ref[...])
        br, bi = br_ref[...], bi_ref[...]
        bbr_ref[...] = cr * br - ci * bi
        bbi_ref[...] = cr * bi + ci * br
        for k, t in enumerate(_scan_tables(lr, li, False)):
            tf_ref[k] = t
        for k, t in enumerate(_scan_tables(lr, -li, True)):
            tr_ref[k] = t

    return _pcall(
        body, name=name,
        out_shape=[jax.ShapeDtypeStruct((SSM_GROUP, N_STATES), F32), jax.ShapeDtypeStruct((SSM_GROUP, N_STATES), F32),
                   jax.ShapeDtypeStruct((TAB_ROWS, SUBLANES, N_STATES), F32),
                   jax.ShapeDtypeStruct((TAB_ROWS, SUBLANES, N_STATES), F32)],
    )(ar, ai, ls, br_t, bi_t)


def ssm_param_backward(ar, ai, ls, br_t, bi_t, dlr_p, dli_p, dbbr, dbbi, group_sum, name):
    def body(ar_ref, ai_ref, ls_ref, br_ref, bi_ref, dlr_ref, dli_ref, dbbr_ref, dbbi_ref, gs_ref,
             dar_ref, dai_ref, dls_ref, dbr_ref, dbi_ref):
        ar, ai = ar_ref[...], ai_ref[...]
        step, mag, lr, li, den, nr, ni, cr, ci = _discretise(ar, ai, ls_ref[...])
        br, bi, dbbr_v, dbbi_v = br_ref[...], bi_ref[...], dbbr_ref[...], dbbi_ref[...]
        dbr_ref[...] = cr * dbbr_v + ci * dbbi_v
        dbi_ref[...] = cr * dbbi_v - ci * dbbr_v
        dcr = jnp.sum(dbbr_v * br + dbbi_v * bi, axis=0, keepdims=True)
        dci = jnp.sum(dbbi_v * br - dbbr_v * bi, axis=0, keepdims=True)
        dnr = (dcr * ar - dci * ai) / den
        dni = (dcr * ai + dci * ar) / den
        dden = -(cr * dcr + ci * dci) / den
        dar = (dcr * nr + dci * ni) / den + dden * 2.0 * ar
        dai = (dcr * ni - dci * nr) / den + dden * 2.0 * ai
        dlr = jnp.sum(dlr_ref[...], axis=0, keepdims=True) + dnr
        dli = jnp.sum(dli_ref[...], axis=0, keepdims=True) + dni
        dmag = (dlr * lr + dli * li) / mag
        dang = dli * lr - dlr * li
        dar_ref[...] = dar + dmag * mag * step
        dai_ref[...] = dai + dang * step
        dstep = dmag * mag * ar + dang * ai
        dls_ref[...] = jnp.dot(dstep * step, gs_ref[...], preferred_element_type=F32, precision=lax.Precision.HIGHEST)

    vec = jax.ShapeDtypeStruct((1, N_STATES), F32)
    mat = jax.ShapeDtypeStruct((SSM_GROUP, N_STATES), F32)
    return _pcall(body, name=name, out_shape=[vec, vec, jax.ShapeDtypeStruct((1, LANES), F32), mat, mat])(
        ar, ai, ls, br_t, bi_t, dlr_p, dli_p, dbbr, dbbi, group_sum)


def _scan_rows(a, b, tabs, carry, reverse):
    for k, d in enumerate((1, 2, 4)):
        shift = SUBLANES - d if reverse else d
        sr, si = pltpu.roll(a, shift, axis=0), pltpu.roll(b, shift, axis=0)
        pr, pi = _cmul(tabs[2 * k], tabs[2 * k + 1], sr, si)
        a, b = a + pr, b + pi
    pr, pi = _cmul(tabs[6], tabs[7], carry[0], carry[1])
    return a + pr, b + pi


def _time_groups(seq, reverse):
    meta = [seq + SUBLANES * g for g in range(N_META // SUBLANES)]
    return meta[::-1] if reverse else meta


def ssm_forward_scan(proj3, b_comb, tabf, c_comb, dvec, seq, name):
    n_b, n_l, _ = proj3.shape
    u_blk = (D_MODEL + 2 * KV_WIDTH) // LANES

    def body(u_ref, b_ref, tab_ref, c_ref, d_ref, x_ref, y_ref, bu, xs):
        j = pl.program_id(1)
        u = u_ref[...]
        bu[...] = _dot(u, b_ref[...])
        tabs = [tab_ref[k] for k in range(TAB_ROWS)]

        def group(r0, carry):
            rows = pl.ds(r0, SUBLANES)
            a, b = _scan_rows(bu[rows, :SCAN_COLS], bu[rows, SCAN_COLS:], tabs, carry, False)
            xs[rows, :SCAN_COLS] = a
            xs[rows, SCAN_COLS:] = b
            return (jnp.broadcast_to(a[SUBLANES - 1:, :], a.shape), jnp.broadcast_to(b[SUBLANES - 1:, :], b.shape))

        zero = jnp.zeros((SUBLANES, SCAN_COLS), F32)
        carry = (zero, zero)
        for r0 in _time_groups(seq, False):
            carry = group(r0, carry)
        span = SCAN_UNROLL * SUBLANES

        def groups(t, c):
            for k in range(SCAN_UNROLL):
                c = group(pl.multiple_of(t * span, span) + k * SUBLANES, c)
            return c

        lax.fori_loop(0, seq // span, groups, carry)
        x16 = xs[...].astype(BF16)
        x_ref[...] = x16
        contrib = _dot(x16, c_ref[...])

        @pl.when(j % 2 == 0)
        def _():
            y_ref[...] = contrib + d_ref[...] * u.astype(F32)

        @pl.when(j % 2 == 1)
        def _():
            y_ref[...] += contrib

    return _pcall(
        body, name=name, grid=(n_b, N_SCAN_BLK),
        in_specs=[pl.BlockSpec((None, n_l, LANES), lambda b, j: (b, 0, u_blk + j // 2)),
                  pl.BlockSpec((None, LANES, 2 * SCAN_COLS), lambda b, j: (j, 0, 0)),
                  pl.BlockSpec((TAB_ROWS, SUBLANES, SCAN_COLS), lambda b, j: (0, 0, j)),
                  pl.BlockSpec((None, 2 * SCAN_COLS, LANES), lambda b, j: (j, 0, 0)),
                  pl.BlockSpec((1, LANES), lambda b, j: (0, j // 2))],
        out_specs=[pl.BlockSpec((None, n_l, 2 * SCAN_COLS), lambda b, j: (b, 0, j)),
                   pl.BlockSpec((None, n_l, LANES), lambda b, j: (b, 0, j // 2))],
        out_shape=[jax.ShapeDtypeStruct((n_b, n_l, 2 * N_STATES), BF16),
                   jax.ShapeDtypeStruct((n_b, n_l, SSM_WIDTH), F32)],
        scratch_shapes=[pltpu.VMEM((n_l, 2 * SCAN_COLS), F32)] * 2,
        compiler_params=_cp(("arbitrary", "arbitrary"), VMEM_BIG),
    )(proj3, b_comb, tabf, c_comb, dvec)


def ssm_backward_scan(dyraw3, xs3, dproj3, c_comb_t, tabr, b_comb_t, dvec, seq, name):
    n_b, n_l, _ = xs3.shape
    u_blk = (D_MODEL + 2 * KV_WIDTH) // LANES

    def body(dy_ref, x_ref, _, c_ref, tab_ref, b_ref, d_ref, du_ref, g_ref, dlr_ref, dli_ref, dx, gs, xs, du_acc):
        j = pl.program_id(1)
        dy = dy_ref[...]
        dx[...] = _dot(dy, c_ref[...])
        xs[...] = x_ref[...].astype(F32)
        tabs = [tab_ref[k] for k in range(TAB_ROWS)]
        last_row = lax.broadcasted_iota(jnp.int32, (SUBLANES, SCAN_COLS), 0) == SUBLANES - 1

        def group(r0, state):
            cr, ci, acc_r, acc_i = state
            rows = pl.ds(r0, SUBLANES)
            a, b = _scan_rows(dx[rows, :SCAN_COLS], dx[rows, SCAN_COLS:], tabs, (cr, ci), True)
            gs[rows, :SCAN_COLS] = a
            gs[rows, SCAN_COLS:] = b
            na = jnp.where(last_row, cr, pltpu.roll(a, SUBLANES - 1, axis=0))
            nb = jnp.where(last_row, ci, pltpu.roll(b, SUBLANES - 1, axis=0))
            xa, xb = xs[rows, :SCAN_COLS], xs[rows, SCAN_COLS:]
            return (jnp.broadcast_to(a[:1, :], a.shape), jnp.broadcast_to(b[:1, :], b.shape),
                    acc_r + na * xa + nb * xb, acc_i + nb * xa - na * xb)

        zero = jnp.zeros((SUBLANES, SCAN_COLS), F32)
        span = SCAN_UNROLL * SUBLANES
        n_spans = seq // span

        def groups(t, s):
            for k in reversed(range(SCAN_UNROLL)):
                s = group(pl.multiple_of((n_spans - 1 - t) * span, span) + k * SUBLANES, s)
            return s

        state = lax.fori_loop(0, n_spans, groups, (zero, zero, zero, zero))
        for r0 in _time_groups(seq, True):
            state = group(r0, state)
        dlr_ref[...] = state[2]
        dli_ref[...] = state[3]
        g16 = gs[...].astype(BF16)
        g_ref[...] = g16
        contrib = _dot(g16, b_ref[...])

        @pl.when(j % 2 == 0)
        def _():
            du_acc[...] = contrib + d_ref[...] * dy.astype(F32)

        @pl.when(j % 2 == 1)
        def _():
            du_ref[...] = (du_acc[...] + contrib).astype(BF16)

    state_blk = pl.BlockSpec((None, n_l, 2 * SCAN_COLS), lambda b, j: (b, 0, j))
    dl_blk = pl.BlockSpec((None, SUBLANES, SCAN_COLS), lambda b, j: (b, 0, j))
    return _pcall(
        body, name=name, grid=(n_b, N_SCAN_BLK),
        in_specs=[pl.BlockSpec((None, n_l, LANES), lambda b, j: (b, 0, j // 2)), state_blk,
                  pl.BlockSpec(memory_space=pl.ANY),
                  pl.BlockSpec((None, LANES, 2 * SCAN_COLS), lambda b, j: (j, 0, 0)),
                  pl.BlockSpec((TAB_ROWS, SUBLANES, SCAN_COLS), lambda b, j: (0, 0, j)),
                  pl.BlockSpec((None, 2 * SCAN_COLS, LANES), lambda b, j: (j, 0, 0)),
                  pl.BlockSpec((1, LANES), lambda b, j: (0, j // 2))],
        out_specs=[pl.BlockSpec((None, n_l, LANES), lambda b, j: (b, 0, u_blk + j // 2)), state_blk, dl_blk, dl_blk],
        out_shape=[jax.ShapeDtypeStruct(dproj3.shape, BF16), jax.ShapeDtypeStruct((n_b, n_l, 2 * N_STATES), BF16),
                   jax.ShapeDtypeStruct((n_b, SUBLANES, N_STATES), F32), jax.ShapeDtypeStruct((n_b, SUBLANES, N_STATES), F32)],
        scratch_shapes=[pltpu.VMEM((n_l, 2 * SCAN_COLS), F32)] * 3 + [pltpu.VMEM((n_l, LANES), F32)],
        input_output_aliases={2: 0},
        compiler_params=_cp(("arbitrary", "arbitrary"), VMEM_BIG),
    )(dyraw3, xs3, dproj3, c_comb_t, tabr, b_comb_t, dvec)


def ssm_param_grads(proj, gs, xs, dyraw, tm, name):
    t_rows = proj.shape[0]
    ni = t_rows // tm
    u_blk = (D_MODEL + 2 * KV_WIDTH) // LANES
    width = 2 * SCAN_COLS

    def body(u_ref, g_ref, x_ref, dy_ref, db_ref, dc_ref, dd_ref):
        cb, i = pl.program_id(0), pl.program_id(1)
        u, dy = u_ref[...], dy_ref[...]
        _accumulate(db_ref, _dot_tn(u, g_ref[...]), i == 0)
        _accumulate(dc_ref, _dot_tn(x_ref[...], dy), i == 0)

        @pl.when(cb % 2 == 0)
        def _():
            _accumulate(dd_ref, jnp.sum(dy.astype(F32) * u.astype(F32), axis=0, keepdims=True), i == 0)

    return _pcall(
        body, name=name, grid=(N_SCAN_BLK, ni),
        in_specs=[pl.BlockSpec((tm, LANES), lambda cb, i: (i, u_blk + cb // 2)),
                  pl.BlockSpec((tm, width), lambda cb, i: (i, cb)),
                  pl.BlockSpec((tm, width), lambda cb, i: (i, cb)),
                  pl.BlockSpec((tm, LANES), lambda cb, i: (i, cb // 2))],
        out_specs=[pl.BlockSpec((None, LANES, width), lambda cb, i: (cb, 0, 0)),
                   pl.BlockSpec((None, width, LANES), lambda cb, i: (cb, 0, 0)),
                   pl.BlockSpec((1, LANES), lambda cb, i: (0, cb // 2))],
        out_shape=[jax.ShapeDtypeStruct((N_SCAN_BLK, LANES, width), F32),
                   jax.ShapeDtypeStruct((N_SCAN_BLK, width, LANES), F32), jax.ShapeDtypeStruct((1, SSM_WIDTH), F32)],
        compiler_params=_cp(("arbitrary", "arbitrary"), VMEM_BIG),
    )(proj, gs, xs, dyraw)


def sum_leading(x, name):
    def body(x_ref, o_ref):
        acc = x_ref[0]
        for k in range(1, x.shape[0]):
            acc = acc + x_ref[k]
        o_ref[...] = acc

    return _pcall(body, name=name, out_shape=jax.ShapeDtypeStruct(x.shape[1:], x.dtype))(x)


WEIGHTS = ['meta_tokens', 'ffn1_norm', 'ffn1_w1', 'ffn1_w3', 'ffn1_w2', 'mix_norm', 'w_in', 'attn_sinks', 'ssm_a_re',
           'ssm_a_im', 'ssm_log_step', 'ssm_b_re', 'ssm_b_im', 'ssm_c_re', 'ssm_c_im', 'ssm_d', 'ssm_glu_a', 'ssm_glu_b',
           'w_out', 'ffn2_norm', 'ffn2_w1', 'ffn2_w3', 'ffn2_w2', 'final_norm']
SHARDED = ['ffn1_w1', 'ffn1_w3', 'ffn1_w2', 'ffn2_w1', 'ffn2_w3', 'ffn2_w2', 'w_in', 'ssm_glu_a', 'ssm_glu_b', 'w_out']
REPLICATED = ['ffn1_norm', 'mix_norm', 'ffn2_norm', 'final_norm', 'attn_sinks', 'ssm_a_re', 'ssm_a_im', 'ssm_log_step',
              'ssm_b_re', 'ssm_b_im', 'ssm_c_re', 'ssm_c_im', 'ssm_d']
PACK_COLS = 1024


def _pack(arrays):
    parts = []
    for a in arrays:
        flat = a.reshape(-1)
        chunk = SUBLANES * PACK_COLS
        padded = -(-flat.shape[0] // chunk) * chunk
        parts.append(jnp.pad(flat, (0, padded - flat.shape[0])).reshape(-1, PACK_COLS))
    return jnp.concatenate(parts, axis=0)


def _unpack(packed, shapes):
    out, row = [], 0
    for shape in shapes:
        size = 1
        for s in shape:
            size *= s
        chunk = SUBLANES * PACK_COLS
        rows = -(-size // chunk) * SUBLANES
        out.append(packed[row:row + rows].reshape(-1)[:size].reshape(shape))
        row += rows
    return out


def kernel(x, meta_tokens, ffn1_norm, ffn1_w1, ffn1_w3, ffn1_w2, mix_norm, w_in, attn_sinks, ssm_a_re, ssm_a_im, ssm_log_step, ssm_b_re, ssm_b_im, ssm_c_re, ssm_c_im, ssm_d, ssm_glu_a, ssm_glu_b, w_out, ffn2_norm, ffn2_w1, ffn2_w3, ffn2_w2, final_norm, loss_target, m_meta_tokens, m_ffn1_norm, m_ffn1_w1, m_ffn1_w3, m_ffn1_w2, m_mix_norm, m_w_in, m_attn_sinks, m_ssm_a_re, m_ssm_a_im, m_ssm_log_step, m_ssm_b_re, m_ssm_b_im, m_ssm_c_re, m_ssm_c_im, m_ssm_d, m_ssm_glu_a, m_ssm_glu_b, m_w_out, m_ffn2_norm, m_ffn2_w1, m_ffn2_w3, m_ffn2_w2, m_final_norm, v_meta_tokens, v_ffn1_norm, v_ffn1_w1, v_ffn1_w3, v_ffn1_w2, v_mix_norm, v_w_in, v_attn_sinks, v_ssm_a_re, v_ssm_a_im, v_ssm_log_step, v_ssm_b_re, v_ssm_b_im, v_ssm_c_re, v_ssm_c_im, v_ssm_d, v_ssm_glu_a, v_ssm_glu_b, v_w_out, v_ffn2_norm, v_ffn2_w1, v_ffn2_w3, v_ffn2_w2, v_final_norm):
    given = dict(locals())
    w = {n: given[n] for n in WEIGHTS}
    m = {n: given["m_" + n] for n in WEIGHTS}
    v = {n: given["v_" + n] for n in WEIGHTS}

    n_b, seq, _ = x.shape
    n_l = seq + N_META
    t_rows = n_b * n_l
    tm = _row_tile(n_l, 688)
    px, py, pc = _my_place()
    me = 4 * px + 2 * py + pc

    glu = jnp.stack([ssm_glu_a[0], ssm_glu_b[0]]).astype(BF16)
    ffn_names = ['ffn1_w1', 'ffn1_w3', 'ffn1_w2', 'ffn2_w1', 'ffn2_w3', 'ffn2_w2']

    def hidden_on_rows(n, t):
        return t[0] if n.endswith('w2') else t[0].T

    def hidden_on_rows_back(n, t):
        return t[None] if n.endswith('w2') else t.T[None]

    me_idx = jnp.reshape(me, (1,)).astype(jnp.int32)
    first_names, later_names = ffn_names[:3], ffn_names[3:]
    *first, metag = all_gather_list(
        [hidden_on_rows(n, w[n]).astype(BF16) for n in first_names] + [meta_tokens], meta_tokens, "ag_first")
    win_send, win_recv, win_shard, win_land, win_token = exchange_start(
        [w_in[0].astype(BF16)], first[0], True, "ag_w_in_start")
    later_shards = [hidden_on_rows(n, w[n]).astype(BF16) for n in later_names] + [glu, w_out[0].astype(BF16)]
    ag_send, ag_recv, later_shards, later_lands, ag_token = exchange_start(later_shards, win_token, True, "ag_later_start")
    full = {n: g.reshape(D_FF, D_MODEL) for n, g in zip(first_names, first)}
    meta_full = metag.transpose(1, 0, 2).reshape(N_META, D_MODEL)

    final_g = final_norm.reshape(1, D_MODEL)

    ar = ssm_a_re.reshape(1, N_STATES)
    ai = ssm_a_im.reshape(1, N_STATES)
    ls = jnp.repeat(ssm_log_step.reshape(SSM_GROUPS), SSM_STATE).reshape(1, N_STATES)
    br_t = ssm_b_re[0].transpose(2, 0, 1).reshape(SSM_GROUP, N_STATES)
    bi_t = ssm_b_im[0].transpose(2, 0, 1).reshape(SSM_GROUP, N_STATES)
    bbr, bbi, tabf, tabr = ssm_prepare(ar, ai, ls, br_t, bi_t, "ssm_prepare")
    bbr_g = bbr.reshape(SSM_GROUP, SSM_GROUPS, SSM_STATE).transpose(1, 0, 2)
    bbi_g = bbi.reshape(SSM_GROUP, SSM_GROUPS, SSM_STATE).transpose(1, 0, 2)
    groups_per_blk = SCAN_COLS // SSM_STATE
    half = ((jnp.arange(N_SCAN_BLK) % 2)[:, None] == jnp.arange(2)[None, :]).astype(F32)
    eye = jnp.eye(groups_per_blk, dtype=F32)

    def scan_blocks(re_g, im_g):
        def one(t):
            t = t.reshape(N_SCAN_BLK, groups_per_blk, SSM_GROUP, SSM_STATE)
            t = t[:, :, :, None, :] * eye[None, :, None, :, None]
            t = t.reshape(N_SCAN_BLK, LANES // 2, SCAN_COLS)
            return (t[:, None] * half[:, :, None, None]).reshape(N_SCAN_BLK, LANES, SCAN_COLS)
        return jnp.concatenate([one(re_g), one(im_g)], axis=-1).astype(BF16)

    b_comb = scan_blocks(bbr_g, bbi_g)
    c_comb_t = scan_blocks(ssm_c_re[0], -ssm_c_im[0])
    b_comb_t, c_comb = b_comb.transpose(0, 2, 1), c_comb_t.transpose(0, 2, 1)

    ffn1_w = (full['ffn1_w1'], full['ffn1_w3'], full['ffn1_w2'])
    h1, hn1, a1, b1, h0 = ffn_forward(x, ffn1_norm, *ffn1_w, ag_token, tm, "ffn1_fwd", meta=meta_full)
    win_shard, (wing,) = exchange_wait(win_send, win_recv, win_shard, win_land, h1, True, "ag_w_in_wait")
    wing = lax.dynamic_update_slice_in_dim(wing, win_shard[0][None], me, axis=0)
    hnm, proj = mix_forward(h1, mix_norm, wing, tm, "mix_fwd")
    proj3 = proj.reshape(n_b, n_l, IN_WIDTH)
    attn3 = attention_forward(proj3, attn_sinks, seq, "attn_fwd")
    attn = attn3.reshape(t_rows, D_MODEL)
    xs3, yraw3 = ssm_forward_scan(proj3, b_comb, tabf, c_comb, ssm_d, seq, "ssm_fwd")
    yraw = yraw3.reshape(t_rows, SSM_WIDTH)
    later_shards, later = exchange_wait(ag_send, ag_recv, later_shards, later_lands, yraw3, True, "ag_later_wait")
    later = [lax.dynamic_update_slice_in_dim(z, s[None], me, axis=0) for z, s in zip(later, later_shards)]
    for n, g in zip(later_names, later):
        full[n] = g.reshape(D_FF, D_MODEL)
    ffn2_w = (full['ffn2_w1'], full['ffn2_w3'], full['ffn2_w2'])
    glug, wog = later[len(later_names):]
    glu_a = glug[:, 0].transpose(1, 0, 2).reshape(SSM_WIDTH, D_MODEL)
    glu_b = glug[:, 1].transpose(1, 0, 2).reshape(SSM_WIDTH, D_MODEL)
    w_out_full = wog.reshape(D_MODEL, D_MODEL)
    h2 = merge_forward(h1, yraw, attn, proj, glu_a, glu_b, w_out_full, tm, "merge_fwd")
    h3, hn2, a2, b2 = ffn_forward(h2, ffn2_norm, *ffn2_w, ag_token, tm, "ffn2_fwd")
    dh3, loss_part, g_final = final_loss_backward(h3, loss_target, final_g, seq, tm, "loss_bwd")
    loss = lax.psum(loss_part[0, 0], ("x", "y", "c"))

    def blocked_ffn(*grads_by_hidden_row):
        return tuple(t.reshape(N_DEV, FF_BLK, D_MODEL) for t in grads_by_hidden_row)

    def blocked_cols(full_grad):
        r = full_grad.shape[0]
        return full_grad.reshape(r, N_DEV, full_grad.shape[1] // N_DEV).transpose(1, 0, 2).astype(BF16)

    early = {}

    def start_reduce(names, tag):
        srcs = [dw[n] for n in names]
        send, recv, srcs, lands, token = exchange_start(srcs, srcs[0], False, "rs_" + tag + "_start")
        early[tag] = (names, send, recv, srcs, lands)
        return token

    dw = {}
    da2, db2, dh3_half = ffn_backward_hidden(dh3, a2, b2, ffn2_w[2], g_final, tm, "ffn2_bwd_hid")
    dw['ffn2_w1'], dw['ffn2_w3'], dw['ffn2_w2'] = blocked_ffn(
        *ffn_backward_weights(hn2, dh3_half, a2, b2, da2, db2, n_l, FF_BWD_COLS, "ffn2_bwd_w"))
    token = start_reduce(later_names, "ffn2")
    dh2, g_ffn2_norm = ffn_backward_input(dh3, h2, ffn2_norm, da2, db2, ffn2_w[0], ffn2_w[1], token, tm, "ffn2_bwd_in")
    dattn, dyraw, dproj, *for_weights = merge_backward(dh2, yraw, attn, proj, glu_a, glu_b, w_out_full, token, tm,
                                                       "merge_bwd")
    d_wo, d_ga, d_gb = merge_backward_weights(*for_weights, tm, "merge_bwd_w")
    dw['ssm_glu_a'] = blocked_cols(d_ga)
    dw['ssm_glu_b'] = blocked_cols(d_gb)
    dw['w_out'] = d_wo.reshape(N_DEV, D_MODEL // N_DEV, D_MODEL).astype(BF16)
    token = start_reduce(['ssm_glu_a', 'ssm_glu_b', 'w_out'], "mix")
    dproj3 = dproj.reshape(n_b, n_l, IN_WIDTH)
    dproj3, dsink_p = attention_backward(proj3, dattn.reshape(n_b, n_l, D_MODEL), dproj3, attn_sinks, token, seq,
                                         "attn_bwd")
    dproj3, gs3, dlr_p, dli_p = ssm_backward_scan(
        dyraw.reshape(n_b, n_l, SSM_WIDTH), xs3, dproj3, c_comb_t, tabr, b_comb_t, ssm_d, seq, "ssm_bwd")
    dproj = dproj3.reshape(t_rows, IN_WIDTH)
    d_bd, d_cd, g_d = ssm_param_grads(proj, gs3.reshape(t_rows, 2 * N_STATES), xs3.reshape(t_rows, 2 * N_STATES),
                                      dyraw, n_l, "ssm_bwd_w")
    w_in_full = wing.transpose(1, 0, 2).reshape(D_MODEL, IN_WIDTH)
    dh1, g_mix_norm = mix_backward_act(dh2, h1, mix_norm, dproj, w_in_full, tm, "mix_bwd_act")
    dw['w_in'] = mix_backward_weights(hnm, dproj, n_l, "mix_bwd_w")
    token = start_reduce(['w_in'], "w_in")

    def group_blocks(part, channels_first):
        if channels_first:
            t = jnp.sum(part.reshape(N_SCAN_BLK, 2, LANES // 2, SCAN_COLS) * half[:, :, None, None], axis=1)
            t = t.reshape(N_SCAN_BLK, groups_per_blk, SSM_GROUP, groups_per_blk, SSM_STATE)
            t = jnp.sum(t * eye[None, :, None, :, None], axis=3)
            return t.reshape(SSM_GROUPS, SSM_GROUP, SSM_STATE)
        t = jnp.sum(part.reshape(N_SCAN_BLK, SCAN_COLS, 2, LANES // 2) * half[:, None, :, None], axis=2)
        t = t.reshape(N_SCAN_BLK, groups_per_blk, SSM_STATE, groups_per_blk, SSM_GROUP)
        t = jnp.sum(t * eye[None, :, None, :, None], axis=3)
        return t.reshape(SSM_GROUPS, SSM_STATE, SSM_GROUP).transpose(0, 2, 1)

    dbbr = group_blocks(d_bd[:, :, :SCAN_COLS], True).transpose(1, 0, 2).reshape(SSM_GROUP, N_STATES)
    dbbi = group_blocks(d_bd[:, :, SCAN_COLS:], True).transpose(1, 0, 2).reshape(SSM_GROUP, N_STATES)
    g_c_re = group_blocks(d_cd[:, :SCAN_COLS, :], False)[None]
    g_c_im = -group_blocks(d_cd[:, SCAN_COLS:, :], False)[None]
    group_sum = (jnp.arange(N_STATES)[:, None] // SSM_STATE == jnp.arange(LANES)[None, :]).astype(F32)
    g_ar, g_ai, g_ls, g_br, g_bi = ssm_param_backward(
        ar, ai, ls, br_t, bi_t, dlr_p.reshape(n_b * SUBLANES, N_STATES), dli_p.reshape(n_b * SUBLANES, N_STATES),
        dbbr, dbbi, group_sum, "ssm_bwd_params")
    g_sinks = sum_leading(dsink_p, "sink_sum")[0:1, :N_KV_HEADS * Q_PER_KV]

    small = {
        'mix_norm': g_mix_norm, 'ffn2_norm': g_ffn2_norm, 'final_norm': g_final.reshape(D_MODEL),
        'attn_sinks': g_sinks, 'ssm_a_re': g_ar.reshape(1, SSM_GROUPS, SSM_STATE), 'ssm_a_im': g_ai.reshape(1, SSM_GROUPS, SSM_STATE),
        'ssm_log_step': g_ls[:, :SSM_GROUPS],
        'ssm_b_re': g_br.reshape(SSM_GROUP, SSM_GROUPS, SSM_STATE).transpose(1, 2, 0)[None],
        'ssm_b_im': g_bi.reshape(SSM_GROUP, SSM_GROUPS, SSM_STATE).transpose(1, 2, 0)[None],
        'ssm_c_re': g_c_re, 'ssm_c_im': g_c_im, 'ssm_d': g_d,
    }
    early_small = [n for n in REPLICATED if n in small]
    sg_send, sg_recv, sg_src, sg_land, token = exchange_start(
        [_pack([small[n] for n in early_small])], token, True, "ag_small_start")
    da1, db1, dh1_half = ffn_backward_hidden(dh1, a1, b1, ffn1_w[2], token, tm, "ffn1_bwd_hid")
    d_w1t, d_w3t = ffn_backward_weights_in(hn1, da1, db1, n_l, FF_BWD_COLS, "ffn1_bwd_w_in")
    dw['ffn1_w1'], dw['ffn1_w3'] = blocked_ffn(d_w1t, d_w3t)
    token = start_reduce(first_names[:2], "ffn1_in")
    (dw['ffn1_w2'],) = blocked_ffn(
        ffn_backward_weights_out(dh1_half, a1, b1, token, n_l, FF_BWD_COLS, "ffn1_bwd_w_out"))
    token = start_reduce(first_names[2:], "ffn1_out")
    grad_x, meta_rows_grad, g_ffn1_norm = ffn_backward_input(
        dh1, h0, ffn1_norm, da1, db1, ffn1_w[0], ffn1_w[1], token, tm, "ffn1_bwd_in", examples=(n_b, seq))
    g_meta = sum_leading(meta_rows_grad, "meta_sum")

    grads, deltas, new_m, new_v = {}, {}, {}, {}

    def views(n):
        if n in ffn_names:
            return functools.partial(hidden_on_rows, n), functools.partial(hidden_on_rows_back, n)
        return (lambda t: t[0]), (lambda t: t[None])

    def finish_reduce(tag, previous):
        names, send, recv, srcs, lands = early[tag]
        srcs, lands = exchange_wait(send, recv, srcs, lands, previous, False, "rs_" + tag + "_wait")
        for n, g, land in zip(names, srcs, lands):
            two_d, back = views(n)
            out = adamw_exchanged(me_idx, g, land, two_d(w[n]), two_d(m[n]), two_d(v[n]), "adamw_" + n)
            grads[n], deltas[n], new_m[n], new_v[n] = (back(o) for o in out)
            previous = out[1]
        return previous

    sl_send, sl_recv, sl_src, sl_land, previous = exchange_start(
        [_pack([g_ffn1_norm, g_meta])], g_meta, True, "ag_small_late_start")
    for tag in ("ffn2", "mix", "w_in"):
        previous = finish_reduce(tag, previous)

    zeros_meta = jnp.zeros((N_META, D_MODEL), F32)
    sl_src, (late_parts,) = exchange_wait(sl_send, sl_recv, sl_src, sl_land, previous, True, "ag_small_late_wait")
    late_parts = lax.dynamic_update_slice_in_dim(late_parts, sl_src[0][None], me, axis=0)
    sg_src, (early_parts,) = exchange_wait(sg_send, sg_recv, sg_src, sg_land, late_parts, True, "ag_small_wait")
    early_parts = lax.dynamic_update_slice_in_dim(early_parts, sg_src[0][None], me, axis=0)

    def small_update(parts, names, extra, tag):
        pack_of = lambda d: _pack([d[n] for n in names] + extra)
        packed = adamw_small(parts, pack_of(w), pack_of(m), pack_of(v), "adamw_small_" + tag)
        unpacked = [_unpack(p, [w[n].shape for n in names] + [e.shape for e in extra]) for p in packed]
        for k, n in enumerate(names):
            grads[n], deltas[n], new_m[n], new_v[n] = (u[k] for u in unpacked)
        return packed, unpacked

    small_update(early_parts, early_small, [], "early")
    packed_out, unpacked = small_update(late_parts, ['ffn1_norm'], [zeros_meta], "late")
    g_meta_full = unpacked[0][-1]
    grads['meta_tokens'] = lax.dynamic_index_in_dim(
        g_meta_full.reshape(N_META, N_DEV, D_MODEL // N_DEV), me, axis=1, keepdims=False)
    deltas['meta_tokens'], new_m['meta_tokens'], new_v['meta_tokens'] = adamw_plain(
        grads['meta_tokens'], w['meta_tokens'], m['meta_tokens'], v['meta_tokens'], "adamw_meta")

    finish_reduce("ffn1_out", finish_reduce("ffn1_in", packed_out[0]))

    return (loss, grad_x, *[grads[n] for n in WEIGHTS], *[deltas[n] for n in WEIGHTS],
            *[new_m[n] for n in WEIGHTS], *[new_v[n] for n in WEIGHTS])
```

```python
import functools

import jax
import jax.numpy as jnp
from jax import lax
from jax.experimental import pallas as pl
from jax.experimental.pallas import tpu as pltpu

F32 = jnp.float32
BF16 = jnp.bfloat16
MESH = pl.DeviceIdType.MESH

N_DEV = 8
D_MODEL = 1024
N_META = 16
HEAD_DIM = 64
N_KV_HEADS = 4
Q_PER_KV = 4
BLOCK = 128
KV_WIDTH = N_KV_HEADS * HEAD_DIM
SSM_GROUP = 16
SSM_WIDTH = 512
SSM_GROUPS = 32
SSM_STATE = 64
N_STATES = SSM_GROUPS * SSM_STATE
D_FF = 2816
FF_BLK = D_FF // N_DEV
IN_WIDTH = 4096
IN_BLK = IN_WIDTH // N_DEV
NORM_EPS = 1e-6
NEG_INF = -1e30
SCAN_COLS = 256
N_SCAN_BLK = N_STATES // SCAN_COLS
SUBLANES = 8
LANES = 128
MXU_WIDTH = 256
FF_BWD_COLS = MXU_WIDTH

ADAM_LR = 0.001
ADAM_B1 = 0.9
ADAM_B2 = 0.999
ADAM_EPS = 1e-08
ADAM_WD = 0.01
ADAM_STEP = 10

VMEM_BIG = 56 * 1024 * 1024


def _cp(sem=None, vmem=None):
    kw = {}
    if sem is not None:
        kw["dimension_semantics"] = sem
    if vmem is not None:
        kw["vmem_limit_bytes"] = vmem
    return pltpu.CompilerParams(**kw)


def _pcall(body, **kw):
    return pl.pallas_call(body, **kw)


def _dot(a, b):
    return jnp.dot(a, b, preferred_element_type=F32)


def _dot_nt(a, b):
    return lax.dot_general(a, b, (((1,), (1,)), ((), ())), preferred_element_type=F32)


def _dot_tn(a, b):
    return lax.dot_general(a, b, (((0,), (0,)), ((), ())), preferred_element_type=F32)


def _sigmoid(x):
    return 1.0 / (1.0 + jnp.exp(-x))


def _row_tile(rows, cap):
    best = None
    for t in range(16, min(rows, cap) + 1, 16):
        if rows % t == 0:
            best = t
    assert best is not None, rows
    return best


def _my_place():
    return lax.axis_index("x"), lax.axis_index("y"), lax.axis_index("c")


def all_gather_list(shards, after, name):
    n = len(shards)

    def body(*refs):
        ins, outs = refs[:n], refs[n + 1:2 * n + 1]
        send_sems, recv_sems, local_sems = refs[2 * n + 1:]
        x, y, c = _my_place()
        me, sibling = (x, y, c), (x, y, 1 - c)
        chips = [(1 - x, y), (x, 1 - y), (1 - x, 1 - y)]

        def blk(a, px, py, pc):
            return outs[a].at[4 * px + 2 * py + pc]

        def copy(a, k, block, to, src=None):
            return pltpu.make_async_remote_copy(
                src_ref=blk(a, *block) if src is None else src, dst_ref=blk(a, *block),
                send_sem=send_sems.at[a * 7 + k], recv_sem=recv_sems.at[a * 7 + k],
                device_id=to, device_id_type=MESH)

        mine = [pltpu.make_async_copy(ins[a], blk(a, *me), local_sems.at[a]) for a in range(n)]
        for cp in mine:
            cp.start()
        first = []
        for a in range(n):
            first.append(copy(a, 0, me, sibling, src=ins[a]))
            first += [copy(a, 1 + j, me, (*chip, c), src=ins[a]) for j, chip in enumerate(chips)]
        for cp in first:
            cp.start()
        passed = []
        for j, chip in enumerate(chips):
            for a in range(n):
                copy(a, 1 + j, (*chip, c), me).wait_recv()
                cp = copy(a, 4 + j, (*chip, c), sibling)
                cp.start()
                passed.append(cp)
        for a in range(n):
            copy(a, 0, sibling, me).wait_recv()
            for j, chip in enumerate(chips):
                copy(a, 4 + j, (*chip, 1 - c), me).wait_recv()
        for cp in first + passed:
            cp.wait_send()
        for cp in mine:
            cp.wait()

    any_spec = pl.BlockSpec(memory_space=pl.ANY)
    return _pcall(
        body, name=name,
        out_shape=[jax.ShapeDtypeStruct((N_DEV,) + s.shape, s.dtype) for s in shards],
        in_specs=[any_spec] * (n + 1), out_specs=[any_spec] * n,
        scratch_shapes=[pltpu.SemaphoreType.DMA((7 * n,)), pltpu.SemaphoreType.DMA((7 * n,)),
                        pltpu.SemaphoreType.DMA((n,))],
    )(*shards, after)


HBM_SPEC = pl.BlockSpec(memory_space=pltpu.HBM)
SEM_SPEC = pl.BlockSpec(memory_space=pltpu.SEMAPHORE)
N_PEERS = N_DEV - 1


def _related(k):
    x, y, c = _my_place()
    px = 1 - x if k & 4 else x
    py = 1 - y if k & 2 else y
    pc = 1 - c if k & 1 else c
    return (px, py, pc), 4 * px + 2 * py + pc


def _exchange_copies(srcs, lands, send_sems, recv_sems, gather):
    x, y, c = _my_place()
    me = 4 * x + 2 * y + c
    copies = []
    for a, (src, land) in enumerate(zip(srcs, lands)):
        for k in range(1, N_DEV):
            peer, d = _related(k)
            copies.append(pltpu.make_async_remote_copy(
                src_ref=src if gather else src.at[d], dst_ref=land.at[me] if gather else land.at[k],
                send_sem=send_sems.at[a * N_PEERS + k - 1], recv_sem=recv_sems.at[a * N_PEERS + k - 1],
                device_id=peer, device_id_type=MESH))
    return copies


def exchange_start(srcs, after, gather, name):
    n = len(srcs)
    land_shapes = [((N_DEV,) + s.shape) if gather else s.shape for s in srcs]

    def body(*refs):
        send_sems, recv_sems = refs[2 * n + 1], refs[2 * n + 2]
        for cp in _exchange_copies(refs[:n], refs[n:2 * n], send_sems, recv_sems, gather):
            cp.start()
        token = refs[-1]
        token[...] = jnp.zeros_like(token)

    sems = pltpu.SemaphoreType.DMA((n * N_PEERS,))
    lands = [pltpu.with_memory_space_constraint(lax.empty(shape, s.dtype), pltpu.HBM) for shape, s in zip(land_shapes, srcs)]
    out = _pcall(
        body, name=name,
        out_shape=(sems, sems, *[pltpu.HBM(s.shape, s.dtype) for s in srcs],
                   *[pltpu.HBM(shape, s.dtype) for shape, s in zip(land_shapes, srcs)],
                   jax.ShapeDtypeStruct((SUBLANES, LANES), F32)),
        in_specs=[HBM_SPEC] * (2 * n) + [pl.BlockSpec(memory_space=pl.ANY)],
        out_specs=(SEM_SPEC, SEM_SPEC, *[HBM_SPEC] * (2 * n), pl.BlockSpec(memory_space=pltpu.VMEM)),
        input_output_aliases={i: 2 + i for i in range(2 * n)},
        compiler_params=pltpu.CompilerParams(has_side_effects=pltpu.SideEffectType.DATAFLOW_SIDE_EFFECTING),
    )(*[pltpu.with_memory_space_constraint(s, pltpu.HBM) for s in srcs], *lands, after)
    return out[0], out[1], list(out[2:2 + n]), list(out[2 + n:2 + 2 * n]), out[-1]


def exchange_wait(send_sems, recv_sems, srcs, lands, after, gather, name):
    n = len(srcs)

    def body(*refs):
        for cp in _exchange_copies(refs[:n], refs[n:2 * n], refs[2 * n], refs[2 * n + 1], gather):
            cp.wait_send()
            cp.wait_recv()

    out = _pcall(
        body, name=name,
        out_shape=(*[pltpu.HBM(s.shape, s.dtype) for s in srcs], *[pltpu.HBM(z.shape, z.dtype) for z in lands]),
        in_specs=[HBM_SPEC] * (2 * n) + [SEM_SPEC, SEM_SPEC, pl.BlockSpec(memory_space=pl.ANY)],
        out_specs=tuple([HBM_SPEC] * (2 * n)),
        input_output_aliases={i: i for i in range(2 * n)},
        compiler_params=pltpu.CompilerParams(has_side_effects=pltpu.SideEffectType.DATAFLOW_SIDE_EFFECTING),
    )(*srcs, *lands, send_sems, recv_sems, after)
    return list(out[:n]), list(out[n:])


def adamw_exchanged(me, g, land, w, m, v, name):
    rows, cols = w.shape
    tr = _row_tile(rows, 256)

    def body(me_ref, g_ref, land_ref, w_ref, m_ref, v_ref, go_ref, d_ref, mo_ref, vo_ref):
        grad = g_ref[...].astype(F32)
        for k in range(1, N_DEV):
            grad = grad + land_ref[k].astype(F32)
        delta, m_new, v_new = _adam_math(w_ref[...], grad, m_ref[...], v_ref[...])
        go_ref[...] = grad
        d_ref[...] = delta
        mo_ref[...] = m_new
        vo_ref[...] = v_new

    tile = pl.BlockSpec((tr, cols), lambda r, ix: (r, 0))
    out = jax.ShapeDtypeStruct((rows, cols), F32)
    return _pcall(
        body, name=name, out_shape=[out] * 4,
        grid_spec=pltpu.PrefetchScalarGridSpec(
            num_scalar_prefetch=1, grid=(rows // tr,),
            in_specs=[pl.BlockSpec((None, tr, cols), lambda r, ix: (ix[0], r, 0)),
                      pl.BlockSpec((N_DEV, tr, cols), lambda r, ix: (0, r, 0)), tile, tile, tile],
            out_specs=[tile] * 4),
        compiler_params=_cp(("arbitrary",)),
    )(me, g, land, w, m, v)


def _adam_math(w, g, m, v):
    m = ADAM_B1 * m + (1.0 - ADAM_B1) * g
    v = ADAM_B2 * v + (1.0 - ADAM_B2) * (g * g)
    m_hat = m / (1.0 - ADAM_B1 ** ADAM_STEP)
    v_hat = v / (1.0 - ADAM_B2 ** ADAM_STEP)
    delta = -ADAM_LR * (m_hat / (jnp.sqrt(v_hat) + ADAM_EPS) + ADAM_WD * w)
    return delta, m, v


def adamw_small(parts, w, m, v, name):
    _, rows, cols = parts.shape

    def body(p_ref, w_ref, m_ref, v_ref, go_ref, d_ref, mo_ref, vo_ref):
        grad = p_ref[0]
        for k in range(1, N_DEV):
            grad = grad + p_ref[k]
        delta, m_new, v_new = _adam_math(w_ref[...], grad, m_ref[...], v_ref[...])
        go_ref[...] = grad
        d_ref[...] = delta
        mo_ref[...] = m_new
        vo_ref[...] = v_new

    out = jax.ShapeDtypeStruct((rows, cols), F32)
    return _pcall(body, name=name, out_shape=[out] * 4, compiler_params=_cp(vmem=VMEM_BIG))(parts, w, m, v)


def adamw_plain(g, w, m, v, name):
    def body(g_ref, w_ref, m_ref, v_ref, d_ref, mo_ref, vo_ref):
        delta, m_new, v_new = _adam_math(w_ref[...], g_ref[...], m_ref[...], v_ref[...])
        d_ref[...] = delta
        mo_ref[...] = m_new
        vo_ref[...] = v_new

    out = jax.ShapeDtypeStruct(w.shape, F32)
    return _pcall(body, name=name, out_shape=[out] * 3)(g, w, m, v)


def _rms_fwd(x, g):
    r = lax.rsqrt(jnp.mean(x * x, axis=-1, keepdims=True) + NORM_EPS)
    return x * r * g


def _rms_bwd(x, g, dy):
    r = lax.rsqrt(jnp.mean(x * x, axis=-1, keepdims=True) + NORM_EPS)
    xh = x * r
    t = dy * g
    dx = r * (t - xh * jnp.mean(t * xh, axis=-1, keepdims=True))
    return dx, jnp.sum(dy * xh, axis=0, keepdims=True)


def _accumulate(ref, val, first):
    @pl.when(first)
    def _():
        ref[...] = val

    @pl.when(jnp.logical_not(first))
    def _():
        ref[...] += val


def _col_chunks(width):
    return [(c0, min(MXU_WIDTH, width - c0)) for c0 in range(0, width, MXU_WIDTH)]


ANY_SPEC = pl.BlockSpec(memory_space=pl.ANY)


def ffn_forward(h, norm, w1, w3, w2, after, tm, name, meta=None):
    if meta is None:
        t_rows = h.shape[0]
        h_spec = pl.BlockSpec((tm, D_MODEL), lambda i: (i, 0))
    else:
        tiles = (h.shape[1] + N_META) // tm
        t_rows = h.shape[0] * tiles * tm
        h_spec = pl.BlockSpec((None, tm, D_MODEL), lambda i: (i // tiles, i % tiles, 0))

    def body(h_ref, g_ref, w1_ref, w3_ref, w2_ref, _, *rest):
        if meta is None:
            out_ref, hn_ref, a_ref, b_ref, hid_ref = rest
            h_in = h_ref[...]
        else:
            meta_ref, out_ref, hn_ref, a_ref, b_ref, h0_ref, hid_ref = rest
            h_in = h_ref[...]
            with_meta = jnp.concatenate([h_in[:tm - N_META], meta_ref[...]], axis=0)
            h_in = jnp.where(pl.program_id(0) % tiles == tiles - 1, with_meta, h_in)
            h0_ref[...] = h_in
        hn = _rms_fwd(h_in, g_ref[...]).astype(BF16)
        hn_ref[...] = hn
        for c0, cw in _col_chunks(D_FF):
            a = _dot_nt(hn, w1_ref[c0:c0 + cw, :])
            b = _dot_nt(hn, w3_ref[c0:c0 + cw, :])
            a_ref[:, c0:c0 + cw] = a.astype(BF16)
            b_ref[:, c0:c0 + cw] = b.astype(BF16)
            hid_ref[:, c0:c0 + cw] = (a * _sigmoid(a) * b).astype(BF16)
        out_ref[...] = h_in + 0.5 * _dot(hid_ref[...], w2_ref[...])

    row = pl.BlockSpec((tm, D_MODEL), lambda i: (i, 0))
    hid_blk = pl.BlockSpec((tm, D_FF), lambda i: (i, 0))
    weight = _resident((D_FF, D_MODEL))
    wide = jax.ShapeDtypeStruct((t_rows, D_MODEL), F32)
    extra_in = [] if meta is None else [meta]
    return _pcall(
        body, name=name, grid=(t_rows // tm,),
        in_specs=[h_spec, pl.BlockSpec((1, D_MODEL), lambda i: (0, 0)), weight, weight, weight, ANY_SPEC]
        + [pl.BlockSpec((N_META, D_MODEL), lambda i: (0, 0))] * len(extra_in),
        out_specs=[row, row, hid_blk, hid_blk] + [row] * len(extra_in),
        out_shape=[wide, jax.ShapeDtypeStruct((t_rows, D_MODEL), BF16),
                   jax.ShapeDtypeStruct((t_rows, D_FF), BF16), jax.ShapeDtypeStruct((t_rows, D_FF), BF16)]
        + [wide] * len(extra_in),
        scratch_shapes=[pltpu.VMEM((tm, D_FF), BF16)],
        compiler_params=_cp(("arbitrary",), VMEM_BIG),
    )(h, norm, w1, w3, w2, after, *extra_in)


def _resident(shape):
    return pl.BlockSpec(shape, lambda *_: (0,) * len(shape), pipeline_mode=pl.Buffered(1))


def ffn_backward_hidden(dh, a, b, w2, after, tm, name):
    t_rows = dh.shape[0]

    def body(dh_ref, a_ref, b_ref, w2_ref, _, da_ref, db_ref, dhb_ref):
        dhb = (0.5 * dh_ref[...]).astype(BF16)
        dhb_ref[...] = dhb
        for c0, cw in _col_chunks(D_FF):
            dhid = _dot_nt(dhb, w2_ref[c0:c0 + cw, :])
            av = a_ref[:, c0:c0 + cw].astype(F32)
            bv = b_ref[:, c0:c0 + cw].astype(F32)
            s = _sigmoid(av)
            da_ref[:, c0:c0 + cw] = (dhid * bv * (s * (1.0 + av * (1.0 - s)))).astype(BF16)
            db_ref[:, c0:c0 + cw] = (dhid * (av * s)).astype(BF16)

    hid = pl.BlockSpec((tm, D_FF), lambda i: (i, 0))
    row = pl.BlockSpec((tm, D_MODEL), lambda i: (i, 0))
    return _pcall(
        body, name=name, grid=(t_rows // tm,),
        in_specs=[row, hid, hid, _resident((D_FF, D_MODEL)), ANY_SPEC],
        out_specs=[hid, hid, row],
        out_shape=[jax.ShapeDtypeStruct((t_rows, D_FF), BF16), jax.ShapeDtypeStruct((t_rows, D_FF), BF16),
                   jax.ShapeDtypeStruct((t_rows, D_MODEL), BF16)],
        compiler_params=_cp(("arbitrary",), VMEM_BIG),
    )(dh, a, b, w2, after)


def ffn_backward_input(dh, h, norm, da, db, w1, w3, after, tm, name, examples=None):
    t_rows = h.shape[0]

    def body(dh_ref, h_ref, g_ref, da_ref, db_ref, w1_ref, w3_ref, _, dhin_ref, *rest):
        dg_ref = rest[-1]
        dhn = _dot(da_ref[...], w1_ref[...]) + _dot(db_ref[...], w3_ref[...])
        dx, dg = _rms_bwd(h_ref[...], g_ref[...], dhn)
        dhin = dh_ref[...] + dx
        dhin_ref[...] = dhin
        _accumulate(dg_ref, dg, pl.program_id(0) == 0)
        if examples is not None:
            @pl.when(pl.program_id(0) % tiles == tiles - 1)
            def _():
                rest[0][...] = dhin[tm - N_META:, :]

    row = pl.BlockSpec((tm, D_MODEL), lambda i: (i, 0))
    vec = pl.BlockSpec((1, D_MODEL), lambda i: (0, 0))
    hid = pl.BlockSpec((tm, D_FF), lambda i: (i, 0))
    if examples is None:
        out_specs = [row, vec]
        out_shape = [jax.ShapeDtypeStruct((t_rows, D_MODEL), F32), jax.ShapeDtypeStruct((1, D_MODEL), F32)]
    else:
        n_b, seq = examples
        tiles = (seq + N_META) // tm
        out_specs = [pl.BlockSpec((None, tm, D_MODEL), lambda i: (i // tiles, i % tiles, 0)),
                     pl.BlockSpec((None, N_META, D_MODEL), lambda i: (i // tiles, 0, 0)), vec]
        out_shape = [jax.ShapeDtypeStruct((n_b, seq, D_MODEL), F32), jax.ShapeDtypeStruct((n_b, N_META, D_MODEL), F32),
                     jax.ShapeDtypeStruct((1, D_MODEL), F32)]
    return _pcall(
        body, name=name, grid=(t_rows // tm,),
        in_specs=[row, row, vec, hid, hid, _resident((D_FF, D_MODEL)), _resident((D_FF, D_MODEL)), ANY_SPEC],
        out_specs=out_specs, out_shape=out_shape,
        compiler_params=_cp(("arbitrary",), VMEM_BIG),
    )(dh, h, norm, da, db, w1, w3, after)


def ffn_backward_weights(hn, dh, a, b, da, db, tm, tn, name):
    t_rows = hn.shape[0]
    ni = t_rows // tm
    kc = _row_tile(tm, 688)

    def body(hn_ref, dh_ref, a_ref, b_ref, da_ref, db_ref, dw1_ref, dw3_ref, dw2_ref, acc1, acc3, acc2):
        i = pl.program_id(1)
        parts = None
        for r0 in range(0, tm, kc):
            rows = slice(r0, r0 + kc)
            hn_v = hn_ref[rows, :]
            av = a_ref[rows, :].astype(F32)
            hid = (av * _sigmoid(av) * b_ref[rows, :].astype(F32)).astype(BF16)
            new = (_dot_tn(hn_v, da_ref[rows, :]), _dot_tn(hn_v, db_ref[rows, :]), _dot_tn(dh_ref[rows, :], hid))
            parts = new if parts is None else tuple(p + q for p, q in zip(parts, new))
        _accumulate(acc1, parts[0], i == 0)
        _accumulate(acc3, parts[1], i == 0)
        _accumulate(acc2, parts[2], i == 0)

        @pl.when(i == ni - 1)
        def _():
            dw1_ref[...] = acc1[...].T.astype(BF16)
            dw3_ref[...] = acc3[...].T.astype(BF16)
            dw2_ref[...] = acc2[...].T.astype(BF16)

    row = pl.BlockSpec((tm, D_MODEL), lambda j, i: (i, 0))
    hid_blk = pl.BlockSpec((tm, tn), lambda j, i: (i, j))
    w_row = pl.BlockSpec((tn, D_MODEL), lambda j, i: (j, 0))
    out = jax.ShapeDtypeStruct((D_FF, D_MODEL), BF16)
    return _pcall(
        body, name=name, grid=(D_FF // tn, ni),
        in_specs=[row, row, hid_blk, hid_blk, hid_blk, hid_blk],
        out_specs=[w_row, w_row, w_row], out_shape=[out, out, out],
        scratch_shapes=[pltpu.VMEM((D_MODEL, tn), F32)] * 3,
        compiler_params=_cp(("arbitrary", "arbitrary"), VMEM_BIG),
    )(hn, dh, a, b, da, db)


def _weight_gradient_call(body, wide, hidden, after, n_out, tm, tn, name):
    ni = wide.shape[0] // tm
    row = pl.BlockSpec((tm, D_MODEL), lambda j, i: (i, 0))
    hid_blk = pl.BlockSpec((tm, tn), lambda j, i: (i, j))
    w_row = pl.BlockSpec((tn, D_MODEL), lambda j, i: (j, 0))
    out = jax.ShapeDtypeStruct((D_FF, D_MODEL), BF16)
    return _pcall(
        body, name=name, grid=(D_FF // tn, ni),
        in_specs=[row] + [hid_blk] * len(hidden) + [ANY_SPEC],
        out_specs=[w_row] * n_out, out_shape=[out] * n_out,
        scratch_shapes=[pltpu.VMEM((D_MODEL, tn), F32)] * n_out,
        compiler_params=_cp(("arbitrary", "arbitrary"), VMEM_BIG),
    )(wide, *hidden, after)


def ffn_backward_weights_in(hn, da, db, tm, tn, name):
    ni = hn.shape[0] // tm
    kc = _row_tile(tm, 688)

    def body(hn_ref, da_ref, db_ref, _, dw1_ref, dw3_ref, acc1, acc3):
        i = pl.program_id(1)
        parts = None
        for r0 in range(0, tm, kc):
            rows = slice(r0, r0 + kc)
            hn_v = hn_ref[rows, :]
            new = (_dot_tn(hn_v, da_ref[rows, :]), _dot_tn(hn_v, db_ref[rows, :]))
            parts = new if parts is None else tuple(p + q for p, q in zip(parts, new))
        _accumulate(acc1, parts[0], i == 0)
        _accumulate(acc3, parts[1], i == 0)

        @pl.when(i == ni - 1)
        def _():
            dw1_ref[...] = acc1[...].T.astype(BF16)
            dw3_ref[...] = acc3[...].T.astype(BF16)

    return _weight_gradient_call(body, hn, [da, db], da, 2, tm, tn, name)


def ffn_backward_weights_out(dh, a, b, after, tm, tn, name):
    ni = dh.shape[0] // tm
    kc = _row_tile(tm, 688)

    def body(dh_ref, a_ref, b_ref, _, dw2_ref, acc2):
        i = pl.program_id(1)
        part = None
        for r0 in range(0, tm, kc):
            rows = slice(r0, r0 + kc)
            av = a_ref[rows, :].astype(F32)
            hid = (av * _sigmoid(av) * b_ref[rows, :].astype(F32)).astype(BF16)
            new = _dot_tn(dh_ref[rows, :], hid)
            part = new if part is None else part + new
        _accumulate(acc2, part, i == 0)

        @pl.when(i == ni - 1)
        def _():
            dw2_ref[...] = acc2[...].T.astype(BF16)

    return _weight_gradient_call(body, dh, [a, b], after, 1, tm, tn, name)[0]


def mix_forward(h, norm, wing, tm, name):
    t_rows = h.shape[0]

    def body(h_ref, g_ref, w_ref, hn_ref, p_ref):
        hn = _rms_fwd(h_ref[...], g_ref[...]).astype(BF16)
        hn_ref[...] = hn
        for j in range(N_DEV):
            p_ref[:, j * IN_BLK:(j + 1) * IN_BLK] = _dot(hn, w_ref[j]).astype(BF16)

    row = pl.BlockSpec((tm, D_MODEL), lambda i: (i, 0))
    return _pcall(
        body, name=name, grid=(t_rows // tm,),
        in_specs=[row, pl.BlockSpec((1, D_MODEL), lambda i: (0, 0)),
                  pl.BlockSpec((N_DEV, D_MODEL, IN_BLK), lambda i: (0, 0, 0))],
        out_specs=[row, pl.BlockSpec((tm, IN_WIDTH), lambda i: (i, 0))],
        out_shape=[jax.ShapeDtypeStruct((t_rows, D_MODEL), BF16), jax.ShapeDtypeStruct((t_rows, IN_WIDTH), BF16)],
        compiler_params=_cp(("arbitrary",), VMEM_BIG),
    )(h, norm, wing)


def mix_backward_act(dh, h, norm, dproj, w_in_full, tm, name):
    t_rows = h.shape[0]

    def body(dh_ref, h_ref, g_ref, dp_ref, w_ref, dhin_ref, dg_ref):
        dx, dg = _rms_bwd(h_ref[...], g_ref[...], _dot_nt(dp_ref[...], w_ref[...]))
        dhin_ref[...] = dh_ref[...] + dx
        _accumulate(dg_ref, dg, pl.program_id(0) == 0)

    row = pl.BlockSpec((tm, D_MODEL), lambda i: (i, 0))
    vec = pl.BlockSpec((1, D_MODEL), lambda i: (0, 0))
    return _pcall(
        body, name=name, grid=(t_rows // tm,),
        in_specs=[row, row, vec, pl.BlockSpec((tm, IN_WIDTH), lambda i: (i, 0)), _resident((D_MODEL, IN_WIDTH))],
        out_specs=[row, vec],
        out_shape=[jax.ShapeDtypeStruct((t_rows, D_MODEL), F32), jax.ShapeDtypeStruct((1, D_MODEL), F32)],
        compiler_params=_cp(("arbitrary",), VMEM_BIG),
    )(dh, h, norm, dproj, w_in_full)


def mix_backward_weights(hn, dproj, tm, name):
    t_rows = hn.shape[0]
    ni = t_rows // tm
    per_step = 2

    kc = _row_tile(tm, 688)

    def body(hn_ref, dp_ref, dw_ref, acc):
        i = pl.program_id(1)
        part = functools.reduce(lambda u, w: u + w, [_dot_tn(hn_ref[r0:r0 + kc, :], dp_ref[r0:r0 + kc, :])
                                                    for r0 in range(0, tm, kc)])
        _accumulate(acc, part, i == 0)

        @pl.when(i == ni - 1)
        def _():
            for k in range(per_step):
                dw_ref[k] = acc[:, k * IN_BLK:(k + 1) * IN_BLK].astype(BF16)

    return _pcall(
        body, name=name, grid=(N_DEV // per_step, ni),
        in_specs=[pl.BlockSpec((tm, D_MODEL), lambda j, i: (i, 0)),
                  pl.BlockSpec((tm, per_step * IN_BLK), lambda j, i: (i, j))],
        out_specs=pl.BlockSpec((per_step, D_MODEL, IN_BLK), lambda j, i: (j, 0, 0)),
        out_shape=jax.ShapeDtypeStruct((N_DEV, D_MODEL, IN_BLK), BF16),
        scratch_shapes=[pltpu.VMEM((D_MODEL, per_step * IN_BLK), F32)],
        compiler_params=_cp(("arbitrary", "arbitrary"), VMEM_BIG),
    )(hn, dproj)


GELU_C = 0.7978845608028654
GELU_K = 0.044715


def _gelu(x):
    return 0.5 * x * (1.0 + jnp.tanh(GELU_C * (x + GELU_K * (x * x * x))))


def _gelu_and_grad(x):
    th = jnp.tanh(GELU_C * (x + GELU_K * (x * x * x)))
    val = 0.5 * x * (1.0 + th)
    grad = 0.5 * (1.0 + th) + 0.5 * x * (1.0 - th * th) * (GELU_C * (1.0 + 3.0 * GELU_K * (x * x)))
    return val, grad


def merge_forward(h, yraw, attn, proj, glu_a, glu_b, w_out, tm, name):
    t_rows = h.shape[0]

    def body(h_ref, y_ref, at_ref, gate_ref, a_ref, b_ref, wo_ref, out_ref):
        y = _gelu(y_ref[...]).astype(BF16)
        ssm = _dot(y, a_ref[...]) * _sigmoid(_dot(y, b_ref[...]))
        ga = gate_ref[:, :D_MODEL].astype(F32)
        gs = gate_ref[:, D_MODEL:].astype(F32)
        merged = _sigmoid(ga) * at_ref[...].astype(F32) + _sigmoid(gs) * ssm
        out_ref[...] = h_ref[...] + _dot(merged.astype(BF16), wo_ref[...])

    row = pl.BlockSpec((tm, D_MODEL), lambda i: (i, 0))
    glu = pl.BlockSpec((SSM_WIDTH, D_MODEL), lambda i: (0, 0))
    return _pcall(
        body, name=name, grid=(t_rows // tm,),
        in_specs=[row, pl.BlockSpec((tm, SSM_WIDTH), lambda i: (i, 0)), row,
                  pl.BlockSpec((tm, 2 * D_MODEL), lambda i: (i, 1)), glu, glu,
                  pl.BlockSpec((D_MODEL, D_MODEL), lambda i: (0, 0))],
        out_specs=row, out_shape=jax.ShapeDtypeStruct((t_rows, D_MODEL), F32),
        compiler_params=_cp(("arbitrary",), VMEM_BIG),
    )(h, yraw, attn, proj, glu_a, glu_b, w_out)


def merge_backward(dh, yraw, attn, proj, glu_a, glu_b, w_out, after, tm, name):
    t_rows = dh.shape[0]

    def body(dh_ref, y_ref, at_ref, gate_ref, a_ref, b_ref, wo_ref, _,
             dat_ref, dy_ref, dgate_ref, d16_ref, mg_ref, y16_ref, dya_ref, dyb_ref):
        d16 = dh_ref[...].astype(BF16)
        d16_ref[...] = d16
        gel, dgel = _gelu_and_grad(y_ref[...].astype(F32))
        y16 = gel.astype(BF16)
        y16_ref[...] = y16
        dy = None
        for c0, cw in _col_chunks(D_MODEL):
            cols = slice(c0, c0 + cw)
            gcols = slice(D_MODEL + c0, D_MODEL + c0 + cw)
            dmerged = _dot_nt(d16, wo_ref[cols, :])
            ya = _dot(y16, a_ref[:, cols])
            sb = _sigmoid(_dot(y16, b_ref[:, cols]))
            ssm = ya * sb
            sa = _sigmoid(gate_ref[:, cols].astype(F32))
            ss = _sigmoid(gate_ref[:, gcols].astype(F32))
            attn_v = at_ref[:, cols].astype(F32)
            mg_ref[:, cols] = (sa * attn_v + ss * ssm).astype(BF16)
            dat_ref[:, cols] = (dmerged * sa).astype(BF16)
            dgate_ref[:, cols] = (dmerged * attn_v * sa * (1.0 - sa)).astype(BF16)
            dgate_ref[:, gcols] = (dmerged * ssm * ss * (1.0 - ss)).astype(BF16)
            dssm = dmerged * ss
            dya = (dssm * sb).astype(BF16)
            dyb = (dssm * ya * sb * (1.0 - sb)).astype(BF16)
            dya_ref[:, cols] = dya
            dyb_ref[:, cols] = dyb
            part = _dot_nt(dya, a_ref[:, cols]) + _dot_nt(dyb, b_ref[:, cols])
            dy = part if dy is None else dy + part
        dy_ref[...] = (dy * dgel).astype(BF16)

    row = pl.BlockSpec((tm, D_MODEL), lambda i: (i, 0))
    ssm_row = pl.BlockSpec((tm, SSM_WIDTH), lambda i: (i, 0))
    gates = pl.BlockSpec((tm, 2 * D_MODEL), lambda i: (i, 1))
    wide = jax.ShapeDtypeStruct((t_rows, D_MODEL), BF16)
    narrow = jax.ShapeDtypeStruct((t_rows, SSM_WIDTH), BF16)
    return _pcall(
        body, name=name, grid=(t_rows // tm,),
        in_specs=[row, ssm_row, row, gates, _resident((SSM_WIDTH, D_MODEL)), _resident((SSM_WIDTH, D_MODEL)),
                  _resident((D_MODEL, D_MODEL)), ANY_SPEC],
        out_specs=[row, ssm_row, gates, row, row, ssm_row, row, row],
        out_shape=[wide, narrow, jax.ShapeDtypeStruct((t_rows, IN_WIDTH), BF16), wide, wide, narrow, wide, wide],
        compiler_params=_cp(("arbitrary",), VMEM_BIG),
    )(dh, yraw, attn, proj, glu_a, glu_b, w_out, after)


def merge_backward_weights(d16, merged, y16, dya, dyb, after, tm, name):
    t_rows = d16.shape[0]

    def body(d_ref, mg_ref, y_ref, dya_ref, dyb_ref, _, dwo_ref, da_ref, db_ref):
        first = pl.program_id(0) == 0
        y16 = y_ref[...]
        _accumulate(dwo_ref, _dot_tn(mg_ref[...], d_ref[...]), first)
        _accumulate(da_ref, _dot_tn(y16, dya_ref[...]), first)
        _accumulate(db_ref, _dot_tn(y16, dyb_ref[...]), first)

    row = pl.BlockSpec((tm, D_MODEL), lambda i: (i, 0))
    ssm_row = pl.BlockSpec((tm, SSM_WIDTH), lambda i: (i, 0))
    glu = pl.BlockSpec((SSM_WIDTH, D_MODEL), lambda i: (0, 0))
    wo = pl.BlockSpec((D_MODEL, D_MODEL), lambda i: (0, 0))
    return _pcall(
        body, name=name, grid=(t_rows // tm,),
        in_specs=[row, row, ssm_row, row, row, ANY_SPEC], out_specs=[wo, glu, glu],
        out_shape=[jax.ShapeDtypeStruct((D_MODEL, D_MODEL), F32), jax.ShapeDtypeStruct((SSM_WIDTH, D_MODEL), F32),
                   jax.ShapeDtypeStruct((SSM_WIDTH, D_MODEL), F32)],
        compiler_params=_cp(("arbitrary",), VMEM_BIG),
    )(d16, merged, y16, dya, dyb, after)


def final_loss_backward(h, target, norm, seq, tm, name):
    t_rows = h.shape[0]
    tiles_per_example = (seq + N_META) // tm

    def body(h_ref, t_ref, g_ref, dh_ref, loss_ref, dg_ref):
        i = pl.program_id(0)
        x = h_ref[...]
        g = g_ref[...]
        r = lax.rsqrt(jnp.mean(x * x, axis=-1, keepdims=True) + NORM_EPS)
        xh = x * r
        pos = lax.broadcasted_iota(jnp.int32, (tm, 1), 0) + (i % tiles_per_example) * tm
        diff = jnp.where(pos < seq, xh * g - t_ref[...], 0.0)
        part = 0.5 * jnp.sum(jnp.sum(diff * diff, axis=-1, keepdims=True), axis=0, keepdims=True) / D_MODEL
        dy = diff / D_MODEL
        t = dy * g
        dh_ref[...] = r * (t - xh * jnp.mean(t * xh, axis=-1, keepdims=True))
        _accumulate(loss_ref, jnp.broadcast_to(part, (1, LANES)), i == 0)
        _accumulate(dg_ref, jnp.sum(dy * xh, axis=0, keepdims=True), i == 0)

    row = pl.BlockSpec((tm, D_MODEL), lambda i: (i, 0))
    vec = pl.BlockSpec((1, D_MODEL), lambda i: (0, 0))
    per_example = pl.BlockSpec((None, tm, D_MODEL), lambda i: (i // tiles_per_example, i % tiles_per_example, 0))
    return _pcall(
        body, name=name, grid=(t_rows // tm,),
        in_specs=[row, per_example, vec],
        out_specs=[row, pl.BlockSpec((1, LANES), lambda i: (0, 0)), vec],
        out_shape=[jax.ShapeDtypeStruct((t_rows, D_MODEL), F32), jax.ShapeDtypeStruct((1, LANES), F32),
                   jax.ShapeDtypeStruct((1, D_MODEL), F32)],
        compiler_params=_cp(("arbitrary",), VMEM_BIG),
    )(h, target, norm)


ATTN_SCALE = HEAD_DIM ** -0.5
STACK_HEADS = (0, 2, 1, 3)
META_PAD = LANES - N_META


def _lane_half(shape, hf):
    lane = lax.broadcasted_iota(jnp.int32, shape, 1)
    return (lane < HEAD_DIM) if hf == 0 else (lane >= HEAD_DIM)


def _kv_variants(ref, rows, kh, pad_rows=0):
    tile = kh // 2
    t = ref[rows, tile * LANES:(tile + 1) * LANES].astype(F32)
    swapped = pltpu.roll(t, HEAD_DIM, axis=1)
    at_low, at_high = (t, swapped) if kh % 2 == 0 else (swapped, t)
    lo = jnp.where(_lane_half(t.shape, 0), at_low, 0.0).astype(BF16)
    hi = jnp.where(_lane_half(t.shape, 1), at_high, 0.0).astype(BF16)
    if pad_rows:
        zeros = jnp.zeros((pad_rows, LANES), BF16)
        lo, hi = jnp.concatenate([lo, zeros], axis=0), jnp.concatenate([hi, zeros], axis=0)
    return lo, hi


def _key_tiles(ref, key_rows, kh):
    return [_kv_variants(ref, r, kh, META_PAD if i == len(key_rows) - 1 else 0) for i, r in enumerate(key_rows)]


def _to_kv_lanes(lo, hi, kh):
    lo = jnp.where(_lane_half(lo.shape, 0), lo, 0.0)
    hi = jnp.where(_lane_half(hi.shape, 1), hi, 0.0)
    if kh % 2 == 0:
        return lo + pltpu.roll(hi, HEAD_DIM, axis=1)
    return pltpu.roll(lo, HEAD_DIM, axis=1) + hi


def _stacked(ref, rows, kh):
    col = kh * 2 * LANES
    return jnp.concatenate([ref[rows, col:col + LANES], ref[rows, col + LANES:col + 2 * LANES]], axis=0)


def _sink_column(sink_ref, kh, nq):
    row = lax.broadcasted_iota(jnp.int32, (4 * nq, 1), 0)
    col = jnp.zeros((4 * nq, 1), F32)
    for quarter, g in enumerate(STACK_HEADS):
        col = jnp.where(row // nq == quarter, sink_ref[0, kh * Q_PER_KV + g], col)
    return col


def _softmax_parts(qs, key_tiles, masks, sink):
    scores = []
    for (k_lo, k_hi), mask in zip(key_tiles, masks):
        s = jnp.concatenate([_dot_nt(qs, k_lo), _dot_nt(qs, k_hi)], axis=0) * ATTN_SCALE
        scores.append(s if mask is None else jnp.where(mask, s, NEG_INF))
    m = jnp.maximum(_row_reduce(scores, jnp.maximum, jnp.max), sink)
    probs = [jnp.exp(s - m) for s in scores]
    e_sink = jnp.exp(sink - m)
    den = _row_sums(probs) + e_sink
    return probs, 1.0 / den, e_sink


def _row_reduce(tiles, combine, reduce):
    chunks = [t[:, c:c + LANES] for t in tiles for c in range(0, t.shape[-1], LANES)]
    return reduce(functools.reduce(combine, chunks), axis=-1, keepdims=True)


def _row_sums(tiles):
    return _row_reduce(tiles, lambda u, w: u + w, jnp.sum)


def _band_mask(nq, first):
    keys = BLOCK if first else 2 * BLOCK
    qi = lax.broadcasted_iota(jnp.int32, (4 * nq, keys), 0) % nq
    kj = lax.broadcasted_iota(jnp.int32, (4 * nq, keys), 1)
    if first:
        return kj <= qi
    return jnp.logical_and(kj > qi, kj <= qi + BLOCK)


def _meta_mask(nq, causal):
    qi = lax.broadcasted_iota(jnp.int32, (4 * nq, LANES), 0) % nq
    kj = lax.broadcasted_iota(jnp.int32, (4 * nq, LANES), 1)
    return jnp.logical_and(kj < N_META, kj <= qi) if causal else kj < N_META


def _attention_schedule(seq, queries, carry):
    meta_rows = pl.ds(seq, N_META)
    meta_ok = _meta_mask(BLOCK, False)
    carry = queries(pl.ds(0, BLOCK), BLOCK, [pl.ds(0, BLOCK), meta_rows], [_band_mask(BLOCK, True), meta_ok], carry)

    def block(n, c):
        r0 = pl.multiple_of(n * BLOCK, BLOCK)
        p0 = pl.multiple_of((n - 1) * BLOCK, BLOCK)
        return queries(pl.ds(r0, BLOCK), BLOCK, [pl.ds(p0, 2 * BLOCK), meta_rows], [_band_mask(BLOCK, False), meta_ok], c)

    carry = lax.fori_loop(1, seq // BLOCK, block, carry)
    return queries(meta_rows, N_META, [meta_rows], [_meta_mask(N_META, True)], carry)


def attention_forward(proj3, sinks, seq, name):
    n_b, n_l, _ = proj3.shape

    def body(sink_ref, q_ref, k_ref, v_ref, o_ref):
        def queries(q_rows, nq, key_rows, masks, carry):
            for kh in range(N_KV_HEADS):
                ks = _key_tiles(k_ref, key_rows, kh)
                vs = _key_tiles(v_ref, key_rows, kh)
                qs = _stacked(q_ref, q_rows, kh)
                probs, inv, _ = _softmax_parts(qs, ks, masks, _sink_column(sink_ref, kh, nq))
                probs = [p.astype(BF16) for p in probs]
                o_lo = functools.reduce(lambda u, w: u + w, [_dot(p[:2 * nq], v_lo) for p, (v_lo, _) in zip(probs, vs)])
                o_hi = functools.reduce(lambda u, w: u + w, [_dot(p[2 * nq:], v_hi) for p, (_, v_hi) in zip(probs, vs)])
                out = (o_lo * inv[:2 * nq] + o_hi * inv[2 * nq:]).astype(BF16)
                col = kh * 2 * LANES
                o_ref[q_rows, col:col + LANES] = out[:nq]
                o_ref[q_rows, col + LANES:col + 2 * LANES] = out[nq:]
            return carry

        _attention_schedule(seq, queries, 0)

    return _pcall(
        body, name=name, grid=(n_b,),
        in_specs=[pl.BlockSpec(memory_space=pltpu.SMEM),
                  pl.BlockSpec((None, n_l, D_MODEL), lambda b: (b, 0, 0)),
                  pl.BlockSpec((None, n_l, KV_WIDTH), lambda b: (b, 0, D_MODEL // KV_WIDTH)),
                  pl.BlockSpec((None, n_l, KV_WIDTH), lambda b: (b, 0, D_MODEL // KV_WIDTH + 1))],
        out_specs=pl.BlockSpec((None, n_l, D_MODEL), lambda b: (b, 0, 0)),
        out_shape=jax.ShapeDtypeStruct((n_b, n_l, D_MODEL), BF16),
        compiler_params=_cp(("arbitrary",), VMEM_BIG),
    )(sinks, proj3, proj3, proj3)


def attention_backward(proj3, dattn3, dproj3, sinks, after, seq, name):
    n_b, n_l, _ = proj3.shape
    qkv_width = D_MODEL + 2 * KV_WIDTH

    def body(sink_ref, q_ref, k_ref, v_ref, do_ref, _, __, dqkv_ref, dsink_ref, dk_ref, dv_ref):
        dk_ref[...] = jnp.zeros_like(dk_ref)
        dv_ref[...] = jnp.zeros_like(dv_ref)
        sub = lax.broadcasted_iota(jnp.int32, (SUBLANES, LANES), 0)
        lane = lax.broadcasted_iota(jnp.int32, (SUBLANES, LANES), 1)

        def queries(q_rows, nq, key_rows, masks, dsink):
            for kh in range(N_KV_HEADS):
                ks = _key_tiles(k_ref, key_rows, kh)
                vs = _key_tiles(v_ref, key_rows, kh)
                qs = _stacked(q_ref, q_rows, kh)
                dos = _stacked(do_ref, q_rows, kh)
                probs, inv, e_sink = _softmax_parts(qs, ks, masks, _sink_column(sink_ref, kh, nq))
                probs = [p * inv for p in probs]
                dps = [jnp.concatenate([_dot_nt(dos, v_lo), _dot_nt(dos, v_hi)], axis=0) for v_lo, v_hi in vs]
                delta = _row_sums([p * dp for p, dp in zip(probs, dps)])
                d_sink = -(e_sink * inv) * delta
                for quarter, g in enumerate(STACK_HEADS):
                    d_here = jnp.sum(d_sink[quarter * nq:(quarter + 1) * nq], axis=0, keepdims=True)
                    dsink = dsink + jnp.where(jnp.logical_and(sub == 0, lane == kh * Q_PER_KV + g), d_here, 0.0)
                dq = None
                tile = slice((kh // 2) * LANES, (kh // 2 + 1) * LANES)
                for r, p, dp, (k_lo, k_hi) in zip(key_rows, probs, dps, ks):
                    ds = (p * (dp - delta)).astype(BF16)
                    p16 = p.astype(BF16)
                    dq_x = _dot(ds[:2 * nq], k_lo) + _dot(ds[2 * nq:], k_hi)
                    dq = dq_x if dq is None else dq + dq_x
                    d_k = _to_kv_lanes(_dot_tn(ds[:2 * nq], qs), _dot_tn(ds[2 * nq:], qs), kh) * ATTN_SCALE
                    d_v = _to_kv_lanes(_dot_tn(p16[:2 * nq], dos), _dot_tn(p16[2 * nq:], dos), kh)
                    n_keys = r.size
                    dk_ref[r, tile] += d_k[:n_keys]
                    dv_ref[r, tile] += d_v[:n_keys]
                dq = (dq * ATTN_SCALE).astype(BF16)
                col = kh * 2 * LANES
                dqkv_ref[q_rows, col:col + LANES] = dq[:nq]
                dqkv_ref[q_rows, col + LANES:col + 2 * LANES] = dq[nq:]
            return dsink

        dsink_ref[...] = _attention_schedule(seq, queries, jnp.zeros((SUBLANES, LANES), F32))
        dqkv_ref[:, D_MODEL:D_MODEL + KV_WIDTH] = dk_ref[...].astype(BF16)
        dqkv_ref[:, D_MODEL + KV_WIDTH:] = dv_ref[...].astype(BF16)

    return _pcall(
        body, name=name, grid=(n_b,),
        in_specs=[pl.BlockSpec(memory_space=pltpu.SMEM),
                  pl.BlockSpec((None, n_l, D_MODEL), lambda b: (b, 0, 0)),
                  pl.BlockSpec((None, n_l, KV_WIDTH), lambda b: (b, 0, D_MODEL // KV_WIDTH)),
                  pl.BlockSpec((None, n_l, KV_WIDTH), lambda b: (b, 0, D_MODEL // KV_WIDTH + 1)),
                  pl.BlockSpec((None, n_l, D_MODEL), lambda b: (b, 0, 0)),
                  ANY_SPEC, ANY_SPEC],
        out_specs=[pl.BlockSpec((None, n_l, qkv_width), lambda b: (b, 0, 0)),
                   pl.BlockSpec((None, SUBLANES, LANES), lambda b: (b, 0, 0))],
        out_shape=[jax.ShapeDtypeStruct(dproj3.shape, BF16), jax.ShapeDtypeStruct((n_b, SUBLANES, LANES), F32)],
        scratch_shapes=[pltpu.VMEM((n_l, KV_WIDTH), F32), pltpu.VMEM((n_l, KV_WIDTH), F32)],
        input_output_aliases={5: 0},
        compiler_params=_cp(("arbitrary",), VMEM_BIG),
    )(sinks, proj3, proj3, proj3, dattn3, dproj3, after)


TAB_ROWS = 8
SCAN_UNROLL = 4


def _cmul(ar, ai, br, bi):
    return ar * br - ai * bi, ar * bi + ai * br


def _discretise(ar, ai, ls):
    step = jnp.exp(ls)
    mag = jnp.exp(ar * step)
    ang = ai * step
    cos, sin = jnp.cos(ang), jnp.sin(ang)
    lr, li = mag * cos, mag * sin
    den = ar * ar + ai * ai
    nr, ni = lr - 1.0, li
    cr = (nr * ar + ni * ai) / den
    ci = (ni * ar - nr * ai) / den
    return step, mag, lr, li, den, nr, ni, cr, ci


def _scan_tables(lr, li, reverse):
    n = lr.shape[-1]
    pw = [(lr, li)]
    for _ in range(SUBLANES - 1):
        pw.append(_cmul(pw[-1][0], pw[-1][1], lr, li))
    row = lax.broadcasted_iota(jnp.int32, (SUBLANES, n), 0)
    out = []
    for d in (1, 2, 4):
        ok = (row + d <= SUBLANES - 1) if reverse else (row >= d)
        out += [jnp.where(ok, pw[d - 1][0], 0.0), jnp.where(ok, pw[d - 1][1], 0.0)]
    cr = jnp.zeros((SUBLANES, n), F32)
    ci = jnp.zeros((SUBLANES, n), F32)
    for r in range(SUBLANES):
        e = (SUBLANES - r) if reverse else (r + 1)
        cr = jnp.where(row == r, pw[e - 1][0], cr)
        ci = jnp.where(row == r, pw[e - 1][1], ci)
    return out + [cr, ci]


def ssm_prepare(ar, ai, ls, br_t, bi_t, name):
    def body(ar_ref, ai_ref, ls_ref, br_ref, bi_ref, bbr_ref, bbi_ref, tf_ref, tr_ref):
        _, _, lr, li, _, _, _, cr, ci = _discretise(ar_ref[...], ai_ref[...], ls_ref[...])
        br, bi = br_ref[...], bi_ref[...]
        bbr_ref[...] = cr * br - ci * bi
        bbi_ref[...] = cr * bi + ci * br
        for k, t in enumerate(_scan_tables(lr, li, False)):
            tf_ref[k] = t
        for k, t in enumerate(_scan_tables(lr, -li, True)):
            tr_ref[k] = t

    return _pcall(
        body, name=name,
        out_shape=[jax.ShapeDtypeStruct((SSM_GROUP, N_STATES), F32), jax.ShapeDtypeStruct((SSM_GROUP, N_STATES), F32),
                   jax.ShapeDtypeStruct((TAB_ROWS, SUBLANES, N_STATES), F32),
                   jax.ShapeDtypeStruct((TAB_ROWS, SUBLANES, N_STATES), F32)],
    )(ar, ai, ls, br_t, bi_t)


def ssm_param_backward(ar, ai, ls, br_t, bi_t, dlr_p, dli_p, dbbr, dbbi, group_sum, name):
    def body(ar_ref, ai_ref, ls_ref, br_ref, bi_ref, dlr_ref, dli_ref, dbbr_ref, dbbi_ref, gs_ref,
             dar_ref, dai_ref, dls_ref, dbr_ref, dbi_ref):
        ar, ai = ar_ref[...], ai_ref[...]
        step, mag, lr, li, den, nr, ni, cr, ci = _discretise(ar, ai, ls_ref[...])
        br, bi, dbbr_v, dbbi_v = br_ref[...], bi_ref[...], dbbr_ref[...], dbbi_ref[...]
        dbr_ref[...] = cr * dbbr_v + ci * dbbi_v
        dbi_ref[...] = cr * dbbi_v - ci * dbbr_v
        dcr = jnp.sum(dbbr_v * br + dbbi_v * bi, axis=0, keepdims=True)
        dci = jnp.sum(dbbi_v * br - dbbr_v * bi, axis=0, keepdims=True)
        dnr = (dcr * ar - dci * ai) / den
        dni = (dcr * ai + dci * ar) / den
        dden = -(cr * dcr + ci * dci) / den
        dar = (dcr * nr + dci * ni) / den + dden * 2.0 * ar
        dai = (dcr * ni - dci * nr) / den + dden * 2.0 * ai
        dlr = jnp.sum(dlr_ref[...], axis=0, keepdims=True) + dnr
        dli = jnp.sum(dli_ref[...], axis=0, keepdims=True) + dni
        dmag = (dlr * lr + dli * li) / mag
        dang = dli * lr - dlr * li
        dar_ref[...] = dar + dmag * mag * step
        dai_ref[...] = dai + dang * step
        dstep = dmag * mag * ar + dang * ai
        dls_ref[...] = jnp.dot(dstep * step, gs_ref[...], preferred_element_type=F32, precision=lax.Precision.HIGHEST)

    vec = jax.ShapeDtypeStruct((1, N_STATES), F32)
    mat = jax.ShapeDtypeStruct((SSM_GROUP, N_STATES), F32)
    return _pcall(body, name=name, out_shape=[vec, vec, jax.ShapeDtypeStruct((1, LANES), F32), mat, mat])(
        ar, ai, ls, br_t, bi_t, dlr_p, dli_p, dbbr, dbbi, group_sum)


def _scan_rows(a, b, tabs, carry, reverse):
    for k, d in enumerate((1, 2, 4)):
        shift = SUBLANES - d if reverse else d
        sr, si = pltpu.roll(a, shift, axis=0), pltpu.roll(b, shift, axis=0)
        pr, pi = _cmul(tabs[2 * k], tabs[2 * k + 1], sr, si)
        a, b = a + pr, b + pi
    pr, pi = _cmul(tabs[6], tabs[7], carry[0], carry[1])
    return a + pr, b + pi


def _time_groups(seq, reverse):
    meta = [seq + SUBLANES * g for g in range(N_META // SUBLANES)]
    return meta[::-1] if reverse else meta


def ssm_forward_scan(proj3, b_comb, tabf, c_comb, dvec, seq, name):
    n_b, n_l, _ = proj3.shape
    u_blk = (D_MODEL + 2 * KV_WIDTH) // LANES

    def body(u_ref, b_ref, tab_ref, c_ref, d_ref, x_ref, y_ref, bu, xs):
        j = pl.program_id(1)
        u = u_ref[...]
        bu[...] = _dot(u, b_ref[...])
        tabs = [tab_ref[k] for k in range(TAB_ROWS)]

        def group(r0, carry):
            rows = pl.ds(r0, SUBLANES)
            a, b = _scan_rows(bu[rows, :SCAN_COLS], bu[rows, SCAN_COLS:], tabs, carry, False)
            xs[rows, :SCAN_COLS] = a
            xs[rows, SCAN_COLS:] = b
            return (jnp.broadcast_to(a[SUBLANES - 1:, :], a.shape), jnp.broadcast_to(b[SUBLANES - 1:, :], b.shape))

        zero = jnp.zeros((SUBLANES, SCAN_COLS), F32)
        carry = (zero, zero)
        for r0 in _time_groups(seq, False):
            carry = group(r0, carry)
        span = SCAN_UNROLL * SUBLANES

        def groups(t, c):
            for k in range(SCAN_UNROLL):
                c = group(pl.multiple_of(t * span, span) + k * SUBLANES, c)
            return c

        lax.fori_loop(0, seq // span, groups, carry)
        x16 = xs[...].astype(BF16)
        x_ref[...] = x16
        contrib = _dot(x16, c_ref[...])

        @pl.when(j % 2 == 0)
        def _():
            y_ref[...] = contrib + d_ref[...] * u.astype(F32)

        @pl.when(j % 2 == 1)
        def _():
            y_ref[...] += contrib

    return _pcall(
        body, name=name, grid=(n_b, N_SCAN_BLK),
        in_specs=[pl.BlockSpec((None, n_l, LANES), lambda b, j: (b, 0, u_blk + j // 2)),
                  pl.BlockSpec((None, LANES, 2 * SCAN_COLS), lambda b, j: (j, 0, 0)),
                  pl.BlockSpec((TAB_ROWS, SUBLANES, SCAN_COLS), lambda b, j: (0, 0, j)),
                  pl.BlockSpec((None, 2 * SCAN_COLS, LANES), lambda b, j: (j, 0, 0)),
                  pl.BlockSpec((1, LANES), lambda b, j: (0, j // 2))],
        out_specs=[pl.BlockSpec((None, n_l, 2 * SCAN_COLS), lambda b, j: (b, 0, j)),
                   pl.BlockSpec((None, n_l, LANES), lambda b, j: (b, 0, j // 2))],
        out_shape=[jax.ShapeDtypeStruct((n_b, n_l, 2 * N_STATES), BF16),
                   jax.ShapeDtypeStruct((n_b, n_l, SSM_WIDTH), F32)],
        scratch_shapes=[pltpu.VMEM((n_l, 2 * SCAN_COLS), F32)] * 2,
        compiler_params=_cp(("arbitrary", "arbitrary"), VMEM_BIG),
    )(proj3, b_comb, tabf, c_comb, dvec)


def ssm_backward_scan(dyraw3, xs3, dproj3, c_comb_t, tabr, b_comb_t, dvec, seq, name):
    n_b, n_l, _ = xs3.shape
    u_blk = (D_MODEL + 2 * KV_WIDTH) // LANES

    def body(dy_ref, x_ref, _, c_ref, tab_ref, b_ref, d_ref, du_ref, g_ref, dlr_ref, dli_ref, dx, gs, xs, du_acc):
        j = pl.program_id(1)
        dy = dy_ref[...]
        dx[...] = _dot(dy, c_ref[...])
        xs[...] = x_ref[...].astype(F32)
        tabs = [tab_ref[k] for k in range(TAB_ROWS)]
        last_row = lax.broadcasted_iota(jnp.int32, (SUBLANES, SCAN_COLS), 0) == SUBLANES - 1

        def group(r0, state):
            cr, ci, acc_r, acc_i = state
            rows = pl.ds(r0, SUBLANES)
            a, b = _scan_rows(dx[rows, :SCAN_COLS], dx[rows, SCAN_COLS:], tabs, (cr, ci), True)
            gs[rows, :SCAN_COLS] = a
            gs[rows, SCAN_COLS:] = b
            na = jnp.where(last_row, cr, pltpu.roll(a, SUBLANES - 1, axis=0))
            nb = jnp.where(last_row, ci, pltpu.roll(b, SUBLANES - 1, axis=0))
            xa, xb = xs[rows, :SCAN_COLS], xs[rows, SCAN_COLS:]
            return (jnp.broadcast_to(a[:1, :], a.shape), jnp.broadcast_to(b[:1, :], b.shape),
                    acc_r + na * xa + nb * xb, acc_i + nb * xa - na * xb)

        zero = jnp.zeros((SUBLANES, SCAN_COLS), F32)
        span = SCAN_UNROLL * SUBLANES
        n_spans = seq // span

        def groups(t, s):
            for k in reversed(range(SCAN_UNROLL)):
                s = group(pl.multiple_of((n_spans - 1 - t) * span, span) + k * SUBLANES, s)
            return s

        state = lax.fori_loop(0, n_spans, groups, (zero, zero, zero, zero))
        for r0 in _time_groups(seq, True):
            state = group(r0, state)
        dlr_ref[...] = state[2]
        dli_ref[...] = state[3]
        g16 = gs[...].astype(BF16)
        g_ref[...] = g16
        contrib = _dot(g16, b_ref[...])

        @pl.when(j % 2 == 0)
        def _():
            du_acc[...] = contrib + d_ref[...] * dy.astype(F32)

        @pl.when(j % 2 == 1)
        def _():
            du_ref[...] = (du_acc[...] + contrib).astype(BF16)

    state_blk = pl.BlockSpec((None, n_l, 2 * SCAN_COLS), lambda b, j: (b, 0, j))
    dl_blk = pl.BlockSpec((None, SUBLANES, SCAN_COLS), lambda b, j: (b, 0, j))
    return _pcall(
        body, name=name, grid=(n_b, N_SCAN_BLK),
        in_specs=[pl.BlockSpec((None, n_l, LANES), lambda b, j: (b, 0, j // 2)), state_blk,
                  pl.BlockSpec(memory_space=pl.ANY),
                  pl.BlockSpec((None, LANES, 2 * SCAN_COLS), lambda b, j: (j, 0, 0)),
                  pl.BlockSpec((TAB_ROWS, SUBLANES, SCAN_COLS), lambda b, j: (0, 0, j)),
                  pl.BlockSpec((None, 2 * SCAN_COLS, LANES), lambda b, j: (j, 0, 0)),
                  pl.BlockSpec((1, LANES), lambda b, j: (0, j // 2))],
        out_specs=[pl.BlockSpec((None, n_l, LANES), lambda b, j: (b, 0, u_blk + j // 2)), state_blk, dl_blk, dl_blk],
        out_shape=[jax.ShapeDtypeStruct(dproj3.shape, BF16), jax.ShapeDtypeStruct((n_b, n_l, 2 * N_STATES), BF16),
                   jax.ShapeDtypeStruct((n_b, SUBLANES, N_STATES), F32), jax.ShapeDtypeStruct((n_b, SUBLANES, N_STATES), F32)],
        scratch_shapes=[pltpu.VMEM((n_l, 2 * SCAN_COLS), F32)] * 3 + [pltpu.VMEM((n_l, LANES), F32)],
        input_output_aliases={2: 0},
        compiler_params=_cp(("arbitrary", "arbitrary"), VMEM_BIG),
    )(dyraw3, xs3, dproj3, c_comb_t, tabr, b_comb_t, dvec)


def ssm_param_grads(proj, gs, xs, dyraw, tm, name):
    t_rows = proj.shape[0]
    ni = t_rows // tm
    u_blk = (D_MODEL + 2 * KV_WIDTH) // LANES
    width = 2 * SCAN_COLS

    def body(u_ref, g_ref, x_ref, dy_ref, db_ref, dc_ref, dd_ref):
        cb, i = pl.program_id(0), pl.program_id(1)
        u, dy = u_ref[...], dy_ref[...]
        _accumulate(db_ref, _dot_tn(u, g_ref[...]), i == 0)
        _accumulate(dc_ref, _dot_tn(x_ref[...], dy), i == 0)

        @pl.when(cb % 2 == 0)
        def _():
            _accumulate(dd_ref, jnp.sum(dy.astype(F32) * u.astype(F32), axis=0, keepdims=True), i == 0)

    return _pcall(
        body, name=name, grid=(N_SCAN_BLK, ni),
        in_specs=[pl.BlockSpec((tm, LANES), lambda cb, i: (i, u_blk + cb // 2)),
                  pl.BlockSpec((tm, width), lambda cb, i: (i, cb)),
                  pl.BlockSpec((tm, width), lambda cb, i: (i, cb)),
                  pl.BlockSpec((tm, LANES), lambda cb, i: (i, cb // 2))],
        out_specs=[pl.BlockSpec((None, LANES, width), lambda cb, i: (cb, 0, 0)),
                   pl.BlockSpec((None, width, LANES), lambda cb, i: (cb, 0, 0)),
                   pl.BlockSpec((1, LANES), lambda cb, i: (0, cb // 2))],
        out_shape=[jax.ShapeDtypeStruct((N_SCAN_BLK, LANES, width), F32),
                   jax.ShapeDtypeStruct((N_SCAN_BLK, width, LANES), F32), jax.ShapeDtypeStruct((1, SSM_WIDTH), F32)],
        compiler_params=_cp(("arbitrary", "arbitrary"), VMEM_BIG),
    )(proj, gs, xs, dyraw)


def sum_leading(x, name):
    def body(x_ref, o_ref):
        acc = x_ref[0]
        for k in range(1, x.shape[0]):
            acc = acc + x_ref[k]
        o_ref[...] = acc

    return _pcall(body, name=name, out_shape=jax.ShapeDtypeStruct(x.shape[1:], x.dtype))(x)


WEIGHTS = ['meta_tokens', 'ffn1_norm', 'ffn1_w1', 'ffn1_w3', 'ffn1_w2', 'mix_norm', 'w_in', 'attn_sinks', 'ssm_a_re',
           'ssm_a_im', 'ssm_log_step', 'ssm_b_re', 'ssm_b_im', 'ssm_c_re', 'ssm_c_im', 'ssm_d', 'ssm_glu_a', 'ssm_glu_b',
           'w_out', 'ffn2_norm', 'ffn2_w1', 'ffn2_w3', 'ffn2_w2', 'final_norm']
SHARDED = ['ffn1_w1', 'ffn1_w3', 'ffn1_w2', 'ffn2_w1', 'ffn2_w3', 'ffn2_w2', 'w_in', 'ssm_glu_a', 'ssm_glu_b', 'w_out']
REPLICATED = ['ffn1_norm', 'mix_norm', 'ffn2_norm', 'final_norm', 'attn_sinks', 'ssm_a_re', 'ssm_a_im', 'ssm_log_step',
              'ssm_b_re', 'ssm_b_im', 'ssm_c_re', 'ssm_c_im', 'ssm_d']
PACK_COLS = 1024


def _pack(arrays):
    parts = []
    for a in arrays:
        flat = a.reshape(-1)
        chunk = SUBLANES * PACK_COLS
        padded = -(-flat.shape[0] // chunk) * chunk
        parts.append(jnp.pad(flat, (0, padded - flat.shape[0])).reshape(-1, PACK_COLS))
    return jnp.concatenate(parts, axis=0)


def _unpack(packed, shapes):
    out, row = [], 0
    for shape in shapes:
        size = 1
        for s in shape:
            size *= s
        chunk = SUBLANES * PACK_COLS
        rows = -(-size // chunk) * SUBLANES
        out.append(packed[row:row + rows].reshape(-1)[:size].reshape(shape))
        row += rows
    return out


def kernel(x, meta_tokens, ffn1_norm, ffn1_w1, ffn1_w3, ffn1_w2, mix_norm, w_in, attn_sinks, ssm_a_re, ssm_a_im, ssm_log_step, ssm_b_re, ssm_b_im, ssm_c_re, ssm_c_im, ssm_d, ssm_glu_a, ssm_glu_b, w_out, ffn2_norm, ffn2_w1, ffn2_w3, ffn2_w2, final_norm, loss_target, m_meta_tokens, m_ffn1_norm, m_ffn1_w1, m_ffn1_w3, m_ffn1_w2, m_mix_norm, m_w_in, m_attn_sinks, m_ssm_a_re, m_ssm_a_im, m_ssm_log_step, m_ssm_b_re, m_ssm_b_im, m_ssm_c_re, m_ssm_c_im, m_ssm_d, m_ssm_glu_a, m_ssm_glu_b, m_w_out, m_ffn2_norm, m_ffn2_w1, m_ffn2_w3, m_ffn2_w2, m_final_norm, v_meta_tokens, v_ffn1_norm, v_ffn1_w1, v_ffn1_w3, v_ffn1_w2, v_mix_norm, v_w_in, v_attn_sinks, v_ssm_a_re, v_ssm_a_im, v_ssm_log_step, v_ssm_b_re, v_ssm_b_im, v_ssm_c_re, v_ssm_c_im, v_ssm_d, v_ssm_glu_a, v_ssm_glu_b, v_w_out, v_ffn2_norm, v_ffn2_w1, v_ffn2_w3, v_ffn2_w2, v_final_norm):
    given = dict(locals())
    w = {n: given[n] for n in WEIGHTS}
    m = {n: given["m_" + n] for n in WEIGHTS}
    v = {n: given["v_" + n] for n in WEIGHTS}

    n_b, seq, _ = x.shape
    n_l = seq + N_META
    t_rows = n_b * n_l
    tm = _row_tile(n_l, 688)
    px, py, pc = _my_place()
    me = 4 * px + 2 * py + pc

    glu = jnp.stack([ssm_glu_a[0], ssm_glu_b[0]]).astype(BF16)
    ffn_names = ['ffn1_w1', 'ffn1_w3', 'ffn1_w2', 'ffn2_w1', 'ffn2_w3', 'ffn2_w2']

    def hidden_on_rows(n, t):
        return t[0] if n.endswith('w2') else t[0].T

    def hidden_on_rows_back(n, t):
        return t[None] if n.endswith('w2') else t.T[None]

    me_idx = jnp.reshape(me, (1,)).astype(jnp.int32)
    first_names, later_names = ffn_names[:3], ffn_names[3:]
    *first, metag = all_gather_list(
        [hidden_on_rows(n, w[n]).astype(BF16) for n in first_names] + [meta_tokens], meta_tokens, "ag_first")
    win_send, win_recv, win_shard, win_land, win_token = exchange_start(
        [w_in[0].astype(BF16)], first[0], True, "ag_w_in_start")
    later_shards = [hidden_on_rows(n, w[n]).astype(BF16) for n in later_names] + [glu, w_out[0].astype(BF16)]
    ag_send, ag_recv, later_shards, later_lands, ag_token = exchange_start(later_shards, win_token, True, "ag_later_start")
    full = {n: g.reshape(D_FF, D_MODEL) for n, g in zip(first_names, first)}
    meta_full = metag.transpose(1, 0, 2).reshape(N_META, D_MODEL)

    final_g = final_norm.reshape(1, D_MODEL)

    ar = ssm_a_re.reshape(1, N_STATES)
    ai = ssm_a_im.reshape(1, N_STATES)
    ls = jnp.repeat(ssm_log_step.reshape(SSM_GROUPS), SSM_STATE).reshape(1, N_STATES)
    br_t = ssm_b_re[0].transpose(2, 0, 1).reshape(SSM_GROUP, N_STATES)
    bi_t = ssm_b_im[0].transpose(2, 0, 1).reshape(SSM_GROUP, N_STATES)
    bbr, bbi, tabf, tabr = ssm_prepare(ar, ai, ls, br_t, bi_t, "ssm_prepare")
    bbr_g = bbr.reshape(SSM_GROUP, SSM_GROUPS, SSM_STATE).transpose(1, 0, 2)
    bbi_g = bbi.reshape(SSM_GROUP, SSM_GROUPS, SSM_STATE).transpose(1, 0, 2)
    groups_per_blk = SCAN_COLS // SSM_STATE
    half = ((jnp.arange(N_SCAN_BLK) % 2)[:, None] == jnp.arange(2)[None, :]).astype(F32)
    eye = jnp.eye(groups_per_blk, dtype=F32)

    def scan_blocks(re_g, im_g):
        def one(t):
            t = t.reshape(N_SCAN_BLK, groups_per_blk, SSM_GROUP, SSM_STATE)
            t = t[:, :, :, None, :] * eye[None, :, None, :, None]
            t = t.reshape(N_SCAN_BLK, LANES // 2, SCAN_COLS)
            return (t[:, None] * half[:, :, None, None]).reshape(N_SCAN_BLK, LANES, SCAN_COLS)
        return jnp.concatenate([one(re_g), one(im_g)], axis=-1).astype(BF16)

    b_comb = scan_blocks(bbr_g, bbi_g)
    c_comb_t = scan_blocks(ssm_c_re[0], -ssm_c_im[0])
    b_comb_t, c_comb = b_comb.transpose(0, 2, 1), c_comb_t.transpose(0, 2, 1)

    ffn1_w = (full['ffn1_w1'], full['ffn1_w3'], full['ffn1_w2'])
    h1, hn1, a1, b1, h0 = ffn_forward(x, ffn1_norm, *ffn1_w, ag_token, tm, "ffn1_fwd", meta=meta_full)
    win_shard, (wing,) = exchange_wait(win_send, win_recv, win_shard, win_land, h1, True, "ag_w_in_wait")
    wing = lax.dynamic_update_slice_in_dim(wing, win_shard[0][None], me, axis=0)
    hnm, proj = mix_forward(h1, mix_norm, wing, tm, "mix_fwd")
    proj3 = proj.reshape(n_b, n_l, IN_WIDTH)
    attn3 = attention_forward(proj3, attn_sinks, seq, "attn_fwd")
    attn = attn3.reshape(t_rows, D_MODEL)
    xs3, yraw3 = ssm_forward_scan(proj3, b_comb, tabf, c_comb, ssm_d, seq, "ssm_fwd")
    yraw = yraw3.reshape(t_rows, SSM_WIDTH)
    later_shards, later = exchange_wait(ag_send, ag_recv, later_shards, later_lands, yraw3, True, "ag_later_wait")
    later = [lax.dynamic_update_slice_in_dim(z, s[None], me, axis=0) for z, s in zip(later, later_shards)]
    for n, g in zip(later_names, later):
        full[n] = g.reshape(D_FF, D_MODEL)
    ffn2_w = (full['ffn2_w1'], full['ffn2_w3'], full['ffn2_w2'])
    glug, wog = later[len(later_names):]
    glu_a = glug[:, 0].transpose(1, 0, 2).reshape(SSM_WIDTH, D_MODEL)
    glu_b = glug[:, 1].transpose(1, 0, 2).reshape(SSM_WIDTH, D_MODEL)
    w_out_full = wog.reshape(D_MODEL, D_MODEL)
    h2 = merge_forward(h1, yraw, attn, proj, glu_a, glu_b, w_out_full, tm, "merge_fwd")
    h3, hn2, a2, b2 = ffn_forward(h2, ffn2_norm, *ffn2_w, ag_token, tm, "ffn2_fwd")
    dh3, loss_part, g_final = final_loss_backward(h3, loss_target, final_g, seq, tm, "loss_bwd")
    loss = lax.psum(loss_part[0, 0], ("x", "y", "c"))

    def blocked_ffn(*grads_by_hidden_row):
        return tuple(t.reshape(N_DEV, FF_BLK, D_MODEL) for t in grads_by_hidden_row)

    def blocked_cols(full_grad):
        r = full_grad.shape[0]
        return full_grad.reshape(r, N_DEV, full_grad.shape[1] // N_DEV).transpose(1, 0, 2).astype(BF16)

    early = {}

    def start_reduce(names, tag):
        srcs = [dw[n] for n in names]
        send, recv, srcs, lands, token = exchange_start(srcs, srcs[0], False, "rs_" + tag + "_start")
        early[tag] = (names, send, recv, srcs, lands)
        return token

    dw = {}
    da2, db2, dh3_half = ffn_backward_hidden(dh3, a2, b2, ffn2_w[2], g_final, tm, "ffn2_bwd_hid")
    dw['ffn2_w1'], dw['ffn2_w3'], dw['ffn2_w2'] = blocked_ffn(
        *ffn_backward_weights(hn2, dh3_half, a2, b2, da2, db2, n_l, FF_BWD_COLS, "ffn2_bwd_w"))
    token = start_reduce(later_names, "ffn2")
    dh2, g_ffn2_norm = ffn_backward_input(dh3, h2, ffn2_norm, da2, db2, ffn2_w[0], ffn2_w[1], token, tm, "ffn2_bwd_in")
    dattn, dyraw, dproj, *for_weights = merge_backward(dh2, yraw, attn, proj, glu_a, glu_b, w_out_full, token, tm,
                                                       "merge_bwd")
    dproj3 = dproj.reshape(n_b, n_l, IN_WIDTH)
    dproj3, dsink_p = attention_backward(proj3, dattn.reshape(n_b, n_l, D_MODEL), dproj3, attn_sinks, token, seq,
                                         "attn_bwd")
    dproj3, gs3, dlr_p, dli_p = ssm_backward_scan(
        dyraw.reshape(n_b, n_l, SSM_WIDTH), xs3, dproj3, c_comb_t, tabr, b_comb_t, ssm_d, seq, "ssm_bwd")
    dproj = dproj3.reshape(t_rows, IN_WIDTH)
    d_bd, d_cd, g_d = ssm_param_grads(proj, gs3.reshape(t_rows, 2 * N_STATES), xs3.reshape(t_rows, 2 * N_STATES),
                                      dyraw, n_l, "ssm_bwd_w")
    w_in_full = wing.transpose(1, 0, 2).reshape(D_MODEL, IN_WIDTH)
    dh1, g_mix_norm = mix_backward_act(dh2, h1, mix_norm, dproj, w_in_full, tm, "mix_bwd_act")
    dw['w_in'] = mix_backward_weights(hnm, dproj, n_l, "mix_bwd_w")
    token = start_reduce(['w_in'], "w_in")

    def group_blocks(part, channels_first):
        if channels_first:
            t = jnp.sum(part.reshape(N_SCAN_BLK, 2, LANES // 2, SCAN_COLS) * half[:, :, None, None], axis=1)
            t = t.reshape(N_SCAN_BLK, groups_per_blk, SSM_GROUP, groups_per_blk, SSM_STATE)
            t = jnp.sum(t * eye[None, :, None, :, None], axis=3)
            return t.reshape(SSM_GROUPS, SSM_GROUP, SSM_STATE)
        t = jnp.sum(part.reshape(N_SCAN_BLK, SCAN_COLS, 2, LANES // 2) * half[:, None, :, None], axis=2)
        t = t.reshape(N_SCAN_BLK, groups_per_blk, SSM_STATE, groups_per_blk, SSM_GROUP)
        t = jnp.sum(t * eye[None, :, None, :, None], axis=3)
        return t.reshape(SSM_GROUPS, SSM_STATE, SSM_GROUP).transpose(0, 2, 1)

    dbbr = group_blocks(d_bd[:, :, :SCAN_COLS], True).transpose(1, 0, 2).reshape(SSM_GROUP, N_STATES)
    dbbi = group_blocks(d_bd[:, :, SCAN_COLS:], True).transpose(1, 0, 2).reshape(SSM_GROUP, N_STATES)
    g_c_re = group_blocks(d_cd[:, :SCAN_COLS, :], False)[None]
    g_c_im = -group_blocks(d_cd[:, SCAN_COLS:, :], False)[None]
    group_sum = (jnp.arange(N_STATES)[:, None] // SSM_STATE == jnp.arange(LANES)[None, :]).astype(F32)
    g_ar, g_ai, g_ls, g_br, g_bi = ssm_param_backward(
        ar, ai, ls, br_t, bi_t, dlr_p.reshape(n_b * SUBLANES, N_STATES), dli_p.reshape(n_b * SUBLANES, N_STATES),
        dbbr, dbbi, group_sum, "ssm_bwd_params")
    g_sinks = sum_leading(dsink_p, "sink_sum")[0:1, :N_KV_HEADS * Q_PER_KV]

    small = {
        'mix_norm': g_mix_norm, 'ffn2_norm': g_ffn2_norm, 'final_norm': g_final.reshape(D_MODEL),
        'attn_sinks': g_sinks, 'ssm_a_re': g_ar.reshape(1, SSM_GROUPS, SSM_STATE), 'ssm_a_im': g_ai.reshape(1, SSM_GROUPS, SSM_STATE),
        'ssm_log_step': g_ls[:, :SSM_GROUPS],
        'ssm_b_re': g_br.reshape(SSM_GROUP, SSM_GROUPS, SSM_STATE).transpose(1, 2, 0)[None],
        'ssm_b_im': g_bi.reshape(SSM_GROUP, SSM_GROUPS, SSM_STATE).transpose(1, 2, 0)[None],
        'ssm_c_re': g_c_re, 'ssm_c_im': g_c_im, 'ssm_d': g_d,
    }
    early_small = [n for n in REPLICATED if n in small]
    sg_send, sg_recv, sg_src, sg_land, token = exchange_start(
        [_pack([small[n] for n in early_small])], token, True, "ag_small_start")
    da1, db1, dh1_half = ffn_backward_hidden(dh1, a1, b1, ffn1_w[2], token, tm, "ffn1_bwd_hid")
    d_w1t, d_w3t = ffn_backward_weights_in(hn1, da1, db1, n_l, FF_BWD_COLS, "ffn1_bwd_w_in")
    dw['ffn1_w1'], dw['ffn1_w3'] = blocked_ffn(d_w1t, d_w3t)
    token = start_reduce(first_names[:2], "ffn1_in")
    (dw['ffn1_w2'],) = blocked_ffn(
        ffn_backward_weights_out(dh1_half, a1, b1, token, n_l, FF_BWD_COLS, "ffn1_bwd_w_out"))
    token = start_reduce(first_names[2:], "ffn1_out")
    grad_x, meta_rows_grad, g_ffn1_norm = ffn_backward_input(
        dh1, h0, ffn1_norm, da1, db1, ffn1_w[0], ffn1_w[1], token, tm, "ffn1_bwd_in", examples=(n_b, seq))
    g_meta = sum_leading(meta_rows_grad, "meta_sum")
    d_wo, d_ga, d_gb = merge_backward_weights(*for_weights, g_meta, tm, "merge_bwd_w")
    dw['ssm_glu_a'] = blocked_cols(d_ga)
    dw['ssm_glu_b'] = blocked_cols(d_gb)
    dw['w_out'] = d_wo.reshape(N_DEV, D_MODEL // N_DEV, D_MODEL).astype(BF16)
    token = start_reduce(['ssm_glu_a', 'ssm_glu_b', 'w_out'], "mix")

    grads, deltas, new_m, new_v = {}, {}, {}, {}

    def views(n):
        if n in ffn_names:
            return functools.partial(hidden_on_rows, n), functools.partial(hidden_on_rows_back, n)
        return (lambda t: t[0]), (lambda t: t[None])

    def finish_reduce(tag, previous):
        names, send, recv, srcs, lands = early[tag]
        srcs, lands = exchange_wait(send, recv, srcs, lands, previous, False, "rs_" + tag + "_wait")
        for n, g, land in zip(names, srcs, lands):
            two_d, back = views(n)
            out = adamw_exchanged(me_idx, g, land, two_d(w[n]), two_d(m[n]), two_d(v[n]), "adamw_" + n)
            grads[n], deltas[n], new_m[n], new_v[n] = (back(o) for o in out)
            previous = out[1]
        return previous

    sl_send, sl_recv, sl_src, sl_land, previous = exchange_start(
        [_pack([g_ffn1_norm, g_meta])], token, True, "ag_small_late_start")
    for tag in ("ffn2", "w_in"):
        previous = finish_reduce(tag, previous)

    zeros_meta = jnp.zeros((N_META, D_MODEL), F32)
    sl_src, (late_parts,) = exchange_wait(sl_send, sl_recv, sl_src, sl_land, previous, True, "ag_small_late_wait")
    late_parts = lax.dynamic_update_slice_in_dim(late_parts, sl_src[0][None], me, axis=0)
    sg_src, (early_parts,) = exchange_wait(sg_send, sg_recv, sg_src, sg_land, late_parts, True, "ag_small_wait")
    early_parts = lax.dynamic_update_slice_in_dim(early_parts, sg_src[0][None], me, axis=0)

    def small_update(parts, names, extra, tag):
        pack_of = lambda d: _pack([d[n] for n in names] + extra)
        packed = adamw_small(parts, pack_of(w), pack_of(m), pack_of(v), "adamw_small_" + tag)
        unpacked = [_unpack(p, [w[n].shape for n in names] + [e.shape for e in extra]) for p in packed]
        for k, n in enumerate(names):
            grads[n], deltas[n], new_m[n], new_v[n] = (u[k] for u in unpacked)
        return packed, unpacked

    small_update(early_parts, early_small, [], "early")
    packed_out, unpacked = small_update(late_parts, ['ffn1_norm'], [zeros_meta], "late")
    g_meta_full = unpacked[0][-1]
    grads['meta_tokens'] = lax.dynamic_index_in_dim(
        g_meta_full.reshape(N_META, N_DEV, D_MODEL // N_DEV), me, axis=1, keepdims=False)
    deltas['meta_tokens'], new_m['meta_tokens'], new_v['meta_tokens'] = adamw_plain(
        grads['meta_tokens'], w['meta_tokens'], m['meta_tokens'], v['meta_tokens'], "adamw_meta")

    finish_reduce("mix", finish_reduce("ffn1_out", finish_reduce("ffn1_in", packed_out[0])))

    return (loss, grad_x, *[grads[n] for n in WEIGHTS], *[deltas[n] for n in WEIGHTS],
            *[new_m[n] for n in WEIGHTS], *[new_v[n] for n in WEIGHTS])
```

```python
import functools

import jax
import jax.numpy as jnp
from jax import lax
from jax.experimental import pallas as pl
from jax.experimental.pallas import tpu as pltpu

F32 = jnp.float32
BF16 = jnp.bfloat16
MESH = pl.DeviceIdType.MESH

N_DEV = 8
D_MODEL = 1024
N_META = 16
HEAD_DIM = 64
N_KV_HEADS = 4
Q_PER_KV = 4
BLOCK = 128
KV_WIDTH = N_KV_HEADS * HEAD_DIM
SSM_GROUP = 16
SSM_WIDTH = 512
SSM_GROUPS = 32
SSM_STATE = 64
N_STATES = SSM_GROUPS * SSM_STATE
D_FF = 2816
FF_BLK = D_FF // N_DEV
IN_WIDTH = 4096
IN_BLK = IN_WIDTH // N_DEV
NORM_EPS = 1e-6
NEG_INF = -1e30
SCAN_COLS = 256
N_SCAN_BLK = N_STATES // SCAN_COLS
SUBLANES = 8
LANES = 128
MXU_WIDTH = 256
FF_BWD_COLS = MXU_WIDTH

ADAM_LR = 0.001
ADAM_B1 = 0.9
ADAM_B2 = 0.999
ADAM_EPS = 1e-08
ADAM_WD = 0.01
ADAM_STEP = 10

VMEM_BIG = 56 * 1024 * 1024


def _cp(sem=None, vmem=None):
    kw = {}
    if sem is not None:
        kw["dimension_semantics"] = sem
    if vmem is not None:
        kw["vmem_limit_bytes"] = vmem
    return pltpu.CompilerParams(**kw)


def _pcall(body, **kw):
    return pl.pallas_call(body, **kw)


def _dot(a, b):
    return jnp.dot(a, b, preferred_element_type=F32)


def _dot_nt(a, b):
    return lax.dot_general(a, b, (((1,), (1,)), ((), ())), preferred_element_type=F32)


def _dot_tn(a, b):
    return lax.dot_general(a, b, (((0,), (0,)), ((), ())), preferred_element_type=F32)


def _sigmoid(x):
    return 0.5 + 0.5 * jnp.tanh(0.5 * x)


def _row_tile(rows, cap):
    best = None
    for t in range(16, min(rows, cap) + 1, 16):
        if rows % t == 0:
            best = t
    assert best is not None, rows
    return best


def _my_place():
    return lax.axis_index("x"), lax.axis_index("y"), lax.axis_index("c")


def all_gather_list(shards, after, name):
    n = len(shards)

    def body(*refs):
        ins, outs = refs[:n], refs[n + 1:2 * n + 1]
        send_sems, recv_sems, local_sems = refs[2 * n + 1:]
        x, y, c = _my_place()
        me, sibling = (x, y, c), (x, y, 1 - c)
        chips = [(1 - x, y), (x, 1 - y), (1 - x, 1 - y)]

        def blk(a, px, py, pc):
            return outs[a].at[4 * px + 2 * py + pc]

        def copy(a, k, block, to, src=None):
            return pltpu.make_async_remote_copy(
                src_ref=blk(a, *block) if src is None else src, dst_ref=blk(a, *block),
                send_sem=send_sems.at[a * 7 + k], recv_sem=recv_sems.at[a * 7 + k],
                device_id=to, device_id_type=MESH)

        mine = [pltpu.make_async_copy(ins[a], blk(a, *me), local_sems.at[a]) for a in range(n)]
        for cp in mine:
            cp.start()
        first = []
        for a in range(n):
            first.append(copy(a, 0, me, sibling, src=ins[a]))
            first += [copy(a, 1 + j, me, (*chip, c), src=ins[a]) for j, chip in enumerate(chips)]
        for cp in first:
            cp.start()
        passed = []
        for j, chip in enumerate(chips):
            for a in range(n):
                copy(a, 1 + j, (*chip, c), me).wait_recv()
                cp = copy(a, 4 + j, (*chip, c), sibling)
                cp.start()
                passed.append(cp)
        for a in range(n):
            copy(a, 0, sibling, me).wait_recv()
            for j, chip in enumerate(chips):
                copy(a, 4 + j, (*chip, 1 - c), me).wait_recv()
        for cp in first + passed:
            cp.wait_send()
        for cp in mine:
            cp.wait()

    any_spec = pl.BlockSpec(memory_space=pl.ANY)
    return _pcall(
        body, name=name,
        out_shape=[jax.ShapeDtypeStruct((N_DEV,) + s.shape, s.dtype) for s in shards],
        in_specs=[any_spec] * (n + 1), out_specs=[any_spec] * n,
        scratch_shapes=[pltpu.SemaphoreType.DMA((7 * n,)), pltpu.SemaphoreType.DMA((7 * n,)),
                        pltpu.SemaphoreType.DMA((n,))],
    )(*shards, after)


HBM_SPEC = pl.BlockSpec(memory_space=pltpu.HBM)
SEM_SPEC = pl.BlockSpec(memory_space=pltpu.SEMAPHORE)
N_PEERS = N_DEV - 1


def _related(k):
    x, y, c = _my_place()
    px = 1 - x if k & 4 else x
    py = 1 - y if k & 2 else y
    pc = 1 - c if k & 1 else c
    return (px, py, pc), 4 * px + 2 * py + pc


def _exchange_copies(srcs, lands, send_sems, recv_sems, gather):
    x, y, c = _my_place()
    me = 4 * x + 2 * y + c
    copies = []
    for a, (src, land) in enumerate(zip(srcs, lands)):
        for k in range(1, N_DEV):
            peer, d = _related(k)
            copies.append(pltpu.make_async_remote_copy(
                src_ref=src if gather else src.at[d], dst_ref=land.at[me] if gather else land.at[k],
                send_sem=send_sems.at[a * N_PEERS + k - 1], recv_sem=recv_sems.at[a * N_PEERS + k - 1],
                device_id=peer, device_id_type=MESH))
    return copies


def exchange_start(srcs, after, gather, name):
    n = len(srcs)
    land_shapes = [((N_DEV,) + s.shape) if gather else s.shape for s in srcs]

    def body(*refs):
        send_sems, recv_sems = refs[2 * n + 1], refs[2 * n + 2]
        for cp in _exchange_copies(refs[:n], refs[n:2 * n], send_sems, recv_sems, gather):
            cp.start()
        token = refs[-1]
        token[...] = jnp.zeros_like(token)

    sems = pltpu.SemaphoreType.DMA((n * N_PEERS,))
    lands = [pltpu.with_memory_space_constraint(lax.empty(shape, s.dtype), pltpu.HBM) for shape, s in zip(land_shapes, srcs)]
    out = _pcall(
        body, name=name,
        out_shape=(sems, sems, *[pltpu.HBM(s.shape, s.dtype) for s in srcs],
                   *[pltpu.HBM(shape, s.dtype) for shape, s in zip(land_shapes, srcs)],
                   jax.ShapeDtypeStruct((SUBLANES, LANES), F32)),
        in_specs=[HBM_SPEC] * (2 * n) + [pl.BlockSpec(memory_space=pl.ANY)],
        out_specs=(SEM_SPEC, SEM_SPEC, *[HBM_SPEC] * (2 * n), pl.BlockSpec(memory_space=pltpu.VMEM)),
        input_output_aliases={i: 2 + i for i in range(2 * n)},
        compiler_params=pltpu.CompilerParams(has_side_effects=pltpu.SideEffectType.DATAFLOW_SIDE_EFFECTING),
    )(*[pltpu.with_memory_space_constraint(s, pltpu.HBM) for s in srcs], *lands, after)
    return out[0], out[1], list(out[2:2 + n]), list(out[2 + n:2 + 2 * n]), out[-1]


def exchange_wait(send_sems, recv_sems, srcs, lands, after, gather, name):
    n = len(srcs)

    def body(*refs):
        for cp in _exchange_copies(refs[:n], refs[n:2 * n], refs[2 * n], refs[2 * n + 1], gather):
            cp.wait_send()
            cp.wait_recv()

    out = _pcall(
        body, name=name,
        out_shape=(*[pltpu.HBM(s.shape, s.dtype) for s in srcs], *[pltpu.HBM(z.shape, z.dtype) for z in lands]),
        in_specs=[HBM_SPEC] * (2 * n) + [SEM_SPEC, SEM_SPEC, pl.BlockSpec(memory_space=pl.ANY)],
        out_specs=tuple([HBM_SPEC] * (2 * n)),
        input_output_aliases={i: i for i in range(2 * n)},
        compiler_params=pltpu.CompilerParams(has_side_effects=pltpu.SideEffectType.DATAFLOW_SIDE_EFFECTING),
    )(*srcs, *lands, send_sems, recv_sems, after)
    return list(out[:n]), list(out[n:])


def adamw_exchanged(me, g, land, w, m, v, name):
    rows, cols = w.shape
    tr = _row_tile(rows, 256)

    def body(me_ref, g_ref, land_ref, w_ref, m_ref, v_ref, go_ref, d_ref, mo_ref, vo_ref):
        grad = g_ref[...].astype(F32)
        for k in range(1, N_DEV):
            grad = grad + land_ref[k].astype(F32)
        delta, m_new, v_new = _adam_math(w_ref[...], grad, m_ref[...], v_ref[...])
        go_ref[...] = grad
        d_ref[...] = delta
        mo_ref[...] = m_new
        vo_ref[...] = v_new

    tile = pl.BlockSpec((tr, cols), lambda r, ix: (r, 0))
    out = jax.ShapeDtypeStruct((rows, cols), F32)
    return _pcall(
        body, name=name, out_shape=[out] * 4,
        grid_spec=pltpu.PrefetchScalarGridSpec(
            num_scalar_prefetch=1, grid=(rows // tr,),
            in_specs=[pl.BlockSpec((None, tr, cols), lambda r, ix: (ix[0], r, 0)),
                      pl.BlockSpec((N_DEV, tr, cols), lambda r, ix: (0, r, 0)), tile, tile, tile],
            out_specs=[tile] * 4),
        compiler_params=_cp(("arbitrary",)),
    )(me, g, land, w, m, v)


def _adam_math(w, g, m, v):
    m = ADAM_B1 * m + (1.0 - ADAM_B1) * g
    v = ADAM_B2 * v + (1.0 - ADAM_B2) * (g * g)
    m_hat = m / (1.0 - ADAM_B1 ** ADAM_STEP)
    v_hat = v / (1.0 - ADAM_B2 ** ADAM_STEP)
    delta = -ADAM_LR * (m_hat / (jnp.sqrt(v_hat) + ADAM_EPS) + ADAM_WD * w)
    return delta, m, v


def adamw_small(parts, w, m, v, name):
    _, rows, cols = parts.shape

    def body(p_ref, w_ref, m_ref, v_ref, go_ref, d_ref, mo_ref, vo_ref):
        grad = p_ref[0]
        for k in range(1, N_DEV):
            grad = grad + p_ref[k]
        delta, m_new, v_new = _adam_math(w_ref[...], grad, m_ref[...], v_ref[...])
        go_ref[...] = grad
        d_ref[...] = delta
        mo_ref[...] = m_new
        vo_ref[...] = v_new

    out = jax.ShapeDtypeStruct((rows, cols), F32)
    return _pcall(body, name=name, out_shape=[out] * 4, compiler_params=_cp(vmem=VMEM_BIG))(parts, w, m, v)


def adamw_plain(g, w, m, v, name):
    def body(g_ref, w_ref, m_ref, v_ref, d_ref, mo_ref, vo_ref):
        delta, m_new, v_new = _adam_math(w_ref[...], g_ref[...], m_ref[...], v_ref[...])
        d_ref[...] = delta
        mo_ref[...] = m_new
        vo_ref[...] = v_new

    out = jax.ShapeDtypeStruct(w.shape, F32)
    return _pcall(body, name=name, out_shape=[out] * 3)(g, w, m, v)


def _rms_fwd(x, g):
    r = lax.rsqrt(jnp.mean(x * x, axis=-1, keepdims=True) + NORM_EPS)
    return x * r * g


def _rms_bwd(x, g, dy):
    r = lax.rsqrt(jnp.mean(x * x, axis=-1, keepdims=True) + NORM_EPS)
    xh = x * r
    t = dy * g
    dx = r * (t - xh * jnp.mean(t * xh, axis=-1, keepdims=True))
    return dx, jnp.sum(dy * xh, axis=0, keepdims=True)


def _accumulate(ref, val, first):
    @pl.when(first)
    def _():
        ref[...] = val

    @pl.when(jnp.logical_not(first))
    def _():
        ref[...] += val


def _col_chunks(width):
    return [(c0, min(MXU_WIDTH, width - c0)) for c0 in range(0, width, MXU_WIDTH)]


ANY_SPEC = pl.BlockSpec(memory_space=pl.ANY)


def ffn_forward(h, norm, w1, w3, w2, after, tm, name, meta=None):
    if meta is None:
        t_rows = h.shape[0]
        h_spec = pl.BlockSpec((tm, D_MODEL), lambda i: (i, 0))
    else:
        tiles = (h.shape[1] + N_META) // tm
        t_rows = h.shape[0] * tiles * tm
        h_spec = pl.BlockSpec((None, tm, D_MODEL), lambda i: (i // tiles, i % tiles, 0))

    def body(h_ref, g_ref, w1_ref, w3_ref, w2_ref, _, *rest):
        if meta is None:
            out_ref, hn_ref, a_ref, b_ref, hid_ref = rest
            h_in = h_ref[...]
        else:
            meta_ref, out_ref, hn_ref, a_ref, b_ref, h0_ref, hid_ref = rest
            h_in = h_ref[...]
            with_meta = jnp.concatenate([h_in[:tm - N_META], meta_ref[...]], axis=0)
            h_in = jnp.where(pl.program_id(0) % tiles == tiles - 1, with_meta, h_in)
            h0_ref[...] = h_in
        hn = _rms_fwd(h_in, g_ref[...]).astype(BF16)
        hn_ref[...] = hn
        for c0, cw in _col_chunks(D_FF):
            a = _dot_nt(hn, w1_ref[c0:c0 + cw, :])
            b = _dot_nt(hn, w3_ref[c0:c0 + cw, :])
            a_ref[:, c0:c0 + cw] = a.astype(BF16)
            b_ref[:, c0:c0 + cw] = b.astype(BF16)
            hid_ref[:, c0:c0 + cw] = (a * _sigmoid(a) * b).astype(BF16)
        out_ref[...] = h_in + 0.5 * _dot(hid_ref[...], w2_ref[...])

    row = pl.BlockSpec((tm, D_MODEL), lambda i: (i, 0))
    hid_blk = pl.BlockSpec((tm, D_FF), lambda i: (i, 0))
    weight = _resident((D_FF, D_MODEL))
    wide = jax.ShapeDtypeStruct((t_rows, D_MODEL), F32)
    extra_in = [] if meta is None else [meta]
    return _pcall(
        body, name=name, grid=(t_rows // tm,),
        in_specs=[h_spec, pl.BlockSpec((1, D_MODEL), lambda i: (0, 0)), weight, weight, weight, ANY_SPEC]
        + [pl.BlockSpec((N_META, D_MODEL), lambda i: (0, 0))] * len(extra_in),
        out_specs=[row, row, hid_blk, hid_blk] + [row] * len(extra_in),
        out_shape=[wide, jax.ShapeDtypeStruct((t_rows, D_MODEL), BF16),
                   jax.ShapeDtypeStruct((t_rows, D_FF), BF16), jax.ShapeDtypeStruct((t_rows, D_FF), BF16)]
        + [wide] * len(extra_in),
        scratch_shapes=[pltpu.VMEM((tm, D_FF), BF16)],
        compiler_params=_cp(("arbitrary",), VMEM_BIG),
    )(h, norm, w1, w3, w2, after, *extra_in)


def _resident(shape):
    return pl.BlockSpec(shape, lambda *_: (0,) * len(shape), pipeline_mode=pl.Buffered(1))


def ffn_backward_hidden(dh, a, b, w2, after, tm, name):
    t_rows = dh.shape[0]

    def body(dh_ref, a_ref, b_ref, w2_ref, _, da_ref, db_ref, dhb_ref):
        dhb = (0.5 * dh_ref[...]).astype(BF16)
        dhb_ref[...] = dhb
        for c0, cw in _col_chunks(D_FF):
            dhid = _dot_nt(dhb, w2_ref[c0:c0 + cw, :])
            av = a_ref[:, c0:c0 + cw].astype(F32)
            bv = b_ref[:, c0:c0 + cw].astype(F32)
            s = _sigmoid(av)
            da_ref[:, c0:c0 + cw] = (dhid * bv * (s * (1.0 + av * (1.0 - s)))).astype(BF16)
            db_ref[:, c0:c0 + cw] = (dhid * (av * s)).astype(BF16)

    hid = pl.BlockSpec((tm, D_FF), lambda i: (i, 0))
    row = pl.BlockSpec((tm, D_MODEL), lambda i: (i, 0))
    return _pcall(
        body, name=name, grid=(t_rows // tm,),
        in_specs=[row, hid, hid, _resident((D_FF, D_MODEL)), ANY_SPEC],
        out_specs=[hid, hid, row],
        out_shape=[jax.ShapeDtypeStruct((t_rows, D_FF), BF16), jax.ShapeDtypeStruct((t_rows, D_FF), BF16),
                   jax.ShapeDtypeStruct((t_rows, D_MODEL), BF16)],
        compiler_params=_cp(("arbitrary",), VMEM_BIG),
    )(dh, a, b, w2, after)


def ffn_backward_input(dh, h, norm, da, db, w1, w3, after, tm, name, examples=None):
    t_rows = h.shape[0]

    def body(dh_ref, h_ref, g_ref, da_ref, db_ref, w1_ref, w3_ref, _, dhin_ref, *rest):
        dg_ref = rest[-1]
        dhn = _dot(da_ref[...], w1_ref[...]) + _dot(db_ref[...], w3_ref[...])
        dx, dg = _rms_bwd(h_ref[...], g_ref[...], dhn)
        dhin = dh_ref[...] + dx
        dhin_ref[...] = dhin
        _accumulate(dg_ref, dg, pl.program_id(0) == 0)
        if examples is not None:
            @pl.when(pl.program_id(0) % tiles == tiles - 1)
            def _():
                rest[0][...] = dhin[tm - N_META:, :]

    row = pl.BlockSpec((tm, D_MODEL), lambda i: (i, 0))
    vec = pl.BlockSpec((1, D_MODEL), lambda i: (0, 0))
    hid = pl.BlockSpec((tm, D_FF), lambda i: (i, 0))
    if examples is None:
        out_specs = [row, vec]
        out_shape = [jax.ShapeDtypeStruct((t_rows, D_MODEL), F32), jax.ShapeDtypeStruct((1, D_MODEL), F32)]
    else:
        n_b, seq = examples
        tiles = (seq + N_META) // tm
        out_specs = [pl.BlockSpec((None, tm, D_MODEL), lambda i: (i // tiles, i % tiles, 0)),
                     pl.BlockSpec((None, N_META, D_MODEL), lambda i: (i // tiles, 0, 0)), vec]
        out_shape = [jax.ShapeDtypeStruct((n_b, seq, D_MODEL), F32), jax.ShapeDtypeStruct((n_b, N_META, D_MODEL), F32),
                     jax.ShapeDtypeStruct((1, D_MODEL), F32)]
    return _pcall(
        body, name=name, grid=(t_rows // tm,),
        in_specs=[row, row, vec, hid, hid, _resident((D_FF, D_MODEL)), _resident((D_FF, D_MODEL)), ANY_SPEC],
        out_specs=out_specs, out_shape=out_shape,
        compiler_params=_cp(("arbitrary",), VMEM_BIG),
    )(dh, h, norm, da, db, w1, w3, after)


def ffn_backward_weights(hn, dh, a, b, da, db, tm, tn, name):
    t_rows = hn.shape[0]
    ni = t_rows // tm
    kc = _row_tile(tm, 688)

    def body(hn_ref, dh_ref, a_ref, b_ref, da_ref, db_ref, dw1_ref, dw3_ref, dw2_ref, acc1, acc3, acc2):
        i = pl.program_id(1)
        parts = None
        for r0 in range(0, tm, kc):
            rows = slice(r0, r0 + kc)
            hn_v = hn_ref[rows, :]
            av = a_ref[rows, :].astype(F32)
            hid = (av * _sigmoid(av) * b_ref[rows, :].astype(F32)).astype(BF16)
            new = (_dot_tn(hn_v, da_ref[rows, :]), _dot_tn(hn_v, db_ref[rows, :]), _dot_tn(dh_ref[rows, :], hid))
            parts = new if parts is None else tuple(p + q for p, q in zip(parts, new))
        _accumulate(acc1, parts[0], i == 0)
        _accumulate(acc3, parts[1], i == 0)
        _accumulate(acc2, parts[2], i == 0)

        @pl.when(i == ni - 1)
        def _():
            dw1_ref[...] = acc1[...].T.astype(BF16)
            dw3_ref[...] = acc3[...].T.astype(BF16)
            dw2_ref[...] = acc2[...].T.astype(BF16)

    row = pl.BlockSpec((tm, D_MODEL), lambda j, i: (i, 0))
    hid_blk = pl.BlockSpec((tm, tn), lambda j, i: (i, j))
    w_row = pl.BlockSpec((tn, D_MODEL), lambda j, i: (j, 0))
    out = jax.ShapeDtypeStruct((D_FF, D_MODEL), BF16)
    return _pcall(
        body, name=name, grid=(D_FF // tn, ni),
        in_specs=[row, row, hid_blk, hid_blk, hid_blk, hid_blk],
        out_specs=[w_row, w_row, w_row], out_shape=[out, out, out],
        scratch_shapes=[pltpu.VMEM((D_MODEL, tn), F32)] * 3,
        compiler_params=_cp(("arbitrary", "arbitrary"), VMEM_BIG),
    )(hn, dh, a, b, da, db)


def mix_forward(h, norm, wing, tm, name):
    t_rows = h.shape[0]

    def body(h_ref, g_ref, w_ref, hn_ref, p_ref):
        hn = _rms_fwd(h_ref[...], g_ref[...]).astype(BF16)
        hn_ref[...] = hn
        for j in range(N_DEV):
            p_ref[:, j * IN_BLK:(j + 1) * IN_BLK] = _dot(hn, w_ref[j]).astype(BF16)

    row = pl.BlockSpec((tm, D_MODEL), lambda i: (i, 0))
    return _pcall(
        body, name=name, grid=(t_rows // tm,),
        in_specs=[row, pl.BlockSpec((1, D_MODEL), lambda i: (0, 0)),
                  pl.BlockSpec((N_DEV, D_MODEL, IN_BLK), lambda i: (0, 0, 0))],
        out_specs=[row, pl.BlockSpec((tm, IN_WIDTH), lambda i: (i, 0))],
        out_shape=[jax.ShapeDtypeStruct((t_rows, D_MODEL), BF16), jax.ShapeDtypeStruct((t_rows, IN_WIDTH), BF16)],
        compiler_params=_cp(("arbitrary",), VMEM_BIG),
    )(h, norm, wing)


def mix_backward_act(dh, h, norm, dproj, w_in_full, tm, name):
    t_rows = h.shape[0]

    def body(dh_ref, h_ref, g_ref, dp_ref, w_ref, dhin_ref, dg_ref):
        dx, dg = _rms_bwd(h_ref[...], g_ref[...], _dot_nt(dp_ref[...], w_ref[...]))
        dhin_ref[...] = dh_ref[...] + dx
        _accumulate(dg_ref, dg, pl.program_id(0) == 0)

    row = pl.BlockSpec((tm, D_MODEL), lambda i: (i, 0))
    vec = pl.BlockSpec((1, D_MODEL), lambda i: (0, 0))
    return _pcall(
        body, name=name, grid=(t_rows // tm,),
        in_specs=[row, row, vec, pl.BlockSpec((tm, IN_WIDTH), lambda i: (i, 0)), _resident((D_MODEL, IN_WIDTH))],
        out_specs=[row, vec],
        out_shape=[jax.ShapeDtypeStruct((t_rows, D_MODEL), F32), jax.ShapeDtypeStruct((1, D_MODEL), F32)],
        compiler_params=_cp(("arbitrary",), VMEM_BIG),
    )(dh, h, norm, dproj, w_in_full)


def mix_backward_weights(hn, dproj, tm, name):
    t_rows = hn.shape[0]
    ni = t_rows // tm
    per_step = 2

    kc = _row_tile(tm, 688)

    def body(hn_ref, dp_ref, dw_ref, acc):
        i = pl.program_id(1)
        part = functools.reduce(lambda u, w: u + w, [_dot_tn(hn_ref[r0:r0 + kc, :], dp_ref[r0:r0 + kc, :])
                                                    for r0 in range(0, tm, kc)])
        _accumulate(acc, part, i == 0)

        @pl.when(i == ni - 1)
        def _():
            for k in range(per_step):
                dw_ref[k] = acc[:, k * IN_BLK:(k + 1) * IN_BLK].astype(BF16)

    return _pcall(
        body, name=name, grid=(N_DEV // per_step, ni),
        in_specs=[pl.BlockSpec((tm, D_MODEL), lambda j, i: (i, 0)),
                  pl.BlockSpec((tm, per_step * IN_BLK), lambda j, i: (i, j))],
        out_specs=pl.BlockSpec((per_step, D_MODEL, IN_BLK), lambda j, i: (j, 0, 0)),
        out_shape=jax.ShapeDtypeStruct((N_DEV, D_MODEL, IN_BLK), BF16),
        scratch_shapes=[pltpu.VMEM((D_MODEL, per_step * IN_BLK), F32)],
        compiler_params=_cp(("arbitrary", "arbitrary"), VMEM_BIG),
    )(hn, dproj)


GELU_C = 0.7978845608028654
GELU_K = 0.044715


def _gelu(x):
    return 0.5 * x * (1.0 + jnp.tanh(GELU_C * (x + GELU_K * (x * x * x))))


def _gelu_and_grad(x):
    th = jnp.tanh(GELU_C * (x + GELU_K * (x * x * x)))
    val = 0.5 * x * (1.0 + th)
    grad = 0.5 * (1.0 + th) + 0.5 * x * (1.0 - th * th) * (GELU_C * (1.0 + 3.0 * GELU_K * (x * x)))
    return val, grad


def merge_forward(h, yraw, attn, proj, glu_a, glu_b, w_out, tm, name):
    t_rows = h.shape[0]

    def body(h_ref, y_ref, at_ref, gate_ref, a_ref, b_ref, wo_ref, out_ref):
        y = _gelu(y_ref[...]).astype(BF16)
        ssm = _dot(y, a_ref[...]) * _sigmoid(_dot(y, b_ref[...]))
        ga = gate_ref[:, :D_MODEL].astype(F32)
        gs = gate_ref[:, D_MODEL:].astype(F32)
        merged = _sigmoid(ga) * at_ref[...].astype(F32) + _sigmoid(gs) * ssm
        out_ref[...] = h_ref[...] + _dot(merged.astype(BF16), wo_ref[...])

    row = pl.BlockSpec((tm, D_MODEL), lambda i: (i, 0))
    glu = pl.BlockSpec((SSM_WIDTH, D_MODEL), lambda i: (0, 0))
    return _pcall(
        body, name=name, grid=(t_rows // tm,),
        in_specs=[row, pl.BlockSpec((tm, SSM_WIDTH), lambda i: (i, 0)), row,
                  pl.BlockSpec((tm, 2 * D_MODEL), lambda i: (i, 1)), glu, glu,
                  pl.BlockSpec((D_MODEL, D_MODEL), lambda i: (0, 0))],
        out_specs=row, out_shape=jax.ShapeDtypeStruct((t_rows, D_MODEL), F32),
        compiler_params=_cp(("arbitrary",), VMEM_BIG),
    )(h, yraw, attn, proj, glu_a, glu_b, w_out)


def merge_backward(dh, yraw, attn, proj, glu_a, glu_b, w_out, after, tm, name):
    t_rows = dh.shape[0]

    def body(dh_ref, y_ref, at_ref, gate_ref, a_ref, b_ref, wo_ref, _,
             dat_ref, dy_ref, dgate_ref, d16_ref, mg_ref, y16_ref, dya_ref, dyb_ref):
        d16 = dh_ref[...].astype(BF16)
        d16_ref[...] = d16
        gel, dgel = _gelu_and_grad(y_ref[...].astype(F32))
        y16 = gel.astype(BF16)
        y16_ref[...] = y16
        dy = None
        for c0, cw in _col_chunks(D_MODEL):
            cols = slice(c0, c0 + cw)
            gcols = slice(D_MODEL + c0, D_MODEL + c0 + cw)
            dmerged = _dot_nt(d16, wo_ref[cols, :])
            ya = _dot(y16, a_ref[:, cols])
            sb = _sigmoid(_dot(y16, b_ref[:, cols]))
            ssm = ya * sb
            sa = _sigmoid(gate_ref[:, cols].astype(F32))
            ss = _sigmoid(gate_ref[:, gcols].astype(F32))
            attn_v = at_ref[:, cols].astype(F32)
            mg_ref[:, cols] = (sa * attn_v + ss * ssm).astype(BF16)
            dat_ref[:, cols] = (dmerged * sa).astype(BF16)
            dgate_ref[:, cols] = (dmerged * attn_v * sa * (1.0 - sa)).astype(BF16)
            dgate_ref[:, gcols] = (dmerged * ssm * ss * (1.0 - ss)).astype(BF16)
            dssm = dmerged * ss
            dya = (dssm * sb).astype(BF16)
            dyb = (dssm * ya * sb * (1.0 - sb)).astype(BF16)
            dya_ref[:, cols] = dya
            dyb_ref[:, cols] = dyb
            part = _dot_nt(dya, a_ref[:, cols]) + _dot_nt(dyb, b_ref[:, cols])
            dy = part if dy is None else dy + part
        dy_ref[...] = (dy * dgel).astype(BF16)

    row = pl.BlockSpec((tm, D_MODEL), lambda i: (i, 0))
    ssm_row = pl.BlockSpec((tm, SSM_WIDTH), lambda i: (i, 0))
    gates = pl.BlockSpec((tm, 2 * D_MODEL), lambda i: (i, 1))
    wide = jax.ShapeDtypeStruct((t_rows, D_MODEL), BF16)
    narrow = jax.ShapeDtypeStruct((t_rows, SSM_WIDTH), BF16)
    return _pcall(
        body, name=name, grid=(t_rows // tm,),
        in_specs=[row, ssm_row, row, gates, _resident((SSM_WIDTH, D_MODEL)), _resident((SSM_WIDTH, D_MODEL)),
                  _resident((D_MODEL, D_MODEL)), ANY_SPEC],
        out_specs=[row, ssm_row, gates, row, row, ssm_row, row, row],
        out_shape=[wide, narrow, jax.ShapeDtypeStruct((t_rows, IN_WIDTH), BF16), wide, wide, narrow, wide, wide],
        compiler_params=_cp(("arbitrary",), VMEM_BIG),
    )(dh, yraw, attn, proj, glu_a, glu_b, w_out, after)


def merge_backward_weights(d16, merged, y16, dya, dyb, tm, name):
    t_rows = d16.shape[0]

    def body(d_ref, mg_ref, y_ref, dya_ref, dyb_ref, dwo_ref, da_ref, db_ref):
        first = pl.program_id(0) == 0
        y16 = y_ref[...]
        _accumulate(dwo_ref, _dot_tn(mg_ref[...], d_ref[...]), first)
        _accumulate(da_ref, _dot_tn(y16, dya_ref[...]), first)
        _accumulate(db_ref, _dot_tn(y16, dyb_ref[...]), first)

    row = pl.BlockSpec((tm, D_MODEL), lambda i: (i, 0))
    ssm_row = pl.BlockSpec((tm, SSM_WIDTH), lambda i: (i, 0))
    glu = pl.BlockSpec((SSM_WIDTH, D_MODEL), lambda i: (0, 0))
    wo = pl.BlockSpec((D_MODEL, D_MODEL), lambda i: (0, 0))
    return _pcall(
        body, name=name, grid=(t_rows // tm,),
        in_specs=[row, row, ssm_row, row, row], out_specs=[wo, glu, glu],
        out_shape=[jax.ShapeDtypeStruct((D_MODEL, D_MODEL), F32), jax.ShapeDtypeStruct((SSM_WIDTH, D_MODEL), F32),
                   jax.ShapeDtypeStruct((SSM_WIDTH, D_MODEL), F32)],
        compiler_params=_cp(("arbitrary",), VMEM_BIG),
    )(d16, merged, y16, dya, dyb)


def final_loss_backward(h, target, norm, seq, tm, name):
    t_rows = h.shape[0]
    tiles_per_example = (seq + N_META) // tm

    def body(h_ref, t_ref, g_ref, dh_ref, loss_ref, dg_ref):
        i = pl.program_id(0)
        x = h_ref[...]
        g = g_ref[...]
        r = lax.rsqrt(jnp.mean(x * x, axis=-1, keepdims=True) + NORM_EPS)
        xh = x * r
        pos = lax.broadcasted_iota(jnp.int32, (tm, 1), 0) + (i % tiles_per_example) * tm
        diff = jnp.where(pos < seq, xh * g - t_ref[...], 0.0)
        part = 0.5 * jnp.sum(jnp.sum(diff * diff, axis=-1, keepdims=True), axis=0, keepdims=True) / D_MODEL
        dy = diff / D_MODEL
        t = dy * g
        dh_ref[...] = r * (t - xh * jnp.mean(t * xh, axis=-1, keepdims=True))
        _accumulate(loss_ref, jnp.broadcast_to(part, (1, LANES)), i == 0)
        _accumulate(dg_ref, jnp.sum(dy * xh, axis=0, keepdims=True), i == 0)

    row = pl.BlockSpec((tm, D_MODEL), lambda i: (i, 0))
    vec = pl.BlockSpec((1, D_MODEL), lambda i: (0, 0))
    per_example = pl.BlockSpec((None, tm, D_MODEL), lambda i: (i // tiles_per_example, i % tiles_per_example, 0))
    return _pcall(
        body, name=name, grid=(t_rows // tm,),
        in_specs=[row, per_example, vec],
        out_specs=[row, pl.BlockSpec((1, LANES), lambda i: (0, 0)), vec],
        out_shape=[jax.ShapeDtypeStruct((t_rows, D_MODEL), F32), jax.ShapeDtypeStruct((1, LANES), F32),
                   jax.ShapeDtypeStruct((1, D_MODEL), F32)],
        compiler_params=_cp(("arbitrary",), VMEM_BIG),
    )(h, target, norm)


ATTN_SCALE = HEAD_DIM ** -0.5
STACK_HEADS = (0, 2, 1, 3)
META_PAD = LANES - N_META


def _lane_half(shape, hf):
    lane = lax.broadcasted_iota(jnp.int32, shape, 1)
    return (lane < HEAD_DIM) if hf == 0 else (lane >= HEAD_DIM)


def _kv_variants(ref, rows, kh, pad_rows=0):
    tile = kh // 2
    t = ref[rows, tile * LANES:(tile + 1) * LANES].astype(F32)
    swapped = pltpu.roll(t, HEAD_DIM, axis=1)
    at_low, at_high = (t, swapped) if kh % 2 == 0 else (swapped, t)
    lo = jnp.where(_lane_half(t.shape, 0), at_low, 0.0).astype(BF16)
    hi = jnp.where(_lane_half(t.shape, 1), at_high, 0.0).astype(BF16)
    if pad_rows:
        zeros = jnp.zeros((pad_rows, LANES), BF16)
        lo, hi = jnp.concatenate([lo, zeros], axis=0), jnp.concatenate([hi, zeros], axis=0)
    return lo, hi


def _key_tiles(ref, key_rows, kh):
    return [_kv_variants(ref, r, kh, META_PAD if i == len(key_rows) - 1 else 0) for i, r in enumerate(key_rows)]


def _to_kv_lanes(lo, hi, kh):
    lo = jnp.where(_lane_half(lo.shape, 0), lo, 0.0)
    hi = jnp.where(_lane_half(hi.shape, 1), hi, 0.0)
    if kh % 2 == 0:
        return lo + pltpu.roll(hi, HEAD_DIM, axis=1)
    return pltpu.roll(lo, HEAD_DIM, axis=1) + hi


def _stacked(ref, rows, kh):
    col = kh * 2 * LANES
    return jnp.concatenate([ref[rows, col:col + LANES], ref[rows, col + LANES:col + 2 * LANES]], axis=0)


def _sink_column(sink_ref, kh, nq):
    row = lax.broadcasted_iota(jnp.int32, (4 * nq, 1), 0)
    col = jnp.zeros((4 * nq, 1), F32)
    for quarter, g in enumerate(STACK_HEADS):
        col = jnp.where(row // nq == quarter, sink_ref[0, kh * Q_PER_KV + g], col)
    return col


def _softmax_parts(qs, key_tiles, masks, sink):
    scores = []
    for (k_lo, k_hi), mask in zip(key_tiles, masks):
        s = jnp.concatenate([_dot_nt(qs, k_lo), _dot_nt(qs, k_hi)], axis=0) * ATTN_SCALE
        scores.append(s if mask is None else jnp.where(mask, s, NEG_INF))
    m = jnp.maximum(_row_reduce(scores, jnp.maximum, jnp.max), sink)
    probs = [jnp.exp(s - m) for s in scores]
    e_sink = jnp.exp(sink - m)
    den = _row_sums(probs) + e_sink
    return probs, 1.0 / den, e_sink


def _row_reduce(tiles, combine, reduce):
    chunks = [t[:, c:c + LANES] for t in tiles for c in range(0, t.shape[-1], LANES)]
    return reduce(functools.reduce(combine, chunks), axis=-1, keepdims=True)


def _row_sums(tiles):
    return _row_reduce(tiles, lambda u, w: u + w, jnp.sum)


def _band_mask(nq, first):
    keys = BLOCK if first else 2 * BLOCK
    qi = lax.broadcasted_iota(jnp.int32, (4 * nq, keys), 0) % nq
    kj = lax.broadcasted_iota(jnp.int32, (4 * nq, keys), 1)
    if first:
        return kj <= qi
    return jnp.logical_and(kj > qi, kj <= qi + BLOCK)


def _meta_mask(nq, causal):
    qi = lax.broadcasted_iota(jnp.int32, (4 * nq, LANES), 0) % nq
    kj = lax.broadcasted_iota(jnp.int32, (4 * nq, LANES), 1)
    return jnp.logical_and(kj < N_META, kj <= qi) if causal else kj < N_META


def _attention_schedule(seq, queries, carry):
    meta_rows = pl.ds(seq, N_META)
    meta_ok = _meta_mask(BLOCK, False)
    carry = queries(pl.ds(0, BLOCK), BLOCK, [pl.ds(0, BLOCK), meta_rows], [_band_mask(BLOCK, True), meta_ok], carry)

    def block(n, c):
        r0 = pl.multiple_of(n * BLOCK, BLOCK)
        p0 = pl.multiple_of((n - 1) * BLOCK, BLOCK)
        return queries(pl.ds(r0, BLOCK), BLOCK, [pl.ds(p0, 2 * BLOCK), meta_rows], [_band_mask(BLOCK, False), meta_ok], c)

    carry = lax.fori_loop(1, seq // BLOCK, block, carry)
    return queries(meta_rows, N_META, [meta_rows], [_meta_mask(N_META, True)], carry)


def attention_forward(proj3, sinks, seq, name):
    n_b, n_l, _ = proj3.shape

    def body(sink_ref, q_ref, k_ref, v_ref, o_ref):
        def queries(q_rows, nq, key_rows, masks, carry):
            for kh in range(N_KV_HEADS):
                ks = _key_tiles(k_ref, key_rows, kh)
                vs = _key_tiles(v_ref, key_rows, kh)
                qs = _stacked(q_ref, q_rows, kh)
                probs, inv, _ = _softmax_parts(qs, ks, masks, _sink_column(sink_ref, kh, nq))
                probs = [p.astype(BF16) for p in probs]
                o_lo = functools.reduce(lambda u, w: u + w, [_dot(p[:2 * nq], v_lo) for p, (v_lo, _) in zip(probs, vs)])
                o_hi = functools.reduce(lambda u, w: u + w, [_dot(p[2 * nq:], v_hi) for p, (_, v_hi) in zip(probs, vs)])
                out = (o_lo * inv[:2 * nq] + o_hi * inv[2 * nq:]).astype(BF16)
                col = kh * 2 * LANES
                o_ref[q_rows, col:col + LANES] = out[:nq]
                o_ref[q_rows, col + LANES:col + 2 * LANES] = out[nq:]
            return carry

        _attention_schedule(seq, queries, 0)

    return _pcall(
        body, name=name, grid=(n_b,),
        in_specs=[pl.BlockSpec(memory_space=pltpu.SMEM),
                  pl.BlockSpec((None, n_l, D_MODEL), lambda b: (b, 0, 0)),
                  pl.BlockSpec((None, n_l, KV_WIDTH), lambda b: (b, 0, D_MODEL // KV_WIDTH)),
                  pl.BlockSpec((None, n_l, KV_WIDTH), lambda b: (b, 0, D_MODEL // KV_WIDTH + 1))],
        out_specs=pl.BlockSpec((None, n_l, D_MODEL), lambda b: (b, 0, 0)),
        out_shape=jax.ShapeDtypeStruct((n_b, n_l, D_MODEL), BF16),
        compiler_params=_cp(("arbitrary",), VMEM_BIG),
    )(sinks, proj3, proj3, proj3)


def attention_backward(proj3, dattn3, dproj3, sinks, after, seq, name):
    n_b, n_l, _ = proj3.shape
    qkv_width = D_MODEL + 2 * KV_WIDTH

    def body(sink_ref, q_ref, k_ref, v_ref, do_ref, _, __, dqkv_ref, dsink_ref, dk_ref, dv_ref):
        dk_ref[...] = jnp.zeros_like(dk_ref)
        dv_ref[...] = jnp.zeros_like(dv_ref)
        sub = lax.broadcasted_iota(jnp.int32, (SUBLANES, LANES), 0)
        lane = lax.broadcasted_iota(jnp.int32, (SUBLANES, LANES), 1)

        def queries(q_rows, nq, key_rows, masks, dsink):
            for kh in range(N_KV_HEADS):
                ks = _key_tiles(k_ref, key_rows, kh)
                vs = _key_tiles(v_ref, key_rows, kh)
                qs = _stacked(q_ref, q_rows, kh)
                dos = _stacked(do_ref, q_rows, kh)
                probs, inv, e_sink = _softmax_parts(qs, ks, masks, _sink_column(sink_ref, kh, nq))
                probs = [p * inv for p in probs]
                dps = [jnp.concatenate([_dot_nt(dos, v_lo), _dot_nt(dos, v_hi)], axis=0) for v_lo, v_hi in vs]
                delta = _row_sums([p * dp for p, dp in zip(probs, dps)])
                d_sink = -(e_sink * inv) * delta
                for quarter, g in enumerate(STACK_HEADS):
                    d_here = jnp.sum(d_sink[quarter * nq:(quarter + 1) * nq], axis=0, keepdims=True)
                    dsink = dsink + jnp.where(jnp.logical_and(sub == 0, lane == kh * Q_PER_KV + g), d_here, 0.0)
                dq = None
                tile = slice((kh // 2) * LANES, (kh // 2 + 1) * LANES)
                for r, p, dp, (k_lo, k_hi) in zip(key_rows, probs, dps, ks):
                    ds = (p * (dp - delta)).astype(BF16)
                    p16 = p.astype(BF16)
                    dq_x = _dot(ds[:2 * nq], k_lo) + _dot(ds[2 * nq:], k_hi)
                    dq = dq_x if dq is None else dq + dq_x
                    d_k = _to_kv_lanes(_dot_tn(ds[:2 * nq], qs), _dot_tn(ds[2 * nq:], qs), kh) * ATTN_SCALE
                    d_v = _to_kv_lanes(_dot_tn(p16[:2 * nq], dos), _dot_tn(p16[2 * nq:], dos), kh)
                    n_keys = r.size
                    dk_ref[r, tile] += d_k[:n_keys]
                    dv_ref[r, tile] += d_v[:n_keys]
                dq = (dq * ATTN_SCALE).astype(BF16)
                col = kh * 2 * LANES
                dqkv_ref[q_rows, col:col + LANES] = dq[:nq]
                dqkv_ref[q_rows, col + LANES:col + 2 * LANES] = dq[nq:]
            return dsink

        dsink_ref[...] = _attention_schedule(seq, queries, jnp.zeros((SUBLANES, LANES), F32))
        dqkv_ref[:, D_MODEL:D_MODEL + KV_WIDTH] = dk_ref[...].astype(BF16)
        dqkv_ref[:, D_MODEL + KV_WIDTH:] = dv_ref[...].astype(BF16)

    return _pcall(
        body, name=name, grid=(n_b,),
        in_specs=[pl.BlockSpec(memory_space=pltpu.SMEM),
                  pl.BlockSpec((None, n_l, D_MODEL), lambda b: (b, 0, 0)),
                  pl.BlockSpec((None, n_l, KV_WIDTH), lambda b: (b, 0, D_MODEL // KV_WIDTH)),
                  pl.BlockSpec((None, n_l, KV_WIDTH), lambda b: (b, 0, D_MODEL // KV_WIDTH + 1)),
                  pl.BlockSpec((None, n_l, D_MODEL), lambda b: (b, 0, 0)),
                  ANY_SPEC, ANY_SPEC],
        out_specs=[pl.BlockSpec((None, n_l, qkv_width), lambda b: (b, 0, 0)),
                   pl.BlockSpec((None, SUBLANES, LANES), lambda b: (b, 0, 0))],
        out_shape=[jax.ShapeDtypeStruct(dproj3.shape, BF16), jax.ShapeDtypeStruct((n_b, SUBLANES, LANES), F32)],
        scratch_shapes=[pltpu.VMEM((n_l, KV_WIDTH), F32), pltpu.VMEM((n_l, KV_WIDTH), F32)],
        input_output_aliases={5: 0},
        compiler_params=_cp(("arbitrary",), VMEM_BIG),
    )(sinks, proj3, proj3, proj3, dattn3, dproj3, after)


TAB_ROWS = 8
SCAN_UNROLL = 4


def _cmul(ar, ai, br, bi):
    return ar * br - ai * bi, ar * bi + ai * br


def _discretise(ar, ai, ls):
    step = jnp.exp(ls)
    mag = jnp.exp(ar * step)
    ang = ai * step
    cos, sin = jnp.cos(ang), jnp.sin(ang)
    lr, li = mag * cos, mag * sin
    den = ar * ar + ai * ai
    nr, ni = lr - 1.0, li
    cr = (nr * ar + ni * ai) / den
    ci = (ni * ar - nr * ai) / den
    return step, mag, lr, li, den, nr, ni, cr, ci


def _scan_tables(lr, li, reverse):
    n = lr.shape[-1]
    pw = [(lr, li)]
    for _ in range(SUBLANES - 1):
        pw.append(_cmul(pw[-1][0], pw[-1][1], lr, li))
    row = lax.broadcasted_iota(jnp.int32, (SUBLANES, n), 0)
    out = []
    for d in (1, 2, 4):
        ok = (row + d <= SUBLANES - 1) if reverse else (row >= d)
        out += [jnp.where(ok, pw[d - 1][0], 0.0), jnp.where(ok, pw[d - 1][1], 0.0)]
    cr = jnp.zeros((SUBLANES, n), F32)
    ci = jnp.zeros((SUBLANES, n), F32)
    for r in range(SUBLANES):
        e = (SUBLANES - r) if reverse else (r + 1)
        cr = jnp.where(row == r, pw[e - 1][0], cr)
        ci = jnp.where(row == r, pw[e - 1][1], ci)
    return out + [cr, ci]


def ssm_prepare(ar, ai, ls, br_t, bi_t, name):
    def body(ar_ref, ai_ref, ls_ref, br_ref, bi_ref, bbr_ref, bbi_ref, tf_ref, tr_ref):
        _, _, lr, li, _, _, _, cr, ci = _discretise(ar_ref[...], ai_ref[...], ls_ref[...])
        br, bi = br_ref[...], bi_ref[...]
        bbr_ref[...] = cr * br - ci * bi
        bbi_ref[...] = cr * bi + ci * br
        for k, t in enumerate(_scan_tables(lr, li, False)):
            tf_ref[k] = t
        for k, t in enumerate(_scan_tables(lr, -li, True)):
            tr_ref[k] = t

    return _pcall(
        body, name=name,
        out_shape=[jax.ShapeDtypeStruct((SSM_GROUP, N_STATES), F32), jax.ShapeDtypeStruct((SSM_GROUP, N_STATES), F32),
                   jax.ShapeDtypeStruct((TAB_ROWS, SUBLANES, N_STATES), F32),
                   jax.ShapeDtypeStruct((TAB_ROWS, SUBLANES, N_STATES), F32)],
    )(ar, ai, ls, br_t, bi_t)


def ssm_param_backward(ar, ai, ls, br_t, bi_t, dlr_p, dli_p, dbbr, dbbi, group_sum, name):
    def body(ar_ref, ai_ref, ls_ref, br_ref, bi_ref, dlr_ref, dli_ref, dbbr_ref, dbbi_ref, gs_ref,
             dar_ref, dai_ref, dls_ref, dbr_ref, dbi_ref):
        ar, ai = ar_ref[...], ai_ref[...]
        step, mag, lr, li, den, nr, ni, cr, ci = _discretise(ar, ai, ls_ref[...])
        br, bi, dbbr_v, dbbi_v = br_ref[...], bi_ref[...], dbbr_ref[...], dbbi_ref[...]
        dbr_ref[...] = cr * dbbr_v + ci * dbbi_v
        dbi_ref[...] = cr * dbbi_v - ci * dbbr_v
        dcr = jnp.sum(dbbr_v * br + dbbi_v * bi, axis=0, keepdims=True)
        dci = jnp.sum(dbbi_v * br - dbbr_v * bi, axis=0, keepdims=True)
        dnr = (dcr * ar - dci * ai) / den
        dni = (dcr * ai + dci * ar) / den
        dden = -(cr * dcr + ci * dci) / den
        dar = (dcr * nr + dci * ni) / den + dden * 2.0 * ar
        dai = (dcr * ni - dci * nr) / den + dden * 2.0 * ai
        dlr = jnp.sum(dlr_ref[...], axis=0, keepdims=True) + dnr
        dli = jnp.sum(dli_ref[...], axis=0, keepdims=True) + dni
        dmag = (dlr * lr + dli * li) / mag
        dang = dli * lr - dlr * li
        dar_ref[...] = dar + dmag * mag * step
        dai_ref[...] = dai + dang * step
        dstep = dmag * mag * ar + dang * ai
        dls_ref[...] = jnp.dot(dstep * step, gs_ref[...], preferred_element_type=F32, precision=lax.Precision.HIGHEST)

    vec = jax.ShapeDtypeStruct((1, N_STATES), F32)
    mat = jax.ShapeDtypeStruct((SSM_GROUP, N_STATES), F32)
    return _pcall(body, name=name, out_shape=[vec, vec, jax.ShapeDtypeStruct((1, LANES), F32), mat, mat])(
        ar, ai, ls, br_t, bi_t, dlr_p, dli_p, dbbr, dbbi, group_sum)


def _scan_rows(a, b, tabs, carry, reverse):
    for k, d in enumerate((1, 2, 4)):
        shift = SUBLANES - d if reverse else d
        sr, si = pltpu.roll(a, shift, axis=0), pltpu.roll(b, shift, axis=0)
        pr, pi = _cmul(tabs[2 * k], tabs[2 * k + 1], sr, si)
        a, b = a + pr, b + pi
    pr, pi = _cmul(tabs[6], tabs[7], carry[0], carry[1])
    return a + pr, b + pi


def _time_groups(seq, reverse):
    meta = [seq + SUBLANES * g for g in range(N_META // SUBLANES)]
    return meta[::-1] if reverse else meta


def ssm_forward_scan(proj3, b_comb, tabf, c_comb, dvec, seq, name):
    n_b, n_l, _ = proj3.shape
    u_blk = (D_MODEL + 2 * KV_WIDTH) // LANES

    def body(u_ref, b_ref, tab_ref, c_ref, d_ref, x_ref, y_ref, bu, xs):
        j = pl.program_id(1)
        u = u_ref[...]
        bu[...] = _dot(u, b_ref[...])
        tabs = [tab_ref[k] for k in range(TAB_ROWS)]

        def group(r0, carry):
            rows = pl.ds(r0, SUBLANES)
            a, b = _scan_rows(bu[rows, :SCAN_COLS], bu[rows, SCAN_COLS:], tabs, carry, False)
            xs[rows, :SCAN_COLS] = a
            xs[rows, SCAN_COLS:] = b
            return (jnp.broadcast_to(a[SUBLANES - 1:, :], a.shape), jnp.broadcast_to(b[SUBLANES - 1:, :], b.shape))

        zero = jnp.zeros((SUBLANES, SCAN_COLS), F32)
        carry = (zero, zero)
        for r0 in _time_groups(seq, False):
            carry = group(r0, carry)
        span = SCAN_UNROLL * SUBLANES

        def groups(t, c):
            for k in range(SCAN_UNROLL):
                c = group(pl.multiple_of(t * span, span) + k * SUBLANES, c)
            return c

        lax.fori_loop(0, seq // span, groups, carry)
        x16 = xs[...].astype(BF16)
        x_ref[...] = x16
        contrib = _dot(x16, c_ref[...])

        @pl.when(j % 2 == 0)
        def _():
            y_ref[...] = contrib + d_ref[...] * u.astype(F32)

        @pl.when(j % 2 == 1)
        def _():
            y_ref[...] += contrib

    return _pcall(
        body, name=name, grid=(n_b, N_SCAN_BLK),
        in_specs=[pl.BlockSpec((None, n_l, LANES), lambda b, j: (b, 0, u_blk + j // 2)),
                  pl.BlockSpec((None, LANES, 2 * SCAN_COLS), lambda b, j: (j, 0, 0)),
                  pl.BlockSpec((TAB_ROWS, SUBLANES, SCAN_COLS), lambda b, j: (0, 0, j)),
                  pl.BlockSpec((None, 2 * SCAN_COLS, LANES), lambda b, j: (j, 0, 0)),
                  pl.BlockSpec((1, LANES), lambda b, j: (0, j // 2))],
        out_specs=[pl.BlockSpec((None, n_l, 2 * SCAN_COLS), lambda b, j: (b, 0, j)),
                   pl.BlockSpec((None, n_l, LANES), lambda b, j: (b, 0, j // 2))],
        out_shape=[jax.ShapeDtypeStruct((n_b, n_l, 2 * N_STATES), BF16),
                   jax.ShapeDtypeStruct((n_b, n_l, SSM_WIDTH), F32)],
        scratch_shapes=[pltpu.VMEM((n_l, 2 * SCAN_COLS), F32)] * 2,
        compiler_params=_cp(("arbitrary", "arbitrary"), VMEM_BIG),
    )(proj3, b_comb, tabf, c_comb, dvec)


def ssm_backward_scan(dyraw3, xs3, dproj3, c_comb_t, tabr, b_comb_t, dvec, seq, name):
    n_b, n_l, _ = xs3.shape
    u_blk = (D_MODEL + 2 * KV_WIDTH) // LANES

    def body(dy_ref, x_ref, _, c_ref, tab_ref, b_ref, d_ref, du_ref, g_ref, dlr_ref, dli_ref, dx, gs, xs, du_acc):
        j = pl.program_id(1)
        dy = dy_ref[...]
        dx[...] = _dot(dy, c_ref[...])
        xs[...] = x_ref[...].astype(F32)
        tabs = [tab_ref[k] for k in range(TAB_ROWS)]
        last_row = lax.broadcasted_iota(jnp.int32, (SUBLANES, SCAN_COLS), 0) == SUBLANES - 1

        def group(r0, state):
            cr, ci, acc_r, acc_i = state
            rows = pl.ds(r0, SUBLANES)
            a, b = _scan_rows(dx[rows, :SCAN_COLS], dx[rows, SCAN_COLS:], tabs, (cr, ci), True)
            gs[rows, :SCAN_COLS] = a
            gs[rows, SCAN_COLS:] = b
            na = jnp.where(last_row, cr, pltpu.roll(a, SUBLANES - 1, axis=0))
            nb = jnp.where(last_row, ci, pltpu.roll(b, SUBLANES - 1, axis=0))
            xa, xb = xs[rows, :SCAN_COLS], xs[rows, SCAN_COLS:]
            return (jnp.broadcast_to(a[:1, :], a.shape), jnp.broadcast_to(b[:1, :], b.shape),
                    acc_r + na * xa + nb * xb, acc_i + nb * xa - na * xb)

        zero = jnp.zeros((SUBLANES, SCAN_COLS), F32)
        span = SCAN_UNROLL * SUBLANES
        n_spans = seq // span

        def groups(t, s):
            for k in reversed(range(SCAN_UNROLL)):
                s = group(pl.multiple_of((n_spans - 1 - t) * span, span) + k * SUBLANES, s)
            return s

        state = lax.fori_loop(0, n_spans, groups, (zero, zero, zero, zero))
        for r0 in _time_groups(seq, True):
            state = group(r0, state)
        dlr_ref[...] = state[2]
        dli_ref[...] = state[3]
        g16 = gs[...].astype(BF16)
        g_ref[...] = g16
        contrib = _dot(g16, b_ref[...])

        @pl.when(j % 2 == 0)
        def _():
            du_acc[...] = contrib + d_ref[...] * dy.astype(F32)

        @pl.when(j % 2 == 1)
        def _():
            du_ref[...] = (du_acc[...] + contrib).astype(BF16)

    state_blk = pl.BlockSpec((None, n_l, 2 * SCAN_COLS), lambda b, j: (b, 0, j))
    dl_blk = pl.BlockSpec((None, SUBLANES, SCAN_COLS), lambda b, j: (b, 0, j))
    return _pcall(
        body, name=name, grid=(n_b, N_SCAN_BLK),
        in_specs=[pl.BlockSpec((None, n_l, LANES), lambda b, j: (b, 0, j // 2)), state_blk,
                  pl.BlockSpec(memory_space=pl.ANY),
                  pl.BlockSpec((None, LANES, 2 * SCAN_COLS), lambda b, j: (j, 0, 0)),
                  pl.BlockSpec((TAB_ROWS, SUBLANES, SCAN_COLS), lambda b, j: (0, 0, j)),
                  pl.BlockSpec((None, 2 * SCAN_COLS, LANES), lambda b, j: (j, 0, 0)),
                  pl.BlockSpec((1, LANES), lambda b, j: (0, j // 2))],
        out_specs=[pl.BlockSpec((None, n_l, LANES), lambda b, j: (b, 0, u_blk + j // 2)), state_blk, dl_blk, dl_blk],
        out_shape=[jax.ShapeDtypeStruct(dproj3.shape, BF16), jax.ShapeDtypeStruct((n_b, n_l, 2 * N_STATES), BF16),
                   jax.ShapeDtypeStruct((n_b, SUBLANES, N_STATES), F32), jax.ShapeDtypeStruct((n_b, SUBLANES, N_STATES), F32)],
        scratch_shapes=[pltpu.VMEM((n_l, 2 * SCAN_COLS), F32)] * 3 + [pltpu.VMEM((n_l, LANES), F32)],
        input_output_aliases={2: 0},
        compiler_params=_cp(("arbitrary", "arbitrary"), VMEM_BIG),
    )(dyraw3, xs3, dproj3, c_comb_t, tabr, b_comb_t, dvec)


def ssm_param_grads(proj, gs, xs, dyraw, tm, name):
    t_rows = proj.shape[0]
    ni = t_rows // tm
    u_blk = (D_MODEL + 2 * KV_WIDTH) // LANES
    width = 2 * SCAN_COLS

    def body(u_ref, g_ref, x_ref, dy_ref, db_ref, dc_ref, dd_ref):
        cb, i = pl.program_id(0), pl.program_id(1)
        u, dy = u_ref[...], dy_ref[...]
        _accumulate(db_ref, _dot_tn(u, g_ref[...]), i == 0)
        _accumulate(dc_ref, _dot_tn(x_ref[...], dy), i == 0)

        @pl.when(cb % 2 == 0)
        def _():
            _accumulate(dd_ref, jnp.sum(dy.astype(F32) * u.astype(F32), axis=0, keepdims=True), i == 0)

    return _pcall(
        body, name=name, grid=(N_SCAN_BLK, ni),
        in_specs=[pl.BlockSpec((tm, LANES), lambda cb, i: (i, u_blk + cb // 2)),
                  pl.BlockSpec((tm, width), lambda cb, i: (i, cb)),
                  pl.BlockSpec((tm, width), lambda cb, i: (i, cb)),
                  pl.BlockSpec((tm, LANES), lambda cb, i: (i, cb // 2))],
        out_specs=[pl.BlockSpec((None, LANES, width), lambda cb, i: (cb, 0, 0)),
                   pl.BlockSpec((None, width, LANES), lambda cb, i: (cb, 0, 0)),
                   pl.BlockSpec((1, LANES), lambda cb, i: (0, cb // 2))],
        out_shape=[jax.ShapeDtypeStruct((N_SCAN_BLK, LANES, width), F32),
                   jax.ShapeDtypeStruct((N_SCAN_BLK, width, LANES), F32), jax.ShapeDtypeStruct((1, SSM_WIDTH), F32)],
        compiler_params=_cp(("arbitrary", "arbitrary"), VMEM_BIG),
    )(proj, gs, xs, dyraw)


def sum_leading(x, name):
    def body(x_ref, o_ref):
        acc = x_ref[0]
        for k in range(1, x.shape[0]):
            acc = acc + x_ref[k]
        o_ref[...] = acc

    return _pcall(body, name=name, out_shape=jax.ShapeDtypeStruct(x.shape[1:], x.dtype))(x)


WEIGHTS = ['meta_tokens', 'ffn1_norm', 'ffn1_w1', 'ffn1_w3', 'ffn1_w2', 'mix_norm', 'w_in', 'attn_sinks', 'ssm_a_re',
           'ssm_a_im', 'ssm_log_step', 'ssm_b_re', 'ssm_b_im', 'ssm_c_re', 'ssm_c_im', 'ssm_d', 'ssm_glu_a', 'ssm_glu_b',
           'w_out', 'ffn2_norm', 'ffn2_w1', 'ffn2_w3', 'ffn2_w2', 'final_norm']
SHARDED = ['ffn1_w1', 'ffn1_w3', 'ffn1_w2', 'ffn2_w1', 'ffn2_w3', 'ffn2_w2', 'w_in', 'ssm_glu_a', 'ssm_glu_b', 'w_out']
REPLICATED = ['ffn1_norm', 'mix_norm', 'ffn2_norm', 'final_norm', 'attn_sinks', 'ssm_a_re', 'ssm_a_im', 'ssm_log_step',
              'ssm_b_re', 'ssm_b_im', 'ssm_c_re', 'ssm_c_im', 'ssm_d']
PACK_COLS = 1024


def _pack(arrays):
    parts = []
    for a in arrays:
        flat = a.reshape(-1)
        chunk = SUBLANES * PACK_COLS
        padded = -(-flat.shape[0] // chunk) * chunk
        parts.append(jnp.pad(flat, (0, padded - flat.shape[0])).reshape(-1, PACK_COLS))
    return jnp.concatenate(parts, axis=0)


def _unpack(packed, shapes):
    out, row = [], 0
    for shape in shapes:
        size = 1
        for s in shape:
            size *= s
        chunk = SUBLANES * PACK_COLS
        rows = -(-size // chunk) * SUBLANES
        out.append(packed[row:row + rows].reshape(-1)[:size].reshape(shape))
        row += rows
    return out


def kernel(x, meta_tokens, ffn1_norm, ffn1_w1, ffn1_w3, ffn1_w2, mix_norm, w_in, attn_sinks, ssm_a_re, ssm_a_im, ssm_log_step, ssm_b_re, ssm_b_im, ssm_c_re, ssm_c_im, ssm_d, ssm_glu_a, ssm_glu_b, w_out, ffn2_norm, ffn2_w1, ffn2_w3, ffn2_w2, final_norm, loss_target, m_meta_tokens, m_ffn1_norm, m_ffn1_w1, m_ffn1_w3, m_ffn1_w2, m_mix_norm, m_w_in, m_attn_sinks, m_ssm_a_re, m_ssm_a_im, m_ssm_log_step, m_ssm_b_re, m_ssm_b_im, m_ssm_c_re, m_ssm_c_im, m_ssm_d, m_ssm_glu_a, m_ssm_glu_b, m_w_out, m_ffn2_norm, m_ffn2_w1, m_ffn2_w3, m_ffn2_w2, m_final_norm, v_meta_tokens, v_ffn1_norm, v_ffn1_w1, v_ffn1_w3, v_ffn1_w2, v_mix_norm, v_w_in, v_attn_sinks, v_ssm_a_re, v_ssm_a_im, v_ssm_log_step, v_ssm_b_re, v_ssm_b_im, v_ssm_c_re, v_ssm_c_im, v_ssm_d, v_ssm_glu_a, v_ssm_glu_b, v_w_out, v_ffn2_norm, v_ffn2_w1, v_ffn2_w3, v_ffn2_w2, v_final_norm):
    given = dict(locals())
    w = {n: given[n] for n in WEIGHTS}
    m = {n: given["m_" + n] for n in WEIGHTS}
    v = {n: given["v_" + n] for n in WEIGHTS}

    n_b, seq, _ = x.shape
    n_l = seq + N_META
    t_rows = n_b * n_l
    tm = _row_tile(n_l, 688)
    px, py, pc = _my_place()
    me = 4 * px + 2 * py + pc

    glu = jnp.stack([ssm_glu_a[0], ssm_glu_b[0]]).astype(BF16)
    ffn_names = ['ffn1_w1', 'ffn1_w3', 'ffn1_w2', 'ffn2_w1', 'ffn2_w3', 'ffn2_w2']

    def hidden_on_rows(n, t):
        return t[0] if n.endswith('w2') else t[0].T

    def hidden_on_rows_back(n, t):
        return t[None] if n.endswith('w2') else t.T[None]

    me_idx = jnp.reshape(me, (1,)).astype(jnp.int32)
    first_names, later_names = ffn_names[:3], ffn_names[3:]
    *first, metag = all_gather_list(
        [hidden_on_rows(n, w[n]).astype(BF16) for n in first_names] + [meta_tokens], meta_tokens, "ag_first")
    win_send, win_recv, win_shard, win_land, win_token = exchange_start(
        [w_in[0].astype(BF16)], first[0], True, "ag_w_in_start")
    later_shards = [hidden_on_rows(n, w[n]).astype(BF16) for n in later_names] + [glu, w_out[0].astype(BF16)]
    ag_send, ag_recv, later_shards, later_lands, ag_token = exchange_start(later_shards, win_token, True, "ag_later_start")
    full = {n: g.reshape(D_FF, D_MODEL) for n, g in zip(first_names, first)}
    meta_full = metag.transpose(1, 0, 2).reshape(N_META, D_MODEL)

    final_g = final_norm.reshape(1, D_MODEL)

    ar = ssm_a_re.reshape(1, N_STATES)
    ai = ssm_a_im.reshape(1, N_STATES)
    ls = jnp.repeat(ssm_log_step.reshape(SSM_GROUPS), SSM_STATE).reshape(1, N_STATES)
    br_t = ssm_b_re[0].transpose(2, 0, 1).reshape(SSM_GROUP, N_STATES)
    bi_t = ssm_b_im[0].transpose(2, 0, 1).reshape(SSM_GROUP, N_STATES)
    bbr, bbi, tabf, tabr = ssm_prepare(ar, ai, ls, br_t, bi_t, "ssm_prepare")
    bbr_g = bbr.reshape(SSM_GROUP, SSM_GROUPS, SSM_STATE).transpose(1, 0, 2)
    bbi_g = bbi.reshape(SSM_GROUP, SSM_GROUPS, SSM_STATE).transpose(1, 0, 2)
    groups_per_blk = SCAN_COLS // SSM_STATE
    half = ((jnp.arange(N_SCAN_BLK) % 2)[:, None] == jnp.arange(2)[None, :]).astype(F32)
    eye = jnp.eye(groups_per_blk, dtype=F32)

    def scan_blocks(re_g, im_g):
        def one(t):
            t = t.reshape(N_SCAN_BLK, groups_per_blk, SSM_GROUP, SSM_STATE)
            t = t[:, :, :, None, :] * eye[None, :, None, :, None]
            t = t.reshape(N_SCAN_BLK, LANES // 2, SCAN_COLS)
            return (t[:, None] * half[:, :, None, None]).reshape(N_SCAN_BLK, LANES, SCAN_COLS)
        return jnp.concatenate([one(re_g), one(im_g)], axis=-1).astype(BF16)

    b_comb = scan_blocks(bbr_g, bbi_g)
    c_comb_t = scan_blocks(ssm_c_re[0], -ssm_c_im[0])
    b_comb_t, c_comb = b_comb.transpose(0, 2, 1), c_comb_t.transpose(0, 2, 1)

    ffn1_w = (full['ffn1_w1'], full['ffn1_w3'], full['ffn1_w2'])
    h1, hn1, a1, b1, h0 = ffn_forward(x, ffn1_norm, *ffn1_w, ag_token, tm, "ffn1_fwd", meta=meta_full)
    win_shard, (wing,) = exchange_wait(win_send, win_recv, win_shard, win_land, h1, True, "ag_w_in_wait")
    wing = lax.dynamic_update_slice_in_dim(wing, win_shard[0][None], me, axis=0)
    hnm, proj = mix_forward(h1, mix_norm, wing, tm, "mix_fwd")
    proj3 = proj.reshape(n_b, n_l, IN_WIDTH)
    attn3 = attention_forward(proj3, attn_sinks, seq, "attn_fwd")
    attn = attn3.reshape(t_rows, D_MODEL)
    xs3, yraw3 = ssm_forward_scan(proj3, b_comb, tabf, c_comb, ssm_d, seq, "ssm_fwd")
    yraw = yraw3.reshape(t_rows, SSM_WIDTH)
    later_shards, later = exchange_wait(ag_send, ag_recv, later_shards, later_lands, yraw3, True, "ag_later_wait")
    later = [lax.dynamic_update_slice_in_dim(z, s[None], me, axis=0) for z, s in zip(later, later_shards)]
    for n, g in zip(later_names, later):
        full[n] = g.reshape(D_FF, D_MODEL)
    ffn2_w = (full['ffn2_w1'], full['ffn2_w3'], full['ffn2_w2'])
    glug, wog = later[len(later_names):]
    glu_a = glug[:, 0].transpose(1, 0, 2).reshape(SSM_WIDTH, D_MODEL)
    glu_b = glug[:, 1].transpose(1, 0, 2).reshape(SSM_WIDTH, D_MODEL)
    w_out_full = wog.reshape(D_MODEL, D_MODEL)
    h2 = merge_forward(h1, yraw, attn, proj, glu_a, glu_b, w_out_full, tm, "merge_fwd")
    h3, hn2, a2, b2 = ffn_forward(h2, ffn2_norm, *ffn2_w, ag_token, tm, "ffn2_fwd")
    dh3, loss_part, g_final = final_loss_backward(h3, loss_target, final_g, seq, tm, "loss_bwd")
    loss = lax.psum(loss_part[0, 0], ("x", "y", "c"))

    def blocked_ffn(d_w1t, d_w3t, d_w2):
        return tuple(t.reshape(N_DEV, FF_BLK, D_MODEL) for t in (d_w1t, d_w3t, d_w2))

    def blocked_cols(full_grad):
        r = full_grad.shape[0]
        return full_grad.reshape(r, N_DEV, full_grad.shape[1] // N_DEV).transpose(1, 0, 2).astype(BF16)

    early = {}

    def start_reduce(names, tag):
        srcs = [dw[n] for n in names]
        send, recv, srcs, lands, token = exchange_start(srcs, srcs[0], False, "rs_" + tag + "_start")
        early[tag] = (names, send, recv, srcs, lands)
        return token

    dw = {}
    da2, db2, dh3_half = ffn_backward_hidden(dh3, a2, b2, ffn2_w[2], g_final, tm, "ffn2_bwd_hid")
    dw['ffn2_w1'], dw['ffn2_w3'], dw['ffn2_w2'] = blocked_ffn(
        *ffn_backward_weights(hn2, dh3_half, a2, b2, da2, db2, n_l, FF_BWD_COLS, "ffn2_bwd_w"))
    token = start_reduce(later_names, "ffn2")
    dh2, g_ffn2_norm = ffn_backward_input(dh3, h2, ffn2_norm, da2, db2, ffn2_w[0], ffn2_w[1], token, tm, "ffn2_bwd_in")
    dattn, dyraw, dproj, *for_weights = merge_backward(dh2, yraw, attn, proj, glu_a, glu_b, w_out_full, token, tm,
                                                       "merge_bwd")
    d_wo, d_ga, d_gb = merge_backward_weights(*for_weights, tm, "merge_bwd_w")
    dw['ssm_glu_a'] = blocked_cols(d_ga)
    dw['ssm_glu_b'] = blocked_cols(d_gb)
    dw['w_out'] = d_wo.reshape(N_DEV, D_MODEL // N_DEV, D_MODEL).astype(BF16)
    token = start_reduce(['ssm_glu_a', 'ssm_glu_b', 'w_out'], "mix")
    dproj3 = dproj.reshape(n_b, n_l, IN_WIDTH)
    dproj3, dsink_p = attention_backward(proj3, dattn.reshape(n_b, n_l, D_MODEL), dproj3, attn_sinks, token, seq,
                                         "attn_bwd")
    dproj3, gs3, dlr_p, dli_p = ssm_backward_scan(
        dyraw.reshape(n_b, n_l, SSM_WIDTH), xs3, dproj3, c_comb_t, tabr, b_comb_t, ssm_d, seq, "ssm_bwd")
    dproj = dproj3.reshape(t_rows, IN_WIDTH)
    d_bd, d_cd, g_d = ssm_param_grads(proj, gs3.reshape(t_rows, 2 * N_STATES), xs3.reshape(t_rows, 2 * N_STATES),
                                      dyraw, n_l, "ssm_bwd_w")
    w_in_full = wing.transpose(1, 0, 2).reshape(D_MODEL, IN_WIDTH)
    dh1, g_mix_norm = mix_backward_act(dh2, h1, mix_norm, dproj, w_in_full, tm, "mix_bwd_act")
    dw['w_in'] = mix_backward_weights(hnm, dproj, n_l, "mix_bwd_w")
    token = start_reduce(['w_in'], "w_in")

    def group_blocks(part, channels_first):
        if channels_first:
            t = jnp.sum(part.reshape(N_SCAN_BLK, 2, LANES // 2, SCAN_COLS) * half[:, :, None, None], axis=1)
            t = t.reshape(N_SCAN_BLK, groups_per_blk, SSM_GROUP, groups_per_blk, SSM_STATE)
            t = jnp.sum(t * eye[None, :, None, :, None], axis=3)
            return t.reshape(SSM_GROUPS, SSM_GROUP, SSM_STATE)
        t = jnp.sum(part.reshape(N_SCAN_BLK, SCAN_COLS, 2, LANES // 2) * half[:, None, :, None], axis=2)
        t = t.reshape(N_SCAN_BLK, groups_per_blk, SSM_STATE, groups_per_blk, SSM_GROUP)
        t = jnp.sum(t * eye[None, :, None, :, None], axis=3)
        return t.reshape(SSM_GROUPS, SSM_STATE, SSM_GROUP).transpose(0, 2, 1)

    dbbr = group_blocks(d_bd[:, :, :SCAN_COLS], True).transpose(1, 0, 2).reshape(SSM_GROUP, N_STATES)
    dbbi = group_blocks(d_bd[:, :, SCAN_COLS:], True).transpose(1, 0, 2).reshape(SSM_GROUP, N_STATES)
    g_c_re = group_blocks(d_cd[:, :SCAN_COLS, :], False)[None]
    g_c_im = -group_blocks(d_cd[:, SCAN_COLS:, :], False)[None]
    group_sum = (jnp.arange(N_STATES)[:, None] // SSM_STATE == jnp.arange(LANES)[None, :]).astype(F32)
    g_ar, g_ai, g_ls, g_br, g_bi = ssm_param_backward(
        ar, ai, ls, br_t, bi_t, dlr_p.reshape(n_b * SUBLANES, N_STATES), dli_p.reshape(n_b * SUBLANES, N_STATES),
        dbbr, dbbi, group_sum, "ssm_bwd_params")
    g_sinks = sum_leading(dsink_p, "sink_sum")[0:1, :N_KV_HEADS * Q_PER_KV]

    small = {
        'mix_norm': g_mix_norm, 'ffn2_norm': g_ffn2_norm, 'final_norm': g_final.reshape(D_MODEL),
        'attn_sinks': g_sinks, 'ssm_a_re': g_ar.reshape(1, SSM_GROUPS, SSM_STATE), 'ssm_a_im': g_ai.reshape(1, SSM_GROUPS, SSM_STATE),
        'ssm_log_step': g_ls[:, :SSM_GROUPS],
        'ssm_b_re': g_br.reshape(SSM_GROUP, SSM_GROUPS, SSM_STATE).transpose(1, 2, 0)[None],
        'ssm_b_im': g_bi.reshape(SSM_GROUP, SSM_GROUPS, SSM_STATE).transpose(1, 2, 0)[None],
        'ssm_c_re': g_c_re, 'ssm_c_im': g_c_im, 'ssm_d': g_d,
    }
    early_small = [n for n in REPLICATED if n in small]
    sg_send, sg_recv, sg_src, sg_land, token = exchange_start(
        [_pack([small[n] for n in early_small])], token, True, "ag_small_start")
    da1, db1, dh1_half = ffn_backward_hidden(dh1, a1, b1, ffn1_w[2], token, tm, "ffn1_bwd_hid")
    dw['ffn1_w1'], dw['ffn1_w3'], dw['ffn1_w2'] = blocked_ffn(
        *ffn_backward_weights(hn1, dh1_half, a1, b1, da1, db1, n_l, FF_BWD_COLS, "ffn1_bwd_w"))
    token = start_reduce(first_names, "ffn1")
    grad_x, meta_rows_grad, g_ffn1_norm = ffn_backward_input(
        dh1, h0, ffn1_norm, da1, db1, ffn1_w[0], ffn1_w[1], token, tm, "ffn1_bwd_in", examples=(n_b, seq))
    g_meta = sum_leading(meta_rows_grad, "meta_sum")

    grads, deltas, new_m, new_v = {}, {}, {}, {}

    def views(n):
        if n in ffn_names:
            return functools.partial(hidden_on_rows, n), functools.partial(hidden_on_rows_back, n)
        return (lambda t: t[0]), (lambda t: t[None])

    def finish_reduce(tag, previous):
        names, send, recv, srcs, lands = early[tag]
        srcs, lands = exchange_wait(send, recv, srcs, lands, previous, False, "rs_" + tag + "_wait")
        for n, g, land in zip(names, srcs, lands):
            two_d, back = views(n)
            out = adamw_exchanged(me_idx, g, land, two_d(w[n]), two_d(m[n]), two_d(v[n]), "adamw_" + n)
            grads[n], deltas[n], new_m[n], new_v[n] = (back(o) for o in out)
            previous = out[1]
        return previous

    previous = g_meta
    for tag in ("ffn2", "mix", "w_in"):
        previous = finish_reduce(tag, previous)

    zeros_meta = jnp.zeros((N_META, D_MODEL), F32)
    (late_parts,) = all_gather_list([_pack([g_ffn1_norm, g_meta])], previous, "ag_small_late")
    sg_src, (early_parts,) = exchange_wait(sg_send, sg_recv, sg_src, sg_land, late_parts, True, "ag_small_wait")
    early_parts = lax.dynamic_update_slice_in_dim(early_parts, sg_src[0][None], me, axis=0)

    def small_update(parts, names, extra, tag):
        pack_of = lambda d: _pack([d[n] for n in names] + extra)
        packed = adamw_small(parts, pack_of(w), pack_of(m), pack_of(v), "adamw_small_" + tag)
        unpacked = [_unpack(p, [w[n].shape for n in names] + [e.shape for e in extra]) for p in packed]
        for k, n in enumerate(names):
            grads[n], deltas[n], new_m[n], new_v[n] = (u[k] for u in unpacked)
        return packed, unpacked

    small_update(early_parts, early_small, [], "early")
    packed_out, unpacked = small_update(late_parts, ['ffn1_norm'], [zeros_meta], "late")
    g_meta_full = unpacked[0][-1]
    grads['meta_tokens'] = lax.dynamic_index_in_dim(
        g_meta_full.reshape(N_META, N_DEV, D_MODEL // N_DEV), me, axis=1, keepdims=False)
    deltas['meta_tokens'], new_m['meta_tokens'], new_v['meta_tokens'] = adamw_plain(
        grads['meta_tokens'], w['meta_tokens'], m['meta_tokens'], v['meta_tokens'], "adamw_meta")

    finish_reduce("ffn1", packed_out[0])

    return (loss, grad_x, *[grads[n] for n in WEIGHTS], *[deltas[n] for n in WEIGHTS],
            *[new_m[n] for n in WEIGHTS], *[new_v[n] for n in WEIGHTS])
```

```python
import functools

import jax
import jax.numpy as jnp
from jax import lax
from jax.experimental import pallas as pl
from jax.experimental.pallas import tpu as pltpu

F32 = jnp.float32
BF16 = jnp.bfloat16
MESH = pl.DeviceIdType.MESH

N_DEV = 8
D_MODEL = 1024
N_META = 16
HEAD_DIM = 64
N_KV_HEADS = 4
Q_PER_KV = 4
BLOCK = 128
KV_WIDTH = N_KV_HEADS * HEAD_DIM
SSM_GROUP = 16
SSM_WIDTH = 512
SSM_GROUPS = 32
SSM_STATE = 64
N_STATES = SSM_GROUPS * SSM_STATE
D_FF = 2816
FF_BLK = D_FF // N_DEV
IN_WIDTH = 4096
IN_BLK = IN_WIDTH // N_DEV
NORM_EPS = 1e-6
NEG_INF = -1e30
SCAN_COLS = 256
N_SCAN_BLK = N_STATES // SCAN_COLS
SUBLANES = 8
LANES = 128
MXU_WIDTH = 256
FF_BWD_COLS = MXU_WIDTH

ADAM_LR = 0.001
ADAM_B1 = 0.9
ADAM_B2 = 0.999
ADAM_EPS = 1e-08
ADAM_WD = 0.01
ADAM_STEP = 10

VMEM_BIG = 56 * 1024 * 1024


def _cp(sem=None, vmem=None):
    kw = {}
    if sem is not None:
        kw["dimension_semantics"] = sem
    if vmem is not None:
        kw["vmem_limit_bytes"] = vmem
    return pltpu.CompilerParams(**kw)


def _pcall(body, **kw):
    return pl.pallas_call(body, **kw)


def _dot(a, b):
    return jnp.dot(a, b, preferred_element_type=F32)


def _dot_nt(a, b):
    return lax.dot_general(a, b, (((1,), (1,)), ((), ())), preferred_element_type=F32)


def _dot_tn(a, b):
    return lax.dot_general(a, b, (((0,), (0,)), ((), ())), preferred_element_type=F32)


def _sigmoid(x):
    return 0.5 + 0.5 * jnp.tanh(0.5 * x)


def _row_tile(rows, cap):
    best = None
    for t in range(16, min(rows, cap) + 1, 16):
        if rows % t == 0:
            best = t
    assert best is not None, rows
    return best


def _my_place():
    return lax.axis_index("x"), lax.axis_index("y"), lax.axis_index("c")


def all_gather_list(shards, after, name):
    n = len(shards)

    def body(*refs):
        ins, outs = refs[:n], refs[n + 1:2 * n + 1]
        send_sems, recv_sems, local_sems = refs[2 * n + 1:]
        x, y, c = _my_place()
        me, sibling = (x, y, c), (x, y, 1 - c)
        chips = [(1 - x, y), (x, 1 - y), (1 - x, 1 - y)]

        def blk(a, px, py, pc):
            return outs[a].at[4 * px + 2 * py + pc]

        def copy(a, k, block, to, src=None):
            return pltpu.make_async_remote_copy(
                src_ref=blk(a, *block) if src is None else src, dst_ref=blk(a, *block),
                send_sem=send_sems.at[a * 7 + k], recv_sem=recv_sems.at[a * 7 + k],
                device_id=to, device_id_type=MESH)

        mine = [pltpu.make_async_copy(ins[a], blk(a, *me), local_sems.at[a]) for a in range(n)]
        for cp in mine:
            cp.start()
        first = []
        for a in range(n):
            first.append(copy(a, 0, me, sibling, src=ins[a]))
            first += [copy(a, 1 + j, me, (*chip, c), src=ins[a]) for j, chip in enumerate(chips)]
        for cp in first:
            cp.start()
        passed = []
        for j, chip in enumerate(chips):
            for a in range(n):
                copy(a, 1 + j, (*chip, c), me).wait_recv()
                cp = copy(a, 4 + j, (*chip, c), sibling)
                cp.start()
                passed.append(cp)
        for a in range(n):
            copy(a, 0, sibling, me).wait_recv()
            for j, chip in enumerate(chips):
                copy(a, 4 + j, (*chip, 1 - c), me).wait_recv()
        for cp in first + passed:
            cp.wait_send()
        for cp in mine:
            cp.wait()

    any_spec = pl.BlockSpec(memory_space=pl.ANY)
    return _pcall(
        body, name=name,
        out_shape=[jax.ShapeDtypeStruct((N_DEV,) + s.shape, s.dtype) for s in shards],
        in_specs=[any_spec] * (n + 1), out_specs=[any_spec] * n,
        scratch_shapes=[pltpu.SemaphoreType.DMA((7 * n,)), pltpu.SemaphoreType.DMA((7 * n,)),
                        pltpu.SemaphoreType.DMA((n,))],
    )(*shards, after)


HBM_SPEC = pl.BlockSpec(memory_space=pltpu.HBM)
SEM_SPEC = pl.BlockSpec(memory_space=pltpu.SEMAPHORE)
N_PEERS = N_DEV - 1


def _related(k):
    x, y, c = _my_place()
    px = 1 - x if k & 4 else x
    py = 1 - y if k & 2 else y
    pc = 1 - c if k & 1 else c
    return (px, py, pc), 4 * px + 2 * py + pc


def _exchange_copies(srcs, lands, send_sems, recv_sems, gather):
    x, y, c = _my_place()
    me = 4 * x + 2 * y + c
    copies = []
    for a, (src, land) in enumerate(zip(srcs, lands)):
        for k in range(1, N_DEV):
            peer, d = _related(k)
            copies.append(pltpu.make_async_remote_copy(
                src_ref=src if gather else src.at[d], dst_ref=land.at[me] if gather else land.at[k],
                send_sem=send_sems.at[a * N_PEERS + k - 1], recv_sem=recv_sems.at[a * N_PEERS + k - 1],
                device_id=peer, device_id_type=MESH))
    return copies


def exchange_start(srcs, after, gather, name):
    n = len(srcs)
    land_shapes = [((N_DEV,) + s.shape) if gather else s.shape for s in srcs]

    def body(*refs):
        send_sems, recv_sems = refs[2 * n + 1], refs[2 * n + 2]
        for cp in _exchange_copies(refs[:n], refs[n:2 * n], send_sems, recv_sems, gather):
            cp.start()
        token = refs[-1]
        token[...] = jnp.zeros_like(token)

    sems = pltpu.SemaphoreType.DMA((n * N_PEERS,))
    lands = [pltpu.with_memory_space_constraint(lax.empty(shape, s.dtype), pltpu.HBM) for shape, s in zip(land_shapes, srcs)]
    out = _pcall(
        body, name=name,
        out_shape=(sems, sems, *[pltpu.HBM(s.shape, s.dtype) for s in srcs],
                   *[pltpu.HBM(shape, s.dtype) for shape, s in zip(land_shapes, srcs)],
                   jax.ShapeDtypeStruct((SUBLANES, LANES), F32)),
        in_specs=[HBM_SPEC] * (2 * n) + [pl.BlockSpec(memory_space=pl.ANY)],
        out_specs=(SEM_SPEC, SEM_SPEC, *[HBM_SPEC] * (2 * n), pl.BlockSpec(memory_space=pltpu.VMEM)),
        input_output_aliases={i: 2 + i for i in range(2 * n)},
        compiler_params=pltpu.CompilerParams(has_side_effects=pltpu.SideEffectType.DATAFLOW_SIDE_EFFECTING),
    )(*[pltpu.with_memory_space_constraint(s, pltpu.HBM) for s in srcs], *lands, after)
    return out[0], out[1], list(out[2:2 + n]), list(out[2 + n:2 + 2 * n]), out[-1]


def exchange_wait(send_sems, recv_sems, srcs, lands, after, gather, name):
    n = len(srcs)

    def body(*refs):
        for cp in _exchange_copies(refs[:n], refs[n:2 * n], refs[2 * n], refs[2 * n + 1], gather):
            cp.wait_send()
            cp.wait_recv()

    out = _pcall(
        body, name=name,
        out_shape=(*[pltpu.HBM(s.shape, s.dtype) for s in srcs], *[pltpu.HBM(z.shape, z.dtype) for z in lands]),
        in_specs=[HBM_SPEC] * (2 * n) + [SEM_SPEC, SEM_SPEC, pl.BlockSpec(memory_space=pl.ANY)],
        out_specs=tuple([HBM_SPEC] * (2 * n)),
        input_output_aliases={i: i for i in range(2 * n)},
        compiler_params=pltpu.CompilerParams(has_side_effects=pltpu.SideEffectType.DATAFLOW_SIDE_EFFECTING),
    )(*srcs, *lands, send_sems, recv_sems, after)
    return list(out[:n]), list(out[n:])


def adamw_exchanged(me, g, land, w, m, v, name):
    rows, cols = w.shape
    tr = _row_tile(rows, 256)

    def body(me_ref, g_ref, land_ref, w_ref, m_ref, v_ref, go_ref, d_ref, mo_ref, vo_ref):
        grad = g_ref[...].astype(F32)
        for k in range(1, N_DEV):
            grad = grad + land_ref[k].astype(F32)
        delta, m_new, v_new = _adam_math(w_ref[...], grad, m_ref[...], v_ref[...])
        go_ref[...] = grad
        d_ref[...] = delta
        mo_ref[...] = m_new
        vo_ref[...] = v_new

    tile = pl.BlockSpec((tr, cols), lambda r, ix: (r, 0))
    out = jax.ShapeDtypeStruct((rows, cols), F32)
    return _pcall(
        body, name=name, out_shape=[out] * 4,
        grid_spec=pltpu.PrefetchScalarGridSpec(
            num_scalar_prefetch=1, grid=(rows // tr,),
            in_specs=[pl.BlockSpec((None, tr, cols), lambda r, ix: (ix[0], r, 0)),
                      pl.BlockSpec((N_DEV, tr, cols), lambda r, ix: (0, r, 0)), tile, tile, tile],
            out_specs=[tile] * 4),
        compiler_params=_cp(("arbitrary",)),
    )(me, g, land, w, m, v)


def _adam_math(w, g, m, v):
    m = ADAM_B1 * m + (1.0 - ADAM_B1) * g
    v = ADAM_B2 * v + (1.0 - ADAM_B2) * (g * g)
    m_hat = m / (1.0 - ADAM_B1 ** ADAM_STEP)
    v_hat = v / (1.0 - ADAM_B2 ** ADAM_STEP)
    delta = -ADAM_LR * (m_hat / (jnp.sqrt(v_hat) + ADAM_EPS) + ADAM_WD * w)
    return delta, m, v


def adamw_small(parts, w, m, v, name):
    _, rows, cols = parts.shape

    def body(p_ref, w_ref, m_ref, v_ref, go_ref, d_ref, mo_ref, vo_ref):
        grad = p_ref[0]
        for k in range(1, N_DEV):
            grad = grad + p_ref[k]
        delta, m_new, v_new = _adam_math(w_ref[...], grad, m_ref[...], v_ref[...])
        go_ref[...] = grad
        d_ref[...] = delta
        mo_ref[...] = m_new
        vo_ref[...] = v_new

    out = jax.ShapeDtypeStruct((rows, cols), F32)
    return _pcall(body, name=name, out_shape=[out] * 4, compiler_params=_cp(vmem=VMEM_BIG))(parts, w, m, v)


def adamw_plain(g, w, m, v, name):
    def body(g_ref, w_ref, m_ref, v_ref, d_ref, mo_ref, vo_ref):
        delta, m_new, v_new = _adam_math(w_ref[...], g_ref[...], m_ref[...], v_ref[...])
        d_ref[...] = delta
        mo_ref[...] = m_new
        vo_ref[...] = v_new

    out = jax.ShapeDtypeStruct(w.shape, F32)
    return _pcall(body, name=name, out_shape=[out] * 3)(g, w, m, v)


def _rms_fwd(x, g):
    r = lax.rsqrt(jnp.mean(x * x, axis=-1, keepdims=True) + NORM_EPS)
    return x * r * g


def _rms_bwd(x, g, dy):
    r = lax.rsqrt(jnp.mean(x * x, axis=-1, keepdims=True) + NORM_EPS)
    xh = x * r
    t = dy * g
    dx = r * (t - xh * jnp.mean(t * xh, axis=-1, keepdims=True))
    return dx, jnp.sum(dy * xh, axis=0, keepdims=True)


def _accumulate(ref, val, first):
    @pl.when(first)
    def _():
        ref[...] = val

    @pl.when(jnp.logical_not(first))
    def _():
        ref[...] += val


def _col_chunks(width):
    return [(c0, min(MXU_WIDTH, width - c0)) for c0 in range(0, width, MXU_WIDTH)]


ANY_SPEC = pl.BlockSpec(memory_space=pl.ANY)


def ffn_forward(h, norm, w1, w3, w2, after, tm, name, meta=None):
    if meta is None:
        t_rows = h.shape[0]
        h_spec = pl.BlockSpec((tm, D_MODEL), lambda i: (i, 0))
    else:
        tiles = (h.shape[1] + N_META) // tm
        t_rows = h.shape[0] * tiles * tm
        h_spec = pl.BlockSpec((None, tm, D_MODEL), lambda i: (i // tiles, i % tiles, 0))

    def body(h_ref, g_ref, w1_ref, w3_ref, w2_ref, _, *rest):
        if meta is None:
            out_ref, hn_ref, a_ref, b_ref, hid_ref = rest
            h_in = h_ref[...]
        else:
            meta_ref, out_ref, hn_ref, a_ref, b_ref, h0_ref, hid_ref = rest
            h_in = h_ref[...]
            with_meta = jnp.concatenate([h_in[:tm - N_META], meta_ref[...]], axis=0)
            h_in = jnp.where(pl.program_id(0) % tiles == tiles - 1, with_meta, h_in)
            h0_ref[...] = h_in
        hn = _rms_fwd(h_in, g_ref[...]).astype(BF16)
        hn_ref[...] = hn
        for c0, cw in _col_chunks(D_FF):
            a = _dot_nt(hn, w1_ref[c0:c0 + cw, :])
            b = _dot_nt(hn, w3_ref[c0:c0 + cw, :])
            a_ref[:, c0:c0 + cw] = a.astype(BF16)
            b_ref[:, c0:c0 + cw] = b.astype(BF16)
            hid_ref[:, c0:c0 + cw] = (a * _sigmoid(a) * b).astype(BF16)
        out_ref[...] = h_in + 0.5 * _dot(hid_ref[...], w2_ref[...])

    row = pl.BlockSpec((tm, D_MODEL), lambda i: (i, 0))
    hid_blk = pl.BlockSpec((tm, D_FF), lambda i: (i, 0))
    weight = _resident((D_FF, D_MODEL))
    wide = jax.ShapeDtypeStruct((t_rows, D_MODEL), F32)
    extra_in = [] if meta is None else [meta]
    return _pcall(
        body, name=name, grid=(t_rows // tm,),
        in_specs=[h_spec, pl.BlockSpec((1, D_MODEL), lambda i: (0, 0)), weight, weight, weight, ANY_SPEC]
        + [pl.BlockSpec((N_META, D_MODEL), lambda i: (0, 0))] * len(extra_in),
        out_specs=[row, row, hid_blk, hid_blk] + [row] * len(extra_in),
        out_shape=[wide, jax.ShapeDtypeStruct((t_rows, D_MODEL), BF16),
                   jax.ShapeDtypeStruct((t_rows, D_FF), BF16), jax.ShapeDtypeStruct((t_rows, D_FF), BF16)]
        + [wide] * len(extra_in),
        scratch_shapes=[pltpu.VMEM((tm, D_FF), BF16)],
        compiler_params=_cp(("arbitrary",), VMEM_BIG),
    )(h, norm, w1, w3, w2, after, *extra_in)


def _resident(shape):
    return pl.BlockSpec(shape, lambda *_: (0,) * len(shape), pipeline_mode=pl.Buffered(1))


def ffn_backward_hidden(dh, a, b, w2, after, tm, name):
    t_rows = dh.shape[0]

    def body(dh_ref, a_ref, b_ref, w2_ref, _, da_ref, db_ref, dhb_ref):
        dhb = (0.5 * dh_ref[...]).astype(BF16)
        dhb_ref[...] = dhb
        for c0, cw in _col_chunks(D_FF):
            dhid = _dot_nt(dhb, w2_ref[c0:c0 + cw, :])
            av = a_ref[:, c0:c0 + cw].astype(F32)
            bv = b_ref[:, c0:c0 + cw].astype(F32)
            s = _sigmoid(av)
            da_ref[:, c0:c0 + cw] = (dhid * bv * (s * (1.0 + av * (1.0 - s)))).astype(BF16)
            db_ref[:, c0:c0 + cw] = (dhid * (av * s)).astype(BF16)

    hid = pl.BlockSpec((tm, D_FF), lambda i: (i, 0))
    row = pl.BlockSpec((tm, D_MODEL), lambda i: (i, 0))
    return _pcall(
        body, name=name, grid=(t_rows // tm,),
        in_specs=[row, hid, hid, _resident((D_FF, D_MODEL)), ANY_SPEC],
        out_specs=[hid, hid, row],
        out_shape=[jax.ShapeDtypeStruct((t_rows, D_FF), BF16), jax.ShapeDtypeStruct((t_rows, D_FF), BF16),
                   jax.ShapeDtypeStruct((t_rows, D_MODEL), BF16)],
        compiler_params=_cp(("arbitrary",), VMEM_BIG),
    )(dh, a, b, w2, after)


def ffn_backward_input(dh, h, norm, da, db, w1, w3, after, tm, name, examples=None):
    t_rows = h.shape[0]

    def body(dh_ref, h_ref, g_ref, da_ref, db_ref, w1_ref, w3_ref, _, dhin_ref, *rest):
        dg_ref = rest[-1]
        dhn = _dot(da_ref[...], w1_ref[...]) + _dot(db_ref[...], w3_ref[...])
        dx, dg = _rms_bwd(h_ref[...], g_ref[...], dhn)
        dhin = dh_ref[...] + dx
        dhin_ref[...] = dhin
        _accumulate(dg_ref, dg, pl.program_id(0) == 0)
        if examples is not None:
            @pl.when(pl.program_id(0) % tiles == tiles - 1)
            def _():
                rest[0][...] = dhin[tm - N_META:, :]

    row = pl.BlockSpec((tm, D_MODEL), lambda i: (i, 0))
    vec = pl.BlockSpec((1, D_MODEL), lambda i: (0, 0))
    hid = pl.BlockSpec((tm, D_FF), lambda i: (i, 0))
    if examples is None:
        out_specs = [row, vec]
        out_shape = [jax.ShapeDtypeStruct((t_rows, D_MODEL), F32), jax.ShapeDtypeStruct((1, D_MODEL), F32)]
    else:
        n_b, seq = examples
        tiles = (seq + N_META) // tm
        out_specs = [pl.BlockSpec((None, tm, D_MODEL), lambda i: (i // tiles, i % tiles, 0)),
                     pl.BlockSpec((None, N_META, D_MODEL), lambda i: (i // tiles, 0, 0)), vec]
        out_shape = [jax.ShapeDtypeStruct((n_b, seq, D_MODEL), F32), jax.ShapeDtypeStruct((n_b, N_META, D_MODEL), F32),
                     jax.ShapeDtypeStruct((1, D_MODEL), F32)]
    return _pcall(
        body, name=name, grid=(t_rows // tm,),
        in_specs=[row, row, vec, hid, hid, _resident((D_FF, D_MODEL)), _resident((D_FF, D_MODEL)), ANY_SPEC],
        out_specs=out_specs, out_shape=out_shape,
        compiler_params=_cp(("arbitrary",), VMEM_BIG),
    )(dh, h, norm, da, db, w1, w3, after)


def ffn_backward_weights(hn, dh, a, b, da, db, tm, tn, name):
    t_rows = hn.shape[0]
    ni = t_rows // tm
    kc = _row_tile(tm, 688)

    def body(hn_ref, dh_ref, a_ref, b_ref, da_ref, db_ref, dw1_ref, dw3_ref, dw2_ref, acc1, acc3, acc2):
        i = pl.program_id(1)
        parts = None
        for r0 in range(0, tm, kc):
            rows = slice(r0, r0 + kc)
            hn_v = hn_ref[rows, :]
            av = a_ref[rows, :].astype(F32)
            hid = (av * _sigmoid(av) * b_ref[rows, :].astype(F32)).astype(BF16)
            new = (_dot_tn(hn_v, da_ref[rows, :]), _dot_tn(hn_v, db_ref[rows, :]), _dot_tn(dh_ref[rows, :], hid))
            parts = new if parts is None else tuple(p + q for p, q in zip(parts, new))
        _accumulate(acc1, parts[0], i == 0)
        _accumulate(acc3, parts[1], i == 0)
        _accumulate(acc2, parts[2], i == 0)

        @pl.when(i == ni - 1)
        def _():
            dw1_ref[...] = acc1[...].T.astype(BF16)
            dw3_ref[...] = acc3[...].T.astype(BF16)
            dw2_ref[...] = acc2[...].T.astype(BF16)

    row = pl.BlockSpec((tm, D_MODEL), lambda j, i: (i, 0))
    hid_blk = pl.BlockSpec((tm, tn), lambda j, i: (i, j))
    w_row = pl.BlockSpec((tn, D_MODEL), lambda j, i: (j, 0))
    out = jax.ShapeDtypeStruct((D_FF, D_MODEL), BF16)
    return _pcall(
        body, name=name, grid=(D_FF // tn, ni),
        in_specs=[row, row, hid_blk, hid_blk, hid_blk, hid_blk],
        out_specs=[w_row, w_row, w_row], out_shape=[out, out, out],
        scratch_shapes=[pltpu.VMEM((D_MODEL, tn), F32)] * 3,
        compiler_params=_cp(("arbitrary", "arbitrary"), VMEM_BIG),
    )(hn, dh, a, b, da, db)


def mix_forward(h, norm, wing, tm, name):
    t_rows = h.shape[0]

    def body(h_ref, g_ref, w_ref, hn_ref, p_ref):
        hn = _rms_fwd(h_ref[...], g_ref[...]).astype(BF16)
        hn_ref[...] = hn
        for j in range(N_DEV):
            p_ref[:, j * IN_BLK:(j + 1) * IN_BLK] = _dot(hn, w_ref[j]).astype(BF16)

    row = pl.BlockSpec((tm, D_MODEL), lambda i: (i, 0))
    return _pcall(
        body, name=name, grid=(t_rows // tm,),
        in_specs=[row, pl.BlockSpec((1, D_MODEL), lambda i: (0, 0)),
                  pl.BlockSpec((N_DEV, D_MODEL, IN_BLK), lambda i: (0, 0, 0))],
        out_specs=[row, pl.BlockSpec((tm, IN_WIDTH), lambda i: (i, 0))],
        out_shape=[jax.ShapeDtypeStruct((t_rows, D_MODEL), BF16), jax.ShapeDtypeStruct((t_rows, IN_WIDTH), BF16)],
        compiler_params=_cp(("arbitrary",), VMEM_BIG),
    )(h, norm, wing)


def mix_backward_act(dh, h, norm, dproj, w_in_full, tm, name):
    t_rows = h.shape[0]

    def body(dh_ref, h_ref, g_ref, dp_ref, w_ref, dhin_ref, dg_ref):
        dx, dg = _rms_bwd(h_ref[...], g_ref[...], _dot_nt(dp_ref[...], w_ref[...]))
        dhin_ref[...] = dh_ref[...] + dx
        _accumulate(dg_ref, dg, pl.program_id(0) == 0)

    row = pl.BlockSpec((tm, D_MODEL), lambda i: (i, 0))
    vec = pl.BlockSpec((1, D_MODEL), lambda i: (0, 0))
    return _pcall(
        body, name=name, grid=(t_rows // tm,),
        in_specs=[row, row, vec, pl.BlockSpec((tm, IN_WIDTH), lambda i: (i, 0)), _resident((D_MODEL, IN_WIDTH))],
        out_specs=[row, vec],
        out_shape=[jax.ShapeDtypeStruct((t_rows, D_MODEL), F32), jax.ShapeDtypeStruct((1, D_MODEL), F32)],
        compiler_params=_cp(("arbitrary",), VMEM_BIG),
    )(dh, h, norm, dproj, w_in_full)


def mix_backward_weights(hn, dproj, after, tm, name):
    t_rows = hn.shape[0]
    ni = t_rows // tm
    per_step = 2

    kc = _row_tile(tm, 688)

    def body(hn_ref, dp_ref, _, dw_ref, acc):
        i = pl.program_id(1)
        part = functools.reduce(lambda u, w: u + w, [_dot_tn(hn_ref[r0:r0 + kc, :], dp_ref[r0:r0 + kc, :])
                                                    for r0 in range(0, tm, kc)])
        _accumulate(acc, part, i == 0)

        @pl.when(i == ni - 1)
        def _():
            for k in range(per_step):
                dw_ref[k] = acc[:, k * IN_BLK:(k + 1) * IN_BLK].astype(BF16)

    return _pcall(
        body, name=name, grid=(N_DEV // per_step, ni),
        in_specs=[pl.BlockSpec((tm, D_MODEL), lambda j, i: (i, 0)),
                  pl.BlockSpec((tm, per_step * IN_BLK), lambda j, i: (i, j)), ANY_SPEC],
        out_specs=pl.BlockSpec((per_step, D_MODEL, IN_BLK), lambda j, i: (j, 0, 0)),
        out_shape=jax.ShapeDtypeStruct((N_DEV, D_MODEL, IN_BLK), BF16),
        scratch_shapes=[pltpu.VMEM((D_MODEL, per_step * IN_BLK), F32)],
        compiler_params=_cp(("arbitrary", "arbitrary"), VMEM_BIG),
    )(hn, dproj, after)


GELU_C = 0.7978845608028654
GELU_K = 0.044715


def _gelu(x):
    return 0.5 * x * (1.0 + jnp.tanh(GELU_C * (x + GELU_K * (x * x * x))))


def _gelu_and_grad(x):
    th = jnp.tanh(GELU_C * (x + GELU_K * (x * x * x)))
    val = 0.5 * x * (1.0 + th)
    grad = 0.5 * (1.0 + th) + 0.5 * x * (1.0 - th * th) * (GELU_C * (1.0 + 3.0 * GELU_K * (x * x)))
    return val, grad


def merge_forward(h, yraw, attn, proj, glu_a, glu_b, w_out, tm, name):
    t_rows = h.shape[0]

    def body(h_ref, y_ref, at_ref, gate_ref, a_ref, b_ref, wo_ref, out_ref):
        y = _gelu(y_ref[...]).astype(BF16)
        ssm = _dot(y, a_ref[...]) * _sigmoid(_dot(y, b_ref[...]))
        ga = gate_ref[:, :D_MODEL].astype(F32)
        gs = gate_ref[:, D_MODEL:].astype(F32)
        merged = _sigmoid(ga) * at_ref[...].astype(F32) + _sigmoid(gs) * ssm
        out_ref[...] = h_ref[...] + _dot(merged.astype(BF16), wo_ref[...])

    row = pl.BlockSpec((tm, D_MODEL), lambda i: (i, 0))
    glu = pl.BlockSpec((SSM_WIDTH, D_MODEL), lambda i: (0, 0))
    return _pcall(
        body, name=name, grid=(t_rows // tm,),
        in_specs=[row, pl.BlockSpec((tm, SSM_WIDTH), lambda i: (i, 0)), row,
                  pl.BlockSpec((tm, 2 * D_MODEL), lambda i: (i, 1)), glu, glu,
                  pl.BlockSpec((D_MODEL, D_MODEL), lambda i: (0, 0))],
        out_specs=row, out_shape=jax.ShapeDtypeStruct((t_rows, D_MODEL), F32),
        compiler_params=_cp(("arbitrary",), VMEM_BIG),
    )(h, yraw, attn, proj, glu_a, glu_b, w_out)


def merge_backward(dh, yraw, attn, proj, glu_a, glu_b, w_out, after, tm, name):
    t_rows = dh.shape[0]

    def body(dh_ref, y_ref, at_ref, gate_ref, a_ref, b_ref, wo_ref, _,
             dat_ref, dy_ref, dgate_ref, d16_ref, mg_ref, y16_ref, dya_ref, dyb_ref):
        d16 = dh_ref[...].astype(BF16)
        d16_ref[...] = d16
        gel, dgel = _gelu_and_grad(y_ref[...].astype(F32))
        y16 = gel.astype(BF16)
        y16_ref[...] = y16
        dy = None
        for c0, cw in _col_chunks(D_MODEL):
            cols = slice(c0, c0 + cw)
            gcols = slice(D_MODEL + c0, D_MODEL + c0 + cw)
            dmerged = _dot_nt(d16, wo_ref[cols, :])
            ya = _dot(y16, a_ref[:, cols])
            sb = _sigmoid(_dot(y16, b_ref[:, cols]))
            ssm = ya * sb
            sa = _sigmoid(gate_ref[:, cols].astype(F32))
            ss = _sigmoid(gate_ref[:, gcols].astype(F32))
            attn_v = at_ref[:, cols].astype(F32)
            mg_ref[:, cols] = (sa * attn_v + ss * ssm).astype(BF16)
            dat_ref[:, cols] = (dmerged * sa).astype(BF16)
            dgate_ref[:, cols] = (dmerged * attn_v * sa * (1.0 - sa)).astype(BF16)
            dgate_ref[:, gcols] = (dmerged * ssm * ss * (1.0 - ss)).astype(BF16)
            dssm = dmerged * ss
            dya = (dssm * sb).astype(BF16)
            dyb = (dssm * ya * sb * (1.0 - sb)).astype(BF16)
            dya_ref[:, cols] = dya
            dyb_ref[:, cols] = dyb
            part = _dot_nt(dya, a_ref[:, cols]) + _dot_nt(dyb, b_ref[:, cols])
            dy = part if dy is None else dy + part
        dy_ref[...] = (dy * dgel).astype(BF16)

    row = pl.BlockSpec((tm, D_MODEL), lambda i: (i, 0))
    ssm_row = pl.BlockSpec((tm, SSM_WIDTH), lambda i: (i, 0))
    gates = pl.BlockSpec((tm, 2 * D_MODEL), lambda i: (i, 1))
    wide = jax.ShapeDtypeStruct((t_rows, D_MODEL), BF16)
    narrow = jax.ShapeDtypeStruct((t_rows, SSM_WIDTH), BF16)
    return _pcall(
        body, name=name, grid=(t_rows // tm,),
        in_specs=[row, ssm_row, row, gates, _resident((SSM_WIDTH, D_MODEL)), _resident((SSM_WIDTH, D_MODEL)),
                  _resident((D_MODEL, D_MODEL)), ANY_SPEC],
        out_specs=[row, ssm_row, gates, row, row, ssm_row, row, row],
        out_shape=[wide, narrow, jax.ShapeDtypeStruct((t_rows, IN_WIDTH), BF16), wide, wide, narrow, wide, wide],
        compiler_params=_cp(("arbitrary",), VMEM_BIG),
    )(dh, yraw, attn, proj, glu_a, glu_b, w_out, after)


def merge_backward_weights(d16, merged, y16, dya, dyb, tm, name):
    t_rows = d16.shape[0]

    def body(d_ref, mg_ref, y_ref, dya_ref, dyb_ref, dwo_ref, da_ref, db_ref):
        first = pl.program_id(0) == 0
        y16 = y_ref[...]
        _accumulate(dwo_ref, _dot_tn(mg_ref[...], d_ref[...]), first)
        _accumulate(da_ref, _dot_tn(y16, dya_ref[...]), first)
        _accumulate(db_ref, _dot_tn(y16, dyb_ref[...]), first)

    row = pl.BlockSpec((tm, D_MODEL), lambda i: (i, 0))
    ssm_row = pl.BlockSpec((tm, SSM_WIDTH), lambda i: (i, 0))
    glu = pl.BlockSpec((SSM_WIDTH, D_MODEL), lambda i: (0, 0))
    wo = pl.BlockSpec((D_MODEL, D_MODEL), lambda i: (0, 0))
    return _pcall(
        body, name=name, grid=(t_rows // tm,),
        in_specs=[row, row, ssm_row, row, row], out_specs=[wo, glu, glu],
        out_shape=[jax.ShapeDtypeStruct((D_MODEL, D_MODEL), F32), jax.ShapeDtypeStruct((SSM_WIDTH, D_MODEL), F32),
                   jax.ShapeDtypeStruct((SSM_WIDTH, D_MODEL), F32)],
        compiler_params=_cp(("arbitrary",), VMEM_BIG),
    )(d16, merged, y16, dya, dyb)


def final_loss_backward(h, target, norm, seq, tm, name):
    t_rows = h.shape[0]
    tiles_per_example = (seq + N_META) // tm

    def body(h_ref, t_ref, g_ref, dh_ref, loss_ref, dg_ref):
        i = pl.program_id(0)
        x = h_ref[...]
        g = g_ref[...]
        r = lax.rsqrt(jnp.mean(x * x, axis=-1, keepdims=True) + NORM_EPS)
        xh = x * r
        pos = lax.broadcasted_iota(jnp.int32, (tm, 1), 0) + (i % tiles_per_example) * tm
        diff = jnp.where(pos < seq, xh * g - t_ref[...], 0.0)
        part = 0.5 * jnp.sum(jnp.sum(diff * diff, axis=-1, keepdims=True), axis=0, keepdims=True) / D_MODEL
        dy = diff / D_MODEL
        t = dy * g
        dh_ref[...] = r * (t - xh * jnp.mean(t * xh, axis=-1, keepdims=True))
        _accumulate(loss_ref, jnp.broadcast_to(part, (1, LANES)), i == 0)
        _accumulate(dg_ref, jnp.sum(dy * xh, axis=0, keepdims=True), i == 0)

    row = pl.BlockSpec((tm, D_MODEL), lambda i: (i, 0))
    vec = pl.BlockSpec((1, D_MODEL), lambda i: (0, 0))
    per_example = pl.BlockSpec((None, tm, D_MODEL), lambda i: (i // tiles_per_example, i % tiles_per_example, 0))
    return _pcall(
        body, name=name, grid=(t_rows // tm,),
        in_specs=[row, per_example, vec],
        out_specs=[row, pl.BlockSpec((1, LANES), lambda i: (0, 0)), vec],
        out_shape=[jax.ShapeDtypeStruct((t_rows, D_MODEL), F32), jax.ShapeDtypeStruct((1, LANES), F32),
                   jax.ShapeDtypeStruct((1, D_MODEL), F32)],
        compiler_params=_cp(("arbitrary",), VMEM_BIG),
    )(h, target, norm)


ATTN_SCALE = HEAD_DIM ** -0.5
STACK_HEADS = (0, 2, 1, 3)
META_PAD = LANES - N_META


def _lane_half(shape, hf):
    lane = lax.broadcasted_iota(jnp.int32, shape, 1)
    return (lane < HEAD_DIM) if hf == 0 else (lane >= HEAD_DIM)


def _kv_variants(ref, rows, kh, pad_rows=0):
    tile = kh // 2
    t = ref[rows, tile * LANES:(tile + 1) * LANES].astype(F32)
    swapped = pltpu.roll(t, HEAD_DIM, axis=1)
    at_low, at_high = (t, swapped) if kh % 2 == 0 else (swapped, t)
    lo = jnp.where(_lane_half(t.shape, 0), at_low, 0.0).astype(BF16)
    hi = jnp.where(_lane_half(t.shape, 1), at_high, 0.0).astype(BF16)
    if pad_rows:
        zeros = jnp.zeros((pad_rows, LANES), BF16)
        lo, hi = jnp.concatenate([lo, zeros], axis=0), jnp.concatenate([hi, zeros], axis=0)
    return lo, hi


def _key_tiles(ref, key_rows, kh):
    return [_kv_variants(ref, r, kh, META_PAD if i == len(key_rows) - 1 else 0) for i, r in enumerate(key_rows)]


def _to_kv_lanes(lo, hi, kh):
    lo = jnp.where(_lane_half(lo.shape, 0), lo, 0.0)
    hi = jnp.where(_lane_half(hi.shape, 1), hi, 0.0)
    if kh % 2 == 0:
        return lo + pltpu.roll(hi, HEAD_DIM, axis=1)
    return pltpu.roll(lo, HEAD_DIM, axis=1) + hi


def _stacked(ref, rows, kh):
    col = kh * 2 * LANES
    return jnp.concatenate([ref[rows, col:col + LANES], ref[rows, col + LANES:col + 2 * LANES]], axis=0)


def _sink_column(sink_ref, kh, nq):
    row = lax.broadcasted_iota(jnp.int32, (4 * nq, 1), 0)
    col = jnp.zeros((4 * nq, 1), F32)
    for quarter, g in enumerate(STACK_HEADS):
        col = jnp.where(row // nq == quarter, sink_ref[0, kh * Q_PER_KV + g], col)
    return col


def _softmax_parts(qs, key_tiles, masks, sink):
    scores = []
    for (k_lo, k_hi), mask in zip(key_tiles, masks):
        s = jnp.concatenate([_dot_nt(qs, k_lo), _dot_nt(qs, k_hi)], axis=0) * ATTN_SCALE
        scores.append(s if mask is None else jnp.where(mask, s, NEG_INF))
    m = jnp.maximum(_row_reduce(scores, jnp.maximum, jnp.max), sink)
    probs = [jnp.exp(s - m) for s in scores]
    e_sink = jnp.exp(sink - m)
    den = _row_sums(probs) + e_sink
    return probs, 1.0 / den, e_sink


def _row_reduce(tiles, combine, reduce):
    chunks = [t[:, c:c + LANES] for t in tiles for c in range(0, t.shape[-1], LANES)]
    return reduce(functools.reduce(combine, chunks), axis=-1, keepdims=True)


def _row_sums(tiles):
    return _row_reduce(tiles, lambda u, w: u + w, jnp.sum)


def _band_mask(nq, first):
    keys = BLOCK if first else 2 * BLOCK
    qi = lax.broadcasted_iota(jnp.int32, (4 * nq, keys), 0) % nq
    kj = lax.broadcasted_iota(jnp.int32, (4 * nq, keys), 1)
    if first:
        return kj <= qi
    return jnp.logical_and(kj > qi, kj <= qi + BLOCK)


def _meta_mask(nq, causal):
    qi = lax.broadcasted_iota(jnp.int32, (4 * nq, LANES), 0) % nq
    kj = lax.broadcasted_iota(jnp.int32, (4 * nq, LANES), 1)
    return jnp.logical_and(kj < N_META, kj <= qi) if causal else kj < N_META


def _attention_schedule(seq, queries, carry):
    meta_rows = pl.ds(seq, N_META)
    meta_ok = _meta_mask(BLOCK, False)
    carry = queries(pl.ds(0, BLOCK), BLOCK, [pl.ds(0, BLOCK), meta_rows], [_band_mask(BLOCK, True), meta_ok], carry)

    def block(n, c):
        r0 = pl.multiple_of(n * BLOCK, BLOCK)
        p0 = pl.multiple_of((n - 1) * BLOCK, BLOCK)
        return queries(pl.ds(r0, BLOCK), BLOCK, [pl.ds(p0, 2 * BLOCK), meta_rows], [_band_mask(BLOCK, False), meta_ok], c)

    carry = lax.fori_loop(1, seq // BLOCK, block, carry)
    return queries(meta_rows, N_META, [meta_rows], [_meta_mask(N_META, True)], carry)


def attention_forward(proj3, sinks, seq, name):
    n_b, n_l, _ = proj3.shape

    def body(sink_ref, q_ref, k_ref, v_ref, o_ref):
        def queries(q_rows, nq, key_rows, masks, carry):
            for kh in range(N_KV_HEADS):
                ks = _key_tiles(k_ref, key_rows, kh)
                vs = _key_tiles(v_ref, key_rows, kh)
                qs = _stacked(q_ref, q_rows, kh)
                probs, inv, _ = _softmax_parts(qs, ks, masks, _sink_column(sink_ref, kh, nq))
                probs = [p.astype(BF16) for p in probs]
                o_lo = functools.reduce(lambda u, w: u + w, [_dot(p[:2 * nq], v_lo) for p, (v_lo, _) in zip(probs, vs)])
                o_hi = functools.reduce(lambda u, w: u + w, [_dot(p[2 * nq:], v_hi) for p, (_, v_hi) in zip(probs, vs)])
                out = (o_lo * inv[:2 * nq] + o_hi * inv[2 * nq:]).astype(BF16)
                col = kh * 2 * LANES
                o_ref[q_rows, col:col + LANES] = out[:nq]
                o_ref[q_rows, col + LANES:col + 2 * LANES] = out[nq:]
            return carry

        _attention_schedule(seq, queries, 0)

    return _pcall(
        body, name=name, grid=(n_b,),
        in_specs=[pl.BlockSpec(memory_space=pltpu.SMEM),
                  pl.BlockSpec((None, n_l, D_MODEL), lambda b: (b, 0, 0)),
                  pl.BlockSpec((None, n_l, KV_WIDTH), lambda b: (b, 0, D_MODEL // KV_WIDTH)),
                  pl.BlockSpec((None, n_l, KV_WIDTH), lambda b: (b, 0, D_MODEL // KV_WIDTH + 1))],
        out_specs=pl.BlockSpec((None, n_l, D_MODEL), lambda b: (b, 0, 0)),
        out_shape=jax.ShapeDtypeStruct((n_b, n_l, D_MODEL), BF16),
        compiler_params=_cp(("arbitrary",), VMEM_BIG),
    )(sinks, proj3, proj3, proj3)


def attention_backward(proj3, dattn3, dproj3, sinks, after, seq, name):
    n_b, n_l, _ = proj3.shape
    qkv_width = D_MODEL + 2 * KV_WIDTH

    def body(sink_ref, q_ref, k_ref, v_ref, do_ref, _, __, dqkv_ref, dsink_ref, dk_ref, dv_ref):
        dk_ref[...] = jnp.zeros_like(dk_ref)
        dv_ref[...] = jnp.zeros_like(dv_ref)
        sub = lax.broadcasted_iota(jnp.int32, (SUBLANES, LANES), 0)
        lane = lax.broadcasted_iota(jnp.int32, (SUBLANES, LANES), 1)

        def queries(q_rows, nq, key_rows, masks, dsink):
            for kh in range(N_KV_HEADS):
                ks = _key_tiles(k_ref, key_rows, kh)
                vs = _key_tiles(v_ref, key_rows, kh)
                qs = _stacked(q_ref, q_rows, kh)
                dos = _stacked(do_ref, q_rows, kh)
                probs, inv, e_sink = _softmax_parts(qs, ks, masks, _sink_column(sink_ref, kh, nq))
                probs = [p * inv for p in probs]
                dps = [jnp.concatenate([_dot_nt(dos, v_lo), _dot_nt(dos, v_hi)], axis=0) for v_lo, v_hi in vs]
                delta = _row_sums([p * dp for p, dp in zip(probs, dps)])
                d_sink = -(e_sink * inv) * delta
                for quarter, g in enumerate(STACK_HEADS):
                    d_here = jnp.sum(d_sink[quarter * nq:(quarter + 1) * nq], axis=0, keepdims=True)
                    dsink = dsink + jnp.where(jnp.logical_and(sub == 0, lane == kh * Q_PER_KV + g), d_here, 0.0)
                dq = None
                tile = slice((kh // 2) * LANES, (kh // 2 + 1) * LANES)
                for r, p, dp, (k_lo, k_hi) in zip(key_rows, probs, dps, ks):
                    ds = (p * (dp - delta)).astype(BF16)
                    p16 = p.astype(BF16)
                    dq_x = _dot(ds[:2 * nq], k_lo) + _dot(ds[2 * nq:], k_hi)
                    dq = dq_x if dq is None else dq + dq_x
                    d_k = _to_kv_lanes(_dot_tn(ds[:2 * nq], qs), _dot_tn(ds[2 * nq:], qs), kh) * ATTN_SCALE
                    d_v = _to_kv_lanes(_dot_tn(p16[:2 * nq], dos), _dot_tn(p16[2 * nq:], dos), kh)
                    n_keys = r.size
                    dk_ref[r, tile] += d_k[:n_keys]
                    dv_ref[r, tile] += d_v[:n_keys]
                dq = (dq * ATTN_SCALE).astype(BF16)
                col = kh * 2 * LANES
                dqkv_ref[q_rows, col:col + LANES] = dq[:nq]
                dqkv_ref[q_rows, col + LANES:col + 2 * LANES] = dq[nq:]
            return dsink

        dsink_ref[...] = _attention_schedule(seq, queries, jnp.zeros((SUBLANES, LANES), F32))
        dqkv_ref[:, D_MODEL:D_MODEL + KV_WIDTH] = dk_ref[...].astype(BF16)
        dqkv_ref[:, D_MODEL + KV_WIDTH:] = dv_ref[...].astype(BF16)

    return _pcall(
        body, name=name, grid=(n_b,),
        in_specs=[pl.BlockSpec(memory_space=pltpu.SMEM),
                  pl.BlockSpec((None, n_l, D_MODEL), lambda b: (b, 0, 0)),
                  pl.BlockSpec((None, n_l, KV_WIDTH), lambda b: (b, 0, D_MODEL // KV_WIDTH)),
                  pl.BlockSpec((None, n_l, KV_WIDTH), lambda b: (b, 0, D_MODEL // KV_WIDTH + 1)),
                  pl.BlockSpec((None, n_l, D_MODEL), lambda b: (b, 0, 0)),
                  ANY_SPEC, ANY_SPEC],
        out_specs=[pl.BlockSpec((None, n_l, qkv_width), lambda b: (b, 0, 0)),
                   pl.BlockSpec((None, SUBLANES, LANES), lambda b: (b, 0, 0))],
        out_shape=[jax.ShapeDtypeStruct(dproj3.shape, BF16), jax.ShapeDtypeStruct((n_b, SUBLANES, LANES), F32)],
        scratch_shapes=[pltpu.VMEM((n_l, KV_WIDTH), F32), pltpu.VMEM((n_l, KV_WIDTH), F32)],
        input_output_aliases={5: 0},
        compiler_params=_cp(("arbitrary",), VMEM_BIG),
    )(sinks, proj3, proj3, proj3, dattn3, dproj3, after)


TAB_ROWS = 8
SCAN_UNROLL = 4


def _cmul(ar, ai, br, bi):
    return ar * br - ai * bi, ar * bi + ai * br


def _discretise(ar, ai, ls):
    step = jnp.exp(ls)
    mag = jnp.exp(ar * step)
    ang = ai * step
    cos, sin = jnp.cos(ang), jnp.sin(ang)
    lr, li = mag * cos, mag * sin
    den = ar * ar + ai * ai
    nr, ni = lr - 1.0, li
    cr = (nr * ar + ni * ai) / den
    ci = (ni * ar - nr * ai) / den
    return step, mag, lr, li, den, nr, ni, cr, ci


def _scan_tables(lr, li, reverse):
    n = lr.shape[-1]
    pw = [(lr, li)]
    for _ in range(SUBLANES - 1):
        pw.append(_cmul(pw[-1][0], pw[-1][1], lr, li))
    row = lax.broadcasted_iota(jnp.int32, (SUBLANES, n), 0)
    out = []
    for d in (1, 2, 4):
        ok = (row + d <= SUBLANES - 1) if reverse else (row >= d)
        out += [jnp.where(ok, pw[d - 1][0], 0.0), jnp.where(ok, pw[d - 1][1], 0.0)]
    cr = jnp.zeros((SUBLANES, n), F32)
    ci = jnp.zeros((SUBLANES, n), F32)
    for r in range(SUBLANES):
        e = (SUBLANES - r) if reverse else (r + 1)
        cr = jnp.where(row == r, pw[e - 1][0], cr)
        ci = jnp.where(row == r, pw[e - 1][1], ci)
    return out + [cr, ci]


def ssm_prepare(ar, ai, ls, br_t, bi_t, name):
    def body(ar_ref, ai_ref, ls_ref, br_ref, bi_ref, bbr_ref, bbi_ref, tf_ref, tr_ref):
        _, _, lr, li, _, _, _, cr, ci = _discretise(ar_ref[...], ai_ref[...], ls_ref[...])
        br, bi = br_ref[...], bi_ref[...]
        bbr_ref[...] = cr * br - ci * bi
        bbi_ref[...] = cr * bi + ci * br
        for k, t in enumerate(_scan_tables(lr, li, False)):
            tf_ref[k] = t
        for k, t in enumerate(_scan_tables(lr, -li, True)):
            tr_ref[k] = t

    return _pcall(
        body, name=name,
        out_shape=[jax.ShapeDtypeStruct((SSM_GROUP, N_STATES), F32), jax.ShapeDtypeStruct((SSM_GROUP, N_STATES), F32),
                   jax.ShapeDtypeStruct((TAB_ROWS, SUBLANES, N_STATES), F32),
                   jax.ShapeDtypeStruct((TAB_ROWS, SUBLANES, N_STATES), F32)],
    )(ar, ai, ls, br_t, bi_t)


def ssm_param_backward(ar, ai, ls, br_t, bi_t, dlr_p, dli_p, dbbr, dbbi, group_sum, name):
    def body(ar_ref, ai_ref, ls_ref, br_ref, bi_ref, dlr_ref, dli_ref, dbbr_ref, dbbi_ref, gs_ref,
             dar_ref, dai_ref, dls_ref, dbr_ref, dbi_ref):
        ar, ai = ar_ref[...], ai_ref[...]
        step, mag, lr, li, den, nr, ni, cr, ci = _discretise(ar, ai, ls_ref[...])
        br, bi, dbbr_v, dbbi_v = br_ref[...], bi_ref[...], dbbr_ref[...], dbbi_ref[...]
        dbr_ref[...] = cr * dbbr_v + ci * dbbi_v
        dbi_ref[...] = cr * dbbi_v - ci * dbbr_v
        dcr = jnp.sum(dbbr_v * br + dbbi_v * bi, axis=0, keepdims=True)
        dci = jnp.sum(dbbi_v * br - dbbr_v * bi, axis=0, keepdims=True)
        dnr = (dcr * ar - dci * ai) / den
        dni = (dcr * ai + dci * ar) / den
        dden = -(cr * dcr + ci * dci) / den
        dar = (dcr * nr + dci * ni) / den + dden * 2.0 * ar
        dai = (dcr * ni - dci * nr) / den + dden * 2.0 * ai
        dlr = jnp.sum(dlr_ref[...], axis=0, keepdims=True) + dnr
        dli = jnp.sum(dli_ref[...], axis=0, keepdims=True) + dni
        dmag = (dlr * lr + dli * li) / mag
        dang = dli * lr - dlr * li
        dar_ref[...] = dar + dmag * mag * step
        dai_ref[...] = dai + dang * step
        dstep = dmag * mag * ar + dang * ai
        dls_ref[...] = jnp.dot(dstep * step, gs_ref[...], preferred_element_type=F32, precision=lax.Precision.HIGHEST)

    vec = jax.ShapeDtypeStruct((1, N_STATES), F32)
    mat = jax.ShapeDtypeStruct((SSM_GROUP, N_STATES), F32)
    return _pcall(body, name=name, out_shape=[vec, vec, jax.ShapeDtypeStruct((1, LANES), F32), mat, mat])(
        ar, ai, ls, br_t, bi_t, dlr_p, dli_p, dbbr, dbbi, group_sum)


def _scan_rows(a, b, tabs, carry, reverse):
    for k, d in enumerate((1, 2, 4)):
        shift = SUBLANES - d if reverse else d
        sr, si = pltpu.roll(a, shift, axis=0), pltpu.roll(b, shift, axis=0)
        pr, pi = _cmul(tabs[2 * k], tabs[2 * k + 1], sr, si)
        a, b = a + pr, b + pi
    pr, pi = _cmul(tabs[6], tabs[7], carry[0], carry[1])
    return a + pr, b + pi


def _time_groups(seq, reverse):
    meta = [seq + SUBLANES * g for g in range(N_META // SUBLANES)]
    return meta[::-1] if reverse else meta


def ssm_forward_scan(proj3, b_comb, tabf, c_comb, dvec, seq, name):
    n_b, n_l, _ = proj3.shape
    u_blk = (D_MODEL + 2 * KV_WIDTH) // LANES

    def body(u_ref, b_ref, tab_ref, c_ref, d_ref, x_ref, y_ref, bu, xs):
        j = pl.program_id(1)
        u = u_ref[...]
        bu[...] = _dot(u, b_ref[...])
        tabs = [tab_ref[k] for k in range(TAB_ROWS)]

        def group(r0, carry):
            rows = pl.ds(r0, SUBLANES)
            a, b = _scan_rows(bu[rows, :SCAN_COLS], bu[rows, SCAN_COLS:], tabs, carry, False)
            xs[rows, :SCAN_COLS] = a
            xs[rows, SCAN_COLS:] = b
            return (jnp.broadcast_to(a[SUBLANES - 1:, :], a.shape), jnp.broadcast_to(b[SUBLANES - 1:, :], b.shape))

        zero = jnp.zeros((SUBLANES, SCAN_COLS), F32)
        carry = (zero, zero)
        for r0 in _time_groups(seq, False):
            carry = group(r0, carry)
        span = SCAN_UNROLL * SUBLANES

        def groups(t, c):
            for k in range(SCAN_UNROLL):
                c = group(pl.multiple_of(t * span, span) + k * SUBLANES, c)
            return c

        lax.fori_loop(0, seq // span, groups, carry)
        x16 = xs[...].astype(BF16)
        x_ref[...] = x16
        contrib = _dot(x16, c_ref[...])

        @pl.when(j % 2 == 0)
        def _():
            y_ref[...] = contrib + d_ref[...] * u.astype(F32)

        @pl.when(j % 2 == 1)
        def _():
            y_ref[...] += contrib

    return _pcall(
        body, name=name, grid=(n_b, N_SCAN_BLK),
        in_specs=[pl.BlockSpec((None, n_l, LANES), lambda b, j: (b, 0, u_blk + j // 2)),
                  pl.BlockSpec((None, LANES, 2 * SCAN_COLS), lambda b, j: (j, 0, 0)),
                  pl.BlockSpec((TAB_ROWS, SUBLANES, SCAN_COLS), lambda b, j: (0, 0, j)),
                  pl.BlockSpec((None, 2 * SCAN_COLS, LANES), lambda b, j: (j, 0, 0)),
                  pl.BlockSpec((1, LANES), lambda b, j: (0, j // 2))],
        out_specs=[pl.BlockSpec((None, n_l, 2 * SCAN_COLS), lambda b, j: (b, 0, j)),
                   pl.BlockSpec((None, n_l, LANES), lambda b, j: (b, 0, j // 2))],
        out_shape=[jax.ShapeDtypeStruct((n_b, n_l, 2 * N_STATES), BF16),
                   jax.ShapeDtypeStruct((n_b, n_l, SSM_WIDTH), F32)],
        scratch_shapes=[pltpu.VMEM((n_l, 2 * SCAN_COLS), F32)] * 2,
        compiler_params=_cp(("arbitrary", "arbitrary"), VMEM_BIG),
    )(proj3, b_comb, tabf, c_comb, dvec)


def ssm_backward_scan(dyraw3, xs3, dproj3, c_comb_t, tabr, b_comb_t, dvec, seq, name):
    n_b, n_l, _ = xs3.shape
    u_blk = (D_MODEL + 2 * KV_WIDTH) // LANES

    def body(dy_ref, x_ref, _, c_ref, tab_ref, b_ref, d_ref, du_ref, g_ref, dlr_ref, dli_ref, dx, gs, xs, du_acc):
        j = pl.program_id(1)
        dy = dy_ref[...]
        dx[...] = _dot(dy, c_ref[...])
        xs[...] = x_ref[...].astype(F32)
        tabs = [tab_ref[k] for k in range(TAB_ROWS)]
        last_row = lax.broadcasted_iota(jnp.int32, (SUBLANES, SCAN_COLS), 0) == SUBLANES - 1

        def group(r0, state):
            cr, ci, acc_r, acc_i = state
            rows = pl.ds(r0, SUBLANES)
            a, b = _scan_rows(dx[rows, :SCAN_COLS], dx[rows, SCAN_COLS:], tabs, (cr, ci), True)
            gs[rows, :SCAN_COLS] = a
            gs[rows, SCAN_COLS:] = b
            na = jnp.where(last_row, cr, pltpu.roll(a, SUBLANES - 1, axis=0))
            nb = jnp.where(last_row, ci, pltpu.roll(b, SUBLANES - 1, axis=0))
            xa, xb = xs[rows, :SCAN_COLS], xs[rows, SCAN_COLS:]
            return (jnp.broadcast_to(a[:1, :], a.shape), jnp.broadcast_to(b[:1, :], b.shape),
                    acc_r + na * xa + nb * xb, acc_i + nb * xa - na * xb)

        zero = jnp.zeros((SUBLANES, SCAN_COLS), F32)
        span = SCAN_UNROLL * SUBLANES
        n_spans = seq // span

        def groups(t, s):
            for k in reversed(range(SCAN_UNROLL)):
                s = group(pl.multiple_of((n_spans - 1 - t) * span, span) + k * SUBLANES, s)
            return s

        state = lax.fori_loop(0, n_spans, groups, (zero, zero, zero, zero))
        for r0 in _time_groups(seq, True):
            state = group(r0, state)
        dlr_ref[...] = state[2]
        dli_ref[...] = state[3]
        g16 = gs[...].astype(BF16)
        g_ref[...] = g16
        contrib = _dot(g16, b_ref[...])

        @pl.when(j % 2 == 0)
        def _():
            du_acc[...] = contrib + d_ref[...] * dy.astype(F32)

        @pl.when(j % 2 == 1)
        def _():
            du_ref[...] = (du_acc[...] + contrib).astype(BF16)

    state_blk = pl.BlockSpec((None, n_l, 2 * SCAN_COLS), lambda b, j: (b, 0, j))
    dl_blk = pl.BlockSpec((None, SUBLANES, SCAN_COLS), lambda b, j: (b, 0, j))
    return _pcall(
        body, name=name, grid=(n_b, N_SCAN_BLK),
        in_specs=[pl.BlockSpec((None, n_l, LANES), lambda b, j: (b, 0, j // 2)), state_blk,
                  pl.BlockSpec(memory_space=pl.ANY),
                  pl.BlockSpec((None, LANES, 2 * SCAN_COLS), lambda b, j: (j, 0, 0)),
                  pl.BlockSpec((TAB_ROWS, SUBLANES, SCAN_COLS), lambda b, j: (0, 0, j)),
                  pl.BlockSpec((None, 2 * SCAN_COLS, LANES), lambda b, j: (j, 0, 0)),
                  pl.BlockSpec((1, LANES), lambda b, j: (0, j // 2))],
        out_specs=[pl.BlockSpec((None, n_l, LANES), lambda b, j: (b, 0, u_blk + j // 2)), state_blk, dl_blk, dl_blk],
        out_shape=[jax.ShapeDtypeStruct(dproj3.shape, BF16), jax.ShapeDtypeStruct((n_b, n_l, 2 * N_STATES), BF16),
                   jax.ShapeDtypeStruct((n_b, SUBLANES, N_STATES), F32), jax.ShapeDtypeStruct((n_b, SUBLANES, N_STATES), F32)],
        scratch_shapes=[pltpu.VMEM((n_l, 2 * SCAN_COLS), F32)] * 3 + [pltpu.VMEM((n_l, LANES), F32)],
        input_output_aliases={2: 0},
        compiler_params=_cp(("arbitrary", "arbitrary"), VMEM_BIG),
    )(dyraw3, xs3, dproj3, c_comb_t, tabr, b_comb_t, dvec)


def ssm_param_grads(proj, gs, xs, dyraw, after, tm, name):
    t_rows = proj.shape[0]
    ni = t_rows // tm
    u_blk = (D_MODEL + 2 * KV_WIDTH) // LANES
    width = 2 * SCAN_COLS

    def body(u_ref, g_ref, x_ref, dy_ref, _, db_ref, dc_ref, dd_ref):
        cb, i = pl.program_id(0), pl.program_id(1)
        u, dy = u_ref[...], dy_ref[...]
        _accumulate(db_ref, _dot_tn(u, g_ref[...]), i == 0)
        _accumulate(dc_ref, _dot_tn(x_ref[...], dy), i == 0)

        @pl.when(cb % 2 == 0)
        def _():
            _accumulate(dd_ref, jnp.sum(dy.astype(F32) * u.astype(F32), axis=0, keepdims=True), i == 0)

    return _pcall(
        body, name=name, grid=(N_SCAN_BLK, ni),
        in_specs=[pl.BlockSpec((tm, LANES), lambda cb, i: (i, u_blk + cb // 2)),
                  pl.BlockSpec((tm, width), lambda cb, i: (i, cb)),
                  pl.BlockSpec((tm, width), lambda cb, i: (i, cb)),
                  pl.BlockSpec((tm, LANES), lambda cb, i: (i, cb // 2)), ANY_SPEC],
        out_specs=[pl.BlockSpec((None, LANES, width), lambda cb, i: (cb, 0, 0)),
                   pl.BlockSpec((None, width, LANES), lambda cb, i: (cb, 0, 0)),
                   pl.BlockSpec((1, LANES), lambda cb, i: (0, cb // 2))],
        out_shape=[jax.ShapeDtypeStruct((N_SCAN_BLK, LANES, width), F32),
                   jax.ShapeDtypeStruct((N_SCAN_BLK, width, LANES), F32), jax.ShapeDtypeStruct((1, SSM_WIDTH), F32)],
        compiler_params=_cp(("arbitrary", "arbitrary"), VMEM_BIG),
    )(proj, gs, xs, dyraw, after)


def sum_leading(x, name):
    def body(x_ref, o_ref):
        acc = x_ref[0]
        for k in range(1, x.shape[0]):
            acc = acc + x_ref[k]
        o_ref[...] = acc

    return _pcall(body, name=name, out_shape=jax.ShapeDtypeStruct(x.shape[1:], x.dtype))(x)


WEIGHTS = ['meta_tokens', 'ffn1_norm', 'ffn1_w1', 'ffn1_w3', 'ffn1_w2', 'mix_norm', 'w_in', 'attn_sinks', 'ssm_a_re',
           'ssm_a_im', 'ssm_log_step', 'ssm_b_re', 'ssm_b_im', 'ssm_c_re', 'ssm_c_im', 'ssm_d', 'ssm_glu_a', 'ssm_glu_b',
           'w_out', 'ffn2_norm', 'ffn2_w1', 'ffn2_w3', 'ffn2_w2', 'final_norm']
SHARDED = ['ffn1_w1', 'ffn1_w3', 'ffn1_w2', 'ffn2_w1', 'ffn2_w3', 'ffn2_w2', 'w_in', 'ssm_glu_a', 'ssm_glu_b', 'w_out']
REPLICATED = ['ffn1_norm', 'mix_norm', 'ffn2_norm', 'final_norm', 'attn_sinks', 'ssm_a_re', 'ssm_a_im', 'ssm_log_step',
              'ssm_b_re', 'ssm_b_im', 'ssm_c_re', 'ssm_c_im', 'ssm_d']
PACK_COLS = 1024


def _pack(arrays):
    parts = []
    for a in arrays:
        flat = a.reshape(-1)
        chunk = SUBLANES * PACK_COLS
        padded = -(-flat.shape[0] // chunk) * chunk
        parts.append(jnp.pad(flat, (0, padded - flat.shape[0])).reshape(-1, PACK_COLS))
    return jnp.concatenate(parts, axis=0)


def _unpack(packed, shapes):
    out, row = [], 0
    for shape in shapes:
        size = 1
        for s in shape:
            size *= s
        chunk = SUBLANES * PACK_COLS
        rows = -(-size // chunk) * SUBLANES
        out.append(packed[row:row + rows].reshape(-1)[:size].reshape(shape))
        row += rows
    return out


def kernel(x, meta_tokens, ffn1_norm, ffn1_w1, ffn1_w3, ffn1_w2, mix_norm, w_in, attn_sinks, ssm_a_re, ssm_a_im, ssm_log_step, ssm_b_re, ssm_b_im, ssm_c_re, ssm_c_im, ssm_d, ssm_glu_a, ssm_glu_b, w_out, ffn2_norm, ffn2_w1, ffn2_w3, ffn2_w2, final_norm, loss_target, m_meta_tokens, m_ffn1_norm, m_ffn1_w1, m_ffn1_w3, m_ffn1_w2, m_mix_norm, m_w_in, m_attn_sinks, m_ssm_a_re, m_ssm_a_im, m_ssm_log_step, m_ssm_b_re, m_ssm_b_im, m_ssm_c_re, m_ssm_c_im, m_ssm_d, m_ssm_glu_a, m_ssm_glu_b, m_w_out, m_ffn2_norm, m_ffn2_w1, m_ffn2_w3, m_ffn2_w2, m_final_norm, v_meta_tokens, v_ffn1_norm, v_ffn1_w1, v_ffn1_w3, v_ffn1_w2, v_mix_norm, v_w_in, v_attn_sinks, v_ssm_a_re, v_ssm_a_im, v_ssm_log_step, v_ssm_b_re, v_ssm_b_im, v_ssm_c_re, v_ssm_c_im, v_ssm_d, v_ssm_glu_a, v_ssm_glu_b, v_w_out, v_ffn2_norm, v_ffn2_w1, v_ffn2_w3, v_ffn2_w2, v_final_norm):
    given = dict(locals())
    w = {n: given[n] for n in WEIGHTS}
    m = {n: given["m_" + n] for n in WEIGHTS}
    v = {n: given["v_" + n] for n in WEIGHTS}

    n_b, seq, _ = x.shape
    n_l = seq + N_META
    t_rows = n_b * n_l
    tm = _row_tile(n_l, 688)
    px, py, pc = _my_place()
    me = 4 * px + 2 * py + pc

    glu = jnp.stack([ssm_glu_a[0], ssm_glu_b[0]]).astype(BF16)
    ffn_names = ['ffn1_w1', 'ffn1_w3', 'ffn1_w2', 'ffn2_w1', 'ffn2_w3', 'ffn2_w2']

    def hidden_on_rows(n, t):
        return t[0] if n.endswith('w2') else t[0].T

    def hidden_on_rows_back(n, t):
        return t[None] if n.endswith('w2') else t.T[None]

    me_idx = jnp.reshape(me, (1,)).astype(jnp.int32)
    first_names, later_names = ffn_names[:3], ffn_names[3:]
    *first, metag = all_gather_list(
        [hidden_on_rows(n, w[n]).astype(BF16) for n in first_names] + [meta_tokens], meta_tokens, "ag_first")
    win_send, win_recv, win_shard, win_land, win_token = exchange_start(
        [w_in[0].astype(BF16)], first[0], True, "ag_w_in_start")
    later_shards = [hidden_on_rows(n, w[n]).astype(BF16) for n in later_names] + [glu, w_out[0].astype(BF16)]
    ag_send, ag_recv, later_shards, later_lands, ag_token = exchange_start(later_shards, win_token, True, "ag_later_start")
    full = {n: g.reshape(D_FF, D_MODEL) for n, g in zip(first_names, first)}
    meta_full = metag.transpose(1, 0, 2).reshape(N_META, D_MODEL)

    final_g = final_norm.reshape(1, D_MODEL)

    ar = ssm_a_re.reshape(1, N_STATES)
    ai = ssm_a_im.reshape(1, N_STATES)
    ls = jnp.repeat(ssm_log_step.reshape(SSM_GROUPS), SSM_STATE).reshape(1, N_STATES)
    br_t = ssm_b_re[0].transpose(2, 0, 1).reshape(SSM_GROUP, N_STATES)
    bi_t = ssm_b_im[0].transpose(2, 0, 1).reshape(SSM_GROUP, N_STATES)
    bbr, bbi, tabf, tabr = ssm_prepare(ar, ai, ls, br_t, bi_t, "ssm_prepare")
    bbr_g = bbr.reshape(SSM_GROUP, SSM_GROUPS, SSM_STATE).transpose(1, 0, 2)
    bbi_g = bbi.reshape(SSM_GROUP, SSM_GROUPS, SSM_STATE).transpose(1, 0, 2)
    groups_per_blk = SCAN_COLS // SSM_STATE
    half = ((jnp.arange(N_SCAN_BLK) % 2)[:, None] == jnp.arange(2)[None, :]).astype(F32)
    eye = jnp.eye(groups_per_blk, dtype=F32)

    def scan_blocks(re_g, im_g):
        def one(t):
            t = t.reshape(N_SCAN_BLK, groups_per_blk, SSM_GROUP, SSM_STATE)
            t = t[:, :, :, None, :] * eye[None, :, None, :, None]
            t = t.reshape(N_SCAN_BLK, LANES // 2, SCAN_COLS)
            return (t[:, None] * half[:, :, None, None]).reshape(N_SCAN_BLK, LANES, SCAN_COLS)
        return jnp.concatenate([one(re_g), one(im_g)], axis=-1).astype(BF16)

    b_comb = scan_blocks(bbr_g, bbi_g)
    c_comb_t = scan_blocks(ssm_c_re[0], -ssm_c_im[0])
    b_comb_t, c_comb = b_comb.transpose(0, 2, 1), c_comb_t.transpose(0, 2, 1)

    ffn1_w = (full['ffn1_w1'], full['ffn1_w3'], full['ffn1_w2'])
    h1, hn1, a1, b1, h0 = ffn_forward(x, ffn1_norm, *ffn1_w, ag_token, tm, "ffn1_fwd", meta=meta_full)
    win_shard, (wing,) = exchange_wait(win_send, win_recv, win_shard, win_land, h1, True, "ag_w_in_wait")
    wing = lax.dynamic_update_slice_in_dim(wing, win_shard[0][None], me, axis=0)
    hnm, proj = mix_forward(h1, mix_norm, wing, tm, "mix_fwd")
    proj3 = proj.reshape(n_b, n_l, IN_WIDTH)
    attn3 = attention_forward(proj3, attn_sinks, seq, "attn_fwd")
    attn = attn3.reshape(t_rows, D_MODEL)
    xs3, yraw3 = ssm_forward_scan(proj3, b_comb, tabf, c_comb, ssm_d, seq, "ssm_fwd")
    yraw = yraw3.reshape(t_rows, SSM_WIDTH)
    later_shards, later = exchange_wait(ag_send, ag_recv, later_shards, later_lands, yraw3, True, "ag_later_wait")
    later = [lax.dynamic_update_slice_in_dim(z, s[None], me, axis=0) for z, s in zip(later, later_shards)]
    for n, g in zip(later_names, later):
        full[n] = g.reshape(D_FF, D_MODEL)
    ffn2_w = (full['ffn2_w1'], full['ffn2_w3'], full['ffn2_w2'])
    glug, wog = later[len(later_names):]
    glu_a = glug[:, 0].transpose(1, 0, 2).reshape(SSM_WIDTH, D_MODEL)
    glu_b = glug[:, 1].transpose(1, 0, 2).reshape(SSM_WIDTH, D_MODEL)
    w_out_full = wog.reshape(D_MODEL, D_MODEL)
    h2 = merge_forward(h1, yraw, attn, proj, glu_a, glu_b, w_out_full, tm, "merge_fwd")
    h3, hn2, a2, b2 = ffn_forward(h2, ffn2_norm, *ffn2_w, ag_token, tm, "ffn2_fwd")
    dh3, loss_part, g_final = final_loss_backward(h3, loss_target, final_g, seq, tm, "loss_bwd")
    loss = lax.psum(loss_part[0, 0], ("x", "y", "c"))

    def blocked_ffn(d_w1t, d_w3t, d_w2):
        return tuple(t.reshape(N_DEV, FF_BLK, D_MODEL) for t in (d_w1t, d_w3t, d_w2))

    def blocked_cols(full_grad):
        r = full_grad.shape[0]
        return full_grad.reshape(r, N_DEV, full_grad.shape[1] // N_DEV).transpose(1, 0, 2).astype(BF16)

    early = {}

    def start_reduce(names, tag):
        srcs = [dw[n] for n in names]
        send, recv, srcs, lands, token = exchange_start(srcs, srcs[0], False, "rs_" + tag + "_start")
        early[tag] = (names, send, recv, srcs, lands)
        return token

    dw = {}
    da2, db2, dh3_half = ffn_backward_hidden(dh3, a2, b2, ffn2_w[2], g_final, tm, "ffn2_bwd_hid")
    dw['ffn2_w1'], dw['ffn2_w3'], dw['ffn2_w2'] = blocked_ffn(
        *ffn_backward_weights(hn2, dh3_half, a2, b2, da2, db2, n_l, FF_BWD_COLS, "ffn2_bwd_w"))
    token = start_reduce(later_names, "ffn2")
    dh2, g_ffn2_norm = ffn_backward_input(dh3, h2, ffn2_norm, da2, db2, ffn2_w[0], ffn2_w[1], token, tm, "ffn2_bwd_in")
    dattn, dyraw, dproj, *for_weights = merge_backward(dh2, yraw, attn, proj, glu_a, glu_b, w_out_full, token, tm,
                                                       "merge_bwd")
    d_wo, d_ga, d_gb = merge_backward_weights(*for_weights, tm, "merge_bwd_w")
    dw['ssm_glu_a'] = blocked_cols(d_ga)
    dw['ssm_glu_b'] = blocked_cols(d_gb)
    dw['w_out'] = d_wo.reshape(N_DEV, D_MODEL // N_DEV, D_MODEL).astype(BF16)
    token = start_reduce(['ssm_glu_a', 'ssm_glu_b', 'w_out'], "mix")
    dproj3 = dproj.reshape(n_b, n_l, IN_WIDTH)
    dproj3, dsink_p = attention_backward(proj3, dattn.reshape(n_b, n_l, D_MODEL), dproj3, attn_sinks, token, seq,
                                         "attn_bwd")
    dproj3, gs3, dlr_p, dli_p = ssm_backward_scan(
        dyraw.reshape(n_b, n_l, SSM_WIDTH), xs3, dproj3, c_comb_t, tabr, b_comb_t, ssm_d, seq, "ssm_bwd")
    dproj = dproj3.reshape(t_rows, IN_WIDTH)
    w_in_full = wing.transpose(1, 0, 2).reshape(D_MODEL, IN_WIDTH)
    dh1, g_mix_norm = mix_backward_act(dh2, h1, mix_norm, dproj, w_in_full, tm, "mix_bwd_act")
    da1, db1, dh1_half = ffn_backward_hidden(dh1, a1, b1, ffn1_w[2], token, tm, "ffn1_bwd_hid")
    dw['ffn1_w1'], dw['ffn1_w3'], dw['ffn1_w2'] = blocked_ffn(
        *ffn_backward_weights(hn1, dh1_half, a1, b1, da1, db1, n_l, FF_BWD_COLS, "ffn1_bwd_w"))
    token = start_reduce(first_names, "ffn1")
    grad_x, meta_rows_grad, g_ffn1_norm = ffn_backward_input(
        dh1, h0, ffn1_norm, da1, db1, ffn1_w[0], ffn1_w[1], token, tm, "ffn1_bwd_in", examples=(n_b, seq))
    g_meta = sum_leading(meta_rows_grad, "meta_sum")

    dw['w_in'] = mix_backward_weights(hnm, dproj, g_meta, n_l, "mix_bwd_w")
    token = start_reduce(['w_in'], "w_in")
    d_bd, d_cd, g_d = ssm_param_grads(proj, gs3.reshape(t_rows, 2 * N_STATES), xs3.reshape(t_rows, 2 * N_STATES),
                                      dyraw, token, n_l, "ssm_bwd_w")

    def group_blocks(part, channels_first):
        if channels_first:
            t = jnp.sum(part.reshape(N_SCAN_BLK, 2, LANES // 2, SCAN_COLS) * half[:, :, None, None], axis=1)
            t = t.reshape(N_SCAN_BLK, groups_per_blk, SSM_GROUP, groups_per_blk, SSM_STATE)
            t = jnp.sum(t * eye[None, :, None, :, None], axis=3)
            return t.reshape(SSM_GROUPS, SSM_GROUP, SSM_STATE)
        t = jnp.sum(part.reshape(N_SCAN_BLK, SCAN_COLS, 2, LANES // 2) * half[:, None, :, None], axis=2)
        t = t.reshape(N_SCAN_BLK, groups_per_blk, SSM_STATE, groups_per_blk, SSM_GROUP)
        t = jnp.sum(t * eye[None, :, None, :, None], axis=3)
        return t.reshape(SSM_GROUPS, SSM_STATE, SSM_GROUP).transpose(0, 2, 1)

    dbbr = group_blocks(d_bd[:, :, :SCAN_COLS], True).transpose(1, 0, 2).reshape(SSM_GROUP, N_STATES)
    dbbi = group_blocks(d_bd[:, :, SCAN_COLS:], True).transpose(1, 0, 2).reshape(SSM_GROUP, N_STATES)
    g_c_re = group_blocks(d_cd[:, :SCAN_COLS, :], False)[None]
    g_c_im = -group_blocks(d_cd[:, SCAN_COLS:, :], False)[None]
    group_sum = (jnp.arange(N_STATES)[:, None] // SSM_STATE == jnp.arange(LANES)[None, :]).astype(F32)
    g_ar, g_ai, g_ls, g_br, g_bi = ssm_param_backward(
        ar, ai, ls, br_t, bi_t, dlr_p.reshape(n_b * SUBLANES, N_STATES), dli_p.reshape(n_b * SUBLANES, N_STATES),
        dbbr, dbbi, group_sum, "ssm_bwd_params")
    g_sinks = sum_leading(dsink_p, "sink_sum")[0:1, :N_KV_HEADS * Q_PER_KV]

    small = {
        'ffn1_norm': g_ffn1_norm, 'mix_norm': g_mix_norm, 'ffn2_norm': g_ffn2_norm, 'final_norm': g_final.reshape(D_MODEL),
        'attn_sinks': g_sinks, 'ssm_a_re': g_ar.reshape(1, SSM_GROUPS, SSM_STATE), 'ssm_a_im': g_ai.reshape(1, SSM_GROUPS, SSM_STATE),
        'ssm_log_step': g_ls[:, :SSM_GROUPS],
        'ssm_b_re': g_br.reshape(SSM_GROUP, SSM_GROUPS, SSM_STATE).transpose(1, 2, 0)[None],
        'ssm_b_im': g_bi.reshape(SSM_GROUP, SSM_GROUPS, SSM_STATE).transpose(1, 2, 0)[None],
        'ssm_c_re': g_c_re, 'ssm_c_im': g_c_im, 'ssm_d': g_d,
    }
    sg_send, sg_recv, sg_src, sg_land, token = exchange_start(
        [_pack([small[n] for n in REPLICATED] + [g_meta])], token, True, "ag_small_start")

    grads, deltas, new_m, new_v = {}, {}, {}, {}

    def views(n):
        if n in ffn_names:
            return functools.partial(hidden_on_rows, n), functools.partial(hidden_on_rows_back, n)
        return (lambda t: t[0]), (lambda t: t[None])

    def finish_reduce(tag, previous):
        names, send, recv, srcs, lands = early[tag]
        srcs, lands = exchange_wait(send, recv, srcs, lands, previous, False, "rs_" + tag + "_wait")
        for n, g, land in zip(names, srcs, lands):
            two_d, back = views(n)
            out = adamw_exchanged(me_idx, g, land, two_d(w[n]), two_d(m[n]), two_d(v[n]), "adamw_" + n)
            grads[n], deltas[n], new_m[n], new_v[n] = (back(o) for o in out)
            previous = out[1]
        return previous

    previous = token
    for tag in ("ffn2", "mix"):
        previous = finish_reduce(tag, previous)

    zeros_meta = jnp.zeros((N_META, D_MODEL), F32)
    sg_src, (small_parts,) = exchange_wait(sg_send, sg_recv, sg_src, sg_land, previous, True, "ag_small_wait")
    small_parts = lax.dynamic_update_slice_in_dim(small_parts, sg_src[0][None], me, axis=0)

    def small_update(parts, names, extra, tag):
        pack_of = lambda d: _pack([d[n] for n in names] + extra)
        packed = adamw_small(parts, pack_of(w), pack_of(m), pack_of(v), "adamw_small_" + tag)
        unpacked = [_unpack(p, [w[n].shape for n in names] + [e.shape for e in extra]) for p in packed]
        for k, n in enumerate(names):
            grads[n], deltas[n], new_m[n], new_v[n] = (u[k] for u in unpacked)
        return packed, unpacked

    packed_out, unpacked = small_update(small_parts, REPLICATED, [zeros_meta], "all")
    g_meta_full = unpacked[0][-1]
    grads['meta_tokens'] = lax.dynamic_index_in_dim(
        g_meta_full.reshape(N_META, N_DEV, D_MODEL // N_DEV), me, axis=1, keepdims=False)
    deltas['meta_tokens'], new_m['meta_tokens'], new_v['meta_tokens'] = adamw_plain(
        grads['meta_tokens'], w['meta_tokens'], m['meta_tokens'], v['meta_tokens'], "adamw_meta")

    finish_reduce("w_in", finish_reduce("ffn1", packed_out[0]))

    return (loss, grad_x, *[grads[n] for n in WEIGHTS], *[deltas[n] for n in WEIGHTS],
            *[new_m[n] for n in WEIGHTS], *[new_v[n] for n in WEIGHTS])
```

```python
import functools

import jax
import jax.numpy as jnp
from jax import lax
from jax.experimental import pallas as pl
from jax.experimental.pallas import tpu as pltpu

F32 = jnp.float32
BF16 = jnp.bfloat16
MESH = pl.DeviceIdType.MESH

N_DEV = 8
D_MODEL = 1024
N_META = 16
HEAD_DIM = 64
N_KV_HEADS = 4
Q_PER_KV = 4
BLOCK = 128
KV_WIDTH = N_KV_HEADS * HEAD_DIM
SSM_GROUP = 16
SSM_WIDTH = 512
SSM_GROUPS = 32
SSM_STATE = 64
N_STATES = SSM_GROUPS * SSM_STATE
D_FF = 2816
FF_BLK = D_FF // N_DEV
IN_WIDTH = 4096
IN_BLK = IN_WIDTH // N_DEV
NORM_EPS = 1e-6
NEG_INF = -1e30
SCAN_COLS = 256
N_SCAN_BLK = N_STATES // SCAN_COLS
SUBLANES = 8
LANES = 128
MXU_WIDTH = 256
FF_BWD_COLS = MXU_WIDTH

ADAM_LR = 0.001
ADAM_B1 = 0.9
ADAM_B2 = 0.999
ADAM_EPS = 1e-08
ADAM_WD = 0.01
ADAM_STEP = 10

VMEM_BIG = 56 * 1024 * 1024


def _cp(sem=None, vmem=None):
    kw = {}
    if sem is not None:
        kw["dimension_semantics"] = sem
    if vmem is not None:
        kw["vmem_limit_bytes"] = vmem
    return pltpu.CompilerParams(**kw)


def _pcall(body, **kw):
    return pl.pallas_call(body, **kw)


def _dot(a, b):
    return jnp.dot(a, b, preferred_element_type=F32)


def _dot_nt(a, b):
    return lax.dot_general(a, b, (((1,), (1,)), ((), ())), preferred_element_type=F32)


def _dot_tn(a, b):
    return lax.dot_general(a, b, (((0,), (0,)), ((), ())), preferred_element_type=F32)


def _sigmoid(x):
    return 0.5 + 0.5 * jnp.tanh(0.5 * x)


def _row_tile(rows, cap):
    best = None
    for t in range(16, min(rows, cap) + 1, 16):
        if rows % t == 0:
            best = t
    assert best is not None, rows
    return best


def _my_place():
    return lax.axis_index("x"), lax.axis_index("y"), lax.axis_index("c")


def all_gather_list(shards, after, name):
    n = len(shards)

    def body(*refs):
        ins, outs = refs[:n], refs[n + 1:2 * n + 1]
        send_sems, recv_sems, local_sems = refs[2 * n + 1:]
        x, y, c = _my_place()
        me, sibling = (x, y, c), (x, y, 1 - c)
        chips = [(1 - x, y), (x, 1 - y), (1 - x, 1 - y)]

        def blk(a, px, py, pc):
            return outs[a].at[4 * px + 2 * py + pc]

        def copy(a, k, block, to, src=None):
            return pltpu.make_async_remote_copy(
                src_ref=blk(a, *block) if src is None else src, dst_ref=blk(a, *block),
                send_sem=send_sems.at[a * 7 + k], recv_sem=recv_sems.at[a * 7 + k],
                device_id=to, device_id_type=MESH)

        mine = [pltpu.make_async_copy(ins[a], blk(a, *me), local_sems.at[a]) for a in range(n)]
        for cp in mine:
            cp.start()
        first = []
        for a in range(n):
            first.append(copy(a, 0, me, sibling, src=ins[a]))
            first += [copy(a, 1 + j, me, (*chip, c), src=ins[a]) for j, chip in enumerate(chips)]
        for cp in first:
            cp.start()
        passed = []
        for j, chip in enumerate(chips):
            for a in range(n):
                copy(a, 1 + j, (*chip, c), me).wait_recv()
                cp = copy(a, 4 + j, (*chip, c), sibling)
                cp.start()
                passed.append(cp)
        for a in range(n):
            copy(a, 0, sibling, me).wait_recv()
            for j, chip in enumerate(chips):
                copy(a, 4 + j, (*chip, 1 - c), me).wait_recv()
        for cp in first + passed:
            cp.wait_send()
        for cp in mine:
            cp.wait()

    any_spec = pl.BlockSpec(memory_space=pl.ANY)
    return _pcall(
        body, name=name,
        out_shape=[jax.ShapeDtypeStruct((N_DEV,) + s.shape, s.dtype) for s in shards],
        in_specs=[any_spec] * (n + 1), out_specs=[any_spec] * n,
        scratch_shapes=[pltpu.SemaphoreType.DMA((7 * n,)), pltpu.SemaphoreType.DMA((7 * n,)),
                        pltpu.SemaphoreType.DMA((n,))],
    )(*shards, after)


HBM_SPEC = pl.BlockSpec(memory_space=pltpu.HBM)
SEM_SPEC = pl.BlockSpec(memory_space=pltpu.SEMAPHORE)
N_PEERS = N_DEV - 1


def _related(k):
    x, y, c = _my_place()
    px = 1 - x if k & 4 else x
    py = 1 - y if k & 2 else y
    pc = 1 - c if k & 1 else c
    return (px, py, pc), 4 * px + 2 * py + pc


def _exchange_copies(srcs, lands, send_sems, recv_sems, gather):
    x, y, c = _my_place()
    me = 4 * x + 2 * y + c
    copies = []
    for a, (src, land) in enumerate(zip(srcs, lands)):
        for k in range(1, N_DEV):
            peer, d = _related(k)
            copies.append(pltpu.make_async_remote_copy(
                src_ref=src if gather else src.at[d], dst_ref=land.at[me] if gather else land.at[k],
                send_sem=send_sems.at[a * N_PEERS + k - 1], recv_sem=recv_sems.at[a * N_PEERS + k - 1],
                device_id=peer, device_id_type=MESH))
    return copies


def exchange_start(srcs, after, gather, name):
    n = len(srcs)
    land_shapes = [((N_DEV,) + s.shape) if gather else s.shape for s in srcs]

    def body(*refs):
        send_sems, recv_sems = refs[2 * n + 1], refs[2 * n + 2]
        for cp in _exchange_copies(refs[:n], refs[n:2 * n], send_sems, recv_sems, gather):
            cp.start()
        token = refs[-1]
        token[...] = jnp.zeros_like(token)

    sems = pltpu.SemaphoreType.DMA((n * N_PEERS,))
    lands = [pltpu.with_memory_space_constraint(lax.empty(shape, s.dtype), pltpu.HBM) for shape, s in zip(land_shapes, srcs)]
    out = _pcall(
        body, name=name,
        out_shape=(sems, sems, *[pltpu.HBM(s.shape, s.dtype) for s in srcs],
                   *[pltpu.HBM(shape, s.dtype) for shape, s in zip(land_shapes, srcs)],
                   jax.ShapeDtypeStruct((SUBLANES, LANES), F32)),
        in_specs=[HBM_SPEC] * (2 * n) + [pl.BlockSpec(memory_space=pl.ANY)],
        out_specs=(SEM_SPEC, SEM_SPEC, *[HBM_SPEC] * (2 * n), pl.BlockSpec(memory_space=pltpu.VMEM)),
        input_output_aliases={i: 2 + i for i in range(2 * n)},
        compiler_params=pltpu.CompilerParams(has_side_effects=pltpu.SideEffectType.DATAFLOW_SIDE_EFFECTING),
    )(*[pltpu.with_memory_space_constraint(s, pltpu.HBM) for s in srcs], *lands, after)
    return out[0], out[1], list(out[2:2 + n]), list(out[2 + n:2 + 2 * n]), out[-1]


def exchange_wait(send_sems, recv_sems, srcs, lands, after, gather, name):
    n = len(srcs)

    def body(*refs):
        for cp in _exchange_copies(refs[:n], refs[n:2 * n], refs[2 * n], refs[2 * n + 1], gather):
            cp.wait_send()
            cp.wait_recv()

    out = _pcall(
        body, name=name,
        out_shape=(*[pltpu.HBM(s.shape, s.dtype) for s in srcs], *[pltpu.HBM(z.shape, z.dtype) for z in lands]),
        in_specs=[HBM_SPEC] * (2 * n) + [SEM_SPEC, SEM_SPEC, pl.BlockSpec(memory_space=pl.ANY)],
        out_specs=tuple([HBM_SPEC] * (2 * n)),
        input_output_aliases={i: i for i in range(2 * n)},
        compiler_params=pltpu.CompilerParams(has_side_effects=pltpu.SideEffectType.DATAFLOW_SIDE_EFFECTING),
    )(*srcs, *lands, send_sems, recv_sems, after)
    return list(out[:n]), list(out[n:])


def adamw_exchanged(me, g, land, w, m, v, name):
    rows, cols = w.shape
    tr = _row_tile(rows, 256)

    def body(me_ref, g_ref, land_ref, w_ref, m_ref, v_ref, go_ref, d_ref, mo_ref, vo_ref):
        grad = g_ref[...].astype(F32)
        for k in range(1, N_DEV):
            grad = grad + land_ref[k].astype(F32)
        delta, m_new, v_new = _adam_math(w_ref[...], grad, m_ref[...], v_ref[...])
        go_ref[...] = grad
        d_ref[...] = delta
        mo_ref[...] = m_new
        vo_ref[...] = v_new

    tile = pl.BlockSpec((tr, cols), lambda r, ix: (r, 0))
    out = jax.ShapeDtypeStruct((rows, cols), F32)
    return _pcall(
        body, name=name, out_shape=[out] * 4,
        grid_spec=pltpu.PrefetchScalarGridSpec(
            num_scalar_prefetch=1, grid=(rows // tr,),
            in_specs=[pl.BlockSpec((None, tr, cols), lambda r, ix: (ix[0], r, 0)),
                      pl.BlockSpec((N_DEV, tr, cols), lambda r, ix: (0, r, 0)), tile, tile, tile],
            out_specs=[tile] * 4),
        compiler_params=_cp(("arbitrary",)),
    )(me, g, land, w, m, v)


def _adam_math(w, g, m, v):
    m = ADAM_B1 * m + (1.0 - ADAM_B1) * g
    v = ADAM_B2 * v + (1.0 - ADAM_B2) * (g * g)
    m_hat = m / (1.0 - ADAM_B1 ** ADAM_STEP)
    v_hat = v / (1.0 - ADAM_B2 ** ADAM_STEP)
    delta = -ADAM_LR * (m_hat / (jnp.sqrt(v_hat) + ADAM_EPS) + ADAM_WD * w)
    return delta, m, v


def adamw_small(parts, w, m, v, name):
    _, rows, cols = parts.shape

    def body(p_ref, w_ref, m_ref, v_ref, go_ref, d_ref, mo_ref, vo_ref):
        grad = p_ref[0]
        for k in range(1, N_DEV):
            grad = grad + p_ref[k]
        delta, m_new, v_new = _adam_math(w_ref[...], grad, m_ref[...], v_ref[...])
        go_ref[...] = grad
        d_ref[...] = delta
        mo_ref[...] = m_new
        vo_ref[...] = v_new

    out = jax.ShapeDtypeStruct((rows, cols), F32)
    return _pcall(body, name=name, out_shape=[out] * 4, compiler_params=_cp(vmem=VMEM_BIG))(parts, w, m, v)


def adamw_plain(g, w, m, v, name):
    def body(g_ref, w_ref, m_ref, v_ref, d_ref, mo_ref, vo_ref):
        delta, m_new, v_new = _adam_math(w_ref[...], g_ref[...], m_ref[...], v_ref[...])
        d_ref[...] = delta
        mo_ref[...] = m_new
        vo_ref[...] = v_new

    out = jax.ShapeDtypeStruct(w.shape, F32)
    return _pcall(body, name=name, out_shape=[out] * 3)(g, w, m, v)


def _rms_fwd(x, g):
    r = lax.rsqrt(jnp.mean(x * x, axis=-1, keepdims=True) + NORM_EPS)
    return x * r * g


def _rms_bwd(x, g, dy):
    r = lax.rsqrt(jnp.mean(x * x, axis=-1, keepdims=True) + NORM_EPS)
    xh = x * r
    t = dy * g
    dx = r * (t - xh * jnp.mean(t * xh, axis=-1, keepdims=True))
    return dx, jnp.sum(dy * xh, axis=0, keepdims=True)


def _accumulate(ref, val, first):
    @pl.when(first)
    def _():
        ref[...] = val

    @pl.when(jnp.logical_not(first))
    def _():
        ref[...] += val


def _col_chunks(width):
    return [(c0, min(MXU_WIDTH, width - c0)) for c0 in range(0, width, MXU_WIDTH)]


ANY_SPEC = pl.BlockSpec(memory_space=pl.ANY)


def ffn_forward(h, norm, w1, w3, w2, after, tm, name, meta=None):
    if meta is None:
        t_rows = h.shape[0]
        h_spec = pl.BlockSpec((tm, D_MODEL), lambda i: (i, 0))
    else:
        tiles = (h.shape[1] + N_META) // tm
        t_rows = h.shape[0] * tiles * tm
        h_spec = pl.BlockSpec((None, tm, D_MODEL), lambda i: (i // tiles, i % tiles, 0))

    def body(h_ref, g_ref, w1_ref, w3_ref, w2_ref, _, *rest):
        if meta is None:
            out_ref, hn_ref, a_ref, b_ref, hid_ref = rest
            h_in = h_ref[...]
        else:
            meta_ref, out_ref, hn_ref, a_ref, b_ref, h0_ref, hid_ref = rest
            h_in = h_ref[...]
            with_meta = jnp.concatenate([h_in[:tm - N_META], meta_ref[...]], axis=0)
            h_in = jnp.where(pl.program_id(0) % tiles == tiles - 1, with_meta, h_in)
            h0_ref[...] = h_in
        hn = _rms_fwd(h_in, g_ref[...]).astype(BF16)
        hn_ref[...] = hn
        for c0, cw in _col_chunks(D_FF):
            a = _dot_nt(hn, w1_ref[c0:c0 + cw, :])
            b = _dot_nt(hn, w3_ref[c0:c0 + cw, :])
            a_ref[:, c0:c0 + cw] = a.astype(BF16)
            b_ref[:, c0:c0 + cw] = b.astype(BF16)
            hid_ref[:, c0:c0 + cw] = (a * _sigmoid(a) * b).astype(BF16)
        out_ref[...] = h_in + 0.5 * _dot(hid_ref[...], w2_ref[...])

    row = pl.BlockSpec((tm, D_MODEL), lambda i: (i, 0))
    hid_blk = pl.BlockSpec((tm, D_FF), lambda i: (i, 0))
    weight = _resident((D_FF, D_MODEL))
    wide = jax.ShapeDtypeStruct((t_rows, D_MODEL), F32)
    extra_in = [] if meta is None else [meta]
    return _pcall(
        body, name=name, grid=(t_rows // tm,),
        in_specs=[h_spec, pl.BlockSpec((1, D_MODEL), lambda i: (0, 0)), weight, weight, weight, ANY_SPEC]
        + [pl.BlockSpec((N_META, D_MODEL), lambda i: (0, 0))] * len(extra_in),
        out_specs=[row, row, hid_blk, hid_blk] + [row] * len(extra_in),
        out_shape=[wide, jax.ShapeDtypeStruct((t_rows, D_MODEL), BF16),
                   jax.ShapeDtypeStruct((t_rows, D_FF), BF16), jax.ShapeDtypeStruct((t_rows, D_FF), BF16)]
        + [wide] * len(extra_in),
        scratch_shapes=[pltpu.VMEM((tm, D_FF), BF16)],
        compiler_params=_cp(("arbitrary",), VMEM_BIG),
    )(h, norm, w1, w3, w2, after, *extra_in)


def _resident(shape):
    return pl.BlockSpec(shape, lambda *_: (0,) * len(shape), pipeline_mode=pl.Buffered(1))


def ffn_backward_hidden(dh, a, b, w2, after, tm, name):
    t_rows = dh.shape[0]

    def body(dh_ref, a_ref, b_ref, w2_ref, _, da_ref, db_ref, dhb_ref):
        dhb = (0.5 * dh_ref[...]).astype(BF16)
        dhb_ref[...] = dhb
        for c0, cw in _col_chunks(D_FF):
            dhid = _dot_nt(dhb, w2_ref[c0:c0 + cw, :])
            av = a_ref[:, c0:c0 + cw].astype(F32)
            bv = b_ref[:, c0:c0 + cw].astype(F32)
            s = _sigmoid(av)
            da_ref[:, c0:c0 + cw] = (dhid * bv * (s * (1.0 + av * (1.0 - s)))).astype(BF16)
            db_ref[:, c0:c0 + cw] = (dhid * (av * s)).astype(BF16)

    hid = pl.BlockSpec((tm, D_FF), lambda i: (i, 0))
    row = pl.BlockSpec((tm, D_MODEL), lambda i: (i, 0))
    return _pcall(
        body, name=name, grid=(t_rows // tm,),
        in_specs=[row, hid, hid, _resident((D_FF, D_MODEL)), ANY_SPEC],
        out_specs=[hid, hid, row],
        out_shape=[jax.ShapeDtypeStruct((t_rows, D_FF), BF16), jax.ShapeDtypeStruct((t_rows, D_FF), BF16),
                   jax.ShapeDtypeStruct((t_rows, D_MODEL), BF16)],
        compiler_params=_cp(("arbitrary",), VMEM_BIG),
    )(dh, a, b, w2, after)


def ffn_backward_input(dh, h, norm, da, db, w1, w3, after, tm, name, examples=None):
    t_rows = h.shape[0]

    def body(dh_ref, h_ref, g_ref, da_ref, db_ref, w1_ref, w3_ref, _, dhin_ref, *rest):
        dg_ref = rest[-1]
        dhn = _dot(da_ref[...], w1_ref[...]) + _dot(db_ref[...], w3_ref[...])
        dx, dg = _rms_bwd(h_ref[...], g_ref[...], dhn)
        dhin = dh_ref[...] + dx
        dhin_ref[...] = dhin
        _accumulate(dg_ref, dg, pl.program_id(0) == 0)
        if examples is not None:
            @pl.when(pl.program_id(0) % tiles == tiles - 1)
            def _():
                rest[0][...] = dhin[tm - N_META:, :]

    row = pl.BlockSpec((tm, D_MODEL), lambda i: (i, 0))
    vec = pl.BlockSpec((1, D_MODEL), lambda i: (0, 0))
    hid = pl.BlockSpec((tm, D_FF), lambda i: (i, 0))
    if examples is None:
        out_specs = [row, vec]
        out_shape = [jax.ShapeDtypeStruct((t_rows, D_MODEL), F32), jax.ShapeDtypeStruct((1, D_MODEL), F32)]
    else:
        n_b, seq = examples
        tiles = (seq + N_META) // tm
        out_specs = [pl.BlockSpec((None, tm, D_MODEL), lambda i: (i // tiles, i % tiles, 0)),
                     pl.BlockSpec((None, N_META, D_MODEL), lambda i: (i // tiles, 0, 0)), vec]
        out_shape = [jax.ShapeDtypeStruct((n_b, seq, D_MODEL), F32), jax.ShapeDtypeStruct((n_b, N_META, D_MODEL), F32),
                     jax.ShapeDtypeStruct((1, D_MODEL), F32)]
    return _pcall(
        body, name=name, grid=(t_rows // tm,),
        in_specs=[row, row, vec, hid, hid, _resident((D_FF, D_MODEL)), _resident((D_FF, D_MODEL)), ANY_SPEC],
        out_specs=out_specs, out_shape=out_shape,
        compiler_params=_cp(("arbitrary",), VMEM_BIG),
    )(dh, h, norm, da, db, w1, w3, after)


def ffn_backward_weights(hn, dh, a, b, da, db, tm, tn, name):
    t_rows = hn.shape[0]
    ni = t_rows // tm
    kc = _row_tile(tm, 688)

    def body(hn_ref, dh_ref, a_ref, b_ref, da_ref, db_ref, dw1_ref, dw3_ref, dw2_ref, acc1, acc3, acc2):
        i = pl.program_id(1)
        parts = None
        for r0 in range(0, tm, kc):
            rows = slice(r0, r0 + kc)
            hn_v = hn_ref[rows, :]
            av = a_ref[rows, :].astype(F32)
            hid = (av * _sigmoid(av) * b_ref[rows, :].astype(F32)).astype(BF16)
            new = (_dot_tn(hn_v, da_ref[rows, :]), _dot_tn(hn_v, db_ref[rows, :]), _dot_tn(dh_ref[rows, :], hid))
            parts = new if parts is None else tuple(p + q for p, q in zip(parts, new))
        _accumulate(acc1, parts[0], i == 0)
        _accumulate(acc3, parts[1], i == 0)
        _accumulate(acc2, parts[2], i == 0)

        @pl.when(i == ni - 1)
        def _():
            dw1_ref[...] = acc1[...].T.astype(BF16)
            dw3_ref[...] = acc3[...].T.astype(BF16)
            dw2_ref[...] = acc2[...].T.astype(BF16)

    row = pl.BlockSpec((tm, D_MODEL), lambda j, i: (i, 0))
    hid_blk = pl.BlockSpec((tm, tn), lambda j, i: (i, j))
    w_row = pl.BlockSpec((tn, D_MODEL), lambda j, i: (j, 0))
    out = jax.ShapeDtypeStruct((D_FF, D_MODEL), BF16)
    return _pcall(
        body, name=name, grid=(D_FF // tn, ni),
        in_specs=[row, row, hid_blk, hid_blk, hid_blk, hid_blk],
        out_specs=[w_row, w_row, w_row], out_shape=[out, out, out],
        scratch_shapes=[pltpu.VMEM((D_MODEL, tn), F32)] * 3,
        compiler_params=_cp(("arbitrary", "arbitrary"), VMEM_BIG),
    )(hn, dh, a, b, da, db)


def mix_forward(h, norm, wing, tm, name):
    t_rows = h.shape[0]

    def body(h_ref, g_ref, w_ref, hn_ref, p_ref):
        hn = _rms_fwd(h_ref[...], g_ref[...]).astype(BF16)
        hn_ref[...] = hn
        for j in range(N_DEV):
            p_ref[:, j * IN_BLK:(j + 1) * IN_BLK] = _dot(hn, w_ref[j]).astype(BF16)

    row = pl.BlockSpec((tm, D_MODEL), lambda i: (i, 0))
    return _pcall(
        body, name=name, grid=(t_rows // tm,),
        in_specs=[row, pl.BlockSpec((1, D_MODEL), lambda i: (0, 0)),
                  pl.BlockSpec((N_DEV, D_MODEL, IN_BLK), lambda i: (0, 0, 0))],
        out_specs=[row, pl.BlockSpec((tm, IN_WIDTH), lambda i: (i, 0))],
        out_shape=[jax.ShapeDtypeStruct((t_rows, D_MODEL), BF16), jax.ShapeDtypeStruct((t_rows, IN_WIDTH), BF16)],
        compiler_params=_cp(("arbitrary",), VMEM_BIG),
    )(h, norm, wing)


def mix_backward_act(dh, h, norm, dproj, w_in_full, tm, name):
    t_rows = h.shape[0]

    def body(dh_ref, h_ref, g_ref, dp_ref, w_ref, dhin_ref, dg_ref):
        dx, dg = _rms_bwd(h_ref[...], g_ref[...], _dot_nt(dp_ref[...], w_ref[...]))
        dhin_ref[...] = dh_ref[...] + dx
        _accumulate(dg_ref, dg, pl.program_id(0) == 0)

    row = pl.BlockSpec((tm, D_MODEL), lambda i: (i, 0))
    vec = pl.BlockSpec((1, D_MODEL), lambda i: (0, 0))
    return _pcall(
        body, name=name, grid=(t_rows // tm,),
        in_specs=[row, row, vec, pl.BlockSpec((tm, IN_WIDTH), lambda i: (i, 0)), _resident((D_MODEL, IN_WIDTH))],
        out_specs=[row, vec],
        out_shape=[jax.ShapeDtypeStruct((t_rows, D_MODEL), F32), jax.ShapeDtypeStruct((1, D_MODEL), F32)],
        compiler_params=_cp(("arbitrary",), VMEM_BIG),
    )(dh, h, norm, dproj, w_in_full)


def mix_backward_weights(hn, dproj, tm, name):
    t_rows = hn.shape[0]
    ni = t_rows // tm
    per_step = 2

    kc = _row_tile(tm, 688)

    def body(hn_ref, dp_ref, dw_ref, acc):
        i = pl.program_id(1)
        part = functools.reduce(lambda u, w: u + w, [_dot_tn(hn_ref[r0:r0 + kc, :], dp_ref[r0:r0 + kc, :])
                                                    for r0 in range(0, tm, kc)])
        _accumulate(acc, part, i == 0)

        @pl.when(i == ni - 1)
        def _():
            for k in range(per_step):
                dw_ref[k] = acc[:, k * IN_BLK:(k + 1) * IN_BLK].astype(BF16)

    return _pcall(
        body, name=name, grid=(N_DEV // per_step, ni),
        in_specs=[pl.BlockSpec((tm, D_MODEL), lambda j, i: (i, 0)),
                  pl.BlockSpec((tm, per_step * IN_BLK), lambda j, i: (i, j))],
        out_specs=pl.BlockSpec((per_step, D_MODEL, IN_BLK), lambda j, i: (j, 0, 0)),
        out_shape=jax.ShapeDtypeStruct((N_DEV, D_MODEL, IN_BLK), BF16),
        scratch_shapes=[pltpu.VMEM((D_MODEL, per_step * IN_BLK), F32)],
        compiler_params=_cp(("arbitrary", "arbitrary"), VMEM_BIG),
    )(hn, dproj)


GELU_C = 0.7978845608028654
GELU_K = 0.044715


def _gelu(x):
    return 0.5 * x * (1.0 + jnp.tanh(GELU_C * (x + GELU_K * (x * x * x))))


def _gelu_and_grad(x):
    th = jnp.tanh(GELU_C * (x + GELU_K * (x * x * x)))
    val = 0.5 * x * (1.0 + th)
    grad = 0.5 * (1.0 + th) + 0.5 * x * (1.0 - th * th) * (GELU_C * (1.0 + 3.0 * GELU_K * (x * x)))
    return val, grad


def merge_forward(h, yraw, attn, proj, glu_a, glu_b, w_out, tm, name):
    t_rows = h.shape[0]

    def body(h_ref, y_ref, at_ref, gate_ref, a_ref, b_ref, wo_ref, out_ref):
        y = _gelu(y_ref[...]).astype(BF16)
        ssm = _dot(y, a_ref[...]) * _sigmoid(_dot(y, b_ref[...]))
        ga = gate_ref[:, :D_MODEL].astype(F32)
        gs = gate_ref[:, D_MODEL:].astype(F32)
        merged = _sigmoid(ga) * at_ref[...].astype(F32) + _sigmoid(gs) * ssm
        out_ref[...] = h_ref[...] + _dot(merged.astype(BF16), wo_ref[...])

    row = pl.BlockSpec((tm, D_MODEL), lambda i: (i, 0))
    glu = pl.BlockSpec((SSM_WIDTH, D_MODEL), lambda i: (0, 0))
    return _pcall(
        body, name=name, grid=(t_rows // tm,),
        in_specs=[row, pl.BlockSpec((tm, SSM_WIDTH), lambda i: (i, 0)), row,
                  pl.BlockSpec((tm, 2 * D_MODEL), lambda i: (i, 1)), glu, glu,
                  pl.BlockSpec((D_MODEL, D_MODEL), lambda i: (0, 0))],
        out_specs=row, out_shape=jax.ShapeDtypeStruct((t_rows, D_MODEL), F32),
        compiler_params=_cp(("arbitrary",), VMEM_BIG),
    )(h, yraw, attn, proj, glu_a, glu_b, w_out)


def merge_backward(dh, yraw, attn, proj, glu_a, glu_b, w_out, after, tm, name):
    t_rows = dh.shape[0]

    def body(dh_ref, y_ref, at_ref, gate_ref, a_ref, b_ref, wo_ref, _,
             dat_ref, dy_ref, dgate_ref, d16_ref, mg_ref, y16_ref, dya_ref, dyb_ref):
        d16 = dh_ref[...].astype(BF16)
        d16_ref[...] = d16
        gel, dgel = _gelu_and_grad(y_ref[...].astype(F32))
        y16 = gel.astype(BF16)
        y16_ref[...] = y16
        dy = None
        for c0, cw in _col_chunks(D_MODEL):
            cols = slice(c0, c0 + cw)
            gcols = slice(D_MODEL + c0, D_MODEL + c0 + cw)
            dmerged = _dot_nt(d16, wo_ref[cols, :])
            ya = _dot(y16, a_ref[:, cols])
            sb = _sigmoid(_dot(y16, b_ref[:, cols]))
            ssm = ya * sb
            sa = _sigmoid(gate_ref[:, cols].astype(F32))
            ss = _sigmoid(gate_ref[:, gcols].astype(F32))
            attn_v = at_ref[:, cols].astype(F32)
            mg_ref[:, cols] = (sa * attn_v + ss * ssm).astype(BF16)
            dat_ref[:, cols] = (dmerged * sa).astype(BF16)
            dgate_ref[:, cols] = (dmerged * attn_v * sa * (1.0 - sa)).astype(BF16)
            dgate_ref[:, gcols] = (dmerged * ssm * ss * (1.0 - ss)).astype(BF16)
            dssm = dmerged * ss
            dya = (dssm * sb).astype(BF16)
            dyb = (dssm * ya * sb * (1.0 - sb)).astype(BF16)
            dya_ref[:, cols] = dya
            dyb_ref[:, cols] = dyb
            part = _dot_nt(dya, a_ref[:, cols]) + _dot_nt(dyb, b_ref[:, cols])
            dy = part if dy is None else dy + part
        dy_ref[...] = (dy * dgel).astype(BF16)

    row = pl.BlockSpec((tm, D_MODEL), lambda i: (i, 0))
    ssm_row = pl.BlockSpec((tm, SSM_WIDTH), lambda i: (i, 0))
    gates = pl.BlockSpec((tm, 2 * D_MODEL), lambda i: (i, 1))
    wide = jax.ShapeDtypeStruct((t_rows, D_MODEL), BF16)
    narrow = jax.ShapeDtypeStruct((t_rows, SSM_WIDTH), BF16)
    return _pcall(
        body, name=name, grid=(t_rows // tm,),
        in_specs=[row, ssm_row, row, gates, _resident((SSM_WIDTH, D_MODEL)), _resident((SSM_WIDTH, D_MODEL)),
                  _resident((D_MODEL, D_MODEL)), ANY_SPEC],
        out_specs=[row, ssm_row, gates, row, row, ssm_row, row, row],
        out_shape=[wide, narrow, jax.ShapeDtypeStruct((t_rows, IN_WIDTH), BF16), wide, wide, narrow, wide, wide],
        compiler_params=_cp(("arbitrary",), VMEM_BIG),
    )(dh, yraw, attn, proj, glu_a, glu_b, w_out, after)


def merge_backward_weights(d16, merged, y16, dya, dyb, tm, name):
    t_rows = d16.shape[0]

    def body(d_ref, mg_ref, y_ref, dya_ref, dyb_ref, dwo_ref, da_ref, db_ref):
        first = pl.program_id(0) == 0
        y16 = y_ref[...]
        _accumulate(dwo_ref, _dot_tn(mg_ref[...], d_ref[...]), first)
        _accumulate(da_ref, _dot_tn(y16, dya_ref[...]), first)
        _accumulate(db_ref, _dot_tn(y16, dyb_ref[...]), first)

    row = pl.BlockSpec((tm, D_MODEL), lambda i: (i, 0))
    ssm_row = pl.BlockSpec((tm, SSM_WIDTH), lambda i: (i, 0))
    glu = pl.BlockSpec((SSM_WIDTH, D_MODEL), lambda i: (0, 0))
    wo = pl.BlockSpec((D_MODEL, D_MODEL), lambda i: (0, 0))
    return _pcall(
        body, name=name, grid=(t_rows // tm,),
        in_specs=[row, row, ssm_row, row, row], out_specs=[wo, glu, glu],
        out_shape=[jax.ShapeDtypeStruct((D_MODEL, D_MODEL), F32), jax.ShapeDtypeStruct((SSM_WIDTH, D_MODEL), F32),
                   jax.ShapeDtypeStruct((SSM_WIDTH, D_MODEL), F32)],
        compiler_params=_cp(("arbitrary",), VMEM_BIG),
    )(d16, merged, y16, dya, dyb)


def final_loss_backward(h, target, norm, seq, tm, name):
    t_rows = h.shape[0]
    tiles_per_example = (seq + N_META) // tm

    def body(h_ref, t_ref, g_ref, dh_ref, loss_ref, dg_ref):
        i = pl.program_id(0)
        x = h_ref[...]
        g = g_ref[...]
        r = lax.rsqrt(jnp.mean(x * x, axis=-1, keepdims=True) + NORM_EPS)
        xh = x * r
        pos = lax.broadcasted_iota(jnp.int32, (tm, 1), 0) + (i % tiles_per_example) * tm
        diff = jnp.where(pos < seq, xh * g - t_ref[...], 0.0)
        part = 0.5 * jnp.sum(jnp.sum(diff * diff, axis=-1, keepdims=True), axis=0, keepdims=True) / D_MODEL
        dy = diff / D_MODEL
        t = dy * g
        dh_ref[...] = r * (t - xh * jnp.mean(t * xh, axis=-1, keepdims=True))
        _accumulate(loss_ref, jnp.broadcast_to(part, (1, LANES)), i == 0)
        _accumulate(dg_ref, jnp.sum(dy * xh, axis=0, keepdims=True), i == 0)

    row = pl.BlockSpec((tm, D_MODEL), lambda i: (i, 0))
    vec = pl.BlockSpec((1, D_MODEL), lambda i: (0, 0))
    per_example = pl.BlockSpec((None, tm, D_MODEL), lambda i: (i // tiles_per_example, i % tiles_per_example, 0))
    return _pcall(
        body, name=name, grid=(t_rows // tm,),
        in_specs=[row, per_example, vec],
        out_specs=[row, pl.BlockSpec((1, LANES), lambda i: (0, 0)), vec],
        out_shape=[jax.ShapeDtypeStruct((t_rows, D_MODEL), F32), jax.ShapeDtypeStruct((1, LANES), F32),
                   jax.ShapeDtypeStruct((1, D_MODEL), F32)],
        compiler_params=_cp(("arbitrary",), VMEM_BIG),
    )(h, target, norm)


ATTN_SCALE = HEAD_DIM ** -0.5
STACK_HEADS = (0, 2, 1, 3)
META_PAD = LANES - N_META


def _lane_half(shape, hf):
    lane = lax.broadcasted_iota(jnp.int32, shape, 1)
    return (lane < HEAD_DIM) if hf == 0 else (lane >= HEAD_DIM)


def _kv_variants(ref, rows, kh, pad_rows=0):
    tile = kh // 2
    t = ref[rows, tile * LANES:(tile + 1) * LANES].astype(F32)
    swapped = pltpu.roll(t, HEAD_DIM, axis=1)
    at_low, at_high = (t, swapped) if kh % 2 == 0 else (swapped, t)
    lo = jnp.where(_lane_half(t.shape, 0), at_low, 0.0).astype(BF16)
    hi = jnp.where(_lane_half(t.shape, 1), at_high, 0.0).astype(BF16)
    if pad_rows:
        zeros = jnp.zeros((pad_rows, LANES), BF16)
        lo, hi = jnp.concatenate([lo, zeros], axis=0), jnp.concatenate([hi, zeros], axis=0)
    return lo, hi


def _key_tiles(ref, key_rows, kh):
    return [_kv_variants(ref, r, kh, META_PAD if i == len(key_rows) - 1 else 0) for i, r in enumerate(key_rows)]


def _to_kv_lanes(lo, hi, kh):
    lo = jnp.where(_lane_half(lo.shape, 0), lo, 0.0)
    hi = jnp.where(_lane_half(hi.shape, 1), hi, 0.0)
    if kh % 2 == 0:
        return lo + pltpu.roll(hi, HEAD_DIM, axis=1)
    return pltpu.roll(lo, HEAD_DIM, axis=1) + hi


def _stacked(ref, rows, kh):
    col = kh * 2 * LANES
    return jnp.concatenate([ref[rows, col:col + LANES], ref[rows, col + LANES:col + 2 * LANES]], axis=0)


def _sink_column(sink_ref, kh, nq):
    row = lax.broadcasted_iota(jnp.int32, (4 * nq, 1), 0)
    col = jnp.zeros((4 * nq, 1), F32)
    for quarter, g in enumerate(STACK_HEADS):
        col = jnp.where(row // nq == quarter, sink_ref[0, kh * Q_PER_KV + g], col)
    return col


def _softmax_parts(qs, key_tiles, masks, sink):
    scores = []
    for (k_lo, k_hi), mask in zip(key_tiles, masks):
        s = jnp.concatenate([_dot_nt(qs, k_lo), _dot_nt(qs, k_hi)], axis=0) * ATTN_SCALE
        scores.append(s if mask is None else jnp.where(mask, s, NEG_INF))
    m = jnp.maximum(_row_reduce(scores, jnp.maximum, jnp.max), sink)
    probs = [jnp.exp(s - m) for s in scores]
    e_sink = jnp.exp(sink - m)
    den = _row_sums(probs) + e_sink
    return probs, 1.0 / den, e_sink


def _row_reduce(tiles, combine, reduce):
    chunks = [t[:, c:c + LANES] for t in tiles for c in range(0, t.shape[-1], LANES)]
    return reduce(functools.reduce(combine, chunks), axis=-1, keepdims=True)


def _row_sums(tiles):
    return _row_reduce(tiles, lambda u, w: u + w, jnp.sum)


def _band_mask(nq, first):
    keys = BLOCK if first else 2 * BLOCK
    qi = lax.broadcasted_iota(jnp.int32, (4 * nq, keys), 0) % nq
    kj = lax.broadcasted_iota(jnp.int32, (4 * nq, keys), 1)
    if first:
        return kj <= qi
    return jnp.logical_and(kj > qi, kj <= qi + BLOCK)


def _meta_mask(nq, causal):
    qi = lax.broadcasted_iota(jnp.int32, (4 * nq, LANES), 0) % nq
    kj = lax.broadcasted_iota(jnp.int32, (4 * nq, LANES), 1)
    return jnp.logical_and(kj < N_META, kj <= qi) if causal else kj < N_META


def _attention_schedule(seq, queries, carry):
    meta_rows = pl.ds(seq, N_META)
    meta_ok = _meta_mask(BLOCK, False)
    carry = queries(pl.ds(0, BLOCK), BLOCK, [pl.ds(0, BLOCK), meta_rows], [_band_mask(BLOCK, True), meta_ok], carry)

    def block(n, c):
        r0 = pl.multiple_of(n * BLOCK, BLOCK)
        p0 = pl.multiple_of((n - 1) * BLOCK, BLOCK)
        return queries(pl.ds(r0, BLOCK), BLOCK, [pl.ds(p0, 2 * BLOCK), meta_rows], [_band_mask(BLOCK, False), meta_ok], c)

    carry = lax.fori_loop(1, seq // BLOCK, block, carry)
    return queries(meta_rows, N_META, [meta_rows], [_meta_mask(N_META, True)], carry)


def attention_forward(proj3, sinks, seq, name):
    n_b, n_l, _ = proj3.shape

    def body(sink_ref, q_ref, k_ref, v_ref, o_ref):
        def queries(q_rows, nq, key_rows, masks, carry):
            for kh in range(N_KV_HEADS):
                ks = _key_tiles(k_ref, key_rows, kh)
                vs = _key_tiles(v_ref, key_rows, kh)
                qs = _stacked(q_ref, q_rows, kh)
                probs, inv, _ = _softmax_parts(qs, ks, masks, _sink_column(sink_ref, kh, nq))
                probs = [p.astype(BF16) for p in probs]
                o_lo = functools.reduce(lambda u, w: u + w, [_dot(p[:2 * nq], v_lo) for p, (v_lo, _) in zip(probs, vs)])
                o_hi = functools.reduce(lambda u, w: u + w, [_dot(p[2 * nq:], v_hi) for p, (_, v_hi) in zip(probs, vs)])
                out = (o_lo * inv[:2 * nq] + o_hi * inv[2 * nq:]).astype(BF16)
                col = kh * 2 * LANES
                o_ref[q_rows, col:col + LANES] = out[:nq]
                o_ref[q_rows, col + LANES:col + 2 * LANES] = out[nq:]
            return carry

        _attention_schedule(seq, queries, 0)

    return _pcall(
        body, name=name, grid=(n_b,),
        in_specs=[pl.BlockSpec(memory_space=pltpu.SMEM),
                  pl.BlockSpec((None, n_l, D_MODEL), lambda b: (b, 0, 0)),
                  pl.BlockSpec((None, n_l, KV_WIDTH), lambda b: (b, 0, D_MODEL // KV_WIDTH)),
                  pl.BlockSpec((None, n_l, KV_WIDTH), lambda b: (b, 0, D_MODEL // KV_WIDTH + 1))],
        out_specs=pl.BlockSpec((None, n_l, D_MODEL), lambda b: (b, 0, 0)),
        out_shape=jax.ShapeDtypeStruct((n_b, n_l, D_MODEL), BF16),
        compiler_params=_cp(("arbitrary",), VMEM_BIG),
    )(sinks, proj3, proj3, proj3)


def attention_backward(proj3, dattn3, dproj3, sinks, after, seq, name):
    n_b, n_l, _ = proj3.shape
    qkv_width = D_MODEL + 2 * KV_WIDTH

    def body(sink_ref, q_ref, k_ref, v_ref, do_ref, _, __, dqkv_ref, dsink_ref, dk_ref, dv_ref):
        dk_ref[...] = jnp.zeros_like(dk_ref)
        dv_ref[...] = jnp.zeros_like(dv_ref)
        sub = lax.broadcasted_iota(jnp.int32, (SUBLANES, LANES), 0)
        lane = lax.broadcasted_iota(jnp.int32, (SUBLANES, LANES), 1)

        def queries(q_rows, nq, key_rows, masks, dsink):
            for kh in range(N_KV_HEADS):
                ks = _key_tiles(k_ref, key_rows, kh)
                vs = _key_tiles(v_ref, key_rows, kh)
                qs = _stacked(q_ref, q_rows, kh)
                dos = _stacked(do_ref, q_rows, kh)
                probs, inv, e_sink = _softmax_parts(qs, ks, masks, _sink_column(sink_ref, kh, nq))
                probs = [p * inv for p in probs]
                dps = [jnp.concatenate([_dot_nt(dos, v_lo), _dot_nt(dos, v_hi)], axis=0) for v_lo, v_hi in vs]
                delta = _row_sums([p * dp for p, dp in zip(probs, dps)])
                d_sink = -(e_sink * inv) * delta
                for quarter, g in enumerate(STACK_HEADS):
                    d_here = jnp.sum(d_sink[quarter * nq:(quarter + 1) * nq], axis=0, keepdims=True)
                    dsink = dsink + jnp.where(jnp.logical_and(sub == 0, lane == kh * Q_PER_KV + g), d_here, 0.0)
                dq = None
                tile = slice((kh // 2) * LANES, (kh // 2 + 1) * LANES)
                for r, p, dp, (k_lo, k_hi) in zip(key_rows, probs, dps, ks):
                    ds = (p * (dp - delta)).astype(BF16)
                    p16 = p.astype(BF16)
                    dq_x = _dot(ds[:2 * nq], k_lo) + _dot(ds[2 * nq:], k_hi)
                    dq = dq_x if dq is None else dq + dq_x
                    d_k = _to_kv_lanes(_dot_tn(ds[:2 * nq], qs), _dot_tn(ds[2 * nq:], qs), kh) * ATTN_SCALE
                    d_v = _to_kv_lanes(_dot_tn(p16[:2 * nq], dos), _dot_tn(p16[2 * nq:], dos), kh)
                    n_keys = r.size
                    dk_ref[r, tile] += d_k[:n_keys]
                    dv_ref[r, tile] += d_v[:n_keys]
                dq = (dq * ATTN_SCALE).astype(BF16)
                col = kh * 2 * LANES
                dqkv_ref[q_rows, col:col + LANES] = dq[:nq]
                dqkv_ref[q_rows, col + LANES:col + 2 * LANES] = dq[nq:]
            return dsink

        dsink_ref[...] = _attention_schedule(seq, queries, jnp.zeros((SUBLANES, LANES), F32))
        dqkv_ref[:, D_MODEL:D_MODEL + KV_WIDTH] = dk_ref[...].astype(BF16)
        dqkv_ref[:, D_MODEL + KV_WIDTH:] = dv_ref[...].astype(BF16)

    return _pcall(
        body, name=name, grid=(n_b,),
        in_specs=[pl.BlockSpec(memory_space=pltpu.SMEM),
                  pl.BlockSpec((None, n_l, D_MODEL), lambda b: (b, 0, 0)),
                  pl.BlockSpec((None, n_l, KV_WIDTH), lambda b: (b, 0, D_MODEL // KV_WIDTH)),
                  pl.BlockSpec((None, n_l, KV_WIDTH), lambda b: (b, 0, D_MODEL // KV_WIDTH + 1)),
                  pl.BlockSpec((None, n_l, D_MODEL), lambda b: (b, 0, 0)),
                  ANY_SPEC, ANY_SPEC],
        out_specs=[pl.BlockSpec((None, n_l, qkv_width), lambda b: (b, 0, 0)),
                   pl.BlockSpec((None, SUBLANES, LANES), lambda b: (b, 0, 0))],
        out_shape=[jax.ShapeDtypeStruct(dproj3.shape, BF16), jax.ShapeDtypeStruct((n_b, SUBLANES, LANES), F32)],
        scratch_shapes=[pltpu.VMEM((n_l, KV_WIDTH), F32), pltpu.VMEM((n_l, KV_WIDTH), F32)],
        input_output_aliases={5: 0},
        compiler_params=_cp(("arbitrary",), VMEM_BIG),
    )(sinks, proj3, proj3, proj3, dattn3, dproj3, after)


TAB_ROWS = 8
SCAN_UNROLL = 4


def _cmul(ar, ai, br, bi):
    return ar * br - ai * bi, ar * bi + ai * br


def _discretise(ar, ai, ls):
    step = jnp.exp(ls)
    mag = jnp.exp(ar * step)
    ang = ai * step
    cos, sin = jnp.cos(ang), jnp.sin(ang)
    lr, li = mag * cos, mag * sin
    den = ar * ar + ai * ai
    nr, ni = lr - 1.0, li
    cr = (nr * ar + ni * ai) / den
    ci = (ni * ar - nr * ai) / den
    return step, mag, lr, li, den, nr, ni, cr, ci


def _scan_tables(lr, li, reverse):
    n = lr.shape[-1]
    pw = [(lr, li)]
    for _ in range(SUBLANES - 1):
        pw.append(_cmul(pw[-1][0], pw[-1][1], lr, li))
    row = lax.broadcasted_iota(jnp.int32, (SUBLANES, n), 0)
    out = []
    for d in (1, 2, 4):
        ok = (row + d <= SUBLANES - 1) if reverse else (row >= d)
        out += [jnp.where(ok, pw[d - 1][0], 0.0), jnp.where(ok, pw[d - 1][1], 0.0)]
    cr = jnp.zeros((SUBLANES, n), F32)
    ci = jnp.zeros((SUBLANES, n), F32)
    for r in range(SUBLANES):
        e = (SUBLANES - r) if reverse else (r + 1)
        cr = jnp.where(row == r, pw[e - 1][0], cr)
        ci = jnp.where(row == r, pw[e - 1][1], ci)
    return out + [cr, ci]


def ssm_prepare(ar, ai, ls, br_t, bi_t, name):
    def body(ar_ref, ai_ref, ls_ref, br_ref, bi_ref, bbr_ref, bbi_ref, tf_ref, tr_ref):
        _, _, lr, li, _, _, _, cr, ci = _discretise(ar_ref[...], ai_ref[...], ls_ref[...])
        br, bi = br_ref[...], bi_ref[...]
        bbr_ref[...] = cr * br - ci * bi
        bbi_ref[...] = cr * bi + ci * br
        for k, t in enumerate(_scan_tables(lr, li, False)):
            tf_ref[k] = t
        for k, t in enumerate(_scan_tables(lr, -li, True)):
            tr_ref[k] = t

    return _pcall(
        body, name=name,
        out_shape=[jax.ShapeDtypeStruct((SSM_GROUP, N_STATES), F32), jax.ShapeDtypeStruct((SSM_GROUP, N_STATES), F32),
                   jax.ShapeDtypeStruct((TAB_ROWS, SUBLANES, N_STATES), F32),
                   jax.ShapeDtypeStruct((TAB_ROWS, SUBLANES, N_STATES), F32)],
    )(ar, ai, ls, br_t, bi_t)


def ssm_param_backward(ar, ai, ls, br_t, bi_t, dlr_p, dli_p, dbbr, dbbi, group_sum, name):
    def body(ar_ref, ai_ref, ls_ref, br_ref, bi_ref, dlr_ref, dli_ref, dbbr_ref, dbbi_ref, gs_ref,
             dar_ref, dai_ref, dls_ref, dbr_ref, dbi_ref):
        ar, ai = ar_ref[...], ai_ref[...]
        step, mag, lr, li, den, nr, ni, cr, ci = _discretise(ar, ai, ls_ref[...])
        br, bi, dbbr_v, dbbi_v = br_ref[...], bi_ref[...], dbbr_ref[...], dbbi_ref[...]
        dbr_ref[...] = cr * dbbr_v + ci * dbbi_v
        dbi_ref[...] = cr * dbbi_v - ci * dbbr_v
        dcr = jnp.sum(dbbr_v * br + dbbi_v * bi, axis=0, keepdims=True)
        dci = jnp.sum(dbbi_v * br - dbbr_v * bi, axis=0, keepdims=True)
        dnr = (dcr * ar - dci * ai) / den
        dni = (dcr * ai + dci * ar) / den
        dden = -(cr * dcr + ci * dci) / den
        dar = (dcr * nr + dci * ni) / den + dden * 2.0 * ar
        dai = (dcr * ni - dci * nr) / den + dden * 2.0 * ai
        dlr = jnp.sum(dlr_ref[...], axis=0, keepdims=True) + dnr
        dli = jnp.sum(dli_ref[...], axis=0, keepdims=True) + dni
        dmag = (dlr * lr + dli * li) / mag
        dang = dli * lr - dlr * li
        dar_ref[...] = dar + dmag * mag * step
        dai_ref[...] = dai + dang * step
        dstep = dmag * mag * ar + dang * ai
        dls_ref[...] = jnp.dot(dstep * step, gs_ref[...], preferred_element_type=F32, precision=lax.Precision.HIGHEST)

    vec = jax.ShapeDtypeStruct((1, N_STATES), F32)
    mat = jax.ShapeDtypeStruct((SSM_GROUP, N_STATES), F32)
    return _pcall(body, name=name, out_shape=[vec, vec, jax.ShapeDtypeStruct((1, LANES), F32), mat, mat])(
        ar, ai, ls, br_t, bi_t, dlr_p, dli_p, dbbr, dbbi, group_sum)


def _scan_rows(a, b, tabs, carry, reverse):
    for k, d in enumerate((1, 2, 4)):
        shift = SUBLANES - d if reverse else d
        sr, si = pltpu.roll(a, shift, axis=0), pltpu.roll(b, shift, axis=0)
        pr, pi = _cmul(tabs[2 * k], tabs[2 * k + 1], sr, si)
        a, b = a + pr, b + pi
    pr, pi = _cmul(tabs[6], tabs[7], carry[0], carry[1])
    return a + pr, b + pi


def _time_groups(seq, reverse):
    meta = [seq + SUBLANES * g for g in range(N_META // SUBLANES)]
    return meta[::-1] if reverse else meta


def ssm_forward_scan(proj3, b_comb, tabf, c_comb, dvec, seq, name):
    n_b, n_l, _ = proj3.shape
    u_blk = (D_MODEL + 2 * KV_WIDTH) // LANES

    def body(u_ref, b_ref, tab_ref, c_ref, d_ref, x_ref, y_ref, bu, xs):
        j = pl.program_id(1)
        u = u_ref[...]
        bu[...] = _dot(u, b_ref[...])
        tabs = [tab_ref[k] for k in range(TAB_ROWS)]

        def group(r0, carry):
            rows = pl.ds(r0, SUBLANES)
            a, b = _scan_rows(bu[rows, :SCAN_COLS], bu[rows, SCAN_COLS:], tabs, carry, False)
            xs[rows, :SCAN_COLS] = a
            xs[rows, SCAN_COLS:] = b
            return (jnp.broadcast_to(a[SUBLANES - 1:, :], a.shape), jnp.broadcast_to(b[SUBLANES - 1:, :], b.shape))

        zero = jnp.zeros((SUBLANES, SCAN_COLS), F32)
        carry = (zero, zero)
        for r0 in _time_groups(seq, False):
            carry = group(r0, carry)
        span = SCAN_UNROLL * SUBLANES

        def groups(t, c):
            for k in range(SCAN_UNROLL):
                c = group(pl.multiple_of(t * span, span) + k * SUBLANES, c)
            return c

        lax.fori_loop(0, seq // span, groups, carry)
        x16 = xs[...].astype(BF16)
        x_ref[...] = x16
        contrib = _dot(x16, c_ref[...])

        @pl.when(j % 2 == 0)
        def _():
            y_ref[...] = contrib + d_ref[...] * u.astype(F32)

        @pl.when(j % 2 == 1)
        def _():
            y_ref[...] += contrib

    return _pcall(
        body, name=name, grid=(n_b, N_SCAN_BLK),
        in_specs=[pl.BlockSpec((None, n_l, LANES), lambda b, j: (b, 0, u_blk + j // 2)),
                  pl.BlockSpec((None, LANES, 2 * SCAN_COLS), lambda b, j: (j, 0, 0)),
                  pl.BlockSpec((TAB_ROWS, SUBLANES, SCAN_COLS), lambda b, j: (0, 0, j)),
                  pl.BlockSpec((None, 2 * SCAN_COLS, LANES), lambda b, j: (j, 0, 0)),
                  pl.BlockSpec((1, LANES), lambda b, j: (0, j // 2))],
        out_specs=[pl.BlockSpec((None, n_l, 2 * SCAN_COLS), lambda b, j: (b, 0, j)),
                   pl.BlockSpec((None, n_l, LANES), lambda b, j: (b, 0, j // 2))],
        out_shape=[jax.ShapeDtypeStruct((n_b, n_l, 2 * N_STATES), BF16),
                   jax.ShapeDtypeStruct((n_b, n_l, SSM_WIDTH), F32)],
        scratch_shapes=[pltpu.VMEM((n_l, 2 * SCAN_COLS), F32)] * 2,
        compiler_params=_cp(("arbitrary", "arbitrary"), VMEM_BIG),
    )(proj3, b_comb, tabf, c_comb, dvec)


def ssm_backward_scan(dyraw3, xs3, dproj3, c_comb_t, tabr, b_comb_t, dvec, seq, name):
    n_b, n_l, _ = xs3.shape
    u_blk = (D_MODEL + 2 * KV_WIDTH) // LANES

    def body(dy_ref, x_ref, _, c_ref, tab_ref, b_ref, d_ref, du_ref, g_ref, dlr_ref, dli_ref, dx, gs, xs, du_acc):
        j = pl.program_id(1)
        dy = dy_ref[...]
        dx[...] = _dot(dy, c_ref[...])
        xs[...] = x_ref[...].astype(F32)
        tabs = [tab_ref[k] for k in range(TAB_ROWS)]
        last_row = lax.broadcasted_iota(jnp.int32, (SUBLANES, SCAN_COLS), 0) == SUBLANES - 1

        def group(r0, state):
            cr, ci, acc_r, acc_i = state
            rows = pl.ds(r0, SUBLANES)
            a, b = _scan_rows(dx[rows, :SCAN_COLS], dx[rows, SCAN_COLS:], tabs, (cr, ci), True)
            gs[rows, :SCAN_COLS] = a
            gs[rows, SCAN_COLS:] = b
            na = jnp.where(last_row, cr, pltpu.roll(a, SUBLANES - 1, axis=0))
            nb = jnp.where(last_row, ci, pltpu.roll(b, SUBLANES - 1, axis=0))
            xa, xb = xs[rows, :SCAN_COLS], xs[rows, SCAN_COLS:]
            return (jnp.broadcast_to(a[:1, :], a.shape), jnp.broadcast_to(b[:1, :], b.shape),
                    acc_r + na * xa + nb * xb, acc_i + nb * xa - na * xb)

        zero = jnp.zeros((SUBLANES, SCAN_COLS), F32)
        span = SCAN_UNROLL * SUBLANES
        n_spans = seq // span

        def groups(t, s):
            for k in reversed(range(SCAN_UNROLL)):
                s = group(pl.multiple_of((n_spans - 1 - t) * span, span) + k * SUBLANES, s)
            return s

        state = lax.fori_loop(0, n_spans, groups, (zero, zero, zero, zero))
        for r0 in _time_groups(seq, True):
            state = group(r0, state)
        dlr_ref[...] = state[2]
        dli_ref[...] = state[3]
        g16 = gs[...].astype(BF16)
        g_ref[...] = g16
        contrib = _dot(g16, b_ref[...])

        @pl.when(j % 2 == 0)
        def _():
            du_acc[...] = contrib + d_ref[...] * dy.astype(F32)

        @pl.when(j % 2 == 1)
        def _():
            du_ref[...] = (du_acc[...] + contrib).astype(BF16)

    state_blk = pl.BlockSpec((None, n_l, 2 * SCAN_COLS), lambda b, j: (b, 0, j))
    dl_blk = pl.BlockSpec((None, SUBLANES, SCAN_COLS), lambda b, j: (b, 0, j))
    return _pcall(
        body, name=name, grid=(n_b, N_SCAN_BLK),
        in_specs=[pl.BlockSpec((None, n_l, LANES), lambda b, j: (b, 0, j // 2)), state_blk,
                  pl.BlockSpec(memory_space=pl.ANY),
                  pl.BlockSpec((None, LANES, 2 * SCAN_COLS), lambda b, j: (j, 0, 0)),
                  pl.BlockSpec((TAB_ROWS, SUBLANES, SCAN_COLS), lambda b, j: (0, 0, j)),
                  pl.BlockSpec((None, 2 * SCAN_COLS, LANES), lambda b, j: (j, 0, 0)),
                  pl.BlockSpec((1, LANES), lambda b, j: (0, j // 2))],
        out_specs=[pl.BlockSpec((None, n_l, LANES), lambda b, j: (b, 0, u_blk + j // 2)), state_blk, dl_blk, dl_blk],
        out_shape=[jax.ShapeDtypeStruct(dproj3.shape, BF16), jax.ShapeDtypeStruct((n_b, n_l, 2 * N_STATES), BF16),
                   jax.ShapeDtypeStruct((n_b, SUBLANES, N_STATES), F32), jax.ShapeDtypeStruct((n_b, SUBLANES, N_STATES), F32)],
        scratch_shapes=[pltpu.VMEM((n_l, 2 * SCAN_COLS), F32)] * 3 + [pltpu.VMEM((n_l, LANES), F32)],
        input_output_aliases={2: 0},
        compiler_params=_cp(("arbitrary", "arbitrary"), VMEM_BIG),
    )(dyraw3, xs3, dproj3, c_comb_t, tabr, b_comb_t, dvec)


def ssm_param_grads(proj, gs, xs, dyraw, tm, name):
    t_rows = proj.shape[0]
    ni = t_rows // tm
    u_blk = (D_MODEL + 2 * KV_WIDTH) // LANES
    width = 2 * SCAN_COLS

    def body(u_ref, g_ref, x_ref, dy_ref, db_ref, dc_ref, dd_ref):
        cb, i = pl.program_id(0), pl.program_id(1)
        u, dy = u_ref[...], dy_ref[...]
        _accumulate(db_ref, _dot_tn(u, g_ref[...]), i == 0)
        _accumulate(dc_ref, _dot_tn(x_ref[...], dy), i == 0)

        @pl.when(cb % 2 == 0)
        def _():
            _accumulate(dd_ref, jnp.sum(dy.astype(F32) * u.astype(F32), axis=0, keepdims=True), i == 0)

    return _pcall(
        body, name=name, grid=(N_SCAN_BLK, ni),
        in_specs=[pl.BlockSpec((tm, LANES), lambda cb, i: (i, u_blk + cb // 2)),
                  pl.BlockSpec((tm, width), lambda cb, i: (i, cb)),
                  pl.BlockSpec((tm, width), lambda cb, i: (i, cb)),
                  pl.BlockSpec((tm, LANES), lambda cb, i: (i, cb // 2))],
        out_specs=[pl.BlockSpec((None, LANES, width), lambda cb, i: (cb, 0, 0)),
                   pl.BlockSpec((None, width, LANES), lambda cb, i: (cb, 0, 0)),
                   pl.BlockSpec((1, LANES), lambda cb, i: (0, cb // 2))],
        out_shape=[jax.ShapeDtypeStruct((N_SCAN_BLK, LANES, width), F32),
                   jax.ShapeDtypeStruct((N_SCAN_BLK, width, LANES), F32), jax.ShapeDtypeStruct((1, SSM_WIDTH), F32)],
        compiler_params=_cp(("arbitrary", "arbitrary"), VMEM_BIG),
    )(proj, gs, xs, dyraw)


def sum_leading(x, name):
    def body(x_ref, o_ref):
        acc = x_ref[0]
        for k in range(1, x.shape[0]):
            acc = acc + x_ref[k]
        o_ref[...] = acc

    return _pcall(body, name=name, out_shape=jax.ShapeDtypeStruct(x.shape[1:], x.dtype))(x)


WEIGHTS = ['meta_tokens', 'ffn1_norm', 'ffn1_w1', 'ffn1_w3', 'ffn1_w2', 'mix_norm', 'w_in', 'attn_sinks', 'ssm_a_re',
           'ssm_a_im', 'ssm_log_step', 'ssm_b_re', 'ssm_b_im', 'ssm_c_re', 'ssm_c_im', 'ssm_d', 'ssm_glu_a', 'ssm_glu_b',
           'w_out', 'ffn2_norm', 'ffn2_w1', 'ffn2_w3', 'ffn2_w2', 'final_norm']
SHARDED = ['ffn1_w1', 'ffn1_w3', 'ffn1_w2', 'ffn2_w1', 'ffn2_w3', 'ffn2_w2', 'w_in', 'ssm_glu_a', 'ssm_glu_b', 'w_out']
REPLICATED = ['ffn1_norm', 'mix_norm', 'ffn2_norm', 'final_norm', 'attn_sinks', 'ssm_a_re', 'ssm_a_im', 'ssm_log_step',
              'ssm_b_re', 'ssm_b_im', 'ssm_c_re', 'ssm_c_im', 'ssm_d']
PACK_COLS = 1024


def _pack(arrays):
    parts = []
    for a in arrays:
        flat = a.reshape(-1)
        chunk = SUBLANES * PACK_COLS
        padded = -(-flat.shape[0] // chunk) * chunk
        parts.append(jnp.pad(flat, (0, padded - flat.shape[0])).reshape(-1, PACK_COLS))
    return jnp.concatenate(parts, axis=0)


def _unpack(packed, shapes):
    out, row = [], 0
    for shape in shapes:
        size = 1
        for s in shape:
            size *= s
        chunk = SUBLANES * PACK_COLS
        rows = -(-size // chunk) * SUBLANES
        out.append(packed[row:row + rows].reshape(-1)[:size].reshape(shape))
        row += rows
    return out


def kernel(x, meta_tokens, ffn1_norm, ffn1_w1, ffn1_w3, ffn1_w2, mix_norm, w_in, attn_sinks, ssm_a_re, ssm_a_im, ssm_log_step, ssm_b_re, ssm_b_im, ssm_c_re, ssm_c_im, ssm_d, ssm_glu_a, ssm_glu_b, w_out, ffn2_norm, ffn2_w1, ffn2_w3, ffn2_w2, final_norm, loss_target, m_meta_tokens, m_ffn1_norm, m_ffn1_w1, m_ffn1_w3, m_ffn1_w2, m_mix_norm, m_w_in, m_attn_sinks, m_ssm_a_re, m_ssm_a_im, m_ssm_log_step, m_ssm_b_re, m_ssm_b_im, m_ssm_c_re, m_ssm_c_im, m_ssm_d, m_ssm_glu_a, m_ssm_glu_b, m_w_out, m_ffn2_norm, m_ffn2_w1, m_ffn2_w3, m_ffn2_w2, m_final_norm, v_meta_tokens, v_ffn1_norm, v_ffn1_w1, v_ffn1_w3, v_ffn1_w2, v_mix_norm, v_w_in, v_attn_sinks, v_ssm_a_re, v_ssm_a_im, v_ssm_log_step, v_ssm_b_re, v_ssm_b_im, v_ssm_c_re, v_ssm_c_im, v_ssm_d, v_ssm_glu_a, v_ssm_glu_b, v_w_out, v_ffn2_norm, v_ffn2_w1, v_ffn2_w3, v_ffn2_w2, v_final_norm):
    given = dict(locals())
    w = {n: given[n] for n in WEIGHTS}
    m = {n: given["m_" + n] for n in WEIGHTS}
    v = {n: given["v_" + n] for n in WEIGHTS}

    n_b, seq, _ = x.shape
    n_l = seq + N_META
    t_rows = n_b * n_l
    tm = _row_tile(n_l, 688)
    px, py, pc = _my_place()
    me = 4 * px + 2 * py + pc

    glu = jnp.stack([ssm_glu_a[0], ssm_glu_b[0]]).astype(BF16)
    ffn_names = ['ffn1_w1', 'ffn1_w3', 'ffn1_w2', 'ffn2_w1', 'ffn2_w3', 'ffn2_w2']

    def hidden_on_rows(n, t):
        return t[0] if n.endswith('w2') else t[0].T

    def hidden_on_rows_back(n, t):
        return t[None] if n.endswith('w2') else t.T[None]

    me_idx = jnp.reshape(me, (1,)).astype(jnp.int32)
    first_names, later_names = ffn_names[:3], ffn_names[3:]
    *first, metag = all_gather_list(
        [hidden_on_rows(n, w[n]).astype(BF16) for n in first_names] + [meta_tokens], meta_tokens, "ag_first")
    win_send, win_recv, win_shard, win_land, win_token = exchange_start(
        [w_in[0].astype(BF16)], first[0], True, "ag_w_in_start")
    later_shards = [hidden_on_rows(n, w[n]).astype(BF16) for n in later_names] + [glu, w_out[0].astype(BF16)]
    ag_send, ag_recv, later_shards, later_lands, ag_token = exchange_start(later_shards, win_token, True, "ag_later_start")
    full = {n: g.reshape(D_FF, D_MODEL) for n, g in zip(first_names, first)}
    meta_full = metag.transpose(1, 0, 2).reshape(N_META, D_MODEL)

    final_g = final_norm.reshape(1, D_MODEL)

    ar = ssm_a_re.reshape(1, N_STATES)
    ai = ssm_a_im.reshape(1, N_STATES)
    ls = jnp.repeat(ssm_log_step.reshape(SSM_GROUPS), SSM_STATE).reshape(1, N_STATES)
    br_t = ssm_b_re[0].transpose(2, 0, 1).reshape(SSM_GROUP, N_STATES)
    bi_t = ssm_b_im[0].transpose(2, 0, 1).reshape(SSM_GROUP, N_STATES)
    bbr, bbi, tabf, tabr = ssm_prepare(ar, ai, ls, br_t, bi_t, "ssm_prepare")
    bbr_g = bbr.reshape(SSM_GROUP, SSM_GROUPS, SSM_STATE).transpose(1, 0, 2)
    bbi_g = bbi.reshape(SSM_GROUP, SSM_GROUPS, SSM_STATE).transpose(1, 0, 2)
    groups_per_blk = SCAN_COLS // SSM_STATE
    half = ((jnp.arange(N_SCAN_BLK) % 2)[:, None] == jnp.arange(2)[None, :]).astype(F32)
    eye = jnp.eye(groups_per_blk, dtype=F32)

    def scan_blocks(re_g, im_g):
        def one(t):
            t = t.reshape(N_SCAN_BLK, groups_per_blk, SSM_GROUP, SSM_STATE)
            t = t[:, :, :, None, :] * eye[None, :, None, :, None]
            t = t.reshape(N_SCAN_BLK, LANES // 2, SCAN_COLS)
            return (t[:, None] * half[:, :, None, None]).reshape(N_SCAN_BLK, LANES, SCAN_COLS)
        return jnp.concatenate([one(re_g), one(im_g)], axis=-1).astype(BF16)

    b_comb = scan_blocks(bbr_g, bbi_g)
    c_comb_t = scan_blocks(ssm_c_re[0], -ssm_c_im[0])
    b_comb_t, c_comb = b_comb.transpose(0, 2, 1), c_comb_t.transpose(0, 2, 1)

    ffn1_w = (full['ffn1_w1'], full['ffn1_w3'], full['ffn1_w2'])
    h1, hn1, a1, b1, h0 = ffn_forward(x, ffn1_norm, *ffn1_w, ag_token, tm, "ffn1_fwd", meta=meta_full)
    win_shard, (wing,) = exchange_wait(win_send, win_recv, win_shard, win_land, h1, True, "ag_w_in_wait")
    wing = lax.dynamic_update_slice_in_dim(wing, win_shard[0][None], me, axis=0)
    hnm, proj = mix_forward(h1, mix_norm, wing, tm, "mix_fwd")
    proj3 = proj.reshape(n_b, n_l, IN_WIDTH)
    attn3 = attention_forward(proj3, attn_sinks, seq, "attn_fwd")
    attn = attn3.reshape(t_rows, D_MODEL)
    xs3, yraw3 = ssm_forward_scan(proj3, b_comb, tabf, c_comb, ssm_d, seq, "ssm_fwd")
    yraw = yraw3.reshape(t_rows, SSM_WIDTH)
    later_shards, later = exchange_wait(ag_send, ag_recv, later_shards, later_lands, yraw3, True, "ag_later_wait")
    later = [lax.dynamic_update_slice_in_dim(z, s[None], me, axis=0) for z, s in zip(later, later_shards)]
    for n, g in zip(later_names, later):
        full[n] = g.reshape(D_FF, D_MODEL)
    ffn2_w = (full['ffn2_w1'], full['ffn2_w3'], full['ffn2_w2'])
    glug, wog = later[len(later_names):]
    glu_a = glug[:, 0].transpose(1, 0, 2).reshape(SSM_WIDTH, D_MODEL)
    glu_b = glug[:, 1].transpose(1, 0, 2).reshape(SSM_WIDTH, D_MODEL)
    w_out_full = wog.reshape(D_MODEL, D_MODEL)
    h2 = merge_forward(h1, yraw, attn, proj, glu_a, glu_b, w_out_full, tm, "merge_fwd")
    h3, hn2, a2, b2 = ffn_forward(h2, ffn2_norm, *ffn2_w, ag_token, tm, "ffn2_fwd")
    dh3, loss_part, g_final = final_loss_backward(h3, loss_target, final_g, seq, tm, "loss_bwd")

    def blocked_ffn(d_w1t, d_w3t, d_w2):
        return tuple(t.reshape(N_DEV, FF_BLK, D_MODEL) for t in (d_w1t, d_w3t, d_w2))

    def blocked_cols(full_grad):
        r = full_grad.shape[0]
        return full_grad.reshape(r, N_DEV, full_grad.shape[1] // N_DEV).transpose(1, 0, 2).astype(BF16)

    early = {}

    def start_reduce(names, tag):
        srcs = [dw[n] for n in names]
        send, recv, srcs, lands, token = exchange_start(srcs, srcs[0], False, "rs_" + tag + "_start")
        early[tag] = (names, send, recv, srcs, lands)
        return token

    dw = {}
    da2, db2, dh3_half = ffn_backward_hidden(dh3, a2, b2, ffn2_w[2], g_final, tm, "ffn2_bwd_hid")
    dw['ffn2_w1'], dw['ffn2_w3'], dw['ffn2_w2'] = blocked_ffn(
        *ffn_backward_weights(hn2, dh3_half, a2, b2, da2, db2, n_l, FF_BWD_COLS, "ffn2_bwd_w"))
    token = start_reduce(later_names, "ffn2")
    dh2, g_ffn2_norm = ffn_backward_input(dh3, h2, ffn2_norm, da2, db2, ffn2_w[0], ffn2_w[1], token, tm, "ffn2_bwd_in")
    dattn, dyraw, dproj, *for_weights = merge_backward(dh2, yraw, attn, proj, glu_a, glu_b, w_out_full, token, tm,
                                                       "merge_bwd")
    d_wo, d_ga, d_gb = merge_backward_weights(*for_weights, tm, "merge_bwd_w")
    dw['ssm_glu_a'] = blocked_cols(d_ga)
    dw['ssm_glu_b'] = blocked_cols(d_gb)
    dw['w_out'] = d_wo.reshape(N_DEV, D_MODEL // N_DEV, D_MODEL).astype(BF16)
    token = start_reduce(['ssm_glu_a', 'ssm_glu_b', 'w_out'], "mix")
    dproj3 = dproj.reshape(n_b, n_l, IN_WIDTH)
    dproj3, dsink_p = attention_backward(proj3, dattn.reshape(n_b, n_l, D_MODEL), dproj3, attn_sinks, token, seq,
                                         "attn_bwd")
    dproj3, gs3, dlr_p, dli_p = ssm_backward_scan(
        dyraw.reshape(n_b, n_l, SSM_WIDTH), xs3, dproj3, c_comb_t, tabr, b_comb_t, ssm_d, seq, "ssm_bwd")
    dproj = dproj3.reshape(t_rows, IN_WIDTH)
    d_bd, d_cd, g_d = ssm_param_grads(proj, gs3.reshape(t_rows, 2 * N_STATES), xs3.reshape(t_rows, 2 * N_STATES),
                                      dyraw, n_l, "ssm_bwd_w")
    w_in_full = wing.transpose(1, 0, 2).reshape(D_MODEL, IN_WIDTH)
    dh1, g_mix_norm = mix_backward_act(dh2, h1, mix_norm, dproj, w_in_full, tm, "mix_bwd_act")
    dw['w_in'] = mix_backward_weights(hnm, dproj, n_l, "mix_bwd_w")
    token = start_reduce(['w_in'], "w_in")

    def group_blocks(part, channels_first):
        if channels_first:
            t = jnp.sum(part.reshape(N_SCAN_BLK, 2, LANES // 2, SCAN_COLS) * half[:, :, None, None], axis=1)
            t = t.reshape(N_SCAN_BLK, groups_per_blk, SSM_GROUP, groups_per_blk, SSM_STATE)
            t = jnp.sum(t * eye[None, :, None, :, None], axis=3)
            return t.reshape(SSM_GROUPS, SSM_GROUP, SSM_STATE)
        t = jnp.sum(part.reshape(N_SCAN_BLK, SCAN_COLS, 2, LANES // 2) * half[:, None, :, None], axis=2)
        t = t.reshape(N_SCAN_BLK, groups_per_blk, SSM_STATE, groups_per_blk, SSM_GROUP)
        t = jnp.sum(t * eye[None, :, None, :, None], axis=3)
        return t.reshape(SSM_GROUPS, SSM_STATE, SSM_GROUP).transpose(0, 2, 1)

    dbbr = group_blocks(d_bd[:, :, :SCAN_COLS], True).transpose(1, 0, 2).reshape(SSM_GROUP, N_STATES)
    dbbi = group_blocks(d_bd[:, :, SCAN_COLS:], True).transpose(1, 0, 2).reshape(SSM_GROUP, N_STATES)
    g_c_re = group_blocks(d_cd[:, :SCAN_COLS, :], False)[None]
    g_c_im = -group_blocks(d_cd[:, SCAN_COLS:, :], False)[None]
    group_sum = (jnp.arange(N_STATES)[:, None] // SSM_STATE == jnp.arange(LANES)[None, :]).astype(F32)
    g_ar, g_ai, g_ls, g_br, g_bi = ssm_param_backward(
        ar, ai, ls, br_t, bi_t, dlr_p.reshape(n_b * SUBLANES, N_STATES), dli_p.reshape(n_b * SUBLANES, N_STATES),
        dbbr, dbbi, group_sum, "ssm_bwd_params")
    g_sinks = sum_leading(dsink_p, "sink_sum")[0:1, :N_KV_HEADS * Q_PER_KV]

    small = {
        'mix_norm': g_mix_norm, 'ffn2_norm': g_ffn2_norm, 'final_norm': g_final.reshape(D_MODEL),
        'attn_sinks': g_sinks, 'ssm_a_re': g_ar.reshape(1, SSM_GROUPS, SSM_STATE), 'ssm_a_im': g_ai.reshape(1, SSM_GROUPS, SSM_STATE),
        'ssm_log_step': g_ls[:, :SSM_GROUPS],
        'ssm_b_re': g_br.reshape(SSM_GROUP, SSM_GROUPS, SSM_STATE).transpose(1, 2, 0)[None],
        'ssm_b_im': g_bi.reshape(SSM_GROUP, SSM_GROUPS, SSM_STATE).transpose(1, 2, 0)[None],
        'ssm_c_re': g_c_re, 'ssm_c_im': g_c_im, 'ssm_d': g_d,
    }
    early_small = [n for n in REPLICATED if n in small]
    sg_send, sg_recv, sg_src, sg_land, token = exchange_start(
        [_pack([small[n] for n in early_small] + [loss_part])], token, True, "ag_small_start")
    da1, db1, dh1_half = ffn_backward_hidden(dh1, a1, b1, ffn1_w[2], token, tm, "ffn1_bwd_hid")
    dw['ffn1_w1'], dw['ffn1_w3'], dw['ffn1_w2'] = blocked_ffn(
        *ffn_backward_weights(hn1, dh1_half, a1, b1, da1, db1, n_l, FF_BWD_COLS, "ffn1_bwd_w"))
    token = start_reduce(first_names, "ffn1")
    grad_x, meta_rows_grad, g_ffn1_norm = ffn_backward_input(
        dh1, h0, ffn1_norm, da1, db1, ffn1_w[0], ffn1_w[1], token, tm, "ffn1_bwd_in", examples=(n_b, seq))
    g_meta = sum_leading(meta_rows_grad, "meta_sum")

    grads, deltas, new_m, new_v = {}, {}, {}, {}

    def views(n):
        if n in ffn_names:
            return functools.partial(hidden_on_rows, n), functools.partial(hidden_on_rows_back, n)
        return (lambda t: t[0]), (lambda t: t[None])

    def finish_reduce(tag, previous):
        names, send, recv, srcs, lands = early[tag]
        srcs, lands = exchange_wait(send, recv, srcs, lands, previous, False, "rs_" + tag + "_wait")
        for n, g, land in zip(names, srcs, lands):
            two_d, back = views(n)
            out = adamw_exchanged(me_idx, g, land, two_d(w[n]), two_d(m[n]), two_d(v[n]), "adamw_" + n)
            grads[n], deltas[n], new_m[n], new_v[n] = (back(o) for o in out)
            previous = out[1]
        return previous

    previous = g_meta
    for tag in ("ffn2", "mix", "w_in"):
        previous = finish_reduce(tag, previous)

    zeros_meta = jnp.zeros((N_META, D_MODEL), F32)
    (late_parts,) = all_gather_list([_pack([g_ffn1_norm, g_meta])], previous, "ag_small_late")
    sg_src, (early_parts,) = exchange_wait(sg_send, sg_recv, sg_src, sg_land, late_parts, True, "ag_small_wait")
    early_parts = lax.dynamic_update_slice_in_dim(early_parts, sg_src[0][None], me, axis=0)

    def small_update(parts, names, extra, tag):
        pack_of = lambda d: _pack([d[n] for n in names] + extra)
        packed = adamw_small(parts, pack_of(w), pack_of(m), pack_of(v), "adamw_small_" + tag)
        unpacked = [_unpack(p, [w[n].shape for n in names] + [e.shape for e in extra]) for p in packed]
        for k, n in enumerate(names):
            grads[n], deltas[n], new_m[n], new_v[n] = (u[k] for u in unpacked)
        return packed, unpacked

    _, early_out = small_update(early_parts, early_small, [jnp.zeros_like(loss_part)], "early")
    loss = early_out[0][-1][0, 0]
    packed_out, unpacked = small_update(late_parts, ['ffn1_norm'], [zeros_meta], "late")
    g_meta_full = unpacked[0][-1]
    grads['meta_tokens'] = lax.dynamic_index_in_dim(
        g_meta_full.reshape(N_META, N_DEV, D_MODEL // N_DEV), me, axis=1, keepdims=False)
    deltas['meta_tokens'], new_m['meta_tokens'], new_v['meta_tokens'] = adamw_plain(
        grads['meta_tokens'], w['meta_tokens'], m['meta_tokens'], v['meta_tokens'], "adamw_meta")

    finish_reduce("ffn1", packed_out[0])

    return (loss, grad_x, *[grads[n] for n in WEIGHTS], *[deltas[n] for n in WEIGHTS],
            *[new_m[n] for n in WEIGHTS], *[new_v[n] for n in WEIGHTS])
```

```python
import functools

import jax
import jax.numpy as jnp
from jax import lax
from jax.experimental import pallas as pl
from jax.experimental.pallas import tpu as pltpu

F32 = jnp.float32
BF16 = jnp.bfloat16
MESH = pl.DeviceIdType.MESH

N_DEV = 8
D_MODEL = 1024
N_META = 16
HEAD_DIM = 64
N_KV_HEADS = 4
Q_PER_KV = 4
BLOCK = 128
KV_WIDTH = N_KV_HEADS * HEAD_DIM
SSM_GROUP = 16
SSM_WIDTH = 512
SSM_GROUPS = 32
SSM_STATE = 64
N_STATES = SSM_GROUPS * SSM_STATE
D_FF = 2816
FF_BLK = D_FF // N_DEV
IN_WIDTH = 4096
IN_BLK = IN_WIDTH // N_DEV
NORM_EPS = 1e-6
NEG_INF = -1e30
SCAN_COLS = 256
N_SCAN_BLK = N_STATES // SCAN_COLS
SUBLANES = 8
LANES = 128
MXU_WIDTH = 256
FF_BWD_COLS = MXU_WIDTH

ADAM_LR = 0.001
ADAM_B1 = 0.9
ADAM_B2 = 0.999
ADAM_EPS = 1e-08
ADAM_WD = 0.01
ADAM_STEP = 10

VMEM_BIG = 56 * 1024 * 1024


def _cp(sem=None, vmem=None):
    kw = {}
    if sem is not None:
        kw["dimension_semantics"] = sem
    if vmem is not None:
        kw["vmem_limit_bytes"] = vmem
    return pltpu.CompilerParams(**kw)


def _pcall(body, **kw):
    return pl.pallas_call(body, **kw)


def _dot(a, b):
    return jnp.dot(a, b, preferred_element_type=F32)


def _dot_nt(a, b):
    return lax.dot_general(a, b, (((1,), (1,)), ((), ())), preferred_element_type=F32)


def _dot_tn(a, b):
    return lax.dot_general(a, b, (((0,), (0,)), ((), ())), preferred_element_type=F32)


def _sigmoid(x):
    return 0.5 + 0.5 * jnp.tanh(0.5 * x)


def _row_tile(rows, cap):
    best = None
    for t in range(16, min(rows, cap) + 1, 16):
        if rows % t == 0:
            best = t
    assert best is not None, rows
    return best


def _my_place():
    return lax.axis_index("x"), lax.axis_index("y"), lax.axis_index("c")


def all_gather_list(shards, after, name):
    n = len(shards)

    def body(*refs):
        ins, outs = refs[:n], refs[n + 1:2 * n + 1]
        send_sems, recv_sems, local_sems = refs[2 * n + 1:]
        x, y, c = _my_place()
        me, sibling = (x, y, c), (x, y, 1 - c)
        chips = [(1 - x, y), (x, 1 - y), (1 - x, 1 - y)]

        def blk(a, px, py, pc):
            return outs[a].at[4 * px + 2 * py + pc]

        def copy(a, k, block, to, src=None):
            return pltpu.make_async_remote_copy(
                src_ref=blk(a, *block) if src is None else src, dst_ref=blk(a, *block),
                send_sem=send_sems.at[a * 7 + k], recv_sem=recv_sems.at[a * 7 + k],
                device_id=to, device_id_type=MESH)

        mine = [pltpu.make_async_copy(ins[a], blk(a, *me), local_sems.at[a]) for a in range(n)]
        for cp in mine:
            cp.start()
        first = []
        for a in range(n):
            first.append(copy(a, 0, me, sibling, src=ins[a]))
            first += [copy(a, 1 + j, me, (*chip, c), src=ins[a]) for j, chip in enumerate(chips)]
        for cp in first:
            cp.start()
        passed = []
        for j, chip in enumerate(chips):
            for a in range(n):
                copy(a, 1 + j, (*chip, c), me).wait_recv()
                cp = copy(a, 4 + j, (*chip, c), sibling)
                cp.start()
                passed.append(cp)
        for a in range(n):
            copy(a, 0, sibling, me).wait_recv()
            for j, chip in enumerate(chips):
                copy(a, 4 + j, (*chip, 1 - c), me).wait_recv()
        for cp in first + passed:
            cp.wait_send()
        for cp in mine:
            cp.wait()

    any_spec = pl.BlockSpec(memory_space=pl.ANY)
    return _pcall(
        body, name=name,
        out_shape=[jax.ShapeDtypeStruct((N_DEV,) + s.shape, s.dtype) for s in shards],
        in_specs=[any_spec] * (n + 1), out_specs=[any_spec] * n,
        scratch_shapes=[pltpu.SemaphoreType.DMA((7 * n,)), pltpu.SemaphoreType.DMA((7 * n,)),
                        pltpu.SemaphoreType.DMA((n,))],
    )(*shards, after)


HBM_SPEC = pl.BlockSpec(memory_space=pltpu.HBM)
SEM_SPEC = pl.BlockSpec(memory_space=pltpu.SEMAPHORE)
N_PEERS = N_DEV - 1


def _related(k):
    x, y, c = _my_place()
    px = 1 - x if k & 4 else x
    py = 1 - y if k & 2 else y
    pc = 1 - c if k & 1 else c
    return (px, py, pc), 4 * px + 2 * py + pc


def _exchange_copies(srcs, lands, send_sems, recv_sems, gather):
    x, y, c = _my_place()
    me = 4 * x + 2 * y + c
    copies = []
    for a, (src, land) in enumerate(zip(srcs, lands)):
        for k in range(1, N_DEV):
            peer, d = _related(k)
            copies.append(pltpu.make_async_remote_copy(
                src_ref=src if gather else src.at[d], dst_ref=land.at[me] if gather else land.at[k],
                send_sem=send_sems.at[a * N_PEERS + k - 1], recv_sem=recv_sems.at[a * N_PEERS + k - 1],
                device_id=peer, device_id_type=MESH))
    return copies


def exchange_start(srcs, after, gather, name):
    n = len(srcs)
    land_shapes = [((N_DEV,) + s.shape) if gather else s.shape for s in srcs]

    def body(*refs):
        send_sems, recv_sems = refs[2 * n + 1], refs[2 * n + 2]
        for cp in _exchange_copies(refs[:n], refs[n:2 * n], send_sems, recv_sems, gather):
            cp.start()
        token = refs[-1]
        token[...] = jnp.zeros_like(token)

    sems = pltpu.SemaphoreType.DMA((n * N_PEERS,))
    lands = [pltpu.with_memory_space_constraint(lax.empty(shape, s.dtype), pltpu.HBM) for shape, s in zip(land_shapes, srcs)]
    out = _pcall(
        body, name=name,
        out_shape=(sems, sems, *[pltpu.HBM(s.shape, s.dtype) for s in srcs],
                   *[pltpu.HBM(shape, s.dtype) for shape, s in zip(land_shapes, srcs)],
                   jax.ShapeDtypeStruct((SUBLANES, LANES), F32)),
        in_specs=[HBM_SPEC] * (2 * n) + [pl.BlockSpec(memory_space=pl.ANY)],
        out_specs=(SEM_SPEC, SEM_SPEC, *[HBM_SPEC] * (2 * n), pl.BlockSpec(memory_space=pltpu.VMEM)),
        input_output_aliases={i: 2 + i for i in range(2 * n)},
        compiler_params=pltpu.CompilerParams(has_side_effects=pltpu.SideEffectType.DATAFLOW_SIDE_EFFECTING),
    )(*[pltpu.with_memory_space_constraint(s, pltpu.HBM) for s in srcs], *lands, after)
    return out[0], out[1], list(out[2:2 + n]), list(out[2 + n:2 + 2 * n]), out[-1]


def exchange_wait(send_sems, recv_sems, srcs, lands, after, gather, name):
    n = len(srcs)

    def body(*refs):
        for cp in _exchange_copies(refs[:n], refs[n:2 * n], refs[2 * n], refs[2 * n + 1], gather):
            cp.wait_send()
            cp.wait_recv()

    out = _pcall(
        body, name=name,
        out_shape=(*[pltpu.HBM(s.shape, s.dtype) for s in srcs], *[pltpu.HBM(z.shape, z.dtype) for z in lands]),
        in_specs=[HBM_SPEC] * (2 * n) + [SEM_SPEC, SEM_SPEC, pl.BlockSpec(memory_space=pl.ANY)],
        out_specs=tuple([HBM_SPEC] * (2 * n)),
        input_output_aliases={i: i for i in range(2 * n)},
        compiler_params=pltpu.CompilerParams(has_side_effects=pltpu.SideEffectType.DATAFLOW_SIDE_EFFECTING),
    )(*srcs, *lands, send_sems, recv_sems, after)
    return list(out[:n]), list(out[n:])


def adamw_exchanged(me, g, land, w, m, v, name):
    rows, cols = w.shape
    tr = _row_tile(rows, 256)

    def body(me_ref, g_ref, land_ref, w_ref, m_ref, v_ref, go_ref, d_ref, mo_ref, vo_ref):
        grad = g_ref[...].astype(F32)
        for k in range(1, N_DEV):
            grad = grad + land_ref[k].astype(F32)
        delta, m_new, v_new = _adam_math(w_ref[...], grad, m_ref[...], v_ref[...])
        go_ref[...] = grad
        d_ref[...] = delta
        mo_ref[...] = m_new
        vo_ref[...] = v_new

    tile = pl.BlockSpec((tr, cols), lambda r, ix: (r, 0))
    out = jax.ShapeDtypeStruct((rows, cols), F32)
    return _pcall(
        body, name=name, out_shape=[out] * 4,
        grid_spec=pltpu.PrefetchScalarGridSpec(
            num_scalar_prefetch=1, grid=(rows // tr,),
            in_specs=[pl.BlockSpec((None, tr, cols), lambda r, ix: (ix[0], r, 0)),
                      pl.BlockSpec((N_DEV, tr, cols), lambda r, ix: (0, r, 0)), tile, tile, tile],
            out_specs=[tile] * 4),
        compiler_params=_cp(("arbitrary",)),
    )(me, g, land, w, m, v)


def _adam_math(w, g, m, v):
    m = ADAM_B1 * m + (1.0 - ADAM_B1) * g
    v = ADAM_B2 * v + (1.0 - ADAM_B2) * (g * g)
    m_hat = m / (1.0 - ADAM_B1 ** ADAM_STEP)
    v_hat = v / (1.0 - ADAM_B2 ** ADAM_STEP)
    delta = -ADAM_LR * (m_hat / (jnp.sqrt(v_hat) + ADAM_EPS) + ADAM_WD * w)
    return delta, m, v


def adamw_small(parts, w, m, v, name):
    _, rows, cols = parts.shape

    def body(p_ref, w_ref, m_ref, v_ref, go_ref, d_ref, mo_ref, vo_ref):
        grad = p_ref[0]
        for k in range(1, N_DEV):
            grad = grad + p_ref[k]
        delta, m_new, v_new = _adam_math(w_ref[...], grad, m_ref[...], v_ref[...])
        go_ref[...] = grad
        d_ref[...] = delta
        mo_ref[...] = m_new
        vo_ref[...] = v_new

    out = jax.ShapeDtypeStruct((rows, cols), F32)
    return _pcall(body, name=name, out_shape=[out] * 4, compiler_params=_cp(vmem=VMEM_BIG))(parts, w, m, v)


def adamw_plain(g, w, m, v, name):
    def body(g_ref, w_ref, m_ref, v_ref, d_ref, mo_ref, vo_ref):
        delta, m_new, v_new = _adam_math(w_ref[...], g_ref[...], m_ref[...], v_ref[...])
        d_ref[...] = delta
        mo_ref[...] = m_new
        vo_ref[...] = v_new

    out = jax.ShapeDtypeStruct(w.shape, F32)
    return _pcall(body, name=name, out_shape=[out] * 3)(g, w, m, v)


def _rms_fwd(x, g):
    r = lax.rsqrt(jnp.mean(x * x, axis=-1, keepdims=True) + NORM_EPS)
    return x * r * g


def _rms_bwd(x, g, dy):
    r = lax.rsqrt(jnp.mean(x * x, axis=-1, keepdims=True) + NORM_EPS)
    xh = x * r
    t = dy * g
    dx = r * (t - xh * jnp.mean(t * xh, axis=-1, keepdims=True))
    return dx, jnp.sum(dy * xh, axis=0, keepdims=True)


def _accumulate(ref, val, first):
    @pl.when(first)
    def _():
        ref[...] = val

    @pl.when(jnp.logical_not(first))
    def _():
        ref[...] += val


def _col_chunks(width):
    return [(c0, min(MXU_WIDTH, width - c0)) for c0 in range(0, width, MXU_WIDTH)]


ANY_SPEC = pl.BlockSpec(memory_space=pl.ANY)


def ffn_forward(h, norm, w1, w3, w2, after, tm, name, meta=None):
    if meta is None:
        t_rows = h.shape[0]
        h_spec = pl.BlockSpec((tm, D_MODEL), lambda i: (i, 0))
    else:
        tiles = (h.shape[1] + N_META) // tm
        t_rows = h.shape[0] * tiles * tm
        h_spec = pl.BlockSpec((None, tm, D_MODEL), lambda i: (i // tiles, i % tiles, 0))

    split = w2 is None

    def body(h_ref, g_ref, w1_ref, w3_ref, *rest):
        w2_ref = None if split else rest[0]
        rest = rest[1:] if split else rest[2:]
        if meta is None:
            out_ref, hn_ref, a_ref, b_ref = rest[:4]
            h_in = h_ref[...]
        else:
            meta_ref, out_ref, hn_ref, a_ref, b_ref, h0_ref = rest[:6]
            h_in = h_ref[...]
            with_meta = jnp.concatenate([h_in[:tm - N_META], meta_ref[...]], axis=0)
            h_in = jnp.where(pl.program_id(0) % tiles == tiles - 1, with_meta, h_in)
            h0_ref[...] = h_in
        hn = _rms_fwd(h_in, g_ref[...]).astype(BF16)
        hn_ref[...] = hn
        hid_ref = out_ref if split else rest[-1]
        for c0, cw in _col_chunks(D_FF):
            a = _dot_nt(hn, w1_ref[c0:c0 + cw, :])
            b = _dot_nt(hn, w3_ref[c0:c0 + cw, :])
            a_ref[:, c0:c0 + cw] = a.astype(BF16)
            b_ref[:, c0:c0 + cw] = b.astype(BF16)
            hid_ref[:, c0:c0 + cw] = (a * _sigmoid(a) * b).astype(BF16)
        if not split:
            out_ref[...] = h_in + 0.5 * _dot(hid_ref[...], w2_ref[...])

    row = pl.BlockSpec((tm, D_MODEL), lambda i: (i, 0))
    hid_blk = pl.BlockSpec((tm, D_FF), lambda i: (i, 0))
    weight = _resident((D_FF, D_MODEL))
    wide = jax.ShapeDtypeStruct((t_rows, D_MODEL), F32)
    hid_shape = jax.ShapeDtypeStruct((t_rows, D_FF), BF16)
    extra_in = [] if meta is None else [meta]
    return _pcall(
        body, name=name, grid=(t_rows // tm,),
        in_specs=[h_spec, pl.BlockSpec((1, D_MODEL), lambda i: (0, 0))] + [weight] * (2 if split else 3) + [ANY_SPEC]
        + [pl.BlockSpec((N_META, D_MODEL), lambda i: (0, 0))] * len(extra_in),
        out_specs=[hid_blk if split else row, row, hid_blk, hid_blk] + [row] * len(extra_in),
        out_shape=[hid_shape if split else wide, jax.ShapeDtypeStruct((t_rows, D_MODEL), BF16), hid_shape, hid_shape]
        + [wide] * len(extra_in),
        scratch_shapes=[] if split else [pltpu.VMEM((tm, D_FF), BF16)],
        compiler_params=_cp(("arbitrary",), VMEM_BIG),
    )(h, norm, w1, w3, *([] if split else [w2]), after, *extra_in)


def ffn_down(h, hid, w2, tm, name):
    t_rows = h.shape[0]

    def body(h_ref, hid_ref, w2_ref, out_ref):
        out_ref[...] = h_ref[...] + 0.5 * _dot(hid_ref[...], w2_ref[...])

    row = pl.BlockSpec((tm, D_MODEL), lambda i: (i, 0))
    return _pcall(
        body, name=name, grid=(t_rows // tm,),
        in_specs=[row, pl.BlockSpec((tm, D_FF), lambda i: (i, 0)), _resident((D_FF, D_MODEL))],
        out_specs=[row],
        out_shape=[jax.ShapeDtypeStruct((t_rows, D_MODEL), F32)],
        compiler_params=_cp(("arbitrary",), VMEM_BIG),
    )(h, hid, w2)


def _resident(shape):
    return pl.BlockSpec(shape, lambda *_: (0,) * len(shape), pipeline_mode=pl.Buffered(1))


def ffn_backward_hidden(dh, a, b, w2, after, tm, name):
    t_rows = dh.shape[0]

    def body(dh_ref, a_ref, b_ref, w2_ref, _, da_ref, db_ref, dhb_ref):
        dhb = (0.5 * dh_ref[...]).astype(BF16)
        dhb_ref[...] = dhb
        for c0, cw in _col_chunks(D_FF):
            dhid = _dot_nt(dhb, w2_ref[c0:c0 + cw, :])
            av = a_ref[:, c0:c0 + cw].astype(F32)
            bv = b_ref[:, c0:c0 + cw].astype(F32)
            s = _sigmoid(av)
            da_ref[:, c0:c0 + cw] = (dhid * bv * (s * (1.0 + av * (1.0 - s)))).astype(BF16)
            db_ref[:, c0:c0 + cw] = (dhid * (av * s)).astype(BF16)

    hid = pl.BlockSpec((tm, D_FF), lambda i: (i, 0))
    row = pl.BlockSpec((tm, D_MODEL), lambda i: (i, 0))
    return _pcall(
        body, name=name, grid=(t_rows // tm,),
        in_specs=[row, hid, hid, _resident((D_FF, D_MODEL)), ANY_SPEC],
        out_specs=[hid, hid, row],
        out_shape=[jax.ShapeDtypeStruct((t_rows, D_FF), BF16), jax.ShapeDtypeStruct((t_rows, D_FF), BF16),
                   jax.ShapeDtypeStruct((t_rows, D_MODEL), BF16)],
        compiler_params=_cp(("arbitrary",), VMEM_BIG),
    )(dh, a, b, w2, after)


def ffn_backward_input(dh, h, norm, da, db, w1, w3, after, tm, name, examples=None):
    t_rows = h.shape[0]

    def body(dh_ref, h_ref, g_ref, da_ref, db_ref, w1_ref, w3_ref, _, dhin_ref, *rest):
        dg_ref = rest[-1]
        dhn = _dot(da_ref[...], w1_ref[...]) + _dot(db_ref[...], w3_ref[...])
        dx, dg = _rms_bwd(h_ref[...], g_ref[...], dhn)
        dhin = dh_ref[...] + dx
        dhin_ref[...] = dhin
        _accumulate(dg_ref, dg, pl.program_id(0) == 0)
        if examples is not None:
            @pl.when(pl.program_id(0) % tiles == tiles - 1)
            def _():
                rest[0][...] = dhin[tm - N_META:, :]

    row = pl.BlockSpec((tm, D_MODEL), lambda i: (i, 0))
    vec = pl.BlockSpec((1, D_MODEL), lambda i: (0, 0))
    hid = pl.BlockSpec((tm, D_FF), lambda i: (i, 0))
    if examples is None:
        out_specs = [row, vec]
        out_shape = [jax.ShapeDtypeStruct((t_rows, D_MODEL), F32), jax.ShapeDtypeStruct((1, D_MODEL), F32)]
    else:
        n_b, seq = examples
        tiles = (seq + N_META) // tm
        out_specs = [pl.BlockSpec((None, tm, D_MODEL), lambda i: (i // tiles, i % tiles, 0)),
                     pl.BlockSpec((None, N_META, D_MODEL), lambda i: (i // tiles, 0, 0)), vec]
        out_shape = [jax.ShapeDtypeStruct((n_b, seq, D_MODEL), F32), jax.ShapeDtypeStruct((n_b, N_META, D_MODEL), F32),
                     jax.ShapeDtypeStruct((1, D_MODEL), F32)]
    return _pcall(
        body, name=name, grid=(t_rows // tm,),
        in_specs=[row, row, vec, hid, hid, _resident((D_FF, D_MODEL)), _resident((D_FF, D_MODEL)), ANY_SPEC],
        out_specs=out_specs, out_shape=out_shape,
        compiler_params=_cp(("arbitrary",), VMEM_BIG),
    )(dh, h, norm, da, db, w1, w3, after)


def ffn_backward_weights(hn, dh, a, b, da, db, tm, tn, name):
    t_rows = hn.shape[0]
    ni = t_rows // tm
    kc = _row_tile(tm, 688)

    def body(hn_ref, dh_ref, a_ref, b_ref, da_ref, db_ref, dw1_ref, dw3_ref, dw2_ref, acc1, acc3, acc2):
        i = pl.program_id(1)
        parts = None
        for r0 in range(0, tm, kc):
            rows = slice(r0, r0 + kc)
            hn_v = hn_ref[rows, :]
            av = a_ref[rows, :].astype(F32)
            hid = (av * _sigmoid(av) * b_ref[rows, :].astype(F32)).astype(BF16)
            new = (_dot_tn(hn_v, da_ref[rows, :]), _dot_tn(hn_v, db_ref[rows, :]), _dot_tn(dh_ref[rows, :], hid))
            parts = new if parts is None else tuple(p + q for p, q in zip(parts, new))
        _accumulate(acc1, parts[0], i == 0)
        _accumulate(acc3, parts[1], i == 0)
        _accumulate(acc2, parts[2], i == 0)

        @pl.when(i == ni - 1)
        def _():
            dw1_ref[...] = acc1[...].T.astype(BF16)
            dw3_ref[...] = acc3[...].T.astype(BF16)
            dw2_ref[...] = acc2[...].T.astype(BF16)

    row = pl.BlockSpec((tm, D_MODEL), lambda j, i: (i, 0))
    hid_blk = pl.BlockSpec((tm, tn), lambda j, i: (i, j))
    w_row = pl.BlockSpec((tn, D_MODEL), lambda j, i: (j, 0))
    out = jax.ShapeDtypeStruct((D_FF, D_MODEL), BF16)
    return _pcall(
        body, name=name, grid=(D_FF // tn, ni),
        in_specs=[row, row, hid_blk, hid_blk, hid_blk, hid_blk],
        out_specs=[w_row, w_row, w_row], out_shape=[out, out, out],
        scratch_shapes=[pltpu.VMEM((D_MODEL, tn), F32)] * 3,
        compiler_params=_cp(("arbitrary", "arbitrary"), VMEM_BIG),
    )(hn, dh, a, b, da, db)


def mix_forward(h, norm, wing, tm, name):
    t_rows = h.shape[0]

    def body(h_ref, g_ref, w_ref, hn_ref, p_ref):
        hn = _rms_fwd(h_ref[...], g_ref[...]).astype(BF16)
        hn_ref[...] = hn
        for j in range(N_DEV):
            p_ref[:, j * IN_BLK:(j + 1) * IN_BLK] = _dot(hn, w_ref[j]).astype(BF16)

    row = pl.BlockSpec((tm, D_MODEL), lambda i: (i, 0))
    return _pcall(
        body, name=name, grid=(t_rows // tm,),
        in_specs=[row, pl.BlockSpec((1, D_MODEL), lambda i: (0, 0)),
                  pl.BlockSpec((N_DEV, D_MODEL, IN_BLK), lambda i: (0, 0, 0))],
        out_specs=[row, pl.BlockSpec((tm, IN_WIDTH), lambda i: (i, 0))],
        out_shape=[jax.ShapeDtypeStruct((t_rows, D_MODEL), BF16), jax.ShapeDtypeStruct((t_rows, IN_WIDTH), BF16)],
        compiler_params=_cp(("arbitrary",), VMEM_BIG),
    )(h, norm, wing)


def mix_backward_act(dh, h, norm, dproj, w_in_full, tm, name):
    t_rows = h.shape[0]

    def body(dh_ref, h_ref, g_ref, dp_ref, w_ref, dhin_ref, dg_ref):
        dx, dg = _rms_bwd(h_ref[...], g_ref[...], _dot_nt(dp_ref[...], w_ref[...]))
        dhin_ref[...] = dh_ref[...] + dx
        _accumulate(dg_ref, dg, pl.program_id(0) == 0)

    row = pl.BlockSpec((tm, D_MODEL), lambda i: (i, 0))
    vec = pl.BlockSpec((1, D_MODEL), lambda i: (0, 0))
    return _pcall(
        body, name=name, grid=(t_rows // tm,),
        in_specs=[row, row, vec, pl.BlockSpec((tm, IN_WIDTH), lambda i: (i, 0)), _resident((D_MODEL, IN_WIDTH))],
        out_specs=[row, vec],
        out_shape=[jax.ShapeDtypeStruct((t_rows, D_MODEL), F32), jax.ShapeDtypeStruct((1, D_MODEL), F32)],
        compiler_params=_cp(("arbitrary",), VMEM_BIG),
    )(dh, h, norm, dproj, w_in_full)


def mix_backward_weights(hn, dproj, tm, name):
    t_rows = hn.shape[0]
    ni = t_rows // tm
    per_step = 2

    kc = _row_tile(tm, 688)

    def body(hn_ref, dp_ref, dw_ref, acc):
        i = pl.program_id(1)
        part = functools.reduce(lambda u, w: u + w, [_dot_tn(hn_ref[r0:r0 + kc, :], dp_ref[r0:r0 + kc, :])
                                                    for r0 in range(0, tm, kc)])
        _accumulate(acc, part, i == 0)

        @pl.when(i == ni - 1)
        def _():
            for k in range(per_step):
                dw_ref[k] = acc[:, k * IN_BLK:(k + 1) * IN_BLK].astype(BF16)

    return _pcall(
        body, name=name, grid=(N_DEV // per_step, ni),
        in_specs=[pl.BlockSpec((tm, D_MODEL), lambda j, i: (i, 0)),
                  pl.BlockSpec((tm, per_step * IN_BLK), lambda j, i: (i, j))],
        out_specs=pl.BlockSpec((per_step, D_MODEL, IN_BLK), lambda j, i: (j, 0, 0)),
        out_shape=jax.ShapeDtypeStruct((N_DEV, D_MODEL, IN_BLK), BF16),
        scratch_shapes=[pltpu.VMEM((D_MODEL, per_step * IN_BLK), F32)],
        compiler_params=_cp(("arbitrary", "arbitrary"), VMEM_BIG),
    )(hn, dproj)


GELU_C = 0.7978845608028654
GELU_K = 0.044715


def _gelu(x):
    return 0.5 * x * (1.0 + jnp.tanh(GELU_C * (x + GELU_K * (x * x * x))))


def _gelu_and_grad(x):
    th = jnp.tanh(GELU_C * (x + GELU_K * (x * x * x)))
    val = 0.5 * x * (1.0 + th)
    grad = 0.5 * (1.0 + th) + 0.5 * x * (1.0 - th * th) * (GELU_C * (1.0 + 3.0 * GELU_K * (x * x)))
    return val, grad


def merge_forward(h, yraw, attn, proj, glu_a, glu_b, w_out, tm, name):
    t_rows = h.shape[0]

    def body(h_ref, y_ref, at_ref, gate_ref, a_ref, b_ref, wo_ref, out_ref):
        y = _gelu(y_ref[...]).astype(BF16)
        ssm = _dot(y, a_ref[...]) * _sigmoid(_dot(y, b_ref[...]))
        ga = gate_ref[:, :D_MODEL].astype(F32)
        gs = gate_ref[:, D_MODEL:].astype(F32)
        merged = _sigmoid(ga) * at_ref[...].astype(F32) + _sigmoid(gs) * ssm
        out_ref[...] = h_ref[...] + _dot(merged.astype(BF16), wo_ref[...])

    row = pl.BlockSpec((tm, D_MODEL), lambda i: (i, 0))
    glu = pl.BlockSpec((SSM_WIDTH, D_MODEL), lambda i: (0, 0))
    return _pcall(
        body, name=name, grid=(t_rows // tm,),
        in_specs=[row, pl.BlockSpec((tm, SSM_WIDTH), lambda i: (i, 0)), row,
                  pl.BlockSpec((tm, 2 * D_MODEL), lambda i: (i, 1)), glu, glu,
                  pl.BlockSpec((D_MODEL, D_MODEL), lambda i: (0, 0))],
        out_specs=row, out_shape=jax.ShapeDtypeStruct((t_rows, D_MODEL), F32),
        compiler_params=_cp(("arbitrary",), VMEM_BIG),
    )(h, yraw, attn, proj, glu_a, glu_b, w_out)


def merge_backward(dh, yraw, attn, proj, glu_a, glu_b, w_out, after, tm, name):
    t_rows = dh.shape[0]

    def body(dh_ref, y_ref, at_ref, gate_ref, a_ref, b_ref, wo_ref, _,
             dat_ref, dy_ref, dgate_ref, d16_ref, mg_ref, y16_ref, dya_ref, dyb_ref):
        d16 = dh_ref[...].astype(BF16)
        d16_ref[...] = d16
        gel, dgel = _gelu_and_grad(y_ref[...].astype(F32))
        y16 = gel.astype(BF16)
        y16_ref[...] = y16
        dy = None
        for c0, cw in _col_chunks(D_MODEL):
            cols = slice(c0, c0 + cw)
            gcols = slice(D_MODEL + c0, D_MODEL + c0 + cw)
            dmerged = _dot_nt(d16, wo_ref[cols, :])
            ya = _dot(y16, a_ref[:, cols])
            sb = _sigmoid(_dot(y16, b_ref[:, cols]))
            ssm = ya * sb
            sa = _sigmoid(gate_ref[:, cols].astype(F32))
            ss = _sigmoid(gate_ref[:, gcols].astype(F32))
            attn_v = at_ref[:, cols].astype(F32)
            mg_ref[:, cols] = (sa * attn_v + ss * ssm).astype(BF16)
            dat_ref[:, cols] = (dmerged * sa).astype(BF16)
            dgate_ref[:, cols] = (dmerged * attn_v * sa * (1.0 - sa)).astype(BF16)
            dgate_ref[:, gcols] = (dmerged * ssm * ss * (1.0 - ss)).astype(BF16)
            dssm = dmerged * ss
            dya = (dssm * sb).astype(BF16)
            dyb = (dssm * ya * sb * (1.0 - sb)).astype(BF16)
            dya_ref[:, cols] = dya
            dyb_ref[:, cols] = dyb
            part = _dot_nt(dya, a_ref[:, cols]) + _dot_nt(dyb, b_ref[:, cols])
            dy = part if dy is None else dy + part
        dy_ref[...] = (dy * dgel).astype(BF16)

    row = pl.BlockSpec((tm, D_MODEL), lambda i: (i, 0))
    ssm_row = pl.BlockSpec((tm, SSM_WIDTH), lambda i: (i, 0))
    gates = pl.BlockSpec((tm, 2 * D_MODEL), lambda i: (i, 1))
    wide = jax.ShapeDtypeStruct((t_rows, D_MODEL), BF16)
    narrow = jax.ShapeDtypeStruct((t_rows, SSM_WIDTH), BF16)
    return _pcall(
        body, name=name, grid=(t_rows // tm,),
        in_specs=[row, ssm_row, row, gates, _resident((SSM_WIDTH, D_MODEL)), _resident((SSM_WIDTH, D_MODEL)),
                  _resident((D_MODEL, D_MODEL)), ANY_SPEC],
        out_specs=[row, ssm_row, gates, row, row, ssm_row, row, row],
        out_shape=[wide, narrow, jax.ShapeDtypeStruct((t_rows, IN_WIDTH), BF16), wide, wide, narrow, wide, wide],
        compiler_params=_cp(("arbitrary",), VMEM_BIG),
    )(dh, yraw, attn, proj, glu_a, glu_b, w_out, after)


def merge_backward_weights(d16, merged, y16, dya, dyb, tm, name):
    t_rows = d16.shape[0]

    def body(d_ref, mg_ref, y_ref, dya_ref, dyb_ref, dwo_ref, da_ref, db_ref):
        first = pl.program_id(0) == 0
        y16 = y_ref[...]
        _accumulate(dwo_ref, _dot_tn(mg_ref[...], d_ref[...]), first)
        _accumulate(da_ref, _dot_tn(y16, dya_ref[...]), first)
        _accumulate(db_ref, _dot_tn(y16, dyb_ref[...]), first)

    row = pl.BlockSpec((tm, D_MODEL), lambda i: (i, 0))
    ssm_row = pl.BlockSpec((tm, SSM_WIDTH), lambda i: (i, 0))
    glu = pl.BlockSpec((SSM_WIDTH, D_MODEL), lambda i: (0, 0))
    wo = pl.BlockSpec((D_MODEL, D_MODEL), lambda i: (0, 0))
    return _pcall(
        body, name=name, grid=(t_rows // tm,),
        in_specs=[row, row, ssm_row, row, row], out_specs=[wo, glu, glu],
        out_shape=[jax.ShapeDtypeStruct((D_MODEL, D_MODEL), F32), jax.ShapeDtypeStruct((SSM_WIDTH, D_MODEL), F32),
                   jax.ShapeDtypeStruct((SSM_WIDTH, D_MODEL), F32)],
        compiler_params=_cp(("arbitrary",), VMEM_BIG),
    )(d16, merged, y16, dya, dyb)


def final_loss_backward(h, target, norm, seq, tm, name):
    t_rows = h.shape[0]
    tiles_per_example = (seq + N_META) // tm

    def body(h_ref, t_ref, g_ref, dh_ref, loss_ref, dg_ref):
        i = pl.program_id(0)
        x = h_ref[...]
        g = g_ref[...]
        r = lax.rsqrt(jnp.mean(x * x, axis=-1, keepdims=True) + NORM_EPS)
        xh = x * r
        pos = lax.broadcasted_iota(jnp.int32, (tm, 1), 0) + (i % tiles_per_example) * tm
        diff = jnp.where(pos < seq, xh * g - t_ref[...], 0.0)
        part = 0.5 * jnp.sum(jnp.sum(diff * diff, axis=-1, keepdims=True), axis=0, keepdims=True) / D_MODEL
        dy = diff / D_MODEL
        t = dy * g
        dh_ref[...] = r * (t - xh * jnp.mean(t * xh, axis=-1, keepdims=True))
        _accumulate(loss_ref, jnp.broadcast_to(part, (1, LANES)), i == 0)
        _accumulate(dg_ref, jnp.sum(dy * xh, axis=0, keepdims=True), i == 0)

    row = pl.BlockSpec((tm, D_MODEL), lambda i: (i, 0))
    vec = pl.BlockSpec((1, D_MODEL), lambda i: (0, 0))
    per_example = pl.BlockSpec((None, tm, D_MODEL), lambda i: (i // tiles_per_example, i % tiles_per_example, 0))
    return _pcall(
        body, name=name, grid=(t_rows // tm,),
        in_specs=[row, per_example, vec],
        out_specs=[row, pl.BlockSpec((1, LANES), lambda i: (0, 0)), vec],
        out_shape=[jax.ShapeDtypeStruct((t_rows, D_MODEL), F32), jax.ShapeDtypeStruct((1, LANES), F32),
                   jax.ShapeDtypeStruct((1, D_MODEL), F32)],
        compiler_params=_cp(("arbitrary",), VMEM_BIG),
    )(h, target, norm)


ATTN_SCALE = HEAD_DIM ** -0.5
STACK_HEADS = (0, 2, 1, 3)
META_PAD = LANES - N_META


def _lane_half(shape, hf):
    lane = lax.broadcasted_iota(jnp.int32, shape, 1)
    return (lane < HEAD_DIM) if hf == 0 else (lane >= HEAD_DIM)


def _kv_variants(ref, rows, kh, pad_rows=0):
    tile = kh // 2
    t = ref[rows, tile * LANES:(tile + 1) * LANES].astype(F32)
    swapped = pltpu.roll(t, HEAD_DIM, axis=1)
    at_low, at_high = (t, swapped) if kh % 2 == 0 else (swapped, t)
    lo = jnp.where(_lane_half(t.shape, 0), at_low, 0.0).astype(BF16)
    hi = jnp.where(_lane_half(t.shape, 1), at_high, 0.0).astype(BF16)
    if pad_rows:
        zeros = jnp.zeros((pad_rows, LANES), BF16)
        lo, hi = jnp.concatenate([lo, zeros], axis=0), jnp.concatenate([hi, zeros], axis=0)
    return lo, hi


def _key_tiles(ref, key_rows, kh):
    return [_kv_variants(ref, r, kh, META_PAD if i == len(key_rows) - 1 else 0) for i, r in enumerate(key_rows)]


def _to_kv_lanes(lo, hi, kh):
    lo = jnp.where(_lane_half(lo.shape, 0), lo, 0.0)
    hi = jnp.where(_lane_half(hi.shape, 1), hi, 0.0)
    if kh % 2 == 0:
        return lo + pltpu.roll(hi, HEAD_DIM, axis=1)
    return pltpu.roll(lo, HEAD_DIM, axis=1) + hi


def _stacked(ref, rows, kh):
    col = kh * 2 * LANES
    return jnp.concatenate([ref[rows, col:col + LANES], ref[rows, col + LANES:col + 2 * LANES]], axis=0)


def _sink_column(sink_ref, kh, nq):
    row = lax.broadcasted_iota(jnp.int32, (4 * nq, 1), 0)
    col = jnp.zeros((4 * nq, 1), F32)
    for quarter, g in enumerate(STACK_HEADS):
        col = jnp.where(row // nq == quarter, sink_ref[0, kh * Q_PER_KV + g], col)
    return col


def _softmax_parts(qs, key_tiles, masks, sink):
    scores = []
    for (k_lo, k_hi), mask in zip(key_tiles, masks):
        s = jnp.concatenate([_dot_nt(qs, k_lo), _dot_nt(qs, k_hi)], axis=0) * ATTN_SCALE
        scores.append(s if mask is None else jnp.where(mask, s, NEG_INF))
    m = jnp.maximum(_row_reduce(scores, jnp.maximum, jnp.max), sink)
    probs = [jnp.exp(s - m) for s in scores]
    e_sink = jnp.exp(sink - m)
    den = _row_sums(probs) + e_sink
    return probs, 1.0 / den, e_sink


def _row_reduce(tiles, combine, reduce):
    chunks = [t[:, c:c + LANES] for t in tiles for c in range(0, t.shape[-1], LANES)]
    return reduce(functools.reduce(combine, chunks), axis=-1, keepdims=True)


def _row_sums(tiles):
    return _row_reduce(tiles, lambda u, w: u + w, jnp.sum)


def _band_mask(nq, first):
    keys = BLOCK if first else 2 * BLOCK
    qi = lax.broadcasted_iota(jnp.int32, (4 * nq, keys), 0) % nq
    kj = lax.broadcasted_iota(jnp.int32, (4 * nq, keys), 1)
    if first:
        return kj <= qi
    return jnp.logical_and(kj > qi, kj <= qi + BLOCK)


def _meta_mask(nq, causal):
    qi = lax.broadcasted_iota(jnp.int32, (4 * nq, LANES), 0) % nq
    kj = lax.broadcasted_iota(jnp.int32, (4 * nq, LANES), 1)
    return jnp.logical_and(kj < N_META, kj <= qi) if causal else kj < N_META


def _attention_schedule(seq, queries, carry):
    meta_rows = pl.ds(seq, N_META)
    meta_ok = _meta_mask(BLOCK, False)
    carry = queries(pl.ds(0, BLOCK), BLOCK, [pl.ds(0, BLOCK), meta_rows], [_band_mask(BLOCK, True), meta_ok], carry)

    def block(n, c):
        r0 = pl.multiple_of(n * BLOCK, BLOCK)
        p0 = pl.multiple_of((n - 1) * BLOCK, BLOCK)
        return queries(pl.ds(r0, BLOCK), BLOCK, [pl.ds(p0, 2 * BLOCK), meta_rows], [_band_mask(BLOCK, False), meta_ok], c)

    carry = lax.fori_loop(1, seq // BLOCK, block, carry)
    return queries(meta_rows, N_META, [meta_rows], [_meta_mask(N_META, True)], carry)


def attention_forward(proj3, sinks, seq, name):
    n_b, n_l, _ = proj3.shape

    def body(sink_ref, q_ref, k_ref, v_ref, o_ref):
        def queries(q_rows, nq, key_rows, masks, carry):
            for kh in range(N_KV_HEADS):
                ks = _key_tiles(k_ref, key_rows, kh)
                vs = _key_tiles(v_ref, key_rows, kh)
                qs = _stacked(q_ref, q_rows, kh)
                probs, inv, _ = _softmax_parts(qs, ks, masks, _sink_column(sink_ref, kh, nq))
                probs = [p.astype(BF16) for p in probs]
                o_lo = functools.reduce(lambda u, w: u + w, [_dot(p[:2 * nq], v_lo) for p, (v_lo, _) in zip(probs, vs)])
                o_hi = functools.reduce(lambda u, w: u + w, [_dot(p[2 * nq:], v_hi) for p, (_, v_hi) in zip(probs, vs)])
                out = (o_lo * inv[:2 * nq] + o_hi * inv[2 * nq:]).astype(BF16)
                col = kh * 2 * LANES
                o_ref[q_rows, col:col + LANES] = out[:nq]
                o_ref[q_rows, col + LANES:col + 2 * LANES] = out[nq:]
            return carry

        _attention_schedule(seq, queries, 0)

    return _pcall(
        body, name=name, grid=(n_b,),
        in_specs=[pl.BlockSpec(memory_space=pltpu.SMEM),
                  pl.BlockSpec((None, n_l, D_MODEL), lambda b: (b, 0, 0)),
                  pl.BlockSpec((None, n_l, KV_WIDTH), lambda b: (b, 0, D_MODEL // KV_WIDTH)),
                  pl.BlockSpec((None, n_l, KV_WIDTH), lambda b: (b, 0, D_MODEL // KV_WIDTH + 1))],
        out_specs=pl.BlockSpec((None, n_l, D_MODEL), lambda b: (b, 0, 0)),
        out_shape=jax.ShapeDtypeStruct((n_b, n_l, D_MODEL), BF16),
        compiler_params=_cp(("arbitrary",), VMEM_BIG),
    )(sinks, proj3, proj3, proj3)


def attention_backward(proj3, dattn3, dproj3, sinks, after, seq, name):
    n_b, n_l, _ = proj3.shape
    qkv_width = D_MODEL + 2 * KV_WIDTH

    def body(sink_ref, q_ref, k_ref, v_ref, do_ref, _, __, dqkv_ref, dsink_ref, dk_ref, dv_ref):
        dk_ref[...] = jnp.zeros_like(dk_ref)
        dv_ref[...] = jnp.zeros_like(dv_ref)
        sub = lax.broadcasted_iota(jnp.int32, (SUBLANES, LANES), 0)
        lane = lax.broadcasted_iota(jnp.int32, (SUBLANES, LANES), 1)

        def queries(q_rows, nq, key_rows, masks, dsink):
            for kh in range(N_KV_HEADS):
                ks = _key_tiles(k_ref, key_rows, kh)
                vs = _key_tiles(v_ref, key_rows, kh)
                qs = _stacked(q_ref, q_rows, kh)
                dos = _stacked(do_ref, q_rows, kh)
                probs, inv, e_sink = _softmax_parts(qs, ks, masks, _sink_column(sink_ref, kh, nq))
                probs = [p * inv for p in probs]
                dps = [jnp.concatenate([_dot_nt(dos, v_lo), _dot_nt(dos, v_hi)], axis=0) for v_lo, v_hi in vs]
                delta = _row_sums([p * dp for p, dp in zip(probs, dps)])
                d_sink = -(e_sink * inv) * delta
                for quarter, g in enumerate(STACK_HEADS):
                    d_here = jnp.sum(d_sink[quarter * nq:(quarter + 1) * nq], axis=0, keepdims=True)
                    dsink = dsink + jnp.where(jnp.logical_and(sub == 0, lane == kh * Q_PER_KV + g), d_here, 0.0)
                dq = None
                tile = slice((kh // 2) * LANES, (kh // 2 + 1) * LANES)
                for r, p, dp, (k_lo, k_hi) in zip(key_rows, probs, dps, ks):
                    ds = (p * (dp - delta)).astype(BF16)
                    p16 = p.astype(BF16)
                    dq_x = _dot(ds[:2 * nq], k_lo) + _dot(ds[2 * nq:], k_hi)
                    dq = dq_x if dq is None else dq + dq_x
                    d_k = _to_kv_lanes(_dot_tn(ds[:2 * nq], qs), _dot_tn(ds[2 * nq:], qs), kh) * ATTN_SCALE
                    d_v = _to_kv_lanes(_dot_tn(p16[:2 * nq], dos), _dot_tn(p16[2 * nq:], dos), kh)
                    n_keys = r.size
                    dk_ref[r, tile] += d_k[:n_keys]
                    dv_ref[r, tile] += d_v[:n_keys]
                dq = (dq * ATTN_SCALE).astype(BF16)
                col = kh * 2 * LANES
                dqkv_ref[q_rows, col:col + LANES] = dq[:nq]
                dqkv_ref[q_rows, col + LANES:col + 2 * LANES] = dq[nq:]
            return dsink

        dsink_ref[...] = _attention_schedule(seq, queries, jnp.zeros((SUBLANES, LANES), F32))
        dqkv_ref[:, D_MODEL:D_MODEL + KV_WIDTH] = dk_ref[...].astype(BF16)
        dqkv_ref[:, D_MODEL + KV_WIDTH:] = dv_ref[...].astype(BF16)

    return _pcall(
        body, name=name, grid=(n_b,),
        in_specs=[pl.BlockSpec(memory_space=pltpu.SMEM),
                  pl.BlockSpec((None, n_l, D_MODEL), lambda b: (b, 0, 0)),
                  pl.BlockSpec((None, n_l, KV_WIDTH), lambda b: (b, 0, D_MODEL // KV_WIDTH)),
                  pl.BlockSpec((None, n_l, KV_WIDTH), lambda b: (b, 0, D_MODEL // KV_WIDTH + 1)),
                  pl.BlockSpec((None, n_l, D_MODEL), lambda b: (b, 0, 0)),
                  ANY_SPEC, ANY_SPEC],
        out_specs=[pl.BlockSpec((None, n_l, qkv_width), lambda b: (b, 0, 0)),
                   pl.BlockSpec((None, SUBLANES, LANES), lambda b: (b, 0, 0))],
        out_shape=[jax.ShapeDtypeStruct(dproj3.shape, BF16), jax.ShapeDtypeStruct((n_b, SUBLANES, LANES), F32)],
        scratch_shapes=[pltpu.VMEM((n_l, KV_WIDTH), F32), pltpu.VMEM((n_l, KV_WIDTH), F32)],
        input_output_aliases={5: 0},
        compiler_params=_cp(("arbitrary",), VMEM_BIG),
    )(sinks, proj3, proj3, proj3, dattn3, dproj3, after)


TAB_ROWS = 8
SCAN_UNROLL = 4


def _cmul(ar, ai, br, bi):
    return ar * br - ai * bi, ar * bi + ai * br


def _discretise(ar, ai, ls):
    step = jnp.exp(ls)
    mag = jnp.exp(ar * step)
    ang = ai * step
    cos, sin = jnp.cos(ang), jnp.sin(ang)
    lr, li = mag * cos, mag * sin
    den = ar * ar + ai * ai
    nr, ni = lr - 1.0, li
    cr = (nr * ar + ni * ai) / den
    ci = (ni * ar - nr * ai) / den
    return step, mag, lr, li, den, nr, ni, cr, ci


def _scan_tables(lr, li, reverse):
    n = lr.shape[-1]
    pw = [(lr, li)]
    for _ in range(SUBLANES - 1):
        pw.append(_cmul(pw[-1][0], pw[-1][1], lr, li))
    row = lax.broadcasted_iota(jnp.int32, (SUBLANES, n), 0)
    out = []
    for d in (1, 2, 4):
        ok = (row + d <= SUBLANES - 1) if reverse else (row >= d)
        out += [jnp.where(ok, pw[d - 1][0], 0.0), jnp.where(ok, pw[d - 1][1], 0.0)]
    cr = jnp.zeros((SUBLANES, n), F32)
    ci = jnp.zeros((SUBLANES, n), F32)
    for r in range(SUBLANES):
        e = (SUBLANES - r) if reverse else (r + 1)
        cr = jnp.where(row == r, pw[e - 1][0], cr)
        ci = jnp.where(row == r, pw[e - 1][1], ci)
    return out + [cr, ci]


def ssm_prepare(ar, ai, ls, br_t, bi_t, name):
    def body(ar_ref, ai_ref, ls_ref, br_ref, bi_ref, bbr_ref, bbi_ref, tf_ref, tr_ref):
        _, _, lr, li, _, _, _, cr, ci = _discretise(ar_ref[...], ai_ref[...], ls_ref[...])
        br, bi = br_ref[...], bi_ref[...]
        bbr_ref[...] = cr * br - ci * bi
        bbi_ref[...] = cr * bi + ci * br
        for k, t in enumerate(_scan_tables(lr, li, False)):
            tf_ref[k] = t
        for k, t in enumerate(_scan_tables(lr, -li, True)):
            tr_ref[k] = t

    return _pcall(
        body, name=name,
        out_shape=[jax.ShapeDtypeStruct((SSM_GROUP, N_STATES), F32), jax.ShapeDtypeStruct((SSM_GROUP, N_STATES), F32),
                   jax.ShapeDtypeStruct((TAB_ROWS, SUBLANES, N_STATES), F32),
                   jax.ShapeDtypeStruct((TAB_ROWS, SUBLANES, N_STATES), F32)],
    )(ar, ai, ls, br_t, bi_t)


def ssm_param_backward(ar, ai, ls, br_t, bi_t, dlr_p, dli_p, dbbr, dbbi, group_sum, name):
    def body(ar_ref, ai_ref, ls_ref, br_ref, bi_ref, dlr_ref, dli_ref, dbbr_ref, dbbi_ref, gs_ref,
             dar_ref, dai_ref, dls_ref, dbr_ref, dbi_ref):
        ar, ai = ar_ref[...], ai_ref[...]
        step, mag, lr, li, den, nr, ni, cr, ci = _discretise(ar, ai, ls_ref[...])
        br, bi, dbbr_v, dbbi_v = br_ref[...], bi_ref[...], dbbr_ref[...], dbbi_ref[...]
        dbr_ref[...] = cr * dbbr_v + ci * dbbi_v
        dbi_ref[...] = cr * dbbi_v - ci * dbbr_v
        dcr = jnp.sum(dbbr_v * br + dbbi_v * bi, axis=0, keepdims=True)
        dci = jnp.sum(dbbi_v * br - dbbr_v * bi, axis=0, keepdims=True)
        dnr = (dcr * ar - dci * ai) / den
        dni = (dcr * ai + dci * ar) / den
        dden = -(cr * dcr + ci * dci) / den
        dar = (dcr * nr + dci * ni) / den + dden * 2.0 * ar
        dai = (dcr * ni - dci * nr) / den + dden * 2.0 * ai
        dlr = jnp.sum(dlr_ref[...], axis=0, keepdims=True) + dnr
        dli = jnp.sum(dli_ref[...], axis=0, keepdims=True) + dni
        dmag = (dlr * lr + dli * li) / mag
        dang = dli * lr - dlr * li
        dar_ref[...] = dar + dmag * mag * step
        dai_ref[...] = dai + dang * step
        dstep = dmag * mag * ar + dang * ai
        dls_ref[...] = jnp.dot(dstep * step, gs_ref[...], preferred_element_type=F32, precision=lax.Precision.HIGHEST)

    vec = jax.ShapeDtypeStruct((1, N_STATES), F32)
    mat = jax.ShapeDtypeStruct((SSM_GROUP, N_STATES), F32)
    return _pcall(body, name=name, out_shape=[vec, vec, jax.ShapeDtypeStruct((1, LANES), F32), mat, mat])(
        ar, ai, ls, br_t, bi_t, dlr_p, dli_p, dbbr, dbbi, group_sum)


def _scan_rows(a, b, tabs, carry, reverse):
    for k, d in enumerate((1, 2, 4)):
        shift = SUBLANES - d if reverse else d
        sr, si = pltpu.roll(a, shift, axis=0), pltpu.roll(b, shift, axis=0)
        pr, pi = _cmul(tabs[2 * k], tabs[2 * k + 1], sr, si)
        a, b = a + pr, b + pi
    pr, pi = _cmul(tabs[6], tabs[7], carry[0], carry[1])
    return a + pr, b + pi


def _time_groups(seq, reverse):
    meta = [seq + SUBLANES * g for g in range(N_META // SUBLANES)]
    return meta[::-1] if reverse else meta


def ssm_forward_scan(proj3, b_comb, tabf, c_comb, dvec, seq, name):
    n_b, n_l, _ = proj3.shape
    u_blk = (D_MODEL + 2 * KV_WIDTH) // LANES

    def body(u_ref, b_ref, tab_ref, c_ref, d_ref, x_ref, y_ref, bu, xs):
        j = pl.program_id(1)
        u = u_ref[...]
        bu[...] = _dot(u, b_ref[...])
        tabs = [tab_ref[k] for k in range(TAB_ROWS)]

        def group(r0, carry):
            rows = pl.ds(r0, SUBLANES)
            a, b = _scan_rows(bu[rows, :SCAN_COLS], bu[rows, SCAN_COLS:], tabs, carry, False)
            xs[rows, :SCAN_COLS] = a
            xs[rows, SCAN_COLS:] = b
            return (jnp.broadcast_to(a[SUBLANES - 1:, :], a.shape), jnp.broadcast_to(b[SUBLANES - 1:, :], b.shape))

        zero = jnp.zeros((SUBLANES, SCAN_COLS), F32)
        carry = (zero, zero)
        for r0 in _time_groups(seq, False):
            carry = group(r0, carry)
        span = SCAN_UNROLL * SUBLANES

        def groups(t, c):
            for k in range(SCAN_UNROLL):
                c = group(pl.multiple_of(t * span, span) + k * SUBLANES, c)
            return c

        lax.fori_loop(0, seq // span, groups, carry)
        x16 = xs[...].astype(BF16)
        x_ref[...] = x16
        contrib = _dot(x16, c_ref[...])

        @pl.when(j % 2 == 0)
        def _():
            y_ref[...] = contrib + d_ref[...] * u.astype(F32)

        @pl.when(j % 2 == 1)
        def _():
            y_ref[...] += contrib

    return _pcall(
        body, name=name, grid=(n_b, N_SCAN_BLK),
        in_specs=[pl.BlockSpec((None, n_l, LANES), lambda b, j: (b, 0, u_blk + j // 2)),
                  pl.BlockSpec((None, LANES, 2 * SCAN_COLS), lambda b, j: (j, 0, 0)),
                  pl.BlockSpec((TAB_ROWS, SUBLANES, SCAN_COLS), lambda b, j: (0, 0, j)),
                  pl.BlockSpec((None, 2 * SCAN_COLS, LANES), lambda b, j: (j, 0, 0)),
                  pl.BlockSpec((1, LANES), lambda b, j: (0, j // 2))],
        out_specs=[pl.BlockSpec((None, n_l, 2 * SCAN_COLS), lambda b, j: (b, 0, j)),
                   pl.BlockSpec((None, n_l, LANES), lambda b, j: (b, 0, j // 2))],
        out_shape=[jax.ShapeDtypeStruct((n_b, n_l, 2 * N_STATES), BF16),
                   jax.ShapeDtypeStruct((n_b, n_l, SSM_WIDTH), F32)],
        scratch_shapes=[pltpu.VMEM((n_l, 2 * SCAN_COLS), F32)] * 2,
        compiler_params=_cp(("arbitrary", "arbitrary"), VMEM_BIG),
    )(proj3, b_comb, tabf, c_comb, dvec)


def ssm_backward_scan(dyraw3, xs3, dproj3, c_comb_t, tabr, b_comb_t, dvec, seq, name):
    n_b, n_l, _ = xs3.shape
    u_blk = (D_MODEL + 2 * KV_WIDTH) // LANES

    def body(dy_ref, x_ref, _, c_ref, tab_ref, b_ref, d_ref, du_ref, g_ref, dlr_ref, dli_ref, dx, gs, xs, du_acc):
        j = pl.program_id(1)
        dy = dy_ref[...]
        dx[...] = _dot(dy, c_ref[...])
        xs[...] = x_ref[...].astype(F32)
        tabs = [tab_ref[k] for k in range(TAB_ROWS)]
        last_row = lax.broadcasted_iota(jnp.int32, (SUBLANES, SCAN_COLS), 0) == SUBLANES - 1

        def group(r0, state):
            cr, ci, acc_r, acc_i = state
            rows = pl.ds(r0, SUBLANES)
            a, b = _scan_rows(dx[rows, :SCAN_COLS], dx[rows, SCAN_COLS:], tabs, (cr, ci), True)
            gs[rows, :SCAN_COLS] = a
            gs[rows, SCAN_COLS:] = b
            na = jnp.where(last_row, cr, pltpu.roll(a, SUBLANES - 1, axis=0))
            nb = jnp.where(last_row, ci, pltpu.roll(b, SUBLANES - 1, axis=0))
            xa, xb = xs[rows, :SCAN_COLS], xs[rows, SCAN_COLS:]
            return (jnp.broadcast_to(a[:1, :], a.shape), jnp.broadcast_to(b[:1, :], b.shape),
                    acc_r + na * xa + nb * xb, acc_i + nb * xa - na * xb)

        zero = jnp.zeros((SUBLANES, SCAN_COLS), F32)
        span = SCAN_UNROLL * SUBLANES
        n_spans = seq // span

        def groups(t, s):
            for k in reversed(range(SCAN_UNROLL)):
                s = group(pl.multiple_of((n_spans - 1 - t) * span, span) + k * SUBLANES, s)
            return s

        state = lax.fori_loop(0, n_spans, groups, (zero, zero, zero, zero))
        for r0 in _time_groups(seq, True):
            state = group(r0, state)
        dlr_ref[...] = state[2]
        dli_ref[...] = state[3]
        g16 = gs[...].astype(BF16)
        g_ref[...] = g16
        contrib = _dot(g16, b_ref[...])

        @pl.when(j % 2 == 0)
        def _():
            du_acc[...] = contrib + d_ref[...] * dy.astype(F32)

        @pl.when(j % 2 == 1)
        def _():
            du_ref[...] = (du_acc[...] + contrib).astype(BF16)

    state_blk = pl.BlockSpec((None, n_l, 2 * SCAN_COLS), lambda b, j: (b, 0, j))
    dl_blk = pl.BlockSpec((None, SUBLANES, SCAN_COLS), lambda b, j: (b, 0, j))
    return _pcall(
        body, name=name, grid=(n_b, N_SCAN_BLK),
        in_specs=[pl.BlockSpec((None, n_l, LANES), lambda b, j: (b, 0, j // 2)), state_blk,
                  pl.BlockSpec(memory_space=pl.ANY),
                  pl.BlockSpec((None, LANES, 2 * SCAN_COLS), lambda b, j: (j, 0, 0)),
                  pl.BlockSpec((TAB_ROWS, SUBLANES, SCAN_COLS), lambda b, j: (0, 0, j)),
                  pl.BlockSpec((None, 2 * SCAN_COLS, LANES), lambda b, j: (j, 0, 0)),
                  pl.BlockSpec((1, LANES), lambda b, j: (0, j // 2))],
        out_specs=[pl.BlockSpec((None, n_l, LANES), lambda b, j: (b, 0, u_blk + j // 2)), state_blk, dl_blk, dl_blk],
        out_shape=[jax.ShapeDtypeStruct(dproj3.shape, BF16), jax.ShapeDtypeStruct((n_b, n_l, 2 * N_STATES), BF16),
                   jax.ShapeDtypeStruct((n_b, SUBLANES, N_STATES), F32), jax.ShapeDtypeStruct((n_b, SUBLANES, N_STATES), F32)],
        scratch_shapes=[pltpu.VMEM((n_l, 2 * SCAN_COLS), F32)] * 3 + [pltpu.VMEM((n_l, LANES), F32)],
        input_output_aliases={2: 0},
        compiler_params=_cp(("arbitrary", "arbitrary"), VMEM_BIG),
    )(dyraw3, xs3, dproj3, c_comb_t, tabr, b_comb_t, dvec)


def ssm_param_grads(proj, gs, xs, dyraw, tm, name):
    t_rows = proj.shape[0]
    ni = t_rows // tm
    u_blk = (D_MODEL + 2 * KV_WIDTH) // LANES
    width = 2 * SCAN_COLS

    def body(u_ref, g_ref, x_ref, dy_ref, db_ref, dc_ref, dd_ref):
        cb, i = pl.program_id(0), pl.program_id(1)
        u, dy = u_ref[...], dy_ref[...]
        _accumulate(db_ref, _dot_tn(u, g_ref[...]), i == 0)
        _accumulate(dc_ref, _dot_tn(x_ref[...], dy), i == 0)

        @pl.when(cb % 2 == 0)
        def _():
            _accumulate(dd_ref, jnp.sum(dy.astype(F32) * u.astype(F32), axis=0, keepdims=True), i == 0)

    return _pcall(
        body, name=name, grid=(N_SCAN_BLK, ni),
        in_specs=[pl.BlockSpec((tm, LANES), lambda cb, i: (i, u_blk + cb // 2)),
                  pl.BlockSpec((tm, width), lambda cb, i: (i, cb)),
                  pl.BlockSpec((tm, width), lambda cb, i: (i, cb)),
                  pl.BlockSpec((tm, LANES), lambda cb, i: (i, cb // 2))],
        out_specs=[pl.BlockSpec((None, LANES, width), lambda cb, i: (cb, 0, 0)),
                   pl.BlockSpec((None, width, LANES), lambda cb, i: (cb, 0, 0)),
                   pl.BlockSpec((1, LANES), lambda cb, i: (0, cb // 2))],
        out_shape=[jax.ShapeDtypeStruct((N_SCAN_BLK, LANES, width), F32),
                   jax.ShapeDtypeStruct((N_SCAN_BLK, width, LANES), F32), jax.ShapeDtypeStruct((1, SSM_WIDTH), F32)],
        compiler_params=_cp(("arbitrary", "arbitrary"), VMEM_BIG),
    )(proj, gs, xs, dyraw)


def sum_leading(x, name):
    def body(x_ref, o_ref):
        acc = x_ref[0]
        for k in range(1, x.shape[0]):
            acc = acc + x_ref[k]
        o_ref[...] = acc

    return _pcall(body, name=name, out_shape=jax.ShapeDtypeStruct(x.shape[1:], x.dtype))(x)


WEIGHTS = ['meta_tokens', 'ffn1_norm', 'ffn1_w1', 'ffn1_w3', 'ffn1_w2', 'mix_norm', 'w_in', 'attn_sinks', 'ssm_a_re',
           'ssm_a_im', 'ssm_log_step', 'ssm_b_re', 'ssm_b_im', 'ssm_c_re', 'ssm_c_im', 'ssm_d', 'ssm_glu_a', 'ssm_glu_b',
           'w_out', 'ffn2_norm', 'ffn2_w1', 'ffn2_w3', 'ffn2_w2', 'final_norm']
SHARDED = ['ffn1_w1', 'ffn1_w3', 'ffn1_w2', 'ffn2_w1', 'ffn2_w3', 'ffn2_w2', 'w_in', 'ssm_glu_a', 'ssm_glu_b', 'w_out']
REPLICATED = ['ffn1_norm', 'mix_norm', 'ffn2_norm', 'final_norm', 'attn_sinks', 'ssm_a_re', 'ssm_a_im', 'ssm_log_step',
              'ssm_b_re', 'ssm_b_im', 'ssm_c_re', 'ssm_c_im', 'ssm_d']
PACK_COLS = 1024


def _pack(arrays):
    parts = []
    for a in arrays:
        flat = a.reshape(-1)
        chunk = SUBLANES * PACK_COLS
        padded = -(-flat.shape[0] // chunk) * chunk
        parts.append(jnp.pad(flat, (0, padded - flat.shape[0])).reshape(-1, PACK_COLS))
    return jnp.concatenate(parts, axis=0)


def _unpack(packed, shapes):
    out, row = [], 0
    for shape in shapes:
        size = 1
        for s in shape:
            size *= s
        chunk = SUBLANES * PACK_COLS
        rows = -(-size // chunk) * SUBLANES
        out.append(packed[row:row + rows].reshape(-1)[:size].reshape(shape))
        row += rows
    return out


def kernel(x, meta_tokens, ffn1_norm, ffn1_w1, ffn1_w3, ffn1_w2, mix_norm, w_in, attn_sinks, ssm_a_re, ssm_a_im, ssm_log_step, ssm_b_re, ssm_b_im, ssm_c_re, ssm_c_im, ssm_d, ssm_glu_a, ssm_glu_b, w_out, ffn2_norm, ffn2_w1, ffn2_w3, ffn2_w2, final_norm, loss_target, m_meta_tokens, m_ffn1_norm, m_ffn1_w1, m_ffn1_w3, m_ffn1_w2, m_mix_norm, m_w_in, m_attn_sinks, m_ssm_a_re, m_ssm_a_im, m_ssm_log_step, m_ssm_b_re, m_ssm_b_im, m_ssm_c_re, m_ssm_c_im, m_ssm_d, m_ssm_glu_a, m_ssm_glu_b, m_w_out, m_ffn2_norm, m_ffn2_w1, m_ffn2_w3, m_ffn2_w2, m_final_norm, v_meta_tokens, v_ffn1_norm, v_ffn1_w1, v_ffn1_w3, v_ffn1_w2, v_mix_norm, v_w_in, v_attn_sinks, v_ssm_a_re, v_ssm_a_im, v_ssm_log_step, v_ssm_b_re, v_ssm_b_im, v_ssm_c_re, v_ssm_c_im, v_ssm_d, v_ssm_glu_a, v_ssm_glu_b, v_w_out, v_ffn2_norm, v_ffn2_w1, v_ffn2_w3, v_ffn2_w2, v_final_norm):
    given = dict(locals())
    w = {n: given[n] for n in WEIGHTS}
    m = {n: given["m_" + n] for n in WEIGHTS}
    v = {n: given["v_" + n] for n in WEIGHTS}

    n_b, seq, _ = x.shape
    n_l = seq + N_META
    t_rows = n_b * n_l
    tm = _row_tile(n_l, 688)
    px, py, pc = _my_place()
    me = 4 * px + 2 * py + pc

    glu = jnp.stack([ssm_glu_a[0], ssm_glu_b[0]]).astype(BF16)
    ffn_names = ['ffn1_w1', 'ffn1_w3', 'ffn1_w2', 'ffn2_w1', 'ffn2_w3', 'ffn2_w2']

    def hidden_on_rows(n, t):
        return t[0] if n.endswith('w2') else t[0].T

    def hidden_on_rows_back(n, t):
        return t[None] if n.endswith('w2') else t.T[None]

    me_idx = jnp.reshape(me, (1,)).astype(jnp.int32)
    first_names, later_names = ffn_names[:2], ffn_names[3:]
    *first, metag = all_gather_list(
        [hidden_on_rows(n, w[n]).astype(BF16) for n in first_names] + [meta_tokens], meta_tokens, "ag_first")
    w2_send, w2_recv, w2_shard, w2_land, w2_token = exchange_start(
        [hidden_on_rows('ffn1_w2', w['ffn1_w2']).astype(BF16)], first[0], True, "ag_w2_start")
    win_send, win_recv, win_shard, win_land, win_token = exchange_start(
        [w_in[0].astype(BF16)], w2_token, True, "ag_w_in_start")
    later_shards = [hidden_on_rows(n, w[n]).astype(BF16) for n in later_names] + [glu, w_out[0].astype(BF16)]
    ag_send, ag_recv, later_shards, later_lands, ag_token = exchange_start(later_shards, win_token, True, "ag_later_start")
    full = {n: g.reshape(D_FF, D_MODEL) for n, g in zip(first_names, first)}
    meta_full = metag.transpose(1, 0, 2).reshape(N_META, D_MODEL)

    final_g = final_norm.reshape(1, D_MODEL)

    ar = ssm_a_re.reshape(1, N_STATES)
    ai = ssm_a_im.reshape(1, N_STATES)
    ls = jnp.repeat(ssm_log_step.reshape(SSM_GROUPS), SSM_STATE).reshape(1, N_STATES)
    br_t = ssm_b_re[0].transpose(2, 0, 1).reshape(SSM_GROUP, N_STATES)
    bi_t = ssm_b_im[0].transpose(2, 0, 1).reshape(SSM_GROUP, N_STATES)
    bbr, bbi, tabf, tabr = ssm_prepare(ar, ai, ls, br_t, bi_t, "ssm_prepare")
    bbr_g = bbr.reshape(SSM_GROUP, SSM_GROUPS, SSM_STATE).transpose(1, 0, 2)
    bbi_g = bbi.reshape(SSM_GROUP, SSM_GROUPS, SSM_STATE).transpose(1, 0, 2)
    groups_per_blk = SCAN_COLS // SSM_STATE
    half = ((jnp.arange(N_SCAN_BLK) % 2)[:, None] == jnp.arange(2)[None, :]).astype(F32)
    eye = jnp.eye(groups_per_blk, dtype=F32)

    def scan_blocks(re_g, im_g):
        def one(t):
            t = t.reshape(N_SCAN_BLK, groups_per_blk, SSM_GROUP, SSM_STATE)
            t = t[:, :, :, None, :] * eye[None, :, None, :, None]
            t = t.reshape(N_SCAN_BLK, LANES // 2, SCAN_COLS)
            return (t[:, None] * half[:, :, None, None]).reshape(N_SCAN_BLK, LANES, SCAN_COLS)
        return jnp.concatenate([one(re_g), one(im_g)], axis=-1).astype(BF16)

    b_comb = scan_blocks(bbr_g, bbi_g)
    c_comb_t = scan_blocks(ssm_c_re[0], -ssm_c_im[0])
    b_comb_t, c_comb = b_comb.transpose(0, 2, 1), c_comb_t.transpose(0, 2, 1)

    hid1, hn1, a1, b1, h0 = ffn_forward(
        x, ffn1_norm, full['ffn1_w1'], full['ffn1_w3'], None, ag_token, tm, "ffn1_fwd", meta=meta_full)
    w2_shard, (w2g,) = exchange_wait(w2_send, w2_recv, w2_shard, w2_land, hid1, True, "ag_w2_wait")
    full['ffn1_w2'] = lax.dynamic_update_slice_in_dim(w2g, w2_shard[0][None], me, axis=0).reshape(D_FF, D_MODEL)
    ffn1_w = (full['ffn1_w1'], full['ffn1_w3'], full['ffn1_w2'])
    (h1,) = ffn_down(h0, hid1, full['ffn1_w2'], tm, "ffn1_down")
    win_shard, (wing,) = exchange_wait(win_send, win_recv, win_shard, win_land, h1, True, "ag_w_in_wait")
    wing = lax.dynamic_update_slice_in_dim(wing, win_shard[0][None], me, axis=0)
    hnm, proj = mix_forward(h1, mix_norm, wing, tm, "mix_fwd")
    proj3 = proj.reshape(n_b, n_l, IN_WIDTH)
    attn3 = attention_forward(proj3, attn_sinks, seq, "attn_fwd")
    attn = attn3.reshape(t_rows, D_MODEL)
    xs3, yraw3 = ssm_forward_scan(proj3, b_comb, tabf, c_comb, ssm_d, seq, "ssm_fwd")
    yraw = yraw3.reshape(t_rows, SSM_WIDTH)
    later_shards, later = exchange_wait(ag_send, ag_recv, later_shards, later_lands, yraw3, True, "ag_later_wait")
    later = [lax.dynamic_update_slice_in_dim(z, s[None], me, axis=0) for z, s in zip(later, later_shards)]
    for n, g in zip(later_names, later):
        full[n] = g.reshape(D_FF, D_MODEL)
    ffn2_w = (full['ffn2_w1'], full['ffn2_w3'], full['ffn2_w2'])
    glug, wog = later[len(later_names):]
    glu_a = glug[:, 0].transpose(1, 0, 2).reshape(SSM_WIDTH, D_MODEL)
    glu_b = glug[:, 1].transpose(1, 0, 2).reshape(SSM_WIDTH, D_MODEL)
    w_out_full = wog.reshape(D_MODEL, D_MODEL)
    h2 = merge_forward(h1, yraw, attn, proj, glu_a, glu_b, w_out_full, tm, "merge_fwd")
    h3, hn2, a2, b2 = ffn_forward(h2, ffn2_norm, *ffn2_w, ag_token, tm, "ffn2_fwd")
    dh3, loss_part, g_final = final_loss_backward(h3, loss_target, final_g, seq, tm, "loss_bwd")

    def blocked_ffn(d_w1t, d_w3t, d_w2):
        return tuple(t.reshape(N_DEV, FF_BLK, D_MODEL) for t in (d_w1t, d_w3t, d_w2))

    def blocked_cols(full_grad):
        r = full_grad.shape[0]
        return full_grad.reshape(r, N_DEV, full_grad.shape[1] // N_DEV).transpose(1, 0, 2).astype(BF16)

    early = {}

    def start_reduce(names, tag):
        srcs = [dw[n] for n in names]
        send, recv, srcs, lands, token = exchange_start(srcs, srcs[0], False, "rs_" + tag + "_start")
        early[tag] = (names, send, recv, srcs, lands)
        return token

    dw = {}
    da2, db2, dh3_half = ffn_backward_hidden(dh3, a2, b2, ffn2_w[2], g_final, tm, "ffn2_bwd_hid")
    dw['ffn2_w1'], dw['ffn2_w3'], dw['ffn2_w2'] = blocked_ffn(
        *ffn_backward_weights(hn2, dh3_half, a2, b2, da2, db2, n_l, FF_BWD_COLS, "ffn2_bwd_w"))
    token = start_reduce(later_names, "ffn2")
    dh2, g_ffn2_norm = ffn_backward_input(dh3, h2, ffn2_norm, da2, db2, ffn2_w[0], ffn2_w[1], token, tm, "ffn2_bwd_in")
    dattn, dyraw, dproj, *for_weights = merge_backward(dh2, yraw, attn, proj, glu_a, glu_b, w_out_full, token, tm,
                                                       "merge_bwd")
    d_wo, d_ga, d_gb = merge_backward_weights(*for_weights, tm, "merge_bwd_w")
    dw['ssm_glu_a'] = blocked_cols(d_ga)
    dw['ssm_glu_b'] = blocked_cols(d_gb)
    dw['w_out'] = d_wo.reshape(N_DEV, D_MODEL // N_DEV, D_MODEL).astype(BF16)
    token = start_reduce(['ssm_glu_a', 'ssm_glu_b', 'w_out'], "mix")
    dproj3 = dproj.reshape(n_b, n_l, IN_WIDTH)
    dproj3, dsink_p = attention_backward(proj3, dattn.reshape(n_b, n_l, D_MODEL), dproj3, attn_sinks, token, seq,
                                         "attn_bwd")
    dproj3, gs3, dlr_p, dli_p = ssm_backward_scan(
        dyraw.reshape(n_b, n_l, SSM_WIDTH), xs3, dproj3, c_comb_t, tabr, b_comb_t, ssm_d, seq, "ssm_bwd")
    dproj = dproj3.reshape(t_rows, IN_WIDTH)
    d_bd, d_cd, g_d = ssm_param_grads(proj, gs3.reshape(t_rows, 2 * N_STATES), xs3.reshape(t_rows, 2 * N_STATES),
                                      dyraw, n_l, "ssm_bwd_w")
    w_in_full = wing.transpose(1, 0, 2).reshape(D_MODEL, IN_WIDTH)
    dh1, g_mix_norm = mix_backward_act(dh2, h1, mix_norm, dproj, w_in_full, tm, "mix_bwd_act")
    dw['w_in'] = mix_backward_weights(hnm, dproj, n_l, "mix_bwd_w")
    token = start_reduce(['w_in'], "w_in")

    def group_blocks(part, channels_first):
        if channels_first:
            t = jnp.sum(part.reshape(N_SCAN_BLK, 2, LANES // 2, SCAN_COLS) * half[:, :, None, None], axis=1)
            t = t.reshape(N_SCAN_BLK, groups_per_blk, SSM_GROUP, groups_per_blk, SSM_STATE)
            t = jnp.sum(t * eye[None, :, None, :, None], axis=3)
            return t.reshape(SSM_GROUPS, SSM_GROUP, SSM_STATE)
        t = jnp.sum(part.reshape(N_SCAN_BLK, SCAN_COLS, 2, LANES // 2) * half[:, None, :, None], axis=2)
        t = t.reshape(N_SCAN_BLK, groups_per_blk, SSM_STATE, groups_per_blk, SSM_GROUP)
        t = jnp.sum(t * eye[None, :, None, :, None], axis=3)
        return t.reshape(SSM_GROUPS, SSM_STATE, SSM_GROUP).transpose(0, 2, 1)

    dbbr = group_blocks(d_bd[:, :, :SCAN_COLS], True).transpose(1, 0, 2).reshape(SSM_GROUP, N_STATES)
    dbbi = group_blocks(d_bd[:, :, SCAN_COLS:], True).transpose(1, 0, 2).reshape(SSM_GROUP, N_STATES)
    g_c_re = group_blocks(d_cd[:, :SCAN_COLS, :], False)[None]
    g_c_im = -group_blocks(d_cd[:, SCAN_COLS:, :], False)[None]
    group_sum = (jnp.arange(N_STATES)[:, None] // SSM_STATE == jnp.arange(LANES)[None, :]).astype(F32)
    g_ar, g_ai, g_ls, g_br, g_bi = ssm_param_backward(
        ar, ai, ls, br_t, bi_t, dlr_p.reshape(n_b * SUBLANES, N_STATES), dli_p.reshape(n_b * SUBLANES, N_STATES),
        dbbr, dbbi, group_sum, "ssm_bwd_params")
    g_sinks = sum_leading(dsink_p, "sink_sum")[0:1, :N_KV_HEADS * Q_PER_KV]

    small = {
        'mix_norm': g_mix_norm, 'ffn2_norm': g_ffn2_norm, 'final_norm': g_final.reshape(D_MODEL),
        'attn_sinks': g_sinks, 'ssm_a_re': g_ar.reshape(1, SSM_GROUPS, SSM_STATE), 'ssm_a_im': g_ai.reshape(1, SSM_GROUPS, SSM_STATE),
        'ssm_log_step': g_ls[:, :SSM_GROUPS],
        'ssm_b_re': g_br.reshape(SSM_GROUP, SSM_GROUPS, SSM_STATE).transpose(1, 2, 0)[None],
        'ssm_b_im': g_bi.reshape(SSM_GROUP, SSM_GROUPS, SSM_STATE).transpose(1, 2, 0)[None],
        'ssm_c_re': g_c_re, 'ssm_c_im': g_c_im, 'ssm_d': g_d,
    }
    early_small = [n for n in REPLICATED if n in small]
    sg_send, sg_recv, sg_src, sg_land, token = exchange_start(
        [_pack([small[n] for n in early_small] + [loss_part])], token, True, "ag_small_start")
    da1, db1, dh1_half = ffn_backward_hidden(dh1, a1, b1, ffn1_w[2], token, tm, "ffn1_bwd_hid")
    dw['ffn1_w1'], dw['ffn1_w3'], dw['ffn1_w2'] = blocked_ffn(
        *ffn_backward_weights(hn1, dh1_half, a1, b1, da1, db1, n_l, FF_BWD_COLS, "ffn1_bwd_w"))
    token = start_reduce(ffn_names[:3], "ffn1")
    grad_x, meta_rows_grad, g_ffn1_norm = ffn_backward_input(
        dh1, h0, ffn1_norm, da1, db1, ffn1_w[0], ffn1_w[1], token, tm, "ffn1_bwd_in", examples=(n_b, seq))
    g_meta = sum_leading(meta_rows_grad, "meta_sum")

    grads, deltas, new_m, new_v = {}, {}, {}, {}

    def views(n):
        if n in ffn_names:
            return functools.partial(hidden_on_rows, n), functools.partial(hidden_on_rows_back, n)
        return (lambda t: t[0]), (lambda t: t[None])

    def finish_reduce(tag, previous):
        names, send, recv, srcs, lands = early[tag]
        srcs, lands = exchange_wait(send, recv, srcs, lands, previous, False, "rs_" + tag + "_wait")
        for n, g, land in zip(names, srcs, lands):
            two_d, back = views(n)
            out = adamw_exchanged(me_idx, g, land, two_d(w[n]), two_d(m[n]), two_d(v[n]), "adamw_" + n)
            grads[n], deltas[n], new_m[n], new_v[n] = (back(o) for o in out)
            previous = out[1]
        return previous

    previous = g_meta
    for tag in ("ffn2", "mix", "w_in"):
        previous = finish_reduce(tag, previous)

    zeros_meta = jnp.zeros((N_META, D_MODEL), F32)
    (late_parts,) = all_gather_list([_pack([g_ffn1_norm, g_meta])], previous, "ag_small_late")
    sg_src, (early_parts,) = exchange_wait(sg_send, sg_recv, sg_src, sg_land, late_parts, True, "ag_small_wait")
    early_parts = lax.dynamic_update_slice_in_dim(early_parts, sg_src[0][None], me, axis=0)

    def small_update(parts, names, extra, tag):
        pack_of = lambda d: _pack([d[n] for n in names] + extra)
        packed = adamw_small(parts, pack_of(w), pack_of(m), pack_of(v), "adamw_small_" + tag)
        unpacked = [_unpack(p, [w[n].shape for n in names] + [e.shape for e in extra]) for p in packed]
        for k, n in enumerate(names):
            grads[n], deltas[n], new_m[n], new_v[n] = (u[k] for u in unpacked)
        return packed, unpacked

    _, early_out = small_update(early_parts, early_small, [jnp.zeros_like(loss_part)], "early")
    loss = early_out[0][-1][0, 0]
    packed_out, unpacked = small_update(late_parts, ['ffn1_norm'], [zeros_meta], "late")
    g_meta_full = unpacked[0][-1]
    grads['meta_tokens'] = lax.dynamic_index_in_dim(
        g_meta_full.reshape(N_META, N_DEV, D_MODEL // N_DEV), me, axis=1, keepdims=False)
    deltas['meta_tokens'], new_m['meta_tokens'], new_v['meta_tokens'] = adamw_plain(
        grads['meta_tokens'], w['meta_tokens'], m['meta_tokens'], v['meta_tokens'], "adamw_meta")

    finish_reduce("ffn1", packed_out[0])

    return (loss, grad_x, *[grads[n] for n in WEIGHTS], *[deltas[n] for n in WEIGHTS],
            *[new_m[n] for n in WEIGHTS], *[new_v[n] for n in WEIGHTS])
```

```python
import functools

import jax
import jax.numpy as jnp
from jax import lax
from jax.experimental import pallas as pl
from jax.experimental.pallas import tpu as pltpu

F32 = jnp.float32
BF16 = jnp.bfloat16
MESH = pl.DeviceIdType.MESH

N_DEV = 8
D_MODEL = 1024
N_META = 16
HEAD_DIM = 64
N_KV_HEADS = 4
Q_PER_KV = 4
BLOCK = 128
KV_WIDTH = N_KV_HEADS * HEAD_DIM
SSM_GROUP = 16
SSM_WIDTH = 512
SSM_GROUPS = 32
SSM_STATE = 64
N_STATES = SSM_GROUPS * SSM_STATE
D_FF = 2816
FF_BLK = D_FF // N_DEV
IN_WIDTH = 4096
IN_BLK = IN_WIDTH // N_DEV
NORM_EPS = 1e-6
NEG_INF = -1e30
SCAN_COLS = 256
N_SCAN_BLK = N_STATES // SCAN_COLS
SUBLANES = 8
LANES = 128
MXU_WIDTH = 256
FF_BWD_COLS = MXU_WIDTH

ADAM_LR = 0.001
ADAM_B1 = 0.9
ADAM_B2 = 0.999
ADAM_EPS = 1e-08
ADAM_WD = 0.01
ADAM_STEP = 10

VMEM_BIG = 56 * 1024 * 1024


def _cp(sem=None, vmem=None):
    kw = {}
    if sem is not None:
        kw["dimension_semantics"] = sem
    if vmem is not None:
        kw["vmem_limit_bytes"] = vmem
    return pltpu.CompilerParams(**kw)


def _pcall(body, **kw):
    return pl.pallas_call(body, **kw)


def _dot(a, b):
    return jnp.dot(a, b, preferred_element_type=F32)


def _dot_nt(a, b):
    return lax.dot_general(a, b, (((1,), (1,)), ((), ())), preferred_element_type=F32)


def _dot_tn(a, b):
    return lax.dot_general(a, b, (((0,), (0,)), ((), ())), preferred_element_type=F32)


def _sigmoid(x):
    return 0.5 + 0.5 * jnp.tanh(0.5 * x)


def _row_tile(rows, cap):
    best = None
    for t in range(16, min(rows, cap) + 1, 16):
        if rows % t == 0:
            best = t
    assert best is not None, rows
    return best


def _my_place():
    return lax.axis_index("x"), lax.axis_index("y"), lax.axis_index("c")


def all_gather_list(shards, after, name):
    n = len(shards)

    def body(*refs):
        ins, outs = refs[:n], refs[n + 1:2 * n + 1]
        send_sems, recv_sems, local_sems = refs[2 * n + 1:]
        x, y, c = _my_place()
        me, sibling = (x, y, c), (x, y, 1 - c)
        chips = [(1 - x, y), (x, 1 - y), (1 - x, 1 - y)]

        def blk(a, px, py, pc):
            return outs[a].at[4 * px + 2 * py + pc]

        def copy(a, k, block, to, src=None):
            return pltpu.make_async_remote_copy(
                src_ref=blk(a, *block) if src is None else src, dst_ref=blk(a, *block),
                send_sem=send_sems.at[a * 7 + k], recv_sem=recv_sems.at[a * 7 + k],
                device_id=to, device_id_type=MESH)

        mine = [pltpu.make_async_copy(ins[a], blk(a, *me), local_sems.at[a]) for a in range(n)]
        for cp in mine:
            cp.start()
        first = []
        for a in range(n):
            first.append(copy(a, 0, me, sibling, src=ins[a]))
            first += [copy(a, 1 + j, me, (*chip, c), src=ins[a]) for j, chip in enumerate(chips)]
        for cp in first:
            cp.start()
        passed = []
        for j, chip in enumerate(chips):
            for a in range(n):
                copy(a, 1 + j, (*chip, c), me).wait_recv()
                cp = copy(a, 4 + j, (*chip, c), sibling)
                cp.start()
                passed.append(cp)
        for a in range(n):
            copy(a, 0, sibling, me).wait_recv()
            for j, chip in enumerate(chips):
                copy(a, 4 + j, (*chip, 1 - c), me).wait_recv()
        for cp in first + passed:
            cp.wait_send()
        for cp in mine:
            cp.wait()

    any_spec = pl.BlockSpec(memory_space=pl.ANY)
    return _pcall(
        body, name=name,
        out_shape=[jax.ShapeDtypeStruct((N_DEV,) + s.shape, s.dtype) for s in shards],
        in_specs=[any_spec] * (n + 1), out_specs=[any_spec] * n,
        scratch_shapes=[pltpu.SemaphoreType.DMA((7 * n,)), pltpu.SemaphoreType.DMA((7 * n,)),
                        pltpu.SemaphoreType.DMA((n,))],
    )(*shards, after)


HBM_SPEC = pl.BlockSpec(memory_space=pltpu.HBM)
SEM_SPEC = pl.BlockSpec(memory_space=pltpu.SEMAPHORE)
N_PEERS = N_DEV - 1


def _related(k):
    x, y, c = _my_place()
    px = 1 - x if k & 4 else x
    py = 1 - y if k & 2 else y
    pc = 1 - c if k & 1 else c
    return (px, py, pc), 4 * px + 2 * py + pc


def _exchange_copies(srcs, lands, send_sems, recv_sems, gather):
    x, y, c = _my_place()
    me = 4 * x + 2 * y + c
    copies = []
    for a, (src, land) in enumerate(zip(srcs, lands)):
        for k in range(1, N_DEV):
            peer, d = _related(k)
            copies.append(pltpu.make_async_remote_copy(
                src_ref=src if gather else src.at[d], dst_ref=land.at[me] if gather else land.at[k],
                send_sem=send_sems.at[a * N_PEERS + k - 1], recv_sem=recv_sems.at[a * N_PEERS + k - 1],
                device_id=peer, device_id_type=MESH))
    return copies


def exchange_start(srcs, after, gather, name):
    n = len(srcs)
    land_shapes = [((N_DEV,) + s.shape) if gather else s.shape for s in srcs]

    def body(*refs):
        send_sems, recv_sems = refs[2 * n + 1], refs[2 * n + 2]
        for cp in _exchange_copies(refs[:n], refs[n:2 * n], send_sems, recv_sems, gather):
            cp.start()
        token = refs[-1]
        token[...] = jnp.zeros_like(token)

    sems = pltpu.SemaphoreType.DMA((n * N_PEERS,))
    lands = [pltpu.with_memory_space_constraint(lax.empty(shape, s.dtype), pltpu.HBM) for shape, s in zip(land_shapes, srcs)]
    out = _pcall(
        body, name=name,
        out_shape=(sems, sems, *[pltpu.HBM(s.shape, s.dtype) for s in srcs],
                   *[pltpu.HBM(shape, s.dtype) for shape, s in zip(land_shapes, srcs)],
                   jax.ShapeDtypeStruct((SUBLANES, LANES), F32)),
        in_specs=[HBM_SPEC] * (2 * n) + [pl.BlockSpec(memory_space=pl.ANY)],
        out_specs=(SEM_SPEC, SEM_SPEC, *[HBM_SPEC] * (2 * n), pl.BlockSpec(memory_space=pltpu.VMEM)),
        input_output_aliases={i: 2 + i for i in range(2 * n)},
        compiler_params=pltpu.CompilerParams(has_side_effects=pltpu.SideEffectType.DATAFLOW_SIDE_EFFECTING),
    )(*[pltpu.with_memory_space_constraint(s, pltpu.HBM) for s in srcs], *lands, after)
    return out[0], out[1], list(out[2:2 + n]), list(out[2 + n:2 + 2 * n]), out[-1]


def exchange_wait(send_sems, recv_sems, srcs, lands, after, gather, name):
    n = len(srcs)

    def body(*refs):
        for cp in _exchange_copies(refs[:n], refs[n:2 * n], refs[2 * n], refs[2 * n + 1], gather):
            cp.wait_send()
            cp.wait_recv()

    out = _pcall(
        body, name=name,
        out_shape=(*[pltpu.HBM(s.shape, s.dtype) for s in srcs], *[pltpu.HBM(z.shape, z.dtype) for z in lands]),
        in_specs=[HBM_SPEC] * (2 * n) + [SEM_SPEC, SEM_SPEC, pl.BlockSpec(memory_space=pl.ANY)],
        out_specs=tuple([HBM_SPEC] * (2 * n)),
        input_output_aliases={i: i for i in range(2 * n)},
        compiler_params=pltpu.CompilerParams(has_side_effects=pltpu.SideEffectType.DATAFLOW_SIDE_EFFECTING),
    )(*srcs, *lands, send_sems, recv_sems, after)
    return list(out[:n]), list(out[n:])


def adamw_exchanged(me, g, land, w, m, v, name):
    rows, cols = w.shape
    tr = _row_tile(rows, 256)

    def body(me_ref, g_ref, land_ref, w_ref, m_ref, v_ref, go_ref, d_ref, mo_ref, vo_ref):
        grad = g_ref[...].astype(F32)
        for k in range(1, N_DEV):
            grad = grad + land_ref[k].astype(F32)
        delta, m_new, v_new = _adam_math(w_ref[...], grad, m_ref[...], v_ref[...])
        go_ref[...] = grad
        d_ref[...] = delta
        mo_ref[...] = m_new
        vo_ref[...] = v_new

    tile = pl.BlockSpec((tr, cols), lambda r, ix: (r, 0))
    out = jax.ShapeDtypeStruct((rows, cols), F32)
    return _pcall(
        body, name=name, out_shape=[out] * 4,
        grid_spec=pltpu.PrefetchScalarGridSpec(
            num_scalar_prefetch=1, grid=(rows // tr,),
            in_specs=[pl.BlockSpec((None, tr, cols), lambda r, ix: (ix[0], r, 0)),
                      pl.BlockSpec((N_DEV, tr, cols), lambda r, ix: (0, r, 0)), tile, tile, tile],
            out_specs=[tile] * 4),
        compiler_params=_cp(("arbitrary",)),
    )(me, g, land, w, m, v)


def _adam_math(w, g, m, v):
    m = ADAM_B1 * m + (1.0 - ADAM_B1) * g
    v = ADAM_B2 * v + (1.0 - ADAM_B2) * (g * g)
    m_hat = m / (1.0 - ADAM_B1 ** ADAM_STEP)
    v_hat = v / (1.0 - ADAM_B2 ** ADAM_STEP)
    delta = -ADAM_LR * (m_hat / (jnp.sqrt(v_hat) + ADAM_EPS) + ADAM_WD * w)
    return delta, m, v


def adamw_small(parts, w, m, v, name):
    _, rows, cols = parts.shape

    def body(p_ref, w_ref, m_ref, v_ref, go_ref, d_ref, mo_ref, vo_ref):
        grad = p_ref[0]
        for k in range(1, N_DEV):
            grad = grad + p_ref[k]
        delta, m_new, v_new = _adam_math(w_ref[...], grad, m_ref[...], v_ref[...])
        go_ref[...] = grad
        d_ref[...] = delta
        mo_ref[...] = m_new
        vo_ref[...] = v_new

    out = jax.ShapeDtypeStruct((rows, cols), F32)
    return _pcall(body, name=name, out_shape=[out] * 4, compiler_params=_cp(vmem=VMEM_BIG))(parts, w, m, v)


def adamw_plain(g, w, m, v, name):
    def body(g_ref, w_ref, m_ref, v_ref, d_ref, mo_ref, vo_ref):
        delta, m_new, v_new = _adam_math(w_ref[...], g_ref[...], m_ref[...], v_ref[...])
        d_ref[...] = delta
        mo_ref[...] = m_new
        vo_ref[...] = v_new

    out = jax.ShapeDtypeStruct(w.shape, F32)
    return _pcall(body, name=name, out_shape=[out] * 3)(g, w, m, v)


def _rms_fwd(x, g):
    r = lax.rsqrt(jnp.mean(x * x, axis=-1, keepdims=True) + NORM_EPS)
    return x * r * g


def _rms_bwd(x, g, dy):
    r = lax.rsqrt(jnp.mean(x * x, axis=-1, keepdims=True) + NORM_EPS)
    xh = x * r
    t = dy * g
    dx = r * (t - xh * jnp.mean(t * xh, axis=-1, keepdims=True))
    return dx, jnp.sum(dy * xh, axis=0, keepdims=True)


def _accumulate(ref, val, first):
    @pl.when(first)
    def _():
        ref[...] = val

    @pl.when(jnp.logical_not(first))
    def _():
        ref[...] += val


def _col_chunks(width):
    return [(c0, min(MXU_WIDTH, width - c0)) for c0 in range(0, width, MXU_WIDTH)]


ANY_SPEC = pl.BlockSpec(memory_space=pl.ANY)


def ffn_forward(h, norm, w1, w3, w2, after, tm, name, meta=None):
    if meta is None:
        t_rows = h.shape[0]
        h_spec = pl.BlockSpec((tm, D_MODEL), lambda i: (i, 0))
    else:
        tiles = (h.shape[1] + N_META) // tm
        t_rows = h.shape[0] * tiles * tm
        h_spec = pl.BlockSpec((None, tm, D_MODEL), lambda i: (i // tiles, i % tiles, 0))

    split = w2 is None

    def body(h_ref, g_ref, w1_ref, w3_ref, *rest):
        w2_ref = None if split else rest[0]
        rest = rest[1:] if split else rest[2:]
        if meta is None:
            out_ref, hn_ref, a_ref, b_ref = rest[:4]
            h_in = h_ref[...]
        else:
            meta_ref, out_ref, hn_ref, a_ref, b_ref, h0_ref = rest[:6]
            h_in = h_ref[...]
            with_meta = jnp.concatenate([h_in[:tm - N_META], meta_ref[...]], axis=0)
            h_in = jnp.where(pl.program_id(0) % tiles == tiles - 1, with_meta, h_in)
            h0_ref[...] = h_in
        hn = _rms_fwd(h_in, g_ref[...]).astype(BF16)
        hn_ref[...] = hn
        hid_ref = out_ref if split else rest[-1]
        for c0, cw in _col_chunks(D_FF):
            a = _dot_nt(hn, w1_ref[c0:c0 + cw, :])
            b = _dot_nt(hn, w3_ref[c0:c0 + cw, :])
            a_ref[:, c0:c0 + cw] = a.astype(BF16)
            b_ref[:, c0:c0 + cw] = b.astype(BF16)
            hid_ref[:, c0:c0 + cw] = (a * _sigmoid(a) * b).astype(BF16)
        if not split:
            out_ref[...] = h_in + 0.5 * _dot(hid_ref[...], w2_ref[...])

    row = pl.BlockSpec((tm, D_MODEL), lambda i: (i, 0))
    hid_blk = pl.BlockSpec((tm, D_FF), lambda i: (i, 0))
    weight = _resident((D_FF, D_MODEL))
    wide = jax.ShapeDtypeStruct((t_rows, D_MODEL), F32)
    hid_shape = jax.ShapeDtypeStruct((t_rows, D_FF), BF16)
    extra_in = [] if meta is None else [meta]
    return _pcall(
        body, name=name, grid=(t_rows // tm,),
        in_specs=[h_spec, pl.BlockSpec((1, D_MODEL), lambda i: (0, 0))] + [weight] * (2 if split else 3) + [ANY_SPEC]
        + [pl.BlockSpec((N_META, D_MODEL), lambda i: (0, 0))] * len(extra_in),
        out_specs=[hid_blk if split else row, row, hid_blk, hid_blk] + [row] * len(extra_in),
        out_shape=[hid_shape if split else wide, jax.ShapeDtypeStruct((t_rows, D_MODEL), BF16), hid_shape, hid_shape]
        + [wide] * len(extra_in),
        scratch_shapes=[] if split else [pltpu.VMEM((tm, D_FF), BF16)],
        compiler_params=_cp(("arbitrary",), VMEM_BIG),
    )(h, norm, w1, w3, *([] if split else [w2]), after, *extra_in)


def ffn_down(h, hid, w2, tm, name):
    t_rows = h.shape[0]

    def body(h_ref, hid_ref, w2_ref, out_ref):
        out_ref[...] = h_ref[...] + 0.5 * _dot(hid_ref[...], w2_ref[...])

    row = pl.BlockSpec((tm, D_MODEL), lambda i: (i, 0))
    return _pcall(
        body, name=name, grid=(t_rows // tm,),
        in_specs=[row, pl.BlockSpec((tm, D_FF), lambda i: (i, 0)), _resident((D_FF, D_MODEL))],
        out_specs=[row],
        out_shape=[jax.ShapeDtypeStruct((t_rows, D_MODEL), F32)],
        compiler_params=_cp(("arbitrary",), VMEM_BIG),
    )(h, hid, w2)


def _resident(shape):
    return pl.BlockSpec(shape, lambda *_: (0,) * len(shape), pipeline_mode=pl.Buffered(1))


def ffn_backward_hidden(dh, a, b, w2, after, tm, name):
    t_rows = dh.shape[0]

    def body(dh_ref, a_ref, b_ref, w2_ref, _, da_ref, db_ref, dhb_ref):
        dhb = (0.5 * dh_ref[...]).astype(BF16)
        dhb_ref[...] = dhb
        for c0, cw in _col_chunks(D_FF):
            dhid = _dot_nt(dhb, w2_ref[c0:c0 + cw, :])
            av = a_ref[:, c0:c0 + cw].astype(F32)
            bv = b_ref[:, c0:c0 + cw].astype(F32)
            s = _sigmoid(av)
            da_ref[:, c0:c0 + cw] = (dhid * bv * (s * (1.0 + av * (1.0 - s)))).astype(BF16)
            db_ref[:, c0:c0 + cw] = (dhid * (av * s)).astype(BF16)

    hid = pl.BlockSpec((tm, D_FF), lambda i: (i, 0))
    row = pl.BlockSpec((tm, D_MODEL), lambda i: (i, 0))
    return _pcall(
        body, name=name, grid=(t_rows // tm,),
        in_specs=[row, hid, hid, _resident((D_FF, D_MODEL)), ANY_SPEC],
        out_specs=[hid, hid, row],
        out_shape=[jax.ShapeDtypeStruct((t_rows, D_FF), BF16), jax.ShapeDtypeStruct((t_rows, D_FF), BF16),
                   jax.ShapeDtypeStruct((t_rows, D_MODEL), BF16)],
        compiler_params=_cp(("arbitrary",), VMEM_BIG),
    )(dh, a, b, w2, after)


def ffn_backward_input(dh, h, norm, da, db, w1, w3, after, tm, name, examples=None):
    t_rows = h.shape[0]

    def body(dh_ref, h_ref, g_ref, da_ref, db_ref, w1_ref, w3_ref, _, dhin_ref, *rest):
        dg_ref = rest[-1]
        dhn = _dot(da_ref[...], w1_ref[...]) + _dot(db_ref[...], w3_ref[...])
        dx, dg = _rms_bwd(h_ref[...], g_ref[...], dhn)
        dhin = dh_ref[...] + dx
        dhin_ref[...] = dhin
        _accumulate(dg_ref, dg, pl.program_id(0) == 0)
        if examples is not None:
            @pl.when(pl.program_id(0) % tiles == tiles - 1)
            def _():
                rest[0][...] = dhin[tm - N_META:, :]

    row = pl.BlockSpec((tm, D_MODEL), lambda i: (i, 0))
    vec = pl.BlockSpec((1, D_MODEL), lambda i: (0, 0))
    hid = pl.BlockSpec((tm, D_FF), lambda i: (i, 0))
    if examples is None:
        out_specs = [row, vec]
        out_shape = [jax.ShapeDtypeStruct((t_rows, D_MODEL), F32), jax.ShapeDtypeStruct((1, D_MODEL), F32)]
    else:
        n_b, seq = examples
        tiles = (seq + N_META) // tm
        out_specs = [pl.BlockSpec((None, tm, D_MODEL), lambda i: (i // tiles, i % tiles, 0)),
                     pl.BlockSpec((None, N_META, D_MODEL), lambda i: (i // tiles, 0, 0)), vec]
        out_shape = [jax.ShapeDtypeStruct((n_b, seq, D_MODEL), F32), jax.ShapeDtypeStruct((n_b, N_META, D_MODEL), F32),
                     jax.ShapeDtypeStruct((1, D_MODEL), F32)]
    return _pcall(
        body, name=name, grid=(t_rows // tm,),
        in_specs=[row, row, vec, hid, hid, _resident((D_FF, D_MODEL)), _resident((D_FF, D_MODEL)), ANY_SPEC],
        out_specs=out_specs, out_shape=out_shape,
        compiler_params=_cp(("arbitrary",), VMEM_BIG),
    )(dh, h, norm, da, db, w1, w3, after)


def ffn_backward_weights(hn, dh, a, b, da, db, tm, tn, name):
    t_rows = hn.shape[0]
    ni = t_rows // tm
    kc = _row_tile(tm, 688)

    def body(hn_ref, dh_ref, a_ref, b_ref, da_ref, db_ref, dw1_ref, dw3_ref, dw2_ref, acc1, acc3, acc2):
        i = pl.program_id(1)
        parts = None
        for r0 in range(0, tm, kc):
            rows = slice(r0, r0 + kc)
            hn_v = hn_ref[rows, :]
            av = a_ref[rows, :].astype(F32)
            hid = (av * _sigmoid(av) * b_ref[rows, :].astype(F32)).astype(BF16)
            new = (_dot_tn(hn_v, da_ref[rows, :]), _dot_tn(hn_v, db_ref[rows, :]), _dot_tn(dh_ref[rows, :], hid))
            parts = new if parts is None else tuple(p + q for p, q in zip(parts, new))
        _accumulate(acc1, parts[0], i == 0)
        _accumulate(acc3, parts[1], i == 0)
        _accumulate(acc2, parts[2], i == 0)

        @pl.when(i == ni - 1)
        def _():
            dw1_ref[...] = acc1[...].T.astype(BF16)
            dw3_ref[...] = acc3[...].T.astype(BF16)
            dw2_ref[...] = acc2[...].T.astype(BF16)

    row = pl.BlockSpec((tm, D_MODEL), lambda j, i: (i, 0))
    hid_blk = pl.BlockSpec((tm, tn), lambda j, i: (i, j))
    w_row = pl.BlockSpec((tn, D_MODEL), lambda j, i: (j, 0))
    out = jax.ShapeDtypeStruct((D_FF, D_MODEL), BF16)
    return _pcall(
        body, name=name, grid=(D_FF // tn, ni),
        in_specs=[row, row, hid_blk, hid_blk, hid_blk, hid_blk],
        out_specs=[w_row, w_row, w_row], out_shape=[out, out, out],
        scratch_shapes=[pltpu.VMEM((D_MODEL, tn), F32)] * 3,
        compiler_params=_cp(("arbitrary", "arbitrary"), VMEM_BIG),
    )(hn, dh, a, b, da, db)


def mix_forward(h, norm, wing, tm, name):
    t_rows = h.shape[0]

    def body(h_ref, g_ref, w_ref, hn_ref, p_ref):
        hn = _rms_fwd(h_ref[...], g_ref[...]).astype(BF16)
        hn_ref[...] = hn
        for j in range(N_DEV):
            p_ref[:, j * IN_BLK:(j + 1) * IN_BLK] = _dot(hn, w_ref[j]).astype(BF16)

    row = pl.BlockSpec((tm, D_MODEL), lambda i: (i, 0))
    return _pcall(
        body, name=name, grid=(t_rows // tm,),
        in_specs=[row, pl.BlockSpec((1, D_MODEL), lambda i: (0, 0)),
                  pl.BlockSpec((N_DEV, D_MODEL, IN_BLK), lambda i: (0, 0, 0))],
        out_specs=[row, pl.BlockSpec((tm, IN_WIDTH), lambda i: (i, 0))],
        out_shape=[jax.ShapeDtypeStruct((t_rows, D_MODEL), BF16), jax.ShapeDtypeStruct((t_rows, IN_WIDTH), BF16)],
        compiler_params=_cp(("arbitrary",), VMEM_BIG),
    )(h, norm, wing)


def mix_backward_act(dh, h, norm, dproj, w_in_full, tm, name):
    t_rows = h.shape[0]

    def body(dh_ref, h_ref, g_ref, dp_ref, w_ref, dhin_ref, dg_ref):
        dx, dg = _rms_bwd(h_ref[...], g_ref[...], _dot_nt(dp_ref[...], w_ref[...]))
        dhin_ref[...] = dh_ref[...] + dx
        _accumulate(dg_ref, dg, pl.program_id(0) == 0)

    row = pl.BlockSpec((tm, D_MODEL), lambda i: (i, 0))
    vec = pl.BlockSpec((1, D_MODEL), lambda i: (0, 0))
    return _pcall(
        body, name=name, grid=(t_rows // tm,),
        in_specs=[row, row, vec, pl.BlockSpec((tm, IN_WIDTH), lambda i: (i, 0)), _resident((D_MODEL, IN_WIDTH))],
        out_specs=[row, vec],
        out_shape=[jax.ShapeDtypeStruct((t_rows, D_MODEL), F32), jax.ShapeDtypeStruct((1, D_MODEL), F32)],
        compiler_params=_cp(("arbitrary",), VMEM_BIG),
    )(dh, h, norm, dproj, w_in_full)


def mix_backward_weights(hn, dproj, tm, name):
    t_rows = hn.shape[0]
    ni = t_rows // tm
    per_step = 2

    kc = _row_tile(tm, 688)

    def body(hn_ref, dp_ref, dw_ref, acc):
        i = pl.program_id(1)
        part = functools.reduce(lambda u, w: u + w, [_dot_tn(hn_ref[r0:r0 + kc, :], dp_ref[r0:r0 + kc, :])
                                                    for r0 in range(0, tm, kc)])
        _accumulate(acc, part, i == 0)

        @pl.when(i == ni - 1)
        def _():
            for k in range(per_step):
                dw_ref[k] = acc[:, k * IN_BLK:(k + 1) * IN_BLK].astype(BF16)

    return _pcall(
        body, name=name, grid=(N_DEV // per_step, ni),
        in_specs=[pl.BlockSpec((tm, D_MODEL), lambda j, i: (i, 0)),
                  pl.BlockSpec((tm, per_step * IN_BLK), lambda j, i: (i, j))],
        out_specs=pl.BlockSpec((per_step, D_MODEL, IN_BLK), lambda j, i: (j, 0, 0)),
        out_shape=jax.ShapeDtypeStruct((N_DEV, D_MODEL, IN_BLK), BF16),
        scratch_shapes=[pltpu.VMEM((D_MODEL, per_step * IN_BLK), F32)],
        compiler_params=_cp(("arbitrary", "arbitrary"), VMEM_BIG),
    )(hn, dproj)


GELU_C = 0.7978845608028654
GELU_K = 0.044715


def _gelu(x):
    return 0.5 * x * (1.0 + jnp.tanh(GELU_C * (x + GELU_K * (x * x * x))))


def _gelu_and_grad(x):
    th = jnp.tanh(GELU_C * (x + GELU_K * (x * x * x)))
    val = 0.5 * x * (1.0 + th)
    grad = 0.5 * (1.0 + th) + 0.5 * x * (1.0 - th * th) * (GELU_C * (1.0 + 3.0 * GELU_K * (x * x)))
    return val, grad


def merge_forward(h, yraw, attn, proj, glu_a, glu_b, w_out, tm, name):
    t_rows = h.shape[0]

    def body(h_ref, y_ref, at_ref, gate_ref, a_ref, b_ref, wo_ref, out_ref):
        y = _gelu(y_ref[...]).astype(BF16)
        ssm = _dot(y, a_ref[...]) * _sigmoid(_dot(y, b_ref[...]))
        ga = gate_ref[:, :D_MODEL].astype(F32)
        gs = gate_ref[:, D_MODEL:].astype(F32)
        merged = _sigmoid(ga) * at_ref[...].astype(F32) + _sigmoid(gs) * ssm
        out_ref[...] = h_ref[...] + _dot(merged.astype(BF16), wo_ref[...])

    row = pl.BlockSpec((tm, D_MODEL), lambda i: (i, 0))
    glu = pl.BlockSpec((SSM_WIDTH, D_MODEL), lambda i: (0, 0))
    return _pcall(
        body, name=name, grid=(t_rows // tm,),
        in_specs=[row, pl.BlockSpec((tm, SSM_WIDTH), lambda i: (i, 0)), row,
                  pl.BlockSpec((tm, 2 * D_MODEL), lambda i: (i, 1)), glu, glu,
                  pl.BlockSpec((D_MODEL, D_MODEL), lambda i: (0, 0))],
        out_specs=row, out_shape=jax.ShapeDtypeStruct((t_rows, D_MODEL), F32),
        compiler_params=_cp(("arbitrary",), VMEM_BIG),
    )(h, yraw, attn, proj, glu_a, glu_b, w_out)


def merge_backward(dh, yraw, attn, proj, glu_a, glu_b, w_out, after, tm, name):
    t_rows = dh.shape[0]

    def body(dh_ref, y_ref, at_ref, gate_ref, a_ref, b_ref, wo_ref, _,
             dat_ref, dy_ref, dgate_ref, d16_ref, mg_ref, y16_ref, dya_ref, dyb_ref):
        d16 = dh_ref[...].astype(BF16)
        d16_ref[...] = d16
        gel, dgel = _gelu_and_grad(y_ref[...].astype(F32))
        y16 = gel.astype(BF16)
        y16_ref[...] = y16
        dy = None
        for c0, cw in _col_chunks(D_MODEL):
            cols = slice(c0, c0 + cw)
            gcols = slice(D_MODEL + c0, D_MODEL + c0 + cw)
            dmerged = _dot_nt(d16, wo_ref[cols, :])
            ya = _dot(y16, a_ref[:, cols])
            sb = _sigmoid(_dot(y16, b_ref[:, cols]))
            ssm = ya * sb
            sa = _sigmoid(gate_ref[:, cols].astype(F32))
            ss = _sigmoid(gate_ref[:, gcols].astype(F32))
            attn_v = at_ref[:, cols].astype(F32)
            mg_ref[:, cols] = (sa * attn_v + ss * ssm).astype(BF16)
            dat_ref[:, cols] = (dmerged * sa).astype(BF16)
            dgate_ref[:, cols] = (dmerged * attn_v * sa * (1.0 - sa)).astype(BF16)
            dgate_ref[:, gcols] = (dmerged * ssm * ss * (1.0 - ss)).astype(BF16)
            dssm = dmerged * ss
            dya = (dssm * sb).astype(BF16)
            dyb = (dssm * ya * sb * (1.0 - sb)).astype(BF16)
            dya_ref[:, cols] = dya
            dyb_ref[:, cols] = dyb
            part = _dot_nt(dya, a_ref[:, cols]) + _dot_nt(dyb, b_ref[:, cols])
            dy = part if dy is None else dy + part
        dy_ref[...] = (dy * dgel).astype(BF16)

    row = pl.BlockSpec((tm, D_MODEL), lambda i: (i, 0))
    ssm_row = pl.BlockSpec((tm, SSM_WIDTH), lambda i: (i, 0))
    gates = pl.BlockSpec((tm, 2 * D_MODEL), lambda i: (i, 1))
    wide = jax.ShapeDtypeStruct((t_rows, D_MODEL), BF16)
    narrow = jax.ShapeDtypeStruct((t_rows, SSM_WIDTH), BF16)
    return _pcall(
        body, name=name, grid=(t_rows // tm,),
        in_specs=[row, ssm_row, row, gates, _resident((SSM_WIDTH, D_MODEL)), _resident((SSM_WIDTH, D_MODEL)),
                  _resident((D_MODEL, D_MODEL)), ANY_SPEC],
        out_specs=[row, ssm_row, gates, row, row, ssm_row, row, row],
        out_shape=[wide, narrow, jax.ShapeDtypeStruct((t_rows, IN_WIDTH), BF16), wide, wide, narrow, wide, wide],
        compiler_params=_cp(("arbitrary",), VMEM_BIG),
    )(dh, yraw, attn, proj, glu_a, glu_b, w_out, after)


def merge_backward_weights(d16, merged, y16, dya, dyb, tm, name):
    t_rows = d16.shape[0]

    def body(d_ref, mg_ref, y_ref, dya_ref, dyb_ref, dwo_ref, da_ref, db_ref):
        first = pl.program_id(0) == 0
        y16 = y_ref[...]
        _accumulate(dwo_ref, _dot_tn(mg_ref[...], d_ref[...]), first)
        _accumulate(da_ref, _dot_tn(y16, dya_ref[...]), first)
        _accumulate(db_ref, _dot_tn(y16, dyb_ref[...]), first)

    row = pl.BlockSpec((tm, D_MODEL), lambda i: (i, 0))
    ssm_row = pl.BlockSpec((tm, SSM_WIDTH), lambda i: (i, 0))
    glu = pl.BlockSpec((SSM_WIDTH, D_MODEL), lambda i: (0, 0))
    wo = pl.BlockSpec((D_MODEL, D_MODEL), lambda i: (0, 0))
    return _pcall(
        body, name=name, grid=(t_rows // tm,),
        in_specs=[row, row, ssm_row, row, row], out_specs=[wo, glu, glu],
        out_shape=[jax.ShapeDtypeStruct((D_MODEL, D_MODEL), F32), jax.ShapeDtypeStruct((SSM_WIDTH, D_MODEL), F32),
                   jax.ShapeDtypeStruct((SSM_WIDTH, D_MODEL), F32)],
        compiler_params=_cp(("arbitrary",), VMEM_BIG),
    )(d16, merged, y16, dya, dyb)


def final_loss_backward(h, target, norm, seq, tm, name):
    t_rows = h.shape[0]
    tiles_per_example = (seq + N_META) // tm

    def body(h_ref, t_ref, g_ref, dh_ref, loss_ref, dg_ref):
        i = pl.program_id(0)
        x = h_ref[...]
        g = g_ref[...]
        r = lax.rsqrt(jnp.mean(x * x, axis=-1, keepdims=True) + NORM_EPS)
        xh = x * r
        pos = lax.broadcasted_iota(jnp.int32, (tm, 1), 0) + (i % tiles_per_example) * tm
        diff = jnp.where(pos < seq, xh * g - t_ref[...], 0.0)
        part = 0.5 * jnp.sum(jnp.sum(diff * diff, axis=-1, keepdims=True), axis=0, keepdims=True) / D_MODEL
        dy = diff / D_MODEL
        t = dy * g
        dh_ref[...] = r * (t - xh * jnp.mean(t * xh, axis=-1, keepdims=True))
        _accumulate(loss_ref, jnp.broadcast_to(part, (1, LANES)), i == 0)
        _accumulate(dg_ref, jnp.sum(dy * xh, axis=0, keepdims=True), i == 0)

    row = pl.BlockSpec((tm, D_MODEL), lambda i: (i, 0))
    vec = pl.BlockSpec((1, D_MODEL), lambda i: (0, 0))
    per_example = pl.BlockSpec((None, tm, D_MODEL), lambda i: (i // tiles_per_example, i % tiles_per_example, 0))
    return _pcall(
        body, name=name, grid=(t_rows // tm,),
        in_specs=[row, per_example, vec],
        out_specs=[row, pl.BlockSpec((1, LANES), lambda i: (0, 0)), vec],
        out_shape=[jax.ShapeDtypeStruct((t_rows, D_MODEL), F32), jax.ShapeDtypeStruct((1, LANES), F32),
                   jax.ShapeDtypeStruct((1, D_MODEL), F32)],
        compiler_params=_cp(("arbitrary",), VMEM_BIG),
    )(h, target, norm)


ATTN_SCALE = HEAD_DIM ** -0.5
STACK_HEADS = (0, 2, 1, 3)
META_PAD = LANES - N_META


def _lane_half(shape, hf):
    lane = lax.broadcasted_iota(jnp.int32, shape, 1)
    return (lane < HEAD_DIM) if hf == 0 else (lane >= HEAD_DIM)


def _kv_variants(ref, rows, kh, pad_rows=0):
    tile = kh // 2
    t = ref[rows, tile * LANES:(tile + 1) * LANES].astype(F32)
    swapped = pltpu.roll(t, HEAD_DIM, axis=1)
    at_low, at_high = (t, swapped) if kh % 2 == 0 else (swapped, t)
    lo = jnp.where(_lane_half(t.shape, 0), at_low, 0.0).astype(BF16)
    hi = jnp.where(_lane_half(t.shape, 1), at_high, 0.0).astype(BF16)
    if pad_rows:
        zeros = jnp.zeros((pad_rows, LANES), BF16)
        lo, hi = jnp.concatenate([lo, zeros], axis=0), jnp.concatenate([hi, zeros], axis=0)
    return lo, hi


def _key_tiles(ref, key_rows, kh):
    return [_kv_variants(ref, r, kh, META_PAD if i == len(key_rows) - 1 else 0) for i, r in enumerate(key_rows)]


def _to_kv_lanes(lo, hi, kh):
    lo = jnp.where(_lane_half(lo.shape, 0), lo, 0.0)
    hi = jnp.where(_lane_half(hi.shape, 1), hi, 0.0)
    if kh % 2 == 0:
        return lo + pltpu.roll(hi, HEAD_DIM, axis=1)
    return pltpu.roll(lo, HEAD_DIM, axis=1) + hi


def _stacked(ref, rows, kh):
    col = kh * 2 * LANES
    return jnp.concatenate([ref[rows, col:col + LANES], ref[rows, col + LANES:col + 2 * LANES]], axis=0)


def _sink_column(sink_ref, kh, nq):
    row = lax.broadcasted_iota(jnp.int32, (4 * nq, 1), 0)
    col = jnp.zeros((4 * nq, 1), F32)
    for quarter, g in enumerate(STACK_HEADS):
        col = jnp.where(row // nq == quarter, sink_ref[0, kh * Q_PER_KV + g], col)
    return col


def _softmax_parts(qs, key_tiles, masks, sink):
    scores = []
    for (k_lo, k_hi), mask in zip(key_tiles, masks):
        s = jnp.concatenate([_dot_nt(qs, k_lo), _dot_nt(qs, k_hi)], axis=0) * ATTN_SCALE
        scores.append(s if mask is None else jnp.where(mask, s, NEG_INF))
    m = jnp.maximum(_row_reduce(scores, jnp.maximum, jnp.max), sink)
    probs = [jnp.exp(s - m) for s in scores]
    e_sink = jnp.exp(sink - m)
    den = _row_sums(probs) + e_sink
    return probs, 1.0 / den, e_sink


def _row_reduce(tiles, combine, reduce):
    chunks = [t[:, c:c + LANES] for t in tiles for c in range(0, t.shape[-1], LANES)]
    return reduce(functools.reduce(combine, chunks), axis=-1, keepdims=True)


def _row_sums(tiles):
    return _row_reduce(tiles, lambda u, w: u + w, jnp.sum)


def _band_mask(nq, first):
    keys = BLOCK if first else 2 * BLOCK
    qi = lax.broadcasted_iota(jnp.int32, (4 * nq, keys), 0) % nq
    kj = lax.broadcasted_iota(jnp.int32, (4 * nq, keys), 1)
    if first:
        return kj <= qi
    return jnp.logical_and(kj > qi, kj <= qi + BLOCK)


def _meta_mask(nq, causal):
    qi = lax.broadcasted_iota(jnp.int32, (4 * nq, LANES), 0) % nq
    kj = lax.broadcasted_iota(jnp.int32, (4 * nq, LANES), 1)
    return jnp.logical_and(kj < N_META, kj <= qi) if causal else kj < N_META


def _attention_schedule(seq, queries, carry):
    meta_rows = pl.ds(seq, N_META)
    meta_ok = _meta_mask(BLOCK, False)
    carry = queries(pl.ds(0, BLOCK), BLOCK, [pl.ds(0, BLOCK), meta_rows], [_band_mask(BLOCK, True), meta_ok], carry)

    def block(n, c):
        r0 = pl.multiple_of(n * BLOCK, BLOCK)
        p0 = pl.multiple_of((n - 1) * BLOCK, BLOCK)
        return queries(pl.ds(r0, BLOCK), BLOCK, [pl.ds(p0, 2 * BLOCK), meta_rows], [_band_mask(BLOCK, False), meta_ok], c)

    carry = lax.fori_loop(1, seq // BLOCK, block, carry)
    return queries(meta_rows, N_META, [meta_rows], [_meta_mask(N_META, True)], carry)


def attention_forward(proj3, sinks, seq, name):
    n_b, n_l, _ = proj3.shape

    def body(sink_ref, q_ref, k_ref, v_ref, o_ref):
        def queries(q_rows, nq, key_rows, masks, carry):
            for kh in range(N_KV_HEADS):
                ks = _key_tiles(k_ref, key_rows, kh)
                vs = _key_tiles(v_ref, key_rows, kh)
                qs = _stacked(q_ref, q_rows, kh)
                probs, inv, _ = _softmax_parts(qs, ks, masks, _sink_column(sink_ref, kh, nq))
                probs = [p.astype(BF16) for p in probs]
                o_lo = functools.reduce(lambda u, w: u + w, [_dot(p[:2 * nq], v_lo) for p, (v_lo, _) in zip(probs, vs)])
                o_hi = functools.reduce(lambda u, w: u + w, [_dot(p[2 * nq:], v_hi) for p, (_, v_hi) in zip(probs, vs)])
                out = (o_lo * inv[:2 * nq] + o_hi * inv[2 * nq:]).astype(BF16)
                col = kh * 2 * LANES
                o_ref[q_rows, col:col + LANES] = out[:nq]
                o_ref[q_rows, col + LANES:col + 2 * LANES] = out[nq:]
            return carry

        _attention_schedule(seq, queries, 0)

    return _pcall(
        body, name=name, grid=(n_b,),
        in_specs=[pl.BlockSpec(memory_space=pltpu.SMEM),
                  pl.BlockSpec((None, n_l, D_MODEL), lambda b: (b, 0, 0)),
                  pl.BlockSpec((None, n_l, KV_WIDTH), lambda b: (b, 0, D_MODEL // KV_WIDTH)),
                  pl.BlockSpec((None, n_l, KV_WIDTH), lambda b: (b, 0, D_MODEL // KV_WIDTH + 1))],
        out_specs=pl.BlockSpec((None, n_l, D_MODEL), lambda b: (b, 0, 0)),
        out_shape=jax.ShapeDtypeStruct((n_b, n_l, D_MODEL), BF16),
        compiler_params=_cp(("arbitrary",), VMEM_BIG),
    )(sinks, proj3, proj3, proj3)


def attention_backward(proj3, dattn3, dproj3, sinks, after, seq, name):
    n_b, n_l, _ = proj3.shape
    qkv_width = D_MODEL + 2 * KV_WIDTH

    def body(sink_ref, q_ref, k_ref, v_ref, do_ref, _, __, dqkv_ref, dsink_ref, dk_ref, dv_ref):
        dk_ref[...] = jnp.zeros_like(dk_ref)
        dv_ref[...] = jnp.zeros_like(dv_ref)
        sub = lax.broadcasted_iota(jnp.int32, (SUBLANES, LANES), 0)
        lane = lax.broadcasted_iota(jnp.int32, (SUBLANES, LANES), 1)

        def queries(q_rows, nq, key_rows, masks, dsink):
            for kh in range(N_KV_HEADS):
                ks = _key_tiles(k_ref, key_rows, kh)
                vs = _key_tiles(v_ref, key_rows, kh)
                qs = _stacked(q_ref, q_rows, kh)
                dos = _stacked(do_ref, q_rows, kh)
                probs, inv, e_sink = _softmax_parts(qs, ks, masks, _sink_column(sink_ref, kh, nq))
                probs = [p * inv for p in probs]
                dps = [jnp.concatenate([_dot_nt(dos, v_lo), _dot_nt(dos, v_hi)], axis=0) for v_lo, v_hi in vs]
                delta = _row_sums([p * dp for p, dp in zip(probs, dps)])
                d_sink = -(e_sink * inv) * delta
                for quarter, g in enumerate(STACK_HEADS):
                    d_here = jnp.sum(d_sink[quarter * nq:(quarter + 1) * nq], axis=0, keepdims=True)
                    dsink = dsink + jnp.where(jnp.logical_and(sub == 0, lane == kh * Q_PER_KV + g), d_here, 0.0)
                dq = None
                tile = slice((kh // 2) * LANES, (kh // 2 + 1) * LANES)
                for r, p, dp, (k_lo, k_hi) in zip(key_rows, probs, dps, ks):
                    ds = (p * (dp - delta)).astype(BF16)
                    p16 = p.astype(BF16)
                    dq_x = _dot(ds[:2 * nq], k_lo) + _dot(ds[2 * nq:], k_hi)
                    dq = dq_x if dq is None else dq + dq_x
                    d_k = _to_kv_lanes(_dot_tn(ds[:2 * nq], qs), _dot_tn(ds[2 * nq:], qs), kh) * ATTN_SCALE
                    d_v = _to_kv_lanes(_dot_tn(p16[:2 * nq], dos), _dot_tn(p16[2 * nq:], dos), kh)
                    n_keys = r.size
                    dk_ref[r, tile] += d_k[:n_keys]
                    dv_ref[r, tile] += d_v[:n_keys]
                dq = (dq * ATTN_SCALE).astype(BF16)
                col = kh * 2 * LANES
                dqkv_ref[q_rows, col:col + LANES] = dq[:nq]
                dqkv_ref[q_rows, col + LANES:col + 2 * LANES] = dq[nq:]
            return dsink

        dsink_ref[...] = _attention_schedule(seq, queries, jnp.zeros((SUBLANES, LANES), F32))
        dqkv_ref[:, D_MODEL:D_MODEL + KV_WIDTH] = dk_ref[...].astype(BF16)
        dqkv_ref[:, D_MODEL + KV_WIDTH:] = dv_ref[...].astype(BF16)

    return _pcall(
        body, name=name, grid=(n_b,),
        in_specs=[pl.BlockSpec(memory_space=pltpu.SMEM),
                  pl.BlockSpec((None, n_l, D_MODEL), lambda b: (b, 0, 0)),
                  pl.BlockSpec((None, n_l, KV_WIDTH), lambda b: (b, 0, D_MODEL // KV_WIDTH)),
                  pl.BlockSpec((None, n_l, KV_WIDTH), lambda b: (b, 0, D_MODEL // KV_WIDTH + 1)),
                  pl.BlockSpec((None, n_l, D_MODEL), lambda b: (b, 0, 0)),
                  ANY_SPEC, ANY_SPEC],
        out_specs=[pl.BlockSpec((None, n_l, qkv_width), lambda b: (b, 0, 0)),
                   pl.BlockSpec((None, SUBLANES, LANES), lambda b: (b, 0, 0))],
        out_shape=[jax.ShapeDtypeStruct(dproj3.shape, BF16), jax.ShapeDtypeStruct((n_b, SUBLANES, LANES), F32)],
        scratch_shapes=[pltpu.VMEM((n_l, KV_WIDTH), F32), pltpu.VMEM((n_l, KV_WIDTH), F32)],
        input_output_aliases={5: 0},
        compiler_params=_cp(("arbitrary",), VMEM_BIG),
    )(sinks, proj3, proj3, proj3, dattn3, dproj3, after)


TAB_ROWS = 8
SCAN_UNROLL = 4


def _cmul(ar, ai, br, bi):
    return ar * br - ai * bi, ar * bi + ai * br


def _discretise(ar, ai, ls):
    step = jnp.exp(ls)
    mag = jnp.exp(ar * step)
    ang = ai * step
    cos, sin = jnp.cos(ang), jnp.sin(ang)
    lr, li = mag * cos, mag * sin
    den = ar * ar + ai * ai
    nr, ni = lr - 1.0, li
    cr = (nr * ar + ni * ai) / den
    ci = (ni * ar - nr * ai) / den
    return step, mag, lr, li, den, nr, ni, cr, ci


def _scan_tables(lr, li, reverse):
    n = lr.shape[-1]
    pw = [(lr, li)]
    for _ in range(SUBLANES - 1):
        pw.append(_cmul(pw[-1][0], pw[-1][1], lr, li))
    row = lax.broadcasted_iota(jnp.int32, (SUBLANES, n), 0)
    out = []
    for d in (1, 2, 4):
        ok = (row + d <= SUBLANES - 1) if reverse else (row >= d)
        out += [jnp.where(ok, pw[d - 1][0], 0.0), jnp.where(ok, pw[d - 1][1], 0.0)]
    cr = jnp.zeros((SUBLANES, n), F32)
    ci = jnp.zeros((SUBLANES, n), F32)
    for r in range(SUBLANES):
        e = (SUBLANES - r) if reverse else (r + 1)
        cr = jnp.where(row == r, pw[e - 1][0], cr)
        ci = jnp.where(row == r, pw[e - 1][1], ci)
    return out + [cr, ci]


def ssm_prepare(ar, ai, ls, br_t, bi_t, name):
    def body(ar_ref, ai_ref, ls_ref, br_ref, bi_ref, bbr_ref, bbi_ref, tf_ref, tr_ref):
        _, _, lr, li, _, _, _, cr, ci = _discretise(ar_ref[...], ai_ref[...], ls_ref[...])
        br, bi = br_ref[...], bi_ref[...]
        bbr_ref[...] = cr * br - ci * bi
        bbi_ref[...] = cr * bi + ci * br
        for k, t in enumerate(_scan_tables(lr, li, False)):
            tf_ref[k] = t
        for k, t in enumerate(_scan_tables(lr, -li, True)):
            tr_ref[k] = t

    return _pcall(
        body, name=name,
        out_shape=[jax.ShapeDtypeStruct((SSM_GROUP, N_STATES), F32), jax.ShapeDtypeStruct((SSM_GROUP, N_STATES), F32),
                   jax.ShapeDtypeStruct((TAB_ROWS, SUBLANES, N_STATES), F32),
                   jax.ShapeDtypeStruct((TAB_ROWS, SUBLANES, N_STATES), F32)],
    )(ar, ai, ls, br_t, bi_t)


def ssm_param_backward(ar, ai, ls, br_t, bi_t, dlr_p, dli_p, dbbr, dbbi, group_sum, name):
    def body(ar_ref, ai_ref, ls_ref, br_ref, bi_ref, dlr_ref, dli_ref, dbbr_ref, dbbi_ref, gs_ref,
             dar_ref, dai_ref, dls_ref, dbr_ref, dbi_ref):
        ar, ai = ar_ref[...], ai_ref[...]
        step, mag, lr, li, den, nr, ni, cr, ci = _discretise(ar, ai, ls_ref[...])
        br, bi, dbbr_v, dbbi_v = br_ref[...], bi_ref[...], dbbr_ref[...], dbbi_ref[...]
        dbr_ref[...] = cr * dbbr_v + ci * dbbi_v
        dbi_ref[...] = cr * dbbi_v - ci * dbbr_v
        dcr = jnp.sum(dbbr_v * br + dbbi_v * bi, axis=0, keepdims=True)
        dci = jnp.sum(dbbi_v * br - dbbr_v * bi, axis=0, keepdims=True)
        dnr = (dcr * ar - dci * ai) / den
        dni = (dcr * ai + dci * ar) / den
        dden = -(cr * dcr + ci * dci) / den
        dar = (dcr * nr + dci * ni) / den + dden * 2.0 * ar
        dai = (dcr * ni - dci * nr) / den + dden * 2.0 * ai
        dlr = jnp.sum(dlr_ref[...], axis=0, keepdims=True) + dnr
        dli = jnp.sum(dli_ref[...], axis=0, keepdims=True) + dni
        dmag = (dlr * lr + dli * li) / mag
        dang = dli * lr - dlr * li
        dar_ref[...] = dar + dmag * mag * step
        dai_ref[...] = dai + dang * step
        dstep = dmag * mag * ar + dang * ai
        dls_ref[...] = jnp.dot(dstep * step, gs_ref[...], preferred_element_type=F32, precision=lax.Precision.HIGHEST)

    vec = jax.ShapeDtypeStruct((1, N_STATES), F32)
    mat = jax.ShapeDtypeStruct((SSM_GROUP, N_STATES), F32)
    return _pcall(body, name=name, out_shape=[vec, vec, jax.ShapeDtypeStruct((1, LANES), F32), mat, mat])(
        ar, ai, ls, br_t, bi_t, dlr_p, dli_p, dbbr, dbbi, group_sum)


def _scan_rows(a, b, tabs, carry, reverse):
    for k, d in enumerate((1, 2, 4)):
        shift = SUBLANES - d if reverse else d
        sr, si = pltpu.roll(a, shift, axis=0), pltpu.roll(b, shift, axis=0)
        pr, pi = _cmul(tabs[2 * k], tabs[2 * k + 1], sr, si)
        a, b = a + pr, b + pi
    pr, pi = _cmul(tabs[6], tabs[7], carry[0], carry[1])
    return a + pr, b + pi


def _time_groups(seq, reverse):
    meta = [seq + SUBLANES * g for g in range(N_META // SUBLANES)]
    return meta[::-1] if reverse else meta


def ssm_forward_scan(proj3, b_comb, tabf, c_comb, dvec, seq, name):
    n_b, n_l, _ = proj3.shape
    u_blk = (D_MODEL + 2 * KV_WIDTH) // LANES

    def body(u_ref, b_ref, tab_ref, c_ref, d_ref, x_ref, y_ref, bu, xs):
        j = pl.program_id(1)
        u = u_ref[...]
        bu[...] = _dot(u, b_ref[...])
        tabs = [tab_ref[k] for k in range(TAB_ROWS)]

        def group(r0, carry):
            rows = pl.ds(r0, SUBLANES)
            a, b = _scan_rows(bu[rows, :SCAN_COLS], bu[rows, SCAN_COLS:], tabs, carry, False)
            xs[rows, :SCAN_COLS] = a
            xs[rows, SCAN_COLS:] = b
            return (jnp.broadcast_to(a[SUBLANES - 1:, :], a.shape), jnp.broadcast_to(b[SUBLANES - 1:, :], b.shape))

        zero = jnp.zeros((SUBLANES, SCAN_COLS), F32)
        carry = (zero, zero)
        for r0 in _time_groups(seq, False):
            carry = group(r0, carry)
        span = SCAN_UNROLL * SUBLANES

        def groups(t, c):
            for k in range(SCAN_UNROLL):
                c = group(pl.multiple_of(t * span, span) + k * SUBLANES, c)
            return c

        lax.fori_loop(0, seq // span, groups, carry)
        x16 = xs[...].astype(BF16)
        x_ref[...] = x16
        contrib = _dot(x16, c_ref[...])

        @pl.when(j % 2 == 0)
        def _():
            y_ref[...] = contrib + d_ref[...] * u.astype(F32)

        @pl.when(j % 2 == 1)
        def _():
            y_ref[...] += contrib

    return _pcall(
        body, name=name, grid=(n_b, N_SCAN_BLK),
        in_specs=[pl.BlockSpec((None, n_l, LANES), lambda b, j: (b, 0, u_blk + j // 2)),
                  pl.BlockSpec((None, LANES, 2 * SCAN_COLS), lambda b, j: (j, 0, 0)),
                  pl.BlockSpec((TAB_ROWS, SUBLANES, SCAN_COLS), lambda b, j: (0, 0, j)),
                  pl.BlockSpec((None, 2 * SCAN_COLS, LANES), lambda b, j: (j, 0, 0)),
                  pl.BlockSpec((1, LANES), lambda b, j: (0, j // 2))],
        out_specs=[pl.BlockSpec((None, n_l, 2 * SCAN_COLS), lambda b, j: (b, 0, j)),
                   pl.BlockSpec((None, n_l, LANES), lambda b, j: (b, 0, j // 2))],
        out_shape=[jax.ShapeDtypeStruct((n_b, n_l, 2 * N_STATES), BF16),
                   jax.ShapeDtypeStruct((n_b, n_l, SSM_WIDTH), F32)],
        scratch_shapes=[pltpu.VMEM((n_l, 2 * SCAN_COLS), F32)] * 2,
        compiler_params=_cp(("arbitrary", "arbitrary"), VMEM_BIG),
    )(proj3, b_comb, tabf, c_comb, dvec)


def ssm_backward_scan(dyraw3, xs3, dproj3, c_comb_t, tabr, b_comb_t, dvec, seq, name):
    n_b, n_l, _ = xs3.shape
    u_blk = (D_MODEL + 2 * KV_WIDTH) // LANES

    def body(dy_ref, x_ref, _, c_ref, tab_ref, b_ref, d_ref, du_ref, g_ref, dlr_ref, dli_ref, dx, gs, xs, du_acc):
        j = pl.program_id(1)
        dy = dy_ref[...]
        dx[...] = _dot(dy, c_ref[...])
        xs[...] = x_ref[...].astype(F32)
        tabs = [tab_ref[k] for k in range(TAB_ROWS)]
        last_row = lax.broadcasted_iota(jnp.int32, (SUBLANES, SCAN_COLS), 0) == SUBLANES - 1

        def group(r0, state):
            cr, ci, acc_r, acc_i = state
            rows = pl.ds(r0, SUBLANES)
            a, b = _scan_rows(dx[rows, :SCAN_COLS], dx[rows, SCAN_COLS:], tabs, (cr, ci), True)
            gs[rows, :SCAN_COLS] = a
            gs[rows, SCAN_COLS:] = b
            na = jnp.where(last_row, cr, pltpu.roll(a, SUBLANES - 1, axis=0))
            nb = jnp.where(last_row, ci, pltpu.roll(b, SUBLANES - 1, axis=0))
            xa, xb = xs[rows, :SCAN_COLS], xs[rows, SCAN_COLS:]
            return (jnp.broadcast_to(a[:1, :], a.shape), jnp.broadcast_to(b[:1, :], b.shape),
                    acc_r + na * xa + nb * xb, acc_i + nb * xa - na * xb)

        zero = jnp.zeros((SUBLANES, SCAN_COLS), F32)
        span = SCAN_UNROLL * SUBLANES
        n_spans = seq // span

        def groups(t, s):
            for k in reversed(range(SCAN_UNROLL)):
                s = group(pl.multiple_of((n_spans - 1 - t) * span, span) + k * SUBLANES, s)
            return s

        state = lax.fori_loop(0, n_spans, groups, (zero, zero, zero, zero))
        for r0 in _time_groups(seq, True):
            state = group(r0, state)
        dlr_ref[...] = state[2]
        dli_ref[...] = state[3]
        g16 = gs[...].astype(BF16)
        g_ref[...] = g16
        contrib = _dot(g16, b_ref[...])

        @pl.when(j % 2 == 0)
        def _():
            du_acc[...] = contrib + d_ref[...] * dy.astype(F32)

        @pl.when(j % 2 == 1)
        def _():
            du_ref[...] = (du_acc[...] + contrib).astype(BF16)

    state_blk = pl.BlockSpec((None, n_l, 2 * SCAN_COLS), lambda b, j: (b, 0, j))
    dl_blk = pl.BlockSpec((None, SUBLANES, SCAN_COLS), lambda b, j: (b, 0, j))
    return _pcall(
        body, name=name, grid=(n_b, N_SCAN_BLK),
        in_specs=[pl.BlockSpec((None, n_l, LANES), lambda b, j: (b, 0, j // 2)), state_blk,
                  pl.BlockSpec(memory_space=pl.ANY),
                  pl.BlockSpec((None, LANES, 2 * SCAN_COLS), lambda b, j: (j, 0, 0)),
                  pl.BlockSpec((TAB_ROWS, SUBLANES, SCAN_COLS), lambda b, j: (0, 0, j)),
                  pl.BlockSpec((None, 2 * SCAN_COLS, LANES), lambda b, j: (j, 0, 0)),
                  pl.BlockSpec((1, LANES), lambda b, j: (0, j // 2))],
        out_specs=[pl.BlockSpec((None, n_l, LANES), lambda b, j: (b, 0, u_blk + j // 2)), state_blk, dl_blk, dl_blk],
        out_shape=[jax.ShapeDtypeStruct(dproj3.shape, BF16), jax.ShapeDtypeStruct((n_b, n_l, 2 * N_STATES), BF16),
                   jax.ShapeDtypeStruct((n_b, SUBLANES, N_STATES), F32), jax.ShapeDtypeStruct((n_b, SUBLANES, N_STATES), F32)],
        scratch_shapes=[pltpu.VMEM((n_l, 2 * SCAN_COLS), F32)] * 3 + [pltpu.VMEM((n_l, LANES), F32)],
        input_output_aliases={2: 0},
        compiler_params=_cp(("arbitrary", "arbitrary"), VMEM_BIG),
    )(dyraw3, xs3, dproj3, c_comb_t, tabr, b_comb_t, dvec)


def ssm_param_grads(proj, gs, xs, dyraw, tm, name):
    t_rows = proj.shape[0]
    ni = t_rows // tm
    u_blk = (D_MODEL + 2 * KV_WIDTH) // LANES
    width = 2 * SCAN_COLS

    def body(u_ref, g_ref, x_ref, dy_ref, db_ref, dc_ref, dd_ref):
        cb, i = pl.program_id(0), pl.program_id(1)
        u, dy = u_ref[...], dy_ref[...]
        _accumulate(db_ref, _dot_tn(u, g_ref[...]), i == 0)
        _accumulate(dc_ref, _dot_tn(x_ref[...], dy), i == 0)

        @pl.when(cb % 2 == 0)
        def _():
            _accumulate(dd_ref, jnp.sum(dy.astype(F32) * u.astype(F32), axis=0, keepdims=True), i == 0)

    return _pcall(
        body, name=name, grid=(N_SCAN_BLK, ni),
        in_specs=[pl.BlockSpec((tm, LANES), lambda cb, i: (i, u_blk + cb // 2)),
                  pl.BlockSpec((tm, width), lambda cb, i: (i, cb)),
                  pl.BlockSpec((tm, width), lambda cb, i: (i, cb)),
                  pl.BlockSpec((tm, LANES), lambda cb, i: (i, cb // 2))],
        out_specs=[pl.BlockSpec((None, LANES, width), lambda cb, i: (cb, 0, 0)),
                   pl.BlockSpec((None, width, LANES), lambda cb, i: (cb, 0, 0)),
                   pl.BlockSpec((1, LANES), lambda cb, i: (0, cb // 2))],
        out_shape=[jax.ShapeDtypeStruct((N_SCAN_BLK, LANES, width), F32),
                   jax.ShapeDtypeStruct((N_SCAN_BLK, width, LANES), F32), jax.ShapeDtypeStruct((1, SSM_WIDTH), F32)],
        compiler_params=_cp(("arbitrary", "arbitrary"), VMEM_BIG),
    )(proj, gs, xs, dyraw)


def sum_leading(x, name):
    def body(x_ref, o_ref):
        acc = x_ref[0]
        for k in range(1, x.shape[0]):
            acc = acc + x_ref[k]
        o_ref[...] = acc

    return _pcall(body, name=name, out_shape=jax.ShapeDtypeStruct(x.shape[1:], x.dtype))(x)


WEIGHTS = ['meta_tokens', 'ffn1_norm', 'ffn1_w1', 'ffn1_w3', 'ffn1_w2', 'mix_norm', 'w_in', 'attn_sinks', 'ssm_a_re',
           'ssm_a_im', 'ssm_log_step', 'ssm_b_re', 'ssm_b_im', 'ssm_c_re', 'ssm_c_im', 'ssm_d', 'ssm_glu_a', 'ssm_glu_b',
           'w_out', 'ffn2_norm', 'ffn2_w1', 'ffn2_w3', 'ffn2_w2', 'final_norm']
SHARDED = ['ffn1_w1', 'ffn1_w3', 'ffn1_w2', 'ffn2_w1', 'ffn2_w3', 'ffn2_w2', 'w_in', 'ssm_glu_a', 'ssm_glu_b', 'w_out']
REPLICATED = ['ffn1_norm', 'mix_norm', 'ffn2_norm', 'final_norm', 'attn_sinks', 'ssm_a_re', 'ssm_a_im', 'ssm_log_step',
              'ssm_b_re', 'ssm_b_im', 'ssm_c_re', 'ssm_c_im', 'ssm_d']
PACK_COLS = 1024


def _pack(arrays):
    parts = []
    for a in arrays:
        flat = a.reshape(-1)
        chunk = SUBLANES * PACK_COLS
        padded = -(-flat.shape[0] // chunk) * chunk
        parts.append(jnp.pad(flat, (0, padded - flat.shape[0])).reshape(-1, PACK_COLS))
    return jnp.concatenate(parts, axis=0)


def _unpack(packed, shapes):
    out, row = [], 0
    for shape in shapes:
        size = 1
        for s in shape:
            size *= s
        chunk = SUBLANES * PACK_COLS
        rows = -(-size // chunk) * SUBLANES
        out.append(packed[row:row + rows].reshape(-1)[:size].reshape(shape))
        row += rows
    return out


def kernel(x, meta_tokens, ffn1_norm, ffn1_w1, ffn1_w3, ffn1_w2, mix_norm, w_in, attn_sinks, ssm_a_re, ssm_a_im, ssm_log_step, ssm_b_re, ssm_b_im, ssm_c_re, ssm_c_im, ssm_d, ssm_glu_a, ssm_glu_b, w_out, ffn2_norm, ffn2_w1, ffn2_w3, ffn2_w2, final_norm, loss_target, m_meta_tokens, m_ffn1_norm, m_ffn1_w1, m_ffn1_w3, m_ffn1_w2, m_mix_norm, m_w_in, m_attn_sinks, m_ssm_a_re, m_ssm_a_im, m_ssm_log_step, m_ssm_b_re, m_ssm_b_im, m_ssm_c_re, m_ssm_c_im, m_ssm_d, m_ssm_glu_a, m_ssm_glu_b, m_w_out, m_ffn2_norm, m_ffn2_w1, m_ffn2_w3, m_ffn2_w2, m_final_norm, v_meta_tokens, v_ffn1_norm, v_ffn1_w1, v_ffn1_w3, v_ffn1_w2, v_mix_norm, v_w_in, v_attn_sinks, v_ssm_a_re, v_ssm_a_im, v_ssm_log_step, v_ssm_b_re, v_ssm_b_im, v_ssm_c_re, v_ssm_c_im, v_ssm_d, v_ssm_glu_a, v_ssm_glu_b, v_w_out, v_ffn2_norm, v_ffn2_w1, v_ffn2_w3, v_ffn2_w2, v_final_norm):
    given = dict(locals())
    w = {n: given[n] for n in WEIGHTS}
    m = {n: given["m_" + n] for n in WEIGHTS}
    v = {n: given["v_" + n] for n in WEIGHTS}

    n_b, seq, _ = x.shape
    n_l = seq + N_META
    t_rows = n_b * n_l
    tm = _row_tile(n_l, 688)
    px, py, pc = _my_place()
    me = 4 * px + 2 * py + pc

    glu = jnp.stack([ssm_glu_a[0], ssm_glu_b[0]]).astype(BF16)
    ffn_names = ['ffn1_w1', 'ffn1_w3', 'ffn1_w2', 'ffn2_w1', 'ffn2_w3', 'ffn2_w2']

    def hidden_on_rows(n, t):
        return t[0] if n.endswith('w2') else t[0].T

    def hidden_on_rows_back(n, t):
        return t[None] if n.endswith('w2') else t.T[None]

    me_idx = jnp.reshape(me, (1,)).astype(jnp.int32)
    first_names, later_names = ffn_names[:2], ffn_names[3:]
    *first, metag = all_gather_list(
        [hidden_on_rows(n, w[n]).astype(BF16) for n in first_names] + [meta_tokens], meta_tokens, "ag_first")
    w2_send, w2_recv, w2_shard, w2_land, w2_token = exchange_start(
        [hidden_on_rows('ffn1_w2', w['ffn1_w2']).astype(BF16)], first[0], True, "ag_w2_start")
    win_send, win_recv, win_shard, win_land, win_token = exchange_start(
        [w_in[0].astype(BF16)], w2_token, True, "ag_w_in_start")
    later_shards = [hidden_on_rows(n, w[n]).astype(BF16) for n in later_names] + [glu, w_out[0].astype(BF16)]
    ag_send, ag_recv, later_shards, later_lands, ag_token = exchange_start(later_shards, win_token, True, "ag_later_start")
    full = {n: g.reshape(D_FF, D_MODEL) for n, g in zip(first_names, first)}
    meta_full = metag.transpose(1, 0, 2).reshape(N_META, D_MODEL)

    final_g = final_norm.reshape(1, D_MODEL)

    ar = ssm_a_re.reshape(1, N_STATES)
    ai = ssm_a_im.reshape(1, N_STATES)
    ls = jnp.repeat(ssm_log_step.reshape(SSM_GROUPS), SSM_STATE).reshape(1, N_STATES)
    br_t = ssm_b_re[0].transpose(2, 0, 1).reshape(SSM_GROUP, N_STATES)
    bi_t = ssm_b_im[0].transpose(2, 0, 1).reshape(SSM_GROUP, N_STATES)
    bbr, bbi, tabf, tabr = ssm_prepare(ar, ai, ls, br_t, bi_t, "ssm_prepare")
    bbr_g = bbr.reshape(SSM_GROUP, SSM_GROUPS, SSM_STATE).transpose(1, 0, 2)
    bbi_g = bbi.reshape(SSM_GROUP, SSM_GROUPS, SSM_STATE).transpose(1, 0, 2)
    groups_per_blk = SCAN_COLS // SSM_STATE
    half = ((jnp.arange(N_SCAN_BLK) % 2)[:, None] == jnp.arange(2)[None, :]).astype(F32)
    eye = jnp.eye(groups_per_blk, dtype=F32)

    def scan_blocks(re_g, im_g):
        def one(t):
            t = t.reshape(N_SCAN_BLK, groups_per_blk, SSM_GROUP, SSM_STATE)
            t = t[:, :, :, None, :] * eye[None, :, None, :, None]
            t = t.reshape(N_SCAN_BLK, LANES // 2, SCAN_COLS)
            return (t[:, None] * half[:, :, None, None]).reshape(N_SCAN_BLK, LANES, SCAN_COLS)
        return jnp.concatenate([one(re_g), one(im_g)], axis=-1).astype(BF16)

    b_comb = scan_blocks(bbr_g, bbi_g)
    c_comb_t = scan_blocks(ssm_c_re[0], -ssm_c_im[0])
    b_comb_t, c_comb = b_comb.transpose(0, 2, 1), c_comb_t.transpose(0, 2, 1)

    hid1, hn1, a1, b1, h0 = ffn_forward(
        x, ffn1_norm, full['ffn1_w1'], full['ffn1_w3'], None, ag_token, tm, "ffn1_fwd", meta=meta_full)
    w2_shard, (w2g,) = exchange_wait(w2_send, w2_recv, w2_shard, w2_land, hid1, True, "ag_w2_wait")
    full['ffn1_w2'] = lax.dynamic_update_slice_in_dim(w2g, w2_shard[0][None], me, axis=0).reshape(D_FF, D_MODEL)
    ffn1_w = (full['ffn1_w1'], full['ffn1_w3'], full['ffn1_w2'])
    (h1,) = ffn_down(h0, hid1, full['ffn1_w2'], tm, "ffn1_down")
    win_shard, (wing,) = exchange_wait(win_send, win_recv, win_shard, win_land, h1, True, "ag_w_in_wait")
    wing = lax.dynamic_update_slice_in_dim(wing, win_shard[0][None], me, axis=0)
    hnm, proj = mix_forward(h1, mix_norm, wing, tm, "mix_fwd")
    proj3 = proj.reshape(n_b, n_l, IN_WIDTH)
    attn3 = attention_forward(proj3, attn_sinks, seq, "attn_fwd")
    attn = attn3.reshape(t_rows, D_MODEL)
    xs3, yraw3 = ssm_forward_scan(proj3, b_comb, tabf, c_comb, ssm_d, seq, "ssm_fwd")
    yraw = yraw3.reshape(t_rows, SSM_WIDTH)
    later_shards, later = exchange_wait(ag_send, ag_recv, later_shards, later_lands, yraw3, True, "ag_later_wait")
    later = [lax.dynamic_update_slice_in_dim(z, s[None], me, axis=0) for z, s in zip(later, later_shards)]
    for n, g in zip(later_names, later):
        full[n] = g.reshape(D_FF, D_MODEL)
    ffn2_w = (full['ffn2_w1'], full['ffn2_w3'], full['ffn2_w2'])
    glug, wog = later[len(later_names):]
    glu_a = glug[:, 0].transpose(1, 0, 2).reshape(SSM_WIDTH, D_MODEL)
    glu_b = glug[:, 1].transpose(1, 0, 2).reshape(SSM_WIDTH, D_MODEL)
    w_out_full = wog.reshape(D_MODEL, D_MODEL)
    h2 = merge_forward(h1, yraw, attn, proj, glu_a, glu_b, w_out_full, tm, "merge_fwd")
    h3, hn2, a2, b2 = ffn_forward(h2, ffn2_norm, *ffn2_w, ag_token, tm, "ffn2_fwd")
    dh3, loss_part, g_final = final_loss_backward(h3, loss_target, final_g, seq, tm, "loss_bwd")

    def blocked_ffn(d_w1t, d_w3t, d_w2):
        return tuple(t.reshape(N_DEV, FF_BLK, D_MODEL) for t in (d_w1t, d_w3t, d_w2))

    def blocked_cols(full_grad):
        r = full_grad.shape[0]
        return full_grad.reshape(r, N_DEV, full_grad.shape[1] // N_DEV).transpose(1, 0, 2).astype(BF16)

    early = {}

    def start_reduce(names, tag, after):
        srcs = [dw[n] for n in names]
        send, recv, srcs, lands, token = exchange_start(srcs, after, False, "rs_" + tag + "_start")
        early[tag] = (names, send, recv, srcs, lands)
        return token

    dw = {}
    da2, db2, dh3_half = ffn_backward_hidden(dh3, a2, b2, ffn2_w[2], g_final, tm, "ffn2_bwd_hid")
    dw['ffn2_w1'], dw['ffn2_w3'], dw['ffn2_w2'] = blocked_ffn(
        *ffn_backward_weights(hn2, dh3_half, a2, b2, da2, db2, n_l, FF_BWD_COLS, "ffn2_bwd_w"))
    token = start_reduce(later_names, "ffn2", a2)
    dh2, g_ffn2_norm = ffn_backward_input(dh3, h2, ffn2_norm, da2, db2, ffn2_w[0], ffn2_w[1], token, tm, "ffn2_bwd_in")
    dattn, dyraw, dproj, *for_weights = merge_backward(dh2, yraw, attn, proj, glu_a, glu_b, w_out_full, token, tm,
                                                       "merge_bwd")
    d_wo, d_ga, d_gb = merge_backward_weights(*for_weights, tm, "merge_bwd_w")
    dw['ssm_glu_a'] = blocked_cols(d_ga)
    dw['ssm_glu_b'] = blocked_cols(d_gb)
    dw['w_out'] = d_wo.reshape(N_DEV, D_MODEL // N_DEV, D_MODEL).astype(BF16)
    token = start_reduce(['ssm_glu_a', 'ssm_glu_b', 'w_out'], "mix", a2)
    dproj3 = dproj.reshape(n_b, n_l, IN_WIDTH)
    dproj3, dsink_p = attention_backward(proj3, dattn.reshape(n_b, n_l, D_MODEL), dproj3, attn_sinks, token, seq,
                                         "attn_bwd")
    dproj3, gs3, dlr_p, dli_p = ssm_backward_scan(
        dyraw.reshape(n_b, n_l, SSM_WIDTH), xs3, dproj3, c_comb_t, tabr, b_comb_t, ssm_d, seq, "ssm_bwd")
    dproj = dproj3.reshape(t_rows, IN_WIDTH)
    d_bd, d_cd, g_d = ssm_param_grads(proj, gs3.reshape(t_rows, 2 * N_STATES), xs3.reshape(t_rows, 2 * N_STATES),
                                      dyraw, n_l, "ssm_bwd_w")
    w_in_full = wing.transpose(1, 0, 2).reshape(D_MODEL, IN_WIDTH)
    dh1, g_mix_norm = mix_backward_act(dh2, h1, mix_norm, dproj, w_in_full, tm, "mix_bwd_act")
    dw['w_in'] = mix_backward_weights(hnm, dproj, n_l, "mix_bwd_w")
    token = start_reduce(['w_in'], "w_in", hnm)

    def group_blocks(part, channels_first):
        if channels_first:
            t = jnp.sum(part.reshape(N_SCAN_BLK, 2, LANES // 2, SCAN_COLS) * half[:, :, None, None], axis=1)
            t = t.reshape(N_SCAN_BLK, groups_per_blk, SSM_GROUP, groups_per_blk, SSM_STATE)
            t = jnp.sum(t * eye[None, :, None, :, None], axis=3)
            return t.reshape(SSM_GROUPS, SSM_GROUP, SSM_STATE)
        t = jnp.sum(part.reshape(N_SCAN_BLK, SCAN_COLS, 2, LANES // 2) * half[:, None, :, None], axis=2)
        t = t.reshape(N_SCAN_BLK, groups_per_blk, SSM_STATE, groups_per_blk, SSM_GROUP)
        t = jnp.sum(t * eye[None, :, None, :, None], axis=3)
        return t.reshape(SSM_GROUPS, SSM_STATE, SSM_GROUP).transpose(0, 2, 1)

    dbbr = group_blocks(d_bd[:, :, :SCAN_COLS], True).transpose(1, 0, 2).reshape(SSM_GROUP, N_STATES)
    dbbi = group_blocks(d_bd[:, :, SCAN_COLS:], True).transpose(1, 0, 2).reshape(SSM_GROUP, N_STATES)
    g_c_re = group_blocks(d_cd[:, :SCAN_COLS, :], False)[None]
    g_c_im = -group_blocks(d_cd[:, SCAN_COLS:, :], False)[None]
    group_sum = (jnp.arange(N_STATES)[:, None] // SSM_STATE == jnp.arange(LANES)[None, :]).astype(F32)
    g_ar, g_ai, g_ls, g_br, g_bi = ssm_param_backward(
        ar, ai, ls, br_t, bi_t, dlr_p.reshape(n_b * SUBLANES, N_STATES), dli_p.reshape(n_b * SUBLANES, N_STATES),
        dbbr, dbbi, group_sum, "ssm_bwd_params")
    g_sinks = sum_leading(dsink_p, "sink_sum")[0:1, :N_KV_HEADS * Q_PER_KV]

    small = {
        'mix_norm': g_mix_norm, 'ffn2_norm': g_ffn2_norm, 'final_norm': g_final.reshape(D_MODEL),
        'attn_sinks': g_sinks, 'ssm_a_re': g_ar.reshape(1, SSM_GROUPS, SSM_STATE), 'ssm_a_im': g_ai.reshape(1, SSM_GROUPS, SSM_STATE),
        'ssm_log_step': g_ls[:, :SSM_GROUPS],
        'ssm_b_re': g_br.reshape(SSM_GROUP, SSM_GROUPS, SSM_STATE).transpose(1, 2, 0)[None],
        'ssm_b_im': g_bi.reshape(SSM_GROUP, SSM_GROUPS, SSM_STATE).transpose(1, 2, 0)[None],
        'ssm_c_re': g_c_re, 'ssm_c_im': g_c_im, 'ssm_d': g_d,
    }
    early_small = [n for n in REPLICATED if n in small]
    sg_send, sg_recv, sg_src, sg_land, token = exchange_start(
        [_pack([small[n] for n in early_small] + [loss_part])], token, True, "ag_small_start")
    da1, db1, dh1_half = ffn_backward_hidden(dh1, a1, b1, ffn1_w[2], token, tm, "ffn1_bwd_hid")
    dw['ffn1_w1'], dw['ffn1_w3'], dw['ffn1_w2'] = blocked_ffn(
        *ffn_backward_weights(hn1, dh1_half, a1, b1, da1, db1, n_l, FF_BWD_COLS, "ffn1_bwd_w"))
    token = start_reduce(ffn_names[:3], "ffn1", a1)
    grad_x, meta_rows_grad, g_ffn1_norm = ffn_backward_input(
        dh1, h0, ffn1_norm, da1, db1, ffn1_w[0], ffn1_w[1], token, tm, "ffn1_bwd_in", examples=(n_b, seq))
    g_meta = sum_leading(meta_rows_grad, "meta_sum")

    grads, deltas, new_m, new_v = {}, {}, {}, {}

    def views(n):
        if n in ffn_names:
            return functools.partial(hidden_on_rows, n), functools.partial(hidden_on_rows_back, n)
        return (lambda t: t[0]), (lambda t: t[None])

    def finish_reduce(tag, previous):
        names, send, recv, srcs, lands = early[tag]
        srcs, lands = exchange_wait(send, recv, srcs, lands, previous, False, "rs_" + tag + "_wait")
        for n, g, land in zip(names, srcs, lands):
            two_d, back = views(n)
            out = adamw_exchanged(me_idx, g, land, two_d(w[n]), two_d(m[n]), two_d(v[n]), "adamw_" + n)
            grads[n], deltas[n], new_m[n], new_v[n] = (back(o) for o in out)
            previous = out[1]
        return previous

    previous = g_meta
    for tag in ("ffn2", "mix", "w_in"):
        previous = finish_reduce(tag, previous)

    zeros_meta = jnp.zeros((N_META, D_MODEL), F32)
    (late_parts,) = all_gather_list([_pack([g_ffn1_norm, g_meta])], previous, "ag_small_late")
    sg_src, (early_parts,) = exchange_wait(sg_send, sg_recv, sg_src, sg_land, late_parts, True, "ag_small_wait")
    early_parts = lax.dynamic_update_slice_in_dim(early_parts, sg_src[0][None], me, axis=0)

    def small_update(parts, names, extra, tag):
        pack_of = lambda d: _pack([d[n] for n in names] + extra)
        packed = adamw_small(parts, pack_of(w), pack_of(m), pack_of(v), "adamw_small_" + tag)
        unpacked = [_unpack(p, [w[n].shape for n in names] + [e.shape for e in extra]) for p in packed]
        for k, n in enumerate(names):
            grads[n], deltas[n], new_m[n], new_v[n] = (u[k] for u in unpacked)
        return packed, unpacked

    _, early_out = small_update(early_parts, early_small, [jnp.zeros_like(loss_part)], "early")
    loss = early_out[0][-1][0, 0]
    packed_out, unpacked = small_update(late_parts, ['ffn1_norm'], [zeros_meta], "late")
    g_meta_full = unpacked[0][-1]
    grads['meta_tokens'] = lax.dynamic_index_in_dim(
        g_meta_full.reshape(N_META, N_DEV, D_MODEL // N_DEV), me, axis=1, keepdims=False)
    deltas['meta_tokens'], new_m['meta_tokens'], new_v['meta_tokens'] = adamw_plain(
        grads['meta_tokens'], w['meta_tokens'], m['meta_tokens'], v['meta_tokens'], "adamw_meta")

    finish_reduce("ffn1", packed_out[0])

    return (loss, grad_x, *[grads[n] for n in WEIGHTS], *[deltas[n] for n in WEIGHTS],
            *[new_m[n] for n in WEIGHTS], *[new_v[n] for n in WEIGHTS])
```
